```python
import jax, jax.numpy as jnp
from jax import lax
import numpy as np

D_MODEL = 1024
BATCH = 8
SEQ = 2048
DEPTH = 1

CONV_WIDTH = D_MODEL
CONV_GROUPS = 8
LRU_WIDTH = D_MODEL
LRU_HEADS = 4
LRU_HEAD_DIM = LRU_WIDTH // LRU_HEADS
SHORT_CONV_K = 3
LRU_CONV_K = 4
FFN_CONV_K = 3
D_FF = 3 * D_MODEL
LRU_C = 8.0
RMS_EPS = 1e-6
IN_COLS = 3 * CONV_WIDTH + 2 * LRU_WIDTH + 2 * D_MODEL
SPLITS = (CONV_WIDTH, 2 * CONV_WIDTH, 3 * CONV_WIDTH,
          3 * CONV_WIDTH + LRU_WIDTH, 3 * CONV_WIDTH + 2 * LRU_WIDTH,
          3 * CONV_WIDTH + 2 * LRU_WIDTH + D_MODEL)

kernel_name = "hybrid_shortconv_rglru_convffn_sandwich"


def rmsnorm(x, g):
    xf = x.astype(jnp.float32)
    y = xf * lax.rsqrt(jnp.mean(xf * xf, axis=-1, keepdims=True) + RMS_EPS)
    return (y * g.astype(jnp.float32)).astype(x.dtype)


def causal_dwconv(x, w, b=None):
    k_width = w.shape[0]
    s = x.shape[1]
    xp = jnp.pad(x, ((0, 0), (k_width - 1, 0), (0, 0)))
    y = xp[:, 0:s] * w[0]
    for k in range(1, k_width):
        y = y + xp[:, k:k + s] * w[k]
    if b is not None:
        y = y + b
    return y


def rg_lru(x, w_a, b_a, w_x, b_x, lam):
    bsz, s, w = x.shape
    xf = x.astype(jnp.float32)
    xh = xf.reshape(bsz, s, LRU_HEADS, LRU_HEAD_DIM)
    r = jax.nn.sigmoid(jnp.einsum("bshi,hij->bshj", xh, w_a.astype(jnp.float32)) + b_a.astype(jnp.float32)).reshape(bsz, s, w)
    i = jax.nn.sigmoid(jnp.einsum("bshi,hij->bshj", xh, w_x.astype(jnp.float32)) + b_x.astype(jnp.float32)).reshape(bsz, s, w)
    log_a = LRU_C * r * jax.nn.log_sigmoid(lam.astype(jnp.float32))
    a = jnp.exp(log_a)
    mult = jnp.sqrt(-jnp.expm1(2.0 * log_a))
    first = (jnp.arange(s) == 0)[None, :, None]
    mult = jnp.where(first, 1.0, mult)
    u = mult * (i * xf)

    def combine(left, right):
        a1, b1 = left
        a2, b2 = right
        return a1 * a2, a2 * b1 + b2

    _, h = lax.associative_scan(combine, (a, u), axis=1)
    return h.astype(x.dtype)


def _fwd_setup_inputs(seed: int = 0) -> dict:
    key = jax.random.key(seed)
    ks = jax.random.split(key, 24)
    f32 = jnp.float32

    def nrm(k, shape, fan_in):
        return jax.random.normal(k, shape, f32) * (fan_in ** -0.5)

    def gain(k, n):
        return 1.0 + 0.05 * jax.random.normal(k, (DEPTH, n), f32)

    u = jax.random.uniform(ks[14], (DEPTH, LRU_WIDTH), f32, 0.9, 0.999)
    a0 = u ** (1.0 / LRU_C)
    lam = jnp.log(a0) - jnp.log1p(-a0)

    return {
        "x": jax.random.normal(ks[0], (BATCH, SEQ, D_MODEL), f32),
        "norm_mix_pre": gain(ks[1], D_MODEL),
        "norm_mix_post": gain(ks[2], D_MODEL),
        "norm_ffn_pre": gain(ks[3], D_MODEL),
        "norm_ffn_post": gain(ks[4], D_MODEL),
        "w_in": nrm(ks[5], (DEPTH, D_MODEL, IN_COLS), D_MODEL),
        "conv_short_w": nrm(ks[6], (DEPTH, SHORT_CONV_K, CONV_WIDTH), SHORT_CONV_K),
        "w_conv_branch": nrm(ks[7], (DEPTH, CONV_WIDTH, D_MODEL), CONV_WIDTH),
        "lru_conv_w": nrm(ks[8], (DEPTH, LRU_CONV_K, LRU_WIDTH), LRU_CONV_K),
        "lru_conv_b": 0.02 * jax.random.normal(ks[9], (DEPTH, LRU_WIDTH), f32),
        "lru_wa": nrm(ks[10], (DEPTH, LRU_HEADS, LRU_HEAD_DIM, LRU_HEAD_DIM), LRU_HEAD_DIM),
        "lru_ba": 0.02 * jax.random.normal(ks[11], (DEPTH, LRU_HEADS, LRU_HEAD_DIM), f32),
        "lru_wx": nrm(ks[12], (DEPTH, LRU_HEADS, LRU_HEAD_DIM, LRU_HEAD_DIM), LRU_HEAD_DIM),
        "lru_bx": 0.02 * jax.random.normal(ks[13], (DEPTH, LRU_HEADS, LRU_HEAD_DIM), f32),
        "lru_lambda": lam,
        "w_lru_branch": nrm(ks[15], (DEPTH, LRU_WIDTH, D_MODEL), LRU_WIDTH),
        "w_out": nrm(ks[16], (DEPTH, D_MODEL, D_MODEL), D_MODEL),
        "ffn_w_up": nrm(ks[17], (DEPTH, D_MODEL, 2 * D_FF), D_MODEL),
        "ffn_conv_w": nrm(ks[18], (DEPTH, FFN_CONV_K, 2 * D_FF), FFN_CONV_K),
        "ffn_conv_b": 0.02 * jax.random.normal(ks[19], (DEPTH, 2 * D_FF), f32),
        "ffn_w_down": nrm(ks[20], (DEPTH, D_FF, D_MODEL), D_FF),
    }


def _fwd_reference(x, norm_mix_pre, norm_mix_post, norm_ffn_pre, norm_ffn_post, w_in, conv_short_w,
              w_conv_branch, lru_conv_w, lru_conv_b, lru_wa, lru_ba, lru_wx, lru_bx, lru_lambda,
              w_lru_branch, w_out, ffn_w_up, ffn_conv_w, ffn_conv_b, ffn_w_down):
    for l in range(DEPTH):
        h = rmsnorm(x, norm_mix_pre[l])
        proj = jnp.einsum("bsd,dc->bsc", h, w_in[l])
        c_b, c_c, c_x, l_x, l_y, g_conv, g_lru = jnp.split(proj, SPLITS, axis=-1)
        y_a = c_b * causal_dwconv(c_c * c_x, conv_short_w[l])
        xl = causal_dwconv(l_x, lru_conv_w[l], lru_conv_b[l])
        hl = rg_lru(xl, lru_wa[l], lru_ba[l], lru_wx[l], lru_bx[l], lru_lambda[l])
        y_b = hl * jax.nn.gelu(l_y, approximate=True)
        merged = (jax.nn.sigmoid(g_conv) * jnp.einsum("bsc,cd->bsd", y_a, w_conv_branch[l])
                  + jax.nn.sigmoid(g_lru) * jnp.einsum("bsc,cd->bsd", y_b, w_lru_branch[l]))
        mix = jnp.einsum("bsd,de->bse", merged, w_out[l])
        x = x + rmsnorm(mix, norm_mix_post[l])
        h = rmsnorm(x, norm_ffn_pre[l])
        up = jnp.einsum("bsd,df->bsf", h, ffn_w_up[l])
        up = causal_dwconv(up, ffn_conv_w[l], ffn_conv_b[l])
        gate, val = jnp.split(up, 2, axis=-1)
        f = jax.nn.gelu(gate, approximate=True) * val
        out = jnp.einsum("bsf,fd->bsd", f, ffn_w_down[l])
        x = x + rmsnorm(out, norm_ffn_post[l])
    return x


import jax as _jax
import jax.numpy as _jnp

TWIN_FORMAT = 'train_step'
FWD_PARAMS = ['x', 'norm_mix_pre', 'norm_mix_post', 'norm_ffn_pre', 'norm_ffn_post', 'w_in', 'conv_short_w', 'w_conv_branch', 'lru_conv_w', 'lru_conv_b', 'lru_wa', 'lru_ba', 'lru_wx', 'lru_bx', 'lru_lambda', 'w_lru_branch', 'w_out', 'ffn_w_up', 'ffn_conv_w', 'ffn_conv_b', 'ffn_w_down']
TWIN_WEIGHTS = ['norm_mix_pre', 'norm_mix_post', 'norm_ffn_pre', 'norm_ffn_post', 'w_in', 'conv_short_w', 'w_conv_branch', 'lru_conv_w', 'lru_conv_b', 'lru_wa', 'lru_ba', 'lru_wx', 'lru_bx', 'lru_lambda', 'w_lru_branch', 'w_out', 'ffn_w_up', 'ffn_conv_w', 'ffn_conv_b', 'ffn_w_down']
TWIN_DIFF_INPUT = 'x'
TWIN_INPUTS = ['x', 'norm_mix_pre', 'norm_mix_post', 'norm_ffn_pre', 'norm_ffn_post', 'w_in', 'conv_short_w', 'w_conv_branch', 'lru_conv_w', 'lru_conv_b', 'lru_wa', 'lru_ba', 'lru_wx', 'lru_bx', 'lru_lambda', 'w_lru_branch', 'w_out', 'ffn_w_up', 'ffn_conv_w', 'ffn_conv_b', 'ffn_w_down', 'loss_target', 'm_norm_mix_pre', 'm_norm_mix_post', 'm_norm_ffn_pre', 'm_norm_ffn_post', 'm_w_in', 'm_conv_short_w', 'm_w_conv_branch', 'm_lru_conv_w', 'm_lru_conv_b', 'm_lru_wa', 'm_lru_ba', 'm_lru_wx', 'm_lru_bx', 'm_lru_lambda', 'm_w_lru_branch', 'm_w_out', 'm_ffn_w_up', 'm_ffn_conv_w', 'm_ffn_conv_b', 'm_ffn_w_down', 'v_norm_mix_pre', 'v_norm_mix_post', 'v_norm_ffn_pre', 'v_norm_ffn_post', 'v_w_in', 'v_conv_short_w', 'v_w_conv_branch', 'v_lru_conv_w', 'v_lru_conv_b', 'v_lru_wa', 'v_lru_ba', 'v_lru_wx', 'v_lru_bx', 'v_lru_lambda', 'v_w_lru_branch', 'v_w_out', 'v_ffn_w_up', 'v_ffn_conv_w', 'v_ffn_conv_b', 'v_ffn_w_down']
TWIN_OUTPUTS = ['loss', 'grad_x', 'grad_norm_mix_pre', 'grad_norm_mix_post', 'grad_norm_ffn_pre', 'grad_norm_ffn_post', 'grad_w_in', 'grad_conv_short_w', 'grad_w_conv_branch', 'grad_lru_conv_w', 'grad_lru_conv_b', 'grad_lru_wa', 'grad_lru_ba', 'grad_lru_wx', 'grad_lru_bx', 'grad_lru_lambda', 'grad_w_lru_branch', 'grad_w_out', 'grad_ffn_w_up', 'grad_ffn_conv_w', 'grad_ffn_conv_b', 'grad_ffn_w_down', 'delta_norm_mix_pre', 'delta_norm_mix_post', 'delta_norm_ffn_pre', 'delta_norm_ffn_post', 'delta_w_in', 'delta_conv_short_w', 'delta_w_conv_branch', 'delta_lru_conv_w', 'delta_lru_conv_b', 'delta_lru_wa', 'delta_lru_ba', 'delta_lru_wx', 'delta_lru_bx', 'delta_lru_lambda', 'delta_w_lru_branch', 'delta_w_out', 'delta_ffn_w_up', 'delta_ffn_conv_w', 'delta_ffn_conv_b', 'delta_ffn_w_down', 'new_m_norm_mix_pre', 'new_m_norm_mix_post', 'new_m_norm_ffn_pre', 'new_m_norm_ffn_post', 'new_m_w_in', 'new_m_conv_short_w', 'new_m_w_conv_branch', 'new_m_lru_conv_w', 'new_m_lru_conv_b', 'new_m_lru_wa', 'new_m_lru_ba', 'new_m_lru_wx', 'new_m_lru_bx', 'new_m_lru_lambda', 'new_m_w_lru_branch', 'new_m_w_out', 'new_m_ffn_w_up', 'new_m_ffn_conv_w', 'new_m_ffn_conv_b', 'new_m_ffn_w_down', 'new_v_norm_mix_pre', 'new_v_norm_mix_post', 'new_v_norm_ffn_pre', 'new_v_norm_ffn_post', 'new_v_w_in', 'new_v_conv_short_w', 'new_v_w_conv_branch', 'new_v_lru_conv_w', 'new_v_lru_conv_b', 'new_v_lru_wa', 'new_v_lru_ba', 'new_v_lru_wx', 'new_v_lru_bx', 'new_v_lru_lambda', 'new_v_w_lru_branch', 'new_v_w_out', 'new_v_ffn_w_up', 'new_v_ffn_conv_w', 'new_v_ffn_conv_b', 'new_v_ffn_w_down']
TWIN_LEAF_KINDS = {'loss': 'loss', 'grad_x': 'grad_x', 'grad_norm_mix_pre': 'grad_w', 'grad_norm_mix_post': 'grad_w', 'grad_norm_ffn_pre': 'grad_w', 'grad_norm_ffn_post': 'grad_w', 'grad_w_in': 'grad_w', 'grad_conv_short_w': 'grad_w', 'grad_w_conv_branch': 'grad_w', 'grad_lru_conv_w': 'grad_w', 'grad_lru_conv_b': 'grad_w', 'grad_lru_wa': 'grad_w', 'grad_lru_ba': 'grad_w', 'grad_lru_wx': 'grad_w', 'grad_lru_bx': 'grad_w', 'grad_lru_lambda': 'grad_w', 'grad_w_lru_branch': 'grad_w', 'grad_w_out': 'grad_w', 'grad_ffn_w_up': 'grad_w', 'grad_ffn_conv_w': 'grad_w', 'grad_ffn_conv_b': 'grad_w', 'grad_ffn_w_down': 'grad_w', 'delta_norm_mix_pre': 'delta_w', 'delta_norm_mix_post': 'delta_w', 'delta_norm_ffn_pre': 'delta_w', 'delta_norm_ffn_post': 'delta_w', 'delta_w_in': 'delta_w', 'delta_conv_short_w': 'delta_w', 'delta_w_conv_branch': 'delta_w', 'delta_lru_conv_w': 'delta_w', 'delta_lru_conv_b': 'delta_w', 'delta_lru_wa': 'delta_w', 'delta_lru_ba': 'delta_w', 'delta_lru_wx': 'delta_w', 'delta_lru_bx': 'delta_w', 'delta_lru_lambda': 'delta_w', 'delta_w_lru_branch': 'delta_w', 'delta_w_out': 'delta_w', 'delta_ffn_w_up': 'delta_w', 'delta_ffn_conv_w': 'delta_w', 'delta_ffn_conv_b': 'delta_w', 'delta_ffn_w_down': 'delta_w', 'new_m_norm_mix_pre': 'new_m', 'new_m_norm_mix_post': 'new_m', 'new_m_norm_ffn_pre': 'new_m', 'new_m_norm_ffn_post': 'new_m', 'new_m_w_in': 'new_m', 'new_m_conv_short_w': 'new_m', 'new_m_w_conv_branch': 'new_m', 'new_m_lru_conv_w': 'new_m', 'new_m_lru_conv_b': 'new_m', 'new_m_lru_wa': 'new_m', 'new_m_lru_ba': 'new_m', 'new_m_lru_wx': 'new_m', 'new_m_lru_bx': 'new_m', 'new_m_lru_lambda': 'new_m', 'new_m_w_lru_branch': 'new_m', 'new_m_w_out': 'new_m', 'new_m_ffn_w_up': 'new_m', 'new_m_ffn_conv_w': 'new_m', 'new_m_ffn_conv_b': 'new_m', 'new_m_ffn_w_down': 'new_m', 'new_v_norm_mix_pre': 'new_v', 'new_v_norm_mix_post': 'new_v', 'new_v_norm_ffn_pre': 'new_v', 'new_v_norm_ffn_post': 'new_v', 'new_v_w_in': 'new_v', 'new_v_conv_short_w': 'new_v', 'new_v_w_conv_branch': 'new_v', 'new_v_lru_conv_w': 'new_v', 'new_v_lru_conv_b': 'new_v', 'new_v_lru_wa': 'new_v', 'new_v_lru_ba': 'new_v', 'new_v_lru_wx': 'new_v', 'new_v_lru_bx': 'new_v', 'new_v_lru_lambda': 'new_v', 'new_v_w_lru_branch': 'new_v', 'new_v_w_out': 'new_v', 'new_v_ffn_w_up': 'new_v', 'new_v_ffn_conv_w': 'new_v', 'new_v_ffn_conv_b': 'new_v', 'new_v_ffn_w_down': 'new_v'}


def _forward(args):
    return _fwd_reference(*[args[k] for k in FWD_PARAMS])


def _output_shape():
    out = _jax.eval_shape(lambda: _forward(_fwd_setup_inputs(0)))
    return out.shape, out.dtype

N_MICROBATCH = 1
ADAM_LR = 0.001
ADAM_B1 = 0.9
ADAM_B2 = 0.999
ADAM_EPS = 1e-08
ADAM_WD = 0.01
ADAM_STEP = 10
PER_EXAMPLE_BATCH_AXIS = {'x': 0, 'loss_target': 0}
SHARED_INPUTS = []
_WEIGHT_DTYPES = {'norm_mix_pre': _jnp.float32, 'norm_mix_post': _jnp.float32, 'norm_ffn_pre': _jnp.float32, 'norm_ffn_post': _jnp.float32, 'w_in': _jnp.float32, 'conv_short_w': _jnp.float32, 'w_conv_branch': _jnp.float32, 'lru_conv_w': _jnp.float32, 'lru_conv_b': _jnp.float32, 'lru_wa': _jnp.float32, 'lru_ba': _jnp.float32, 'lru_wx': _jnp.float32, 'lru_bx': _jnp.float32, 'lru_lambda': _jnp.float32, 'w_lru_branch': _jnp.float32, 'w_out': _jnp.float32, 'ffn_w_up': _jnp.float32, 'ffn_conv_w': _jnp.float32, 'ffn_conv_b': _jnp.float32, 'ffn_w_down': _jnp.float32}
MOMENT_SCALE = {'norm_mix_pre': 5.814019e-01, 'norm_mix_post': 1.603664e+01, 'norm_ffn_pre': 3.512444e-01, 'norm_ffn_post': 1.595593e+01, 'w_in': 2.067850e-01, 'conv_short_w': 3.011765e-01, 'w_conv_branch': 2.999619e-01, 'lru_conv_w': 1.239461e-01, 'lru_conv_b': 1.833417e+00, 'lru_wa': 3.742103e-02, 'lru_ba': 2.978942e-02, 'lru_wx': 6.733611e-02, 'lru_bx': 4.489041e-02, 'lru_lambda': 5.243269e-02, 'w_lru_branch': 1.295661e-01, 'w_out': 3.412052e-01, 'ffn_w_up': 1.428337e-01, 'ffn_conv_w': 1.488039e-01, 'ffn_conv_b': 2.460552e-01, 'ffn_w_down': 2.726249e-01}


def _to_microbatches(a, axis):
    t = _jnp.moveaxis(a, axis, 0)
    t = t.reshape((N_MICROBATCH, t.shape[0] // N_MICROBATCH) + t.shape[1:])
    return _jnp.moveaxis(t, 1, axis + 1)


def setup_inputs(seed: int = 0) -> dict:
    inp = _fwd_setup_inputs(seed)
    key = _jax.random.fold_in(_jax.random.key(seed), 7919)
    shape, _ = _output_shape()
    out = dict(inp)
    out["loss_target"] = _jax.random.normal(_jax.random.fold_in(key, 0), shape, _jnp.float32)
    for i, name in enumerate(TWIN_WEIGHTS):
        w = inp[name].astype(_jnp.float32)
        if MOMENT_SCALE is None:
            s = _jnp.sqrt(_jnp.mean(_jnp.square(w)) + 1e-30)
        else:
            s = MOMENT_SCALE[name]
        km, kv = _jax.random.split(_jax.random.fold_in(key, i + 1))
        out[name] = w
        out["m_" + name] = s * _jax.random.normal(km, w.shape, _jnp.float32)
        out["v_" + name] = (s * s) * _jax.random.uniform(kv, w.shape, _jnp.float32, 0.5, 1.5)
    if N_MICROBATCH > 1:
        for name, axis in PER_EXAMPLE_BATCH_AXIS.items():
            out[name] = _to_microbatches(out[name], axis)
    return {'x': out['x'], 'norm_mix_pre': out['norm_mix_pre'], 'norm_mix_post': out['norm_mix_post'], 'norm_ffn_pre': out['norm_ffn_pre'], 'norm_ffn_post': out['norm_ffn_post'], 'w_in': out['w_in'], 'conv_short_w': out['conv_short_w'], 'w_conv_branch': out['w_conv_branch'], 'lru_conv_w': out['lru_conv_w'], 'lru_conv_b': out['lru_conv_b'], 'lru_wa': out['lru_wa'], 'lru_ba': out['lru_ba'], 'lru_wx': out['lru_wx'], 'lru_bx': out['lru_bx'], 'lru_lambda': out['lru_lambda'], 'w_lru_branch': out['w_lru_branch'], 'w_out': out['w_out'], 'ffn_w_up': out['ffn_w_up'], 'ffn_conv_w': out['ffn_conv_w'], 'ffn_conv_b': out['ffn_conv_b'], 'ffn_w_down': out['ffn_w_down'], 'loss_target': out['loss_target'], 'm_norm_mix_pre': out['m_norm_mix_pre'], 'm_norm_mix_post': out['m_norm_mix_post'], 'm_norm_ffn_pre': out['m_norm_ffn_pre'], 'm_norm_ffn_post': out['m_norm_ffn_post'], 'm_w_in': out['m_w_in'], 'm_conv_short_w': out['m_conv_short_w'], 'm_w_conv_branch': out['m_w_conv_branch'], 'm_lru_conv_w': out['m_lru_conv_w'], 'm_lru_conv_b': out['m_lru_conv_b'], 'm_lru_wa': out['m_lru_wa'], 'm_lru_ba': out['m_lru_ba'], 'm_lru_wx': out['m_lru_wx'], 'm_lru_bx': out['m_lru_bx'], 'm_lru_lambda': out['m_lru_lambda'], 'm_w_lru_branch': out['m_w_lru_branch'], 'm_w_out': out['m_w_out'], 'm_ffn_w_up': out['m_ffn_w_up'], 'm_ffn_conv_w': out['m_ffn_conv_w'], 'm_ffn_conv_b': out['m_ffn_conv_b'], 'm_ffn_w_down': out['m_ffn_w_down'], 'v_norm_mix_pre': out['v_norm_mix_pre'], 'v_norm_mix_post': out['v_norm_mix_post'], 'v_norm_ffn_pre': out['v_norm_ffn_pre'], 'v_norm_ffn_post': out['v_norm_ffn_post'], 'v_w_in': out['v_w_in'], 'v_conv_short_w': out['v_conv_short_w'], 'v_w_conv_branch': out['v_w_conv_branch'], 'v_lru_conv_w': out['v_lru_conv_w'], 'v_lru_conv_b': out['v_lru_conv_b'], 'v_lru_wa': out['v_lru_wa'], 'v_lru_ba': out['v_lru_ba'], 'v_lru_wx': out['v_lru_wx'], 'v_lru_bx': out['v_lru_bx'], 'v_lru_lambda': out['v_lru_lambda'], 'v_w_lru_branch': out['v_w_lru_branch'], 'v_w_out': out['v_w_out'], 'v_ffn_w_up': out['v_ffn_w_up'], 'v_ffn_conv_w': out['v_ffn_conv_w'], 'v_ffn_conv_b': out['v_ffn_conv_b'], 'v_ffn_w_down': out['v_ffn_w_down']}


def _loss(weights, diff, rest, loss_target):
    with _jax.named_scope("forward"):
        args = {**rest, TWIN_DIFF_INPUT: diff, **{k: w.astype(_WEIGHT_DTYPES[k]) for k, w in weights.items()}}
        y = _forward(args)
    with _jax.named_scope("loss_head"):
        err = _jnp.square(y.astype(_jnp.float32) - loss_target)
        return 0.5 * _jnp.sum(_jnp.mean(err, axis=-1)) if err.ndim else 0.5 * err


def _adamw(w, g, m, v):
    m = ADAM_B1 * m + (1.0 - ADAM_B1) * g
    v = ADAM_B2 * v + (1.0 - ADAM_B2) * _jnp.square(g)
    m_hat = m / (1.0 - ADAM_B1 ** ADAM_STEP)
    v_hat = v / (1.0 - ADAM_B2 ** ADAM_STEP)
    delta = -ADAM_LR * (m_hat / (_jnp.sqrt(v_hat) + ADAM_EPS) + ADAM_WD * w)
    return delta, m, v


def reference(x, norm_mix_pre, norm_mix_post, norm_ffn_pre, norm_ffn_post, w_in, conv_short_w, w_conv_branch, lru_conv_w, lru_conv_b, lru_wa, lru_ba, lru_wx, lru_bx, lru_lambda, w_lru_branch, w_out, ffn_w_up, ffn_conv_w, ffn_conv_b, ffn_w_down, loss_target, m_norm_mix_pre, m_norm_mix_post, m_norm_ffn_pre, m_norm_ffn_post, m_w_in, m_conv_short_w, m_w_conv_branch, m_lru_conv_w, m_lru_conv_b, m_lru_wa, m_lru_ba, m_lru_wx, m_lru_bx, m_lru_lambda, m_w_lru_branch, m_w_out, m_ffn_w_up, m_ffn_conv_w, m_ffn_conv_b, m_ffn_w_down, v_norm_mix_pre, v_norm_mix_post, v_norm_ffn_pre, v_norm_ffn_post, v_w_in, v_conv_short_w, v_w_conv_branch, v_lru_conv_w, v_lru_conv_b, v_lru_wa, v_lru_ba, v_lru_wx, v_lru_bx, v_lru_lambda, v_w_lru_branch, v_w_out, v_ffn_w_up, v_ffn_conv_w, v_ffn_conv_b, v_ffn_w_down):
    given = dict(x=x, norm_mix_pre=norm_mix_pre, norm_mix_post=norm_mix_post, norm_ffn_pre=norm_ffn_pre, norm_ffn_post=norm_ffn_post, w_in=w_in, conv_short_w=conv_short_w, w_conv_branch=w_conv_branch, lru_conv_w=lru_conv_w, lru_conv_b=lru_conv_b, lru_wa=lru_wa, lru_ba=lru_ba, lru_wx=lru_wx, lru_bx=lru_bx, lru_lambda=lru_lambda, w_lru_branch=w_lru_branch, w_out=w_out, ffn_w_up=ffn_w_up, ffn_conv_w=ffn_conv_w, ffn_conv_b=ffn_conv_b, ffn_w_down=ffn_w_down, loss_target=loss_target, m_norm_mix_pre=m_norm_mix_pre, m_norm_mix_post=m_norm_mix_post, m_norm_ffn_pre=m_norm_ffn_pre, m_norm_ffn_post=m_norm_ffn_post, m_w_in=m_w_in, m_conv_short_w=m_conv_short_w, m_w_conv_branch=m_w_conv_branch, m_lru_conv_w=m_lru_conv_w, m_lru_conv_b=m_lru_conv_b, m_lru_wa=m_lru_wa, m_lru_ba=m_lru_ba, m_lru_wx=m_lru_wx, m_lru_bx=m_lru_bx, m_lru_lambda=m_lru_lambda, m_w_lru_branch=m_w_lru_branch, m_w_out=m_w_out, m_ffn_w_up=m_ffn_w_up, m_ffn_conv_w=m_ffn_conv_w, m_ffn_conv_b=m_ffn_conv_b, m_ffn_w_down=m_ffn_w_down, v_norm_mix_pre=v_norm_mix_pre, v_norm_mix_post=v_norm_mix_post, v_norm_ffn_pre=v_norm_ffn_pre, v_norm_ffn_post=v_norm_ffn_post, v_w_in=v_w_in, v_conv_short_w=v_conv_short_w, v_w_conv_branch=v_w_conv_branch, v_lru_conv_w=v_lru_conv_w, v_lru_conv_b=v_lru_conv_b, v_lru_wa=v_lru_wa, v_lru_ba=v_lru_ba, v_lru_wx=v_lru_wx, v_lru_bx=v_lru_bx, v_lru_lambda=v_lru_lambda, v_w_lru_branch=v_w_lru_branch, v_w_out=v_w_out, v_ffn_w_up=v_ffn_w_up, v_ffn_conv_w=v_ffn_conv_w, v_ffn_conv_b=v_ffn_conv_b, v_ffn_w_down=v_ffn_w_down)
    weights = {n: given[n] for n in TWIN_WEIGHTS}
    shared = {n: given[n] for n in SHARED_INPUTS}
    per_example = {n: given[n] for n in ['x']}
    grad_fn = _jax.value_and_grad(_loss, argnums=(0, 1))

    def one_microbatch(ex, loss_target):
        ex = dict(ex)
        diff = ex.pop(TWIN_DIFF_INPUT)
        return grad_fn(weights, diff, {**shared, **ex}, loss_target)

    if N_MICROBATCH == 1:
        loss, (grad_w, grad_x) = one_microbatch(per_example, given["loss_target"])
    else:
        def body(carry, xs):
            loss_sum, grad_sum = carry
            l_k, (gw_k, gx_k) = one_microbatch(xs[0], xs[1])
            with _jax.named_scope("update"):
                return (loss_sum + l_k, _jax.tree.map(_jnp.add, grad_sum, gw_k)), gx_k

        init = (_jnp.zeros((), _jnp.float32), _jax.tree.map(_jnp.zeros_like, weights))
        (loss, grad_w), grad_x = _jax.lax.scan(body, init, (per_example, given["loss_target"]))
    with _jax.named_scope("update"):
        delta_w, new_m, new_v = {}, {}, {}
        for n in TWIN_WEIGHTS:
            delta_w[n], new_m[n], new_v[n] = _adamw(weights[n], grad_w[n], given["m_" + n], given["v_" + n])
    return (loss, grad_x, *[grad_w[n] for n in TWIN_WEIGHTS], *[delta_w[n] for n in TWIN_WEIGHTS],
            *[new_m[n] for n in TWIN_WEIGHTS], *[new_v[n] for n in TWIN_WEIGHTS])
```

```python
import functools
import math

import jax
import jax.numpy as jnp
from jax import lax
from jax.experimental import pallas as pl
from jax.experimental.pallas import tpu as pltpu

F32 = jnp.float32
BF16 = jnp.bfloat16
MESH = pl.DeviceIdType.MESH

N_DEV = 8
D_MODEL = 1024
N_HEADS = 4
HEAD_DIM = D_MODEL // N_HEADS
D_FF = 3 * D_MODEL
IN_COLS = 7 * D_MODEL
LRU_C = 8.0
RMS_EPS = 1e-6
ADAM_LR = 0.001
ADAM_B1 = 0.9
ADAM_B2 = 0.999
ADAM_EPS = 1e-08
ADAM_WD = 0.01
ADAM_STEP = 10
GELU_K = math.sqrt(2.0 / math.pi)
GELU_C = 0.044715

LANES = 128
SUBLANES = 8
PAD = SUBLANES
VMEM_LIMIT = 56 * 1024 * 1024
CB = 256

HBM_SPEC = pl.BlockSpec(memory_space=pltpu.HBM)
VMEM_SPEC = pl.BlockSpec(memory_space=pltpu.VMEM)


def _params(*sem):
    if sem:
        return pltpu.CompilerParams(dimension_semantics=sem, vmem_limit_bytes=VMEM_LIMIT)
    return pltpu.CompilerParams(vmem_limit_bytes=VMEM_LIMIT)


def _row_chunk(t):
    return min(256, t)


def _row_block(rows, cap):
    return next(rb for rb in range(min(cap, rows), 0, -16) if rows % rb == 0)


def _gelu(x):
    return 0.5 * x * (1.0 + jnp.tanh(GELU_K * (x + GELU_C * x * x * x)))


def _gelu_and_grad(x):
    t = jnp.tanh(GELU_K * (x + GELU_C * x * x * x))
    g = 0.5 * x * (1.0 + t)
    dg = 0.5 * (1.0 + t) + 0.5 * x * (1.0 - t * t) * GELU_K * (1.0 + 3.0 * GELU_C * x * x)
    return g, dg


def _expm1_neg(x):
    series = x * (1.0 + x * (0.5 + x * (1.0 / 6.0 + x * (1.0 / 24.0 + x * (1.0 / 120.0)))))
    return jnp.where(x > -0.05, series, jnp.exp(x) - 1.0)


def _log_sigmoid(x):
    return jnp.minimum(x, 0.0) - jnp.log1p(jnp.exp(-jnp.abs(x)))


def _dot(a, b):
    return jnp.dot(a, b, preferred_element_type=F32)


def _dot_nt(a, b):
    return lax.dot_general(a, b, (((1,), (1,)), ((), ())), preferred_element_type=F32)


def _dot_tn(a, b):
    return lax.dot_general(a, b, (((0,), (0,)), ((), ())), preferred_element_type=F32)


def _rms_fwd(x):
    r = lax.rsqrt(jnp.mean(x * x, axis=-1, keepdims=True) + RMS_EPS)
    return x * r, r


def _rms_bwd(n, r, gdy):
    return r * (gdy - n * jnp.mean(n * gdy, axis=-1, keepdims=True))


def _conv_causal(pad_ref, w, r0, rows, taps):
    acc = None
    for k in range(taps):
        term = w[k:k + 1, :] * pad_ref[pl.ds(PAD + r0 - (taps - 1 - k), rows), :]
        acc = term if acc is None else acc + term
    return acc


def _conv_anticausal(pad_ref, w, r0, rows, taps):
    acc = None
    for k in range(taps):
        term = w[k:k + 1, :] * pad_ref[pl.ds(r0 + (taps - 1 - k), rows), :]
        acc = term if acc is None else acc + term
    return acc


def _conv_wgrad(g, xpad_ref, r0, rows, taps):
    return [jnp.sum(g * xpad_ref[pl.ds(PAD + r0 - (taps - 1 - k), rows), :], axis=0, keepdims=True)
            for k in range(taps)]


def _position():
    return lax.axis_index("x"), lax.axis_index("y"), lax.axis_index("c")


def _block_of(x, y, c):
    return 4 * x + 2 * y + c


def _chip(x, y, k):
    return (x + (k & 1)) % 2, (y + (k >> 1)) % 2


def _cols(width):
    def at(ref, d):
        return ref.at[:, pl.ds(pl.multiple_of(d * width, LANES), width)]
    return at


def _rows(height):
    def at(ref, d):
        return ref.at[pl.ds(pl.multiple_of(d * height, 16), height), :]
    return at


def _lead(ref, d):
    return ref.at[d]


def _gather_weights(shards, blocks, full_shapes, small):
    n = len(shards)
    small_rows = small.shape[0]

    def body(*refs):
        ins, small_in = refs[:n], refs[n]
        outs, small_out = refs[n + 1:2 * n + 1], refs[2 * n + 1]
        stage = refs[2 * n + 2:3 * n + 2]
        send, recv, local = refs[3 * n + 2:]
        x, y, c = _position()
        me = _block_of(x, y, c)
        sibling = (x, y, 1 - c)

        for a in range(n):
            stage[a][...] = ins[a][...].astype(BF16)

        def copy(a, k, block, to, src=None):
            dst = blocks[a](outs[a], block)
            return pltpu.make_async_remote_copy(
                src_ref=dst if src is None else src, dst_ref=dst, send_sem=send.at[a, k], recv_sem=recv.at[a, k],
                device_id=to, device_id_type=MESH)

        def small_copy(k):
            px, py, pc = (x + (k & 1)) % 2, (y + ((k >> 1) & 1)) % 2, (c + (k >> 2)) % 2
            return pltpu.make_async_remote_copy(
                src_ref=small_in, dst_ref=small_out.at[me], send_sem=send.at[n, k - 1], recv_sem=recv.at[n, k - 1],
                device_id=(px, py, pc), device_id_type=MESH)

        def small_arrival(k):
            px, py, pc = (x + (k & 1)) % 2, (y + ((k >> 1) & 1)) % 2, (c + (k >> 2)) % 2
            return pltpu.make_async_remote_copy(
                src_ref=small_in, dst_ref=small_out.at[_block_of(px, py, pc)], send_sem=send.at[n, k - 1],
                recv_sem=recv.at[n, k - 1], device_id=(px, py, pc), device_id_type=MESH)

        small_out[me] = small_in[...]
        small_sends = [small_copy(k) for k in range(1, N_DEV)]
        for cp in small_sends:
            cp.start()

        mine, first, passed = [], [], []
        for a in range(n):
            own = pltpu.make_async_copy(stage[a], blocks[a](outs[a], me), local.at[a])
            own.start()
            mine.append(own)
            sends = [copy(a, 0, me, sibling, src=stage[a])]
            sends += [copy(a, k, me, (*_chip(x, y, k), c), src=stage[a]) for k in (1, 2, 3)]
            for cp in sends:
                cp.start()
            first += sends
        for a in range(n):
            for k in (1, 2, 3):
                landed = _block_of(*_chip(x, y, k), c)
                copy(a, k, landed, (x, y, c)).wait_recv()
                fwd = copy(a, 3 + k, landed, sibling)
                fwd.start()
                passed.append(fwd)
        for a in range(n):
            copy(a, 0, _block_of(x, y, 1 - c), (x, y, c)).wait_recv()
            for k in (1, 2, 3):
                copy(a, 3 + k, _block_of(*_chip(x, y, k), 1 - c), (x, y, c)).wait_recv()
        for k in range(1, N_DEV):
            small_arrival(k).wait_recv()
        for cp in first + passed + small_sends:
            cp.wait_send()
        for own in mine:
            own.wait()

    out_shape = [jax.ShapeDtypeStruct(s, BF16) for s in full_shapes]
    out_shape.append(jax.ShapeDtypeStruct((N_DEV, small_rows, LANES), F32))
    return pl.pallas_call(
        body, name="gather_weights", out_shape=out_shape,
        in_specs=[VMEM_SPEC] * (n + 1), out_specs=[HBM_SPEC] * n + [VMEM_SPEC],
        scratch_shapes=[pltpu.VMEM(s.shape, BF16) for s in shards]
        + [pltpu.SemaphoreType.DMA((n + 1, 7)), pltpu.SemaphoreType.DMA((n + 1, 7)), pltpu.SemaphoreType.DMA((n,))],
        compiler_params=_params(),
    )(*shards, small)


def _exchange_pair(grads, blocks, shard_shapes):
    n = len(grads)

    def body(*refs):
        ins = refs[:n]
        kept, got = refs[n:2 * n], refs[2 * n:3 * n]
        send, recv, local = refs[3 * n:]
        x, y, c = _position()
        copies, moves = [], []
        for a in range(n):
            for k in range(4):
                chip = _chip(x, y, k)
                mv = pltpu.make_async_copy(blocks[a](ins[a], _block_of(*chip, c)), kept[a].at[k], local.at[a, k])
                mv.start()
                moves.append(mv)
                cp = pltpu.make_async_remote_copy(
                    src_ref=blocks[a](ins[a], _block_of(*chip, 1 - c)), dst_ref=got[a].at[k],
                    send_sem=send.at[a, k], recv_sem=recv.at[a, k], device_id=(x, y, 1 - c), device_id_type=MESH)
                cp.start()
                copies.append(cp)
        for cp in copies:
            cp.wait()
        for mv in moves:
            mv.wait()

    slots = [jax.ShapeDtypeStruct((4,) + tuple(s), BF16) for s in shard_shapes]
    return pl.pallas_call(
        body, name="reduce_pair_exchange", out_shape=slots + slots,
        in_specs=[HBM_SPEC] * n, out_specs=[HBM_SPEC] * (2 * n),
        scratch_shapes=[pltpu.SemaphoreType.DMA((n, 4)), pltpu.SemaphoreType.DMA((n, 4)), pltpu.SemaphoreType.DMA((n, 4))],
        compiler_params=_params(),
    )(*grads)


def _exchange_chips(pair_sums):
    n = len(pair_sums)

    def body(*refs):
        ins, outs = refs[:n], refs[n:2 * n]
        send, recv = refs[2 * n:]
        x, y, c = _position()
        copies = []
        for a in range(n):
            for k in (1, 2, 3):
                cp = pltpu.make_async_remote_copy(
                    src_ref=ins[a].at[k], dst_ref=outs[a].at[k - 1], send_sem=send.at[a, k - 1],
                    recv_sem=recv.at[a, k - 1], device_id=(*_chip(x, y, k), c), device_id_type=MESH)
                cp.start()
                copies.append(cp)
        for cp in copies:
            cp.wait()

    return pl.pallas_call(
        body, name="reduce_chip_exchange",
        out_shape=[jax.ShapeDtypeStruct((3,) + tuple(p.shape[1:]), BF16) for p in pair_sums],
        in_specs=[HBM_SPEC] * n, out_specs=[HBM_SPEC] * n,
        scratch_shapes=[pltpu.SemaphoreType.DMA((n, 3)), pltpu.SemaphoreType.DMA((n, 3))],
        compiler_params=_params(),
    )(*pair_sums)


def _allreduce_small(part):
    rows = part.shape[0]

    def body(in_ref, out_ref, buf, send, recv):
        x, y, c = _position()
        me = _block_of(x, y, c)

        def peer(k):
            return (x + (k & 1)) % 2, (y + ((k >> 1) & 1)) % 2, (c + (k >> 2)) % 2

        sends = []
        for k in range(1, N_DEV):
            cp = pltpu.make_async_remote_copy(src_ref=in_ref, dst_ref=buf.at[me], send_sem=send.at[k - 1],
                                              recv_sem=recv.at[k - 1], device_id=peer(k), device_id_type=MESH)
            cp.start()
            sends.append(cp)
        buf[me] = in_ref[...]
        for k in range(1, N_DEV):
            pltpu.make_async_remote_copy(src_ref=in_ref, dst_ref=buf.at[_block_of(*peer(k))], send_sem=send.at[k - 1],
                                         recv_sem=recv.at[k - 1], device_id=peer(k), device_id_type=MESH).wait_recv()
        total = buf[0]
        for d in range(1, N_DEV):
            total = total + buf[d]
        out_ref[...] = total
        for cp in sends:
            cp.wait_send()

    return pl.pallas_call(
        body, name="allreduce_small", out_shape=jax.ShapeDtypeStruct(part.shape, F32),
        in_specs=[VMEM_SPEC], out_specs=VMEM_SPEC,
        scratch_shapes=[pltpu.VMEM((N_DEV, rows, LANES), F32), pltpu.SemaphoreType.DMA((7,)), pltpu.SemaphoreType.DMA((7,))],
        compiler_params=_params(),
    )(part)


def _in_proj(x, g1, w_in):
    t = x.shape[0]
    tm, bn = min(512, t), 1024

    def body(x_ref, g_ref, w_ref, proj_ref, h_ref, h_s):
        @pl.when(pl.program_id(1) == 0)
        def _():
            n, _ = _rms_fwd(x_ref[...])
            h_s[...] = (n * g_ref[...]).astype(BF16)
            h_ref[...] = h_s[...]
        proj_ref[...] = _dot(h_s[...], w_ref[...]).astype(BF16)

    return pl.pallas_call(
        body, name="in_proj", grid=(t // tm, IN_COLS // bn),
        out_shape=[jax.ShapeDtypeStruct((t, IN_COLS), BF16), jax.ShapeDtypeStruct((t, D_MODEL), BF16)],
        in_specs=[pl.BlockSpec((tm, D_MODEL), lambda i, j: (i, 0)), pl.BlockSpec((1, D_MODEL), lambda i, j: (0, 0)),
                  pl.BlockSpec((D_MODEL, bn), lambda i, j: (0, j))],
        out_specs=[pl.BlockSpec((tm, bn), lambda i, j: (i, j)), pl.BlockSpec((tm, D_MODEL), lambda i, j: (i, 0))],
        scratch_shapes=[pltpu.VMEM((tm, D_MODEL), BF16)],
        compiler_params=_params("parallel", "arbitrary"),
    )(x, g1, w_in)


def _section(s, t):
    return pl.BlockSpec((t, CB), lambda h, s=s: (0, s * (D_MODEL // CB) + h))


def _conv_mixer_fwd(proj, w_short):
    t = proj.shape[0]
    rc = _row_chunk(t)

    def body(b_ref, c_ref, x_ref, w_ref, y_ref, pad):
        pad[pl.ds(0, PAD), :] = jnp.zeros((PAD, CB), F32)
        for r0 in range(0, t, rc):
            rows = pl.ds(r0, rc)
            pad[pl.ds(PAD + r0, rc), :] = c_ref[rows, :].astype(F32) * x_ref[rows, :].astype(F32)
        w = w_ref[...]
        for r0 in range(0, t, rc):
            rows = pl.ds(r0, rc)
            y_ref[rows, :] = (b_ref[rows, :].astype(F32) * _conv_causal(pad, w, r0, rc, 3)).astype(BF16)

    return pl.pallas_call(
        body, name="conv_mixer_fwd", grid=(D_MODEL // CB,),
        out_shape=jax.ShapeDtypeStruct((t, D_MODEL), BF16),
        in_specs=[_section(0, t), _section(1, t), _section(2, t), pl.BlockSpec((3, CB), lambda h: (0, h))],
        out_specs=pl.BlockSpec((t, CB), lambda h: (0, h)),
        scratch_shapes=[pltpu.VMEM((t + PAD, CB), F32)],
        compiler_params=_params("parallel"),
    )(proj, proj, proj, w_short)


def _lru_gates(xl, wa, ba, wx, bx, ls, first_row):
    xb = xl.astype(BF16)
    ra = jax.nn.sigmoid(_dot(xb, wa) + ba)
    ia = jax.nn.sigmoid(_dot(xb, wx) + bx)
    la = LRU_C * ra * ls
    a = jnp.exp(la)
    one_minus = -_expm1_neg(2.0 * la)
    mult = jnp.where(first_row, 1.0, jnp.sqrt(one_minus))
    return xb, ra, ia, a, one_minus, mult


def _head_specs():
    vec = pl.BlockSpec((1, CB), lambda h: (0, h))
    mat = pl.BlockSpec((N_DEV, None, HEAD_DIM // N_DEV, HEAD_DIM), lambda h: (0, h, 0, 0))
    return vec, mat


def _lru_fwd(proj, w_conv, b_conv, wa, ba, wx, bx, lam):
    t = proj.shape[0]
    rc = _row_chunk(t)
    vec, mat = _head_specs()

    def body(lx_ref, ly_ref, wc_ref, bc_ref, wa_ref, ba_ref, wx_ref, bx_ref, lam_ref, yb_ref, hl_ref, pad, a_s, u_s):
        pad[pl.ds(0, PAD), :] = jnp.zeros((PAD, CB), F32)
        for r0 in range(0, t, rc):
            pad[pl.ds(PAD + r0, rc), :] = lx_ref[pl.ds(r0, rc), :].astype(F32)
        wc, bc = wc_ref[...], bc_ref[...]
        wa_m, wx_m = wa_ref[...].reshape(HEAD_DIM, HEAD_DIM), wx_ref[...].reshape(HEAD_DIM, HEAD_DIM)
        ls = _log_sigmoid(lam_ref[...])
        for r0 in range(0, t, rc):
            xl = _conv_causal(pad, wc, r0, rc, 4) + bc
            first = (lax.broadcasted_iota(jnp.int32, (rc, CB), 0) + r0) == 0
            _, _, ia, a, _, mult = _lru_gates(xl, wa_m, ba_ref[...], wx_m, bx_ref[...], ls, first)
            a_s[pl.ds(r0, rc), :] = a
            u_s[pl.ds(r0, rc), :] = mult * (ia * xl)

        row = lax.broadcasted_iota(jnp.int32, (SUBLANES, CB), 0)

        def group(g, carry):
            r = pl.multiple_of(g * SUBLANES, SUBLANES)
            a_g, b_g = a_s[pl.ds(r, SUBLANES), :], u_s[pl.ds(r, SUBLANES), :]
            for s in (1, 2, 4):
                keep = row >= s
                b_g = jnp.where(keep, a_g * pltpu.roll(b_g, s, 0) + b_g, b_g)
                a_g = jnp.where(keep, a_g * pltpu.roll(a_g, s, 0), a_g)
            h_g = b_g + a_g * carry
            hl_ref[pl.ds(r, SUBLANES), :] = h_g
            return jnp.broadcast_to(h_g[SUBLANES - 1:SUBLANES, :], (SUBLANES, CB))

        lax.fori_loop(0, t // SUBLANES, group, jnp.zeros((SUBLANES, CB), F32))
        for r0 in range(0, t, rc):
            rows = pl.ds(r0, rc)
            yb_ref[rows, :] = (hl_ref[rows, :] * _gelu(ly_ref[rows, :].astype(F32))).astype(BF16)

    blk = pl.BlockSpec((t, CB), lambda h: (0, h))
    return pl.pallas_call(
        body, name="lru_fwd", grid=(N_HEADS,),
        out_shape=[jax.ShapeDtypeStruct((t, D_MODEL), BF16), jax.ShapeDtypeStruct((t, D_MODEL), F32)],
        in_specs=[_section(3, t), _section(4, t), pl.BlockSpec((4, CB), lambda h: (0, h)), vec, mat, vec, mat, vec, vec],
        out_specs=[blk, blk],
        scratch_shapes=[pltpu.VMEM((t + PAD, CB), F32), pltpu.VMEM((t, CB), F32), pltpu.VMEM((t, CB), F32)],
        compiler_params=_params("parallel"),
    )(proj, proj, w_conv, b_conv, wa, ba, wx, bx, lam)


def _merge(y_a, y_b, proj, x, w_cb, w_lb, w_out, g2, g3):
    t = x.shape[0]
    tm = min(256, t)

    def body(ya_ref, yb_ref, gc_ref, gl_ref, x_ref, wcb_ref, wlb_ref, wo_ref, g2_ref, g3_ref,
             pa_ref, pb_ref, mg_ref, mix_ref, x1_ref, h2_ref):
        pa = _dot(ya_ref[...], wcb_ref[...]).astype(BF16)
        pb = _dot(yb_ref[...], wlb_ref[...]).astype(BF16)
        pa_ref[...] = pa
        pb_ref[...] = pb
        merged = (jax.nn.sigmoid(gc_ref[...].astype(F32)) * pa.astype(F32)
                  + jax.nn.sigmoid(gl_ref[...].astype(F32)) * pb.astype(F32)).astype(BF16)
        mg_ref[...] = merged
        mix = _dot(merged, wo_ref[...])
        mix_ref[...] = mix
        n2, _ = _rms_fwd(mix)
        x1 = x_ref[...] + n2 * g2_ref[...]
        x1_ref[...] = x1
        n3, _ = _rms_fwd(x1)
        h2_ref[...] = (n3 * g3_ref[...]).astype(BF16)

    row = pl.BlockSpec((tm, D_MODEL), lambda i: (i, 0))
    full = pl.BlockSpec((D_MODEL, D_MODEL), lambda i: (0, 0))
    vec = pl.BlockSpec((1, D_MODEL), lambda i: (0, 0))
    act = jax.ShapeDtypeStruct((t, D_MODEL), BF16)
    res = jax.ShapeDtypeStruct((t, D_MODEL), F32)
    return pl.pallas_call(
        body, name="merge_fwd", grid=(t // tm,), out_shape=[act, act, act, res, res, act],
        in_specs=[row, row, pl.BlockSpec((tm, D_MODEL), lambda i: (i, 5)), pl.BlockSpec((tm, D_MODEL), lambda i: (i, 6)),
                  row, full, full, full, vec, vec],
        out_specs=[row] * 6,
        compiler_params=_params("parallel"),
    )(y_a, y_b, proj, proj, x, w_cb, w_lb, w_out, g2, g3)


N_FF_BLOCKS = D_FF // CB


def _ffn_up(h2, w_up, w_conv, b_conv):
    t = h2.shape[0]
    rc = _row_chunk(t)

    def body(h_ref, wg_ref, wv_ref, cg_ref, cv_ref, bg_ref, bv_ref, ug_ref, uv_ref, f_ref, pad_g, pad_v):
        zeros = jnp.zeros((PAD, CB), F32)
        pad_g[pl.ds(0, PAD), :] = zeros
        pad_v[pl.ds(0, PAD), :] = zeros
        for r0 in range(0, t, rc):
            rows = pl.ds(r0, rc)
            ug = _dot(h_ref[rows, :], wg_ref[...]).astype(BF16)
            uv = _dot(h_ref[rows, :], wv_ref[...]).astype(BF16)
            ug_ref[rows, :] = ug
            uv_ref[rows, :] = uv
            pad_g[pl.ds(PAD + r0, rc), :] = ug.astype(F32)
            pad_v[pl.ds(PAD + r0, rc), :] = uv.astype(F32)
        cg, cv = cg_ref[...], cv_ref[...]
        for r0 in range(0, t, rc):
            gate = _conv_causal(pad_g, cg, r0, rc, 3) + bg_ref[...]
            val = _conv_causal(pad_v, cv, r0, rc, 3) + bv_ref[...]
            f_ref[pl.ds(r0, rc), :] = (_gelu(gate) * val).astype(BF16)

    nb = N_FF_BLOCKS
    act = jax.ShapeDtypeStruct((t, D_FF), BF16)
    blk = pl.BlockSpec((t, CB), lambda j: (0, j))
    return pl.pallas_call(
        body, name="ffn_up_fwd", grid=(nb,), out_shape=[act, act, act],
        in_specs=[pl.BlockSpec((t, D_MODEL), lambda j: (0, 0)),
                  pl.BlockSpec((D_MODEL, CB), lambda j: (0, j)), pl.BlockSpec((D_MODEL, CB), lambda j: (0, nb + j)),
                  pl.BlockSpec((3, CB), lambda j: (0, j)), pl.BlockSpec((3, CB), lambda j: (0, nb + j)),
                  pl.BlockSpec((1, CB), lambda j: (0, j)), pl.BlockSpec((1, CB), lambda j: (0, nb + j))],
        out_specs=[blk, blk, blk],
        scratch_shapes=[pltpu.VMEM((t + PAD, CB), F32), pltpu.VMEM((t + PAD, CB), F32)],
        compiler_params=_params("parallel"),
    )(h2, w_up, w_up, w_conv, w_conv, b_conv, b_conv)


def _ffn_down(f, w_down, x1, target, g4):
    t = f.shape[0]
    tm = min(256, t)

    def body(f_ref, w_ref, x1_ref, tg_ref, g_ref, dy_ref, dout_ref, df_ref, dg_ref, loss_ref):
        @pl.when(pl.program_id(0) == 0)
        def _():
            dg_ref[...] = jnp.zeros_like(dg_ref)
            loss_ref[...] = jnp.zeros_like(loss_ref)
        out = _dot(f_ref[...], w_ref[...])
        n4, r4 = _rms_fwd(out)
        err = x1_ref[...] + n4 * g_ref[...] - tg_ref[...]
        loss_ref[...] += jnp.full(loss_ref.shape, 0.5 / D_MODEL, F32) * jnp.sum(err * err)
        dy = err * (1.0 / D_MODEL)
        dy_ref[...] = dy
        dg_ref[...] += jnp.sum(dy * n4, axis=0, keepdims=True)
        d_out = _rms_bwd(n4, r4, dy * g_ref[...]).astype(BF16)
        dout_ref[...] = d_out
        df_ref[...] = _dot_nt(d_out, w_ref[...]).astype(BF16)

    row = pl.BlockSpec((tm, D_MODEL), lambda i: (i, 0))
    wide = pl.BlockSpec((tm, D_FF), lambda i: (i, 0))
    vec = pl.BlockSpec((1, D_MODEL), lambda i: (0, 0))
    return pl.pallas_call(
        body, name="ffn_down_fwd_bwd", grid=(t // tm,),
        out_shape=[jax.ShapeDtypeStruct((t, D_MODEL), F32), jax.ShapeDtypeStruct((t, D_MODEL), BF16),
                   jax.ShapeDtypeStruct((t, D_FF), BF16), jax.ShapeDtypeStruct((1, D_MODEL), F32),
                   jax.ShapeDtypeStruct((SUBLANES, LANES), F32)],
        in_specs=[wide, pl.BlockSpec((D_FF, D_MODEL), lambda i: (0, 0)), row, row, vec],
        out_specs=[row, row, wide, vec, pl.BlockSpec((SUBLANES, LANES), lambda i: (0, 0))],
        compiler_params=_params("arbitrary"),
    )(f, w_down, x1, target, g4)


def _grad_tn(a, b, bm, name):
    t, m = a.shape
    n = b.shape[1]

    def body(a_ref, b_ref, o_ref):
        o_ref[...] = _dot_tn(a_ref[...], b_ref[...]).astype(BF16)

    return pl.pallas_call(
        body, name=name, grid=(m // bm,), out_shape=jax.ShapeDtypeStruct((m, n), BF16),
        in_specs=[pl.BlockSpec((t, bm), lambda i: (0, i)), pl.BlockSpec((t, n), lambda i: (0, 0))],
        out_specs=pl.BlockSpec((bm, n), lambda i: (i, 0)),
        compiler_params=_params("parallel"),
    )(a, b)


def _ffn_up_bwd(up_g, up_v, d_f, w_conv, b_conv, h2, w_up):
    t = h2.shape[0]
    rc = _row_chunk(t)
    nb = N_FF_BLOCKS

    def body(ug_ref, uv_ref, df_ref, cg_ref, cv_ref, bg_ref, bv_ref, h_ref, w_ref,
             dw_ref, dcw_ref, dcb_ref, dh_ref, d_up, small, pad_g, pad_v, back_g, back_v):
        j, k = pl.program_id(0), pl.program_id(1)

        @pl.when((j == 0) & (k == 0))
        def _():
            dh_ref[...] = jnp.zeros_like(dh_ref)

        @pl.when(k == 0)
        def _():
            zeros = jnp.zeros((PAD, CB), F32)
            pad_g[pl.ds(0, PAD), :] = zeros
            pad_v[pl.ds(0, PAD), :] = zeros
            back_g[pl.ds(t, PAD), :] = zeros
            back_v[pl.ds(t, PAD), :] = zeros
            for r0 in range(0, t, rc):
                pad_g[pl.ds(PAD + r0, rc), :] = ug_ref[pl.ds(r0, rc), :].astype(F32)
                pad_v[pl.ds(PAD + r0, rc), :] = uv_ref[pl.ds(r0, rc), :].astype(F32)
            cg, cv = cg_ref[...], cv_ref[...]
            for r0 in range(0, t, rc):
                rows = pl.ds(r0, rc)
                gate = _conv_causal(pad_g, cg, r0, rc, 3) + bg_ref[...]
                val = _conv_causal(pad_v, cv, r0, rc, 3) + bv_ref[...]
                act, d_act = _gelu_and_grad(gate)
                d_f = df_ref[rows, :].astype(F32)
                back_g[rows, :] = d_f * val * d_act
                back_v[rows, :] = d_f * act
            for which, (back, pad, cw) in enumerate(((back_g, pad_g, cg), (back_v, pad_v, cv))):
                taps = [jnp.zeros((1, CB), F32)] * 3
                bias = jnp.zeros((1, CB), F32)
                for r0 in range(0, t, rc):
                    rows = pl.ds(r0, rc)
                    d_up[which, rows, :] = _conv_anticausal(back, cw, r0, rc, 3).astype(BF16)
                    g = back[rows, :]
                    taps = [acc + new for acc, new in zip(taps, _conv_wgrad(g, pad, r0, rc, 3))]
                    bias = bias + jnp.sum(g, axis=0, keepdims=True)
                small[which] = jnp.concatenate(taps + [bias] + [jnp.zeros((SUBLANES - 4, CB), F32)], axis=0)

        for r0 in range(0, t, rc):
            rows = pl.ds(r0, rc)
            dh_ref[rows, :] += _dot_nt(d_up[k, rows, :], w_ref[...])
        dw_ref[...] = _dot_tn(h_ref[...], d_up[k]).astype(BF16)
        dcw_ref[...] = small[k, pl.ds(0, 3), :]
        dcb_ref[...] = small[k, pl.ds(3, 1), :]

    blk = pl.BlockSpec((t, CB), lambda j, k: (0, j))
    both = lambda rows: pl.BlockSpec((rows, CB), lambda j, k: (0, nb * k + j))
    gate = lambda rows: pl.BlockSpec((rows, CB), lambda j, k: (0, j))
    val = lambda rows: pl.BlockSpec((rows, CB), lambda j, k: (0, nb + j))
    return pl.pallas_call(
        body, name="ffn_up_bwd", grid=(nb, 2),
        out_shape=[jax.ShapeDtypeStruct((D_MODEL, 2 * D_FF), BF16), jax.ShapeDtypeStruct((3, 2 * D_FF), F32),
                   jax.ShapeDtypeStruct((1, 2 * D_FF), F32), jax.ShapeDtypeStruct((t, D_MODEL), F32)],
        in_specs=[blk, blk, blk, gate(3), val(3), gate(1), val(1),
                  pl.BlockSpec((t, D_MODEL), lambda j, k: (0, 0)), both(D_MODEL)],
        out_specs=[both(D_MODEL), both(3), both(1), pl.BlockSpec((t, D_MODEL), lambda j, k: (0, 0))],
        scratch_shapes=[pltpu.VMEM((2, t, CB), BF16), pltpu.VMEM((2, SUBLANES, CB), F32)]
        + [pltpu.VMEM((t + PAD, CB), F32)] * 4,
        compiler_params=_params("arbitrary", "arbitrary"),
    )(up_g, up_v, d_f, w_conv, w_conv, b_conv, b_conv, h2, w_up)


def _merge_bwd(dy, d_h2, x1, mix, g3, g2, w_out, w_cb, w_lb, pa, pb, proj):
    t = dy.shape[0]
    tm = min(256, t)

    def body(dy_ref, dh2_ref, x1_ref, mix_ref, g3_ref, g2_ref, wo_ref, wcb_ref, wlb_ref, pa_ref, pb_ref, gc_ref, gl_ref,
             dx1_ref, dmix_ref, dpa_ref, dpb_ref, dya_ref, dyb_ref, dgate_ref, dg3_ref, dg2_ref):
        @pl.when(pl.program_id(0) == 0)
        def _():
            dg3_ref[...] = jnp.zeros_like(dg3_ref)
            dg2_ref[...] = jnp.zeros_like(dg2_ref)
        n3, r3 = _rms_fwd(x1_ref[...])
        d_h2 = dh2_ref[...]
        dg3_ref[...] += jnp.sum(d_h2 * n3, axis=0, keepdims=True)
        dx1 = dy_ref[...] + _rms_bwd(n3, r3, d_h2 * g3_ref[...])
        dx1_ref[...] = dx1
        n2, r2 = _rms_fwd(mix_ref[...])
        dg2_ref[...] += jnp.sum(dx1 * n2, axis=0, keepdims=True)
        d_mix = _rms_bwd(n2, r2, dx1 * g2_ref[...]).astype(BF16)
        dmix_ref[...] = d_mix
        d_merged = _dot_nt(d_mix, wo_ref[...])
        sc = jax.nn.sigmoid(gc_ref[...].astype(F32))
        sl = jax.nn.sigmoid(gl_ref[...].astype(F32))
        d_pa = (d_merged * sc).astype(BF16)
        d_pb = (d_merged * sl).astype(BF16)
        dpa_ref[...] = d_pa
        dpb_ref[...] = d_pb
        dgate_ref[0] = (d_merged * pa_ref[...].astype(F32) * sc * (1.0 - sc)).astype(BF16)
        dgate_ref[1] = (d_merged * pb_ref[...].astype(F32) * sl * (1.0 - sl)).astype(BF16)
        dya_ref[...] = _dot_nt(d_pa, wcb_ref[...]).astype(BF16)
        dyb_ref[...] = _dot_nt(d_pb, wlb_ref[...]).astype(BF16)

    row = pl.BlockSpec((tm, D_MODEL), lambda i: (i, 0))
    full = pl.BlockSpec((D_MODEL, D_MODEL), lambda i: (0, 0))
    vec = pl.BlockSpec((1, D_MODEL), lambda i: (0, 0))
    act = jax.ShapeDtypeStruct((t, D_MODEL), BF16)
    small = jax.ShapeDtypeStruct((1, D_MODEL), F32)
    return pl.pallas_call(
        body, name="merge_bwd", grid=(t // tm,),
        out_shape=[jax.ShapeDtypeStruct((t, D_MODEL), F32), act, act, act, act, act,
                   jax.ShapeDtypeStruct((2, t, D_MODEL), BF16), small, small],
        in_specs=[row, row, row, row, vec, vec, full, full, full, row, row,
                  pl.BlockSpec((tm, D_MODEL), lambda i: (i, 5)), pl.BlockSpec((tm, D_MODEL), lambda i: (i, 6))],
        out_specs=[row] * 6 + [pl.BlockSpec((2, tm, D_MODEL), lambda i: (0, i, 0)), vec, vec],
        compiler_params=_params("arbitrary"),
    )(dy, d_h2, x1, mix, g3, g2, w_out, w_cb, w_lb, pa, pb, proj, proj)


def _conv_mixer_bwd(proj, d_ya, w_short):
    t = proj.shape[0]
    rc = _row_chunk(t)

    def body(b_ref, c_ref, x_ref, dy_ref, w_ref, d_ref, dw_ref, pad, back):
        pad[pl.ds(0, PAD), :] = jnp.zeros((PAD, CB), F32)
        back[pl.ds(t, PAD), :] = jnp.zeros((PAD, CB), F32)
        for r0 in range(0, t, rc):
            rows = pl.ds(r0, rc)
            pad[pl.ds(PAD + r0, rc), :] = c_ref[rows, :].astype(F32) * x_ref[rows, :].astype(F32)
        w = w_ref[...]
        for r0 in range(0, t, rc):
            rows = pl.ds(r0, rc)
            d_y = dy_ref[rows, :].astype(F32)
            d_ref[0, rows, :] = (d_y * _conv_causal(pad, w, r0, rc, 3)).astype(BF16)
            back[rows, :] = d_y * b_ref[rows, :].astype(F32)
        taps = [jnp.zeros((1, CB), F32)] * 3
        for r0 in range(0, t, rc):
            rows = pl.ds(r0, rc)
            d_u = _conv_anticausal(back, w, r0, rc, 3)
            d_ref[1, rows, :] = (d_u * x_ref[rows, :].astype(F32)).astype(BF16)
            d_ref[2, rows, :] = (d_u * c_ref[rows, :].astype(F32)).astype(BF16)
            taps = [acc + new for acc, new in zip(taps, _conv_wgrad(back[rows, :], pad, r0, rc, 3))]
        dw_ref[...] = jnp.concatenate(taps, axis=0)

    blk = pl.BlockSpec((t, CB), lambda h: (0, h))
    return pl.pallas_call(
        body, name="conv_mixer_bwd", grid=(D_MODEL // CB,),
        out_shape=[jax.ShapeDtypeStruct((3, t, D_MODEL), BF16), jax.ShapeDtypeStruct((3, D_MODEL), F32)],
        in_specs=[_section(0, t), _section(1, t), _section(2, t), blk, pl.BlockSpec((3, CB), lambda h: (0, h))],
        out_specs=[pl.BlockSpec((3, t, CB), lambda h: (0, 0, h)), pl.BlockSpec((3, CB), lambda h: (0, h))],
        scratch_shapes=[pltpu.VMEM((t + PAD, CB), F32), pltpu.VMEM((t + PAD, CB), F32)],
        compiler_params=_params("parallel"),
    )(proj, proj, proj, d_ya, w_short)


LRU_SMALL_ROWS = 8


def _lru_bwd(proj, hl, d_yb, w_conv, b_conv, wa, ba, wx, bx, lam):
    t = proj.shape[0]
    rc = _row_chunk(t)
    vec, mat = _head_specs()

    def body(lx_ref, ly_ref, hl_ref, dy_ref, wc_ref, bc_ref, wa_ref, ba_ref, wx_ref, bx_ref, lam_ref,
             d_ref, dwa_ref, dwx_ref, small_ref, pad, a_next, dh_s, h_prev, back, acc_a, acc_x):
        zeros = jnp.zeros((PAD, CB), F32)
        pad[pl.ds(0, PAD), :] = zeros
        h_prev[pl.ds(0, PAD), :] = zeros
        a_next[pl.ds(t, PAD), :] = zeros
        back[pl.ds(t, PAD), :] = zeros
        for r0 in range(0, t, rc):
            pad[pl.ds(PAD + r0, rc), :] = lx_ref[pl.ds(r0, rc), :].astype(F32)
            h_prev[pl.ds(PAD + r0, rc), :] = hl_ref[pl.ds(r0, rc), :]
        wc, bc = wc_ref[...], bc_ref[...]
        wa_m, wx_m = wa_ref[...].reshape(HEAD_DIM, HEAD_DIM), wx_ref[...].reshape(HEAD_DIM, HEAD_DIM)
        ls = _log_sigmoid(lam_ref[...])

        def gates(r0):
            xl = _conv_causal(pad, wc, r0, rc, 4) + bc
            first = (lax.broadcasted_iota(jnp.int32, (rc, CB), 0) + r0) == 0
            return (xl, first) + _lru_gates(xl, wa_m, ba_ref[...], wx_m, bx_ref[...], ls, first)

        for r0 in range(0, t, rc):
            rows = pl.ds(r0, rc)
            a = gates(r0)[5]
            a_next[pl.ds(PAD - 1 + r0, rc), :] = a
            act, d_act = _gelu_and_grad(ly_ref[rows, :].astype(F32))
            d_y = dy_ref[rows, :].astype(F32)
            dh_s[rows, :] = d_y * act
            d_ref[1, rows, :] = (d_y * hl_ref[rows, :] * d_act).astype(BF16)

        row = lax.broadcasted_iota(jnp.int32, (SUBLANES, CB), 0)
        groups = t // SUBLANES

        def group(i, carry):
            r = pl.multiple_of((groups - 1 - i) * SUBLANES, SUBLANES)
            a_g, b_g = a_next[pl.ds(PAD + r, SUBLANES), :], dh_s[pl.ds(r, SUBLANES), :]
            for s in (1, 2, 4):
                keep = row < SUBLANES - s
                b_g = jnp.where(keep, a_g * pltpu.roll(b_g, SUBLANES - s, 0) + b_g, b_g)
                a_g = jnp.where(keep, a_g * pltpu.roll(a_g, SUBLANES - s, 0), a_g)
            d_g = b_g + a_g * carry
            dh_s[pl.ds(r, SUBLANES), :] = d_g
            return jnp.broadcast_to(d_g[0:1, :], (SUBLANES, CB))

        lax.fori_loop(0, groups, group, jnp.zeros((SUBLANES, CB), F32))

        acc_a[...] = jnp.zeros_like(acc_a)
        acc_x[...] = jnp.zeros_like(acc_x)
        d_ba = d_bx = d_ls = jnp.zeros((1, CB), F32)
        for r0 in range(0, t, rc):
            rows = pl.ds(r0, rc)
            xl, first, xb, ra, ia, a, one_minus, mult = gates(r0)
            d_h = dh_s[rows, :]
            d_a = d_h * h_prev[pl.ds(PAD - 1 + r0, rc), :]
            d_mult = d_h * ia * xl
            d_ia = d_h * mult * xl
            d_xl = d_h * mult * ia
            d_mult_d_la = jnp.where(first, 0.0, (one_minus - 1.0) / mult)
            d_la = d_a * a + d_mult * d_mult_d_la
            d_ls = d_ls + jnp.sum(d_la * ra, axis=0, keepdims=True) * LRU_C
            d_za = d_la * (LRU_C * ls) * ra * (1.0 - ra)
            d_zx = d_ia * ia * (1.0 - ia)
            d_ba = d_ba + jnp.sum(d_za, axis=0, keepdims=True)
            d_bx = d_bx + jnp.sum(d_zx, axis=0, keepdims=True)
            d_za, d_zx = d_za.astype(BF16), d_zx.astype(BF16)
            acc_a[...] += _dot_tn(xb, d_za)
            acc_x[...] += _dot_tn(xb, d_zx)
            back[rows, :] = d_xl + _dot_nt(d_za, wa_m) + _dot_nt(d_zx, wx_m)
        taps = [jnp.zeros((1, CB), F32)] * 4
        d_bc = jnp.zeros((1, CB), F32)
        for r0 in range(0, t, rc):
            rows = pl.ds(r0, rc)
            d_ref[0, rows, :] = _conv_anticausal(back, wc, r0, rc, 4).astype(BF16)
            g = back[rows, :]
            taps = [acc + new for acc, new in zip(taps, _conv_wgrad(g, pad, r0, rc, 4))]
            d_bc = d_bc + jnp.sum(g, axis=0, keepdims=True)
        d_lam = d_ls * jax.nn.sigmoid(-lam_ref[...])
        small_ref[...] = jnp.concatenate(taps + [d_bc, d_ba, d_bx, d_lam], axis=0)
        dwa_ref[...] = acc_a[...].reshape(N_DEV, HEAD_DIM // N_DEV, HEAD_DIM).astype(BF16)
        dwx_ref[...] = acc_x[...].reshape(N_DEV, HEAD_DIM // N_DEV, HEAD_DIM).astype(BF16)

    blk = pl.BlockSpec((t, CB), lambda h: (0, h))
    gate_grad = jax.ShapeDtypeStruct((N_DEV, N_HEADS, HEAD_DIM // N_DEV, HEAD_DIM), BF16)
    return pl.pallas_call(
        body, name="lru_bwd", grid=(N_HEADS,),
        out_shape=[jax.ShapeDtypeStruct((2, t, D_MODEL), BF16), gate_grad, gate_grad,
                   jax.ShapeDtypeStruct((LRU_SMALL_ROWS, D_MODEL), F32)],
        in_specs=[_section(3, t), _section(4, t), blk, blk, pl.BlockSpec((4, CB), lambda h: (0, h)),
                  vec, mat, vec, mat, vec, vec],
        out_specs=[pl.BlockSpec((2, t, CB), lambda h: (0, 0, h)), mat, mat,
                   pl.BlockSpec((LRU_SMALL_ROWS, CB), lambda h: (0, h))],
        scratch_shapes=[pltpu.VMEM((t + PAD, CB), F32), pltpu.VMEM((t + PAD, CB), F32), pltpu.VMEM((t, CB), F32),
                        pltpu.VMEM((t + PAD, CB), F32), pltpu.VMEM((t + PAD, CB), F32),
                        pltpu.VMEM((HEAD_DIM, HEAD_DIM), F32), pltpu.VMEM((HEAD_DIM, HEAD_DIM), F32)],
        compiler_params=_params("parallel"),
    )(proj, proj, hl, d_yb, w_conv, b_conv, wa, ba, wx, bx, lam)


def _stack_maps(halves):
    def conv(sec, part):
        return jnp.minimum(sec, 2), jnp.where(sec < 3, part, halves - 1)

    def lru(sec, part):
        return jnp.clip(sec - 3, 0, 1), jnp.where(sec < 3, 0, jnp.where(sec < 5, part, halves - 1))

    def gate(sec, part):
        return jnp.clip(sec - 5, 0, 1), jnp.where(sec < 5, 0, part)

    return conv, lru, gate


def _pick_stack(sec, refs, fn):
    @pl.when(sec < 3)
    def _():
        fn(refs[0])

    @pl.when((sec >= 3) & (sec < 5))
    def _():
        fn(refs[1])

    @pl.when(sec >= 5)
    def _():
        fn(refs[2])


def _in_proj_wgrad(h, d_conv, d_lru, d_gate):
    t = h.shape[0]
    halves, bn = 2, D_MODEL // 2
    maps = _stack_maps(halves)

    def body(h_ref, dc_ref, dl_ref, dg_ref, o_ref):
        def emit(ref):
            o_ref[...] = _dot_tn(h_ref[...], ref[...]).astype(BF16)
        _pick_stack(pl.program_id(0) // halves, (dc_ref, dl_ref, dg_ref), emit)

    def spec(m):
        def index(s):
            stack, part = m(s // halves, s % halves)
            return stack, 0, part
        return pl.BlockSpec((None, t, bn), index)

    return pl.pallas_call(
        body, name="in_proj_wgrad", grid=(7 * halves,), out_shape=jax.ShapeDtypeStruct((D_MODEL, IN_COLS), BF16),
        in_specs=[pl.BlockSpec((t, D_MODEL), lambda s: (0, 0))] + [spec(m) for m in maps],
        out_specs=pl.BlockSpec((D_MODEL, bn), lambda s: (0, s)),
        compiler_params=_params("arbitrary"),
    )(h, d_conv, d_lru, d_gate)


def _in_proj_xgrad(d_conv, d_lru, d_gate, w_in, x, dx1, g1):
    t = x.shape[0]
    tm = min(512, t)
    maps = _stack_maps(1)

    def body(dc_ref, dl_ref, dg_ref, w_ref, x_ref, dx1_ref, g_ref, dx_ref, dgain_ref, acc):
        i, s = pl.program_id(0), pl.program_id(1)

        @pl.when((i == 0) & (s == 0))
        def _():
            dgain_ref[...] = jnp.zeros_like(dgain_ref)

        @pl.when(s == 0)
        def _():
            acc[...] = jnp.zeros_like(acc)

        def add(ref):
            acc[...] += _dot_nt(ref[...], w_ref[...])
        _pick_stack(s, (dc_ref, dl_ref, dg_ref), add)

        @pl.when(s == 6)
        def _():
            n1, r1 = _rms_fwd(x_ref[...])
            d_h = acc[...]
            dgain_ref[...] += jnp.sum(d_h * n1, axis=0, keepdims=True)
            dx_ref[...] = dx1_ref[...] + _rms_bwd(n1, r1, d_h * g_ref[...])

    def spec(m):
        def index(i, s):
            return m(s, 0)[0], i, 0
        return pl.BlockSpec((None, tm, D_MODEL), index)

    row = pl.BlockSpec((tm, D_MODEL), lambda i, s: (i, 0))
    vec = pl.BlockSpec((1, D_MODEL), lambda i, s: (0, 0))
    return pl.pallas_call(
        body, name="in_proj_xgrad", grid=(t // tm, 7),
        out_shape=[jax.ShapeDtypeStruct((t, D_MODEL), F32), jax.ShapeDtypeStruct((1, D_MODEL), F32)],
        in_specs=[spec(m) for m in maps] + [pl.BlockSpec((D_MODEL, D_MODEL), lambda i, s: (0, s)), row, row, vec],
        out_specs=[row, vec],
        scratch_shapes=[pltpu.VMEM((tm, D_MODEL), F32)],
        compiler_params=_params("arbitrary", "arbitrary"),
    )(d_conv, d_lru, d_gate, w_in, x, dx1, g1)


def _add_pair(kept, got, name):
    shape = kept.shape
    cols = shape[-1]
    a, b = kept.reshape(-1, cols), got.reshape(-1, cols)
    rows = a.shape[0]
    rb = _row_block(rows, 512)

    def body(a_ref, b_ref, o_ref):
        o_ref[...] = (a_ref[...].astype(F32) + b_ref[...].astype(F32)).astype(BF16)

    blk = pl.BlockSpec((rb, cols), lambda i: (i, 0))
    out = pl.pallas_call(
        body, name=name, grid=(rows // rb,), out_shape=jax.ShapeDtypeStruct((rows, cols), BF16),
        in_specs=[blk, blk], out_specs=blk, compiler_params=_params("parallel"),
    )(a, b)
    return out.reshape(shape)


def _adamw(w, g, m, v):
    m = ADAM_B1 * m + (1.0 - ADAM_B1) * g
    v = ADAM_B2 * v + (1.0 - ADAM_B2) * (g * g)
    m_hat = m / (1.0 - ADAM_B1 ** ADAM_STEP)
    v_hat = v / (1.0 - ADAM_B2 ** ADAM_STEP)
    return -ADAM_LR * (m_hat / (jnp.sqrt(v_hat) + ADAM_EPS) + ADAM_WD * w), m, v


def _adam_large(w, m, v, own, others, name):
    shape = w.shape
    cols = shape[-1]
    w2, m2, v2 = (a.reshape(-1, cols) for a in (w, m, v))
    rows = w2.shape[0]
    own, others = own.reshape(4, rows, cols), others.reshape(3, rows, cols)
    rb = _row_block(rows, 256)

    def body(w_ref, m_ref, v_ref, own_ref, oth_ref, g_ref, d_ref, nm_ref, nv_ref):
        g = own_ref[...].astype(F32)
        for k in range(3):
            g = g + oth_ref[k].astype(F32)
        g_ref[...] = g
        d_ref[...], nm_ref[...], nv_ref[...] = _adamw(w_ref[...], g, m_ref[...], v_ref[...])

    blk = pl.BlockSpec((rb, cols), lambda i: (i, 0))
    res = jax.ShapeDtypeStruct((rows, cols), F32)
    outs = pl.pallas_call(
        body, name=name, grid=(rows // rb,), out_shape=[res] * 4,
        in_specs=[blk, blk, blk, pl.BlockSpec((None, rb, cols), lambda i: (0, i, 0)),
                  pl.BlockSpec((3, rb, cols), lambda i: (0, i, 0))],
        out_specs=[blk] * 4, compiler_params=_params("parallel"),
    )(w2, m2, v2, own, others)
    return [o.reshape(shape) for o in outs]


def _adam_small(ws, gs, ms, vs):
    n = len(ws)

    def body(*refs):
        w_refs, g_refs, m_refs, v_refs = (refs[i * n:(i + 1) * n] for i in range(4))
        outs = refs[4 * n:]
        for i in range(n):
            d, m, v = _adamw(w_refs[i][...], g_refs[i][...], m_refs[i][...], v_refs[i][...])
            outs[i][...], outs[n + i][...], outs[2 * n + i][...] = d, m, v

    shapes = [jax.ShapeDtypeStruct(w.shape, F32) for w in ws]
    outs = pl.pallas_call(
        body, name="adam_small", out_shape=shapes * 3,
        in_specs=[VMEM_SPEC] * (4 * n), out_specs=[VMEM_SPEC] * (3 * n), compiler_params=_params(),
    )(*ws, *gs, *ms, *vs)
    return outs[:n], outs[n:2 * n], outs[2 * n:]


def _pack_rows(pieces):
    rows = []
    for p in pieces:
        flat = p.reshape(-1)
        pad = (-flat.shape[0]) % LANES
        if pad:
            flat = jnp.concatenate([flat, jnp.zeros((pad,), F32)])
        rows.append(flat.reshape(-1, LANES))
    out = jnp.concatenate(rows, axis=0)
    pad = (-out.shape[0]) % SUBLANES
    if pad:
        out = jnp.concatenate([out, jnp.zeros((pad, LANES), F32)], axis=0)
    return out


def kernel(x, norm_mix_pre, norm_mix_post, norm_ffn_pre, norm_ffn_post, w_in, conv_short_w, w_conv_branch, lru_conv_w, lru_conv_b, lru_wa, lru_ba, lru_wx, lru_bx, lru_lambda, w_lru_branch, w_out, ffn_w_up, ffn_conv_w, ffn_conv_b, ffn_w_down, loss_target, m_norm_mix_pre, m_norm_mix_post, m_norm_ffn_pre, m_norm_ffn_post, m_w_in, m_conv_short_w, m_w_conv_branch, m_lru_conv_w, m_lru_conv_b, m_lru_wa, m_lru_ba, m_lru_wx, m_lru_bx, m_lru_lambda, m_w_lru_branch, m_w_out, m_ffn_w_up, m_ffn_conv_w, m_ffn_conv_b, m_ffn_w_down, v_norm_mix_pre, v_norm_mix_post, v_norm_ffn_pre, v_norm_ffn_post, v_w_in, v_conv_short_w, v_w_conv_branch, v_lru_conv_w, v_lru_conv_b, v_lru_wa, v_lru_ba, v_lru_wx, v_lru_bx, v_lru_lambda, v_w_lru_branch, v_w_out, v_ffn_w_up, v_ffn_conv_w, v_ffn_conv_b, v_ffn_w_down):
    t = x.shape[1]
    xi, yi, ci = _position()
    me = _block_of(xi, yi, ci)
    x2, target = x[0], loss_target[0]
    shard_in, shard_up = IN_COLS // N_DEV, 2 * D_FF // N_DEV
    shard_sq, shard_down, shard_head = D_MODEL // N_DEV, D_FF // N_DEV, HEAD_DIM // N_DEV

    large = [w_in[0], w_conv_branch[0], w_lru_branch[0], w_out[0], lru_wa[0], lru_wx[0], ffn_w_up[0], ffn_w_down[0]]
    blocks = [_cols(shard_in), _rows(shard_sq), _rows(shard_sq), _rows(shard_sq), _lead, _lead,
              _cols(shard_up), _rows(shard_down)]
    gate_full = (N_DEV, N_HEADS, shard_head, HEAD_DIM)
    full_shapes = [(D_MODEL, IN_COLS), (D_MODEL, D_MODEL), (D_MODEL, D_MODEL), (D_MODEL, D_MODEL), gate_full, gate_full,
                   (D_MODEL, 2 * D_FF), (D_FF, D_MODEL)]
    small_mine = _pack_rows([conv_short_w, lru_conv_w, lru_ba, lru_bx, ffn_conv_w])
    *gathered, small_all = _gather_weights(large, blocks, full_shapes, small_mine)
    g_in, g_cb, g_lb, g_out, g_wa, g_wx, g_up, g_down = gathered

    def cols_of(r0, n, width):
        part = small_all[:, r0:r0 + n * width // LANES, :].reshape(N_DEV, n, width)
        return part.transpose(1, 0, 2).reshape(n, N_DEV * width)

    c_short = cols_of(0, 3, LANES)
    c_lru = cols_of(3, 4, LANES)
    b_a = cols_of(7, N_HEADS, shard_head).reshape(1, D_MODEL)
    b_x = cols_of(8, N_HEADS, shard_head).reshape(1, D_MODEL)
    c_ffn = cols_of(9, 3, shard_up)

    proj, h = _in_proj(x2, norm_mix_pre, g_in)
    y_a = _conv_mixer_fwd(proj, c_short)
    y_b, hl = _lru_fwd(proj, c_lru, lru_conv_b, g_wa, b_a, g_wx, b_x, lru_lambda)
    pa, pb, merged, mix, x1, h2 = _merge(y_a, y_b, proj, x2, g_cb, g_lb, g_out, norm_mix_post, norm_ffn_pre)
    up_g, up_v, f = _ffn_up(h2, g_up, c_ffn, ffn_conv_b)
    dy, d_out, d_f, dg4, loss_part = _ffn_down(f, g_down, x1, target, norm_ffn_post)

    gw_down = _grad_tn(f, d_out, min(512, D_FF), "ffn_down_wgrad")
    gw_up, gc_ffn, gb_ffn, d_h2 = _ffn_up_bwd(up_g, up_v, d_f, c_ffn, ffn_conv_b, h2, g_up)
    dx1, d_mix, d_pa, d_pb, d_ya, d_yb, d_gate, dg3, dg2 = _merge_bwd(
        dy, d_h2, x1, mix, norm_ffn_pre, norm_mix_post, g_out, g_cb, g_lb, pa, pb, proj)
    gw_out = _grad_tn(merged, d_mix, CB, "w_out_wgrad")
    gw_cb = _grad_tn(y_a, d_pa, CB, "w_conv_branch_wgrad")
    gw_lb = _grad_tn(y_b, d_pb, CB, "w_lru_branch_wgrad")
    d_conv, gc_short = _conv_mixer_bwd(proj, d_ya, c_short)
    d_lru, gw_a, gw_x, g_lru_small = _lru_bwd(proj, hl, d_yb, c_lru, lru_conv_b, g_wa, b_a, g_wx, b_x, lru_lambda)
    gw_in = _in_proj_wgrad(h, d_conv, d_lru, d_gate)
    dx, dg1 = _in_proj_xgrad(d_conv, d_lru, d_gate, g_in, x2, dx1, norm_mix_pre)

    grads = [gw_in, gw_cb, gw_lb, gw_out, gw_a, gw_x, gw_up, gw_down]
    shard_shapes = [(D_MODEL, shard_in), (shard_sq, D_MODEL), (shard_sq, D_MODEL), (shard_sq, D_MODEL),
                    (N_HEADS, shard_head, HEAD_DIM), (N_HEADS, shard_head, HEAD_DIM), (D_MODEL, shard_up),
                    (shard_down, D_MODEL)]
    names = ["w_in", "w_conv_branch", "w_lru_branch", "w_out", "lru_wa", "lru_wx", "ffn_w_up", "ffn_w_down"]
    n = len(grads)
    pair = _exchange_pair(grads, blocks, shard_shapes)
    sums = [_add_pair(pair[a], pair[n + a], "pair_sum_" + names[a]) for a in range(n)]
    others = _exchange_chips(sums)
    moments = {"w_in": (m_w_in, v_w_in), "w_conv_branch": (m_w_conv_branch, v_w_conv_branch),
               "w_lru_branch": (m_w_lru_branch, v_w_lru_branch), "w_out": (m_w_out, v_w_out),
               "lru_wa": (m_lru_wa, v_lru_wa), "lru_wx": (m_lru_wx, v_lru_wx), "ffn_w_up": (m_ffn_w_up, v_ffn_w_up),
               "ffn_w_down": (m_ffn_w_down, v_ffn_w_down)}
    weights = {"w_in": w_in, "w_conv_branch": w_conv_branch, "w_lru_branch": w_lru_branch, "w_out": w_out,
               "lru_wa": lru_wa, "lru_wx": lru_wx, "ffn_w_up": ffn_w_up, "ffn_w_down": ffn_w_down}
    out_g, out_d, out_m, out_v = {}, {}, {}, {}
    for a, name in enumerate(names):
        out_g[name], out_d[name], out_m[name], out_v[name] = _adam_large(
            weights[name], *moments[name], sums[a], others[a], "adam_" + name)

    pieces = [dg1, dg2, dg3, dg4, g_lru_small[4:5], g_lru_small[7:8], gb_ffn, gc_short, g_lru_small[0:4],
              g_lru_small[5:6], g_lru_small[6:7], gc_ffn, loss_part]
    total = _allreduce_small(_pack_rows(pieces))
    sizes = [p.size for p in pieces]
    starts = [0]
    for s in sizes:
        starts.append(starts[-1] + (s + LANES - 1) // LANES)

    def piece(i, shape):
        return total[starts[i]:starts[i + 1]].reshape(-1)[:sizes[i]].reshape(shape)

    loss = total[starts[12], 0]

    def col_shard(full, width):
        return lax.dynamic_slice_in_dim(full, me * width, width, axis=1)

    def head_shard(full):
        return lax.dynamic_slice_in_dim(full.reshape(N_HEADS, HEAD_DIM), me * shard_head, shard_head, axis=1)

    small_names = ["norm_mix_pre", "norm_mix_post", "norm_ffn_pre", "norm_ffn_post", "lru_conv_b", "lru_lambda",
                   "ffn_conv_b", "conv_short_w", "lru_conv_w", "lru_ba", "lru_bx", "ffn_conv_w"]
    small_g = [piece(0, (1, D_MODEL)), piece(1, (1, D_MODEL)), piece(2, (1, D_MODEL)), piece(3, (1, D_MODEL)),
               piece(4, (1, D_MODEL)), piece(5, (1, D_MODEL)), piece(6, (1, 2 * D_FF)),
               col_shard(piece(7, (3, D_MODEL)), LANES), col_shard(piece(8, (4, D_MODEL)), LANES),
               head_shard(piece(9, (1, D_MODEL))), head_shard(piece(10, (1, D_MODEL))),
               col_shard(piece(11, (3, 2 * D_FF)), shard_up)]
    small_w = [norm_mix_pre, norm_mix_post, norm_ffn_pre, norm_ffn_post, lru_conv_b, lru_lambda, ffn_conv_b,
               conv_short_w[0], lru_conv_w[0], lru_ba[0], lru_bx[0], ffn_conv_w[0]]
    small_m = [m_norm_mix_pre, m_norm_mix_post, m_norm_ffn_pre, m_norm_ffn_post, m_lru_conv_b, m_lru_lambda,
               m_ffn_conv_b, m_conv_short_w[0], m_lru_conv_w[0], m_lru_ba[0], m_lru_bx[0], m_ffn_conv_w[0]]
    small_v = [v_norm_mix_pre, v_norm_mix_post, v_norm_ffn_pre, v_norm_ffn_post, v_lru_conv_b, v_lru_lambda,
               v_ffn_conv_b, v_conv_short_w[0], v_lru_conv_w[0], v_lru_ba[0], v_lru_bx[0], v_ffn_conv_w[0]]
    s_d, s_m, s_v = _adam_small(small_w, small_g, small_m, small_v)
    for i, name in enumerate(small_names):
        shape = small_w[i].shape if i < 7 else (1,) + small_w[i].shape
        out_g[name] = small_g[i].reshape(shape)
        out_d[name], out_m[name], out_v[name] = s_d[i].reshape(shape), s_m[i].reshape(shape), s_v[i].reshape(shape)

    order = ["norm_mix_pre", "norm_mix_post", "norm_ffn_pre", "norm_ffn_post", "w_in", "conv_short_w", "w_conv_branch",
             "lru_conv_w", "lru_conv_b", "lru_wa", "lru_ba", "lru_wx", "lru_bx", "lru_lambda", "w_lru_branch", "w_out",
             "ffn_w_up", "ffn_conv_w", "ffn_conv_b", "ffn_w_down"]
    return (loss, dx.reshape(1, t, D_MODEL), *[out_g[k] for k in order], *[out_d[k] for k in order],
            *[out_m[k] for k in order], *[out_v[k] for k in order])
```

```python
import functools
import math

import jax
import jax.numpy as jnp
from jax import lax
from jax.experimental import pallas as pl
from jax.experimental.pallas import tpu as pltpu

F32 = jnp.float32
BF16 = jnp.bfloat16
MESH = pl.DeviceIdType.MESH

N_DEV = 8
D_MODEL = 1024
N_HEADS = 4
HEAD_DIM = D_MODEL // N_HEADS
D_FF = 3 * D_MODEL
IN_COLS = 7 * D_MODEL
LRU_C = 8.0
RMS_EPS = 1e-6
ADAM_LR = 0.001
ADAM_B1 = 0.9
ADAM_B2 = 0.999
ADAM_EPS = 1e-08
ADAM_WD = 0.01
ADAM_STEP = 10
GELU_K = math.sqrt(2.0 / math.pi)
GELU_C = 0.044715

LANES = 128
SUBLANES = 8
PAD = SUBLANES
VMEM_LIMIT = 56 * 1024 * 1024
CB = 256

HBM_SPEC = pl.BlockSpec(memory_space=pltpu.HBM)
VMEM_SPEC = pl.BlockSpec(memory_space=pltpu.VMEM)


def _params(*sem):
    if sem:
        return pltpu.CompilerParams(dimension_semantics=sem, vmem_limit_bytes=VMEM_LIMIT)
    return pltpu.CompilerParams(vmem_limit_bytes=VMEM_LIMIT)


def _row_chunk(t):
    return min(256, t)


def _row_block(rows, cap):
    return next(rb for rb in range(min(cap, rows), 0, -16) if rows % rb == 0)


def _gelu(x):
    return 0.5 * x * (1.0 + jnp.tanh(GELU_K * (x + GELU_C * x * x * x)))


def _gelu_and_grad(x):
    t = jnp.tanh(GELU_K * (x + GELU_C * x * x * x))
    g = 0.5 * x * (1.0 + t)
    dg = 0.5 * (1.0 + t) + 0.5 * x * (1.0 - t * t) * GELU_K * (1.0 + 3.0 * GELU_C * x * x)
    return g, dg


def _expm1_neg(x):
    series = x * (1.0 + x * (0.5 + x * (1.0 / 6.0 + x * (1.0 / 24.0 + x * (1.0 / 120.0)))))
    return jnp.where(x > -0.05, series, jnp.exp(x) - 1.0)


def _log_sigmoid(x):
    return jnp.minimum(x, 0.0) - jnp.log1p(jnp.exp(-jnp.abs(x)))


def _dot(a, b):
    return jnp.dot(a, b, preferred_element_type=F32)


def _dot_nt(a, b):
    return lax.dot_general(a, b, (((1,), (1,)), ((), ())), preferred_element_type=F32)


def _dot_tn(a, b):
    return lax.dot_general(a, b, (((0,), (0,)), ((), ())), preferred_element_type=F32)


def _rms_fwd(x):
    r = lax.rsqrt(jnp.mean(x * x, axis=-1, keepdims=True) + RMS_EPS)
    return x * r, r


def _rms_bwd(n, r, gdy):
    return r * (gdy - n * jnp.mean(n * gdy, axis=-1, keepdims=True))


def _conv_causal(pad_ref, w, r0, rows, taps):
    acc = None
    for k in range(taps):
        term = w[k:k + 1, :] * pad_ref[pl.ds(PAD + r0 - (taps - 1 - k), rows), :]
        acc = term if acc is None else acc + term
    return acc


def _conv_anticausal(pad_ref, w, r0, rows, taps):
    acc = None
    for k in range(taps):
        term = w[k:k + 1, :] * pad_ref[pl.ds(r0 + (taps - 1 - k), rows), :]
        acc = term if acc is None else acc + term
    return acc


def _conv_wgrad(g, xpad_ref, r0, rows, taps):
    return [jnp.sum(g * xpad_ref[pl.ds(PAD + r0 - (taps - 1 - k), rows), :], axis=0, keepdims=True)
            for k in range(taps)]


def _position():
    return lax.axis_index("x"), lax.axis_index("y"), lax.axis_index("c")


def _block_of(x, y, c):
    return 4 * x + 2 * y + c


def _chip(x, y, k):
    return (x + (k & 1)) % 2, (y + (k >> 1)) % 2


def _cols(width):
    def at(ref, d):
        return ref.at[:, pl.ds(pl.multiple_of(d * width, LANES), width)]
    return at


def _rows(height):
    def at(ref, d):
        return ref.at[pl.ds(pl.multiple_of(d * height, 16), height), :]
    return at


def _lead(ref, d):
    return ref.at[d]


def _gather_weights(shards, blocks, full_shapes, small):
    n = len(shards)
    small_rows = small.shape[0]

    def body(*refs):
        ins, small_in = refs[:n], refs[n]
        outs, small_out = refs[n + 1:2 * n + 1], refs[2 * n + 1]
        stage = refs[2 * n + 2:3 * n + 2]
        send, recv, local = refs[3 * n + 2:]
        x, y, c = _position()
        me = _block_of(x, y, c)
        sibling = (x, y, 1 - c)

        for a in range(n):
            stage[a][...] = ins[a][...].astype(BF16)

        def copy(a, k, block, to, src=None):
            dst = blocks[a](outs[a], block)
            return pltpu.make_async_remote_copy(
                src_ref=dst if src is None else src, dst_ref=dst, send_sem=send.at[a, k], recv_sem=recv.at[a, k],
                device_id=to, device_id_type=MESH)

        def small_copy(k):
            px, py, pc = (x + (k & 1)) % 2, (y + ((k >> 1) & 1)) % 2, (c + (k >> 2)) % 2
            return pltpu.make_async_remote_copy(
                src_ref=small_in, dst_ref=small_out.at[me], send_sem=send.at[n, k - 1], recv_sem=recv.at[n, k - 1],
                device_id=(px, py, pc), device_id_type=MESH)

        def small_arrival(k):
            px, py, pc = (x + (k & 1)) % 2, (y + ((k >> 1) & 1)) % 2, (c + (k >> 2)) % 2
            return pltpu.make_async_remote_copy(
                src_ref=small_in, dst_ref=small_out.at[_block_of(px, py, pc)], send_sem=send.at[n, k - 1],
                recv_sem=recv.at[n, k - 1], device_id=(px, py, pc), device_id_type=MESH)

        small_out[me] = small_in[...]
        small_sends = [small_copy(k) for k in range(1, N_DEV)]
        for cp in small_sends:
            cp.start()

        mine, first, passed = [], [], []
        for a in range(n):
            own = pltpu.make_async_copy(stage[a], blocks[a](outs[a], me), local.at[a])
            own.start()
            mine.append(own)
            sends = [copy(a, 0, me, sibling, src=stage[a])]
            sends += [copy(a, k, me, (*_chip(x, y, k), c), src=stage[a]) for k in (1, 2, 3)]
            for cp in sends:
                cp.start()
            first += sends
        for a in range(n):
            for k in (1, 2, 3):
                landed = _block_of(*_chip(x, y, k), c)
                copy(a, k, landed, (x, y, c)).wait_recv()
                fwd = copy(a, 3 + k, landed, sibling)
                fwd.start()
                passed.append(fwd)
        for a in range(n):
            copy(a, 0, _block_of(x, y, 1 - c), (x, y, c)).wait_recv()
            for k in (1, 2, 3):
                copy(a, 3 + k, _block_of(*_chip(x, y, k), 1 - c), (x, y, c)).wait_recv()
        for k in range(1, N_DEV):
            small_arrival(k).wait_recv()
        for cp in first + passed + small_sends:
            cp.wait_send()
        for own in mine:
            own.wait()

    out_shape = [jax.ShapeDtypeStruct(s, BF16) for s in full_shapes]
    out_shape.append(jax.ShapeDtypeStruct((N_DEV, small_rows, LANES), F32))
    return pl.pallas_call(
        body, name="gather_weights", out_shape=out_shape,
        in_specs=[VMEM_SPEC] * (n + 1), out_specs=[HBM_SPEC] * n + [VMEM_SPEC],
        scratch_shapes=[pltpu.VMEM(s.shape, BF16) for s in shards]
        + [pltpu.SemaphoreType.DMA((n + 1, 7)), pltpu.SemaphoreType.DMA((n + 1, 7)), pltpu.SemaphoreType.DMA((n,))],
        compiler_params=_params(),
    )(*shards, small)


def _exchange_pair(grads, blocks, shard_shapes):
    n = len(grads)

    def body(*refs):
        ins, got = refs[:n], refs[n:2 * n]
        send, recv = refs[2 * n:]
        x, y, c = _position()
        copies = []
        for a in range(n):
            for k in range(4):
                cp = pltpu.make_async_remote_copy(
                    src_ref=blocks[a](ins[a], _block_of(*_chip(x, y, k), 1 - c)), dst_ref=got[a].at[k],
                    send_sem=send.at[a, k], recv_sem=recv.at[a, k], device_id=(x, y, 1 - c), device_id_type=MESH)
                cp.start()
                copies.append(cp)
        for cp in copies:
            cp.wait()

    return pl.pallas_call(
        body, name="reduce_pair_exchange", out_shape=[jax.ShapeDtypeStruct((4,) + tuple(s), BF16) for s in shard_shapes],
        in_specs=[HBM_SPEC] * n, out_specs=[HBM_SPEC] * n,
        scratch_shapes=[pltpu.SemaphoreType.DMA((n, 4)), pltpu.SemaphoreType.DMA((n, 4))],
        compiler_params=_params(),
    )(*grads)


def _exchange_chips(pair_sums):
    n = len(pair_sums)

    def body(*refs):
        ins, outs = refs[:n], refs[n:2 * n]
        send, recv = refs[2 * n:]
        x, y, c = _position()
        copies = []
        for a in range(n):
            for k in (1, 2, 3):
                cp = pltpu.make_async_remote_copy(
                    src_ref=ins[a].at[k], dst_ref=outs[a].at[k - 1], send_sem=send.at[a, k - 1],
                    recv_sem=recv.at[a, k - 1], device_id=(*_chip(x, y, k), c), device_id_type=MESH)
                cp.start()
                copies.append(cp)
        for cp in copies:
            cp.wait()

    return pl.pallas_call(
        body, name="reduce_chip_exchange",
        out_shape=[jax.ShapeDtypeStruct((3,) + tuple(p.shape[1:]), BF16) for p in pair_sums],
        in_specs=[HBM_SPEC] * n, out_specs=[HBM_SPEC] * n,
        scratch_shapes=[pltpu.SemaphoreType.DMA((n, 3)), pltpu.SemaphoreType.DMA((n, 3))],
        compiler_params=_params(),
    )(*pair_sums)


def _allreduce_small(part):
    rows = part.shape[0]

    def body(in_ref, out_ref, buf, send, recv):
        x, y, c = _position()
        me = _block_of(x, y, c)

        def peer(k):
            return (x + (k & 1)) % 2, (y + ((k >> 1) & 1)) % 2, (c + (k >> 2)) % 2

        sends = []
        for k in range(1, N_DEV):
            cp = pltpu.make_async_remote_copy(src_ref=in_ref, dst_ref=buf.at[me], send_sem=send.at[k - 1],
                                              recv_sem=recv.at[k - 1], device_id=peer(k), device_id_type=MESH)
            cp.start()
            sends.append(cp)
        buf[me] = in_ref[...]
        for k in range(1, N_DEV):
            pltpu.make_async_remote_copy(src_ref=in_ref, dst_ref=buf.at[_block_of(*peer(k))], send_sem=send.at[k - 1],
                                         recv_sem=recv.at[k - 1], device_id=peer(k), device_id_type=MESH).wait_recv()
        total = buf[0]
        for d in range(1, N_DEV):
            total = total + buf[d]
        out_ref[...] = total
        for cp in sends:
            cp.wait_send()

    return pl.pallas_call(
        body, name="allreduce_small", out_shape=jax.ShapeDtypeStruct(part.shape, F32),
        in_specs=[VMEM_SPEC], out_specs=VMEM_SPEC,
        scratch_shapes=[pltpu.VMEM((N_DEV, rows, LANES), F32), pltpu.SemaphoreType.DMA((7,)), pltpu.SemaphoreType.DMA((7,))],
        compiler_params=_params(),
    )(part)


def _in_proj(x, g1, w_in):
    t = x.shape[0]
    tm, bn = min(512, t), 1024

    def body(x_ref, g_ref, w_ref, proj_ref, h_ref, h_s):
        @pl.when(pl.program_id(1) == 0)
        def _():
            n, _ = _rms_fwd(x_ref[...])
            h_s[...] = (n * g_ref[...]).astype(BF16)
            h_ref[...] = h_s[...]
        proj_ref[...] = _dot(h_s[...], w_ref[...]).astype(BF16)

    return pl.pallas_call(
        body, name="in_proj", grid=(t // tm, IN_COLS // bn),
        out_shape=[jax.ShapeDtypeStruct((t, IN_COLS), BF16), jax.ShapeDtypeStruct((t, D_MODEL), BF16)],
        in_specs=[pl.BlockSpec((tm, D_MODEL), lambda i, j: (i, 0)), pl.BlockSpec((1, D_MODEL), lambda i, j: (0, 0)),
                  pl.BlockSpec((D_MODEL, bn), lambda i, j: (0, j))],
        out_specs=[pl.BlockSpec((tm, bn), lambda i, j: (i, j)), pl.BlockSpec((tm, D_MODEL), lambda i, j: (i, 0))],
        scratch_shapes=[pltpu.VMEM((tm, D_MODEL), BF16)],
        compiler_params=_params("parallel", "arbitrary"),
    )(x, g1, w_in)


def _section(s, t):
    return pl.BlockSpec((t, CB), lambda h, s=s: (0, s * (D_MODEL // CB) + h))


def _conv_mixer_fwd(proj, w_short):
    t = proj.shape[0]
    rc = _row_chunk(t)

    def body(b_ref, c_ref, x_ref, w_ref, y_ref, pad):
        pad[pl.ds(0, PAD), :] = jnp.zeros((PAD, CB), F32)
        for r0 in range(0, t, rc):
            rows = pl.ds(r0, rc)
            pad[pl.ds(PAD + r0, rc), :] = c_ref[rows, :].astype(F32) * x_ref[rows, :].astype(F32)
        w = w_ref[...]
        for r0 in range(0, t, rc):
            rows = pl.ds(r0, rc)
            y_ref[rows, :] = (b_ref[rows, :].astype(F32) * _conv_causal(pad, w, r0, rc, 3)).astype(BF16)

    return pl.pallas_call(
        body, name="conv_mixer_fwd", grid=(D_MODEL // CB,),
        out_shape=jax.ShapeDtypeStruct((t, D_MODEL), BF16),
        in_specs=[_section(0, t), _section(1, t), _section(2, t), pl.BlockSpec((3, CB), lambda h: (0, h))],
        out_specs=pl.BlockSpec((t, CB), lambda h: (0, h)),
        scratch_shapes=[pltpu.VMEM((t + PAD, CB), F32)],
        compiler_params=_params("parallel"),
    )(proj, proj, proj, w_short)


def _lru_gates(xl, wa, ba, wx, bx, ls, first_row):
    xb = xl.astype(BF16)
    ra = jax.nn.sigmoid(_dot(xb, wa) + ba)
    ia = jax.nn.sigmoid(_dot(xb, wx) + bx)
    la = LRU_C * ra * ls
    a = jnp.exp(la)
    one_minus = -_expm1_neg(2.0 * la)
    mult = jnp.where(first_row, 1.0, jnp.sqrt(one_minus))
    return xb, ra, ia, a, one_minus, mult


def _head_specs():
    vec = pl.BlockSpec((1, CB), lambda h: (0, h))
    mat = pl.BlockSpec((N_DEV, None, HEAD_DIM // N_DEV, HEAD_DIM), lambda h: (0, h, 0, 0))
    return vec, mat


def _lru_fwd(proj, w_conv, b_conv, wa, ba, wx, bx, lam):
    t = proj.shape[0]
    rc = _row_chunk(t)
    vec, mat = _head_specs()

    def body(lx_ref, ly_ref, wc_ref, bc_ref, wa_ref, ba_ref, wx_ref, bx_ref, lam_ref, yb_ref, hl_ref, pad, a_s, u_s):
        pad[pl.ds(0, PAD), :] = jnp.zeros((PAD, CB), F32)
        for r0 in range(0, t, rc):
            pad[pl.ds(PAD + r0, rc), :] = lx_ref[pl.ds(r0, rc), :].astype(F32)
        wc, bc = wc_ref[...], bc_ref[...]
        wa_m, wx_m = wa_ref[...].reshape(HEAD_DIM, HEAD_DIM), wx_ref[...].reshape(HEAD_DIM, HEAD_DIM)
        ls = _log_sigmoid(lam_ref[...])
        for r0 in range(0, t, rc):
            xl = _conv_causal(pad, wc, r0, rc, 4) + bc
            first = (lax.broadcasted_iota(jnp.int32, (rc, CB), 0) + r0) == 0
            _, _, ia, a, _, mult = _lru_gates(xl, wa_m, ba_ref[...], wx_m, bx_ref[...], ls, first)
            a_s[pl.ds(r0, rc), :] = a
            u_s[pl.ds(r0, rc), :] = mult * (ia * xl)

        row = lax.broadcasted_iota(jnp.int32, (SUBLANES, CB), 0)

        def group(g, carry):
            r = pl.multiple_of(g * SUBLANES, SUBLANES)
            a_g, b_g = a_s[pl.ds(r, SUBLANES), :], u_s[pl.ds(r, SUBLANES), :]
            for s in (1, 2, 4):
                keep = row >= s
                b_g = jnp.where(keep, a_g * pltpu.roll(b_g, s, 0) + b_g, b_g)
                a_g = jnp.where(keep, a_g * pltpu.roll(a_g, s, 0), a_g)
            h_g = b_g + a_g * carry
            hl_ref[pl.ds(r, SUBLANES), :] = h_g
            return jnp.broadcast_to(h_g[SUBLANES - 1:SUBLANES, :], (SUBLANES, CB))

        lax.fori_loop(0, t // SUBLANES, group, jnp.zeros((SUBLANES, CB), F32))
        for r0 in range(0, t, rc):
            rows = pl.ds(r0, rc)
            yb_ref[rows, :] = (hl_ref[rows, :] * _gelu(ly_ref[rows, :].astype(F32))).astype(BF16)

    blk = pl.BlockSpec((t, CB), lambda h: (0, h))
    return pl.pallas_call(
        body, name="lru_fwd", grid=(N_HEADS,),
        out_shape=[jax.ShapeDtypeStruct((t, D_MODEL), BF16), jax.ShapeDtypeStruct((t, D_MODEL), F32)],
        in_specs=[_section(3, t), _section(4, t), pl.BlockSpec((4, CB), lambda h: (0, h)), vec, mat, vec, mat, vec, vec],
        out_specs=[blk, blk],
        scratch_shapes=[pltpu.VMEM((t + PAD, CB), F32), pltpu.VMEM((t, CB), F32), pltpu.VMEM((t, CB), F32)],
        compiler_params=_params("parallel"),
    )(proj, proj, w_conv, b_conv, wa, ba, wx, bx, lam)


def _merge(y_a, y_b, proj, x, w_cb, w_lb, w_out, g2, g3):
    t = x.shape[0]
    tm = min(256, t)

    def body(ya_ref, yb_ref, gc_ref, gl_ref, x_ref, wcb_ref, wlb_ref, wo_ref, g2_ref, g3_ref,
             pa_ref, pb_ref, mg_ref, mix_ref, x1_ref, h2_ref):
        pa = _dot(ya_ref[...], wcb_ref[...]).astype(BF16)
        pb = _dot(yb_ref[...], wlb_ref[...]).astype(BF16)
        pa_ref[...] = pa
        pb_ref[...] = pb
        merged = (jax.nn.sigmoid(gc_ref[...].astype(F32)) * pa.astype(F32)
                  + jax.nn.sigmoid(gl_ref[...].astype(F32)) * pb.astype(F32)).astype(BF16)
        mg_ref[...] = merged
        mix = _dot(merged, wo_ref[...])
        mix_ref[...] = mix
        n2, _ = _rms_fwd(mix)
        x1 = x_ref[...] + n2 * g2_ref[...]
        x1_ref[...] = x1
        n3, _ = _rms_fwd(x1)
        h2_ref[...] = (n3 * g3_ref[...]).astype(BF16)

    row = pl.BlockSpec((tm, D_MODEL), lambda i: (i, 0))
    full = pl.BlockSpec((D_MODEL, D_MODEL), lambda i: (0, 0))
    vec = pl.BlockSpec((1, D_MODEL), lambda i: (0, 0))
    act = jax.ShapeDtypeStruct((t, D_MODEL), BF16)
    res = jax.ShapeDtypeStruct((t, D_MODEL), F32)
    return pl.pallas_call(
        body, name="merge_fwd", grid=(t // tm,), out_shape=[act, act, act, res, res, act],
        in_specs=[row, row, pl.BlockSpec((tm, D_MODEL), lambda i: (i, 5)), pl.BlockSpec((tm, D_MODEL), lambda i: (i, 6)),
                  row, full, full, full, vec, vec],
        out_specs=[row] * 6,
        compiler_params=_params("parallel"),
    )(y_a, y_b, proj, proj, x, w_cb, w_lb, w_out, g2, g3)


N_FF_BLOCKS = D_FF // CB


def _ffn_up(h2, w_up, w_conv, b_conv):
    t = h2.shape[0]
    rc = _row_chunk(t)

    def body(h_ref, wg_ref, wv_ref, cg_ref, cv_ref, bg_ref, bv_ref, ug_ref, uv_ref, f_ref, pad_g, pad_v):
        zeros = jnp.zeros((PAD, CB), F32)
        pad_g[pl.ds(0, PAD), :] = zeros
        pad_v[pl.ds(0, PAD), :] = zeros
        for r0 in range(0, t, rc):
            rows = pl.ds(r0, rc)
            ug = _dot(h_ref[rows, :], wg_ref[...]).astype(BF16)
            uv = _dot(h_ref[rows, :], wv_ref[...]).astype(BF16)
            ug_ref[rows, :] = ug
            uv_ref[rows, :] = uv
            pad_g[pl.ds(PAD + r0, rc), :] = ug.astype(F32)
            pad_v[pl.ds(PAD + r0, rc), :] = uv.astype(F32)
        cg, cv = cg_ref[...], cv_ref[...]
        for r0 in range(0, t, rc):
            gate = _conv_causal(pad_g, cg, r0, rc, 3) + bg_ref[...]
            val = _conv_causal(pad_v, cv, r0, rc, 3) + bv_ref[...]
            f_ref[pl.ds(r0, rc), :] = (_gelu(gate) * val).astype(BF16)

    nb = N_FF_BLOCKS
    act = jax.ShapeDtypeStruct((t, D_FF), BF16)
    blk = pl.BlockSpec((t, CB), lambda j: (0, j))
    return pl.pallas_call(
        body, name="ffn_up_fwd", grid=(nb,), out_shape=[act, act, act],
        in_specs=[pl.BlockSpec((t, D_MODEL), lambda j: (0, 0)),
                  pl.BlockSpec((D_MODEL, CB), lambda j: (0, j)), pl.BlockSpec((D_MODEL, CB), lambda j: (0, nb + j)),
                  pl.BlockSpec((3, CB), lambda j: (0, j)), pl.BlockSpec((3, CB), lambda j: (0, nb + j)),
                  pl.BlockSpec((1, CB), lambda j: (0, j)), pl.BlockSpec((1, CB), lambda j: (0, nb + j))],
        out_specs=[blk, blk, blk],
        scratch_shapes=[pltpu.VMEM((t + PAD, CB), F32), pltpu.VMEM((t + PAD, CB), F32)],
        compiler_params=_params("parallel"),
    )(h2, w_up, w_up, w_conv, w_conv, b_conv, b_conv)


def _ffn_down(f, w_down, x1, target, g4):
    t = f.shape[0]
    tm = min(256, t)

    def body(f_ref, w_ref, x1_ref, tg_ref, g_ref, dy_ref, dout_ref, df_ref, dg_ref, loss_ref):
        @pl.when(pl.program_id(0) == 0)
        def _():
            dg_ref[...] = jnp.zeros_like(dg_ref)
            loss_ref[...] = jnp.zeros_like(loss_ref)
        out = _dot(f_ref[...], w_ref[...])
        n4, r4 = _rms_fwd(out)
        err = x1_ref[...] + n4 * g_ref[...] - tg_ref[...]
        loss_ref[...] += jnp.full(loss_ref.shape, 0.5 / D_MODEL, F32) * jnp.sum(err * err)
        dy = err * (1.0 / D_MODEL)
        dy_ref[...] = dy
        dg_ref[...] += jnp.sum(dy * n4, axis=0, keepdims=True)
        d_out = _rms_bwd(n4, r4, dy * g_ref[...]).astype(BF16)
        dout_ref[...] = d_out
        df_ref[...] = _dot_nt(d_out, w_ref[...]).astype(BF16)

    row = pl.BlockSpec((tm, D_MODEL), lambda i: (i, 0))
    wide = pl.BlockSpec((tm, D_FF), lambda i: (i, 0))
    vec = pl.BlockSpec((1, D_MODEL), lambda i: (0, 0))
    return pl.pallas_call(
        body, name="ffn_down_fwd_bwd", grid=(t // tm,),
        out_shape=[jax.ShapeDtypeStruct((t, D_MODEL), F32), jax.ShapeDtypeStruct((t, D_MODEL), BF16),
                   jax.ShapeDtypeStruct((t, D_FF), BF16), jax.ShapeDtypeStruct((1, D_MODEL), F32),
                   jax.ShapeDtypeStruct((SUBLANES, LANES), F32)],
        in_specs=[wide, pl.BlockSpec((D_FF, D_MODEL), lambda i: (0, 0)), row, row, vec],
        out_specs=[row, row, wide, vec, pl.BlockSpec((SUBLANES, LANES), lambda i: (0, 0))],
        compiler_params=_params("arbitrary"),
    )(f, w_down, x1, target, g4)


def _grad_tn(a, b, bm, name):
    t, m = a.shape
    n = b.shape[1]

    def body(a_ref, b_ref, o_ref):
        o_ref[...] = _dot_tn(a_ref[...], b_ref[...]).astype(BF16)

    return pl.pallas_call(
        body, name=name, grid=(m // bm,), out_shape=jax.ShapeDtypeStruct((m, n), BF16),
        in_specs=[pl.BlockSpec((t, bm), lambda i: (0, i)), pl.BlockSpec((t, n), lambda i: (0, 0))],
        out_specs=pl.BlockSpec((bm, n), lambda i: (i, 0)),
        compiler_params=_params("parallel"),
    )(a, b)


def _ffn_up_bwd(up_g, up_v, d_f, w_conv, b_conv, h2, w_up):
    t = h2.shape[0]
    rc = _row_chunk(t)
    nb = N_FF_BLOCKS

    def body(ug_ref, uv_ref, df_ref, cg_ref, cv_ref, bg_ref, bv_ref, h_ref, w_ref,
             dw_ref, dcw_ref, dcb_ref, dh_ref, d_up, small, pad_g, pad_v, back_g, back_v):
        j, k = pl.program_id(0), pl.program_id(1)

        @pl.when((j == 0) & (k == 0))
        def _():
            dh_ref[...] = jnp.zeros_like(dh_ref)

        @pl.when(k == 0)
        def _():
            zeros = jnp.zeros((PAD, CB), F32)
            pad_g[pl.ds(0, PAD), :] = zeros
            pad_v[pl.ds(0, PAD), :] = zeros
            back_g[pl.ds(t, PAD), :] = zeros
            back_v[pl.ds(t, PAD), :] = zeros
            for r0 in range(0, t, rc):
                pad_g[pl.ds(PAD + r0, rc), :] = ug_ref[pl.ds(r0, rc), :].astype(F32)
                pad_v[pl.ds(PAD + r0, rc), :] = uv_ref[pl.ds(r0, rc), :].astype(F32)
            cg, cv = cg_ref[...], cv_ref[...]
            for r0 in range(0, t, rc):
                rows = pl.ds(r0, rc)
                gate = _conv_causal(pad_g, cg, r0, rc, 3) + bg_ref[...]
                val = _conv_causal(pad_v, cv, r0, rc, 3) + bv_ref[...]
                act, d_act = _gelu_and_grad(gate)
                d_f = df_ref[rows, :].astype(F32)
                back_g[rows, :] = d_f * val * d_act
                back_v[rows, :] = d_f * act
            for which, (back, pad, cw) in enumerate(((back_g, pad_g, cg), (back_v, pad_v, cv))):
                taps = [jnp.zeros((1, CB), F32)] * 3
                bias = jnp.zeros((1, CB), F32)
                for r0 in range(0, t, rc):
                    rows = pl.ds(r0, rc)
                    d_up[which, rows, :] = _conv_anticausal(back, cw, r0, rc, 3).astype(BF16)
                    g = back[rows, :]
                    taps = [acc + new for acc, new in zip(taps, _conv_wgrad(g, pad, r0, rc, 3))]
                    bias = bias + jnp.sum(g, axis=0, keepdims=True)
                small[which] = jnp.concatenate(taps + [bias] + [jnp.zeros((SUBLANES - 4, CB), F32)], axis=0)

        for r0 in range(0, t, rc):
            rows = pl.ds(r0, rc)
            dh_ref[rows, :] += _dot_nt(d_up[k, rows, :], w_ref[...])
        dw_ref[...] = _dot_tn(h_ref[...], d_up[k]).astype(BF16)
        dcw_ref[...] = small[k, pl.ds(0, 3), :]
        dcb_ref[...] = small[k, pl.ds(3, 1), :]

    blk = pl.BlockSpec((t, CB), lambda j, k: (0, j))
    both = lambda rows: pl.BlockSpec((rows, CB), lambda j, k: (0, nb * k + j))
    gate = lambda rows: pl.BlockSpec((rows, CB), lambda j, k: (0, j))
    val = lambda rows: pl.BlockSpec((rows, CB), lambda j, k: (0, nb + j))
    return pl.pallas_call(
        body, name="ffn_up_bwd", grid=(nb, 2),
        out_shape=[jax.ShapeDtypeStruct((D_MODEL, 2 * D_FF), BF16), jax.ShapeDtypeStruct((3, 2 * D_FF), F32),
                   jax.ShapeDtypeStruct((1, 2 * D_FF), F32), jax.ShapeDtypeStruct((t, D_MODEL), F32)],
        in_specs=[blk, blk, blk, gate(3), val(3), gate(1), val(1),
                  pl.BlockSpec((t, D_MODEL), lambda j, k: (0, 0)), both(D_MODEL)],
        out_specs=[both(D_MODEL), both(3), both(1), pl.BlockSpec((t, D_MODEL), lambda j, k: (0, 0))],
        scratch_shapes=[pltpu.VMEM((2, t, CB), BF16), pltpu.VMEM((2, SUBLANES, CB), F32)]
        + [pltpu.VMEM((t + PAD, CB), F32)] * 4,
        compiler_params=_params("arbitrary", "arbitrary"),
    )(up_g, up_v, d_f, w_conv, w_conv, b_conv, b_conv, h2, w_up)


def _merge_bwd(dy, d_h2, x1, mix, g3, g2, w_out, w_cb, w_lb, pa, pb, proj):
    t = dy.shape[0]
    tm = min(256, t)

    def body(dy_ref, dh2_ref, x1_ref, mix_ref, g3_ref, g2_ref, wo_ref, wcb_ref, wlb_ref, pa_ref, pb_ref, gc_ref, gl_ref,
             dx1_ref, dmix_ref, dpa_ref, dpb_ref, dya_ref, dyb_ref, dgate_ref, dg3_ref, dg2_ref):
        @pl.when(pl.program_id(0) == 0)
        def _():
            dg3_ref[...] = jnp.zeros_like(dg3_ref)
            dg2_ref[...] = jnp.zeros_like(dg2_ref)
        n3, r3 = _rms_fwd(x1_ref[...])
        d_h2 = dh2_ref[...]
        dg3_ref[...] += jnp.sum(d_h2 * n3, axis=0, keepdims=True)
        dx1 = dy_ref[...] + _rms_bwd(n3, r3, d_h2 * g3_ref[...])
        dx1_ref[...] = dx1
        n2, r2 = _rms_fwd(mix_ref[...])
        dg2_ref[...] += jnp.sum(dx1 * n2, axis=0, keepdims=True)
        d_mix = _rms_bwd(n2, r2, dx1 * g2_ref[...]).astype(BF16)
        dmix_ref[...] = d_mix
        d_merged = _dot_nt(d_mix, wo_ref[...])
        sc = jax.nn.sigmoid(gc_ref[...].astype(F32))
        sl = jax.nn.sigmoid(gl_ref[...].astype(F32))
        d_pa = (d_merged * sc).astype(BF16)
        d_pb = (d_merged * sl).astype(BF16)
        dpa_ref[...] = d_pa
        dpb_ref[...] = d_pb
        dgate_ref[0] = (d_merged * pa_ref[...].astype(F32) * sc * (1.0 - sc)).astype(BF16)
        dgate_ref[1] = (d_merged * pb_ref[...].astype(F32) * sl * (1.0 - sl)).astype(BF16)
        dya_ref[...] = _dot_nt(d_pa, wcb_ref[...]).astype(BF16)
        dyb_ref[...] = _dot_nt(d_pb, wlb_ref[...]).astype(BF16)

    row = pl.BlockSpec((tm, D_MODEL), lambda i: (i, 0))
    full = pl.BlockSpec((D_MODEL, D_MODEL), lambda i: (0, 0))
    vec = pl.BlockSpec((1, D_MODEL), lambda i: (0, 0))
    act = jax.ShapeDtypeStruct((t, D_MODEL), BF16)
    small = jax.ShapeDtypeStruct((1, D_MODEL), F32)
    return pl.pallas_call(
        body, name="merge_bwd", grid=(t // tm,),
        out_shape=[jax.ShapeDtypeStruct((t, D_MODEL), F32), act, act, act, act, act,
                   jax.ShapeDtypeStruct((2, t, D_MODEL), BF16), small, small],
        in_specs=[row, row, row, row, vec, vec, full, full, full, row, row,
                  pl.BlockSpec((tm, D_MODEL), lambda i: (i, 5)), pl.BlockSpec((tm, D_MODEL), lambda i: (i, 6))],
        out_specs=[row] * 6 + [pl.BlockSpec((2, tm, D_MODEL), lambda i: (0, i, 0)), vec, vec],
        compiler_params=_params("arbitrary"),
    )(dy, d_h2, x1, mix, g3, g2, w_out, w_cb, w_lb, pa, pb, proj, proj)


def _conv_mixer_bwd(proj, d_ya, w_short):
    t = proj.shape[0]
    rc = _row_chunk(t)

    def body(b_ref, c_ref, x_ref, dy_ref, w_ref, d_ref, dw_ref, pad, back):
        pad[pl.ds(0, PAD), :] = jnp.zeros((PAD, CB), F32)
        back[pl.ds(t, PAD), :] = jnp.zeros((PAD, CB), F32)
        for r0 in range(0, t, rc):
            rows = pl.ds(r0, rc)
            pad[pl.ds(PAD + r0, rc), :] = c_ref[rows, :].astype(F32) * x_ref[rows, :].astype(F32)
        w = w_ref[...]
        for r0 in range(0, t, rc):
            rows = pl.ds(r0, rc)
            d_y = dy_ref[rows, :].astype(F32)
            d_ref[0, rows, :] = (d_y * _conv_causal(pad, w, r0, rc, 3)).astype(BF16)
            back[rows, :] = d_y * b_ref[rows, :].astype(F32)
        taps = [jnp.zeros((1, CB), F32)] * 3
        for r0 in range(0, t, rc):
            rows = pl.ds(r0, rc)
            d_u = _conv_anticausal(back, w, r0, rc, 3)
            d_ref[1, rows, :] = (d_u * x_ref[rows, :].astype(F32)).astype(BF16)
            d_ref[2, rows, :] = (d_u * c_ref[rows, :].astype(F32)).astype(BF16)
            taps = [acc + new for acc, new in zip(taps, _conv_wgrad(back[rows, :], pad, r0, rc, 3))]
        dw_ref[...] = jnp.concatenate(taps, axis=0)

    blk = pl.BlockSpec((t, CB), lambda h: (0, h))
    return pl.pallas_call(
        body, name="conv_mixer_bwd", grid=(D_MODEL // CB,),
        out_shape=[jax.ShapeDtypeStruct((3, t, D_MODEL), BF16), jax.ShapeDtypeStruct((3, D_MODEL), F32)],
        in_specs=[_section(0, t), _section(1, t), _section(2, t), blk, pl.BlockSpec((3, CB), lambda h: (0, h))],
        out_specs=[pl.BlockSpec((3, t, CB), lambda h: (0, 0, h)), pl.BlockSpec((3, CB), lambda h: (0, h))],
        scratch_shapes=[pltpu.VMEM((t + PAD, CB), F32), pltpu.VMEM((t + PAD, CB), F32)],
        compiler_params=_params("parallel"),
    )(proj, proj, proj, d_ya, w_short)


LRU_SMALL_ROWS = 8


def _lru_bwd(proj, hl, d_yb, w_conv, b_conv, wa, ba, wx, bx, lam):
    t = proj.shape[0]
    rc = _row_chunk(t)
    vec, mat = _head_specs()

    def body(lx_ref, ly_ref, hl_ref, dy_ref, wc_ref, bc_ref, wa_ref, ba_ref, wx_ref, bx_ref, lam_ref,
             d_ref, dwa_ref, dwx_ref, small_ref, pad, a_next, dh_s, h_prev, back, acc_a, acc_x):
        zeros = jnp.zeros((PAD, CB), F32)
        pad[pl.ds(0, PAD), :] = zeros
        h_prev[pl.ds(0, PAD), :] = zeros
        a_next[pl.ds(t, PAD), :] = zeros
        back[pl.ds(t, PAD), :] = zeros
        for r0 in range(0, t, rc):
            pad[pl.ds(PAD + r0, rc), :] = lx_ref[pl.ds(r0, rc), :].astype(F32)
            h_prev[pl.ds(PAD + r0, rc), :] = hl_ref[pl.ds(r0, rc), :]
        wc, bc = wc_ref[...], bc_ref[...]
        wa_m, wx_m = wa_ref[...].reshape(HEAD_DIM, HEAD_DIM), wx_ref[...].reshape(HEAD_DIM, HEAD_DIM)
        ls = _log_sigmoid(lam_ref[...])

        def gates(r0):
            xl = _conv_causal(pad, wc, r0, rc, 4) + bc
            first = (lax.broadcasted_iota(jnp.int32, (rc, CB), 0) + r0) == 0
            return (xl, first) + _lru_gates(xl, wa_m, ba_ref[...], wx_m, bx_ref[...], ls, first)

        for r0 in range(0, t, rc):
            rows = pl.ds(r0, rc)
            a = gates(r0)[5]
            a_next[pl.ds(PAD - 1 + r0, rc), :] = a
            act, d_act = _gelu_and_grad(ly_ref[rows, :].astype(F32))
            d_y = dy_ref[rows, :].astype(F32)
            dh_s[rows, :] = d_y * act
            d_ref[1, rows, :] = (d_y * hl_ref[rows, :] * d_act).astype(BF16)

        row = lax.broadcasted_iota(jnp.int32, (SUBLANES, CB), 0)
        groups = t // SUBLANES

        def group(i, carry):
            r = pl.multiple_of((groups - 1 - i) * SUBLANES, SUBLANES)
            a_g, b_g = a_next[pl.ds(PAD + r, SUBLANES), :], dh_s[pl.ds(r, SUBLANES), :]
            for s in (1, 2, 4):
                keep = row < SUBLANES - s
                b_g = jnp.where(keep, a_g * pltpu.roll(b_g, SUBLANES - s, 0) + b_g, b_g)
                a_g = jnp.where(keep, a_g * pltpu.roll(a_g, SUBLANES - s, 0), a_g)
            d_g = b_g + a_g * carry
            dh_s[pl.ds(r, SUBLANES), :] = d_g
            return jnp.broadcast_to(d_g[0:1, :], (SUBLANES, CB))

        lax.fori_loop(0, groups, group, jnp.zeros((SUBLANES, CB), F32))

        acc_a[...] = jnp.zeros_like(acc_a)
        acc_x[...] = jnp.zeros_like(acc_x)
        d_ba = d_bx = d_ls = jnp.zeros((1, CB), F32)
        for r0 in range(0, t, rc):
            rows = pl.ds(r0, rc)
            xl, first, xb, ra, ia, a, one_minus, mult = gates(r0)
            d_h = dh_s[rows, :]
            d_a = d_h * h_prev[pl.ds(PAD - 1 + r0, rc), :]
            d_mult = d_h * ia * xl
            d_ia = d_h * mult * xl
            d_xl = d_h * mult * ia
            d_mult_d_la = jnp.where(first, 0.0, (one_minus - 1.0) / mult)
            d_la = d_a * a + d_mult * d_mult_d_la
            d_ls = d_ls + jnp.sum(d_la * ra, axis=0, keepdims=True) * LRU_C
            d_za = d_la * (LRU_C * ls) * ra * (1.0 - ra)
            d_zx = d_ia * ia * (1.0 - ia)
            d_ba = d_ba + jnp.sum(d_za, axis=0, keepdims=True)
            d_bx = d_bx + jnp.sum(d_zx, axis=0, keepdims=True)
            d_za, d_zx = d_za.astype(BF16), d_zx.astype(BF16)
            acc_a[...] += _dot_tn(xb, d_za)
            acc_x[...] += _dot_tn(xb, d_zx)
            back[rows, :] = d_xl + _dot_nt(d_za, wa_m) + _dot_nt(d_zx, wx_m)
        taps = [jnp.zeros((1, CB), F32)] * 4
        d_bc = jnp.zeros((1, CB), F32)
        for r0 in range(0, t, rc):
            rows = pl.ds(r0, rc)
            d_ref[0, rows, :] = _conv_anticausal(back, wc, r0, rc, 4).astype(BF16)
            g = back[rows, :]
            taps = [acc + new for acc, new in zip(taps, _conv_wgrad(g, pad, r0, rc, 4))]
            d_bc = d_bc + jnp.sum(g, axis=0, keepdims=True)
        d_lam = d_ls * jax.nn.sigmoid(-lam_ref[...])
        small_ref[...] = jnp.concatenate(taps + [d_bc, d_ba, d_bx, d_lam], axis=0)
        dwa_ref[...] = acc_a[...].reshape(N_DEV, HEAD_DIM // N_DEV, HEAD_DIM).astype(BF16)
        dwx_ref[...] = acc_x[...].reshape(N_DEV, HEAD_DIM // N_DEV, HEAD_DIM).astype(BF16)

    blk = pl.BlockSpec((t, CB), lambda h: (0, h))
    gate_grad = jax.ShapeDtypeStruct((N_DEV, N_HEADS, HEAD_DIM // N_DEV, HEAD_DIM), BF16)
    return pl.pallas_call(
        body, name="lru_bwd", grid=(N_HEADS,),
        out_shape=[jax.ShapeDtypeStruct((2, t, D_MODEL), BF16), gate_grad, gate_grad,
                   jax.ShapeDtypeStruct((LRU_SMALL_ROWS, D_MODEL), F32)],
        in_specs=[_section(3, t), _section(4, t), blk, blk, pl.BlockSpec((4, CB), lambda h: (0, h)),
                  vec, mat, vec, mat, vec, vec],
        out_specs=[pl.BlockSpec((2, t, CB), lambda h: (0, 0, h)), mat, mat,
                   pl.BlockSpec((LRU_SMALL_ROWS, CB), lambda h: (0, h))],
        scratch_shapes=[pltpu.VMEM((t + PAD, CB), F32), pltpu.VMEM((t + PAD, CB), F32), pltpu.VMEM((t, CB), F32),
                        pltpu.VMEM((t + PAD, CB), F32), pltpu.VMEM((t + PAD, CB), F32),
                        pltpu.VMEM((HEAD_DIM, HEAD_DIM), F32), pltpu.VMEM((HEAD_DIM, HEAD_DIM), F32)],
        compiler_params=_params("parallel"),
    )(proj, proj, hl, d_yb, w_conv, b_conv, wa, ba, wx, bx, lam)


def _stack_maps(halves):
    def conv(sec, part):
        return jnp.minimum(sec, 2), jnp.where(sec < 3, part, halves - 1)

    def lru(sec, part):
        return jnp.clip(sec - 3, 0, 1), jnp.where(sec < 3, 0, jnp.where(sec < 5, part, halves - 1))

    def gate(sec, part):
        return jnp.clip(sec - 5, 0, 1), jnp.where(sec < 5, 0, part)

    return conv, lru, gate


def _pick_stack(sec, refs, fn):
    @pl.when(sec < 3)
    def _():
        fn(refs[0])

    @pl.when((sec >= 3) & (sec < 5))
    def _():
        fn(refs[1])

    @pl.when(sec >= 5)
    def _():
        fn(refs[2])


def _in_proj_wgrad(h, d_conv, d_lru, d_gate):
    t = h.shape[0]
    halves, bn = 2, D_MODEL // 2
    maps = _stack_maps(halves)

    def body(h_ref, dc_ref, dl_ref, dg_ref, o_ref):
        def emit(ref):
            o_ref[...] = _dot_tn(h_ref[...], ref[...]).astype(BF16)
        _pick_stack(pl.program_id(0) // halves, (dc_ref, dl_ref, dg_ref), emit)

    def spec(m):
        def index(s):
            stack, part = m(s // halves, s % halves)
            return stack, 0, part
        return pl.BlockSpec((None, t, bn), index)

    return pl.pallas_call(
        body, name="in_proj_wgrad", grid=(7 * halves,), out_shape=jax.ShapeDtypeStruct((D_MODEL, IN_COLS), BF16),
        in_specs=[pl.BlockSpec((t, D_MODEL), lambda s: (0, 0))] + [spec(m) for m in maps],
        out_specs=pl.BlockSpec((D_MODEL, bn), lambda s: (0, s)),
        compiler_params=_params("arbitrary"),
    )(h, d_conv, d_lru, d_gate)


def _in_proj_xgrad(d_conv, d_lru, d_gate, w_in, x, dx1, g1):
    t = x.shape[0]
    tm = min(512, t)
    maps = _stack_maps(1)

    def body(dc_ref, dl_ref, dg_ref, w_ref, x_ref, dx1_ref, g_ref, dx_ref, dgain_ref, acc):
        i, s = pl.program_id(0), pl.program_id(1)

        @pl.when((i == 0) & (s == 0))
        def _():
            dgain_ref[...] = jnp.zeros_like(dgain_ref)

        @pl.when(s == 0)
        def _():
            acc[...] = jnp.zeros_like(acc)

        def add(ref):
            acc[...] += _dot_nt(ref[...], w_ref[...])
        _pick_stack(s, (dc_ref, dl_ref, dg_ref), add)

        @pl.when(s == 6)
        def _():
            n1, r1 = _rms_fwd(x_ref[...])
            d_h = acc[...]
            dgain_ref[...] += jnp.sum(d_h * n1, axis=0, keepdims=True)
            dx_ref[...] = dx1_ref[...] + _rms_bwd(n1, r1, d_h * g_ref[...])

    def spec(m):
        def index(i, s):
            return m(s, 0)[0], i, 0
        return pl.BlockSpec((None, tm, D_MODEL), index)

    row = pl.BlockSpec((tm, D_MODEL), lambda i, s: (i, 0))
    vec = pl.BlockSpec((1, D_MODEL), lambda i, s: (0, 0))
    return pl.pallas_call(
        body, name="in_proj_xgrad", grid=(t // tm, 7),
        out_shape=[jax.ShapeDtypeStruct((t, D_MODEL), F32), jax.ShapeDtypeStruct((1, D_MODEL), F32)],
        in_specs=[spec(m) for m in maps] + [pl.BlockSpec((D_MODEL, D_MODEL), lambda i, s: (0, s)), row, row, vec],
        out_specs=[row, vec],
        scratch_shapes=[pltpu.VMEM((tm, D_MODEL), F32)],
        compiler_params=_params("arbitrary", "arbitrary"),
    )(d_conv, d_lru, d_gate, w_in, x, dx1, g1)


def _add_pair(grad, got, by_cols, pos, name):
    cols = got.shape[-1]
    got3 = got.reshape(4, -1, cols)
    rows = got3.shape[1]
    rb = _row_block(rows, 512)

    def block(k, p):
        return 4 * ((p[0] + k % 2) % 2) + 2 * ((p[1] + k // 2) % 2) + p[2]

    if by_cols:
        g_in, g_spec = grad, pl.BlockSpec((rb, cols), lambda k, i, p: (i, block(k, p)))
    else:
        g_in = grad.reshape(N_DEV, rows, cols)
        g_spec = pl.BlockSpec((None, rb, cols), lambda k, i, p: (block(k, p), i, 0))
    slot = pl.BlockSpec((None, rb, cols), lambda k, i, p: (k, i, 0))

    def body(pos_ref, a_ref, b_ref, o_ref):
        o_ref[...] = (a_ref[...].astype(F32) + b_ref[...].astype(F32)).astype(BF16)

    out = pl.pallas_call(
        body, name=name, out_shape=jax.ShapeDtypeStruct(got3.shape, BF16),
        grid_spec=pltpu.PrefetchScalarGridSpec(num_scalar_prefetch=1, grid=(4, rows // rb),
                                               in_specs=[g_spec, slot], out_specs=slot),
        compiler_params=_params("parallel", "parallel"),
    )(pos, g_in, got3)
    return out.reshape(got.shape)


def _adamw(w, g, m, v):
    m = ADAM_B1 * m + (1.0 - ADAM_B1) * g
    v = ADAM_B2 * v + (1.0 - ADAM_B2) * (g * g)
    m_hat = m / (1.0 - ADAM_B1 ** ADAM_STEP)
    v_hat = v / (1.0 - ADAM_B2 ** ADAM_STEP)
    return -ADAM_LR * (m_hat / (jnp.sqrt(v_hat) + ADAM_EPS) + ADAM_WD * w), m, v


def _adam_large(w, m, v, own, others, name):
    shape = w.shape
    cols = shape[-1]
    w2, m2, v2 = (a.reshape(-1, cols) for a in (w, m, v))
    rows = w2.shape[0]
    own, others = own.reshape(4, rows, cols), others.reshape(3, rows, cols)
    rb = _row_block(rows, 256)

    def body(w_ref, m_ref, v_ref, own_ref, oth_ref, g_ref, d_ref, nm_ref, nv_ref):
        g = own_ref[...].astype(F32)
        for k in range(3):
            g = g + oth_ref[k].astype(F32)
        g_ref[...] = g
        d_ref[...], nm_ref[...], nv_ref[...] = _adamw(w_ref[...], g, m_ref[...], v_ref[...])

    blk = pl.BlockSpec((rb, cols), lambda i: (i, 0))
    res = jax.ShapeDtypeStruct((rows, cols), F32)
    outs = pl.pallas_call(
        body, name=name, grid=(rows // rb,), out_shape=[res] * 4,
        in_specs=[blk, blk, blk, pl.BlockSpec((None, rb, cols), lambda i: (0, i, 0)),
                  pl.BlockSpec((3, rb, cols), lambda i: (0, i, 0))],
        out_specs=[blk] * 4, compiler_params=_params("parallel"),
    )(w2, m2, v2, own, others)
    return [o.reshape(shape) for o in outs]


def _adam_small(ws, gs, ms, vs):
    n = len(ws)

    def body(*refs):
        w_refs, g_refs, m_refs, v_refs = (refs[i * n:(i + 1) * n] for i in range(4))
        outs = refs[4 * n:]
        for i in range(n):
            d, m, v = _adamw(w_refs[i][...], g_refs[i][...], m_refs[i][...], v_refs[i][...])
            outs[i][...], outs[n + i][...], outs[2 * n + i][...] = d, m, v

    shapes = [jax.ShapeDtypeStruct(w.shape, F32) for w in ws]
    outs = pl.pallas_call(
        body, name="adam_small", out_shape=shapes * 3,
        in_specs=[VMEM_SPEC] * (4 * n), out_specs=[VMEM_SPEC] * (3 * n), compiler_params=_params(),
    )(*ws, *gs, *ms, *vs)
    return outs[:n], outs[n:2 * n], outs[2 * n:]


def _pack_rows(pieces):
    tile = SUBLANES * LANES
    return jnp.concatenate([jnp.pad(p.reshape(-1), (0, (-p.size) % tile)).reshape(-1, LANES) for p in pieces], axis=0)


def _packed_starts(sizes):
    tile = SUBLANES * LANES
    starts = [0]
    for s in sizes:
        starts.append(starts[-1] + (s + tile - 1) // tile * SUBLANES)
    return starts


def kernel(x, norm_mix_pre, norm_mix_post, norm_ffn_pre, norm_ffn_post, w_in, conv_short_w, w_conv_branch, lru_conv_w, lru_conv_b, lru_wa, lru_ba, lru_wx, lru_bx, lru_lambda, w_lru_branch, w_out, ffn_w_up, ffn_conv_w, ffn_conv_b, ffn_w_down, loss_target, m_norm_mix_pre, m_norm_mix_post, m_norm_ffn_pre, m_norm_ffn_post, m_w_in, m_conv_short_w, m_w_conv_branch, m_lru_conv_w, m_lru_conv_b, m_lru_wa, m_lru_ba, m_lru_wx, m_lru_bx, m_lru_lambda, m_w_lru_branch, m_w_out, m_ffn_w_up, m_ffn_conv_w, m_ffn_conv_b, m_ffn_w_down, v_norm_mix_pre, v_norm_mix_post, v_norm_ffn_pre, v_norm_ffn_post, v_w_in, v_conv_short_w, v_w_conv_branch, v_lru_conv_w, v_lru_conv_b, v_lru_wa, v_lru_ba, v_lru_wx, v_lru_bx, v_lru_lambda, v_w_lru_branch, v_w_out, v_ffn_w_up, v_ffn_conv_w, v_ffn_conv_b, v_ffn_w_down):
    t = x.shape[1]
    xi, yi, ci = _position()
    me = _block_of(xi, yi, ci)
    x2, target = x[0], loss_target[0]
    shard_in, shard_up = IN_COLS // N_DEV, 2 * D_FF // N_DEV
    shard_sq, shard_down, shard_head = D_MODEL // N_DEV, D_FF // N_DEV, HEAD_DIM // N_DEV

    large = [w_in[0], w_conv_branch[0], w_lru_branch[0], w_out[0], lru_wa[0], lru_wx[0], ffn_w_up[0], ffn_w_down[0]]
    blocks = [_cols(shard_in), _rows(shard_sq), _rows(shard_sq), _rows(shard_sq), _lead, _lead,
              _cols(shard_up), _rows(shard_down)]
    gate_full = (N_DEV, N_HEADS, shard_head, HEAD_DIM)
    full_shapes = [(D_MODEL, IN_COLS), (D_MODEL, D_MODEL), (D_MODEL, D_MODEL), (D_MODEL, D_MODEL), gate_full, gate_full,
                   (D_MODEL, 2 * D_FF), (D_FF, D_MODEL)]
    small_sharded = [conv_short_w, lru_conv_w, lru_ba, lru_bx, ffn_conv_w]
    small_mine = _pack_rows(small_sharded)
    small_at = _packed_starts([p.size for p in small_sharded])
    *gathered, small_all = _gather_weights(large, blocks, full_shapes, small_mine)
    g_in, g_cb, g_lb, g_out, g_wa, g_wx, g_up, g_down = gathered

    def cols_of(r0, n, width):
        part = small_all[:, r0:r0 + n * width // LANES, :].reshape(N_DEV, n, width)
        return part.transpose(1, 0, 2).reshape(n, N_DEV * width)

    c_short = cols_of(small_at[0], 3, LANES)
    c_lru = cols_of(small_at[1], 4, LANES)
    b_a = cols_of(small_at[2], N_HEADS, shard_head).reshape(1, D_MODEL)
    b_x = cols_of(small_at[3], N_HEADS, shard_head).reshape(1, D_MODEL)
    c_ffn = cols_of(small_at[4], 3, shard_up)

    proj, h = _in_proj(x2, norm_mix_pre, g_in)
    y_a = _conv_mixer_fwd(proj, c_short)
    y_b, hl = _lru_fwd(proj, c_lru, lru_conv_b, g_wa, b_a, g_wx, b_x, lru_lambda)
    pa, pb, merged, mix, x1, h2 = _merge(y_a, y_b, proj, x2, g_cb, g_lb, g_out, norm_mix_post, norm_ffn_pre)
    up_g, up_v, f = _ffn_up(h2, g_up, c_ffn, ffn_conv_b)
    dy, d_out, d_f, dg4, loss_part = _ffn_down(f, g_down, x1, target, norm_ffn_post)

    gw_down = _grad_tn(f, d_out, min(512, D_FF), "ffn_down_wgrad")
    gw_up, gc_ffn, gb_ffn, d_h2 = _ffn_up_bwd(up_g, up_v, d_f, c_ffn, ffn_conv_b, h2, g_up)
    dx1, d_mix, d_pa, d_pb, d_ya, d_yb, d_gate, dg3, dg2 = _merge_bwd(
        dy, d_h2, x1, mix, norm_ffn_pre, norm_mix_post, g_out, g_cb, g_lb, pa, pb, proj)
    gw_out = _grad_tn(merged, d_mix, CB, "w_out_wgrad")
    gw_cb = _grad_tn(y_a, d_pa, CB, "w_conv_branch_wgrad")
    gw_lb = _grad_tn(y_b, d_pb, CB, "w_lru_branch_wgrad")
    d_conv, gc_short = _conv_mixer_bwd(proj, d_ya, c_short)
    d_lru, gw_a, gw_x, g_lru_small = _lru_bwd(proj, hl, d_yb, c_lru, lru_conv_b, g_wa, b_a, g_wx, b_x, lru_lambda)
    gw_in = _in_proj_wgrad(h, d_conv, d_lru, d_gate)
    dx, dg1 = _in_proj_xgrad(d_conv, d_lru, d_gate, g_in, x2, dx1, norm_mix_pre)

    grads = [gw_in, gw_cb, gw_lb, gw_out, gw_a, gw_x, gw_up, gw_down]
    shard_shapes = [(D_MODEL, shard_in), (shard_sq, D_MODEL), (shard_sq, D_MODEL), (shard_sq, D_MODEL),
                    (N_HEADS, shard_head, HEAD_DIM), (N_HEADS, shard_head, HEAD_DIM), (D_MODEL, shard_up),
                    (shard_down, D_MODEL)]
    names = ["w_in", "w_conv_branch", "w_lru_branch", "w_out", "lru_wa", "lru_wx", "ffn_w_up", "ffn_w_down"]
    n = len(grads)
    by_cols = [True, False, False, False, False, False, True, False]
    pos = jnp.stack([xi, yi, ci]).astype(jnp.int32)
    got = _exchange_pair(grads, blocks, shard_shapes)
    sums = [_add_pair(grads[a], got[a], by_cols[a], pos, "pair_sum_" + names[a]) for a in range(n)]
    others = _exchange_chips(sums)
    moments = {"w_in": (m_w_in, v_w_in), "w_conv_branch": (m_w_conv_branch, v_w_conv_branch),
               "w_lru_branch": (m_w_lru_branch, v_w_lru_branch), "w_out": (m_w_out, v_w_out),
               "lru_wa": (m_lru_wa, v_lru_wa), "lru_wx": (m_lru_wx, v_lru_wx), "ffn_w_up": (m_ffn_w_up, v_ffn_w_up),
               "ffn_w_down": (m_ffn_w_down, v_ffn_w_down)}
    weights = {"w_in": w_in, "w_conv_branch": w_conv_branch, "w_lru_branch": w_lru_branch, "w_out": w_out,
               "lru_wa": lru_wa, "lru_wx": lru_wx, "ffn_w_up": ffn_w_up, "ffn_w_down": ffn_w_down}
    out_g, out_d, out_m, out_v = {}, {}, {}, {}
    for a, name in enumerate(names):
        out_g[name], out_d[name], out_m[name], out_v[name] = _adam_large(
            weights[name], *moments[name], sums[a], others[a], "adam_" + name)

    pieces = [dg1, dg2, dg3, dg4, g_lru_small[4:5], g_lru_small[7:8], gb_ffn, gc_short, g_lru_small[0:4],
              g_lru_small[5:6], g_lru_small[6:7], gc_ffn, loss_part]
    total = _allreduce_small(_pack_rows(pieces))
    sizes = [p.size for p in pieces]
    starts = _packed_starts(sizes)

    def piece(i, shape):
        return total[starts[i]:starts[i + 1]].reshape(-1)[:sizes[i]].reshape(shape)

    loss = total[starts[12], 0]

    def col_shard(full, width):
        return lax.dynamic_slice_in_dim(full, me * width, width, axis=1)

    def head_shard(full):
        return lax.dynamic_slice_in_dim(full.reshape(N_HEADS, HEAD_DIM), me * shard_head, shard_head, axis=1)

    small_names = ["norm_mix_pre", "norm_mix_post", "norm_ffn_pre", "norm_ffn_post", "lru_conv_b", "lru_lambda",
                   "ffn_conv_b", "conv_short_w", "lru_conv_w", "lru_ba", "lru_bx", "ffn_conv_w"]
    small_g = [piece(0, (1, D_MODEL)), piece(1, (1, D_MODEL)), piece(2, (1, D_MODEL)), piece(3, (1, D_MODEL)),
               piece(4, (1, D_MODEL)), piece(5, (1, D_MODEL)), piece(6, (1, 2 * D_FF)),
               col_shard(piece(7, (3, D_MODEL)), LANES), col_shard(piece(8, (4, D_MODEL)), LANES),
               head_shard(piece(9, (1, D_MODEL))), head_shard(piece(10, (1, D_MODEL))),
               col_shard(piece(11, (3, 2 * D_FF)), shard_up)]
    small_w = [norm_mix_pre, norm_mix_post, norm_ffn_pre, norm_ffn_post, lru_conv_b, lru_lambda, ffn_conv_b,
               conv_short_w[0], lru_conv_w[0], lru_ba[0], lru_bx[0], ffn_conv_w[0]]
    small_m = [m_norm_mix_pre, m_norm_mix_post, m_norm_ffn_pre, m_norm_ffn_post, m_lru_conv_b, m_lru_lambda,
               m_ffn_conv_b, m_conv_short_w[0], m_lru_conv_w[0], m_lru_ba[0], m_lru_bx[0], m_ffn_conv_w[0]]
    small_v = [v_norm_mix_pre, v_norm_mix_post, v_norm_ffn_pre, v_norm_ffn_post, v_lru_conv_b, v_lru_lambda,
               v_ffn_conv_b, v_conv_short_w[0], v_lru_conv_w[0], v_lru_ba[0], v_lru_bx[0], v_ffn_conv_w[0]]
    s_d, s_m, s_v = _adam_small(small_w, small_g, small_m, small_v)
    for i, name in enumerate(small_names):
        shape = small_w[i].shape if i < 7 else (1,) + small_w[i].shape
        out_g[name] = small_g[i].reshape(shape)
        out_d[name], out_m[name], out_v[name] = s_d[i].reshape(shape), s_m[i].reshape(shape), s_v[i].reshape(shape)

    order = ["norm_mix_pre", "norm_mix_post", "norm_ffn_pre", "norm_ffn_post", "w_in", "conv_short_w", "w_conv_branch",
             "lru_conv_w", "lru_conv_b", "lru_wa", "lru_ba", "lru_wx", "lru_bx", "lru_lambda", "w_lru_branch", "w_out",
             "ffn_w_up", "ffn_conv_w", "ffn_conv_b", "ffn_w_down"]
    return (loss, dx.reshape(1, t, D_MODEL), *[out_g[k] for k in order], *[out_d[k] for k in order],
            *[out_m[k] for k in order], *[out_v[k] for k in order])
```

```python
import functools
import math

import jax
import jax.numpy as jnp
from jax import lax
from jax.experimental import pallas as pl
from jax.experimental.pallas import tpu as pltpu

F32 = jnp.float32
BF16 = jnp.bfloat16
MESH = pl.DeviceIdType.MESH

N_DEV = 8
D_MODEL = 1024
N_HEADS = 4
HEAD_DIM = D_MODEL // N_HEADS
D_FF = 3 * D_MODEL
IN_COLS = 7 * D_MODEL
LRU_C = 8.0
RMS_EPS = 1e-6
ADAM_LR = 0.001
ADAM_B1 = 0.9
ADAM_B2 = 0.999
ADAM_EPS = 1e-08
ADAM_WD = 0.01
ADAM_STEP = 10
GELU_K = math.sqrt(2.0 / math.pi)
GELU_C = 0.044715

LANES = 128
SUBLANES = 8
PAD = SUBLANES
VMEM_LIMIT = 56 * 1024 * 1024
CB = 256

HBM_SPEC = pl.BlockSpec(memory_space=pltpu.HBM)
SEM_SPEC = pl.BlockSpec(memory_space=pltpu.SEMAPHORE)
DATAFLOW_EFFECT = pltpu.SideEffectType.DATAFLOW_SIDE_EFFECTING
VMEM_SPEC = pl.BlockSpec(memory_space=pltpu.VMEM)


def _params(*sem):
    if sem:
        return pltpu.CompilerParams(dimension_semantics=sem, vmem_limit_bytes=VMEM_LIMIT)
    return pltpu.CompilerParams(vmem_limit_bytes=VMEM_LIMIT)


def _row_chunk(t):
    return min(256, t)


def _row_block(rows, cap):
    return next(rb for rb in range(min(cap, rows), 0, -16) if rows % rb == 0)


def _gelu(x):
    return 0.5 * x * (1.0 + jnp.tanh(GELU_K * (x + GELU_C * x * x * x)))


def _gelu_and_grad(x):
    t = jnp.tanh(GELU_K * (x + GELU_C * x * x * x))
    g = 0.5 * x * (1.0 + t)
    dg = 0.5 * (1.0 + t) + 0.5 * x * (1.0 - t * t) * GELU_K * (1.0 + 3.0 * GELU_C * x * x)
    return g, dg


def _expm1_neg(x):
    series = x * (1.0 + x * (0.5 + x * (1.0 / 6.0 + x * (1.0 / 24.0 + x * (1.0 / 120.0)))))
    return jnp.where(x > -0.05, series, jnp.exp(x) - 1.0)


def _log_sigmoid(x):
    return jnp.minimum(x, 0.0) - jnp.log1p(jnp.exp(-jnp.abs(x)))


def _dot(a, b):
    return jnp.dot(a, b, preferred_element_type=F32)


def _dot_nt(a, b):
    return lax.dot_general(a, b, (((1,), (1,)), ((), ())), preferred_element_type=F32)


def _dot_tn(a, b):
    return lax.dot_general(a, b, (((0,), (0,)), ((), ())), preferred_element_type=F32)


def _rms_fwd(x):
    r = lax.rsqrt(jnp.mean(x * x, axis=-1, keepdims=True) + RMS_EPS)
    return x * r, r


def _rms_bwd(n, r, gdy):
    return r * (gdy - n * jnp.mean(n * gdy, axis=-1, keepdims=True))


def _conv_causal(pad_ref, w, r0, rows, taps):
    acc = None
    for k in range(taps):
        term = w[k:k + 1, :] * pad_ref[pl.ds(PAD + r0 - (taps - 1 - k), rows), :]
        acc = term if acc is None else acc + term
    return acc


def _conv_anticausal(pad_ref, w, r0, rows, taps):
    acc = None
    for k in range(taps):
        term = w[k:k + 1, :] * pad_ref[pl.ds(r0 + (taps - 1 - k), rows), :]
        acc = term if acc is None else acc + term
    return acc


def _conv_wgrad(g, xpad_ref, r0, rows, taps):
    return [jnp.sum(g * xpad_ref[pl.ds(PAD + r0 - (taps - 1 - k), rows), :], axis=0, keepdims=True)
            for k in range(taps)]


def _position():
    return lax.axis_index("x"), lax.axis_index("y"), lax.axis_index("c")


def _block_of(x, y, c):
    return 4 * x + 2 * y + c


def _chip(x, y, k):
    return (x + (k & 1)) % 2, (y + (k >> 1)) % 2


def _cols(width):
    def at(ref, d):
        return ref.at[:, pl.ds(pl.multiple_of(d * width, LANES), width)]
    return at


def _rows(height):
    def at(ref, d):
        return ref.at[pl.ds(pl.multiple_of(d * height, 16), height), :]
    return at


def _lead(ref, d):
    return ref.at[d]


def _gather_weights(shards, blocks, full_shapes, small):
    n = len(shards)
    small_rows = small.shape[0]

    def body(*refs):
        ins, small_in = refs[:n], refs[n]
        outs, small_out = refs[n + 1:2 * n + 1], refs[2 * n + 1]
        stage = refs[2 * n + 2:3 * n + 2]
        send, recv, local = refs[3 * n + 2:]
        x, y, c = _position()
        me = _block_of(x, y, c)
        sibling = (x, y, 1 - c)

        for a in range(n):
            stage[a][...] = ins[a][...].astype(BF16)

        def copy(a, k, block, to, src=None):
            dst = blocks[a](outs[a], block)
            return pltpu.make_async_remote_copy(
                src_ref=dst if src is None else src, dst_ref=dst, send_sem=send.at[a, k], recv_sem=recv.at[a, k],
                device_id=to, device_id_type=MESH)

        def small_copy(k):
            px, py, pc = (x + (k & 1)) % 2, (y + ((k >> 1) & 1)) % 2, (c + (k >> 2)) % 2
            return pltpu.make_async_remote_copy(
                src_ref=small_in, dst_ref=small_out.at[me], send_sem=send.at[n, k - 1], recv_sem=recv.at[n, k - 1],
                device_id=(px, py, pc), device_id_type=MESH)

        def small_arrival(k):
            px, py, pc = (x + (k & 1)) % 2, (y + ((k >> 1) & 1)) % 2, (c + (k >> 2)) % 2
            return pltpu.make_async_remote_copy(
                src_ref=small_in, dst_ref=small_out.at[_block_of(px, py, pc)], send_sem=send.at[n, k - 1],
                recv_sem=recv.at[n, k - 1], device_id=(px, py, pc), device_id_type=MESH)

        small_out[me] = small_in[...]
        small_sends = [small_copy(k) for k in range(1, N_DEV)]
        for cp in small_sends:
            cp.start()

        mine, first, passed = [], [], []
        for a in range(n):
            own = pltpu.make_async_copy(stage[a], blocks[a](outs[a], me), local.at[a])
            own.start()
            mine.append(own)
            sends = [copy(a, 0, me, sibling, src=stage[a])]
            sends += [copy(a, k, me, (*_chip(x, y, k), c), src=stage[a]) for k in (1, 2, 3)]
            for cp in sends:
                cp.start()
            first += sends
        for a in range(n):
            for k in (1, 2, 3):
                landed = _block_of(*_chip(x, y, k), c)
                copy(a, k, landed, (x, y, c)).wait_recv()
                fwd = copy(a, 3 + k, landed, sibling)
                fwd.start()
                passed.append(fwd)
        for a in range(n):
            copy(a, 0, _block_of(x, y, 1 - c), (x, y, c)).wait_recv()
            for k in (1, 2, 3):
                copy(a, 3 + k, _block_of(*_chip(x, y, k), 1 - c), (x, y, c)).wait_recv()
        for k in range(1, N_DEV):
            small_arrival(k).wait_recv()
        for cp in first + passed + small_sends:
            cp.wait_send()
        for own in mine:
            own.wait()

    out_shape = [jax.ShapeDtypeStruct(s, BF16) for s in full_shapes]
    out_shape.append(jax.ShapeDtypeStruct((N_DEV, small_rows, LANES), F32))
    return pl.pallas_call(
        body, name="gather_weights", out_shape=out_shape,
        in_specs=[VMEM_SPEC] * (n + 1), out_specs=[HBM_SPEC] * n + [VMEM_SPEC],
        scratch_shapes=[pltpu.VMEM(s.shape, BF16) for s in shards]
        + [pltpu.SemaphoreType.DMA((n + 1, 7)), pltpu.SemaphoreType.DMA((n + 1, 7)), pltpu.SemaphoreType.DMA((n,))],
        compiler_params=_params(),
    )(*shards, small)


def _exchange_pair(grads, blocks, shard_shapes, name):
    n = len(grads)

    def body(*refs):
        ins, got = refs[:n], refs[n:2 * n]
        send, recv = refs[2 * n:]
        x, y, c = _position()
        copies = []
        for a in range(n):
            for k in range(4):
                cp = pltpu.make_async_remote_copy(
                    src_ref=blocks[a](ins[a], _block_of(*_chip(x, y, k), 1 - c)), dst_ref=got[a].at[k],
                    send_sem=send.at[a, k], recv_sem=recv.at[a, k], device_id=(x, y, 1 - c), device_id_type=MESH)
                cp.start()
                copies.append(cp)
        for cp in copies:
            cp.wait()

    return pl.pallas_call(
        body, name=name, out_shape=[jax.ShapeDtypeStruct((4,) + tuple(s), BF16) for s in shard_shapes],
        in_specs=[HBM_SPEC] * n, out_specs=[HBM_SPEC] * n,
        scratch_shapes=[pltpu.SemaphoreType.DMA((n, 4)), pltpu.SemaphoreType.DMA((n, 4))],
        compiler_params=_params(),
    )(*grads)


def _chip_copies(sums, lands, send, recv):
    x, y, c = _position()
    return [pltpu.make_async_remote_copy(
        src_ref=sums[a].at[k], dst_ref=lands[a].at[k - 1], send_sem=send[3 * a + k - 1], recv_sem=recv[3 * a + k - 1],
        device_id=(*_chip(x, y, k), c), device_id_type=MESH) for a in range(len(sums)) for k in (1, 2, 3)]


def _exchange_chips_start(pair_sums, name):
    n = len(pair_sums)
    lands = [pltpu.with_memory_space_constraint(lax.empty((3,) + tuple(p.shape[1:]), BF16), pltpu.HBM) for p in pair_sums]

    def body(*refs):
        sums, zones = refs[:n], refs[n:2 * n]
        send, recv = refs[2 * n:5 * n], refs[5 * n:8 * n]
        token = refs[-1]
        for cp in _chip_copies(sums, zones, send, recv):
            cp.start()
        token[...] = jnp.zeros_like(token)

    outs = pl.pallas_call(
        body, name=name,
        out_shape=(*[pltpu.SemaphoreType.DMA(())] * (6 * n),
                   *[pltpu.HBM(p.shape, BF16) for p in pair_sums], *[pltpu.HBM(z.shape, BF16) for z in lands],
                   jax.ShapeDtypeStruct((SUBLANES, LANES), F32)),
        in_specs=[HBM_SPEC] * (2 * n), out_specs=(*[SEM_SPEC] * (6 * n), *[HBM_SPEC] * (2 * n), VMEM_SPEC),
        input_output_aliases={i: 6 * n + i for i in range(2 * n)},
        compiler_params=pltpu.CompilerParams(has_side_effects=DATAFLOW_EFFECT),
    )(*[pltpu.with_memory_space_constraint(p, pltpu.HBM) for p in pair_sums], *lands)
    return outs[:3 * n], outs[3 * n:6 * n], outs[6 * n:7 * n], outs[7 * n:8 * n], outs[-1]


def _exchange_chips_wait(send, recv, sums, lands, after, name):
    n = len(sums)

    def body(*refs):
        sums_in, zones = refs[:n], refs[n:2 * n]
        send_in, recv_in = refs[2 * n:5 * n], refs[5 * n:8 * n]
        for cp in _chip_copies(sums_in, zones, send_in, recv_in):
            cp.wait_send()
            cp.wait_recv()

    outs = pl.pallas_call(
        body, name=name,
        out_shape=(*[pltpu.HBM(p.shape, BF16) for p in sums], *[pltpu.HBM(z.shape, BF16) for z in lands]),
        in_specs=[HBM_SPEC] * (2 * n) + [SEM_SPEC] * (6 * n) + [pl.BlockSpec(memory_space=pl.ANY)],
        out_specs=[HBM_SPEC] * (2 * n), input_output_aliases={i: i for i in range(2 * n)},
        compiler_params=pltpu.CompilerParams(has_side_effects=DATAFLOW_EFFECT),
    )(*sums, *lands, *send, *recv, after)
    return outs[:n], outs[n:]


def _allreduce_small(part):
    rows = part.shape[0]

    def body(in_ref, out_ref, buf, send, recv):
        x, y, c = _position()
        me = _block_of(x, y, c)

        def peer(k):
            return (x + (k & 1)) % 2, (y + ((k >> 1) & 1)) % 2, (c + (k >> 2)) % 2

        sends = []
        for k in range(1, N_DEV):
            cp = pltpu.make_async_remote_copy(src_ref=in_ref, dst_ref=buf.at[me], send_sem=send.at[k - 1],
                                              recv_sem=recv.at[k - 1], device_id=peer(k), device_id_type=MESH)
            cp.start()
            sends.append(cp)
        buf[me] = in_ref[...]
        for k in range(1, N_DEV):
            pltpu.make_async_remote_copy(src_ref=in_ref, dst_ref=buf.at[_block_of(*peer(k))], send_sem=send.at[k - 1],
                                         recv_sem=recv.at[k - 1], device_id=peer(k), device_id_type=MESH).wait_recv()
        total = buf[0]
        for d in range(1, N_DEV):
            total = total + buf[d]
        out_ref[...] = total
        for cp in sends:
            cp.wait_send()

    return pl.pallas_call(
        body, name="allreduce_small", out_shape=jax.ShapeDtypeStruct(part.shape, F32),
        in_specs=[VMEM_SPEC], out_specs=VMEM_SPEC,
        scratch_shapes=[pltpu.VMEM((N_DEV, rows, LANES), F32), pltpu.SemaphoreType.DMA((7,)), pltpu.SemaphoreType.DMA((7,))],
        compiler_params=_params(),
    )(part)


def _in_proj(x, g1, w_in):
    t = x.shape[0]
    tm, bn = min(512, t), 1024

    def body(x_ref, g_ref, w_ref, proj_ref, h_ref, h_s):
        @pl.when(pl.program_id(1) == 0)
        def _():
            n, _ = _rms_fwd(x_ref[...])
            h_s[...] = (n * g_ref[...]).astype(BF16)
            h_ref[...] = h_s[...]
        proj_ref[...] = _dot(h_s[...], w_ref[...]).astype(BF16)

    return pl.pallas_call(
        body, name="in_proj", grid=(t // tm, IN_COLS // bn),
        out_shape=[jax.ShapeDtypeStruct((t, IN_COLS), BF16), jax.ShapeDtypeStruct((t, D_MODEL), BF16)],
        in_specs=[pl.BlockSpec((tm, D_MODEL), lambda i, j: (i, 0)), pl.BlockSpec((1, D_MODEL), lambda i, j: (0, 0)),
                  pl.BlockSpec((D_MODEL, bn), lambda i, j: (0, j))],
        out_specs=[pl.BlockSpec((tm, bn), lambda i, j: (i, j)), pl.BlockSpec((tm, D_MODEL), lambda i, j: (i, 0))],
        scratch_shapes=[pltpu.VMEM((tm, D_MODEL), BF16)],
        compiler_params=_params("parallel", "arbitrary"),
    )(x, g1, w_in)


def _section(s, t):
    return pl.BlockSpec((t, CB), lambda h, s=s: (0, s * (D_MODEL // CB) + h))


def _conv_mixer_fwd(proj, w_short):
    t = proj.shape[0]
    rc = _row_chunk(t)

    def body(b_ref, c_ref, x_ref, w_ref, y_ref, pad):
        pad[pl.ds(0, PAD), :] = jnp.zeros((PAD, CB), F32)
        for r0 in range(0, t, rc):
            rows = pl.ds(r0, rc)
            pad[pl.ds(PAD + r0, rc), :] = c_ref[rows, :].astype(F32) * x_ref[rows, :].astype(F32)
        w = w_ref[...]
        for r0 in range(0, t, rc):
            rows = pl.ds(r0, rc)
            y_ref[rows, :] = (b_ref[rows, :].astype(F32) * _conv_causal(pad, w, r0, rc, 3)).astype(BF16)

    return pl.pallas_call(
        body, name="conv_mixer_fwd", grid=(D_MODEL // CB,),
        out_shape=jax.ShapeDtypeStruct((t, D_MODEL), BF16),
        in_specs=[_section(0, t), _section(1, t), _section(2, t), pl.BlockSpec((3, CB), lambda h: (0, h))],
        out_specs=pl.BlockSpec((t, CB), lambda h: (0, h)),
        scratch_shapes=[pltpu.VMEM((t + PAD, CB), F32)],
        compiler_params=_params("parallel"),
    )(proj, proj, proj, w_short)


def _lru_gates(xl, wa, ba, wx, bx, ls, first_row):
    xb = xl.astype(BF16)
    ra = jax.nn.sigmoid(_dot(xb, wa) + ba)
    ia = jax.nn.sigmoid(_dot(xb, wx) + bx)
    la = LRU_C * ra * ls
    a = jnp.exp(la)
    one_minus = -_expm1_neg(2.0 * la)
    mult = jnp.where(first_row, 1.0, jnp.sqrt(one_minus))
    return xb, ra, ia, a, one_minus, mult


def _head_specs():
    vec = pl.BlockSpec((1, CB), lambda h: (0, h))
    mat = pl.BlockSpec((N_DEV, None, HEAD_DIM // N_DEV, HEAD_DIM), lambda h: (0, h, 0, 0))
    return vec, mat


def _lru_fwd(proj, w_conv, b_conv, wa, ba, wx, bx, lam):
    t = proj.shape[0]
    rc = _row_chunk(t)
    vec, mat = _head_specs()

    def body(lx_ref, ly_ref, wc_ref, bc_ref, wa_ref, ba_ref, wx_ref, bx_ref, lam_ref, yb_ref, hl_ref, pad, a_s, u_s):
        pad[pl.ds(0, PAD), :] = jnp.zeros((PAD, CB), F32)
        for r0 in range(0, t, rc):
            pad[pl.ds(PAD + r0, rc), :] = lx_ref[pl.ds(r0, rc), :].astype(F32)
        wc, bc = wc_ref[...], bc_ref[...]
        wa_m, wx_m = wa_ref[...].reshape(HEAD_DIM, HEAD_DIM), wx_ref[...].reshape(HEAD_DIM, HEAD_DIM)
        ls = _log_sigmoid(lam_ref[...])
        for r0 in range(0, t, rc):
            xl = _conv_causal(pad, wc, r0, rc, 4) + bc
            first = (lax.broadcasted_iota(jnp.int32, (rc, CB), 0) + r0) == 0
            _, _, ia, a, _, mult = _lru_gates(xl, wa_m, ba_ref[...], wx_m, bx_ref[...], ls, first)
            a_s[pl.ds(r0, rc), :] = a
            u_s[pl.ds(r0, rc), :] = mult * (ia * xl)

        row = lax.broadcasted_iota(jnp.int32, (SUBLANES, CB), 0)

        def group(g, carry):
            r = pl.multiple_of(g * SUBLANES, SUBLANES)
            a_g, b_g = a_s[pl.ds(r, SUBLANES), :], u_s[pl.ds(r, SUBLANES), :]
            for s in (1, 2, 4):
                keep = row >= s
                b_g = jnp.where(keep, a_g * pltpu.roll(b_g, s, 0) + b_g, b_g)
                a_g = jnp.where(keep, a_g * pltpu.roll(a_g, s, 0), a_g)
            h_g = b_g + a_g * carry
            hl_ref[pl.ds(r, SUBLANES), :] = h_g
            return jnp.broadcast_to(h_g[SUBLANES - 1:SUBLANES, :], (SUBLANES, CB))

        lax.fori_loop(0, t // SUBLANES, group, jnp.zeros((SUBLANES, CB), F32))
        for r0 in range(0, t, rc):
            rows = pl.ds(r0, rc)
            yb_ref[rows, :] = (hl_ref[rows, :] * _gelu(ly_ref[rows, :].astype(F32))).astype(BF16)

    blk = pl.BlockSpec((t, CB), lambda h: (0, h))
    return pl.pallas_call(
        body, name="lru_fwd", grid=(N_HEADS,),
        out_shape=[jax.ShapeDtypeStruct((t, D_MODEL), BF16), jax.ShapeDtypeStruct((t, D_MODEL), F32)],
        in_specs=[_section(3, t), _section(4, t), pl.BlockSpec((4, CB), lambda h: (0, h)), vec, mat, vec, mat, vec, vec],
        out_specs=[blk, blk],
        scratch_shapes=[pltpu.VMEM((t + PAD, CB), F32), pltpu.VMEM((t, CB), F32), pltpu.VMEM((t, CB), F32)],
        compiler_params=_params("parallel"),
    )(proj, proj, w_conv, b_conv, wa, ba, wx, bx, lam)


def _merge(y_a, y_b, proj, x, w_cb, w_lb, w_out, g2, g3):
    t = x.shape[0]
    tm = min(256, t)

    def body(ya_ref, yb_ref, gc_ref, gl_ref, x_ref, wcb_ref, wlb_ref, wo_ref, g2_ref, g3_ref,
             pa_ref, pb_ref, mg_ref, mix_ref, x1_ref, h2_ref):
        pa = _dot(ya_ref[...], wcb_ref[...]).astype(BF16)
        pb = _dot(yb_ref[...], wlb_ref[...]).astype(BF16)
        pa_ref[...] = pa
        pb_ref[...] = pb
        merged = (jax.nn.sigmoid(gc_ref[...].astype(F32)) * pa.astype(F32)
                  + jax.nn.sigmoid(gl_ref[...].astype(F32)) * pb.astype(F32)).astype(BF16)
        mg_ref[...] = merged
        mix = _dot(merged, wo_ref[...])
        mix_ref[...] = mix
        n2, _ = _rms_fwd(mix)
        x1 = x_ref[...] + n2 * g2_ref[...]
        x1_ref[...] = x1
        n3, _ = _rms_fwd(x1)
        h2_ref[...] = (n3 * g3_ref[...]).astype(BF16)

    row = pl.BlockSpec((tm, D_MODEL), lambda i: (i, 0))
    full = pl.BlockSpec((D_MODEL, D_MODEL), lambda i: (0, 0))
    vec = pl.BlockSpec((1, D_MODEL), lambda i: (0, 0))
    act = jax.ShapeDtypeStruct((t, D_MODEL), BF16)
    res = jax.ShapeDtypeStruct((t, D_MODEL), F32)
    return pl.pallas_call(
        body, name="merge_fwd", grid=(t // tm,), out_shape=[act, act, act, res, res, act],
        in_specs=[row, row, pl.BlockSpec((tm, D_MODEL), lambda i: (i, 5)), pl.BlockSpec((tm, D_MODEL), lambda i: (i, 6)),
                  row, full, full, full, vec, vec],
        out_specs=[row] * 6,
        compiler_params=_params("parallel"),
    )(y_a, y_b, proj, proj, x, w_cb, w_lb, w_out, g2, g3)


N_FF_BLOCKS = D_FF // CB


def _ffn_up(h2, w_up, w_conv, b_conv):
    t = h2.shape[0]
    rc = _row_chunk(t)

    def body(h_ref, wg_ref, wv_ref, cg_ref, cv_ref, bg_ref, bv_ref, ug_ref, uv_ref, f_ref, pad_g, pad_v):
        zeros = jnp.zeros((PAD, CB), F32)
        pad_g[pl.ds(0, PAD), :] = zeros
        pad_v[pl.ds(0, PAD), :] = zeros
        for r0 in range(0, t, rc):
            rows = pl.ds(r0, rc)
            ug = _dot(h_ref[rows, :], wg_ref[...]).astype(BF16)
            uv = _dot(h_ref[rows, :], wv_ref[...]).astype(BF16)
            ug_ref[rows, :] = ug
            uv_ref[rows, :] = uv
            pad_g[pl.ds(PAD + r0, rc), :] = ug.astype(F32)
            pad_v[pl.ds(PAD + r0, rc), :] = uv.astype(F32)
        cg, cv = cg_ref[...], cv_ref[...]
        for r0 in range(0, t, rc):
            gate = _conv_causal(pad_g, cg, r0, rc, 3) + bg_ref[...]
            val = _conv_causal(pad_v, cv, r0, rc, 3) + bv_ref[...]
            f_ref[pl.ds(r0, rc), :] = (_gelu(gate) * val).astype(BF16)

    nb = N_FF_BLOCKS
    act = jax.ShapeDtypeStruct((t, D_FF), BF16)
    blk = pl.BlockSpec((t, CB), lambda j: (0, j))
    return pl.pallas_call(
        body, name="ffn_up_fwd", grid=(nb,), out_shape=[act, act, act],
        in_specs=[pl.BlockSpec((t, D_MODEL), lambda j: (0, 0)),
                  pl.BlockSpec((D_MODEL, CB), lambda j: (0, j)), pl.BlockSpec((D_MODEL, CB), lambda j: (0, nb + j)),
                  pl.BlockSpec((3, CB), lambda j: (0, j)), pl.BlockSpec((3, CB), lambda j: (0, nb + j)),
                  pl.BlockSpec((1, CB), lambda j: (0, j)), pl.BlockSpec((1, CB), lambda j: (0, nb + j))],
        out_specs=[blk, blk, blk],
        scratch_shapes=[pltpu.VMEM((t + PAD, CB), F32), pltpu.VMEM((t + PAD, CB), F32)],
        compiler_params=_params("parallel"),
    )(h2, w_up, w_up, w_conv, w_conv, b_conv, b_conv)


def _ffn_down(f, w_down, x1, target, g4):
    t = f.shape[0]
    tm = min(256, t)

    def body(f_ref, w_ref, x1_ref, tg_ref, g_ref, dy_ref, dout_ref, df_ref, dg_ref, loss_ref):
        @pl.when(pl.program_id(0) == 0)
        def _():
            dg_ref[...] = jnp.zeros_like(dg_ref)
            loss_ref[...] = jnp.zeros_like(loss_ref)
        out = _dot(f_ref[...], w_ref[...])
        n4, r4 = _rms_fwd(out)
        err = x1_ref[...] + n4 * g_ref[...] - tg_ref[...]
        loss_ref[...] += jnp.full(loss_ref.shape, 0.5 / D_MODEL, F32) * jnp.sum(err * err)
        dy = err * (1.0 / D_MODEL)
        dy_ref[...] = dy
        dg_ref[...] += jnp.sum(dy * n4, axis=0, keepdims=True)
        d_out = _rms_bwd(n4, r4, dy * g_ref[...]).astype(BF16)
        dout_ref[...] = d_out
        df_ref[...] = _dot_nt(d_out, w_ref[...]).astype(BF16)

    row = pl.BlockSpec((tm, D_MODEL), lambda i: (i, 0))
    wide = pl.BlockSpec((tm, D_FF), lambda i: (i, 0))
    vec = pl.BlockSpec((1, D_MODEL), lambda i: (0, 0))
    return pl.pallas_call(
        body, name="ffn_down_fwd_bwd", grid=(t // tm,),
        out_shape=[jax.ShapeDtypeStruct((t, D_MODEL), F32), jax.ShapeDtypeStruct((t, D_MODEL), BF16),
                   jax.ShapeDtypeStruct((t, D_FF), BF16), jax.ShapeDtypeStruct((1, D_MODEL), F32),
                   jax.ShapeDtypeStruct((SUBLANES, LANES), F32)],
        in_specs=[wide, pl.BlockSpec((D_FF, D_MODEL), lambda i: (0, 0)), row, row, vec],
        out_specs=[row, row, wide, vec, pl.BlockSpec((SUBLANES, LANES), lambda i: (0, 0))],
        compiler_params=_params("arbitrary"),
    )(f, w_down, x1, target, g4)


def _grad_tn(a, b, bm, name):
    t, m = a.shape
    n = b.shape[1]

    def body(a_ref, b_ref, o_ref):
        o_ref[...] = _dot_tn(a_ref[...], b_ref[...]).astype(BF16)

    return pl.pallas_call(
        body, name=name, grid=(m // bm,), out_shape=jax.ShapeDtypeStruct((m, n), BF16),
        in_specs=[pl.BlockSpec((t, bm), lambda i: (0, i)), pl.BlockSpec((t, n), lambda i: (0, 0))],
        out_specs=pl.BlockSpec((bm, n), lambda i: (i, 0)),
        compiler_params=_params("parallel"),
    )(a, b)


def _ffn_up_bwd(up_g, up_v, d_f, w_conv, b_conv, h2, w_up):
    t = h2.shape[0]
    rc = _row_chunk(t)
    nb = N_FF_BLOCKS

    def body(ug_ref, uv_ref, df_ref, cg_ref, cv_ref, bg_ref, bv_ref, h_ref, w_ref,
             dw_ref, dcw_ref, dcb_ref, dh_ref, d_up, small, pad_g, pad_v, back_g, back_v):
        j, k = pl.program_id(0), pl.program_id(1)

        @pl.when((j == 0) & (k == 0))
        def _():
            dh_ref[...] = jnp.zeros_like(dh_ref)

        @pl.when(k == 0)
        def _():
            zeros = jnp.zeros((PAD, CB), F32)
            pad_g[pl.ds(0, PAD), :] = zeros
            pad_v[pl.ds(0, PAD), :] = zeros
            back_g[pl.ds(t, PAD), :] = zeros
            back_v[pl.ds(t, PAD), :] = zeros
            for r0 in range(0, t, rc):
                pad_g[pl.ds(PAD + r0, rc), :] = ug_ref[pl.ds(r0, rc), :].astype(F32)
                pad_v[pl.ds(PAD + r0, rc), :] = uv_ref[pl.ds(r0, rc), :].astype(F32)
            cg, cv = cg_ref[...], cv_ref[...]
            for r0 in range(0, t, rc):
                rows = pl.ds(r0, rc)
                gate = _conv_causal(pad_g, cg, r0, rc, 3) + bg_ref[...]
                val = _conv_causal(pad_v, cv, r0, rc, 3) + bv_ref[...]
                act, d_act = _gelu_and_grad(gate)
                d_f = df_ref[rows, :].astype(F32)
                back_g[rows, :] = d_f * val * d_act
                back_v[rows, :] = d_f * act
            for which, (back, pad, cw) in enumerate(((back_g, pad_g, cg), (back_v, pad_v, cv))):
                taps = [jnp.zeros((1, CB), F32)] * 3
                bias = jnp.zeros((1, CB), F32)
                for r0 in range(0, t, rc):
                    rows = pl.ds(r0, rc)
                    d_up[which, rows, :] = _conv_anticausal(back, cw, r0, rc, 3).astype(BF16)
                    g = back[rows, :]
                    taps = [acc + new for acc, new in zip(taps, _conv_wgrad(g, pad, r0, rc, 3))]
                    bias = bias + jnp.sum(g, axis=0, keepdims=True)
                small[which] = jnp.concatenate(taps + [bias] + [jnp.zeros((SUBLANES - 4, CB), F32)], axis=0)

        for r0 in range(0, t, rc):
            rows = pl.ds(r0, rc)
            dh_ref[rows, :] += _dot_nt(d_up[k, rows, :], w_ref[...])
        dw_ref[...] = _dot_tn(h_ref[...], d_up[k]).astype(BF16)
        dcw_ref[...] = small[k, pl.ds(0, 3), :]
        dcb_ref[...] = small[k, pl.ds(3, 1), :]

    blk = pl.BlockSpec((t, CB), lambda j, k: (0, j))
    both = lambda rows: pl.BlockSpec((rows, CB), lambda j, k: (0, nb * k + j))
    gate = lambda rows: pl.BlockSpec((rows, CB), lambda j, k: (0, j))
    val = lambda rows: pl.BlockSpec((rows, CB), lambda j, k: (0, nb + j))
    return pl.pallas_call(
        body, name="ffn_up_bwd", grid=(nb, 2),
        out_shape=[jax.ShapeDtypeStruct((D_MODEL, 2 * D_FF), BF16), jax.ShapeDtypeStruct((3, 2 * D_FF), F32),
                   jax.ShapeDtypeStruct((1, 2 * D_FF), F32), jax.ShapeDtypeStruct((t, D_MODEL), F32)],
        in_specs=[blk, blk, blk, gate(3), val(3), gate(1), val(1),
                  pl.BlockSpec((t, D_MODEL), lambda j, k: (0, 0)), both(D_MODEL)],
        out_specs=[both(D_MODEL), both(3), both(1), pl.BlockSpec((t, D_MODEL), lambda j, k: (0, 0))],
        scratch_shapes=[pltpu.VMEM((2, t, CB), BF16), pltpu.VMEM((2, SUBLANES, CB), F32)]
        + [pltpu.VMEM((t + PAD, CB), F32)] * 4,
        compiler_params=_params("arbitrary", "arbitrary"),
    )(up_g, up_v, d_f, w_conv, w_conv, b_conv, b_conv, h2, w_up)


def _merge_bwd(dy, d_h2, x1, mix, g3, g2, w_out, w_cb, w_lb, pa, pb, proj):
    t = dy.shape[0]
    tm = min(256, t)

    def body(dy_ref, dh2_ref, x1_ref, mix_ref, g3_ref, g2_ref, wo_ref, wcb_ref, wlb_ref, pa_ref, pb_ref, gc_ref, gl_ref,
             dx1_ref, dmix_ref, dpa_ref, dpb_ref, dya_ref, dyb_ref, dgate_ref, dg3_ref, dg2_ref):
        @pl.when(pl.program_id(0) == 0)
        def _():
            dg3_ref[...] = jnp.zeros_like(dg3_ref)
            dg2_ref[...] = jnp.zeros_like(dg2_ref)
        n3, r3 = _rms_fwd(x1_ref[...])
        d_h2 = dh2_ref[...]
        dg3_ref[...] += jnp.sum(d_h2 * n3, axis=0, keepdims=True)
        dx1 = dy_ref[...] + _rms_bwd(n3, r3, d_h2 * g3_ref[...])
        dx1_ref[...] = dx1
        n2, r2 = _rms_fwd(mix_ref[...])
        dg2_ref[...] += jnp.sum(dx1 * n2, axis=0, keepdims=True)
        d_mix = _rms_bwd(n2, r2, dx1 * g2_ref[...]).astype(BF16)
        dmix_ref[...] = d_mix
        d_merged = _dot_nt(d_mix, wo_ref[...])
        sc = jax.nn.sigmoid(gc_ref[...].astype(F32))
        sl = jax.nn.sigmoid(gl_ref[...].astype(F32))
        d_pa = (d_merged * sc).astype(BF16)
        d_pb = (d_merged * sl).astype(BF16)
        dpa_ref[...] = d_pa
        dpb_ref[...] = d_pb
        dgate_ref[0] = (d_merged * pa_ref[...].astype(F32) * sc * (1.0 - sc)).astype(BF16)
        dgate_ref[1] = (d_merged * pb_ref[...].astype(F32) * sl * (1.0 - sl)).astype(BF16)
        dya_ref[...] = _dot_nt(d_pa, wcb_ref[...]).astype(BF16)
        dyb_ref[...] = _dot_nt(d_pb, wlb_ref[...]).astype(BF16)

    row = pl.BlockSpec((tm, D_MODEL), lambda i: (i, 0))
    full = pl.BlockSpec((D_MODEL, D_MODEL), lambda i: (0, 0))
    vec = pl.BlockSpec((1, D_MODEL), lambda i: (0, 0))
    act = jax.ShapeDtypeStruct((t, D_MODEL), BF16)
    small = jax.ShapeDtypeStruct((1, D_MODEL), F32)
    return pl.pallas_call(
        body, name="merge_bwd", grid=(t // tm,),
        out_shape=[jax.ShapeDtypeStruct((t, D_MODEL), F32), act, act, act, act, act,
                   jax.ShapeDtypeStruct((2, t, D_MODEL), BF16), small, small],
        in_specs=[row, row, row, row, vec, vec, full, full, full, row, row,
                  pl.BlockSpec((tm, D_MODEL), lambda i: (i, 5)), pl.BlockSpec((tm, D_MODEL), lambda i: (i, 6))],
        out_specs=[row] * 6 + [pl.BlockSpec((2, tm, D_MODEL), lambda i: (0, i, 0)), vec, vec],
        compiler_params=_params("arbitrary"),
    )(dy, d_h2, x1, mix, g3, g2, w_out, w_cb, w_lb, pa, pb, proj, proj)


def _conv_mixer_bwd(proj, d_ya, w_short):
    t = proj.shape[0]
    rc = _row_chunk(t)

    def body(b_ref, c_ref, x_ref, dy_ref, w_ref, d_ref, dw_ref, pad, back):
        pad[pl.ds(0, PAD), :] = jnp.zeros((PAD, CB), F32)
        back[pl.ds(t, PAD), :] = jnp.zeros((PAD, CB), F32)
        for r0 in range(0, t, rc):
            rows = pl.ds(r0, rc)
            pad[pl.ds(PAD + r0, rc), :] = c_ref[rows, :].astype(F32) * x_ref[rows, :].astype(F32)
        w = w_ref[...]
        for r0 in range(0, t, rc):
            rows = pl.ds(r0, rc)
            d_y = dy_ref[rows, :].astype(F32)
            d_ref[0, rows, :] = (d_y * _conv_causal(pad, w, r0, rc, 3)).astype(BF16)
            back[rows, :] = d_y * b_ref[rows, :].astype(F32)
        taps = [jnp.zeros((1, CB), F32)] * 3
        for r0 in range(0, t, rc):
            rows = pl.ds(r0, rc)
            d_u = _conv_anticausal(back, w, r0, rc, 3)
            d_ref[1, rows, :] = (d_u * x_ref[rows, :].astype(F32)).astype(BF16)
            d_ref[2, rows, :] = (d_u * c_ref[rows, :].astype(F32)).astype(BF16)
            taps = [acc + new for acc, new in zip(taps, _conv_wgrad(back[rows, :], pad, r0, rc, 3))]
        dw_ref[...] = jnp.concatenate(taps, axis=0)

    blk = pl.BlockSpec((t, CB), lambda h: (0, h))
    return pl.pallas_call(
        body, name="conv_mixer_bwd", grid=(D_MODEL // CB,),
        out_shape=[jax.ShapeDtypeStruct((3, t, D_MODEL), BF16), jax.ShapeDtypeStruct((3, D_MODEL), F32)],
        in_specs=[_section(0, t), _section(1, t), _section(2, t), blk, pl.BlockSpec((3, CB), lambda h: (0, h))],
        out_specs=[pl.BlockSpec((3, t, CB), lambda h: (0, 0, h)), pl.BlockSpec((3, CB), lambda h: (0, h))],
        scratch_shapes=[pltpu.VMEM((t + PAD, CB), F32), pltpu.VMEM((t + PAD, CB), F32)],
        compiler_params=_params("parallel"),
    )(proj, proj, proj, d_ya, w_short)


LRU_SMALL_ROWS = 8


def _lru_bwd(proj, hl, d_yb, w_conv, b_conv, wa, ba, wx, bx, lam):
    t = proj.shape[0]
    rc = _row_chunk(t)
    vec, mat = _head_specs()

    def body(lx_ref, ly_ref, hl_ref, dy_ref, wc_ref, bc_ref, wa_ref, ba_ref, wx_ref, bx_ref, lam_ref,
             d_ref, dwa_ref, dwx_ref, small_ref, pad, a_next, dh_s, h_prev, back, acc_a, acc_x):
        zeros = jnp.zeros((PAD, CB), F32)
        pad[pl.ds(0, PAD), :] = zeros
        h_prev[pl.ds(0, PAD), :] = zeros
        a_next[pl.ds(t, PAD), :] = zeros
        back[pl.ds(t, PAD), :] = zeros
        for r0 in range(0, t, rc):
            pad[pl.ds(PAD + r0, rc), :] = lx_ref[pl.ds(r0, rc), :].astype(F32)
            h_prev[pl.ds(PAD + r0, rc), :] = hl_ref[pl.ds(r0, rc), :]
        wc, bc = wc_ref[...], bc_ref[...]
        wa_m, wx_m = wa_ref[...].reshape(HEAD_DIM, HEAD_DIM), wx_ref[...].reshape(HEAD_DIM, HEAD_DIM)
        ls = _log_sigmoid(lam_ref[...])

        def gates(r0):
            xl = _conv_causal(pad, wc, r0, rc, 4) + bc
            first = (lax.broadcasted_iota(jnp.int32, (rc, CB), 0) + r0) == 0
            return (xl, first) + _lru_gates(xl, wa_m, ba_ref[...], wx_m, bx_ref[...], ls, first)

        for r0 in range(0, t, rc):
            rows = pl.ds(r0, rc)
            a = gates(r0)[5]
            a_next[pl.ds(PAD - 1 + r0, rc), :] = a
            act, d_act = _gelu_and_grad(ly_ref[rows, :].astype(F32))
            d_y = dy_ref[rows, :].astype(F32)
            dh_s[rows, :] = d_y * act
            d_ref[1, rows, :] = (d_y * hl_ref[rows, :] * d_act).astype(BF16)

        row = lax.broadcasted_iota(jnp.int32, (SUBLANES, CB), 0)
        groups = t // SUBLANES

        def group(i, carry):
            r = pl.multiple_of((groups - 1 - i) * SUBLANES, SUBLANES)
            a_g, b_g = a_next[pl.ds(PAD + r, SUBLANES), :], dh_s[pl.ds(r, SUBLANES), :]
            for s in (1, 2, 4):
                keep = row < SUBLANES - s
                b_g = jnp.where(keep, a_g * pltpu.roll(b_g, SUBLANES - s, 0) + b_g, b_g)
                a_g = jnp.where(keep, a_g * pltpu.roll(a_g, SUBLANES - s, 0), a_g)
            d_g = b_g + a_g * carry
            dh_s[pl.ds(r, SUBLANES), :] = d_g
            return jnp.broadcast_to(d_g[0:1, :], (SUBLANES, CB))

        lax.fori_loop(0, groups, group, jnp.zeros((SUBLANES, CB), F32))

        acc_a[...] = jnp.zeros_like(acc_a)
        acc_x[...] = jnp.zeros_like(acc_x)
        d_ba = d_bx = d_ls = jnp.zeros((1, CB), F32)
        for r0 in range(0, t, rc):
            rows = pl.ds(r0, rc)
            xl, first, xb, ra, ia, a, one_minus, mult = gates(r0)
            d_h = dh_s[rows, :]
            d_a = d_h * h_prev[pl.ds(PAD - 1 + r0, rc), :]
            d_mult = d_h * ia * xl
            d_ia = d_h * mult * xl
            d_xl = d_h * mult * ia
            d_mult_d_la = jnp.where(first, 0.0, (one_minus - 1.0) / mult)
            d_la = d_a * a + d_mult * d_mult_d_la
            d_ls = d_ls + jnp.sum(d_la * ra, axis=0, keepdims=True) * LRU_C
            d_za = d_la * (LRU_C * ls) * ra * (1.0 - ra)
            d_zx = d_ia * ia * (1.0 - ia)
            d_ba = d_ba + jnp.sum(d_za, axis=0, keepdims=True)
            d_bx = d_bx + jnp.sum(d_zx, axis=0, keepdims=True)
            d_za, d_zx = d_za.astype(BF16), d_zx.astype(BF16)
            acc_a[...] += _dot_tn(xb, d_za)
            acc_x[...] += _dot_tn(xb, d_zx)
            back[rows, :] = d_xl + _dot_nt(d_za, wa_m) + _dot_nt(d_zx, wx_m)
        taps = [jnp.zeros((1, CB), F32)] * 4
        d_bc = jnp.zeros((1, CB), F32)
        for r0 in range(0, t, rc):
            rows = pl.ds(r0, rc)
            d_ref[0, rows, :] = _conv_anticausal(back, wc, r0, rc, 4).astype(BF16)
            g = back[rows, :]
            taps = [acc + new for acc, new in zip(taps, _conv_wgrad(g, pad, r0, rc, 4))]
            d_bc = d_bc + jnp.sum(g, axis=0, keepdims=True)
        d_lam = d_ls * jax.nn.sigmoid(-lam_ref[...])
        small_ref[...] = jnp.concatenate(taps + [d_bc, d_ba, d_bx, d_lam], axis=0)
        dwa_ref[...] = acc_a[...].reshape(N_DEV, HEAD_DIM // N_DEV, HEAD_DIM).astype(BF16)
        dwx_ref[...] = acc_x[...].reshape(N_DEV, HEAD_DIM // N_DEV, HEAD_DIM).astype(BF16)

    blk = pl.BlockSpec((t, CB), lambda h: (0, h))
    gate_grad = jax.ShapeDtypeStruct((N_DEV, N_HEADS, HEAD_DIM // N_DEV, HEAD_DIM), BF16)
    return pl.pallas_call(
        body, name="lru_bwd", grid=(N_HEADS,),
        out_shape=[jax.ShapeDtypeStruct((2, t, D_MODEL), BF16), gate_grad, gate_grad,
                   jax.ShapeDtypeStruct((LRU_SMALL_ROWS, D_MODEL), F32)],
        in_specs=[_section(3, t), _section(4, t), blk, blk, pl.BlockSpec((4, CB), lambda h: (0, h)),
                  vec, mat, vec, mat, vec, vec],
        out_specs=[pl.BlockSpec((2, t, CB), lambda h: (0, 0, h)), mat, mat,
                   pl.BlockSpec((LRU_SMALL_ROWS, CB), lambda h: (0, h))],
        scratch_shapes=[pltpu.VMEM((t + PAD, CB), F32), pltpu.VMEM((t + PAD, CB), F32), pltpu.VMEM((t, CB), F32),
                        pltpu.VMEM((t + PAD, CB), F32), pltpu.VMEM((t + PAD, CB), F32),
                        pltpu.VMEM((HEAD_DIM, HEAD_DIM), F32), pltpu.VMEM((HEAD_DIM, HEAD_DIM), F32)],
        compiler_params=_params("parallel"),
    )(proj, proj, hl, d_yb, w_conv, b_conv, wa, ba, wx, bx, lam)


def _stack_maps(halves):
    def conv(sec, part):
        return jnp.minimum(sec, 2), jnp.where(sec < 3, part, halves - 1)

    def lru(sec, part):
        return jnp.clip(sec - 3, 0, 1), jnp.where(sec < 3, 0, jnp.where(sec < 5, part, halves - 1))

    def gate(sec, part):
        return jnp.clip(sec - 5, 0, 1), jnp.where(sec < 5, 0, part)

    return conv, lru, gate


def _pick_stack(sec, refs, fn):
    @pl.when(sec < 3)
    def _():
        fn(refs[0])

    @pl.when((sec >= 3) & (sec < 5))
    def _():
        fn(refs[1])

    @pl.when(sec >= 5)
    def _():
        fn(refs[2])


def _in_proj_wgrad(h, d_conv, d_lru, d_gate):
    t = h.shape[0]
    halves, bn = 2, D_MODEL // 2
    maps = _stack_maps(halves)

    def body(h_ref, dc_ref, dl_ref, dg_ref, o_ref):
        def emit(ref):
            o_ref[...] = _dot_tn(h_ref[...], ref[...]).astype(BF16)
        _pick_stack(pl.program_id(0) // halves, (dc_ref, dl_ref, dg_ref), emit)

    def spec(m):
        def index(s):
            stack, part = m(s // halves, s % halves)
            return stack, 0, part
        return pl.BlockSpec((None, t, bn), index)

    return pl.pallas_call(
        body, name="in_proj_wgrad", grid=(7 * halves,), out_shape=jax.ShapeDtypeStruct((D_MODEL, IN_COLS), BF16),
        in_specs=[pl.BlockSpec((t, D_MODEL), lambda s: (0, 0))] + [spec(m) for m in maps],
        out_specs=pl.BlockSpec((D_MODEL, bn), lambda s: (0, s)),
        compiler_params=_params("arbitrary"),
    )(h, d_conv, d_lru, d_gate)


def _in_proj_xgrad(d_conv, d_lru, d_gate, w_in, x, dx1, g1):
    t = x.shape[0]
    tm = min(512, t)
    maps = _stack_maps(1)

    def body(dc_ref, dl_ref, dg_ref, w_ref, x_ref, dx1_ref, g_ref, dx_ref, dgain_ref, acc):
        i, s = pl.program_id(0), pl.program_id(1)

        @pl.when((i == 0) & (s == 0))
        def _():
            dgain_ref[...] = jnp.zeros_like(dgain_ref)

        @pl.when(s == 0)
        def _():
            acc[...] = jnp.zeros_like(acc)

        def add(ref):
            acc[...] += _dot_nt(ref[...], w_ref[...])
        _pick_stack(s, (dc_ref, dl_ref, dg_ref), add)

        @pl.when(s == 6)
        def _():
            n1, r1 = _rms_fwd(x_ref[...])
            d_h = acc[...]
            dgain_ref[...] += jnp.sum(d_h * n1, axis=0, keepdims=True)
            dx_ref[...] = dx1_ref[...] + _rms_bwd(n1, r1, d_h * g_ref[...])

    def spec(m):
        def index(i, s):
            return m(s, 0)[0], i, 0
        return pl.BlockSpec((None, tm, D_MODEL), index)

    row = pl.BlockSpec((tm, D_MODEL), lambda i, s: (i, 0))
    vec = pl.BlockSpec((1, D_MODEL), lambda i, s: (0, 0))
    return pl.pallas_call(
        body, name="in_proj_xgrad", grid=(t // tm, 7),
        out_shape=[jax.ShapeDtypeStruct((t, D_MODEL), F32), jax.ShapeDtypeStruct((1, D_MODEL), F32)],
        in_specs=[spec(m) for m in maps] + [pl.BlockSpec((D_MODEL, D_MODEL), lambda i, s: (0, s)), row, row, vec],
        out_specs=[row, vec],
        scratch_shapes=[pltpu.VMEM((tm, D_MODEL), F32)],
        compiler_params=_params("arbitrary", "arbitrary"),
    )(d_conv, d_lru, d_gate, w_in, x, dx1, g1)


def _add_pair(grad, got, by_cols, pos, name):
    cols = got.shape[-1]
    got3 = got.reshape(4, -1, cols)
    rows = got3.shape[1]
    rb = _row_block(rows, 512)

    def block(k, p):
        return 4 * ((p[0] + k % 2) % 2) + 2 * ((p[1] + k // 2) % 2) + p[2]

    if by_cols:
        g_in, g_spec = grad, pl.BlockSpec((rb, cols), lambda k, i, p: (i, block(k, p)))
    else:
        g_in = grad.reshape(N_DEV, rows, cols)
        g_spec = pl.BlockSpec((None, rb, cols), lambda k, i, p: (block(k, p), i, 0))
    slot = pl.BlockSpec((None, rb, cols), lambda k, i, p: (k, i, 0))

    def body(pos_ref, a_ref, b_ref, o_ref):
        o_ref[...] = (a_ref[...].astype(F32) + b_ref[...].astype(F32)).astype(BF16)

    out = pl.pallas_call(
        body, name=name, out_shape=jax.ShapeDtypeStruct(got3.shape, BF16),
        grid_spec=pltpu.PrefetchScalarGridSpec(num_scalar_prefetch=1, grid=(4, rows // rb),
                                               in_specs=[g_spec, slot], out_specs=slot),
        compiler_params=_params("parallel", "parallel"),
    )(pos, g_in, got3)
    return out.reshape(got.shape)


def _adamw(w, g, m, v):
    m = ADAM_B1 * m + (1.0 - ADAM_B1) * g
    v = ADAM_B2 * v + (1.0 - ADAM_B2) * (g * g)
    m_hat = m / (1.0 - ADAM_B1 ** ADAM_STEP)
    v_hat = v / (1.0 - ADAM_B2 ** ADAM_STEP)
    return -ADAM_LR * (m_hat / (jnp.sqrt(v_hat) + ADAM_EPS) + ADAM_WD * w), m, v


def _adam_large(w, m, v, own, others, name):
    shape = w.shape
    cols = shape[-1]
    w2, m2, v2 = (a.reshape(-1, cols) for a in (w, m, v))
    rows = w2.shape[0]
    own, others = own.reshape(4, rows, cols), others.reshape(3, rows, cols)
    rb = _row_block(rows, 256)

    def body(w_ref, m_ref, v_ref, own_ref, oth_ref, g_ref, d_ref, nm_ref, nv_ref):
        g = own_ref[...].astype(F32)
        for k in range(3):
            g = g + oth_ref[k].astype(F32)
        g_ref[...] = g
        d_ref[...], nm_ref[...], nv_ref[...] = _adamw(w_ref[...], g, m_ref[...], v_ref[...])

    blk = pl.BlockSpec((rb, cols), lambda i: (i, 0))
    res = jax.ShapeDtypeStruct((rows, cols), F32)
    outs = pl.pallas_call(
        body, name=name, grid=(rows // rb,), out_shape=[res] * 4,
        in_specs=[blk, blk, blk, pl.BlockSpec((None, rb, cols), lambda i: (0, i, 0)),
                  pl.BlockSpec((3, rb, cols), lambda i: (0, i, 0))],
        out_specs=[blk] * 4, compiler_params=_params("parallel"),
    )(w2, m2, v2, own, others)
    return [o.reshape(shape) for o in outs]


def _adam_small(ws, gs, ms, vs):
    n = len(ws)

    def body(*refs):
        w_refs, g_refs, m_refs, v_refs = (refs[i * n:(i + 1) * n] for i in range(4))
        outs = refs[4 * n:]
        for i in range(n):
            d, m, v = _adamw(w_refs[i][...], g_refs[i][...], m_refs[i][...], v_refs[i][...])
            outs[i][...], outs[n + i][...], outs[2 * n + i][...] = d, m, v

    shapes = [jax.ShapeDtypeStruct(w.shape, F32) for w in ws]
    outs = pl.pallas_call(
        body, name="adam_small", out_shape=shapes * 3,
        in_specs=[VMEM_SPEC] * (4 * n), out_specs=[VMEM_SPEC] * (3 * n), compiler_params=_params(),
    )(*ws, *gs, *ms, *vs)
    return outs[:n], outs[n:2 * n], outs[2 * n:]


def _pack_rows(pieces):
    tile = SUBLANES * LANES
    return jnp.concatenate([jnp.pad(p.reshape(-1), (0, (-p.size) % tile)).reshape(-1, LANES) for p in pieces], axis=0)


def _packed_starts(sizes):
    tile = SUBLANES * LANES
    starts = [0]
    for s in sizes:
        starts.append(starts[-1] + (s + tile - 1) // tile * SUBLANES)
    return starts


def kernel(x, norm_mix_pre, norm_mix_post, norm_ffn_pre, norm_ffn_post, w_in, conv_short_w, w_conv_branch, lru_conv_w, lru_conv_b, lru_wa, lru_ba, lru_wx, lru_bx, lru_lambda, w_lru_branch, w_out, ffn_w_up, ffn_conv_w, ffn_conv_b, ffn_w_down, loss_target, m_norm_mix_pre, m_norm_mix_post, m_norm_ffn_pre, m_norm_ffn_post, m_w_in, m_conv_short_w, m_w_conv_branch, m_lru_conv_w, m_lru_conv_b, m_lru_wa, m_lru_ba, m_lru_wx, m_lru_bx, m_lru_lambda, m_w_lru_branch, m_w_out, m_ffn_w_up, m_ffn_conv_w, m_ffn_conv_b, m_ffn_w_down, v_norm_mix_pre, v_norm_mix_post, v_norm_ffn_pre, v_norm_ffn_post, v_w_in, v_conv_short_w, v_w_conv_branch, v_lru_conv_w, v_lru_conv_b, v_lru_wa, v_lru_ba, v_lru_wx, v_lru_bx, v_lru_lambda, v_w_lru_branch, v_w_out, v_ffn_w_up, v_ffn_conv_w, v_ffn_conv_b, v_ffn_w_down):
    t = x.shape[1]
    xi, yi, ci = _position()
    me = _block_of(xi, yi, ci)
    x2, target = x[0], loss_target[0]
    shard_in, shard_up = IN_COLS // N_DEV, 2 * D_FF // N_DEV
    shard_sq, shard_down, shard_head = D_MODEL // N_DEV, D_FF // N_DEV, HEAD_DIM // N_DEV

    large = [w_in[0], w_conv_branch[0], w_lru_branch[0], w_out[0], lru_wa[0], lru_wx[0], ffn_w_up[0], ffn_w_down[0]]
    blocks = [_cols(shard_in), _rows(shard_sq), _rows(shard_sq), _rows(shard_sq), _lead, _lead,
              _cols(shard_up), _rows(shard_down)]
    gate_full = (N_DEV, N_HEADS, shard_head, HEAD_DIM)
    full_shapes = [(D_MODEL, IN_COLS), (D_MODEL, D_MODEL), (D_MODEL, D_MODEL), (D_MODEL, D_MODEL), gate_full, gate_full,
                   (D_MODEL, 2 * D_FF), (D_FF, D_MODEL)]
    small_sharded = [conv_short_w, lru_conv_w, lru_ba, lru_bx, ffn_conv_w]
    small_mine = _pack_rows(small_sharded)
    small_at = _packed_starts([p.size for p in small_sharded])
    *gathered, small_all = _gather_weights(large, blocks, full_shapes, small_mine)
    g_in, g_cb, g_lb, g_out, g_wa, g_wx, g_up, g_down = gathered

    def cols_of(r0, n, width):
        part = small_all[:, r0:r0 + n * width // LANES, :].reshape(N_DEV, n, width)
        return part.transpose(1, 0, 2).reshape(n, N_DEV * width)

    c_short = cols_of(small_at[0], 3, LANES)
    c_lru = cols_of(small_at[1], 4, LANES)
    b_a = cols_of(small_at[2], N_HEADS, shard_head).reshape(1, D_MODEL)
    b_x = cols_of(small_at[3], N_HEADS, shard_head).reshape(1, D_MODEL)
    c_ffn = cols_of(small_at[4], 3, shard_up)

    proj, h = _in_proj(x2, norm_mix_pre, g_in)
    y_a = _conv_mixer_fwd(proj, c_short)
    y_b, hl = _lru_fwd(proj, c_lru, lru_conv_b, g_wa, b_a, g_wx, b_x, lru_lambda)
    pa, pb, merged, mix, x1, h2 = _merge(y_a, y_b, proj, x2, g_cb, g_lb, g_out, norm_mix_post, norm_ffn_pre)
    up_g, up_v, f = _ffn_up(h2, g_up, c_ffn, ffn_conv_b)
    dy, d_out, d_f, dg4, loss_part = _ffn_down(f, g_down, x1, target, norm_ffn_post)

    names = ["w_in", "w_conv_branch", "w_lru_branch", "w_out", "lru_wa", "lru_wx", "ffn_w_up", "ffn_w_down"]
    block_of = dict(zip(names, blocks))
    shard_shapes = {"w_in": (D_MODEL, shard_in), "w_conv_branch": (shard_sq, D_MODEL), "w_lru_branch": (shard_sq, D_MODEL),
                    "w_out": (shard_sq, D_MODEL), "lru_wa": (N_HEADS, shard_head, HEAD_DIM),
                    "lru_wx": (N_HEADS, shard_head, HEAD_DIM), "ffn_w_up": (D_MODEL, shard_up),
                    "ffn_w_down": (shard_down, D_MODEL)}
    pos = jnp.stack([xi, yi, ci]).astype(jnp.int32)

    def reduce_start(tag, grads):
        keys = list(grads)
        got = _exchange_pair([grads[k] for k in keys], [block_of[k] for k in keys], [shard_shapes[k] for k in keys],
                             "reduce_pair_exchange_" + tag)
        sums = [_add_pair(grads[k], g, k in ("w_in", "ffn_w_up"), pos, "pair_sum_" + k) for k, g in zip(keys, got)]
        return (keys,) + _exchange_chips_start(sums, "reduce_chip_start_" + tag)

    def behind(flight, operand):
        return operand + flight[-1][0:1, 0:1]

    gw_down = _grad_tn(f, d_out, min(512, D_FF), "ffn_down_wgrad")
    flight_down = reduce_start("down", {"ffn_w_down": gw_down})
    gw_up, gc_ffn, gb_ffn, d_h2 = _ffn_up_bwd(up_g, up_v, d_f, c_ffn, behind(flight_down, ffn_conv_b), h2, g_up)
    flight_up = reduce_start("up", {"ffn_w_up": gw_up})
    dx1, d_mix, d_pa, d_pb, d_ya, d_yb, d_gate, dg3, dg2 = _merge_bwd(
        dy, d_h2, x1, mix, behind(flight_up, norm_ffn_pre), norm_mix_post, g_out, g_cb, g_lb, pa, pb, proj)
    gw_out = _grad_tn(merged, d_mix, CB, "w_out_wgrad")
    gw_cb = _grad_tn(y_a, d_pa, CB, "w_conv_branch_wgrad")
    gw_lb = _grad_tn(y_b, d_pb, CB, "w_lru_branch_wgrad")
    flight_mix = reduce_start("mix", {"w_conv_branch": gw_cb, "w_lru_branch": gw_lb, "w_out": gw_out})
    d_conv, gc_short = _conv_mixer_bwd(proj, d_ya, behind(flight_mix, c_short))
    d_lru, gw_a, gw_x, g_lru_small = _lru_bwd(proj, hl, d_yb, c_lru, lru_conv_b, g_wa, b_a, g_wx, b_x, lru_lambda)
    gw_in = _in_proj_wgrad(h, d_conv, d_lru, d_gate)
    flight_in = reduce_start("in", {"lru_wa": gw_a, "lru_wx": gw_x, "w_in": gw_in})
    dx, dg1 = _in_proj_xgrad(d_conv, d_lru, d_gate, g_in, x2, dx1, behind(flight_in, norm_mix_pre))

    moments ={"w_in": (m_w_in, v_w_in), "w_conv_branch": (m_w_conv_branch, v_w_conv_branch),
               "w_lru_branch": (m_w_lru_branch, v_w_lru_branch), "w_out": (m_w_out, v_w_out),
               "lru_wa": (m_lru_wa, v_lru_wa), "lru_wx": (m_lru_wx, v_lru_wx), "ffn_w_up": (m_ffn_w_up, v_ffn_w_up),
               "ffn_w_down": (m_ffn_w_down, v_ffn_w_down)}
    weights = {"w_in": w_in, "w_conv_branch": w_conv_branch, "w_lru_branch": w_lru_branch, "w_out": w_out,
               "lru_wa": lru_wa, "lru_wx": lru_wx, "ffn_w_up": ffn_w_up, "ffn_w_down": ffn_w_down}
    out_g, out_d, out_m, out_v = {}, {}, {}, {}

    pieces = [dg1, dg2, dg3, dg4, g_lru_small[4:5], g_lru_small[7:8], gb_ffn, gc_short, g_lru_small[0:4],
              g_lru_small[5:6], g_lru_small[6:7], gc_ffn, loss_part]
    total = _allreduce_small(_pack_rows(pieces))
    sizes = [p.size for p in pieces]
    starts = _packed_starts(sizes)

    def piece(i, shape):
        return total[starts[i]:starts[i + 1]].reshape(-1)[:sizes[i]].reshape(shape)

    loss = total[starts[12], 0]

    def col_shard(full, width):
        return lax.dynamic_slice_in_dim(full, me * width, width, axis=1)

    def head_shard(full):
        return lax.dynamic_slice_in_dim(full.reshape(N_HEADS, HEAD_DIM), me * shard_head, shard_head, axis=1)

    small_names = ["norm_mix_pre", "norm_mix_post", "norm_ffn_pre", "norm_ffn_post", "lru_conv_b", "lru_lambda",
                   "ffn_conv_b", "conv_short_w", "lru_conv_w", "lru_ba", "lru_bx", "ffn_conv_w"]
    small_g = [piece(0, (1, D_MODEL)), piece(1, (1, D_MODEL)), piece(2, (1, D_MODEL)), piece(3, (1, D_MODEL)),
               piece(4, (1, D_MODEL)), piece(5, (1, D_MODEL)), piece(6, (1, 2 * D_FF)),
               col_shard(piece(7, (3, D_MODEL)), LANES), col_shard(piece(8, (4, D_MODEL)), LANES),
               head_shard(piece(9, (1, D_MODEL))), head_shard(piece(10, (1, D_MODEL))),
               col_shard(piece(11, (3, 2 * D_FF)), shard_up)]
    small_w = [norm_mix_pre, norm_mix_post, norm_ffn_pre, norm_ffn_post, lru_conv_b, lru_lambda, ffn_conv_b,
               conv_short_w[0], lru_conv_w[0], lru_ba[0], lru_bx[0], ffn_conv_w[0]]
    small_m = [m_norm_mix_pre, m_norm_mix_post, m_norm_ffn_pre, m_norm_ffn_post, m_lru_conv_b, m_lru_lambda,
               m_ffn_conv_b, m_conv_short_w[0], m_lru_conv_w[0], m_lru_ba[0], m_lru_bx[0], m_ffn_conv_w[0]]
    small_v = [v_norm_mix_pre, v_norm_mix_post, v_norm_ffn_pre, v_norm_ffn_post, v_lru_conv_b, v_lru_lambda,
               v_ffn_conv_b, v_conv_short_w[0], v_lru_conv_w[0], v_lru_ba[0], v_lru_bx[0], v_ffn_conv_w[0]]
    s_d, s_m, s_v = _adam_small(small_w, small_g, small_m, small_v)
    for i, name in enumerate(small_names):
        shape = small_w[i].shape if i < 7 else (1,) + small_w[i].shape
        out_g[name] = small_g[i].reshape(shape)
        out_d[name], out_m[name], out_v[name] = s_d[i].reshape(shape), s_m[i].reshape(shape), s_v[i].reshape(shape)

    after = s_d[0]
    for tag, (keys, send, recv, sums, lands, _) in (("down", flight_down), ("up", flight_up), ("mix", flight_mix),
                                                    ("in", flight_in)):
        sums, others = _exchange_chips_wait(send, recv, sums, lands, after, "reduce_chip_wait_" + tag)
        for k, own, oth in zip(keys, sums, others):
            out_g[k], out_d[k], out_m[k], out_v[k] = _adam_large(weights[k], *moments[k], own, oth, "adam_" + k)
        after = out_d[keys[-1]]

    order = ["norm_mix_pre", "norm_mix_post", "norm_ffn_pre", "norm_ffn_post", "w_in", "conv_short_w", "w_conv_branch",
             "lru_conv_w", "lru_conv_b", "lru_wa", "lru_ba", "lru_wx", "lru_bx", "lru_lambda", "w_lru_branch", "w_out",
             "ffn_w_up", "ffn_conv_w", "ffn_conv_b", "ffn_w_down"]
    return (loss, dx.reshape(1, t, D_MODEL), *[out_g[k] for k in order], *[out_d[k] for k in order],
            *[out_m[k] for k in order], *[out_v[k] for k in order])
```

```python
import functools
import math

import jax
import jax.numpy as jnp
from jax import lax
from jax.experimental import pallas as pl
from jax.experimental.pallas import tpu as pltpu

F32 = jnp.float32
BF16 = jnp.bfloat16
MESH = pl.DeviceIdType.MESH

N_DEV = 8
D_MODEL = 1024
N_HEADS = 4
HEAD_DIM = D_MODEL // N_HEADS
D_FF = 3 * D_MODEL
IN_COLS = 7 * D_MODEL
LRU_C = 8.0
RMS_EPS = 1e-6
ADAM_LR = 0.001
ADAM_B1 = 0.9
ADAM_B2 = 0.999
ADAM_EPS = 1e-08
ADAM_WD = 0.01
ADAM_STEP = 10
GELU_K = math.sqrt(2.0 / math.pi)
GELU_C = 0.044715

LANES = 128
SUBLANES = 8
PAD = SUBLANES
VMEM_LIMIT = 56 * 1024 * 1024
CB = 256

HBM_SPEC = pl.BlockSpec(memory_space=pltpu.HBM)
SEM_SPEC = pl.BlockSpec(memory_space=pltpu.SEMAPHORE)
DATAFLOW_EFFECT = pltpu.SideEffectType.DATAFLOW_SIDE_EFFECTING
VMEM_SPEC = pl.BlockSpec(memory_space=pltpu.VMEM)


def _params(*sem):
    if sem:
        return pltpu.CompilerParams(dimension_semantics=sem, vmem_limit_bytes=VMEM_LIMIT)
    return pltpu.CompilerParams(vmem_limit_bytes=VMEM_LIMIT)


def _row_chunk(t):
    return min(256, t)


def _row_block(rows, cap):
    return next(rb for rb in range(min(cap, rows), 0, -16) if rows % rb == 0)


def _gelu(x):
    return 0.5 * x * (1.0 + jnp.tanh(GELU_K * (x + GELU_C * x * x * x)))


def _gelu_and_grad(x):
    t = jnp.tanh(GELU_K * (x + GELU_C * x * x * x))
    g = 0.5 * x * (1.0 + t)
    dg = 0.5 * (1.0 + t) + 0.5 * x * (1.0 - t * t) * GELU_K * (1.0 + 3.0 * GELU_C * x * x)
    return g, dg


def _expm1_neg(x):
    series = x * (1.0 + x * (0.5 + x * (1.0 / 6.0 + x * (1.0 / 24.0 + x * (1.0 / 120.0)))))
    return jnp.where(x > -0.05, series, jnp.exp(x) - 1.0)


def _log_sigmoid(x):
    return jnp.minimum(x, 0.0) - jnp.log1p(jnp.exp(-jnp.abs(x)))


def _dot(a, b):
    return jnp.dot(a, b, preferred_element_type=F32)


def _dot_nt(a, b):
    return lax.dot_general(a, b, (((1,), (1,)), ((), ())), preferred_element_type=F32)


def _dot_tn(a, b):
    return lax.dot_general(a, b, (((0,), (0,)), ((), ())), preferred_element_type=F32)


def _rms_fwd(x):
    r = lax.rsqrt(jnp.mean(x * x, axis=-1, keepdims=True) + RMS_EPS)
    return x * r, r


def _rms_bwd(n, r, gdy):
    return r * (gdy - n * jnp.mean(n * gdy, axis=-1, keepdims=True))


def _conv_causal(pad_ref, w, r0, rows, taps):
    acc = None
    for k in range(taps):
        term = w[k:k + 1, :] * pad_ref[pl.ds(PAD + r0 - (taps - 1 - k), rows), :]
        acc = term if acc is None else acc + term
    return acc


def _conv_anticausal(pad_ref, w, r0, rows, taps):
    acc = None
    for k in range(taps):
        term = w[k:k + 1, :] * pad_ref[pl.ds(r0 + (taps - 1 - k), rows), :]
        acc = term if acc is None else acc + term
    return acc


def _conv_wgrad(g, xpad_ref, r0, rows, taps):
    return [jnp.sum(g * xpad_ref[pl.ds(PAD + r0 - (taps - 1 - k), rows), :], axis=0, keepdims=True)
            for k in range(taps)]


def _position():
    return lax.axis_index("x"), lax.axis_index("y"), lax.axis_index("c")


def _block_of(x, y, c):
    return 4 * x + 2 * y + c


def _chip(x, y, k):
    return (x + (k & 1)) % 2, (y + (k >> 1)) % 2


def _cols(width):
    def at(ref, d):
        return ref.at[:, pl.ds(pl.multiple_of(d * width, LANES), width)]
    return at


def _rows(height):
    def at(ref, d):
        return ref.at[pl.ds(pl.multiple_of(d * height, 16), height), :]
    return at


def _lead(ref, d):
    return ref.at[d]


def _gather_weights(shards, blocks, full_shapes, small, n_now):
    n = len(shards)
    small_rows = small.shape[0]

    def body(*refs):
        ins, small_in = refs[:n], refs[n]
        outs, small_out = refs[n + 1:2 * n + 1], refs[2 * n + 1]
        stage = refs[2 * n + 2:3 * n + 2]
        send, recv, local = refs[3 * n + 2:]
        x, y, c = _position()
        me = _block_of(x, y, c)
        sibling = (x, y, 1 - c)

        for a in range(n):
            stage[a][...] = ins[a][...].astype(BF16)

        def copy(a, k, block, to, src=None):
            dst = blocks[a](outs[a], block)
            return pltpu.make_async_remote_copy(
                src_ref=dst if src is None else src, dst_ref=dst, send_sem=send.at[a, k], recv_sem=recv.at[a, k],
                device_id=to, device_id_type=MESH)

        def small_copy(k):
            px, py, pc = (x + (k & 1)) % 2, (y + ((k >> 1) & 1)) % 2, (c + (k >> 2)) % 2
            return pltpu.make_async_remote_copy(
                src_ref=small_in, dst_ref=small_out.at[me], send_sem=send.at[n_now, k - 1], recv_sem=recv.at[n_now, k - 1],
                device_id=(px, py, pc), device_id_type=MESH)

        def small_arrival(k):
            px, py, pc = (x + (k & 1)) % 2, (y + ((k >> 1) & 1)) % 2, (c + (k >> 2)) % 2
            return pltpu.make_async_remote_copy(
                src_ref=small_in, dst_ref=small_out.at[_block_of(px, py, pc)], send_sem=send.at[n_now, k - 1],
                recv_sem=recv.at[n_now, k - 1], device_id=(px, py, pc), device_id_type=MESH)

        small_out[me] = small_in[...]
        small_sends = [small_copy(k) for k in range(1, N_DEV)]
        for cp in small_sends:
            cp.start()

        mine, first, passed = [], [], []
        for a in range(n):
            own = pltpu.make_async_copy(stage[a], blocks[a](outs[a], me), local.at[a])
            own.start()
            mine.append(own)
            if a >= n_now:
                continue
            sends = [copy(a, 0, me, sibling, src=stage[a])]
            sends += [copy(a, k, me, (*_chip(x, y, k), c), src=stage[a]) for k in (1, 2, 3)]
            for cp in sends:
                cp.start()
            first += sends
        for a in range(n_now):
            for k in (1, 2, 3):
                landed = _block_of(*_chip(x, y, k), c)
                copy(a, k, landed, (x, y, c)).wait_recv()
                fwd = copy(a, 3 + k, landed, sibling)
                fwd.start()
                passed.append(fwd)
        for a in range(n_now):
            copy(a, 0, _block_of(x, y, 1 - c), (x, y, c)).wait_recv()
            for k in (1, 2, 3):
                copy(a, 3 + k, _block_of(*_chip(x, y, k), 1 - c), (x, y, c)).wait_recv()
        for k in range(1, N_DEV):
            small_arrival(k).wait_recv()
        for cp in first + passed + small_sends:
            cp.wait_send()
        for own in mine:
            own.wait()

    out_shape = [jax.ShapeDtypeStruct(s, BF16) for s in full_shapes]
    out_shape.append(jax.ShapeDtypeStruct((N_DEV, small_rows, LANES), F32))
    return pl.pallas_call(
        body, name="gather_weights", out_shape=out_shape,
        in_specs=[VMEM_SPEC] * (n + 1), out_specs=[HBM_SPEC] * n + [VMEM_SPEC],
        scratch_shapes=[pltpu.VMEM(s.shape, BF16) for s in shards]
        + [pltpu.SemaphoreType.DMA((n_now + 1, 7)), pltpu.SemaphoreType.DMA((n_now + 1, 7)),
           pltpu.SemaphoreType.DMA((n,))],
        compiler_params=_params(),
    )(*shards, small)


def _gather_first(full, blocks, send, recv):
    x, y, c = _position()
    me = _block_of(x, y, c)
    peers = [(x, y, 1 - c)] + [(*_chip(x, y, k), c) for k in (1, 2, 3)]

    def copy(a, k, block):
        at = blocks[a](full[a], block)
        return pltpu.make_async_remote_copy(src_ref=at, dst_ref=at, send_sem=send[4 * a + k], recv_sem=recv[4 * a + k],
                                            device_id=peers[k], device_id_type=MESH)

    sends = [copy(a, k, me) for a in range(len(full)) for k in range(4)]
    arrivals = [copy(a, k, _block_of(*peers[k])) for a in range(len(full)) for k in range(4)]
    return sends, arrivals


def _gather_second(full, blocks, send, recv):
    x, y, c = _position()

    def copy(a, k, cc):
        at = blocks[a](full[a], _block_of(*_chip(x, y, k), cc))
        return pltpu.make_async_remote_copy(src_ref=at, dst_ref=at, send_sem=send[3 * a + k - 1],
                                            recv_sem=recv[3 * a + k - 1], device_id=(x, y, 1 - c), device_id_type=MESH)

    sends = [copy(a, k, c) for a in range(len(full)) for k in (1, 2, 3)]
    arrivals = [copy(a, k, 1 - c) for a in range(len(full)) for k in (1, 2, 3)]
    return sends, arrivals


def _split_call(body, name, arrays, sems_in, n_sems_out, after=None, token=False):
    n, m = len(arrays), len(sems_in)

    def kernel_body(*refs):
        outs = refs[n + m + (after is not None):]
        body(refs[:n], refs[n:n + m], outs[:n_sems_out])
        if token:
            outs[-1][...] = jnp.zeros_like(outs[-1])

    extra_in = [] if after is None else [after]
    outs = pl.pallas_call(
        kernel_body, name=name,
        out_shape=(*[pltpu.SemaphoreType.DMA(())] * n_sems_out, *[pltpu.HBM(a.shape, a.dtype) for a in arrays],
                   *([jax.ShapeDtypeStruct((SUBLANES, LANES), F32)] if token else [])),
        in_specs=[HBM_SPEC] * n + [SEM_SPEC] * m + [pl.BlockSpec(memory_space=pl.ANY)] * len(extra_in),
        out_specs=(*[SEM_SPEC] * n_sems_out, *[HBM_SPEC] * n, *([VMEM_SPEC] if token else [])),
        input_output_aliases={i: n_sems_out + i for i in range(n)},
        compiler_params=pltpu.CompilerParams(has_side_effects=DATAFLOW_EFFECT),
    )(*[pltpu.with_memory_space_constraint(a, pltpu.HBM) for a in arrays], *sems_in, *extra_in)
    sems, rest = list(outs[:n_sems_out]), list(outs[n_sems_out:])
    return (sems, rest[:n], rest[n]) if token else (sems, rest[:n])


def _gather_start(full, blocks, name):
    n = len(full)

    def body(arrays, _, sems):
        for cp in _gather_first(arrays, blocks, sems[:4 * n], sems[4 * n:])[0]:
            cp.start()

    sems, arrays, token = _split_call(body, name, full, [], 8 * n, token=True)
    return sems[:4 * n], sems[4 * n:], arrays, token


def _gather_forward(full, blocks, send_first, recv_first, after, name):
    n = len(full)

    def body(arrays, sems_in, sems):
        sends, arrivals = _gather_first(arrays, blocks, sems_in[:4 * n], sems_in[4 * n:])
        for cp in arrivals:
            cp.wait_recv()
        for cp in _gather_second(arrays, blocks, sems[:3 * n], sems[3 * n:])[0]:
            cp.start()
        for cp in sends:
            cp.wait_send()

    sems, arrays = _split_call(body, name, full, [*send_first, *recv_first], 6 * n, after=after)
    return sems[:3 * n], sems[3 * n:], arrays


def _gather_finish(full, blocks, send_second, recv_second, after, name):
    n = len(full)

    def body(arrays, sems_in, _):
        sends, arrivals = _gather_second(arrays, blocks, sems_in[:3 * n], sems_in[3 * n:])
        for cp in sends:
            cp.wait_send()
        for cp in arrivals:
            cp.wait_recv()

    return _split_call(body, name, full, [*send_second, *recv_second], 0, after=after)[1]


def _exchange_pair(grads, blocks, shard_shapes, name):
    n = len(grads)

    def body(*refs):
        ins, got = refs[:n], refs[n:2 * n]
        send, recv = refs[2 * n:]
        x, y, c = _position()
        copies = []
        for a in range(n):
            for k in range(4):
                cp = pltpu.make_async_remote_copy(
                    src_ref=blocks[a](ins[a], _block_of(*_chip(x, y, k), 1 - c)), dst_ref=got[a].at[k],
                    send_sem=send.at[a, k], recv_sem=recv.at[a, k], device_id=(x, y, 1 - c), device_id_type=MESH)
                cp.start()
                copies.append(cp)
        for cp in copies:
            cp.wait()

    return pl.pallas_call(
        body, name=name, out_shape=[jax.ShapeDtypeStruct((4,) + tuple(s), BF16) for s in shard_shapes],
        in_specs=[HBM_SPEC] * n, out_specs=[HBM_SPEC] * n,
        scratch_shapes=[pltpu.SemaphoreType.DMA((n, 4)), pltpu.SemaphoreType.DMA((n, 4))],
        compiler_params=_params(),
    )(*grads)


def _chip_copies(sums, lands, send, recv):
    x, y, c = _position()
    return [pltpu.make_async_remote_copy(
        src_ref=sums[a].at[k], dst_ref=lands[a].at[k - 1], send_sem=send[3 * a + k - 1], recv_sem=recv[3 * a + k - 1],
        device_id=(*_chip(x, y, k), c), device_id_type=MESH) for a in range(len(sums)) for k in (1, 2, 3)]


def _exchange_chips_start(pair_sums, name):
    n = len(pair_sums)
    lands = [pltpu.with_memory_space_constraint(lax.empty((3,) + tuple(p.shape[1:]), BF16), pltpu.HBM) for p in pair_sums]

    def body(*refs):
        sums, zones = refs[:n], refs[n:2 * n]
        send, recv = refs[2 * n:5 * n], refs[5 * n:8 * n]
        token = refs[-1]
        for cp in _chip_copies(sums, zones, send, recv):
            cp.start()
        token[...] = jnp.zeros_like(token)

    outs = pl.pallas_call(
        body, name=name,
        out_shape=(*[pltpu.SemaphoreType.DMA(())] * (6 * n),
                   *[pltpu.HBM(p.shape, BF16) for p in pair_sums], *[pltpu.HBM(z.shape, BF16) for z in lands],
                   jax.ShapeDtypeStruct((SUBLANES, LANES), F32)),
        in_specs=[HBM_SPEC] * (2 * n), out_specs=(*[SEM_SPEC] * (6 * n), *[HBM_SPEC] * (2 * n), VMEM_SPEC),
        input_output_aliases={i: 6 * n + i for i in range(2 * n)},
        compiler_params=pltpu.CompilerParams(has_side_effects=DATAFLOW_EFFECT),
    )(*[pltpu.with_memory_space_constraint(p, pltpu.HBM) for p in pair_sums], *lands)
    return outs[:3 * n], outs[3 * n:6 * n], outs[6 * n:7 * n], outs[7 * n:8 * n], outs[-1]


def _exchange_chips_wait(send, recv, sums, lands, after, name):
    n = len(sums)

    def body(*refs):
        sums_in, zones = refs[:n], refs[n:2 * n]
        send_in, recv_in = refs[2 * n:5 * n], refs[5 * n:8 * n]
        for cp in _chip_copies(sums_in, zones, send_in, recv_in):
            cp.wait_send()
            cp.wait_recv()

    outs = pl.pallas_call(
        body, name=name,
        out_shape=(*[pltpu.HBM(p.shape, BF16) for p in sums], *[pltpu.HBM(z.shape, BF16) for z in lands]),
        in_specs=[HBM_SPEC] * (2 * n) + [SEM_SPEC] * (6 * n) + [pl.BlockSpec(memory_space=pl.ANY)],
        out_specs=[HBM_SPEC] * (2 * n), input_output_aliases={i: i for i in range(2 * n)},
        compiler_params=pltpu.CompilerParams(has_side_effects=DATAFLOW_EFFECT),
    )(*sums, *lands, *send, *recv, after)
    return outs[:n], outs[n:]


def _allreduce_small(part):
    rows = part.shape[0]

    def body(in_ref, out_ref, buf, send, recv):
        x, y, c = _position()
        me = _block_of(x, y, c)

        def peer(k):
            return (x + (k & 1)) % 2, (y + ((k >> 1) & 1)) % 2, (c + (k >> 2)) % 2

        sends = []
        for k in range(1, N_DEV):
            cp = pltpu.make_async_remote_copy(src_ref=in_ref, dst_ref=buf.at[me], send_sem=send.at[k - 1],
                                              recv_sem=recv.at[k - 1], device_id=peer(k), device_id_type=MESH)
            cp.start()
            sends.append(cp)
        buf[me] = in_ref[...]
        for k in range(1, N_DEV):
            pltpu.make_async_remote_copy(src_ref=in_ref, dst_ref=buf.at[_block_of(*peer(k))], send_sem=send.at[k - 1],
                                         recv_sem=recv.at[k - 1], device_id=peer(k), device_id_type=MESH).wait_recv()
        total = buf[0]
        for d in range(1, N_DEV):
            total = total + buf[d]
        out_ref[...] = total
        for cp in sends:
            cp.wait_send()

    return pl.pallas_call(
        body, name="allreduce_small", out_shape=jax.ShapeDtypeStruct(part.shape, F32),
        in_specs=[VMEM_SPEC], out_specs=VMEM_SPEC,
        scratch_shapes=[pltpu.VMEM((N_DEV, rows, LANES), F32), pltpu.SemaphoreType.DMA((7,)), pltpu.SemaphoreType.DMA((7,))],
        compiler_params=_params(),
    )(part)


def _in_proj(x, g1, w_in):
    t = x.shape[0]
    tm, bn = min(512, t), 1024

    def body(x_ref, g_ref, w_ref, proj_ref, h_ref, h_s):
        @pl.when(pl.program_id(1) == 0)
        def _():
            n, _ = _rms_fwd(x_ref[...])
            h_s[...] = (n * g_ref[...]).astype(BF16)
            h_ref[...] = h_s[...]
        proj_ref[...] = _dot(h_s[...], w_ref[...]).astype(BF16)

    return pl.pallas_call(
        body, name="in_proj", grid=(t // tm, IN_COLS // bn),
        out_shape=[jax.ShapeDtypeStruct((t, IN_COLS), BF16), jax.ShapeDtypeStruct((t, D_MODEL), BF16)],
        in_specs=[pl.BlockSpec((tm, D_MODEL), lambda i, j: (i, 0)), pl.BlockSpec((1, D_MODEL), lambda i, j: (0, 0)),
                  pl.BlockSpec((D_MODEL, bn), lambda i, j: (0, j))],
        out_specs=[pl.BlockSpec((tm, bn), lambda i, j: (i, j)), pl.BlockSpec((tm, D_MODEL), lambda i, j: (i, 0))],
        scratch_shapes=[pltpu.VMEM((tm, D_MODEL), BF16)],
        compiler_params=_params("parallel", "arbitrary"),
    )(x, g1, w_in)


def _section(s, t):
    return pl.BlockSpec((t, CB), lambda h, s=s: (0, s * (D_MODEL // CB) + h))


def _conv_mixer_fwd(proj, w_short):
    t = proj.shape[0]
    rc = _row_chunk(t)

    def body(b_ref, c_ref, x_ref, w_ref, y_ref, pad):
        pad[pl.ds(0, PAD), :] = jnp.zeros((PAD, CB), F32)
        for r0 in range(0, t, rc):
            rows = pl.ds(r0, rc)
            pad[pl.ds(PAD + r0, rc), :] = c_ref[rows, :].astype(F32) * x_ref[rows, :].astype(F32)
        w = w_ref[...]
        for r0 in range(0, t, rc):
            rows = pl.ds(r0, rc)
            y_ref[rows, :] = (b_ref[rows, :].astype(F32) * _conv_causal(pad, w, r0, rc, 3)).astype(BF16)

    return pl.pallas_call(
        body, name="conv_mixer_fwd", grid=(D_MODEL // CB,),
        out_shape=jax.ShapeDtypeStruct((t, D_MODEL), BF16),
        in_specs=[_section(0, t), _section(1, t), _section(2, t), pl.BlockSpec((3, CB), lambda h: (0, h))],
        out_specs=pl.BlockSpec((t, CB), lambda h: (0, h)),
        scratch_shapes=[pltpu.VMEM((t + PAD, CB), F32)],
        compiler_params=_params("parallel"),
    )(proj, proj, proj, w_short)


def _lru_gates(xl, wa, ba, wx, bx, ls, first_row):
    xb = xl.astype(BF16)
    ra = jax.nn.sigmoid(_dot(xb, wa) + ba)
    ia = jax.nn.sigmoid(_dot(xb, wx) + bx)
    la = LRU_C * ra * ls
    a = jnp.exp(la)
    one_minus = -_expm1_neg(2.0 * la)
    mult = jnp.where(first_row, 1.0, jnp.sqrt(one_minus))
    return xb, ra, ia, a, one_minus, mult


def _head_specs():
    vec = pl.BlockSpec((1, CB), lambda h: (0, h))
    mat = pl.BlockSpec((N_DEV, None, HEAD_DIM // N_DEV, HEAD_DIM), lambda h: (0, h, 0, 0))
    return vec, mat


def _lru_fwd(proj, w_conv, b_conv, wa, ba, wx, bx, lam):
    t = proj.shape[0]
    rc = _row_chunk(t)
    vec, mat = _head_specs()

    def body(lx_ref, ly_ref, wc_ref, bc_ref, wa_ref, ba_ref, wx_ref, bx_ref, lam_ref, yb_ref, hl_ref, pad, a_s, u_s):
        pad[pl.ds(0, PAD), :] = jnp.zeros((PAD, CB), F32)
        for r0 in range(0, t, rc):
            pad[pl.ds(PAD + r0, rc), :] = lx_ref[pl.ds(r0, rc), :].astype(F32)
        wc, bc = wc_ref[...], bc_ref[...]
        wa_m, wx_m = wa_ref[...].reshape(HEAD_DIM, HEAD_DIM), wx_ref[...].reshape(HEAD_DIM, HEAD_DIM)
        ls = _log_sigmoid(lam_ref[...])
        for r0 in range(0, t, rc):
            xl = _conv_causal(pad, wc, r0, rc, 4) + bc
            first = (lax.broadcasted_iota(jnp.int32, (rc, CB), 0) + r0) == 0
            _, _, ia, a, _, mult = _lru_gates(xl, wa_m, ba_ref[...], wx_m, bx_ref[...], ls, first)
            a_s[pl.ds(r0, rc), :] = a
            u_s[pl.ds(r0, rc), :] = mult * (ia * xl)

        row = lax.broadcasted_iota(jnp.int32, (SUBLANES, CB), 0)

        def group(g, carry):
            r = pl.multiple_of(g * SUBLANES, SUBLANES)
            a_g, b_g = a_s[pl.ds(r, SUBLANES), :], u_s[pl.ds(r, SUBLANES), :]
            for s in (1, 2, 4):
                keep = row >= s
                b_g = jnp.where(keep, a_g * pltpu.roll(b_g, s, 0) + b_g, b_g)
                a_g = jnp.where(keep, a_g * pltpu.roll(a_g, s, 0), a_g)
            h_g = b_g + a_g * carry
            hl_ref[pl.ds(r, SUBLANES), :] = h_g
            return jnp.broadcast_to(h_g[SUBLANES - 1:SUBLANES, :], (SUBLANES, CB))

        lax.fori_loop(0, t // SUBLANES, group, jnp.zeros((SUBLANES, CB), F32))
        for r0 in range(0, t, rc):
            rows = pl.ds(r0, rc)
            yb_ref[rows, :] = (hl_ref[rows, :] * _gelu(ly_ref[rows, :].astype(F32))).astype(BF16)

    blk = pl.BlockSpec((t, CB), lambda h: (0, h))
    return pl.pallas_call(
        body, name="lru_fwd", grid=(N_HEADS,),
        out_shape=[jax.ShapeDtypeStruct((t, D_MODEL), BF16), jax.ShapeDtypeStruct((t, D_MODEL), F32)],
        in_specs=[_section(3, t), _section(4, t), pl.BlockSpec((4, CB), lambda h: (0, h)), vec, mat, vec, mat, vec, vec],
        out_specs=[blk, blk],
        scratch_shapes=[pltpu.VMEM((t + PAD, CB), F32), pltpu.VMEM((t, CB), F32), pltpu.VMEM((t, CB), F32)],
        compiler_params=_params("parallel"),
    )(proj, proj, w_conv, b_conv, wa, ba, wx, bx, lam)


def _merge(y_a, y_b, proj, x, w_cb, w_lb, w_out, g2, g3):
    t = x.shape[0]
    tm = min(256, t)

    def body(ya_ref, yb_ref, gc_ref, gl_ref, x_ref, wcb_ref, wlb_ref, wo_ref, g2_ref, g3_ref,
             pa_ref, pb_ref, mg_ref, mix_ref, x1_ref, h2_ref):
        pa = _dot(ya_ref[...], wcb_ref[...]).astype(BF16)
        pb = _dot(yb_ref[...], wlb_ref[...]).astype(BF16)
        pa_ref[...] = pa
        pb_ref[...] = pb
        merged = (jax.nn.sigmoid(gc_ref[...].astype(F32)) * pa.astype(F32)
                  + jax.nn.sigmoid(gl_ref[...].astype(F32)) * pb.astype(F32)).astype(BF16)
        mg_ref[...] = merged
        mix = _dot(merged, wo_ref[...])
        mix_ref[...] = mix
        n2, _ = _rms_fwd(mix)
        x1 = x_ref[...] + n2 * g2_ref[...]
        x1_ref[...] = x1
        n3, _ = _rms_fwd(x1)
        h2_ref[...] = (n3 * g3_ref[...]).astype(BF16)

    row = pl.BlockSpec((tm, D_MODEL), lambda i: (i, 0))
    full = pl.BlockSpec((D_MODEL, D_MODEL), lambda i: (0, 0))
    vec = pl.BlockSpec((1, D_MODEL), lambda i: (0, 0))
    act = jax.ShapeDtypeStruct((t, D_MODEL), BF16)
    res = jax.ShapeDtypeStruct((t, D_MODEL), F32)
    return pl.pallas_call(
        body, name="merge_fwd", grid=(t // tm,), out_shape=[act, act, act, res, res, act],
        in_specs=[row, row, pl.BlockSpec((tm, D_MODEL), lambda i: (i, 5)), pl.BlockSpec((tm, D_MODEL), lambda i: (i, 6)),
                  row, full, full, full, vec, vec],
        out_specs=[row] * 6,
        compiler_params=_params("parallel"),
    )(y_a, y_b, proj, proj, x, w_cb, w_lb, w_out, g2, g3)


N_FF_BLOCKS = D_FF // CB


def _ffn_up(h2, w_up, w_conv, b_conv):
    t = h2.shape[0]
    rc = _row_chunk(t)

    def body(h_ref, wg_ref, wv_ref, cg_ref, cv_ref, bg_ref, bv_ref, ug_ref, uv_ref, f_ref, pad_g, pad_v):
        zeros = jnp.zeros((PAD, CB), F32)
        pad_g[pl.ds(0, PAD), :] = zeros
        pad_v[pl.ds(0, PAD), :] = zeros
        for r0 in range(0, t, rc):
            rows = pl.ds(r0, rc)
            ug = _dot(h_ref[rows, :], wg_ref[...]).astype(BF16)
            uv = _dot(h_ref[rows, :], wv_ref[...]).astype(BF16)
            ug_ref[rows, :] = ug
            uv_ref[rows, :] = uv
            pad_g[pl.ds(PAD + r0, rc), :] = ug.astype(F32)
            pad_v[pl.ds(PAD + r0, rc), :] = uv.astype(F32)
        cg, cv = cg_ref[...], cv_ref[...]
        for r0 in range(0, t, rc):
            gate = _conv_causal(pad_g, cg, r0, rc, 3) + bg_ref[...]
            val = _conv_causal(pad_v, cv, r0, rc, 3) + bv_ref[...]
            f_ref[pl.ds(r0, rc), :] = (_gelu(gate) * val).astype(BF16)

    nb = N_FF_BLOCKS
    act = jax.ShapeDtypeStruct((t, D_FF), BF16)
    blk = pl.BlockSpec((t, CB), lambda j: (0, j))
    return pl.pallas_call(
        body, name="ffn_up_fwd", grid=(nb,), out_shape=[act, act, act],
        in_specs=[pl.BlockSpec((t, D_MODEL), lambda j: (0, 0)),
                  pl.BlockSpec((D_MODEL, CB), lambda j: (0, j)), pl.BlockSpec((D_MODEL, CB), lambda j: (0, nb + j)),
                  pl.BlockSpec((3, CB), lambda j: (0, j)), pl.BlockSpec((3, CB), lambda j: (0, nb + j)),
                  pl.BlockSpec((1, CB), lambda j: (0, j)), pl.BlockSpec((1, CB), lambda j: (0, nb + j))],
        out_specs=[blk, blk, blk],
        scratch_shapes=[pltpu.VMEM((t + PAD, CB), F32), pltpu.VMEM((t + PAD, CB), F32)],
        compiler_params=_params("parallel"),
    )(h2, w_up, w_up, w_conv, w_conv, b_conv, b_conv)


def _ffn_down(f, w_down, x1, target, g4):
    t = f.shape[0]
    tm = min(256, t)

    def body(f_ref, w_ref, x1_ref, tg_ref, g_ref, dy_ref, dout_ref, df_ref, dg_ref, loss_ref):
        @pl.when(pl.program_id(0) == 0)
        def _():
            dg_ref[...] = jnp.zeros_like(dg_ref)
            loss_ref[...] = jnp.zeros_like(loss_ref)
        out = _dot(f_ref[...], w_ref[...])
        n4, r4 = _rms_fwd(out)
        err = x1_ref[...] + n4 * g_ref[...] - tg_ref[...]
        loss_ref[...] += jnp.full(loss_ref.shape, 0.5 / D_MODEL, F32) * jnp.sum(err * err)
        dy = err * (1.0 / D_MODEL)
        dy_ref[...] = dy
        dg_ref[...] += jnp.sum(dy * n4, axis=0, keepdims=True)
        d_out = _rms_bwd(n4, r4, dy * g_ref[...]).astype(BF16)
        dout_ref[...] = d_out
        df_ref[...] = _dot_nt(d_out, w_ref[...]).astype(BF16)

    row = pl.BlockSpec((tm, D_MODEL), lambda i: (i, 0))
    wide = pl.BlockSpec((tm, D_FF), lambda i: (i, 0))
    vec = pl.BlockSpec((1, D_MODEL), lambda i: (0, 0))
    return pl.pallas_call(
        body, name="ffn_down_fwd_bwd", grid=(t // tm,),
        out_shape=[jax.ShapeDtypeStruct((t, D_MODEL), F32), jax.ShapeDtypeStruct((t, D_MODEL), BF16),
                   jax.ShapeDtypeStruct((t, D_FF), BF16), jax.ShapeDtypeStruct((1, D_MODEL), F32),
                   jax.ShapeDtypeStruct((SUBLANES, LANES), F32)],
        in_specs=[wide, pl.BlockSpec((D_FF, D_MODEL), lambda i: (0, 0)), row, row, vec],
        out_specs=[row, row, wide, vec, pl.BlockSpec((SUBLANES, LANES), lambda i: (0, 0))],
        compiler_params=_params("arbitrary"),
    )(f, w_down, x1, target, g4)


def _grad_tn(a, b, bm, name):
    t, m = a.shape
    n = b.shape[1]

    def body(a_ref, b_ref, o_ref):
        o_ref[...] = _dot_tn(a_ref[...], b_ref[...]).astype(BF16)

    return pl.pallas_call(
        body, name=name, grid=(m // bm,), out_shape=jax.ShapeDtypeStruct((m, n), BF16),
        in_specs=[pl.BlockSpec((t, bm), lambda i: (0, i)), pl.BlockSpec((t, n), lambda i: (0, 0))],
        out_specs=pl.BlockSpec((bm, n), lambda i: (i, 0)),
        compiler_params=_params("parallel"),
    )(a, b)


def _ffn_up_bwd(up_g, up_v, d_f, w_conv, b_conv, h2, w_up):
    t = h2.shape[0]
    rc = _row_chunk(t)
    nb = N_FF_BLOCKS

    def body(ug_ref, uv_ref, df_ref, cg_ref, cv_ref, bg_ref, bv_ref, h_ref, w_ref,
             dw_ref, dcw_ref, dcb_ref, dh_ref, d_up, small, pad_g, pad_v, back_g, back_v):
        j, k = pl.program_id(0), pl.program_id(1)

        @pl.when((j == 0) & (k == 0))
        def _():
            dh_ref[...] = jnp.zeros_like(dh_ref)

        @pl.when(k == 0)
        def _():
            zeros = jnp.zeros((PAD, CB), F32)
            pad_g[pl.ds(0, PAD), :] = zeros
            pad_v[pl.ds(0, PAD), :] = zeros
            back_g[pl.ds(t, PAD), :] = zeros
            back_v[pl.ds(t, PAD), :] = zeros
            for r0 in range(0, t, rc):
                pad_g[pl.ds(PAD + r0, rc), :] = ug_ref[pl.ds(r0, rc), :].astype(F32)
                pad_v[pl.ds(PAD + r0, rc), :] = uv_ref[pl.ds(r0, rc), :].astype(F32)
            cg, cv = cg_ref[...], cv_ref[...]
            for r0 in range(0, t, rc):
                rows = pl.ds(r0, rc)
                gate = _conv_causal(pad_g, cg, r0, rc, 3) + bg_ref[...]
                val = _conv_causal(pad_v, cv, r0, rc, 3) + bv_ref[...]
                act, d_act = _gelu_and_grad(gate)
                d_f = df_ref[rows, :].astype(F32)
                back_g[rows, :] = d_f * val * d_act
                back_v[rows, :] = d_f * act
            for which, (back, pad, cw) in enumerate(((back_g, pad_g, cg), (back_v, pad_v, cv))):
                taps = [jnp.zeros((1, CB), F32)] * 3
                bias = jnp.zeros((1, CB), F32)
                for r0 in range(0, t, rc):
                    rows = pl.ds(r0, rc)
                    d_up[which, rows, :] = _conv_anticausal(back, cw, r0, rc, 3).astype(BF16)
                    g = back[rows, :]
                    taps = [acc + new for acc, new in zip(taps, _conv_wgrad(g, pad, r0, rc, 3))]
                    bias = bias + jnp.sum(g, axis=0, keepdims=True)
                small[which] = jnp.concatenate(taps + [bias] + [jnp.zeros((SUBLANES - 4, CB), F32)], axis=0)

        for r0 in range(0, t, rc):
            rows = pl.ds(r0, rc)
            dh_ref[rows, :] += _dot_nt(d_up[k, rows, :], w_ref[...])
        dw_ref[...] = _dot_tn(h_ref[...], d_up[k]).astype(BF16)
        dcw_ref[...] = small[k, pl.ds(0, 3), :]
        dcb_ref[...] = small[k, pl.ds(3, 1), :]

    blk = pl.BlockSpec((t, CB), lambda j, k: (0, j))
    both = lambda rows: pl.BlockSpec((rows, CB), lambda j, k: (0, nb * k + j))
    gate = lambda rows: pl.BlockSpec((rows, CB), lambda j, k: (0, j))
    val = lambda rows: pl.BlockSpec((rows, CB), lambda j, k: (0, nb + j))
    return pl.pallas_call(
        body, name="ffn_up_bwd", grid=(nb, 2),
        out_shape=[jax.ShapeDtypeStruct((D_MODEL, 2 * D_FF), BF16), jax.ShapeDtypeStruct((3, 2 * D_FF), F32),
                   jax.ShapeDtypeStruct((1, 2 * D_FF), F32), jax.ShapeDtypeStruct((t, D_MODEL), F32)],
        in_specs=[blk, blk, blk, gate(3), val(3), gate(1), val(1),
                  pl.BlockSpec((t, D_MODEL), lambda j, k: (0, 0)), both(D_MODEL)],
        out_specs=[both(D_MODEL), both(3), both(1), pl.BlockSpec((t, D_MODEL), lambda j, k: (0, 0))],
        scratch_shapes=[pltpu.VMEM((2, t, CB), BF16), pltpu.VMEM((2, SUBLANES, CB), F32)]
        + [pltpu.VMEM((t + PAD, CB), F32)] * 4,
        compiler_params=_params("arbitrary", "arbitrary"),
    )(up_g, up_v, d_f, w_conv, w_conv, b_conv, b_conv, h2, w_up)


def _merge_bwd(dy, d_h2, x1, mix, g3, g2, w_out, w_cb, w_lb, pa, pb, proj):
    t = dy.shape[0]
    tm = min(256, t)

    def body(dy_ref, dh2_ref, x1_ref, mix_ref, g3_ref, g2_ref, wo_ref, wcb_ref, wlb_ref, pa_ref, pb_ref, gc_ref, gl_ref,
             dx1_ref, dmix_ref, dpa_ref, dpb_ref, dya_ref, dyb_ref, dgate_ref, dg3_ref, dg2_ref):
        @pl.when(pl.program_id(0) == 0)
        def _():
            dg3_ref[...] = jnp.zeros_like(dg3_ref)
            dg2_ref[...] = jnp.zeros_like(dg2_ref)
        n3, r3 = _rms_fwd(x1_ref[...])
        d_h2 = dh2_ref[...]
        dg3_ref[...] += jnp.sum(d_h2 * n3, axis=0, keepdims=True)
        dx1 = dy_ref[...] + _rms_bwd(n3, r3, d_h2 * g3_ref[...])
        dx1_ref[...] = dx1
        n2, r2 = _rms_fwd(mix_ref[...])
        dg2_ref[...] += jnp.sum(dx1 * n2, axis=0, keepdims=True)
        d_mix = _rms_bwd(n2, r2, dx1 * g2_ref[...]).astype(BF16)
        dmix_ref[...] = d_mix
        d_merged = _dot_nt(d_mix, wo_ref[...])
        sc = jax.nn.sigmoid(gc_ref[...].astype(F32))
        sl = jax.nn.sigmoid(gl_ref[...].astype(F32))
        d_pa = (d_merged * sc).astype(BF16)
        d_pb = (d_merged * sl).astype(BF16)
        dpa_ref[...] = d_pa
        dpb_ref[...] = d_pb
        dgate_ref[0] = (d_merged * pa_ref[...].astype(F32) * sc * (1.0 - sc)).astype(BF16)
        dgate_ref[1] = (d_merged * pb_ref[...].astype(F32) * sl * (1.0 - sl)).astype(BF16)
        dya_ref[...] = _dot_nt(d_pa, wcb_ref[...]).astype(BF16)
        dyb_ref[...] = _dot_nt(d_pb, wlb_ref[...]).astype(BF16)

    row = pl.BlockSpec((tm, D_MODEL), lambda i: (i, 0))
    full = pl.BlockSpec((D_MODEL, D_MODEL), lambda i: (0, 0))
    vec = pl.BlockSpec((1, D_MODEL), lambda i: (0, 0))
    act = jax.ShapeDtypeStruct((t, D_MODEL), BF16)
    small = jax.ShapeDtypeStruct((1, D_MODEL), F32)
    return pl.pallas_call(
        body, name="merge_bwd", grid=(t // tm,),
        out_shape=[jax.ShapeDtypeStruct((t, D_MODEL), F32), act, act, act, act, act,
                   jax.ShapeDtypeStruct((2, t, D_MODEL), BF16), small, small],
        in_specs=[row, row, row, row, vec, vec, full, full, full, row, row,
                  pl.BlockSpec((tm, D_MODEL), lambda i: (i, 5)), pl.BlockSpec((tm, D_MODEL), lambda i: (i, 6))],
        out_specs=[row] * 6 + [pl.BlockSpec((2, tm, D_MODEL), lambda i: (0, i, 0)), vec, vec],
        compiler_params=_params("arbitrary"),
    )(dy, d_h2, x1, mix, g3, g2, w_out, w_cb, w_lb, pa, pb, proj, proj)


def _conv_mixer_bwd(proj, d_ya, w_short):
    t = proj.shape[0]
    rc = _row_chunk(t)

    def body(b_ref, c_ref, x_ref, dy_ref, w_ref, d_ref, dw_ref, pad, back):
        pad[pl.ds(0, PAD), :] = jnp.zeros((PAD, CB), F32)
        back[pl.ds(t, PAD), :] = jnp.zeros((PAD, CB), F32)
        for r0 in range(0, t, rc):
            rows = pl.ds(r0, rc)
            pad[pl.ds(PAD + r0, rc), :] = c_ref[rows, :].astype(F32) * x_ref[rows, :].astype(F32)
        w = w_ref[...]
        for r0 in range(0, t, rc):
            rows = pl.ds(r0, rc)
            d_y = dy_ref[rows, :].astype(F32)
            d_ref[0, rows, :] = (d_y * _conv_causal(pad, w, r0, rc, 3)).astype(BF16)
            back[rows, :] = d_y * b_ref[rows, :].astype(F32)
        taps = [jnp.zeros((1, CB), F32)] * 3
        for r0 in range(0, t, rc):
            rows = pl.ds(r0, rc)
            d_u = _conv_anticausal(back, w, r0, rc, 3)
            d_ref[1, rows, :] = (d_u * x_ref[rows, :].astype(F32)).astype(BF16)
            d_ref[2, rows, :] = (d_u * c_ref[rows, :].astype(F32)).astype(BF16)
            taps = [acc + new for acc, new in zip(taps, _conv_wgrad(back[rows, :], pad, r0, rc, 3))]
        dw_ref[...] = jnp.concatenate(taps, axis=0)

    blk = pl.BlockSpec((t, CB), lambda h: (0, h))
    return pl.pallas_call(
        body, name="conv_mixer_bwd", grid=(D_MODEL // CB,),
        out_shape=[jax.ShapeDtypeStruct((3, t, D_MODEL), BF16), jax.ShapeDtypeStruct((3, D_MODEL), F32)],
        in_specs=[_section(0, t), _section(1, t), _section(2, t), blk, pl.BlockSpec((3, CB), lambda h: (0, h))],
        out_specs=[pl.BlockSpec((3, t, CB), lambda h: (0, 0, h)), pl.BlockSpec((3, CB), lambda h: (0, h))],
        scratch_shapes=[pltpu.VMEM((t + PAD, CB), F32), pltpu.VMEM((t + PAD, CB), F32)],
        compiler_params=_params("parallel"),
    )(proj, proj, proj, d_ya, w_short)


LRU_SMALL_ROWS = 8


def _lru_bwd(proj, hl, d_yb, w_conv, b_conv, wa, ba, wx, bx, lam):
    t = proj.shape[0]
    rc = _row_chunk(t)
    vec, mat = _head_specs()

    def body(lx_ref, ly_ref, hl_ref, dy_ref, wc_ref, bc_ref, wa_ref, ba_ref, wx_ref, bx_ref, lam_ref,
             d_ref, dwa_ref, dwx_ref, small_ref, pad, a_next, dh_s, h_prev, back, acc_a, acc_x):
        zeros = jnp.zeros((PAD, CB), F32)
        pad[pl.ds(0, PAD), :] = zeros
        h_prev[pl.ds(0, PAD), :] = zeros
        a_next[pl.ds(t, PAD), :] = zeros
        back[pl.ds(t, PAD), :] = zeros
        for r0 in range(0, t, rc):
            pad[pl.ds(PAD + r0, rc), :] = lx_ref[pl.ds(r0, rc), :].astype(F32)
            h_prev[pl.ds(PAD + r0, rc), :] = hl_ref[pl.ds(r0, rc), :]
        wc, bc = wc_ref[...], bc_ref[...]
        wa_m, wx_m = wa_ref[...].reshape(HEAD_DIM, HEAD_DIM), wx_ref[...].reshape(HEAD_DIM, HEAD_DIM)
        ls = _log_sigmoid(lam_ref[...])

        def gates(r0):
            xl = _conv_causal(pad, wc, r0, rc, 4) + bc
            first = (lax.broadcasted_iota(jnp.int32, (rc, CB), 0) + r0) == 0
            return (xl, first) + _lru_gates(xl, wa_m, ba_ref[...], wx_m, bx_ref[...], ls, first)

        for r0 in range(0, t, rc):
            rows = pl.ds(r0, rc)
            a = gates(r0)[5]
            a_next[pl.ds(PAD - 1 + r0, rc), :] = a
            act, d_act = _gelu_and_grad(ly_ref[rows, :].astype(F32))
            d_y = dy_ref[rows, :].astype(F32)
            dh_s[rows, :] = d_y * act
            d_ref[1, rows, :] = (d_y * hl_ref[rows, :] * d_act).astype(BF16)

        row = lax.broadcasted_iota(jnp.int32, (SUBLANES, CB), 0)
        groups = t // SUBLANES

        def group(i, carry):
            r = pl.multiple_of((groups - 1 - i) * SUBLANES, SUBLANES)
            a_g, b_g = a_next[pl.ds(PAD + r, SUBLANES), :], dh_s[pl.ds(r, SUBLANES), :]
            for s in (1, 2, 4):
                keep = row < SUBLANES - s
                b_g = jnp.where(keep, a_g * pltpu.roll(b_g, SUBLANES - s, 0) + b_g, b_g)
                a_g = jnp.where(keep, a_g * pltpu.roll(a_g, SUBLANES - s, 0), a_g)
            d_g = b_g + a_g * carry
            dh_s[pl.ds(r, SUBLANES), :] = d_g
            return jnp.broadcast_to(d_g[0:1, :], (SUBLANES, CB))

        lax.fori_loop(0, groups, group, jnp.zeros((SUBLANES, CB), F32))

        acc_a[...] = jnp.zeros_like(acc_a)
        acc_x[...] = jnp.zeros_like(acc_x)
        d_ba = d_bx = d_ls = jnp.zeros((1, CB), F32)
        for r0 in range(0, t, rc):
            rows = pl.ds(r0, rc)
            xl, first, xb, ra, ia, a, one_minus, mult = gates(r0)
            d_h = dh_s[rows, :]
            d_a = d_h * h_prev[pl.ds(PAD - 1 + r0, rc), :]
            d_mult = d_h * ia * xl
            d_ia = d_h * mult * xl
            d_xl = d_h * mult * ia
            d_mult_d_la = jnp.where(first, 0.0, (one_minus - 1.0) / mult)
            d_la = d_a * a + d_mult * d_mult_d_la
            d_ls = d_ls + jnp.sum(d_la * ra, axis=0, keepdims=True) * LRU_C
            d_za = d_la * (LRU_C * ls) * ra * (1.0 - ra)
            d_zx = d_ia * ia * (1.0 - ia)
            d_ba = d_ba + jnp.sum(d_za, axis=0, keepdims=True)
            d_bx = d_bx + jnp.sum(d_zx, axis=0, keepdims=True)
            d_za, d_zx = d_za.astype(BF16), d_zx.astype(BF16)
            acc_a[...] += _dot_tn(xb, d_za)
            acc_x[...] += _dot_tn(xb, d_zx)
            back[rows, :] = d_xl + _dot_nt(d_za, wa_m) + _dot_nt(d_zx, wx_m)
        taps = [jnp.zeros((1, CB), F32)] * 4
        d_bc = jnp.zeros((1, CB), F32)
        for r0 in range(0, t, rc):
            rows = pl.ds(r0, rc)
            d_ref[0, rows, :] = _conv_anticausal(back, wc, r0, rc, 4).astype(BF16)
            g = back[rows, :]
            taps = [acc + new for acc, new in zip(taps, _conv_wgrad(g, pad, r0, rc, 4))]
            d_bc = d_bc + jnp.sum(g, axis=0, keepdims=True)
        d_lam = d_ls * jax.nn.sigmoid(-lam_ref[...])
        small_ref[...] = jnp.concatenate(taps + [d_bc, d_ba, d_bx, d_lam], axis=0)
        dwa_ref[...] = acc_a[...].reshape(N_DEV, HEAD_DIM // N_DEV, HEAD_DIM).astype(BF16)
        dwx_ref[...] = acc_x[...].reshape(N_DEV, HEAD_DIM // N_DEV, HEAD_DIM).astype(BF16)

    blk = pl.BlockSpec((t, CB), lambda h: (0, h))
    gate_grad = jax.ShapeDtypeStruct((N_DEV, N_HEADS, HEAD_DIM // N_DEV, HEAD_DIM), BF16)
    return pl.pallas_call(
        body, name="lru_bwd", grid=(N_HEADS,),
        out_shape=[jax.ShapeDtypeStruct((2, t, D_MODEL), BF16), gate_grad, gate_grad,
                   jax.ShapeDtypeStruct((LRU_SMALL_ROWS, D_MODEL), F32)],
        in_specs=[_section(3, t), _section(4, t), blk, blk, pl.BlockSpec((4, CB), lambda h: (0, h)),
                  vec, mat, vec, mat, vec, vec],
        out_specs=[pl.BlockSpec((2, t, CB), lambda h: (0, 0, h)), mat, mat,
                   pl.BlockSpec((LRU_SMALL_ROWS, CB), lambda h: (0, h))],
        scratch_shapes=[pltpu.VMEM((t + PAD, CB), F32), pltpu.VMEM((t + PAD, CB), F32), pltpu.VMEM((t, CB), F32),
                        pltpu.VMEM((t + PAD, CB), F32), pltpu.VMEM((t + PAD, CB), F32),
                        pltpu.VMEM((HEAD_DIM, HEAD_DIM), F32), pltpu.VMEM((HEAD_DIM, HEAD_DIM), F32)],
        compiler_params=_params("parallel"),
    )(proj, proj, hl, d_yb, w_conv, b_conv, wa, ba, wx, bx, lam)


def _stack_maps(halves):
    def conv(sec, part):
        return jnp.minimum(sec, 2), jnp.where(sec < 3, part, halves - 1)

    def lru(sec, part):
        return jnp.clip(sec - 3, 0, 1), jnp.where(sec < 3, 0, jnp.where(sec < 5, part, halves - 1))

    def gate(sec, part):
        return jnp.clip(sec - 5, 0, 1), jnp.where(sec < 5, 0, part)

    return conv, lru, gate


def _pick_stack(sec, refs, fn):
    @pl.when(sec < 3)
    def _():
        fn(refs[0])

    @pl.when((sec >= 3) & (sec < 5))
    def _():
        fn(refs[1])

    @pl.when(sec >= 5)
    def _():
        fn(refs[2])


def _in_proj_wgrad(h, d_conv, d_lru, d_gate):
    t = h.shape[0]
    halves, bn = 2, D_MODEL // 2
    maps = _stack_maps(halves)

    def body(h_ref, dc_ref, dl_ref, dg_ref, o_ref):
        def emit(ref):
            o_ref[...] = _dot_tn(h_ref[...], ref[...]).astype(BF16)
        _pick_stack(pl.program_id(0) // halves, (dc_ref, dl_ref, dg_ref), emit)

    def spec(m):
        def index(s):
            stack, part = m(s // halves, s % halves)
            return stack, 0, part
        return pl.BlockSpec((None, t, bn), index)

    return pl.pallas_call(
        body, name="in_proj_wgrad", grid=(7 * halves,), out_shape=jax.ShapeDtypeStruct((D_MODEL, IN_COLS), BF16),
        in_specs=[pl.BlockSpec((t, D_MODEL), lambda s: (0, 0))] + [spec(m) for m in maps],
        out_specs=pl.BlockSpec((D_MODEL, bn), lambda s: (0, s)),
        compiler_params=_params("arbitrary"),
    )(h, d_conv, d_lru, d_gate)


def _in_proj_xgrad(d_conv, d_lru, d_gate, w_in, x, dx1, g1):
    t = x.shape[0]
    tm = min(512, t)
    maps = _stack_maps(1)

    def body(dc_ref, dl_ref, dg_ref, w_ref, x_ref, dx1_ref, g_ref, dx_ref, dgain_ref, acc):
        i, s = pl.program_id(0), pl.program_id(1)

        @pl.when((i == 0) & (s == 0))
        def _():
            dgain_ref[...] = jnp.zeros_like(dgain_ref)

        @pl.when(s == 0)
        def _():
            acc[...] = jnp.zeros_like(acc)

        def add(ref):
            acc[...] += _dot_nt(ref[...], w_ref[...])
        _pick_stack(s, (dc_ref, dl_ref, dg_ref), add)

        @pl.when(s == 6)
        def _():
            n1, r1 = _rms_fwd(x_ref[...])
            d_h = acc[...]
            dgain_ref[...] += jnp.sum(d_h * n1, axis=0, keepdims=True)
            dx_ref[...] = dx1_ref[...] + _rms_bwd(n1, r1, d_h * g_ref[...])

    def spec(m):
        def index(i, s):
            return m(s, 0)[0], i, 0
        return pl.BlockSpec((None, tm, D_MODEL), index)

    row = pl.BlockSpec((tm, D_MODEL), lambda i, s: (i, 0))
    vec = pl.BlockSpec((1, D_MODEL), lambda i, s: (0, 0))
    return pl.pallas_call(
        body, name="in_proj_xgrad", grid=(t // tm, 7),
        out_shape=[jax.ShapeDtypeStruct((t, D_MODEL), F32), jax.ShapeDtypeStruct((1, D_MODEL), F32)],
        in_specs=[spec(m) for m in maps] + [pl.BlockSpec((D_MODEL, D_MODEL), lambda i, s: (0, s)), row, row, vec],
        out_specs=[row, vec],
        scratch_shapes=[pltpu.VMEM((tm, D_MODEL), F32)],
        compiler_params=_params("arbitrary", "arbitrary"),
    )(d_conv, d_lru, d_gate, w_in, x, dx1, g1)


def _add_pair(grad, got, by_cols, pos, name):
    cols = got.shape[-1]
    got3 = got.reshape(4, -1, cols)
    rows = got3.shape[1]
    rb = _row_block(rows, 512)

    def block(k, p):
        return 4 * ((p[0] + k % 2) % 2) + 2 * ((p[1] + k // 2) % 2) + p[2]

    if by_cols:
        g_in, g_spec = grad, pl.BlockSpec((rb, cols), lambda k, i, p: (i, block(k, p)))
    else:
        g_in = grad.reshape(N_DEV, rows, cols)
        g_spec = pl.BlockSpec((None, rb, cols), lambda k, i, p: (block(k, p), i, 0))
    slot = pl.BlockSpec((None, rb, cols), lambda k, i, p: (k, i, 0))

    def body(pos_ref, a_ref, b_ref, o_ref):
        o_ref[...] = (a_ref[...].astype(F32) + b_ref[...].astype(F32)).astype(BF16)

    out = pl.pallas_call(
        body, name=name, out_shape=jax.ShapeDtypeStruct(got3.shape, BF16),
        grid_spec=pltpu.PrefetchScalarGridSpec(num_scalar_prefetch=1, grid=(4, rows // rb),
                                               in_specs=[g_spec, slot], out_specs=slot),
        compiler_params=_params("parallel", "parallel"),
    )(pos, g_in, got3)
    return out.reshape(got.shape)


def _adamw(w, g, m, v):
    m = ADAM_B1 * m + (1.0 - ADAM_B1) * g
    v = ADAM_B2 * v + (1.0 - ADAM_B2) * (g * g)
    m_hat = m / (1.0 - ADAM_B1 ** ADAM_STEP)
    v_hat = v / (1.0 - ADAM_B2 ** ADAM_STEP)
    return -ADAM_LR * (m_hat / (jnp.sqrt(v_hat) + ADAM_EPS) + ADAM_WD * w), m, v


def _adam_large(w, m, v, own, others, name):
    shape = w.shape
    cols = shape[-1]
    w2, m2, v2 = (a.reshape(-1, cols) for a in (w, m, v))
    rows = w2.shape[0]
    own, others = own.reshape(4, rows, cols), others.reshape(3, rows, cols)
    rb = _row_block(rows, 256)

    def body(w_ref, m_ref, v_ref, own_ref, oth_ref, g_ref, d_ref, nm_ref, nv_ref):
        g = own_ref[...].astype(F32)
        for k in range(3):
            g = g + oth_ref[k].astype(F32)
        g_ref[...] = g
        d_ref[...], nm_ref[...], nv_ref[...] = _adamw(w_ref[...], g, m_ref[...], v_ref[...])

    blk = pl.BlockSpec((rb, cols), lambda i: (i, 0))
    res = jax.ShapeDtypeStruct((rows, cols), F32)
    outs = pl.pallas_call(
        body, name=name, grid=(rows // rb,), out_shape=[res] * 4,
        in_specs=[blk, blk, blk, pl.BlockSpec((None, rb, cols), lambda i: (0, i, 0)),
                  pl.BlockSpec((3, rb, cols), lambda i: (0, i, 0))],
        out_specs=[blk] * 4, compiler_params=_params("parallel"),
    )(w2, m2, v2, own, others)
    return [o.reshape(shape) for o in outs]


def _adam_small(ws, gs, ms, vs):
    n = len(ws)

    def body(*refs):
        w_refs, g_refs, m_refs, v_refs = (refs[i * n:(i + 1) * n] for i in range(4))
        outs = refs[4 * n:]
        for i in range(n):
            d, m, v = _adamw(w_refs[i][...], g_refs[i][...], m_refs[i][...], v_refs[i][...])
            outs[i][...], outs[n + i][...], outs[2 * n + i][...] = d, m, v

    shapes = [jax.ShapeDtypeStruct(w.shape, F32) for w in ws]
    outs = pl.pallas_call(
        body, name="adam_small", out_shape=shapes * 3,
        in_specs=[VMEM_SPEC] * (4 * n), out_specs=[VMEM_SPEC] * (3 * n), compiler_params=_params(),
    )(*ws, *gs, *ms, *vs)
    return outs[:n], outs[n:2 * n], outs[2 * n:]


def _pack_rows(pieces):
    tile = SUBLANES * LANES
    return jnp.concatenate([jnp.pad(p.reshape(-1), (0, (-p.size) % tile)).reshape(-1, LANES) for p in pieces], axis=0)


def _packed_starts(sizes):
    tile = SUBLANES * LANES
    starts = [0]
    for s in sizes:
        starts.append(starts[-1] + (s + tile - 1) // tile * SUBLANES)
    return starts


def kernel(x, norm_mix_pre, norm_mix_post, norm_ffn_pre, norm_ffn_post, w_in, conv_short_w, w_conv_branch, lru_conv_w, lru_conv_b, lru_wa, lru_ba, lru_wx, lru_bx, lru_lambda, w_lru_branch, w_out, ffn_w_up, ffn_conv_w, ffn_conv_b, ffn_w_down, loss_target, m_norm_mix_pre, m_norm_mix_post, m_norm_ffn_pre, m_norm_ffn_post, m_w_in, m_conv_short_w, m_w_conv_branch, m_lru_conv_w, m_lru_conv_b, m_lru_wa, m_lru_ba, m_lru_wx, m_lru_bx, m_lru_lambda, m_w_lru_branch, m_w_out, m_ffn_w_up, m_ffn_conv_w, m_ffn_conv_b, m_ffn_w_down, v_norm_mix_pre, v_norm_mix_post, v_norm_ffn_pre, v_norm_ffn_post, v_w_in, v_conv_short_w, v_w_conv_branch, v_lru_conv_w, v_lru_conv_b, v_lru_wa, v_lru_ba, v_lru_wx, v_lru_bx, v_lru_lambda, v_w_lru_branch, v_w_out, v_ffn_w_up, v_ffn_conv_w, v_ffn_conv_b, v_ffn_w_down):
    t = x.shape[1]
    xi, yi, ci = _position()
    me = _block_of(xi, yi, ci)
    x2, target = x[0], loss_target[0]
    shard_in, shard_up = IN_COLS // N_DEV, 2 * D_FF // N_DEV
    shard_sq, shard_down, shard_head = D_MODEL // N_DEV, D_FF // N_DEV, HEAD_DIM // N_DEV

    names = ["w_in", "lru_wa", "lru_wx", "w_conv_branch", "w_lru_branch", "w_out", "ffn_w_up", "ffn_w_down"]
    large = [w_in[0], lru_wa[0], lru_wx[0], w_conv_branch[0], w_lru_branch[0], w_out[0], ffn_w_up[0], ffn_w_down[0]]
    blocks = [_cols(shard_in), _lead, _lead, _rows(shard_sq), _rows(shard_sq), _rows(shard_sq),
              _cols(shard_up), _rows(shard_down)]
    gate_full = (N_DEV, N_HEADS, shard_head, HEAD_DIM)
    full_shapes = [(D_MODEL, IN_COLS), gate_full, gate_full, (D_MODEL, D_MODEL), (D_MODEL, D_MODEL), (D_MODEL, D_MODEL),
                   (D_MODEL, 2 * D_FF), (D_FF, D_MODEL)]
    n_now = 3
    small_sharded = [conv_short_w, lru_conv_w, lru_ba, lru_bx, ffn_conv_w]
    small_mine = _pack_rows(small_sharded)
    small_at = _packed_starts([p.size for p in small_sharded])
    *gathered, small_all = _gather_weights(large, blocks, full_shapes, small_mine, n_now)
    g_in, g_wa, g_wx = gathered[:n_now]
    later_blocks = blocks[n_now:]
    send1, recv1, later, gather_token = _gather_start(gathered[n_now:], later_blocks, "gather_start")

    def behind(token, operand):
        return operand + token[0:1, 0:1]

    def forward(lo, hi, after, tag):
        return _gather_forward(later[lo:hi], later_blocks[lo:hi], send1[4 * lo:4 * hi], recv1[4 * lo:4 * hi], after,
                               "gather_forward_" + tag)

    def finish(lo, hi, flight, after, tag):
        return _gather_finish(flight[2], later_blocks[lo:hi], flight[0], flight[1], after, "gather_finish_" + tag)

    def cols_of(r0, n, width):
        part = small_all[:, r0:r0 + n * width // LANES, :].reshape(N_DEV, n, width)
        return part.transpose(1, 0, 2).reshape(n, N_DEV * width)

    c_short = cols_of(small_at[0], 3, LANES)
    c_lru = cols_of(small_at[1], 4, LANES)
    b_a = cols_of(small_at[2], N_HEADS, shard_head).reshape(1, D_MODEL)
    b_x = cols_of(small_at[3], N_HEADS, shard_head).reshape(1, D_MODEL)
    c_ffn = cols_of(small_at[4], 3, shard_up)

    proj, h = _in_proj(x2, behind(gather_token, norm_mix_pre), g_in)
    flight_mix_w = forward(0, 3, h, "mix")
    y_a = _conv_mixer_fwd(proj, c_short)
    y_b, hl = _lru_fwd(proj, c_lru, lru_conv_b, g_wa, b_a, g_wx, b_x, lru_lambda)
    flight_up_w = forward(3, 4, y_b, "up")
    g_cb, g_lb, g_out = finish(0, 3, flight_mix_w, y_b, "mix")
    pa, pb, merged, mix, x1, h2 = _merge(y_a, y_b, proj, x2, g_cb, g_lb, g_out, norm_mix_post, norm_ffn_pre)
    flight_down_w = forward(4, 5, h2, "down")
    (g_up,) = finish(3, 4, flight_up_w, h2, "up")
    up_g, up_v, f = _ffn_up(h2, g_up, c_ffn, ffn_conv_b)
    (g_down,) = finish(4, 5, flight_down_w, f, "down")
    dy, d_out, d_f, dg4, loss_part = _ffn_down(f, g_down, x1, target, norm_ffn_post)

    block_of = dict(zip(names, blocks))
    shard_shapes = {"w_in": (D_MODEL, shard_in), "w_conv_branch": (shard_sq, D_MODEL), "w_lru_branch": (shard_sq, D_MODEL),
                    "w_out": (shard_sq, D_MODEL), "lru_wa": (N_HEADS, shard_head, HEAD_DIM),
                    "lru_wx": (N_HEADS, shard_head, HEAD_DIM), "ffn_w_up": (D_MODEL, shard_up),
                    "ffn_w_down": (shard_down, D_MODEL)}
    pos = jnp.stack([xi, yi, ci]).astype(jnp.int32)

    def reduce_start(tag, grads):
        keys = list(grads)
        got = _exchange_pair([grads[k] for k in keys], [block_of[k] for k in keys], [shard_shapes[k] for k in keys],
                             "reduce_pair_exchange_" + tag)
        sums = [_add_pair(grads[k], g, k in ("w_in", "ffn_w_up"), pos, "pair_sum_" + k) for k, g in zip(keys, got)]
        return (keys,) + _exchange_chips_start(sums, "reduce_chip_start_" + tag)

    gw_down = _grad_tn(f, d_out, min(512, D_FF), "ffn_down_wgrad")
    flight_down = reduce_start("down", {"ffn_w_down": gw_down})
    gw_up, gc_ffn, gb_ffn, d_h2 = _ffn_up_bwd(up_g, up_v, d_f, c_ffn, behind(flight_down[-1], ffn_conv_b), h2, g_up)
    flight_up = reduce_start("up", {"ffn_w_up": gw_up})
    dx1, d_mix, d_pa, d_pb, d_ya, d_yb, d_gate, dg3, dg2 = _merge_bwd(
        dy, d_h2, x1, mix, behind(flight_up[-1], norm_ffn_pre), norm_mix_post, g_out, g_cb, g_lb, pa, pb, proj)
    gw_out = _grad_tn(merged, d_mix, CB, "w_out_wgrad")
    gw_cb = _grad_tn(y_a, d_pa, CB, "w_conv_branch_wgrad")
    gw_lb = _grad_tn(y_b, d_pb, CB, "w_lru_branch_wgrad")
    flight_mix = reduce_start("mix", {"w_conv_branch": gw_cb, "w_lru_branch": gw_lb, "w_out": gw_out})
    d_conv, gc_short = _conv_mixer_bwd(proj, d_ya, behind(flight_mix[-1], c_short))
    d_lru, gw_a, gw_x, g_lru_small = _lru_bwd(proj, hl, d_yb, c_lru, lru_conv_b, g_wa, b_a, g_wx, b_x, lru_lambda)
    gw_in = _in_proj_wgrad(h, d_conv, d_lru, d_gate)
    flight_in = reduce_start("in", {"lru_wa": gw_a, "lru_wx": gw_x, "w_in": gw_in})
    dx, dg1 = _in_proj_xgrad(d_conv, d_lru, d_gate, g_in, x2, dx1, behind(flight_in[-1], norm_mix_pre))

    moments ={"w_in": (m_w_in, v_w_in), "w_conv_branch": (m_w_conv_branch, v_w_conv_branch),
               "w_lru_branch": (m_w_lru_branch, v_w_lru_branch), "w_out": (m_w_out, v_w_out),
               "lru_wa": (m_lru_wa, v_lru_wa), "lru_wx": (m_lru_wx, v_lru_wx), "ffn_w_up": (m_ffn_w_up, v_ffn_w_up),
               "ffn_w_down": (m_ffn_w_down, v_ffn_w_down)}
    weights = {"w_in": w_in, "w_conv_branch": w_conv_branch, "w_lru_branch": w_lru_branch, "w_out": w_out,
               "lru_wa": lru_wa, "lru_wx": lru_wx, "ffn_w_up": ffn_w_up, "ffn_w_down": ffn_w_down}
    out_g, out_d, out_m, out_v = {}, {}, {}, {}

    pieces = [dg1, dg2, dg3, dg4, g_lru_small[4:5], g_lru_small[7:8], gb_ffn, gc_short, g_lru_small[0:4],
              g_lru_small[5:6], g_lru_small[6:7], gc_ffn, loss_part]
    total = _allreduce_small(_pack_rows(pieces))
    sizes = [p.size for p in pieces]
    starts = _packed_starts(sizes)

    def piece(i, shape):
        return total[starts[i]:starts[i + 1]].reshape(-1)[:sizes[i]].reshape(shape)

    loss = total[starts[12], 0]

    def col_shard(full, width):
        return lax.dynamic_slice_in_dim(full, me * width, width, axis=1)

    def head_shard(full):
        return lax.dynamic_slice_in_dim(full.reshape(N_HEADS, HEAD_DIM), me * shard_head, shard_head, axis=1)

    small_names = ["norm_mix_pre", "norm_mix_post", "norm_ffn_pre", "norm_ffn_post", "lru_conv_b", "lru_lambda",
                   "ffn_conv_b", "conv_short_w", "lru_conv_w", "lru_ba", "lru_bx", "ffn_conv_w"]
    small_g = [piece(0, (1, D_MODEL)), piece(1, (1, D_MODEL)), piece(2, (1, D_MODEL)), piece(3, (1, D_MODEL)),
               piece(4, (1, D_MODEL)), piece(5, (1, D_MODEL)), piece(6, (1, 2 * D_FF)),
               col_shard(piece(7, (3, D_MODEL)), LANES), col_shard(piece(8, (4, D_MODEL)), LANES),
               head_shard(piece(9, (1, D_MODEL))), head_shard(piece(10, (1, D_MODEL))),
               col_shard(piece(11, (3, 2 * D_FF)), shard_up)]
    small_w = [norm_mix_pre, norm_mix_post, norm_ffn_pre, norm_ffn_post, lru_conv_b, lru_lambda, ffn_conv_b,
               conv_short_w[0], lru_conv_w[0], lru_ba[0], lru_bx[0], ffn_conv_w[0]]
    small_m = [m_norm_mix_pre, m_norm_mix_post, m_norm_ffn_pre, m_norm_ffn_post, m_lru_conv_b, m_lru_lambda,
               m_ffn_conv_b, m_conv_short_w[0], m_lru_conv_w[0], m_lru_ba[0], m_lru_bx[0], m_ffn_conv_w[0]]
    small_v = [v_norm_mix_pre, v_norm_mix_post, v_norm_ffn_pre, v_norm_ffn_post, v_lru_conv_b, v_lru_lambda,
               v_ffn_conv_b, v_conv_short_w[0], v_lru_conv_w[0], v_lru_ba[0], v_lru_bx[0], v_ffn_conv_w[0]]
    s_d, s_m, s_v = _adam_small(small_w, small_g, small_m, small_v)
    for i, name in enumerate(small_names):
        shape = small_w[i].shape if i < 7 else (1,) + small_w[i].shape
        out_g[name] = small_g[i].reshape(shape)
        out_d[name], out_m[name], out_v[name] = s_d[i].reshape(shape), s_m[i].reshape(shape), s_v[i].reshape(shape)

    after = s_d[0]
    for tag, (keys, send, recv, sums, lands, _) in (("down", flight_down), ("up", flight_up), ("mix", flight_mix),
                                                    ("in", flight_in)):
        sums, others = _exchange_chips_wait(send, recv, sums, lands, after, "reduce_chip_wait_" + tag)
        for k, own, oth in zip(keys, sums, others):
            out_g[k], out_d[k], out_m[k], out_v[k] = _adam_large(weights[k], *moments[k], own, oth, "adam_" + k)
        after = out_d[keys[-1]]

    order = ["norm_mix_pre", "norm_mix_post", "norm_ffn_pre", "norm_ffn_post", "w_in", "conv_short_w", "w_conv_branch",
             "lru_conv_w", "lru_conv_b", "lru_wa", "lru_ba", "lru_wx", "lru_bx", "lru_lambda", "w_lru_branch", "w_out",
             "ffn_w_up", "ffn_conv_w", "ffn_conv_b", "ffn_w_down"]
    return (loss, dx.reshape(1, t, D_MODEL), *[out_g[k] for k in order], *[out_d[k] for k in order],
            *[out_m[k] for k in order], *[out_v[k] for k in order])
```

```python
import functools
import math

import jax
import jax.numpy as jnp
from jax import lax
from jax.experimental import pallas as pl
from jax.experimental.pallas import tpu as pltpu

F32 = jnp.float32
BF16 = jnp.bfloat16
MESH = pl.DeviceIdType.MESH

N_DEV = 8
D_MODEL = 1024
N_HEADS = 4
HEAD_DIM = D_MODEL // N_HEADS
D_FF = 3 * D_MODEL
IN_COLS = 7 * D_MODEL
LRU_C = 8.0
RMS_EPS = 1e-6
ADAM_LR = 0.001
ADAM_B1 = 0.9
ADAM_B2 = 0.999
ADAM_EPS = 1e-08
ADAM_WD = 0.01
ADAM_STEP = 10
GELU_K = math.sqrt(2.0 / math.pi)
GELU_C = 0.044715

LANES = 128
SUBLANES = 8
PAD = SUBLANES
VMEM_LIMIT = 56 * 1024 * 1024
CB = 256

HBM_SPEC = pl.BlockSpec(memory_space=pltpu.HBM)
SEM_SPEC = pl.BlockSpec(memory_space=pltpu.SEMAPHORE)
DATAFLOW_EFFECT = pltpu.SideEffectType.DATAFLOW_SIDE_EFFECTING
VMEM_SPEC = pl.BlockSpec(memory_space=pltpu.VMEM)


def _params(*sem):
    if sem:
        return pltpu.CompilerParams(dimension_semantics=sem, vmem_limit_bytes=VMEM_LIMIT)
    return pltpu.CompilerParams(vmem_limit_bytes=VMEM_LIMIT)


def _row_chunk(t):
    return min(256, t)


def _row_block(rows, cap):
    return next(rb for rb in range(min(cap, rows), 0, -16) if rows % rb == 0)


def _gelu(x):
    return 0.5 * x * (1.0 + jnp.tanh(GELU_K * (x + GELU_C * x * x * x)))


def _gelu_and_grad(x):
    t = jnp.tanh(GELU_K * (x + GELU_C * x * x * x))
    g = 0.5 * x * (1.0 + t)
    dg = 0.5 * (1.0 + t) + 0.5 * x * (1.0 - t * t) * GELU_K * (1.0 + 3.0 * GELU_C * x * x)
    return g, dg


def _expm1_neg(x):
    series = x * (1.0 + x * (0.5 + x * (1.0 / 6.0 + x * (1.0 / 24.0 + x * (1.0 / 120.0)))))
    return jnp.where(x > -0.05, series, jnp.exp(x) - 1.0)


def _log_sigmoid(x):
    return jnp.minimum(x, 0.0) - jnp.log1p(jnp.exp(-jnp.abs(x)))


def _dot(a, b):
    return jnp.dot(a, b, preferred_element_type=F32)


def _dot_nt(a, b):
    return lax.dot_general(a, b, (((1,), (1,)), ((), ())), preferred_element_type=F32)


def _dot_tn(a, b):
    return lax.dot_general(a, b, (((0,), (0,)), ((), ())), preferred_element_type=F32)


def _rms_fwd(x):
    r = lax.rsqrt(jnp.mean(x * x, axis=-1, keepdims=True) + RMS_EPS)
    return x * r, r


def _rms_bwd(n, r, gdy):
    return r * (gdy - n * jnp.mean(n * gdy, axis=-1, keepdims=True))


def _conv_causal(pad_ref, w, r0, rows, taps):
    acc = None
    for k in range(taps):
        term = w[k:k + 1, :] * pad_ref[pl.ds(PAD + r0 - (taps - 1 - k), rows), :]
        acc = term if acc is None else acc + term
    return acc


def _conv_anticausal(pad_ref, w, r0, rows, taps):
    acc = None
    for k in range(taps):
        term = w[k:k + 1, :] * pad_ref[pl.ds(r0 + (taps - 1 - k), rows), :]
        acc = term if acc is None else acc + term
    return acc


def _conv_wgrad(g, xpad_ref, r0, rows, taps):
    return [jnp.sum(g * xpad_ref[pl.ds(PAD + r0 - (taps - 1 - k), rows), :], axis=0, keepdims=True)
            for k in range(taps)]


def _position():
    return lax.axis_index("x"), lax.axis_index("y"), lax.axis_index("c")


def _block_of(x, y, c):
    return 4 * x + 2 * y + c


def _chip(x, y, k):
    return (x + (k & 1)) % 2, (y + (k >> 1)) % 2


def _cols(width):
    def at(ref, d):
        return ref.at[:, pl.ds(pl.multiple_of(d * width, LANES), width)]
    return at


def _rows(height):
    def at(ref, d):
        return ref.at[pl.ds(pl.multiple_of(d * height, 16), height), :]
    return at


def _lead(ref, d):
    return ref.at[d]


def _gather_weights(shards, blocks, full_shapes, small, n_now):
    n = len(shards)
    small_rows = small.shape[0]

    def body(*refs):
        ins, small_in = refs[:n], refs[n]
        outs, small_out = refs[n + 1:2 * n + 1], refs[2 * n + 1]
        stage = refs[2 * n + 2:3 * n + 2]
        send, recv, local = refs[3 * n + 2:]
        x, y, c = _position()
        me = _block_of(x, y, c)
        sibling = (x, y, 1 - c)

        for a in range(n):
            stage[a][...] = ins[a][...].astype(BF16)

        def copy(a, k, block, to, src=None):
            dst = blocks[a](outs[a], block)
            return pltpu.make_async_remote_copy(
                src_ref=dst if src is None else src, dst_ref=dst, send_sem=send.at[a, k], recv_sem=recv.at[a, k],
                device_id=to, device_id_type=MESH)

        def small_copy(k):
            px, py, pc = (x + (k & 1)) % 2, (y + ((k >> 1) & 1)) % 2, (c + (k >> 2)) % 2
            return pltpu.make_async_remote_copy(
                src_ref=small_in, dst_ref=small_out.at[me], send_sem=send.at[n_now, k - 1], recv_sem=recv.at[n_now, k - 1],
                device_id=(px, py, pc), device_id_type=MESH)

        def small_arrival(k):
            px, py, pc = (x + (k & 1)) % 2, (y + ((k >> 1) & 1)) % 2, (c + (k >> 2)) % 2
            return pltpu.make_async_remote_copy(
                src_ref=small_in, dst_ref=small_out.at[_block_of(px, py, pc)], send_sem=send.at[n_now, k - 1],
                recv_sem=recv.at[n_now, k - 1], device_id=(px, py, pc), device_id_type=MESH)

        small_out[me] = small_in[...]
        small_sends = [small_copy(k) for k in range(1, N_DEV)]
        for cp in small_sends:
            cp.start()

        mine, first, passed = [], [], []
        for a in range(n):
            own = pltpu.make_async_copy(stage[a], blocks[a](outs[a], me), local.at[a])
            own.start()
            mine.append(own)
            if a >= n_now:
                continue
            sends = [copy(a, 0, me, sibling, src=stage[a])]
            sends += [copy(a, k, me, (*_chip(x, y, k), c), src=stage[a]) for k in (1, 2, 3)]
            for cp in sends:
                cp.start()
            first += sends
        for a in range(n_now):
            for k in (1, 2, 3):
                landed = _block_of(*_chip(x, y, k), c)
                copy(a, k, landed, (x, y, c)).wait_recv()
                fwd = copy(a, 3 + k, landed, sibling)
                fwd.start()
                passed.append(fwd)
        for a in range(n_now):
            copy(a, 0, _block_of(x, y, 1 - c), (x, y, c)).wait_recv()
            for k in (1, 2, 3):
                copy(a, 3 + k, _block_of(*_chip(x, y, k), 1 - c), (x, y, c)).wait_recv()
        for k in range(1, N_DEV):
            small_arrival(k).wait_recv()
        for cp in first + passed + small_sends:
            cp.wait_send()
        for own in mine:
            own.wait()

    out_shape = [jax.ShapeDtypeStruct(s, BF16) for s in full_shapes]
    out_shape.append(jax.ShapeDtypeStruct((N_DEV, small_rows, LANES), F32))
    return pl.pallas_call(
        body, name="gather_weights", out_shape=out_shape,
        in_specs=[VMEM_SPEC] * (n + 1), out_specs=[HBM_SPEC] * n + [VMEM_SPEC],
        scratch_shapes=[pltpu.VMEM(s.shape, BF16) for s in shards]
        + [pltpu.SemaphoreType.DMA((n_now + 1, 7)), pltpu.SemaphoreType.DMA((n_now + 1, 7)),
           pltpu.SemaphoreType.DMA((n,))],
        compiler_params=_params(),
    )(*shards, small)


def _gather_first(full, blocks, send, recv):
    x, y, c = _position()
    me = _block_of(x, y, c)
    peers = [(x, y, 1 - c)] + [(*_chip(x, y, k), c) for k in (1, 2, 3)]

    def copy(a, k, block):
        at = blocks[a](full[a], block)
        return pltpu.make_async_remote_copy(src_ref=at, dst_ref=at, send_sem=send[4 * a + k], recv_sem=recv[4 * a + k],
                                            device_id=peers[k], device_id_type=MESH)

    sends = [copy(a, k, me) for a in range(len(full)) for k in range(4)]
    arrivals = [copy(a, k, _block_of(*peers[k])) for a in range(len(full)) for k in range(4)]
    return sends, arrivals


def _gather_second(full, blocks, send, recv):
    x, y, c = _position()

    def copy(a, k, cc):
        at = blocks[a](full[a], _block_of(*_chip(x, y, k), cc))
        return pltpu.make_async_remote_copy(src_ref=at, dst_ref=at, send_sem=send[3 * a + k - 1],
                                            recv_sem=recv[3 * a + k - 1], device_id=(x, y, 1 - c), device_id_type=MESH)

    sends = [copy(a, k, c) for a in range(len(full)) for k in (1, 2, 3)]
    arrivals = [copy(a, k, 1 - c) for a in range(len(full)) for k in (1, 2, 3)]
    return sends, arrivals


def _split_call(body, name, arrays, sems_in, n_sems_out, after=None, token=False):
    n, m = len(arrays), len(sems_in)

    def kernel_body(*refs):
        outs = refs[n + m + (after is not None):]
        body(refs[:n], refs[n:n + m], outs[:n_sems_out])
        if token:
            outs[-1][...] = jnp.zeros_like(outs[-1])

    extra_in = [] if after is None else [after]
    outs = pl.pallas_call(
        kernel_body, name=name,
        out_shape=(*[pltpu.SemaphoreType.DMA(())] * n_sems_out, *[pltpu.HBM(a.shape, a.dtype) for a in arrays],
                   *([jax.ShapeDtypeStruct((SUBLANES, LANES), F32)] if token else [])),
        in_specs=[HBM_SPEC] * n + [SEM_SPEC] * m + [pl.BlockSpec(memory_space=pl.ANY)] * len(extra_in),
        out_specs=(*[SEM_SPEC] * n_sems_out, *[HBM_SPEC] * n, *([VMEM_SPEC] if token else [])),
        input_output_aliases={i: n_sems_out + i for i in range(n)},
        compiler_params=pltpu.CompilerParams(has_side_effects=DATAFLOW_EFFECT),
    )(*[pltpu.with_memory_space_constraint(a, pltpu.HBM) for a in arrays], *sems_in, *extra_in)
    sems, rest = list(outs[:n_sems_out]), list(outs[n_sems_out:])
    return (sems, rest[:n], rest[n]) if token else (sems, rest[:n])


def _gather_start(full, blocks, name):
    n = len(full)

    def body(arrays, _, sems):
        for cp in _gather_first(arrays, blocks, sems[:4 * n], sems[4 * n:])[0]:
            cp.start()

    sems, arrays, token = _split_call(body, name, full, [], 8 * n, token=True)
    return sems[:4 * n], sems[4 * n:], arrays, token


def _gather_forward(full, blocks, send_first, recv_first, after, name):
    n = len(full)

    def body(arrays, sems_in, sems):
        sends, arrivals = _gather_first(arrays, blocks, sems_in[:4 * n], sems_in[4 * n:])
        for cp in arrivals:
            cp.wait_recv()
        for cp in _gather_second(arrays, blocks, sems[:3 * n], sems[3 * n:])[0]:
            cp.start()
        for cp in sends:
            cp.wait_send()

    sems, arrays = _split_call(body, name, full, [*send_first, *recv_first], 6 * n, after=after)
    return sems[:3 * n], sems[3 * n:], arrays


def _gather_finish(full, blocks, send_second, recv_second, after, name):
    n = len(full)

    def body(arrays, sems_in, _):
        sends, arrivals = _gather_second(arrays, blocks, sems_in[:3 * n], sems_in[3 * n:])
        for cp in sends:
            cp.wait_send()
        for cp in arrivals:
            cp.wait_recv()

    return _split_call(body, name, full, [*send_second, *recv_second], 0, after=after)[1]


def _exchange_pair(grads, blocks, shard_shapes, name):
    n = len(grads)

    def body(*refs):
        ins, got = refs[:n], refs[n:2 * n]
        send, recv = refs[2 * n:]
        x, y, c = _position()
        copies = []
        for a in range(n):
            for k in range(4):
                cp = pltpu.make_async_remote_copy(
                    src_ref=blocks[a](ins[a], _block_of(*_chip(x, y, k), 1 - c)), dst_ref=got[a].at[k],
                    send_sem=send.at[a, k], recv_sem=recv.at[a, k], device_id=(x, y, 1 - c), device_id_type=MESH)
                cp.start()
                copies.append(cp)
        for cp in copies:
            cp.wait()

    return pl.pallas_call(
        body, name=name, out_shape=[jax.ShapeDtypeStruct((4,) + tuple(s), BF16) for s in shard_shapes],
        in_specs=[HBM_SPEC] * n, out_specs=[HBM_SPEC] * n,
        scratch_shapes=[pltpu.SemaphoreType.DMA((n, 4)), pltpu.SemaphoreType.DMA((n, 4))],
        compiler_params=_params(),
    )(*grads)


def _chip_copies(sums, lands, send, recv):
    x, y, c = _position()
    return [pltpu.make_async_remote_copy(
        src_ref=sums[a].at[k], dst_ref=lands[a].at[k - 1], send_sem=send[3 * a + k - 1], recv_sem=recv[3 * a + k - 1],
        device_id=(*_chip(x, y, k), c), device_id_type=MESH) for a in range(len(sums)) for k in (1, 2, 3)]


def _exchange_chips_start(pair_sums, name):
    n = len(pair_sums)
    lands = [pltpu.with_memory_space_constraint(lax.empty((3,) + tuple(p.shape[1:]), BF16), pltpu.HBM) for p in pair_sums]

    def body(*refs):
        sums, zones = refs[:n], refs[n:2 * n]
        send, recv = refs[2 * n:5 * n], refs[5 * n:8 * n]
        token = refs[-1]
        for cp in _chip_copies(sums, zones, send, recv):
            cp.start()
        token[...] = jnp.zeros_like(token)

    outs = pl.pallas_call(
        body, name=name,
        out_shape=(*[pltpu.SemaphoreType.DMA(())] * (6 * n),
                   *[pltpu.HBM(p.shape, BF16) for p in pair_sums], *[pltpu.HBM(z.shape, BF16) for z in lands],
                   jax.ShapeDtypeStruct((SUBLANES, LANES), F32)),
        in_specs=[HBM_SPEC] * (2 * n), out_specs=(*[SEM_SPEC] * (6 * n), *[HBM_SPEC] * (2 * n), VMEM_SPEC),
        input_output_aliases={i: 6 * n + i for i in range(2 * n)},
        compiler_params=pltpu.CompilerParams(has_side_effects=DATAFLOW_EFFECT),
    )(*[pltpu.with_memory_space_constraint(p, pltpu.HBM) for p in pair_sums], *lands)
    return outs[:3 * n], outs[3 * n:6 * n], outs[6 * n:7 * n], outs[7 * n:8 * n], outs[-1]


def _exchange_chips_wait(send, recv, sums, lands, after, name):
    n = len(sums)

    def body(*refs):
        sums_in, zones = refs[:n], refs[n:2 * n]
        send_in, recv_in = refs[2 * n:5 * n], refs[5 * n:8 * n]
        for cp in _chip_copies(sums_in, zones, send_in, recv_in):
            cp.wait_send()
            cp.wait_recv()

    outs = pl.pallas_call(
        body, name=name,
        out_shape=(*[pltpu.HBM(p.shape, BF16) for p in sums], *[pltpu.HBM(z.shape, BF16) for z in lands]),
        in_specs=[HBM_SPEC] * (2 * n) + [SEM_SPEC] * (6 * n) + [pl.BlockSpec(memory_space=pl.ANY)],
        out_specs=[HBM_SPEC] * (2 * n), input_output_aliases={i: i for i in range(2 * n)},
        compiler_params=pltpu.CompilerParams(has_side_effects=DATAFLOW_EFFECT),
    )(*sums, *lands, *send, *recv, after)
    return outs[:n], outs[n:]


def _allreduce_small(part):
    rows = part.shape[0]

    def body(in_ref, out_ref, buf, send, recv):
        x, y, c = _position()
        me = _block_of(x, y, c)

        def peer(k):
            return (x + (k & 1)) % 2, (y + ((k >> 1) & 1)) % 2, (c + (k >> 2)) % 2

        sends = []
        for k in range(1, N_DEV):
            cp = pltpu.make_async_remote_copy(src_ref=in_ref, dst_ref=buf.at[me], send_sem=send.at[k - 1],
                                              recv_sem=recv.at[k - 1], device_id=peer(k), device_id_type=MESH)
            cp.start()
            sends.append(cp)
        buf[me] = in_ref[...]
        for k in range(1, N_DEV):
            pltpu.make_async_remote_copy(src_ref=in_ref, dst_ref=buf.at[_block_of(*peer(k))], send_sem=send.at[k - 1],
                                         recv_sem=recv.at[k - 1], device_id=peer(k), device_id_type=MESH).wait_recv()
        total = buf[0]
        for d in range(1, N_DEV):
            total = total + buf[d]
        out_ref[...] = total
        for cp in sends:
            cp.wait_send()

    return pl.pallas_call(
        body, name="allreduce_small", out_shape=jax.ShapeDtypeStruct(part.shape, F32),
        in_specs=[VMEM_SPEC], out_specs=VMEM_SPEC,
        scratch_shapes=[pltpu.VMEM((N_DEV, rows, LANES), F32), pltpu.SemaphoreType.DMA((7,)), pltpu.SemaphoreType.DMA((7,))],
        compiler_params=_params(),
    )(part)


def _in_proj(x, g1, w_in):
    t = x.shape[0]
    tm, bn = min(512, t), 1024

    def body(x_ref, g_ref, w_ref, proj_ref, h_ref, h_s):
        @pl.when(pl.program_id(1) == 0)
        def _():
            n, _ = _rms_fwd(x_ref[...])
            h_s[...] = (n * g_ref[...]).astype(BF16)
            h_ref[...] = h_s[...]
        proj_ref[...] = _dot(h_s[...], w_ref[...]).astype(BF16)

    return pl.pallas_call(
        body, name="in_proj", grid=(t // tm, IN_COLS // bn),
        out_shape=[jax.ShapeDtypeStruct((t, IN_COLS), BF16), jax.ShapeDtypeStruct((t, D_MODEL), BF16)],
        in_specs=[pl.BlockSpec((tm, D_MODEL), lambda i, j: (i, 0)), pl.BlockSpec((1, D_MODEL), lambda i, j: (0, 0)),
                  pl.BlockSpec((D_MODEL, bn), lambda i, j: (0, j))],
        out_specs=[pl.BlockSpec((tm, bn), lambda i, j: (i, j)), pl.BlockSpec((tm, D_MODEL), lambda i, j: (i, 0))],
        scratch_shapes=[pltpu.VMEM((tm, D_MODEL), BF16)],
        compiler_params=_params("parallel", "arbitrary"),
    )(x, g1, w_in)


def _section(s, t):
    return pl.BlockSpec((t, CB), lambda h, s=s: (0, s * (D_MODEL // CB) + h))


def _conv_mixer_fwd(proj, w_short):
    t = proj.shape[0]
    rc = _row_chunk(t)

    def body(b_ref, c_ref, x_ref, w_ref, y_ref, pad):
        pad[pl.ds(0, PAD), :] = jnp.zeros((PAD, CB), F32)
        for r0 in range(0, t, rc):
            rows = pl.ds(r0, rc)
            pad[pl.ds(PAD + r0, rc), :] = c_ref[rows, :].astype(F32) * x_ref[rows, :].astype(F32)
        w = w_ref[...]
        for r0 in range(0, t, rc):
            rows = pl.ds(r0, rc)
            y_ref[rows, :] = (b_ref[rows, :].astype(F32) * _conv_causal(pad, w, r0, rc, 3)).astype(BF16)

    return pl.pallas_call(
        body, name="conv_mixer_fwd", grid=(D_MODEL // CB,),
        out_shape=jax.ShapeDtypeStruct((t, D_MODEL), BF16),
        in_specs=[_section(0, t), _section(1, t), _section(2, t), pl.BlockSpec((3, CB), lambda h: (0, h))],
        out_specs=pl.BlockSpec((t, CB), lambda h: (0, h)),
        scratch_shapes=[pltpu.VMEM((t + PAD, CB), F32)],
        compiler_params=_params("parallel"),
    )(proj, proj, proj, w_short)


def _lru_gates(xl, wa, ba, wx, bx, ls, first_row):
    xb = xl.astype(BF16)
    ra = jax.nn.sigmoid(_dot(xb, wa) + ba)
    ia = jax.nn.sigmoid(_dot(xb, wx) + bx)
    la = LRU_C * ra * ls
    a = jnp.exp(la)
    one_minus = -_expm1_neg(2.0 * la)
    mult = jnp.where(first_row, 1.0, jnp.sqrt(one_minus))
    return xb, ra, ia, a, one_minus, mult


def _head_specs():
    vec = pl.BlockSpec((1, CB), lambda h: (0, h))
    mat = pl.BlockSpec((N_DEV, None, HEAD_DIM // N_DEV, HEAD_DIM), lambda h: (0, h, 0, 0))
    return vec, mat


def _lru_fwd(proj, w_conv, b_conv, wa, ba, wx, bx, lam):
    t = proj.shape[0]
    rc = _row_chunk(t)
    vec, mat = _head_specs()

    def body(lx_ref, ly_ref, wc_ref, bc_ref, wa_ref, ba_ref, wx_ref, bx_ref, lam_ref, yb_ref, hl_ref, pad, a_s, u_s):
        pad[pl.ds(0, PAD), :] = jnp.zeros((PAD, CB), F32)
        for r0 in range(0, t, rc):
            pad[pl.ds(PAD + r0, rc), :] = lx_ref[pl.ds(r0, rc), :].astype(F32)
        wc, bc = wc_ref[...], bc_ref[...]
        wa_m, wx_m = wa_ref[...].reshape(HEAD_DIM, HEAD_DIM), wx_ref[...].reshape(HEAD_DIM, HEAD_DIM)
        ls = _log_sigmoid(lam_ref[...])
        for r0 in range(0, t, rc):
            xl = _conv_causal(pad, wc, r0, rc, 4) + bc
            first = (lax.broadcasted_iota(jnp.int32, (rc, CB), 0) + r0) == 0
            _, _, ia, a, _, mult = _lru_gates(xl, wa_m, ba_ref[...], wx_m, bx_ref[...], ls, first)
            a_s[pl.ds(r0, rc), :] = a
            u_s[pl.ds(r0, rc), :] = mult * (ia * xl)

        row = lax.broadcasted_iota(jnp.int32, (SUBLANES, CB), 0)

        def group(g, carry):
            r = pl.multiple_of(g * SUBLANES, SUBLANES)
            a_g, b_g = a_s[pl.ds(r, SUBLANES), :], u_s[pl.ds(r, SUBLANES), :]
            for s in (1, 2, 4):
                keep = row >= s
                b_g = jnp.where(keep, a_g * pltpu.roll(b_g, s, 0) + b_g, b_g)
                a_g = jnp.where(keep, a_g * pltpu.roll(a_g, s, 0), a_g)
            h_g = b_g + a_g * carry
            hl_ref[pl.ds(r, SUBLANES), :] = h_g
            return jnp.broadcast_to(h_g[SUBLANES - 1:SUBLANES, :], (SUBLANES, CB))

        lax.fori_loop(0, t // SUBLANES, group, jnp.zeros((SUBLANES, CB), F32))
        for r0 in range(0, t, rc):
            rows = pl.ds(r0, rc)
            yb_ref[rows, :] = (hl_ref[rows, :] * _gelu(ly_ref[rows, :].astype(F32))).astype(BF16)

    blk = pl.BlockSpec((t, CB), lambda h: (0, h))
    return pl.pallas_call(
        body, name="lru_fwd", grid=(N_HEADS,),
        out_shape=[jax.ShapeDtypeStruct((t, D_MODEL), BF16), jax.ShapeDtypeStruct((t, D_MODEL), F32)],
        in_specs=[_section(3, t), _section(4, t), pl.BlockSpec((4, CB), lambda h: (0, h)), vec, mat, vec, mat, vec, vec],
        out_specs=[blk, blk],
        scratch_shapes=[pltpu.VMEM((t + PAD, CB), F32), pltpu.VMEM((t, CB), F32), pltpu.VMEM((t, CB), F32)],
        compiler_params=_params("parallel"),
    )(proj, proj, w_conv, b_conv, wa, ba, wx, bx, lam)


def _merge(y_a, y_b, proj, x, w_cb, w_lb, w_out, g2, g3):
    t = x.shape[0]
    tm = min(256, t)

    def body(ya_ref, yb_ref, gc_ref, gl_ref, x_ref, wcb_ref, wlb_ref, wo_ref, g2_ref, g3_ref,
             pa_ref, pb_ref, mg_ref, mix_ref, x1_ref, h2_ref):
        pa = _dot(ya_ref[...], wcb_ref[...]).astype(BF16)
        pb = _dot(yb_ref[...], wlb_ref[...]).astype(BF16)
        pa_ref[...] = pa
        pb_ref[...] = pb
        merged = (jax.nn.sigmoid(gc_ref[...].astype(F32)) * pa.astype(F32)
                  + jax.nn.sigmoid(gl_ref[...].astype(F32)) * pb.astype(F32)).astype(BF16)
        mg_ref[...] = merged
        mix = _dot(merged, wo_ref[...])
        mix_ref[...] = mix
        n2, _ = _rms_fwd(mix)
        x1 = x_ref[...] + n2 * g2_ref[...]
        x1_ref[...] = x1
        n3, _ = _rms_fwd(x1)
        h2_ref[...] = (n3 * g3_ref[...]).astype(BF16)

    row = pl.BlockSpec((tm, D_MODEL), lambda i: (i, 0))
    full = pl.BlockSpec((D_MODEL, D_MODEL), lambda i: (0, 0))
    vec = pl.BlockSpec((1, D_MODEL), lambda i: (0, 0))
    act = jax.ShapeDtypeStruct((t, D_MODEL), BF16)
    res = jax.ShapeDtypeStruct((t, D_MODEL), F32)
    return pl.pallas_call(
        body, name="merge_fwd", grid=(t // tm,), out_shape=[act, act, act, res, res, act],
        in_specs=[row, row, pl.BlockSpec((tm, D_MODEL), lambda i: (i, 5)), pl.BlockSpec((tm, D_MODEL), lambda i: (i, 6)),
                  row, full, full, full, vec, vec],
        out_specs=[row] * 6,
        compiler_params=_params("parallel"),
    )(y_a, y_b, proj, proj, x, w_cb, w_lb, w_out, g2, g3)


N_FF_BLOCKS = D_FF // CB


def _ffn_up(h2, w_up, w_conv, b_conv):
    t = h2.shape[0]
    rc = _row_chunk(t)
    nb = N_FF_BLOCKS

    def body(h_ref, w_ref, c_ref, b_ref, up_ref, act_ref, f_ref, pad, gate):
        k = pl.program_id(1)
        pad[pl.ds(0, PAD), :] = jnp.zeros((PAD, CB), F32)
        for r0 in range(0, t, rc):
            rows = pl.ds(r0, rc)
            up = _dot(h_ref[rows, :], w_ref[...]).astype(BF16)
            up_ref[rows, :] = up
            pad[pl.ds(PAD + r0, rc), :] = up.astype(F32)
        cw = c_ref[...]
        for r0 in range(0, t, rc):
            rows = pl.ds(r0, rc)
            act = _conv_causal(pad, cw, r0, rc, 3) + b_ref[...]
            act_ref[rows, :] = act.astype(BF16)

            @pl.when(k == 0)
            def _():
                gate[rows, :] = act

            @pl.when(k == 1)
            def _():
                f_ref[rows, :] = (_gelu(gate[rows, :]) * act).astype(BF16)

    half = lambda rows: pl.BlockSpec((rows, CB), lambda j, k: (0, nb * k + j))
    wide = jax.ShapeDtypeStruct((t, 2 * D_FF), BF16)
    return pl.pallas_call(
        body, name="ffn_up_fwd", grid=(nb, 2), out_shape=[wide, wide, jax.ShapeDtypeStruct((t, D_FF), BF16)],
        in_specs=[pl.BlockSpec((t, D_MODEL), lambda j, k: (0, 0)), half(D_MODEL), half(3), half(1)],
        out_specs=[half(t), half(t), pl.BlockSpec((t, CB), lambda j, k: (0, j))],
        scratch_shapes=[pltpu.VMEM((t + PAD, CB), F32), pltpu.VMEM((t, CB), F32)],
        compiler_params=_params("parallel", "arbitrary"),
    )(h2, w_up, w_conv, b_conv)


def _ffn_down(f, act, w_down, x1, target, g4):
    t = f.shape[0]
    tm = min(256, t)
    cc = 512

    def body(f_ref, act_ref, w_ref, x1_ref, tg_ref, g_ref, dy_ref, dout_ref, back_ref, dg_ref, loss_ref):
        @pl.when(pl.program_id(0) == 0)
        def _():
            dg_ref[...] = jnp.zeros_like(dg_ref)
            loss_ref[...] = jnp.zeros_like(loss_ref)
        out = _dot(f_ref[...], w_ref[...])
        n4, r4 = _rms_fwd(out)
        err = x1_ref[...] + n4 * g_ref[...] - tg_ref[...]
        loss_ref[...] += jnp.full(loss_ref.shape, 0.5 / D_MODEL, F32) * jnp.sum(err * err)
        dy = err * (1.0 / D_MODEL)
        dy_ref[...] = dy
        dg_ref[...] += jnp.sum(dy * n4, axis=0, keepdims=True)
        d_out = _rms_bwd(n4, r4, dy * g_ref[...]).astype(BF16)
        dout_ref[...] = d_out
        for c0 in range(0, D_FF, cc):
            d_f = _dot_nt(d_out, w_ref[pl.ds(c0, cc), :])
            gelu, d_gelu = _gelu_and_grad(act_ref[:, pl.ds(c0, cc)].astype(F32))
            val = act_ref[:, pl.ds(D_FF + c0, cc)].astype(F32)
            back_ref[:, pl.ds(c0, cc)] = (d_f * val * d_gelu).astype(BF16)
            back_ref[:, pl.ds(D_FF + c0, cc)] = (d_f * gelu).astype(BF16)

    row = pl.BlockSpec((tm, D_MODEL), lambda i: (i, 0))
    wide = pl.BlockSpec((tm, 2 * D_FF), lambda i: (i, 0))
    vec = pl.BlockSpec((1, D_MODEL), lambda i: (0, 0))
    return pl.pallas_call(
        body, name="ffn_down_fwd_bwd", grid=(t // tm,),
        out_shape=[jax.ShapeDtypeStruct((t, D_MODEL), F32), jax.ShapeDtypeStruct((t, D_MODEL), BF16),
                   jax.ShapeDtypeStruct((t, 2 * D_FF), BF16), jax.ShapeDtypeStruct((1, D_MODEL), F32),
                   jax.ShapeDtypeStruct((SUBLANES, LANES), F32)],
        in_specs=[pl.BlockSpec((tm, D_FF), lambda i: (i, 0)), wide, pl.BlockSpec((D_FF, D_MODEL), lambda i: (0, 0)),
                  row, row, vec],
        out_specs=[row, row, wide, vec, pl.BlockSpec((SUBLANES, LANES), lambda i: (0, 0))],
        compiler_params=_params("arbitrary"),
    )(f, act, w_down, x1, target, g4)


def _grad_tn(a, b, bm, name):
    t, m = a.shape
    n = b.shape[1]

    def body(a_ref, b_ref, o_ref):
        o_ref[...] = _dot_tn(a_ref[...], b_ref[...]).astype(BF16)

    return pl.pallas_call(
        body, name=name, grid=(m // bm,), out_shape=jax.ShapeDtypeStruct((m, n), BF16),
        in_specs=[pl.BlockSpec((t, bm), lambda i: (0, i)), pl.BlockSpec((t, n), lambda i: (0, 0))],
        out_specs=pl.BlockSpec((bm, n), lambda i: (i, 0)),
        compiler_params=_params("parallel"),
    )(a, b)


def _ffn_up_bwd(up, back, w_conv, h2, w_up):
    t = h2.shape[0]
    rc = _row_chunk(t)
    nb = N_FF_BLOCKS

    def body(up_ref, back_ref, c_ref, h_ref, w_ref, dw_ref, dcw_ref, dcb_ref, dh_ref, pad, after, d_up):
        @pl.when((pl.program_id(0) == 0) & (pl.program_id(1) == 0))
        def _():
            dh_ref[...] = jnp.zeros_like(dh_ref)
        pad[pl.ds(0, PAD), :] = jnp.zeros((PAD, CB), F32)
        after[pl.ds(t, PAD), :] = jnp.zeros((PAD, CB), F32)
        for r0 in range(0, t, rc):
            pad[pl.ds(PAD + r0, rc), :] = up_ref[pl.ds(r0, rc), :].astype(F32)
            after[pl.ds(r0, rc), :] = back_ref[pl.ds(r0, rc), :].astype(F32)
        cw = c_ref[...]
        taps = [jnp.zeros((1, CB), F32)] * 3
        bias = jnp.zeros((1, CB), F32)
        for r0 in range(0, t, rc):
            rows = pl.ds(r0, rc)
            d = _conv_anticausal(after, cw, r0, rc, 3).astype(BF16)
            d_up[rows, :] = d
            dh_ref[rows, :] += _dot_nt(d, w_ref[...])
            g = after[rows, :]
            taps = [acc + new for acc, new in zip(taps, _conv_wgrad(g, pad, r0, rc, 3))]
            bias = bias + jnp.sum(g, axis=0, keepdims=True)
        dw_ref[...] = _dot_tn(h_ref[...], d_up[...]).astype(BF16)
        dcw_ref[...] = jnp.concatenate(taps, axis=0)
        dcb_ref[...] = bias

    half = lambda rows: pl.BlockSpec((rows, CB), lambda j, k: (0, nb * k + j))
    whole = pl.BlockSpec((t, D_MODEL), lambda j, k: (0, 0))
    return pl.pallas_call(
        body, name="ffn_up_bwd", grid=(nb, 2),
        out_shape=[jax.ShapeDtypeStruct((D_MODEL, 2 * D_FF), BF16), jax.ShapeDtypeStruct((3, 2 * D_FF), F32),
                   jax.ShapeDtypeStruct((1, 2 * D_FF), F32), jax.ShapeDtypeStruct((t, D_MODEL), F32)],
        in_specs=[half(t), half(t), half(3), whole, half(D_MODEL)],
        out_specs=[half(D_MODEL), half(3), half(1), whole],
        scratch_shapes=[pltpu.VMEM((t + PAD, CB), F32), pltpu.VMEM((t + PAD, CB), F32), pltpu.VMEM((t, CB), BF16)],
        compiler_params=_params("arbitrary", "arbitrary"),
    )(up, back, w_conv, h2, w_up)


def _merge_bwd(dy, d_h2, x1, mix, g3, g2, w_out, w_cb, w_lb, pa, pb, proj):
    t = dy.shape[0]
    tm = min(256, t)

    def body(dy_ref, dh2_ref, x1_ref, mix_ref, g3_ref, g2_ref, wo_ref, wcb_ref, wlb_ref, pa_ref, pb_ref, gc_ref, gl_ref,
             dx1_ref, dmix_ref, dpa_ref, dpb_ref, dya_ref, dyb_ref, dgate_ref, dg3_ref, dg2_ref):
        @pl.when(pl.program_id(0) == 0)
        def _():
            dg3_ref[...] = jnp.zeros_like(dg3_ref)
            dg2_ref[...] = jnp.zeros_like(dg2_ref)
        n3, r3 = _rms_fwd(x1_ref[...])
        d_h2 = dh2_ref[...]
        dg3_ref[...] += jnp.sum(d_h2 * n3, axis=0, keepdims=True)
        dx1 = dy_ref[...] + _rms_bwd(n3, r3, d_h2 * g3_ref[...])
        dx1_ref[...] = dx1
        n2, r2 = _rms_fwd(mix_ref[...])
        dg2_ref[...] += jnp.sum(dx1 * n2, axis=0, keepdims=True)
        d_mix = _rms_bwd(n2, r2, dx1 * g2_ref[...]).astype(BF16)
        dmix_ref[...] = d_mix
        d_merged = _dot_nt(d_mix, wo_ref[...])
        sc = jax.nn.sigmoid(gc_ref[...].astype(F32))
        sl = jax.nn.sigmoid(gl_ref[...].astype(F32))
        d_pa = (d_merged * sc).astype(BF16)
        d_pb = (d_merged * sl).astype(BF16)
        dpa_ref[...] = d_pa
        dpb_ref[...] = d_pb
        dgate_ref[0] = (d_merged * pa_ref[...].astype(F32) * sc * (1.0 - sc)).astype(BF16)
        dgate_ref[1] = (d_merged * pb_ref[...].astype(F32) * sl * (1.0 - sl)).astype(BF16)
        dya_ref[...] = _dot_nt(d_pa, wcb_ref[...]).astype(BF16)
        dyb_ref[...] = _dot_nt(d_pb, wlb_ref[...]).astype(BF16)

    row = pl.BlockSpec((tm, D_MODEL), lambda i: (i, 0))
    full = pl.BlockSpec((D_MODEL, D_MODEL), lambda i: (0, 0))
    vec = pl.BlockSpec((1, D_MODEL), lambda i: (0, 0))
    act = jax.ShapeDtypeStruct((t, D_MODEL), BF16)
    small = jax.ShapeDtypeStruct((1, D_MODEL), F32)
    return pl.pallas_call(
        body, name="merge_bwd", grid=(t // tm,),
        out_shape=[jax.ShapeDtypeStruct((t, D_MODEL), F32), act, act, act, act, act,
                   jax.ShapeDtypeStruct((2, t, D_MODEL), BF16), small, small],
        in_specs=[row, row, row, row, vec, vec, full, full, full, row, row,
                  pl.BlockSpec((tm, D_MODEL), lambda i: (i, 5)), pl.BlockSpec((tm, D_MODEL), lambda i: (i, 6))],
        out_specs=[row] * 6 + [pl.BlockSpec((2, tm, D_MODEL), lambda i: (0, i, 0)), vec, vec],
        compiler_params=_params("arbitrary"),
    )(dy, d_h2, x1, mix, g3, g2, w_out, w_cb, w_lb, pa, pb, proj, proj)


def _conv_mixer_bwd(proj, d_ya, w_short):
    t = proj.shape[0]
    rc = _row_chunk(t)

    def body(b_ref, c_ref, x_ref, dy_ref, w_ref, d_ref, dw_ref, pad, back):
        pad[pl.ds(0, PAD), :] = jnp.zeros((PAD, CB), F32)
        back[pl.ds(t, PAD), :] = jnp.zeros((PAD, CB), F32)
        for r0 in range(0, t, rc):
            rows = pl.ds(r0, rc)
            pad[pl.ds(PAD + r0, rc), :] = c_ref[rows, :].astype(F32) * x_ref[rows, :].astype(F32)
        w = w_ref[...]
        for r0 in range(0, t, rc):
            rows = pl.ds(r0, rc)
            d_y = dy_ref[rows, :].astype(F32)
            d_ref[0, rows, :] = (d_y * _conv_causal(pad, w, r0, rc, 3)).astype(BF16)
            back[rows, :] = d_y * b_ref[rows, :].astype(F32)
        taps = [jnp.zeros((1, CB), F32)] * 3
        for r0 in range(0, t, rc):
            rows = pl.ds(r0, rc)
            d_u = _conv_anticausal(back, w, r0, rc, 3)
            d_ref[1, rows, :] = (d_u * x_ref[rows, :].astype(F32)).astype(BF16)
            d_ref[2, rows, :] = (d_u * c_ref[rows, :].astype(F32)).astype(BF16)
            taps = [acc + new for acc, new in zip(taps, _conv_wgrad(back[rows, :], pad, r0, rc, 3))]
        dw_ref[...] = jnp.concatenate(taps, axis=0)

    blk = pl.BlockSpec((t, CB), lambda h: (0, h))
    return pl.pallas_call(
        body, name="conv_mixer_bwd", grid=(D_MODEL // CB,),
        out_shape=[jax.ShapeDtypeStruct((3, t, D_MODEL), BF16), jax.ShapeDtypeStruct((3, D_MODEL), F32)],
        in_specs=[_section(0, t), _section(1, t), _section(2, t), blk, pl.BlockSpec((3, CB), lambda h: (0, h))],
        out_specs=[pl.BlockSpec((3, t, CB), lambda h: (0, 0, h)), pl.BlockSpec((3, CB), lambda h: (0, h))],
        scratch_shapes=[pltpu.VMEM((t + PAD, CB), F32), pltpu.VMEM((t + PAD, CB), F32)],
        compiler_params=_params("parallel"),
    )(proj, proj, proj, d_ya, w_short)


LRU_SMALL_ROWS = 8


def _lru_bwd(proj, hl, d_yb, w_conv, b_conv, wa, ba, wx, bx, lam):
    t = proj.shape[0]
    rc = _row_chunk(t)
    vec, mat = _head_specs()

    def body(lx_ref, ly_ref, hl_ref, dy_ref, wc_ref, bc_ref, wa_ref, ba_ref, wx_ref, bx_ref, lam_ref,
             d_ref, dwa_ref, dwx_ref, small_ref, pad, a_next, dh_s, h_prev, back, acc_a, acc_x):
        zeros = jnp.zeros((PAD, CB), F32)
        pad[pl.ds(0, PAD), :] = zeros
        h_prev[pl.ds(0, PAD), :] = zeros
        a_next[pl.ds(t, PAD), :] = zeros
        back[pl.ds(t, PAD), :] = zeros
        for r0 in range(0, t, rc):
            pad[pl.ds(PAD + r0, rc), :] = lx_ref[pl.ds(r0, rc), :].astype(F32)
            h_prev[pl.ds(PAD + r0, rc), :] = hl_ref[pl.ds(r0, rc), :]
        wc, bc = wc_ref[...], bc_ref[...]
        wa_m, wx_m = wa_ref[...].reshape(HEAD_DIM, HEAD_DIM), wx_ref[...].reshape(HEAD_DIM, HEAD_DIM)
        ls = _log_sigmoid(lam_ref[...])

        def gates(r0):
            xl = _conv_causal(pad, wc, r0, rc, 4) + bc
            first = (lax.broadcasted_iota(jnp.int32, (rc, CB), 0) + r0) == 0
            return (xl, first) + _lru_gates(xl, wa_m, ba_ref[...], wx_m, bx_ref[...], ls, first)

        for r0 in range(0, t, rc):
            rows = pl.ds(r0, rc)
            a = gates(r0)[5]
            a_next[pl.ds(PAD - 1 + r0, rc), :] = a
            act, d_act = _gelu_and_grad(ly_ref[rows, :].astype(F32))
            d_y = dy_ref[rows, :].astype(F32)
            dh_s[rows, :] = d_y * act
            d_ref[1, rows, :] = (d_y * hl_ref[rows, :] * d_act).astype(BF16)

        row = lax.broadcasted_iota(jnp.int32, (SUBLANES, CB), 0)
        groups = t // SUBLANES

        def group(i, carry):
            r = pl.multiple_of((groups - 1 - i) * SUBLANES, SUBLANES)
            a_g, b_g = a_next[pl.ds(PAD + r, SUBLANES), :], dh_s[pl.ds(r, SUBLANES), :]
            for s in (1, 2, 4):
                keep = row < SUBLANES - s
                b_g = jnp.where(keep, a_g * pltpu.roll(b_g, SUBLANES - s, 0) + b_g, b_g)
                a_g = jnp.where(keep, a_g * pltpu.roll(a_g, SUBLANES - s, 0), a_g)
            d_g = b_g + a_g * carry
            dh_s[pl.ds(r, SUBLANES), :] = d_g
            return jnp.broadcast_to(d_g[0:1, :], (SUBLANES, CB))

        lax.fori_loop(0, groups, group, jnp.zeros((SUBLANES, CB), F32))

        acc_a[...] = jnp.zeros_like(acc_a)
        acc_x[...] = jnp.zeros_like(acc_x)
        d_ba = d_bx = d_ls = jnp.zeros((1, CB), F32)
        for r0 in range(0, t, rc):
            rows = pl.ds(r0, rc)
            xl, first, xb, ra, ia, a, one_minus, mult = gates(r0)
            d_h = dh_s[rows, :]
            d_a = d_h * h_prev[pl.ds(PAD - 1 + r0, rc), :]
            d_mult = d_h * ia * xl
            d_ia = d_h * mult * xl
            d_xl = d_h * mult * ia
            d_mult_d_la = jnp.where(first, 0.0, (one_minus - 1.0) / mult)
            d_la = d_a * a + d_mult * d_mult_d_la
            d_ls = d_ls + jnp.sum(d_la * ra, axis=0, keepdims=True) * LRU_C
            d_za = d_la * (LRU_C * ls) * ra * (1.0 - ra)
            d_zx = d_ia * ia * (1.0 - ia)
            d_ba = d_ba + jnp.sum(d_za, axis=0, keepdims=True)
            d_bx = d_bx + jnp.sum(d_zx, axis=0, keepdims=True)
            d_za, d_zx = d_za.astype(BF16), d_zx.astype(BF16)
            acc_a[...] += _dot_tn(xb, d_za)
            acc_x[...] += _dot_tn(xb, d_zx)
            back[rows, :] = d_xl + _dot_nt(d_za, wa_m) + _dot_nt(d_zx, wx_m)
        taps = [jnp.zeros((1, CB), F32)] * 4
        d_bc = jnp.zeros((1, CB), F32)
        for r0 in range(0, t, rc):
            rows = pl.ds(r0, rc)
            d_ref[0, rows, :] = _conv_anticausal(back, wc, r0, rc, 4).astype(BF16)
            g = back[rows, :]
            taps = [acc + new for acc, new in zip(taps, _conv_wgrad(g, pad, r0, rc, 4))]
            d_bc = d_bc + jnp.sum(g, axis=0, keepdims=True)
        d_lam = d_ls * jax.nn.sigmoid(-lam_ref[...])
        small_ref[...] = jnp.concatenate(taps + [d_bc, d_ba, d_bx, d_lam], axis=0)
        dwa_ref[...] = acc_a[...].reshape(N_DEV, HEAD_DIM // N_DEV, HEAD_DIM).astype(BF16)
        dwx_ref[...] = acc_x[...].reshape(N_DEV, HEAD_DIM // N_DEV, HEAD_DIM).astype(BF16)

    blk = pl.BlockSpec((t, CB), lambda h: (0, h))
    gate_grad = jax.ShapeDtypeStruct((N_DEV, N_HEADS, HEAD_DIM // N_DEV, HEAD_DIM), BF16)
    return pl.pallas_call(
        body, name="lru_bwd", grid=(N_HEADS,),
        out_shape=[jax.ShapeDtypeStruct((2, t, D_MODEL), BF16), gate_grad, gate_grad,
                   jax.ShapeDtypeStruct((LRU_SMALL_ROWS, D_MODEL), F32)],
        in_specs=[_section(3, t), _section(4, t), blk, blk, pl.BlockSpec((4, CB), lambda h: (0, h)),
                  vec, mat, vec, mat, vec, vec],
        out_specs=[pl.BlockSpec((2, t, CB), lambda h: (0, 0, h)), mat, mat,
                   pl.BlockSpec((LRU_SMALL_ROWS, CB), lambda h: (0, h))],
        scratch_shapes=[pltpu.VMEM((t + PAD, CB), F32), pltpu.VMEM((t + PAD, CB), F32), pltpu.VMEM((t, CB), F32),
                        pltpu.VMEM((t + PAD, CB), F32), pltpu.VMEM((t + PAD, CB), F32),
                        pltpu.VMEM((HEAD_DIM, HEAD_DIM), F32), pltpu.VMEM((HEAD_DIM, HEAD_DIM), F32)],
        compiler_params=_params("parallel"),
    )(proj, proj, hl, d_yb, w_conv, b_conv, wa, ba, wx, bx, lam)


def _stack_maps(halves):
    def conv(sec, part):
        return jnp.minimum(sec, 2), jnp.where(sec < 3, part, halves - 1)

    def lru(sec, part):
        return jnp.clip(sec - 3, 0, 1), jnp.where(sec < 3, 0, jnp.where(sec < 5, part, halves - 1))

    def gate(sec, part):
        return jnp.clip(sec - 5, 0, 1), jnp.where(sec < 5, 0, part)

    return conv, lru, gate


def _pick_stack(sec, refs, fn):
    @pl.when(sec < 3)
    def _():
        fn(refs[0])

    @pl.when((sec >= 3) & (sec < 5))
    def _():
        fn(refs[1])

    @pl.when(sec >= 5)
    def _():
        fn(refs[2])


def _in_proj_wgrad(h, d_conv, d_lru, d_gate):
    t = h.shape[0]
    halves, bn = 2, D_MODEL // 2
    maps = _stack_maps(halves)

    def body(h_ref, dc_ref, dl_ref, dg_ref, o_ref):
        def emit(ref):
            o_ref[...] = _dot_tn(h_ref[...], ref[...]).astype(BF16)
        _pick_stack(pl.program_id(0) // halves, (dc_ref, dl_ref, dg_ref), emit)

    def spec(m):
        def index(s):
            stack, part = m(s // halves, s % halves)
            return stack, 0, part
        return pl.BlockSpec((None, t, bn), index)

    return pl.pallas_call(
        body, name="in_proj_wgrad", grid=(7 * halves,), out_shape=jax.ShapeDtypeStruct((D_MODEL, IN_COLS), BF16),
        in_specs=[pl.BlockSpec((t, D_MODEL), lambda s: (0, 0))] + [spec(m) for m in maps],
        out_specs=pl.BlockSpec((D_MODEL, bn), lambda s: (0, s)),
        compiler_params=_params("arbitrary"),
    )(h, d_conv, d_lru, d_gate)


def _in_proj_xgrad(d_conv, d_lru, d_gate, w_in, x, dx1, g1):
    t = x.shape[0]
    tm = min(512, t)
    maps = _stack_maps(1)

    def body(dc_ref, dl_ref, dg_ref, w_ref, x_ref, dx1_ref, g_ref, dx_ref, dgain_ref, acc):
        i, s = pl.program_id(0), pl.program_id(1)

        @pl.when((i == 0) & (s == 0))
        def _():
            dgain_ref[...] = jnp.zeros_like(dgain_ref)

        @pl.when(s == 0)
        def _():
            acc[...] = jnp.zeros_like(acc)

        def add(ref):
            acc[...] += _dot_nt(ref[...], w_ref[...])
        _pick_stack(s, (dc_ref, dl_ref, dg_ref), add)

        @pl.when(s == 6)
        def _():
            n1, r1 = _rms_fwd(x_ref[...])
            d_h = acc[...]
            dgain_ref[...] += jnp.sum(d_h * n1, axis=0, keepdims=True)
            dx_ref[...] = dx1_ref[...] + _rms_bwd(n1, r1, d_h * g_ref[...])

    def spec(m):
        def index(i, s):
            return m(s, 0)[0], i, 0
        return pl.BlockSpec((None, tm, D_MODEL), index)

    row = pl.BlockSpec((tm, D_MODEL), lambda i, s: (i, 0))
    vec = pl.BlockSpec((1, D_MODEL), lambda i, s: (0, 0))
    return pl.pallas_call(
        body, name="in_proj_xgrad", grid=(t // tm, 7),
        out_shape=[jax.ShapeDtypeStruct((t, D_MODEL), F32), jax.ShapeDtypeStruct((1, D_MODEL), F32)],
        in_specs=[spec(m) for m in maps] + [pl.BlockSpec((D_MODEL, D_MODEL), lambda i, s: (0, s)), row, row, vec],
        out_specs=[row, vec],
        scratch_shapes=[pltpu.VMEM((tm, D_MODEL), F32)],
        compiler_params=_params("arbitrary", "arbitrary"),
    )(d_conv, d_lru, d_gate, w_in, x, dx1, g1)


def _add_pair(grad, got, by_cols, pos, name):
    cols = got.shape[-1]
    got3 = got.reshape(4, -1, cols)
    rows = got3.shape[1]
    rb = _row_block(rows, 512)

    def block(k, p):
        return 4 * ((p[0] + k % 2) % 2) + 2 * ((p[1] + k // 2) % 2) + p[2]

    if by_cols:
        g_in, g_spec = grad, pl.BlockSpec((rb, cols), lambda k, i, p: (i, block(k, p)))
    else:
        g_in = grad.reshape(N_DEV, rows, cols)
        g_spec = pl.BlockSpec((None, rb, cols), lambda k, i, p: (block(k, p), i, 0))
    slot = pl.BlockSpec((None, rb, cols), lambda k, i, p: (k, i, 0))

    def body(pos_ref, a_ref, b_ref, o_ref):
        o_ref[...] = (a_ref[...].astype(F32) + b_ref[...].astype(F32)).astype(BF16)

    out = pl.pallas_call(
        body, name=name, out_shape=jax.ShapeDtypeStruct(got3.shape, BF16),
        grid_spec=pltpu.PrefetchScalarGridSpec(num_scalar_prefetch=1, grid=(4, rows // rb),
                                               in_specs=[g_spec, slot], out_specs=slot),
        compiler_params=_params("parallel", "parallel"),
    )(pos, g_in, got3)
    return out.reshape(got.shape)


def _adamw(w, g, m, v):
    m = ADAM_B1 * m + (1.0 - ADAM_B1) * g
    v = ADAM_B2 * v + (1.0 - ADAM_B2) * (g * g)
    m_hat = m / (1.0 - ADAM_B1 ** ADAM_STEP)
    v_hat = v / (1.0 - ADAM_B2 ** ADAM_STEP)
    return -ADAM_LR * (m_hat / (jnp.sqrt(v_hat) + ADAM_EPS) + ADAM_WD * w), m, v


def _adam_large(w, m, v, own, others, name):
    shape = w.shape
    cols = shape[-1]
    w2, m2, v2 = (a.reshape(-1, cols) for a in (w, m, v))
    rows = w2.shape[0]
    own, others = own.reshape(4, rows, cols), others.reshape(3, rows, cols)
    rb = _row_block(rows, 256)

    def body(w_ref, m_ref, v_ref, own_ref, oth_ref, g_ref, d_ref, nm_ref, nv_ref):
        g = own_ref[...].astype(F32)
        for k in range(3):
            g = g + oth_ref[k].astype(F32)
        g_ref[...] = g
        d_ref[...], nm_ref[...], nv_ref[...] = _adamw(w_ref[...], g, m_ref[...], v_ref[...])

    blk = pl.BlockSpec((rb, cols), lambda i: (i, 0))
    res = jax.ShapeDtypeStruct((rows, cols), F32)
    outs = pl.pallas_call(
        body, name=name, grid=(rows // rb,), out_shape=[res] * 4,
        in_specs=[blk, blk, blk, pl.BlockSpec((None, rb, cols), lambda i: (0, i, 0)),
                  pl.BlockSpec((3, rb, cols), lambda i: (0, i, 0))],
        out_specs=[blk] * 4, compiler_params=_params("parallel"),
    )(w2, m2, v2, own, others)
    return [o.reshape(shape) for o in outs]


def _adam_small(ws, gs, ms, vs):
    n = len(ws)

    def body(*refs):
        w_refs, g_refs, m_refs, v_refs = (refs[i * n:(i + 1) * n] for i in range(4))
        outs = refs[4 * n:]
        for i in range(n):
            d, m, v = _adamw(w_refs[i][...], g_refs[i][...], m_refs[i][...], v_refs[i][...])
            outs[i][...], outs[n + i][...], outs[2 * n + i][...] = d, m, v

    shapes = [jax.ShapeDtypeStruct(w.shape, F32) for w in ws]
    outs = pl.pallas_call(
        body, name="adam_small", out_shape=shapes * 3,
        in_specs=[VMEM_SPEC] * (4 * n), out_specs=[VMEM_SPEC] * (3 * n), compiler_params=_params(),
    )(*ws, *gs, *ms, *vs)
    return outs[:n], outs[n:2 * n], outs[2 * n:]


def _pack_rows(pieces):
    tile = SUBLANES * LANES
    return jnp.concatenate([jnp.pad(p.reshape(-1), (0, (-p.size) % tile)).reshape(-1, LANES) for p in pieces], axis=0)


def _packed_starts(sizes):
    tile = SUBLANES * LANES
    starts = [0]
    for s in sizes:
        starts.append(starts[-1] + (s + tile - 1) // tile * SUBLANES)
    return starts


def kernel(x, norm_mix_pre, norm_mix_post, norm_ffn_pre, norm_ffn_post, w_in, conv_short_w, w_conv_branch, lru_conv_w, lru_conv_b, lru_wa, lru_ba, lru_wx, lru_bx, lru_lambda, w_lru_branch, w_out, ffn_w_up, ffn_conv_w, ffn_conv_b, ffn_w_down, loss_target, m_norm_mix_pre, m_norm_mix_post, m_norm_ffn_pre, m_norm_ffn_post, m_w_in, m_conv_short_w, m_w_conv_branch, m_lru_conv_w, m_lru_conv_b, m_lru_wa, m_lru_ba, m_lru_wx, m_lru_bx, m_lru_lambda, m_w_lru_branch, m_w_out, m_ffn_w_up, m_ffn_conv_w, m_ffn_conv_b, m_ffn_w_down, v_norm_mix_pre, v_norm_mix_post, v_norm_ffn_pre, v_norm_ffn_post, v_w_in, v_conv_short_w, v_w_conv_branch, v_lru_conv_w, v_lru_conv_b, v_lru_wa, v_lru_ba, v_lru_wx, v_lru_bx, v_lru_lambda, v_w_lru_branch, v_w_out, v_ffn_w_up, v_ffn_conv_w, v_ffn_conv_b, v_ffn_w_down):
    t = x.shape[1]
    xi, yi, ci = _position()
    me = _block_of(xi, yi, ci)
    x2, target = x[0], loss_target[0]
    shard_in, shard_up = IN_COLS // N_DEV, 2 * D_FF // N_DEV
    shard_sq, shard_down, shard_head = D_MODEL // N_DEV, D_FF // N_DEV, HEAD_DIM // N_DEV

    names = ["w_in", "lru_wa", "lru_wx", "w_conv_branch", "w_lru_branch", "w_out", "ffn_w_up", "ffn_w_down"]
    large = [w_in[0], lru_wa[0], lru_wx[0], w_conv_branch[0], w_lru_branch[0], w_out[0], ffn_w_up[0], ffn_w_down[0]]
    blocks = [_cols(shard_in), _lead, _lead, _rows(shard_sq), _rows(shard_sq), _rows(shard_sq),
              _cols(shard_up), _rows(shard_down)]
    gate_full = (N_DEV, N_HEADS, shard_head, HEAD_DIM)
    full_shapes = [(D_MODEL, IN_COLS), gate_full, gate_full, (D_MODEL, D_MODEL), (D_MODEL, D_MODEL), (D_MODEL, D_MODEL),
                   (D_MODEL, 2 * D_FF), (D_FF, D_MODEL)]
    n_now = 3
    small_sharded = [conv_short_w, lru_conv_w, lru_ba, lru_bx, ffn_conv_w]
    small_mine = _pack_rows(small_sharded)
    small_at = _packed_starts([p.size for p in small_sharded])
    *gathered, small_all = _gather_weights(large, blocks, full_shapes, small_mine, n_now)
    g_in, g_wa, g_wx = gathered[:n_now]
    later_blocks = blocks[n_now:]
    send1, recv1, later, gather_token = _gather_start(gathered[n_now:], later_blocks, "gather_start")

    def behind(token, operand):
        return operand + token[0:1, 0:1]

    def forward(lo, hi, after, tag):
        return _gather_forward(later[lo:hi], later_blocks[lo:hi], send1[4 * lo:4 * hi], recv1[4 * lo:4 * hi], after,
                               "gather_forward_" + tag)

    def finish(lo, hi, flight, after, tag):
        return _gather_finish(flight[2], later_blocks[lo:hi], flight[0], flight[1], after, "gather_finish_" + tag)

    def cols_of(r0, n, width):
        part = small_all[:, r0:r0 + n * width // LANES, :].reshape(N_DEV, n, width)
        return part.transpose(1, 0, 2).reshape(n, N_DEV * width)

    c_short = cols_of(small_at[0], 3, LANES)
    c_lru = cols_of(small_at[1], 4, LANES)
    b_a = cols_of(small_at[2], N_HEADS, shard_head).reshape(1, D_MODEL)
    b_x = cols_of(small_at[3], N_HEADS, shard_head).reshape(1, D_MODEL)
    c_ffn = cols_of(small_at[4], 3, shard_up)

    proj, h = _in_proj(x2, behind(gather_token, norm_mix_pre), g_in)
    flight_mix_w = forward(0, 3, h, "mix")
    y_a = _conv_mixer_fwd(proj, c_short)
    y_b, hl = _lru_fwd(proj, c_lru, lru_conv_b, g_wa, b_a, g_wx, b_x, lru_lambda)
    flight_up_w = forward(3, 4, y_b, "up")
    g_cb, g_lb, g_out = finish(0, 3, flight_mix_w, y_b, "mix")
    pa, pb, merged, mix, x1, h2 = _merge(y_a, y_b, proj, x2, g_cb, g_lb, g_out, norm_mix_post, norm_ffn_pre)
    flight_down_w = forward(4, 5, h2, "down")
    (g_up,) = finish(3, 4, flight_up_w, h2, "up")
    up, act, f = _ffn_up(h2, g_up, c_ffn, ffn_conv_b)
    (g_down,) = finish(4, 5, flight_down_w, f, "down")
    dy, d_out, d_act, dg4, loss_part = _ffn_down(f, act, g_down, x1, target, norm_ffn_post)

    block_of = dict(zip(names, blocks))
    shard_shapes = {"w_in": (D_MODEL, shard_in), "w_conv_branch": (shard_sq, D_MODEL), "w_lru_branch": (shard_sq, D_MODEL),
                    "w_out": (shard_sq, D_MODEL), "lru_wa": (N_HEADS, shard_head, HEAD_DIM),
                    "lru_wx": (N_HEADS, shard_head, HEAD_DIM), "ffn_w_up": (D_MODEL, shard_up),
                    "ffn_w_down": (shard_down, D_MODEL)}
    pos = jnp.stack([xi, yi, ci]).astype(jnp.int32)

    def reduce_start(tag, grads):
        keys = list(grads)
        got = _exchange_pair([grads[k] for k in keys], [block_of[k] for k in keys], [shard_shapes[k] for k in keys],
                             "reduce_pair_exchange_" + tag)
        sums = [_add_pair(grads[k], g, k in ("w_in", "ffn_w_up"), pos, "pair_sum_" + k) for k, g in zip(keys, got)]
        return (keys,) + _exchange_chips_start(sums, "reduce_chip_start_" + tag)

    gw_down = _grad_tn(f, d_out, min(512, D_FF), "ffn_down_wgrad")
    flight_down = reduce_start("down", {"ffn_w_down": gw_down})
    gw_up, gc_ffn, gb_ffn, d_h2 = _ffn_up_bwd(up, d_act, behind(flight_down[-1], c_ffn), h2, g_up)
    flight_up = reduce_start("up", {"ffn_w_up": gw_up})
    dx1, d_mix, d_pa, d_pb, d_ya, d_yb, d_gate, dg3, dg2 = _merge_bwd(
        dy, d_h2, x1, mix, behind(flight_up[-1], norm_ffn_pre), norm_mix_post, g_out, g_cb, g_lb, pa, pb, proj)
    gw_out = _grad_tn(merged, d_mix, CB, "w_out_wgrad")
    gw_cb = _grad_tn(y_a, d_pa, CB, "w_conv_branch_wgrad")
    gw_lb = _grad_tn(y_b, d_pb, CB, "w_lru_branch_wgrad")
    flight_mix = reduce_start("mix", {"w_conv_branch": gw_cb, "w_lru_branch": gw_lb, "w_out": gw_out})
    d_conv, gc_short = _conv_mixer_bwd(proj, d_ya, behind(flight_mix[-1], c_short))
    d_lru, gw_a, gw_x, g_lru_small = _lru_bwd(proj, hl, d_yb, c_lru, lru_conv_b, g_wa, b_a, g_wx, b_x, lru_lambda)
    gw_in = _in_proj_wgrad(h, d_conv, d_lru, d_gate)
    flight_in = reduce_start("in", {"lru_wa": gw_a, "lru_wx": gw_x, "w_in": gw_in})
    dx, dg1 = _in_proj_xgrad(d_conv, d_lru, d_gate, g_in, x2, dx1, behind(flight_in[-1], norm_mix_pre))

    moments ={"w_in": (m_w_in, v_w_in), "w_conv_branch": (m_w_conv_branch, v_w_conv_branch),
               "w_lru_branch": (m_w_lru_branch, v_w_lru_branch), "w_out": (m_w_out, v_w_out),
               "lru_wa": (m_lru_wa, v_lru_wa), "lru_wx": (m_lru_wx, v_lru_wx), "ffn_w_up": (m_ffn_w_up, v_ffn_w_up),
               "ffn_w_down": (m_ffn_w_down, v_ffn_w_down)}
    weights = {"w_in": w_in, "w_conv_branch": w_conv_branch, "w_lru_branch": w_lru_branch, "w_out": w_out,
               "lru_wa": lru_wa, "lru_wx": lru_wx, "ffn_w_up": ffn_w_up, "ffn_w_down": ffn_w_down}
    out_g, out_d, out_m, out_v = {}, {}, {}, {}

    pieces = [dg1, dg2, dg3, dg4, g_lru_small[4:5], g_lru_small[7:8], gb_ffn, gc_short, g_lru_small[0:4],
              g_lru_small[5:6], g_lru_small[6:7], gc_ffn, loss_part]
    total = _allreduce_small(_pack_rows(pieces))
    sizes = [p.size for p in pieces]
    starts = _packed_starts(sizes)

    def piece(i, shape):
        return total[starts[i]:starts[i + 1]].reshape(-1)[:sizes[i]].reshape(shape)

    loss = total[starts[12], 0]

    def col_shard(full, width):
        return lax.dynamic_slice_in_dim(full, me * width, width, axis=1)

    def head_shard(full):
        return lax.dynamic_slice_in_dim(full.reshape(N_HEADS, HEAD_DIM), me * shard_head, shard_head, axis=1)

    small_names = ["norm_mix_pre", "norm_mix_post", "norm_ffn_pre", "norm_ffn_post", "lru_conv_b", "lru_lambda",
                   "ffn_conv_b", "conv_short_w", "lru_conv_w", "lru_ba", "lru_bx", "ffn_conv_w"]
    small_g = [piece(0, (1, D_MODEL)), piece(1, (1, D_MODEL)), piece(2, (1, D_MODEL)), piece(3, (1, D_MODEL)),
               piece(4, (1, D_MODEL)), piece(5, (1, D_MODEL)), piece(6, (1, 2 * D_FF)),
               col_shard(piece(7, (3, D_MODEL)), LANES), col_shard(piece(8, (4, D_MODEL)), LANES),
               head_shard(piece(9, (1, D_MODEL))), head_shard(piece(10, (1, D_MODEL))),
               col_shard(piece(11, (3, 2 * D_FF)), shard_up)]
    small_w = [norm_mix_pre, norm_mix_post, norm_ffn_pre, norm_ffn_post, lru_conv_b, lru_lambda, ffn_conv_b,
               conv_short_w[0], lru_conv_w[0], lru_ba[0], lru_bx[0], ffn_conv_w[0]]
    small_m = [m_norm_mix_pre, m_norm_mix_post, m_norm_ffn_pre, m_norm_ffn_post, m_lru_conv_b, m_lru_lambda,
               m_ffn_conv_b, m_conv_short_w[0], m_lru_conv_w[0], m_lru_ba[0], m_lru_bx[0], m_ffn_conv_w[0]]
    small_v = [v_norm_mix_pre, v_norm_mix_post, v_norm_ffn_pre, v_norm_ffn_post, v_lru_conv_b, v_lru_lambda,
               v_ffn_conv_b, v_conv_short_w[0], v_lru_conv_w[0], v_lru_ba[0], v_lru_bx[0], v_ffn_conv_w[0]]
    s_d, s_m, s_v = _adam_small(small_w, small_g, small_m, small_v)
    for i, name in enumerate(small_names):
        shape = small_w[i].shape if i < 7 else (1,) + small_w[i].shape
        out_g[name] = small_g[i].reshape(shape)
        out_d[name], out_m[name], out_v[name] = s_d[i].reshape(shape), s_m[i].reshape(shape), s_v[i].reshape(shape)

    after = s_d[0]
    for tag, (keys, send, recv, sums, lands, _) in (("down", flight_down), ("up", flight_up), ("mix", flight_mix),
                                                    ("in", flight_in)):
        sums, others = _exchange_chips_wait(send, recv, sums, lands, after, "reduce_chip_wait_" + tag)
        for k, own, oth in zip(keys, sums, others):
            out_g[k], out_d[k], out_m[k], out_v[k] = _adam_large(weights[k], *moments[k], own, oth, "adam_" + k)
        after = out_d[keys[-1]]

    order = ["norm_mix_pre", "norm_mix_post", "norm_ffn_pre", "norm_ffn_post", "w_in", "conv_short_w", "w_conv_branch",
             "lru_conv_w", "lru_conv_b", "lru_wa", "lru_ba", "lru_wx", "lru_bx", "lru_lambda", "w_lru_branch", "w_out",
             "ffn_w_up", "ffn_conv_w", "ffn_conv_b", "ffn_w_down"]
    return (loss, dx.reshape(1, t, D_MODEL), *[out_g[k] for k in order], *[out_d[k] for k in order],
            *[out_m[k] for k in order], *[out_v[k] for k in order])
```

```python
import functools
import math

import jax
import jax.numpy as jnp
from jax import lax
from jax.experimental import pallas as pl
from jax.experimental.pallas import tpu as pltpu

F32 = jnp.float32
BF16 = jnp.bfloat16
MESH = pl.DeviceIdType.MESH

N_DEV = 8
D_MODEL = 1024
N_HEADS = 4
HEAD_DIM = D_MODEL // N_HEADS
D_FF = 3 * D_MODEL
IN_COLS = 7 * D_MODEL
LRU_C = 8.0
RMS_EPS = 1e-6
ADAM_LR = 0.001
ADAM_B1 = 0.9
ADAM_B2 = 0.999
ADAM_EPS = 1e-08
ADAM_WD = 0.01
ADAM_STEP = 10
GELU_K = math.sqrt(2.0 / math.pi)
GELU_C = 0.044715

LANES = 128
SUBLANES = 8
PAD = SUBLANES
VMEM_LIMIT = 56 * 1024 * 1024
CB = 256

HBM_SPEC = pl.BlockSpec(memory_space=pltpu.HBM)
SEM_SPEC = pl.BlockSpec(memory_space=pltpu.SEMAPHORE)
DATAFLOW_EFFECT = pltpu.SideEffectType.DATAFLOW_SIDE_EFFECTING
VMEM_SPEC = pl.BlockSpec(memory_space=pltpu.VMEM)


def _params(*sem):
    if sem:
        return pltpu.CompilerParams(dimension_semantics=sem, vmem_limit_bytes=VMEM_LIMIT)
    return pltpu.CompilerParams(vmem_limit_bytes=VMEM_LIMIT)


def _row_chunk(t):
    return min(256, t)


def _row_block(rows, cap):
    return next(rb for rb in range(min(cap, rows), 0, -16) if rows % rb == 0)


def _gelu(x):
    return 0.5 * x * (1.0 + jnp.tanh(GELU_K * (x + GELU_C * x * x * x)))


def _gelu_and_grad(x):
    t = jnp.tanh(GELU_K * (x + GELU_C * x * x * x))
    g = 0.5 * x * (1.0 + t)
    dg = 0.5 * (1.0 + t) + 0.5 * x * (1.0 - t * t) * GELU_K * (1.0 + 3.0 * GELU_C * x * x)
    return g, dg


def _expm1_neg(x):
    series = x * (1.0 + x * (0.5 + x * (1.0 / 6.0 + x * (1.0 / 24.0 + x * (1.0 / 120.0)))))
    return jnp.where(x > -0.05, series, jnp.exp(x) - 1.0)


def _log_sigmoid(x):
    return jnp.minimum(x, 0.0) - jnp.log1p(jnp.exp(-jnp.abs(x)))


def _dot(a, b):
    return jnp.dot(a, b, preferred_element_type=F32)


def _dot_nt(a, b):
    return lax.dot_general(a, b, (((1,), (1,)), ((), ())), preferred_element_type=F32)


def _dot_tn(a, b):
    return lax.dot_general(a, b, (((0,), (0,)), ((), ())), preferred_element_type=F32)


def _rms_fwd(x):
    r = lax.rsqrt(jnp.mean(x * x, axis=-1, keepdims=True) + RMS_EPS)
    return x * r, r


def _rms_bwd(n, r, gdy):
    return r * (gdy - n * jnp.mean(n * gdy, axis=-1, keepdims=True))


def _conv_causal(pad_ref, w, r0, rows, taps):
    acc = None
    for k in range(taps):
        term = w[k:k + 1, :] * pad_ref[pl.ds(PAD + r0 - (taps - 1 - k), rows), :]
        acc = term if acc is None else acc + term
    return acc


def _conv_anticausal(pad_ref, w, r0, rows, taps):
    acc = None
    for k in range(taps):
        term = w[k:k + 1, :] * pad_ref[pl.ds(r0 + (taps - 1 - k), rows), :]
        acc = term if acc is None else acc + term
    return acc


def _conv_wgrad(g, xpad_ref, r0, rows, taps):
    return [jnp.sum(g * xpad_ref[pl.ds(PAD + r0 - (taps - 1 - k), rows), :], axis=0, keepdims=True)
            for k in range(taps)]


def _position():
    return lax.axis_index("x"), lax.axis_index("y"), lax.axis_index("c")


def _block_of(x, y, c):
    return 4 * x + 2 * y + c


def _chip(x, y, k):
    return (x + (k & 1)) % 2, (y + (k >> 1)) % 2


def _cols(width):
    def at(ref, d):
        return ref.at[:, pl.ds(pl.multiple_of(d * width, LANES), width)]
    return at


def _rows(height):
    def at(ref, d):
        return ref.at[pl.ds(pl.multiple_of(d * height, 16), height), :]
    return at


def _lead(ref, d):
    return ref.at[d]


def _gather_weights(shards, blocks, full_shapes, small, n_now):
    n = len(shards)
    small_rows = small.shape[0]

    def body(*refs):
        ins, small_in = refs[:n], refs[n]
        outs, small_out = refs[n + 1:2 * n + 1], refs[2 * n + 1]
        stage = refs[2 * n + 2:3 * n + 2]
        send, recv, local = refs[3 * n + 2:]
        x, y, c = _position()
        me = _block_of(x, y, c)
        sibling = (x, y, 1 - c)

        for a in range(n):
            stage[a][...] = ins[a][...].astype(BF16)

        def copy(a, k, block, to, src=None):
            dst = blocks[a](outs[a], block)
            return pltpu.make_async_remote_copy(
                src_ref=dst if src is None else src, dst_ref=dst, send_sem=send.at[a, k], recv_sem=recv.at[a, k],
                device_id=to, device_id_type=MESH)

        def small_copy(k):
            px, py, pc = (x + (k & 1)) % 2, (y + ((k >> 1) & 1)) % 2, (c + (k >> 2)) % 2
            return pltpu.make_async_remote_copy(
                src_ref=small_in, dst_ref=small_out.at[me], send_sem=send.at[n_now, k - 1], recv_sem=recv.at[n_now, k - 1],
                device_id=(px, py, pc), device_id_type=MESH)

        def small_arrival(k):
            px, py, pc = (x + (k & 1)) % 2, (y + ((k >> 1) & 1)) % 2, (c + (k >> 2)) % 2
            return pltpu.make_async_remote_copy(
                src_ref=small_in, dst_ref=small_out.at[_block_of(px, py, pc)], send_sem=send.at[n_now, k - 1],
                recv_sem=recv.at[n_now, k - 1], device_id=(px, py, pc), device_id_type=MESH)

        small_out[me] = small_in[...]
        small_sends = [small_copy(k) for k in range(1, N_DEV)]
        for cp in small_sends:
            cp.start()

        mine, first, passed = [], [], []
        for a in range(n):
            own = pltpu.make_async_copy(stage[a], blocks[a](outs[a], me), local.at[a])
            own.start()
            mine.append(own)
            if a >= n_now:
                continue
            sends = [copy(a, 0, me, sibling, src=stage[a])]
            sends += [copy(a, k, me, (*_chip(x, y, k), c), src=stage[a]) for k in (1, 2, 3)]
            for cp in sends:
                cp.start()
            first += sends
        for a in range(n_now):
            for k in (1, 2, 3):
                landed = _block_of(*_chip(x, y, k), c)
                copy(a, k, landed, (x, y, c)).wait_recv()
                fwd = copy(a, 3 + k, landed, sibling)
                fwd.start()
                passed.append(fwd)
        for a in range(n_now):
            copy(a, 0, _block_of(x, y, 1 - c), (x, y, c)).wait_recv()
            for k in (1, 2, 3):
                copy(a, 3 + k, _block_of(*_chip(x, y, k), 1 - c), (x, y, c)).wait_recv()
        for k in range(1, N_DEV):
            small_arrival(k).wait_recv()
        for cp in first + passed + small_sends:
            cp.wait_send()
        for own in mine:
            own.wait()

    out_shape = [jax.ShapeDtypeStruct(s, BF16) for s in full_shapes]
    out_shape.append(jax.ShapeDtypeStruct((N_DEV, small_rows, LANES), F32))
    return pl.pallas_call(
        body, name="gather_weights", out_shape=out_shape,
        in_specs=[VMEM_SPEC] * (n + 1), out_specs=[HBM_SPEC] * n + [VMEM_SPEC],
        scratch_shapes=[pltpu.VMEM(s.shape, BF16) for s in shards]
        + [pltpu.SemaphoreType.DMA((n_now + 1, 7)), pltpu.SemaphoreType.DMA((n_now + 1, 7)),
           pltpu.SemaphoreType.DMA((n,))],
        compiler_params=_params(),
    )(*shards, small)


def _gather_first(full, blocks, send, recv):
    x, y, c = _position()
    me = _block_of(x, y, c)
    peers = [(x, y, 1 - c)] + [(*_chip(x, y, k), c) for k in (1, 2, 3)]

    def copy(a, k, block):
        at = blocks[a](full[a], block)
        return pltpu.make_async_remote_copy(src_ref=at, dst_ref=at, send_sem=send[4 * a + k], recv_sem=recv[4 * a + k],
                                            device_id=peers[k], device_id_type=MESH)

    sends = [copy(a, k, me) for a in range(len(full)) for k in range(4)]
    arrivals = [copy(a, k, _block_of(*peers[k])) for a in range(len(full)) for k in range(4)]
    return sends, arrivals


def _gather_second(full, blocks, send, recv):
    x, y, c = _position()

    def copy(a, k, cc):
        at = blocks[a](full[a], _block_of(*_chip(x, y, k), cc))
        return pltpu.make_async_remote_copy(src_ref=at, dst_ref=at, send_sem=send[3 * a + k - 1],
                                            recv_sem=recv[3 * a + k - 1], device_id=(x, y, 1 - c), device_id_type=MESH)

    sends = [copy(a, k, c) for a in range(len(full)) for k in (1, 2, 3)]
    arrivals = [copy(a, k, 1 - c) for a in range(len(full)) for k in (1, 2, 3)]
    return sends, arrivals


def _split_call(body, name, arrays, sems_in, n_sems_out, after=None, token=False):
    n, m = len(arrays), len(sems_in)

    def kernel_body(*refs):
        outs = refs[n + m + (after is not None):]
        body(refs[:n], refs[n:n + m], outs[:n_sems_out])
        if token:
            outs[-1][...] = jnp.zeros_like(outs[-1])

    extra_in = [] if after is None else [after]
    outs = pl.pallas_call(
        kernel_body, name=name,
        out_shape=(*[pltpu.SemaphoreType.DMA(())] * n_sems_out, *[pltpu.HBM(a.shape, a.dtype) for a in arrays],
                   *([jax.ShapeDtypeStruct((SUBLANES, LANES), F32)] if token else [])),
        in_specs=[HBM_SPEC] * n + [SEM_SPEC] * m + [pl.BlockSpec(memory_space=pl.ANY)] * len(extra_in),
        out_specs=(*[SEM_SPEC] * n_sems_out, *[HBM_SPEC] * n, *([VMEM_SPEC] if token else [])),
        input_output_aliases={i: n_sems_out + i for i in range(n)},
        compiler_params=pltpu.CompilerParams(has_side_effects=DATAFLOW_EFFECT),
    )(*[pltpu.with_memory_space_constraint(a, pltpu.HBM) for a in arrays], *sems_in, *extra_in)
    sems, rest = list(outs[:n_sems_out]), list(outs[n_sems_out:])
    return (sems, rest[:n], rest[n]) if token else (sems, rest[:n])


def _gather_start(full, blocks, name):
    n = len(full)

    def body(arrays, _, sems):
        for cp in _gather_first(arrays, blocks, sems[:4 * n], sems[4 * n:])[0]:
            cp.start()

    sems, arrays, token = _split_call(body, name, full, [], 8 * n, token=True)
    return sems[:4 * n], sems[4 * n:], arrays, token


def _gather_forward(full, blocks, send_first, recv_first, after, name):
    n = len(full)

    def body(arrays, sems_in, sems):
        sends, arrivals = _gather_first(arrays, blocks, sems_in[:4 * n], sems_in[4 * n:])
        for cp in arrivals:
            cp.wait_recv()
        for cp in _gather_second(arrays, blocks, sems[:3 * n], sems[3 * n:])[0]:
            cp.start()
        for cp in sends:
            cp.wait_send()

    sems, arrays = _split_call(body, name, full, [*send_first, *recv_first], 6 * n, after=after)
    return sems[:3 * n], sems[3 * n:], arrays


def _gather_finish(full, blocks, send_second, recv_second, after, name):
    n = len(full)

    def body(arrays, sems_in, _):
        sends, arrivals = _gather_second(arrays, blocks, sems_in[:3 * n], sems_in[3 * n:])
        for cp in sends:
            cp.wait_send()
        for cp in arrivals:
            cp.wait_recv()

    return _split_call(body, name, full, [*send_second, *recv_second], 0, after=after)[1]


def _exchange_pair(grads, blocks, shard_shapes, name):
    n = len(grads)

    def body(*refs):
        ins, got = refs[:n], refs[n:2 * n]
        send, recv = refs[2 * n:]
        x, y, c = _position()
        copies = []
        for a in range(n):
            for k in range(4):
                cp = pltpu.make_async_remote_copy(
                    src_ref=blocks[a](ins[a], _block_of(*_chip(x, y, k), 1 - c)), dst_ref=got[a].at[k],
                    send_sem=send.at[a, k], recv_sem=recv.at[a, k], device_id=(x, y, 1 - c), device_id_type=MESH)
                cp.start()
                copies.append(cp)
        for cp in copies:
            cp.wait()

    return pl.pallas_call(
        body, name=name, out_shape=[jax.ShapeDtypeStruct((4,) + tuple(s), BF16) for s in shard_shapes],
        in_specs=[HBM_SPEC] * n, out_specs=[HBM_SPEC] * n,
        scratch_shapes=[pltpu.SemaphoreType.DMA((n, 4)), pltpu.SemaphoreType.DMA((n, 4))],
        compiler_params=_params(),
    )(*grads)


def _chip_copies(sums, lands, send, recv):
    x, y, c = _position()
    return [pltpu.make_async_remote_copy(
        src_ref=sums[a].at[k], dst_ref=lands[a].at[k - 1], send_sem=send[3 * a + k - 1], recv_sem=recv[3 * a + k - 1],
        device_id=(*_chip(x, y, k), c), device_id_type=MESH) for a in range(len(sums)) for k in (1, 2, 3)]


def _exchange_chips_start(pair_sums, name):
    n = len(pair_sums)
    lands = [pltpu.with_memory_space_constraint(lax.empty((3,) + tuple(p.shape[1:]), BF16), pltpu.HBM) for p in pair_sums]

    def body(*refs):
        sums, zones = refs[:n], refs[n:2 * n]
        send, recv = refs[2 * n:5 * n], refs[5 * n:8 * n]
        token = refs[-1]
        for cp in _chip_copies(sums, zones, send, recv):
            cp.start()
        token[...] = jnp.zeros_like(token)

    outs = pl.pallas_call(
        body, name=name,
        out_shape=(*[pltpu.SemaphoreType.DMA(())] * (6 * n),
                   *[pltpu.HBM(p.shape, BF16) for p in pair_sums], *[pltpu.HBM(z.shape, BF16) for z in lands],
                   jax.ShapeDtypeStruct((SUBLANES, LANES), F32)),
        in_specs=[HBM_SPEC] * (2 * n), out_specs=(*[SEM_SPEC] * (6 * n), *[HBM_SPEC] * (2 * n), VMEM_SPEC),
        input_output_aliases={i: 6 * n + i for i in range(2 * n)},
        compiler_params=pltpu.CompilerParams(has_side_effects=DATAFLOW_EFFECT),
    )(*[pltpu.with_memory_space_constraint(p, pltpu.HBM) for p in pair_sums], *lands)
    return outs[:3 * n], outs[3 * n:6 * n], outs[6 * n:7 * n], outs[7 * n:8 * n], outs[-1]


def _exchange_chips_wait(send, recv, sums, lands, after, name):
    n = len(sums)

    def body(*refs):
        sums_in, zones = refs[:n], refs[n:2 * n]
        send_in, recv_in = refs[2 * n:5 * n], refs[5 * n:8 * n]
        for cp in _chip_copies(sums_in, zones, send_in, recv_in):
            cp.wait_send()
            cp.wait_recv()

    outs = pl.pallas_call(
        body, name=name,
        out_shape=(*[pltpu.HBM(p.shape, BF16) for p in sums], *[pltpu.HBM(z.shape, BF16) for z in lands]),
        in_specs=[HBM_SPEC] * (2 * n) + [SEM_SPEC] * (6 * n) + [pl.BlockSpec(memory_space=pl.ANY)],
        out_specs=[HBM_SPEC] * (2 * n), input_output_aliases={i: i for i in range(2 * n)},
        compiler_params=pltpu.CompilerParams(has_side_effects=DATAFLOW_EFFECT),
    )(*sums, *lands, *send, *recv, after)
    return outs[:n], outs[n:]


def _small_copies(mine, land, send, recv):
    x, y, c = _position()
    me = _block_of(x, y, c)

    def peer(k):
        return (x + (k & 1)) % 2, (y + ((k >> 1) & 1)) % 2, (c + (k >> 2)) % 2

    def copy(k, slot):
        return pltpu.make_async_remote_copy(src_ref=mine, dst_ref=land.at[slot], send_sem=send[k - 1], recv_sem=recv[k - 1],
                                            device_id=peer(k), device_id_type=MESH)

    return [copy(k, me) for k in range(1, N_DEV)], [copy(k, _block_of(*peer(k))) for k in range(1, N_DEV)]


def _small_start(part, name):
    land = jnp.zeros((N_DEV,) + part.shape, F32)

    def body(arrays, _, sems):
        for cp in _small_copies(arrays[0], arrays[1], sems[:7], sems[7:])[0]:
            cp.start()

    sems, arrays = _split_call(body, name, [part, land], [], 14)
    return sems[:7], sems[7:], arrays[0], arrays[1]


def _small_wait(send, recv, part, land, after, name):
    def body(arrays, sems_in, _):
        sends, arrivals = _small_copies(arrays[0], arrays[1], sems_in[:7], sems_in[7:])
        for cp in sends:
            cp.wait_send()
        for cp in arrivals:
            cp.wait_recv()

    return _split_call(body, name, [part, land], [*send, *recv], 0, after=after)[1]


def _small_sum(pairs, me):
    n = len(pairs)

    def body(me_ref, *refs):
        for i in range(n):
            mine, land, out = refs[2 * i], refs[2 * i + 1], refs[2 * n + i]
            total = jnp.zeros(mine.shape, F32)
            for d in range(N_DEV):
                total = total + land[d] + jnp.where(me_ref[0] == d, mine[...], 0.0)
            out[...] = total

    flat = [a for pair in pairs for a in pair]
    return pl.pallas_call(
        body, name="small_sum", out_shape=[jax.ShapeDtypeStruct(mine.shape, F32) for mine, _ in pairs],
        in_specs=[pl.BlockSpec(memory_space=pltpu.SMEM)] + [VMEM_SPEC] * (2 * n), out_specs=[VMEM_SPEC] * n,
        compiler_params=_params(),
    )(me.reshape(1).astype(jnp.int32), *flat)


def _in_proj(x, g1, w_in):
    t = x.shape[0]
    tm, bn = min(1024, t), 1024

    def body(x_ref, g_ref, w_ref, proj_ref, h_ref, h_s):
        @pl.when(pl.program_id(1) == 0)
        def _():
            n, _ = _rms_fwd(x_ref[...])
            h_s[...] = (n * g_ref[...]).astype(BF16)
            h_ref[...] = h_s[...]
        proj_ref[...] = _dot(h_s[...], w_ref[...]).astype(BF16)

    return pl.pallas_call(
        body, name="in_proj", grid=(t // tm, IN_COLS // bn),
        out_shape=[jax.ShapeDtypeStruct((t, IN_COLS), BF16), jax.ShapeDtypeStruct((t, D_MODEL), BF16)],
        in_specs=[pl.BlockSpec((tm, D_MODEL), lambda i, j: (i, 0)), pl.BlockSpec((1, D_MODEL), lambda i, j: (0, 0)),
                  pl.BlockSpec((D_MODEL, bn), lambda i, j: (0, j))],
        out_specs=[pl.BlockSpec((tm, bn), lambda i, j: (i, j)), pl.BlockSpec((tm, D_MODEL), lambda i, j: (i, 0))],
        scratch_shapes=[pltpu.VMEM((tm, D_MODEL), BF16)],
        compiler_params=_params("parallel", "arbitrary"),
    )(x, g1, w_in)


def _section(s, t):
    return pl.BlockSpec((t, CB), lambda h, s=s: (0, s * (D_MODEL // CB) + h))


def _conv_mixer_fwd(proj, w_short):
    t = proj.shape[0]
    rc = _row_chunk(t)

    def body(b_ref, c_ref, x_ref, w_ref, y_ref, pad):
        pad[pl.ds(0, PAD), :] = jnp.zeros((PAD, CB), F32)
        for r0 in range(0, t, rc):
            rows = pl.ds(r0, rc)
            pad[pl.ds(PAD + r0, rc), :] = c_ref[rows, :].astype(F32) * x_ref[rows, :].astype(F32)
        w = w_ref[...]
        for r0 in range(0, t, rc):
            rows = pl.ds(r0, rc)
            y_ref[rows, :] = (b_ref[rows, :].astype(F32) * _conv_causal(pad, w, r0, rc, 3)).astype(BF16)

    return pl.pallas_call(
        body, name="conv_mixer_fwd", grid=(D_MODEL // CB,),
        out_shape=jax.ShapeDtypeStruct((t, D_MODEL), BF16),
        in_specs=[_section(0, t), _section(1, t), _section(2, t), pl.BlockSpec((3, CB), lambda h: (0, h))],
        out_specs=pl.BlockSpec((t, CB), lambda h: (0, h)),
        scratch_shapes=[pltpu.VMEM((t + PAD, CB), F32)],
        compiler_params=_params("parallel"),
    )(proj, proj, proj, w_short)


def _lru_gates(xl, wa, ba, wx, bx, ls, first_row):
    xb = xl.astype(BF16)
    ra = jax.nn.sigmoid(_dot(xb, wa) + ba)
    ia = jax.nn.sigmoid(_dot(xb, wx) + bx)
    la = LRU_C * ra * ls
    a = jnp.exp(la)
    one_minus = -_expm1_neg(2.0 * la)
    mult = jnp.where(first_row, 1.0, jnp.sqrt(one_minus))
    return xb, ra, ia, a, one_minus, mult


def _head_specs():
    vec = pl.BlockSpec((1, CB), lambda h: (0, h))
    mat = pl.BlockSpec((N_DEV, None, HEAD_DIM // N_DEV, HEAD_DIM), lambda h: (0, h, 0, 0))
    return vec, mat


def _lru_fwd(proj, w_conv, b_conv, wa, ba, wx, bx, lam):
    t = proj.shape[0]
    rc = _row_chunk(t)
    vec, mat = _head_specs()

    def body(lx_ref, ly_ref, wc_ref, bc_ref, wa_ref, ba_ref, wx_ref, bx_ref, lam_ref, yb_ref, hl_ref, pad, a_s, u_s):
        pad[pl.ds(0, PAD), :] = jnp.zeros((PAD, CB), F32)
        for r0 in range(0, t, rc):
            pad[pl.ds(PAD + r0, rc), :] = lx_ref[pl.ds(r0, rc), :].astype(F32)
        wc, bc = wc_ref[...], bc_ref[...]
        wa_m, wx_m = wa_ref[...].reshape(HEAD_DIM, HEAD_DIM), wx_ref[...].reshape(HEAD_DIM, HEAD_DIM)
        ls = _log_sigmoid(lam_ref[...])
        for r0 in range(0, t, rc):
            xl = _conv_causal(pad, wc, r0, rc, 4) + bc
            first = (lax.broadcasted_iota(jnp.int32, (rc, CB), 0) + r0) == 0
            _, _, ia, a, _, mult = _lru_gates(xl, wa_m, ba_ref[...], wx_m, bx_ref[...], ls, first)
            a_s[pl.ds(r0, rc), :] = a
            u_s[pl.ds(r0, rc), :] = mult * (ia * xl)

        row = lax.broadcasted_iota(jnp.int32, (SUBLANES, CB), 0)

        def group(g, carry):
            r = pl.multiple_of(g * SUBLANES, SUBLANES)
            a_g, b_g = a_s[pl.ds(r, SUBLANES), :], u_s[pl.ds(r, SUBLANES), :]
            for s in (1, 2, 4):
                keep = row >= s
                b_g = jnp.where(keep, a_g * pltpu.roll(b_g, s, 0) + b_g, b_g)
                a_g = jnp.where(keep, a_g * pltpu.roll(a_g, s, 0), a_g)
            h_g = b_g + a_g * carry
            hl_ref[pl.ds(r, SUBLANES), :] = h_g
            return jnp.broadcast_to(h_g[SUBLANES - 1:SUBLANES, :], (SUBLANES, CB))

        lax.fori_loop(0, t // SUBLANES, group, jnp.zeros((SUBLANES, CB), F32))
        for r0 in range(0, t, rc):
            rows = pl.ds(r0, rc)
            yb_ref[rows, :] = (hl_ref[rows, :] * _gelu(ly_ref[rows, :].astype(F32))).astype(BF16)

    blk = pl.BlockSpec((t, CB), lambda h: (0, h))
    return pl.pallas_call(
        body, name="lru_fwd", grid=(N_HEADS,),
        out_shape=[jax.ShapeDtypeStruct((t, D_MODEL), BF16), jax.ShapeDtypeStruct((t, D_MODEL), F32)],
        in_specs=[_section(3, t), _section(4, t), pl.BlockSpec((4, CB), lambda h: (0, h)), vec, mat, vec, mat, vec, vec],
        out_specs=[blk, blk],
        scratch_shapes=[pltpu.VMEM((t + PAD, CB), F32), pltpu.VMEM((t, CB), F32), pltpu.VMEM((t, CB), F32)],
        compiler_params=_params("parallel"),
    )(proj, proj, w_conv, b_conv, wa, ba, wx, bx, lam)


def _merge(y_a, y_b, proj, x, w_cb, w_lb, w_out, g2, g3):
    t = x.shape[0]
    tm = min(256, t)

    def body(ya_ref, yb_ref, gc_ref, gl_ref, x_ref, wcb_ref, wlb_ref, wo_ref, g2_ref, g3_ref,
             pa_ref, pb_ref, mg_ref, mix_ref, x1_ref, h2_ref):
        pa = _dot(ya_ref[...], wcb_ref[...]).astype(BF16)
        pb = _dot(yb_ref[...], wlb_ref[...]).astype(BF16)
        pa_ref[...] = pa
        pb_ref[...] = pb
        merged = (jax.nn.sigmoid(gc_ref[...].astype(F32)) * pa.astype(F32)
                  + jax.nn.sigmoid(gl_ref[...].astype(F32)) * pb.astype(F32)).astype(BF16)
        mg_ref[...] = merged
        mix = _dot(merged, wo_ref[...])
        mix_ref[...] = mix
        n2, _ = _rms_fwd(mix)
        x1 = x_ref[...] + n2 * g2_ref[...]
        x1_ref[...] = x1
        n3, _ = _rms_fwd(x1)
        h2_ref[...] = (n3 * g3_ref[...]).astype(BF16)

    row = pl.BlockSpec((tm, D_MODEL), lambda i: (i, 0))
    full = pl.BlockSpec((D_MODEL, D_MODEL), lambda i: (0, 0))
    vec = pl.BlockSpec((1, D_MODEL), lambda i: (0, 0))
    act = jax.ShapeDtypeStruct((t, D_MODEL), BF16)
    res = jax.ShapeDtypeStruct((t, D_MODEL), F32)
    return pl.pallas_call(
        body, name="merge_fwd", grid=(t // tm,), out_shape=[act, act, act, res, res, act],
        in_specs=[row, row, pl.BlockSpec((tm, D_MODEL), lambda i: (i, 5)), pl.BlockSpec((tm, D_MODEL), lambda i: (i, 6)),
                  row, full, full, full, vec, vec],
        out_specs=[row] * 6,
        compiler_params=_params("parallel"),
    )(y_a, y_b, proj, proj, x, w_cb, w_lb, w_out, g2, g3)


N_FF_BLOCKS = D_FF // CB


def _ffn_up(h2, w_up, w_conv, b_conv):
    t = h2.shape[0]
    rc = _row_chunk(t)
    nb = N_FF_BLOCKS

    def body(h_ref, w_ref, c_ref, b_ref, up_ref, act_ref, f_ref, pad, gate):
        k = pl.program_id(1)
        pad[pl.ds(0, PAD), :] = jnp.zeros((PAD, CB), F32)
        for r0 in range(0, t, rc):
            rows = pl.ds(r0, rc)
            up = _dot(h_ref[rows, :], w_ref[...]).astype(BF16)
            up_ref[rows, :] = up
            pad[pl.ds(PAD + r0, rc), :] = up.astype(F32)
        cw = c_ref[...]
        for r0 in range(0, t, rc):
            rows = pl.ds(r0, rc)
            act = _conv_causal(pad, cw, r0, rc, 3) + b_ref[...]
            act_ref[rows, :] = act.astype(BF16)

            @pl.when(k == 0)
            def _():
                gate[rows, :] = act

            @pl.when(k == 1)
            def _():
                f_ref[rows, :] = (_gelu(gate[rows, :]) * act).astype(BF16)

    half = lambda rows: pl.BlockSpec((rows, CB), lambda j, k: (0, nb * k + j))
    wide = jax.ShapeDtypeStruct((t, 2 * D_FF), BF16)
    return pl.pallas_call(
        body, name="ffn_up_fwd", grid=(nb, 2), out_shape=[wide, wide, jax.ShapeDtypeStruct((t, D_FF), BF16)],
        in_specs=[pl.BlockSpec((t, D_MODEL), lambda j, k: (0, 0)), half(D_MODEL), half(3), half(1)],
        out_specs=[half(t), half(t), pl.BlockSpec((t, CB), lambda j, k: (0, j))],
        scratch_shapes=[pltpu.VMEM((t + PAD, CB), F32), pltpu.VMEM((t, CB), F32)],
        compiler_params=_params("parallel", "arbitrary"),
    )(h2, w_up, w_conv, b_conv)


def _ffn_down(f, act, w_down, x1, target, g4):
    t = f.shape[0]
    tm = min(256, t)
    cc = 512

    def body(f_ref, act_ref, w_ref, x1_ref, tg_ref, g_ref, dy_ref, dout_ref, back_ref, dg_ref, loss_ref):
        @pl.when(pl.program_id(0) == 0)
        def _():
            dg_ref[...] = jnp.zeros_like(dg_ref)
            loss_ref[...] = jnp.zeros_like(loss_ref)
        out = _dot(f_ref[...], w_ref[...])
        n4, r4 = _rms_fwd(out)
        err = x1_ref[...] + n4 * g_ref[...] - tg_ref[...]
        loss_ref[...] += jnp.full(loss_ref.shape, 0.5 / D_MODEL, F32) * jnp.sum(err * err)
        dy = err * (1.0 / D_MODEL)
        dy_ref[...] = dy
        dg_ref[...] += jnp.sum(dy * n4, axis=0, keepdims=True)
        d_out = _rms_bwd(n4, r4, dy * g_ref[...]).astype(BF16)
        dout_ref[...] = d_out
        for c0 in range(0, D_FF, cc):
            d_f = _dot_nt(d_out, w_ref[pl.ds(c0, cc), :])
            gelu, d_gelu = _gelu_and_grad(act_ref[:, pl.ds(c0, cc)].astype(F32))
            val = act_ref[:, pl.ds(D_FF + c0, cc)].astype(F32)
            back_ref[:, pl.ds(c0, cc)] = (d_f * val * d_gelu).astype(BF16)
            back_ref[:, pl.ds(D_FF + c0, cc)] = (d_f * gelu).astype(BF16)

    row = pl.BlockSpec((tm, D_MODEL), lambda i: (i, 0))
    wide = pl.BlockSpec((tm, 2 * D_FF), lambda i: (i, 0))
    vec = pl.BlockSpec((1, D_MODEL), lambda i: (0, 0))
    return pl.pallas_call(
        body, name="ffn_down_fwd_bwd", grid=(t // tm,),
        out_shape=[jax.ShapeDtypeStruct((t, D_MODEL), F32), jax.ShapeDtypeStruct((t, D_MODEL), BF16),
                   jax.ShapeDtypeStruct((t, 2 * D_FF), BF16), jax.ShapeDtypeStruct((1, D_MODEL), F32),
                   jax.ShapeDtypeStruct((SUBLANES, LANES), F32)],
        in_specs=[pl.BlockSpec((tm, D_FF), lambda i: (i, 0)), wide, pl.BlockSpec((D_FF, D_MODEL), lambda i: (0, 0)),
                  row, row, vec],
        out_specs=[row, row, wide, vec, pl.BlockSpec((SUBLANES, LANES), lambda i: (0, 0))],
        compiler_params=_params("arbitrary"),
    )(f, act, w_down, x1, target, g4)


def _grad_tn(a, b, bm, name):
    t, m = a.shape
    n = b.shape[1]

    def body(a_ref, b_ref, o_ref):
        o_ref[...] = _dot_tn(a_ref[...], b_ref[...]).astype(BF16)

    return pl.pallas_call(
        body, name=name, grid=(m // bm,), out_shape=jax.ShapeDtypeStruct((m, n), BF16),
        in_specs=[pl.BlockSpec((t, bm), lambda i: (0, i)), pl.BlockSpec((t, n), lambda i: (0, 0))],
        out_specs=pl.BlockSpec((bm, n), lambda i: (i, 0)),
        compiler_params=_params("parallel"),
    )(a, b)


def _ffn_up_bwd(up, back, w_conv, h2, w_up):
    t = h2.shape[0]
    rc = _row_chunk(t)
    nb = N_FF_BLOCKS

    def body(up_ref, back_ref, c_ref, h_ref, w_ref, dw_ref, dcw_ref, dcb_ref, dh_ref, pad, after, d_up):
        @pl.when((pl.program_id(0) == 0) & (pl.program_id(1) == 0))
        def _():
            dh_ref[...] = jnp.zeros_like(dh_ref)
        pad[pl.ds(0, PAD), :] = jnp.zeros((PAD, CB), F32)
        after[pl.ds(t, PAD), :] = jnp.zeros((PAD, CB), F32)
        for r0 in range(0, t, rc):
            pad[pl.ds(PAD + r0, rc), :] = up_ref[pl.ds(r0, rc), :].astype(F32)
            after[pl.ds(r0, rc), :] = back_ref[pl.ds(r0, rc), :].astype(F32)
        cw = c_ref[...]
        taps = [jnp.zeros((1, CB), F32)] * 3
        bias = jnp.zeros((1, CB), F32)
        for r0 in range(0, t, rc):
            rows = pl.ds(r0, rc)
            d = _conv_anticausal(after, cw, r0, rc, 3).astype(BF16)
            d_up[rows, :] = d
            dh_ref[rows, :] += _dot_nt(d, w_ref[...])
            g = after[rows, :]
            taps = [acc + new for acc, new in zip(taps, _conv_wgrad(g, pad, r0, rc, 3))]
            bias = bias + jnp.sum(g, axis=0, keepdims=True)
        dw_ref[...] = _dot_tn(h_ref[...], d_up[...]).astype(BF16)
        dcw_ref[...] = jnp.concatenate(taps, axis=0)
        dcb_ref[...] = bias

    half = lambda rows: pl.BlockSpec((rows, CB), lambda j, k: (0, nb * k + j))
    whole = pl.BlockSpec((t, D_MODEL), lambda j, k: (0, 0))
    return pl.pallas_call(
        body, name="ffn_up_bwd", grid=(nb, 2),
        out_shape=[jax.ShapeDtypeStruct((D_MODEL, 2 * D_FF), BF16), jax.ShapeDtypeStruct((3, 2 * D_FF), F32),
                   jax.ShapeDtypeStruct((1, 2 * D_FF), F32), jax.ShapeDtypeStruct((t, D_MODEL), F32)],
        in_specs=[half(t), half(t), half(3), whole, half(D_MODEL)],
        out_specs=[half(D_MODEL), half(3), half(1), whole],
        scratch_shapes=[pltpu.VMEM((t + PAD, CB), F32), pltpu.VMEM((t + PAD, CB), F32), pltpu.VMEM((t, CB), BF16)],
        compiler_params=_params("arbitrary", "arbitrary"),
    )(up, back, w_conv, h2, w_up)


def _merge_bwd(dy, d_h2, x1, mix, g3, g2, w_out, w_cb, w_lb, pa, pb, proj):
    t = dy.shape[0]
    tm = min(256, t)

    def body(dy_ref, dh2_ref, x1_ref, mix_ref, g3_ref, g2_ref, wo_ref, wcb_ref, wlb_ref, pa_ref, pb_ref, gc_ref, gl_ref,
             dx1_ref, dmix_ref, dpa_ref, dpb_ref, dya_ref, dyb_ref, dgate_ref, dg3_ref, dg2_ref):
        @pl.when(pl.program_id(0) == 0)
        def _():
            dg3_ref[...] = jnp.zeros_like(dg3_ref)
            dg2_ref[...] = jnp.zeros_like(dg2_ref)
        n3, r3 = _rms_fwd(x1_ref[...])
        d_h2 = dh2_ref[...]
        dg3_ref[...] += jnp.sum(d_h2 * n3, axis=0, keepdims=True)
        dx1 = dy_ref[...] + _rms_bwd(n3, r3, d_h2 * g3_ref[...])
        dx1_ref[...] = dx1
        n2, r2 = _rms_fwd(mix_ref[...])
        dg2_ref[...] += jnp.sum(dx1 * n2, axis=0, keepdims=True)
        d_mix = _rms_bwd(n2, r2, dx1 * g2_ref[...]).astype(BF16)
        dmix_ref[...] = d_mix
        d_merged = _dot_nt(d_mix, wo_ref[...])
        sc = jax.nn.sigmoid(gc_ref[...].astype(F32))
        sl = jax.nn.sigmoid(gl_ref[...].astype(F32))
        d_pa = (d_merged * sc).astype(BF16)
        d_pb = (d_merged * sl).astype(BF16)
        dpa_ref[...] = d_pa
        dpb_ref[...] = d_pb
        dgate_ref[0] = (d_merged * pa_ref[...].astype(F32) * sc * (1.0 - sc)).astype(BF16)
        dgate_ref[1] = (d_merged * pb_ref[...].astype(F32) * sl * (1.0 - sl)).astype(BF16)
        dya_ref[...] = _dot_nt(d_pa, wcb_ref[...]).astype(BF16)
        dyb_ref[...] = _dot_nt(d_pb, wlb_ref[...]).astype(BF16)

    row = pl.BlockSpec((tm, D_MODEL), lambda i: (i, 0))
    full = pl.BlockSpec((D_MODEL, D_MODEL), lambda i: (0, 0))
    vec = pl.BlockSpec((1, D_MODEL), lambda i: (0, 0))
    act = jax.ShapeDtypeStruct((t, D_MODEL), BF16)
    small = jax.ShapeDtypeStruct((1, D_MODEL), F32)
    return pl.pallas_call(
        body, name="merge_bwd", grid=(t // tm,),
        out_shape=[jax.ShapeDtypeStruct((t, D_MODEL), F32), act, act, act, act, act,
                   jax.ShapeDtypeStruct((2, t, D_MODEL), BF16), small, small],
        in_specs=[row, row, row, row, vec, vec, full, full, full, row, row,
                  pl.BlockSpec((tm, D_MODEL), lambda i: (i, 5)), pl.BlockSpec((tm, D_MODEL), lambda i: (i, 6))],
        out_specs=[row] * 6 + [pl.BlockSpec((2, tm, D_MODEL), lambda i: (0, i, 0)), vec, vec],
        compiler_params=_params("arbitrary"),
    )(dy, d_h2, x1, mix, g3, g2, w_out, w_cb, w_lb, pa, pb, proj, proj)


def _conv_mixer_bwd(proj, d_ya, w_short):
    t = proj.shape[0]
    rc = _row_chunk(t)

    def body(b_ref, c_ref, x_ref, dy_ref, w_ref, d_ref, dw_ref, pad, back):
        pad[pl.ds(0, PAD), :] = jnp.zeros((PAD, CB), F32)
        back[pl.ds(t, PAD), :] = jnp.zeros((PAD, CB), F32)
        for r0 in range(0, t, rc):
            rows = pl.ds(r0, rc)
            pad[pl.ds(PAD + r0, rc), :] = c_ref[rows, :].astype(F32) * x_ref[rows, :].astype(F32)
        w = w_ref[...]
        for r0 in range(0, t, rc):
            rows = pl.ds(r0, rc)
            d_y = dy_ref[rows, :].astype(F32)
            d_ref[0, rows, :] = (d_y * _conv_causal(pad, w, r0, rc, 3)).astype(BF16)
            back[rows, :] = d_y * b_ref[rows, :].astype(F32)
        taps = [jnp.zeros((1, CB), F32)] * 3
        for r0 in range(0, t, rc):
            rows = pl.ds(r0, rc)
            d_u = _conv_anticausal(back, w, r0, rc, 3)
            d_ref[1, rows, :] = (d_u * x_ref[rows, :].astype(F32)).astype(BF16)
            d_ref[2, rows, :] = (d_u * c_ref[rows, :].astype(F32)).astype(BF16)
            taps = [acc + new for acc, new in zip(taps, _conv_wgrad(back[rows, :], pad, r0, rc, 3))]
        dw_ref[...] = jnp.concatenate(taps, axis=0)

    blk = pl.BlockSpec((t, CB), lambda h: (0, h))
    return pl.pallas_call(
        body, name="conv_mixer_bwd", grid=(D_MODEL // CB,),
        out_shape=[jax.ShapeDtypeStruct((3, t, D_MODEL), BF16), jax.ShapeDtypeStruct((3, D_MODEL), F32)],
        in_specs=[_section(0, t), _section(1, t), _section(2, t), blk, pl.BlockSpec((3, CB), lambda h: (0, h))],
        out_specs=[pl.BlockSpec((3, t, CB), lambda h: (0, 0, h)), pl.BlockSpec((3, CB), lambda h: (0, h))],
        scratch_shapes=[pltpu.VMEM((t + PAD, CB), F32), pltpu.VMEM((t + PAD, CB), F32)],
        compiler_params=_params("parallel"),
    )(proj, proj, proj, d_ya, w_short)


LRU_SMALL_ROWS = 8


def _lru_bwd(proj, hl, d_yb, w_conv, b_conv, wa, ba, wx, bx, lam):
    t = proj.shape[0]
    rc = _row_chunk(t)
    vec, mat = _head_specs()

    def body(lx_ref, ly_ref, hl_ref, dy_ref, wc_ref, bc_ref, wa_ref, ba_ref, wx_ref, bx_ref, lam_ref,
             d_ref, dwa_ref, dwx_ref, small_ref, pad, a_next, dh_s, h_prev, back, acc_a, acc_x):
        zeros = jnp.zeros((PAD, CB), F32)
        pad[pl.ds(0, PAD), :] = zeros
        h_prev[pl.ds(0, PAD), :] = zeros
        a_next[pl.ds(t, PAD), :] = zeros
        back[pl.ds(t, PAD), :] = zeros
        for r0 in range(0, t, rc):
            pad[pl.ds(PAD + r0, rc), :] = lx_ref[pl.ds(r0, rc), :].astype(F32)
            h_prev[pl.ds(PAD + r0, rc), :] = hl_ref[pl.ds(r0, rc), :]
        wc, bc = wc_ref[...], bc_ref[...]
        wa_m, wx_m = wa_ref[...].reshape(HEAD_DIM, HEAD_DIM), wx_ref[...].reshape(HEAD_DIM, HEAD_DIM)
        ls = _log_sigmoid(lam_ref[...])

        def gates(r0):
            xl = _conv_causal(pad, wc, r0, rc, 4) + bc
            first = (lax.broadcasted_iota(jnp.int32, (rc, CB), 0) + r0) == 0
            return (xl, first) + _lru_gates(xl, wa_m, ba_ref[...], wx_m, bx_ref[...], ls, first)

        for r0 in range(0, t, rc):
            rows = pl.ds(r0, rc)
            a = gates(r0)[5]
            a_next[pl.ds(PAD - 1 + r0, rc), :] = a
            act, d_act = _gelu_and_grad(ly_ref[rows, :].astype(F32))
            d_y = dy_ref[rows, :].astype(F32)
            dh_s[rows, :] = d_y * act
            d_ref[1, rows, :] = (d_y * hl_ref[rows, :] * d_act).astype(BF16)

        row = lax.broadcasted_iota(jnp.int32, (SUBLANES, CB), 0)
        groups = t // SUBLANES

        def group(i, carry):
            r = pl.multiple_of((groups - 1 - i) * SUBLANES, SUBLANES)
            a_g, b_g = a_next[pl.ds(PAD + r, SUBLANES), :], dh_s[pl.ds(r, SUBLANES), :]
            for s in (1, 2, 4):
                keep = row < SUBLANES - s
                b_g = jnp.where(keep, a_g * pltpu.roll(b_g, SUBLANES - s, 0) + b_g, b_g)
                a_g = jnp.where(keep, a_g * pltpu.roll(a_g, SUBLANES - s, 0), a_g)
            d_g = b_g + a_g * carry
            dh_s[pl.ds(r, SUBLANES), :] = d_g
            return jnp.broadcast_to(d_g[0:1, :], (SUBLANES, CB))

        lax.fori_loop(0, groups, group, jnp.zeros((SUBLANES, CB), F32))

        acc_a[...] = jnp.zeros_like(acc_a)
        acc_x[...] = jnp.zeros_like(acc_x)
        d_ba = d_bx = d_ls = jnp.zeros((1, CB), F32)
        for r0 in range(0, t, rc):
            rows = pl.ds(r0, rc)
            xl, first, xb, ra, ia, a, one_minus, mult = gates(r0)
            d_h = dh_s[rows, :]
            d_a = d_h * h_prev[pl.ds(PAD - 1 + r0, rc), :]
            d_mult = d_h * ia * xl
            d_ia = d_h * mult * xl
            d_xl = d_h * mult * ia
            d_mult_d_la = jnp.where(first, 0.0, (one_minus - 1.0) / mult)
            d_la = d_a * a + d_mult * d_mult_d_la
            d_ls = d_ls + jnp.sum(d_la * ra, axis=0, keepdims=True) * LRU_C
            d_za = d_la * (LRU_C * ls) * ra * (1.0 - ra)
            d_zx = d_ia * ia * (1.0 - ia)
            d_ba = d_ba + jnp.sum(d_za, axis=0, keepdims=True)
            d_bx = d_bx + jnp.sum(d_zx, axis=0, keepdims=True)
            d_za, d_zx = d_za.astype(BF16), d_zx.astype(BF16)
            acc_a[...] += _dot_tn(xb, d_za)
            acc_x[...] += _dot_tn(xb, d_zx)
            back[rows, :] = d_xl + _dot_nt(d_za, wa_m) + _dot_nt(d_zx, wx_m)
        taps = [jnp.zeros((1, CB), F32)] * 4
        d_bc = jnp.zeros((1, CB), F32)
        for r0 in range(0, t, rc):
            rows = pl.ds(r0, rc)
            d_ref[0, rows, :] = _conv_anticausal(back, wc, r0, rc, 4).astype(BF16)
            g = back[rows, :]
            taps = [acc + new for acc, new in zip(taps, _conv_wgrad(g, pad, r0, rc, 4))]
            d_bc = d_bc + jnp.sum(g, axis=0, keepdims=True)
        d_lam = d_ls * jax.nn.sigmoid(-lam_ref[...])
        small_ref[...] = jnp.concatenate(taps + [d_bc, d_ba, d_bx, d_lam], axis=0)
        dwa_ref[...] = acc_a[...].reshape(N_DEV, HEAD_DIM // N_DEV, HEAD_DIM).astype(BF16)
        dwx_ref[...] = acc_x[...].reshape(N_DEV, HEAD_DIM // N_DEV, HEAD_DIM).astype(BF16)

    blk = pl.BlockSpec((t, CB), lambda h: (0, h))
    gate_grad = jax.ShapeDtypeStruct((N_DEV, N_HEADS, HEAD_DIM // N_DEV, HEAD_DIM), BF16)
    return pl.pallas_call(
        body, name="lru_bwd", grid=(N_HEADS,),
        out_shape=[jax.ShapeDtypeStruct((2, t, D_MODEL), BF16), gate_grad, gate_grad,
                   jax.ShapeDtypeStruct((LRU_SMALL_ROWS, D_MODEL), F32)],
        in_specs=[_section(3, t), _section(4, t), blk, blk, pl.BlockSpec((4, CB), lambda h: (0, h)),
                  vec, mat, vec, mat, vec, vec],
        out_specs=[pl.BlockSpec((2, t, CB), lambda h: (0, 0, h)), mat, mat,
                   pl.BlockSpec((LRU_SMALL_ROWS, CB), lambda h: (0, h))],
        scratch_shapes=[pltpu.VMEM((t + PAD, CB), F32), pltpu.VMEM((t + PAD, CB), F32), pltpu.VMEM((t, CB), F32),
                        pltpu.VMEM((t + PAD, CB), F32), pltpu.VMEM((t + PAD, CB), F32),
                        pltpu.VMEM((HEAD_DIM, HEAD_DIM), F32), pltpu.VMEM((HEAD_DIM, HEAD_DIM), F32)],
        compiler_params=_params("parallel"),
    )(proj, proj, hl, d_yb, w_conv, b_conv, wa, ba, wx, bx, lam)


def _stack_maps(halves):
    def conv(sec, part):
        return jnp.minimum(sec, 2), jnp.where(sec < 3, part, halves - 1)

    def lru(sec, part):
        return jnp.clip(sec - 3, 0, 1), jnp.where(sec < 3, 0, jnp.where(sec < 5, part, halves - 1))

    def gate(sec, part):
        return jnp.clip(sec - 5, 0, 1), jnp.where(sec < 5, 0, part)

    return conv, lru, gate


def _pick_stack(sec, refs, fn):
    @pl.when(sec < 3)
    def _():
        fn(refs[0])

    @pl.when((sec >= 3) & (sec < 5))
    def _():
        fn(refs[1])

    @pl.when(sec >= 5)
    def _():
        fn(refs[2])


def _in_proj_wgrad(h, d_conv, d_lru, d_gate):
    t = h.shape[0]
    halves, bn = 2, D_MODEL // 2
    maps = _stack_maps(halves)

    def body(h_ref, dc_ref, dl_ref, dg_ref, o_ref):
        def emit(ref):
            o_ref[...] = _dot_tn(h_ref[...], ref[...]).astype(BF16)
        _pick_stack(pl.program_id(0) // halves, (dc_ref, dl_ref, dg_ref), emit)

    def spec(m):
        def index(s):
            stack, part = m(s // halves, s % halves)
            return stack, 0, part
        return pl.BlockSpec((None, t, bn), index)

    return pl.pallas_call(
        body, name="in_proj_wgrad", grid=(7 * halves,), out_shape=jax.ShapeDtypeStruct((D_MODEL, IN_COLS), BF16),
        in_specs=[pl.BlockSpec((t, D_MODEL), lambda s: (0, 0))] + [spec(m) for m in maps],
        out_specs=pl.BlockSpec((D_MODEL, bn), lambda s: (0, s)),
        compiler_params=_params("arbitrary"),
    )(h, d_conv, d_lru, d_gate)


def _in_proj_xgrad(d_conv, d_lru, d_gate, w_in, x, dx1, g1):
    t = x.shape[0]
    tm = min(1024, t)
    maps = _stack_maps(1)

    def body(dc_ref, dl_ref, dg_ref, w_ref, x_ref, dx1_ref, g_ref, dx_ref, dgain_ref, acc):
        i, s = pl.program_id(0), pl.program_id(1)

        @pl.when((i == 0) & (s == 0))
        def _():
            dgain_ref[...] = jnp.zeros_like(dgain_ref)

        @pl.when(s == 0)
        def _():
            acc[...] = jnp.zeros_like(acc)

        def add(ref):
            acc[...] += _dot_nt(ref[...], w_ref[...])
        _pick_stack(s, (dc_ref, dl_ref, dg_ref), add)

        @pl.when(s == 6)
        def _():
            n1, r1 = _rms_fwd(x_ref[...])
            d_h = acc[...]
            dgain_ref[...] += jnp.sum(d_h * n1, axis=0, keepdims=True)
            dx_ref[...] = dx1_ref[...] + _rms_bwd(n1, r1, d_h * g_ref[...])

    def spec(m):
        def index(i, s):
            return m(s, 0)[0], i, 0
        return pl.BlockSpec((None, tm, D_MODEL), index)

    row = pl.BlockSpec((tm, D_MODEL), lambda i, s: (i, 0))
    vec = pl.BlockSpec((1, D_MODEL), lambda i, s: (0, 0))
    return pl.pallas_call(
        body, name="in_proj_xgrad", grid=(t // tm, 7),
        out_shape=[jax.ShapeDtypeStruct((t, D_MODEL), F32), jax.ShapeDtypeStruct((1, D_MODEL), F32)],
        in_specs=[spec(m) for m in maps] + [pl.BlockSpec((D_MODEL, D_MODEL), lambda i, s: (0, s)), row, row, vec],
        out_specs=[row, vec],
        scratch_shapes=[pltpu.VMEM((tm, D_MODEL), F32)],
        compiler_params=_params("arbitrary", "arbitrary"),
    )(d_conv, d_lru, d_gate, w_in, x, dx1, g1)


def _add_pair(grad, got, by_cols, pos, name):
    cols = got.shape[-1]
    got3 = got.reshape(4, -1, cols)
    rows = got3.shape[1]
    rb = _row_block(rows, 1024)

    def block(k, p):
        return 4 * ((p[0] + k % 2) % 2) + 2 * ((p[1] + k // 2) % 2) + p[2]

    if by_cols:
        g_in, g_spec = grad, pl.BlockSpec((rb, cols), lambda k, i, p: (i, block(k, p)))
    else:
        g_in = grad.reshape(N_DEV, rows, cols)
        g_spec = pl.BlockSpec((None, rb, cols), lambda k, i, p: (block(k, p), i, 0))
    slot = pl.BlockSpec((None, rb, cols), lambda k, i, p: (k, i, 0))

    def body(pos_ref, a_ref, b_ref, o_ref):
        o_ref[...] = (a_ref[...].astype(F32) + b_ref[...].astype(F32)).astype(BF16)

    out = pl.pallas_call(
        body, name=name, out_shape=jax.ShapeDtypeStruct(got3.shape, BF16),
        grid_spec=pltpu.PrefetchScalarGridSpec(num_scalar_prefetch=1, grid=(4, rows // rb),
                                               in_specs=[g_spec, slot], out_specs=slot),
        compiler_params=_params("parallel", "parallel"),
    )(pos, g_in, got3)
    return out.reshape(got.shape)


def _adamw(w, g, m, v):
    m = ADAM_B1 * m + (1.0 - ADAM_B1) * g
    v = ADAM_B2 * v + (1.0 - ADAM_B2) * (g * g)
    m_hat = m / (1.0 - ADAM_B1 ** ADAM_STEP)
    v_hat = v / (1.0 - ADAM_B2 ** ADAM_STEP)
    return -ADAM_LR * (m_hat / (jnp.sqrt(v_hat) + ADAM_EPS) + ADAM_WD * w), m, v


def _adam_large(w, m, v, own, others, name):
    shape = w.shape
    cols = shape[-1]
    w2, m2, v2 = (a.reshape(-1, cols) for a in (w, m, v))
    rows = w2.shape[0]
    own, others = own.reshape(4, rows, cols), others.reshape(3, rows, cols)
    rb = _row_block(rows, 512)

    def body(w_ref, m_ref, v_ref, own_ref, oth_ref, g_ref, d_ref, nm_ref, nv_ref):
        g = own_ref[...].astype(F32)
        for k in range(3):
            g = g + oth_ref[k].astype(F32)
        g_ref[...] = g
        d_ref[...], nm_ref[...], nv_ref[...] = _adamw(w_ref[...], g, m_ref[...], v_ref[...])

    blk = pl.BlockSpec((rb, cols), lambda i: (i, 0))
    res = jax.ShapeDtypeStruct((rows, cols), F32)
    outs = pl.pallas_call(
        body, name=name, grid=(rows // rb,), out_shape=[res] * 4,
        in_specs=[blk, blk, blk, pl.BlockSpec((None, rb, cols), lambda i: (0, i, 0)),
                  pl.BlockSpec((3, rb, cols), lambda i: (0, i, 0))],
        out_specs=[blk] * 4, compiler_params=_params("parallel"),
    )(w2, m2, v2, own, others)
    return [o.reshape(shape) for o in outs]


def _adam_small(ws, gs, ms, vs):
    n = len(ws)

    def body(*refs):
        w_refs, g_refs, m_refs, v_refs = (refs[i * n:(i + 1) * n] for i in range(4))
        outs = refs[4 * n:]
        for i in range(n):
            d, m, v = _adamw(w_refs[i][...], g_refs[i][...], m_refs[i][...], v_refs[i][...])
            outs[i][...], outs[n + i][...], outs[2 * n + i][...] = d, m, v

    shapes = [jax.ShapeDtypeStruct(w.shape, F32) for w in ws]
    outs = pl.pallas_call(
        body, name="adam_small", out_shape=shapes * 3,
        in_specs=[VMEM_SPEC] * (4 * n), out_specs=[VMEM_SPEC] * (3 * n), compiler_params=_params(),
    )(*ws, *gs, *ms, *vs)
    return outs[:n], outs[n:2 * n], outs[2 * n:]


def _pack_rows(pieces):
    tile = SUBLANES * LANES
    return jnp.concatenate([jnp.pad(p.reshape(-1), (0, (-p.size) % tile)).reshape(-1, LANES) for p in pieces], axis=0)


def _packed_starts(sizes):
    tile = SUBLANES * LANES
    starts = [0]
    for s in sizes:
        starts.append(starts[-1] + (s + tile - 1) // tile * SUBLANES)
    return starts


def kernel(x, norm_mix_pre, norm_mix_post, norm_ffn_pre, norm_ffn_post, w_in, conv_short_w, w_conv_branch, lru_conv_w, lru_conv_b, lru_wa, lru_ba, lru_wx, lru_bx, lru_lambda, w_lru_branch, w_out, ffn_w_up, ffn_conv_w, ffn_conv_b, ffn_w_down, loss_target, m_norm_mix_pre, m_norm_mix_post, m_norm_ffn_pre, m_norm_ffn_post, m_w_in, m_conv_short_w, m_w_conv_branch, m_lru_conv_w, m_lru_conv_b, m_lru_wa, m_lru_ba, m_lru_wx, m_lru_bx, m_lru_lambda, m_w_lru_branch, m_w_out, m_ffn_w_up, m_ffn_conv_w, m_ffn_conv_b, m_ffn_w_down, v_norm_mix_pre, v_norm_mix_post, v_norm_ffn_pre, v_norm_ffn_post, v_w_in, v_conv_short_w, v_w_conv_branch, v_lru_conv_w, v_lru_conv_b, v_lru_wa, v_lru_ba, v_lru_wx, v_lru_bx, v_lru_lambda, v_w_lru_branch, v_w_out, v_ffn_w_up, v_ffn_conv_w, v_ffn_conv_b, v_ffn_w_down):
    t = x.shape[1]
    xi, yi, ci = _position()
    me = _block_of(xi, yi, ci)
    x2, target = x[0], loss_target[0]
    shard_in, shard_up = IN_COLS // N_DEV, 2 * D_FF // N_DEV
    shard_sq, shard_down, shard_head = D_MODEL // N_DEV, D_FF // N_DEV, HEAD_DIM // N_DEV

    names = ["w_in", "lru_wa", "lru_wx", "w_conv_branch", "w_lru_branch", "w_out", "ffn_w_up", "ffn_w_down"]
    large = [w_in[0], lru_wa[0], lru_wx[0], w_conv_branch[0], w_lru_branch[0], w_out[0], ffn_w_up[0], ffn_w_down[0]]
    blocks = [_cols(shard_in), _lead, _lead, _rows(shard_sq), _rows(shard_sq), _rows(shard_sq),
              _cols(shard_up), _rows(shard_down)]
    gate_full = (N_DEV, N_HEADS, shard_head, HEAD_DIM)
    full_shapes = [(D_MODEL, IN_COLS), gate_full, gate_full, (D_MODEL, D_MODEL), (D_MODEL, D_MODEL), (D_MODEL, D_MODEL),
                   (D_MODEL, 2 * D_FF), (D_FF, D_MODEL)]
    n_now = 3
    small_sharded = [conv_short_w, lru_conv_w, lru_ba, lru_bx, ffn_conv_w]
    small_mine = _pack_rows(small_sharded)
    small_at = _packed_starts([p.size for p in small_sharded])
    *gathered, small_all = _gather_weights(large, blocks, full_shapes, small_mine, n_now)
    g_in, g_wa, g_wx = gathered[:n_now]
    later_blocks = blocks[n_now:]
    send1, recv1, later, gather_token = _gather_start(gathered[n_now:], later_blocks, "gather_start")

    def behind(token, operand):
        return operand + token[0:1, 0:1]

    def forward(lo, hi, after, tag):
        return _gather_forward(later[lo:hi], later_blocks[lo:hi], send1[4 * lo:4 * hi], recv1[4 * lo:4 * hi], after,
                               "gather_forward_" + tag)

    def finish(lo, hi, flight, after, tag):
        return _gather_finish(flight[2], later_blocks[lo:hi], flight[0], flight[1], after, "gather_finish_" + tag)

    def cols_of(r0, n, width):
        part = small_all[:, r0:r0 + n * width // LANES, :].reshape(N_DEV, n, width)
        return part.transpose(1, 0, 2).reshape(n, N_DEV * width)

    c_short = cols_of(small_at[0], 3, LANES)
    c_lru = cols_of(small_at[1], 4, LANES)
    b_a = cols_of(small_at[2], N_HEADS, shard_head).reshape(1, D_MODEL)
    b_x = cols_of(small_at[3], N_HEADS, shard_head).reshape(1, D_MODEL)
    c_ffn = cols_of(small_at[4], 3, shard_up)

    proj, h = _in_proj(x2, behind(gather_token, norm_mix_pre), g_in)
    flight_mix_w = forward(0, 3, h, "mix")
    y_a = _conv_mixer_fwd(proj, c_short)
    y_b, hl = _lru_fwd(proj, c_lru, lru_conv_b, g_wa, b_a, g_wx, b_x, lru_lambda)
    flight_up_w = forward(3, 4, y_b, "up")
    g_cb, g_lb, g_out = finish(0, 3, flight_mix_w, y_b, "mix")
    pa, pb, merged, mix, x1, h2 = _merge(y_a, y_b, proj, x2, g_cb, g_lb, g_out, norm_mix_post, norm_ffn_pre)
    flight_down_w = forward(4, 5, h2, "down")
    (g_up,) = finish(3, 4, flight_up_w, h2, "up")
    up, act, f = _ffn_up(h2, g_up, c_ffn, ffn_conv_b)
    (g_down,) = finish(4, 5, flight_down_w, f, "down")
    dy, d_out, d_act, dg4, loss_part = _ffn_down(f, act, g_down, x1, target, norm_ffn_post)

    block_of = dict(zip(names, blocks))
    shard_shapes = {"w_in": (D_MODEL, shard_in), "w_conv_branch": (shard_sq, D_MODEL), "w_lru_branch": (shard_sq, D_MODEL),
                    "w_out": (shard_sq, D_MODEL), "lru_wa": (N_HEADS, shard_head, HEAD_DIM),
                    "lru_wx": (N_HEADS, shard_head, HEAD_DIM), "ffn_w_up": (D_MODEL, shard_up),
                    "ffn_w_down": (shard_down, D_MODEL)}
    pos = jnp.stack([xi, yi, ci]).astype(jnp.int32)

    def reduce_start(tag, grads):
        keys = list(grads)
        got = _exchange_pair([grads[k] for k in keys], [block_of[k] for k in keys], [shard_shapes[k] for k in keys],
                             "reduce_pair_exchange_" + tag)
        sums = [_add_pair(grads[k], g, k in ("w_in", "ffn_w_up"), pos, "pair_sum_" + k) for k, g in zip(keys, got)]
        return (keys,) + _exchange_chips_start(sums, "reduce_chip_start_" + tag)

    gw_down = _grad_tn(f, d_out, min(512, D_FF), "ffn_down_wgrad")
    flight_down = reduce_start("down", {"ffn_w_down": gw_down})
    gw_up, gc_ffn, gb_ffn, d_h2 = _ffn_up_bwd(up, d_act, behind(flight_down[-1], c_ffn), h2, g_up)
    flight_up = reduce_start("up", {"ffn_w_up": gw_up})
    dx1, d_mix, d_pa, d_pb, d_ya, d_yb, d_gate, dg3, dg2 = _merge_bwd(
        dy, d_h2, x1, mix, behind(flight_up[-1], norm_ffn_pre), norm_mix_post, g_out, g_cb, g_lb, pa, pb, proj)
    gw_out = _grad_tn(merged, d_mix, CB, "w_out_wgrad")
    gw_cb = _grad_tn(y_a, d_pa, CB, "w_conv_branch_wgrad")
    gw_lb = _grad_tn(y_b, d_pb, CB, "w_lru_branch_wgrad")
    flight_mix = reduce_start("mix", {"w_conv_branch": gw_cb, "w_lru_branch": gw_lb, "w_out": gw_out})
    d_conv, gc_short = _conv_mixer_bwd(proj, d_ya, behind(flight_mix[-1], c_short))
    d_lru, gw_a, gw_x, g_lru_small = _lru_bwd(proj, hl, d_yb, c_lru, lru_conv_b, g_wa, b_a, g_wx, b_x, lru_lambda)
    gw_in = _in_proj_wgrad(h, d_conv, d_lru, d_gate)
    flight_in = reduce_start("in", {"lru_wa": gw_a, "lru_wx": gw_x, "w_in": gw_in})
    dx, dg1 = _in_proj_xgrad(d_conv, d_lru, d_gate, g_in, x2, dx1, behind(flight_in[-1], norm_mix_pre))

    moments ={"w_in": (m_w_in, v_w_in), "w_conv_branch": (m_w_conv_branch, v_w_conv_branch),
               "w_lru_branch": (m_w_lru_branch, v_w_lru_branch), "w_out": (m_w_out, v_w_out),
               "lru_wa": (m_lru_wa, v_lru_wa), "lru_wx": (m_lru_wx, v_lru_wx), "ffn_w_up": (m_ffn_w_up, v_ffn_w_up),
               "ffn_w_down": (m_ffn_w_down, v_ffn_w_down)}
    weights = {"w_in": w_in, "w_conv_branch": w_conv_branch, "w_lru_branch": w_lru_branch, "w_out": w_out,
               "lru_wa": lru_wa, "lru_wx": lru_wx, "ffn_w_up": ffn_w_up, "ffn_w_down": ffn_w_down}
    out_g, out_d, out_m, out_v = {}, {}, {}, {}

    early = [dg2, dg3, dg4, g_lru_small[4:5], g_lru_small[7:8], gb_ffn, gc_short, g_lru_small[0:4],
             g_lru_small[5:6], g_lru_small[6:7], gc_ffn, loss_part]
    flight_small = _small_start(_pack_rows(early), "small_start")
    flight_late = _small_start(_pack_rows([dg1]), "small_start_late")

    after = dx
    for tag, (keys, send, recv, sums, lands, _) in (("down", flight_down), ("up", flight_up), ("mix", flight_mix),
                                                    ("in", flight_in)):
        sums, others = _exchange_chips_wait(send, recv, sums, lands, after, "reduce_chip_wait_" + tag)
        for k, own, oth in zip(keys, sums, others):
            out_g[k], out_d[k], out_m[k], out_v[k] = _adam_large(weights[k], *moments[k], own, oth, "adam_" + k)
        after = out_d[keys[-1]]

    total, total_late = _small_sum([_small_wait(*flight_small, after, "small_wait"),
                                    _small_wait(*flight_late, after, "small_wait_late")], me)
    sizes = [p.size for p in early]
    starts = _packed_starts(sizes)

    def piece(i, shape):
        if i == 0:
            return total_late.reshape(-1)[:D_MODEL].reshape(shape)
        return total[starts[i - 1]:starts[i]].reshape(-1)[:sizes[i - 1]].reshape(shape)

    loss = total[starts[11], 0]

    def col_shard(full, width):
        return lax.dynamic_slice_in_dim(full, me * width, width, axis=1)

    def head_shard(full):
        return lax.dynamic_slice_in_dim(full.reshape(N_HEADS, HEAD_DIM), me * shard_head, shard_head, axis=1)

    small_names = ["norm_mix_pre", "norm_mix_post", "norm_ffn_pre", "norm_ffn_post", "lru_conv_b", "lru_lambda",
                   "ffn_conv_b", "conv_short_w", "lru_conv_w", "lru_ba", "lru_bx", "ffn_conv_w"]
    small_g = [piece(0, (1, D_MODEL)), piece(1, (1, D_MODEL)), piece(2, (1, D_MODEL)), piece(3, (1, D_MODEL)),
               piece(4, (1, D_MODEL)), piece(5, (1, D_MODEL)), piece(6, (1, 2 * D_FF)),
               col_shard(piece(7, (3, D_MODEL)), LANES), col_shard(piece(8, (4, D_MODEL)), LANES),
               head_shard(piece(9, (1, D_MODEL))), head_shard(piece(10, (1, D_MODEL))),
               col_shard(piece(11, (3, 2 * D_FF)), shard_up)]
    small_w = [norm_mix_pre, norm_mix_post, norm_ffn_pre, norm_ffn_post, lru_conv_b, lru_lambda, ffn_conv_b,
               conv_short_w[0], lru_conv_w[0], lru_ba[0], lru_bx[0], ffn_conv_w[0]]
    small_m = [m_norm_mix_pre, m_norm_mix_post, m_norm_ffn_pre, m_norm_ffn_post, m_lru_conv_b, m_lru_lambda,
               m_ffn_conv_b, m_conv_short_w[0], m_lru_conv_w[0], m_lru_ba[0], m_lru_bx[0], m_ffn_conv_w[0]]
    small_v = [v_norm_mix_pre, v_norm_mix_post, v_norm_ffn_pre, v_norm_ffn_post, v_lru_conv_b, v_lru_lambda,
               v_ffn_conv_b, v_conv_short_w[0], v_lru_conv_w[0], v_lru_ba[0], v_lru_bx[0], v_ffn_conv_w[0]]
    s_d, s_m, s_v = _adam_small(small_w, small_g, small_m, small_v)
    for i, name in enumerate(small_names):
        shape = small_w[i].shape if i < 7 else (1,) + small_w[i].shape
        out_g[name] = small_g[i].reshape(shape)
        out_d[name], out_m[name], out_v[name] = s_d[i].reshape(shape), s_m[i].reshape(shape), s_v[i].reshape(shape)

    order = ["norm_mix_pre", "norm_mix_post", "norm_ffn_pre", "norm_ffn_post", "w_in", "conv_short_w", "w_conv_branch",
             "lru_conv_w", "lru_conv_b", "lru_wa", "lru_ba", "lru_wx", "lru_bx", "lru_lambda", "w_lru_branch", "w_out",
             "ffn_w_up", "ffn_conv_w", "ffn_conv_b", "ffn_w_down"]
    return (loss, dx.reshape(1, t, D_MODEL), *[out_g[k] for k in order], *[out_d[k] for k in order],
            *[out_m[k] for k in order], *[out_v[k] for k in order])
```

```python
import functools
import math

import jax
import jax.numpy as jnp
from jax import lax
from jax.experimental import pallas as pl
from jax.experimental.pallas import tpu as pltpu

F32 = jnp.float32
BF16 = jnp.bfloat16
MESH = pl.DeviceIdType.MESH

N_DEV = 8
D_MODEL = 1024
N_HEADS = 4
HEAD_DIM = D_MODEL // N_HEADS
D_FF = 3 * D_MODEL
IN_COLS = 7 * D_MODEL
LRU_C = 8.0
RMS_EPS = 1e-6
ADAM_LR = 0.001
ADAM_B1 = 0.9
ADAM_B2 = 0.999
ADAM_EPS = 1e-08
ADAM_WD = 0.01
ADAM_STEP = 10
GELU_K = math.sqrt(2.0 / math.pi)
GELU_C = 0.044715

LANES = 128
SUBLANES = 8
PAD = SUBLANES
VMEM_LIMIT = 56 * 1024 * 1024
CB = 256

HBM_SPEC = pl.BlockSpec(memory_space=pltpu.HBM)
SEM_SPEC = pl.BlockSpec(memory_space=pltpu.SEMAPHORE)
DATAFLOW_EFFECT = pltpu.SideEffectType.DATAFLOW_SIDE_EFFECTING
VMEM_SPEC = pl.BlockSpec(memory_space=pltpu.VMEM)


def _params(*sem):
    if sem:
        return pltpu.CompilerParams(dimension_semantics=sem, vmem_limit_bytes=VMEM_LIMIT)
    return pltpu.CompilerParams(vmem_limit_bytes=VMEM_LIMIT)


def _row_chunk(t):
    return min(256, t)


def _row_block(rows, cap):
    return next(rb for rb in range(min(cap, rows), 0, -16) if rows % rb == 0)


def _gelu(x):
    return 0.5 * x * (1.0 + jnp.tanh(GELU_K * (x + GELU_C * x * x * x)))


def _gelu_and_grad(x):
    t = jnp.tanh(GELU_K * (x + GELU_C * x * x * x))
    g = 0.5 * x * (1.0 + t)
    dg = 0.5 * (1.0 + t) + 0.5 * x * (1.0 - t * t) * GELU_K * (1.0 + 3.0 * GELU_C * x * x)
    return g, dg


def _expm1_neg(x):
    series = x * (1.0 + x * (0.5 + x * (1.0 / 6.0 + x * (1.0 / 24.0 + x * (1.0 / 120.0)))))
    return jnp.where(x > -0.05, series, jnp.exp(x) - 1.0)


def _log_sigmoid(x):
    return jnp.minimum(x, 0.0) - jnp.log1p(jnp.exp(-jnp.abs(x)))


def _dot(a, b):
    return jnp.dot(a, b, preferred_element_type=F32)


def _dot_nt(a, b):
    return lax.dot_general(a, b, (((1,), (1,)), ((), ())), preferred_element_type=F32)


def _dot_tn(a, b):
    return lax.dot_general(a, b, (((0,), (0,)), ((), ())), preferred_element_type=F32)


def _rms_fwd(x):
    r = lax.rsqrt(jnp.mean(x * x, axis=-1, keepdims=True) + RMS_EPS)
    return x * r, r


def _rms_bwd(n, r, gdy):
    return r * (gdy - n * jnp.mean(n * gdy, axis=-1, keepdims=True))


def _conv_causal(pad_ref, w, r0, rows, taps):
    acc = None
    for k in range(taps):
        term = w[k:k + 1, :] * pad_ref[pl.ds(PAD + r0 - (taps - 1 - k), rows), :]
        acc = term if acc is None else acc + term
    return acc


def _conv_anticausal(pad_ref, w, r0, rows, taps):
    acc = None
    for k in range(taps):
        term = w[k:k + 1, :] * pad_ref[pl.ds(r0 + (taps - 1 - k), rows), :]
        acc = term if acc is None else acc + term
    return acc


def _conv_wgrad(g, xpad_ref, r0, rows, taps):
    return [jnp.sum(g * xpad_ref[pl.ds(PAD + r0 - (taps - 1 - k), rows), :], axis=0, keepdims=True)
            for k in range(taps)]


def _position():
    return lax.axis_index("x"), lax.axis_index("y"), lax.axis_index("c")


def _block_of(x, y, c):
    return 4 * x + 2 * y + c


def _chip(x, y, k):
    return (x + (k & 1)) % 2, (y + (k >> 1)) % 2


def _cols(width):
    def at(ref, d):
        return ref.at[:, pl.ds(pl.multiple_of(d * width, LANES), width)]
    return at


def _rows(height):
    def at(ref, d):
        return ref.at[pl.ds(pl.multiple_of(d * height, 16), height), :]
    return at


def _lead(ref, d):
    return ref.at[d]


def _gather_weights(shards, blocks, full_shapes, small, n_now):
    n = len(shards)
    small_rows = small.shape[0]

    def body(*refs):
        ins, small_in = refs[:n], refs[n]
        outs, small_out = refs[n + 1:2 * n + 1], refs[2 * n + 1]
        stage = refs[2 * n + 2:3 * n + 2]
        send, recv, local = refs[3 * n + 2:]
        x, y, c = _position()
        me = _block_of(x, y, c)
        sibling = (x, y, 1 - c)

        for a in range(n):
            stage[a][...] = ins[a][...].astype(BF16)

        def copy(a, k, block, to, src=None):
            dst = blocks[a](outs[a], block)
            return pltpu.make_async_remote_copy(
                src_ref=dst if src is None else src, dst_ref=dst, send_sem=send.at[a, k], recv_sem=recv.at[a, k],
                device_id=to, device_id_type=MESH)

        def small_copy(k):
            px, py, pc = (x + (k & 1)) % 2, (y + ((k >> 1) & 1)) % 2, (c + (k >> 2)) % 2
            return pltpu.make_async_remote_copy(
                src_ref=small_in, dst_ref=small_out.at[me], send_sem=send.at[n_now, k - 1], recv_sem=recv.at[n_now, k - 1],
                device_id=(px, py, pc), device_id_type=MESH)

        def small_arrival(k):
            px, py, pc = (x + (k & 1)) % 2, (y + ((k >> 1) & 1)) % 2, (c + (k >> 2)) % 2
            return pltpu.make_async_remote_copy(
                src_ref=small_in, dst_ref=small_out.at[_block_of(px, py, pc)], send_sem=send.at[n_now, k - 1],
                recv_sem=recv.at[n_now, k - 1], device_id=(px, py, pc), device_id_type=MESH)

        small_out[me] = small_in[...]
        small_sends = [small_copy(k) for k in range(1, N_DEV)]
        for cp in small_sends:
            cp.start()

        mine, first, passed = [], [], []
        for a in range(n):
            own = pltpu.make_async_copy(stage[a], blocks[a](outs[a], me), local.at[a])
            own.start()
            mine.append(own)
            if a >= n_now:
                continue
            sends = [copy(a, 0, me, sibling, src=stage[a])]
            sends += [copy(a, k, me, (*_chip(x, y, k), c), src=stage[a]) for k in (1, 2, 3)]
            for cp in sends:
                cp.start()
            first += sends
        for a in range(n_now):
            for k in (1, 2, 3):
                landed = _block_of(*_chip(x, y, k), c)
                copy(a, k, landed, (x, y, c)).wait_recv()
                fwd = copy(a, 3 + k, landed, sibling)
                fwd.start()
                passed.append(fwd)
        for a in range(n_now):
            copy(a, 0, _block_of(x, y, 1 - c), (x, y, c)).wait_recv()
            for k in (1, 2, 3):
                copy(a, 3 + k, _block_of(*_chip(x, y, k), 1 - c), (x, y, c)).wait_recv()
        for k in range(1, N_DEV):
            small_arrival(k).wait_recv()
        for cp in first + passed + small_sends:
            cp.wait_send()
        for own in mine:
            own.wait()

    out_shape = [jax.ShapeDtypeStruct(s, BF16) for s in full_shapes]
    out_shape.append(jax.ShapeDtypeStruct((N_DEV, small_rows, LANES), F32))
    return pl.pallas_call(
        body, name="gather_weights", out_shape=out_shape,
        in_specs=[VMEM_SPEC] * (n + 1), out_specs=[HBM_SPEC] * n + [VMEM_SPEC],
        scratch_shapes=[pltpu.VMEM(s.shape, BF16) for s in shards]
        + [pltpu.SemaphoreType.DMA((n_now + 1, 7)), pltpu.SemaphoreType.DMA((n_now + 1, 7)),
           pltpu.SemaphoreType.DMA((n,))],
        compiler_params=_params(),
    )(*shards, small)


def _gather_first(full, blocks, send, recv):
    x, y, c = _position()
    me = _block_of(x, y, c)
    peers = [(x, y, 1 - c)] + [(*_chip(x, y, k), c) for k in (1, 2, 3)]

    def copy(a, k, block):
        at = blocks[a](full[a], block)
        return pltpu.make_async_remote_copy(src_ref=at, dst_ref=at, send_sem=send[4 * a + k], recv_sem=recv[4 * a + k],
                                            device_id=peers[k], device_id_type=MESH)

    sends = [copy(a, k, me) for a in range(len(full)) for k in range(4)]
    arrivals = [copy(a, k, _block_of(*peers[k])) for a in range(len(full)) for k in range(4)]
    return sends, arrivals


def _gather_second(full, blocks, send, recv):
    x, y, c = _position()

    def copy(a, k, cc):
        at = blocks[a](full[a], _block_of(*_chip(x, y, k), cc))
        return pltpu.make_async_remote_copy(src_ref=at, dst_ref=at, send_sem=send[3 * a + k - 1],
                                            recv_sem=recv[3 * a + k - 1], device_id=(x, y, 1 - c), device_id_type=MESH)

    sends = [copy(a, k, c) for a in range(len(full)) for k in (1, 2, 3)]
    arrivals = [copy(a, k, 1 - c) for a in range(len(full)) for k in (1, 2, 3)]
    return sends, arrivals


def _split_call(body, name, arrays, sems_in, n_sems_out, after=None, token=False):
    n, m = len(arrays), len(sems_in)

    def kernel_body(*refs):
        outs = refs[n + m + (after is not None):]
        body(refs[:n], refs[n:n + m], outs[:n_sems_out])
        if token:
            outs[-1][...] = jnp.zeros_like(outs[-1])

    extra_in = [] if after is None else [after]
    outs = pl.pallas_call(
        kernel_body, name=name,
        out_shape=(*[pltpu.SemaphoreType.DMA(())] * n_sems_out, *[pltpu.HBM(a.shape, a.dtype) for a in arrays],
                   *([jax.ShapeDtypeStruct((SUBLANES, LANES), F32)] if token else [])),
        in_specs=[HBM_SPEC] * n + [SEM_SPEC] * m + [pl.BlockSpec(memory_space=pl.ANY)] * len(extra_in),
        out_specs=(*[SEM_SPEC] * n_sems_out, *[HBM_SPEC] * n, *([VMEM_SPEC] if token else [])),
        input_output_aliases={i: n_sems_out + i for i in range(n)},
        compiler_params=pltpu.CompilerParams(has_side_effects=DATAFLOW_EFFECT),
    )(*[pltpu.with_memory_space_constraint(a, pltpu.HBM) for a in arrays], *sems_in, *extra_in)
    sems, rest = list(outs[:n_sems_out]), list(outs[n_sems_out:])
    return (sems, rest[:n], rest[n]) if token else (sems, rest[:n])


def _gather_start(full, blocks, name):
    n = len(full)

    def body(arrays, _, sems):
        for cp in _gather_first(arrays, blocks, sems[:4 * n], sems[4 * n:])[0]:
            cp.start()

    sems, arrays, token = _split_call(body, name, full, [], 8 * n, token=True)
    return sems[:4 * n], sems[4 * n:], arrays, token


def _gather_forward(full, blocks, send_first, recv_first, after, name):
    n = len(full)

    def body(arrays, sems_in, sems):
        sends, arrivals = _gather_first(arrays, blocks, sems_in[:4 * n], sems_in[4 * n:])
        for cp in arrivals:
            cp.wait_recv()
        for cp in _gather_second(arrays, blocks, sems[:3 * n], sems[3 * n:])[0]:
            cp.start()
        for cp in sends:
            cp.wait_send()

    sems, arrays = _split_call(body, name, full, [*send_first, *recv_first], 6 * n, after=after)
    return sems[:3 * n], sems[3 * n:], arrays


def _gather_finish(full, blocks, send_second, recv_second, after, name):
    n = len(full)

    def body(arrays, sems_in, _):
        sends, arrivals = _gather_second(arrays, blocks, sems_in[:3 * n], sems_in[3 * n:])
        for cp in sends:
            cp.wait_send()
        for cp in arrivals:
            cp.wait_recv()

    return _split_call(body, name, full, [*send_second, *recv_second], 0, after=after)[1]


def _exchange_pair(grads, blocks, shard_shapes, name):
    n = len(grads)

    def body(*refs):
        ins, got = refs[:n], refs[n:2 * n]
        send, recv = refs[2 * n:]
        x, y, c = _position()
        copies = []
        for a in range(n):
            for k in range(4):
                cp = pltpu.make_async_remote_copy(
                    src_ref=blocks[a](ins[a], _block_of(*_chip(x, y, k), 1 - c)), dst_ref=got[a].at[k],
                    send_sem=send.at[a, k], recv_sem=recv.at[a, k], device_id=(x, y, 1 - c), device_id_type=MESH)
                cp.start()
                copies.append(cp)
        for cp in copies:
            cp.wait()

    return pl.pallas_call(
        body, name=name, out_shape=[jax.ShapeDtypeStruct((4,) + tuple(s), BF16) for s in shard_shapes],
        in_specs=[HBM_SPEC] * n, out_specs=[HBM_SPEC] * n,
        scratch_shapes=[pltpu.SemaphoreType.DMA((n, 4)), pltpu.SemaphoreType.DMA((n, 4))],
        compiler_params=_params(),
    )(*grads)


def _chip_copies(sums, lands, send, recv):
    x, y, c = _position()
    return [pltpu.make_async_remote_copy(
        src_ref=sums[a].at[k], dst_ref=lands[a].at[k - 1], send_sem=send[3 * a + k - 1], recv_sem=recv[3 * a + k - 1],
        device_id=(*_chip(x, y, k), c), device_id_type=MESH) for a in range(len(sums)) for k in (1, 2, 3)]


def _exchange_chips_start(pair_sums, name):
    n = len(pair_sums)
    lands = [pltpu.with_memory_space_constraint(lax.empty((3,) + tuple(p.shape[1:]), BF16), pltpu.HBM) for p in pair_sums]

    def body(*refs):
        sums, zones = refs[:n], refs[n:2 * n]
        send, recv = refs[2 * n:5 * n], refs[5 * n:8 * n]
        token = refs[-1]
        for cp in _chip_copies(sums, zones, send, recv):
            cp.start()
        token[...] = jnp.zeros_like(token)

    outs = pl.pallas_call(
        body, name=name,
        out_shape=(*[pltpu.SemaphoreType.DMA(())] * (6 * n),
                   *[pltpu.HBM(p.shape, BF16) for p in pair_sums], *[pltpu.HBM(z.shape, BF16) for z in lands],
                   jax.ShapeDtypeStruct((SUBLANES, LANES), F32)),
        in_specs=[HBM_SPEC] * (2 * n), out_specs=(*[SEM_SPEC] * (6 * n), *[HBM_SPEC] * (2 * n), VMEM_SPEC),
        input_output_aliases={i: 6 * n + i for i in range(2 * n)},
        compiler_params=pltpu.CompilerParams(has_side_effects=DATAFLOW_EFFECT),
    )(*[pltpu.with_memory_space_constraint(p, pltpu.HBM) for p in pair_sums], *lands)
    return outs[:3 * n], outs[3 * n:6 * n], outs[6 * n:7 * n], outs[7 * n:8 * n], outs[-1]


def _exchange_chips_wait(send, recv, sums, lands, after, name):
    n = len(sums)

    def body(*refs):
        sums_in, zones = refs[:n], refs[n:2 * n]
        send_in, recv_in = refs[2 * n:5 * n], refs[5 * n:8 * n]
        for cp in _chip_copies(sums_in, zones, send_in, recv_in):
            cp.wait_send()
            cp.wait_recv()

    outs = pl.pallas_call(
        body, name=name,
        out_shape=(*[pltpu.HBM(p.shape, BF16) for p in sums], *[pltpu.HBM(z.shape, BF16) for z in lands]),
        in_specs=[HBM_SPEC] * (2 * n) + [SEM_SPEC] * (6 * n) + [pl.BlockSpec(memory_space=pl.ANY)],
        out_specs=[HBM_SPEC] * (2 * n), input_output_aliases={i: i for i in range(2 * n)},
        compiler_params=pltpu.CompilerParams(has_side_effects=DATAFLOW_EFFECT),
    )(*sums, *lands, *send, *recv, after)
    return outs[:n], outs[n:]


def _small_copies(mine, land, send, recv):
    x, y, c = _position()
    me = _block_of(x, y, c)

    def peer(k):
        return (x + (k & 1)) % 2, (y + ((k >> 1) & 1)) % 2, (c + (k >> 2)) % 2

    def copy(k, slot):
        return pltpu.make_async_remote_copy(src_ref=mine, dst_ref=land.at[slot], send_sem=send[k - 1], recv_sem=recv[k - 1],
                                            device_id=peer(k), device_id_type=MESH)

    return [copy(k, me) for k in range(1, N_DEV)], [copy(k, _block_of(*peer(k))) for k in range(1, N_DEV)]


def _small_start(part, name):
    land = jnp.zeros((N_DEV,) + part.shape, F32)

    def body(arrays, _, sems):
        for cp in _small_copies(arrays[0], arrays[1], sems[:7], sems[7:])[0]:
            cp.start()

    sems, arrays, token = _split_call(body, name, [part, land], [], 14, token=True)
    return sems[:7], sems[7:], arrays[0], arrays[1], token


def _small_wait(send, recv, part, land, after, name):
    def body(arrays, sems_in, _):
        sends, arrivals = _small_copies(arrays[0], arrays[1], sems_in[:7], sems_in[7:])
        for cp in sends:
            cp.wait_send()
        for cp in arrivals:
            cp.wait_recv()

    return _split_call(body, name, [part, land], [*send, *recv], 0, after=after)[1]


def _small_sum(pairs, me):
    n = len(pairs)

    def body(me_ref, *refs):
        for i in range(n):
            mine, land, out = refs[2 * i], refs[2 * i + 1], refs[2 * n + i]
            total = jnp.zeros(mine.shape, F32)
            for d in range(N_DEV):
                total = total + land[d] + jnp.where(me_ref[0] == d, mine[...], 0.0)
            out[...] = total

    flat = [a for pair in pairs for a in pair]
    return pl.pallas_call(
        body, name="small_sum", out_shape=[jax.ShapeDtypeStruct(mine.shape, F32) for mine, _ in pairs],
        in_specs=[pl.BlockSpec(memory_space=pltpu.SMEM)] + [VMEM_SPEC] * (2 * n), out_specs=[VMEM_SPEC] * n,
        compiler_params=_params(),
    )(me.reshape(1).astype(jnp.int32), *flat)


def _in_proj(x, g1, w_in):
    t = x.shape[0]
    tm, bn = min(1024, t), 1024

    def body(x_ref, g_ref, w_ref, proj_ref, h_ref, h_s):
        @pl.when(pl.program_id(1) == 0)
        def _():
            n, _ = _rms_fwd(x_ref[...])
            h_s[...] = (n * g_ref[...]).astype(BF16)
            h_ref[...] = h_s[...]
        proj_ref[...] = _dot(h_s[...], w_ref[...]).astype(BF16)

    return pl.pallas_call(
        body, name="in_proj", grid=(t // tm, IN_COLS // bn),
        out_shape=[jax.ShapeDtypeStruct((t, IN_COLS), BF16), jax.ShapeDtypeStruct((t, D_MODEL), BF16)],
        in_specs=[pl.BlockSpec((tm, D_MODEL), lambda i, j: (i, 0)), pl.BlockSpec((1, D_MODEL), lambda i, j: (0, 0)),
                  pl.BlockSpec((D_MODEL, bn), lambda i, j: (0, j))],
        out_specs=[pl.BlockSpec((tm, bn), lambda i, j: (i, j)), pl.BlockSpec((tm, D_MODEL), lambda i, j: (i, 0))],
        scratch_shapes=[pltpu.VMEM((tm, D_MODEL), BF16)],
        compiler_params=_params("parallel", "arbitrary"),
    )(x, g1, w_in)


def _section(s, t):
    return pl.BlockSpec((t, CB), lambda h, s=s: (0, s * (D_MODEL // CB) + h))


def _conv_mixer_fwd(proj, w_short):
    t = proj.shape[0]
    rc = _row_chunk(t)

    def body(b_ref, c_ref, x_ref, w_ref, y_ref, pad):
        pad[pl.ds(0, PAD), :] = jnp.zeros((PAD, CB), F32)
        for r0 in range(0, t, rc):
            rows = pl.ds(r0, rc)
            pad[pl.ds(PAD + r0, rc), :] = c_ref[rows, :].astype(F32) * x_ref[rows, :].astype(F32)
        w = w_ref[...]
        for r0 in range(0, t, rc):
            rows = pl.ds(r0, rc)
            y_ref[rows, :] = (b_ref[rows, :].astype(F32) * _conv_causal(pad, w, r0, rc, 3)).astype(BF16)

    return pl.pallas_call(
        body, name="conv_mixer_fwd", grid=(D_MODEL // CB,),
        out_shape=jax.ShapeDtypeStruct((t, D_MODEL), BF16),
        in_specs=[_section(0, t), _section(1, t), _section(2, t), pl.BlockSpec((3, CB), lambda h: (0, h))],
        out_specs=pl.BlockSpec((t, CB), lambda h: (0, h)),
        scratch_shapes=[pltpu.VMEM((t + PAD, CB), F32)],
        compiler_params=_params("parallel"),
    )(proj, proj, proj, w_short)


def _lru_gates(xl, wa, ba, wx, bx, ls, first_row):
    xb = xl.astype(BF16)
    ra = jax.nn.sigmoid(_dot(xb, wa) + ba)
    ia = jax.nn.sigmoid(_dot(xb, wx) + bx)
    la = LRU_C * ra * ls
    a = jnp.exp(la)
    one_minus = -_expm1_neg(2.0 * la)
    mult = jnp.where(first_row, 1.0, jnp.sqrt(one_minus))
    return xb, ra, ia, a, one_minus, mult


def _head_specs():
    vec = pl.BlockSpec((1, CB), lambda h: (0, h))
    mat = pl.BlockSpec((N_DEV, None, HEAD_DIM // N_DEV, HEAD_DIM), lambda h: (0, h, 0, 0))
    return vec, mat


def _lru_fwd(proj, w_conv, b_conv, wa, ba, wx, bx, lam):
    t = proj.shape[0]
    rc = _row_chunk(t)
    vec, mat = _head_specs()

    def body(lx_ref, ly_ref, wc_ref, bc_ref, wa_ref, ba_ref, wx_ref, bx_ref, lam_ref, yb_ref, hl_ref, a_ref, kept_ref,
             pad, u_s):
        pad[pl.ds(0, PAD), :] = jnp.zeros((PAD, CB), F32)
        for r0 in range(0, t, rc):
            pad[pl.ds(PAD + r0, rc), :] = lx_ref[pl.ds(r0, rc), :].astype(F32)
        wc, bc = wc_ref[...], bc_ref[...]
        wa_m, wx_m = wa_ref[...].reshape(HEAD_DIM, HEAD_DIM), wx_ref[...].reshape(HEAD_DIM, HEAD_DIM)
        ls = _log_sigmoid(lam_ref[...])
        for r0 in range(0, t, rc):
            rows = pl.ds(r0, rc)
            xl = _conv_causal(pad, wc, r0, rc, 4) + bc
            first = (lax.broadcasted_iota(jnp.int32, (rc, CB), 0) + r0) == 0
            xb, ra, ia, a, _, mult = _lru_gates(xl, wa_m, ba_ref[...], wx_m, bx_ref[...], ls, first)
            a_ref[rows, :] = a
            u_s[rows, :] = mult * (ia * xl)
            kept_ref[0, rows, :] = xb
            kept_ref[1, rows, :] = ra.astype(BF16)
            kept_ref[2, rows, :] = ia.astype(BF16)

        row = lax.broadcasted_iota(jnp.int32, (SUBLANES, CB), 0)

        def group(g, carry):
            r = pl.multiple_of(g * SUBLANES, SUBLANES)
            a_g, b_g = a_ref[pl.ds(r, SUBLANES), :], u_s[pl.ds(r, SUBLANES), :]
            for s in (1, 2, 4):
                keep = row >= s
                b_g = jnp.where(keep, a_g * pltpu.roll(b_g, s, 0) + b_g, b_g)
                a_g = jnp.where(keep, a_g * pltpu.roll(a_g, s, 0), a_g)
            h_g = b_g + a_g * carry
            hl_ref[pl.ds(r, SUBLANES), :] = h_g
            return jnp.broadcast_to(h_g[SUBLANES - 1:SUBLANES, :], (SUBLANES, CB))

        lax.fori_loop(0, t // SUBLANES, group, jnp.zeros((SUBLANES, CB), F32))
        for r0 in range(0, t, rc):
            rows = pl.ds(r0, rc)
            yb_ref[rows, :] = (hl_ref[rows, :] * _gelu(ly_ref[rows, :].astype(F32))).astype(BF16)

    blk = pl.BlockSpec((t, CB), lambda h: (0, h))
    res = jax.ShapeDtypeStruct((t, D_MODEL), F32)
    return pl.pallas_call(
        body, name="lru_fwd", grid=(N_HEADS,),
        out_shape=[jax.ShapeDtypeStruct((t, D_MODEL), BF16), res, res, jax.ShapeDtypeStruct((3, t, D_MODEL), BF16)],
        in_specs=[_section(3, t), _section(4, t), pl.BlockSpec((4, CB), lambda h: (0, h)), vec, mat, vec, mat, vec, vec],
        out_specs=[blk, blk, blk, pl.BlockSpec((3, t, CB), lambda h: (0, 0, h))],
        scratch_shapes=[pltpu.VMEM((t + PAD, CB), F32), pltpu.VMEM((t, CB), F32)],
        compiler_params=_params("parallel"),
    )(proj, proj, w_conv, b_conv, wa, ba, wx, bx, lam)


def _merge(y_a, y_b, proj, x, w_cb, w_lb, w_out, g2, g3):
    t = x.shape[0]
    tm = min(256, t)

    def body(ya_ref, yb_ref, gc_ref, gl_ref, x_ref, wcb_ref, wlb_ref, wo_ref, g2_ref, g3_ref,
             pa_ref, pb_ref, mg_ref, mix_ref, x1_ref, h2_ref):
        pa = _dot(ya_ref[...], wcb_ref[...]).astype(BF16)
        pb = _dot(yb_ref[...], wlb_ref[...]).astype(BF16)
        pa_ref[...] = pa
        pb_ref[...] = pb
        merged = (jax.nn.sigmoid(gc_ref[...].astype(F32)) * pa.astype(F32)
                  + jax.nn.sigmoid(gl_ref[...].astype(F32)) * pb.astype(F32)).astype(BF16)
        mg_ref[...] = merged
        mix = _dot(merged, wo_ref[...])
        mix_ref[...] = mix
        n2, _ = _rms_fwd(mix)
        x1 = x_ref[...] + n2 * g2_ref[...]
        x1_ref[...] = x1
        n3, _ = _rms_fwd(x1)
        h2_ref[...] = (n3 * g3_ref[...]).astype(BF16)

    row = pl.BlockSpec((tm, D_MODEL), lambda i: (i, 0))
    full = pl.BlockSpec((D_MODEL, D_MODEL), lambda i: (0, 0))
    vec = pl.BlockSpec((1, D_MODEL), lambda i: (0, 0))
    act = jax.ShapeDtypeStruct((t, D_MODEL), BF16)
    res = jax.ShapeDtypeStruct((t, D_MODEL), F32)
    return pl.pallas_call(
        body, name="merge_fwd", grid=(t // tm,), out_shape=[act, act, act, res, res, act],
        in_specs=[row, row, pl.BlockSpec((tm, D_MODEL), lambda i: (i, 5)), pl.BlockSpec((tm, D_MODEL), lambda i: (i, 6)),
                  row, full, full, full, vec, vec],
        out_specs=[row] * 6,
        compiler_params=_params("parallel"),
    )(y_a, y_b, proj, proj, x, w_cb, w_lb, w_out, g2, g3)


N_FF_BLOCKS = D_FF // CB


def _ffn_up(h2, w_up, w_conv, b_conv):
    t = h2.shape[0]
    rc = _row_chunk(t)
    nb = N_FF_BLOCKS

    def body(h_ref, w_ref, c_ref, b_ref, up_ref, act_ref, f_ref, pad, gate):
        k = pl.program_id(1)
        pad[pl.ds(0, PAD), :] = jnp.zeros((PAD, CB), F32)
        for r0 in range(0, t, rc):
            rows = pl.ds(r0, rc)
            up = _dot(h_ref[rows, :], w_ref[...]).astype(BF16)
            up_ref[rows, :] = up
            pad[pl.ds(PAD + r0, rc), :] = up.astype(F32)
        cw = c_ref[...]
        for r0 in range(0, t, rc):
            rows = pl.ds(r0, rc)
            act = _conv_causal(pad, cw, r0, rc, 3) + b_ref[...]
            act_ref[rows, :] = act.astype(BF16)

            @pl.when(k == 0)
            def _():
                gate[rows, :] = act

            @pl.when(k == 1)
            def _():
                f_ref[rows, :] = (_gelu(gate[rows, :]) * act).astype(BF16)

    half = lambda rows: pl.BlockSpec((rows, CB), lambda j, k: (0, nb * k + j))
    wide = jax.ShapeDtypeStruct((t, 2 * D_FF), BF16)
    return pl.pallas_call(
        body, name="ffn_up_fwd", grid=(nb, 2), out_shape=[wide, wide, jax.ShapeDtypeStruct((t, D_FF), BF16)],
        in_specs=[pl.BlockSpec((t, D_MODEL), lambda j, k: (0, 0)), half(D_MODEL), half(3), half(1)],
        out_specs=[half(t), half(t), pl.BlockSpec((t, CB), lambda j, k: (0, j))],
        scratch_shapes=[pltpu.VMEM((t + PAD, CB), F32), pltpu.VMEM((t, CB), F32)],
        compiler_params=_params("parallel", "arbitrary"),
    )(h2, w_up, w_conv, b_conv)


def _ffn_down(f, act, w_down, x1, target, g4):
    t = f.shape[0]
    tm = min(256, t)
    cc = 512

    def body(f_ref, act_ref, w_ref, x1_ref, tg_ref, g_ref, dy_ref, dout_ref, back_ref, dg_ref, loss_ref):
        @pl.when(pl.program_id(0) == 0)
        def _():
            dg_ref[...] = jnp.zeros_like(dg_ref)
            loss_ref[...] = jnp.zeros_like(loss_ref)
        out = _dot(f_ref[...], w_ref[...])
        n4, r4 = _rms_fwd(out)
        err = x1_ref[...] + n4 * g_ref[...] - tg_ref[...]
        loss_ref[...] += jnp.full(loss_ref.shape, 0.5 / D_MODEL, F32) * jnp.sum(err * err)
        dy = err * (1.0 / D_MODEL)
        dy_ref[...] = dy
        dg_ref[...] += jnp.sum(dy * n4, axis=0, keepdims=True)
        d_out = _rms_bwd(n4, r4, dy * g_ref[...]).astype(BF16)
        dout_ref[...] = d_out
        for c0 in range(0, D_FF, cc):
            d_f = _dot_nt(d_out, w_ref[pl.ds(c0, cc), :])
            gelu, d_gelu = _gelu_and_grad(act_ref[:, pl.ds(c0, cc)].astype(F32))
            val = act_ref[:, pl.ds(D_FF + c0, cc)].astype(F32)
            back_ref[:, pl.ds(c0, cc)] = (d_f * val * d_gelu).astype(BF16)
            back_ref[:, pl.ds(D_FF + c0, cc)] = (d_f * gelu).astype(BF16)

    row = pl.BlockSpec((tm, D_MODEL), lambda i: (i, 0))
    wide = pl.BlockSpec((tm, 2 * D_FF), lambda i: (i, 0))
    vec = pl.BlockSpec((1, D_MODEL), lambda i: (0, 0))
    return pl.pallas_call(
        body, name="ffn_down_fwd_bwd", grid=(t // tm,),
        out_shape=[jax.ShapeDtypeStruct((t, D_MODEL), F32), jax.ShapeDtypeStruct((t, D_MODEL), BF16),
                   jax.ShapeDtypeStruct((t, 2 * D_FF), BF16), jax.ShapeDtypeStruct((1, D_MODEL), F32),
                   jax.ShapeDtypeStruct((SUBLANES, LANES), F32)],
        in_specs=[pl.BlockSpec((tm, D_FF), lambda i: (i, 0)), wide, pl.BlockSpec((D_FF, D_MODEL), lambda i: (0, 0)),
                  row, row, vec],
        out_specs=[row, row, wide, vec, pl.BlockSpec((SUBLANES, LANES), lambda i: (0, 0))],
        compiler_params=_params("arbitrary"),
    )(f, act, w_down, x1, target, g4)


def _grad_tn(a, b, bm, name):
    t, m = a.shape
    n = b.shape[1]

    def body(a_ref, b_ref, o_ref):
        o_ref[...] = _dot_tn(a_ref[...], b_ref[...]).astype(BF16)

    return pl.pallas_call(
        body, name=name, grid=(m // bm,), out_shape=jax.ShapeDtypeStruct((m, n), BF16),
        in_specs=[pl.BlockSpec((t, bm), lambda i: (0, i)), pl.BlockSpec((t, n), lambda i: (0, 0))],
        out_specs=pl.BlockSpec((bm, n), lambda i: (i, 0)),
        compiler_params=_params("parallel"),
    )(a, b)


def _ffn_up_bwd(up, back, w_conv, h2, w_up):
    t = h2.shape[0]
    rc = _row_chunk(t)
    nb = N_FF_BLOCKS

    def body(up_ref, back_ref, c_ref, h_ref, w_ref, dw_ref, dcw_ref, dcb_ref, dh_ref, pad, after, d_up):
        @pl.when((pl.program_id(0) == 0) & (pl.program_id(1) == 0))
        def _():
            dh_ref[...] = jnp.zeros_like(dh_ref)
        pad[pl.ds(0, PAD), :] = jnp.zeros((PAD, CB), F32)
        after[pl.ds(t, PAD), :] = jnp.zeros((PAD, CB), F32)
        for r0 in range(0, t, rc):
            pad[pl.ds(PAD + r0, rc), :] = up_ref[pl.ds(r0, rc), :].astype(F32)
            after[pl.ds(r0, rc), :] = back_ref[pl.ds(r0, rc), :].astype(F32)
        cw = c_ref[...]
        taps = [jnp.zeros((1, CB), F32)] * 3
        bias = jnp.zeros((1, CB), F32)
        for r0 in range(0, t, rc):
            rows = pl.ds(r0, rc)
            d = _conv_anticausal(after, cw, r0, rc, 3).astype(BF16)
            d_up[rows, :] = d
            dh_ref[rows, :] += _dot_nt(d, w_ref[...])
            g = after[rows, :]
            taps = [acc + new for acc, new in zip(taps, _conv_wgrad(g, pad, r0, rc, 3))]
            bias = bias + jnp.sum(g, axis=0, keepdims=True)
        dw_ref[...] = _dot_tn(h_ref[...], d_up[...]).astype(BF16)
        dcw_ref[...] = jnp.concatenate(taps, axis=0)
        dcb_ref[...] = bias

    half = lambda rows: pl.BlockSpec((rows, CB), lambda j, k: (0, nb * k + j))
    whole = pl.BlockSpec((t, D_MODEL), lambda j, k: (0, 0))
    return pl.pallas_call(
        body, name="ffn_up_bwd", grid=(nb, 2),
        out_shape=[jax.ShapeDtypeStruct((D_MODEL, 2 * D_FF), BF16), jax.ShapeDtypeStruct((3, 2 * D_FF), F32),
                   jax.ShapeDtypeStruct((1, 2 * D_FF), F32), jax.ShapeDtypeStruct((t, D_MODEL), F32)],
        in_specs=[half(t), half(t), half(3), whole, half(D_MODEL)],
        out_specs=[half(D_MODEL), half(3), half(1), whole],
        scratch_shapes=[pltpu.VMEM((t + PAD, CB), F32), pltpu.VMEM((t + PAD, CB), F32), pltpu.VMEM((t, CB), BF16)],
        compiler_params=_params("arbitrary", "arbitrary"),
    )(up, back, w_conv, h2, w_up)


def _merge_bwd(dy, d_h2, x1, mix, g3, g2, w_out, w_cb, w_lb, pa, pb, proj):
    t = dy.shape[0]
    tm = min(256, t)

    def body(dy_ref, dh2_ref, x1_ref, mix_ref, g3_ref, g2_ref, wo_ref, wcb_ref, wlb_ref, pa_ref, pb_ref, gc_ref, gl_ref,
             dx1_ref, dmix_ref, dpa_ref, dpb_ref, dya_ref, dyb_ref, dgate_ref, dg3_ref, dg2_ref):
        @pl.when(pl.program_id(0) == 0)
        def _():
            dg3_ref[...] = jnp.zeros_like(dg3_ref)
            dg2_ref[...] = jnp.zeros_like(dg2_ref)
        n3, r3 = _rms_fwd(x1_ref[...])
        d_h2 = dh2_ref[...]
        dg3_ref[...] += jnp.sum(d_h2 * n3, axis=0, keepdims=True)
        dx1 = dy_ref[...] + _rms_bwd(n3, r3, d_h2 * g3_ref[...])
        dx1_ref[...] = dx1
        n2, r2 = _rms_fwd(mix_ref[...])
        dg2_ref[...] += jnp.sum(dx1 * n2, axis=0, keepdims=True)
        d_mix = _rms_bwd(n2, r2, dx1 * g2_ref[...]).astype(BF16)
        dmix_ref[...] = d_mix
        d_merged = _dot_nt(d_mix, wo_ref[...])
        sc = jax.nn.sigmoid(gc_ref[...].astype(F32))
        sl = jax.nn.sigmoid(gl_ref[...].astype(F32))
        d_pa = (d_merged * sc).astype(BF16)
        d_pb = (d_merged * sl).astype(BF16)
        dpa_ref[...] = d_pa
        dpb_ref[...] = d_pb
        dgate_ref[0] = (d_merged * pa_ref[...].astype(F32) * sc * (1.0 - sc)).astype(BF16)
        dgate_ref[1] = (d_merged * pb_ref[...].astype(F32) * sl * (1.0 - sl)).astype(BF16)
        dya_ref[...] = _dot_nt(d_pa, wcb_ref[...]).astype(BF16)
        dyb_ref[...] = _dot_nt(d_pb, wlb_ref[...]).astype(BF16)

    row = pl.BlockSpec((tm, D_MODEL), lambda i: (i, 0))
    full = pl.BlockSpec((D_MODEL, D_MODEL), lambda i: (0, 0))
    vec = pl.BlockSpec((1, D_MODEL), lambda i: (0, 0))
    act = jax.ShapeDtypeStruct((t, D_MODEL), BF16)
    small = jax.ShapeDtypeStruct((1, D_MODEL), F32)
    return pl.pallas_call(
        body, name="merge_bwd", grid=(t // tm,),
        out_shape=[jax.ShapeDtypeStruct((t, D_MODEL), F32), act, act, act, act, act,
                   jax.ShapeDtypeStruct((2, t, D_MODEL), BF16), small, small],
        in_specs=[row, row, row, row, vec, vec, full, full, full, row, row,
                  pl.BlockSpec((tm, D_MODEL), lambda i: (i, 5)), pl.BlockSpec((tm, D_MODEL), lambda i: (i, 6))],
        out_specs=[row] * 6 + [pl.BlockSpec((2, tm, D_MODEL), lambda i: (0, i, 0)), vec, vec],
        compiler_params=_params("arbitrary"),
    )(dy, d_h2, x1, mix, g3, g2, w_out, w_cb, w_lb, pa, pb, proj, proj)


def _conv_mixer_bwd(proj, d_ya, w_short):
    t = proj.shape[0]
    rc = _row_chunk(t)

    def body(b_ref, c_ref, x_ref, dy_ref, w_ref, d_ref, dw_ref, pad, back):
        pad[pl.ds(0, PAD), :] = jnp.zeros((PAD, CB), F32)
        back[pl.ds(t, PAD), :] = jnp.zeros((PAD, CB), F32)
        for r0 in range(0, t, rc):
            rows = pl.ds(r0, rc)
            pad[pl.ds(PAD + r0, rc), :] = c_ref[rows, :].astype(F32) * x_ref[rows, :].astype(F32)
        w = w_ref[...]
        for r0 in range(0, t, rc):
            rows = pl.ds(r0, rc)
            d_y = dy_ref[rows, :].astype(F32)
            d_ref[0, rows, :] = (d_y * _conv_causal(pad, w, r0, rc, 3)).astype(BF16)
            back[rows, :] = d_y * b_ref[rows, :].astype(F32)
        taps = [jnp.zeros((1, CB), F32)] * 3
        for r0 in range(0, t, rc):
            rows = pl.ds(r0, rc)
            d_u = _conv_anticausal(back, w, r0, rc, 3)
            d_ref[1, rows, :] = (d_u * x_ref[rows, :].astype(F32)).astype(BF16)
            d_ref[2, rows, :] = (d_u * c_ref[rows, :].astype(F32)).astype(BF16)
            taps = [acc + new for acc, new in zip(taps, _conv_wgrad(back[rows, :], pad, r0, rc, 3))]
        dw_ref[...] = jnp.concatenate(taps, axis=0)

    blk = pl.BlockSpec((t, CB), lambda h: (0, h))
    return pl.pallas_call(
        body, name="conv_mixer_bwd", grid=(D_MODEL // CB,),
        out_shape=[jax.ShapeDtypeStruct((3, t, D_MODEL), BF16), jax.ShapeDtypeStruct((3, D_MODEL), F32)],
        in_specs=[_section(0, t), _section(1, t), _section(2, t), blk, pl.BlockSpec((3, CB), lambda h: (0, h))],
        out_specs=[pl.BlockSpec((3, t, CB), lambda h: (0, 0, h)), pl.BlockSpec((3, CB), lambda h: (0, h))],
        scratch_shapes=[pltpu.VMEM((t + PAD, CB), F32), pltpu.VMEM((t + PAD, CB), F32)],
        compiler_params=_params("parallel"),
    )(proj, proj, proj, d_ya, w_short)


LRU_SMALL_ROWS = 8


def _lru_bwd(proj, hl, a_all, kept, d_yb, w_conv, wa, wx, lam):
    t = proj.shape[0]
    rc = _row_chunk(t)
    vec, mat = _head_specs()

    def body(lx_ref, ly_ref, hl_ref, a_ref, kept_ref, dy_ref, wc_ref, wa_ref, wx_ref, lam_ref,
             d_ref, dwa_ref, dwx_ref, small_ref, pad, a_next, dh_s, h_prev, back, acc_a, acc_x):
        zeros = jnp.zeros((PAD, CB), F32)
        pad[pl.ds(0, PAD), :] = zeros
        h_prev[pl.ds(0, PAD), :] = zeros
        a_next[pl.ds(t, PAD), :] = zeros
        back[pl.ds(t, PAD), :] = zeros
        for r0 in range(0, t, rc):
            rows = pl.ds(r0, rc)
            pad[pl.ds(PAD + r0, rc), :] = lx_ref[rows, :].astype(F32)
            h_prev[pl.ds(PAD + r0, rc), :] = hl_ref[rows, :]
            a_next[pl.ds(PAD - 1 + r0, rc), :] = a_ref[rows, :]
            act, d_act = _gelu_and_grad(ly_ref[rows, :].astype(F32))
            d_y = dy_ref[rows, :].astype(F32)
            dh_s[rows, :] = d_y * act
            d_ref[1, rows, :] = (d_y * hl_ref[rows, :] * d_act).astype(BF16)
        wc = wc_ref[...]
        wa_m, wx_m = wa_ref[...].reshape(HEAD_DIM, HEAD_DIM), wx_ref[...].reshape(HEAD_DIM, HEAD_DIM)
        ls = _log_sigmoid(lam_ref[...])

        row = lax.broadcasted_iota(jnp.int32, (SUBLANES, CB), 0)
        groups = t // SUBLANES

        def group(i, carry):
            r = pl.multiple_of((groups - 1 - i) * SUBLANES, SUBLANES)
            a_g, b_g = a_next[pl.ds(PAD + r, SUBLANES), :], dh_s[pl.ds(r, SUBLANES), :]
            for s in (1, 2, 4):
                keep = row < SUBLANES - s
                b_g = jnp.where(keep, a_g * pltpu.roll(b_g, SUBLANES - s, 0) + b_g, b_g)
                a_g = jnp.where(keep, a_g * pltpu.roll(a_g, SUBLANES - s, 0), a_g)
            d_g = b_g + a_g * carry
            dh_s[pl.ds(r, SUBLANES), :] = d_g
            return jnp.broadcast_to(d_g[0:1, :], (SUBLANES, CB))

        lax.fori_loop(0, groups, group, jnp.zeros((SUBLANES, CB), F32))

        acc_a[...] = jnp.zeros_like(acc_a)
        acc_x[...] = jnp.zeros_like(acc_x)
        d_ba = d_bx = d_ls = jnp.zeros((1, CB), F32)
        for r0 in range(0, t, rc):
            rows = pl.ds(r0, rc)
            first = (lax.broadcasted_iota(jnp.int32, (rc, CB), 0) + r0) == 0
            xb, a = kept_ref[0, rows, :], a_ref[rows, :]
            xl, ra, ia = xb.astype(F32), kept_ref[1, rows, :].astype(F32), kept_ref[2, rows, :].astype(F32)
            a_sq = a * a
            mult = jnp.where(first, 1.0, jnp.sqrt(1.0 - a_sq))
            d_h = dh_s[rows, :]
            d_a = d_h * h_prev[pl.ds(PAD - 1 + r0, rc), :]
            d_mult = d_h * ia * xl
            d_ia = d_h * mult * xl
            d_xl = d_h * mult * ia
            d_la = d_a * a + d_mult * jnp.where(first, 0.0, -a_sq / mult)
            d_ls = d_ls + jnp.sum(d_la * ra, axis=0, keepdims=True) * LRU_C
            d_za = d_la * (LRU_C * ls) * ra * (1.0 - ra)
            d_zx = d_ia * ia * (1.0 - ia)
            d_ba = d_ba + jnp.sum(d_za, axis=0, keepdims=True)
            d_bx = d_bx + jnp.sum(d_zx, axis=0, keepdims=True)
            d_za, d_zx = d_za.astype(BF16), d_zx.astype(BF16)
            acc_a[...] += _dot_tn(xb, d_za)
            acc_x[...] += _dot_tn(xb, d_zx)
            back[rows, :] = d_xl + _dot_nt(d_za, wa_m) + _dot_nt(d_zx, wx_m)
        taps = [jnp.zeros((1, CB), F32)] * 4
        d_bc = jnp.zeros((1, CB), F32)
        for r0 in range(0, t, rc):
            rows = pl.ds(r0, rc)
            d_ref[0, rows, :] = _conv_anticausal(back, wc, r0, rc, 4).astype(BF16)
            g = back[rows, :]
            taps = [acc + new for acc, new in zip(taps, _conv_wgrad(g, pad, r0, rc, 4))]
            d_bc = d_bc + jnp.sum(g, axis=0, keepdims=True)
        d_lam = d_ls * jax.nn.sigmoid(-lam_ref[...])
        small_ref[...] = jnp.concatenate(taps + [d_bc, d_ba, d_bx, d_lam], axis=0)
        dwa_ref[...] = acc_a[...].reshape(N_DEV, HEAD_DIM // N_DEV, HEAD_DIM).astype(BF16)
        dwx_ref[...] = acc_x[...].reshape(N_DEV, HEAD_DIM // N_DEV, HEAD_DIM).astype(BF16)

    blk = pl.BlockSpec((t, CB), lambda h: (0, h))
    gate_grad = jax.ShapeDtypeStruct((N_DEV, N_HEADS, HEAD_DIM // N_DEV, HEAD_DIM), BF16)
    return pl.pallas_call(
        body, name="lru_bwd", grid=(N_HEADS,),
        out_shape=[jax.ShapeDtypeStruct((2, t, D_MODEL), BF16), gate_grad, gate_grad,
                   jax.ShapeDtypeStruct((LRU_SMALL_ROWS, D_MODEL), F32)],
        in_specs=[_section(3, t), _section(4, t), blk, blk, pl.BlockSpec((3, t, CB), lambda h: (0, 0, h)), blk,
                  pl.BlockSpec((4, CB), lambda h: (0, h)), mat, mat, vec],
        out_specs=[pl.BlockSpec((2, t, CB), lambda h: (0, 0, h)), mat, mat,
                   pl.BlockSpec((LRU_SMALL_ROWS, CB), lambda h: (0, h))],
        scratch_shapes=[pltpu.VMEM((t + PAD, CB), F32), pltpu.VMEM((t + PAD, CB), F32), pltpu.VMEM((t, CB), F32),
                        pltpu.VMEM((t + PAD, CB), F32), pltpu.VMEM((t + PAD, CB), F32),
                        pltpu.VMEM((HEAD_DIM, HEAD_DIM), F32), pltpu.VMEM((HEAD_DIM, HEAD_DIM), F32)],
        compiler_params=_params("parallel"),
    )(proj, proj, hl, a_all, kept, d_yb, w_conv, wa, wx, lam)


def _stack_maps(halves):
    def conv(sec, part):
        return jnp.minimum(sec, 2), jnp.where(sec < 3, part, halves - 1)

    def lru(sec, part):
        return jnp.clip(sec - 3, 0, 1), jnp.where(sec < 3, 0, jnp.where(sec < 5, part, halves - 1))

    def gate(sec, part):
        return jnp.clip(sec - 5, 0, 1), jnp.where(sec < 5, 0, part)

    return conv, lru, gate


def _pick_stack(sec, refs, fn):
    @pl.when(sec < 3)
    def _():
        fn(refs[0])

    @pl.when((sec >= 3) & (sec < 5))
    def _():
        fn(refs[1])

    @pl.when(sec >= 5)
    def _():
        fn(refs[2])


def _in_proj_wgrad(h, d_conv, d_lru, d_gate):
    t = h.shape[0]
    halves, bn = 2, D_MODEL // 2
    maps = _stack_maps(halves)

    def body(h_ref, dc_ref, dl_ref, dg_ref, o_ref):
        def emit(ref):
            o_ref[...] = _dot_tn(h_ref[...], ref[...]).astype(BF16)
        _pick_stack(pl.program_id(0) // halves, (dc_ref, dl_ref, dg_ref), emit)

    def spec(m):
        def index(s):
            stack, part = m(s // halves, s % halves)
            return stack, 0, part
        return pl.BlockSpec((None, t, bn), index)

    return pl.pallas_call(
        body, name="in_proj_wgrad", grid=(7 * halves,), out_shape=jax.ShapeDtypeStruct((D_MODEL, IN_COLS), BF16),
        in_specs=[pl.BlockSpec((t, D_MODEL), lambda s: (0, 0))] + [spec(m) for m in maps],
        out_specs=pl.BlockSpec((D_MODEL, bn), lambda s: (0, s)),
        compiler_params=_params("arbitrary"),
    )(h, d_conv, d_lru, d_gate)


def _in_proj_xgrad(d_conv, d_lru, d_gate, w_in, x, dx1, g1):
    t = x.shape[0]
    tm = min(1024, t)
    maps = _stack_maps(1)

    def body(dc_ref, dl_ref, dg_ref, w_ref, x_ref, dx1_ref, g_ref, dx_ref, dgain_ref, acc):
        i, s = pl.program_id(0), pl.program_id(1)

        @pl.when((i == 0) & (s == 0))
        def _():
            dgain_ref[...] = jnp.zeros_like(dgain_ref)

        @pl.when(s == 0)
        def _():
            acc[...] = jnp.zeros_like(acc)

        def add(ref):
            acc[...] += _dot_nt(ref[...], w_ref[...])
        _pick_stack(s, (dc_ref, dl_ref, dg_ref), add)

        @pl.when(s == 6)
        def _():
            n1, r1 = _rms_fwd(x_ref[...])
            d_h = acc[...]
            dgain_ref[...] += jnp.sum(d_h * n1, axis=0, keepdims=True)
            dx_ref[...] = dx1_ref[...] + _rms_bwd(n1, r1, d_h * g_ref[...])

    def spec(m):
        def index(i, s):
            return m(s, 0)[0], i, 0
        return pl.BlockSpec((None, tm, D_MODEL), index)

    row = pl.BlockSpec((tm, D_MODEL), lambda i, s: (i, 0))
    vec = pl.BlockSpec((1, D_MODEL), lambda i, s: (0, 0))
    return pl.pallas_call(
        body, name="in_proj_xgrad", grid=(t // tm, 7),
        out_shape=[jax.ShapeDtypeStruct((t, D_MODEL), F32), jax.ShapeDtypeStruct((1, D_MODEL), F32)],
        in_specs=[spec(m) for m in maps] + [pl.BlockSpec((D_MODEL, D_MODEL), lambda i, s: (0, s)), row, row, vec],
        out_specs=[row, vec],
        scratch_shapes=[pltpu.VMEM((tm, D_MODEL), F32)],
        compiler_params=_params("arbitrary", "arbitrary"),
    )(d_conv, d_lru, d_gate, w_in, x, dx1, g1)


def _add_pair(grad, got, by_cols, pos, name):
    cols = got.shape[-1]
    got3 = got.reshape(4, -1, cols)
    rows = got3.shape[1]
    rb = _row_block(rows, 1024)

    def block(k, p):
        return 4 * ((p[0] + k % 2) % 2) + 2 * ((p[1] + k // 2) % 2) + p[2]

    if by_cols:
        g_in, g_spec = grad, pl.BlockSpec((rb, cols), lambda k, i, p: (i, block(k, p)))
    else:
        g_in = grad.reshape(N_DEV, rows, cols)
        g_spec = pl.BlockSpec((None, rb, cols), lambda k, i, p: (block(k, p), i, 0))
    slot = pl.BlockSpec((None, rb, cols), lambda k, i, p: (k, i, 0))

    def body(pos_ref, a_ref, b_ref, o_ref):
        o_ref[...] = (a_ref[...].astype(F32) + b_ref[...].astype(F32)).astype(BF16)

    out = pl.pallas_call(
        body, name=name, out_shape=jax.ShapeDtypeStruct(got3.shape, BF16),
        grid_spec=pltpu.PrefetchScalarGridSpec(num_scalar_prefetch=1, grid=(4, rows // rb),
                                               in_specs=[g_spec, slot], out_specs=slot),
        compiler_params=_params("parallel", "parallel"),
    )(pos, g_in, got3)
    return out.reshape(got.shape)


def _adamw(w, g, m, v):
    m = ADAM_B1 * m + (1.0 - ADAM_B1) * g
    v = ADAM_B2 * v + (1.0 - ADAM_B2) * (g * g)
    m_hat = m / (1.0 - ADAM_B1 ** ADAM_STEP)
    v_hat = v / (1.0 - ADAM_B2 ** ADAM_STEP)
    return -ADAM_LR * (m_hat / (jnp.sqrt(v_hat) + ADAM_EPS) + ADAM_WD * w), m, v


def _adam_large(w, m, v, own, others, name):
    shape = w.shape
    cols = shape[-1]
    w2, m2, v2 = (a.reshape(-1, cols) for a in (w, m, v))
    rows = w2.shape[0]
    own, others = own.reshape(4, rows, cols), others.reshape(3, rows, cols)
    rb = _row_block(rows, 512)

    def body(w_ref, m_ref, v_ref, own_ref, oth_ref, g_ref, d_ref, nm_ref, nv_ref):
        g = own_ref[...].astype(F32)
        for k in range(3):
            g = g + oth_ref[k].astype(F32)
        g_ref[...] = g
        d_ref[...], nm_ref[...], nv_ref[...] = _adamw(w_ref[...], g, m_ref[...], v_ref[...])

    blk = pl.BlockSpec((rb, cols), lambda i: (i, 0))
    res = jax.ShapeDtypeStruct((rows, cols), F32)
    outs = pl.pallas_call(
        body, name=name, grid=(rows // rb,), out_shape=[res] * 4,
        in_specs=[blk, blk, blk, pl.BlockSpec((None, rb, cols), lambda i: (0, i, 0)),
                  pl.BlockSpec((3, rb, cols), lambda i: (0, i, 0))],
        out_specs=[blk] * 4, compiler_params=_params("parallel"),
    )(w2, m2, v2, own, others)
    return [o.reshape(shape) for o in outs]


def _adam_small(ws, gs, ms, vs):
    n = len(ws)

    def body(*refs):
        w_refs, g_refs, m_refs, v_refs = (refs[i * n:(i + 1) * n] for i in range(4))
        outs = refs[4 * n:]
        for i in range(n):
            d, m, v = _adamw(w_refs[i][...], g_refs[i][...], m_refs[i][...], v_refs[i][...])
            outs[i][...], outs[n + i][...], outs[2 * n + i][...] = d, m, v

    shapes = [jax.ShapeDtypeStruct(w.shape, F32) for w in ws]
    outs = pl.pallas_call(
        body, name="adam_small", out_shape=shapes * 3,
        in_specs=[VMEM_SPEC] * (4 * n), out_specs=[VMEM_SPEC] * (3 * n), compiler_params=_params(),
    )(*ws, *gs, *ms, *vs)
    return outs[:n], outs[n:2 * n], outs[2 * n:]


def _pack_rows(pieces):
    tile = SUBLANES * LANES
    return jnp.concatenate([jnp.pad(p.reshape(-1), (0, (-p.size) % tile)).reshape(-1, LANES) for p in pieces], axis=0)


def _packed_starts(sizes):
    tile = SUBLANES * LANES
    starts = [0]
    for s in sizes:
        starts.append(starts[-1] + (s + tile - 1) // tile * SUBLANES)
    return starts


def kernel(x, norm_mix_pre, norm_mix_post, norm_ffn_pre, norm_ffn_post, w_in, conv_short_w, w_conv_branch, lru_conv_w, lru_conv_b, lru_wa, lru_ba, lru_wx, lru_bx, lru_lambda, w_lru_branch, w_out, ffn_w_up, ffn_conv_w, ffn_conv_b, ffn_w_down, loss_target, m_norm_mix_pre, m_norm_mix_post, m_norm_ffn_pre, m_norm_ffn_post, m_w_in, m_conv_short_w, m_w_conv_branch, m_lru_conv_w, m_lru_conv_b, m_lru_wa, m_lru_ba, m_lru_wx, m_lru_bx, m_lru_lambda, m_w_lru_branch, m_w_out, m_ffn_w_up, m_ffn_conv_w, m_ffn_conv_b, m_ffn_w_down, v_norm_mix_pre, v_norm_mix_post, v_norm_ffn_pre, v_norm_ffn_post, v_w_in, v_conv_short_w, v_w_conv_branch, v_lru_conv_w, v_lru_conv_b, v_lru_wa, v_lru_ba, v_lru_wx, v_lru_bx, v_lru_lambda, v_w_lru_branch, v_w_out, v_ffn_w_up, v_ffn_conv_w, v_ffn_conv_b, v_ffn_w_down):
    t = x.shape[1]
    xi, yi, ci = _position()
    me = _block_of(xi, yi, ci)
    x2, target = x[0], loss_target[0]
    shard_in, shard_up = IN_COLS // N_DEV, 2 * D_FF // N_DEV
    shard_sq, shard_down, shard_head = D_MODEL // N_DEV, D_FF // N_DEV, HEAD_DIM // N_DEV

    names = ["w_in", "lru_wa", "lru_wx", "w_conv_branch", "w_lru_branch", "w_out", "ffn_w_up", "ffn_w_down"]
    large = [w_in[0], lru_wa[0], lru_wx[0], w_conv_branch[0], w_lru_branch[0], w_out[0], ffn_w_up[0], ffn_w_down[0]]
    blocks = [_cols(shard_in), _lead, _lead, _rows(shard_sq), _rows(shard_sq), _rows(shard_sq),
              _cols(shard_up), _rows(shard_down)]
    gate_full = (N_DEV, N_HEADS, shard_head, HEAD_DIM)
    full_shapes = [(D_MODEL, IN_COLS), gate_full, gate_full, (D_MODEL, D_MODEL), (D_MODEL, D_MODEL), (D_MODEL, D_MODEL),
                   (D_MODEL, 2 * D_FF), (D_FF, D_MODEL)]
    n_now = 3
    small_sharded = [conv_short_w, lru_conv_w, lru_ba, lru_bx, ffn_conv_w]
    small_mine = _pack_rows(small_sharded)
    small_at = _packed_starts([p.size for p in small_sharded])
    *gathered, small_all = _gather_weights(large, blocks, full_shapes, small_mine, n_now)
    g_in, g_wa, g_wx = gathered[:n_now]
    later_blocks = blocks[n_now:]
    send1, recv1, later, gather_token = _gather_start(gathered[n_now:], later_blocks, "gather_start")

    def behind(token, operand):
        return operand + token[0:1, 0:1]

    def forward(lo, hi, after, tag):
        return _gather_forward(later[lo:hi], later_blocks[lo:hi], send1[4 * lo:4 * hi], recv1[4 * lo:4 * hi], after,
                               "gather_forward_" + tag)

    def finish(lo, hi, flight, after, tag):
        return _gather_finish(flight[2], later_blocks[lo:hi], flight[0], flight[1], after, "gather_finish_" + tag)

    def cols_of(r0, n, width):
        part = small_all[:, r0:r0 + n * width // LANES, :].reshape(N_DEV, n, width)
        return part.transpose(1, 0, 2).reshape(n, N_DEV * width)

    c_short = cols_of(small_at[0], 3, LANES)
    c_lru = cols_of(small_at[1], 4, LANES)
    b_a = cols_of(small_at[2], N_HEADS, shard_head).reshape(1, D_MODEL)
    b_x = cols_of(small_at[3], N_HEADS, shard_head).reshape(1, D_MODEL)
    c_ffn = cols_of(small_at[4], 3, shard_up)

    proj, h = _in_proj(x2, behind(gather_token, norm_mix_pre), g_in)
    flight_mix_w = forward(0, 3, h, "mix")
    y_a = _conv_mixer_fwd(proj, c_short)
    y_b, hl, decay, lru_kept = _lru_fwd(proj, c_lru, lru_conv_b, g_wa, b_a, g_wx, b_x, lru_lambda)
    flight_up_w = forward(3, 4, y_b, "up")
    g_cb, g_lb, g_out = finish(0, 3, flight_mix_w, y_b, "mix")
    pa, pb, merged, mix, x1, h2 = _merge(y_a, y_b, proj, x2, g_cb, g_lb, g_out, norm_mix_post, norm_ffn_pre)
    flight_down_w = forward(4, 5, h2, "down")
    (g_up,) = finish(3, 4, flight_up_w, h2, "up")
    up, act, f = _ffn_up(h2, g_up, c_ffn, ffn_conv_b)
    (g_down,) = finish(4, 5, flight_down_w, f, "down")
    dy, d_out, d_act, dg4, loss_part = _ffn_down(f, act, g_down, x1, target, norm_ffn_post)

    block_of = dict(zip(names, blocks))
    shard_shapes = {"w_in": (D_MODEL, shard_in), "w_conv_branch": (shard_sq, D_MODEL), "w_lru_branch": (shard_sq, D_MODEL),
                    "w_out": (shard_sq, D_MODEL), "lru_wa": (N_HEADS, shard_head, HEAD_DIM),
                    "lru_wx": (N_HEADS, shard_head, HEAD_DIM), "ffn_w_up": (D_MODEL, shard_up),
                    "ffn_w_down": (shard_down, D_MODEL)}
    pos = jnp.stack([xi, yi, ci]).astype(jnp.int32)

    def reduce_start(tag, grads):
        keys = list(grads)
        got = _exchange_pair([grads[k] for k in keys], [block_of[k] for k in keys], [shard_shapes[k] for k in keys],
                             "reduce_pair_exchange_" + tag)
        sums = [_add_pair(grads[k], g, k in ("w_in", "ffn_w_up"), pos, "pair_sum_" + k) for k, g in zip(keys, got)]
        return (keys,) + _exchange_chips_start(sums, "reduce_chip_start_" + tag)

    gw_down = _grad_tn(f, d_out, min(512, D_FF), "ffn_down_wgrad")
    flight_down = reduce_start("down", {"ffn_w_down": gw_down})
    gw_up, gc_ffn, gb_ffn, d_h2 = _ffn_up_bwd(up, d_act, behind(flight_down[-1], c_ffn), h2, g_up)
    flight_up = reduce_start("up", {"ffn_w_up": gw_up})
    dx1, d_mix, d_pa, d_pb, d_ya, d_yb, d_gate, dg3, dg2 = _merge_bwd(
        dy, d_h2, x1, mix, behind(flight_up[-1], norm_ffn_pre), norm_mix_post, g_out, g_cb, g_lb, pa, pb, proj)
    gw_out = _grad_tn(merged, d_mix, CB, "w_out_wgrad")
    gw_cb = _grad_tn(y_a, d_pa, CB, "w_conv_branch_wgrad")
    gw_lb = _grad_tn(y_b, d_pb, CB, "w_lru_branch_wgrad")
    flight_mix = reduce_start("mix", {"w_conv_branch": gw_cb, "w_lru_branch": gw_lb, "w_out": gw_out})
    d_conv, gc_short = _conv_mixer_bwd(proj, d_ya, behind(flight_mix[-1], c_short))
    d_lru, gw_a, gw_x, g_lru_small = _lru_bwd(proj, hl, decay, lru_kept, d_yb, c_lru, g_wa, g_wx, lru_lambda)
    early = [dg2, dg3, dg4, g_lru_small[4:5], g_lru_small[7:8], gb_ffn, gc_short, g_lru_small[0:4],
             g_lru_small[5:6], g_lru_small[6:7], gc_ffn, loss_part]
    flight_small = _small_start(_pack_rows(early), "small_start")
    gw_in = _in_proj_wgrad(h, d_conv, d_lru, d_gate)
    flight_in = reduce_start("in", {"lru_wa": gw_a, "lru_wx": gw_x, "w_in": gw_in})
    dx, dg1 = _in_proj_xgrad(d_conv, d_lru, d_gate, g_in, x2, dx1,
                             behind(flight_small[-1], behind(flight_in[-1], norm_mix_pre)))
    flight_late = _small_start(_pack_rows([dg1]), "small_start_late")

    moments ={"w_in": (m_w_in, v_w_in), "w_conv_branch": (m_w_conv_branch, v_w_conv_branch),
               "w_lru_branch": (m_w_lru_branch, v_w_lru_branch), "w_out": (m_w_out, v_w_out),
               "lru_wa": (m_lru_wa, v_lru_wa), "lru_wx": (m_lru_wx, v_lru_wx), "ffn_w_up": (m_ffn_w_up, v_ffn_w_up),
               "ffn_w_down": (m_ffn_w_down, v_ffn_w_down)}
    weights = {"w_in": w_in, "w_conv_branch": w_conv_branch, "w_lru_branch": w_lru_branch, "w_out": w_out,
               "lru_wa": lru_wa, "lru_wx": lru_wx, "ffn_w_up": ffn_w_up, "ffn_w_down": ffn_w_down}
    out_g, out_d, out_m, out_v = {}, {}, {}, {}

    after = flight_late[-1]
    for tag, (keys, send, recv, sums, lands, _) in (("down", flight_down), ("up", flight_up), ("mix", flight_mix),
                                                    ("in", flight_in)):
        sums, others = _exchange_chips_wait(send, recv, sums, lands, after, "reduce_chip_wait_" + tag)
        for k, own, oth in zip(keys, sums, others):
            out_g[k], out_d[k], out_m[k], out_v[k] = _adam_large(weights[k], *moments[k], own, oth, "adam_" + k)
        after = out_d[keys[-1]]

    total, total_late = _small_sum([_small_wait(*flight_small[:4], after, "small_wait"),
                                    _small_wait(*flight_late[:4], after, "small_wait_late")], me)
    sizes = [p.size for p in early]
    starts = _packed_starts(sizes)

    def piece(i, shape):
        if i == 0:
            return total_late.reshape(-1)[:D_MODEL].reshape(shape)
        return total[starts[i - 1]:starts[i]].reshape(-1)[:sizes[i - 1]].reshape(shape)

    loss = total[starts[11], 0]

    def col_shard(full, width):
        return lax.dynamic_slice_in_dim(full, me * width, width, axis=1)

    def head_shard(full):
        return lax.dynamic_slice_in_dim(full.reshape(N_HEADS, HEAD_DIM), me * shard_head, shard_head, axis=1)

    small_names = ["norm_mix_pre", "norm_mix_post", "norm_ffn_pre", "norm_ffn_post", "lru_conv_b", "lru_lambda",
                   "ffn_conv_b", "conv_short_w", "lru_conv_w", "lru_ba", "lru_bx", "ffn_conv_w"]
    small_g = [piece(0, (1, D_MODEL)), piece(1, (1, D_MODEL)), piece(2, (1, D_MODEL)), piece(3, (1, D_MODEL)),
               piece(4, (1, D_MODEL)), piece(5, (1, D_MODEL)), piece(6, (1, 2 * D_FF)),
               col_shard(piece(7, (3, D_MODEL)), LANES), col_shard(piece(8, (4, D_MODEL)), LANES),
               head_shard(piece(9, (1, D_MODEL))), head_shard(piece(10, (1, D_MODEL))),
               col_shard(piece(11, (3, 2 * D_FF)), shard_up)]
    small_w = [norm_mix_pre, norm_mix_post, norm_ffn_pre, norm_ffn_post, lru_conv_b, lru_lambda, ffn_conv_b,
               conv_short_w[0], lru_conv_w[0], lru_ba[0], lru_bx[0], ffn_conv_w[0]]
    small_m = [m_norm_mix_pre, m_norm_mix_post, m_norm_ffn_pre, m_norm_ffn_post, m_lru_conv_b, m_lru_lambda,
               m_ffn_conv_b, m_conv_short_w[0], m_lru_conv_w[0], m_lru_ba[0], m_lru_bx[0], m_ffn_conv_w[0]]
    small_v = [v_norm_mix_pre, v_norm_mix_post, v_norm_ffn_pre, v_norm_ffn_post, v_lru_conv_b, v_lru_lambda,
               v_ffn_conv_b, v_conv_short_w[0], v_lru_conv_w[0], v_lru_ba[0], v_lru_bx[0], v_ffn_conv_w[0]]
    s_d, s_m, s_v = _adam_small(small_w, small_g, small_m, small_v)
    for i, name in enumerate(small_names):
        shape = small_w[i].shape if i < 7 else (1,) + small_w[i].shape
        out_g[name] = small_g[i].reshape(shape)
        out_d[name], out_m[name], out_v[name] = s_d[i].reshape(shape), s_m[i].reshape(shape), s_v[i].reshape(shape)

    order = ["norm_mix_pre", "norm_mix_post", "norm_ffn_pre", "norm_ffn_post", "w_in", "conv_short_w", "w_conv_branch",
             "lru_conv_w", "lru_conv_b", "lru_wa", "lru_ba", "lru_wx", "lru_bx", "lru_lambda", "w_lru_branch", "w_out",
             "ffn_w_up", "ffn_conv_w", "ffn_conv_b", "ffn_w_down"]
    return (loss, dx.reshape(1, t, D_MODEL), *[out_g[k] for k in order], *[out_d[k] for k in order],
            *[out_m[k] for k in order], *[out_v[k] for k in order])
```

```python
import functools
import math

import jax
import jax.numpy as jnp
from jax import lax
from jax.experimental import pallas as pl
from jax.experimental.pallas import tpu as pltpu

F32 = jnp.float32
BF16 = jnp.bfloat16
MESH = pl.DeviceIdType.MESH

N_DEV = 8
D_MODEL = 1024
N_HEADS = 4
HEAD_DIM = D_MODEL // N_HEADS
D_FF = 3 * D_MODEL
IN_COLS = 7 * D_MODEL
LRU_C = 8.0
RMS_EPS = 1e-6
ADAM_LR = 0.001
ADAM_B1 = 0.9
ADAM_B2 = 0.999
ADAM_EPS = 1e-08
ADAM_WD = 0.01
ADAM_STEP = 10
GELU_K = math.sqrt(2.0 / math.pi)
GELU_C = 0.044715

LANES = 128
SUBLANES = 8
PAD = SUBLANES
VMEM_LIMIT = 56 * 1024 * 1024
CB = 256

HBM_SPEC = pl.BlockSpec(memory_space=pltpu.HBM)
SEM_SPEC = pl.BlockSpec(memory_space=pltpu.SEMAPHORE)
DATAFLOW_EFFECT = pltpu.SideEffectType.DATAFLOW_SIDE_EFFECTING
VMEM_SPEC = pl.BlockSpec(memory_space=pltpu.VMEM)


def _params(*sem):
    if sem:
        return pltpu.CompilerParams(dimension_semantics=sem, vmem_limit_bytes=VMEM_LIMIT)
    return pltpu.CompilerParams(vmem_limit_bytes=VMEM_LIMIT)


def _row_chunk(t):
    return min(256, t)


def _row_block(rows, cap):
    return next(rb for rb in range(min(cap, rows), 0, -16) if rows % rb == 0)


def _gelu(x):
    return 0.5 * x * (1.0 + jnp.tanh(GELU_K * (x + GELU_C * x * x * x)))


def _gelu_and_grad(x):
    t = jnp.tanh(GELU_K * (x + GELU_C * x * x * x))
    g = 0.5 * x * (1.0 + t)
    dg = 0.5 * (1.0 + t) + 0.5 * x * (1.0 - t * t) * GELU_K * (1.0 + 3.0 * GELU_C * x * x)
    return g, dg


def _expm1_neg(x):
    series = x * (1.0 + x * (0.5 + x * (1.0 / 6.0 + x * (1.0 / 24.0 + x * (1.0 / 120.0)))))
    return jnp.where(x > -0.05, series, jnp.exp(x) - 1.0)


def _log_sigmoid(x):
    return jnp.minimum(x, 0.0) - jnp.log1p(jnp.exp(-jnp.abs(x)))


def _dot(a, b):
    return jnp.dot(a, b, preferred_element_type=F32)


def _dot_nt(a, b):
    return lax.dot_general(a, b, (((1,), (1,)), ((), ())), preferred_element_type=F32)


def _dot_tn(a, b):
    return lax.dot_general(a, b, (((0,), (0,)), ((), ())), preferred_element_type=F32)


def _rms_fwd(x):
    r = lax.rsqrt(jnp.mean(x * x, axis=-1, keepdims=True) + RMS_EPS)
    return x * r, r


def _rms_bwd(n, r, gdy):
    return r * (gdy - n * jnp.mean(n * gdy, axis=-1, keepdims=True))


def _conv_causal(pad_ref, w, r0, rows, taps):
    cur = pad_ref[pl.ds(PAD + r0, rows), :]
    before = pad_ref[pl.ds(PAD + r0 - SUBLANES, SUBLANES), :]
    row = lax.broadcasted_iota(jnp.int32, before.shape, 0)
    acc = w[taps - 1:taps, :] * cur
    for j in range(1, taps):
        rolled = pltpu.roll(cur, j, 0)
        top = jnp.where(row < j, pltpu.roll(before, j, 0), rolled[0:SUBLANES, :])
        acc = acc + w[taps - 1 - j:taps - j, :] * jnp.concatenate([top, rolled[SUBLANES:, :]], axis=0)
    return acc


def _conv_bwd(after_ref, w, x, r0, rows, taps):
    ahead = [after_ref[pl.ds(r0 + j, rows), :] for j in range(taps)]
    d_x = None
    for k in range(taps):
        term = w[k:k + 1, :] * ahead[taps - 1 - k]
        d_x = term if d_x is None else d_x + term
    d_w = [jnp.sum(ahead[taps - 1 - k] * x, axis=0, keepdims=True) for k in range(taps)]
    return d_x, d_w, ahead[0]


def _position():
    return lax.axis_index("x"), lax.axis_index("y"), lax.axis_index("c")


def _block_of(x, y, c):
    return 4 * x + 2 * y + c


def _chip(x, y, k):
    return (x + (k & 1)) % 2, (y + (k >> 1)) % 2


def _cols(width):
    def at(ref, d):
        return ref.at[:, pl.ds(pl.multiple_of(d * width, LANES), width)]
    return at


def _rows(height):
    def at(ref, d):
        return ref.at[pl.ds(pl.multiple_of(d * height, 16), height), :]
    return at


def _lead(ref, d):
    return ref.at[d]


def _gather_weights(shards, blocks, full_shapes, small, n_now):
    n = len(shards)
    small_rows = small.shape[0]

    def body(*refs):
        ins, small_in = refs[:n], refs[n]
        outs, small_out = refs[n + 1:2 * n + 1], refs[2 * n + 1]
        stage = refs[2 * n + 2:3 * n + 2]
        send, recv, local = refs[3 * n + 2:]
        x, y, c = _position()
        me = _block_of(x, y, c)
        sibling = (x, y, 1 - c)

        for a in range(n):
            stage[a][...] = ins[a][...].astype(BF16)

        def copy(a, k, block, to, src=None):
            dst = blocks[a](outs[a], block)
            return pltpu.make_async_remote_copy(
                src_ref=dst if src is None else src, dst_ref=dst, send_sem=send.at[a, k], recv_sem=recv.at[a, k],
                device_id=to, device_id_type=MESH)

        def small_copy(k):
            px, py, pc = (x + (k & 1)) % 2, (y + ((k >> 1) & 1)) % 2, (c + (k >> 2)) % 2
            return pltpu.make_async_remote_copy(
                src_ref=small_in, dst_ref=small_out.at[me], send_sem=send.at[n_now, k - 1], recv_sem=recv.at[n_now, k - 1],
                device_id=(px, py, pc), device_id_type=MESH)

        def small_arrival(k):
            px, py, pc = (x + (k & 1)) % 2, (y + ((k >> 1) & 1)) % 2, (c + (k >> 2)) % 2
            return pltpu.make_async_remote_copy(
                src_ref=small_in, dst_ref=small_out.at[_block_of(px, py, pc)], send_sem=send.at[n_now, k - 1],
                recv_sem=recv.at[n_now, k - 1], device_id=(px, py, pc), device_id_type=MESH)

        small_out[me] = small_in[...]
        small_sends = [small_copy(k) for k in range(1, N_DEV)]
        for cp in small_sends:
            cp.start()

        mine, first, passed = [], [], []
        for a in range(n):
            own = pltpu.make_async_copy(stage[a], blocks[a](outs[a], me), local.at[a])
            own.start()
            mine.append(own)
            if a >= n_now:
                continue
            sends = [copy(a, 0, me, sibling, src=stage[a])]
            sends += [copy(a, k, me, (*_chip(x, y, k), c), src=stage[a]) for k in (1, 2, 3)]
            for cp in sends:
                cp.start()
            first += sends
        for a in range(n_now):
            for k in (1, 2, 3):
                landed = _block_of(*_chip(x, y, k), c)
                copy(a, k, landed, (x, y, c)).wait_recv()
                fwd = copy(a, 3 + k, landed, sibling)
                fwd.start()
                passed.append(fwd)
        for a in range(n_now):
            copy(a, 0, _block_of(x, y, 1 - c), (x, y, c)).wait_recv()
            for k in (1, 2, 3):
                copy(a, 3 + k, _block_of(*_chip(x, y, k), 1 - c), (x, y, c)).wait_recv()
        for k in range(1, N_DEV):
            small_arrival(k).wait_recv()
        for cp in first + passed + small_sends:
            cp.wait_send()
        for own in mine:
            own.wait()

    out_shape = [jax.ShapeDtypeStruct(s, BF16) for s in full_shapes]
    out_shape.append(jax.ShapeDtypeStruct((N_DEV, small_rows, LANES), F32))
    return pl.pallas_call(
        body, name="gather_weights", out_shape=out_shape,
        in_specs=[VMEM_SPEC] * (n + 1), out_specs=[HBM_SPEC] * n + [VMEM_SPEC],
        scratch_shapes=[pltpu.VMEM(s.shape, BF16) for s in shards]
        + [pltpu.SemaphoreType.DMA((n_now + 1, 7)), pltpu.SemaphoreType.DMA((n_now + 1, 7)),
           pltpu.SemaphoreType.DMA((n,))],
        compiler_params=_params(),
    )(*shards, small)


def _gather_first(full, blocks, send, recv):
    x, y, c = _position()
    me = _block_of(x, y, c)
    peers = [(x, y, 1 - c)] + [(*_chip(x, y, k), c) for k in (1, 2, 3)]

    def copy(a, k, block):
        at = blocks[a](full[a], block)
        return pltpu.make_async_remote_copy(src_ref=at, dst_ref=at, send_sem=send[4 * a + k], recv_sem=recv[4 * a + k],
                                            device_id=peers[k], device_id_type=MESH)

    sends = [copy(a, k, me) for a in range(len(full)) for k in range(4)]
    arrivals = [copy(a, k, _block_of(*peers[k])) for a in range(len(full)) for k in range(4)]
    return sends, arrivals


def _gather_second(full, blocks, send, recv):
    x, y, c = _position()

    def copy(a, k, cc):
        at = blocks[a](full[a], _block_of(*_chip(x, y, k), cc))
        return pltpu.make_async_remote_copy(src_ref=at, dst_ref=at, send_sem=send[3 * a + k - 1],
                                            recv_sem=recv[3 * a + k - 1], device_id=(x, y, 1 - c), device_id_type=MESH)

    sends = [copy(a, k, c) for a in range(len(full)) for k in (1, 2, 3)]
    arrivals = [copy(a, k, 1 - c) for a in range(len(full)) for k in (1, 2, 3)]
    return sends, arrivals


def _split_call(body, name, arrays, sems_in, n_sems_out, after=None, token=False):
    n, m = len(arrays), len(sems_in)

    def kernel_body(*refs):
        outs = refs[n + m + (after is not None):]
        body(refs[:n], refs[n:n + m], outs[:n_sems_out])
        if token:
            outs[-1][...] = jnp.zeros_like(outs[-1])

    extra_in = [] if after is None else [after]
    outs = pl.pallas_call(
        kernel_body, name=name,
        out_shape=(*[pltpu.SemaphoreType.DMA(())] * n_sems_out, *[pltpu.HBM(a.shape, a.dtype) for a in arrays],
                   *([jax.ShapeDtypeStruct((SUBLANES, LANES), F32)] if token else [])),
        in_specs=[HBM_SPEC] * n + [SEM_SPEC] * m + [pl.BlockSpec(memory_space=pl.ANY)] * len(extra_in),
        out_specs=(*[SEM_SPEC] * n_sems_out, *[HBM_SPEC] * n, *([VMEM_SPEC] if token else [])),
        input_output_aliases={i: n_sems_out + i for i in range(n)},
        compiler_params=pltpu.CompilerParams(has_side_effects=DATAFLOW_EFFECT),
    )(*[pltpu.with_memory_space_constraint(a, pltpu.HBM) for a in arrays], *sems_in, *extra_in)
    sems, rest = list(outs[:n_sems_out]), list(outs[n_sems_out:])
    return (sems, rest[:n], rest[n]) if token else (sems, rest[:n])


def _gather_start(full, blocks, name):
    n = len(full)

    def body(arrays, _, sems):
        for cp in _gather_first(arrays, blocks, sems[:4 * n], sems[4 * n:])[0]:
            cp.start()

    sems, arrays, token = _split_call(body, name, full, [], 8 * n, token=True)
    return sems[:4 * n], sems[4 * n:], arrays, token


def _gather_forward(full, blocks, send_first, recv_first, after, name):
    n = len(full)

    def body(arrays, sems_in, sems):
        sends, arrivals = _gather_first(arrays, blocks, sems_in[:4 * n], sems_in[4 * n:])
        for cp in arrivals:
            cp.wait_recv()
        for cp in _gather_second(arrays, blocks, sems[:3 * n], sems[3 * n:])[0]:
            cp.start()
        for cp in sends:
            cp.wait_send()

    sems, arrays = _split_call(body, name, full, [*send_first, *recv_first], 6 * n, after=after)
    return sems[:3 * n], sems[3 * n:], arrays


def _gather_finish(full, blocks, send_second, recv_second, after, name):
    n = len(full)

    def body(arrays, sems_in, _):
        sends, arrivals = _gather_second(arrays, blocks, sems_in[:3 * n], sems_in[3 * n:])
        for cp in sends:
            cp.wait_send()
        for cp in arrivals:
            cp.wait_recv()

    return _split_call(body, name, full, [*send_second, *recv_second], 0, after=after)[1]


def _exchange_pair(grads, blocks, shard_shapes, name):
    n = len(grads)

    def body(*refs):
        ins, got = refs[:n], refs[n:2 * n]
        send, recv = refs[2 * n:]
        x, y, c = _position()
        copies = []
        for a in range(n):
            for k in range(4):
                cp = pltpu.make_async_remote_copy(
                    src_ref=blocks[a](ins[a], _block_of(*_chip(x, y, k), 1 - c)), dst_ref=got[a].at[k],
                    send_sem=send.at[a, k], recv_sem=recv.at[a, k], device_id=(x, y, 1 - c), device_id_type=MESH)
                cp.start()
                copies.append(cp)
        for cp in copies:
            cp.wait()

    return pl.pallas_call(
        body, name=name, out_shape=[jax.ShapeDtypeStruct((4,) + tuple(s), BF16) for s in shard_shapes],
        in_specs=[HBM_SPEC] * n, out_specs=[HBM_SPEC] * n,
        scratch_shapes=[pltpu.SemaphoreType.DMA((n, 4)), pltpu.SemaphoreType.DMA((n, 4))],
        compiler_params=_params(),
    )(*grads)


def _chip_copies(sums, lands, send, recv):
    x, y, c = _position()
    return [pltpu.make_async_remote_copy(
        src_ref=sums[a].at[k], dst_ref=lands[a].at[k - 1], send_sem=send[3 * a + k - 1], recv_sem=recv[3 * a + k - 1],
        device_id=(*_chip(x, y, k), c), device_id_type=MESH) for a in range(len(sums)) for k in (1, 2, 3)]


def _exchange_chips_start(pair_sums, name):
    n = len(pair_sums)
    lands = [pltpu.with_memory_space_constraint(lax.empty((3,) + tuple(p.shape[1:]), BF16), pltpu.HBM) for p in pair_sums]

    def body(*refs):
        sums, zones = refs[:n], refs[n:2 * n]
        send, recv = refs[2 * n:5 * n], refs[5 * n:8 * n]
        token = refs[-1]
        for cp in _chip_copies(sums, zones, send, recv):
            cp.start()
        token[...] = jnp.zeros_like(token)

    outs = pl.pallas_call(
        body, name=name,
        out_shape=(*[pltpu.SemaphoreType.DMA(())] * (6 * n),
                   *[pltpu.HBM(p.shape, BF16) for p in pair_sums], *[pltpu.HBM(z.shape, BF16) for z in lands],
                   jax.ShapeDtypeStruct((SUBLANES, LANES), F32)),
        in_specs=[HBM_SPEC] * (2 * n), out_specs=(*[SEM_SPEC] * (6 * n), *[HBM_SPEC] * (2 * n), VMEM_SPEC),
        input_output_aliases={i: 6 * n + i for i in range(2 * n)},
        compiler_params=pltpu.CompilerParams(has_side_effects=DATAFLOW_EFFECT),
    )(*[pltpu.with_memory_space_constraint(p, pltpu.HBM) for p in pair_sums], *lands)
    return outs[:3 * n], outs[3 * n:6 * n], outs[6 * n:7 * n], outs[7 * n:8 * n], outs[-1]


def _exchange_chips_wait(send, recv, sums, lands, after, name):
    n = len(sums)

    def body(*refs):
        sums_in, zones = refs[:n], refs[n:2 * n]
        send_in, recv_in = refs[2 * n:5 * n], refs[5 * n:8 * n]
        for cp in _chip_copies(sums_in, zones, send_in, recv_in):
            cp.wait_send()
            cp.wait_recv()

    outs = pl.pallas_call(
        body, name=name,
        out_shape=(*[pltpu.HBM(p.shape, BF16) for p in sums], *[pltpu.HBM(z.shape, BF16) for z in lands]),
        in_specs=[HBM_SPEC] * (2 * n) + [SEM_SPEC] * (6 * n) + [pl.BlockSpec(memory_space=pl.ANY)],
        out_specs=[HBM_SPEC] * (2 * n), input_output_aliases={i: i for i in range(2 * n)},
        compiler_params=pltpu.CompilerParams(has_side_effects=DATAFLOW_EFFECT),
    )(*sums, *lands, *send, *recv, after)
    return outs[:n], outs[n:]


def _small_copies(mine, land, send, recv):
    x, y, c = _position()
    me = _block_of(x, y, c)

    def peer(k):
        return (x + (k & 1)) % 2, (y + ((k >> 1) & 1)) % 2, (c + (k >> 2)) % 2

    def copy(k, slot):
        return pltpu.make_async_remote_copy(src_ref=mine, dst_ref=land.at[slot], send_sem=send[k - 1], recv_sem=recv[k - 1],
                                            device_id=peer(k), device_id_type=MESH)

    return [copy(k, me) for k in range(1, N_DEV)], [copy(k, _block_of(*peer(k))) for k in range(1, N_DEV)]


def _small_start(part, name):
    land = jnp.zeros((N_DEV,) + part.shape, F32)

    def body(arrays, _, sems):
        for cp in _small_copies(arrays[0], arrays[1], sems[:7], sems[7:])[0]:
            cp.start()

    sems, arrays, token = _split_call(body, name, [part, land], [], 14, token=True)
    return sems[:7], sems[7:], arrays[0], arrays[1], token


def _small_wait(send, recv, part, land, after, name):
    def body(arrays, sems_in, _):
        sends, arrivals = _small_copies(arrays[0], arrays[1], sems_in[:7], sems_in[7:])
        for cp in sends:
            cp.wait_send()
        for cp in arrivals:
            cp.wait_recv()

    return _split_call(body, name, [part, land], [*send, *recv], 0, after=after)[1]


def _small_sum(pairs, me):
    n = len(pairs)

    def body(me_ref, *refs):
        for i in range(n):
            mine, land, out = refs[2 * i], refs[2 * i + 1], refs[2 * n + i]
            total = jnp.zeros(mine.shape, F32)
            for d in range(N_DEV):
                total = total + land[d] + jnp.where(me_ref[0] == d, mine[...], 0.0)
            out[...] = total

    flat = [a for pair in pairs for a in pair]
    return pl.pallas_call(
        body, name="small_sum", out_shape=[jax.ShapeDtypeStruct(mine.shape, F32) for mine, _ in pairs],
        in_specs=[pl.BlockSpec(memory_space=pltpu.SMEM)] + [VMEM_SPEC] * (2 * n), out_specs=[VMEM_SPEC] * n,
        compiler_params=_params(),
    )(me.reshape(1).astype(jnp.int32), *flat)


def _in_proj(x, g1, w_in):
    t = x.shape[0]
    tm, bn = min(1024, t), 1024

    def body(x_ref, g_ref, w_ref, proj_ref, h_ref, h_s):
        @pl.when(pl.program_id(1) == 0)
        def _():
            n, _ = _rms_fwd(x_ref[...])
            h_s[...] = (n * g_ref[...]).astype(BF16)
            h_ref[...] = h_s[...]
        proj_ref[...] = _dot(h_s[...], w_ref[...]).astype(BF16)

    return pl.pallas_call(
        body, name="in_proj", grid=(t // tm, IN_COLS // bn),
        out_shape=[jax.ShapeDtypeStruct((t, IN_COLS), BF16), jax.ShapeDtypeStruct((t, D_MODEL), BF16)],
        in_specs=[pl.BlockSpec((tm, D_MODEL), lambda i, j: (i, 0)), pl.BlockSpec((1, D_MODEL), lambda i, j: (0, 0)),
                  pl.BlockSpec((D_MODEL, bn), lambda i, j: (0, j))],
        out_specs=[pl.BlockSpec((tm, bn), lambda i, j: (i, j)), pl.BlockSpec((tm, D_MODEL), lambda i, j: (i, 0))],
        scratch_shapes=[pltpu.VMEM((tm, D_MODEL), BF16)],
        compiler_params=_params("parallel", "arbitrary"),
    )(x, g1, w_in)


def _section(s, t):
    return pl.BlockSpec((t, CB), lambda h, s=s: (0, s * (D_MODEL // CB) + h))


def _conv_mixer_fwd(proj, w_short):
    t = proj.shape[0]
    rc = _row_chunk(t)

    def body(b_ref, c_ref, x_ref, w_ref, y_ref, pad):
        pad[pl.ds(0, PAD), :] = jnp.zeros((PAD, CB), F32)
        for r0 in range(0, t, rc):
            rows = pl.ds(r0, rc)
            pad[pl.ds(PAD + r0, rc), :] = c_ref[rows, :].astype(F32) * x_ref[rows, :].astype(F32)
        w = w_ref[...]
        for r0 in range(0, t, rc):
            rows = pl.ds(r0, rc)
            y_ref[rows, :] = (b_ref[rows, :].astype(F32) * _conv_causal(pad, w, r0, rc, 3)).astype(BF16)

    return pl.pallas_call(
        body, name="conv_mixer_fwd", grid=(D_MODEL // CB,),
        out_shape=jax.ShapeDtypeStruct((t, D_MODEL), BF16),
        in_specs=[_section(0, t), _section(1, t), _section(2, t), pl.BlockSpec((3, CB), lambda h: (0, h))],
        out_specs=pl.BlockSpec((t, CB), lambda h: (0, h)),
        scratch_shapes=[pltpu.VMEM((t + PAD, CB), F32)],
        compiler_params=_params("parallel"),
    )(proj, proj, proj, w_short)


def _lru_gates(xl, wa, ba, wx, bx, ls, first_row):
    xb = xl.astype(BF16)
    ra = jax.nn.sigmoid(_dot(xb, wa) + ba)
    ia = jax.nn.sigmoid(_dot(xb, wx) + bx)
    la = LRU_C * ra * ls
    a = jnp.exp(la)
    one_minus = -_expm1_neg(2.0 * la)
    mult = jnp.where(first_row, 1.0, jnp.sqrt(one_minus))
    return xb, ra, ia, a, one_minus, mult


def _head_specs():
    vec = pl.BlockSpec((1, CB), lambda h: (0, h))
    mat = pl.BlockSpec((N_DEV, None, HEAD_DIM // N_DEV, HEAD_DIM), lambda h: (0, h, 0, 0))
    return vec, mat


def _lru_fwd(proj, w_conv, b_conv, wa, ba, wx, bx, lam):
    t = proj.shape[0]
    rc = _row_chunk(t)
    vec, mat = _head_specs()

    def body(lx_ref, ly_ref, wc_ref, bc_ref, wa_ref, ba_ref, wx_ref, bx_ref, lam_ref, yb_ref, hl_ref, a_ref, kept_ref,
             pad, u_s):
        pad[pl.ds(0, PAD), :] = jnp.zeros((PAD, CB), F32)
        for r0 in range(0, t, rc):
            pad[pl.ds(PAD + r0, rc), :] = lx_ref[pl.ds(r0, rc), :].astype(F32)
        wc, bc = wc_ref[...], bc_ref[...]
        wa_m, wx_m = wa_ref[...].reshape(HEAD_DIM, HEAD_DIM), wx_ref[...].reshape(HEAD_DIM, HEAD_DIM)
        ls = _log_sigmoid(lam_ref[...])
        for r0 in range(0, t, rc):
            rows = pl.ds(r0, rc)
            xl = _conv_causal(pad, wc, r0, rc, 4) + bc
            first = (lax.broadcasted_iota(jnp.int32, (rc, CB), 0) + r0) == 0
            xb, ra, ia, a, _, mult = _lru_gates(xl, wa_m, ba_ref[...], wx_m, bx_ref[...], ls, first)
            a_ref[rows, :] = a
            u_s[rows, :] = mult * (ia * xl)
            kept_ref[0, rows, :] = xb
            kept_ref[1, rows, :] = ra.astype(BF16)
            kept_ref[2, rows, :] = ia.astype(BF16)

        row = lax.broadcasted_iota(jnp.int32, (SUBLANES, CB), 0)

        def group(g, carry):
            r = pl.multiple_of(g * SUBLANES, SUBLANES)
            a_g, b_g = a_ref[pl.ds(r, SUBLANES), :], u_s[pl.ds(r, SUBLANES), :]
            for s in (1, 2, 4):
                keep = row >= s
                b_g = jnp.where(keep, a_g * pltpu.roll(b_g, s, 0) + b_g, b_g)
                a_g = jnp.where(keep, a_g * pltpu.roll(a_g, s, 0), a_g)
            h_g = b_g + a_g * carry
            hl_ref[pl.ds(r, SUBLANES), :] = h_g
            return jnp.broadcast_to(h_g[SUBLANES - 1:SUBLANES, :], (SUBLANES, CB))

        lax.fori_loop(0, t // SUBLANES, group, jnp.zeros((SUBLANES, CB), F32))
        for r0 in range(0, t, rc):
            rows = pl.ds(r0, rc)
            yb_ref[rows, :] = (hl_ref[rows, :] * _gelu(ly_ref[rows, :].astype(F32))).astype(BF16)

    blk = pl.BlockSpec((t, CB), lambda h: (0, h))
    res = jax.ShapeDtypeStruct((t, D_MODEL), F32)
    return pl.pallas_call(
        body, name="lru_fwd", grid=(N_HEADS,),
        out_shape=[jax.ShapeDtypeStruct((t, D_MODEL), BF16), res, res, jax.ShapeDtypeStruct((3, t, D_MODEL), BF16)],
        in_specs=[_section(3, t), _section(4, t), pl.BlockSpec((4, CB), lambda h: (0, h)), vec, mat, vec, mat, vec, vec],
        out_specs=[blk, blk, blk, pl.BlockSpec((3, t, CB), lambda h: (0, 0, h))],
        scratch_shapes=[pltpu.VMEM((t + PAD, CB), F32), pltpu.VMEM((t, CB), F32)],
        compiler_params=_params("parallel"),
    )(proj, proj, w_conv, b_conv, wa, ba, wx, bx, lam)


def _merge(y_a, y_b, proj, x, w_cb, w_lb, w_out, g2, g3):
    t = x.shape[0]
    tm = min(256, t)

    def body(ya_ref, yb_ref, gc_ref, gl_ref, x_ref, wcb_ref, wlb_ref, wo_ref, g2_ref, g3_ref,
             pa_ref, pb_ref, mg_ref, mix_ref, x1_ref, h2_ref):
        pa = _dot(ya_ref[...], wcb_ref[...]).astype(BF16)
        pb = _dot(yb_ref[...], wlb_ref[...]).astype(BF16)
        pa_ref[...] = pa
        pb_ref[...] = pb
        merged = (jax.nn.sigmoid(gc_ref[...].astype(F32)) * pa.astype(F32)
                  + jax.nn.sigmoid(gl_ref[...].astype(F32)) * pb.astype(F32)).astype(BF16)
        mg_ref[...] = merged
        mix = _dot(merged, wo_ref[...])
        mix_ref[...] = mix
        n2, _ = _rms_fwd(mix)
        x1 = x_ref[...] + n2 * g2_ref[...]
        x1_ref[...] = x1
        n3, _ = _rms_fwd(x1)
        h2_ref[...] = (n3 * g3_ref[...]).astype(BF16)

    row = pl.BlockSpec((tm, D_MODEL), lambda i: (i, 0))
    full = pl.BlockSpec((D_MODEL, D_MODEL), lambda i: (0, 0))
    vec = pl.BlockSpec((1, D_MODEL), lambda i: (0, 0))
    act = jax.ShapeDtypeStruct((t, D_MODEL), BF16)
    res = jax.ShapeDtypeStruct((t, D_MODEL), F32)
    return pl.pallas_call(
        body, name="merge_fwd", grid=(t // tm,), out_shape=[act, act, act, res, res, act],
        in_specs=[row, row, pl.BlockSpec((tm, D_MODEL), lambda i: (i, 5)), pl.BlockSpec((tm, D_MODEL), lambda i: (i, 6)),
                  row, full, full, full, vec, vec],
        out_specs=[row] * 6,
        compiler_params=_params("parallel"),
    )(y_a, y_b, proj, proj, x, w_cb, w_lb, w_out, g2, g3)


N_FF_BLOCKS = D_FF // CB


def _ffn_up(h2, w_up, w_conv, b_conv):
    t = h2.shape[0]
    rc = _row_chunk(t)
    nb = N_FF_BLOCKS

    def body(h_ref, w_ref, c_ref, b_ref, up_ref, act_ref, f_ref, pad, gate):
        k = pl.program_id(1)
        pad[pl.ds(0, PAD), :] = jnp.zeros((PAD, CB), F32)
        for r0 in range(0, t, rc):
            rows = pl.ds(r0, rc)
            up = _dot(h_ref[rows, :], w_ref[...]).astype(BF16)
            up_ref[rows, :] = up
            pad[pl.ds(PAD + r0, rc), :] = up.astype(F32)
        cw = c_ref[...]
        for r0 in range(0, t, rc):
            rows = pl.ds(r0, rc)
            act = _conv_causal(pad, cw, r0, rc, 3) + b_ref[...]
            act_ref[rows, :] = act.astype(BF16)

            @pl.when(k == 0)
            def _():
                gate[rows, :] = act

            @pl.when(k == 1)
            def _():
                f_ref[rows, :] = (_gelu(gate[rows, :]) * act).astype(BF16)

    half = lambda rows: pl.BlockSpec((rows, CB), lambda j, k: (0, nb * k + j))
    wide = jax.ShapeDtypeStruct((t, 2 * D_FF), BF16)
    return pl.pallas_call(
        body, name="ffn_up_fwd", grid=(nb, 2), out_shape=[wide, wide, jax.ShapeDtypeStruct((t, D_FF), BF16)],
        in_specs=[pl.BlockSpec((t, D_MODEL), lambda j, k: (0, 0)), half(D_MODEL), half(3), half(1)],
        out_specs=[half(t), half(t), pl.BlockSpec((t, CB), lambda j, k: (0, j))],
        scratch_shapes=[pltpu.VMEM((t + PAD, CB), F32), pltpu.VMEM((t, CB), F32)],
        compiler_params=_params("parallel", "arbitrary"),
    )(h2, w_up, w_conv, b_conv)


def _ffn_down(f, act, w_down, x1, target, g4):
    t = f.shape[0]
    tm = min(256, t)
    cc = 512

    def body(f_ref, act_ref, w_ref, x1_ref, tg_ref, g_ref, dy_ref, dout_ref, back_ref, dg_ref, loss_ref):
        @pl.when(pl.program_id(0) == 0)
        def _():
            dg_ref[...] = jnp.zeros_like(dg_ref)
            loss_ref[...] = jnp.zeros_like(loss_ref)
        out = _dot(f_ref[...], w_ref[...])
        n4, r4 = _rms_fwd(out)
        err = x1_ref[...] + n4 * g_ref[...] - tg_ref[...]
        loss_ref[...] += jnp.full(loss_ref.shape, 0.5 / D_MODEL, F32) * jnp.sum(err * err)
        dy = err * (1.0 / D_MODEL)
        dy_ref[...] = dy
        dg_ref[...] += jnp.sum(dy * n4, axis=0, keepdims=True)
        d_out = _rms_bwd(n4, r4, dy * g_ref[...]).astype(BF16)
        dout_ref[...] = d_out
        for c0 in range(0, D_FF, cc):
            d_f = _dot_nt(d_out, w_ref[pl.ds(c0, cc), :])
            gelu, d_gelu = _gelu_and_grad(act_ref[:, pl.ds(c0, cc)].astype(F32))
            val = act_ref[:, pl.ds(D_FF + c0, cc)].astype(F32)
            back_ref[:, pl.ds(c0, cc)] = (d_f * val * d_gelu).astype(BF16)
            back_ref[:, pl.ds(D_FF + c0, cc)] = (d_f * gelu).astype(BF16)

    row = pl.BlockSpec((tm, D_MODEL), lambda i: (i, 0))
    wide = pl.BlockSpec((tm, 2 * D_FF), lambda i: (i, 0))
    vec = pl.BlockSpec((1, D_MODEL), lambda i: (0, 0))
    return pl.pallas_call(
        body, name="ffn_down_fwd_bwd", grid=(t // tm,),
        out_shape=[jax.ShapeDtypeStruct((t, D_MODEL), F32), jax.ShapeDtypeStruct((t, D_MODEL), BF16),
                   jax.ShapeDtypeStruct((t, 2 * D_FF), BF16), jax.ShapeDtypeStruct((1, D_MODEL), F32),
                   jax.ShapeDtypeStruct((SUBLANES, LANES), F32)],
        in_specs=[pl.BlockSpec((tm, D_FF), lambda i: (i, 0)), wide, pl.BlockSpec((D_FF, D_MODEL), lambda i: (0, 0)),
                  row, row, vec],
        out_specs=[row, row, wide, vec, pl.BlockSpec((SUBLANES, LANES), lambda i: (0, 0))],
        compiler_params=_params("arbitrary"),
    )(f, act, w_down, x1, target, g4)


def _grad_tn(a, b, bm, name):
    t, m = a.shape
    n = b.shape[1]

    def body(a_ref, b_ref, o_ref):
        o_ref[...] = _dot_tn(a_ref[...], b_ref[...]).astype(BF16)

    return pl.pallas_call(
        body, name=name, grid=(m // bm,), out_shape=jax.ShapeDtypeStruct((m, n), BF16),
        in_specs=[pl.BlockSpec((t, bm), lambda i: (0, i)), pl.BlockSpec((t, n), lambda i: (0, 0))],
        out_specs=pl.BlockSpec((bm, n), lambda i: (i, 0)),
        compiler_params=_params("parallel"),
    )(a, b)


def _ffn_up_bwd(up, back, w_conv, h2, w_up):
    t = h2.shape[0]
    rc = _row_chunk(t)
    nb = N_FF_BLOCKS

    def body(up_ref, back_ref, c_ref, h_ref, w_ref, dw_ref, dcw_ref, dcb_ref, dh_ref, after, d_up):
        @pl.when((pl.program_id(0) == 0) & (pl.program_id(1) == 0))
        def _():
            dh_ref[...] = jnp.zeros_like(dh_ref)
        after[pl.ds(t, PAD), :] = jnp.zeros((PAD, CB), F32)
        for r0 in range(0, t, rc):
            after[pl.ds(r0, rc), :] = back_ref[pl.ds(r0, rc), :].astype(F32)
        cw = c_ref[...]
        taps = [jnp.zeros((1, CB), F32)] * 3
        bias = jnp.zeros((1, CB), F32)
        for r0 in range(0, t, rc):
            rows = pl.ds(r0, rc)
            d, shares, g = _conv_bwd(after, cw, up_ref[rows, :].astype(F32), r0, rc, 3)
            d = d.astype(BF16)
            d_up[rows, :] = d
            dh_ref[rows, :] += _dot_nt(d, w_ref[...])
            taps = [acc + new for acc, new in zip(taps, shares)]
            bias = bias + jnp.sum(g, axis=0, keepdims=True)
        dw_ref[...] = _dot_tn(h_ref[...], d_up[...]).astype(BF16)
        dcw_ref[...] = jnp.concatenate(taps, axis=0)
        dcb_ref[...] = bias

    half = lambda rows: pl.BlockSpec((rows, CB), lambda j, k: (0, nb * k + j))
    whole = pl.BlockSpec((t, D_MODEL), lambda j, k: (0, 0))
    return pl.pallas_call(
        body, name="ffn_up_bwd", grid=(nb, 2),
        out_shape=[jax.ShapeDtypeStruct((D_MODEL, 2 * D_FF), BF16), jax.ShapeDtypeStruct((3, 2 * D_FF), F32),
                   jax.ShapeDtypeStruct((1, 2 * D_FF), F32), jax.ShapeDtypeStruct((t, D_MODEL), F32)],
        in_specs=[half(t), half(t), half(3), whole, half(D_MODEL)],
        out_specs=[half(D_MODEL), half(3), half(1), whole],
        scratch_shapes=[pltpu.VMEM((t + PAD, CB), F32), pltpu.VMEM((t, CB), BF16)],
        compiler_params=_params("arbitrary", "arbitrary"),
    )(up, back, w_conv, h2, w_up)


def _merge_bwd(dy, d_h2, x1, mix, g3, g2, w_out, w_cb, w_lb, pa, pb, proj):
    t = dy.shape[0]
    tm = min(256, t)

    def body(dy_ref, dh2_ref, x1_ref, mix_ref, g3_ref, g2_ref, wo_ref, wcb_ref, wlb_ref, pa_ref, pb_ref, gc_ref, gl_ref,
             dx1_ref, dmix_ref, dpa_ref, dpb_ref, dya_ref, dyb_ref, dgate_ref, dg3_ref, dg2_ref):
        @pl.when(pl.program_id(0) == 0)
        def _():
            dg3_ref[...] = jnp.zeros_like(dg3_ref)
            dg2_ref[...] = jnp.zeros_like(dg2_ref)
        n3, r3 = _rms_fwd(x1_ref[...])
        d_h2 = dh2_ref[...]
        dg3_ref[...] += jnp.sum(d_h2 * n3, axis=0, keepdims=True)
        dx1 = dy_ref[...] + _rms_bwd(n3, r3, d_h2 * g3_ref[...])
        dx1_ref[...] = dx1
        n2, r2 = _rms_fwd(mix_ref[...])
        dg2_ref[...] += jnp.sum(dx1 * n2, axis=0, keepdims=True)
        d_mix = _rms_bwd(n2, r2, dx1 * g2_ref[...]).astype(BF16)
        dmix_ref[...] = d_mix
        d_merged = _dot_nt(d_mix, wo_ref[...])
        sc = jax.nn.sigmoid(gc_ref[...].astype(F32))
        sl = jax.nn.sigmoid(gl_ref[...].astype(F32))
        d_pa = (d_merged * sc).astype(BF16)
        d_pb = (d_merged * sl).astype(BF16)
        dpa_ref[...] = d_pa
        dpb_ref[...] = d_pb
        dgate_ref[0] = (d_merged * pa_ref[...].astype(F32) * sc * (1.0 - sc)).astype(BF16)
        dgate_ref[1] = (d_merged * pb_ref[...].astype(F32) * sl * (1.0 - sl)).astype(BF16)
        dya_ref[...] = _dot_nt(d_pa, wcb_ref[...]).astype(BF16)
        dyb_ref[...] = _dot_nt(d_pb, wlb_ref[...]).astype(BF16)

    row = pl.BlockSpec((tm, D_MODEL), lambda i: (i, 0))
    full = pl.BlockSpec((D_MODEL, D_MODEL), lambda i: (0, 0))
    vec = pl.BlockSpec((1, D_MODEL), lambda i: (0, 0))
    act = jax.ShapeDtypeStruct((t, D_MODEL), BF16)
    small = jax.ShapeDtypeStruct((1, D_MODEL), F32)
    return pl.pallas_call(
        body, name="merge_bwd", grid=(t // tm,),
        out_shape=[jax.ShapeDtypeStruct((t, D_MODEL), F32), act, act, act, act, act,
                   jax.ShapeDtypeStruct((2, t, D_MODEL), BF16), small, small],
        in_specs=[row, row, row, row, vec, vec, full, full, full, row, row,
                  pl.BlockSpec((tm, D_MODEL), lambda i: (i, 5)), pl.BlockSpec((tm, D_MODEL), lambda i: (i, 6))],
        out_specs=[row] * 6 + [pl.BlockSpec((2, tm, D_MODEL), lambda i: (0, i, 0)), vec, vec],
        compiler_params=_params("arbitrary"),
    )(dy, d_h2, x1, mix, g3, g2, w_out, w_cb, w_lb, pa, pb, proj, proj)


def _conv_mixer_bwd(proj, d_ya, w_short):
    t = proj.shape[0]
    rc = _row_chunk(t)

    def body(b_ref, c_ref, x_ref, dy_ref, w_ref, d_ref, dw_ref, pad, back):
        pad[pl.ds(0, PAD), :] = jnp.zeros((PAD, CB), F32)
        back[pl.ds(t, PAD), :] = jnp.zeros((PAD, CB), F32)
        for r0 in range(0, t, rc):
            rows = pl.ds(r0, rc)
            pad[pl.ds(PAD + r0, rc), :] = c_ref[rows, :].astype(F32) * x_ref[rows, :].astype(F32)
        w = w_ref[...]
        for r0 in range(0, t, rc):
            rows = pl.ds(r0, rc)
            d_y = dy_ref[rows, :].astype(F32)
            d_ref[0, rows, :] = (d_y * _conv_causal(pad, w, r0, rc, 3)).astype(BF16)
            back[rows, :] = d_y * b_ref[rows, :].astype(F32)
        taps = [jnp.zeros((1, CB), F32)] * 3
        for r0 in range(0, t, rc):
            rows = pl.ds(r0, rc)
            d_u, shares, _ = _conv_bwd(back, w, pad[pl.ds(PAD + r0, rc), :], r0, rc, 3)
            d_ref[1, rows, :] = (d_u * x_ref[rows, :].astype(F32)).astype(BF16)
            d_ref[2, rows, :] = (d_u * c_ref[rows, :].astype(F32)).astype(BF16)
            taps = [acc + new for acc, new in zip(taps, shares)]
        dw_ref[...] = jnp.concatenate(taps, axis=0)

    blk = pl.BlockSpec((t, CB), lambda h: (0, h))
    return pl.pallas_call(
        body, name="conv_mixer_bwd", grid=(D_MODEL // CB,),
        out_shape=[jax.ShapeDtypeStruct((3, t, D_MODEL), BF16), jax.ShapeDtypeStruct((3, D_MODEL), F32)],
        in_specs=[_section(0, t), _section(1, t), _section(2, t), blk, pl.BlockSpec((3, CB), lambda h: (0, h))],
        out_specs=[pl.BlockSpec((3, t, CB), lambda h: (0, 0, h)), pl.BlockSpec((3, CB), lambda h: (0, h))],
        scratch_shapes=[pltpu.VMEM((t + PAD, CB), F32), pltpu.VMEM((t + PAD, CB), F32)],
        compiler_params=_params("parallel"),
    )(proj, proj, proj, d_ya, w_short)


LRU_SMALL_ROWS = 8


def _lru_bwd(proj, hl, a_all, kept, d_yb, w_conv, wa, wx, lam):
    t = proj.shape[0]
    rc = _row_chunk(t)
    vec, mat = _head_specs()

    def body(lx_ref, ly_ref, hl_ref, a_ref, kept_ref, dy_ref, wc_ref, wa_ref, wx_ref, lam_ref,
             d_ref, dwa_ref, dwx_ref, small_ref, a_next, dh_s, h_prev, back, acc_a, acc_x):
        zeros = jnp.zeros((PAD, CB), F32)
        h_prev[pl.ds(0, PAD), :] = zeros
        a_next[pl.ds(t, PAD), :] = zeros
        back[pl.ds(t, PAD), :] = zeros
        for r0 in range(0, t, rc):
            rows = pl.ds(r0, rc)
            h_prev[pl.ds(PAD + r0, rc), :] = hl_ref[rows, :]
            a_next[pl.ds(PAD - 1 + r0, rc), :] = a_ref[rows, :]
            act, d_act = _gelu_and_grad(ly_ref[rows, :].astype(F32))
            d_y = dy_ref[rows, :].astype(F32)
            dh_s[rows, :] = d_y * act
            d_ref[1, rows, :] = (d_y * hl_ref[rows, :] * d_act).astype(BF16)
        wc = wc_ref[...]
        wa_m, wx_m = wa_ref[...].reshape(HEAD_DIM, HEAD_DIM), wx_ref[...].reshape(HEAD_DIM, HEAD_DIM)
        ls = _log_sigmoid(lam_ref[...])

        row = lax.broadcasted_iota(jnp.int32, (SUBLANES, CB), 0)
        groups = t // SUBLANES

        def group(i, carry):
            r = pl.multiple_of((groups - 1 - i) * SUBLANES, SUBLANES)
            a_g, b_g = a_next[pl.ds(PAD + r, SUBLANES), :], dh_s[pl.ds(r, SUBLANES), :]
            for s in (1, 2, 4):
                keep = row < SUBLANES - s
                b_g = jnp.where(keep, a_g * pltpu.roll(b_g, SUBLANES - s, 0) + b_g, b_g)
                a_g = jnp.where(keep, a_g * pltpu.roll(a_g, SUBLANES - s, 0), a_g)
            d_g = b_g + a_g * carry
            dh_s[pl.ds(r, SUBLANES), :] = d_g
            return jnp.broadcast_to(d_g[0:1, :], (SUBLANES, CB))

        lax.fori_loop(0, groups, group, jnp.zeros((SUBLANES, CB), F32))

        acc_a[...] = jnp.zeros_like(acc_a)
        acc_x[...] = jnp.zeros_like(acc_x)
        d_ba = d_bx = d_ls = jnp.zeros((1, CB), F32)
        for r0 in range(0, t, rc):
            rows = pl.ds(r0, rc)
            first = (lax.broadcasted_iota(jnp.int32, (rc, CB), 0) + r0) == 0
            xb, a = kept_ref[0, rows, :], a_ref[rows, :]
            xl, ra, ia = xb.astype(F32), kept_ref[1, rows, :].astype(F32), kept_ref[2, rows, :].astype(F32)
            a_sq = a * a
            mult = jnp.where(first, 1.0, jnp.sqrt(1.0 - a_sq))
            d_h = dh_s[rows, :]
            d_a = d_h * h_prev[pl.ds(PAD - 1 + r0, rc), :]
            d_mult = d_h * ia * xl
            d_ia = d_h * mult * xl
            d_xl = d_h * mult * ia
            d_la = d_a * a + d_mult * jnp.where(first, 0.0, -a_sq / mult)
            d_ls = d_ls + jnp.sum(d_la * ra, axis=0, keepdims=True) * LRU_C
            d_za = d_la * (LRU_C * ls) * ra * (1.0 - ra)
            d_zx = d_ia * ia * (1.0 - ia)
            d_ba = d_ba + jnp.sum(d_za, axis=0, keepdims=True)
            d_bx = d_bx + jnp.sum(d_zx, axis=0, keepdims=True)
            d_za, d_zx = d_za.astype(BF16), d_zx.astype(BF16)
            acc_a[...] += _dot_tn(xb, d_za)
            acc_x[...] += _dot_tn(xb, d_zx)
            back[rows, :] = d_xl + _dot_nt(d_za, wa_m) + _dot_nt(d_zx, wx_m)
        taps = [jnp.zeros((1, CB), F32)] * 4
        d_bc = jnp.zeros((1, CB), F32)
        for r0 in range(0, t, rc):
            rows = pl.ds(r0, rc)
            d_lx, shares, g = _conv_bwd(back, wc, lx_ref[rows, :].astype(F32), r0, rc, 4)
            d_ref[0, rows, :] = d_lx.astype(BF16)
            taps = [acc + new for acc, new in zip(taps, shares)]
            d_bc = d_bc + jnp.sum(g, axis=0, keepdims=True)
        d_lam = d_ls * jax.nn.sigmoid(-lam_ref[...])
        small_ref[...] = jnp.concatenate(taps + [d_bc, d_ba, d_bx, d_lam], axis=0)
        dwa_ref[...] = acc_a[...].reshape(N_DEV, HEAD_DIM // N_DEV, HEAD_DIM).astype(BF16)
        dwx_ref[...] = acc_x[...].reshape(N_DEV, HEAD_DIM // N_DEV, HEAD_DIM).astype(BF16)

    blk = pl.BlockSpec((t, CB), lambda h: (0, h))
    gate_grad = jax.ShapeDtypeStruct((N_DEV, N_HEADS, HEAD_DIM // N_DEV, HEAD_DIM), BF16)
    return pl.pallas_call(
        body, name="lru_bwd", grid=(N_HEADS,),
        out_shape=[jax.ShapeDtypeStruct((2, t, D_MODEL), BF16), gate_grad, gate_grad,
                   jax.ShapeDtypeStruct((LRU_SMALL_ROWS, D_MODEL), F32)],
        in_specs=[_section(3, t), _section(4, t), blk, blk, pl.BlockSpec((3, t, CB), lambda h: (0, 0, h)), blk,
                  pl.BlockSpec((4, CB), lambda h: (0, h)), mat, mat, vec],
        out_specs=[pl.BlockSpec((2, t, CB), lambda h: (0, 0, h)), mat, mat,
                   pl.BlockSpec((LRU_SMALL_ROWS, CB), lambda h: (0, h))],
        scratch_shapes=[pltpu.VMEM((t + PAD, CB), F32), pltpu.VMEM((t, CB), F32),
                        pltpu.VMEM((t + PAD, CB), F32), pltpu.VMEM((t + PAD, CB), F32),
                        pltpu.VMEM((HEAD_DIM, HEAD_DIM), F32), pltpu.VMEM((HEAD_DIM, HEAD_DIM), F32)],
        compiler_params=_params("parallel"),
    )(proj, proj, hl, a_all, kept, d_yb, w_conv, wa, wx, lam)


def _stack_maps(halves):
    def conv(sec, part):
        return jnp.minimum(sec, 2), jnp.where(sec < 3, part, halves - 1)

    def lru(sec, part):
        return jnp.clip(sec - 3, 0, 1), jnp.where(sec < 3, 0, jnp.where(sec < 5, part, halves - 1))

    def gate(sec, part):
        return jnp.clip(sec - 5, 0, 1), jnp.where(sec < 5, 0, part)

    return conv, lru, gate


def _pick_stack(sec, refs, fn):
    @pl.when(sec < 3)
    def _():
        fn(refs[0])

    @pl.when((sec >= 3) & (sec < 5))
    def _():
        fn(refs[1])

    @pl.when(sec >= 5)
    def _():
        fn(refs[2])


def _in_proj_wgrad(h, d_conv, d_lru, d_gate):
    t = h.shape[0]
    halves, bn = 2, D_MODEL // 2
    maps = _stack_maps(halves)

    def body(h_ref, dc_ref, dl_ref, dg_ref, o_ref):
        def emit(ref):
            o_ref[...] = _dot_tn(h_ref[...], ref[...]).astype(BF16)
        _pick_stack(pl.program_id(0) // halves, (dc_ref, dl_ref, dg_ref), emit)

    def spec(m):
        def index(s):
            stack, part = m(s // halves, s % halves)
            return stack, 0, part
        return pl.BlockSpec((None, t, bn), index)

    return pl.pallas_call(
        body, name="in_proj_wgrad", grid=(7 * halves,), out_shape=jax.ShapeDtypeStruct((D_MODEL, IN_COLS), BF16),
        in_specs=[pl.BlockSpec((t, D_MODEL), lambda s: (0, 0))] + [spec(m) for m in maps],
        out_specs=pl.BlockSpec((D_MODEL, bn), lambda s: (0, s)),
        compiler_params=_params("arbitrary"),
    )(h, d_conv, d_lru, d_gate)


def _in_proj_xgrad(d_conv, d_lru, d_gate, w_in, x, dx1, g1):
    t = x.shape[0]
    tm = min(1024, t)
    maps = _stack_maps(1)

    def body(dc_ref, dl_ref, dg_ref, w_ref, x_ref, dx1_ref, g_ref, dx_ref, dgain_ref, acc):
        i, s = pl.program_id(0), pl.program_id(1)

        @pl.when((i == 0) & (s == 0))
        def _():
            dgain_ref[...] = jnp.zeros_like(dgain_ref)

        @pl.when(s == 0)
        def _():
            acc[...] = jnp.zeros_like(acc)

        def add(ref):
            acc[...] += _dot_nt(ref[...], w_ref[...])
        _pick_stack(s, (dc_ref, dl_ref, dg_ref), add)

        @pl.when(s == 6)
        def _():
            n1, r1 = _rms_fwd(x_ref[...])
            d_h = acc[...]
            dgain_ref[...] += jnp.sum(d_h * n1, axis=0, keepdims=True)
            dx_ref[...] = dx1_ref[...] + _rms_bwd(n1, r1, d_h * g_ref[...])

    def spec(m):
        def index(i, s):
            return m(s, 0)[0], i, 0
        return pl.BlockSpec((None, tm, D_MODEL), index)

    row = pl.BlockSpec((tm, D_MODEL), lambda i, s: (i, 0))
    vec = pl.BlockSpec((1, D_MODEL), lambda i, s: (0, 0))
    return pl.pallas_call(
        body, name="in_proj_xgrad", grid=(t // tm, 7),
        out_shape=[jax.ShapeDtypeStruct((t, D_MODEL), F32), jax.ShapeDtypeStruct((1, D_MODEL), F32)],
        in_specs=[spec(m) for m in maps] + [pl.BlockSpec((D_MODEL, D_MODEL), lambda i, s: (0, s)), row, row, vec],
        out_specs=[row, vec],
        scratch_shapes=[pltpu.VMEM((tm, D_MODEL), F32)],
        compiler_params=_params("arbitrary", "arbitrary"),
    )(d_conv, d_lru, d_gate, w_in, x, dx1, g1)


def _add_pair(grad, got, by_cols, pos, name):
    cols = got.shape[-1]
    got3 = got.reshape(4, -1, cols)
    rows = got3.shape[1]
    rb = _row_block(rows, 1024)

    def block(k, p):
        return 4 * ((p[0] + k % 2) % 2) + 2 * ((p[1] + k // 2) % 2) + p[2]

    if by_cols:
        g_in, g_spec = grad, pl.BlockSpec((rb, cols), lambda k, i, p: (i, block(k, p)))
    else:
        g_in = grad.reshape(N_DEV, rows, cols)
        g_spec = pl.BlockSpec((None, rb, cols), lambda k, i, p: (block(k, p), i, 0))
    slot = pl.BlockSpec((None, rb, cols), lambda k, i, p: (k, i, 0))

    def body(pos_ref, a_ref, b_ref, o_ref):
        o_ref[...] = (a_ref[...].astype(F32) + b_ref[...].astype(F32)).astype(BF16)

    out = pl.pallas_call(
        body, name=name, out_shape=jax.ShapeDtypeStruct(got3.shape, BF16),
        grid_spec=pltpu.PrefetchScalarGridSpec(num_scalar_prefetch=1, grid=(4, rows // rb),
                                               in_specs=[g_spec, slot], out_specs=slot),
        compiler_params=_params("parallel", "parallel"),
    )(pos, g_in, got3)
    return out.reshape(got.shape)


def _adamw(w, g, m, v):
    m = ADAM_B1 * m + (1.0 - ADAM_B1) * g
    v = ADAM_B2 * v + (1.0 - ADAM_B2) * (g * g)
    m_hat = m / (1.0 - ADAM_B1 ** ADAM_STEP)
    v_hat = v / (1.0 - ADAM_B2 ** ADAM_STEP)
    return -ADAM_LR * (m_hat / (jnp.sqrt(v_hat) + ADAM_EPS) + ADAM_WD * w), m, v


def _adam_large(w, m, v, own, others, name):
    shape = w.shape
    cols = shape[-1]
    w2, m2, v2 = (a.reshape(-1, cols) for a in (w, m, v))
    rows = w2.shape[0]
    own, others = own.reshape(4, rows, cols), others.reshape(3, rows, cols)
    rb = _row_block(rows, 512)

    def body(w_ref, m_ref, v_ref, own_ref, oth_ref, g_ref, d_ref, nm_ref, nv_ref):
        g = own_ref[...].astype(F32)
        for k in range(3):
            g = g + oth_ref[k].astype(F32)
        g_ref[...] = g
        d_ref[...], nm_ref[...], nv_ref[...] = _adamw(w_ref[...], g, m_ref[...], v_ref[...])

    blk = pl.BlockSpec((rb, cols), lambda i: (i, 0))
    res = jax.ShapeDtypeStruct((rows, cols), F32)
    outs = pl.pallas_call(
        body, name=name, grid=(rows // rb,), out_shape=[res] * 4,
        in_specs=[blk, blk, blk, pl.BlockSpec((None, rb, cols), lambda i: (0, i, 0)),
                  pl.BlockSpec((3, rb, cols), lambda i: (0, i, 0))],
        out_specs=[blk] * 4, compiler_params=_params("parallel"),
    )(w2, m2, v2, own, others)
    return [o.reshape(shape) for o in outs]


def _adam_small(ws, gs, ms, vs):
    n = len(ws)

    def body(*refs):
        w_refs, g_refs, m_refs, v_refs = (refs[i * n:(i + 1) * n] for i in range(4))
        outs = refs[4 * n:]
        for i in range(n):
            d, m, v = _adamw(w_refs[i][...], g_refs[i][...], m_refs[i][...], v_refs[i][...])
            outs[i][...], outs[n + i][...], outs[2 * n + i][...] = d, m, v

    shapes = [jax.ShapeDtypeStruct(w.shape, F32) for w in ws]
    outs = pl.pallas_call(
        body, name="adam_small", out_shape=shapes * 3,
        in_specs=[VMEM_SPEC] * (4 * n), out_specs=[VMEM_SPEC] * (3 * n), compiler_params=_params(),
    )(*ws, *gs, *ms, *vs)
    return outs[:n], outs[n:2 * n], outs[2 * n:]


def _pack_rows(pieces):
    tile = SUBLANES * LANES
    return jnp.concatenate([jnp.pad(p.reshape(-1), (0, (-p.size) % tile)).reshape(-1, LANES) for p in pieces], axis=0)


def _packed_starts(sizes):
    tile = SUBLANES * LANES
    starts = [0]
    for s in sizes:
        starts.append(starts[-1] + (s + tile - 1) // tile * SUBLANES)
    return starts


def kernel(x, norm_mix_pre, norm_mix_post, norm_ffn_pre, norm_ffn_post, w_in, conv_short_w, w_conv_branch, lru_conv_w, lru_conv_b, lru_wa, lru_ba, lru_wx, lru_bx, lru_lambda, w_lru_branch, w_out, ffn_w_up, ffn_conv_w, ffn_conv_b, ffn_w_down, loss_target, m_norm_mix_pre, m_norm_mix_post, m_norm_ffn_pre, m_norm_ffn_post, m_w_in, m_conv_short_w, m_w_conv_branch, m_lru_conv_w, m_lru_conv_b, m_lru_wa, m_lru_ba, m_lru_wx, m_lru_bx, m_lru_lambda, m_w_lru_branch, m_w_out, m_ffn_w_up, m_ffn_conv_w, m_ffn_conv_b, m_ffn_w_down, v_norm_mix_pre, v_norm_mix_post, v_norm_ffn_pre, v_norm_ffn_post, v_w_in, v_conv_short_w, v_w_conv_branch, v_lru_conv_w, v_lru_conv_b, v_lru_wa, v_lru_ba, v_lru_wx, v_lru_bx, v_lru_lambda, v_w_lru_branch, v_w_out, v_ffn_w_up, v_ffn_conv_w, v_ffn_conv_b, v_ffn_w_down):
    t = x.shape[1]
    xi, yi, ci = _position()
    me = _block_of(xi, yi, ci)
    x2, target = x[0], loss_target[0]
    shard_in, shard_up = IN_COLS // N_DEV, 2 * D_FF // N_DEV
    shard_sq, shard_down, shard_head = D_MODEL // N_DEV, D_FF // N_DEV, HEAD_DIM // N_DEV

    names = ["w_in", "lru_wa", "lru_wx", "w_conv_branch", "w_lru_branch", "w_out", "ffn_w_up", "ffn_w_down"]
    large = [w_in[0], lru_wa[0], lru_wx[0], w_conv_branch[0], w_lru_branch[0], w_out[0], ffn_w_up[0], ffn_w_down[0]]
    blocks = [_cols(shard_in), _lead, _lead, _rows(shard_sq), _rows(shard_sq), _rows(shard_sq),
              _cols(shard_up), _rows(shard_down)]
    gate_full = (N_DEV, N_HEADS, shard_head, HEAD_DIM)
    full_shapes = [(D_MODEL, IN_COLS), gate_full, gate_full, (D_MODEL, D_MODEL), (D_MODEL, D_MODEL), (D_MODEL, D_MODEL),
                   (D_MODEL, 2 * D_FF), (D_FF, D_MODEL)]
    n_now = 3
    small_sharded = [conv_short_w, lru_conv_w, lru_ba, lru_bx, ffn_conv_w]
    small_mine = _pack_rows(small_sharded)
    small_at = _packed_starts([p.size for p in small_sharded])
    *gathered, small_all = _gather_weights(large, blocks, full_shapes, small_mine, n_now)
    g_in, g_wa, g_wx = gathered[:n_now]
    later_blocks = blocks[n_now:]
    send1, recv1, later, gather_token = _gather_start(gathered[n_now:], later_blocks, "gather_start")

    def behind(token, operand):
        return operand + token[0:1, 0:1]

    def forward(lo, hi, after, tag):
        return _gather_forward(later[lo:hi], later_blocks[lo:hi], send1[4 * lo:4 * hi], recv1[4 * lo:4 * hi], after,
                               "gather_forward_" + tag)

    def finish(lo, hi, flight, after, tag):
        return _gather_finish(flight[2], later_blocks[lo:hi], flight[0], flight[1], after, "gather_finish_" + tag)

    def cols_of(r0, n, width):
        part = small_all[:, r0:r0 + n * width // LANES, :].reshape(N_DEV, n, width)
        return part.transpose(1, 0, 2).reshape(n, N_DEV * width)

    c_short = cols_of(small_at[0], 3, LANES)
    c_lru = cols_of(small_at[1], 4, LANES)
    b_a = cols_of(small_at[2], N_HEADS, shard_head).reshape(1, D_MODEL)
    b_x = cols_of(small_at[3], N_HEADS, shard_head).reshape(1, D_MODEL)
    c_ffn = cols_of(small_at[4], 3, shard_up)

    proj, h = _in_proj(x2, behind(gather_token, norm_mix_pre), g_in)
    flight_mix_w = forward(0, 3, h, "mix")
    y_a = _conv_mixer_fwd(proj, c_short)
    y_b, hl, decay, lru_kept = _lru_fwd(proj, c_lru, lru_conv_b, g_wa, b_a, g_wx, b_x, lru_lambda)
    flight_up_w = forward(3, 4, y_b, "up")
    g_cb, g_lb, g_out = finish(0, 3, flight_mix_w, y_b, "mix")
    pa, pb, merged, mix, x1, h2 = _merge(y_a, y_b, proj, x2, g_cb, g_lb, g_out, norm_mix_post, norm_ffn_pre)
    flight_down_w = forward(4, 5, h2, "down")
    (g_up,) = finish(3, 4, flight_up_w, h2, "up")
    up, act, f = _ffn_up(h2, g_up, c_ffn, ffn_conv_b)
    (g_down,) = finish(4, 5, flight_down_w, f, "down")
    dy, d_out, d_act, dg4, loss_part = _ffn_down(f, act, g_down, x1, target, norm_ffn_post)

    block_of = dict(zip(names, blocks))
    shard_shapes = {"w_in": (D_MODEL, shard_in), "w_conv_branch": (shard_sq, D_MODEL), "w_lru_branch": (shard_sq, D_MODEL),
                    "w_out": (shard_sq, D_MODEL), "lru_wa": (N_HEADS, shard_head, HEAD_DIM),
                    "lru_wx": (N_HEADS, shard_head, HEAD_DIM), "ffn_w_up": (D_MODEL, shard_up),
                    "ffn_w_down": (shard_down, D_MODEL)}
    pos = jnp.stack([xi, yi, ci]).astype(jnp.int32)

    def reduce_start(tag, grads):
        keys = list(grads)
        got = _exchange_pair([grads[k] for k in keys], [block_of[k] for k in keys], [shard_shapes[k] for k in keys],
                             "reduce_pair_exchange_" + tag)
        sums = [_add_pair(grads[k], g, k in ("w_in", "ffn_w_up"), pos, "pair_sum_" + k) for k, g in zip(keys, got)]
        return (keys,) + _exchange_chips_start(sums, "reduce_chip_start_" + tag)

    gw_down = _grad_tn(f, d_out, min(512, D_FF), "ffn_down_wgrad")
    flight_down = reduce_start("down", {"ffn_w_down": gw_down})
    gw_up, gc_ffn, gb_ffn, d_h2 = _ffn_up_bwd(up, d_act, behind(flight_down[-1], c_ffn), h2, g_up)
    flight_up = reduce_start("up", {"ffn_w_up": gw_up})
    dx1, d_mix, d_pa, d_pb, d_ya, d_yb, d_gate, dg3, dg2 = _merge_bwd(
        dy, d_h2, x1, mix, behind(flight_up[-1], norm_ffn_pre), norm_mix_post, g_out, g_cb, g_lb, pa, pb, proj)
    gw_out = _grad_tn(merged, d_mix, CB, "w_out_wgrad")
    gw_cb = _grad_tn(y_a, d_pa, CB, "w_conv_branch_wgrad")
    gw_lb = _grad_tn(y_b, d_pb, CB, "w_lru_branch_wgrad")
    flight_mix = reduce_start("mix", {"w_conv_branch": gw_cb, "w_lru_branch": gw_lb, "w_out": gw_out})
    d_conv, gc_short = _conv_mixer_bwd(proj, d_ya, behind(flight_mix[-1], c_short))
    d_lru, gw_a, gw_x, g_lru_small = _lru_bwd(proj, hl, decay, lru_kept, d_yb, c_lru, g_wa, g_wx, lru_lambda)
    early = [dg2, dg3, dg4, g_lru_small[4:5], g_lru_small[7:8], gb_ffn, gc_short, g_lru_small[0:4],
             g_lru_small[5:6], g_lru_small[6:7], gc_ffn, loss_part]
    flight_small = _small_start(_pack_rows(early), "small_start")
    gw_in = _in_proj_wgrad(h, d_conv, d_lru, d_gate)
    flight_in = reduce_start("in", {"lru_wa": gw_a, "lru_wx": gw_x, "w_in": gw_in})
    dx, dg1 = _in_proj_xgrad(d_conv, d_lru, d_gate, g_in, x2, dx1,
                             behind(flight_small[-1], behind(flight_in[-1], norm_mix_pre)))
    flight_late = _small_start(_pack_rows([dg1]), "small_start_late")

    moments ={"w_in": (m_w_in, v_w_in), "w_conv_branch": (m_w_conv_branch, v_w_conv_branch),
               "w_lru_branch": (m_w_lru_branch, v_w_lru_branch), "w_out": (m_w_out, v_w_out),
               "lru_wa": (m_lru_wa, v_lru_wa), "lru_wx": (m_lru_wx, v_lru_wx), "ffn_w_up": (m_ffn_w_up, v_ffn_w_up),
               "ffn_w_down": (m_ffn_w_down, v_ffn_w_down)}
    weights = {"w_in": w_in, "w_conv_branch": w_conv_branch, "w_lru_branch": w_lru_branch, "w_out": w_out,
               "lru_wa": lru_wa, "lru_wx": lru_wx, "ffn_w_up": ffn_w_up, "ffn_w_down": ffn_w_down}
    out_g, out_d, out_m, out_v = {}, {}, {}, {}

    after = flight_late[-1]
    for tag, (keys, send, recv, sums, lands, _) in (("down", flight_down), ("up", flight_up), ("mix", flight_mix),
                                                    ("in", flight_in)):
        sums, others = _exchange_chips_wait(send, recv, sums, lands, after, "reduce_chip_wait_" + tag)
        for k, own, oth in zip(keys, sums, others):
            out_g[k], out_d[k], out_m[k], out_v[k] = _adam_large(weights[k], *moments[k], own, oth, "adam_" + k)
        after = out_d[keys[-1]]

    total, total_late = _small_sum([_small_wait(*flight_small[:4], after, "small_wait"),
                                    _small_wait(*flight_late[:4], after, "small_wait_late")], me)
    sizes = [p.size for p in early]
    starts = _packed_starts(sizes)

    def piece(i, shape):
        if i == 0:
            return total_late.reshape(-1)[:D_MODEL].reshape(shape)
        return total[starts[i - 1]:starts[i]].reshape(-1)[:sizes[i - 1]].reshape(shape)

    loss = total[starts[11], 0]

    def col_shard(full, width):
        return lax.dynamic_slice_in_dim(full, me * width, width, axis=1)

    def head_shard(full):
        return lax.dynamic_slice_in_dim(full.reshape(N_HEADS, HEAD_DIM), me * shard_head, shard_head, axis=1)

    small_names = ["norm_mix_pre", "norm_mix_post", "norm_ffn_pre", "norm_ffn_post", "lru_conv_b", "lru_lambda",
                   "ffn_conv_b", "conv_short_w", "lru_conv_w", "lru_ba", "lru_bx", "ffn_conv_w"]
    small_g = [piece(0, (1, D_MODEL)), piece(1, (1, D_MODEL)), piece(2, (1, D_MODEL)), piece(3, (1, D_MODEL)),
               piece(4, (1, D_MODEL)), piece(5, (1, D_MODEL)), piece(6, (1, 2 * D_FF)),
               col_shard(piece(7, (3, D_MODEL)), LANES), col_shard(piece(8, (4, D_MODEL)), LANES),
               head_shard(piece(9, (1, D_MODEL))), head_shard(piece(10, (1, D_MODEL))),
               col_shard(piece(11, (3, 2 * D_FF)), shard_up)]
    small_w = [norm_mix_pre, norm_mix_post, norm_ffn_pre, norm_ffn_post, lru_conv_b, lru_lambda, ffn_conv_b,
               conv_short_w[0], lru_conv_w[0], lru_ba[0], lru_bx[0], ffn_conv_w[0]]
    small_m = [m_norm_mix_pre, m_norm_mix_post, m_norm_ffn_pre, m_norm_ffn_post, m_lru_conv_b, m_lru_lambda,
               m_ffn_conv_b, m_conv_short_w[0], m_lru_conv_w[0], m_lru_ba[0], m_lru_bx[0], m_ffn_conv_w[0]]
    small_v = [v_norm_mix_pre, v_norm_mix_post, v_norm_ffn_pre, v_norm_ffn_post, v_lru_conv_b, v_lru_lambda,
               v_ffn_conv_b, v_conv_short_w[0], v_lru_conv_w[0], v_lru_ba[0], v_lru_bx[0], v_ffn_conv_w[0]]
    s_d, s_m, s_v = _adam_small(small_w, small_g, small_m, small_v)
    for i, name in enumerate(small_names):
        shape = small_w[i].shape if i < 7 else (1,) + small_w[i].shape
        out_g[name] = small_g[i].reshape(shape)
        out_d[name], out_m[name], out_v[name] = s_d[i].reshape(shape), s_m[i].reshape(shape), s_v[i].reshape(shape)

    order = ["norm_mix_pre", "norm_mix_post", "norm_ffn_pre", "norm_ffn_post", "w_in", "conv_short_w", "w_conv_branch",
             "lru_conv_w", "lru_conv_b", "lru_wa", "lru_ba", "lru_wx", "lru_bx", "lru_lambda", "w_lru_branch", "w_out",
             "ffn_w_up", "ffn_conv_w", "ffn_conv_b", "ffn_w_down"]
    return (loss, dx.reshape(1, t, D_MODEL), *[out_g[k] for k in order], *[out_d[k] for k in order],
            *[out_m[k] for k in order], *[out_v[k] for k in order])
```

```python
import functools
import math

import jax
import jax.numpy as jnp
from jax import lax
from jax.experimental import pallas as pl
from jax.experimental.pallas import tpu as pltpu

F32 = jnp.float32
BF16 = jnp.bfloat16
MESH = pl.DeviceIdType.MESH

N_DEV = 8
D_MODEL = 1024
N_HEADS = 4
HEAD_DIM = D_MODEL // N_HEADS
D_FF = 3 * D_MODEL
IN_COLS = 7 * D_MODEL
LRU_C = 8.0
RMS_EPS = 1e-6
ADAM_LR = 0.001
ADAM_B1 = 0.9
ADAM_B2 = 0.999
ADAM_EPS = 1e-08
ADAM_WD = 0.01
ADAM_STEP = 10
GELU_K = math.sqrt(2.0 / math.pi)
GELU_C = 0.044715

LANES = 128
SUBLANES = 8
PAD = SUBLANES
VMEM_LIMIT = 56 * 1024 * 1024
CB = 256

HBM_SPEC = pl.BlockSpec(memory_space=pltpu.HBM)
SEM_SPEC = pl.BlockSpec(memory_space=pltpu.SEMAPHORE)
DATAFLOW_EFFECT = pltpu.SideEffectType.DATAFLOW_SIDE_EFFECTING
VMEM_SPEC = pl.BlockSpec(memory_space=pltpu.VMEM)


def _params(*sem):
    if sem:
        return pltpu.CompilerParams(dimension_semantics=sem, vmem_limit_bytes=VMEM_LIMIT)
    return pltpu.CompilerParams(vmem_limit_bytes=VMEM_LIMIT)


def _row_chunk(t):
    return min(256, t)


def _row_block(rows, cap):
    return next(rb for rb in range(min(cap, rows), 0, -16) if rows % rb == 0)


def _gelu(x):
    return 0.5 * x * (1.0 + jnp.tanh(GELU_K * (x + GELU_C * x * x * x)))


def _gelu_and_grad(x):
    t = jnp.tanh(GELU_K * (x + GELU_C * x * x * x))
    g = 0.5 * x * (1.0 + t)
    dg = 0.5 * (1.0 + t) + 0.5 * x * (1.0 - t * t) * GELU_K * (1.0 + 3.0 * GELU_C * x * x)
    return g, dg


def _expm1_neg(x):
    series = x * (1.0 + x * (0.5 + x * (1.0 / 6.0 + x * (1.0 / 24.0 + x * (1.0 / 120.0)))))
    return jnp.where(x > -0.05, series, jnp.exp(x) - 1.0)


def _log_sigmoid(x):
    return jnp.minimum(x, 0.0) - jnp.log1p(jnp.exp(-jnp.abs(x)))


def _dot(a, b):
    return jnp.dot(a, b, preferred_element_type=F32)


def _dot_nt(a, b):
    return lax.dot_general(a, b, (((1,), (1,)), ((), ())), preferred_element_type=F32)


def _dot_tn(a, b):
    return lax.dot_general(a, b, (((0,), (0,)), ((), ())), preferred_element_type=F32)


def _rms_fwd(x):
    r = lax.rsqrt(jnp.mean(x * x, axis=-1, keepdims=True) + RMS_EPS)
    return x * r, r


def _rms_bwd(n, r, gdy):
    return r * (gdy - n * jnp.mean(n * gdy, axis=-1, keepdims=True))


def _rows_back(pad_ref, r0, rows, j):
    cur = pad_ref[pl.ds(PAD + r0, rows), :]
    if j == 0:
        return cur
    before = pad_ref[pl.ds(PAD + r0 - SUBLANES, SUBLANES), :]
    row = lax.broadcasted_iota(jnp.int32, before.shape, 0)
    rolled = pltpu.roll(cur, j, 0)
    top = jnp.where(row < j, pltpu.roll(before, j, 0), rolled[0:SUBLANES, :])
    return jnp.concatenate([top, rolled[SUBLANES:, :]], axis=0)


def _rows_ahead(pad_ref, r0, rows, j):
    cur = pad_ref[pl.ds(r0, rows), :]
    if j == 0:
        return cur
    after = pad_ref[pl.ds(r0 + rows, SUBLANES), :]
    row = lax.broadcasted_iota(jnp.int32, after.shape, 0)
    rolled = pltpu.roll(cur, rows - j, 0)
    bottom = jnp.where(row >= SUBLANES - j, pltpu.roll(after, SUBLANES - j, 0), rolled[rows - SUBLANES:, :])
    return jnp.concatenate([rolled[:rows - SUBLANES, :], bottom], axis=0)


def _conv_causal(pad_ref, w, r0, rows, taps):
    acc = None
    for k in range(taps):
        term = w[k:k + 1, :] * _rows_back(pad_ref, r0, rows, taps - 1 - k)
        acc = term if acc is None else acc + term
    return acc


def _conv_anticausal(pad_ref, w, r0, rows, taps):
    acc = None
    for k in range(taps):
        term = w[k:k + 1, :] * _rows_ahead(pad_ref, r0, rows, taps - 1 - k)
        acc = term if acc is None else acc + term
    return acc


def _conv_wgrad(g, xpad_ref, r0, rows, taps):
    return [jnp.sum(g * _rows_back(xpad_ref, r0, rows, taps - 1 - k), axis=0, keepdims=True) for k in range(taps)]


def _position():
    return lax.axis_index("x"), lax.axis_index("y"), lax.axis_index("c")


def _block_of(x, y, c):
    return 4 * x + 2 * y + c


def _chip(x, y, k):
    return (x + (k & 1)) % 2, (y + (k >> 1)) % 2


def _cols(width):
    def at(ref, d):
        return ref.at[:, pl.ds(pl.multiple_of(d * width, LANES), width)]
    return at


def _rows(height):
    def at(ref, d):
        return ref.at[pl.ds(pl.multiple_of(d * height, 16), height), :]
    return at


def _lead(ref, d):
    return ref.at[d]


def _gather_weights(shards, blocks, full_shapes, small, n_now):
    n = len(shards)
    small_rows = small.shape[0]

    def body(*refs):
        ins, small_in = refs[:n], refs[n]
        outs, small_out = refs[n + 1:2 * n + 1], refs[2 * n + 1]
        stage = refs[2 * n + 2:3 * n + 2]
        send, recv, local = refs[3 * n + 2:]
        x, y, c = _position()
        me = _block_of(x, y, c)
        sibling = (x, y, 1 - c)

        for a in range(n):
            stage[a][...] = ins[a][...].astype(BF16)

        def copy(a, k, block, to, src=None):
            dst = blocks[a](outs[a], block)
            return pltpu.make_async_remote_copy(
                src_ref=dst if src is None else src, dst_ref=dst, send_sem=send.at[a, k], recv_sem=recv.at[a, k],
                device_id=to, device_id_type=MESH)

        def small_copy(k):
            px, py, pc = (x + (k & 1)) % 2, (y + ((k >> 1) & 1)) % 2, (c + (k >> 2)) % 2
            return pltpu.make_async_remote_copy(
                src_ref=small_in, dst_ref=small_out.at[me], send_sem=send.at[n_now, k - 1], recv_sem=recv.at[n_now, k - 1],
                device_id=(px, py, pc), device_id_type=MESH)

        def small_arrival(k):
            px, py, pc = (x + (k & 1)) % 2, (y + ((k >> 1) & 1)) % 2, (c + (k >> 2)) % 2
            return pltpu.make_async_remote_copy(
                src_ref=small_in, dst_ref=small_out.at[_block_of(px, py, pc)], send_sem=send.at[n_now, k - 1],
                recv_sem=recv.at[n_now, k - 1], device_id=(px, py, pc), device_id_type=MESH)

        small_out[me] = small_in[...]
        small_sends = [small_copy(k) for k in range(1, N_DEV)]
        for cp in small_sends:
            cp.start()

        mine, first, passed = [], [], []
        for a in range(n):
            own = pltpu.make_async_copy(stage[a], blocks[a](outs[a], me), local.at[a])
            own.start()
            mine.append(own)
            if a >= n_now:
                continue
            sends = [copy(a, 0, me, sibling, src=stage[a])]
            sends += [copy(a, k, me, (*_chip(x, y, k), c), src=stage[a]) for k in (1, 2, 3)]
            for cp in sends:
                cp.start()
            first += sends
        for a in range(n_now):
            for k in (1, 2, 3):
                landed = _block_of(*_chip(x, y, k), c)
                copy(a, k, landed, (x, y, c)).wait_recv()
                fwd = copy(a, 3 + k, landed, sibling)
                fwd.start()
                passed.append(fwd)
        for a in range(n_now):
            copy(a, 0, _block_of(x, y, 1 - c), (x, y, c)).wait_recv()
            for k in (1, 2, 3):
                copy(a, 3 + k, _block_of(*_chip(x, y, k), 1 - c), (x, y, c)).wait_recv()
        for k in range(1, N_DEV):
            small_arrival(k).wait_recv()
        for cp in first + passed + small_sends:
            cp.wait_send()
        for own in mine:
            own.wait()

    out_shape = [jax.ShapeDtypeStruct(s, BF16) for s in full_shapes]
    out_shape.append(jax.ShapeDtypeStruct((N_DEV, small_rows, LANES), F32))
    return pl.pallas_call(
        body, name="gather_weights", out_shape=out_shape,
        in_specs=[VMEM_SPEC] * (n + 1), out_specs=[HBM_SPEC] * n + [VMEM_SPEC],
        scratch_shapes=[pltpu.VMEM(s.shape, BF16) for s in shards]
        + [pltpu.SemaphoreType.DMA((n_now + 1, 7)), pltpu.SemaphoreType.DMA((n_now + 1, 7)),
           pltpu.SemaphoreType.DMA((n,))],
        compiler_params=_params(),
    )(*shards, small)


def _gather_first(full, blocks, send, recv):
    x, y, c = _position()
    me = _block_of(x, y, c)
    peers = [(x, y, 1 - c)] + [(*_chip(x, y, k), c) for k in (1, 2, 3)]

    def copy(a, k, block):
        at = blocks[a](full[a], block)
        return pltpu.make_async_remote_copy(src_ref=at, dst_ref=at, send_sem=send[4 * a + k], recv_sem=recv[4 * a + k],
                                            device_id=peers[k], device_id_type=MESH)

    sends = [copy(a, k, me) for a in range(len(full)) for k in range(4)]
    arrivals = [copy(a, k, _block_of(*peers[k])) for a in range(len(full)) for k in range(4)]
    return sends, arrivals


def _gather_second(full, blocks, send, recv):
    x, y, c = _position()

    def copy(a, k, cc):
        at = blocks[a](full[a], _block_of(*_chip(x, y, k), cc))
        return pltpu.make_async_remote_copy(src_ref=at, dst_ref=at, send_sem=send[3 * a + k - 1],
                                            recv_sem=recv[3 * a + k - 1], device_id=(x, y, 1 - c), device_id_type=MESH)

    sends = [copy(a, k, c) for a in range(len(full)) for k in (1, 2, 3)]
    arrivals = [copy(a, k, 1 - c) for a in range(len(full)) for k in (1, 2, 3)]
    return sends, arrivals


def _split_call(body, name, arrays, sems_in, n_sems_out, after=None, token=False):
    n, m = len(arrays), len(sems_in)

    def kernel_body(*refs):
        outs = refs[n + m + (after is not None):]
        body(refs[:n], refs[n:n + m], outs[:n_sems_out])
        if token:
            outs[-1][...] = jnp.zeros_like(outs[-1])

    extra_in = [] if after is None else [after]
    outs = pl.pallas_call(
        kernel_body, name=name,
        out_shape=(*[pltpu.SemaphoreType.DMA(())] * n_sems_out, *[pltpu.HBM(a.shape, a.dtype) for a in arrays],
                   *([jax.ShapeDtypeStruct((SUBLANES, LANES), F32)] if token else [])),
        in_specs=[HBM_SPEC] * n + [SEM_SPEC] * m + [pl.BlockSpec(memory_space=pl.ANY)] * len(extra_in),
        out_specs=(*[SEM_SPEC] * n_sems_out, *[HBM_SPEC] * n, *([VMEM_SPEC] if token else [])),
        input_output_aliases={i: n_sems_out + i for i in range(n)},
        compiler_params=pltpu.CompilerParams(has_side_effects=DATAFLOW_EFFECT),
    )(*[pltpu.with_memory_space_constraint(a, pltpu.HBM) for a in arrays], *sems_in, *extra_in)
    sems, rest = list(outs[:n_sems_out]), list(outs[n_sems_out:])
    return (sems, rest[:n], rest[n]) if token else (sems, rest[:n])


def _gather_start(full, blocks, name):
    n = len(full)

    def body(arrays, _, sems):
        for cp in _gather_first(arrays, blocks, sems[:4 * n], sems[4 * n:])[0]:
            cp.start()

    sems, arrays, token = _split_call(body, name, full, [], 8 * n, token=True)
    return sems[:4 * n], sems[4 * n:], arrays, token


def _gather_forward(full, blocks, send_first, recv_first, after, name):
    n = len(full)

    def body(arrays, sems_in, sems):
        sends, arrivals = _gather_first(arrays, blocks, sems_in[:4 * n], sems_in[4 * n:])
        for cp in arrivals:
            cp.wait_recv()
        for cp in _gather_second(arrays, blocks, sems[:3 * n], sems[3 * n:])[0]:
            cp.start()
        for cp in sends:
            cp.wait_send()

    sems, arrays = _split_call(body, name, full, [*send_first, *recv_first], 6 * n, after=after)
    return sems[:3 * n], sems[3 * n:], arrays


def _gather_finish(full, blocks, send_second, recv_second, after, name):
    n = len(full)

    def body(arrays, sems_in, _):
        sends, arrivals = _gather_second(arrays, blocks, sems_in[:3 * n], sems_in[3 * n:])
        for cp in sends:
            cp.wait_send()
        for cp in arrivals:
            cp.wait_recv()

    return _split_call(body, name, full, [*send_second, *recv_second], 0, after=after)[1]


def _reduce_pair(grads, blocks, shard_shapes, name):
    n = len(grads)

    def body(*refs):
        ins, outs = refs[:n], refs[n:2 * n]
        got, own = refs[2 * n:3 * n], refs[3 * n:4 * n]
        send, recv, local = refs[4 * n:]
        x, y, c = _position()
        copies, loads = [], []
        for a in range(n):
            for k in range(4):
                chip = _chip(x, y, k)
                cp = pltpu.make_async_remote_copy(
                    src_ref=blocks[a](ins[a], _block_of(*chip, 1 - c)), dst_ref=got[a].at[k],
                    send_sem=send.at[a, k], recv_sem=recv.at[a, k], device_id=(x, y, 1 - c), device_id_type=MESH)
                cp.start()
                copies.append(cp)
                ld = pltpu.make_async_copy(blocks[a](ins[a], _block_of(*chip, c)), own[a].at[k], local.at[a, k])
                ld.start()
                loads.append(ld)
        for a in range(n):
            for k in range(4):
                loads[4 * a + k].wait()
                copies[4 * a + k].wait_recv()
                outs[a][k] = (own[a][k].astype(F32) + got[a][k].astype(F32)).astype(BF16)
        for cp in copies:
            cp.wait_send()

    slots = [(4,) + tuple(s) for s in shard_shapes]
    return pl.pallas_call(
        body, name=name, out_shape=[jax.ShapeDtypeStruct(s, BF16) for s in slots],
        in_specs=[HBM_SPEC] * n, out_specs=[VMEM_SPEC] * n,
        scratch_shapes=[pltpu.VMEM(s, BF16) for s in slots] * 2
        + [pltpu.SemaphoreType.DMA((n, 4)), pltpu.SemaphoreType.DMA((n, 4)), pltpu.SemaphoreType.DMA((n, 4))],
        compiler_params=_params(),
    )(*grads)


def _chip_copies(sums, lands, send, recv):
    x, y, c = _position()
    return [pltpu.make_async_remote_copy(
        src_ref=sums[a].at[k], dst_ref=lands[a].at[k - 1], send_sem=send[3 * a + k - 1], recv_sem=recv[3 * a + k - 1],
        device_id=(*_chip(x, y, k), c), device_id_type=MESH) for a in range(len(sums)) for k in (1, 2, 3)]


def _exchange_chips_start(pair_sums, name):
    n = len(pair_sums)
    lands = [pltpu.with_memory_space_constraint(lax.empty((3,) + tuple(p.shape[1:]), BF16), pltpu.HBM) for p in pair_sums]

    def body(*refs):
        sums, zones = refs[:n], refs[n:2 * n]
        send, recv = refs[2 * n:5 * n], refs[5 * n:8 * n]
        token = refs[-1]
        for cp in _chip_copies(sums, zones, send, recv):
            cp.start()
        token[...] = jnp.zeros_like(token)

    outs = pl.pallas_call(
        body, name=name,
        out_shape=(*[pltpu.SemaphoreType.DMA(())] * (6 * n),
                   *[pltpu.HBM(p.shape, BF16) for p in pair_sums], *[pltpu.HBM(z.shape, BF16) for z in lands],
                   jax.ShapeDtypeStruct((SUBLANES, LANES), F32)),
        in_specs=[HBM_SPEC] * (2 * n), out_specs=(*[SEM_SPEC] * (6 * n), *[HBM_SPEC] * (2 * n), VMEM_SPEC),
        input_output_aliases={i: 6 * n + i for i in range(2 * n)},
        compiler_params=pltpu.CompilerParams(has_side_effects=DATAFLOW_EFFECT),
    )(*[pltpu.with_memory_space_constraint(p, pltpu.HBM) for p in pair_sums], *lands)
    return outs[:3 * n], outs[3 * n:6 * n], outs[6 * n:7 * n], outs[7 * n:8 * n], outs[-1]


def _exchange_chips_wait(send, recv, sums, lands, after, name):
    n = len(sums)

    def body(*refs):
        sums_in, zones = refs[:n], refs[n:2 * n]
        send_in, recv_in = refs[2 * n:5 * n], refs[5 * n:8 * n]
        for cp in _chip_copies(sums_in, zones, send_in, recv_in):
            cp.wait_send()
            cp.wait_recv()

    outs = pl.pallas_call(
        body, name=name,
        out_shape=(*[pltpu.HBM(p.shape, BF16) for p in sums], *[pltpu.HBM(z.shape, BF16) for z in lands]),
        in_specs=[HBM_SPEC] * (2 * n) + [SEM_SPEC] * (6 * n) + [pl.BlockSpec(memory_space=pl.ANY)],
        out_specs=[HBM_SPEC] * (2 * n), input_output_aliases={i: i for i in range(2 * n)},
        compiler_params=pltpu.CompilerParams(has_side_effects=DATAFLOW_EFFECT),
    )(*sums, *lands, *send, *recv, after)
    return outs[:n], outs[n:]


def _small_copies(mine, land, send, recv):
    x, y, c = _position()
    me = _block_of(x, y, c)

    def peer(k):
        return (x + (k & 1)) % 2, (y + ((k >> 1) & 1)) % 2, (c + (k >> 2)) % 2

    def copy(k, slot):
        return pltpu.make_async_remote_copy(src_ref=mine, dst_ref=land.at[slot], send_sem=send[k - 1], recv_sem=recv[k - 1],
                                            device_id=peer(k), device_id_type=MESH)

    return [copy(k, me) for k in range(1, N_DEV)], [copy(k, _block_of(*peer(k))) for k in range(1, N_DEV)]


def _small_start(part, name):
    land = jnp.zeros((N_DEV,) + part.shape, F32)

    def body(arrays, _, sems):
        for cp in _small_copies(arrays[0], arrays[1], sems[:7], sems[7:])[0]:
            cp.start()

    sems, arrays, token = _split_call(body, name, [part, land], [], 14, token=True)
    return sems[:7], sems[7:], arrays[0], arrays[1], token


def _small_wait(send, recv, part, land, after, name):
    def body(arrays, sems_in, _):
        sends, arrivals = _small_copies(arrays[0], arrays[1], sems_in[:7], sems_in[7:])
        for cp in sends:
            cp.wait_send()
        for cp in arrivals:
            cp.wait_recv()

    return _split_call(body, name, [part, land], [*send, *recv], 0, after=after)[1]


def _small_sum(pairs, me):
    n = len(pairs)

    def body(me_ref, *refs):
        for i in range(n):
            mine, land, out = refs[2 * i], refs[2 * i + 1], refs[2 * n + i]
            total = jnp.zeros(mine.shape, F32)
            for d in range(N_DEV):
                total = total + land[d] + jnp.where(me_ref[0] == d, mine[...], 0.0)
            out[...] = total

    flat = [a for pair in pairs for a in pair]
    return pl.pallas_call(
        body, name="small_sum", out_shape=[jax.ShapeDtypeStruct(mine.shape, F32) for mine, _ in pairs],
        in_specs=[pl.BlockSpec(memory_space=pltpu.SMEM)] + [VMEM_SPEC] * (2 * n), out_specs=[VMEM_SPEC] * n,
        compiler_params=_params(),
    )(me.reshape(1).astype(jnp.int32), *flat)


def _in_proj(x, g1, w_in):
    t = x.shape[0]
    tm, bn = min(1024, t), 1024

    def body(x_ref, g_ref, w_ref, proj_ref, h_ref, h_s):
        @pl.when(pl.program_id(1) == 0)
        def _():
            n, _ = _rms_fwd(x_ref[...])
            h_s[...] = (n * g_ref[...]).astype(BF16)
            h_ref[...] = h_s[...]
        proj_ref[...] = _dot(h_s[...], w_ref[...]).astype(BF16)

    return pl.pallas_call(
        body, name="in_proj", grid=(t // tm, IN_COLS // bn),
        out_shape=[jax.ShapeDtypeStruct((t, IN_COLS), BF16), jax.ShapeDtypeStruct((t, D_MODEL), BF16)],
        in_specs=[pl.BlockSpec((tm, D_MODEL), lambda i, j: (i, 0)), pl.BlockSpec((1, D_MODEL), lambda i, j: (0, 0)),
                  pl.BlockSpec((D_MODEL, bn), lambda i, j: (0, j))],
        out_specs=[pl.BlockSpec((tm, bn), lambda i, j: (i, j)), pl.BlockSpec((tm, D_MODEL), lambda i, j: (i, 0))],
        scratch_shapes=[pltpu.VMEM((tm, D_MODEL), BF16)],
        compiler_params=_params("parallel", "arbitrary"),
    )(x, g1, w_in)


def _section(s, t):
    return pl.BlockSpec((t, CB), lambda h, s=s: (0, s * (D_MODEL // CB) + h))


def _conv_mixer_fwd(proj, w_short):
    t = proj.shape[0]
    rc = _row_chunk(t)

    def body(b_ref, c_ref, x_ref, w_ref, y_ref, pad):
        pad[pl.ds(0, PAD), :] = jnp.zeros((PAD, CB), F32)
        for r0 in range(0, t, rc):
            rows = pl.ds(r0, rc)
            pad[pl.ds(PAD + r0, rc), :] = c_ref[rows, :].astype(F32) * x_ref[rows, :].astype(F32)
        w = w_ref[...]
        for r0 in range(0, t, rc):
            rows = pl.ds(r0, rc)
            y_ref[rows, :] = (b_ref[rows, :].astype(F32) * _conv_causal(pad, w, r0, rc, 3)).astype(BF16)

    return pl.pallas_call(
        body, name="conv_mixer_fwd", grid=(D_MODEL // CB,),
        out_shape=jax.ShapeDtypeStruct((t, D_MODEL), BF16),
        in_specs=[_section(0, t), _section(1, t), _section(2, t), pl.BlockSpec((3, CB), lambda h: (0, h))],
        out_specs=pl.BlockSpec((t, CB), lambda h: (0, h)),
        scratch_shapes=[pltpu.VMEM((t + PAD, CB), F32)],
        compiler_params=_params("parallel"),
    )(proj, proj, proj, w_short)


def _lru_gates(xl, wa, ba, wx, bx, ls, first_row):
    xb = xl.astype(BF16)
    ra = jax.nn.sigmoid(_dot(xb, wa) + ba)
    ia = jax.nn.sigmoid(_dot(xb, wx) + bx)
    la = LRU_C * ra * ls
    a = jnp.exp(la)
    one_minus = -_expm1_neg(2.0 * la)
    mult = jnp.where(first_row, 1.0, jnp.sqrt(one_minus))
    return xb, ra, ia, a, one_minus, mult


def _head_specs():
    vec = pl.BlockSpec((1, CB), lambda h: (0, h))
    mat = pl.BlockSpec((N_DEV, None, HEAD_DIM // N_DEV, HEAD_DIM), lambda h: (0, h, 0, 0))
    return vec, mat


def _lru_fwd(proj, w_conv, b_conv, wa, ba, wx, bx, lam):
    t = proj.shape[0]
    rc = _row_chunk(t)
    vec, mat = _head_specs()

    def body(lx_ref, ly_ref, wc_ref, bc_ref, wa_ref, ba_ref, wx_ref, bx_ref, lam_ref, yb_ref, hl_ref, a_ref, kept_ref,
             pad, u_s):
        pad[pl.ds(0, PAD), :] = jnp.zeros((PAD, CB), F32)
        for r0 in range(0, t, rc):
            pad[pl.ds(PAD + r0, rc), :] = lx_ref[pl.ds(r0, rc), :].astype(F32)
        wc, bc = wc_ref[...], bc_ref[...]
        wa_m, wx_m = wa_ref[...].reshape(HEAD_DIM, HEAD_DIM), wx_ref[...].reshape(HEAD_DIM, HEAD_DIM)
        ls = _log_sigmoid(lam_ref[...])
        for r0 in range(0, t, rc):
            rows = pl.ds(r0, rc)
            xl = _conv_causal(pad, wc, r0, rc, 4) + bc
            first = (lax.broadcasted_iota(jnp.int32, (rc, CB), 0) + r0) == 0
            xb, ra, ia, a, _, mult = _lru_gates(xl, wa_m, ba_ref[...], wx_m, bx_ref[...], ls, first)
            a_ref[rows, :] = a
            u_s[rows, :] = mult * (ia * xl)
            kept_ref[0, rows, :] = xb
            kept_ref[1, rows, :] = ra.astype(BF16)
            kept_ref[2, rows, :] = ia.astype(BF16)

        row = lax.broadcasted_iota(jnp.int32, (SUBLANES, CB), 0)

        def group(g, carry):
            r = pl.multiple_of(g * SUBLANES, SUBLANES)
            a_g, b_g = a_ref[pl.ds(r, SUBLANES), :], u_s[pl.ds(r, SUBLANES), :]
            for s in (1, 2, 4):
                keep = row >= s
                b_g = jnp.where(keep, a_g * pltpu.roll(b_g, s, 0) + b_g, b_g)
                a_g = jnp.where(keep, a_g * pltpu.roll(a_g, s, 0), a_g)
            h_g = b_g + a_g * carry
            hl_ref[pl.ds(r, SUBLANES), :] = h_g
            return jnp.broadcast_to(h_g[SUBLANES - 1:SUBLANES, :], (SUBLANES, CB))

        lax.fori_loop(0, t // SUBLANES, group, jnp.zeros((SUBLANES, CB), F32))
        for r0 in range(0, t, rc):
            rows = pl.ds(r0, rc)
            yb_ref[rows, :] = (hl_ref[rows, :] * _gelu(ly_ref[rows, :].astype(F32))).astype(BF16)

    blk = pl.BlockSpec((t, CB), lambda h: (0, h))
    res = jax.ShapeDtypeStruct((t, D_MODEL), F32)
    return pl.pallas_call(
        body, name="lru_fwd", grid=(N_HEADS,),
        out_shape=[jax.ShapeDtypeStruct((t, D_MODEL), BF16), res, res, jax.ShapeDtypeStruct((3, t, D_MODEL), BF16)],
        in_specs=[_section(3, t), _section(4, t), pl.BlockSpec((4, CB), lambda h: (0, h)), vec, mat, vec, mat, vec, vec],
        out_specs=[blk, blk, blk, pl.BlockSpec((3, t, CB), lambda h: (0, 0, h))],
        scratch_shapes=[pltpu.VMEM((t + PAD, CB), F32), pltpu.VMEM((t, CB), F32)],
        compiler_params=_params("parallel"),
    )(proj, proj, w_conv, b_conv, wa, ba, wx, bx, lam)


def _merge(y_a, y_b, proj, x, w_cb, w_lb, w_out, g2, g3):
    t = x.shape[0]
    tm = min(256, t)

    def body(ya_ref, yb_ref, gc_ref, gl_ref, x_ref, wcb_ref, wlb_ref, wo_ref, g2_ref, g3_ref,
             pa_ref, pb_ref, mg_ref, mix_ref, x1_ref, h2_ref):
        pa = _dot(ya_ref[...], wcb_ref[...]).astype(BF16)
        pb = _dot(yb_ref[...], wlb_ref[...]).astype(BF16)
        pa_ref[...] = pa
        pb_ref[...] = pb
        merged = (jax.nn.sigmoid(gc_ref[...].astype(F32)) * pa.astype(F32)
                  + jax.nn.sigmoid(gl_ref[...].astype(F32)) * pb.astype(F32)).astype(BF16)
        mg_ref[...] = merged
        mix = _dot(merged, wo_ref[...])
        mix_ref[...] = mix
        n2, _ = _rms_fwd(mix)
        x1 = x_ref[...] + n2 * g2_ref[...]
        x1_ref[...] = x1
        n3, _ = _rms_fwd(x1)
        h2_ref[...] = (n3 * g3_ref[...]).astype(BF16)

    row = pl.BlockSpec((tm, D_MODEL), lambda i: (i, 0))
    full = pl.BlockSpec((D_MODEL, D_MODEL), lambda i: (0, 0))
    vec = pl.BlockSpec((1, D_MODEL), lambda i: (0, 0))
    act = jax.ShapeDtypeStruct((t, D_MODEL), BF16)
    res = jax.ShapeDtypeStruct((t, D_MODEL), F32)
    return pl.pallas_call(
        body, name="merge_fwd", grid=(t // tm,), out_shape=[act, act, act, res, res, act],
        in_specs=[row, row, pl.BlockSpec((tm, D_MODEL), lambda i: (i, 5)), pl.BlockSpec((tm, D_MODEL), lambda i: (i, 6)),
                  row, full, full, full, vec, vec],
        out_specs=[row] * 6,
        compiler_params=_params("parallel"),
    )(y_a, y_b, proj, proj, x, w_cb, w_lb, w_out, g2, g3)


N_FF_BLOCKS = D_FF // CB


def _ffn_up(h2, w_up, w_conv, b_conv):
    t = h2.shape[0]
    rc = _row_chunk(t)
    nb = N_FF_BLOCKS

    def body(h_ref, w_ref, c_ref, b_ref, up_ref, act_ref, f_ref, pad, gate):
        k = pl.program_id(1)
        pad[pl.ds(0, PAD), :] = jnp.zeros((PAD, CB), F32)
        for r0 in range(0, t, rc):
            rows = pl.ds(r0, rc)
            up = _dot(h_ref[rows, :], w_ref[...]).astype(BF16)
            up_ref[rows, :] = up
            pad[pl.ds(PAD + r0, rc), :] = up.astype(F32)
        cw = c_ref[...]
        for r0 in range(0, t, rc):
            rows = pl.ds(r0, rc)
            act = _conv_causal(pad, cw, r0, rc, 3) + b_ref[...]
            act_ref[rows, :] = act.astype(BF16)

            @pl.when(k == 0)
            def _():
                gate[rows, :] = act

            @pl.when(k == 1)
            def _():
                f_ref[rows, :] = (_gelu(gate[rows, :]) * act).astype(BF16)

    half = lambda rows: pl.BlockSpec((rows, CB), lambda j, k: (0, nb * k + j))
    wide = jax.ShapeDtypeStruct((t, 2 * D_FF), BF16)
    return pl.pallas_call(
        body, name="ffn_up_fwd", grid=(nb, 2), out_shape=[wide, wide, jax.ShapeDtypeStruct((t, D_FF), BF16)],
        in_specs=[pl.BlockSpec((t, D_MODEL), lambda j, k: (0, 0)), half(D_MODEL), half(3), half(1)],
        out_specs=[half(t), half(t), pl.BlockSpec((t, CB), lambda j, k: (0, j))],
        scratch_shapes=[pltpu.VMEM((t + PAD, CB), F32), pltpu.VMEM((t, CB), F32)],
        compiler_params=_params("parallel", "arbitrary"),
    )(h2, w_up, w_conv, b_conv)


def _ffn_down(f, act, w_down, x1, target, g4):
    t = f.shape[0]
    tm = min(256, t)
    cc = 512

    def body(f_ref, act_ref, w_ref, x1_ref, tg_ref, g_ref, dy_ref, dout_ref, back_ref, dg_ref, loss_ref):
        @pl.when(pl.program_id(0) == 0)
        def _():
            dg_ref[...] = jnp.zeros_like(dg_ref)
            loss_ref[...] = jnp.zeros_like(loss_ref)
        out = _dot(f_ref[...], w_ref[...])
        n4, r4 = _rms_fwd(out)
        err = x1_ref[...] + n4 * g_ref[...] - tg_ref[...]
        loss_ref[...] += jnp.full(loss_ref.shape, 0.5 / D_MODEL, F32) * jnp.sum(err * err)
        dy = err * (1.0 / D_MODEL)
        dy_ref[...] = dy
        dg_ref[...] += jnp.sum(dy * n4, axis=0, keepdims=True)
        d_out = _rms_bwd(n4, r4, dy * g_ref[...]).astype(BF16)
        dout_ref[...] = d_out
        for c0 in range(0, D_FF, cc):
            d_f = _dot_nt(d_out, w_ref[pl.ds(c0, cc), :])
            gelu, d_gelu = _gelu_and_grad(act_ref[:, pl.ds(c0, cc)].astype(F32))
            val = act_ref[:, pl.ds(D_FF + c0, cc)].astype(F32)
            back_ref[:, pl.ds(c0, cc)] = (d_f * val * d_gelu).astype(BF16)
            back_ref[:, pl.ds(D_FF + c0, cc)] = (d_f * gelu).astype(BF16)

    row = pl.BlockSpec((tm, D_MODEL), lambda i: (i, 0))
    wide = pl.BlockSpec((tm, 2 * D_FF), lambda i: (i, 0))
    vec = pl.BlockSpec((1, D_MODEL), lambda i: (0, 0))
    return pl.pallas_call(
        body, name="ffn_down_fwd_bwd", grid=(t // tm,),
        out_shape=[jax.ShapeDtypeStruct((t, D_MODEL), F32), jax.ShapeDtypeStruct((t, D_MODEL), BF16),
                   jax.ShapeDtypeStruct((t, 2 * D_FF), BF16), jax.ShapeDtypeStruct((1, D_MODEL), F32),
                   jax.ShapeDtypeStruct((SUBLANES, LANES), F32)],
        in_specs=[pl.BlockSpec((tm, D_FF), lambda i: (i, 0)), wide, pl.BlockSpec((D_FF, D_MODEL), lambda i: (0, 0)),
                  row, row, vec],
        out_specs=[row, row, wide, vec, pl.BlockSpec((SUBLANES, LANES), lambda i: (0, 0))],
        compiler_params=_params("arbitrary"),
    )(f, act, w_down, x1, target, g4)


def _grad_tn(a, b, bm, name):
    t, m = a.shape
    n = b.shape[1]

    def body(a_ref, b_ref, o_ref):
        o_ref[...] = _dot_tn(a_ref[...], b_ref[...]).astype(BF16)

    return pl.pallas_call(
        body, name=name, grid=(m // bm,), out_shape=jax.ShapeDtypeStruct((m, n), BF16),
        in_specs=[pl.BlockSpec((t, bm), lambda i: (0, i)), pl.BlockSpec((t, n), lambda i: (0, 0))],
        out_specs=pl.BlockSpec((bm, n), lambda i: (i, 0)),
        compiler_params=_params("parallel"),
    )(a, b)


def _ffn_up_bwd(up, back, w_conv, h2, w_up):
    t = h2.shape[0]
    rc = _row_chunk(t)
    nb = N_FF_BLOCKS

    def body(up_ref, back_ref, c_ref, h_ref, w_ref, dw_ref, dcw_ref, dcb_ref, dh_ref, pad, after, d_up):
        @pl.when((pl.program_id(0) == 0) & (pl.program_id(1) == 0))
        def _():
            dh_ref[...] = jnp.zeros_like(dh_ref)
        pad[pl.ds(0, PAD), :] = jnp.zeros((PAD, CB), F32)
        after[pl.ds(t, PAD), :] = jnp.zeros((PAD, CB), F32)
        for r0 in range(0, t, rc):
            pad[pl.ds(PAD + r0, rc), :] = up_ref[pl.ds(r0, rc), :].astype(F32)
            after[pl.ds(r0, rc), :] = back_ref[pl.ds(r0, rc), :].astype(F32)
        cw = c_ref[...]
        taps = [jnp.zeros((1, CB), F32)] * 3
        bias = jnp.zeros((1, CB), F32)
        for r0 in range(0, t, rc):
            rows = pl.ds(r0, rc)
            d = _conv_anticausal(after, cw, r0, rc, 3).astype(BF16)
            d_up[rows, :] = d
            dh_ref[rows, :] += _dot_nt(d, w_ref[...])
            g = after[rows, :]
            taps = [acc + new for acc, new in zip(taps, _conv_wgrad(g, pad, r0, rc, 3))]
            bias = bias + jnp.sum(g, axis=0, keepdims=True)
        dw_ref[...] = _dot_tn(h_ref[...], d_up[...]).astype(BF16)
        dcw_ref[...] = jnp.concatenate(taps, axis=0)
        dcb_ref[...] = bias

    half = lambda rows: pl.BlockSpec((rows, CB), lambda j, k: (0, nb * k + j))
    whole = pl.BlockSpec((t, D_MODEL), lambda j, k: (0, 0))
    return pl.pallas_call(
        body, name="ffn_up_bwd", grid=(nb, 2),
        out_shape=[jax.ShapeDtypeStruct((D_MODEL, 2 * D_FF), BF16), jax.ShapeDtypeStruct((3, 2 * D_FF), F32),
                   jax.ShapeDtypeStruct((1, 2 * D_FF), F32), jax.ShapeDtypeStruct((t, D_MODEL), F32)],
        in_specs=[half(t), half(t), half(3), whole, half(D_MODEL)],
        out_specs=[half(D_MODEL), half(3), half(1), whole],
        scratch_shapes=[pltpu.VMEM((t + PAD, CB), F32), pltpu.VMEM((t + PAD, CB), F32), pltpu.VMEM((t, CB), BF16)],
        compiler_params=_params("arbitrary", "arbitrary"),
    )(up, back, w_conv, h2, w_up)


def _merge_bwd(dy, d_h2, x1, mix, g3, g2, w_out, w_cb, w_lb, pa, pb, proj):
    t = dy.shape[0]
    tm = min(256, t)

    def body(dy_ref, dh2_ref, x1_ref, mix_ref, g3_ref, g2_ref, wo_ref, wcb_ref, wlb_ref, pa_ref, pb_ref, gc_ref, gl_ref,
             dx1_ref, dmix_ref, dpa_ref, dpb_ref, dya_ref, dyb_ref, dgate_ref, dg3_ref, dg2_ref):
        @pl.when(pl.program_id(0) == 0)
        def _():
            dg3_ref[...] = jnp.zeros_like(dg3_ref)
            dg2_ref[...] = jnp.zeros_like(dg2_ref)
        n3, r3 = _rms_fwd(x1_ref[...])
        d_h2 = dh2_ref[...]
        dg3_ref[...] += jnp.sum(d_h2 * n3, axis=0, keepdims=True)
        dx1 = dy_ref[...] + _rms_bwd(n3, r3, d_h2 * g3_ref[...])
        dx1_ref[...] = dx1
        n2, r2 = _rms_fwd(mix_ref[...])
        dg2_ref[...] += jnp.sum(dx1 * n2, axis=0, keepdims=True)
        d_mix = _rms_bwd(n2, r2, dx1 * g2_ref[...]).astype(BF16)
        dmix_ref[...] = d_mix
        d_merged = _dot_nt(d_mix, wo_ref[...])
        sc = jax.nn.sigmoid(gc_ref[...].astype(F32))
        sl = jax.nn.sigmoid(gl_ref[...].astype(F32))
        d_pa = (d_merged * sc).astype(BF16)
        d_pb = (d_merged * sl).astype(BF16)
        dpa_ref[...] = d_pa
        dpb_ref[...] = d_pb
        dgate_ref[0] = (d_merged * pa_ref[...].astype(F32) * sc * (1.0 - sc)).astype(BF16)
        dgate_ref[1] = (d_merged * pb_ref[...].astype(F32) * sl * (1.0 - sl)).astype(BF16)
        dya_ref[...] = _dot_nt(d_pa, wcb_ref[...]).astype(BF16)
        dyb_ref[...] = _dot_nt(d_pb, wlb_ref[...]).astype(BF16)

    row = pl.BlockSpec((tm, D_MODEL), lambda i: (i, 0))
    full = pl.BlockSpec((D_MODEL, D_MODEL), lambda i: (0, 0))
    vec = pl.BlockSpec((1, D_MODEL), lambda i: (0, 0))
    act = jax.ShapeDtypeStruct((t, D_MODEL), BF16)
    small = jax.ShapeDtypeStruct((1, D_MODEL), F32)
    return pl.pallas_call(
        body, name="merge_bwd", grid=(t // tm,),
        out_shape=[jax.ShapeDtypeStruct((t, D_MODEL), F32), act, act, act, act, act,
                   jax.ShapeDtypeStruct((2, t, D_MODEL), BF16), small, small],
        in_specs=[row, row, row, row, vec, vec, full, full, full, row, row,
                  pl.BlockSpec((tm, D_MODEL), lambda i: (i, 5)), pl.BlockSpec((tm, D_MODEL), lambda i: (i, 6))],
        out_specs=[row] * 6 + [pl.BlockSpec((2, tm, D_MODEL), lambda i: (0, i, 0)), vec, vec],
        compiler_params=_params("arbitrary"),
    )(dy, d_h2, x1, mix, g3, g2, w_out, w_cb, w_lb, pa, pb, proj, proj)


def _conv_mixer_bwd(proj, d_ya, w_short):
    t = proj.shape[0]
    rc = _row_chunk(t)

    def body(b_ref, c_ref, x_ref, dy_ref, w_ref, d_ref, dw_ref, pad, back):
        pad[pl.ds(0, PAD), :] = jnp.zeros((PAD, CB), F32)
        back[pl.ds(t, PAD), :] = jnp.zeros((PAD, CB), F32)
        for r0 in range(0, t, rc):
            rows = pl.ds(r0, rc)
            pad[pl.ds(PAD + r0, rc), :] = c_ref[rows, :].astype(F32) * x_ref[rows, :].astype(F32)
        w = w_ref[...]
        for r0 in range(0, t, rc):
            rows = pl.ds(r0, rc)
            d_y = dy_ref[rows, :].astype(F32)
            d_ref[0, rows, :] = (d_y * _conv_causal(pad, w, r0, rc, 3)).astype(BF16)
            back[rows, :] = d_y * b_ref[rows, :].astype(F32)
        taps = [jnp.zeros((1, CB), F32)] * 3
        for r0 in range(0, t, rc):
            rows = pl.ds(r0, rc)
            d_u = _conv_anticausal(back, w, r0, rc, 3)
            d_ref[1, rows, :] = (d_u * x_ref[rows, :].astype(F32)).astype(BF16)
            d_ref[2, rows, :] = (d_u * c_ref[rows, :].astype(F32)).astype(BF16)
            taps = [acc + new for acc, new in zip(taps, _conv_wgrad(back[rows, :], pad, r0, rc, 3))]
        dw_ref[...] = jnp.concatenate(taps, axis=0)

    blk = pl.BlockSpec((t, CB), lambda h: (0, h))
    return pl.pallas_call(
        body, name="conv_mixer_bwd", grid=(D_MODEL // CB,),
        out_shape=[jax.ShapeDtypeStruct((3, t, D_MODEL), BF16), jax.ShapeDtypeStruct((3, D_MODEL), F32)],
        in_specs=[_section(0, t), _section(1, t), _section(2, t), blk, pl.BlockSpec((3, CB), lambda h: (0, h))],
        out_specs=[pl.BlockSpec((3, t, CB), lambda h: (0, 0, h)), pl.BlockSpec((3, CB), lambda h: (0, h))],
        scratch_shapes=[pltpu.VMEM((t + PAD, CB), F32), pltpu.VMEM((t + PAD, CB), F32)],
        compiler_params=_params("parallel"),
    )(proj, proj, proj, d_ya, w_short)


LRU_SMALL_ROWS = 8


def _lru_bwd(proj, hl, a_all, kept, d_yb, w_conv, wa, wx, lam):
    t = proj.shape[0]
    rc = _row_chunk(t)
    vec, mat = _head_specs()

    def body(lx_ref, ly_ref, hl_ref, a_ref, kept_ref, dy_ref, wc_ref, wa_ref, wx_ref, lam_ref,
             d_ref, dwa_ref, dwx_ref, small_ref, pad, a_next, dh_s, h_prev, back, acc_a, acc_x):
        zeros = jnp.zeros((PAD, CB), F32)
        pad[pl.ds(0, PAD), :] = zeros
        h_prev[pl.ds(0, PAD), :] = zeros
        a_next[pl.ds(t, PAD), :] = zeros
        back[pl.ds(t, PAD), :] = zeros
        for r0 in range(0, t, rc):
            rows = pl.ds(r0, rc)
            pad[pl.ds(PAD + r0, rc), :] = lx_ref[rows, :].astype(F32)
            h_prev[pl.ds(PAD + r0, rc), :] = hl_ref[rows, :]
            a_next[pl.ds(PAD - 1 + r0, rc), :] = a_ref[rows, :]
            act, d_act = _gelu_and_grad(ly_ref[rows, :].astype(F32))
            d_y = dy_ref[rows, :].astype(F32)
            dh_s[rows, :] = d_y * act
            d_ref[1, rows, :] = (d_y * hl_ref[rows, :] * d_act).astype(BF16)
        wc = wc_ref[...]
        wa_m, wx_m = wa_ref[...].reshape(HEAD_DIM, HEAD_DIM), wx_ref[...].reshape(HEAD_DIM, HEAD_DIM)
        ls = _log_sigmoid(lam_ref[...])

        row = lax.broadcasted_iota(jnp.int32, (SUBLANES, CB), 0)
        groups = t // SUBLANES

        def group(i, carry):
            r = pl.multiple_of((groups - 1 - i) * SUBLANES, SUBLANES)
            a_g, b_g = a_next[pl.ds(PAD + r, SUBLANES), :], dh_s[pl.ds(r, SUBLANES), :]
            for s in (1, 2, 4):
                keep = row < SUBLANES - s
                b_g = jnp.where(keep, a_g * pltpu.roll(b_g, SUBLANES - s, 0) + b_g, b_g)
                a_g = jnp.where(keep, a_g * pltpu.roll(a_g, SUBLANES - s, 0), a_g)
            d_g = b_g + a_g * carry
            dh_s[pl.ds(r, SUBLANES), :] = d_g
            return jnp.broadcast_to(d_g[0:1, :], (SUBLANES, CB))

        lax.fori_loop(0, groups, group, jnp.zeros((SUBLANES, CB), F32))

        acc_a[...] = jnp.zeros_like(acc_a)
        acc_x[...] = jnp.zeros_like(acc_x)
        d_ba = d_bx = d_ls = jnp.zeros((1, CB), F32)
        for r0 in range(0, t, rc):
            rows = pl.ds(r0, rc)
            first = (lax.broadcasted_iota(jnp.int32, (rc, CB), 0) + r0) == 0
            xb, a = kept_ref[0, rows, :], a_ref[rows, :]
            xl, ra, ia = xb.astype(F32), kept_ref[1, rows, :].astype(F32), kept_ref[2, rows, :].astype(F32)
            a_sq = a * a
            mult = jnp.where(first, 1.0, jnp.sqrt(1.0 - a_sq))
            d_h = dh_s[rows, :]
            d_a = d_h * _rows_back(h_prev, r0, rc, 1)
            d_mult = d_h * ia * xl
            d_ia = d_h * mult * xl
            d_xl = d_h * mult * ia
            d_la = d_a * a + d_mult * jnp.where(first, 0.0, -a_sq / mult)
            d_ls = d_ls + jnp.sum(d_la * ra, axis=0, keepdims=True) * LRU_C
            d_za = d_la * (LRU_C * ls) * ra * (1.0 - ra)
            d_zx = d_ia * ia * (1.0 - ia)
            d_ba = d_ba + jnp.sum(d_za, axis=0, keepdims=True)
            d_bx = d_bx + jnp.sum(d_zx, axis=0, keepdims=True)
            d_za, d_zx = d_za.astype(BF16), d_zx.astype(BF16)
            acc_a[...] += _dot_tn(xb, d_za)
            acc_x[...] += _dot_tn(xb, d_zx)
            back[rows, :] = d_xl + _dot_nt(d_za, wa_m) + _dot_nt(d_zx, wx_m)
        taps = [jnp.zeros((1, CB), F32)] * 4
        d_bc = jnp.zeros((1, CB), F32)
        for r0 in range(0, t, rc):
            rows = pl.ds(r0, rc)
            d_ref[0, rows, :] = _conv_anticausal(back, wc, r0, rc, 4).astype(BF16)
            g = back[rows, :]
            taps = [acc + new for acc, new in zip(taps, _conv_wgrad(g, pad, r0, rc, 4))]
            d_bc = d_bc + jnp.sum(g, axis=0, keepdims=True)
        d_lam = d_ls * jax.nn.sigmoid(-lam_ref[...])
        small_ref[...] = jnp.concatenate(taps + [d_bc, d_ba, d_bx, d_lam], axis=0)
        dwa_ref[...] = acc_a[...].reshape(N_DEV, HEAD_DIM // N_DEV, HEAD_DIM).astype(BF16)
        dwx_ref[...] = acc_x[...].reshape(N_DEV, HEAD_DIM // N_DEV, HEAD_DIM).astype(BF16)

    blk = pl.BlockSpec((t, CB), lambda h: (0, h))
    gate_grad = jax.ShapeDtypeStruct((N_DEV, N_HEADS, HEAD_DIM // N_DEV, HEAD_DIM), BF16)
    return pl.pallas_call(
        body, name="lru_bwd", grid=(N_HEADS,),
        out_shape=[jax.ShapeDtypeStruct((2, t, D_MODEL), BF16), gate_grad, gate_grad,
                   jax.ShapeDtypeStruct((LRU_SMALL_ROWS, D_MODEL), F32)],
        in_specs=[_section(3, t), _section(4, t), blk, blk, pl.BlockSpec((3, t, CB), lambda h: (0, 0, h)), blk,
                  pl.BlockSpec((4, CB), lambda h: (0, h)), mat, mat, vec],
        out_specs=[pl.BlockSpec((2, t, CB), lambda h: (0, 0, h)), mat, mat,
                   pl.BlockSpec((LRU_SMALL_ROWS, CB), lambda h: (0, h))],
        scratch_shapes=[pltpu.VMEM((t + PAD, CB), F32), pltpu.VMEM((t + PAD, CB), F32), pltpu.VMEM((t, CB), F32),
                        pltpu.VMEM((t + PAD, CB), F32), pltpu.VMEM((t + PAD, CB), F32),
                        pltpu.VMEM((HEAD_DIM, HEAD_DIM), F32), pltpu.VMEM((HEAD_DIM, HEAD_DIM), F32)],
        compiler_params=_params("parallel"),
    )(proj, proj, hl, a_all, kept, d_yb, w_conv, wa, wx, lam)


def _stack_maps(halves):
    def conv(sec, part):
        return jnp.minimum(sec, 2), jnp.where(sec < 3, part, halves - 1)

    def lru(sec, part):
        return jnp.clip(sec - 3, 0, 1), jnp.where(sec < 3, 0, jnp.where(sec < 5, part, halves - 1))

    def gate(sec, part):
        return jnp.clip(sec - 5, 0, 1), jnp.where(sec < 5, 0, part)

    return conv, lru, gate


def _pick_stack(sec, refs, fn):
    @pl.when(sec < 3)
    def _():
        fn(refs[0])

    @pl.when((sec >= 3) & (sec < 5))
    def _():
        fn(refs[1])

    @pl.when(sec >= 5)
    def _():
        fn(refs[2])


def _in_proj_wgrad(h, d_conv, d_lru, d_gate):
    t = h.shape[0]
    halves, bn = 2, D_MODEL // 2
    maps = _stack_maps(halves)

    def body(h_ref, dc_ref, dl_ref, dg_ref, o_ref):
        def emit(ref):
            o_ref[...] = _dot_tn(h_ref[...], ref[...]).astype(BF16)
        _pick_stack(pl.program_id(0) // halves, (dc_ref, dl_ref, dg_ref), emit)

    def spec(m):
        def index(s):
            stack, part = m(s // halves, s % halves)
            return stack, 0, part
        return pl.BlockSpec((None, t, bn), index)

    return pl.pallas_call(
        body, name="in_proj_wgrad", grid=(7 * halves,), out_shape=jax.ShapeDtypeStruct((D_MODEL, IN_COLS), BF16),
        in_specs=[pl.BlockSpec((t, D_MODEL), lambda s: (0, 0))] + [spec(m) for m in maps],
        out_specs=pl.BlockSpec((D_MODEL, bn), lambda s: (0, s)),
        compiler_params=_params("arbitrary"),
    )(h, d_conv, d_lru, d_gate)


def _in_proj_xgrad(d_conv, d_lru, d_gate, w_in, x, dx1, g1):
    t = x.shape[0]
    tm = min(1024, t)
    maps = _stack_maps(1)

    def body(dc_ref, dl_ref, dg_ref, w_ref, x_ref, dx1_ref, g_ref, dx_ref, dgain_ref, acc):
        i, s = pl.program_id(0), pl.program_id(1)

        @pl.when((i == 0) & (s == 0))
        def _():
            dgain_ref[...] = jnp.zeros_like(dgain_ref)

        @pl.when(s == 0)
        def _():
            acc[...] = jnp.zeros_like(acc)

        def add(ref):
            acc[...] += _dot_nt(ref[...], w_ref[...])
        _pick_stack(s, (dc_ref, dl_ref, dg_ref), add)

        @pl.when(s == 6)
        def _():
            n1, r1 = _rms_fwd(x_ref[...])
            d_h = acc[...]
            dgain_ref[...] += jnp.sum(d_h * n1, axis=0, keepdims=True)
            dx_ref[...] = dx1_ref[...] + _rms_bwd(n1, r1, d_h * g_ref[...])

    def spec(m):
        def index(i, s):
            return m(s, 0)[0], i, 0
        return pl.BlockSpec((None, tm, D_MODEL), index)

    row = pl.BlockSpec((tm, D_MODEL), lambda i, s: (i, 0))
    vec = pl.BlockSpec((1, D_MODEL), lambda i, s: (0, 0))
    return pl.pallas_call(
        body, name="in_proj_xgrad", grid=(t // tm, 7),
        out_shape=[jax.ShapeDtypeStruct((t, D_MODEL), F32), jax.ShapeDtypeStruct((1, D_MODEL), F32)],
        in_specs=[spec(m) for m in maps] + [pl.BlockSpec((D_MODEL, D_MODEL), lambda i, s: (0, s)), row, row, vec],
        out_specs=[row, vec],
        scratch_shapes=[pltpu.VMEM((tm, D_MODEL), F32)],
        compiler_params=_params("arbitrary", "arbitrary"),
    )(d_conv, d_lru, d_gate, w_in, x, dx1, g1)


def _adamw(w, g, m, v):
    m = ADAM_B1 * m + (1.0 - ADAM_B1) * g
    v = ADAM_B2 * v + (1.0 - ADAM_B2) * (g * g)
    m_hat = m / (1.0 - ADAM_B1 ** ADAM_STEP)
    v_hat = v / (1.0 - ADAM_B2 ** ADAM_STEP)
    return -ADAM_LR * (m_hat / (jnp.sqrt(v_hat) + ADAM_EPS) + ADAM_WD * w), m, v


def _adam_large(w, m, v, own, others, name):
    shape = w.shape
    cols = shape[-1]
    w2, m2, v2 = (a.reshape(-1, cols) for a in (w, m, v))
    rows = w2.shape[0]
    own, others = own.reshape(4, rows, cols), others.reshape(3, rows, cols)
    rb = _row_block(rows, 512)

    def body(w_ref, m_ref, v_ref, own_ref, oth_ref, g_ref, d_ref, nm_ref, nv_ref):
        g = own_ref[...].astype(F32)
        for k in range(3):
            g = g + oth_ref[k].astype(F32)
        g_ref[...] = g
        d_ref[...], nm_ref[...], nv_ref[...] = _adamw(w_ref[...], g, m_ref[...], v_ref[...])

    blk = pl.BlockSpec((rb, cols), lambda i: (i, 0))
    res = jax.ShapeDtypeStruct((rows, cols), F32)
    outs = pl.pallas_call(
        body, name=name, grid=(rows // rb,), out_shape=[res] * 4,
        in_specs=[blk, blk, blk, pl.BlockSpec((None, rb, cols), lambda i: (0, i, 0)),
                  pl.BlockSpec((3, rb, cols), lambda i: (0, i, 0))],
        out_specs=[blk] * 4, compiler_params=_params("parallel"),
    )(w2, m2, v2, own, others)
    return [o.reshape(shape) for o in outs]


def _adam_small(ws, gs, ms, vs):
    n = len(ws)

    def body(*refs):
        w_refs, g_refs, m_refs, v_refs = (refs[i * n:(i + 1) * n] for i in range(4))
        outs = refs[4 * n:]
        for i in range(n):
            d, m, v = _adamw(w_refs[i][...], g_refs[i][...], m_refs[i][...], v_refs[i][...])
            outs[i][...], outs[n + i][...], outs[2 * n + i][...] = d, m, v

    shapes = [jax.ShapeDtypeStruct(w.shape, F32) for w in ws]
    outs = pl.pallas_call(
        body, name="adam_small", out_shape=shapes * 3,
        in_specs=[VMEM_SPEC] * (4 * n), out_specs=[VMEM_SPEC] * (3 * n), compiler_params=_params(),
    )(*ws, *gs, *ms, *vs)
    return outs[:n], outs[n:2 * n], outs[2 * n:]


def _pack_rows(pieces):
    tile = SUBLANES * LANES
    return jnp.concatenate([jnp.pad(p.reshape(-1), (0, (-p.size) % tile)).reshape(-1, LANES) for p in pieces], axis=0)


def _packed_starts(sizes):
    tile = SUBLANES * LANES
    starts = [0]
    for s in sizes:
        starts.append(starts[-1] + (s + tile - 1) // tile * SUBLANES)
    return starts


def kernel(x, norm_mix_pre, norm_mix_post, norm_ffn_pre, norm_ffn_post, w_in, conv_short_w, w_conv_branch, lru_conv_w, lru_conv_b, lru_wa, lru_ba, lru_wx, lru_bx, lru_lambda, w_lru_branch, w_out, ffn_w_up, ffn_conv_w, ffn_conv_b, ffn_w_down, loss_target, m_norm_mix_pre, m_norm_mix_post, m_norm_ffn_pre, m_norm_ffn_post, m_w_in, m_conv_short_w, m_w_conv_branch, m_lru_conv_w, m_lru_conv_b, m_lru_wa, m_lru_ba, m_lru_wx, m_lru_bx, m_lru_lambda, m_w_lru_branch, m_w_out, m_ffn_w_up, m_ffn_conv_w, m_ffn_conv_b, m_ffn_w_down, v_norm_mix_pre, v_norm_mix_post, v_norm_ffn_pre, v_norm_ffn_post, v_w_in, v_conv_short_w, v_w_conv_branch, v_lru_conv_w, v_lru_conv_b, v_lru_wa, v_lru_ba, v_lru_wx, v_lru_bx, v_lru_lambda, v_w_lru_branch, v_w_out, v_ffn_w_up, v_ffn_conv_w, v_ffn_conv_b, v_ffn_w_down):
    t = x.shape[1]
    xi, yi, ci = _position()
    me = _block_of(xi, yi, ci)
    x2, target = x[0], loss_target[0]
    shard_in, shard_up = IN_COLS // N_DEV, 2 * D_FF // N_DEV
    shard_sq, shard_down, shard_head = D_MODEL // N_DEV, D_FF // N_DEV, HEAD_DIM // N_DEV

    names = ["w_in", "lru_wa", "lru_wx", "w_conv_branch", "w_lru_branch", "w_out", "ffn_w_up", "ffn_w_down"]
    large = [w_in[0], lru_wa[0], lru_wx[0], w_conv_branch[0], w_lru_branch[0], w_out[0], ffn_w_up[0], ffn_w_down[0]]
    blocks = [_cols(shard_in), _lead, _lead, _rows(shard_sq), _rows(shard_sq), _rows(shard_sq),
              _cols(shard_up), _rows(shard_down)]
    gate_full = (N_DEV, N_HEADS, shard_head, HEAD_DIM)
    full_shapes = [(D_MODEL, IN_COLS), gate_full, gate_full, (D_MODEL, D_MODEL), (D_MODEL, D_MODEL), (D_MODEL, D_MODEL),
                   (D_MODEL, 2 * D_FF), (D_FF, D_MODEL)]
    n_now = 3
    small_sharded = [conv_short_w, lru_conv_w, lru_ba, lru_bx, ffn_conv_w]
    small_mine = _pack_rows(small_sharded)
    small_at = _packed_starts([p.size for p in small_sharded])
    *gathered, small_all = _gather_weights(large, blocks, full_shapes, small_mine, n_now)
    g_in, g_wa, g_wx = gathered[:n_now]
    later_blocks = blocks[n_now:]
    send1, recv1, later, gather_token = _gather_start(gathered[n_now:], later_blocks, "gather_start")

    def behind(token, operand):
        return operand + token[0:1, 0:1]

    def forward(lo, hi, after, tag):
        return _gather_forward(later[lo:hi], later_blocks[lo:hi], send1[4 * lo:4 * hi], recv1[4 * lo:4 * hi], after,
                               "gather_forward_" + tag)

    def finish(lo, hi, flight, after, tag):
        return _gather_finish(flight[2], later_blocks[lo:hi], flight[0], flight[1], after, "gather_finish_" + tag)

    def cols_of(r0, n, width):
        part = small_all[:, r0:r0 + n * width // LANES, :].reshape(N_DEV, n, width)
        return part.transpose(1, 0, 2).reshape(n, N_DEV * width)

    c_short = cols_of(small_at[0], 3, LANES)
    c_lru = cols_of(small_at[1], 4, LANES)
    b_a = cols_of(small_at[2], N_HEADS, shard_head).reshape(1, D_MODEL)
    b_x = cols_of(small_at[3], N_HEADS, shard_head).reshape(1, D_MODEL)
    c_ffn = cols_of(small_at[4], 3, shard_up)

    proj, h = _in_proj(x2, behind(gather_token, norm_mix_pre), g_in)
    flight_mix_w = forward(0, 3, h, "mix")
    y_a = _conv_mixer_fwd(proj, c_short)
    y_b, hl, decay, lru_kept = _lru_fwd(proj, c_lru, lru_conv_b, g_wa, b_a, g_wx, b_x, lru_lambda)
    flight_up_w = forward(3, 4, y_b, "up")
    g_cb, g_lb, g_out = finish(0, 3, flight_mix_w, y_b, "mix")
    pa, pb, merged, mix, x1, h2 = _merge(y_a, y_b, proj, x2, g_cb, g_lb, g_out, norm_mix_post, norm_ffn_pre)
    flight_down_w = forward(4, 5, h2, "down")
    (g_up,) = finish(3, 4, flight_up_w, h2, "up")
    up, act, f = _ffn_up(h2, g_up, c_ffn, ffn_conv_b)
    (g_down,) = finish(4, 5, flight_down_w, f, "down")
    dy, d_out, d_act, dg4, loss_part = _ffn_down(f, act, g_down, x1, target, norm_ffn_post)

    block_of = dict(zip(names, blocks))
    shard_shapes = {"w_in": (D_MODEL, shard_in), "w_conv_branch": (shard_sq, D_MODEL), "w_lru_branch": (shard_sq, D_MODEL),
                    "w_out": (shard_sq, D_MODEL), "lru_wa": (N_HEADS, shard_head, HEAD_DIM),
                    "lru_wx": (N_HEADS, shard_head, HEAD_DIM), "ffn_w_up": (D_MODEL, shard_up),
                    "ffn_w_down": (shard_down, D_MODEL)}

    def reduce_start(tag, grads):
        keys = list(grads)
        sums = _reduce_pair([grads[k] for k in keys], [block_of[k] for k in keys], [shard_shapes[k] for k in keys],
                            "reduce_pair_" + tag)
        return (keys,) + _exchange_chips_start(sums, "reduce_chip_start_" + tag)

    gw_down = _grad_tn(f, d_out, min(512, D_FF), "ffn_down_wgrad")
    flight_down = reduce_start("down", {"ffn_w_down": gw_down})
    gw_up, gc_ffn, gb_ffn, d_h2 = _ffn_up_bwd(up, d_act, behind(flight_down[-1], c_ffn), h2, g_up)
    flight_up = reduce_start("up", {"ffn_w_up": gw_up})
    dx1, d_mix, d_pa, d_pb, d_ya, d_yb, d_gate, dg3, dg2 = _merge_bwd(
        dy, d_h2, x1, mix, behind(flight_up[-1], norm_ffn_pre), norm_mix_post, g_out, g_cb, g_lb, pa, pb, proj)
    gw_out = _grad_tn(merged, d_mix, CB, "w_out_wgrad")
    gw_cb = _grad_tn(y_a, d_pa, CB, "w_conv_branch_wgrad")
    gw_lb = _grad_tn(y_b, d_pb, CB, "w_lru_branch_wgrad")
    flight_mix = reduce_start("mix", {"w_conv_branch": gw_cb, "w_lru_branch": gw_lb, "w_out": gw_out})
    d_conv, gc_short = _conv_mixer_bwd(proj, d_ya, behind(flight_mix[-1], c_short))
    d_lru, gw_a, gw_x, g_lru_small = _lru_bwd(proj, hl, decay, lru_kept, d_yb, c_lru, g_wa, g_wx, lru_lambda)
    early = [dg2, dg3, dg4, g_lru_small[4:5], g_lru_small[7:8], gb_ffn, gc_short, g_lru_small[0:4],
             g_lru_small[5:6], g_lru_small[6:7], gc_ffn, loss_part]
    flight_small = _small_start(_pack_rows(early), "small_start")
    gw_in = _in_proj_wgrad(h, d_conv, d_lru, d_gate)
    flight_in = reduce_start("in", {"lru_wa": gw_a, "lru_wx": gw_x, "w_in": gw_in})
    dx, dg1 = _in_proj_xgrad(d_conv, d_lru, d_gate, g_in, x2, dx1,
                             behind(flight_small[-1], behind(flight_in[-1], norm_mix_pre)))
    flight_late = _small_start(_pack_rows([dg1]), "small_start_late")

    moments ={"w_in": (m_w_in, v_w_in), "w_conv_branch": (m_w_conv_branch, v_w_conv_branch),
               "w_lru_branch": (m_w_lru_branch, v_w_lru_branch), "w_out": (m_w_out, v_w_out),
               "lru_wa": (m_lru_wa, v_lru_wa), "lru_wx": (m_lru_wx, v_lru_wx), "ffn_w_up": (m_ffn_w_up, v_ffn_w_up),
               "ffn_w_down": (m_ffn_w_down, v_ffn_w_down)}
    weights = {"w_in": w_in, "w_conv_branch": w_conv_branch, "w_lru_branch": w_lru_branch, "w_out": w_out,
               "lru_wa": lru_wa, "lru_wx": lru_wx, "ffn_w_up": ffn_w_up, "ffn_w_down": ffn_w_down}
    out_g, out_d, out_m, out_v = {}, {}, {}, {}

    after = flight_late[-1]
    for tag, (keys, send, recv, sums, lands, _) in (("down", flight_down), ("up", flight_up), ("mix", flight_mix),
                                                    ("in", flight_in)):
        sums, others = _exchange_chips_wait(send, recv, sums, lands, after, "reduce_chip_wait_" + tag)
        for k, own, oth in zip(keys, sums, others):
            out_g[k], out_d[k], out_m[k], out_v[k] = _adam_large(weights[k], *moments[k], own, oth, "adam_" + k)
        after = out_d[keys[-1]]

    total, total_late = _small_sum([_small_wait(*flight_small[:4], after, "small_wait"),
                                    _small_wait(*flight_late[:4], after, "small_wait_late")], me)
    sizes = [p.size for p in early]
    starts = _packed_starts(sizes)

    def piece(i, shape):
        if i == 0:
            return total_late.reshape(-1)[:D_MODEL].reshape(shape)
        return total[starts[i - 1]:starts[i]].reshape(-1)[:sizes[i - 1]].reshape(shape)

    loss = total[starts[11], 0]

    def col_shard(full, width):
        return lax.dynamic_slice_in_dim(full, me * width, width, axis=1)

    def head_shard(full):
        return lax.dynamic_slice_in_dim(full.reshape(N_HEADS, HEAD_DIM), me * shard_head, shard_head, axis=1)

    small_names = ["norm_mix_pre", "norm_mix_post", "norm_ffn_pre", "norm_ffn_post", "lru_conv_b", "lru_lambda",
                   "ffn_conv_b", "conv_short_w", "lru_conv_w", "lru_ba", "lru_bx", "ffn_conv_w"]
    small_g = [piece(0, (1, D_MODEL)), piece(1, (1, D_MODEL)), piece(2, (1, D_MODEL)), piece(3, (1, D_MODEL)),
               piece(4, (1, D_MODEL)), piece(5, (1, D_MODEL)), piece(6, (1, 2 * D_FF)),
               col_shard(piece(7, (3, D_MODEL)), LANES), col_shard(piece(8, (4, D_MODEL)), LANES),
               head_shard(piece(9, (1, D_MODEL))), head_shard(piece(10, (1, D_MODEL))),
               col_shard(piece(11, (3, 2 * D_FF)), shard_up)]
    small_w = [norm_mix_pre, norm_mix_post, norm_ffn_pre, norm_ffn_post, lru_conv_b, lru_lambda, ffn_conv_b,
               conv_short_w[0], lru_conv_w[0], lru_ba[0], lru_bx[0], ffn_conv_w[0]]
    small_m = [m_norm_mix_pre, m_norm_mix_post, m_norm_ffn_pre, m_norm_ffn_post, m_lru_conv_b, m_lru_lambda,
               m_ffn_conv_b, m_conv_short_w[0], m_lru_conv_w[0], m_lru_ba[0], m_lru_bx[0], m_ffn_conv_w[0]]
    small_v = [v_norm_mix_pre, v_norm_mix_post, v_norm_ffn_pre, v_norm_ffn_post, v_lru_conv_b, v_lru_lambda,
               v_ffn_conv_b, v_conv_short_w[0], v_lru_conv_w[0], v_lru_ba[0], v_lru_bx[0], v_ffn_conv_w[0]]
    s_d, s_m, s_v = _adam_small(small_w, small_g, small_m, small_v)
    for i, name in enumerate(small_names):
        shape = small_w[i].shape if i < 7 else (1,) + small_w[i].shape
        out_g[name] = small_g[i].reshape(shape)
        out_d[name], out_m[name], out_v[name] = s_d[i].reshape(shape), s_m[i].reshape(shape), s_v[i].reshape(shape)

    order = ["norm_mix_pre", "norm_mix_post", "norm_ffn_pre", "norm_ffn_post", "w_in", "conv_short_w", "w_conv_branch",
             "lru_conv_w", "lru_conv_b", "lru_wa", "lru_ba", "lru_wx", "lru_bx", "lru_lambda", "w_lru_branch", "w_out",
             "ffn_w_up", "ffn_conv_w", "ffn_conv_b", "ffn_w_down"]
    return (loss, dx.reshape(1, t, D_MODEL), *[out_g[k] for k in order], *[out_d[k] for k in order],
            *[out_m[k] for k in order], *[out_v[k] for k in order])
```

```python
import functools
import math

import jax
import jax.numpy as jnp
from jax import lax
from jax.experimental import pallas as pl
from jax.experimental.pallas import tpu as pltpu

F32 = jnp.float32
BF16 = jnp.bfloat16
MESH = pl.DeviceIdType.MESH

N_DEV = 8
D_MODEL = 1024
N_HEADS = 4
HEAD_DIM = D_MODEL // N_HEADS
D_FF = 3 * D_MODEL
IN_COLS = 7 * D_MODEL
LRU_C = 8.0
RMS_EPS = 1e-6
ADAM_LR = 0.001
ADAM_B1 = 0.9
ADAM_B2 = 0.999
ADAM_EPS = 1e-08
ADAM_WD = 0.01
ADAM_STEP = 10
GELU_K = math.sqrt(2.0 / math.pi)
GELU_C = 0.044715

LANES = 128
SUBLANES = 8
PAD = SUBLANES
VMEM_LIMIT = 56 * 1024 * 1024
CB = 256

HBM_SPEC = pl.BlockSpec(memory_space=pltpu.HBM)
SEM_SPEC = pl.BlockSpec(memory_space=pltpu.SEMAPHORE)
DATAFLOW_EFFECT = pltpu.SideEffectType.DATAFLOW_SIDE_EFFECTING
VMEM_SPEC = pl.BlockSpec(memory_space=pltpu.VMEM)


def _params(*sem):
    if sem:
        return pltpu.CompilerParams(dimension_semantics=sem, vmem_limit_bytes=VMEM_LIMIT)
    return pltpu.CompilerParams(vmem_limit_bytes=VMEM_LIMIT)


def _row_chunk(t):
    return min(256, t)


def _row_block(rows, cap):
    return next(rb for rb in range(min(cap, rows), 0, -16) if rows % rb == 0)


def _gelu(x):
    return 0.5 * x * (1.0 + jnp.tanh(GELU_K * (x + GELU_C * x * x * x)))


def _gelu_and_grad(x):
    t = jnp.tanh(GELU_K * (x + GELU_C * x * x * x))
    g = 0.5 * x * (1.0 + t)
    dg = 0.5 * (1.0 + t) + 0.5 * x * (1.0 - t * t) * GELU_K * (1.0 + 3.0 * GELU_C * x * x)
    return g, dg


def _expm1_neg(x):
    series = x * (1.0 + x * (0.5 + x * (1.0 / 6.0 + x * (1.0 / 24.0 + x * (1.0 / 120.0)))))
    return jnp.where(x > -0.05, series, jnp.exp(x) - 1.0)


def _log_sigmoid(x):
    return jnp.minimum(x, 0.0) - jnp.log1p(jnp.exp(-jnp.abs(x)))


def _dot(a, b):
    return jnp.dot(a, b, preferred_element_type=F32)


def _dot_nt(a, b):
    return lax.dot_general(a, b, (((1,), (1,)), ((), ())), preferred_element_type=F32)


def _dot_tn(a, b):
    return lax.dot_general(a, b, (((0,), (0,)), ((), ())), preferred_element_type=F32)


def _rms_fwd(x):
    r = lax.rsqrt(jnp.mean(x * x, axis=-1, keepdims=True) + RMS_EPS)
    return x * r, r


def _rms_bwd(n, r, gdy):
    return r * (gdy - n * jnp.mean(n * gdy, axis=-1, keepdims=True))


def _rows_back(pad_ref, r0, rows, j):
    cur = pad_ref[pl.ds(PAD + r0, rows), :]
    if j == 0:
        return cur
    before = pad_ref[pl.ds(PAD + r0 - SUBLANES, SUBLANES), :]
    row = lax.broadcasted_iota(jnp.int32, before.shape, 0)
    rolled = pltpu.roll(cur, j, 0)
    top = jnp.where(row < j, pltpu.roll(before, j, 0), rolled[0:SUBLANES, :])
    return jnp.concatenate([top, rolled[SUBLANES:, :]], axis=0)


def _rows_ahead(pad_ref, r0, rows, j):
    cur = pad_ref[pl.ds(r0, rows), :]
    if j == 0:
        return cur
    after = pad_ref[pl.ds(r0 + rows, SUBLANES), :]
    row = lax.broadcasted_iota(jnp.int32, after.shape, 0)
    rolled = pltpu.roll(cur, rows - j, 0)
    bottom = jnp.where(row >= SUBLANES - j, pltpu.roll(after, SUBLANES - j, 0), rolled[rows - SUBLANES:, :])
    return jnp.concatenate([rolled[:rows - SUBLANES, :], bottom], axis=0)


def _conv_causal(pad_ref, w, r0, rows, taps):
    acc = None
    for k in range(taps):
        term = w[k:k + 1, :] * _rows_back(pad_ref, r0, rows, taps - 1 - k)
        acc = term if acc is None else acc + term
    return acc


def _conv_anticausal(pad_ref, w, r0, rows, taps):
    acc = None
    for k in range(taps):
        term = w[k:k + 1, :] * _rows_ahead(pad_ref, r0, rows, taps - 1 - k)
        acc = term if acc is None else acc + term
    return acc


def _conv_wgrad(g, xpad_ref, r0, rows, taps):
    return [jnp.sum(g * _rows_back(xpad_ref, r0, rows, taps - 1 - k), axis=0, keepdims=True) for k in range(taps)]


def _position():
    return lax.axis_index("x"), lax.axis_index("y"), lax.axis_index("c")


def _block_of(x, y, c):
    return 4 * x + 2 * y + c


def _chip(x, y, k):
    return (x + (k & 1)) % 2, (y + (k >> 1)) % 2


def _cols(width):
    def at(ref, d):
        return ref.at[:, pl.ds(pl.multiple_of(d * width, LANES), width)]
    return at


def _rows(height):
    def at(ref, d):
        return ref.at[pl.ds(pl.multiple_of(d * height, 16), height), :]
    return at


def _lead(ref, d):
    return ref.at[d]


def _gather_weights(shards, blocks, full_shapes, small, n_now, tokens, gain):
    n = len(shards)
    small_rows = small.shape[0]
    t = tokens.shape[0]
    rc = min(512, t)

    def body(*refs):
        ins, small_in, x_ref, g_ref = refs[:n], refs[n], refs[n + 1], refs[n + 2]
        outs, small_out, proj_ref, h_ref = refs[n + 3:2 * n + 3], refs[2 * n + 3], refs[2 * n + 4], refs[2 * n + 5]
        stage = refs[2 * n + 6:3 * n + 6]
        w_buf, p_buf, send, recv, local, w_sem, p_sem = refs[3 * n + 6:]
        x, y, c = _position()
        me = _block_of(x, y, c)
        sibling = (x, y, 1 - c)

        for a in range(n):
            stage[a][...] = ins[a][...].astype(BF16)
        for r0 in range(0, t, rc):
            normed, _ = _rms_fwd(x_ref[pl.ds(r0, rc), :])
            h_ref[pl.ds(r0, rc), :] = (normed * g_ref[...]).astype(BF16)
        stores = []

        def project(w_ref, block):
            i = len(stores)
            if i >= 2:
                stores[i - 2].wait()
            for r0 in range(0, t, rc):
                p_buf[i % 2, pl.ds(r0, rc), :] = _dot(h_ref[pl.ds(r0, rc), :], w_ref[...]).astype(BF16)
            st = pltpu.make_async_copy(p_buf.at[i % 2], blocks[0](proj_ref, block), p_sem.at[i % 2])
            st.start()
            stores.append(st)

        def project_landed(block):
            ld = pltpu.make_async_copy(blocks[0](outs[0], block), w_buf, w_sem)
            ld.start()
            ld.wait()
            project(w_buf, block)

        def copy(a, k, block, to, src=None):
            dst = blocks[a](outs[a], block)
            return pltpu.make_async_remote_copy(
                src_ref=dst if src is None else src, dst_ref=dst, send_sem=send.at[a, k], recv_sem=recv.at[a, k],
                device_id=to, device_id_type=MESH)

        def small_copy(k):
            px, py, pc = (x + (k & 1)) % 2, (y + ((k >> 1) & 1)) % 2, (c + (k >> 2)) % 2
            return pltpu.make_async_remote_copy(
                src_ref=small_in, dst_ref=small_out.at[me], send_sem=send.at[n_now, k - 1], recv_sem=recv.at[n_now, k - 1],
                device_id=(px, py, pc), device_id_type=MESH)

        def small_arrival(k):
            px, py, pc = (x + (k & 1)) % 2, (y + ((k >> 1) & 1)) % 2, (c + (k >> 2)) % 2
            return pltpu.make_async_remote_copy(
                src_ref=small_in, dst_ref=small_out.at[_block_of(px, py, pc)], send_sem=send.at[n_now, k - 1],
                recv_sem=recv.at[n_now, k - 1], device_id=(px, py, pc), device_id_type=MESH)

        small_out[me] = small_in[...]
        small_sends = [small_copy(k) for k in range(1, N_DEV)]
        for cp in small_sends:
            cp.start()

        mine, first, passed = [], [], []
        for a in range(n):
            own = pltpu.make_async_copy(stage[a], blocks[a](outs[a], me), local.at[a])
            own.start()
            mine.append(own)
            if a >= n_now:
                continue
            sends = [copy(a, 0, me, sibling, src=stage[a])]
            sends += [copy(a, k, me, (*_chip(x, y, k), c), src=stage[a]) for k in (1, 2, 3)]
            for cp in sends:
                cp.start()
            first += sends
        project(stage[0], me)
        copy(0, 0, _block_of(x, y, 1 - c), (x, y, c)).wait_recv()
        project_landed(_block_of(x, y, 1 - c))
        for a in range(n_now):
            for k in (1, 2, 3):
                landed = _block_of(*_chip(x, y, k), c)
                copy(a, k, landed, (x, y, c)).wait_recv()
                fwd = copy(a, 3 + k, landed, sibling)
                fwd.start()
                passed.append(fwd)
                if a == 0:
                    project_landed(landed)
        for a in range(n_now):
            if a > 0:
                copy(a, 0, _block_of(x, y, 1 - c), (x, y, c)).wait_recv()
            for k in (1, 2, 3):
                copy(a, 3 + k, _block_of(*_chip(x, y, k), 1 - c), (x, y, c)).wait_recv()
                if a == 0:
                    project_landed(_block_of(*_chip(x, y, k), 1 - c))
        for k in range(1, N_DEV):
            small_arrival(k).wait_recv()
        for cp in first + passed + small_sends:
            cp.wait_send()
        for done in mine + stores[-2:]:
            done.wait()

    out_shape = [jax.ShapeDtypeStruct(s, BF16) for s in full_shapes]
    out_shape += [jax.ShapeDtypeStruct((N_DEV, small_rows, LANES), F32), jax.ShapeDtypeStruct((t, full_shapes[0][1]), BF16),
                  jax.ShapeDtypeStruct(tokens.shape, BF16)]
    return pl.pallas_call(
        body, name="gather_weights", out_shape=out_shape,
        in_specs=[VMEM_SPEC] * (n + 3), out_specs=[HBM_SPEC] * n + [VMEM_SPEC, HBM_SPEC, VMEM_SPEC],
        scratch_shapes=[pltpu.VMEM(s.shape, BF16) for s in shards]
        + [pltpu.VMEM(shards[0].shape, BF16), pltpu.VMEM((2, t, shards[0].shape[1]), BF16),
           pltpu.SemaphoreType.DMA((n_now + 1, 7)), pltpu.SemaphoreType.DMA((n_now + 1, 7)),
           pltpu.SemaphoreType.DMA((n,)), pltpu.SemaphoreType.DMA(()), pltpu.SemaphoreType.DMA((2,))],
        compiler_params=_params(),
    )(*shards, small, tokens, gain)


def _gather_first(full, blocks, send, recv):
    x, y, c = _position()
    me = _block_of(x, y, c)
    peers = [(x, y, 1 - c)] + [(*_chip(x, y, k), c) for k in (1, 2, 3)]

    def copy(a, k, block):
        at = blocks[a](full[a], block)
        return pltpu.make_async_remote_copy(src_ref=at, dst_ref=at, send_sem=send[4 * a + k], recv_sem=recv[4 * a + k],
                                            device_id=peers[k], device_id_type=MESH)

    sends = [copy(a, k, me) for a in range(len(full)) for k in range(4)]
    arrivals = [copy(a, k, _block_of(*peers[k])) for a in range(len(full)) for k in range(4)]
    return sends, arrivals


def _gather_second(full, blocks, send, recv):
    x, y, c = _position()

    def copy(a, k, cc):
        at = blocks[a](full[a], _block_of(*_chip(x, y, k), cc))
        return pltpu.make_async_remote_copy(src_ref=at, dst_ref=at, send_sem=send[3 * a + k - 1],
                                            recv_sem=recv[3 * a + k - 1], device_id=(x, y, 1 - c), device_id_type=MESH)

    sends = [copy(a, k, c) for a in range(len(full)) for k in (1, 2, 3)]
    arrivals = [copy(a, k, 1 - c) for a in range(len(full)) for k in (1, 2, 3)]
    return sends, arrivals


def _split_call(body, name, arrays, sems_in, n_sems_out, after=None, token=False):
    n, m = len(arrays), len(sems_in)

    def kernel_body(*refs):
        outs = refs[n + m + (after is not None):]
        body(refs[:n], refs[n:n + m], outs[:n_sems_out])
        if token:
            outs[-1][...] = jnp.zeros_like(outs[-1])

    extra_in = [] if after is None else [after]
    outs = pl.pallas_call(
        kernel_body, name=name,
        out_shape=(*[pltpu.SemaphoreType.DMA(())] * n_sems_out, *[pltpu.HBM(a.shape, a.dtype) for a in arrays],
                   *([jax.ShapeDtypeStruct((SUBLANES, LANES), F32)] if token else [])),
        in_specs=[HBM_SPEC] * n + [SEM_SPEC] * m + [pl.BlockSpec(memory_space=pl.ANY)] * len(extra_in),
        out_specs=(*[SEM_SPEC] * n_sems_out, *[HBM_SPEC] * n, *([VMEM_SPEC] if token else [])),
        input_output_aliases={i: n_sems_out + i for i in range(n)},
        compiler_params=pltpu.CompilerParams(has_side_effects=DATAFLOW_EFFECT),
    )(*[pltpu.with_memory_space_constraint(a, pltpu.HBM) for a in arrays], *sems_in, *extra_in)
    sems, rest = list(outs[:n_sems_out]), list(outs[n_sems_out:])
    return (sems, rest[:n], rest[n]) if token else (sems, rest[:n])


def _gather_start(full, blocks, name):
    n = len(full)

    def body(arrays, _, sems):
        for cp in _gather_first(arrays, blocks, sems[:4 * n], sems[4 * n:])[0]:
            cp.start()

    sems, arrays, token = _split_call(body, name, full, [], 8 * n, token=True)
    return sems[:4 * n], sems[4 * n:], arrays, token


def _gather_forward(full, blocks, send_first, recv_first, after, name):
    n = len(full)

    def body(arrays, sems_in, sems):
        sends, arrivals = _gather_first(arrays, blocks, sems_in[:4 * n], sems_in[4 * n:])
        for cp in arrivals:
            cp.wait_recv()
        for cp in _gather_second(arrays, blocks, sems[:3 * n], sems[3 * n:])[0]:
            cp.start()
        for cp in sends:
            cp.wait_send()

    sems, arrays = _split_call(body, name, full, [*send_first, *recv_first], 6 * n, after=after)
    return sems[:3 * n], sems[3 * n:], arrays


def _gather_finish(full, blocks, send_second, recv_second, after, name):
    n = len(full)

    def body(arrays, sems_in, _):
        sends, arrivals = _gather_second(arrays, blocks, sems_in[:3 * n], sems_in[3 * n:])
        for cp in sends:
            cp.wait_send()
        for cp in arrivals:
            cp.wait_recv()

    return _split_call(body, name, full, [*send_second, *recv_second], 0, after=after)[1]


def _reduce_pair(grads, blocks, shard_shapes, name):
    n = len(grads)

    def body(*refs):
        ins, outs = refs[:n], refs[n:2 * n]
        got, own = refs[2 * n:3 * n], refs[3 * n:4 * n]
        send, recv, local = refs[4 * n:]
        x, y, c = _position()
        copies, loads = [], []
        for a in range(n):
            for k in range(4):
                chip = _chip(x, y, k)
                cp = pltpu.make_async_remote_copy(
                    src_ref=blocks[a](ins[a], _block_of(*chip, 1 - c)), dst_ref=got[a].at[k],
                    send_sem=send.at[a, k], recv_sem=recv.at[a, k], device_id=(x, y, 1 - c), device_id_type=MESH)
                cp.start()
                copies.append(cp)
                ld = pltpu.make_async_copy(blocks[a](ins[a], _block_of(*chip, c)), own[a].at[k], local.at[a, k])
                ld.start()
                loads.append(ld)
        for a in range(n):
            for k in range(4):
                loads[4 * a + k].wait()
                copies[4 * a + k].wait_recv()
                outs[a][k] = (own[a][k].astype(F32) + got[a][k].astype(F32)).astype(BF16)
        for cp in copies:
            cp.wait_send()

    slots = [(4,) + tuple(s) for s in shard_shapes]
    return pl.pallas_call(
        body, name=name, out_shape=[jax.ShapeDtypeStruct(s, BF16) for s in slots],
        in_specs=[HBM_SPEC] * n, out_specs=[VMEM_SPEC] * n,
        scratch_shapes=[pltpu.VMEM(s, BF16) for s in slots] * 2
        + [pltpu.SemaphoreType.DMA((n, 4)), pltpu.SemaphoreType.DMA((n, 4)), pltpu.SemaphoreType.DMA((n, 4))],
        compiler_params=_params(),
    )(*grads)


def _chip_copies(sums, lands, send, recv):
    x, y, c = _position()
    return [pltpu.make_async_remote_copy(
        src_ref=sums[a].at[k], dst_ref=lands[a].at[k - 1], send_sem=send[3 * a + k - 1], recv_sem=recv[3 * a + k - 1],
        device_id=(*_chip(x, y, k), c), device_id_type=MESH) for a in range(len(sums)) for k in (1, 2, 3)]


def _exchange_chips_start(pair_sums, name):
    n = len(pair_sums)
    lands = [pltpu.with_memory_space_constraint(lax.empty((3,) + tuple(p.shape[1:]), BF16), pltpu.HBM) for p in pair_sums]

    def body(*refs):
        sums, zones = refs[:n], refs[n:2 * n]
        send, recv = refs[2 * n:5 * n], refs[5 * n:8 * n]
        token = refs[-1]
        for cp in _chip_copies(sums, zones, send, recv):
            cp.start()
        token[...] = jnp.zeros_like(token)

    outs = pl.pallas_call(
        body, name=name,
        out_shape=(*[pltpu.SemaphoreType.DMA(())] * (6 * n),
                   *[pltpu.HBM(p.shape, BF16) for p in pair_sums], *[pltpu.HBM(z.shape, BF16) for z in lands],
                   jax.ShapeDtypeStruct((SUBLANES, LANES), F32)),
        in_specs=[HBM_SPEC] * (2 * n), out_specs=(*[SEM_SPEC] * (6 * n), *[HBM_SPEC] * (2 * n), VMEM_SPEC),
        input_output_aliases={i: 6 * n + i for i in range(2 * n)},
        compiler_params=pltpu.CompilerParams(has_side_effects=DATAFLOW_EFFECT),
    )(*[pltpu.with_memory_space_constraint(p, pltpu.HBM) for p in pair_sums], *lands)
    return outs[:3 * n], outs[3 * n:6 * n], outs[6 * n:7 * n], outs[7 * n:8 * n], outs[-1]


def _exchange_chips_wait(send, recv, sums, lands, after, name):
    n = len(sums)

    def body(*refs):
        sums_in, zones = refs[:n], refs[n:2 * n]
        send_in, recv_in = refs[2 * n:5 * n], refs[5 * n:8 * n]
        for cp in _chip_copies(sums_in, zones, send_in, recv_in):
            cp.wait_send()
            cp.wait_recv()

    outs = pl.pallas_call(
        body, name=name,
        out_shape=(*[pltpu.HBM(p.shape, BF16) for p in sums], *[pltpu.HBM(z.shape, BF16) for z in lands]),
        in_specs=[HBM_SPEC] * (2 * n) + [SEM_SPEC] * (6 * n) + [pl.BlockSpec(memory_space=pl.ANY)],
        out_specs=[HBM_SPEC] * (2 * n), input_output_aliases={i: i for i in range(2 * n)},
        compiler_params=pltpu.CompilerParams(has_side_effects=DATAFLOW_EFFECT),
    )(*sums, *lands, *send, *recv, after)
    return outs[:n], outs[n:]


def _small_copies(mine, land, send, recv):
    x, y, c = _position()
    me = _block_of(x, y, c)

    def peer(k):
        return (x + (k & 1)) % 2, (y + ((k >> 1) & 1)) % 2, (c + (k >> 2)) % 2

    def copy(k, slot):
        return pltpu.make_async_remote_copy(src_ref=mine, dst_ref=land.at[slot], send_sem=send[k - 1], recv_sem=recv[k - 1],
                                            device_id=peer(k), device_id_type=MESH)

    return [copy(k, me) for k in range(1, N_DEV)], [copy(k, _block_of(*peer(k))) for k in range(1, N_DEV)]


def _small_start(part, name):
    land = jnp.zeros((N_DEV,) + part.shape, F32)

    def body(arrays, _, sems):
        for cp in _small_copies(arrays[0], arrays[1], sems[:7], sems[7:])[0]:
            cp.start()

    sems, arrays, token = _split_call(body, name, [part, land], [], 14, token=True)
    return sems[:7], sems[7:], arrays[0], arrays[1], token


def _small_wait(send, recv, part, land, after, name):
    def body(arrays, sems_in, _):
        sends, arrivals = _small_copies(arrays[0], arrays[1], sems_in[:7], sems_in[7:])
        for cp in sends:
            cp.wait_send()
        for cp in arrivals:
            cp.wait_recv()

    return _split_call(body, name, [part, land], [*send, *recv], 0, after=after)[1]


def _small_sum(pairs, me):
    n = len(pairs)

    def body(me_ref, *refs):
        for i in range(n):
            mine, land, out = refs[2 * i], refs[2 * i + 1], refs[2 * n + i]
            total = jnp.zeros(mine.shape, F32)
            for d in range(N_DEV):
                total = total + land[d] + jnp.where(me_ref[0] == d, mine[...], 0.0)
            out[...] = total

    flat = [a for pair in pairs for a in pair]
    return pl.pallas_call(
        body, name="small_sum", out_shape=[jax.ShapeDtypeStruct(mine.shape, F32) for mine, _ in pairs],
        in_specs=[pl.BlockSpec(memory_space=pltpu.SMEM)] + [VMEM_SPEC] * (2 * n), out_specs=[VMEM_SPEC] * n,
        compiler_params=_params(),
    )(me.reshape(1).astype(jnp.int32), *flat)


def _section(s, t):
    return pl.BlockSpec((t, CB), lambda h, s=s: (0, s * (D_MODEL // CB) + h))


def _conv_mixer_fwd(proj, w_short):
    t = proj.shape[0]
    rc = _row_chunk(t)

    def body(b_ref, c_ref, x_ref, w_ref, y_ref, pad):
        pad[pl.ds(0, PAD), :] = jnp.zeros((PAD, CB), F32)
        for r0 in range(0, t, rc):
            rows = pl.ds(r0, rc)
            pad[pl.ds(PAD + r0, rc), :] = c_ref[rows, :].astype(F32) * x_ref[rows, :].astype(F32)
        w = w_ref[...]
        for r0 in range(0, t, rc):
            rows = pl.ds(r0, rc)
            y_ref[rows, :] = (b_ref[rows, :].astype(F32) * _conv_causal(pad, w, r0, rc, 3)).astype(BF16)

    return pl.pallas_call(
        body, name="conv_mixer_fwd", grid=(D_MODEL // CB,),
        out_shape=jax.ShapeDtypeStruct((t, D_MODEL), BF16),
        in_specs=[_section(0, t), _section(1, t), _section(2, t), pl.BlockSpec((3, CB), lambda h: (0, h))],
        out_specs=pl.BlockSpec((t, CB), lambda h: (0, h)),
        scratch_shapes=[pltpu.VMEM((t + PAD, CB), F32)],
        compiler_params=_params("parallel"),
    )(proj, proj, proj, w_short)


def _lru_gates(xl, wa, ba, wx, bx, ls, first_row):
    xb = xl.astype(BF16)
    ra = jax.nn.sigmoid(_dot(xb, wa) + ba)
    ia = jax.nn.sigmoid(_dot(xb, wx) + bx)
    la = LRU_C * ra * ls
    a = jnp.exp(la)
    one_minus = -_expm1_neg(2.0 * la)
    mult = jnp.where(first_row, 1.0, jnp.sqrt(one_minus))
    return xb, ra, ia, a, one_minus, mult


def _head_specs():
    vec = pl.BlockSpec((1, CB), lambda h: (0, h))
    mat = pl.BlockSpec((N_DEV, None, HEAD_DIM // N_DEV, HEAD_DIM), lambda h: (0, h, 0, 0))
    return vec, mat


def _lru_fwd(proj, w_conv, b_conv, wa, ba, wx, bx, lam):
    t = proj.shape[0]
    rc = _row_chunk(t)
    vec, mat = _head_specs()

    def body(lx_ref, ly_ref, wc_ref, bc_ref, wa_ref, ba_ref, wx_ref, bx_ref, lam_ref, yb_ref, hl_ref, a_ref, kept_ref,
             pad, u_s):
        pad[pl.ds(0, PAD), :] = jnp.zeros((PAD, CB), F32)
        for r0 in range(0, t, rc):
            pad[pl.ds(PAD + r0, rc), :] = lx_ref[pl.ds(r0, rc), :].astype(F32)
        wc, bc = wc_ref[...], bc_ref[...]
        wa_m, wx_m = wa_ref[...].reshape(HEAD_DIM, HEAD_DIM), wx_ref[...].reshape(HEAD_DIM, HEAD_DIM)
        ls = _log_sigmoid(lam_ref[...])
        for r0 in range(0, t, rc):
            rows = pl.ds(r0, rc)
            xl = _conv_causal(pad, wc, r0, rc, 4) + bc
            first = (lax.broadcasted_iota(jnp.int32, (rc, CB), 0) + r0) == 0
            xb, ra, ia, a, _, mult = _lru_gates(xl, wa_m, ba_ref[...], wx_m, bx_ref[...], ls, first)
            a_ref[rows, :] = a
            u_s[rows, :] = mult * (ia * xl)
            kept_ref[0, rows, :] = xb
            kept_ref[1, rows, :] = ra.astype(BF16)
            kept_ref[2, rows, :] = ia.astype(BF16)

        row = lax.broadcasted_iota(jnp.int32, (SUBLANES, CB), 0)

        def group(g, carry):
            r = pl.multiple_of(g * SUBLANES, SUBLANES)
            a_g, b_g = a_ref[pl.ds(r, SUBLANES), :], u_s[pl.ds(r, SUBLANES), :]
            for s in (1, 2, 4):
                keep = row >= s
                b_g = jnp.where(keep, a_g * pltpu.roll(b_g, s, 0) + b_g, b_g)
                a_g = jnp.where(keep, a_g * pltpu.roll(a_g, s, 0), a_g)
            h_g = b_g + a_g * carry
            hl_ref[pl.ds(r, SUBLANES), :] = h_g
            return jnp.broadcast_to(h_g[SUBLANES - 1:SUBLANES, :], (SUBLANES, CB))

        lax.fori_loop(0, t // SUBLANES, group, jnp.zeros((SUBLANES, CB), F32))
        for r0 in range(0, t, rc):
            rows = pl.ds(r0, rc)
            yb_ref[rows, :] = (hl_ref[rows, :] * _gelu(ly_ref[rows, :].astype(F32))).astype(BF16)

    blk = pl.BlockSpec((t, CB), lambda h: (0, h))
    res = jax.ShapeDtypeStruct((t, D_MODEL), F32)
    return pl.pallas_call(
        body, name="lru_fwd", grid=(N_HEADS,),
        out_shape=[jax.ShapeDtypeStruct((t, D_MODEL), BF16), res, res, jax.ShapeDtypeStruct((3, t, D_MODEL), BF16)],
        in_specs=[_section(3, t), _section(4, t), pl.BlockSpec((4, CB), lambda h: (0, h)), vec, mat, vec, mat, vec, vec],
        out_specs=[blk, blk, blk, pl.BlockSpec((3, t, CB), lambda h: (0, 0, h))],
        scratch_shapes=[pltpu.VMEM((t + PAD, CB), F32), pltpu.VMEM((t, CB), F32)],
        compiler_params=_params("parallel"),
    )(proj, proj, w_conv, b_conv, wa, ba, wx, bx, lam)


def _merge(y_a, y_b, proj, x, w_cb, w_lb, w_out, g2, g3):
    t = x.shape[0]
    tm = min(256, t)

    def body(ya_ref, yb_ref, gc_ref, gl_ref, x_ref, wcb_ref, wlb_ref, wo_ref, g2_ref, g3_ref,
             pa_ref, pb_ref, mg_ref, mix_ref, x1_ref, h2_ref):
        pa = _dot(ya_ref[...], wcb_ref[...]).astype(BF16)
        pb = _dot(yb_ref[...], wlb_ref[...]).astype(BF16)
        pa_ref[...] = pa
        pb_ref[...] = pb
        merged = (jax.nn.sigmoid(gc_ref[...].astype(F32)) * pa.astype(F32)
                  + jax.nn.sigmoid(gl_ref[...].astype(F32)) * pb.astype(F32)).astype(BF16)
        mg_ref[...] = merged
        mix = _dot(merged, wo_ref[...])
        mix_ref[...] = mix
        n2, _ = _rms_fwd(mix)
        x1 = x_ref[...] + n2 * g2_ref[...]
        x1_ref[...] = x1
        n3, _ = _rms_fwd(x1)
        h2_ref[...] = (n3 * g3_ref[...]).astype(BF16)

    row = pl.BlockSpec((tm, D_MODEL), lambda i: (i, 0))
    full = pl.BlockSpec((D_MODEL, D_MODEL), lambda i: (0, 0))
    vec = pl.BlockSpec((1, D_MODEL), lambda i: (0, 0))
    act = jax.ShapeDtypeStruct((t, D_MODEL), BF16)
    res = jax.ShapeDtypeStruct((t, D_MODEL), F32)
    return pl.pallas_call(
        body, name="merge_fwd", grid=(t // tm,), out_shape=[act, act, act, res, res, act],
        in_specs=[row, row, pl.BlockSpec((tm, D_MODEL), lambda i: (i, 5)), pl.BlockSpec((tm, D_MODEL), lambda i: (i, 6)),
                  row, full, full, full, vec, vec],
        out_specs=[row] * 6,
        compiler_params=_params("parallel"),
    )(y_a, y_b, proj, proj, x, w_cb, w_lb, w_out, g2, g3)


N_FF_BLOCKS = D_FF // CB


def _ffn_up(h2, w_up, w_conv, b_conv):
    t = h2.shape[0]
    rc = _row_chunk(t)
    nb = N_FF_BLOCKS

    def body(h_ref, w_ref, c_ref, b_ref, up_ref, act_ref, f_ref, pad, gate):
        k = pl.program_id(1)
        pad[pl.ds(0, PAD), :] = jnp.zeros((PAD, CB), F32)
        for r0 in range(0, t, rc):
            rows = pl.ds(r0, rc)
            up = _dot(h_ref[rows, :], w_ref[...]).astype(BF16)
            up_ref[rows, :] = up
            pad[pl.ds(PAD + r0, rc), :] = up.astype(F32)
        cw = c_ref[...]
        for r0 in range(0, t, rc):
            rows = pl.ds(r0, rc)
            act = _conv_causal(pad, cw, r0, rc, 3) + b_ref[...]
            act_ref[rows, :] = act.astype(BF16)

            @pl.when(k == 0)
            def _():
                gate[rows, :] = act

            @pl.when(k == 1)
            def _():
                f_ref[rows, :] = (_gelu(gate[rows, :]) * act).astype(BF16)

    half = lambda rows: pl.BlockSpec((rows, CB), lambda j, k: (0, nb * k + j))
    wide = jax.ShapeDtypeStruct((t, 2 * D_FF), BF16)
    return pl.pallas_call(
        body, name="ffn_up_fwd", grid=(nb, 2), out_shape=[wide, wide, jax.ShapeDtypeStruct((t, D_FF), BF16)],
        in_specs=[pl.BlockSpec((t, D_MODEL), lambda j, k: (0, 0)), half(D_MODEL), half(3), half(1)],
        out_specs=[half(t), half(t), pl.BlockSpec((t, CB), lambda j, k: (0, j))],
        scratch_shapes=[pltpu.VMEM((t + PAD, CB), F32), pltpu.VMEM((t, CB), F32)],
        compiler_params=_params("parallel", "arbitrary"),
    )(h2, w_up, w_conv, b_conv)


def _ffn_down(f, act, w_down, x1, target, g4):
    t = f.shape[0]
    tm = min(256, t)
    cc = 512

    def body(f_ref, act_ref, w_ref, x1_ref, tg_ref, g_ref, dy_ref, dout_ref, back_ref, dg_ref, loss_ref):
        @pl.when(pl.program_id(0) == 0)
        def _():
            dg_ref[...] = jnp.zeros_like(dg_ref)
            loss_ref[...] = jnp.zeros_like(loss_ref)
        out = _dot(f_ref[...], w_ref[...])
        n4, r4 = _rms_fwd(out)
        err = x1_ref[...] + n4 * g_ref[...] - tg_ref[...]
        loss_ref[...] += jnp.full(loss_ref.shape, 0.5 / D_MODEL, F32) * jnp.sum(err * err)
        dy = err * (1.0 / D_MODEL)
        dy_ref[...] = dy
        dg_ref[...] += jnp.sum(dy * n4, axis=0, keepdims=True)
        d_out = _rms_bwd(n4, r4, dy * g_ref[...]).astype(BF16)
        dout_ref[...] = d_out
        for c0 in range(0, D_FF, cc):
            d_f = _dot_nt(d_out, w_ref[pl.ds(c0, cc), :])
            gelu, d_gelu = _gelu_and_grad(act_ref[:, pl.ds(c0, cc)].astype(F32))
            val = act_ref[:, pl.ds(D_FF + c0, cc)].astype(F32)
            back_ref[:, pl.ds(c0, cc)] = (d_f * val * d_gelu).astype(BF16)
            back_ref[:, pl.ds(D_FF + c0, cc)] = (d_f * gelu).astype(BF16)

    row = pl.BlockSpec((tm, D_MODEL), lambda i: (i, 0))
    wide = pl.BlockSpec((tm, 2 * D_FF), lambda i: (i, 0))
    vec = pl.BlockSpec((1, D_MODEL), lambda i: (0, 0))
    return pl.pallas_call(
        body, name="ffn_down_fwd_bwd", grid=(t // tm,),
        out_shape=[jax.ShapeDtypeStruct((t, D_MODEL), F32), jax.ShapeDtypeStruct((t, D_MODEL), BF16),
                   jax.ShapeDtypeStruct((t, 2 * D_FF), BF16), jax.ShapeDtypeStruct((1, D_MODEL), F32),
                   jax.ShapeDtypeStruct((SUBLANES, LANES), F32)],
        in_specs=[pl.BlockSpec((tm, D_FF), lambda i: (i, 0)), wide, pl.BlockSpec((D_FF, D_MODEL), lambda i: (0, 0)),
                  row, row, vec],
        out_specs=[row, row, wide, vec, pl.BlockSpec((SUBLANES, LANES), lambda i: (0, 0))],
        compiler_params=_params("arbitrary"),
    )(f, act, w_down, x1, target, g4)


def _grad_tn(a, b, bm, name):
    t, m = a.shape
    n = b.shape[1]

    def body(a_ref, b_ref, o_ref):
        o_ref[...] = _dot_tn(a_ref[...], b_ref[...]).astype(BF16)

    return pl.pallas_call(
        body, name=name, grid=(m // bm,), out_shape=jax.ShapeDtypeStruct((m, n), BF16),
        in_specs=[pl.BlockSpec((t, bm), lambda i: (0, i)), pl.BlockSpec((t, n), lambda i: (0, 0))],
        out_specs=pl.BlockSpec((bm, n), lambda i: (i, 0)),
        compiler_params=_params("parallel"),
    )(a, b)


def _ffn_up_bwd(up, back, w_conv, h2, w_up):
    t = h2.shape[0]
    rc = _row_chunk(t)
    nb = N_FF_BLOCKS

    def body(up_ref, back_ref, c_ref, h_ref, w_ref, dw_ref, dcw_ref, dcb_ref, dh_ref, pad, after, d_up):
        @pl.when((pl.program_id(0) == 0) & (pl.program_id(1) == 0))
        def _():
            dh_ref[...] = jnp.zeros_like(dh_ref)
        pad[pl.ds(0, PAD), :] = jnp.zeros((PAD, CB), F32)
        after[pl.ds(t, PAD), :] = jnp.zeros((PAD, CB), F32)
        for r0 in range(0, t, rc):
            pad[pl.ds(PAD + r0, rc), :] = up_ref[pl.ds(r0, rc), :].astype(F32)
            after[pl.ds(r0, rc), :] = back_ref[pl.ds(r0, rc), :].astype(F32)
        cw = c_ref[...]
        taps = [jnp.zeros((1, CB), F32)] * 3
        bias = jnp.zeros((1, CB), F32)
        for r0 in range(0, t, rc):
            rows = pl.ds(r0, rc)
            d = _conv_anticausal(after, cw, r0, rc, 3).astype(BF16)
            d_up[rows, :] = d
            dh_ref[rows, :] += _dot_nt(d, w_ref[...])
            g = after[rows, :]
            taps = [acc + new for acc, new in zip(taps, _conv_wgrad(g, pad, r0, rc, 3))]
            bias = bias + jnp.sum(g, axis=0, keepdims=True)
        dw_ref[...] = _dot_tn(h_ref[...], d_up[...]).astype(BF16)
        dcw_ref[...] = jnp.concatenate(taps, axis=0)
        dcb_ref[...] = bias

    half = lambda rows: pl.BlockSpec((rows, CB), lambda j, k: (0, nb * k + j))
    whole = pl.BlockSpec((t, D_MODEL), lambda j, k: (0, 0))
    return pl.pallas_call(
        body, name="ffn_up_bwd", grid=(nb, 2),
        out_shape=[jax.ShapeDtypeStruct((D_MODEL, 2 * D_FF), BF16), jax.ShapeDtypeStruct((3, 2 * D_FF), F32),
                   jax.ShapeDtypeStruct((1, 2 * D_FF), F32), jax.ShapeDtypeStruct((t, D_MODEL), F32)],
        in_specs=[half(t), half(t), half(3), whole, half(D_MODEL)],
        out_specs=[half(D_MODEL), half(3), half(1), whole],
        scratch_shapes=[pltpu.VMEM((t + PAD, CB), F32), pltpu.VMEM((t + PAD, CB), F32), pltpu.VMEM((t, CB), BF16)],
        compiler_params=_params("arbitrary", "arbitrary"),
    )(up, back, w_conv, h2, w_up)


def _merge_bwd(dy, d_h2, x1, mix, g3, g2, w_out, w_cb, w_lb, pa, pb, proj):
    t = dy.shape[0]
    tm = min(256, t)

    def body(dy_ref, dh2_ref, x1_ref, mix_ref, g3_ref, g2_ref, wo_ref, wcb_ref, wlb_ref, pa_ref, pb_ref, gc_ref, gl_ref,
             dx1_ref, dmix_ref, dpa_ref, dpb_ref, dya_ref, dyb_ref, dgate_ref, dg3_ref, dg2_ref):
        @pl.when(pl.program_id(0) == 0)
        def _():
            dg3_ref[...] = jnp.zeros_like(dg3_ref)
            dg2_ref[...] = jnp.zeros_like(dg2_ref)
        n3, r3 = _rms_fwd(x1_ref[...])
        d_h2 = dh2_ref[...]
        dg3_ref[...] += jnp.sum(d_h2 * n3, axis=0, keepdims=True)
        dx1 = dy_ref[...] + _rms_bwd(n3, r3, d_h2 * g3_ref[...])
        dx1_ref[...] = dx1
        n2, r2 = _rms_fwd(mix_ref[...])
        dg2_ref[...] += jnp.sum(dx1 * n2, axis=0, keepdims=True)
        d_mix = _rms_bwd(n2, r2, dx1 * g2_ref[...]).astype(BF16)
        dmix_ref[...] = d_mix
        d_merged = _dot_nt(d_mix, wo_ref[...])
        sc = jax.nn.sigmoid(gc_ref[...].astype(F32))
        sl = jax.nn.sigmoid(gl_ref[...].astype(F32))
        d_pa = (d_merged * sc).astype(BF16)
        d_pb = (d_merged * sl).astype(BF16)
        dpa_ref[...] = d_pa
        dpb_ref[...] = d_pb
        dgate_ref[0] = (d_merged * pa_ref[...].astype(F32) * sc * (1.0 - sc)).astype(BF16)
        dgate_ref[1] = (d_merged * pb_ref[...].astype(F32) * sl * (1.0 - sl)).astype(BF16)
        dya_ref[...] = _dot_nt(d_pa, wcb_ref[...]).astype(BF16)
        dyb_ref[...] = _dot_nt(d_pb, wlb_ref[...]).astype(BF16)

    row = pl.BlockSpec((tm, D_MODEL), lambda i: (i, 0))
    full = pl.BlockSpec((D_MODEL, D_MODEL), lambda i: (0, 0))
    vec = pl.BlockSpec((1, D_MODEL), lambda i: (0, 0))
    act = jax.ShapeDtypeStruct((t, D_MODEL), BF16)
    small = jax.ShapeDtypeStruct((1, D_MODEL), F32)
    return pl.pallas_call(
        body, name="merge_bwd", grid=(t // tm,),
        out_shape=[jax.ShapeDtypeStruct((t, D_MODEL), F32), act, act, act, act, act,
                   jax.ShapeDtypeStruct((2, t, D_MODEL), BF16), small, small],
        in_specs=[row, row, row, row, vec, vec, full, full, full, row, row,
                  pl.BlockSpec((tm, D_MODEL), lambda i: (i, 5)), pl.BlockSpec((tm, D_MODEL), lambda i: (i, 6))],
        out_specs=[row] * 6 + [pl.BlockSpec((2, tm, D_MODEL), lambda i: (0, i, 0)), vec, vec],
        compiler_params=_params("arbitrary"),
    )(dy, d_h2, x1, mix, g3, g2, w_out, w_cb, w_lb, pa, pb, proj, proj)


def _conv_mixer_bwd(proj, d_ya, w_short):
    t = proj.shape[0]
    rc = _row_chunk(t)

    def body(b_ref, c_ref, x_ref, dy_ref, w_ref, d_ref, dw_ref, pad, back):
        pad[pl.ds(0, PAD), :] = jnp.zeros((PAD, CB), F32)
        back[pl.ds(t, PAD), :] = jnp.zeros((PAD, CB), F32)
        for r0 in range(0, t, rc):
            rows = pl.ds(r0, rc)
            pad[pl.ds(PAD + r0, rc), :] = c_ref[rows, :].astype(F32) * x_ref[rows, :].astype(F32)
        w = w_ref[...]
        for r0 in range(0, t, rc):
            rows = pl.ds(r0, rc)
            d_y = dy_ref[rows, :].astype(F32)
            d_ref[0, rows, :] = (d_y * _conv_causal(pad, w, r0, rc, 3)).astype(BF16)
            back[rows, :] = d_y * b_ref[rows, :].astype(F32)
        taps = [jnp.zeros((1, CB), F32)] * 3
        for r0 in range(0, t, rc):
            rows = pl.ds(r0, rc)
            d_u = _conv_anticausal(back, w, r0, rc, 3)
            d_ref[1, rows, :] = (d_u * x_ref[rows, :].astype(F32)).astype(BF16)
            d_ref[2, rows, :] = (d_u * c_ref[rows, :].astype(F32)).astype(BF16)
            taps = [acc + new for acc, new in zip(taps, _conv_wgrad(back[rows, :], pad, r0, rc, 3))]
        dw_ref[...] = jnp.concatenate(taps, axis=0)

    blk = pl.BlockSpec((t, CB), lambda h: (0, h))
    return pl.pallas_call(
        body, name="conv_mixer_bwd", grid=(D_MODEL // CB,),
        out_shape=[jax.ShapeDtypeStruct((3, t, D_MODEL), BF16), jax.ShapeDtypeStruct((3, D_MODEL), F32)],
        in_specs=[_section(0, t), _section(1, t), _section(2, t), blk, pl.BlockSpec((3, CB), lambda h: (0, h))],
        out_specs=[pl.BlockSpec((3, t, CB), lambda h: (0, 0, h)), pl.BlockSpec((3, CB), lambda h: (0, h))],
        scratch_shapes=[pltpu.VMEM((t + PAD, CB), F32), pltpu.VMEM((t + PAD, CB), F32)],
        compiler_params=_params("parallel"),
    )(proj, proj, proj, d_ya, w_short)


LRU_SMALL_ROWS = 8


def _lru_bwd(proj, hl, a_all, kept, d_yb, w_conv, wa, wx, lam):
    t = proj.shape[0]
    rc = _row_chunk(t)
    vec, mat = _head_specs()

    def body(lx_ref, ly_ref, hl_ref, a_ref, kept_ref, dy_ref, wc_ref, wa_ref, wx_ref, lam_ref,
             d_ref, dwa_ref, dwx_ref, small_ref, pad, a_next, dh_s, h_prev, back, acc_a, acc_x):
        zeros = jnp.zeros((PAD, CB), F32)
        pad[pl.ds(0, PAD), :] = zeros
        h_prev[pl.ds(0, PAD), :] = zeros
        a_next[pl.ds(t, PAD), :] = zeros
        back[pl.ds(t, PAD), :] = zeros
        for r0 in range(0, t, rc):
            rows = pl.ds(r0, rc)
            pad[pl.ds(PAD + r0, rc), :] = lx_ref[rows, :].astype(F32)
            h_prev[pl.ds(PAD + r0, rc), :] = hl_ref[rows, :]
            a_next[pl.ds(PAD - 1 + r0, rc), :] = a_ref[rows, :]
            act, d_act = _gelu_and_grad(ly_ref[rows, :].astype(F32))
            d_y = dy_ref[rows, :].astype(F32)
            dh_s[rows, :] = d_y * act
            d_ref[1, rows, :] = (d_y * hl_ref[rows, :] * d_act).astype(BF16)
        wc = wc_ref[...]
        wa_m, wx_m = wa_ref[...].reshape(HEAD_DIM, HEAD_DIM), wx_ref[...].reshape(HEAD_DIM, HEAD_DIM)
        ls = _log_sigmoid(lam_ref[...])

        row = lax.broadcasted_iota(jnp.int32, (SUBLANES, CB), 0)
        groups = t // SUBLANES

        def group(i, carry):
            r = pl.multiple_of((groups - 1 - i) * SUBLANES, SUBLANES)
            a_g, b_g = a_next[pl.ds(PAD + r, SUBLANES), :], dh_s[pl.ds(r, SUBLANES), :]
            for s in (1, 2, 4):
                keep = row < SUBLANES - s
                b_g = jnp.where(keep, a_g * pltpu.roll(b_g, SUBLANES - s, 0) + b_g, b_g)
                a_g = jnp.where(keep, a_g * pltpu.roll(a_g, SUBLANES - s, 0), a_g)
            d_g = b_g + a_g * carry
            dh_s[pl.ds(r, SUBLANES), :] = d_g
            return jnp.broadcast_to(d_g[0:1, :], (SUBLANES, CB))

        lax.fori_loop(0, groups, group, jnp.zeros((SUBLANES, CB), F32))

        acc_a[...] = jnp.zeros_like(acc_a)
        acc_x[...] = jnp.zeros_like(acc_x)
        d_ba = d_bx = d_ls = jnp.zeros((1, CB), F32)
        for r0 in range(0, t, rc):
            rows = pl.ds(r0, rc)
            first = (lax.broadcasted_iota(jnp.int32, (rc, CB), 0) + r0) == 0
            xb, a = kept_ref[0, rows, :], a_ref[rows, :]
            xl, ra, ia = xb.astype(F32), kept_ref[1, rows, :].astype(F32), kept_ref[2, rows, :].astype(F32)
            a_sq = a * a
            mult = jnp.where(first, 1.0, jnp.sqrt(1.0 - a_sq))
            d_h = dh_s[rows, :]
            d_a = d_h * _rows_back(h_prev, r0, rc, 1)
            d_mult = d_h * ia * xl
            d_ia = d_h * mult * xl
            d_xl = d_h * mult * ia
            d_la = d_a * a + d_mult * jnp.where(first, 0.0, -a_sq / mult)
            d_ls = d_ls + jnp.sum(d_la * ra, axis=0, keepdims=True) * LRU_C
            d_za = d_la * (LRU_C * ls) * ra * (1.0 - ra)
            d_zx = d_ia * ia * (1.0 - ia)
            d_ba = d_ba + jnp.sum(d_za, axis=0, keepdims=True)
            d_bx = d_bx + jnp.sum(d_zx, axis=0, keepdims=True)
            d_za, d_zx = d_za.astype(BF16), d_zx.astype(BF16)
            acc_a[...] += _dot_tn(xb, d_za)
            acc_x[...] += _dot_tn(xb, d_zx)
            back[rows, :] = d_xl + _dot_nt(d_za, wa_m) + _dot_nt(d_zx, wx_m)
        taps = [jnp.zeros((1, CB), F32)] * 4
        d_bc = jnp.zeros((1, CB), F32)
        for r0 in range(0, t, rc):
            rows = pl.ds(r0, rc)
            d_ref[0, rows, :] = _conv_anticausal(back, wc, r0, rc, 4).astype(BF16)
            g = back[rows, :]
            taps = [acc + new for acc, new in zip(taps, _conv_wgrad(g, pad, r0, rc, 4))]
            d_bc = d_bc + jnp.sum(g, axis=0, keepdims=True)
        d_lam = d_ls * jax.nn.sigmoid(-lam_ref[...])
        small_ref[...] = jnp.concatenate(taps + [d_bc, d_ba, d_bx, d_lam], axis=0)
        dwa_ref[...] = acc_a[...].reshape(N_DEV, HEAD_DIM // N_DEV, HEAD_DIM).astype(BF16)
        dwx_ref[...] = acc_x[...].reshape(N_DEV, HEAD_DIM // N_DEV, HEAD_DIM).astype(BF16)

    blk = pl.BlockSpec((t, CB), lambda h: (0, h))
    gate_grad = jax.ShapeDtypeStruct((N_DEV, N_HEADS, HEAD_DIM // N_DEV, HEAD_DIM), BF16)
    return pl.pallas_call(
        body, name="lru_bwd", grid=(N_HEADS,),
        out_shape=[jax.ShapeDtypeStruct((2, t, D_MODEL), BF16), gate_grad, gate_grad,
                   jax.ShapeDtypeStruct((LRU_SMALL_ROWS, D_MODEL), F32)],
        in_specs=[_section(3, t), _section(4, t), blk, blk, pl.BlockSpec((3, t, CB), lambda h: (0, 0, h)), blk,
                  pl.BlockSpec((4, CB), lambda h: (0, h)), mat, mat, vec],
        out_specs=[pl.BlockSpec((2, t, CB), lambda h: (0, 0, h)), mat, mat,
                   pl.BlockSpec((LRU_SMALL_ROWS, CB), lambda h: (0, h))],
        scratch_shapes=[pltpu.VMEM((t + PAD, CB), F32), pltpu.VMEM((t + PAD, CB), F32), pltpu.VMEM((t, CB), F32),
                        pltpu.VMEM((t + PAD, CB), F32), pltpu.VMEM((t + PAD, CB), F32),
                        pltpu.VMEM((HEAD_DIM, HEAD_DIM), F32), pltpu.VMEM((HEAD_DIM, HEAD_DIM), F32)],
        compiler_params=_params("parallel"),
    )(proj, proj, hl, a_all, kept, d_yb, w_conv, wa, wx, lam)


def _stack_maps(halves):
    def conv(sec, part):
        return jnp.minimum(sec, 2), jnp.where(sec < 3, part, halves - 1)

    def lru(sec, part):
        return jnp.clip(sec - 3, 0, 1), jnp.where(sec < 3, 0, jnp.where(sec < 5, part, halves - 1))

    def gate(sec, part):
        return jnp.clip(sec - 5, 0, 1), jnp.where(sec < 5, 0, part)

    return conv, lru, gate


def _pick_stack(sec, refs, fn):
    @pl.when(sec < 3)
    def _():
        fn(refs[0])

    @pl.when((sec >= 3) & (sec < 5))
    def _():
        fn(refs[1])

    @pl.when(sec >= 5)
    def _():
        fn(refs[2])


def _in_proj_wgrad(h, d_conv, d_lru, d_gate):
    t = h.shape[0]
    halves, bn = 2, D_MODEL // 2
    maps = _stack_maps(halves)

    def body(h_ref, dc_ref, dl_ref, dg_ref, o_ref):
        def emit(ref):
            o_ref[...] = _dot_tn(h_ref[...], ref[...]).astype(BF16)
        _pick_stack(pl.program_id(0) // halves, (dc_ref, dl_ref, dg_ref), emit)

    def spec(m):
        def index(s):
            stack, part = m(s // halves, s % halves)
            return stack, 0, part
        return pl.BlockSpec((None, t, bn), index)

    return pl.pallas_call(
        body, name="in_proj_wgrad", grid=(7 * halves,), out_shape=jax.ShapeDtypeStruct((D_MODEL, IN_COLS), BF16),
        in_specs=[pl.BlockSpec((t, D_MODEL), lambda s: (0, 0))] + [spec(m) for m in maps],
        out_specs=pl.BlockSpec((D_MODEL, bn), lambda s: (0, s)),
        compiler_params=_params("arbitrary"),
    )(h, d_conv, d_lru, d_gate)


def _in_proj_xgrad(d_conv, d_lru, d_gate, w_in, x, dx1, g1):
    t = x.shape[0]
    tm = min(1024, t)
    maps = _stack_maps(1)

    def body(dc_ref, dl_ref, dg_ref, w_ref, x_ref, dx1_ref, g_ref, dx_ref, dgain_ref, acc):
        i, s = pl.program_id(0), pl.program_id(1)

        @pl.when((i == 0) & (s == 0))
        def _():
            dgain_ref[...] = jnp.zeros_like(dgain_ref)

        @pl.when(s == 0)
        def _():
            acc[...] = jnp.zeros_like(acc)

        def add(ref):
            acc[...] += _dot_nt(ref[...], w_ref[...])
        _pick_stack(s, (dc_ref, dl_ref, dg_ref), add)

        @pl.when(s == 6)
        def _():
            n1, r1 = _rms_fwd(x_ref[...])
            d_h = acc[...]
            dgain_ref[...] += jnp.sum(d_h * n1, axis=0, keepdims=True)
            dx_ref[...] = dx1_ref[...] + _rms_bwd(n1, r1, d_h * g_ref[...])

    def spec(m):
        def index(i, s):
            return m(s, 0)[0], i, 0
        return pl.BlockSpec((None, tm, D_MODEL), index)

    row = pl.BlockSpec((tm, D_MODEL), lambda i, s: (i, 0))
    vec = pl.BlockSpec((1, D_MODEL), lambda i, s: (0, 0))
    return pl.pallas_call(
        body, name="in_proj_xgrad", grid=(t // tm, 7),
        out_shape=[jax.ShapeDtypeStruct((t, D_MODEL), F32), jax.ShapeDtypeStruct((1, D_MODEL), F32)],
        in_specs=[spec(m) for m in maps] + [pl.BlockSpec((D_MODEL, D_MODEL), lambda i, s: (0, s)), row, row, vec],
        out_specs=[row, vec],
        scratch_shapes=[pltpu.VMEM((tm, D_MODEL), F32)],
        compiler_params=_params("arbitrary", "arbitrary"),
    )(d_conv, d_lru, d_gate, w_in, x, dx1, g1)


def _adamw(w, g, m, v):
    m = ADAM_B1 * m + (1.0 - ADAM_B1) * g
    v = ADAM_B2 * v + (1.0 - ADAM_B2) * (g * g)
    m_hat = m / (1.0 - ADAM_B1 ** ADAM_STEP)
    v_hat = v / (1.0 - ADAM_B2 ** ADAM_STEP)
    return -ADAM_LR * (m_hat / (jnp.sqrt(v_hat) + ADAM_EPS) + ADAM_WD * w), m, v


def _adam_large(w, m, v, own, others, name):
    shape = w.shape
    cols = shape[-1]
    w2, m2, v2 = (a.reshape(-1, cols) for a in (w, m, v))
    rows = w2.shape[0]
    own, others = own.reshape(4, rows, cols), others.reshape(3, rows, cols)
    rb = _row_block(rows, 512)

    def body(w_ref, m_ref, v_ref, own_ref, oth_ref, g_ref, d_ref, nm_ref, nv_ref):
        g = own_ref[...].astype(F32)
        for k in range(3):
            g = g + oth_ref[k].astype(F32)
        g_ref[...] = g
        d_ref[...], nm_ref[...], nv_ref[...] = _adamw(w_ref[...], g, m_ref[...], v_ref[...])

    blk = pl.BlockSpec((rb, cols), lambda i: (i, 0))
    res = jax.ShapeDtypeStruct((rows, cols), F32)
    outs = pl.pallas_call(
        body, name=name, grid=(rows // rb,), out_shape=[res] * 4,
        in_specs=[blk, blk, blk, pl.BlockSpec((None, rb, cols), lambda i: (0, i, 0)),
                  pl.BlockSpec((3, rb, cols), lambda i: (0, i, 0))],
        out_specs=[blk] * 4, compiler_params=_params("parallel"),
    )(w2, m2, v2, own, others)
    return [o.reshape(shape) for o in outs]


def _adam_small(ws, gs, ms, vs):
    n = len(ws)

    def body(*refs):
        w_refs, g_refs, m_refs, v_refs = (refs[i * n:(i + 1) * n] for i in range(4))
        outs = refs[4 * n:]
        for i in range(n):
            d, m, v = _adamw(w_refs[i][...], g_refs[i][...], m_refs[i][...], v_refs[i][...])
            outs[i][...], outs[n + i][...], outs[2 * n + i][...] = d, m, v

    shapes = [jax.ShapeDtypeStruct(w.shape, F32) for w in ws]
    outs = pl.pallas_call(
        body, name="adam_small", out_shape=shapes * 3,
        in_specs=[VMEM_SPEC] * (4 * n), out_specs=[VMEM_SPEC] * (3 * n), compiler_params=_params(),
    )(*ws, *gs, *ms, *vs)
    return outs[:n], outs[n:2 * n], outs[2 * n:]


def _pack_rows(pieces):
    tile = SUBLANES * LANES
    return jnp.concatenate([jnp.pad(p.reshape(-1), (0, (-p.size) % tile)).reshape(-1, LANES) for p in pieces], axis=0)


def _packed_starts(sizes):
    tile = SUBLANES * LANES
    starts = [0]
    for s in sizes:
        starts.append(starts[-1] + (s + tile - 1) // tile * SUBLANES)
    return starts


def kernel(x, norm_mix_pre, norm_mix_post, norm_ffn_pre, norm_ffn_post, w_in, conv_short_w, w_conv_branch, lru_conv_w, lru_conv_b, lru_wa, lru_ba, lru_wx, lru_bx, lru_lambda, w_lru_branch, w_out, ffn_w_up, ffn_conv_w, ffn_conv_b, ffn_w_down, loss_target, m_norm_mix_pre, m_norm_mix_post, m_norm_ffn_pre, m_norm_ffn_post, m_w_in, m_conv_short_w, m_w_conv_branch, m_lru_conv_w, m_lru_conv_b, m_lru_wa, m_lru_ba, m_lru_wx, m_lru_bx, m_lru_lambda, m_w_lru_branch, m_w_out, m_ffn_w_up, m_ffn_conv_w, m_ffn_conv_b, m_ffn_w_down, v_norm_mix_pre, v_norm_mix_post, v_norm_ffn_pre, v_norm_ffn_post, v_w_in, v_conv_short_w, v_w_conv_branch, v_lru_conv_w, v_lru_conv_b, v_lru_wa, v_lru_ba, v_lru_wx, v_lru_bx, v_lru_lambda, v_w_lru_branch, v_w_out, v_ffn_w_up, v_ffn_conv_w, v_ffn_conv_b, v_ffn_w_down):
    t = x.shape[1]
    xi, yi, ci = _position()
    me = _block_of(xi, yi, ci)
    x2, target = x[0], loss_target[0]
    shard_in, shard_up = IN_COLS // N_DEV, 2 * D_FF // N_DEV
    shard_sq, shard_down, shard_head = D_MODEL // N_DEV, D_FF // N_DEV, HEAD_DIM // N_DEV

    names = ["w_in", "lru_wa", "lru_wx", "w_conv_branch", "w_lru_branch", "w_out", "ffn_w_up", "ffn_w_down"]
    large = [w_in[0], lru_wa[0], lru_wx[0], w_conv_branch[0], w_lru_branch[0], w_out[0], ffn_w_up[0], ffn_w_down[0]]
    blocks = [_cols(shard_in), _lead, _lead, _rows(shard_sq), _rows(shard_sq), _rows(shard_sq),
              _cols(shard_up), _rows(shard_down)]
    gate_full = (N_DEV, N_HEADS, shard_head, HEAD_DIM)
    full_shapes = [(D_MODEL, IN_COLS), gate_full, gate_full, (D_MODEL, D_MODEL), (D_MODEL, D_MODEL), (D_MODEL, D_MODEL),
                   (D_MODEL, 2 * D_FF), (D_FF, D_MODEL)]
    n_now = 3
    small_sharded = [conv_short_w, lru_conv_w, lru_ba, lru_bx, ffn_conv_w]
    small_mine = _pack_rows(small_sharded)
    small_at = _packed_starts([p.size for p in small_sharded])
    *gathered, small_all, proj, h = _gather_weights(large, blocks, full_shapes, small_mine, n_now, x2, norm_mix_pre)
    g_in, g_wa, g_wx = gathered[:n_now]
    later_blocks = blocks[n_now:]
    send1, recv1, later, gather_token = _gather_start(gathered[n_now:], later_blocks, "gather_start")

    def behind(token, operand):
        return operand + token[0:1, 0:1]

    def forward(lo, hi, after, tag):
        return _gather_forward(later[lo:hi], later_blocks[lo:hi], send1[4 * lo:4 * hi], recv1[4 * lo:4 * hi], after,
                               "gather_forward_" + tag)

    def finish(lo, hi, flight, after, tag):
        return _gather_finish(flight[2], later_blocks[lo:hi], flight[0], flight[1], after, "gather_finish_" + tag)

    def cols_of(r0, n, width):
        part = small_all[:, r0:r0 + n * width // LANES, :].reshape(N_DEV, n, width)
        return part.transpose(1, 0, 2).reshape(n, N_DEV * width)

    c_short = cols_of(small_at[0], 3, LANES)
    c_lru = cols_of(small_at[1], 4, LANES)
    b_a = cols_of(small_at[2], N_HEADS, shard_head).reshape(1, D_MODEL)
    b_x = cols_of(small_at[3], N_HEADS, shard_head).reshape(1, D_MODEL)
    c_ffn = cols_of(small_at[4], 3, shard_up)

    y_a = _conv_mixer_fwd(proj, behind(gather_token, c_short))
    y_b, hl, decay, lru_kept = _lru_fwd(proj, c_lru, lru_conv_b, g_wa, b_a, g_wx, b_x, lru_lambda)
    flight_mix_w = forward(0, 3, y_b, "mix")
    g_cb, g_lb, g_out = finish(0, 3, flight_mix_w, y_b, "mix")
    pa, pb, merged, mix, x1, h2 = _merge(y_a, y_b, proj, x2, g_cb, g_lb, g_out, norm_mix_post, norm_ffn_pre)
    flight_up_w = forward(3, 4, h2, "up")
    (g_up,) = finish(3, 4, flight_up_w, h2, "up")
    up, act, f = _ffn_up(h2, g_up, c_ffn, ffn_conv_b)
    flight_down_w = forward(4, 5, f, "down")
    (g_down,) = finish(4, 5, flight_down_w, f, "down")
    dy, d_out, d_act, dg4, loss_part = _ffn_down(f, act, g_down, x1, target, norm_ffn_post)

    block_of = dict(zip(names, blocks))
    shard_shapes = {"w_in": (D_MODEL, shard_in), "w_conv_branch": (shard_sq, D_MODEL), "w_lru_branch": (shard_sq, D_MODEL),
                    "w_out": (shard_sq, D_MODEL), "lru_wa": (N_HEADS, shard_head, HEAD_DIM),
                    "lru_wx": (N_HEADS, shard_head, HEAD_DIM), "ffn_w_up": (D_MODEL, shard_up),
                    "ffn_w_down": (shard_down, D_MODEL)}

    def reduce_start(tag, grads):
        keys = list(grads)
        sums = _reduce_pair([grads[k] for k in keys], [block_of[k] for k in keys], [shard_shapes[k] for k in keys],
                            "reduce_pair_" + tag)
        return (keys,) + _exchange_chips_start(sums, "reduce_chip_start_" + tag)

    gw_down = _grad_tn(f, d_out, min(512, D_FF), "ffn_down_wgrad")
    flight_down = reduce_start("down", {"ffn_w_down": gw_down})
    gw_up, gc_ffn, gb_ffn, d_h2 = _ffn_up_bwd(up, d_act, behind(flight_down[-1], c_ffn), h2, g_up)
    flight_up = reduce_start("up", {"ffn_w_up": gw_up})
    dx1, d_mix, d_pa, d_pb, d_ya, d_yb, d_gate, dg3, dg2 = _merge_bwd(
        dy, d_h2, x1, mix, behind(flight_up[-1], norm_ffn_pre), norm_mix_post, g_out, g_cb, g_lb, pa, pb, proj)
    gw_out = _grad_tn(merged, d_mix, CB, "w_out_wgrad")
    gw_cb = _grad_tn(y_a, d_pa, CB, "w_conv_branch_wgrad")
    gw_lb = _grad_tn(y_b, d_pb, CB, "w_lru_branch_wgrad")
    flight_mix = reduce_start("mix", {"w_conv_branch": gw_cb, "w_lru_branch": gw_lb, "w_out": gw_out})
    d_conv, gc_short = _conv_mixer_bwd(proj, d_ya, behind(flight_mix[-1], c_short))
    d_lru, gw_a, gw_x, g_lru_small = _lru_bwd(proj, hl, decay, lru_kept, d_yb, c_lru, g_wa, g_wx, lru_lambda)
    early = [dg2, dg3, dg4, g_lru_small[4:5], g_lru_small[7:8], gb_ffn, gc_short, g_lru_small[0:4],
             g_lru_small[5:6], g_lru_small[6:7], gc_ffn, loss_part]
    flight_small = _small_start(_pack_rows(early), "small_start")
    gw_in = _in_proj_wgrad(h, d_conv, d_lru, d_gate)
    flight_in = reduce_start("in", {"lru_wa": gw_a, "lru_wx": gw_x, "w_in": gw_in})
    dx, dg1 = _in_proj_xgrad(d_conv, d_lru, d_gate, g_in, x2, dx1,
                             behind(flight_small[-1], behind(flight_in[-1], norm_mix_pre)))
    flight_late = _small_start(_pack_rows([dg1]), "small_start_late")

    moments ={"w_in": (m_w_in, v_w_in), "w_conv_branch": (m_w_conv_branch, v_w_conv_branch),
               "w_lru_branch": (m_w_lru_branch, v_w_lru_branch), "w_out": (m_w_out, v_w_out),
               "lru_wa": (m_lru_wa, v_lru_wa), "lru_wx": (m_lru_wx, v_lru_wx), "ffn_w_up": (m_ffn_w_up, v_ffn_w_up),
               "ffn_w_down": (m_ffn_w_down, v_ffn_w_down)}
    weights = {"w_in": w_in, "w_conv_branch": w_conv_branch, "w_lru_branch": w_lru_branch, "w_out": w_out,
               "lru_wa": lru_wa, "lru_wx": lru_wx, "ffn_w_up": ffn_w_up, "ffn_w_down": ffn_w_down}
    out_g, out_d, out_m, out_v = {}, {}, {}, {}

    after = flight_late[-1]
    for tag, (keys, send, recv, sums, lands, _) in (("down", flight_down), ("up", flight_up), ("mix", flight_mix),
                                                    ("in", flight_in)):
        sums, others = _exchange_chips_wait(send, recv, sums, lands, after, "reduce_chip_wait_" + tag)
        for k, own, oth in zip(keys, sums, others):
            out_g[k], out_d[k], out_m[k], out_v[k] = _adam_large(weights[k], *moments[k], own, oth, "adam_" + k)
        after = out_d[keys[-1]]

    total, total_late = _small_sum([_small_wait(*flight_small[:4], after, "small_wait"),
                                    _small_wait(*flight_late[:4], after, "small_wait_late")], me)
    sizes = [p.size for p in early]
    starts = _packed_starts(sizes)

    def piece(i, shape):
        if i == 0:
            return total_late.reshape(-1)[:D_MODEL].reshape(shape)
        return total[starts[i - 1]:starts[i]].reshape(-1)[:sizes[i - 1]].reshape(shape)

    loss = total[starts[11], 0]

    def col_shard(full, width):
        return lax.dynamic_slice_in_dim(full, me * width, width, axis=1)

    def head_shard(full):
        return lax.dynamic_slice_in_dim(full.reshape(N_HEADS, HEAD_DIM), me * shard_head, shard_head, axis=1)

    small_names = ["norm_mix_pre", "norm_mix_post", "norm_ffn_pre", "norm_ffn_post", "lru_conv_b", "lru_lambda",
                   "ffn_conv_b", "conv_short_w", "lru_conv_w", "lru_ba", "lru_bx", "ffn_conv_w"]
    small_g = [piece(0, (1, D_MODEL)), piece(1, (1, D_MODEL)), piece(2, (1, D_MODEL)), piece(3, (1, D_MODEL)),
               piece(4, (1, D_MODEL)), piece(5, (1, D_MODEL)), piece(6, (1, 2 * D_FF)),
               col_shard(piece(7, (3, D_MODEL)), LANES), col_shard(piece(8, (4, D_MODEL)), LANES),
               head_shard(piece(9, (1, D_MODEL))), head_shard(piece(10, (1, D_MODEL))),
               col_shard(piece(11, (3, 2 * D_FF)), shard_up)]
    small_w = [norm_mix_pre, norm_mix_post, norm_ffn_pre, norm_ffn_post, lru_conv_b, lru_lambda, ffn_conv_b,
               conv_short_w[0], lru_conv_w[0], lru_ba[0], lru_bx[0], ffn_conv_w[0]]
    small_m = [m_norm_mix_pre, m_norm_mix_post, m_norm_ffn_pre, m_norm_ffn_post, m_lru_conv_b, m_lru_lambda,
               m_ffn_conv_b, m_conv_short_w[0], m_lru_conv_w[0], m_lru_ba[0], m_lru_bx[0], m_ffn_conv_w[0]]
    small_v = [v_norm_mix_pre, v_norm_mix_post, v_norm_ffn_pre, v_norm_ffn_post, v_lru_conv_b, v_lru_lambda,
               v_ffn_conv_b, v_conv_short_w[0], v_lru_conv_w[0], v_lru_ba[0], v_lru_bx[0], v_ffn_conv_w[0]]
    s_d, s_m, s_v = _adam_small(small_w, small_g, small_m, small_v)
    for i, name in enumerate(small_names):
        shape = small_w[i].shape if i < 7 else (1,) + small_w[i].shape
        out_g[name] = small_g[i].reshape(shape)
        out_d[name], out_m[name], out_v[name] = s_d[i].reshape(shape), s_m[i].reshape(shape), s_v[i].reshape(shape)

    order = ["norm_mix_pre", "norm_mix_post", "norm_ffn_pre", "norm_ffn_post", "w_in", "conv_short_w", "w_conv_branch",
             "lru_conv_w", "lru_conv_b", "lru_wa", "lru_ba", "lru_wx", "lru_bx", "lru_lambda", "w_lru_branch", "w_out",
             "ffn_w_up", "ffn_conv_w", "ffn_conv_b", "ffn_w_down"]
    return (loss, dx.reshape(1, t, D_MODEL), *[out_g[k] for k in order], *[out_d[k] for k in order],
            *[out_m[k] for k in order], *[out_v[k] for k in order])
```

```python
import functools
import math

import jax
import jax.numpy as jnp
from jax import lax
from jax.experimental import pallas as pl
from jax.experimental.pallas import tpu as pltpu

F32 = jnp.float32
BF16 = jnp.bfloat16
MESH = pl.DeviceIdType.MESH

N_DEV = 8
D_MODEL = 1024
N_HEADS = 4
HEAD_DIM = D_MODEL // N_HEADS
D_FF = 3 * D_MODEL
IN_COLS = 7 * D_MODEL
LRU_C = 8.0
RMS_EPS = 1e-6
ADAM_LR = 0.001
ADAM_B1 = 0.9
ADAM_B2 = 0.999
ADAM_EPS = 1e-08
ADAM_WD = 0.01
ADAM_STEP = 10
GELU_K = math.sqrt(2.0 / math.pi)
GELU_C = 0.044715

LANES = 128
SUBLANES = 8
PAD = SUBLANES
VMEM_LIMIT = 56 * 1024 * 1024
CB = 256

HBM_SPEC = pl.BlockSpec(memory_space=pltpu.HBM)
SEM_SPEC = pl.BlockSpec(memory_space=pltpu.SEMAPHORE)
DATAFLOW_EFFECT = pltpu.SideEffectType.DATAFLOW_SIDE_EFFECTING
VMEM_SPEC = pl.BlockSpec(memory_space=pltpu.VMEM)


def _params(*sem):
    if sem:
        return pltpu.CompilerParams(dimension_semantics=sem, vmem_limit_bytes=VMEM_LIMIT)
    return pltpu.CompilerParams(vmem_limit_bytes=VMEM_LIMIT)


def _row_chunk(t):
    return min(256, t)


def _row_block(rows, cap):
    return next(rb for rb in range(min(cap, rows), 0, -16) if rows % rb == 0)


def _gelu(x):
    return 0.5 * x * (1.0 + jnp.tanh(GELU_K * (x + GELU_C * x * x * x)))


def _gelu_and_grad(x):
    t = jnp.tanh(GELU_K * (x + GELU_C * x * x * x))
    g = 0.5 * x * (1.0 + t)
    dg = 0.5 * (1.0 + t) + 0.5 * x * (1.0 - t * t) * GELU_K * (1.0 + 3.0 * GELU_C * x * x)
    return g, dg


def _expm1_neg(x):
    series = x * (1.0 + x * (0.5 + x * (1.0 / 6.0 + x * (1.0 / 24.0 + x * (1.0 / 120.0)))))
    return jnp.where(x > -0.05, series, jnp.exp(x) - 1.0)


def _log_sigmoid(x):
    return jnp.minimum(x, 0.0) - jnp.log1p(jnp.exp(-jnp.abs(x)))


def _dot(a, b):
    return jnp.dot(a, b, preferred_element_type=F32)


def _dot_nt(a, b):
    return lax.dot_general(a, b, (((1,), (1,)), ((), ())), preferred_element_type=F32)


def _dot_tn(a, b):
    return lax.dot_general(a, b, (((0,), (0,)), ((), ())), preferred_element_type=F32)


def _rms_fwd(x):
    r = lax.rsqrt(jnp.mean(x * x, axis=-1, keepdims=True) + RMS_EPS)
    return x * r, r


def _rms_bwd(n, r, gdy):
    return r * (gdy - n * jnp.mean(n * gdy, axis=-1, keepdims=True))


def _rows_back(pad_ref, r0, rows, j):
    cur = pad_ref[pl.ds(PAD + r0, rows), :]
    if j == 0:
        return cur
    before = pad_ref[pl.ds(PAD + r0 - SUBLANES, SUBLANES), :]
    row = lax.broadcasted_iota(jnp.int32, before.shape, 0)
    rolled = pltpu.roll(cur, j, 0)
    top = jnp.where(row < j, pltpu.roll(before, j, 0), rolled[0:SUBLANES, :])
    return jnp.concatenate([top, rolled[SUBLANES:, :]], axis=0)


def _rows_ahead(pad_ref, r0, rows, j):
    cur = pad_ref[pl.ds(r0, rows), :]
    if j == 0:
        return cur
    after = pad_ref[pl.ds(r0 + rows, SUBLANES), :]
    row = lax.broadcasted_iota(jnp.int32, after.shape, 0)
    rolled = pltpu.roll(cur, rows - j, 0)
    bottom = jnp.where(row >= SUBLANES - j, pltpu.roll(after, SUBLANES - j, 0), rolled[rows - SUBLANES:, :])
    return jnp.concatenate([rolled[:rows - SUBLANES, :], bottom], axis=0)


def _conv_causal(pad_ref, w, r0, rows, taps):
    acc = None
    for k in range(taps):
        term = w[k:k + 1, :] * _rows_back(pad_ref, r0, rows, taps - 1 - k)
        acc = term if acc is None else acc + term
    return acc


def _conv_anticausal(pad_ref, w, r0, rows, taps):
    acc = None
    for k in range(taps):
        term = w[k:k + 1, :] * _rows_ahead(pad_ref, r0, rows, taps - 1 - k)
        acc = term if acc is None else acc + term
    return acc


def _conv_wgrad(g, xpad_ref, r0, rows, taps):
    return [jnp.sum(g * _rows_back(xpad_ref, r0, rows, taps - 1 - k), axis=0, keepdims=True) for k in range(taps)]


def _position():
    return lax.axis_index("x"), lax.axis_index("y"), lax.axis_index("c")


def _block_of(x, y, c):
    return 4 * x + 2 * y + c


def _chip(x, y, k):
    return (x + (k & 1)) % 2, (y + (k >> 1)) % 2


def _cols(width):
    def at(ref, d, half=None):
        cols = pl.ds(pl.multiple_of(d * width, LANES), width)
        if half is None:
            return ref.at[:, cols]
        return ref.at[pl.ds(half * (ref.shape[0] // 2), ref.shape[0] // 2), cols]
    return at


def _rows(height):
    def at(ref, d, half=None):
        if half is None:
            return ref.at[pl.ds(pl.multiple_of(d * height, 16), height), :]
        return ref.at[pl.ds(pl.multiple_of(d * height + half * (height // 2), 16), height // 2), :]
    return at


def _lead(ref, d, half=None):
    if half is None:
        return ref.at[d]
    return ref.at[d, pl.ds(half * (ref.shape[1] // 2), ref.shape[1] // 2)]


def _gather_weights(shards, blocks, full_shapes, small, n_now, tokens, gain):
    n = len(shards)
    small_rows = small.shape[0]
    t = tokens.shape[0]
    rc = min(512, t)

    def body(*refs):
        ins, small_in, x_ref, g_ref = refs[:n], refs[n], refs[n + 1], refs[n + 2]
        outs, small_out, proj_ref, h_ref = refs[n + 3:2 * n + 3], refs[2 * n + 3], refs[2 * n + 4], refs[2 * n + 5]
        stage = refs[2 * n + 6:3 * n + 6]
        w_buf, p_buf, send, recv, local, w_sem, p_sem = refs[3 * n + 6:]
        x, y, c = _position()
        me = _block_of(x, y, c)
        sibling = (x, y, 1 - c)

        for a in range(n):
            stage[a][...] = ins[a][...].astype(BF16)
        for r0 in range(0, t, rc):
            normed, _ = _rms_fwd(x_ref[pl.ds(r0, rc), :])
            h_ref[pl.ds(r0, rc), :] = (normed * g_ref[...]).astype(BF16)
        stores = []

        def project(w_ref, block):
            i = len(stores)
            if i >= 2:
                stores[i - 2].wait()
            for r0 in range(0, t, rc):
                p_buf[i % 2, pl.ds(r0, rc), :] = _dot(h_ref[pl.ds(r0, rc), :], w_ref[...]).astype(BF16)
            st = pltpu.make_async_copy(p_buf.at[i % 2], blocks[0](proj_ref, block), p_sem.at[i % 2])
            st.start()
            stores.append(st)

        def project_landed(block):
            ld = pltpu.make_async_copy(blocks[0](outs[0], block), w_buf, w_sem)
            ld.start()
            ld.wait()
            project(w_buf, block)

        def copy(a, k, block, to, src=None, half=None):
            dst = blocks[a](outs[a], block, half)
            return pltpu.make_async_remote_copy(
                src_ref=dst if src is None else src, dst_ref=dst, send_sem=send.at[a, k], recv_sem=recv.at[a, k],
                device_id=to, device_id_type=MESH)

        def small_copy(k):
            px, py, pc = (x + (k & 1)) % 2, (y + ((k >> 1) & 1)) % 2, (c + (k >> 2)) % 2
            return pltpu.make_async_remote_copy(
                src_ref=small_in, dst_ref=small_out.at[me], send_sem=send.at[n_now, k - 1], recv_sem=recv.at[n_now, k - 1],
                device_id=(px, py, pc), device_id_type=MESH)

        def small_arrival(k):
            px, py, pc = (x + (k & 1)) % 2, (y + ((k >> 1) & 1)) % 2, (c + (k >> 2)) % 2
            return pltpu.make_async_remote_copy(
                src_ref=small_in, dst_ref=small_out.at[_block_of(px, py, pc)], send_sem=send.at[n_now, k - 1],
                recv_sem=recv.at[n_now, k - 1], device_id=(px, py, pc), device_id_type=MESH)

        small_out[me] = small_in[...]
        small_sends = [small_copy(k) for k in range(1, N_DEV)]
        for cp in small_sends:
            cp.start()

        mine, first, passed = [], [], []
        for a in range(n):
            own = pltpu.make_async_copy(stage[a], blocks[a](outs[a], me), local.at[a])
            own.start()
            mine.append(own)
            if a >= n_now:
                continue
            sends = [copy(a, 0, me, sibling, src=stage[a])]
            sends += [copy(a, k, me, (*_chip(x, y, k), c), src=stage[a]) for k in (1, 2)]
            for cp in sends:
                cp.start()
            first += sends

        here = (x, y, c)
        across = [(*_chip(x, y, k), c) for k in (1, 2)]
        near = [[_block_of(*_chip(x, y, k), cc) for k in (1, 2)] for cc in (c, 1 - c)]
        far = [_block_of(*_chip(x, y, 3), cc) for cc in (c, 1 - c)]

        def launch(cp):
            cp.start()
            passed.append(cp)

        project(stage[0], me)
        copy(0, 0, _block_of(x, y, 1 - c), here).wait_recv()
        project_landed(_block_of(x, y, 1 - c))
        for a in range(n_now):
            for i in (0, 1):
                copy(a, 1 + i, near[0][i], here).wait_recv()
                launch(copy(a, 3 + i, near[0][i], across[1 - i], half=i))
                launch(copy(a, 5 + i, near[0][i], sibling))
            if a == 0:
                project_landed(near[0][0])
                project_landed(near[0][1])
        for i in (0, 1):
            copy(0, 5 + i, near[1][i], here).wait_recv()
            project_landed(near[1][i])
        for a in range(n_now):
            for i in (0, 1):
                copy(a, 3 + i, far[0], here, half=i).wait_recv()
                launch(copy(a, 7 + i, far[0], sibling, half=i))
            if a == 0:
                project_landed(far[0])
        for a in range(n_now):
            if a > 0:
                copy(a, 0, _block_of(x, y, 1 - c), here).wait_recv()
                for i in (0, 1):
                    copy(a, 5 + i, near[1][i], here).wait_recv()
            for i in (0, 1):
                copy(a, 7 + i, far[1], here, half=i).wait_recv()
            if a == 0:
                project_landed(far[1])
        for k in range(1, N_DEV):
            small_arrival(k).wait_recv()
        for cp in first + passed + small_sends:
            cp.wait_send()
        for done in mine + stores[-2:]:
            done.wait()

    out_shape = [jax.ShapeDtypeStruct(s, BF16) for s in full_shapes]
    out_shape += [jax.ShapeDtypeStruct((N_DEV, small_rows, LANES), F32), jax.ShapeDtypeStruct((t, full_shapes[0][1]), BF16),
                  jax.ShapeDtypeStruct(tokens.shape, BF16)]
    return pl.pallas_call(
        body, name="gather_weights", out_shape=out_shape,
        in_specs=[VMEM_SPEC] * (n + 3), out_specs=[HBM_SPEC] * n + [VMEM_SPEC, HBM_SPEC, VMEM_SPEC],
        scratch_shapes=[pltpu.VMEM(s.shape, BF16) for s in shards]
        + [pltpu.VMEM(shards[0].shape, BF16), pltpu.VMEM((2, t, shards[0].shape[1]), BF16),
           pltpu.SemaphoreType.DMA((n_now + 1, 9)), pltpu.SemaphoreType.DMA((n_now + 1, 9)),
           pltpu.SemaphoreType.DMA((n,)), pltpu.SemaphoreType.DMA(()), pltpu.SemaphoreType.DMA((2,))],
        compiler_params=_params(),
    )(*shards, small, tokens, gain)


def _gather_first(full, blocks, send, recv):
    x, y, c = _position()
    me = _block_of(x, y, c)
    peers = [(x, y, 1 - c)] + [(*_chip(x, y, k), c) for k in (1, 2, 3)]

    def copy(a, k, block):
        at = blocks[a](full[a], block)
        return pltpu.make_async_remote_copy(src_ref=at, dst_ref=at, send_sem=send[4 * a + k], recv_sem=recv[4 * a + k],
                                            device_id=peers[k], device_id_type=MESH)

    sends = [copy(a, k, me) for a in range(len(full)) for k in range(4)]
    arrivals = [copy(a, k, _block_of(*peers[k])) for a in range(len(full)) for k in range(4)]
    return sends, arrivals


def _gather_second(full, blocks, send, recv):
    x, y, c = _position()

    def copy(a, k, cc):
        at = blocks[a](full[a], _block_of(*_chip(x, y, k), cc))
        return pltpu.make_async_remote_copy(src_ref=at, dst_ref=at, send_sem=send[3 * a + k - 1],
                                            recv_sem=recv[3 * a + k - 1], device_id=(x, y, 1 - c), device_id_type=MESH)

    sends = [copy(a, k, c) for a in range(len(full)) for k in (1, 2, 3)]
    arrivals = [copy(a, k, 1 - c) for a in range(len(full)) for k in (1, 2, 3)]
    return sends, arrivals


def _split_call(body, name, arrays, sems_in, n_sems_out, after=None, token=False):
    n, m = len(arrays), len(sems_in)

    def kernel_body(*refs):
        outs = refs[n + m + (after is not None):]
        body(refs[:n], refs[n:n + m], outs[:n_sems_out])
        if token:
            outs[-1][...] = jnp.zeros_like(outs[-1])

    extra_in = [] if after is None else [after]
    outs = pl.pallas_call(
        kernel_body, name=name,
        out_shape=(*[pltpu.SemaphoreType.DMA(())] * n_sems_out, *[pltpu.HBM(a.shape, a.dtype) for a in arrays],
                   *([jax.ShapeDtypeStruct((SUBLANES, LANES), F32)] if token else [])),
        in_specs=[HBM_SPEC] * n + [SEM_SPEC] * m + [pl.BlockSpec(memory_space=pl.ANY)] * len(extra_in),
        out_specs=(*[SEM_SPEC] * n_sems_out, *[HBM_SPEC] * n, *([VMEM_SPEC] if token else [])),
        input_output_aliases={i: n_sems_out + i for i in range(n)},
        compiler_params=pltpu.CompilerParams(has_side_effects=DATAFLOW_EFFECT),
    )(*[pltpu.with_memory_space_constraint(a, pltpu.HBM) for a in arrays], *sems_in, *extra_in)
    sems, rest = list(outs[:n_sems_out]), list(outs[n_sems_out:])
    return (sems, rest[:n], rest[n]) if token else (sems, rest[:n])


def _gather_start(full, blocks, name):
    n = len(full)

    def body(arrays, _, sems):
        for cp in _gather_first(arrays, blocks, sems[:4 * n], sems[4 * n:])[0]:
            cp.start()

    sems, arrays, token = _split_call(body, name, full, [], 8 * n, token=True)
    return sems[:4 * n], sems[4 * n:], arrays, token


def _gather_forward(full, blocks, send_first, recv_first, after, name):
    n = len(full)

    def body(arrays, sems_in, sems):
        sends, arrivals = _gather_first(arrays, blocks, sems_in[:4 * n], sems_in[4 * n:])
        for cp in arrivals:
            cp.wait_recv()
        for cp in _gather_second(arrays, blocks, sems[:3 * n], sems[3 * n:])[0]:
            cp.start()
        for cp in sends:
            cp.wait_send()

    sems, arrays = _split_call(body, name, full, [*send_first, *recv_first], 6 * n, after=after)
    return sems[:3 * n], sems[3 * n:], arrays


def _gather_finish(full, blocks, send_second, recv_second, after, name):
    n = len(full)

    def body(arrays, sems_in, _):
        sends, arrivals = _gather_second(arrays, blocks, sems_in[:3 * n], sems_in[3 * n:])
        for cp in sends:
            cp.wait_send()
        for cp in arrivals:
            cp.wait_recv()

    return _split_call(body, name, full, [*send_second, *recv_second], 0, after=after)[1]


def _reduce_pair(grads, blocks, shard_shapes, name):
    n = len(grads)

    def body(*refs):
        ins, outs = refs[:n], refs[n:2 * n]
        got, own = refs[2 * n:3 * n], refs[3 * n:4 * n]
        send, recv, local = refs[4 * n:]
        x, y, c = _position()
        copies, loads = [], []
        for a in range(n):
            for k in range(4):
                chip = _chip(x, y, k)
                cp = pltpu.make_async_remote_copy(
                    src_ref=blocks[a](ins[a], _block_of(*chip, 1 - c)), dst_ref=got[a].at[k],
                    send_sem=send.at[a, k], recv_sem=recv.at[a, k], device_id=(x, y, 1 - c), device_id_type=MESH)
                cp.start()
                copies.append(cp)
                ld = pltpu.make_async_copy(blocks[a](ins[a], _block_of(*chip, c)), own[a].at[k], local.at[a, k])
                ld.start()
                loads.append(ld)
        for a in range(n):
            for k in range(4):
                loads[4 * a + k].wait()
                copies[4 * a + k].wait_recv()
                outs[a][k] = (own[a][k].astype(F32) + got[a][k].astype(F32)).astype(BF16)
        for cp in copies:
            cp.wait_send()

    slots = [(4,) + tuple(s) for s in shard_shapes]
    return pl.pallas_call(
        body, name=name, out_shape=[jax.ShapeDtypeStruct(s, BF16) for s in slots],
        in_specs=[HBM_SPEC] * n, out_specs=[VMEM_SPEC] * n,
        scratch_shapes=[pltpu.VMEM(s, BF16) for s in slots] * 2
        + [pltpu.SemaphoreType.DMA((n, 4)), pltpu.SemaphoreType.DMA((n, 4)), pltpu.SemaphoreType.DMA((n, 4))],
        compiler_params=_params(),
    )(*grads)


def _chip_copies(sums, lands, send, recv):
    x, y, c = _position()
    return [pltpu.make_async_remote_copy(
        src_ref=sums[a].at[k], dst_ref=lands[a].at[k - 1], send_sem=send[3 * a + k - 1], recv_sem=recv[3 * a + k - 1],
        device_id=(*_chip(x, y, k), c), device_id_type=MESH) for a in range(len(sums)) for k in (1, 2, 3)]


def _exchange_chips_start(pair_sums, name):
    n = len(pair_sums)
    lands = [pltpu.with_memory_space_constraint(lax.empty((3,) + tuple(p.shape[1:]), BF16), pltpu.HBM) for p in pair_sums]

    def body(*refs):
        sums, zones = refs[:n], refs[n:2 * n]
        send, recv = refs[2 * n:5 * n], refs[5 * n:8 * n]
        token = refs[-1]
        for cp in _chip_copies(sums, zones, send, recv):
            cp.start()
        token[...] = jnp.zeros_like(token)

    outs = pl.pallas_call(
        body, name=name,
        out_shape=(*[pltpu.SemaphoreType.DMA(())] * (6 * n),
                   *[pltpu.HBM(p.shape, BF16) for p in pair_sums], *[pltpu.HBM(z.shape, BF16) for z in lands],
                   jax.ShapeDtypeStruct((SUBLANES, LANES), F32)),
        in_specs=[HBM_SPEC] * (2 * n), out_specs=(*[SEM_SPEC] * (6 * n), *[HBM_SPEC] * (2 * n), VMEM_SPEC),
        input_output_aliases={i: 6 * n + i for i in range(2 * n)},
        compiler_params=pltpu.CompilerParams(has_side_effects=DATAFLOW_EFFECT),
    )(*[pltpu.with_memory_space_constraint(p, pltpu.HBM) for p in pair_sums], *lands)
    return outs[:3 * n], outs[3 * n:6 * n], outs[6 * n:7 * n], outs[7 * n:8 * n], outs[-1]


def _exchange_chips_wait(send, recv, sums, lands, after, name):
    n = len(sums)

    def body(*refs):
        sums_in, zones = refs[:n], refs[n:2 * n]
        send_in, recv_in = refs[2 * n:5 * n], refs[5 * n:8 * n]
        for cp in _chip_copies(sums_in, zones, send_in, recv_in):
            cp.wait_send()
            cp.wait_recv()

    outs = pl.pallas_call(
        body, name=name,
        out_shape=(*[pltpu.HBM(p.shape, BF16) for p in sums], *[pltpu.HBM(z.shape, BF16) for z in lands]),
        in_specs=[HBM_SPEC] * (2 * n) + [SEM_SPEC] * (6 * n) + [pl.BlockSpec(memory_space=pl.ANY)],
        out_specs=[HBM_SPEC] * (2 * n), input_output_aliases={i: i for i in range(2 * n)},
        compiler_params=pltpu.CompilerParams(has_side_effects=DATAFLOW_EFFECT),
    )(*sums, *lands, *send, *recv, after)
    return outs[:n], outs[n:]


def _small_copies(mine, land, send, recv):
    x, y, c = _position()
    me = _block_of(x, y, c)

    def peer(k):
        return (x + (k & 1)) % 2, (y + ((k >> 1) & 1)) % 2, (c + (k >> 2)) % 2

    def copy(k, slot):
        return pltpu.make_async_remote_copy(src_ref=mine, dst_ref=land.at[slot], send_sem=send[k - 1], recv_sem=recv[k - 1],
                                            device_id=peer(k), device_id_type=MESH)

    return [copy(k, me) for k in range(1, N_DEV)], [copy(k, _block_of(*peer(k))) for k in range(1, N_DEV)]


def _small_start(part, name):
    land = jnp.zeros((N_DEV,) + part.shape, F32)

    def body(arrays, _, sems):
        for cp in _small_copies(arrays[0], arrays[1], sems[:7], sems[7:])[0]:
            cp.start()

    sems, arrays, token = _split_call(body, name, [part, land], [], 14, token=True)
    return sems[:7], sems[7:], arrays[0], arrays[1], token


def _small_wait(send, recv, part, land, after, name):
    def body(arrays, sems_in, _):
        sends, arrivals = _small_copies(arrays[0], arrays[1], sems_in[:7], sems_in[7:])
        for cp in sends:
            cp.wait_send()
        for cp in arrivals:
            cp.wait_recv()

    return _split_call(body, name, [part, land], [*send, *recv], 0, after=after)[1]


def _small_sum(pairs, me):
    n = len(pairs)

    def body(me_ref, *refs):
        for i in range(n):
            mine, land, out = refs[2 * i], refs[2 * i + 1], refs[2 * n + i]
            total = jnp.zeros(mine.shape, F32)
            for d in range(N_DEV):
                total = total + land[d] + jnp.where(me_ref[0] == d, mine[...], 0.0)
            out[...] = total

    flat = [a for pair in pairs for a in pair]
    return pl.pallas_call(
        body, name="small_sum", out_shape=[jax.ShapeDtypeStruct(mine.shape, F32) for mine, _ in pairs],
        in_specs=[pl.BlockSpec(memory_space=pltpu.SMEM)] + [VMEM_SPEC] * (2 * n), out_specs=[VMEM_SPEC] * n,
        compiler_params=_params(),
    )(me.reshape(1).astype(jnp.int32), *flat)


def _section(s, t):
    return pl.BlockSpec((t, CB), lambda h, s=s: (0, s * (D_MODEL // CB) + h))


def _conv_mixer_fwd(proj, w_short):
    t = proj.shape[0]
    rc = _row_chunk(t)

    def body(b_ref, c_ref, x_ref, w_ref, y_ref, pad):
        pad[pl.ds(0, PAD), :] = jnp.zeros((PAD, CB), F32)
        for r0 in range(0, t, rc):
            rows = pl.ds(r0, rc)
            pad[pl.ds(PAD + r0, rc), :] = c_ref[rows, :].astype(F32) * x_ref[rows, :].astype(F32)
        w = w_ref[...]
        for r0 in range(0, t, rc):
            rows = pl.ds(r0, rc)
            y_ref[rows, :] = (b_ref[rows, :].astype(F32) * _conv_causal(pad, w, r0, rc, 3)).astype(BF16)

    return pl.pallas_call(
        body, name="conv_mixer_fwd", grid=(D_MODEL // CB,),
        out_shape=jax.ShapeDtypeStruct((t, D_MODEL), BF16),
        in_specs=[_section(0, t), _section(1, t), _section(2, t), pl.BlockSpec((3, CB), lambda h: (0, h))],
        out_specs=pl.BlockSpec((t, CB), lambda h: (0, h)),
        scratch_shapes=[pltpu.VMEM((t + PAD, CB), F32)],
        compiler_params=_params("parallel"),
    )(proj, proj, proj, w_short)


def _lru_gates(xl, wa, ba, wx, bx, ls, first_row):
    xb = xl.astype(BF16)
    ra = jax.nn.sigmoid(_dot(xb, wa) + ba)
    ia = jax.nn.sigmoid(_dot(xb, wx) + bx)
    la = LRU_C * ra * ls
    a = jnp.exp(la)
    one_minus = -_expm1_neg(2.0 * la)
    mult = jnp.where(first_row, 1.0, jnp.sqrt(one_minus))
    return xb, ra, ia, a, one_minus, mult


def _head_specs():
    vec = pl.BlockSpec((1, CB), lambda h: (0, h))
    mat = pl.BlockSpec((N_DEV, None, HEAD_DIM // N_DEV, HEAD_DIM), lambda h: (0, h, 0, 0))
    return vec, mat


def _lru_fwd(proj, w_conv, b_conv, wa, ba, wx, bx, lam):
    t = proj.shape[0]
    rc = _row_chunk(t)
    vec, mat = _head_specs()

    def body(lx_ref, ly_ref, wc_ref, bc_ref, wa_ref, ba_ref, wx_ref, bx_ref, lam_ref, yb_ref, hl_ref, a_ref, kept_ref,
             pad, u_s):
        pad[pl.ds(0, PAD), :] = jnp.zeros((PAD, CB), F32)
        for r0 in range(0, t, rc):
            pad[pl.ds(PAD + r0, rc), :] = lx_ref[pl.ds(r0, rc), :].astype(F32)
        wc, bc = wc_ref[...], bc_ref[...]
        wa_m, wx_m = wa_ref[...].reshape(HEAD_DIM, HEAD_DIM), wx_ref[...].reshape(HEAD_DIM, HEAD_DIM)
        ls = _log_sigmoid(lam_ref[...])
        for r0 in range(0, t, rc):
            rows = pl.ds(r0, rc)
            xl = _conv_causal(pad, wc, r0, rc, 4) + bc
            first = (lax.broadcasted_iota(jnp.int32, (rc, CB), 0) + r0) == 0
            xb, ra, ia, a, _, mult = _lru_gates(xl, wa_m, ba_ref[...], wx_m, bx_ref[...], ls, first)
            a_ref[rows, :] = a
            u_s[rows, :] = mult * (ia * xl)
            kept_ref[0, rows, :] = xb
            kept_ref[1, rows, :] = ra.astype(BF16)
            kept_ref[2, rows, :] = ia.astype(BF16)

        row = lax.broadcasted_iota(jnp.int32, (SUBLANES, CB), 0)

        def group(g, carry):
            r = pl.multiple_of(g * SUBLANES, SUBLANES)
            a_g, b_g = a_ref[pl.ds(r, SUBLANES), :], u_s[pl.ds(r, SUBLANES), :]
            for s in (1, 2, 4):
                keep = row >= s
                b_g = jnp.where(keep, a_g * pltpu.roll(b_g, s, 0) + b_g, b_g)
                a_g = jnp.where(keep, a_g * pltpu.roll(a_g, s, 0), a_g)
            h_g = b_g + a_g * carry
            hl_ref[pl.ds(r, SUBLANES), :] = h_g
            return jnp.broadcast_to(h_g[SUBLANES - 1:SUBLANES, :], (SUBLANES, CB))

        lax.fori_loop(0, t // SUBLANES, group, jnp.zeros((SUBLANES, CB), F32))
        for r0 in range(0, t, rc):
            rows = pl.ds(r0, rc)
            yb_ref[rows, :] = (hl_ref[rows, :] * _gelu(ly_ref[rows, :].astype(F32))).astype(BF16)

    blk = pl.BlockSpec((t, CB), lambda h: (0, h))
    res = jax.ShapeDtypeStruct((t, D_MODEL), F32)
    return pl.pallas_call(
        body, name="lru_fwd", grid=(N_HEADS,),
        out_shape=[jax.ShapeDtypeStruct((t, D_MODEL), BF16), res, res, jax.ShapeDtypeStruct((3, t, D_MODEL), BF16)],
        in_specs=[_section(3, t), _section(4, t), pl.BlockSpec((4, CB), lambda h: (0, h)), vec, mat, vec, mat, vec, vec],
        out_specs=[blk, blk, blk, pl.BlockSpec((3, t, CB), lambda h: (0, 0, h))],
        scratch_shapes=[pltpu.VMEM((t + PAD, CB), F32), pltpu.VMEM((t, CB), F32)],
        compiler_params=_params("parallel"),
    )(proj, proj, w_conv, b_conv, wa, ba, wx, bx, lam)


def _merge(y_a, y_b, proj, x, w_cb, w_lb, w_out, g2, g3):
    t = x.shape[0]
    tm = min(256, t)

    def body(ya_ref, yb_ref, gc_ref, gl_ref, x_ref, wcb_ref, wlb_ref, wo_ref, g2_ref, g3_ref,
             pa_ref, pb_ref, mg_ref, mix_ref, x1_ref, h2_ref):
        pa = _dot(ya_ref[...], wcb_ref[...]).astype(BF16)
        pb = _dot(yb_ref[...], wlb_ref[...]).astype(BF16)
        pa_ref[...] = pa
        pb_ref[...] = pb
        merged = (jax.nn.sigmoid(gc_ref[...].astype(F32)) * pa.astype(F32)
                  + jax.nn.sigmoid(gl_ref[...].astype(F32)) * pb.astype(F32)).astype(BF16)
        mg_ref[...] = merged
        mix = _dot(merged, wo_ref[...])
        mix_ref[...] = mix
        n2, _ = _rms_fwd(mix)
        x1 = x_ref[...] + n2 * g2_ref[...]
        x1_ref[...] = x1
        n3, _ = _rms_fwd(x1)
        h2_ref[...] = (n3 * g3_ref[...]).astype(BF16)

    row = pl.BlockSpec((tm, D_MODEL), lambda i: (i, 0))
    full = pl.BlockSpec((D_MODEL, D_MODEL), lambda i: (0, 0))
    vec = pl.BlockSpec((1, D_MODEL), lambda i: (0, 0))
    act = jax.ShapeDtypeStruct((t, D_MODEL), BF16)
    res = jax.ShapeDtypeStruct((t, D_MODEL), F32)
    return pl.pallas_call(
        body, name="merge_fwd", grid=(t // tm,), out_shape=[act, act, act, res, res, act],
        in_specs=[row, row, pl.BlockSpec((tm, D_MODEL), lambda i: (i, 5)), pl.BlockSpec((tm, D_MODEL), lambda i: (i, 6)),
                  row, full, full, full, vec, vec],
        out_specs=[row] * 6,
        compiler_params=_params("parallel"),
    )(y_a, y_b, proj, proj, x, w_cb, w_lb, w_out, g2, g3)


N_FF_BLOCKS = D_FF // CB


def _ffn_up(h2, w_up, w_conv, b_conv):
    t = h2.shape[0]
    rc = _row_chunk(t)
    nb = N_FF_BLOCKS

    def body(h_ref, w_ref, c_ref, b_ref, up_ref, act_ref, f_ref, pad, gate):
        k = pl.program_id(1)
        pad[pl.ds(0, PAD), :] = jnp.zeros((PAD, CB), F32)
        for r0 in range(0, t, rc):
            rows = pl.ds(r0, rc)
            up = _dot(h_ref[rows, :], w_ref[...]).astype(BF16)
            up_ref[rows, :] = up
            pad[pl.ds(PAD + r0, rc), :] = up.astype(F32)
        cw = c_ref[...]
        for r0 in range(0, t, rc):
            rows = pl.ds(r0, rc)
            act = _conv_causal(pad, cw, r0, rc, 3) + b_ref[...]
            act_ref[rows, :] = act.astype(BF16)

            @pl.when(k == 0)
            def _():
                gate[rows, :] = act

            @pl.when(k == 1)
            def _():
                f_ref[rows, :] = (_gelu(gate[rows, :]) * act).astype(BF16)

    half = lambda rows: pl.BlockSpec((rows, CB), lambda j, k: (0, nb * k + j))
    wide = jax.ShapeDtypeStruct((t, 2 * D_FF), BF16)
    return pl.pallas_call(
        body, name="ffn_up_fwd", grid=(nb, 2), out_shape=[wide, wide, jax.ShapeDtypeStruct((t, D_FF), BF16)],
        in_specs=[pl.BlockSpec((t, D_MODEL), lambda j, k: (0, 0)), half(D_MODEL), half(3), half(1)],
        out_specs=[half(t), half(t), pl.BlockSpec((t, CB), lambda j, k: (0, j))],
        scratch_shapes=[pltpu.VMEM((t + PAD, CB), F32), pltpu.VMEM((t, CB), F32)],
        compiler_params=_params("parallel", "arbitrary"),
    )(h2, w_up, w_conv, b_conv)


def _ffn_down(f, act, w_down, x1, target, g4):
    t = f.shape[0]
    tm = min(256, t)
    cc = 512

    def body(f_ref, act_ref, w_ref, x1_ref, tg_ref, g_ref, dy_ref, dout_ref, back_ref, dg_ref, loss_ref):
        @pl.when(pl.program_id(0) == 0)
        def _():
            dg_ref[...] = jnp.zeros_like(dg_ref)
            loss_ref[...] = jnp.zeros_like(loss_ref)
        out = _dot(f_ref[...], w_ref[...])
        n4, r4 = _rms_fwd(out)
        err = x1_ref[...] + n4 * g_ref[...] - tg_ref[...]
        loss_ref[...] += jnp.full(loss_ref.shape, 0.5 / D_MODEL, F32) * jnp.sum(err * err)
        dy = err * (1.0 / D_MODEL)
        dy_ref[...] = dy
        dg_ref[...] += jnp.sum(dy * n4, axis=0, keepdims=True)
        d_out = _rms_bwd(n4, r4, dy * g_ref[...]).astype(BF16)
        dout_ref[...] = d_out
        for c0 in range(0, D_FF, cc):
            d_f = _dot_nt(d_out, w_ref[pl.ds(c0, cc), :])
            gelu, d_gelu = _gelu_and_grad(act_ref[:, pl.ds(c0, cc)].astype(F32))
            val = act_ref[:, pl.ds(D_FF + c0, cc)].astype(F32)
            back_ref[:, pl.ds(c0, cc)] = (d_f * val * d_gelu).astype(BF16)
            back_ref[:, pl.ds(D_FF + c0, cc)] = (d_f * gelu).astype(BF16)

    row = pl.BlockSpec((tm, D_MODEL), lambda i: (i, 0))
    wide = pl.BlockSpec((tm, 2 * D_FF), lambda i: (i, 0))
    vec = pl.BlockSpec((1, D_MODEL), lambda i: (0, 0))
    return pl.pallas_call(
        body, name="ffn_down_fwd_bwd", grid=(t // tm,),
        out_shape=[jax.ShapeDtypeStruct((t, D_MODEL), F32), jax.ShapeDtypeStruct((t, D_MODEL), BF16),
                   jax.ShapeDtypeStruct((t, 2 * D_FF), BF16), jax.ShapeDtypeStruct((1, D_MODEL), F32),
                   jax.ShapeDtypeStruct((SUBLANES, LANES), F32)],
        in_specs=[pl.BlockSpec((tm, D_FF), lambda i: (i, 0)), wide, pl.BlockSpec((D_FF, D_MODEL), lambda i: (0, 0)),
                  row, row, vec],
        out_specs=[row, row, wide, vec, pl.BlockSpec((SUBLANES, LANES), lambda i: (0, 0))],
        compiler_params=_params("arbitrary"),
    )(f, act, w_down, x1, target, g4)


def _grad_tn(a, b, bm, name):
    t, m = a.shape
    n = b.shape[1]

    def body(a_ref, b_ref, o_ref):
        o_ref[...] = _dot_tn(a_ref[...], b_ref[...]).astype(BF16)

    return pl.pallas_call(
        body, name=name, grid=(m // bm,), out_shape=jax.ShapeDtypeStruct((m, n), BF16),
        in_specs=[pl.BlockSpec((t, bm), lambda i: (0, i)), pl.BlockSpec((t, n), lambda i: (0, 0))],
        out_specs=pl.BlockSpec((bm, n), lambda i: (i, 0)),
        compiler_params=_params("parallel"),
    )(a, b)


def _ffn_up_bwd(up, back, w_conv, h2, w_up):
    t = h2.shape[0]
    rc = _row_chunk(t)
    nb = N_FF_BLOCKS

    def body(up_ref, back_ref, c_ref, h_ref, w_ref, dw_ref, dcw_ref, dcb_ref, dh_ref, pad, after, d_up):
        @pl.when((pl.program_id(0) == 0) & (pl.program_id(1) == 0))
        def _():
            dh_ref[...] = jnp.zeros_like(dh_ref)
        pad[pl.ds(0, PAD), :] = jnp.zeros((PAD, CB), F32)
        after[pl.ds(t, PAD), :] = jnp.zeros((PAD, CB), F32)
        for r0 in range(0, t, rc):
            pad[pl.ds(PAD + r0, rc), :] = up_ref[pl.ds(r0, rc), :].astype(F32)
            after[pl.ds(r0, rc), :] = back_ref[pl.ds(r0, rc), :].astype(F32)
        cw = c_ref[...]
        taps = [jnp.zeros((1, CB), F32)] * 3
        bias = jnp.zeros((1, CB), F32)
        for r0 in range(0, t, rc):
            rows = pl.ds(r0, rc)
            d = _conv_anticausal(after, cw, r0, rc, 3).astype(BF16)
            d_up[rows, :] = d
            dh_ref[rows, :] += _dot_nt(d, w_ref[...])
            g = after[rows, :]
            taps = [acc + new for acc, new in zip(taps, _conv_wgrad(g, pad, r0, rc, 3))]
            bias = bias + jnp.sum(g, axis=0, keepdims=True)
        dw_ref[...] = _dot_tn(h_ref[...], d_up[...]).astype(BF16)
        dcw_ref[...] = jnp.concatenate(taps, axis=0)
        dcb_ref[...] = bias

    half = lambda rows: pl.BlockSpec((rows, CB), lambda j, k: (0, nb * k + j))
    whole = pl.BlockSpec((t, D_MODEL), lambda j, k: (0, 0))
    return pl.pallas_call(
        body, name="ffn_up_bwd", grid=(nb, 2),
        out_shape=[jax.ShapeDtypeStruct((D_MODEL, 2 * D_FF), BF16), jax.ShapeDtypeStruct((3, 2 * D_FF), F32),
                   jax.ShapeDtypeStruct((1, 2 * D_FF), F32), jax.ShapeDtypeStruct((t, D_MODEL), F32)],
        in_specs=[half(t), half(t), half(3), whole, half(D_MODEL)],
        out_specs=[half(D_MODEL), half(3), half(1), whole],
        scratch_shapes=[pltpu.VMEM((t + PAD, CB), F32), pltpu.VMEM((t + PAD, CB), F32), pltpu.VMEM((t, CB), BF16)],
        compiler_params=_params("arbitrary", "arbitrary"),
    )(up, back, w_conv, h2, w_up)


def _merge_bwd(dy, d_h2, x1, mix, g3, g2, w_out, w_cb, w_lb, pa, pb, proj):
    t = dy.shape[0]
    tm = min(256, t)

    def body(dy_ref, dh2_ref, x1_ref, mix_ref, g3_ref, g2_ref, wo_ref, wcb_ref, wlb_ref, pa_ref, pb_ref, gc_ref, gl_ref,
             dx1_ref, dmix_ref, dpa_ref, dpb_ref, dya_ref, dyb_ref, dgate_ref, dg3_ref, dg2_ref):
        @pl.when(pl.program_id(0) == 0)
        def _():
            dg3_ref[...] = jnp.zeros_like(dg3_ref)
            dg2_ref[...] = jnp.zeros_like(dg2_ref)
        n3, r3 = _rms_fwd(x1_ref[...])
        d_h2 = dh2_ref[...]
        dg3_ref[...] += jnp.sum(d_h2 * n3, axis=0, keepdims=True)
        dx1 = dy_ref[...] + _rms_bwd(n3, r3, d_h2 * g3_ref[...])
        dx1_ref[...] = dx1
        n2, r2 = _rms_fwd(mix_ref[...])
        dg2_ref[...] += jnp.sum(dx1 * n2, axis=0, keepdims=True)
        d_mix = _rms_bwd(n2, r2, dx1 * g2_ref[...]).astype(BF16)
        dmix_ref[...] = d_mix
        d_merged = _dot_nt(d_mix, wo_ref[...])
        sc = jax.nn.sigmoid(gc_ref[...].astype(F32))
        sl = jax.nn.sigmoid(gl_ref[...].astype(F32))
        d_pa = (d_merged * sc).astype(BF16)
        d_pb = (d_merged * sl).astype(BF16)
        dpa_ref[...] = d_pa
        dpb_ref[...] = d_pb
        dgate_ref[0] = (d_merged * pa_ref[...].astype(F32) * sc * (1.0 - sc)).astype(BF16)
        dgate_ref[1] = (d_merged * pb_ref[...].astype(F32) * sl * (1.0 - sl)).astype(BF16)
        dya_ref[...] = _dot_nt(d_pa, wcb_ref[...]).astype(BF16)
        dyb_ref[...] = _dot_nt(d_pb, wlb_ref[...]).astype(BF16)

    row = pl.BlockSpec((tm, D_MODEL), lambda i: (i, 0))
    full = pl.BlockSpec((D_MODEL, D_MODEL), lambda i: (0, 0))
    vec = pl.BlockSpec((1, D_MODEL), lambda i: (0, 0))
    act = jax.ShapeDtypeStruct((t, D_MODEL), BF16)
    small = jax.ShapeDtypeStruct((1, D_MODEL), F32)
    return pl.pallas_call(
        body, name="merge_bwd", grid=(t // tm,),
        out_shape=[jax.ShapeDtypeStruct((t, D_MODEL), F32), act, act, act, act, act,
                   jax.ShapeDtypeStruct((2, t, D_MODEL), BF16), small, small],
        in_specs=[row, row, row, row, vec, vec, full, full, full, row, row,
                  pl.BlockSpec((tm, D_MODEL), lambda i: (i, 5)), pl.BlockSpec((tm, D_MODEL), lambda i: (i, 6))],
        out_specs=[row] * 6 + [pl.BlockSpec((2, tm, D_MODEL), lambda i: (0, i, 0)), vec, vec],
        compiler_params=_params("arbitrary"),
    )(dy, d_h2, x1, mix, g3, g2, w_out, w_cb, w_lb, pa, pb, proj, proj)


def _conv_mixer_bwd(proj, d_ya, w_short):
    t = proj.shape[0]
    rc = _row_chunk(t)

    def body(b_ref, c_ref, x_ref, dy_ref, w_ref, d_ref, dw_ref, pad, back):
        pad[pl.ds(0, PAD), :] = jnp.zeros((PAD, CB), F32)
        back[pl.ds(t, PAD), :] = jnp.zeros((PAD, CB), F32)
        for r0 in range(0, t, rc):
            rows = pl.ds(r0, rc)
            pad[pl.ds(PAD + r0, rc), :] = c_ref[rows, :].astype(F32) * x_ref[rows, :].astype(F32)
        w = w_ref[...]
        for r0 in range(0, t, rc):
            rows = pl.ds(r0, rc)
            d_y = dy_ref[rows, :].astype(F32)
            d_ref[0, rows, :] = (d_y * _conv_causal(pad, w, r0, rc, 3)).astype(BF16)
            back[rows, :] = d_y * b_ref[rows, :].astype(F32)
        taps = [jnp.zeros((1, CB), F32)] * 3
        for r0 in range(0, t, rc):
            rows = pl.ds(r0, rc)
            d_u = _conv_anticausal(back, w, r0, rc, 3)
            d_ref[1, rows, :] = (d_u * x_ref[rows, :].astype(F32)).astype(BF16)
            d_ref[2, rows, :] = (d_u * c_ref[rows, :].astype(F32)).astype(BF16)
            taps = [acc + new for acc, new in zip(taps, _conv_wgrad(back[rows, :], pad, r0, rc, 3))]
        dw_ref[...] = jnp.concatenate(taps, axis=0)

    blk = pl.BlockSpec((t, CB), lambda h: (0, h))
    return pl.pallas_call(
        body, name="conv_mixer_bwd", grid=(D_MODEL // CB,),
        out_shape=[jax.ShapeDtypeStruct((3, t, D_MODEL), BF16), jax.ShapeDtypeStruct((3, D_MODEL), F32)],
        in_specs=[_section(0, t), _section(1, t), _section(2, t), blk, pl.BlockSpec((3, CB), lambda h: (0, h))],
        out_specs=[pl.BlockSpec((3, t, CB), lambda h: (0, 0, h)), pl.BlockSpec((3, CB), lambda h: (0, h))],
        scratch_shapes=[pltpu.VMEM((t + PAD, CB), F32), pltpu.VMEM((t + PAD, CB), F32)],
        compiler_params=_params("parallel"),
    )(proj, proj, proj, d_ya, w_short)


LRU_SMALL_ROWS = 8


def _lru_bwd(proj, hl, a_all, kept, d_yb, w_conv, wa, wx, lam):
    t = proj.shape[0]
    rc = _row_chunk(t)
    vec, mat = _head_specs()

    def body(lx_ref, ly_ref, hl_ref, a_ref, kept_ref, dy_ref, wc_ref, wa_ref, wx_ref, lam_ref,
             d_ref, dwa_ref, dwx_ref, small_ref, pad, a_next, dh_s, h_prev, back, acc_a, acc_x):
        zeros = jnp.zeros((PAD, CB), F32)
        pad[pl.ds(0, PAD), :] = zeros
        h_prev[pl.ds(0, PAD), :] = zeros
        a_next[pl.ds(t, PAD), :] = zeros
        back[pl.ds(t, PAD), :] = zeros
        for r0 in range(0, t, rc):
            rows = pl.ds(r0, rc)
            pad[pl.ds(PAD + r0, rc), :] = lx_ref[rows, :].astype(F32)
            h_prev[pl.ds(PAD + r0, rc), :] = hl_ref[rows, :]
            a_next[pl.ds(PAD - 1 + r0, rc), :] = a_ref[rows, :]
            act, d_act = _gelu_and_grad(ly_ref[rows, :].astype(F32))
            d_y = dy_ref[rows, :].astype(F32)
            dh_s[rows, :] = d_y * act
            d_ref[1, rows, :] = (d_y * hl_ref[rows, :] * d_act).astype(BF16)
        wc = wc_ref[...]
        wa_m, wx_m = wa_ref[...].reshape(HEAD_DIM, HEAD_DIM), wx_ref[...].reshape(HEAD_DIM, HEAD_DIM)
        ls = _log_sigmoid(lam_ref[...])

        row = lax.broadcasted_iota(jnp.int32, (SUBLANES, CB), 0)
        groups = t // SUBLANES

        def group(i, carry):
            r = pl.multiple_of((groups - 1 - i) * SUBLANES, SUBLANES)
            a_g, b_g = a_next[pl.ds(PAD + r, SUBLANES), :], dh_s[pl.ds(r, SUBLANES), :]
            for s in (1, 2, 4):
                keep = row < SUBLANES - s
                b_g = jnp.where(keep, a_g * pltpu.roll(b_g, SUBLANES - s, 0) + b_g, b_g)
                a_g = jnp.where(keep, a_g * pltpu.roll(a_g, SUBLANES - s, 0), a_g)
            d_g = b_g + a_g * carry
            dh_s[pl.ds(r, SUBLANES), :] = d_g
            return jnp.broadcast_to(d_g[0:1, :], (SUBLANES, CB))

        lax.fori_loop(0, groups, group, jnp.zeros((SUBLANES, CB), F32))

        acc_a[...] = jnp.zeros_like(acc_a)
        acc_x[...] = jnp.zeros_like(acc_x)
        d_ba = d_bx = d_ls = jnp.zeros((1, CB), F32)
        for r0 in range(0, t, rc):
            rows = pl.ds(r0, rc)
            first = (lax.broadcasted_iota(jnp.int32, (rc, CB), 0) + r0) == 0
            xb, a = kept_ref[0, rows, :], a_ref[rows, :]
            xl, ra, ia = xb.astype(F32), kept_ref[1, rows, :].astype(F32), kept_ref[2, rows, :].astype(F32)
            a_sq = a * a
            mult = jnp.where(first, 1.0, jnp.sqrt(1.0 - a_sq))
            d_h = dh_s[rows, :]
            d_a = d_h * _rows_back(h_prev, r0, rc, 1)
            d_mult = d_h * ia * xl
            d_ia = d_h * mult * xl
            d_xl = d_h * mult * ia
            d_la = d_a * a + d_mult * jnp.where(first, 0.0, -a_sq / mult)
            d_ls = d_ls + jnp.sum(d_la * ra, axis=0, keepdims=True) * LRU_C
            d_za = d_la * (LRU_C * ls) * ra * (1.0 - ra)
            d_zx = d_ia * ia * (1.0 - ia)
            d_ba = d_ba + jnp.sum(d_za, axis=0, keepdims=True)
            d_bx = d_bx + jnp.sum(d_zx, axis=0, keepdims=True)
            d_za, d_zx = d_za.astype(BF16), d_zx.astype(BF16)
            acc_a[...] += _dot_tn(xb, d_za)
            acc_x[...] += _dot_tn(xb, d_zx)
            back[rows, :] = d_xl + _dot_nt(d_za, wa_m) + _dot_nt(d_zx, wx_m)
        taps = [jnp.zeros((1, CB), F32)] * 4
        d_bc = jnp.zeros((1, CB), F32)
        for r0 in range(0, t, rc):
            rows = pl.ds(r0, rc)
            d_ref[0, rows, :] = _conv_anticausal(back, wc, r0, rc, 4).astype(BF16)
            g = back[rows, :]
            taps = [acc + new for acc, new in zip(taps, _conv_wgrad(g, pad, r0, rc, 4))]
            d_bc = d_bc + jnp.sum(g, axis=0, keepdims=True)
        d_lam = d_ls * jax.nn.sigmoid(-lam_ref[...])
        small_ref[...] = jnp.concatenate(taps + [d_bc, d_ba, d_bx, d_lam], axis=0)
        dwa_ref[...] = acc_a[...].reshape(N_DEV, HEAD_DIM // N_DEV, HEAD_DIM).astype(BF16)
        dwx_ref[...] = acc_x[...].reshape(N_DEV, HEAD_DIM // N_DEV, HEAD_DIM).astype(BF16)

    blk = pl.BlockSpec((t, CB), lambda h: (0, h))
    gate_grad = jax.ShapeDtypeStruct((N_DEV, N_HEADS, HEAD_DIM // N_DEV, HEAD_DIM), BF16)
    return pl.pallas_call(
        body, name="lru_bwd", grid=(N_HEADS,),
        out_shape=[jax.ShapeDtypeStruct((2, t, D_MODEL), BF16), gate_grad, gate_grad,
                   jax.ShapeDtypeStruct((LRU_SMALL_ROWS, D_MODEL), F32)],
        in_specs=[_section(3, t), _section(4, t), blk, blk, pl.BlockSpec((3, t, CB), lambda h: (0, 0, h)), blk,
                  pl.BlockSpec((4, CB), lambda h: (0, h)), mat, mat, vec],
        out_specs=[pl.BlockSpec((2, t, CB), lambda h: (0, 0, h)), mat, mat,
                   pl.BlockSpec((LRU_SMALL_ROWS, CB), lambda h: (0, h))],
        scratch_shapes=[pltpu.VMEM((t + PAD, CB), F32), pltpu.VMEM((t + PAD, CB), F32), pltpu.VMEM((t, CB), F32),
                        pltpu.VMEM((t + PAD, CB), F32), pltpu.VMEM((t + PAD, CB), F32),
                        pltpu.VMEM((HEAD_DIM, HEAD_DIM), F32), pltpu.VMEM((HEAD_DIM, HEAD_DIM), F32)],
        compiler_params=_params("parallel"),
    )(proj, proj, hl, a_all, kept, d_yb, w_conv, wa, wx, lam)


def _stack_maps(halves):
    def conv(sec, part):
        return jnp.minimum(sec, 2), jnp.where(sec < 3, part, halves - 1)

    def lru(sec, part):
        return jnp.clip(sec - 3, 0, 1), jnp.where(sec < 3, 0, jnp.where(sec < 5, part, halves - 1))

    def gate(sec, part):
        return jnp.clip(sec - 5, 0, 1), jnp.where(sec < 5, 0, part)

    return conv, lru, gate


def _pick_stack(sec, refs, fn):
    @pl.when(sec < 3)
    def _():
        fn(refs[0])

    @pl.when((sec >= 3) & (sec < 5))
    def _():
        fn(refs[1])

    @pl.when(sec >= 5)
    def _():
        fn(refs[2])


def _in_proj_wgrad(h, d_conv, d_lru, d_gate):
    t = h.shape[0]
    halves, bn = 2, D_MODEL // 2
    maps = _stack_maps(halves)

    def body(h_ref, dc_ref, dl_ref, dg_ref, o_ref):
        def emit(ref):
            o_ref[...] = _dot_tn(h_ref[...], ref[...]).astype(BF16)
        _pick_stack(pl.program_id(0) // halves, (dc_ref, dl_ref, dg_ref), emit)

    def spec(m):
        def index(s):
            stack, part = m(s // halves, s % halves)
            return stack, 0, part
        return pl.BlockSpec((None, t, bn), index)

    return pl.pallas_call(
        body, name="in_proj_wgrad", grid=(7 * halves,), out_shape=jax.ShapeDtypeStruct((D_MODEL, IN_COLS), BF16),
        in_specs=[pl.BlockSpec((t, D_MODEL), lambda s: (0, 0))] + [spec(m) for m in maps],
        out_specs=pl.BlockSpec((D_MODEL, bn), lambda s: (0, s)),
        compiler_params=_params("arbitrary"),
    )(h, d_conv, d_lru, d_gate)


def _in_proj_xgrad(d_conv, d_lru, d_gate, w_in, x, dx1, g1):
    t = x.shape[0]
    tm = min(1024, t)
    maps = _stack_maps(1)

    def body(dc_ref, dl_ref, dg_ref, w_ref, x_ref, dx1_ref, g_ref, dx_ref, dgain_ref, acc):
        i, s = pl.program_id(0), pl.program_id(1)

        @pl.when((i == 0) & (s == 0))
        def _():
            dgain_ref[...] = jnp.zeros_like(dgain_ref)

        @pl.when(s == 0)
        def _():
            acc[...] = jnp.zeros_like(acc)

        def add(ref):
            acc[...] += _dot_nt(ref[...], w_ref[...])
        _pick_stack(s, (dc_ref, dl_ref, dg_ref), add)

        @pl.when(s == 6)
        def _():
            n1, r1 = _rms_fwd(x_ref[...])
            d_h = acc[...]
            dgain_ref[...] += jnp.sum(d_h * n1, axis=0, keepdims=True)
            dx_ref[...] = dx1_ref[...] + _rms_bwd(n1, r1, d_h * g_ref[...])

    def spec(m):
        def index(i, s):
            return m(s, 0)[0], i, 0
        return pl.BlockSpec((None, tm, D_MODEL), index)

    row = pl.BlockSpec((tm, D_MODEL), lambda i, s: (i, 0))
    vec = pl.BlockSpec((1, D_MODEL), lambda i, s: (0, 0))
    return pl.pallas_call(
        body, name="in_proj_xgrad", grid=(t // tm, 7),
        out_shape=[jax.ShapeDtypeStruct((t, D_MODEL), F32), jax.ShapeDtypeStruct((1, D_MODEL), F32)],
        in_specs=[spec(m) for m in maps] + [pl.BlockSpec((D_MODEL, D_MODEL), lambda i, s: (0, s)), row, row, vec],
        out_specs=[row, vec],
        scratch_shapes=[pltpu.VMEM((tm, D_MODEL), F32)],
        compiler_params=_params("arbitrary", "arbitrary"),
    )(d_conv, d_lru, d_gate, w_in, x, dx1, g1)


def _adamw(w, g, m, v):
    m = ADAM_B1 * m + (1.0 - ADAM_B1) * g
    v = ADAM_B2 * v + (1.0 - ADAM_B2) * (g * g)
    m_hat = m / (1.0 - ADAM_B1 ** ADAM_STEP)
    v_hat = v / (1.0 - ADAM_B2 ** ADAM_STEP)
    return -ADAM_LR * (m_hat / (jnp.sqrt(v_hat) + ADAM_EPS) + ADAM_WD * w), m, v


def _adam_large(w, m, v, own, others, name):
    shape = w.shape
    cols = shape[-1]
    w2, m2, v2 = (a.reshape(-1, cols) for a in (w, m, v))
    rows = w2.shape[0]
    own, others = own.reshape(4, rows, cols), others.reshape(3, rows, cols)
    rb = _row_block(rows, 512)

    def body(w_ref, m_ref, v_ref, own_ref, oth_ref, g_ref, d_ref, nm_ref, nv_ref):
        g = own_ref[...].astype(F32)
        for k in range(3):
            g = g + oth_ref[k].astype(F32)
        g_ref[...] = g
        d_ref[...], nm_ref[...], nv_ref[...] = _adamw(w_ref[...], g, m_ref[...], v_ref[...])

    blk = pl.BlockSpec((rb, cols), lambda i: (i, 0))
    res = jax.ShapeDtypeStruct((rows, cols), F32)
    outs = pl.pallas_call(
        body, name=name, grid=(rows // rb,), out_shape=[res] * 4,
        in_specs=[blk, blk, blk, pl.BlockSpec((None, rb, cols), lambda i: (0, i, 0)),
                  pl.BlockSpec((3, rb, cols), lambda i: (0, i, 0))],
        out_specs=[blk] * 4, compiler_params=_params("parallel"),
    )(w2, m2, v2, own, others)
    return [o.reshape(shape) for o in outs]


def _adam_small(ws, gs, ms, vs):
    n = len(ws)

    def body(*refs):
        w_refs, g_refs, m_refs, v_refs = (refs[i * n:(i + 1) * n] for i in range(4))
        outs = refs[4 * n:]
        for i in range(n):
            d, m, v = _adamw(w_refs[i][...], g_refs[i][...], m_refs[i][...], v_refs[i][...])
            outs[i][...], outs[n + i][...], outs[2 * n + i][...] = d, m, v

    shapes = [jax.ShapeDtypeStruct(w.shape, F32) for w in ws]
    outs = pl.pallas_call(
        body, name="adam_small", out_shape=shapes * 3,
        in_specs=[VMEM_SPEC] * (4 * n), out_specs=[VMEM_SPEC] * (3 * n), compiler_params=_params(),
    )(*ws, *gs, *ms, *vs)
    return outs[:n], outs[n:2 * n], outs[2 * n:]


def _pack_rows(pieces):
    tile = SUBLANES * LANES
    return jnp.concatenate([jnp.pad(p.reshape(-1), (0, (-p.size) % tile)).reshape(-1, LANES) for p in pieces], axis=0)


def _packed_starts(sizes):
    tile = SUBLANES * LANES
    starts = [0]
    for s in sizes:
        starts.append(starts[-1] + (s + tile - 1) // tile * SUBLANES)
    return starts


def kernel(x, norm_mix_pre, norm_mix_post, norm_ffn_pre, norm_ffn_post, w_in, conv_short_w, w_conv_branch, lru_conv_w, lru_conv_b, lru_wa, lru_ba, lru_wx, lru_bx, lru_lambda, w_lru_branch, w_out, ffn_w_up, ffn_conv_w, ffn_conv_b, ffn_w_down, loss_target, m_norm_mix_pre, m_norm_mix_post, m_norm_ffn_pre, m_norm_ffn_post, m_w_in, m_conv_short_w, m_w_conv_branch, m_lru_conv_w, m_lru_conv_b, m_lru_wa, m_lru_ba, m_lru_wx, m_lru_bx, m_lru_lambda, m_w_lru_branch, m_w_out, m_ffn_w_up, m_ffn_conv_w, m_ffn_conv_b, m_ffn_w_down, v_norm_mix_pre, v_norm_mix_post, v_norm_ffn_pre, v_norm_ffn_post, v_w_in, v_conv_short_w, v_w_conv_branch, v_lru_conv_w, v_lru_conv_b, v_lru_wa, v_lru_ba, v_lru_wx, v_lru_bx, v_lru_lambda, v_w_lru_branch, v_w_out, v_ffn_w_up, v_ffn_conv_w, v_ffn_conv_b, v_ffn_w_down):
    t = x.shape[1]
    xi, yi, ci = _position()
    me = _block_of(xi, yi, ci)
    x2, target = x[0], loss_target[0]
    shard_in, shard_up = IN_COLS // N_DEV, 2 * D_FF // N_DEV
    shard_sq, shard_down, shard_head = D_MODEL // N_DEV, D_FF // N_DEV, HEAD_DIM // N_DEV

    names = ["w_in", "lru_wa", "lru_wx", "w_conv_branch", "w_lru_branch", "w_out", "ffn_w_up", "ffn_w_down"]
    large = [w_in[0], lru_wa[0], lru_wx[0], w_conv_branch[0], w_lru_branch[0], w_out[0], ffn_w_up[0], ffn_w_down[0]]
    blocks = [_cols(shard_in), _lead, _lead, _rows(shard_sq), _rows(shard_sq), _rows(shard_sq),
              _cols(shard_up), _rows(shard_down)]
    gate_full = (N_DEV, N_HEADS, shard_head, HEAD_DIM)
    full_shapes = [(D_MODEL, IN_COLS), gate_full, gate_full, (D_MODEL, D_MODEL), (D_MODEL, D_MODEL), (D_MODEL, D_MODEL),
                   (D_MODEL, 2 * D_FF), (D_FF, D_MODEL)]
    n_now = 3
    small_sharded = [conv_short_w, lru_conv_w, lru_ba, lru_bx, ffn_conv_w]
    small_mine = _pack_rows(small_sharded)
    small_at = _packed_starts([p.size for p in small_sharded])
    *gathered, small_all, proj, h = _gather_weights(large, blocks, full_shapes, small_mine, n_now, x2, norm_mix_pre)
    g_in, g_wa, g_wx = gathered[:n_now]
    later_blocks = blocks[n_now:]
    send1, recv1, later, gather_token = _gather_start(gathered[n_now:], later_blocks, "gather_start")

    def behind(token, operand):
        return operand + token[0:1, 0:1]

    def forward(lo, hi, after, tag):
        return _gather_forward(later[lo:hi], later_blocks[lo:hi], send1[4 * lo:4 * hi], recv1[4 * lo:4 * hi], after,
                               "gather_forward_" + tag)

    def finish(lo, hi, flight, after, tag):
        return _gather_finish(flight[2], later_blocks[lo:hi], flight[0], flight[1], after, "gather_finish_" + tag)

    def cols_of(r0, n, width):
        part = small_all[:, r0:r0 + n * width // LANES, :].reshape(N_DEV, n, width)
        return part.transpose(1, 0, 2).reshape(n, N_DEV * width)

    c_short = cols_of(small_at[0], 3, LANES)
    c_lru = cols_of(small_at[1], 4, LANES)
    b_a = cols_of(small_at[2], N_HEADS, shard_head).reshape(1, D_MODEL)
    b_x = cols_of(small_at[3], N_HEADS, shard_head).reshape(1, D_MODEL)
    c_ffn = cols_of(small_at[4], 3, shard_up)

    y_a = _conv_mixer_fwd(proj, behind(gather_token, c_short))
    y_b, hl, decay, lru_kept = _lru_fwd(proj, behind(gather_token, c_lru), lru_conv_b, g_wa, b_a, g_wx, b_x, lru_lambda)
    flight_mix_w = forward(0, 3, y_b, "mix")
    g_cb, g_lb, g_out = finish(0, 3, flight_mix_w, y_b, "mix")
    pa, pb, merged, mix, x1, h2 = _merge(y_a, y_b, proj, x2, g_cb, g_lb, g_out, norm_mix_post, norm_ffn_pre)
    flight_up_w = forward(3, 4, h2, "up")
    (g_up,) = finish(3, 4, flight_up_w, h2, "up")
    up, act, f = _ffn_up(h2, g_up, c_ffn, ffn_conv_b)
    flight_down_w = forward(4, 5, f, "down")
    (g_down,) = finish(4, 5, flight_down_w, f, "down")
    dy, d_out, d_act, dg4, loss_part = _ffn_down(f, act, g_down, x1, target, norm_ffn_post)

    block_of = dict(zip(names, blocks))
    shard_shapes = {"w_in": (D_MODEL, shard_in), "w_conv_branch": (shard_sq, D_MODEL), "w_lru_branch": (shard_sq, D_MODEL),
                    "w_out": (shard_sq, D_MODEL), "lru_wa": (N_HEADS, shard_head, HEAD_DIM),
                    "lru_wx": (N_HEADS, shard_head, HEAD_DIM), "ffn_w_up": (D_MODEL, shard_up),
                    "ffn_w_down": (shard_down, D_MODEL)}

    def reduce_start(tag, grads):
        keys = list(grads)
        sums = _reduce_pair([grads[k] for k in keys], [block_of[k] for k in keys], [shard_shapes[k] for k in keys],
                            "reduce_pair_" + tag)
        return (keys,) + _exchange_chips_start(sums, "reduce_chip_start_" + tag)

    gw_down = _grad_tn(f, d_out, min(512, D_FF), "ffn_down_wgrad")
    flight_down = reduce_start("down", {"ffn_w_down": gw_down})
    gw_up, gc_ffn, gb_ffn, d_h2 = _ffn_up_bwd(up, d_act, behind(flight_down[-1], c_ffn), h2, g_up)
    flight_up = reduce_start("up", {"ffn_w_up": gw_up})
    dx1, d_mix, d_pa, d_pb, d_ya, d_yb, d_gate, dg3, dg2 = _merge_bwd(
        dy, d_h2, x1, mix, behind(flight_up[-1], norm_ffn_pre), norm_mix_post, g_out, g_cb, g_lb, pa, pb, proj)
    gw_out = _grad_tn(merged, d_mix, CB, "w_out_wgrad")
    gw_cb = _grad_tn(y_a, d_pa, CB, "w_conv_branch_wgrad")
    gw_lb = _grad_tn(y_b, d_pb, CB, "w_lru_branch_wgrad")
    flight_mix = reduce_start("mix", {"w_conv_branch": gw_cb, "w_lru_branch": gw_lb, "w_out": gw_out})
    d_conv, gc_short = _conv_mixer_bwd(proj, d_ya, behind(flight_mix[-1], c_short))
    d_lru, gw_a, gw_x, g_lru_small = _lru_bwd(proj, hl, decay, lru_kept, d_yb, c_lru, g_wa, g_wx, lru_lambda)
    early = [dg2, dg3, dg4, g_lru_small[4:5], g_lru_small[7:8], gb_ffn, gc_short, g_lru_small[0:4],
             g_lru_small[5:6], g_lru_small[6:7], gc_ffn, loss_part]
    flight_small = _small_start(_pack_rows(early), "small_start")
    gw_in = _in_proj_wgrad(h, d_conv, d_lru, d_gate)
    flight_in = reduce_start("in", {"lru_wa": gw_a, "lru_wx": gw_x, "w_in": gw_in})
    dx, dg1 = _in_proj_xgrad(d_conv, d_lru, d_gate, g_in, x2, dx1,
                             behind(flight_small[-1], behind(flight_in[-1], norm_mix_pre)))
    flight_late = _small_start(_pack_rows([dg1]), "small_start_late")

    moments ={"w_in": (m_w_in, v_w_in), "w_conv_branch": (m_w_conv_branch, v_w_conv_branch),
               "w_lru_branch": (m_w_lru_branch, v_w_lru_branch), "w_out": (m_w_out, v_w_out),
               "lru_wa": (m_lru_wa, v_lru_wa), "lru_wx": (m_lru_wx, v_lru_wx), "ffn_w_up": (m_ffn_w_up, v_ffn_w_up),
               "ffn_w_down": (m_ffn_w_down, v_ffn_w_down)}
    weights = {"w_in": w_in, "w_conv_branch": w_conv_branch, "w_lru_branch": w_lru_branch, "w_out": w_out,
               "lru_wa": lru_wa, "lru_wx": lru_wx, "ffn_w_up": ffn_w_up, "ffn_w_down": ffn_w_down}
    out_g, out_d, out_m, out_v = {}, {}, {}, {}

    after = flight_late[-1]
    for tag, (keys, send, recv, sums, lands, _) in (("down", flight_down), ("up", flight_up), ("mix", flight_mix),
                                                    ("in", flight_in)):
        sums, others = _exchange_chips_wait(send, recv, sums, lands, after, "reduce_chip_wait_" + tag)
        for k, own, oth in zip(keys, sums, others):
            out_g[k], out_d[k], out_m[k], out_v[k] = _adam_large(weights[k], *moments[k], own, oth, "adam_" + k)
        after = out_d[keys[-1]]

    total, total_late = _small_sum([_small_wait(*flight_small[:4], after, "small_wait"),
                                    _small_wait(*flight_late[:4], after, "small_wait_late")], me)
    sizes = [p.size for p in early]
    starts = _packed_starts(sizes)

    def piece(i, shape):
        if i == 0:
            return total_late.reshape(-1)[:D_MODEL].reshape(shape)
        return total[starts[i - 1]:starts[i]].reshape(-1)[:sizes[i - 1]].reshape(shape)

    loss = total[starts[11], 0]

    def col_shard(full, width):
        return lax.dynamic_slice_in_dim(full, me * width, width, axis=1)

    def head_shard(full):
        return lax.dynamic_slice_in_dim(full.reshape(N_HEADS, HEAD_DIM), me * shard_head, shard_head, axis=1)

    small_names = ["norm_mix_pre", "norm_mix_post", "norm_ffn_pre", "norm_ffn_post", "lru_conv_b", "lru_lambda",
                   "ffn_conv_b", "conv_short_w", "lru_conv_w", "lru_ba", "lru_bx", "ffn_conv_w"]
    small_g = [piece(0, (1, D_MODEL)), piece(1, (1, D_MODEL)), piece(2, (1, D_MODEL)), piece(3, (1, D_MODEL)),
               piece(4, (1, D_MODEL)), piece(5, (1, D_MODEL)), piece(6, (1, 2 * D_FF)),
               col_shard(piece(7, (3, D_MODEL)), LANES), col_shard(piece(8, (4, D_MODEL)), LANES),
               head_shard(piece(9, (1, D_MODEL))), head_shard(piece(10, (1, D_MODEL))),
               col_shard(piece(11, (3, 2 * D_FF)), shard_up)]
    small_w = [norm_mix_pre, norm_mix_post, norm_ffn_pre, norm_ffn_post, lru_conv_b, lru_lambda, ffn_conv_b,
               conv_short_w[0], lru_conv_w[0], lru_ba[0], lru_bx[0], ffn_conv_w[0]]
    small_m = [m_norm_mix_pre, m_norm_mix_post, m_norm_ffn_pre, m_norm_ffn_post, m_lru_conv_b, m_lru_lambda,
               m_ffn_conv_b, m_conv_short_w[0], m_lru_conv_w[0], m_lru_ba[0], m_lru_bx[0], m_ffn_conv_w[0]]
    small_v = [v_norm_mix_pre, v_norm_mix_post, v_norm_ffn_pre, v_norm_ffn_post, v_lru_conv_b, v_lru_lambda,
               v_ffn_conv_b, v_conv_short_w[0], v_lru_conv_w[0], v_lru_ba[0], v_lru_bx[0], v_ffn_conv_w[0]]
    s_d, s_m, s_v = _adam_small(small_w, small_g, small_m, small_v)
    for i, name in enumerate(small_names):
        shape = small_w[i].shape if i < 7 else (1,) + small_w[i].shape
        out_g[name] = small_g[i].reshape(shape)
        out_d[name], out_m[name], out_v[name] = s_d[i].reshape(shape), s_m[i].reshape(shape), s_v[i].reshape(shape)

    order = ["norm_mix_pre", "norm_mix_post", "norm_ffn_pre", "norm_ffn_post", "w_in", "conv_short_w", "w_conv_branch",
             "lru_conv_w", "lru_conv_b", "lru_wa", "lru_ba", "lru_wx", "lru_bx", "lru_lambda", "w_lru_branch", "w_out",
             "ffn_w_up", "ffn_conv_w", "ffn_conv_b", "ffn_w_down"]
    return (loss, dx.reshape(1, t, D_MODEL), *[out_g[k] for k in order], *[out_d[k] for k in order],
            *[out_m[k] for k in order], *[out_v[k] for k in order])
```

```python
import functools
import math

import jax
import jax.numpy as jnp
from jax import lax
from jax.experimental import pallas as pl
from jax.experimental.pallas import tpu as pltpu

F32 = jnp.float32
BF16 = jnp.bfloat16
MESH = pl.DeviceIdType.MESH

N_DEV = 8
D_MODEL = 1024
N_HEADS = 4
HEAD_DIM = D_MODEL // N_HEADS
D_FF = 3 * D_MODEL
IN_COLS = 7 * D_MODEL
LRU_C = 8.0
RMS_EPS = 1e-6
ADAM_LR = 0.001
ADAM_B1 = 0.9
ADAM_B2 = 0.999
ADAM_EPS = 1e-08
ADAM_WD = 0.01
ADAM_STEP = 10
GELU_K = math.sqrt(2.0 / math.pi)
GELU_C = 0.044715

LANES = 128
SUBLANES = 8
PAD = SUBLANES
VMEM_LIMIT = 56 * 1024 * 1024
CB = 256

HBM_SPEC = pl.BlockSpec(memory_space=pltpu.HBM)
SEM_SPEC = pl.BlockSpec(memory_space=pltpu.SEMAPHORE)
DATAFLOW_EFFECT = pltpu.SideEffectType.DATAFLOW_SIDE_EFFECTING
VMEM_SPEC = pl.BlockSpec(memory_space=pltpu.VMEM)


def _params(*sem):
    if sem:
        return pltpu.CompilerParams(dimension_semantics=sem, vmem_limit_bytes=VMEM_LIMIT)
    return pltpu.CompilerParams(vmem_limit_bytes=VMEM_LIMIT)


def _row_chunk(t):
    return min(256, t)


def _row_block(rows, cap):
    return next(rb for rb in range(min(cap, rows), 0, -16) if rows % rb == 0)


def _gelu(x):
    return 0.5 * x * (1.0 + jnp.tanh(GELU_K * (x + GELU_C * x * x * x)))


def _gelu_and_grad(x):
    t = jnp.tanh(GELU_K * (x + GELU_C * x * x * x))
    g = 0.5 * x * (1.0 + t)
    dg = 0.5 * (1.0 + t) + 0.5 * x * (1.0 - t * t) * GELU_K * (1.0 + 3.0 * GELU_C * x * x)
    return g, dg


def _expm1_neg(x):
    series = x * (1.0 + x * (0.5 + x * (1.0 / 6.0 + x * (1.0 / 24.0 + x * (1.0 / 120.0)))))
    return jnp.where(x > -0.05, series, jnp.exp(x) - 1.0)


def _log_sigmoid(x):
    return jnp.minimum(x, 0.0) - jnp.log1p(jnp.exp(-jnp.abs(x)))


def _dot(a, b):
    return jnp.dot(a, b, preferred_element_type=F32)


def _dot_nt(a, b):
    return lax.dot_general(a, b, (((1,), (1,)), ((), ())), preferred_element_type=F32)


def _dot_tn(a, b):
    return lax.dot_general(a, b, (((0,), (0,)), ((), ())), preferred_element_type=F32)


def _rms_fwd(x):
    r = lax.rsqrt(jnp.mean(x * x, axis=-1, keepdims=True) + RMS_EPS)
    return x * r, r


def _rms_bwd(n, r, gdy):
    return r * (gdy - n * jnp.mean(n * gdy, axis=-1, keepdims=True))


def _rows_back(pad_ref, r0, rows, j):
    cur = pad_ref[pl.ds(PAD + r0, rows), :]
    if j == 0:
        return cur
    before = pad_ref[pl.ds(PAD + r0 - SUBLANES, SUBLANES), :]
    row = lax.broadcasted_iota(jnp.int32, before.shape, 0)
    rolled = pltpu.roll(cur, j, 0)
    top = jnp.where(row < j, pltpu.roll(before, j, 0), rolled[0:SUBLANES, :])
    return jnp.concatenate([top, rolled[SUBLANES:, :]], axis=0)


def _rows_ahead(pad_ref, r0, rows, j):
    cur = pad_ref[pl.ds(r0, rows), :]
    if j == 0:
        return cur
    after = pad_ref[pl.ds(r0 + rows, SUBLANES), :]
    row = lax.broadcasted_iota(jnp.int32, after.shape, 0)
    rolled = pltpu.roll(cur, rows - j, 0)
    bottom = jnp.where(row >= SUBLANES - j, pltpu.roll(after, SUBLANES - j, 0), rolled[rows - SUBLANES:, :])
    return jnp.concatenate([rolled[:rows - SUBLANES, :], bottom], axis=0)


def _conv_causal(pad_ref, w, r0, rows, taps):
    acc = None
    for k in range(taps):
        term = w[k:k + 1, :] * _rows_back(pad_ref, r0, rows, taps - 1 - k)
        acc = term if acc is None else acc + term
    return acc


def _conv_anticausal(pad_ref, w, r0, rows, taps):
    acc = None
    for k in range(taps):
        term = w[k:k + 1, :] * _rows_ahead(pad_ref, r0, rows, taps - 1 - k)
        acc = term if acc is None else acc + term
    return acc


def _conv_wgrad(g, xpad_ref, r0, rows, taps):
    return [jnp.sum(g * _rows_back(xpad_ref, r0, rows, taps - 1 - k), axis=0, keepdims=True) for k in range(taps)]


def _position():
    return lax.axis_index("x"), lax.axis_index("y"), lax.axis_index("c")


def _block_of(x, y, c):
    return 4 * x + 2 * y + c


def _chip(x, y, k):
    return (x + (k & 1)) % 2, (y + (k >> 1)) % 2


def _cols(width):
    def at(ref, d, half=None):
        cols = pl.ds(pl.multiple_of(d * width, LANES), width)
        if half is None:
            return ref.at[:, cols]
        return ref.at[pl.ds(half * (ref.shape[0] // 2), ref.shape[0] // 2), cols]
    return at


def _rows(height):
    def at(ref, d, half=None):
        if half is None:
            return ref.at[pl.ds(pl.multiple_of(d * height, 16), height), :]
        return ref.at[pl.ds(pl.multiple_of(d * height + half * (height // 2), 16), height // 2), :]
    return at


def _lead(ref, d, half=None):
    if half is None:
        return ref.at[d]
    return ref.at[d, pl.ds(half * (ref.shape[1] // 2), ref.shape[1] // 2)]


def _gather_weights(shards, blocks, full_shapes, small, n_now, tokens, gain):
    n = len(shards)
    small_rows = small.shape[0]
    t = tokens.shape[0]
    rc = min(512, t)

    def body(*refs):
        ins, small_in, x_ref, g_ref = refs[:n], refs[n], refs[n + 1], refs[n + 2]
        outs, small_out, proj_ref, h_ref = refs[n + 3:2 * n + 3], refs[2 * n + 3], refs[2 * n + 4], refs[2 * n + 5]
        stage = refs[2 * n + 6:3 * n + 6]
        w_buf, p_buf, send, recv, local, w_sem, p_sem = refs[3 * n + 6:]
        x, y, c = _position()
        me = _block_of(x, y, c)
        sibling = (x, y, 1 - c)

        for a in range(n):
            stage[a][...] = ins[a][...].astype(BF16)
        for r0 in range(0, t, rc):
            normed, _ = _rms_fwd(x_ref[pl.ds(r0, rc), :])
            h_ref[pl.ds(r0, rc), :] = (normed * g_ref[...]).astype(BF16)
        stores = []

        def project(w_ref, block):
            i = len(stores)
            if i >= 2:
                stores[i - 2].wait()
            for r0 in range(0, t, rc):
                p_buf[i % 2, pl.ds(r0, rc), :] = _dot(h_ref[pl.ds(r0, rc), :], w_ref[...]).astype(BF16)
            st = pltpu.make_async_copy(p_buf.at[i % 2], blocks[0](proj_ref, block), p_sem.at[i % 2])
            st.start()
            stores.append(st)

        def project_landed(block):
            ld = pltpu.make_async_copy(blocks[0](outs[0], block), w_buf, w_sem)
            ld.start()
            ld.wait()
            project(w_buf, block)

        def copy(a, k, block, to, src=None, half=None):
            dst = blocks[a](outs[a], block, half)
            return pltpu.make_async_remote_copy(
                src_ref=dst if src is None else src, dst_ref=dst, send_sem=send.at[a, k], recv_sem=recv.at[a, k],
                device_id=to, device_id_type=MESH)

        def small_copy(k):
            px, py, pc = (x + (k & 1)) % 2, (y + ((k >> 1) & 1)) % 2, (c + (k >> 2)) % 2
            return pltpu.make_async_remote_copy(
                src_ref=small_in, dst_ref=small_out.at[me], send_sem=send.at[n_now, k - 1], recv_sem=recv.at[n_now, k - 1],
                device_id=(px, py, pc), device_id_type=MESH)

        def small_arrival(k):
            px, py, pc = (x + (k & 1)) % 2, (y + ((k >> 1) & 1)) % 2, (c + (k >> 2)) % 2
            return pltpu.make_async_remote_copy(
                src_ref=small_in, dst_ref=small_out.at[_block_of(px, py, pc)], send_sem=send.at[n_now, k - 1],
                recv_sem=recv.at[n_now, k - 1], device_id=(px, py, pc), device_id_type=MESH)

        small_out[me] = small_in[...]
        small_sends = [small_copy(k) for k in range(1, N_DEV)]
        for cp in small_sends:
            cp.start()

        mine, first, passed = [], [], []
        for a in range(n):
            own = pltpu.make_async_copy(stage[a], blocks[a](outs[a], me), local.at[a])
            own.start()
            mine.append(own)
            if a >= n_now:
                continue
            sends = [copy(a, 0, me, sibling, src=stage[a])]
            sends += [copy(a, k, me, (*_chip(x, y, k), c), src=stage[a]) for k in (1, 2)]
            for cp in sends:
                cp.start()
            first += sends

        here = (x, y, c)
        across = [(*_chip(x, y, k), c) for k in (1, 2)]
        near = [[_block_of(*_chip(x, y, k), cc) for k in (1, 2)] for cc in (c, 1 - c)]
        far = [_block_of(*_chip(x, y, 3), cc) for cc in (c, 1 - c)]

        def launch(cp):
            cp.start()
            passed.append(cp)

        project(stage[0], me)
        copy(0, 0, _block_of(x, y, 1 - c), here).wait_recv()
        project_landed(_block_of(x, y, 1 - c))
        for a in range(n_now):
            for i in (0, 1):
                copy(a, 1 + i, near[0][i], here).wait_recv()
                launch(copy(a, 3 + i, near[0][i], across[1 - i], half=i))
                launch(copy(a, 5 + i, near[0][i], sibling))
            if a == 0:
                project_landed(near[0][0])
                project_landed(near[0][1])
        for i in (0, 1):
            copy(0, 5 + i, near[1][i], here).wait_recv()
            project_landed(near[1][i])
        for a in range(n_now):
            for i in (0, 1):
                copy(a, 3 + i, far[0], here, half=i).wait_recv()
                launch(copy(a, 7 + i, far[0], sibling, half=i))
            if a == 0:
                project_landed(far[0])
        for a in range(n_now):
            if a > 0:
                copy(a, 0, _block_of(x, y, 1 - c), here).wait_recv()
                for i in (0, 1):
                    copy(a, 5 + i, near[1][i], here).wait_recv()
            for i in (0, 1):
                copy(a, 7 + i, far[1], here, half=i).wait_recv()
            if a == 0:
                project_landed(far[1])
        for k in range(1, N_DEV):
            small_arrival(k).wait_recv()
        for cp in first + passed + small_sends:
            cp.wait_send()
        for done in mine + stores[-2:]:
            done.wait()

    out_shape = [jax.ShapeDtypeStruct(s, BF16) for s in full_shapes]
    out_shape += [jax.ShapeDtypeStruct((N_DEV, small_rows, LANES), F32), jax.ShapeDtypeStruct((t, full_shapes[0][1]), BF16),
                  jax.ShapeDtypeStruct(tokens.shape, BF16)]
    return pl.pallas_call(
        body, name="gather_weights", out_shape=out_shape,
        in_specs=[VMEM_SPEC] * (n + 3), out_specs=[HBM_SPEC] * n + [VMEM_SPEC, HBM_SPEC, VMEM_SPEC],
        scratch_shapes=[pltpu.VMEM(s.shape, BF16) for s in shards]
        + [pltpu.VMEM(shards[0].shape, BF16), pltpu.VMEM((2, t, shards[0].shape[1]), BF16),
           pltpu.SemaphoreType.DMA((n_now + 1, 9)), pltpu.SemaphoreType.DMA((n_now + 1, 9)),
           pltpu.SemaphoreType.DMA((n,)), pltpu.SemaphoreType.DMA(()), pltpu.SemaphoreType.DMA((2,))],
        compiler_params=_params(),
    )(*shards, small, tokens, gain)


def _gather_first(full, blocks, send, recv):
    x, y, c = _position()
    me = _block_of(x, y, c)
    peers = [(x, y, 1 - c)] + [(*_chip(x, y, k), c) for k in (1, 2, 3)]

    def copy(a, k, block):
        at = blocks[a](full[a], block)
        return pltpu.make_async_remote_copy(src_ref=at, dst_ref=at, send_sem=send[4 * a + k], recv_sem=recv[4 * a + k],
                                            device_id=peers[k], device_id_type=MESH)

    sends = [copy(a, k, me) for a in range(len(full)) for k in range(4)]
    arrivals = [copy(a, k, _block_of(*peers[k])) for a in range(len(full)) for k in range(4)]
    return sends, arrivals


def _gather_second(full, blocks, send, recv):
    x, y, c = _position()

    def copy(a, k, cc):
        at = blocks[a](full[a], _block_of(*_chip(x, y, k), cc))
        return pltpu.make_async_remote_copy(src_ref=at, dst_ref=at, send_sem=send[3 * a + k - 1],
                                            recv_sem=recv[3 * a + k - 1], device_id=(x, y, 1 - c), device_id_type=MESH)

    sends = [copy(a, k, c) for a in range(len(full)) for k in (1, 2, 3)]
    arrivals = [copy(a, k, 1 - c) for a in range(len(full)) for k in (1, 2, 3)]
    return sends, arrivals


def _split_call(body, name, arrays, sems_in, n_sems_out, after=None, token=False):
    n, m = len(arrays), len(sems_in)

    def kernel_body(*refs):
        outs = refs[n + m + (after is not None):]
        body(refs[:n], refs[n:n + m], outs[:n_sems_out])
        if token:
            outs[-1][...] = jnp.zeros_like(outs[-1])

    extra_in = [] if after is None else [after]
    outs = pl.pallas_call(
        kernel_body, name=name,
        out_shape=(*[pltpu.SemaphoreType.DMA(())] * n_sems_out, *[pltpu.HBM(a.shape, a.dtype) for a in arrays],
                   *([jax.ShapeDtypeStruct((SUBLANES, LANES), F32)] if token else [])),
        in_specs=[HBM_SPEC] * n + [SEM_SPEC] * m + [pl.BlockSpec(memory_space=pl.ANY)] * len(extra_in),
        out_specs=(*[SEM_SPEC] * n_sems_out, *[HBM_SPEC] * n, *([VMEM_SPEC] if token else [])),
        input_output_aliases={i: n_sems_out + i for i in range(n)},
        compiler_params=pltpu.CompilerParams(has_side_effects=DATAFLOW_EFFECT),
    )(*[pltpu.with_memory_space_constraint(a, pltpu.HBM) for a in arrays], *sems_in, *extra_in)
    sems, rest = list(outs[:n_sems_out]), list(outs[n_sems_out:])
    return (sems, rest[:n], rest[n]) if token else (sems, rest[:n])


def _gather_start(full, blocks, name):
    n = len(full)

    def body(arrays, _, sems):
        for cp in _gather_first(arrays, blocks, sems[:4 * n], sems[4 * n:])[0]:
            cp.start()

    sems, arrays, token = _split_call(body, name, full, [], 8 * n, token=True)
    return sems[:4 * n], sems[4 * n:], arrays, token


def _gather_forward(full, blocks, send_first, recv_first, after, name):
    n = len(full)

    def body(arrays, sems_in, sems):
        sends, arrivals = _gather_first(arrays, blocks, sems_in[:4 * n], sems_in[4 * n:])
        for cp in arrivals:
            cp.wait_recv()
        for cp in _gather_second(arrays, blocks, sems[:3 * n], sems[3 * n:])[0]:
            cp.start()
        for cp in sends:
            cp.wait_send()

    sems, arrays = _split_call(body, name, full, [*send_first, *recv_first], 6 * n, after=after)
    return sems[:3 * n], sems[3 * n:], arrays


def _gather_finish(full, blocks, send_second, recv_second, after, name):
    n = len(full)

    def body(arrays, sems_in, _):
        sends, arrivals = _gather_second(arrays, blocks, sems_in[:3 * n], sems_in[3 * n:])
        for cp in sends:
            cp.wait_send()
        for cp in arrivals:
            cp.wait_recv()

    return _split_call(body, name, full, [*send_second, *recv_second], 0, after=after)[1]


def _reduce_pair(grads, blocks, shard_shapes, name):
    n = len(grads)

    def body(*refs):
        ins, outs = refs[:n], refs[n:2 * n]
        got, own = refs[2 * n:3 * n], refs[3 * n:4 * n]
        send, recv, local = refs[4 * n:]
        x, y, c = _position()
        copies, loads = [], []
        for a in range(n):
            for k in range(4):
                chip = _chip(x, y, k)
                cp = pltpu.make_async_remote_copy(
                    src_ref=blocks[a](ins[a], _block_of(*chip, 1 - c)), dst_ref=got[a].at[k],
                    send_sem=send.at[a, k], recv_sem=recv.at[a, k], device_id=(x, y, 1 - c), device_id_type=MESH)
                cp.start()
                copies.append(cp)
                ld = pltpu.make_async_copy(blocks[a](ins[a], _block_of(*chip, c)), own[a].at[k], local.at[a, k])
                ld.start()
                loads.append(ld)
        for a in range(n):
            for k in range(4):
                loads[4 * a + k].wait()
                copies[4 * a + k].wait_recv()
                outs[a][k] = (own[a][k].astype(F32) + got[a][k].astype(F32)).astype(BF16)
        for cp in copies:
            cp.wait_send()

    slots = [(4,) + tuple(s) for s in shard_shapes]
    return pl.pallas_call(
        body, name=name, out_shape=[jax.ShapeDtypeStruct(s, BF16) for s in slots],
        in_specs=[HBM_SPEC] * n, out_specs=[VMEM_SPEC] * n,
        scratch_shapes=[pltpu.VMEM(s, BF16) for s in slots] * 2
        + [pltpu.SemaphoreType.DMA((n, 4)), pltpu.SemaphoreType.DMA((n, 4)), pltpu.SemaphoreType.DMA((n, 4))],
        compiler_params=_params(),
    )(*grads)


def _chip_copies(sums, lands, send, recv):
    x, y, c = _position()
    return [pltpu.make_async_remote_copy(
        src_ref=sums[a].at[k], dst_ref=lands[a].at[k - 1], send_sem=send[3 * a + k - 1], recv_sem=recv[3 * a + k - 1],
        device_id=(*_chip(x, y, k), c), device_id_type=MESH) for a in range(len(sums)) for k in (1, 2, 3)]


def _exchange_chips_start(pair_sums, name):
    n = len(pair_sums)
    lands = [pltpu.with_memory_space_constraint(lax.empty((3,) + tuple(p.shape[1:]), BF16), pltpu.HBM) for p in pair_sums]

    def body(*refs):
        sums, zones = refs[:n], refs[n:2 * n]
        send, recv = refs[2 * n:5 * n], refs[5 * n:8 * n]
        token = refs[-1]
        for cp in _chip_copies(sums, zones, send, recv):
            cp.start()
        token[...] = jnp.zeros_like(token)

    outs = pl.pallas_call(
        body, name=name,
        out_shape=(*[pltpu.SemaphoreType.DMA(())] * (6 * n),
                   *[pltpu.HBM(p.shape, BF16) for p in pair_sums], *[pltpu.HBM(z.shape, BF16) for z in lands],
                   jax.ShapeDtypeStruct((SUBLANES, LANES), F32)),
        in_specs=[HBM_SPEC] * (2 * n), out_specs=(*[SEM_SPEC] * (6 * n), *[HBM_SPEC] * (2 * n), VMEM_SPEC),
        input_output_aliases={i: 6 * n + i for i in range(2 * n)},
        compiler_params=pltpu.CompilerParams(has_side_effects=DATAFLOW_EFFECT),
    )(*[pltpu.with_memory_space_constraint(p, pltpu.HBM) for p in pair_sums], *lands)
    return outs[:3 * n], outs[3 * n:6 * n], outs[6 * n:7 * n], outs[7 * n:8 * n], outs[-1]


def _exchange_chips_wait(send, recv, sums, lands, after, name):
    n = len(sums)

    def body(*refs):
        sums_in, zones = refs[:n], refs[n:2 * n]
        send_in, recv_in = refs[2 * n:5 * n], refs[5 * n:8 * n]
        for cp in _chip_copies(sums_in, zones, send_in, recv_in):
            cp.wait_send()
            cp.wait_recv()

    outs = pl.pallas_call(
        body, name=name,
        out_shape=(*[pltpu.HBM(p.shape, BF16) for p in sums], *[pltpu.HBM(z.shape, BF16) for z in lands]),
        in_specs=[HBM_SPEC] * (2 * n) + [SEM_SPEC] * (6 * n) + [pl.BlockSpec(memory_space=pl.ANY)],
        out_specs=[HBM_SPEC] * (2 * n), input_output_aliases={i: i for i in range(2 * n)},
        compiler_params=pltpu.CompilerParams(has_side_effects=DATAFLOW_EFFECT),
    )(*sums, *lands, *send, *recv, after)
    return outs[:n], outs[n:]


def _small_copies(mine, land, send, recv):
    x, y, c = _position()
    me = _block_of(x, y, c)

    def peer(k):
        return (x + (k & 1)) % 2, (y + ((k >> 1) & 1)) % 2, (c + (k >> 2)) % 2

    def copy(k, slot):
        return pltpu.make_async_remote_copy(src_ref=mine, dst_ref=land.at[slot], send_sem=send[k - 1], recv_sem=recv[k - 1],
                                            device_id=peer(k), device_id_type=MESH)

    return [copy(k, me) for k in range(1, N_DEV)], [copy(k, _block_of(*peer(k))) for k in range(1, N_DEV)]


def _small_start(part, name):
    land = jnp.zeros((N_DEV,) + part.shape, F32)

    def body(arrays, _, sems):
        for cp in _small_copies(arrays[0], arrays[1], sems[:7], sems[7:])[0]:
            cp.start()

    sems, arrays, token = _split_call(body, name, [part, land], [], 14, token=True)
    return sems[:7], sems[7:], arrays[0], arrays[1], token


def _small_wait(send, recv, part, land, after, name):
    def body(arrays, sems_in, _):
        sends, arrivals = _small_copies(arrays[0], arrays[1], sems_in[:7], sems_in[7:])
        for cp in sends:
            cp.wait_send()
        for cp in arrivals:
            cp.wait_recv()

    return _split_call(body, name, [part, land], [*send, *recv], 0, after=after)[1]


def _small_sum(pairs, me):
    n = len(pairs)

    def body(me_ref, *refs):
        for i in range(n):
            mine, land, out = refs[2 * i], refs[2 * i + 1], refs[2 * n + i]
            total = jnp.zeros(mine.shape, F32)
            for d in range(N_DEV):
                total = total + land[d] + jnp.where(me_ref[0] == d, mine[...], 0.0)
            out[...] = total

    flat = [a for pair in pairs for a in pair]
    return pl.pallas_call(
        body, name="small_sum", out_shape=[jax.ShapeDtypeStruct(mine.shape, F32) for mine, _ in pairs],
        in_specs=[pl.BlockSpec(memory_space=pltpu.SMEM)] + [VMEM_SPEC] * (2 * n), out_specs=[VMEM_SPEC] * n,
        compiler_params=_params(),
    )(me.reshape(1).astype(jnp.int32), *flat)


def _section(s, t):
    return pl.BlockSpec((t, CB), lambda h, s=s: (0, s * (D_MODEL // CB) + h))


def _conv_mixer_fwd(proj, w_short):
    t = proj.shape[0]
    rc = _row_chunk(t)

    def body(b_ref, c_ref, x_ref, w_ref, y_ref, pad):
        pad[pl.ds(0, PAD), :] = jnp.zeros((PAD, CB), F32)
        for r0 in range(0, t, rc):
            rows = pl.ds(r0, rc)
            pad[pl.ds(PAD + r0, rc), :] = c_ref[rows, :].astype(F32) * x_ref[rows, :].astype(F32)
        w = w_ref[...]
        for r0 in range(0, t, rc):
            rows = pl.ds(r0, rc)
            y_ref[rows, :] = (b_ref[rows, :].astype(F32) * _conv_causal(pad, w, r0, rc, 3)).astype(BF16)

    return pl.pallas_call(
        body, name="conv_mixer_fwd", grid=(D_MODEL // CB,),
        out_shape=jax.ShapeDtypeStruct((t, D_MODEL), BF16),
        in_specs=[_section(0, t), _section(1, t), _section(2, t), pl.BlockSpec((3, CB), lambda h: (0, h))],
        out_specs=pl.BlockSpec((t, CB), lambda h: (0, h)),
        scratch_shapes=[pltpu.VMEM((t + PAD, CB), F32)],
        compiler_params=_params("parallel"),
    )(proj, proj, proj, w_short)


def _lru_gates(xl, wa, ba, wx, bx, ls, first_row):
    xb = xl.astype(BF16)
    ra = jax.nn.sigmoid(_dot(xb, wa) + ba)
    ia = jax.nn.sigmoid(_dot(xb, wx) + bx)
    la = LRU_C * ra * ls
    a = jnp.exp(la)
    one_minus = -_expm1_neg(2.0 * la)
    mult = jnp.where(first_row, 1.0, jnp.sqrt(one_minus))
    return xb, ra, ia, a, one_minus, mult


def _head_specs():
    vec = pl.BlockSpec((1, CB), lambda h: (0, h))
    mat = pl.BlockSpec((N_DEV, None, HEAD_DIM // N_DEV, HEAD_DIM), lambda h: (0, h, 0, 0))
    return vec, mat


def _lru_fwd(proj, w_conv, b_conv, wa, ba, wx, bx, lam):
    t = proj.shape[0]
    rc = _row_chunk(t)
    vec, mat = _head_specs()

    def body(lx_ref, ly_ref, wc_ref, bc_ref, wa_ref, ba_ref, wx_ref, bx_ref, lam_ref, yb_ref, hl_ref, a_ref, kept_ref,
             pad, u_s):
        pad[pl.ds(0, PAD), :] = jnp.zeros((PAD, CB), F32)
        for r0 in range(0, t, rc):
            pad[pl.ds(PAD + r0, rc), :] = lx_ref[pl.ds(r0, rc), :].astype(F32)
        wc, bc = wc_ref[...], bc_ref[...]
        wa_m, wx_m = wa_ref[...].reshape(HEAD_DIM, HEAD_DIM), wx_ref[...].reshape(HEAD_DIM, HEAD_DIM)
        ls = _log_sigmoid(lam_ref[...])
        for r0 in range(0, t, rc):
            rows = pl.ds(r0, rc)
            xl = _conv_causal(pad, wc, r0, rc, 4) + bc
            first = (lax.broadcasted_iota(jnp.int32, (rc, CB), 0) + r0) == 0
            xb, ra, ia, a, _, mult = _lru_gates(xl, wa_m, ba_ref[...], wx_m, bx_ref[...], ls, first)
            a_ref[rows, :] = a
            u_s[rows, :] = mult * (ia * xl)
            kept_ref[0, rows, :] = xb
            kept_ref[1, rows, :] = ra.astype(BF16)
            kept_ref[2, rows, :] = ia.astype(BF16)

        row = lax.broadcasted_iota(jnp.int32, (SUBLANES, CB), 0)

        def group(g, carry):
            r = pl.multiple_of(g * SUBLANES, SUBLANES)
            a_g, b_g = a_ref[pl.ds(r, SUBLANES), :], u_s[pl.ds(r, SUBLANES), :]
            for s in (1, 2, 4):
                keep = row >= s
                b_g = jnp.where(keep, a_g * pltpu.roll(b_g, s, 0) + b_g, b_g)
                a_g = jnp.where(keep, a_g * pltpu.roll(a_g, s, 0), a_g)
            h_g = b_g + a_g * carry
            hl_ref[pl.ds(r, SUBLANES), :] = h_g
            return jnp.broadcast_to(h_g[SUBLANES - 1:SUBLANES, :], (SUBLANES, CB))

        lax.fori_loop(0, t // SUBLANES, group, jnp.zeros((SUBLANES, CB), F32))
        for r0 in range(0, t, rc):
            rows = pl.ds(r0, rc)
            yb_ref[rows, :] = (hl_ref[rows, :] * _gelu(ly_ref[rows, :].astype(F32))).astype(BF16)

    blk = pl.BlockSpec((t, CB), lambda h: (0, h))
    res = jax.ShapeDtypeStruct((t, D_MODEL), F32)
    return pl.pallas_call(
        body, name="lru_fwd", grid=(N_HEADS,),
        out_shape=[jax.ShapeDtypeStruct((t, D_MODEL), BF16), res, res, jax.ShapeDtypeStruct((3, t, D_MODEL), BF16)],
        in_specs=[_section(3, t), _section(4, t), pl.BlockSpec((4, CB), lambda h: (0, h)), vec, mat, vec, mat, vec, vec],
        out_specs=[blk, blk, blk, pl.BlockSpec((3, t, CB), lambda h: (0, 0, h))],
        scratch_shapes=[pltpu.VMEM((t + PAD, CB), F32), pltpu.VMEM((t, CB), F32)],
        compiler_params=_params("parallel"),
    )(proj, proj, w_conv, b_conv, wa, ba, wx, bx, lam)


def _merge(y_a, y_b, proj, x, w_cb, w_lb, w_out, g2, g3):
    t = x.shape[0]
    tm = min(512, t)

    def body(ya_ref, yb_ref, gc_ref, gl_ref, x_ref, wcb_ref, wlb_ref, wo_ref, g2_ref, g3_ref,
             pa_ref, pb_ref, mg_ref, mix_ref, x1_ref, h2_ref):
        pa = _dot(ya_ref[...], wcb_ref[...]).astype(BF16)
        pb = _dot(yb_ref[...], wlb_ref[...]).astype(BF16)
        pa_ref[...] = pa
        pb_ref[...] = pb
        merged = (jax.nn.sigmoid(gc_ref[...].astype(F32)) * pa.astype(F32)
                  + jax.nn.sigmoid(gl_ref[...].astype(F32)) * pb.astype(F32)).astype(BF16)
        mg_ref[...] = merged
        mix = _dot(merged, wo_ref[...])
        mix_ref[...] = mix
        n2, _ = _rms_fwd(mix)
        x1 = x_ref[...] + n2 * g2_ref[...]
        x1_ref[...] = x1
        n3, _ = _rms_fwd(x1)
        h2_ref[...] = (n3 * g3_ref[...]).astype(BF16)

    row = pl.BlockSpec((tm, D_MODEL), lambda i: (i, 0))
    full = pl.BlockSpec((D_MODEL, D_MODEL), lambda i: (0, 0))
    vec = pl.BlockSpec((1, D_MODEL), lambda i: (0, 0))
    act = jax.ShapeDtypeStruct((t, D_MODEL), BF16)
    res = jax.ShapeDtypeStruct((t, D_MODEL), F32)
    return pl.pallas_call(
        body, name="merge_fwd", grid=(t // tm,), out_shape=[act, act, act, res, res, act],
        in_specs=[row, row, pl.BlockSpec((tm, D_MODEL), lambda i: (i, 5)), pl.BlockSpec((tm, D_MODEL), lambda i: (i, 6)),
                  row, full, full, full, vec, vec],
        out_specs=[row] * 6,
        compiler_params=_params("parallel"),
    )(y_a, y_b, proj, proj, x, w_cb, w_lb, w_out, g2, g3)


N_FF_BLOCKS = D_FF // CB


def _ffn_up(h2, w_up, w_conv, b_conv):
    t = h2.shape[0]
    rc = _row_chunk(t)
    nb = N_FF_BLOCKS

    def body(h_ref, w_ref, c_ref, b_ref, up_ref, act_ref, f_ref, pad, gate):
        k = pl.program_id(1)
        pad[pl.ds(0, PAD), :] = jnp.zeros((PAD, CB), F32)
        cw = c_ref[...]
        for r0 in range(0, t, rc):
            rows = pl.ds(r0, rc)
            up = _dot(h_ref[rows, :], w_ref[...]).astype(BF16)
            up_ref[rows, :] = up
            pad[pl.ds(PAD + r0, rc), :] = up.astype(F32)
            act = _conv_causal(pad, cw, r0, rc, 3) + b_ref[...]
            act_ref[rows, :] = act.astype(BF16)

            @pl.when(k == 0)
            def _():
                gate[rows, :] = act

            @pl.when(k == 1)
            def _():
                f_ref[rows, :] = (_gelu(gate[rows, :]) * act).astype(BF16)

    half = lambda rows: pl.BlockSpec((rows, CB), lambda j, k: (0, nb * k + j))
    wide = jax.ShapeDtypeStruct((t, 2 * D_FF), BF16)
    return pl.pallas_call(
        body, name="ffn_up_fwd", grid=(nb, 2), out_shape=[wide, wide, jax.ShapeDtypeStruct((t, D_FF), BF16)],
        in_specs=[pl.BlockSpec((t, D_MODEL), lambda j, k: (0, 0)), half(D_MODEL), half(3), half(1)],
        out_specs=[half(t), half(t), pl.BlockSpec((t, CB), lambda j, k: (0, j))],
        scratch_shapes=[pltpu.VMEM((t + PAD, CB), F32), pltpu.VMEM((t, CB), F32)],
        compiler_params=_params("parallel", "arbitrary"),
    )(h2, w_up, w_conv, b_conv)


def _ffn_down(f, act, w_down, x1, target, g4):
    t = f.shape[0]
    tm = min(256, t)
    cc = 512

    def body(f_ref, act_ref, w_ref, x1_ref, tg_ref, g_ref, dy_ref, dout_ref, back_ref, dg_ref, loss_ref):
        @pl.when(pl.program_id(0) == 0)
        def _():
            dg_ref[...] = jnp.zeros_like(dg_ref)
            loss_ref[...] = jnp.zeros_like(loss_ref)
        out = _dot(f_ref[...], w_ref[...])
        n4, r4 = _rms_fwd(out)
        err = x1_ref[...] + n4 * g_ref[...] - tg_ref[...]
        loss_ref[...] += jnp.full(loss_ref.shape, 0.5 / D_MODEL, F32) * jnp.sum(err * err)
        dy = err * (1.0 / D_MODEL)
        dy_ref[...] = dy
        dg_ref[...] += jnp.sum(dy * n4, axis=0, keepdims=True)
        d_out = _rms_bwd(n4, r4, dy * g_ref[...]).astype(BF16)
        dout_ref[...] = d_out
        for c0 in range(0, D_FF, cc):
            d_f = _dot_nt(d_out, w_ref[pl.ds(c0, cc), :])
            gelu, d_gelu = _gelu_and_grad(act_ref[:, pl.ds(c0, cc)].astype(F32))
            val = act_ref[:, pl.ds(D_FF + c0, cc)].astype(F32)
            back_ref[:, pl.ds(c0, cc)] = (d_f * val * d_gelu).astype(BF16)
            back_ref[:, pl.ds(D_FF + c0, cc)] = (d_f * gelu).astype(BF16)

    row = pl.BlockSpec((tm, D_MODEL), lambda i: (i, 0))
    wide = pl.BlockSpec((tm, 2 * D_FF), lambda i: (i, 0))
    vec = pl.BlockSpec((1, D_MODEL), lambda i: (0, 0))
    return pl.pallas_call(
        body, name="ffn_down_fwd_bwd", grid=(t // tm,),
        out_shape=[jax.ShapeDtypeStruct((t, D_MODEL), F32), jax.ShapeDtypeStruct((t, D_MODEL), BF16),
                   jax.ShapeDtypeStruct((t, 2 * D_FF), BF16), jax.ShapeDtypeStruct((1, D_MODEL), F32),
                   jax.ShapeDtypeStruct((SUBLANES, LANES), F32)],
        in_specs=[pl.BlockSpec((tm, D_FF), lambda i: (i, 0)), wide, pl.BlockSpec((D_FF, D_MODEL), lambda i: (0, 0)),
                  row, row, vec],
        out_specs=[row, row, wide, vec, pl.BlockSpec((SUBLANES, LANES), lambda i: (0, 0))],
        compiler_params=_params("arbitrary"),
    )(f, act, w_down, x1, target, g4)


def _grad_tn(a, b, bm, name):
    t, m = a.shape
    n = b.shape[1]

    def body(a_ref, b_ref, o_ref):
        o_ref[...] = _dot_tn(a_ref[...], b_ref[...]).astype(BF16)

    return pl.pallas_call(
        body, name=name, grid=(m // bm,), out_shape=jax.ShapeDtypeStruct((m, n), BF16),
        in_specs=[pl.BlockSpec((t, bm), lambda i: (0, i)), pl.BlockSpec((t, n), lambda i: (0, 0))],
        out_specs=pl.BlockSpec((bm, n), lambda i: (i, 0)),
        compiler_params=_params("parallel"),
    )(a, b)


def _ffn_up_bwd(up, back, w_conv, h2, w_up):
    t = h2.shape[0]
    rc = _row_chunk(t)
    nb = N_FF_BLOCKS

    def body(up_ref, back_ref, c_ref, h_ref, w_ref, dw_ref, dcw_ref, dcb_ref, dh_ref, pad, after, d_up):
        @pl.when((pl.program_id(0) == 0) & (pl.program_id(1) == 0))
        def _():
            dh_ref[...] = jnp.zeros_like(dh_ref)
        pad[pl.ds(0, PAD), :] = jnp.zeros((PAD, CB), F32)
        after[pl.ds(t, PAD), :] = jnp.zeros((PAD, CB), F32)
        for r0 in range(0, t, rc):
            pad[pl.ds(PAD + r0, rc), :] = up_ref[pl.ds(r0, rc), :].astype(F32)
            after[pl.ds(r0, rc), :] = back_ref[pl.ds(r0, rc), :].astype(F32)
        cw = c_ref[...]
        taps = [jnp.zeros((1, CB), F32)] * 3
        bias = jnp.zeros((1, CB), F32)
        for r0 in range(0, t, rc):
            rows = pl.ds(r0, rc)
            d = _conv_anticausal(after, cw, r0, rc, 3).astype(BF16)
            d_up[rows, :] = d
            dh_ref[rows, :] += _dot_nt(d, w_ref[...])
            g = after[rows, :]
            taps = [acc + new for acc, new in zip(taps, _conv_wgrad(g, pad, r0, rc, 3))]
            bias = bias + jnp.sum(g, axis=0, keepdims=True)
        dw_ref[...] = _dot_tn(h_ref[...], d_up[...]).astype(BF16)
        dcw_ref[...] = jnp.concatenate(taps, axis=0)
        dcb_ref[...] = bias

    half = lambda rows: pl.BlockSpec((rows, CB), lambda j, k: (0, nb * k + j))
    whole = pl.BlockSpec((t, D_MODEL), lambda j, k: (0, 0))
    return pl.pallas_call(
        body, name="ffn_up_bwd", grid=(nb, 2),
        out_shape=[jax.ShapeDtypeStruct((D_MODEL, 2 * D_FF), BF16), jax.ShapeDtypeStruct((3, 2 * D_FF), F32),
                   jax.ShapeDtypeStruct((1, 2 * D_FF), F32), jax.ShapeDtypeStruct((t, D_MODEL), F32)],
        in_specs=[half(t), half(t), half(3), whole, half(D_MODEL)],
        out_specs=[half(D_MODEL), half(3), half(1), whole],
        scratch_shapes=[pltpu.VMEM((t + PAD, CB), F32), pltpu.VMEM((t + PAD, CB), F32), pltpu.VMEM((t, CB), BF16)],
        compiler_params=_params("arbitrary", "arbitrary"),
    )(up, back, w_conv, h2, w_up)


def _merge_bwd(dy, d_h2, x1, mix, g3, g2, w_out, w_cb, w_lb, pa, pb, proj):
    t = dy.shape[0]
    tm = min(256, t)

    def body(dy_ref, dh2_ref, x1_ref, mix_ref, g3_ref, g2_ref, wo_ref, wcb_ref, wlb_ref, pa_ref, pb_ref, gc_ref, gl_ref,
             dx1_ref, dmix_ref, dpa_ref, dpb_ref, dya_ref, dyb_ref, dgate_ref, dg3_ref, dg2_ref):
        @pl.when(pl.program_id(0) == 0)
        def _():
            dg3_ref[...] = jnp.zeros_like(dg3_ref)
            dg2_ref[...] = jnp.zeros_like(dg2_ref)
        n3, r3 = _rms_fwd(x1_ref[...])
        d_h2 = dh2_ref[...]
        dg3_ref[...] += jnp.sum(d_h2 * n3, axis=0, keepdims=True)
        dx1 = dy_ref[...] + _rms_bwd(n3, r3, d_h2 * g3_ref[...])
        dx1_ref[...] = dx1
        n2, r2 = _rms_fwd(mix_ref[...])
        dg2_ref[...] += jnp.sum(dx1 * n2, axis=0, keepdims=True)
        d_mix = _rms_bwd(n2, r2, dx1 * g2_ref[...]).astype(BF16)
        dmix_ref[...] = d_mix
        d_merged = _dot_nt(d_mix, wo_ref[...])
        sc = jax.nn.sigmoid(gc_ref[...].astype(F32))
        sl = jax.nn.sigmoid(gl_ref[...].astype(F32))
        d_pa = (d_merged * sc).astype(BF16)
        d_pb = (d_merged * sl).astype(BF16)
        dpa_ref[...] = d_pa
        dpb_ref[...] = d_pb
        dgate_ref[0] = (d_merged * pa_ref[...].astype(F32) * sc * (1.0 - sc)).astype(BF16)
        dgate_ref[1] = (d_merged * pb_ref[...].astype(F32) * sl * (1.0 - sl)).astype(BF16)
        dya_ref[...] = _dot_nt(d_pa, wcb_ref[...]).astype(BF16)
        dyb_ref[...] = _dot_nt(d_pb, wlb_ref[...]).astype(BF16)

    row = pl.BlockSpec((tm, D_MODEL), lambda i: (i, 0))
    full = pl.BlockSpec((D_MODEL, D_MODEL), lambda i: (0, 0))
    vec = pl.BlockSpec((1, D_MODEL), lambda i: (0, 0))
    act = jax.ShapeDtypeStruct((t, D_MODEL), BF16)
    small = jax.ShapeDtypeStruct((1, D_MODEL), F32)
    return pl.pallas_call(
        body, name="merge_bwd", grid=(t // tm,),
        out_shape=[jax.ShapeDtypeStruct((t, D_MODEL), F32), act, act, act, act, act,
                   jax.ShapeDtypeStruct((2, t, D_MODEL), BF16), small, small],
        in_specs=[row, row, row, row, vec, vec, full, full, full, row, row,
                  pl.BlockSpec((tm, D_MODEL), lambda i: (i, 5)), pl.BlockSpec((tm, D_MODEL), lambda i: (i, 6))],
        out_specs=[row] * 6 + [pl.BlockSpec((2, tm, D_MODEL), lambda i: (0, i, 0)), vec, vec],
        compiler_params=_params("arbitrary"),
    )(dy, d_h2, x1, mix, g3, g2, w_out, w_cb, w_lb, pa, pb, proj, proj)


def _conv_mixer_bwd(proj, d_ya, w_short):
    t = proj.shape[0]
    rc = _row_chunk(t)

    def body(b_ref, c_ref, x_ref, dy_ref, w_ref, d_ref, dw_ref, pad, back):
        pad[pl.ds(0, PAD), :] = jnp.zeros((PAD, CB), F32)
        back[pl.ds(t, PAD), :] = jnp.zeros((PAD, CB), F32)
        for r0 in range(0, t, rc):
            rows = pl.ds(r0, rc)
            pad[pl.ds(PAD + r0, rc), :] = c_ref[rows, :].astype(F32) * x_ref[rows, :].astype(F32)
        w = w_ref[...]
        for r0 in range(0, t, rc):
            rows = pl.ds(r0, rc)
            d_y = dy_ref[rows, :].astype(F32)
            d_ref[0, rows, :] = (d_y * _conv_causal(pad, w, r0, rc, 3)).astype(BF16)
            back[rows, :] = d_y * b_ref[rows, :].astype(F32)
        taps = [jnp.zeros((1, CB), F32)] * 3
        for r0 in range(0, t, rc):
            rows = pl.ds(r0, rc)
            d_u = _conv_anticausal(back, w, r0, rc, 3)
            d_ref[1, rows, :] = (d_u * x_ref[rows, :].astype(F32)).astype(BF16)
            d_ref[2, rows, :] = (d_u * c_ref[rows, :].astype(F32)).astype(BF16)
            taps = [acc + new for acc, new in zip(taps, _conv_wgrad(back[rows, :], pad, r0, rc, 3))]
        dw_ref[...] = jnp.concatenate(taps, axis=0)

    blk = pl.BlockSpec((t, CB), lambda h: (0, h))
    return pl.pallas_call(
        body, name="conv_mixer_bwd", grid=(D_MODEL // CB,),
        out_shape=[jax.ShapeDtypeStruct((3, t, D_MODEL), BF16), jax.ShapeDtypeStruct((3, D_MODEL), F32)],
        in_specs=[_section(0, t), _section(1, t), _section(2, t), blk, pl.BlockSpec((3, CB), lambda h: (0, h))],
        out_specs=[pl.BlockSpec((3, t, CB), lambda h: (0, 0, h)), pl.BlockSpec((3, CB), lambda h: (0, h))],
        scratch_shapes=[pltpu.VMEM((t + PAD, CB), F32), pltpu.VMEM((t + PAD, CB), F32)],
        compiler_params=_params("parallel"),
    )(proj, proj, proj, d_ya, w_short)


LRU_SMALL_ROWS = 8


def _lru_bwd(proj, hl, a_all, kept, d_yb, w_conv, wa, wx, lam):
    t = proj.shape[0]
    rc = _row_chunk(t)
    vec, mat = _head_specs()

    def body(lx_ref, ly_ref, hl_ref, a_ref, kept_ref, dy_ref, wc_ref, wa_ref, wx_ref, lam_ref,
             d_ref, dwa_ref, dwx_ref, small_ref, pad, a_next, dh_s, h_prev, back, acc_a, acc_x):
        zeros = jnp.zeros((PAD, CB), F32)
        pad[pl.ds(0, PAD), :] = zeros
        h_prev[pl.ds(0, PAD), :] = zeros
        a_next[pl.ds(t, PAD), :] = zeros
        back[pl.ds(t, PAD), :] = zeros
        for r0 in range(0, t, rc):
            rows = pl.ds(r0, rc)
            pad[pl.ds(PAD + r0, rc), :] = lx_ref[rows, :].astype(F32)
            h_prev[pl.ds(PAD + r0, rc), :] = hl_ref[rows, :]
            a_next[pl.ds(PAD - 1 + r0, rc), :] = a_ref[rows, :]
            act, d_act = _gelu_and_grad(ly_ref[rows, :].astype(F32))
            d_y = dy_ref[rows, :].astype(F32)
            dh_s[rows, :] = d_y * act
            d_ref[1, rows, :] = (d_y * hl_ref[rows, :] * d_act).astype(BF16)
        wc = wc_ref[...]
        wa_m, wx_m = wa_ref[...].reshape(HEAD_DIM, HEAD_DIM), wx_ref[...].reshape(HEAD_DIM, HEAD_DIM)
        ls = _log_sigmoid(lam_ref[...])

        row = lax.broadcasted_iota(jnp.int32, (SUBLANES, CB), 0)
        groups = t // SUBLANES

        def group(i, carry):
            r = pl.multiple_of((groups - 1 - i) * SUBLANES, SUBLANES)
            a_g, b_g = a_next[pl.ds(PAD + r, SUBLANES), :], dh_s[pl.ds(r, SUBLANES), :]
            for s in (1, 2, 4):
                keep = row < SUBLANES - s
                b_g = jnp.where(keep, a_g * pltpu.roll(b_g, SUBLANES - s, 0) + b_g, b_g)
                a_g = jnp.where(keep, a_g * pltpu.roll(a_g, SUBLANES - s, 0), a_g)
            d_g = b_g + a_g * carry
            dh_s[pl.ds(r, SUBLANES), :] = d_g
            return jnp.broadcast_to(d_g[0:1, :], (SUBLANES, CB))

        lax.fori_loop(0, groups, group, jnp.zeros((SUBLANES, CB), F32))

        acc_a[...] = jnp.zeros_like(acc_a)
        acc_x[...] = jnp.zeros_like(acc_x)
        d_ba = d_bx = d_ls = jnp.zeros((1, CB), F32)
        for r0 in range(0, t, rc):
            rows = pl.ds(r0, rc)
            first = (lax.broadcasted_iota(jnp.int32, (rc, CB), 0) + r0) == 0
            xb, a = kept_ref[0, rows, :], a_ref[rows, :]
            xl, ra, ia = xb.astype(F32), kept_ref[1, rows, :].astype(F32), kept_ref[2, rows, :].astype(F32)
            a_sq = a * a
            mult = jnp.where(first, 1.0, jnp.sqrt(1.0 - a_sq))
            d_h = dh_s[rows, :]
            d_a = d_h * _rows_back(h_prev, r0, rc, 1)
            d_mult = d_h * ia * xl
            d_ia = d_h * mult * xl
            d_xl = d_h * mult * ia
            d_la = d_a * a + d_mult * jnp.where(first, 0.0, -a_sq / mult)
            d_ls = d_ls + jnp.sum(d_la * ra, axis=0, keepdims=True) * LRU_C
            d_za = d_la * (LRU_C * ls) * ra * (1.0 - ra)
            d_zx = d_ia * ia * (1.0 - ia)
            d_ba = d_ba + jnp.sum(d_za, axis=0, keepdims=True)
            d_bx = d_bx + jnp.sum(d_zx, axis=0, keepdims=True)
            d_za, d_zx = d_za.astype(BF16), d_zx.astype(BF16)
            acc_a[...] += _dot_tn(xb, d_za)
            acc_x[...] += _dot_tn(xb, d_zx)
            back[rows, :] = d_xl + _dot_nt(d_za, wa_m) + _dot_nt(d_zx, wx_m)
        taps = [jnp.zeros((1, CB), F32)] * 4
        d_bc = jnp.zeros((1, CB), F32)
        for r0 in range(0, t, rc):
            rows = pl.ds(r0, rc)
            d_ref[0, rows, :] = _conv_anticausal(back, wc, r0, rc, 4).astype(BF16)
            g = back[rows, :]
            taps = [acc + new for acc, new in zip(taps, _conv_wgrad(g, pad, r0, rc, 4))]
            d_bc = d_bc + jnp.sum(g, axis=0, keepdims=True)
        d_lam = d_ls * jax.nn.sigmoid(-lam_ref[...])
        small_ref[...] = jnp.concatenate(taps + [d_bc, d_ba, d_bx, d_lam], axis=0)
        dwa_ref[...] = acc_a[...].reshape(N_DEV, HEAD_DIM // N_DEV, HEAD_DIM).astype(BF16)
        dwx_ref[...] = acc_x[...].reshape(N_DEV, HEAD_DIM // N_DEV, HEAD_DIM).astype(BF16)

    blk = pl.BlockSpec((t, CB), lambda h: (0, h))
    gate_grad = jax.ShapeDtypeStruct((N_DEV, N_HEADS, HEAD_DIM // N_DEV, HEAD_DIM), BF16)
    return pl.pallas_call(
        body, name="lru_bwd", grid=(N_HEADS,),
        out_shape=[jax.ShapeDtypeStruct((2, t, D_MODEL), BF16), gate_grad, gate_grad,
                   jax.ShapeDtypeStruct((LRU_SMALL_ROWS, D_MODEL), F32)],
        in_specs=[_section(3, t), _section(4, t), blk, blk, pl.BlockSpec((3, t, CB), lambda h: (0, 0, h)), blk,
                  pl.BlockSpec((4, CB), lambda h: (0, h)), mat, mat, vec],
        out_specs=[pl.BlockSpec((2, t, CB), lambda h: (0, 0, h)), mat, mat,
                   pl.BlockSpec((LRU_SMALL_ROWS, CB), lambda h: (0, h))],
        scratch_shapes=[pltpu.VMEM((t + PAD, CB), F32), pltpu.VMEM((t + PAD, CB), F32), pltpu.VMEM((t, CB), F32),
                        pltpu.VMEM((t + PAD, CB), F32), pltpu.VMEM((t + PAD, CB), F32),
                        pltpu.VMEM((HEAD_DIM, HEAD_DIM), F32), pltpu.VMEM((HEAD_DIM, HEAD_DIM), F32)],
        compiler_params=_params("parallel"),
    )(proj, proj, hl, a_all, kept, d_yb, w_conv, wa, wx, lam)


def _stack_maps(halves):
    def conv(sec, part):
        return jnp.minimum(sec, 2), jnp.where(sec < 3, part, halves - 1)

    def lru(sec, part):
        return jnp.clip(sec - 3, 0, 1), jnp.where(sec < 3, 0, jnp.where(sec < 5, part, halves - 1))

    def gate(sec, part):
        return jnp.clip(sec - 5, 0, 1), jnp.where(sec < 5, 0, part)

    return conv, lru, gate


def _pick_stack(sec, refs, fn):
    @pl.when(sec < 3)
    def _():
        fn(refs[0])

    @pl.when((sec >= 3) & (sec < 5))
    def _():
        fn(refs[1])

    @pl.when(sec >= 5)
    def _():
        fn(refs[2])


def _in_proj_wgrad(h, d_conv, d_lru, d_gate):
    t = h.shape[0]
    halves, bn = 1, D_MODEL
    maps = _stack_maps(halves)

    def body(h_ref, dc_ref, dl_ref, dg_ref, o_ref):
        def emit(ref):
            o_ref[...] = _dot_tn(h_ref[...], ref[...]).astype(BF16)
        _pick_stack(pl.program_id(0) // halves, (dc_ref, dl_ref, dg_ref), emit)

    def spec(m):
        def index(s):
            stack, part = m(s // halves, s % halves)
            return stack, 0, part
        return pl.BlockSpec((None, t, bn), index)

    return pl.pallas_call(
        body, name="in_proj_wgrad", grid=(7 * halves,), out_shape=jax.ShapeDtypeStruct((D_MODEL, IN_COLS), BF16),
        in_specs=[pl.BlockSpec((t, D_MODEL), lambda s: (0, 0))] + [spec(m) for m in maps],
        out_specs=pl.BlockSpec((D_MODEL, bn), lambda s: (0, s)),
        compiler_params=_params("arbitrary"),
    )(h, d_conv, d_lru, d_gate)


def _in_proj_xgrad(d_conv, d_lru, d_gate, w_in, x, dx1, g1):
    t = x.shape[0]
    tm = min(1024, t)
    maps = _stack_maps(1)

    def body(dc_ref, dl_ref, dg_ref, w_ref, x_ref, dx1_ref, g_ref, dx_ref, dgain_ref, acc):
        i, s = pl.program_id(0), pl.program_id(1)

        @pl.when((i == 0) & (s == 0))
        def _():
            dgain_ref[...] = jnp.zeros_like(dgain_ref)

        @pl.when(s == 0)
        def _():
            acc[...] = jnp.zeros_like(acc)

        def add(ref):
            acc[...] += _dot_nt(ref[...], w_ref[...])
        _pick_stack(s, (dc_ref, dl_ref, dg_ref), add)

        @pl.when(s == 6)
        def _():
            n1, r1 = _rms_fwd(x_ref[...])
            d_h = acc[...]
            dgain_ref[...] += jnp.sum(d_h * n1, axis=0, keepdims=True)
            dx_ref[...] = dx1_ref[...] + _rms_bwd(n1, r1, d_h * g_ref[...])

    def spec(m):
        def index(i, s):
            return m(s, 0)[0], i, 0
        return pl.BlockSpec((None, tm, D_MODEL), index)

    row = pl.BlockSpec((tm, D_MODEL), lambda i, s: (i, 0))
    vec = pl.BlockSpec((1, D_MODEL), lambda i, s: (0, 0))
    return pl.pallas_call(
        body, name="in_proj_xgrad", grid=(t // tm, 7),
        out_shape=[jax.ShapeDtypeStruct((t, D_MODEL), F32), jax.ShapeDtypeStruct((1, D_MODEL), F32)],
        in_specs=[spec(m) for m in maps] + [pl.BlockSpec((D_MODEL, D_MODEL), lambda i, s: (0, s)), row, row, vec],
        out_specs=[row, vec],
        scratch_shapes=[pltpu.VMEM((tm, D_MODEL), F32)],
        compiler_params=_params("arbitrary", "arbitrary"),
    )(d_conv, d_lru, d_gate, w_in, x, dx1, g1)


def _adamw(w, g, m, v):
    m = ADAM_B1 * m + (1.0 - ADAM_B1) * g
    v = ADAM_B2 * v + (1.0 - ADAM_B2) * (g * g)
    m_hat = m / (1.0 - ADAM_B1 ** ADAM_STEP)
    v_hat = v / (1.0 - ADAM_B2 ** ADAM_STEP)
    return -ADAM_LR * (m_hat / (jnp.sqrt(v_hat) + ADAM_EPS) + ADAM_WD * w), m, v


def _adam_large(w, m, v, own, others, name):
    shape = w.shape
    cols = shape[-1]
    w2, m2, v2 = (a.reshape(-1, cols) for a in (w, m, v))
    rows = w2.shape[0]
    own, others = own.reshape(4, rows, cols), others.reshape(3, rows, cols)
    rb = _row_block(rows, 512)

    def body(w_ref, m_ref, v_ref, own_ref, oth_ref, g_ref, d_ref, nm_ref, nv_ref):
        g = own_ref[...].astype(F32)
        for k in range(3):
            g = g + oth_ref[k].astype(F32)
        g_ref[...] = g
        d_ref[...], nm_ref[...], nv_ref[...] = _adamw(w_ref[...], g, m_ref[...], v_ref[...])

    blk = pl.BlockSpec((rb, cols), lambda i: (i, 0))
    res = jax.ShapeDtypeStruct((rows, cols), F32)
    outs = pl.pallas_call(
        body, name=name, grid=(rows // rb,), out_shape=[res] * 4,
        in_specs=[blk, blk, blk, pl.BlockSpec((None, rb, cols), lambda i: (0, i, 0)),
                  pl.BlockSpec((3, rb, cols), lambda i: (0, i, 0))],
        out_specs=[blk] * 4, compiler_params=_params("parallel"),
    )(w2, m2, v2, own, others)
    return [o.reshape(shape) for o in outs]


def _adam_small(ws, gs, ms, vs):
    n = len(ws)

    def body(*refs):
        w_refs, g_refs, m_refs, v_refs = (refs[i * n:(i + 1) * n] for i in range(4))
        outs = refs[4 * n:]
        for i in range(n):
            d, m, v = _adamw(w_refs[i][...], g_refs[i][...], m_refs[i][...], v_refs[i][...])
            outs[i][...], outs[n + i][...], outs[2 * n + i][...] = d, m, v

    shapes = [jax.ShapeDtypeStruct(w.shape, F32) for w in ws]
    outs = pl.pallas_call(
        body, name="adam_small", out_shape=shapes * 3,
        in_specs=[VMEM_SPEC] * (4 * n), out_specs=[VMEM_SPEC] * (3 * n), compiler_params=_params(),
    )(*ws, *gs, *ms, *vs)
    return outs[:n], outs[n:2 * n], outs[2 * n:]


def _pack_rows(pieces):
    tile = SUBLANES * LANES
    return jnp.concatenate([jnp.pad(p.reshape(-1), (0, (-p.size) % tile)).reshape(-1, LANES) for p in pieces], axis=0)


def _packed_starts(sizes):
    tile = SUBLANES * LANES
    starts = [0]
    for s in sizes:
        starts.append(starts[-1] + (s + tile - 1) // tile * SUBLANES)
    return starts


def kernel(x, norm_mix_pre, norm_mix_post, norm_ffn_pre, norm_ffn_post, w_in, conv_short_w, w_conv_branch, lru_conv_w, lru_conv_b, lru_wa, lru_ba, lru_wx, lru_bx, lru_lambda, w_lru_branch, w_out, ffn_w_up, ffn_conv_w, ffn_conv_b, ffn_w_down, loss_target, m_norm_mix_pre, m_norm_mix_post, m_norm_ffn_pre, m_norm_ffn_post, m_w_in, m_conv_short_w, m_w_conv_branch, m_lru_conv_w, m_lru_conv_b, m_lru_wa, m_lru_ba, m_lru_wx, m_lru_bx, m_lru_lambda, m_w_lru_branch, m_w_out, m_ffn_w_up, m_ffn_conv_w, m_ffn_conv_b, m_ffn_w_down, v_norm_mix_pre, v_norm_mix_post, v_norm_ffn_pre, v_norm_ffn_post, v_w_in, v_conv_short_w, v_w_conv_branch, v_lru_conv_w, v_lru_conv_b, v_lru_wa, v_lru_ba, v_lru_wx, v_lru_bx, v_lru_lambda, v_w_lru_branch, v_w_out, v_ffn_w_up, v_ffn_conv_w, v_ffn_conv_b, v_ffn_w_down):
    t = x.shape[1]
    xi, yi, ci = _position()
    me = _block_of(xi, yi, ci)
    x2, target = x[0], loss_target[0]
    shard_in, shard_up = IN_COLS // N_DEV, 2 * D_FF // N_DEV
    shard_sq, shard_down, shard_head = D_MODEL // N_DEV, D_FF // N_DEV, HEAD_DIM // N_DEV

    names = ["w_in", "lru_wa", "lru_wx", "w_conv_branch", "w_lru_branch", "w_out", "ffn_w_up", "ffn_w_down"]
    large = [w_in[0], lru_wa[0], lru_wx[0], w_conv_branch[0], w_lru_branch[0], w_out[0], ffn_w_up[0], ffn_w_down[0]]
    blocks = [_cols(shard_in), _lead, _lead, _rows(shard_sq), _rows(shard_sq), _rows(shard_sq),
              _cols(shard_up), _rows(shard_down)]
    gate_full = (N_DEV, N_HEADS, shard_head, HEAD_DIM)
    full_shapes = [(D_MODEL, IN_COLS), gate_full, gate_full, (D_MODEL, D_MODEL), (D_MODEL, D_MODEL), (D_MODEL, D_MODEL),
                   (D_MODEL, 2 * D_FF), (D_FF, D_MODEL)]
    n_now = 3
    small_sharded = [conv_short_w, lru_conv_w, lru_ba, lru_bx, ffn_conv_w]
    small_mine = _pack_rows(small_sharded)
    small_at = _packed_starts([p.size for p in small_sharded])
    *gathered, small_all, proj, h = _gather_weights(large, blocks, full_shapes, small_mine, n_now, x2, norm_mix_pre)
    g_in, g_wa, g_wx = gathered[:n_now]
    later_blocks = blocks[n_now:]
    send1, recv1, later, gather_token = _gather_start(gathered[n_now:], later_blocks, "gather_start")

    def behind(token, operand):
        return operand + token[0:1, 0:1]

    def forward(lo, hi, after, tag):
        return _gather_forward(later[lo:hi], later_blocks[lo:hi], send1[4 * lo:4 * hi], recv1[4 * lo:4 * hi], after,
                               "gather_forward_" + tag)

    def finish(lo, hi, flight, after, tag):
        return _gather_finish(flight[2], later_blocks[lo:hi], flight[0], flight[1], after, "gather_finish_" + tag)

    def cols_of(r0, n, width):
        part = small_all[:, r0:r0 + n * width // LANES, :].reshape(N_DEV, n, width)
        return part.transpose(1, 0, 2).reshape(n, N_DEV * width)

    c_short = cols_of(small_at[0], 3, LANES)
    c_lru = cols_of(small_at[1], 4, LANES)
    b_a = cols_of(small_at[2], N_HEADS, shard_head).reshape(1, D_MODEL)
    b_x = cols_of(small_at[3], N_HEADS, shard_head).reshape(1, D_MODEL)
    c_ffn = cols_of(small_at[4], 3, shard_up)

    y_a = _conv_mixer_fwd(proj, behind(gather_token, c_short))
    y_b, hl, decay, lru_kept = _lru_fwd(proj, behind(gather_token, c_lru), lru_conv_b, g_wa, b_a, g_wx, b_x, lru_lambda)
    flight_mix_w = forward(0, 3, y_b, "mix")
    g_cb, g_lb, g_out = finish(0, 3, flight_mix_w, y_b, "mix")
    pa, pb, merged, mix, x1, h2 = _merge(y_a, y_b, proj, x2, g_cb, g_lb, g_out, norm_mix_post, norm_ffn_pre)
    flight_up_w = forward(3, 4, h2, "up")
    (g_up,) = finish(3, 4, flight_up_w, h2, "up")
    up, act, f = _ffn_up(h2, g_up, c_ffn, ffn_conv_b)
    flight_down_w = forward(4, 5, f, "down")
    (g_down,) = finish(4, 5, flight_down_w, f, "down")
    dy, d_out, d_act, dg4, loss_part = _ffn_down(f, act, g_down, x1, target, norm_ffn_post)

    block_of = dict(zip(names, blocks))
    shard_shapes = {"w_in": (D_MODEL, shard_in), "w_conv_branch": (shard_sq, D_MODEL), "w_lru_branch": (shard_sq, D_MODEL),
                    "w_out": (shard_sq, D_MODEL), "lru_wa": (N_HEADS, shard_head, HEAD_DIM),
                    "lru_wx": (N_HEADS, shard_head, HEAD_DIM), "ffn_w_up": (D_MODEL, shard_up),
                    "ffn_w_down": (shard_down, D_MODEL)}

    def reduce_start(tag, grads):
        keys = list(grads)
        sums = _reduce_pair([grads[k] for k in keys], [block_of[k] for k in keys], [shard_shapes[k] for k in keys],
                            "reduce_pair_" + tag)
        return (keys,) + _exchange_chips_start(sums, "reduce_chip_start_" + tag)

    gw_down = _grad_tn(f, d_out, min(512, D_FF), "ffn_down_wgrad")
    flight_down = reduce_start("down", {"ffn_w_down": gw_down})
    gw_up, gc_ffn, gb_ffn, d_h2 = _ffn_up_bwd(up, d_act, behind(flight_down[-1], c_ffn), h2, g_up)
    flight_up = reduce_start("up", {"ffn_w_up": gw_up})
    dx1, d_mix, d_pa, d_pb, d_ya, d_yb, d_gate, dg3, dg2 = _merge_bwd(
        dy, d_h2, x1, mix, behind(flight_up[-1], norm_ffn_pre), norm_mix_post, g_out, g_cb, g_lb, pa, pb, proj)
    gw_out = _grad_tn(merged, d_mix, CB, "w_out_wgrad")
    gw_cb = _grad_tn(y_a, d_pa, CB, "w_conv_branch_wgrad")
    gw_lb = _grad_tn(y_b, d_pb, CB, "w_lru_branch_wgrad")
    flight_mix = reduce_start("mix", {"w_conv_branch": gw_cb, "w_lru_branch": gw_lb, "w_out": gw_out})
    d_conv, gc_short = _conv_mixer_bwd(proj, d_ya, behind(flight_mix[-1], c_short))
    d_lru, gw_a, gw_x, g_lru_small = _lru_bwd(proj, hl, decay, lru_kept, d_yb, c_lru, g_wa, g_wx, lru_lambda)
    early = [dg2, dg3, dg4, g_lru_small[4:5], g_lru_small[7:8], gb_ffn, gc_short, g_lru_small[0:4],
             g_lru_small[5:6], g_lru_small[6:7], gc_ffn, loss_part]
    flight_small = _small_start(_pack_rows(early), "small_start")
    gw_in = _in_proj_wgrad(h, d_conv, d_lru, d_gate)
    flight_in = reduce_start("in", {"lru_wa": gw_a, "lru_wx": gw_x, "w_in": gw_in})
    dx, dg1 = _in_proj_xgrad(d_conv, d_lru, d_gate, g_in, x2, dx1,
                             behind(flight_small[-1], behind(flight_in[-1], norm_mix_pre)))
    flight_late = _small_start(_pack_rows([dg1]), "small_start_late")

    moments ={"w_in": (m_w_in, v_w_in), "w_conv_branch": (m_w_conv_branch, v_w_conv_branch),
               "w_lru_branch": (m_w_lru_branch, v_w_lru_branch), "w_out": (m_w_out, v_w_out),
               "lru_wa": (m_lru_wa, v_lru_wa), "lru_wx": (m_lru_wx, v_lru_wx), "ffn_w_up": (m_ffn_w_up, v_ffn_w_up),
               "ffn_w_down": (m_ffn_w_down, v_ffn_w_down)}
    weights = {"w_in": w_in, "w_conv_branch": w_conv_branch, "w_lru_branch": w_lru_branch, "w_out": w_out,
               "lru_wa": lru_wa, "lru_wx": lru_wx, "ffn_w_up": ffn_w_up, "ffn_w_down": ffn_w_down}
    out_g, out_d, out_m, out_v = {}, {}, {}, {}

    after = flight_late[-1]
    for tag, (keys, send, recv, sums, lands, _) in (("down", flight_down), ("up", flight_up), ("mix", flight_mix),
                                                    ("in", flight_in)):
        sums, others = _exchange_chips_wait(send, recv, sums, lands, after, "reduce_chip_wait_" + tag)
        for k, own, oth in zip(keys, sums, others):
            out_g[k], out_d[k], out_m[k], out_v[k] = _adam_large(weights[k], *moments[k], own, oth, "adam_" + k)
        after = out_d[keys[-1]]

    total, total_late = _small_sum([_small_wait(*flight_small[:4], after, "small_wait"),
                                    _small_wait(*flight_late[:4], after, "small_wait_late")], me)
    sizes = [p.size for p in early]
    starts = _packed_starts(sizes)

    def piece(i, shape):
        if i == 0:
            return total_late.reshape(-1)[:D_MODEL].reshape(shape)
        return total[starts[i - 1]:starts[i]].reshape(-1)[:sizes[i - 1]].reshape(shape)

    loss = total[starts[11], 0]

    def col_shard(full, width):
        return lax.dynamic_slice_in_dim(full, me * width, width, axis=1)

    def head_shard(full):
        return lax.dynamic_slice_in_dim(full.reshape(N_HEADS, HEAD_DIM), me * shard_head, shard_head, axis=1)

    small_names = ["norm_mix_pre", "norm_mix_post", "norm_ffn_pre", "norm_ffn_post", "lru_conv_b", "lru_lambda",
                   "ffn_conv_b", "conv_short_w", "lru_conv_w", "lru_ba", "lru_bx", "ffn_conv_w"]
    small_g = [piece(0, (1, D_MODEL)), piece(1, (1, D_MODEL)), piece(2, (1, D_MODEL)), piece(3, (1, D_MODEL)),
               piece(4, (1, D_MODEL)), piece(5, (1, D_MODEL)), piece(6, (1, 2 * D_FF)),
               col_shard(piece(7, (3, D_MODEL)), LANES), col_shard(piece(8, (4, D_MODEL)), LANES),
               head_shard(piece(9, (1, D_MODEL))), head_shard(piece(10, (1, D_MODEL))),
               col_shard(piece(11, (3, 2 * D_FF)), shard_up)]
    small_w = [norm_mix_pre, norm_mix_post, norm_ffn_pre, norm_ffn_post, lru_conv_b, lru_lambda, ffn_conv_b,
               conv_short_w[0], lru_conv_w[0], lru_ba[0], lru_bx[0], ffn_conv_w[0]]
    small_m = [m_norm_mix_pre, m_norm_mix_post, m_norm_ffn_pre, m_norm_ffn_post, m_lru_conv_b, m_lru_lambda,
               m_ffn_conv_b, m_conv_short_w[0], m_lru_conv_w[0], m_lru_ba[0], m_lru_bx[0], m_ffn_conv_w[0]]
    small_v = [v_norm_mix_pre, v_norm_mix_post, v_norm_ffn_pre, v_norm_ffn_post, v_lru_conv_b, v_lru_lambda,
               v_ffn_conv_b, v_conv_short_w[0], v_lru_conv_w[0], v_lru_ba[0], v_lru_bx[0], v_ffn_conv_w[0]]
    s_d, s_m, s_v = _adam_small(small_w, small_g, small_m, small_v)
    for i, name in enumerate(small_names):
        shape = small_w[i].shape if i < 7 else (1,) + small_w[i].shape
        out_g[name] = small_g[i].reshape(shape)
        out_d[name], out_m[name], out_v[name] = s_d[i].reshape(shape), s_m[i].reshape(shape), s_v[i].reshape(shape)

    order = ["norm_mix_pre", "norm_mix_post", "norm_ffn_pre", "norm_ffn_post", "w_in", "conv_short_w", "w_conv_branch",
             "lru_conv_w", "lru_conv_b", "lru_wa", "lru_ba", "lru_wx", "lru_bx", "lru_lambda", "w_lru_branch", "w_out",
             "ffn_w_up", "ffn_conv_w", "ffn_conv_b", "ffn_w_down"]
    return (loss, dx.reshape(1, t, D_MODEL), *[out_g[k] for k in order], *[out_d[k] for k in order],
            *[out_m[k] for k in order], *[out_v[k] for k in order])
```

```python
import functools
import math

import jax
import jax.numpy as jnp
from jax import lax
from jax.experimental import pallas as pl
from jax.experimental.pallas import tpu as pltpu

F32 = jnp.float32
BF16 = jnp.bfloat16
MESH = pl.DeviceIdType.MESH

N_DEV = 8
D_MODEL = 1024
N_HEADS = 4
HEAD_DIM = D_MODEL // N_HEADS
D_FF = 3 * D_MODEL
IN_COLS = 7 * D_MODEL
LRU_C = 8.0
RMS_EPS = 1e-6
ADAM_LR = 0.001
ADAM_B1 = 0.9
ADAM_B2 = 0.999
ADAM_EPS = 1e-08
ADAM_WD = 0.01
ADAM_STEP = 10
GELU_K = math.sqrt(2.0 / math.pi)
GELU_C = 0.044715

LANES = 128
SUBLANES = 8
PAD = SUBLANES
VMEM_LIMIT = 56 * 1024 * 1024
CB = 256

HBM_SPEC = pl.BlockSpec(memory_space=pltpu.HBM)
SEM_SPEC = pl.BlockSpec(memory_space=pltpu.SEMAPHORE)
DATAFLOW_EFFECT = pltpu.SideEffectType.DATAFLOW_SIDE_EFFECTING
VMEM_SPEC = pl.BlockSpec(memory_space=pltpu.VMEM)


def _params(*sem):
    if sem:
        return pltpu.CompilerParams(dimension_semantics=sem, vmem_limit_bytes=VMEM_LIMIT)
    return pltpu.CompilerParams(vmem_limit_bytes=VMEM_LIMIT)


def _row_chunk(t):
    return min(256, t)


def _row_block(rows, cap):
    return next(rb for rb in range(min(cap, rows), 0, -16) if rows % rb == 0)


def _gelu(x):
    return 0.5 * x * (1.0 + jnp.tanh(GELU_K * (x + GELU_C * x * x * x)))


def _gelu_and_grad(x):
    t = jnp.tanh(GELU_K * (x + GELU_C * x * x * x))
    g = 0.5 * x * (1.0 + t)
    dg = 0.5 * (1.0 + t) + 0.5 * x * (1.0 - t * t) * GELU_K * (1.0 + 3.0 * GELU_C * x * x)
    return g, dg


def _expm1_neg(x):
    series = x * (1.0 + x * (0.5 + x * (1.0 / 6.0 + x * (1.0 / 24.0 + x * (1.0 / 120.0)))))
    return jnp.where(x > -0.05, series, jnp.exp(x) - 1.0)


def _log_sigmoid(x):
    return jnp.minimum(x, 0.0) - jnp.log1p(jnp.exp(-jnp.abs(x)))


def _dot(a, b):
    return jnp.dot(a, b, preferred_element_type=F32)


def _dot_nt(a, b):
    return lax.dot_general(a, b, (((1,), (1,)), ((), ())), preferred_element_type=F32)


def _dot_tn(a, b):
    return lax.dot_general(a, b, (((0,), (0,)), ((), ())), preferred_element_type=F32)


def _rms_fwd(x):
    r = lax.rsqrt(jnp.mean(x * x, axis=-1, keepdims=True) + RMS_EPS)
    return x * r, r


def _rms_bwd(n, r, gdy):
    return r * (gdy - n * jnp.mean(n * gdy, axis=-1, keepdims=True))


def _rows_back(pad_ref, r0, rows, j):
    cur = pad_ref[pl.ds(PAD + r0, rows), :]
    if j == 0:
        return cur
    before = pad_ref[pl.ds(PAD + r0 - SUBLANES, SUBLANES), :]
    row = lax.broadcasted_iota(jnp.int32, before.shape, 0)
    rolled = pltpu.roll(cur, j, 0)
    top = jnp.where(row < j, pltpu.roll(before, j, 0), rolled[0:SUBLANES, :])
    return jnp.concatenate([top, rolled[SUBLANES:, :]], axis=0)


def _rows_ahead(pad_ref, r0, rows, j):
    cur = pad_ref[pl.ds(r0, rows), :]
    if j == 0:
        return cur
    after = pad_ref[pl.ds(r0 + rows, SUBLANES), :]
    row = lax.broadcasted_iota(jnp.int32, after.shape, 0)
    rolled = pltpu.roll(cur, rows - j, 0)
    bottom = jnp.where(row >= SUBLANES - j, pltpu.roll(after, SUBLANES - j, 0), rolled[rows - SUBLANES:, :])
    return jnp.concatenate([rolled[:rows - SUBLANES, :], bottom], axis=0)


def _conv_causal(pad_ref, w, r0, rows, taps):
    acc = None
    for k in range(taps):
        term = w[k:k + 1, :] * _rows_back(pad_ref, r0, rows, taps - 1 - k)
        acc = term if acc is None else acc + term
    return acc


def _conv_anticausal(pad_ref, w, r0, rows, taps):
    acc = None
    for k in range(taps):
        term = w[k:k + 1, :] * _rows_ahead(pad_ref, r0, rows, taps - 1 - k)
        acc = term if acc is None else acc + term
    return acc


def _conv_wgrad(g, xpad_ref, r0, rows, taps):
    return [jnp.sum(g * _rows_back(xpad_ref, r0, rows, taps - 1 - k), axis=0, keepdims=True) for k in range(taps)]


def _position():
    return lax.axis_index("x"), lax.axis_index("y"), lax.axis_index("c")


def _block_of(x, y, c):
    return 4 * x + 2 * y + c


def _chip(x, y, k):
    return (x + (k & 1)) % 2, (y + (k >> 1)) % 2


def _cols(width):
    def at(ref, d, half=None):
        cols = pl.ds(pl.multiple_of(d * width, LANES), width)
        if half is None:
            return ref.at[:, cols]
        return ref.at[pl.ds(half * (ref.shape[0] // 2), ref.shape[0] // 2), cols]
    return at


def _rows(height):
    def at(ref, d, half=None):
        if half is None:
            return ref.at[pl.ds(pl.multiple_of(d * height, 16), height), :]
        return ref.at[pl.ds(pl.multiple_of(d * height + half * (height // 2), 16), height // 2), :]
    return at


def _lead(ref, d, half=None):
    if half is None:
        return ref.at[d]
    return ref.at[d, pl.ds(half * (ref.shape[1] // 2), ref.shape[1] // 2)]


def _gather_weights(shards, blocks, full_shapes, small, n_now, tokens, gain):
    n = len(shards)
    small_rows = small.shape[0]
    t = tokens.shape[0]
    rc = min(512, t)

    def body(*refs):
        ins, small_in, x_ref, g_ref = refs[:n], refs[n], refs[n + 1], refs[n + 2]
        outs, small_out, proj_ref, h_ref = refs[n + 3:2 * n + 3], refs[2 * n + 3], refs[2 * n + 4], refs[2 * n + 5]
        stage = refs[2 * n + 6:3 * n + 6]
        w_buf, p_buf, send, recv, local, w_sem, p_sem = refs[3 * n + 6:]
        x, y, c = _position()
        me = _block_of(x, y, c)
        sibling = (x, y, 1 - c)

        for a in range(n):
            stage[a][...] = ins[a][...].astype(BF16)
        for r0 in range(0, t, rc):
            normed, _ = _rms_fwd(x_ref[pl.ds(r0, rc), :])
            h_ref[pl.ds(r0, rc), :] = (normed * g_ref[...]).astype(BF16)
        stores = []

        def project(w_ref, block):
            i = len(stores)
            if i >= 2:
                stores[i - 2].wait()
            for r0 in range(0, t, rc):
                p_buf[i % 2, pl.ds(r0, rc), :] = _dot(h_ref[pl.ds(r0, rc), :], w_ref[...]).astype(BF16)
            st = pltpu.make_async_copy(p_buf.at[i % 2], blocks[0](proj_ref, block), p_sem.at[i % 2])
            st.start()
            stores.append(st)

        def project_landed(block):
            ld = pltpu.make_async_copy(blocks[0](outs[0], block), w_buf, w_sem)
            ld.start()
            ld.wait()
            project(w_buf, block)

        def copy(a, k, block, to, src=None, half=None):
            dst = blocks[a](outs[a], block, half)
            return pltpu.make_async_remote_copy(
                src_ref=dst if src is None else src, dst_ref=dst, send_sem=send.at[a, k], recv_sem=recv.at[a, k],
                device_id=to, device_id_type=MESH)

        def small_copy(k):
            px, py, pc = (x + (k & 1)) % 2, (y + ((k >> 1) & 1)) % 2, (c + (k >> 2)) % 2
            return pltpu.make_async_remote_copy(
                src_ref=small_in, dst_ref=small_out.at[me], send_sem=send.at[n_now, k - 1], recv_sem=recv.at[n_now, k - 1],
                device_id=(px, py, pc), device_id_type=MESH)

        def small_arrival(k):
            px, py, pc = (x + (k & 1)) % 2, (y + ((k >> 1) & 1)) % 2, (c + (k >> 2)) % 2
            return pltpu.make_async_remote_copy(
                src_ref=small_in, dst_ref=small_out.at[_block_of(px, py, pc)], send_sem=send.at[n_now, k - 1],
                recv_sem=recv.at[n_now, k - 1], device_id=(px, py, pc), device_id_type=MESH)

        small_out[me] = small_in[...]
        small_sends = [small_copy(k) for k in range(1, N_DEV)]
        for cp in small_sends:
            cp.start()

        mine, first, passed = [], [], []
        for a in range(n):
            own = pltpu.make_async_copy(stage[a], blocks[a](outs[a], me), local.at[a])
            own.start()
            mine.append(own)
            if a >= n_now:
                continue
            sends = [copy(a, 0, me, sibling, src=stage[a])]
            sends += [copy(a, k, me, (*_chip(x, y, k), c), src=stage[a]) for k in (1, 2)]
            for cp in sends:
                cp.start()
            first += sends

        here = (x, y, c)
        across = [(*_chip(x, y, k), c) for k in (1, 2)]
        near = [[_block_of(*_chip(x, y, k), cc) for k in (1, 2)] for cc in (c, 1 - c)]
        far = [_block_of(*_chip(x, y, 3), cc) for cc in (c, 1 - c)]

        def launch(cp):
            cp.start()
            passed.append(cp)

        project(stage[0], me)
        copy(0, 0, _block_of(x, y, 1 - c), here).wait_recv()
        project_landed(_block_of(x, y, 1 - c))
        for a in range(n_now):
            for i in (0, 1):
                copy(a, 1 + i, near[0][i], here).wait_recv()
                launch(copy(a, 3 + i, near[0][i], across[1 - i], half=i))
                launch(copy(a, 5 + i, near[0][i], sibling))
            if a == 0:
                project_landed(near[0][0])
                project_landed(near[0][1])
        for i in (0, 1):
            copy(0, 5 + i, near[1][i], here).wait_recv()
            project_landed(near[1][i])
        for a in range(n_now):
            for i in (0, 1):
                copy(a, 3 + i, far[0], here, half=i).wait_recv()
                launch(copy(a, 7 + i, far[0], sibling, half=i))
            if a == 0:
                project_landed(far[0])
        for a in range(n_now):
            if a > 0:
                copy(a, 0, _block_of(x, y, 1 - c), here).wait_recv()
                for i in (0, 1):
                    copy(a, 5 + i, near[1][i], here).wait_recv()
            for i in (0, 1):
                copy(a, 7 + i, far[1], here, half=i).wait_recv()
            if a == 0:
                project_landed(far[1])
        for k in range(1, N_DEV):
            small_arrival(k).wait_recv()
        for cp in first + passed + small_sends:
            cp.wait_send()
        for done in mine + stores[-2:]:
            done.wait()

    out_shape = [jax.ShapeDtypeStruct(s, BF16) for s in full_shapes]
    out_shape += [jax.ShapeDtypeStruct((N_DEV, small_rows, LANES), F32), jax.ShapeDtypeStruct((t, full_shapes[0][1]), BF16),
                  jax.ShapeDtypeStruct(tokens.shape, BF16)]
    return pl.pallas_call(
        body, name="gather_weights", out_shape=out_shape,
        in_specs=[VMEM_SPEC] * (n + 3), out_specs=[HBM_SPEC] * n + [VMEM_SPEC, HBM_SPEC, VMEM_SPEC],
        scratch_shapes=[pltpu.VMEM(s.shape, BF16) for s in shards]
        + [pltpu.VMEM(shards[0].shape, BF16), pltpu.VMEM((2, t, shards[0].shape[1]), BF16),
           pltpu.SemaphoreType.DMA((n_now + 1, 9)), pltpu.SemaphoreType.DMA((n_now + 1, 9)),
           pltpu.SemaphoreType.DMA((n,)), pltpu.SemaphoreType.DMA(()), pltpu.SemaphoreType.DMA((2,))],
        compiler_params=_params(),
    )(*shards, small, tokens, gain)


def _gather_first(full, blocks, send, recv):
    x, y, c = _position()
    me = _block_of(x, y, c)
    peers = [(x, y, 1 - c)] + [(*_chip(x, y, k), c) for k in (1, 2, 3)]

    def copy(a, k, block):
        at = blocks[a](full[a], block)
        return pltpu.make_async_remote_copy(src_ref=at, dst_ref=at, send_sem=send[4 * a + k], recv_sem=recv[4 * a + k],
                                            device_id=peers[k], device_id_type=MESH)

    sends = [copy(a, k, me) for a in range(len(full)) for k in range(4)]
    arrivals = [copy(a, k, _block_of(*peers[k])) for a in range(len(full)) for k in range(4)]
    return sends, arrivals


def _gather_second(full, blocks, send, recv):
    x, y, c = _position()

    def copy(a, k, cc):
        at = blocks[a](full[a], _block_of(*_chip(x, y, k), cc))
        return pltpu.make_async_remote_copy(src_ref=at, dst_ref=at, send_sem=send[3 * a + k - 1],
                                            recv_sem=recv[3 * a + k - 1], device_id=(x, y, 1 - c), device_id_type=MESH)

    sends = [copy(a, k, c) for a in range(len(full)) for k in (1, 2, 3)]
    arrivals = [copy(a, k, 1 - c) for a in range(len(full)) for k in (1, 2, 3)]
    return sends, arrivals


def _split_call(body, name, arrays, sems_in, n_sems_out, after=None, token=False):
    n, m = len(arrays), len(sems_in)

    def kernel_body(*refs):
        outs = refs[n + m + (after is not None):]
        body(refs[:n], refs[n:n + m], outs[:n_sems_out])
        if token:
            outs[-1][...] = jnp.zeros_like(outs[-1])

    extra_in = [] if after is None else [after]
    outs = pl.pallas_call(
        kernel_body, name=name,
        out_shape=(*[pltpu.SemaphoreType.DMA(())] * n_sems_out, *[pltpu.HBM(a.shape, a.dtype) for a in arrays],
                   *([jax.ShapeDtypeStruct((SUBLANES, LANES), F32)] if token else [])),
        in_specs=[HBM_SPEC] * n + [SEM_SPEC] * m + [pl.BlockSpec(memory_space=pl.ANY)] * len(extra_in),
        out_specs=(*[SEM_SPEC] * n_sems_out, *[HBM_SPEC] * n, *([VMEM_SPEC] if token else [])),
        input_output_aliases={i: n_sems_out + i for i in range(n)},
        compiler_params=pltpu.CompilerParams(has_side_effects=DATAFLOW_EFFECT),
    )(*[pltpu.with_memory_space_constraint(a, pltpu.HBM) for a in arrays], *sems_in, *extra_in)
    sems, rest = list(outs[:n_sems_out]), list(outs[n_sems_out:])
    return (sems, rest[:n], rest[n]) if token else (sems, rest[:n])


def _gather_start(full, blocks, name):
    n = len(full)

    def body(arrays, _, sems):
        for cp in _gather_first(arrays, blocks, sems[:4 * n], sems[4 * n:])[0]:
            cp.start()

    sems, arrays, token = _split_call(body, name, full, [], 8 * n, token=True)
    return sems[:4 * n], sems[4 * n:], arrays, token


def _gather_forward(full, blocks, send_first, recv_first, after, name):
    n = len(full)

    def body(arrays, sems_in, sems):
        sends, arrivals = _gather_first(arrays, blocks, sems_in[:4 * n], sems_in[4 * n:])
        for cp in arrivals:
            cp.wait_recv()
        for cp in _gather_second(arrays, blocks, sems[:3 * n], sems[3 * n:])[0]:
            cp.start()
        for cp in sends:
            cp.wait_send()

    sems, arrays = _split_call(body, name, full, [*send_first, *recv_first], 6 * n, after=after)
    return sems[:3 * n], sems[3 * n:], arrays


def _gather_finish(full, blocks, send_second, recv_second, after, name):
    n = len(full)

    def body(arrays, sems_in, _):
        sends, arrivals = _gather_second(arrays, blocks, sems_in[:3 * n], sems_in[3 * n:])
        for cp in sends:
            cp.wait_send()
        for cp in arrivals:
            cp.wait_recv()

    return _split_call(body, name, full, [*send_second, *recv_second], 0, after=after)[1]


def _reduce_pair(grads, blocks, shard_shapes, name):
    n = len(grads)

    def body(*refs):
        ins, outs = refs[:n], refs[n:2 * n]
        got, own = refs[2 * n:3 * n], refs[3 * n:4 * n]
        send, recv, local = refs[4 * n:]
        x, y, c = _position()
        copies, loads = [], []
        for a in range(n):
            for k in range(4):
                chip = _chip(x, y, k)
                cp = pltpu.make_async_remote_copy(
                    src_ref=blocks[a](ins[a], _block_of(*chip, 1 - c)), dst_ref=got[a].at[k],
                    send_sem=send.at[a, k], recv_sem=recv.at[a, k], device_id=(x, y, 1 - c), device_id_type=MESH)
                cp.start()
                copies.append(cp)
                ld = pltpu.make_async_copy(blocks[a](ins[a], _block_of(*chip, c)), own[a].at[k], local.at[a, k])
                ld.start()
                loads.append(ld)
        for a in range(n):
            for k in range(4):
                loads[4 * a + k].wait()
                copies[4 * a + k].wait_recv()
                outs[a][k] = (own[a][k].astype(F32) + got[a][k].astype(F32)).astype(BF16)
        for cp in copies:
            cp.wait_send()

    slots = [(4,) + tuple(s) for s in shard_shapes]
    return pl.pallas_call(
        body, name=name, out_shape=[jax.ShapeDtypeStruct(s, BF16) for s in slots],
        in_specs=[HBM_SPEC] * n, out_specs=[VMEM_SPEC] * n,
        scratch_shapes=[pltpu.VMEM(s, BF16) for s in slots] * 2
        + [pltpu.SemaphoreType.DMA((n, 4)), pltpu.SemaphoreType.DMA((n, 4)), pltpu.SemaphoreType.DMA((n, 4))],
        compiler_params=_params(),
    )(*grads)


def _chip_copies(sums, lands, send, recv):
    x, y, c = _position()
    return [pltpu.make_async_remote_copy(
        src_ref=sums[a].at[k], dst_ref=lands[a].at[k - 1], send_sem=send[3 * a + k - 1], recv_sem=recv[3 * a + k - 1],
        device_id=(*_chip(x, y, k), c), device_id_type=MESH) for a in range(len(sums)) for k in (1, 2, 3)]


def _exchange_chips_start(pair_sums, name):
    n = len(pair_sums)
    lands = [pltpu.with_memory_space_constraint(lax.empty((3,) + tuple(p.shape[1:]), BF16), pltpu.HBM) for p in pair_sums]

    def body(*refs):
        sums, zones = refs[:n], refs[n:2 * n]
        send, recv = refs[2 * n:5 * n], refs[5 * n:8 * n]
        token = refs[-1]
        for cp in _chip_copies(sums, zones, send, recv):
            cp.start()
        token[...] = jnp.zeros_like(token)

    outs = pl.pallas_call(
        body, name=name,
        out_shape=(*[pltpu.SemaphoreType.DMA(())] * (6 * n),
                   *[pltpu.HBM(p.shape, BF16) for p in pair_sums], *[pltpu.HBM(z.shape, BF16) for z in lands],
                   jax.ShapeDtypeStruct((SUBLANES, LANES), F32)),
        in_specs=[HBM_SPEC] * (2 * n), out_specs=(*[SEM_SPEC] * (6 * n), *[HBM_SPEC] * (2 * n), VMEM_SPEC),
        input_output_aliases={i: 6 * n + i for i in range(2 * n)},
        compiler_params=pltpu.CompilerParams(has_side_effects=DATAFLOW_EFFECT),
    )(*[pltpu.with_memory_space_constraint(p, pltpu.HBM) for p in pair_sums], *lands)
    return outs[:3 * n], outs[3 * n:6 * n], outs[6 * n:7 * n], outs[7 * n:8 * n], outs[-1]


def _exchange_chips_wait(send, recv, sums, lands, after, name):
    n = len(sums)

    def body(*refs):
        sums_in, zones = refs[:n], refs[n:2 * n]
        send_in, recv_in = refs[2 * n:5 * n], refs[5 * n:8 * n]
        for cp in _chip_copies(sums_in, zones, send_in, recv_in):
            cp.wait_send()
            cp.wait_recv()

    outs = pl.pallas_call(
        body, name=name,
        out_shape=(*[pltpu.HBM(p.shape, BF16) for p in sums], *[pltpu.HBM(z.shape, BF16) for z in lands]),
        in_specs=[HBM_SPEC] * (2 * n) + [SEM_SPEC] * (6 * n) + [pl.BlockSpec(memory_space=pl.ANY)],
        out_specs=[HBM_SPEC] * (2 * n), input_output_aliases={i: i for i in range(2 * n)},
        compiler_params=pltpu.CompilerParams(has_side_effects=DATAFLOW_EFFECT),
    )(*sums, *lands, *send, *recv, after)
    return outs[:n], outs[n:]


def _small_copies(mine, land, send, recv):
    x, y, c = _position()
    me = _block_of(x, y, c)

    def peer(k):
        return (x + (k & 1)) % 2, (y + ((k >> 1) & 1)) % 2, (c + (k >> 2)) % 2

    def copy(k, slot):
        return pltpu.make_async_remote_copy(src_ref=mine, dst_ref=land.at[slot], send_sem=send[k - 1], recv_sem=recv[k - 1],
                                            device_id=peer(k), device_id_type=MESH)

    return [copy(k, me) for k in range(1, N_DEV)], [copy(k, _block_of(*peer(k))) for k in range(1, N_DEV)]


def _small_start(part, name):
    land = jnp.zeros((N_DEV,) + part.shape, F32)

    def body(arrays, _, sems):
        for cp in _small_copies(arrays[0], arrays[1], sems[:7], sems[7:])[0]:
            cp.start()

    sems, arrays, token = _split_call(body, name, [part, land], [], 14, token=True)
    return sems[:7], sems[7:], arrays[0], arrays[1], token


def _small_wait(send, recv, part, land, after, name):
    def body(arrays, sems_in, _):
        sends, arrivals = _small_copies(arrays[0], arrays[1], sems_in[:7], sems_in[7:])
        for cp in sends:
            cp.wait_send()
        for cp in arrivals:
            cp.wait_recv()

    return _split_call(body, name, [part, land], [*send, *recv], 0, after=after)[1]


def _small_sum(pairs, me):
    n = len(pairs)

    def body(me_ref, *refs):
        for i in range(n):
            mine, land, out = refs[2 * i], refs[2 * i + 1], refs[2 * n + i]
            total = jnp.zeros(mine.shape, F32)
            for d in range(N_DEV):
                total = total + land[d] + jnp.where(me_ref[0] == d, mine[...], 0.0)
            out[...] = total

    flat = [a for pair in pairs for a in pair]
    return pl.pallas_call(
        body, name="small_sum", out_shape=[jax.ShapeDtypeStruct(mine.shape, F32) for mine, _ in pairs],
        in_specs=[pl.BlockSpec(memory_space=pltpu.SMEM)] + [VMEM_SPEC] * (2 * n), out_specs=[VMEM_SPEC] * n,
        compiler_params=_params(),
    )(me.reshape(1).astype(jnp.int32), *flat)


def _section(s, t):
    return pl.BlockSpec((t, CB), lambda h, s=s: (0, s * (D_MODEL // CB) + h))


def _conv_mixer_fwd(proj, w_short):
    t = proj.shape[0]
    rc = _row_chunk(t)

    def body(b_ref, c_ref, x_ref, w_ref, y_ref, pad):
        pad[pl.ds(0, PAD), :] = jnp.zeros((PAD, CB), F32)
        for r0 in range(0, t, rc):
            rows = pl.ds(r0, rc)
            pad[pl.ds(PAD + r0, rc), :] = c_ref[rows, :].astype(F32) * x_ref[rows, :].astype(F32)
        w = w_ref[...]
        for r0 in range(0, t, rc):
            rows = pl.ds(r0, rc)
            y_ref[rows, :] = (b_ref[rows, :].astype(F32) * _conv_causal(pad, w, r0, rc, 3)).astype(BF16)

    return pl.pallas_call(
        body, name="conv_mixer_fwd", grid=(D_MODEL // CB,),
        out_shape=jax.ShapeDtypeStruct((t, D_MODEL), BF16),
        in_specs=[_section(0, t), _section(1, t), _section(2, t), pl.BlockSpec((3, CB), lambda h: (0, h))],
        out_specs=pl.BlockSpec((t, CB), lambda h: (0, h)),
        scratch_shapes=[pltpu.VMEM((t + PAD, CB), F32)],
        compiler_params=_params("parallel"),
    )(proj, proj, proj, w_short)


def _lru_gates(xl, wa, ba, wx, bx, ls, first_row):
    xb = xl.astype(BF16)
    ra = jax.nn.sigmoid(_dot(xb, wa) + ba)
    ia = jax.nn.sigmoid(_dot(xb, wx) + bx)
    la = LRU_C * ra * ls
    a = jnp.exp(la)
    one_minus = -_expm1_neg(2.0 * la)
    mult = jnp.where(first_row, 1.0, jnp.sqrt(one_minus))
    return xb, ra, ia, a, one_minus, mult


def _head_specs():
    vec = pl.BlockSpec((1, CB), lambda h: (0, h))
    mat = pl.BlockSpec((N_DEV, None, HEAD_DIM // N_DEV, HEAD_DIM), lambda h: (0, h, 0, 0))
    return vec, mat


def _lru_fwd(proj, w_conv, b_conv, wa, ba, wx, bx, lam):
    t = proj.shape[0]
    rc = _row_chunk(t)
    vec, mat = _head_specs()

    def body(lx_ref, ly_ref, wc_ref, bc_ref, wa_ref, ba_ref, wx_ref, bx_ref, lam_ref, yb_ref, hl_ref, a_ref, kept_ref,
             pad, u_s):
        pad[pl.ds(0, PAD), :] = jnp.zeros((PAD, CB), F32)
        for r0 in range(0, t, rc):
            pad[pl.ds(PAD + r0, rc), :] = lx_ref[pl.ds(r0, rc), :].astype(F32)
        wc, bc = wc_ref[...], bc_ref[...]
        wa_m, wx_m = wa_ref[...].reshape(HEAD_DIM, HEAD_DIM), wx_ref[...].reshape(HEAD_DIM, HEAD_DIM)
        ls = _log_sigmoid(lam_ref[...])
        for r0 in range(0, t, rc):
            rows = pl.ds(r0, rc)
            xl = _conv_causal(pad, wc, r0, rc, 4) + bc
            first = (lax.broadcasted_iota(jnp.int32, (rc, CB), 0) + r0) == 0
            xb, ra, ia, a, _, mult = _lru_gates(xl, wa_m, ba_ref[...], wx_m, bx_ref[...], ls, first)
            a_ref[rows, :] = a
            u_s[rows, :] = mult * (ia * xl)
            kept_ref[0, rows, :] = xb
            kept_ref[1, rows, :] = ra.astype(BF16)
            kept_ref[2, rows, :] = ia.astype(BF16)

        row = lax.broadcasted_iota(jnp.int32, (SUBLANES, CB), 0)

        def group(g, carry):
            r = pl.multiple_of(g * SUBLANES, SUBLANES)
            a_g, b_g = a_ref[pl.ds(r, SUBLANES), :], u_s[pl.ds(r, SUBLANES), :]
            for s in (1, 2, 4):
                keep = row >= s
                b_g = jnp.where(keep, a_g * pltpu.roll(b_g, s, 0) + b_g, b_g)
                a_g = jnp.where(keep, a_g * pltpu.roll(a_g, s, 0), a_g)
            h_g = b_g + a_g * carry
            hl_ref[pl.ds(r, SUBLANES), :] = h_g
            return jnp.broadcast_to(h_g[SUBLANES - 1:SUBLANES, :], (SUBLANES, CB))

        lax.fori_loop(0, t // SUBLANES, group, jnp.zeros((SUBLANES, CB), F32))
        for r0 in range(0, t, rc):
            rows = pl.ds(r0, rc)
            yb_ref[rows, :] = (hl_ref[rows, :] * _gelu(ly_ref[rows, :].astype(F32))).astype(BF16)

    blk = pl.BlockSpec((t, CB), lambda h: (0, h))
    res = jax.ShapeDtypeStruct((t, D_MODEL), F32)
    return pl.pallas_call(
        body, name="lru_fwd", grid=(N_HEADS,),
        out_shape=[jax.ShapeDtypeStruct((t, D_MODEL), BF16), res, res, jax.ShapeDtypeStruct((3, t, D_MODEL), BF16)],
        in_specs=[_section(3, t), _section(4, t), pl.BlockSpec((4, CB), lambda h: (0, h)), vec, mat, vec, mat, vec, vec],
        out_specs=[blk, blk, blk, pl.BlockSpec((3, t, CB), lambda h: (0, 0, h))],
        scratch_shapes=[pltpu.VMEM((t + PAD, CB), F32), pltpu.VMEM((t, CB), F32)],
        compiler_params=_params("parallel"),
    )(proj, proj, w_conv, b_conv, wa, ba, wx, bx, lam)


def _merge(y_a, y_b, proj, x, w_cb, w_lb, w_out, g2, g3):
    t = x.shape[0]
    tm = min(512, t)

    def body(ya_ref, yb_ref, gc_ref, gl_ref, x_ref, wcb_ref, wlb_ref, wo_ref, g2_ref, g3_ref,
             pa_ref, pb_ref, mg_ref, mix_ref, x1_ref, h2_ref):
        pa = _dot(ya_ref[...], wcb_ref[...]).astype(BF16)
        pb = _dot(yb_ref[...], wlb_ref[...]).astype(BF16)
        pa_ref[...] = pa
        pb_ref[...] = pb
        merged = (jax.nn.sigmoid(gc_ref[...].astype(F32)) * pa.astype(F32)
                  + jax.nn.sigmoid(gl_ref[...].astype(F32)) * pb.astype(F32)).astype(BF16)
        mg_ref[...] = merged
        mix = _dot(merged, wo_ref[...])
        mix_ref[...] = mix
        n2, _ = _rms_fwd(mix)
        x1 = x_ref[...] + n2 * g2_ref[...]
        x1_ref[...] = x1
        n3, _ = _rms_fwd(x1)
        h2_ref[...] = (n3 * g3_ref[...]).astype(BF16)

    row = pl.BlockSpec((tm, D_MODEL), lambda i: (i, 0))
    full = pl.BlockSpec((D_MODEL, D_MODEL), lambda i: (0, 0))
    vec = pl.BlockSpec((1, D_MODEL), lambda i: (0, 0))
    act = jax.ShapeDtypeStruct((t, D_MODEL), BF16)
    res = jax.ShapeDtypeStruct((t, D_MODEL), F32)
    return pl.pallas_call(
        body, name="merge_fwd", grid=(t // tm,), out_shape=[act, act, act, res, res, act],
        in_specs=[row, row, pl.BlockSpec((tm, D_MODEL), lambda i: (i, 5)), pl.BlockSpec((tm, D_MODEL), lambda i: (i, 6)),
                  row, full, full, full, vec, vec],
        out_specs=[row] * 6,
        compiler_params=_params("parallel"),
    )(y_a, y_b, proj, proj, x, w_cb, w_lb, w_out, g2, g3)


N_FF_BLOCKS = D_FF // CB
FFN_BWD_COLS = 512


def _ffn_up(h2, w_up, w_conv, b_conv):
    t = h2.shape[0]
    rc = _row_chunk(t)
    nb = N_FF_BLOCKS

    def body(h_ref, w_ref, c_ref, b_ref, up_ref, act_ref, f_ref, pad, gate):
        k = pl.program_id(1)
        pad[pl.ds(0, PAD), :] = jnp.zeros((PAD, CB), F32)
        for r0 in range(0, t, rc):
            rows = pl.ds(r0, rc)
            up = _dot(h_ref[rows, :], w_ref[...]).astype(BF16)
            up_ref[rows, :] = up
            pad[pl.ds(PAD + r0, rc), :] = up.astype(F32)
        cw = c_ref[...]
        for r0 in range(0, t, rc):
            rows = pl.ds(r0, rc)
            act = _conv_causal(pad, cw, r0, rc, 3) + b_ref[...]
            act_ref[rows, :] = act.astype(BF16)

            @pl.when(k == 0)
            def _():
                gate[rows, :] = act

            @pl.when(k == 1)
            def _():
                f_ref[rows, :] = (_gelu(gate[rows, :]) * act).astype(BF16)

    half = lambda rows: pl.BlockSpec((rows, CB), lambda j, k: (0, nb * k + j))
    wide = jax.ShapeDtypeStruct((t, 2 * D_FF), BF16)
    return pl.pallas_call(
        body, name="ffn_up_fwd", grid=(nb, 2), out_shape=[wide, wide, jax.ShapeDtypeStruct((t, D_FF), BF16)],
        in_specs=[pl.BlockSpec((t, D_MODEL), lambda j, k: (0, 0)), half(D_MODEL), half(3), half(1)],
        out_specs=[half(t), half(t), pl.BlockSpec((t, CB), lambda j, k: (0, j))],
        scratch_shapes=[pltpu.VMEM((t + PAD, CB), F32), pltpu.VMEM((t, CB), F32)],
        compiler_params=_params("parallel", "arbitrary"),
    )(h2, w_up, w_conv, b_conv)


def _ffn_down(f, act, w_down, x1, target, g4):
    t = f.shape[0]
    tm = min(256, t)
    cc = 512

    def body(f_ref, act_ref, w_ref, x1_ref, tg_ref, g_ref, dy_ref, dout_ref, back_ref, dg_ref, loss_ref):
        @pl.when(pl.program_id(0) == 0)
        def _():
            dg_ref[...] = jnp.zeros_like(dg_ref)
            loss_ref[...] = jnp.zeros_like(loss_ref)
        out = _dot(f_ref[...], w_ref[...])
        n4, r4 = _rms_fwd(out)
        err = x1_ref[...] + n4 * g_ref[...] - tg_ref[...]
        loss_ref[...] += jnp.full(loss_ref.shape, 0.5 / D_MODEL, F32) * jnp.sum(err * err)
        dy = err * (1.0 / D_MODEL)
        dy_ref[...] = dy
        dg_ref[...] += jnp.sum(dy * n4, axis=0, keepdims=True)
        d_out = _rms_bwd(n4, r4, dy * g_ref[...]).astype(BF16)
        dout_ref[...] = d_out
        for c0 in range(0, D_FF, cc):
            d_f = _dot_nt(d_out, w_ref[pl.ds(c0, cc), :])
            gelu, d_gelu = _gelu_and_grad(act_ref[:, pl.ds(c0, cc)].astype(F32))
            val = act_ref[:, pl.ds(D_FF + c0, cc)].astype(F32)
            back_ref[:, pl.ds(c0, cc)] = (d_f * val * d_gelu).astype(BF16)
            back_ref[:, pl.ds(D_FF + c0, cc)] = (d_f * gelu).astype(BF16)

    row = pl.BlockSpec((tm, D_MODEL), lambda i: (i, 0))
    wide = pl.BlockSpec((tm, 2 * D_FF), lambda i: (i, 0))
    vec = pl.BlockSpec((1, D_MODEL), lambda i: (0, 0))
    return pl.pallas_call(
        body, name="ffn_down_fwd_bwd", grid=(t // tm,),
        out_shape=[jax.ShapeDtypeStruct((t, D_MODEL), F32), jax.ShapeDtypeStruct((t, D_MODEL), BF16),
                   jax.ShapeDtypeStruct((t, 2 * D_FF), BF16), jax.ShapeDtypeStruct((1, D_MODEL), F32),
                   jax.ShapeDtypeStruct((SUBLANES, LANES), F32)],
        in_specs=[pl.BlockSpec((tm, D_FF), lambda i: (i, 0)), wide, pl.BlockSpec((D_FF, D_MODEL), lambda i: (0, 0)),
                  row, row, vec],
        out_specs=[row, row, wide, vec, pl.BlockSpec((SUBLANES, LANES), lambda i: (0, 0))],
        compiler_params=_params("arbitrary"),
    )(f, act, w_down, x1, target, g4)


def _grad_tn(a, b, bm, name):
    t, m = a.shape
    n = b.shape[1]

    def body(a_ref, b_ref, o_ref):
        o_ref[...] = _dot_tn(a_ref[...], b_ref[...]).astype(BF16)

    return pl.pallas_call(
        body, name=name, grid=(m // bm,), out_shape=jax.ShapeDtypeStruct((m, n), BF16),
        in_specs=[pl.BlockSpec((t, bm), lambda i: (0, i)), pl.BlockSpec((t, n), lambda i: (0, 0))],
        out_specs=pl.BlockSpec((bm, n), lambda i: (i, 0)),
        compiler_params=_params("parallel"),
    )(a, b)


def _ffn_up_bwd(up, back, w_conv, h2, w_up):
    t = h2.shape[0]
    rc = _row_chunk(t)
    cb = FFN_BWD_COLS

    def body(up_ref, back_ref, c_ref, h_ref, w_ref, dw_ref, dcw_ref, dcb_ref, dh_ref, pad, after, d_up):
        @pl.when(pl.program_id(0) == 0)
        def _():
            dh_ref[...] = jnp.zeros_like(dh_ref)
        pad[pl.ds(0, PAD), :] = jnp.zeros((PAD, cb), F32)
        after[pl.ds(t, PAD), :] = jnp.zeros((PAD, cb), F32)
        for r0 in range(0, t, rc):
            pad[pl.ds(PAD + r0, rc), :] = up_ref[pl.ds(r0, rc), :].astype(F32)
            after[pl.ds(r0, rc), :] = back_ref[pl.ds(r0, rc), :].astype(F32)
        cw = c_ref[...]
        taps = [jnp.zeros((1, cb), F32)] * 3
        bias = jnp.zeros((1, cb), F32)
        for r0 in range(0, t, rc):
            rows = pl.ds(r0, rc)
            d = _conv_anticausal(after, cw, r0, rc, 3).astype(BF16)
            d_up[rows, :] = d
            dh_ref[rows, :] += _dot_nt(d, w_ref[...])
            g = after[rows, :]
            taps = [acc + new for acc, new in zip(taps, _conv_wgrad(g, pad, r0, rc, 3))]
            bias = bias + jnp.sum(g, axis=0, keepdims=True)
        dw_ref[...] = _dot_tn(h_ref[...], d_up[...]).astype(BF16)
        dcw_ref[...] = jnp.concatenate(taps, axis=0)
        dcb_ref[...] = bias

    cols = lambda rows: pl.BlockSpec((rows, cb), lambda j: (0, j))
    whole = pl.BlockSpec((t, D_MODEL), lambda j: (0, 0))
    return pl.pallas_call(
        body, name="ffn_up_bwd", grid=(2 * D_FF // cb,),
        out_shape=[jax.ShapeDtypeStruct((D_MODEL, 2 * D_FF), BF16), jax.ShapeDtypeStruct((3, 2 * D_FF), F32),
                   jax.ShapeDtypeStruct((1, 2 * D_FF), F32), jax.ShapeDtypeStruct((t, D_MODEL), F32)],
        in_specs=[cols(t), cols(t), cols(3), whole, cols(D_MODEL)],
        out_specs=[cols(D_MODEL), cols(3), cols(1), whole],
        scratch_shapes=[pltpu.VMEM((t + PAD, cb), F32), pltpu.VMEM((t + PAD, cb), F32), pltpu.VMEM((t, cb), BF16)],
        compiler_params=_params("arbitrary"),
    )(up, back, w_conv, h2, w_up)


def _merge_bwd(dy, d_h2, x1, mix, g3, g2, w_out, w_cb, w_lb, pa, pb, proj):
    t = dy.shape[0]
    tm = min(256, t)

    def body(dy_ref, dh2_ref, x1_ref, mix_ref, g3_ref, g2_ref, wo_ref, wcb_ref, wlb_ref, pa_ref, pb_ref, gc_ref, gl_ref,
             dx1_ref, dmix_ref, dpa_ref, dpb_ref, dya_ref, dyb_ref, dgate_ref, dg3_ref, dg2_ref):
        @pl.when(pl.program_id(0) == 0)
        def _():
            dg3_ref[...] = jnp.zeros_like(dg3_ref)
            dg2_ref[...] = jnp.zeros_like(dg2_ref)
        n3, r3 = _rms_fwd(x1_ref[...])
        d_h2 = dh2_ref[...]
        dg3_ref[...] += jnp.sum(d_h2 * n3, axis=0, keepdims=True)
        dx1 = dy_ref[...] + _rms_bwd(n3, r3, d_h2 * g3_ref[...])
        dx1_ref[...] = dx1
        n2, r2 = _rms_fwd(mix_ref[...])
        dg2_ref[...] += jnp.sum(dx1 * n2, axis=0, keepdims=True)
        d_mix = _rms_bwd(n2, r2, dx1 * g2_ref[...]).astype(BF16)
        dmix_ref[...] = d_mix
        d_merged = _dot_nt(d_mix, wo_ref[...])
        sc = jax.nn.sigmoid(gc_ref[...].astype(F32))
        sl = jax.nn.sigmoid(gl_ref[...].astype(F32))
        d_pa = (d_merged * sc).astype(BF16)
        d_pb = (d_merged * sl).astype(BF16)
        dpa_ref[...] = d_pa
        dpb_ref[...] = d_pb
        dgate_ref[0] = (d_merged * pa_ref[...].astype(F32) * sc * (1.0 - sc)).astype(BF16)
        dgate_ref[1] = (d_merged * pb_ref[...].astype(F32) * sl * (1.0 - sl)).astype(BF16)
        dya_ref[...] = _dot_nt(d_pa, wcb_ref[...]).astype(BF16)
        dyb_ref[...] = _dot_nt(d_pb, wlb_ref[...]).astype(BF16)

    row = pl.BlockSpec((tm, D_MODEL), lambda i: (i, 0))
    full = pl.BlockSpec((D_MODEL, D_MODEL), lambda i: (0, 0))
    vec = pl.BlockSpec((1, D_MODEL), lambda i: (0, 0))
    act = jax.ShapeDtypeStruct((t, D_MODEL), BF16)
    small = jax.ShapeDtypeStruct((1, D_MODEL), F32)
    return pl.pallas_call(
        body, name="merge_bwd", grid=(t // tm,),
        out_shape=[jax.ShapeDtypeStruct((t, D_MODEL), F32), act, act, act, act, act,
                   jax.ShapeDtypeStruct((2, t, D_MODEL), BF16), small, small],
        in_specs=[row, row, row, row, vec, vec, full, full, full, row, row,
                  pl.BlockSpec((tm, D_MODEL), lambda i: (i, 5)), pl.BlockSpec((tm, D_MODEL), lambda i: (i, 6))],
        out_specs=[row] * 6 + [pl.BlockSpec((2, tm, D_MODEL), lambda i: (0, i, 0)), vec, vec],
        compiler_params=_params("arbitrary"),
    )(dy, d_h2, x1, mix, g3, g2, w_out, w_cb, w_lb, pa, pb, proj, proj)


def _conv_mixer_bwd(proj, d_ya, w_short):
    t = proj.shape[0]
    rc = _row_chunk(t)

    def body(b_ref, c_ref, x_ref, dy_ref, w_ref, d_ref, dw_ref, pad, back):
        pad[pl.ds(0, PAD), :] = jnp.zeros((PAD, CB), F32)
        back[pl.ds(t, PAD), :] = jnp.zeros((PAD, CB), F32)
        for r0 in range(0, t, rc):
            rows = pl.ds(r0, rc)
            pad[pl.ds(PAD + r0, rc), :] = c_ref[rows, :].astype(F32) * x_ref[rows, :].astype(F32)
        w = w_ref[...]
        for r0 in range(0, t, rc):
            rows = pl.ds(r0, rc)
            d_y = dy_ref[rows, :].astype(F32)
            d_ref[0, rows, :] = (d_y * _conv_causal(pad, w, r0, rc, 3)).astype(BF16)
            back[rows, :] = d_y * b_ref[rows, :].astype(F32)
        taps = [jnp.zeros((1, CB), F32)] * 3
        for r0 in range(0, t, rc):
            rows = pl.ds(r0, rc)
            d_u = _conv_anticausal(back, w, r0, rc, 3)
            d_ref[1, rows, :] = (d_u * x_ref[rows, :].astype(F32)).astype(BF16)
            d_ref[2, rows, :] = (d_u * c_ref[rows, :].astype(F32)).astype(BF16)
            taps = [acc + new for acc, new in zip(taps, _conv_wgrad(back[rows, :], pad, r0, rc, 3))]
        dw_ref[...] = jnp.concatenate(taps, axis=0)

    blk = pl.BlockSpec((t, CB), lambda h: (0, h))
    return pl.pallas_call(
        body, name="conv_mixer_bwd", grid=(D_MODEL // CB,),
        out_shape=[jax.ShapeDtypeStruct((3, t, D_MODEL), BF16), jax.ShapeDtypeStruct((3, D_MODEL), F32)],
        in_specs=[_section(0, t), _section(1, t), _section(2, t), blk, pl.BlockSpec((3, CB), lambda h: (0, h))],
        out_specs=[pl.BlockSpec((3, t, CB), lambda h: (0, 0, h)), pl.BlockSpec((3, CB), lambda h: (0, h))],
        scratch_shapes=[pltpu.VMEM((t + PAD, CB), F32), pltpu.VMEM((t + PAD, CB), F32)],
        compiler_params=_params("parallel"),
    )(proj, proj, proj, d_ya, w_short)


LRU_SMALL_ROWS = 8


def _lru_bwd(proj, hl, a_all, kept, d_yb, w_conv, wa, wx, lam):
    t = proj.shape[0]
    rc = _row_chunk(t)
    vec, mat = _head_specs()

    def body(lx_ref, ly_ref, hl_ref, a_ref, kept_ref, dy_ref, wc_ref, wa_ref, wx_ref, lam_ref,
             d_ref, dwa_ref, dwx_ref, small_ref, pad, a_next, dh_s, h_prev, back, acc_a, acc_x):
        zeros = jnp.zeros((PAD, CB), F32)
        pad[pl.ds(0, PAD), :] = zeros
        h_prev[pl.ds(0, PAD), :] = zeros
        a_next[pl.ds(t, PAD), :] = zeros
        back[pl.ds(t, PAD), :] = zeros
        for r0 in range(0, t, rc):
            rows = pl.ds(r0, rc)
            pad[pl.ds(PAD + r0, rc), :] = lx_ref[rows, :].astype(F32)
            h_prev[pl.ds(PAD + r0, rc), :] = hl_ref[rows, :]
            a_next[pl.ds(PAD - 1 + r0, rc), :] = a_ref[rows, :]
            act, d_act = _gelu_and_grad(ly_ref[rows, :].astype(F32))
            d_y = dy_ref[rows, :].astype(F32)
            dh_s[rows, :] = d_y * act
            d_ref[1, rows, :] = (d_y * hl_ref[rows, :] * d_act).astype(BF16)
        wc = wc_ref[...]
        wa_m, wx_m = wa_ref[...].reshape(HEAD_DIM, HEAD_DIM), wx_ref[...].reshape(HEAD_DIM, HEAD_DIM)
        ls = _log_sigmoid(lam_ref[...])

        row = lax.broadcasted_iota(jnp.int32, (SUBLANES, CB), 0)
        groups = t // SUBLANES

        def group(i, carry):
            r = pl.multiple_of((groups - 1 - i) * SUBLANES, SUBLANES)
            a_g, b_g = a_next[pl.ds(PAD + r, SUBLANES), :], dh_s[pl.ds(r, SUBLANES), :]
            for s in (1, 2, 4):
                keep = row < SUBLANES - s
                b_g = jnp.where(keep, a_g * pltpu.roll(b_g, SUBLANES - s, 0) + b_g, b_g)
                a_g = jnp.where(keep, a_g * pltpu.roll(a_g, SUBLANES - s, 0), a_g)
            d_g = b_g + a_g * carry
            dh_s[pl.ds(r, SUBLANES), :] = d_g
            return jnp.broadcast_to(d_g[0:1, :], (SUBLANES, CB))

        lax.fori_loop(0, groups, group, jnp.zeros((SUBLANES, CB), F32))

        acc_a[...] = jnp.zeros_like(acc_a)
        acc_x[...] = jnp.zeros_like(acc_x)
        d_ba = d_bx = d_ls = jnp.zeros((1, CB), F32)
        for r0 in range(0, t, rc):
            rows = pl.ds(r0, rc)
            first = (lax.broadcasted_iota(jnp.int32, (rc, CB), 0) + r0) == 0
            xb, a = kept_ref[0, rows, :], a_ref[rows, :]
            xl, ra, ia = xb.astype(F32), kept_ref[1, rows, :].astype(F32), kept_ref[2, rows, :].astype(F32)
            a_sq = a * a
            mult = jnp.where(first, 1.0, jnp.sqrt(1.0 - a_sq))
            d_h = dh_s[rows, :]
            d_a = d_h * _rows_back(h_prev, r0, rc, 1)
            d_mult = d_h * ia * xl
            d_ia = d_h * mult * xl
            d_xl = d_h * mult * ia
            d_la = d_a * a + d_mult * jnp.where(first, 0.0, -a_sq / mult)
            d_ls = d_ls + jnp.sum(d_la * ra, axis=0, keepdims=True) * LRU_C
            d_za = d_la * (LRU_C * ls) * ra * (1.0 - ra)
            d_zx = d_ia * ia * (1.0 - ia)
            d_ba = d_ba + jnp.sum(d_za, axis=0, keepdims=True)
            d_bx = d_bx + jnp.sum(d_zx, axis=0, keepdims=True)
            d_za, d_zx = d_za.astype(BF16), d_zx.astype(BF16)
            acc_a[...] += _dot_tn(xb, d_za)
            acc_x[...] += _dot_tn(xb, d_zx)
            back[rows, :] = d_xl + _dot_nt(d_za, wa_m) + _dot_nt(d_zx, wx_m)
        taps = [jnp.zeros((1, CB), F32)] * 4
        d_bc = jnp.zeros((1, CB), F32)
        for r0 in range(0, t, rc):
            rows = pl.ds(r0, rc)
            d_ref[0, rows, :] = _conv_anticausal(back, wc, r0, rc, 4).astype(BF16)
            g = back[rows, :]
            taps = [acc + new for acc, new in zip(taps, _conv_wgrad(g, pad, r0, rc, 4))]
            d_bc = d_bc + jnp.sum(g, axis=0, keepdims=True)
        d_lam = d_ls * jax.nn.sigmoid(-lam_ref[...])
        small_ref[...] = jnp.concatenate(taps + [d_bc, d_ba, d_bx, d_lam], axis=0)
        dwa_ref[...] = acc_a[...].reshape(N_DEV, HEAD_DIM // N_DEV, HEAD_DIM).astype(BF16)
        dwx_ref[...] = acc_x[...].reshape(N_DEV, HEAD_DIM // N_DEV, HEAD_DIM).astype(BF16)

    blk = pl.BlockSpec((t, CB), lambda h: (0, h))
    gate_grad = jax.ShapeDtypeStruct((N_DEV, N_HEADS, HEAD_DIM // N_DEV, HEAD_DIM), BF16)
    return pl.pallas_call(
        body, name="lru_bwd", grid=(N_HEADS,),
        out_shape=[jax.ShapeDtypeStruct((2, t, D_MODEL), BF16), gate_grad, gate_grad,
                   jax.ShapeDtypeStruct((LRU_SMALL_ROWS, D_MODEL), F32)],
        in_specs=[_section(3, t), _section(4, t), blk, blk, pl.BlockSpec((3, t, CB), lambda h: (0, 0, h)), blk,
                  pl.BlockSpec((4, CB), lambda h: (0, h)), mat, mat, vec],
        out_specs=[pl.BlockSpec((2, t, CB), lambda h: (0, 0, h)), mat, mat,
                   pl.BlockSpec((LRU_SMALL_ROWS, CB), lambda h: (0, h))],
        scratch_shapes=[pltpu.VMEM((t + PAD, CB), F32), pltpu.VMEM((t + PAD, CB), F32), pltpu.VMEM((t, CB), F32),
                        pltpu.VMEM((t + PAD, CB), F32), pltpu.VMEM((t + PAD, CB), F32),
                        pltpu.VMEM((HEAD_DIM, HEAD_DIM), F32), pltpu.VMEM((HEAD_DIM, HEAD_DIM), F32)],
        compiler_params=_params("parallel"),
    )(proj, proj, hl, a_all, kept, d_yb, w_conv, wa, wx, lam)


def _stack_maps(halves):
    def conv(sec, part):
        return jnp.minimum(sec, 2), jnp.where(sec < 3, part, halves - 1)

    def lru(sec, part):
        return jnp.clip(sec - 3, 0, 1), jnp.where(sec < 3, 0, jnp.where(sec < 5, part, halves - 1))

    def gate(sec, part):
        return jnp.clip(sec - 5, 0, 1), jnp.where(sec < 5, 0, part)

    return conv, lru, gate


def _pick_stack(sec, refs, fn):
    @pl.when(sec < 3)
    def _():
        fn(refs[0])

    @pl.when((sec >= 3) & (sec < 5))
    def _():
        fn(refs[1])

    @pl.when(sec >= 5)
    def _():
        fn(refs[2])


def _in_proj_wgrad(h, d_conv, d_lru, d_gate):
    t = h.shape[0]
    halves, bn = 1, D_MODEL
    maps = _stack_maps(halves)

    def body(h_ref, dc_ref, dl_ref, dg_ref, o_ref):
        def emit(ref):
            o_ref[...] = _dot_tn(h_ref[...], ref[...]).astype(BF16)
        _pick_stack(pl.program_id(0) // halves, (dc_ref, dl_ref, dg_ref), emit)

    def spec(m):
        def index(s):
            stack, part = m(s // halves, s % halves)
            return stack, 0, part
        return pl.BlockSpec((None, t, bn), index)

    return pl.pallas_call(
        body, name="in_proj_wgrad", grid=(7 * halves,), out_shape=jax.ShapeDtypeStruct((D_MODEL, IN_COLS), BF16),
        in_specs=[pl.BlockSpec((t, D_MODEL), lambda s: (0, 0))] + [spec(m) for m in maps],
        out_specs=pl.BlockSpec((D_MODEL, bn), lambda s: (0, s)),
        compiler_params=_params("arbitrary"),
    )(h, d_conv, d_lru, d_gate)


def _in_proj_xgrad(d_conv, d_lru, d_gate, w_in, x, dx1, g1):
    t = x.shape[0]
    tm = min(1024, t)
    maps = _stack_maps(1)

    def body(dc_ref, dl_ref, dg_ref, w_ref, x_ref, dx1_ref, g_ref, dx_ref, dgain_ref, acc):
        i, s = pl.program_id(0), pl.program_id(1)

        @pl.when((i == 0) & (s == 0))
        def _():
            dgain_ref[...] = jnp.zeros_like(dgain_ref)

        @pl.when(s == 0)
        def _():
            acc[...] = jnp.zeros_like(acc)

        def add(ref):
            acc[...] += _dot_nt(ref[...], w_ref[...])
        _pick_stack(s, (dc_ref, dl_ref, dg_ref), add)

        @pl.when(s == 6)
        def _():
            n1, r1 = _rms_fwd(x_ref[...])
            d_h = acc[...]
            dgain_ref[...] += jnp.sum(d_h * n1, axis=0, keepdims=True)
            dx_ref[...] = dx1_ref[...] + _rms_bwd(n1, r1, d_h * g_ref[...])

    def spec(m):
        def index(i, s):
            return m(s, 0)[0], i, 0
        return pl.BlockSpec((None, tm, D_MODEL), index)

    row = pl.BlockSpec((tm, D_MODEL), lambda i, s: (i, 0))
    vec = pl.BlockSpec((1, D_MODEL), lambda i, s: (0, 0))
    return pl.pallas_call(
        body, name="in_proj_xgrad", grid=(t // tm, 7),
        out_shape=[jax.ShapeDtypeStruct((t, D_MODEL), F32), jax.ShapeDtypeStruct((1, D_MODEL), F32)],
        in_specs=[spec(m) for m in maps] + [pl.BlockSpec((D_MODEL, D_MODEL), lambda i, s: (0, s)), row, row, vec],
        out_specs=[row, vec],
        scratch_shapes=[pltpu.VMEM((tm, D_MODEL), F32)],
        compiler_params=_params("arbitrary", "arbitrary"),
    )(d_conv, d_lru, d_gate, w_in, x, dx1, g1)


def _adamw(w, g, m, v):
    m = ADAM_B1 * m + (1.0 - ADAM_B1) * g
    v = ADAM_B2 * v + (1.0 - ADAM_B2) * (g * g)
    m_hat = m / (1.0 - ADAM_B1 ** ADAM_STEP)
    v_hat = v / (1.0 - ADAM_B2 ** ADAM_STEP)
    return -ADAM_LR * (m_hat / (jnp.sqrt(v_hat) + ADAM_EPS) + ADAM_WD * w), m, v


def _adam_large(w, m, v, own, others, name):
    shape = w.shape
    cols = shape[-1]
    w2, m2, v2 = (a.reshape(-1, cols) for a in (w, m, v))
    rows = w2.shape[0]
    own, others = own.reshape(4, rows, cols), others.reshape(3, rows, cols)
    rb = _row_block(rows, 512)

    def body(w_ref, m_ref, v_ref, own_ref, oth_ref, g_ref, d_ref, nm_ref, nv_ref):
        g = own_ref[...].astype(F32)
        for k in range(3):
            g = g + oth_ref[k].astype(F32)
        g_ref[...] = g
        d_ref[...], nm_ref[...], nv_ref[...] = _adamw(w_ref[...], g, m_ref[...], v_ref[...])

    blk = pl.BlockSpec((rb, cols), lambda i: (i, 0))
    res = jax.ShapeDtypeStruct((rows, cols), F32)
    outs = pl.pallas_call(
        body, name=name, grid=(rows // rb,), out_shape=[res] * 4,
        in_specs=[blk, blk, blk, pl.BlockSpec((None, rb, cols), lambda i: (0, i, 0)),
                  pl.BlockSpec((3, rb, cols), lambda i: (0, i, 0))],
        out_specs=[blk] * 4, compiler_params=_params("parallel"),
    )(w2, m2, v2, own, others)
    return [o.reshape(shape) for o in outs]


def _adam_small(ws, gs, ms, vs):
    n = len(ws)

    def body(*refs):
        w_refs, g_refs, m_refs, v_refs = (refs[i * n:(i + 1) * n] for i in range(4))
        outs = refs[4 * n:]
        for i in range(n):
            d, m, v = _adamw(w_refs[i][...], g_refs[i][...], m_refs[i][...], v_refs[i][...])
            outs[i][...], outs[n + i][...], outs[2 * n + i][...] = d, m, v

    shapes = [jax.ShapeDtypeStruct(w.shape, F32) for w in ws]
    outs = pl.pallas_call(
        body, name="adam_small", out_shape=shapes * 3,
        in_specs=[VMEM_SPEC] * (4 * n), out_specs=[VMEM_SPEC] * (3 * n), compiler_params=_params(),
    )(*ws, *gs, *ms, *vs)
    return outs[:n], outs[n:2 * n], outs[2 * n:]


def _pack_rows(pieces):
    tile = SUBLANES * LANES
    return jnp.concatenate([jnp.pad(p.reshape(-1), (0, (-p.size) % tile)).reshape(-1, LANES) for p in pieces], axis=0)


def _packed_starts(sizes):
    tile = SUBLANES * LANES
    starts = [0]
    for s in sizes:
        starts.append(starts[-1] + (s + tile - 1) // tile * SUBLANES)
    return starts


def kernel(x, norm_mix_pre, norm_mix_post, norm_ffn_pre, norm_ffn_post, w_in, conv_short_w, w_conv_branch, lru_conv_w, lru_conv_b, lru_wa, lru_ba, lru_wx, lru_bx, lru_lambda, w_lru_branch, w_out, ffn_w_up, ffn_conv_w, ffn_conv_b, ffn_w_down, loss_target, m_norm_mix_pre, m_norm_mix_post, m_norm_ffn_pre, m_norm_ffn_post, m_w_in, m_conv_short_w, m_w_conv_branch, m_lru_conv_w, m_lru_conv_b, m_lru_wa, m_lru_ba, m_lru_wx, m_lru_bx, m_lru_lambda, m_w_lru_branch, m_w_out, m_ffn_w_up, m_ffn_conv_w, m_ffn_conv_b, m_ffn_w_down, v_norm_mix_pre, v_norm_mix_post, v_norm_ffn_pre, v_norm_ffn_post, v_w_in, v_conv_short_w, v_w_conv_branch, v_lru_conv_w, v_lru_conv_b, v_lru_wa, v_lru_ba, v_lru_wx, v_lru_bx, v_lru_lambda, v_w_lru_branch, v_w_out, v_ffn_w_up, v_ffn_conv_w, v_ffn_conv_b, v_ffn_w_down):
    t = x.shape[1]
    xi, yi, ci = _position()
    me = _block_of(xi, yi, ci)
    x2, target = x[0], loss_target[0]
    shard_in, shard_up = IN_COLS // N_DEV, 2 * D_FF // N_DEV
    shard_sq, shard_down, shard_head = D_MODEL // N_DEV, D_FF // N_DEV, HEAD_DIM // N_DEV

    names = ["w_in", "lru_wa", "lru_wx", "w_conv_branch", "w_lru_branch", "w_out", "ffn_w_up", "ffn_w_down"]
    large = [w_in[0], lru_wa[0], lru_wx[0], w_conv_branch[0], w_lru_branch[0], w_out[0], ffn_w_up[0], ffn_w_down[0]]
    blocks = [_cols(shard_in), _lead, _lead, _rows(shard_sq), _rows(shard_sq), _rows(shard_sq),
              _cols(shard_up), _rows(shard_down)]
    gate_full = (N_DEV, N_HEADS, shard_head, HEAD_DIM)
    full_shapes = [(D_MODEL, IN_COLS), gate_full, gate_full, (D_MODEL, D_MODEL), (D_MODEL, D_MODEL), (D_MODEL, D_MODEL),
                   (D_MODEL, 2 * D_FF), (D_FF, D_MODEL)]
    n_now = 3
    small_sharded = [conv_short_w, lru_conv_w, lru_ba, lru_bx, ffn_conv_w]
    small_mine = _pack_rows(small_sharded)
    small_at = _packed_starts([p.size for p in small_sharded])
    *gathered, small_all, proj, h = _gather_weights(large, blocks, full_shapes, small_mine, n_now, x2, norm_mix_pre)
    g_in, g_wa, g_wx = gathered[:n_now]
    later_blocks = blocks[n_now:]
    send1, recv1, later, gather_token = _gather_start(gathered[n_now:], later_blocks, "gather_start")

    def behind(token, operand):
        return operand + token[0:1, 0:1]

    def forward(lo, hi, after, tag):
        return _gather_forward(later[lo:hi], later_blocks[lo:hi], send1[4 * lo:4 * hi], recv1[4 * lo:4 * hi], after,
                               "gather_forward_" + tag)

    def finish(lo, hi, flight, after, tag):
        return _gather_finish(flight[2], later_blocks[lo:hi], flight[0], flight[1], after, "gather_finish_" + tag)

    def cols_of(r0, n, width):
        part = small_all[:, r0:r0 + n * width // LANES, :].reshape(N_DEV, n, width)
        return part.transpose(1, 0, 2).reshape(n, N_DEV * width)

    c_short = cols_of(small_at[0], 3, LANES)
    c_lru = cols_of(small_at[1], 4, LANES)
    b_a = cols_of(small_at[2], N_HEADS, shard_head).reshape(1, D_MODEL)
    b_x = cols_of(small_at[3], N_HEADS, shard_head).reshape(1, D_MODEL)
    c_ffn = cols_of(small_at[4], 3, shard_up)

    y_a = _conv_mixer_fwd(proj, behind(gather_token, c_short))
    y_b, hl, decay, lru_kept = _lru_fwd(proj, behind(gather_token, c_lru), lru_conv_b, g_wa, b_a, g_wx, b_x, lru_lambda)
    flight_mix_w = forward(0, 3, y_b, "mix")
    g_cb, g_lb, g_out = finish(0, 3, flight_mix_w, y_b, "mix")
    pa, pb, merged, mix, x1, h2 = _merge(y_a, y_b, proj, x2, g_cb, g_lb, g_out, norm_mix_post, norm_ffn_pre)
    flight_up_w = forward(3, 4, h2, "up")
    (g_up,) = finish(3, 4, flight_up_w, h2, "up")
    up, act, f = _ffn_up(h2, g_up, c_ffn, ffn_conv_b)
    flight_down_w = forward(4, 5, f, "down")
    (g_down,) = finish(4, 5, flight_down_w, f, "down")
    dy, d_out, d_act, dg4, loss_part = _ffn_down(f, act, g_down, x1, target, norm_ffn_post)

    block_of = dict(zip(names, blocks))
    shard_shapes = {"w_in": (D_MODEL, shard_in), "w_conv_branch": (shard_sq, D_MODEL), "w_lru_branch": (shard_sq, D_MODEL),
                    "w_out": (shard_sq, D_MODEL), "lru_wa": (N_HEADS, shard_head, HEAD_DIM),
                    "lru_wx": (N_HEADS, shard_head, HEAD_DIM), "ffn_w_up": (D_MODEL, shard_up),
                    "ffn_w_down": (shard_down, D_MODEL)}

    def reduce_start(tag, grads):
        keys = list(grads)
        sums = _reduce_pair([grads[k] for k in keys], [block_of[k] for k in keys], [shard_shapes[k] for k in keys],
                            "reduce_pair_" + tag)
        return (keys,) + _exchange_chips_start(sums, "reduce_chip_start_" + tag)

    gw_down = _grad_tn(f, d_out, min(512, D_FF), "ffn_down_wgrad")
    flight_down = reduce_start("down", {"ffn_w_down": gw_down})
    gw_up, gc_ffn, gb_ffn, d_h2 = _ffn_up_bwd(up, d_act, behind(flight_down[-1], c_ffn), h2, g_up)
    flight_up = reduce_start("up", {"ffn_w_up": gw_up})
    dx1, d_mix, d_pa, d_pb, d_ya, d_yb, d_gate, dg3, dg2 = _merge_bwd(
        dy, d_h2, x1, mix, behind(flight_up[-1], norm_ffn_pre), norm_mix_post, g_out, g_cb, g_lb, pa, pb, proj)
    gw_out = _grad_tn(merged, d_mix, CB, "w_out_wgrad")
    gw_cb = _grad_tn(y_a, d_pa, CB, "w_conv_branch_wgrad")
    gw_lb = _grad_tn(y_b, d_pb, CB, "w_lru_branch_wgrad")
    flight_mix = reduce_start("mix", {"w_conv_branch": gw_cb, "w_lru_branch": gw_lb, "w_out": gw_out})
    d_conv, gc_short = _conv_mixer_bwd(proj, d_ya, behind(flight_mix[-1], c_short))
    d_lru, gw_a, gw_x, g_lru_small = _lru_bwd(proj, hl, decay, lru_kept, d_yb, c_lru, g_wa, g_wx, lru_lambda)
    early = [dg2, dg3, dg4, g_lru_small[4:5], g_lru_small[7:8], gb_ffn, gc_short, g_lru_small[0:4],
             g_lru_small[5:6], g_lru_small[6:7], gc_ffn, loss_part]
    flight_small = _small_start(_pack_rows(early), "small_start")
    gw_in = _in_proj_wgrad(h, d_conv, d_lru, d_gate)
    flight_in = reduce_start("in", {"lru_wa": gw_a, "lru_wx": gw_x, "w_in": gw_in})
    dx, dg1 = _in_proj_xgrad(d_conv, d_lru, d_gate, g_in, x2, dx1,
                             behind(flight_small[-1], behind(flight_in[-1], norm_mix_pre)))
    flight_late = _small_start(_pack_rows([dg1]), "small_start_late")

    moments ={"w_in": (m_w_in, v_w_in), "w_conv_branch": (m_w_conv_branch, v_w_conv_branch),
               "w_lru_branch": (m_w_lru_branch, v_w_lru_branch), "w_out": (m_w_out, v_w_out),
               "lru_wa": (m_lru_wa, v_lru_wa), "lru_wx": (m_lru_wx, v_lru_wx), "ffn_w_up": (m_ffn_w_up, v_ffn_w_up),
               "ffn_w_down": (m_ffn_w_down, v_ffn_w_down)}
    weights = {"w_in": w_in, "w_conv_branch": w_conv_branch, "w_lru_branch": w_lru_branch, "w_out": w_out,
               "lru_wa": lru_wa, "lru_wx": lru_wx, "ffn_w_up": ffn_w_up, "ffn_w_down": ffn_w_down}
    out_g, out_d, out_m, out_v = {}, {}, {}, {}

    after = flight_late[-1]
    for tag, (keys, send, recv, sums, lands, _) in (("down", flight_down), ("up", flight_up), ("mix", flight_mix),
                                                    ("in", flight_in)):
        sums, others = _exchange_chips_wait(send, recv, sums, lands, after, "reduce_chip_wait_" + tag)
        for k, own, oth in zip(keys, sums, others):
            out_g[k], out_d[k], out_m[k], out_v[k] = _adam_large(weights[k], *moments[k], own, oth, "adam_" + k)
        after = out_d[keys[-1]]

    total, total_late = _small_sum([_small_wait(*flight_small[:4], after, "small_wait"),
                                    _small_wait(*flight_late[:4], after, "small_wait_late")], me)
    sizes = [p.size for p in early]
    starts = _packed_starts(sizes)

    def piece(i, shape):
        if i == 0:
            return total_late.reshape(-1)[:D_MODEL].reshape(shape)
        return total[starts[i - 1]:starts[i]].reshape(-1)[:sizes[i - 1]].reshape(shape)

    loss = total[starts[11], 0]

    def col_shard(full, width):
        return lax.dynamic_slice_in_dim(full, me * width, width, axis=1)

    def head_shard(full):
        return lax.dynamic_slice_in_dim(full.reshape(N_HEADS, HEAD_DIM), me * shard_head, shard_head, axis=1)

    small_names = ["norm_mix_pre", "norm_mix_post", "norm_ffn_pre", "norm_ffn_post", "lru_conv_b", "lru_lambda",
                   "ffn_conv_b", "conv_short_w", "lru_conv_w", "lru_ba", "lru_bx", "ffn_conv_w"]
    small_g = [piece(0, (1, D_MODEL)), piece(1, (1, D_MODEL)), piece(2, (1, D_MODEL)), piece(3, (1, D_MODEL)),
               piece(4, (1, D_MODEL)), piece(5, (1, D_MODEL)), piece(6, (1, 2 * D_FF)),
               col_shard(piece(7, (3, D_MODEL)), LANES), col_shard(piece(8, (4, D_MODEL)), LANES),
               head_shard(piece(9, (1, D_MODEL))), head_shard(piece(10, (1, D_MODEL))),
               col_shard(piece(11, (3, 2 * D_FF)), shard_up)]
    small_w = [norm_mix_pre, norm_mix_post, norm_ffn_pre, norm_ffn_post, lru_conv_b, lru_lambda, ffn_conv_b,
               conv_short_w[0], lru_conv_w[0], lru_ba[0], lru_bx[0], ffn_conv_w[0]]
    small_m = [m_norm_mix_pre, m_norm_mix_post, m_norm_ffn_pre, m_norm_ffn_post, m_lru_conv_b, m_lru_lambda,
               m_ffn_conv_b, m_conv_short_w[0], m_lru_conv_w[0], m_lru_ba[0], m_lru_bx[0], m_ffn_conv_w[0]]
    small_v = [v_norm_mix_pre, v_norm_mix_post, v_norm_ffn_pre, v_norm_ffn_post, v_lru_conv_b, v_lru_lambda,
               v_ffn_conv_b, v_conv_short_w[0], v_lru_conv_w[0], v_lru_ba[0], v_lru_bx[0], v_ffn_conv_w[0]]
    s_d, s_m, s_v = _adam_small(small_w, small_g, small_m, small_v)
    for i, name in enumerate(small_names):
        shape = small_w[i].shape if i < 7 else (1,) + small_w[i].shape
        out_g[name] = small_g[i].reshape(shape)
        out_d[name], out_m[name], out_v[name] = s_d[i].reshape(shape), s_m[i].reshape(shape), s_v[i].reshape(shape)

    order = ["norm_mix_pre", "norm_mix_post", "norm_ffn_pre", "norm_ffn_post", "w_in", "conv_short_w", "w_conv_branch",
             "lru_conv_w", "lru_conv_b", "lru_wa", "lru_ba", "lru_wx", "lru_bx", "lru_lambda", "w_lru_branch", "w_out",
             "ffn_w_up", "ffn_conv_w", "ffn_conv_b", "ffn_w_down"]
    return (loss, dx.reshape(1, t, D_MODEL), *[out_g[k] for k in order], *[out_d[k] for k in order],
            *[out_m[k] for k in order], *[out_v[k] for k in order])
```

```python
import functools
import math

import jax
import jax.numpy as jnp
from jax import lax
from jax.experimental import pallas as pl
from jax.experimental.pallas import tpu as pltpu

F32 = jnp.float32
BF16 = jnp.bfloat16
MESH = pl.DeviceIdType.MESH

N_DEV = 8
D_MODEL = 1024
N_HEADS = 4
HEAD_DIM = D_MODEL // N_HEADS
D_FF = 3 * D_MODEL
IN_COLS = 7 * D_MODEL
LRU_C = 8.0
RMS_EPS = 1e-6
ADAM_LR = 0.001
ADAM_B1 = 0.9
ADAM_B2 = 0.999
ADAM_EPS = 1e-08
ADAM_WD = 0.01
ADAM_STEP = 10
GELU_K = math.sqrt(2.0 / math.pi)
GELU_C = 0.044715

LANES = 128
SUBLANES = 8
PAD = SUBLANES
VMEM_LIMIT = 56 * 1024 * 1024
CB = 256
ROW_SLICE = 32

HBM_SPEC = pl.BlockSpec(memory_space=pltpu.HBM)
SEM_SPEC = pl.BlockSpec(memory_space=pltpu.SEMAPHORE)
DATAFLOW_EFFECT = pltpu.SideEffectType.DATAFLOW_SIDE_EFFECTING
VMEM_SPEC = pl.BlockSpec(memory_space=pltpu.VMEM)


def _params(*sem):
    if sem:
        return pltpu.CompilerParams(dimension_semantics=sem, vmem_limit_bytes=VMEM_LIMIT)
    return pltpu.CompilerParams(vmem_limit_bytes=VMEM_LIMIT)


def _row_chunk(t):
    return min(256, t)


def _row_block(rows, cap):
    return next(rb for rb in range(min(cap, rows), 0, -16) if rows % rb == 0)


def _gelu(x):
    return 0.5 * x * (1.0 + jnp.tanh(GELU_K * (x + GELU_C * x * x * x)))


def _gelu_and_grad(x):
    t = jnp.tanh(GELU_K * (x + GELU_C * x * x * x))
    g = 0.5 * x * (1.0 + t)
    dg = 0.5 * (1.0 + t) + 0.5 * x * (1.0 - t * t) * GELU_K * (1.0 + 3.0 * GELU_C * x * x)
    return g, dg


def _expm1_neg(x):
    series = x * (1.0 + x * (0.5 + x * (1.0 / 6.0 + x * (1.0 / 24.0 + x * (1.0 / 120.0)))))
    return jnp.where(x > -0.05, series, jnp.exp(x) - 1.0)


def _log_sigmoid(x):
    return jnp.minimum(x, 0.0) - jnp.log1p(jnp.exp(-jnp.abs(x)))


def _dot(a, b):
    return jnp.dot(a, b, preferred_element_type=F32)


def _dot_nt(a, b):
    return lax.dot_general(a, b, (((1,), (1,)), ((), ())), preferred_element_type=F32)


def _dot_tn(a, b):
    return lax.dot_general(a, b, (((0,), (0,)), ((), ())), preferred_element_type=F32)


def _rms_fwd(x):
    r = lax.rsqrt(jnp.mean(x * x, axis=-1, keepdims=True) + RMS_EPS)
    return x * r, r


def _rms_bwd(n, r, gdy):
    return r * (gdy - n * jnp.mean(n * gdy, axis=-1, keepdims=True))


def _rows_back(pad_ref, r0, rows, j):
    cur = pad_ref[pl.ds(PAD + r0, rows), :]
    if j == 0:
        return cur
    before = pad_ref[pl.ds(PAD + r0 - SUBLANES, SUBLANES), :]
    row = lax.broadcasted_iota(jnp.int32, before.shape, 0)
    rolled = pltpu.roll(cur, j, 0)
    top = jnp.where(row < j, pltpu.roll(before, j, 0), rolled[0:SUBLANES, :])
    return jnp.concatenate([top, rolled[SUBLANES:, :]], axis=0)


def _rows_ahead(pad_ref, r0, rows, j):
    cur = pad_ref[pl.ds(r0, rows), :]
    if j == 0:
        return cur
    after = pad_ref[pl.ds(r0 + rows, SUBLANES), :]
    row = lax.broadcasted_iota(jnp.int32, after.shape, 0)
    rolled = pltpu.roll(cur, rows - j, 0)
    bottom = jnp.where(row >= SUBLANES - j, pltpu.roll(after, SUBLANES - j, 0), rolled[rows - SUBLANES:, :])
    return jnp.concatenate([rolled[:rows - SUBLANES, :], bottom], axis=0)


def _fold_rows(v):
    return v.reshape(v.shape[0] // SUBLANES, SUBLANES, v.shape[1]).sum(axis=0)


def _conv_causal(pad_ref, w, r0, rows, taps):
    acc = None
    for k in range(taps):
        term = w[k:k + 1, :] * _rows_back(pad_ref, r0, rows, taps - 1 - k)
        acc = term if acc is None else acc + term
    return acc


def _conv_anticausal(pad_ref, w, r0, rows, taps):
    acc = None
    for k in range(taps):
        term = w[k:k + 1, :] * _rows_ahead(pad_ref, r0, rows, taps - 1 - k)
        acc = term if acc is None else acc + term
    return acc


def _conv_wgrad(g, xpad_ref, r0, rows, taps):
    return [jnp.sum(g * _rows_back(xpad_ref, r0, rows, taps - 1 - k), axis=0, keepdims=True) for k in range(taps)]


def _position():
    return lax.axis_index("x"), lax.axis_index("y"), lax.axis_index("c")


def _block_of(x, y, c):
    return 4 * x + 2 * y + c


def _chip(x, y, k):
    return (x + (k & 1)) % 2, (y + (k >> 1)) % 2


def _cols(width):
    def at(ref, d, half=None):
        cols = pl.ds(pl.multiple_of(d * width, LANES), width)
        if half is None:
            return ref.at[:, cols]
        return ref.at[pl.ds(half * (ref.shape[0] // 2), ref.shape[0] // 2), cols]
    return at


def _rows(height):
    def at(ref, d, half=None):
        if half is None:
            return ref.at[pl.ds(pl.multiple_of(d * height, 16), height), :]
        return ref.at[pl.ds(pl.multiple_of(d * height + half * (height // 2), 16), height // 2), :]
    return at


def _lead(ref, d, half=None):
    if half is None:
        return ref.at[d]
    return ref.at[d, pl.ds(half * (ref.shape[1] // 2), ref.shape[1] // 2)]


def _gather_weights(shards, blocks, full_shapes, small, n_now, tokens, gain):
    n = len(shards)
    small_rows = small.shape[0]
    t = tokens.shape[0]
    rc = min(512, t)

    def body(*refs):
        ins, small_in, x_ref, g_ref = refs[:n], refs[n], refs[n + 1], refs[n + 2]
        outs, small_out, proj_ref, h_ref = refs[n + 3:2 * n + 3], refs[2 * n + 3], refs[2 * n + 4], refs[2 * n + 5]
        stage = refs[2 * n + 6:3 * n + 6]
        w_buf, p_buf, send, recv, local, w_sem, p_sem = refs[3 * n + 6:]
        x, y, c = _position()
        me = _block_of(x, y, c)
        sibling = (x, y, 1 - c)

        for a in range(n):
            stage[a][...] = ins[a][...].astype(BF16)
        for r0 in range(0, t, rc):
            normed, _ = _rms_fwd(x_ref[pl.ds(r0, rc), :])
            h_ref[pl.ds(r0, rc), :] = (normed * g_ref[...]).astype(BF16)
        stores = []

        def project(w_ref, block):
            i = len(stores)
            if i >= 2:
                stores[i - 2].wait()
            for r0 in range(0, t, rc):
                p_buf[i % 2, pl.ds(r0, rc), :] = _dot(h_ref[pl.ds(r0, rc), :], w_ref[...]).astype(BF16)
            st = pltpu.make_async_copy(p_buf.at[i % 2], blocks[0](proj_ref, block), p_sem.at[i % 2])
            st.start()
            stores.append(st)

        def project_landed(block):
            ld = pltpu.make_async_copy(blocks[0](outs[0], block), w_buf, w_sem)
            ld.start()
            ld.wait()
            project(w_buf, block)

        def copy(a, k, block, to, src=None, half=None):
            dst = blocks[a](outs[a], block, half)
            return pltpu.make_async_remote_copy(
                src_ref=dst if src is None else src, dst_ref=dst, send_sem=send.at[a, k], recv_sem=recv.at[a, k],
                device_id=to, device_id_type=MESH)

        def small_copy(k):
            px, py, pc = (x + (k & 1)) % 2, (y + ((k >> 1) & 1)) % 2, (c + (k >> 2)) % 2
            return pltpu.make_async_remote_copy(
                src_ref=small_in, dst_ref=small_out.at[me], send_sem=send.at[n_now, k - 1], recv_sem=recv.at[n_now, k - 1],
                device_id=(px, py, pc), device_id_type=MESH)

        def small_arrival(k):
            px, py, pc = (x + (k & 1)) % 2, (y + ((k >> 1) & 1)) % 2, (c + (k >> 2)) % 2
            return pltpu.make_async_remote_copy(
                src_ref=small_in, dst_ref=small_out.at[_block_of(px, py, pc)], send_sem=send.at[n_now, k - 1],
                recv_sem=recv.at[n_now, k - 1], device_id=(px, py, pc), device_id_type=MESH)

        small_out[me] = small_in[...]
        small_sends = [small_copy(k) for k in range(1, N_DEV)]
        for cp in small_sends:
            cp.start()

        mine, first, passed = [], [], []
        for a in range(n):
            own = pltpu.make_async_copy(stage[a], blocks[a](outs[a], me), local.at[a])
            own.start()
            mine.append(own)
            if a >= n_now:
                continue
            sends = [copy(a, 0, me, sibling, src=stage[a])]
            sends += [copy(a, k, me, (*_chip(x, y, k), c), src=stage[a]) for k in (1, 2)]
            for cp in sends:
                cp.start()
            first += sends

        here = (x, y, c)
        across = [(*_chip(x, y, k), c) for k in (1, 2)]
        near = [[_block_of(*_chip(x, y, k), cc) for k in (1, 2)] for cc in (c, 1 - c)]
        far = [_block_of(*_chip(x, y, 3), cc) for cc in (c, 1 - c)]

        def launch(cp):
            cp.start()
            passed.append(cp)

        project(stage[0], me)
        copy(0, 0, _block_of(x, y, 1 - c), here).wait_recv()
        project_landed(_block_of(x, y, 1 - c))
        for a in range(n_now):
            for i in (0, 1):
                copy(a, 1 + i, near[0][i], here).wait_recv()
                launch(copy(a, 3 + i, near[0][i], across[1 - i], half=i))
                launch(copy(a, 5 + i, near[0][i], sibling))
            if a == 0:
                project_landed(near[0][0])
                project_landed(near[0][1])
        for i in (0, 1):
            copy(0, 5 + i, near[1][i], here).wait_recv()
            project_landed(near[1][i])
        for a in range(n_now):
            for i in (0, 1):
                copy(a, 3 + i, far[0], here, half=i).wait_recv()
                launch(copy(a, 7 + i, far[0], sibling, half=i))
            if a == 0:
                project_landed(far[0])
        for a in range(n_now):
            if a > 0:
                copy(a, 0, _block_of(x, y, 1 - c), here).wait_recv()
                for i in (0, 1):
                    copy(a, 5 + i, near[1][i], here).wait_recv()
            for i in (0, 1):
                copy(a, 7 + i, far[1], here, half=i).wait_recv()
            if a == 0:
                project_landed(far[1])
        for k in range(1, N_DEV):
            small_arrival(k).wait_recv()
        for cp in first + passed + small_sends:
            cp.wait_send()
        for done in mine + stores[-2:]:
            done.wait()

    out_shape = [jax.ShapeDtypeStruct(s, BF16) for s in full_shapes]
    out_shape += [jax.ShapeDtypeStruct((N_DEV, small_rows, LANES), F32), jax.ShapeDtypeStruct((t, full_shapes[0][1]), BF16),
                  jax.ShapeDtypeStruct(tokens.shape, BF16)]
    return pl.pallas_call(
        body, name="gather_weights", out_shape=out_shape,
        in_specs=[VMEM_SPEC] * (n + 3), out_specs=[HBM_SPEC] * n + [VMEM_SPEC, HBM_SPEC, VMEM_SPEC],
        scratch_shapes=[pltpu.VMEM(s.shape, BF16) for s in shards]
        + [pltpu.VMEM(shards[0].shape, BF16), pltpu.VMEM((2, t, shards[0].shape[1]), BF16),
           pltpu.SemaphoreType.DMA((n_now + 1, 9)), pltpu.SemaphoreType.DMA((n_now + 1, 9)),
           pltpu.SemaphoreType.DMA((n,)), pltpu.SemaphoreType.DMA(()), pltpu.SemaphoreType.DMA((2,))],
        compiler_params=_params(),
    )(*shards, small, tokens, gain)


def _gather_first(full, blocks, send, recv):
    x, y, c = _position()
    me = _block_of(x, y, c)
    peers = [(x, y, 1 - c)] + [(*_chip(x, y, k), c) for k in (1, 2, 3)]

    def copy(a, k, block):
        at = blocks[a](full[a], block)
        return pltpu.make_async_remote_copy(src_ref=at, dst_ref=at, send_sem=send[4 * a + k], recv_sem=recv[4 * a + k],
                                            device_id=peers[k], device_id_type=MESH)

    sends = [copy(a, k, me) for a in range(len(full)) for k in range(4)]
    arrivals = [copy(a, k, _block_of(*peers[k])) for a in range(len(full)) for k in range(4)]
    return sends, arrivals


def _gather_second(full, blocks, send, recv):
    x, y, c = _position()

    def copy(a, k, cc):
        at = blocks[a](full[a], _block_of(*_chip(x, y, k), cc))
        return pltpu.make_async_remote_copy(src_ref=at, dst_ref=at, send_sem=send[3 * a + k - 1],
                                            recv_sem=recv[3 * a + k - 1], device_id=(x, y, 1 - c), device_id_type=MESH)

    sends = [copy(a, k, c) for a in range(len(full)) for k in (1, 2, 3)]
    arrivals = [copy(a, k, 1 - c) for a in range(len(full)) for k in (1, 2, 3)]
    return sends, arrivals


def _split_call(body, name, arrays, sems_in, n_sems_out, after=None, token=False):
    n, m = len(arrays), len(sems_in)

    def kernel_body(*refs):
        outs = refs[n + m + (after is not None):]
        body(refs[:n], refs[n:n + m], outs[:n_sems_out])
        if token:
            outs[-1][...] = jnp.zeros_like(outs[-1])

    extra_in = [] if after is None else [after]
    outs = pl.pallas_call(
        kernel_body, name=name,
        out_shape=(*[pltpu.SemaphoreType.DMA(())] * n_sems_out, *[pltpu.HBM(a.shape, a.dtype) for a in arrays],
                   *([jax.ShapeDtypeStruct((SUBLANES, LANES), F32)] if token else [])),
        in_specs=[HBM_SPEC] * n + [SEM_SPEC] * m + [pl.BlockSpec(memory_space=pl.ANY)] * len(extra_in),
        out_specs=(*[SEM_SPEC] * n_sems_out, *[HBM_SPEC] * n, *([VMEM_SPEC] if token else [])),
        input_output_aliases={i: n_sems_out + i for i in range(n)},
        compiler_params=pltpu.CompilerParams(has_side_effects=DATAFLOW_EFFECT),
    )(*[pltpu.with_memory_space_constraint(a, pltpu.HBM) for a in arrays], *sems_in, *extra_in)
    sems, rest = list(outs[:n_sems_out]), list(outs[n_sems_out:])
    return (sems, rest[:n], rest[n]) if token else (sems, rest[:n])


def _gather_start(full, blocks, name):
    n = len(full)

    def body(arrays, _, sems):
        for cp in _gather_first(arrays, blocks, sems[:4 * n], sems[4 * n:])[0]:
            cp.start()

    sems, arrays, token = _split_call(body, name, full, [], 8 * n, token=True)
    return sems[:4 * n], sems[4 * n:], arrays, token


def _gather_forward(full, blocks, send_first, recv_first, after, name):
    n = len(full)

    def body(arrays, sems_in, sems):
        sends, arrivals = _gather_first(arrays, blocks, sems_in[:4 * n], sems_in[4 * n:])
        for cp in arrivals:
            cp.wait_recv()
        for cp in _gather_second(arrays, blocks, sems[:3 * n], sems[3 * n:])[0]:
            cp.start()
        for cp in sends:
            cp.wait_send()

    sems, arrays = _split_call(body, name, full, [*send_first, *recv_first], 6 * n, after=after)
    return sems[:3 * n], sems[3 * n:], arrays


def _gather_finish(full, blocks, send_second, recv_second, after, name):
    n = len(full)

    def body(arrays, sems_in, _):
        sends, arrivals = _gather_second(arrays, blocks, sems_in[:3 * n], sems_in[3 * n:])
        for cp in sends:
            cp.wait_send()
        for cp in arrivals:
            cp.wait_recv()

    return _split_call(body, name, full, [*send_second, *recv_second], 0, after=after)[1]


def _reduce_pair(grads, blocks, shard_shapes, name):
    n = len(grads)

    def body(*refs):
        ins, outs = refs[:n], refs[n:2 * n]
        got, own = refs[2 * n:3 * n], refs[3 * n:4 * n]
        send, recv, local = refs[4 * n:]
        x, y, c = _position()
        copies, loads = [], []
        for a in range(n):
            for k in range(4):
                chip = _chip(x, y, k)
                cp = pltpu.make_async_remote_copy(
                    src_ref=blocks[a](ins[a], _block_of(*chip, 1 - c)), dst_ref=got[a].at[k],
                    send_sem=send.at[a, k], recv_sem=recv.at[a, k], device_id=(x, y, 1 - c), device_id_type=MESH)
                cp.start()
                copies.append(cp)
                ld = pltpu.make_async_copy(blocks[a](ins[a], _block_of(*chip, c)), own[a].at[k], local.at[a, k])
                ld.start()
                loads.append(ld)
        for a in range(n):
            for k in range(4):
                loads[4 * a + k].wait()
                copies[4 * a + k].wait_recv()
                outs[a][k] = (own[a][k].astype(F32) + got[a][k].astype(F32)).astype(BF16)
        for cp in copies:
            cp.wait_send()

    slots = [(4,) + tuple(s) for s in shard_shapes]
    return pl.pallas_call(
        body, name=name, out_shape=[jax.ShapeDtypeStruct(s, BF16) for s in slots],
        in_specs=[HBM_SPEC] * n, out_specs=[VMEM_SPEC] * n,
        scratch_shapes=[pltpu.VMEM(s, BF16) for s in slots] * 2
        + [pltpu.SemaphoreType.DMA((n, 4)), pltpu.SemaphoreType.DMA((n, 4)), pltpu.SemaphoreType.DMA((n, 4))],
        compiler_params=_params(),
    )(*grads)


def _chip_copies(sums, lands, send, recv):
    x, y, c = _position()
    return [pltpu.make_async_remote_copy(
        src_ref=sums[a].at[k], dst_ref=lands[a].at[k - 1], send_sem=send[3 * a + k - 1], recv_sem=recv[3 * a + k - 1],
        device_id=(*_chip(x, y, k), c), device_id_type=MESH) for a in range(len(sums)) for k in (1, 2, 3)]


def _exchange_chips_start(pair_sums, name):
    n = len(pair_sums)
    lands = [pltpu.with_memory_space_constraint(lax.empty((3,) + tuple(p.shape[1:]), BF16), pltpu.HBM) for p in pair_sums]

    def body(*refs):
        sums, zones = refs[:n], refs[n:2 * n]
        send, recv = refs[2 * n:5 * n], refs[5 * n:8 * n]
        token = refs[-1]
        for cp in _chip_copies(sums, zones, send, recv):
            cp.start()
        token[...] = jnp.zeros_like(token)

    outs = pl.pallas_call(
        body, name=name,
        out_shape=(*[pltpu.SemaphoreType.DMA(())] * (6 * n),
                   *[pltpu.HBM(p.shape, BF16) for p in pair_sums], *[pltpu.HBM(z.shape, BF16) for z in lands],
                   jax.ShapeDtypeStruct((SUBLANES, LANES), F32)),
        in_specs=[HBM_SPEC] * (2 * n), out_specs=(*[SEM_SPEC] * (6 * n), *[HBM_SPEC] * (2 * n), VMEM_SPEC),
        input_output_aliases={i: 6 * n + i for i in range(2 * n)},
        compiler_params=pltpu.CompilerParams(has_side_effects=DATAFLOW_EFFECT),
    )(*[pltpu.with_memory_space_constraint(p, pltpu.HBM) for p in pair_sums], *lands)
    return outs[:3 * n], outs[3 * n:6 * n], outs[6 * n:7 * n], outs[7 * n:8 * n], outs[-1]


def _exchange_chips_wait(send, recv, sums, lands, after, name):
    n = len(sums)

    def body(*refs):
        sums_in, zones = refs[:n], refs[n:2 * n]
        send_in, recv_in = refs[2 * n:5 * n], refs[5 * n:8 * n]
        for cp in _chip_copies(sums_in, zones, send_in, recv_in):
            cp.wait_send()
            cp.wait_recv()

    outs = pl.pallas_call(
        body, name=name,
        out_shape=(*[pltpu.HBM(p.shape, BF16) for p in sums], *[pltpu.HBM(z.shape, BF16) for z in lands]),
        in_specs=[HBM_SPEC] * (2 * n) + [SEM_SPEC] * (6 * n) + [pl.BlockSpec(memory_space=pl.ANY)],
        out_specs=[HBM_SPEC] * (2 * n), input_output_aliases={i: i for i in range(2 * n)},
        compiler_params=pltpu.CompilerParams(has_side_effects=DATAFLOW_EFFECT),
    )(*sums, *lands, *send, *recv, after)
    return outs[:n], outs[n:]


def _small_copies(mine, land, send, recv):
    x, y, c = _position()
    me = _block_of(x, y, c)

    def peer(k):
        return (x + (k & 1)) % 2, (y + ((k >> 1) & 1)) % 2, (c + (k >> 2)) % 2

    def copy(k, slot):
        return pltpu.make_async_remote_copy(src_ref=mine, dst_ref=land.at[slot], send_sem=send[k - 1], recv_sem=recv[k - 1],
                                            device_id=peer(k), device_id_type=MESH)

    return [copy(k, me) for k in range(1, N_DEV)], [copy(k, _block_of(*peer(k))) for k in range(1, N_DEV)]


def _small_start(part, name):
    land = jnp.zeros((N_DEV,) + part.shape, F32)

    def body(arrays, _, sems):
        for cp in _small_copies(arrays[0], arrays[1], sems[:7], sems[7:])[0]:
            cp.start()

    sems, arrays, token = _split_call(body, name, [part, land], [], 14, token=True)
    return sems[:7], sems[7:], arrays[0], arrays[1], token


def _small_wait(send, recv, part, land, after, name):
    def body(arrays, sems_in, _):
        sends, arrivals = _small_copies(arrays[0], arrays[1], sems_in[:7], sems_in[7:])
        for cp in sends:
            cp.wait_send()
        for cp in arrivals:
            cp.wait_recv()

    return _split_call(body, name, [part, land], [*send, *recv], 0, after=after)[1]


def _small_sum(pairs, me):
    n = len(pairs)

    def body(me_ref, *refs):
        for i in range(n):
            mine, land, out = refs[2 * i], refs[2 * i + 1], refs[2 * n + i]
            total = jnp.zeros(mine.shape, F32)
            for d in range(N_DEV):
                total = total + land[d] + jnp.where(me_ref[0] == d, mine[...], 0.0)
            out[...] = total

    flat = [a for pair in pairs for a in pair]
    return pl.pallas_call(
        body, name="small_sum", out_shape=[jax.ShapeDtypeStruct(mine.shape, F32) for mine, _ in pairs],
        in_specs=[pl.BlockSpec(memory_space=pltpu.SMEM)] + [VMEM_SPEC] * (2 * n), out_specs=[VMEM_SPEC] * n,
        compiler_params=_params(),
    )(me.reshape(1).astype(jnp.int32), *flat)


def _section(s, t):
    return pl.BlockSpec((t, CB), lambda h, s=s: (0, s * (D_MODEL // CB) + h))


def _conv_mixer_fwd(proj, w_short):
    t = proj.shape[0]
    rc = _row_chunk(t)

    def body(b_ref, c_ref, x_ref, w_ref, y_ref, pad):
        pad[pl.ds(0, PAD), :] = jnp.zeros((PAD, CB), F32)
        for r0 in range(0, t, rc):
            rows = pl.ds(r0, rc)
            pad[pl.ds(PAD + r0, rc), :] = c_ref[rows, :].astype(F32) * x_ref[rows, :].astype(F32)
        w = w_ref[...]
        for r0 in range(0, t, rc):
            rows = pl.ds(r0, rc)
            y_ref[rows, :] = (b_ref[rows, :].astype(F32) * _conv_causal(pad, w, r0, rc, 3)).astype(BF16)

    return pl.pallas_call(
        body, name="conv_mixer_fwd", grid=(D_MODEL // CB,),
        out_shape=jax.ShapeDtypeStruct((t, D_MODEL), BF16),
        in_specs=[_section(0, t), _section(1, t), _section(2, t), pl.BlockSpec((3, CB), lambda h: (0, h))],
        out_specs=pl.BlockSpec((t, CB), lambda h: (0, h)),
        scratch_shapes=[pltpu.VMEM((t + PAD, CB), F32)],
        compiler_params=_params("parallel"),
    )(proj, proj, proj, w_short)


def _lru_gates(xl, wa, ba, wx, bx, ls, first_row):
    xb = xl.astype(BF16)
    ra = jax.nn.sigmoid(_dot(xb, wa) + ba)
    ia = jax.nn.sigmoid(_dot(xb, wx) + bx)
    la = LRU_C * ra * ls
    a = jnp.exp(la)
    one_minus = -_expm1_neg(2.0 * la)
    mult = jnp.where(first_row, 1.0, jnp.sqrt(one_minus))
    return xb, ra, ia, a, one_minus, mult


def _head_specs():
    vec = pl.BlockSpec((1, CB), lambda h: (0, h))
    mat = pl.BlockSpec((N_DEV, None, HEAD_DIM // N_DEV, HEAD_DIM), lambda h: (0, h, 0, 0))
    return vec, mat


def _lru_fwd(proj, w_conv, b_conv, wa, ba, wx, bx, lam):
    t = proj.shape[0]
    rc = _row_chunk(t)
    vec, mat = _head_specs()

    def body(lx_ref, ly_ref, wc_ref, bc_ref, wa_ref, ba_ref, wx_ref, bx_ref, lam_ref, yb_ref, hl_ref, a_ref, kept_ref,
             pad, u_s):
        pad[pl.ds(0, PAD), :] = jnp.zeros((PAD, CB), F32)
        for r0 in range(0, t, rc):
            pad[pl.ds(PAD + r0, rc), :] = lx_ref[pl.ds(r0, rc), :].astype(F32)
        wc, bc = wc_ref[...], bc_ref[...]
        wa_m, wx_m = wa_ref[...].reshape(HEAD_DIM, HEAD_DIM), wx_ref[...].reshape(HEAD_DIM, HEAD_DIM)
        ls = _log_sigmoid(lam_ref[...])
        for r0 in range(0, t, rc):
            rows = pl.ds(r0, rc)
            xl = _conv_causal(pad, wc, r0, rc, 4) + bc
            first = (lax.broadcasted_iota(jnp.int32, (rc, CB), 0) + r0) == 0
            xb, ra, ia, a, _, mult = _lru_gates(xl, wa_m, ba_ref[...], wx_m, bx_ref[...], ls, first)
            a_ref[rows, :] = a
            u_s[rows, :] = mult * (ia * xl)
            kept_ref[0, rows, :] = xb
            kept_ref[1, rows, :] = ra.astype(BF16)
            kept_ref[2, rows, :] = ia.astype(BF16)

        row = lax.broadcasted_iota(jnp.int32, (SUBLANES, CB), 0)

        def group(g, carry):
            r = pl.multiple_of(g * SUBLANES, SUBLANES)
            a_g, b_g = a_ref[pl.ds(r, SUBLANES), :], u_s[pl.ds(r, SUBLANES), :]
            for s in (1, 2, 4):
                keep = row >= s
                b_g = jnp.where(keep, a_g * pltpu.roll(b_g, s, 0) + b_g, b_g)
                a_g = jnp.where(keep, a_g * pltpu.roll(a_g, s, 0), a_g)
            h_g = b_g + a_g * carry
            hl_ref[pl.ds(r, SUBLANES), :] = h_g
            return jnp.broadcast_to(h_g[SUBLANES - 1:SUBLANES, :], (SUBLANES, CB))

        lax.fori_loop(0, t // SUBLANES, group, jnp.zeros((SUBLANES, CB), F32))
        for r0 in range(0, t, rc):
            rows = pl.ds(r0, rc)
            yb_ref[rows, :] = (hl_ref[rows, :] * _gelu(ly_ref[rows, :].astype(F32))).astype(BF16)

    blk = pl.BlockSpec((t, CB), lambda h: (0, h))
    res = jax.ShapeDtypeStruct((t, D_MODEL), F32)
    return pl.pallas_call(
        body, name="lru_fwd", grid=(N_HEADS,),
        out_shape=[jax.ShapeDtypeStruct((t, D_MODEL), BF16), res, res, jax.ShapeDtypeStruct((3, t, D_MODEL), BF16)],
        in_specs=[_section(3, t), _section(4, t), pl.BlockSpec((4, CB), lambda h: (0, h)), vec, mat, vec, mat, vec, vec],
        out_specs=[blk, blk, blk, pl.BlockSpec((3, t, CB), lambda h: (0, 0, h))],
        scratch_shapes=[pltpu.VMEM((t + PAD, CB), F32), pltpu.VMEM((t, CB), F32)],
        compiler_params=_params("parallel"),
    )(proj, proj, w_conv, b_conv, wa, ba, wx, bx, lam)


def _merge(y_a, y_b, proj, x, w_cb, w_lb, w_out, g2, g3):
    t = x.shape[0]
    tm = min(512, t)

    def body(ya_ref, yb_ref, gc_ref, gl_ref, x_ref, wcb_ref, wlb_ref, wo_ref, g2_ref, g3_ref,
             pa_ref, pb_ref, mg_ref, mix_ref, x1_ref, h2_ref):
        pa = _dot(ya_ref[...], wcb_ref[...]).astype(BF16)
        pb = _dot(yb_ref[...], wlb_ref[...]).astype(BF16)
        pa_ref[...] = pa
        pb_ref[...] = pb
        merged = (jax.nn.sigmoid(gc_ref[...].astype(F32)) * pa.astype(F32)
                  + jax.nn.sigmoid(gl_ref[...].astype(F32)) * pb.astype(F32)).astype(BF16)
        mg_ref[...] = merged
        mix = _dot(merged, wo_ref[...])
        mix_ref[...] = mix
        n2, _ = _rms_fwd(mix)
        x1 = x_ref[...] + n2 * g2_ref[...]
        x1_ref[...] = x1
        n3, _ = _rms_fwd(x1)
        h2_ref[...] = (n3 * g3_ref[...]).astype(BF16)

    row = pl.BlockSpec((tm, D_MODEL), lambda i: (i, 0))
    full = pl.BlockSpec((D_MODEL, D_MODEL), lambda i: (0, 0))
    vec = pl.BlockSpec((1, D_MODEL), lambda i: (0, 0))
    act = jax.ShapeDtypeStruct((t, D_MODEL), BF16)
    res = jax.ShapeDtypeStruct((t, D_MODEL), F32)
    return pl.pallas_call(
        body, name="merge_fwd", grid=(t // tm,), out_shape=[act, act, act, res, res, act],
        in_specs=[row, row, pl.BlockSpec((tm, D_MODEL), lambda i: (i, 5)), pl.BlockSpec((tm, D_MODEL), lambda i: (i, 6)),
                  row, full, full, full, vec, vec],
        out_specs=[row] * 6,
        compiler_params=_params("parallel"),
    )(y_a, y_b, proj, proj, x, w_cb, w_lb, w_out, g2, g3)


N_FF_BLOCKS = D_FF // CB
FFN_BWD_COLS = 512


def _ffn_up(h2, w_up, w_conv, b_conv):
    t = h2.shape[0]
    rc = _row_chunk(t)
    nb = N_FF_BLOCKS

    def body(h_ref, w_ref, c_ref, b_ref, up_ref, act_ref, f_ref, pad, gate):
        k = pl.program_id(1)
        pad[pl.ds(0, PAD), :] = jnp.zeros((PAD, CB), F32)
        for r0 in range(0, t, rc):
            rows = pl.ds(r0, rc)
            up = _dot(h_ref[rows, :], w_ref[...]).astype(BF16)
            up_ref[rows, :] = up
            pad[pl.ds(PAD + r0, rc), :] = up.astype(F32)
        cw = c_ref[...]
        for r0 in range(0, t, rc):
            rows = pl.ds(r0, rc)
            act = _conv_causal(pad, cw, r0, rc, 3) + b_ref[...]
            act_ref[rows, :] = act.astype(BF16)

            @pl.when(k == 0)
            def _():
                gate[rows, :] = act

            @pl.when(k == 1)
            def _():
                f_ref[rows, :] = (_gelu(gate[rows, :]) * act).astype(BF16)

    half = lambda rows: pl.BlockSpec((rows, CB), lambda j, k: (0, nb * k + j))
    wide = jax.ShapeDtypeStruct((t, 2 * D_FF), BF16)
    return pl.pallas_call(
        body, name="ffn_up_fwd", grid=(nb, 2), out_shape=[wide, wide, jax.ShapeDtypeStruct((t, D_FF), BF16)],
        in_specs=[pl.BlockSpec((t, D_MODEL), lambda j, k: (0, 0)), half(D_MODEL), half(3), half(1)],
        out_specs=[half(t), half(t), pl.BlockSpec((t, CB), lambda j, k: (0, j))],
        scratch_shapes=[pltpu.VMEM((t + PAD, CB), F32), pltpu.VMEM((t, CB), F32)],
        compiler_params=_params("parallel", "arbitrary"),
    )(h2, w_up, w_conv, b_conv)


def _ffn_down(f, act, w_down, x1, target, g4):
    t = f.shape[0]
    tm = min(256, t)
    cc = 512

    def body(f_ref, act_ref, w_ref, x1_ref, tg_ref, g_ref, dy_ref, dout_ref, back_ref, dg_ref, loss_ref):
        @pl.when(pl.program_id(0) == 0)
        def _():
            dg_ref[...] = jnp.zeros_like(dg_ref)
            loss_ref[...] = jnp.zeros_like(loss_ref)
        out = _dot(f_ref[...], w_ref[...])
        n4, r4 = _rms_fwd(out)
        err = x1_ref[...] + n4 * g_ref[...] - tg_ref[...]
        loss_ref[...] += jnp.full(loss_ref.shape, 0.5 / D_MODEL, F32) * jnp.sum(err * err)
        dy = err * (1.0 / D_MODEL)
        dy_ref[...] = dy
        dg_ref[...] += jnp.sum(dy * n4, axis=0, keepdims=True)
        d_out = _rms_bwd(n4, r4, dy * g_ref[...]).astype(BF16)
        dout_ref[...] = d_out
        for c0 in range(0, D_FF, cc):
            d_f = _dot_nt(d_out, w_ref[pl.ds(c0, cc), :])
            gelu, d_gelu = _gelu_and_grad(act_ref[:, pl.ds(c0, cc)].astype(F32))
            val = act_ref[:, pl.ds(D_FF + c0, cc)].astype(F32)
            back_ref[:, pl.ds(c0, cc)] = (d_f * val * d_gelu).astype(BF16)
            back_ref[:, pl.ds(D_FF + c0, cc)] = (d_f * gelu).astype(BF16)

    row = pl.BlockSpec((tm, D_MODEL), lambda i: (i, 0))
    wide = pl.BlockSpec((tm, 2 * D_FF), lambda i: (i, 0))
    vec = pl.BlockSpec((1, D_MODEL), lambda i: (0, 0))
    return pl.pallas_call(
        body, name="ffn_down_fwd_bwd", grid=(t // tm,),
        out_shape=[jax.ShapeDtypeStruct((t, D_MODEL), F32), jax.ShapeDtypeStruct((t, D_MODEL), BF16),
                   jax.ShapeDtypeStruct((t, 2 * D_FF), BF16), jax.ShapeDtypeStruct((1, D_MODEL), F32),
                   jax.ShapeDtypeStruct((SUBLANES, LANES), F32)],
        in_specs=[pl.BlockSpec((tm, D_FF), lambda i: (i, 0)), wide, pl.BlockSpec((D_FF, D_MODEL), lambda i: (0, 0)),
                  row, row, vec],
        out_specs=[row, row, wide, vec, pl.BlockSpec((SUBLANES, LANES), lambda i: (0, 0))],
        compiler_params=_params("arbitrary"),
    )(f, act, w_down, x1, target, g4)


def _grad_tn(a, b, bm, name):
    t, m = a.shape
    n = b.shape[1]

    def body(a_ref, b_ref, o_ref):
        o_ref[...] = _dot_tn(a_ref[...], b_ref[...]).astype(BF16)

    return pl.pallas_call(
        body, name=name, grid=(m // bm,), out_shape=jax.ShapeDtypeStruct((m, n), BF16),
        in_specs=[pl.BlockSpec((t, bm), lambda i: (0, i)), pl.BlockSpec((t, n), lambda i: (0, 0))],
        out_specs=pl.BlockSpec((bm, n), lambda i: (i, 0)),
        compiler_params=_params("parallel"),
    )(a, b)


def _ffn_up_bwd(up, back, w_conv, h2, w_up):
    t = h2.shape[0]
    rc = _row_chunk(t)
    cb = FFN_BWD_COLS

    def body(up_ref, back_ref, c_ref, h_ref, w_ref, dw_ref, dcw_ref, dcb_ref, dh_ref, pad, after, d_up):
        @pl.when(pl.program_id(0) == 0)
        def _():
            dh_ref[...] = jnp.zeros_like(dh_ref)
        pad[pl.ds(0, PAD), :] = jnp.zeros((PAD, cb), F32)
        after[pl.ds(t, PAD), :] = jnp.zeros((PAD, cb), F32)
        for r0 in range(0, t, rc):
            pad[pl.ds(PAD + r0, rc), :] = up_ref[pl.ds(r0, rc), :].astype(F32)
            after[pl.ds(r0, rc), :] = back_ref[pl.ds(r0, rc), :].astype(F32)
        cw = c_ref[...]
        taps = [jnp.zeros((SUBLANES, cb), F32)] * 3
        bias = jnp.zeros((SUBLANES, cb), F32)
        for r0 in range(0, t, rc):
            for q0 in range(r0, r0 + rc, ROW_SLICE):
                rows = pl.ds(q0, ROW_SLICE)
                d_up[rows, :] = _conv_anticausal(after, cw, q0, ROW_SLICE, 3).astype(BF16)
                g = after[rows, :]
                taps = [acc + _fold_rows(g * _rows_back(pad, q0, ROW_SLICE, 2 - k)) for k, acc in enumerate(taps)]
                bias = bias + _fold_rows(g)
            rows = pl.ds(r0, rc)
            dh_ref[rows, :] += _dot_nt(d_up[rows, :], w_ref[...])
        dw_ref[...] = _dot_tn(h_ref[...], d_up[...]).astype(BF16)
        dcw_ref[...] = jnp.concatenate([jnp.sum(acc, axis=0, keepdims=True) for acc in taps], axis=0)
        dcb_ref[...] = jnp.sum(bias, axis=0, keepdims=True)

    cols = lambda rows: pl.BlockSpec((rows, cb), lambda j: (0, j))
    whole = pl.BlockSpec((t, D_MODEL), lambda j: (0, 0))
    return pl.pallas_call(
        body, name="ffn_up_bwd", grid=(2 * D_FF // cb,),
        out_shape=[jax.ShapeDtypeStruct((D_MODEL, 2 * D_FF), BF16), jax.ShapeDtypeStruct((3, 2 * D_FF), F32),
                   jax.ShapeDtypeStruct((1, 2 * D_FF), F32), jax.ShapeDtypeStruct((t, D_MODEL), F32)],
        in_specs=[cols(t), cols(t), cols(3), whole, cols(D_MODEL)],
        out_specs=[cols(D_MODEL), cols(3), cols(1), whole],
        scratch_shapes=[pltpu.VMEM((t + PAD, cb), F32), pltpu.VMEM((t + PAD, cb), F32), pltpu.VMEM((t, cb), BF16)],
        compiler_params=_params("arbitrary"),
    )(up, back, w_conv, h2, w_up)


def _merge_bwd(dy, d_h2, x1, mix, g3, g2, w_out, w_cb, w_lb, pa, pb, proj):
    t = dy.shape[0]
    tm = min(256, t)

    def body(dy_ref, dh2_ref, x1_ref, mix_ref, g3_ref, g2_ref, wo_ref, wcb_ref, wlb_ref, pa_ref, pb_ref, gc_ref, gl_ref,
             dx1_ref, dmix_ref, dpa_ref, dpb_ref, dya_ref, dyb_ref, dgate_ref, dg3_ref, dg2_ref):
        @pl.when(pl.program_id(0) == 0)
        def _():
            dg3_ref[...] = jnp.zeros_like(dg3_ref)
            dg2_ref[...] = jnp.zeros_like(dg2_ref)
        n3, r3 = _rms_fwd(x1_ref[...])
        d_h2 = dh2_ref[...]
        dg3_ref[...] += jnp.sum(d_h2 * n3, axis=0, keepdims=True)
        dx1 = dy_ref[...] + _rms_bwd(n3, r3, d_h2 * g3_ref[...])
        dx1_ref[...] = dx1
        n2, r2 = _rms_fwd(mix_ref[...])
        dg2_ref[...] += jnp.sum(dx1 * n2, axis=0, keepdims=True)
        d_mix = _rms_bwd(n2, r2, dx1 * g2_ref[...]).astype(BF16)
        dmix_ref[...] = d_mix
        d_merged = _dot_nt(d_mix, wo_ref[...])
        sc = jax.nn.sigmoid(gc_ref[...].astype(F32))
        sl = jax.nn.sigmoid(gl_ref[...].astype(F32))
        d_pa = (d_merged * sc).astype(BF16)
        d_pb = (d_merged * sl).astype(BF16)
        dpa_ref[...] = d_pa
        dpb_ref[...] = d_pb
        dgate_ref[0] = (d_merged * pa_ref[...].astype(F32) * sc * (1.0 - sc)).astype(BF16)
        dgate_ref[1] = (d_merged * pb_ref[...].astype(F32) * sl * (1.0 - sl)).astype(BF16)
        dya_ref[...] = _dot_nt(d_pa, wcb_ref[...]).astype(BF16)
        dyb_ref[...] = _dot_nt(d_pb, wlb_ref[...]).astype(BF16)

    row = pl.BlockSpec((tm, D_MODEL), lambda i: (i, 0))
    full = pl.BlockSpec((D_MODEL, D_MODEL), lambda i: (0, 0))
    vec = pl.BlockSpec((1, D_MODEL), lambda i: (0, 0))
    act = jax.ShapeDtypeStruct((t, D_MODEL), BF16)
    small = jax.ShapeDtypeStruct((1, D_MODEL), F32)
    return pl.pallas_call(
        body, name="merge_bwd", grid=(t // tm,),
        out_shape=[jax.ShapeDtypeStruct((t, D_MODEL), F32), act, act, act, act, act,
                   jax.ShapeDtypeStruct((2, t, D_MODEL), BF16), small, small],
        in_specs=[row, row, row, row, vec, vec, full, full, full, row, row,
                  pl.BlockSpec((tm, D_MODEL), lambda i: (i, 5)), pl.BlockSpec((tm, D_MODEL), lambda i: (i, 6))],
        out_specs=[row] * 6 + [pl.BlockSpec((2, tm, D_MODEL), lambda i: (0, i, 0)), vec, vec],
        compiler_params=_params("arbitrary"),
    )(dy, d_h2, x1, mix, g3, g2, w_out, w_cb, w_lb, pa, pb, proj, proj)


def _conv_mixer_bwd(proj, d_ya, w_short):
    t = proj.shape[0]
    rc = _row_chunk(t)

    def body(b_ref, c_ref, x_ref, dy_ref, w_ref, d_ref, dw_ref, pad, back):
        pad[pl.ds(0, PAD), :] = jnp.zeros((PAD, CB), F32)
        back[pl.ds(t, PAD), :] = jnp.zeros((PAD, CB), F32)
        for r0 in range(0, t, rc):
            rows = pl.ds(r0, rc)
            pad[pl.ds(PAD + r0, rc), :] = c_ref[rows, :].astype(F32) * x_ref[rows, :].astype(F32)
        w = w_ref[...]
        for r0 in range(0, t, rc):
            rows = pl.ds(r0, rc)
            d_y = dy_ref[rows, :].astype(F32)
            d_ref[0, rows, :] = (d_y * _conv_causal(pad, w, r0, rc, 3)).astype(BF16)
            back[rows, :] = d_y * b_ref[rows, :].astype(F32)
        taps = [jnp.zeros((1, CB), F32)] * 3
        for r0 in range(0, t, rc):
            rows = pl.ds(r0, rc)
            d_u = _conv_anticausal(back, w, r0, rc, 3)
            d_ref[1, rows, :] = (d_u * x_ref[rows, :].astype(F32)).astype(BF16)
            d_ref[2, rows, :] = (d_u * c_ref[rows, :].astype(F32)).astype(BF16)
            taps = [acc + new for acc, new in zip(taps, _conv_wgrad(back[rows, :], pad, r0, rc, 3))]
        dw_ref[...] = jnp.concatenate(taps, axis=0)

    blk = pl.BlockSpec((t, CB), lambda h: (0, h))
    return pl.pallas_call(
        body, name="conv_mixer_bwd", grid=(D_MODEL // CB,),
        out_shape=[jax.ShapeDtypeStruct((3, t, D_MODEL), BF16), jax.ShapeDtypeStruct((3, D_MODEL), F32)],
        in_specs=[_section(0, t), _section(1, t), _section(2, t), blk, pl.BlockSpec((3, CB), lambda h: (0, h))],
        out_specs=[pl.BlockSpec((3, t, CB), lambda h: (0, 0, h)), pl.BlockSpec((3, CB), lambda h: (0, h))],
        scratch_shapes=[pltpu.VMEM((t + PAD, CB), F32), pltpu.VMEM((t + PAD, CB), F32)],
        compiler_params=_params("parallel"),
    )(proj, proj, proj, d_ya, w_short)


LRU_SMALL_ROWS = 8


def _lru_bwd(proj, hl, a_all, kept, d_yb, w_conv, wa, wx, lam):
    t = proj.shape[0]
    rc = _row_chunk(t)
    vec, mat = _head_specs()

    def body(lx_ref, ly_ref, hl_ref, a_ref, kept_ref, dy_ref, wc_ref, wa_ref, wx_ref, lam_ref,
             d_ref, dwa_ref, dwx_ref, small_ref, pad, a_next, dh_s, h_prev, back, acc_a, acc_x, dz_a, dz_x):
        zeros = jnp.zeros((PAD, CB), F32)
        pad[pl.ds(0, PAD), :] = zeros
        h_prev[pl.ds(0, PAD), :] = zeros
        a_next[pl.ds(t, PAD), :] = zeros
        back[pl.ds(t, PAD), :] = zeros
        for r0 in range(0, t, ROW_SLICE):
            rows = pl.ds(r0, ROW_SLICE)
            pad[pl.ds(PAD + r0, ROW_SLICE), :] = lx_ref[rows, :].astype(F32)
            h_prev[pl.ds(PAD + r0, ROW_SLICE), :] = hl_ref[rows, :]
            a_next[pl.ds(PAD - 1 + r0, ROW_SLICE), :] = a_ref[rows, :]
            act, d_act = _gelu_and_grad(ly_ref[rows, :].astype(F32))
            d_y = dy_ref[rows, :].astype(F32)
            dh_s[rows, :] = d_y * act
            d_ref[1, rows, :] = (d_y * hl_ref[rows, :] * d_act).astype(BF16)
        wc = wc_ref[...]
        wa_m, wx_m = wa_ref[...].reshape(HEAD_DIM, HEAD_DIM), wx_ref[...].reshape(HEAD_DIM, HEAD_DIM)
        ls = _log_sigmoid(lam_ref[...])

        row = lax.broadcasted_iota(jnp.int32, (SUBLANES, CB), 0)
        groups = t // SUBLANES

        def group(i, carry):
            r = pl.multiple_of((groups - 1 - i) * SUBLANES, SUBLANES)
            a_g, b_g = a_next[pl.ds(PAD + r, SUBLANES), :], dh_s[pl.ds(r, SUBLANES), :]
            for s in (1, 2, 4):
                keep = row < SUBLANES - s
                b_g = jnp.where(keep, a_g * pltpu.roll(b_g, SUBLANES - s, 0) + b_g, b_g)
                a_g = jnp.where(keep, a_g * pltpu.roll(a_g, SUBLANES - s, 0), a_g)
            d_g = b_g + a_g * carry
            dh_s[pl.ds(r, SUBLANES), :] = d_g
            return jnp.broadcast_to(d_g[0:1, :], (SUBLANES, CB))

        lax.fori_loop(0, groups, group, jnp.zeros((SUBLANES, CB), F32))

        acc_a[...] = jnp.zeros_like(acc_a)
        acc_x[...] = jnp.zeros_like(acc_x)
        d_ba = d_bx = d_ls = jnp.zeros((SUBLANES, CB), F32)
        for r0 in range(0, t, rc):
            for q0 in range(r0, r0 + rc, ROW_SLICE):
                rows, local = pl.ds(q0, ROW_SLICE), pl.ds(q0 - r0, ROW_SLICE)
                a = a_ref[rows, :]
                xl, ra, ia = (kept_ref[i, rows, :].astype(F32) for i in range(3))
                a_sq = a * a
                mult = jnp.sqrt(1.0 - a_sq)
                slope = -a_sq / mult
                if q0 == 0:
                    first = lax.broadcasted_iota(jnp.int32, (ROW_SLICE, CB), 0) == 0
                    mult, slope = jnp.where(first, 1.0, mult), jnp.where(first, 0.0, slope)
                d_h = dh_s[rows, :]
                d_la = d_h * _rows_back(h_prev, q0, ROW_SLICE, 1) * a + d_h * ia * xl * slope
                d_za = d_la * (LRU_C * ls) * ra * (1.0 - ra)
                d_zx = d_h * mult * xl * ia * (1.0 - ia)
                d_ls = d_ls + _fold_rows(d_la * ra)
                d_ba = d_ba + _fold_rows(d_za)
                d_bx = d_bx + _fold_rows(d_zx)
                dz_a[local, :] = d_za.astype(BF16)
                dz_x[local, :] = d_zx.astype(BF16)
                back[rows, :] = d_h * mult * ia
            rows = pl.ds(r0, rc)
            xb = kept_ref[0, rows, :]
            acc_a[...] += _dot_tn(xb, dz_a[...])
            acc_x[...] += _dot_tn(xb, dz_x[...])
            back[rows, :] += _dot_nt(dz_a[...], wa_m) + _dot_nt(dz_x[...], wx_m)
        taps = [jnp.zeros((SUBLANES, CB), F32)] * 4
        d_bc = jnp.zeros((SUBLANES, CB), F32)
        for q0 in range(0, t, ROW_SLICE):
            rows = pl.ds(q0, ROW_SLICE)
            d_ref[0, rows, :] = _conv_anticausal(back, wc, q0, ROW_SLICE, 4).astype(BF16)
            g = back[rows, :]
            taps = [acc + _fold_rows(g * _rows_back(pad, q0, ROW_SLICE, 3 - k)) for k, acc in enumerate(taps)]
            d_bc = d_bc + _fold_rows(g)
        d_lam = d_ls * LRU_C * jax.nn.sigmoid(-lam_ref[...])
        small_ref[...] = jnp.concatenate(
            [jnp.sum(v, axis=0, keepdims=True) for v in taps + [d_bc, d_ba, d_bx, d_lam]], axis=0)
        dwa_ref[...] = acc_a[...].reshape(N_DEV, HEAD_DIM // N_DEV, HEAD_DIM).astype(BF16)
        dwx_ref[...] = acc_x[...].reshape(N_DEV, HEAD_DIM // N_DEV, HEAD_DIM).astype(BF16)

    blk = pl.BlockSpec((t, CB), lambda h: (0, h))
    gate_grad = jax.ShapeDtypeStruct((N_DEV, N_HEADS, HEAD_DIM // N_DEV, HEAD_DIM), BF16)
    return pl.pallas_call(
        body, name="lru_bwd", grid=(N_HEADS,),
        out_shape=[jax.ShapeDtypeStruct((2, t, D_MODEL), BF16), gate_grad, gate_grad,
                   jax.ShapeDtypeStruct((LRU_SMALL_ROWS, D_MODEL), F32)],
        in_specs=[_section(3, t), _section(4, t), blk, blk, pl.BlockSpec((3, t, CB), lambda h: (0, 0, h)), blk,
                  pl.BlockSpec((4, CB), lambda h: (0, h)), mat, mat, vec],
        out_specs=[pl.BlockSpec((2, t, CB), lambda h: (0, 0, h)), mat, mat,
                   pl.BlockSpec((LRU_SMALL_ROWS, CB), lambda h: (0, h))],
        scratch_shapes=[pltpu.VMEM((t + PAD, CB), F32), pltpu.VMEM((t + PAD, CB), F32), pltpu.VMEM((t, CB), F32),
                        pltpu.VMEM((t + PAD, CB), F32), pltpu.VMEM((t + PAD, CB), F32),
                        pltpu.VMEM((HEAD_DIM, HEAD_DIM), F32), pltpu.VMEM((HEAD_DIM, HEAD_DIM), F32),
                        pltpu.VMEM((rc, CB), BF16), pltpu.VMEM((rc, CB), BF16)],
        compiler_params=_params("parallel"),
    )(proj, proj, hl, a_all, kept, d_yb, w_conv, wa, wx, lam)


def _stack_maps(halves):
    def conv(sec, part):
        return jnp.minimum(sec, 2), jnp.where(sec < 3, part, halves - 1)

    def lru(sec, part):
        return jnp.clip(sec - 3, 0, 1), jnp.where(sec < 3, 0, jnp.where(sec < 5, part, halves - 1))

    def gate(sec, part):
        return jnp.clip(sec - 5, 0, 1), jnp.where(sec < 5, 0, part)

    return conv, lru, gate


def _pick_stack(sec, refs, fn):
    @pl.when(sec < 3)
    def _():
        fn(refs[0])

    @pl.when((sec >= 3) & (sec < 5))
    def _():
        fn(refs[1])

    @pl.when(sec >= 5)
    def _():
        fn(refs[2])


def _in_proj_wgrad(h, d_conv, d_lru, d_gate):
    t = h.shape[0]
    halves, bn = 1, D_MODEL
    maps = _stack_maps(halves)

    def body(h_ref, dc_ref, dl_ref, dg_ref, o_ref):
        def emit(ref):
            o_ref[...] = _dot_tn(h_ref[...], ref[...]).astype(BF16)
        _pick_stack(pl.program_id(0) // halves, (dc_ref, dl_ref, dg_ref), emit)

    def spec(m):
        def index(s):
            stack, part = m(s // halves, s % halves)
            return stack, 0, part
        return pl.BlockSpec((None, t, bn), index)

    return pl.pallas_call(
        body, name="in_proj_wgrad", grid=(7 * halves,), out_shape=jax.ShapeDtypeStruct((D_MODEL, IN_COLS), BF16),
        in_specs=[pl.BlockSpec((t, D_MODEL), lambda s: (0, 0))] + [spec(m) for m in maps],
        out_specs=pl.BlockSpec((D_MODEL, bn), lambda s: (0, s)),
        compiler_params=_params("arbitrary"),
    )(h, d_conv, d_lru, d_gate)


def _in_proj_xgrad(d_conv, d_lru, d_gate, w_in, x, dx1, g1):
    t = x.shape[0]
    tm = min(1024, t)
    maps = _stack_maps(1)

    def body(dc_ref, dl_ref, dg_ref, w_ref, x_ref, dx1_ref, g_ref, dx_ref, dgain_ref, acc):
        i, s = pl.program_id(0), pl.program_id(1)

        @pl.when((i == 0) & (s == 0))
        def _():
            dgain_ref[...] = jnp.zeros_like(dgain_ref)

        @pl.when(s == 0)
        def _():
            acc[...] = jnp.zeros_like(acc)

        def add(ref):
            acc[...] += _dot_nt(ref[...], w_ref[...])
        _pick_stack(s, (dc_ref, dl_ref, dg_ref), add)

        @pl.when(s == 6)
        def _():
            n1, r1 = _rms_fwd(x_ref[...])
            d_h = acc[...]
            dgain_ref[...] += jnp.sum(d_h * n1, axis=0, keepdims=True)
            dx_ref[...] = dx1_ref[...] + _rms_bwd(n1, r1, d_h * g_ref[...])

    def spec(m):
        def index(i, s):
            return m(s, 0)[0], i, 0
        return pl.BlockSpec((None, tm, D_MODEL), index)

    row = pl.BlockSpec((tm, D_MODEL), lambda i, s: (i, 0))
    vec = pl.BlockSpec((1, D_MODEL), lambda i, s: (0, 0))
    return pl.pallas_call(
        body, name="in_proj_xgrad", grid=(t // tm, 7),
        out_shape=[jax.ShapeDtypeStruct((t, D_MODEL), F32), jax.ShapeDtypeStruct((1, D_MODEL), F32)],
        in_specs=[spec(m) for m in maps] + [pl.BlockSpec((D_MODEL, D_MODEL), lambda i, s: (0, s)), row, row, vec],
        out_specs=[row, vec],
        scratch_shapes=[pltpu.VMEM((tm, D_MODEL), F32)],
        compiler_params=_params("arbitrary", "arbitrary"),
    )(d_conv, d_lru, d_gate, w_in, x, dx1, g1)


def _adamw(w, g, m, v):
    m = ADAM_B1 * m + (1.0 - ADAM_B1) * g
    v = ADAM_B2 * v + (1.0 - ADAM_B2) * (g * g)
    m_hat = m / (1.0 - ADAM_B1 ** ADAM_STEP)
    v_hat = v / (1.0 - ADAM_B2 ** ADAM_STEP)
    return -ADAM_LR * (m_hat / (jnp.sqrt(v_hat) + ADAM_EPS) + ADAM_WD * w), m, v


def _adam_large(w, m, v, own, others, name):
    shape = w.shape
    cols = shape[-1]
    w2, m2, v2 = (a.reshape(-1, cols) for a in (w, m, v))
    rows = w2.shape[0]
    own, others = own.reshape(4, rows, cols), others.reshape(3, rows, cols)
    rb = _row_block(rows, 512)

    def body(w_ref, m_ref, v_ref, own_ref, oth_ref, g_ref, d_ref, nm_ref, nv_ref):
        g = own_ref[...].astype(F32)
        for k in range(3):
            g = g + oth_ref[k].astype(F32)
        g_ref[...] = g
        d_ref[...], nm_ref[...], nv_ref[...] = _adamw(w_ref[...], g, m_ref[...], v_ref[...])

    blk = pl.BlockSpec((rb, cols), lambda i: (i, 0))
    res = jax.ShapeDtypeStruct((rows, cols), F32)
    outs = pl.pallas_call(
        body, name=name, grid=(rows // rb,), out_shape=[res] * 4,
        in_specs=[blk, blk, blk, pl.BlockSpec((None, rb, cols), lambda i: (0, i, 0)),
                  pl.BlockSpec((3, rb, cols), lambda i: (0, i, 0))],
        out_specs=[blk] * 4, compiler_params=_params("parallel"),
    )(w2, m2, v2, own, others)
    return [o.reshape(shape) for o in outs]


def _adam_small(ws, gs, ms, vs):
    n = len(ws)

    def body(*refs):
        w_refs, g_refs, m_refs, v_refs = (refs[i * n:(i + 1) * n] for i in range(4))
        outs = refs[4 * n:]
        for i in range(n):
            d, m, v = _adamw(w_refs[i][...], g_refs[i][...], m_refs[i][...], v_refs[i][...])
            outs[i][...], outs[n + i][...], outs[2 * n + i][...] = d, m, v

    shapes = [jax.ShapeDtypeStruct(w.shape, F32) for w in ws]
    outs = pl.pallas_call(
        body, name="adam_small", out_shape=shapes * 3,
        in_specs=[VMEM_SPEC] * (4 * n), out_specs=[VMEM_SPEC] * (3 * n), compiler_params=_params(),
    )(*ws, *gs, *ms, *vs)
    return outs[:n], outs[n:2 * n], outs[2 * n:]


def _pack_rows(pieces):
    tile = SUBLANES * LANES
    return jnp.concatenate([jnp.pad(p.reshape(-1), (0, (-p.size) % tile)).reshape(-1, LANES) for p in pieces], axis=0)


def _packed_starts(sizes):
    tile = SUBLANES * LANES
    starts = [0]
    for s in sizes:
        starts.append(starts[-1] + (s + tile - 1) // tile * SUBLANES)
    return starts


def kernel(x, norm_mix_pre, norm_mix_post, norm_ffn_pre, norm_ffn_post, w_in, conv_short_w, w_conv_branch, lru_conv_w, lru_conv_b, lru_wa, lru_ba, lru_wx, lru_bx, lru_lambda, w_lru_branch, w_out, ffn_w_up, ffn_conv_w, ffn_conv_b, ffn_w_down, loss_target, m_norm_mix_pre, m_norm_mix_post, m_norm_ffn_pre, m_norm_ffn_post, m_w_in, m_conv_short_w, m_w_conv_branch, m_lru_conv_w, m_lru_conv_b, m_lru_wa, m_lru_ba, m_lru_wx, m_lru_bx, m_lru_lambda, m_w_lru_branch, m_w_out, m_ffn_w_up, m_ffn_conv_w, m_ffn_conv_b, m_ffn_w_down, v_norm_mix_pre, v_norm_mix_post, v_norm_ffn_pre, v_norm_ffn_post, v_w_in, v_conv_short_w, v_w_conv_branch, v_lru_conv_w, v_lru_conv_b, v_lru_wa, v_lru_ba, v_lru_wx, v_lru_bx, v_lru_lambda, v_w_lru_branch, v_w_out, v_ffn_w_up, v_ffn_conv_w, v_ffn_conv_b, v_ffn_w_down):
    t = x.shape[1]
    xi, yi, ci = _position()
    me = _block_of(xi, yi, ci)
    x2, target = x[0], loss_target[0]
    shard_in, shard_up = IN_COLS // N_DEV, 2 * D_FF // N_DEV
    shard_sq, shard_down, shard_head = D_MODEL // N_DEV, D_FF // N_DEV, HEAD_DIM // N_DEV

    names = ["w_in", "lru_wa", "lru_wx", "w_conv_branch", "w_lru_branch", "w_out", "ffn_w_up", "ffn_w_down"]
    large = [w_in[0], lru_wa[0], lru_wx[0], w_conv_branch[0], w_lru_branch[0], w_out[0], ffn_w_up[0], ffn_w_down[0]]
    blocks = [_cols(shard_in), _lead, _lead, _rows(shard_sq), _rows(shard_sq), _rows(shard_sq),
              _cols(shard_up), _rows(shard_down)]
    gate_full = (N_DEV, N_HEADS, shard_head, HEAD_DIM)
    full_shapes = [(D_MODEL, IN_COLS), gate_full, gate_full, (D_MODEL, D_MODEL), (D_MODEL, D_MODEL), (D_MODEL, D_MODEL),
                   (D_MODEL, 2 * D_FF), (D_FF, D_MODEL)]
    n_now = 3
    small_sharded = [conv_short_w, lru_conv_w, lru_ba, lru_bx, ffn_conv_w]
    small_mine = _pack_rows(small_sharded)
    small_at = _packed_starts([p.size for p in small_sharded])
    *gathered, small_all, proj, h = _gather_weights(large, blocks, full_shapes, small_mine, n_now, x2, norm_mix_pre)
    g_in, g_wa, g_wx = gathered[:n_now]
    later_blocks = blocks[n_now:]
    send1, recv1, later, gather_token = _gather_start(gathered[n_now:], later_blocks, "gather_start")

    def behind(token, operand):
        return operand + token[0:1, 0:1]

    def forward(lo, hi, after, tag):
        return _gather_forward(later[lo:hi], later_blocks[lo:hi], send1[4 * lo:4 * hi], recv1[4 * lo:4 * hi], after,
                               "gather_forward_" + tag)

    def finish(lo, hi, flight, after, tag):
        return _gather_finish(flight[2], later_blocks[lo:hi], flight[0], flight[1], after, "gather_finish_" + tag)

    def cols_of(r0, n, width):
        part = small_all[:, r0:r0 + n * width // LANES, :].reshape(N_DEV, n, width)
        return part.transpose(1, 0, 2).reshape(n, N_DEV * width)

    c_short = cols_of(small_at[0], 3, LANES)
    c_lru = cols_of(small_at[1], 4, LANES)
    b_a = cols_of(small_at[2], N_HEADS, shard_head).reshape(1, D_MODEL)
    b_x = cols_of(small_at[3], N_HEADS, shard_head).reshape(1, D_MODEL)
    c_ffn = cols_of(small_at[4], 3, shard_up)

    y_a = _conv_mixer_fwd(proj, behind(gather_token, c_short))
    y_b, hl, decay, lru_kept = _lru_fwd(proj, behind(gather_token, c_lru), lru_conv_b, g_wa, b_a, g_wx, b_x, lru_lambda)
    flight_mix_w = forward(0, 3, y_b, "mix")
    g_cb, g_lb, g_out = finish(0, 3, flight_mix_w, y_b, "mix")
    pa, pb, merged, mix, x1, h2 = _merge(y_a, y_b, proj, x2, g_cb, g_lb, g_out, norm_mix_post, norm_ffn_pre)
    flight_up_w = forward(3, 4, h2, "up")
    (g_up,) = finish(3, 4, flight_up_w, h2, "up")
    up, act, f = _ffn_up(h2, g_up, c_ffn, ffn_conv_b)
    flight_down_w = forward(4, 5, f, "down")
    (g_down,) = finish(4, 5, flight_down_w, f, "down")
    dy, d_out, d_act, dg4, loss_part = _ffn_down(f, act, g_down, x1, target, norm_ffn_post)

    block_of = dict(zip(names, blocks))
    shard_shapes = {"w_in": (D_MODEL, shard_in), "w_conv_branch": (shard_sq, D_MODEL), "w_lru_branch": (shard_sq, D_MODEL),
                    "w_out": (shard_sq, D_MODEL), "lru_wa": (N_HEADS, shard_head, HEAD_DIM),
                    "lru_wx": (N_HEADS, shard_head, HEAD_DIM), "ffn_w_up": (D_MODEL, shard_up),
                    "ffn_w_down": (shard_down, D_MODEL)}

    def reduce_start(tag, grads):
        keys = list(grads)
        sums = _reduce_pair([grads[k] for k in keys], [block_of[k] for k in keys], [shard_shapes[k] for k in keys],
                            "reduce_pair_" + tag)
        return (keys,) + _exchange_chips_start(sums, "reduce_chip_start_" + tag)

    gw_down = _grad_tn(f, d_out, min(512, D_FF), "ffn_down_wgrad")
    flight_down = reduce_start("down", {"ffn_w_down": gw_down})
    gw_up, gc_ffn, gb_ffn, d_h2 = _ffn_up_bwd(up, d_act, behind(flight_down[-1], c_ffn), h2, g_up)
    flight_up = reduce_start("up", {"ffn_w_up": gw_up})
    dx1, d_mix, d_pa, d_pb, d_ya, d_yb, d_gate, dg3, dg2 = _merge_bwd(
        dy, d_h2, x1, mix, behind(flight_up[-1], norm_ffn_pre), norm_mix_post, g_out, g_cb, g_lb, pa, pb, proj)
    gw_out = _grad_tn(merged, d_mix, CB, "w_out_wgrad")
    gw_cb = _grad_tn(y_a, d_pa, CB, "w_conv_branch_wgrad")
    gw_lb = _grad_tn(y_b, d_pb, CB, "w_lru_branch_wgrad")
    flight_mix = reduce_start("mix", {"w_conv_branch": gw_cb, "w_lru_branch": gw_lb, "w_out": gw_out})
    d_conv, gc_short = _conv_mixer_bwd(proj, d_ya, behind(flight_mix[-1], c_short))
    d_lru, gw_a, gw_x, g_lru_small = _lru_bwd(proj, hl, decay, lru_kept, d_yb, c_lru, g_wa, g_wx, lru_lambda)
    early = [dg2, dg3, dg4, g_lru_small[4:5], g_lru_small[7:8], gb_ffn, gc_short, g_lru_small[0:4],
             g_lru_small[5:6], g_lru_small[6:7], gc_ffn, loss_part]
    flight_small = _small_start(_pack_rows(early), "small_start")
    gw_in = _in_proj_wgrad(h, d_conv, d_lru, d_gate)
    flight_in = reduce_start("in", {"lru_wa": gw_a, "lru_wx": gw_x, "w_in": gw_in})
    dx, dg1 = _in_proj_xgrad(d_conv, d_lru, d_gate, g_in, x2, dx1,
                             behind(flight_small[-1], behind(flight_in[-1], norm_mix_pre)))
    flight_late = _small_start(_pack_rows([dg1]), "small_start_late")

    moments ={"w_in": (m_w_in, v_w_in), "w_conv_branch": (m_w_conv_branch, v_w_conv_branch),
               "w_lru_branch": (m_w_lru_branch, v_w_lru_branch), "w_out": (m_w_out, v_w_out),
               "lru_wa": (m_lru_wa, v_lru_wa), "lru_wx": (m_lru_wx, v_lru_wx), "ffn_w_up": (m_ffn_w_up, v_ffn_w_up),
               "ffn_w_down": (m_ffn_w_down, v_ffn_w_down)}
    weights = {"w_in": w_in, "w_conv_branch": w_conv_branch, "w_lru_branch": w_lru_branch, "w_out": w_out,
               "lru_wa": lru_wa, "lru_wx": lru_wx, "ffn_w_up": ffn_w_up, "ffn_w_down": ffn_w_down}
    out_g, out_d, out_m, out_v = {}, {}, {}, {}

    after = flight_late[-1]
    for tag, (keys, send, recv, sums, lands, _) in (("down", flight_down), ("up", flight_up), ("mix", flight_mix),
                                                    ("in", flight_in)):
        sums, others = _exchange_chips_wait(send, recv, sums, lands, after, "reduce_chip_wait_" + tag)
        for k, own, oth in zip(keys, sums, others):
            out_g[k], out_d[k], out_m[k], out_v[k] = _adam_large(weights[k], *moments[k], own, oth, "adam_" + k)
        after = out_d[keys[-1]]

    total, total_late = _small_sum([_small_wait(*flight_small[:4], after, "small_wait"),
                                    _small_wait(*flight_late[:4], after, "small_wait_late")], me)
    sizes = [p.size for p in early]
    starts = _packed_starts(sizes)

    def piece(i, shape):
        if i == 0:
            return total_late.reshape(-1)[:D_MODEL].reshape(shape)
        return total[starts[i - 1]:starts[i]].reshape(-1)[:sizes[i - 1]].reshape(shape)

    loss = total[starts[11], 0]

    def col_shard(full, width):
        return lax.dynamic_slice_in_dim(full, me * width, width, axis=1)

    def head_shard(full):
        return lax.dynamic_slice_in_dim(full.reshape(N_HEADS, HEAD_DIM), me * shard_head, shard_head, axis=1)

    small_names = ["norm_mix_pre", "norm_mix_post", "norm_ffn_pre", "norm_ffn_post", "lru_conv_b", "lru_lambda",
                   "ffn_conv_b", "conv_short_w", "lru_conv_w", "lru_ba", "lru_bx", "ffn_conv_w"]
    small_g = [piece(0, (1, D_MODEL)), piece(1, (1, D_MODEL)), piece(2, (1, D_MODEL)), piece(3, (1, D_MODEL)),
               piece(4, (1, D_MODEL)), piece(5, (1, D_MODEL)), piece(6, (1, 2 * D_FF)),
               col_shard(piece(7, (3, D_MODEL)), LANES), col_shard(piece(8, (4, D_MODEL)), LANES),
               head_shard(piece(9, (1, D_MODEL))), head_shard(piece(10, (1, D_MODEL))),
               col_shard(piece(11, (3, 2 * D_FF)), shard_up)]
    small_w = [norm_mix_pre, norm_mix_post, norm_ffn_pre, norm_ffn_post, lru_conv_b, lru_lambda, ffn_conv_b,
               conv_short_w[0], lru_conv_w[0], lru_ba[0], lru_bx[0], ffn_conv_w[0]]
    small_m = [m_norm_mix_pre, m_norm_mix_post, m_norm_ffn_pre, m_norm_ffn_post, m_lru_conv_b, m_lru_lambda,
               m_ffn_conv_b, m_conv_short_w[0], m_lru_conv_w[0], m_lru_ba[0], m_lru_bx[0], m_ffn_conv_w[0]]
    small_v = [v_norm_mix_pre, v_norm_mix_post, v_norm_ffn_pre, v_norm_ffn_post, v_lru_conv_b, v_lru_lambda,
               v_ffn_conv_b, v_conv_short_w[0], v_lru_conv_w[0], v_lru_ba[0], v_lru_bx[0], v_ffn_conv_w[0]]
    s_d, s_m, s_v = _adam_small(small_w, small_g, small_m, small_v)
    for i, name in enumerate(small_names):
        shape = small_w[i].shape if i < 7 else (1,) + small_w[i].shape
        out_g[name] = small_g[i].reshape(shape)
        out_d[name], out_m[name], out_v[name] = s_d[i].reshape(shape), s_m[i].reshape(shape), s_v[i].reshape(shape)

    order = ["norm_mix_pre", "norm_mix_post", "norm_ffn_pre", "norm_ffn_post", "w_in", "conv_short_w", "w_conv_branch",
             "lru_conv_w", "lru_conv_b", "lru_wa", "lru_ba", "lru_wx", "lru_bx", "lru_lambda", "w_lru_branch", "w_out",
             "ffn_w_up", "ffn_conv_w", "ffn_conv_b", "ffn_w_down"]
    return (loss, dx.reshape(1, t, D_MODEL), *[out_g[k] for k in order], *[out_d[k] for k in order],
            *[out_m[k] for k in order], *[out_v[k] for k in order])
```

```python
import functools
import math

import jax
import jax.numpy as jnp
from jax import lax
from jax.experimental import pallas as pl
from jax.experimental.pallas import tpu as pltpu

F32 = jnp.float32
BF16 = jnp.bfloat16
MESH = pl.DeviceIdType.MESH

N_DEV = 8
D_MODEL = 1024
N_HEADS = 4
HEAD_DIM = D_MODEL // N_HEADS
D_FF = 3 * D_MODEL
IN_COLS = 7 * D_MODEL
LRU_C = 8.0
RMS_EPS = 1e-6
ADAM_LR = 0.001
ADAM_B1 = 0.9
ADAM_B2 = 0.999
ADAM_EPS = 1e-08
ADAM_WD = 0.01
ADAM_STEP = 10
GELU_K = math.sqrt(2.0 / math.pi)
GELU_C = 0.044715

LANES = 128
SUBLANES = 8
PAD = SUBLANES
VMEM_LIMIT = 56 * 1024 * 1024
CB = 256
ROW_SLICE = 32

HBM_SPEC = pl.BlockSpec(memory_space=pltpu.HBM)
SEM_SPEC = pl.BlockSpec(memory_space=pltpu.SEMAPHORE)
DATAFLOW_EFFECT = pltpu.SideEffectType.DATAFLOW_SIDE_EFFECTING
VMEM_SPEC = pl.BlockSpec(memory_space=pltpu.VMEM)


def _params(*sem):
    if sem:
        return pltpu.CompilerParams(dimension_semantics=sem, vmem_limit_bytes=VMEM_LIMIT)
    return pltpu.CompilerParams(vmem_limit_bytes=VMEM_LIMIT)


def _row_chunk(t):
    return min(256, t)


def _row_block(rows, cap):
    return next(rb for rb in range(min(cap, rows), 0, -16) if rows % rb == 0)


def _gelu(x):
    return 0.5 * x * (1.0 + jnp.tanh(GELU_K * (x + GELU_C * x * x * x)))


def _gelu_and_grad(x):
    t = jnp.tanh(GELU_K * (x + GELU_C * x * x * x))
    g = 0.5 * x * (1.0 + t)
    dg = 0.5 * (1.0 + t) + 0.5 * x * (1.0 - t * t) * GELU_K * (1.0 + 3.0 * GELU_C * x * x)
    return g, dg


def _expm1_neg(x):
    series = x * (1.0 + x * (0.5 + x * (1.0 / 6.0 + x * (1.0 / 24.0 + x * (1.0 / 120.0)))))
    return jnp.where(x > -0.05, series, jnp.exp(x) - 1.0)


def _log_sigmoid(x):
    return jnp.minimum(x, 0.0) - jnp.log1p(jnp.exp(-jnp.abs(x)))


def _dot(a, b):
    return jnp.dot(a, b, preferred_element_type=F32)


def _dot_nt(a, b):
    return lax.dot_general(a, b, (((1,), (1,)), ((), ())), preferred_element_type=F32)


def _dot_tn(a, b):
    return lax.dot_general(a, b, (((0,), (0,)), ((), ())), preferred_element_type=F32)


def _rms_fwd(x):
    r = lax.rsqrt(jnp.mean(x * x, axis=-1, keepdims=True) + RMS_EPS)
    return x * r, r


def _rms_bwd(n, r, gdy):
    return r * (gdy - n * jnp.mean(n * gdy, axis=-1, keepdims=True))


def _rows_back(pad_ref, r0, rows, j):
    cur = pad_ref[pl.ds(PAD + r0, rows), :]
    if j == 0:
        return cur
    before = pad_ref[pl.ds(PAD + r0 - SUBLANES, SUBLANES), :]
    row = lax.broadcasted_iota(jnp.int32, before.shape, 0)
    rolled = pltpu.roll(cur, j, 0)
    top = jnp.where(row < j, pltpu.roll(before, j, 0), rolled[0:SUBLANES, :])
    return jnp.concatenate([top, rolled[SUBLANES:, :]], axis=0)


def _rows_ahead(pad_ref, r0, rows, j):
    cur = pad_ref[pl.ds(r0, rows), :]
    if j == 0:
        return cur
    after = pad_ref[pl.ds(r0 + rows, SUBLANES), :]
    row = lax.broadcasted_iota(jnp.int32, after.shape, 0)
    rolled = pltpu.roll(cur, rows - j, 0)
    bottom = jnp.where(row >= SUBLANES - j, pltpu.roll(after, SUBLANES - j, 0), rolled[rows - SUBLANES:, :])
    return jnp.concatenate([rolled[:rows - SUBLANES, :], bottom], axis=0)


def _fold_rows(v):
    return v.reshape(v.shape[0] // SUBLANES, SUBLANES, v.shape[1]).sum(axis=0)


def _conv_causal(pad_ref, w, r0, rows, taps):
    acc = None
    for k in range(taps):
        term = w[k:k + 1, :] * _rows_back(pad_ref, r0, rows, taps - 1 - k)
        acc = term if acc is None else acc + term
    return acc


def _conv_anticausal(pad_ref, w, r0, rows, taps):
    acc = None
    for k in range(taps):
        term = w[k:k + 1, :] * _rows_ahead(pad_ref, r0, rows, taps - 1 - k)
        acc = term if acc is None else acc + term
    return acc


def _conv_wgrad(g, xpad_ref, r0, rows, taps):
    return [jnp.sum(g * _rows_back(xpad_ref, r0, rows, taps - 1 - k), axis=0, keepdims=True) for k in range(taps)]


def _position():
    return lax.axis_index("x"), lax.axis_index("y"), lax.axis_index("c")


def _block_of(x, y, c):
    return 4 * x + 2 * y + c


def _chip(x, y, k):
    return (x + (k & 1)) % 2, (y + (k >> 1)) % 2


def _cols(width):
    def at(ref, d, half=None):
        cols = pl.ds(pl.multiple_of(d * width, LANES), width)
        if half is None:
            return ref.at[:, cols]
        return ref.at[pl.ds(half * (ref.shape[0] // 2), ref.shape[0] // 2), cols]
    return at


def _rows(height):
    def at(ref, d, half=None):
        if half is None:
            return ref.at[pl.ds(pl.multiple_of(d * height, 16), height), :]
        return ref.at[pl.ds(pl.multiple_of(d * height + half * (height // 2), 16), height // 2), :]
    return at


def _lead(ref, d, half=None):
    if half is None:
        return ref.at[d]
    return ref.at[d, pl.ds(half * (ref.shape[1] // 2), ref.shape[1] // 2)]


def _gather_weights(shards, blocks, full_shapes, small, n_now, tokens, gain):
    n = len(shards)
    small_rows = small.shape[0]
    t = tokens.shape[0]
    rc = min(512, t)

    def body(*refs):
        ins, small_in, x_ref, g_ref = refs[:n], refs[n], refs[n + 1], refs[n + 2]
        outs, small_out, proj_ref, h_ref = refs[n + 3:2 * n + 3], refs[2 * n + 3], refs[2 * n + 4], refs[2 * n + 5]
        stage = refs[2 * n + 6:3 * n + 6]
        w_buf, p_buf, send, recv, local, w_sem, p_sem = refs[3 * n + 6:]
        x, y, c = _position()
        me = _block_of(x, y, c)
        sibling = (x, y, 1 - c)

        for a in range(n):
            stage[a][...] = ins[a][...].astype(BF16)
        for r0 in range(0, t, rc):
            normed, _ = _rms_fwd(x_ref[pl.ds(r0, rc), :])
            h_ref[pl.ds(r0, rc), :] = (normed * g_ref[...]).astype(BF16)
        stores = []

        def project(w_ref, block):
            i = len(stores)
            if i >= 2:
                stores[i - 2].wait()
            for r0 in range(0, t, rc):
                p_buf[i % 2, pl.ds(r0, rc), :] = _dot(h_ref[pl.ds(r0, rc), :], w_ref[...]).astype(BF16)
            st = pltpu.make_async_copy(p_buf.at[i % 2], blocks[0](proj_ref, block), p_sem.at[i % 2])
            st.start()
            stores.append(st)

        def project_landed(block):
            ld = pltpu.make_async_copy(blocks[0](outs[0], block), w_buf, w_sem)
            ld.start()
            ld.wait()
            project(w_buf, block)

        def copy(a, k, block, to, src=None, half=None):
            dst = blocks[a](outs[a], block, half)
            return pltpu.make_async_remote_copy(
                src_ref=dst if src is None else src, dst_ref=dst, send_sem=send.at[a, k], recv_sem=recv.at[a, k],
                device_id=to, device_id_type=MESH)

        def small_copy(k):
            px, py, pc = (x + (k & 1)) % 2, (y + ((k >> 1) & 1)) % 2, (c + (k >> 2)) % 2
            return pltpu.make_async_remote_copy(
                src_ref=small_in, dst_ref=small_out.at[me], send_sem=send.at[n_now, k - 1], recv_sem=recv.at[n_now, k - 1],
                device_id=(px, py, pc), device_id_type=MESH)

        def small_arrival(k):
            px, py, pc = (x + (k & 1)) % 2, (y + ((k >> 1) & 1)) % 2, (c + (k >> 2)) % 2
            return pltpu.make_async_remote_copy(
                src_ref=small_in, dst_ref=small_out.at[_block_of(px, py, pc)], send_sem=send.at[n_now, k - 1],
                recv_sem=recv.at[n_now, k - 1], device_id=(px, py, pc), device_id_type=MESH)

        small_out[me] = small_in[...]
        small_sends = [small_copy(k) for k in range(1, N_DEV)]
        for cp in small_sends:
            cp.start()

        here = (x, y, c)
        across = [(*_chip(x, y, k), c) for k in (1, 2)]
        near = [[_block_of(*_chip(x, y, k), cc) for k in (1, 2)] for cc in (c, 1 - c)]
        far = [_block_of(*_chip(x, y, 3), cc) for cc in (c, 1 - c)]
        mine, first, passed = [], [], []
        for a in range(n):
            own = pltpu.make_async_copy(stage[a], blocks[a](outs[a], me), local.at[a])
            own.start()
            mine.append(own)
            if a >= n_now:
                continue
            rows = stage[a].shape[0] // 2
            halves = [stage[a].at[pl.ds(i * rows, rows)] for i in (0, 1)]
            sends = [copy(a, 0, me, sibling, src=stage[a])]
            sends += [copy(a, 1 + i, me, across[0], src=halves[i], half=i) for i in (0, 1)]
            sends += [copy(a, 3 + i, me, across[1], src=halves[1 - i], half=1 - i) for i in (0, 1)]
            for cp in sends:
                cp.start()
            first += sends

        def launch(cp):
            cp.start()
            passed.append(cp)

        project(stage[0], me)
        copy(0, 0, _block_of(x, y, 1 - c), here).wait_recv()
        project_landed(_block_of(x, y, 1 - c))
        for a in range(n_now):
            copy(a, 1, near[0][0], here, half=0).wait_recv()
            launch(copy(a, 5, near[0][0], across[1], half=0))
            copy(a, 3, near[0][1], here, half=1).wait_recv()
            launch(copy(a, 6, near[0][1], across[0], half=1))
        for a in range(n_now):
            copy(a, 2, near[0][0], here, half=1).wait_recv()
            launch(copy(a, 7, near[0][0], sibling))
            copy(a, 4, near[0][1], here, half=0).wait_recv()
            launch(copy(a, 8, near[0][1], sibling))
            if a == 0:
                project_landed(near[0][0])
                project_landed(near[0][1])
        for i in (0, 1):
            copy(0, 7 + i, near[1][i], here).wait_recv()
            project_landed(near[1][i])
        for a in range(n_now):
            for i in (0, 1):
                copy(a, 5 + i, far[0], here, half=i).wait_recv()
                launch(copy(a, 9 + i, far[0], sibling, half=i))
            if a == 0:
                project_landed(far[0])
        for a in range(n_now):
            if a > 0:
                copy(a, 0, _block_of(x, y, 1 - c), here).wait_recv()
                for i in (0, 1):
                    copy(a, 7 + i, near[1][i], here).wait_recv()
            for i in (0, 1):
                copy(a, 9 + i, far[1], here, half=i).wait_recv()
            if a == 0:
                project_landed(far[1])
        for k in range(1, N_DEV):
            small_arrival(k).wait_recv()
        for cp in first + passed + small_sends:
            cp.wait_send()
        for done in mine + stores[-2:]:
            done.wait()

    out_shape = [jax.ShapeDtypeStruct(s, BF16) for s in full_shapes]
    out_shape += [jax.ShapeDtypeStruct((N_DEV, small_rows, LANES), F32), jax.ShapeDtypeStruct((t, full_shapes[0][1]), BF16),
                  jax.ShapeDtypeStruct(tokens.shape, BF16)]
    return pl.pallas_call(
        body, name="gather_weights", out_shape=out_shape,
        in_specs=[VMEM_SPEC] * (n + 3), out_specs=[HBM_SPEC] * n + [VMEM_SPEC, HBM_SPEC, VMEM_SPEC],
        scratch_shapes=[pltpu.VMEM(s.shape, BF16) for s in shards]
        + [pltpu.VMEM(shards[0].shape, BF16), pltpu.VMEM((2, t, shards[0].shape[1]), BF16),
           pltpu.SemaphoreType.DMA((n_now + 1, 11)), pltpu.SemaphoreType.DMA((n_now + 1, 11)),
           pltpu.SemaphoreType.DMA((n,)), pltpu.SemaphoreType.DMA(()), pltpu.SemaphoreType.DMA((2,))],
        compiler_params=_params(),
    )(*shards, small, tokens, gain)


def _gather_first(full, blocks, send, recv):
    x, y, c = _position()
    me = _block_of(x, y, c)
    peers = [(x, y, 1 - c)] + [(*_chip(x, y, k), c) for k in (1, 2, 3)]

    def copy(a, k, block):
        at = blocks[a](full[a], block)
        return pltpu.make_async_remote_copy(src_ref=at, dst_ref=at, send_sem=send[4 * a + k], recv_sem=recv[4 * a + k],
                                            device_id=peers[k], device_id_type=MESH)

    sends = [copy(a, k, me) for a in range(len(full)) for k in range(4)]
    arrivals = [copy(a, k, _block_of(*peers[k])) for a in range(len(full)) for k in range(4)]
    return sends, arrivals


def _gather_second(full, blocks, send, recv):
    x, y, c = _position()

    def copy(a, k, cc):
        at = blocks[a](full[a], _block_of(*_chip(x, y, k), cc))
        return pltpu.make_async_remote_copy(src_ref=at, dst_ref=at, send_sem=send[3 * a + k - 1],
                                            recv_sem=recv[3 * a + k - 1], device_id=(x, y, 1 - c), device_id_type=MESH)

    sends = [copy(a, k, c) for a in range(len(full)) for k in (1, 2, 3)]
    arrivals = [copy(a, k, 1 - c) for a in range(len(full)) for k in (1, 2, 3)]
    return sends, arrivals


def _split_call(body, name, arrays, sems_in, n_sems_out, after=None, token=False):
    n, m = len(arrays), len(sems_in)

    def kernel_body(*refs):
        outs = refs[n + m + (after is not None):]
        body(refs[:n], refs[n:n + m], outs[:n_sems_out])
        if token:
            outs[-1][...] = jnp.zeros_like(outs[-1])

    extra_in = [] if after is None else [after]
    outs = pl.pallas_call(
        kernel_body, name=name,
        out_shape=(*[pltpu.SemaphoreType.DMA(())] * n_sems_out, *[pltpu.HBM(a.shape, a.dtype) for a in arrays],
                   *([jax.ShapeDtypeStruct((SUBLANES, LANES), F32)] if token else [])),
        in_specs=[HBM_SPEC] * n + [SEM_SPEC] * m + [pl.BlockSpec(memory_space=pl.ANY)] * len(extra_in),
        out_specs=(*[SEM_SPEC] * n_sems_out, *[HBM_SPEC] * n, *([VMEM_SPEC] if token else [])),
        input_output_aliases={i: n_sems_out + i for i in range(n)},
        compiler_params=pltpu.CompilerParams(has_side_effects=DATAFLOW_EFFECT),
    )(*[pltpu.with_memory_space_constraint(a, pltpu.HBM) for a in arrays], *sems_in, *extra_in)
    sems, rest = list(outs[:n_sems_out]), list(outs[n_sems_out:])
    return (sems, rest[:n], rest[n]) if token else (sems, rest[:n])


def _gather_start(full, blocks, name):
    n = len(full)

    def body(arrays, _, sems):
        for cp in _gather_first(arrays, blocks, sems[:4 * n], sems[4 * n:])[0]:
            cp.start()

    sems, arrays, token = _split_call(body, name, full, [], 8 * n, token=True)
    return sems[:4 * n], sems[4 * n:], arrays, token


def _gather_forward(full, blocks, send_first, recv_first, after, name):
    n = len(full)

    def body(arrays, sems_in, sems):
        sends, arrivals = _gather_first(arrays, blocks, sems_in[:4 * n], sems_in[4 * n:])
        for cp in arrivals:
            cp.wait_recv()
        for cp in _gather_second(arrays, blocks, sems[:3 * n], sems[3 * n:])[0]:
            cp.start()
        for cp in sends:
            cp.wait_send()

    sems, arrays = _split_call(body, name, full, [*send_first, *recv_first], 6 * n, after=after)
    return sems[:3 * n], sems[3 * n:], arrays


def _gather_finish(full, blocks, send_second, recv_second, after, name):
    n = len(full)

    def body(arrays, sems_in, _):
        sends, arrivals = _gather_second(arrays, blocks, sems_in[:3 * n], sems_in[3 * n:])
        for cp in sends:
            cp.wait_send()
        for cp in arrivals:
            cp.wait_recv()

    return _split_call(body, name, full, [*send_second, *recv_second], 0, after=after)[1]


def _reduce_pair(grads, blocks, shard_shapes, name):
    n = len(grads)

    def body(*refs):
        ins, outs = refs[:n], refs[n:2 * n]
        got, own = refs[2 * n:3 * n], refs[3 * n:4 * n]
        send, recv, local = refs[4 * n:]
        x, y, c = _position()
        copies, loads = [], []
        for a in range(n):
            for k in range(4):
                chip = _chip(x, y, k)
                cp = pltpu.make_async_remote_copy(
                    src_ref=blocks[a](ins[a], _block_of(*chip, 1 - c)), dst_ref=got[a].at[k],
                    send_sem=send.at[a, k], recv_sem=recv.at[a, k], device_id=(x, y, 1 - c), device_id_type=MESH)
                cp.start()
                copies.append(cp)
                ld = pltpu.make_async_copy(blocks[a](ins[a], _block_of(*chip, c)), own[a].at[k], local.at[a, k])
                ld.start()
                loads.append(ld)
        for a in range(n):
            for k in range(4):
                loads[4 * a + k].wait()
                copies[4 * a + k].wait_recv()
                outs[a][k] = (own[a][k].astype(F32) + got[a][k].astype(F32)).astype(BF16)
        for cp in copies:
            cp.wait_send()

    slots = [(4,) + tuple(s) for s in shard_shapes]
    return pl.pallas_call(
        body, name=name, out_shape=[jax.ShapeDtypeStruct(s, BF16) for s in slots],
        in_specs=[HBM_SPEC] * n, out_specs=[VMEM_SPEC] * n,
        scratch_shapes=[pltpu.VMEM(s, BF16) for s in slots] * 2
        + [pltpu.SemaphoreType.DMA((n, 4)), pltpu.SemaphoreType.DMA((n, 4)), pltpu.SemaphoreType.DMA((n, 4))],
        compiler_params=_params(),
    )(*grads)


def _chip_copies(sums, lands, send, recv):
    x, y, c = _position()
    return [pltpu.make_async_remote_copy(
        src_ref=sums[a].at[k], dst_ref=lands[a].at[k - 1], send_sem=send[3 * a + k - 1], recv_sem=recv[3 * a + k - 1],
        device_id=(*_chip(x, y, k), c), device_id_type=MESH) for a in range(len(sums)) for k in (1, 2, 3)]


def _exchange_chips_start(pair_sums, name):
    n = len(pair_sums)
    lands = [pltpu.with_memory_space_constraint(lax.empty((3,) + tuple(p.shape[1:]), BF16), pltpu.HBM) for p in pair_sums]

    def body(*refs):
        sums, zones = refs[:n], refs[n:2 * n]
        send, recv = refs[2 * n:5 * n], refs[5 * n:8 * n]
        token = refs[-1]
        for cp in _chip_copies(sums, zones, send, recv):
            cp.start()
        token[...] = jnp.zeros_like(token)

    outs = pl.pallas_call(
        body, name=name,
        out_shape=(*[pltpu.SemaphoreType.DMA(())] * (6 * n),
                   *[pltpu.HBM(p.shape, BF16) for p in pair_sums], *[pltpu.HBM(z.shape, BF16) for z in lands],
                   jax.ShapeDtypeStruct((SUBLANES, LANES), F32)),
        in_specs=[HBM_SPEC] * (2 * n), out_specs=(*[SEM_SPEC] * (6 * n), *[HBM_SPEC] * (2 * n), VMEM_SPEC),
        input_output_aliases={i: 6 * n + i for i in range(2 * n)},
        compiler_params=pltpu.CompilerParams(has_side_effects=DATAFLOW_EFFECT),
    )(*[pltpu.with_memory_space_constraint(p, pltpu.HBM) for p in pair_sums], *lands)
    return outs[:3 * n], outs[3 * n:6 * n], outs[6 * n:7 * n], outs[7 * n:8 * n], outs[-1]


def _exchange_chips_wait(send, recv, sums, lands, after, name):
    n = len(sums)

    def body(*refs):
        sums_in, zones = refs[:n], refs[n:2 * n]
        send_in, recv_in = refs[2 * n:5 * n], refs[5 * n:8 * n]
        for cp in _chip_copies(sums_in, zones, send_in, recv_in):
            cp.wait_send()
            cp.wait_recv()

    outs = pl.pallas_call(
        body, name=name,
        out_shape=(*[pltpu.HBM(p.shape, BF16) for p in sums], *[pltpu.HBM(z.shape, BF16) for z in lands]),
        in_specs=[HBM_SPEC] * (2 * n) + [SEM_SPEC] * (6 * n) + [pl.BlockSpec(memory_space=pl.ANY)],
        out_specs=[HBM_SPEC] * (2 * n), input_output_aliases={i: i for i in range(2 * n)},
        compiler_params=pltpu.CompilerParams(has_side_effects=DATAFLOW_EFFECT),
    )(*sums, *lands, *send, *recv, after)
    return outs[:n], outs[n:]


def _small_copies(mine, land, send, recv):
    x, y, c = _position()
    me = _block_of(x, y, c)

    def peer(k):
        return (x + (k & 1)) % 2, (y + ((k >> 1) & 1)) % 2, (c + (k >> 2)) % 2

    def copy(k, slot):
        return pltpu.make_async_remote_copy(src_ref=mine, dst_ref=land.at[slot], send_sem=send[k - 1], recv_sem=recv[k - 1],
                                            device_id=peer(k), device_id_type=MESH)

    return [copy(k, me) for k in range(1, N_DEV)], [copy(k, _block_of(*peer(k))) for k in range(1, N_DEV)]


def _small_start(part, name):
    land = jnp.zeros((N_DEV,) + part.shape, F32)

    def body(arrays, _, sems):
        for cp in _small_copies(arrays[0], arrays[1], sems[:7], sems[7:])[0]:
            cp.start()

    sems, arrays, token = _split_call(body, name, [part, land], [], 14, token=True)
    return sems[:7], sems[7:], arrays[0], arrays[1], token


def _small_wait(send, recv, part, land, after, name):
    def body(arrays, sems_in, _):
        sends, arrivals = _small_copies(arrays[0], arrays[1], sems_in[:7], sems_in[7:])
        for cp in sends:
            cp.wait_send()
        for cp in arrivals:
            cp.wait_recv()

    return _split_call(body, name, [part, land], [*send, *recv], 0, after=after)[1]


def _small_sum(pairs, me):
    n = len(pairs)

    def body(me_ref, *refs):
        for i in range(n):
            mine, land, out = refs[2 * i], refs[2 * i + 1], refs[2 * n + i]
            total = jnp.zeros(mine.shape, F32)
            for d in range(N_DEV):
                total = total + land[d] + jnp.where(me_ref[0] == d, mine[...], 0.0)
            out[...] = total

    flat = [a for pair in pairs for a in pair]
    return pl.pallas_call(
        body, name="small_sum", out_shape=[jax.ShapeDtypeStruct(mine.shape, F32) for mine, _ in pairs],
        in_specs=[pl.BlockSpec(memory_space=pltpu.SMEM)] + [VMEM_SPEC] * (2 * n), out_specs=[VMEM_SPEC] * n,
        compiler_params=_params(),
    )(me.reshape(1).astype(jnp.int32), *flat)


def _section(s, t):
    return pl.BlockSpec((t, CB), lambda h, s=s: (0, s * (D_MODEL // CB) + h))


def _conv_mixer_fwd(proj, w_short):
    t = proj.shape[0]
    rc = _row_chunk(t)

    def body(b_ref, c_ref, x_ref, w_ref, y_ref, pad):
        pad[pl.ds(0, PAD), :] = jnp.zeros((PAD, CB), F32)
        for r0 in range(0, t, rc):
            rows = pl.ds(r0, rc)
            pad[pl.ds(PAD + r0, rc), :] = c_ref[rows, :].astype(F32) * x_ref[rows, :].astype(F32)
        w = w_ref[...]
        for r0 in range(0, t, rc):
            rows = pl.ds(r0, rc)
            y_ref[rows, :] = (b_ref[rows, :].astype(F32) * _conv_causal(pad, w, r0, rc, 3)).astype(BF16)

    return pl.pallas_call(
        body, name="conv_mixer_fwd", grid=(D_MODEL // CB,),
        out_shape=jax.ShapeDtypeStruct((t, D_MODEL), BF16),
        in_specs=[_section(0, t), _section(1, t), _section(2, t), pl.BlockSpec((3, CB), lambda h: (0, h))],
        out_specs=pl.BlockSpec((t, CB), lambda h: (0, h)),
        scratch_shapes=[pltpu.VMEM((t + PAD, CB), F32)],
        compiler_params=_params("parallel"),
    )(proj, proj, proj, w_short)


def _lru_gates(xl, wa, ba, wx, bx, ls, first_row):
    xb = xl.astype(BF16)
    ra = jax.nn.sigmoid(_dot(xb, wa) + ba)
    ia = jax.nn.sigmoid(_dot(xb, wx) + bx)
    la = LRU_C * ra * ls
    a = jnp.exp(la)
    one_minus = -_expm1_neg(2.0 * la)
    mult = jnp.where(first_row, 1.0, jnp.sqrt(one_minus))
    return xb, ra, ia, a, one_minus, mult


def _head_specs():
    vec = pl.BlockSpec((1, CB), lambda h: (0, h))
    mat = pl.BlockSpec((N_DEV, None, HEAD_DIM // N_DEV, HEAD_DIM), lambda h: (0, h, 0, 0))
    return vec, mat


def _lru_fwd(proj, w_conv, b_conv, wa, ba, wx, bx, lam):
    t = proj.shape[0]
    rc = _row_chunk(t)
    vec, mat = _head_specs()

    def body(lx_ref, ly_ref, wc_ref, bc_ref, wa_ref, ba_ref, wx_ref, bx_ref, lam_ref, yb_ref, hl_ref, a_ref, kept_ref,
             pad, u_s):
        pad[pl.ds(0, PAD), :] = jnp.zeros((PAD, CB), F32)
        for r0 in range(0, t, rc):
            pad[pl.ds(PAD + r0, rc), :] = lx_ref[pl.ds(r0, rc), :].astype(F32)
        wc, bc = wc_ref[...], bc_ref[...]
        wa_m, wx_m = wa_ref[...].reshape(HEAD_DIM, HEAD_DIM), wx_ref[...].reshape(HEAD_DIM, HEAD_DIM)
        ls = _log_sigmoid(lam_ref[...])
        for r0 in range(0, t, rc):
            rows = pl.ds(r0, rc)
            xl = _conv_causal(pad, wc, r0, rc, 4) + bc
            first = (lax.broadcasted_iota(jnp.int32, (rc, CB), 0) + r0) == 0
            xb, ra, ia, a, _, mult = _lru_gates(xl, wa_m, ba_ref[...], wx_m, bx_ref[...], ls, first)
            a_ref[rows, :] = a
            u_s[rows, :] = mult * (ia * xl)
            kept_ref[0, rows, :] = xb
            kept_ref[1, rows, :] = ra.astype(BF16)
            kept_ref[2, rows, :] = ia.astype(BF16)

        row = lax.broadcasted_iota(jnp.int32, (SUBLANES, CB), 0)

        def group(g, carry):
            r = pl.multiple_of(g * SUBLANES, SUBLANES)
            a_g, b_g = a_ref[pl.ds(r, SUBLANES), :], u_s[pl.ds(r, SUBLANES), :]
            for s in (1, 2, 4):
                keep = row >= s
                b_g = jnp.where(keep, a_g * pltpu.roll(b_g, s, 0) + b_g, b_g)
                a_g = jnp.where(keep, a_g * pltpu.roll(a_g, s, 0), a_g)
            h_g = b_g + a_g * carry
            hl_ref[pl.ds(r, SUBLANES), :] = h_g
            return jnp.broadcast_to(h_g[SUBLANES - 1:SUBLANES, :], (SUBLANES, CB))

        lax.fori_loop(0, t // SUBLANES, group, jnp.zeros((SUBLANES, CB), F32))
        for r0 in range(0, t, rc):
            rows = pl.ds(r0, rc)
            yb_ref[rows, :] = (hl_ref[rows, :] * _gelu(ly_ref[rows, :].astype(F32))).astype(BF16)

    blk = pl.BlockSpec((t, CB), lambda h: (0, h))
    res = jax.ShapeDtypeStruct((t, D_MODEL), F32)
    return pl.pallas_call(
        body, name="lru_fwd", grid=(N_HEADS,),
        out_shape=[jax.ShapeDtypeStruct((t, D_MODEL), BF16), res, res, jax.ShapeDtypeStruct((3, t, D_MODEL), BF16)],
        in_specs=[_section(3, t), _section(4, t), pl.BlockSpec((4, CB), lambda h: (0, h)), vec, mat, vec, mat, vec, vec],
        out_specs=[blk, blk, blk, pl.BlockSpec((3, t, CB), lambda h: (0, 0, h))],
        scratch_shapes=[pltpu.VMEM((t + PAD, CB), F32), pltpu.VMEM((t, CB), F32)],
        compiler_params=_params("parallel"),
    )(proj, proj, w_conv, b_conv, wa, ba, wx, bx, lam)


def _merge(y_a, y_b, proj, x, w_cb, w_lb, w_out, g2, g3):
    t = x.shape[0]
    tm = min(512, t)

    def body(ya_ref, yb_ref, gc_ref, gl_ref, x_ref, wcb_ref, wlb_ref, wo_ref, g2_ref, g3_ref,
             pa_ref, pb_ref, mg_ref, mix_ref, x1_ref, h2_ref):
        pa = _dot(ya_ref[...], wcb_ref[...]).astype(BF16)
        pb = _dot(yb_ref[...], wlb_ref[...]).astype(BF16)
        pa_ref[...] = pa
        pb_ref[...] = pb
        merged = (jax.nn.sigmoid(gc_ref[...].astype(F32)) * pa.astype(F32)
                  + jax.nn.sigmoid(gl_ref[...].astype(F32)) * pb.astype(F32)).astype(BF16)
        mg_ref[...] = merged
        mix = _dot(merged, wo_ref[...])
        mix_ref[...] = mix
        n2, _ = _rms_fwd(mix)
        x1 = x_ref[...] + n2 * g2_ref[...]
        x1_ref[...] = x1
        n3, _ = _rms_fwd(x1)
        h2_ref[...] = (n3 * g3_ref[...]).astype(BF16)

    row = pl.BlockSpec((tm, D_MODEL), lambda i: (i, 0))
    full = pl.BlockSpec((D_MODEL, D_MODEL), lambda i: (0, 0))
    vec = pl.BlockSpec((1, D_MODEL), lambda i: (0, 0))
    act = jax.ShapeDtypeStruct((t, D_MODEL), BF16)
    res = jax.ShapeDtypeStruct((t, D_MODEL), F32)
    return pl.pallas_call(
        body, name="merge_fwd", grid=(t // tm,), out_shape=[act, act, act, res, res, act],
        in_specs=[row, row, pl.BlockSpec((tm, D_MODEL), lambda i: (i, 5)), pl.BlockSpec((tm, D_MODEL), lambda i: (i, 6)),
                  row, full, full, full, vec, vec],
        out_specs=[row] * 6,
        compiler_params=_params("parallel"),
    )(y_a, y_b, proj, proj, x, w_cb, w_lb, w_out, g2, g3)


N_FF_BLOCKS = D_FF // CB
FFN_BWD_COLS = 512


def _ffn_up(h2, w_up, w_conv, b_conv):
    t = h2.shape[0]
    rc = _row_chunk(t)
    nb = N_FF_BLOCKS

    def body(h_ref, w_ref, c_ref, b_ref, up_ref, act_ref, f_ref, pad, gate):
        k = pl.program_id(1)
        pad[pl.ds(0, PAD), :] = jnp.zeros((PAD, CB), F32)
        for r0 in range(0, t, rc):
            rows = pl.ds(r0, rc)
            up = _dot(h_ref[rows, :], w_ref[...]).astype(BF16)
            up_ref[rows, :] = up
            pad[pl.ds(PAD + r0, rc), :] = up.astype(F32)
        cw = c_ref[...]
        for r0 in range(0, t, rc):
            rows = pl.ds(r0, rc)
            act = _conv_causal(pad, cw, r0, rc, 3) + b_ref[...]
            act_ref[rows, :] = act.astype(BF16)

            @pl.when(k == 0)
            def _():
                gate[rows, :] = act

            @pl.when(k == 1)
            def _():
                f_ref[rows, :] = (_gelu(gate[rows, :]) * act).astype(BF16)

    half = lambda rows: pl.BlockSpec((rows, CB), lambda j, k: (0, nb * k + j))
    wide = jax.ShapeDtypeStruct((t, 2 * D_FF), BF16)
    return pl.pallas_call(
        body, name="ffn_up_fwd", grid=(nb, 2), out_shape=[wide, wide, jax.ShapeDtypeStruct((t, D_FF), BF16)],
        in_specs=[pl.BlockSpec((t, D_MODEL), lambda j, k: (0, 0)), half(D_MODEL), half(3), half(1)],
        out_specs=[half(t), half(t), pl.BlockSpec((t, CB), lambda j, k: (0, j))],
        scratch_shapes=[pltpu.VMEM((t + PAD, CB), F32), pltpu.VMEM((t, CB), F32)],
        compiler_params=_params("parallel", "arbitrary"),
    )(h2, w_up, w_conv, b_conv)


def _ffn_down(f, act, w_down, x1, target, g4):
    t = f.shape[0]
    tm = min(256, t)
    cc = 512

    def body(f_ref, act_ref, w_ref, x1_ref, tg_ref, g_ref, dy_ref, dout_ref, back_ref, dg_ref, loss_ref):
        @pl.when(pl.program_id(0) == 0)
        def _():
            dg_ref[...] = jnp.zeros_like(dg_ref)
            loss_ref[...] = jnp.zeros_like(loss_ref)
        out = _dot(f_ref[...], w_ref[...])
        n4, r4 = _rms_fwd(out)
        err = x1_ref[...] + n4 * g_ref[...] - tg_ref[...]
        loss_ref[...] += jnp.full(loss_ref.shape, 0.5 / D_MODEL, F32) * jnp.sum(err * err)
        dy = err * (1.0 / D_MODEL)
        dy_ref[...] = dy
        dg_ref[...] += jnp.sum(dy * n4, axis=0, keepdims=True)
        d_out = _rms_bwd(n4, r4, dy * g_ref[...]).astype(BF16)
        dout_ref[...] = d_out
        for c0 in range(0, D_FF, cc):
            d_f = _dot_nt(d_out, w_ref[pl.ds(c0, cc), :])
            gelu, d_gelu = _gelu_and_grad(act_ref[:, pl.ds(c0, cc)].astype(F32))
            val = act_ref[:, pl.ds(D_FF + c0, cc)].astype(F32)
            back_ref[:, pl.ds(c0, cc)] = (d_f * val * d_gelu).astype(BF16)
            back_ref[:, pl.ds(D_FF + c0, cc)] = (d_f * gelu).astype(BF16)

    row = pl.BlockSpec((tm, D_MODEL), lambda i: (i, 0))
    wide = pl.BlockSpec((tm, 2 * D_FF), lambda i: (i, 0))
    vec = pl.BlockSpec((1, D_MODEL), lambda i: (0, 0))
    return pl.pallas_call(
        body, name="ffn_down_fwd_bwd", grid=(t // tm,),
        out_shape=[jax.ShapeDtypeStruct((t, D_MODEL), F32), jax.ShapeDtypeStruct((t, D_MODEL), BF16),
                   jax.ShapeDtypeStruct((t, 2 * D_FF), BF16), jax.ShapeDtypeStruct((1, D_MODEL), F32),
                   jax.ShapeDtypeStruct((SUBLANES, LANES), F32)],
        in_specs=[pl.BlockSpec((tm, D_FF), lambda i: (i, 0)), wide, pl.BlockSpec((D_FF, D_MODEL), lambda i: (0, 0)),
                  row, row, vec],
        out_specs=[row, row, wide, vec, pl.BlockSpec((SUBLANES, LANES), lambda i: (0, 0))],
        compiler_params=_params("arbitrary"),
    )(f, act, w_down, x1, target, g4)


def _grad_tn(a, b, bm, name):
    t, m = a.shape
    n = b.shape[1]

    def body(a_ref, b_ref, o_ref):
        o_ref[...] = _dot_tn(a_ref[...], b_ref[...]).astype(BF16)

    return pl.pallas_call(
        body, name=name, grid=(m // bm,), out_shape=jax.ShapeDtypeStruct((m, n), BF16),
        in_specs=[pl.BlockSpec((t, bm), lambda i: (0, i)), pl.BlockSpec((t, n), lambda i: (0, 0))],
        out_specs=pl.BlockSpec((bm, n), lambda i: (i, 0)),
        compiler_params=_params("parallel"),
    )(a, b)


def _ffn_up_bwd(up, back, w_conv, h2, w_up):
    t = h2.shape[0]
    rc = _row_chunk(t)
    cb = FFN_BWD_COLS

    def body(up_ref, back_ref, c_ref, h_ref, w_ref, dw_ref, dcw_ref, dcb_ref, dh_ref, pad, after, d_up):
        @pl.when(pl.program_id(0) == 0)
        def _():
            dh_ref[...] = jnp.zeros_like(dh_ref)
        pad[pl.ds(0, PAD), :] = jnp.zeros((PAD, cb), F32)
        after[pl.ds(t, PAD), :] = jnp.zeros((PAD, cb), F32)
        for r0 in range(0, t, rc):
            pad[pl.ds(PAD + r0, rc), :] = up_ref[pl.ds(r0, rc), :].astype(F32)
            after[pl.ds(r0, rc), :] = back_ref[pl.ds(r0, rc), :].astype(F32)
        cw = c_ref[...]
        taps = [jnp.zeros((SUBLANES, cb), F32)] * 3
        bias = jnp.zeros((SUBLANES, cb), F32)
        for r0 in range(0, t, rc):
            for q0 in range(r0, r0 + rc, ROW_SLICE):
                rows = pl.ds(q0, ROW_SLICE)
                d_up[rows, :] = _conv_anticausal(after, cw, q0, ROW_SLICE, 3).astype(BF16)
                g = after[rows, :]
                taps = [acc + _fold_rows(g * _rows_back(pad, q0, ROW_SLICE, 2 - k)) for k, acc in enumerate(taps)]
                bias = bias + _fold_rows(g)
            rows = pl.ds(r0, rc)
            dh_ref[rows, :] += _dot_nt(d_up[rows, :], w_ref[...])
        dw_ref[...] = _dot_tn(h_ref[...], d_up[...]).astype(BF16)
        dcw_ref[...] = jnp.concatenate([jnp.sum(acc, axis=0, keepdims=True) for acc in taps], axis=0)
        dcb_ref[...] = jnp.sum(bias, axis=0, keepdims=True)

    cols = lambda rows: pl.BlockSpec((rows, cb), lambda j: (0, j))
    whole = pl.BlockSpec((t, D_MODEL), lambda j: (0, 0))
    return pl.pallas_call(
        body, name="ffn_up_bwd", grid=(2 * D_FF // cb,),
        out_shape=[jax.ShapeDtypeStruct((D_MODEL, 2 * D_FF), BF16), jax.ShapeDtypeStruct((3, 2 * D_FF), F32),
                   jax.ShapeDtypeStruct((1, 2 * D_FF), F32), jax.ShapeDtypeStruct((t, D_MODEL), F32)],
        in_specs=[cols(t), cols(t), cols(3), whole, cols(D_MODEL)],
        out_specs=[cols(D_MODEL), cols(3), cols(1), whole],
        scratch_shapes=[pltpu.VMEM((t + PAD, cb), F32), pltpu.VMEM((t + PAD, cb), F32), pltpu.VMEM((t, cb), BF16)],
        compiler_params=_params("arbitrary"),
    )(up, back, w_conv, h2, w_up)


def _merge_bwd(dy, d_h2, x1, mix, g3, g2, w_out, w_cb, w_lb, pa, pb, proj):
    t = dy.shape[0]
    tm = min(256, t)

    def body(dy_ref, dh2_ref, x1_ref, mix_ref, g3_ref, g2_ref, wo_ref, wcb_ref, wlb_ref, pa_ref, pb_ref, gc_ref, gl_ref,
             dx1_ref, dmix_ref, dpa_ref, dpb_ref, dya_ref, dyb_ref, dgate_ref, dg3_ref, dg2_ref):
        @pl.when(pl.program_id(0) == 0)
        def _():
            dg3_ref[...] = jnp.zeros_like(dg3_ref)
            dg2_ref[...] = jnp.zeros_like(dg2_ref)
        n3, r3 = _rms_fwd(x1_ref[...])
        d_h2 = dh2_ref[...]
        dg3_ref[...] += jnp.sum(d_h2 * n3, axis=0, keepdims=True)
        dx1 = dy_ref[...] + _rms_bwd(n3, r3, d_h2 * g3_ref[...])
        dx1_ref[...] = dx1
        n2, r2 = _rms_fwd(mix_ref[...])
        dg2_ref[...] += jnp.sum(dx1 * n2, axis=0, keepdims=True)
        d_mix = _rms_bwd(n2, r2, dx1 * g2_ref[...]).astype(BF16)
        dmix_ref[...] = d_mix
        d_merged = _dot_nt(d_mix, wo_ref[...])
        sc = jax.nn.sigmoid(gc_ref[...].astype(F32))
        sl = jax.nn.sigmoid(gl_ref[...].astype(F32))
        d_pa = (d_merged * sc).astype(BF16)
        d_pb = (d_merged * sl).astype(BF16)
        dpa_ref[...] = d_pa
        dpb_ref[...] = d_pb
        dgate_ref[0] = (d_merged * pa_ref[...].astype(F32) * sc * (1.0 - sc)).astype(BF16)
        dgate_ref[1] = (d_merged * pb_ref[...].astype(F32) * sl * (1.0 - sl)).astype(BF16)
        dya_ref[...] = _dot_nt(d_pa, wcb_ref[...]).astype(BF16)
        dyb_ref[...] = _dot_nt(d_pb, wlb_ref[...]).astype(BF16)

    row = pl.BlockSpec((tm, D_MODEL), lambda i: (i, 0))
    full = pl.BlockSpec((D_MODEL, D_MODEL), lambda i: (0, 0))
    vec = pl.BlockSpec((1, D_MODEL), lambda i: (0, 0))
    act = jax.ShapeDtypeStruct((t, D_MODEL), BF16)
    small = jax.ShapeDtypeStruct((1, D_MODEL), F32)
    return pl.pallas_call(
        body, name="merge_bwd", grid=(t // tm,),
        out_shape=[jax.ShapeDtypeStruct((t, D_MODEL), F32), act, act, act, act, act,
                   jax.ShapeDtypeStruct((2, t, D_MODEL), BF16), small, small],
        in_specs=[row, row, row, row, vec, vec, full, full, full, row, row,
                  pl.BlockSpec((tm, D_MODEL), lambda i: (i, 5)), pl.BlockSpec((tm, D_MODEL), lambda i: (i, 6))],
        out_specs=[row] * 6 + [pl.BlockSpec((2, tm, D_MODEL), lambda i: (0, i, 0)), vec, vec],
        compiler_params=_params("arbitrary"),
    )(dy, d_h2, x1, mix, g3, g2, w_out, w_cb, w_lb, pa, pb, proj, proj)


def _conv_mixer_bwd(proj, d_ya, w_short):
    t = proj.shape[0]
    rc = _row_chunk(t)

    def body(b_ref, c_ref, x_ref, dy_ref, w_ref, d_ref, dw_ref, pad, back):
        pad[pl.ds(0, PAD), :] = jnp.zeros((PAD, CB), F32)
        back[pl.ds(t, PAD), :] = jnp.zeros((PAD, CB), F32)
        for r0 in range(0, t, rc):
            rows = pl.ds(r0, rc)
            pad[pl.ds(PAD + r0, rc), :] = c_ref[rows, :].astype(F32) * x_ref[rows, :].astype(F32)
        w = w_ref[...]
        for r0 in range(0, t, rc):
            rows = pl.ds(r0, rc)
            d_y = dy_ref[rows, :].astype(F32)
            d_ref[0, rows, :] = (d_y * _conv_causal(pad, w, r0, rc, 3)).astype(BF16)
            back[rows, :] = d_y * b_ref[rows, :].astype(F32)
        taps = [jnp.zeros((1, CB), F32)] * 3
        for r0 in range(0, t, rc):
            rows = pl.ds(r0, rc)
            d_u = _conv_anticausal(back, w, r0, rc, 3)
            d_ref[1, rows, :] = (d_u * x_ref[rows, :].astype(F32)).astype(BF16)
            d_ref[2, rows, :] = (d_u * c_ref[rows, :].astype(F32)).astype(BF16)
            taps = [acc + new for acc, new in zip(taps, _conv_wgrad(back[rows, :], pad, r0, rc, 3))]
        dw_ref[...] = jnp.concatenate(taps, axis=0)

    blk = pl.BlockSpec((t, CB), lambda h: (0, h))
    return pl.pallas_call(
        body, name="conv_mixer_bwd", grid=(D_MODEL // CB,),
        out_shape=[jax.ShapeDtypeStruct((3, t, D_MODEL), BF16), jax.ShapeDtypeStruct((3, D_MODEL), F32)],
        in_specs=[_section(0, t), _section(1, t), _section(2, t), blk, pl.BlockSpec((3, CB), lambda h: (0, h))],
        out_specs=[pl.BlockSpec((3, t, CB), lambda h: (0, 0, h)), pl.BlockSpec((3, CB), lambda h: (0, h))],
        scratch_shapes=[pltpu.VMEM((t + PAD, CB), F32), pltpu.VMEM((t + PAD, CB), F32)],
        compiler_params=_params("parallel"),
    )(proj, proj, proj, d_ya, w_short)


LRU_SMALL_ROWS = 8


def _lru_bwd(proj, hl, a_all, kept, d_yb, w_conv, wa, wx, lam):
    t = proj.shape[0]
    rc = _row_chunk(t)
    vec, mat = _head_specs()

    def body(lx_ref, ly_ref, hl_ref, a_ref, kept_ref, dy_ref, wc_ref, wa_ref, wx_ref, lam_ref,
             d_ref, dwa_ref, dwx_ref, small_ref, pad, a_next, dh_s, h_prev, back, acc_a, acc_x, dz_a, dz_x):
        zeros = jnp.zeros((PAD, CB), F32)
        pad[pl.ds(0, PAD), :] = zeros
        h_prev[pl.ds(0, PAD), :] = zeros
        a_next[pl.ds(t, PAD), :] = zeros
        back[pl.ds(t, PAD), :] = zeros
        for r0 in range(0, t, ROW_SLICE):
            rows = pl.ds(r0, ROW_SLICE)
            pad[pl.ds(PAD + r0, ROW_SLICE), :] = lx_ref[rows, :].astype(F32)
            h_prev[pl.ds(PAD + r0, ROW_SLICE), :] = hl_ref[rows, :]
            a_next[pl.ds(PAD - 1 + r0, ROW_SLICE), :] = a_ref[rows, :]
            act, d_act = _gelu_and_grad(ly_ref[rows, :].astype(F32))
            d_y = dy_ref[rows, :].astype(F32)
            dh_s[rows, :] = d_y * act
            d_ref[1, rows, :] = (d_y * hl_ref[rows, :] * d_act).astype(BF16)
        wc = wc_ref[...]
        wa_m, wx_m = wa_ref[...].reshape(HEAD_DIM, HEAD_DIM), wx_ref[...].reshape(HEAD_DIM, HEAD_DIM)
        ls = _log_sigmoid(lam_ref[...])

        row = lax.broadcasted_iota(jnp.int32, (SUBLANES, CB), 0)
        groups = t // SUBLANES

        def group(i, carry):
            r = pl.multiple_of((groups - 1 - i) * SUBLANES, SUBLANES)
            a_g, b_g = a_next[pl.ds(PAD + r, SUBLANES), :], dh_s[pl.ds(r, SUBLANES), :]
            for s in (1, 2, 4):
                keep = row < SUBLANES - s
                b_g = jnp.where(keep, a_g * pltpu.roll(b_g, SUBLANES - s, 0) + b_g, b_g)
                a_g = jnp.where(keep, a_g * pltpu.roll(a_g, SUBLANES - s, 0), a_g)
            d_g = b_g + a_g * carry
            dh_s[pl.ds(r, SUBLANES), :] = d_g
            return jnp.broadcast_to(d_g[0:1, :], (SUBLANES, CB))

        lax.fori_loop(0, groups, group, jnp.zeros((SUBLANES, CB), F32))

        acc_a[...] = jnp.zeros_like(acc_a)
        acc_x[...] = jnp.zeros_like(acc_x)
        d_ba = d_bx = d_ls = jnp.zeros((SUBLANES, CB), F32)
        for r0 in range(0, t, rc):
            for q0 in range(r0, r0 + rc, ROW_SLICE):
                rows, local = pl.ds(q0, ROW_SLICE), pl.ds(q0 - r0, ROW_SLICE)
                a = a_ref[rows, :]
                xl, ra, ia = (kept_ref[i, rows, :].astype(F32) for i in range(3))
                a_sq = a * a
                mult = jnp.sqrt(1.0 - a_sq)
                slope = -a_sq / mult
                if q0 == 0:
                    first = lax.broadcasted_iota(jnp.int32, (ROW_SLICE, CB), 0) == 0
                    mult, slope = jnp.where(first, 1.0, mult), jnp.where(first, 0.0, slope)
                d_h = dh_s[rows, :]
                d_la = d_h * _rows_back(h_prev, q0, ROW_SLICE, 1) * a + d_h * ia * xl * slope
                d_za = d_la * (LRU_C * ls) * ra * (1.0 - ra)
                d_zx = d_h * mult * xl * ia * (1.0 - ia)
                d_ls = d_ls + _fold_rows(d_la * ra)
                d_ba = d_ba + _fold_rows(d_za)
                d_bx = d_bx + _fold_rows(d_zx)
                dz_a[local, :] = d_za.astype(BF16)
                dz_x[local, :] = d_zx.astype(BF16)
                back[rows, :] = d_h * mult * ia
            rows = pl.ds(r0, rc)
            xb = kept_ref[0, rows, :]
            acc_a[...] += _dot_tn(xb, dz_a[...])
            acc_x[...] += _dot_tn(xb, dz_x[...])
            back[rows, :] += _dot_nt(dz_a[...], wa_m) + _dot_nt(dz_x[...], wx_m)
        taps = [jnp.zeros((SUBLANES, CB), F32)] * 4
        d_bc = jnp.zeros((SUBLANES, CB), F32)
        for q0 in range(0, t, ROW_SLICE):
            rows = pl.ds(q0, ROW_SLICE)
            d_ref[0, rows, :] = _conv_anticausal(back, wc, q0, ROW_SLICE, 4).astype(BF16)
            g = back[rows, :]
            taps = [acc + _fold_rows(g * _rows_back(pad, q0, ROW_SLICE, 3 - k)) for k, acc in enumerate(taps)]
            d_bc = d_bc + _fold_rows(g)
        d_lam = d_ls * LRU_C * jax.nn.sigmoid(-lam_ref[...])
        small_ref[...] = jnp.concatenate(
            [jnp.sum(v, axis=0, keepdims=True) for v in taps + [d_bc, d_ba, d_bx, d_lam]], axis=0)
        dwa_ref[...] = acc_a[...].reshape(N_DEV, HEAD_DIM // N_DEV, HEAD_DIM).astype(BF16)
        dwx_ref[...] = acc_x[...].reshape(N_DEV, HEAD_DIM // N_DEV, HEAD_DIM).astype(BF16)

    blk = pl.BlockSpec((t, CB), lambda h: (0, h))
    gate_grad = jax.ShapeDtypeStruct((N_DEV, N_HEADS, HEAD_DIM // N_DEV, HEAD_DIM), BF16)
    return pl.pallas_call(
        body, name="lru_bwd", grid=(N_HEADS,),
        out_shape=[jax.ShapeDtypeStruct((2, t, D_MODEL), BF16), gate_grad, gate_grad,
                   jax.ShapeDtypeStruct((LRU_SMALL_ROWS, D_MODEL), F32)],
        in_specs=[_section(3, t), _section(4, t), blk, blk, pl.BlockSpec((3, t, CB), lambda h: (0, 0, h)), blk,
                  pl.BlockSpec((4, CB), lambda h: (0, h)), mat, mat, vec],
        out_specs=[pl.BlockSpec((2, t, CB), lambda h: (0, 0, h)), mat, mat,
                   pl.BlockSpec((LRU_SMALL_ROWS, CB), lambda h: (0, h))],
        scratch_shapes=[pltpu.VMEM((t + PAD, CB), F32), pltpu.VMEM((t + PAD, CB), F32), pltpu.VMEM((t, CB), F32),
                        pltpu.VMEM((t + PAD, CB), F32), pltpu.VMEM((t + PAD, CB), F32),
                        pltpu.VMEM((HEAD_DIM, HEAD_DIM), F32), pltpu.VMEM((HEAD_DIM, HEAD_DIM), F32),
                        pltpu.VMEM((rc, CB), BF16), pltpu.VMEM((rc, CB), BF16)],
        compiler_params=_params("parallel"),
    )(proj, proj, hl, a_all, kept, d_yb, w_conv, wa, wx, lam)


def _stack_maps(halves):
    def conv(sec, part):
        return jnp.minimum(sec, 2), jnp.where(sec < 3, part, halves - 1)

    def lru(sec, part):
        return jnp.clip(sec - 3, 0, 1), jnp.where(sec < 3, 0, jnp.where(sec < 5, part, halves - 1))

    def gate(sec, part):
        return jnp.clip(sec - 5, 0, 1), jnp.where(sec < 5, 0, part)

    return conv, lru, gate


def _pick_stack(sec, refs, fn):
    @pl.when(sec < 3)
    def _():
        fn(refs[0])

    @pl.when((sec >= 3) & (sec < 5))
    def _():
        fn(refs[1])

    @pl.when(sec >= 5)
    def _():
        fn(refs[2])


def _in_proj_wgrad(h, d_conv, d_lru, d_gate):
    t = h.shape[0]
    halves, bn = 1, D_MODEL
    maps = _stack_maps(halves)

    def body(h_ref, dc_ref, dl_ref, dg_ref, o_ref):
        def emit(ref):
            o_ref[...] = _dot_tn(h_ref[...], ref[...]).astype(BF16)
        _pick_stack(pl.program_id(0) // halves, (dc_ref, dl_ref, dg_ref), emit)

    def spec(m):
        def index(s):
            stack, part = m(s // halves, s % halves)
            return stack, 0, part
        return pl.BlockSpec((None, t, bn), index)

    return pl.pallas_call(
        body, name="in_proj_wgrad", grid=(7 * halves,), out_shape=jax.ShapeDtypeStruct((D_MODEL, IN_COLS), BF16),
        in_specs=[pl.BlockSpec((t, D_MODEL), lambda s: (0, 0))] + [spec(m) for m in maps],
        out_specs=pl.BlockSpec((D_MODEL, bn), lambda s: (0, s)),
        compiler_params=_params("arbitrary"),
    )(h, d_conv, d_lru, d_gate)


def _in_proj_xgrad(d_conv, d_lru, d_gate, w_in, x, dx1, g1):
    t = x.shape[0]
    tm = min(1024, t)
    maps = _stack_maps(1)

    def body(dc_ref, dl_ref, dg_ref, w_ref, x_ref, dx1_ref, g_ref, dx_ref, dgain_ref, acc):
        i, s = pl.program_id(0), pl.program_id(1)

        @pl.when((i == 0) & (s == 0))
        def _():
            dgain_ref[...] = jnp.zeros_like(dgain_ref)

        @pl.when(s == 0)
        def _():
            acc[...] = jnp.zeros_like(acc)

        def add(ref):
            acc[...] += _dot_nt(ref[...], w_ref[...])
        _pick_stack(s, (dc_ref, dl_ref, dg_ref), add)

        @pl.when(s == 6)
        def _():
            n1, r1 = _rms_fwd(x_ref[...])
            d_h = acc[...]
            dgain_ref[...] += jnp.sum(d_h * n1, axis=0, keepdims=True)
            dx_ref[...] = dx1_ref[...] + _rms_bwd(n1, r1, d_h * g_ref[...])

    def spec(m):
        def index(i, s):
            return m(s, 0)[0], i, 0
        return pl.BlockSpec((None, tm, D_MODEL), index)

    row = pl.BlockSpec((tm, D_MODEL), lambda i, s: (i, 0))
    vec = pl.BlockSpec((1, D_MODEL), lambda i, s: (0, 0))
    return pl.pallas_call(
        body, name="in_proj_xgrad", grid=(t // tm, 7),
        out_shape=[jax.ShapeDtypeStruct((t, D_MODEL), F32), jax.ShapeDtypeStruct((1, D_MODEL), F32)],
        in_specs=[spec(m) for m in maps] + [pl.BlockSpec((D_MODEL, D_MODEL), lambda i, s: (0, s)), row, row, vec],
        out_specs=[row, vec],
        scratch_shapes=[pltpu.VMEM((tm, D_MODEL), F32)],
        compiler_params=_params("arbitrary", "arbitrary"),
    )(d_conv, d_lru, d_gate, w_in, x, dx1, g1)


def _adamw(w, g, m, v):
    m = ADAM_B1 * m + (1.0 - ADAM_B1) * g
    v = ADAM_B2 * v + (1.0 - ADAM_B2) * (g * g)
    m_hat = m / (1.0 - ADAM_B1 ** ADAM_STEP)
    v_hat = v / (1.0 - ADAM_B2 ** ADAM_STEP)
    return -ADAM_LR * (m_hat / (jnp.sqrt(v_hat) + ADAM_EPS) + ADAM_WD * w), m, v


def _adam_large(w, m, v, own, others, name):
    shape = w.shape
    cols = shape[-1]
    w2, m2, v2 = (a.reshape(-1, cols) for a in (w, m, v))
    rows = w2.shape[0]
    own, others = own.reshape(4, rows, cols), others.reshape(3, rows, cols)
    rb = _row_block(rows, 512)

    def body(w_ref, m_ref, v_ref, own_ref, oth_ref, g_ref, d_ref, nm_ref, nv_ref):
        g = own_ref[...].astype(F32)
        for k in range(3):
            g = g + oth_ref[k].astype(F32)
        g_ref[...] = g
        d_ref[...], nm_ref[...], nv_ref[...] = _adamw(w_ref[...], g, m_ref[...], v_ref[...])

    blk = pl.BlockSpec((rb, cols), lambda i: (i, 0))
    res = jax.ShapeDtypeStruct((rows, cols), F32)
    outs = pl.pallas_call(
        body, name=name, grid=(rows // rb,), out_shape=[res] * 4,
        in_specs=[blk, blk, blk, pl.BlockSpec((None, rb, cols), lambda i: (0, i, 0)),
                  pl.BlockSpec((3, rb, cols), lambda i: (0, i, 0))],
        out_specs=[blk] * 4, compiler_params=_params("parallel"),
    )(w2, m2, v2, own, others)
    return [o.reshape(shape) for o in outs]


def _adam_small(ws, gs, ms, vs):
    n = len(ws)

    def body(*refs):
        w_refs, g_refs, m_refs, v_refs = (refs[i * n:(i + 1) * n] for i in range(4))
        outs = refs[4 * n:]
        for i in range(n):
            d, m, v = _adamw(w_refs[i][...], g_refs[i][...], m_refs[i][...], v_refs[i][...])
            outs[i][...], outs[n + i][...], outs[2 * n + i][...] = d, m, v

    shapes = [jax.ShapeDtypeStruct(w.shape, F32) for w in ws]
    outs = pl.pallas_call(
        body, name="adam_small", out_shape=shapes * 3,
        in_specs=[VMEM_SPEC] * (4 * n), out_specs=[VMEM_SPEC] * (3 * n), compiler_params=_params(),
    )(*ws, *gs, *ms, *vs)
    return outs[:n], outs[n:2 * n], outs[2 * n:]


def _pack_rows(pieces):
    tile = SUBLANES * LANES
    return jnp.concatenate([jnp.pad(p.reshape(-1), (0, (-p.size) % tile)).reshape(-1, LANES) for p in pieces], axis=0)


def _packed_starts(sizes):
    tile = SUBLANES * LANES
    starts = [0]
    for s in sizes:
        starts.append(starts[-1] + (s + tile - 1) // tile * SUBLANES)
    return starts


def kernel(x, norm_mix_pre, norm_mix_post, norm_ffn_pre, norm_ffn_post, w_in, conv_short_w, w_conv_branch, lru_conv_w, lru_conv_b, lru_wa, lru_ba, lru_wx, lru_bx, lru_lambda, w_lru_branch, w_out, ffn_w_up, ffn_conv_w, ffn_conv_b, ffn_w_down, loss_target, m_norm_mix_pre, m_norm_mix_post, m_norm_ffn_pre, m_norm_ffn_post, m_w_in, m_conv_short_w, m_w_conv_branch, m_lru_conv_w, m_lru_conv_b, m_lru_wa, m_lru_ba, m_lru_wx, m_lru_bx, m_lru_lambda, m_w_lru_branch, m_w_out, m_ffn_w_up, m_ffn_conv_w, m_ffn_conv_b, m_ffn_w_down, v_norm_mix_pre, v_norm_mix_post, v_norm_ffn_pre, v_norm_ffn_post, v_w_in, v_conv_short_w, v_w_conv_branch, v_lru_conv_w, v_lru_conv_b, v_lru_wa, v_lru_ba, v_lru_wx, v_lru_bx, v_lru_lambda, v_w_lru_branch, v_w_out, v_ffn_w_up, v_ffn_conv_w, v_ffn_conv_b, v_ffn_w_down):
    t = x.shape[1]
    xi, yi, ci = _position()
    me = _block_of(xi, yi, ci)
    x2, target = x[0], loss_target[0]
    shard_in, shard_up = IN_COLS // N_DEV, 2 * D_FF // N_DEV
    shard_sq, shard_down, shard_head = D_MODEL // N_DEV, D_FF // N_DEV, HEAD_DIM // N_DEV

    names = ["w_in", "lru_wa", "lru_wx", "w_conv_branch", "w_lru_branch", "w_out", "ffn_w_up", "ffn_w_down"]
    large = [w_in[0], lru_wa[0], lru_wx[0], w_conv_branch[0], w_lru_branch[0], w_out[0], ffn_w_up[0], ffn_w_down[0]]
    blocks = [_cols(shard_in), _lead, _lead, _rows(shard_sq), _rows(shard_sq), _rows(shard_sq),
              _cols(shard_up), _rows(shard_down)]
    gate_full = (N_DEV, N_HEADS, shard_head, HEAD_DIM)
    full_shapes = [(D_MODEL, IN_COLS), gate_full, gate_full, (D_MODEL, D_MODEL), (D_MODEL, D_MODEL), (D_MODEL, D_MODEL),
                   (D_MODEL, 2 * D_FF), (D_FF, D_MODEL)]
    n_now = 3
    small_sharded = [conv_short_w, lru_conv_w, lru_ba, lru_bx, ffn_conv_w]
    small_mine = _pack_rows(small_sharded)
    small_at = _packed_starts([p.size for p in small_sharded])
    *gathered, small_all, proj, h = _gather_weights(large, blocks, full_shapes, small_mine, n_now, x2, norm_mix_pre)
    g_in, g_wa, g_wx = gathered[:n_now]
    later_blocks = blocks[n_now:]
    send1, recv1, later, gather_token = _gather_start(gathered[n_now:], later_blocks, "gather_start")

    def behind(token, operand):
        return operand + token[0:1, 0:1]

    def forward(lo, hi, after, tag):
        return _gather_forward(later[lo:hi], later_blocks[lo:hi], send1[4 * lo:4 * hi], recv1[4 * lo:4 * hi], after,
                               "gather_forward_" + tag)

    def finish(lo, hi, flight, after, tag):
        return _gather_finish(flight[2], later_blocks[lo:hi], flight[0], flight[1], after, "gather_finish_" + tag)

    def cols_of(r0, n, width):
        part = small_all[:, r0:r0 + n * width // LANES, :].reshape(N_DEV, n, width)
        return part.transpose(1, 0, 2).reshape(n, N_DEV * width)

    c_short = cols_of(small_at[0], 3, LANES)
    c_lru = cols_of(small_at[1], 4, LANES)
    b_a = cols_of(small_at[2], N_HEADS, shard_head).reshape(1, D_MODEL)
    b_x = cols_of(small_at[3], N_HEADS, shard_head).reshape(1, D_MODEL)
    c_ffn = cols_of(small_at[4], 3, shard_up)

    y_a = _conv_mixer_fwd(proj, behind(gather_token, c_short))
    y_b, hl, decay, lru_kept = _lru_fwd(proj, behind(gather_token, c_lru), lru_conv_b, g_wa, b_a, g_wx, b_x, lru_lambda)
    flight_mix_w = forward(0, 3, y_b, "mix")
    g_cb, g_lb, g_out = finish(0, 3, flight_mix_w, y_b, "mix")
    pa, pb, merged, mix, x1, h2 = _merge(y_a, y_b, proj, x2, g_cb, g_lb, g_out, norm_mix_post, norm_ffn_pre)
    flight_up_w = forward(3, 4, h2, "up")
    (g_up,) = finish(3, 4, flight_up_w, h2, "up")
    up, act, f = _ffn_up(h2, g_up, c_ffn, ffn_conv_b)
    flight_down_w = forward(4, 5, f, "down")
    (g_down,) = finish(4, 5, flight_down_w, f, "down")
    dy, d_out, d_act, dg4, loss_part = _ffn_down(f, act, g_down, x1, target, norm_ffn_post)

    block_of = dict(zip(names, blocks))
    shard_shapes = {"w_in": (D_MODEL, shard_in), "w_conv_branch": (shard_sq, D_MODEL), "w_lru_branch": (shard_sq, D_MODEL),
                    "w_out": (shard_sq, D_MODEL), "lru_wa": (N_HEADS, shard_head, HEAD_DIM),
                    "lru_wx": (N_HEADS, shard_head, HEAD_DIM), "ffn_w_up": (D_MODEL, shard_up),
                    "ffn_w_down": (shard_down, D_MODEL)}

    def reduce_start(tag, grads):
        keys = list(grads)
        sums = _reduce_pair([grads[k] for k in keys], [block_of[k] for k in keys], [shard_shapes[k] for k in keys],
                            "reduce_pair_" + tag)
        return (keys,) + _exchange_chips_start(sums, "reduce_chip_start_" + tag)

    gw_down = _grad_tn(f, d_out, min(512, D_FF), "ffn_down_wgrad")
    flight_down = reduce_start("down", {"ffn_w_down": gw_down})
    gw_up, gc_ffn, gb_ffn, d_h2 = _ffn_up_bwd(up, d_act, behind(flight_down[-1], c_ffn), h2, g_up)
    flight_up = reduce_start("up", {"ffn_w_up": gw_up})
    dx1, d_mix, d_pa, d_pb, d_ya, d_yb, d_gate, dg3, dg2 = _merge_bwd(
        dy, d_h2, x1, mix, behind(flight_up[-1], norm_ffn_pre), norm_mix_post, g_out, g_cb, g_lb, pa, pb, proj)
    gw_out = _grad_tn(merged, d_mix, CB, "w_out_wgrad")
    gw_cb = _grad_tn(y_a, d_pa, CB, "w_conv_branch_wgrad")
    gw_lb = _grad_tn(y_b, d_pb, CB, "w_lru_branch_wgrad")
    flight_mix = reduce_start("mix", {"w_conv_branch": gw_cb, "w_lru_branch": gw_lb, "w_out": gw_out})
    d_conv, gc_short = _conv_mixer_bwd(proj, d_ya, behind(flight_mix[-1], c_short))
    d_lru, gw_a, gw_x, g_lru_small = _lru_bwd(proj, hl, decay, lru_kept, d_yb, c_lru, g_wa, g_wx, lru_lambda)
    early = [dg2, dg3, dg4, g_lru_small[4:5], g_lru_small[7:8], gb_ffn, gc_short, g_lru_small[0:4],
             g_lru_small[5:6], g_lru_small[6:7], gc_ffn, loss_part]
    flight_small = _small_start(_pack_rows(early), "small_start")
    gw_in = _in_proj_wgrad(h, d_conv, d_lru, d_gate)
    flight_in = reduce_start("in", {"lru_wa": gw_a, "lru_wx": gw_x, "w_in": gw_in})
    dx, dg1 = _in_proj_xgrad(d_conv, d_lru, d_gate, g_in, x2, dx1,
                             behind(flight_small[-1], behind(flight_in[-1], norm_mix_pre)))
    flight_late = _small_start(_pack_rows([dg1]), "small_start_late")

    moments ={"w_in": (m_w_in, v_w_in), "w_conv_branch": (m_w_conv_branch, v_w_conv_branch),
               "w_lru_branch": (m_w_lru_branch, v_w_lru_branch), "w_out": (m_w_out, v_w_out),
               "lru_wa": (m_lru_wa, v_lru_wa), "lru_wx": (m_lru_wx, v_lru_wx), "ffn_w_up": (m_ffn_w_up, v_ffn_w_up),
               "ffn_w_down": (m_ffn_w_down, v_ffn_w_down)}
    weights = {"w_in": w_in, "w_conv_branch": w_conv_branch, "w_lru_branch": w_lru_branch, "w_out": w_out,
               "lru_wa": lru_wa, "lru_wx": lru_wx, "ffn_w_up": ffn_w_up, "ffn_w_down": ffn_w_down}
    out_g, out_d, out_m, out_v = {}, {}, {}, {}

    after = flight_late[-1]
    for tag, (keys, send, recv, sums, lands, _) in (("down", flight_down), ("up", flight_up), ("mix", flight_mix),
                                                    ("in", flight_in)):
        sums, others = _exchange_chips_wait(send, recv, sums, lands, after, "reduce_chip_wait_" + tag)
        for k, own, oth in zip(keys, sums, others):
            out_g[k], out_d[k], out_m[k], out_v[k] = _adam_large(weights[k], *moments[k], own, oth, "adam_" + k)
        after = out_d[keys[-1]]

    total, total_late = _small_sum([_small_wait(*flight_small[:4], after, "small_wait"),
                                    _small_wait(*flight_late[:4], after, "small_wait_late")], me)
    sizes = [p.size for p in early]
    starts = _packed_starts(sizes)

    def piece(i, shape):
        if i == 0:
            return total_late.reshape(-1)[:D_MODEL].reshape(shape)
        return total[starts[i - 1]:starts[i]].reshape(-1)[:sizes[i - 1]].reshape(shape)

    loss = total[starts[11], 0]

    def col_shard(full, width):
        return lax.dynamic_slice_in_dim(full, me * width, width, axis=1)

    def head_shard(full):
        return lax.dynamic_slice_in_dim(full.reshape(N_HEADS, HEAD_DIM), me * shard_head, shard_head, axis=1)

    small_names = ["norm_mix_pre", "norm_mix_post", "norm_ffn_pre", "norm_ffn_post", "lru_conv_b", "lru_lambda",
                   "ffn_conv_b", "conv_short_w", "lru_conv_w", "lru_ba", "lru_bx", "ffn_conv_w"]
    small_g = [piece(0, (1, D_MODEL)), piece(1, (1, D_MODEL)), piece(2, (1, D_MODEL)), piece(3, (1, D_MODEL)),
               piece(4, (1, D_MODEL)), piece(5, (1, D_MODEL)), piece(6, (1, 2 * D_FF)),
               col_shard(piece(7, (3, D_MODEL)), LANES), col_shard(piece(8, (4, D_MODEL)), LANES),
               head_shard(piece(9, (1, D_MODEL))), head_shard(piece(10, (1, D_MODEL))),
               col_shard(piece(11, (3, 2 * D_FF)), shard_up)]
    small_w = [norm_mix_pre, norm_mix_post, norm_ffn_pre, norm_ffn_post, lru_conv_b, lru_lambda, ffn_conv_b,
               conv_short_w[0], lru_conv_w[0], lru_ba[0], lru_bx[0], ffn_conv_w[0]]
    small_m = [m_norm_mix_pre, m_norm_mix_post, m_norm_ffn_pre, m_norm_ffn_post, m_lru_conv_b, m_lru_lambda,
               m_ffn_conv_b, m_conv_short_w[0], m_lru_conv_w[0], m_lru_ba[0], m_lru_bx[0], m_ffn_conv_w[0]]
    small_v = [v_norm_mix_pre, v_norm_mix_post, v_norm_ffn_pre, v_norm_ffn_post, v_lru_conv_b, v_lru_lambda,
               v_ffn_conv_b, v_conv_short_w[0], v_lru_conv_w[0], v_lru_ba[0], v_lru_bx[0], v_ffn_conv_w[0]]
    s_d, s_m, s_v = _adam_small(small_w, small_g, small_m, small_v)
    for i, name in enumerate(small_names):
        shape = small_w[i].shape if i < 7 else (1,) + small_w[i].shape
        out_g[name] = small_g[i].reshape(shape)
        out_d[name], out_m[name], out_v[name] = s_d[i].reshape(shape), s_m[i].reshape(shape), s_v[i].reshape(shape)

    order = ["norm_mix_pre", "norm_mix_post", "norm_ffn_pre", "norm_ffn_post", "w_in", "conv_short_w", "w_conv_branch",
             "lru_conv_w", "lru_conv_b", "lru_wa", "lru_ba", "lru_wx", "lru_bx", "lru_lambda", "w_lru_branch", "w_out",
             "ffn_w_up", "ffn_conv_w", "ffn_conv_b", "ffn_w_down"]
    return (loss, dx.reshape(1, t, D_MODEL), *[out_g[k] for k in order], *[out_d[k] for k in order],
            *[out_m[k] for k in order], *[out_v[k] for k in order])
```

```python
import functools
import math

import jax
import jax.numpy as jnp
from jax import lax
from jax.experimental import pallas as pl
from jax.experimental.pallas import tpu as pltpu

F32 = jnp.float32
BF16 = jnp.bfloat16
MESH = pl.DeviceIdType.MESH

N_DEV = 8
D_MODEL = 1024
N_HEADS = 4
HEAD_DIM = D_MODEL // N_HEADS
D_FF = 3 * D_MODEL
IN_COLS = 7 * D_MODEL
LRU_C = 8.0
RMS_EPS = 1e-6
ADAM_LR = 0.001
ADAM_B1 = 0.9
ADAM_B2 = 0.999
ADAM_EPS = 1e-08
ADAM_WD = 0.01
ADAM_STEP = 10
GELU_K = math.sqrt(2.0 / math.pi)
GELU_C = 0.044715

LANES = 128
SUBLANES = 8
PAD = SUBLANES
VMEM_LIMIT = 56 * 1024 * 1024
CB = 256
ROW_SLICE = 32
SCAN_UNROLL = 4

HBM_SPEC = pl.BlockSpec(memory_space=pltpu.HBM)
SEM_SPEC = pl.BlockSpec(memory_space=pltpu.SEMAPHORE)
DATAFLOW_EFFECT = pltpu.SideEffectType.DATAFLOW_SIDE_EFFECTING
VMEM_SPEC = pl.BlockSpec(memory_space=pltpu.VMEM)


def _params(*sem):
    if sem:
        return pltpu.CompilerParams(dimension_semantics=sem, vmem_limit_bytes=VMEM_LIMIT)
    return pltpu.CompilerParams(vmem_limit_bytes=VMEM_LIMIT)


def _row_chunk(t):
    return min(256, t)


def _row_block(rows, cap):
    return next(rb for rb in range(min(cap, rows), 0, -16) if rows % rb == 0)


def _gelu(x):
    return 0.5 * x * (1.0 + jnp.tanh(GELU_K * (x + GELU_C * x * x * x)))


def _gelu_and_grad(x):
    t = jnp.tanh(GELU_K * (x + GELU_C * x * x * x))
    g = 0.5 * x * (1.0 + t)
    dg = 0.5 * (1.0 + t) + 0.5 * x * (1.0 - t * t) * GELU_K * (1.0 + 3.0 * GELU_C * x * x)
    return g, dg


def _expm1_neg(x):
    series = x * (1.0 + x * (0.5 + x * (1.0 / 6.0 + x * (1.0 / 24.0 + x * (1.0 / 120.0)))))
    return jnp.where(x > -0.05, series, jnp.exp(x) - 1.0)


def _log_sigmoid(x):
    return jnp.minimum(x, 0.0) - jnp.log1p(jnp.exp(-jnp.abs(x)))


def _dot(a, b):
    return jnp.dot(a, b, preferred_element_type=F32)


def _dot_nt(a, b):
    return lax.dot_general(a, b, (((1,), (1,)), ((), ())), preferred_element_type=F32)


def _dot_tn(a, b):
    return lax.dot_general(a, b, (((0,), (0,)), ((), ())), preferred_element_type=F32)


def _rms_fwd(x):
    r = lax.rsqrt(jnp.mean(x * x, axis=-1, keepdims=True) + RMS_EPS)
    return x * r, r


def _rms_bwd(n, r, gdy):
    return r * (gdy - n * jnp.mean(n * gdy, axis=-1, keepdims=True))


def _rows_back(pad_ref, r0, rows, j):
    cur = pad_ref[pl.ds(PAD + r0, rows), :]
    if j == 0:
        return cur
    before = pad_ref[pl.ds(PAD + r0 - SUBLANES, SUBLANES), :]
    row = lax.broadcasted_iota(jnp.int32, before.shape, 0)
    rolled = pltpu.roll(cur, j, 0)
    top = jnp.where(row < j, pltpu.roll(before, j, 0), rolled[0:SUBLANES, :])
    return jnp.concatenate([top, rolled[SUBLANES:, :]], axis=0)


def _rows_ahead(pad_ref, r0, rows, j):
    cur = pad_ref[pl.ds(r0, rows), :]
    if j == 0:
        return cur
    after = pad_ref[pl.ds(r0 + rows, SUBLANES), :]
    row = lax.broadcasted_iota(jnp.int32, after.shape, 0)
    rolled = pltpu.roll(cur, rows - j, 0)
    bottom = jnp.where(row >= SUBLANES - j, pltpu.roll(after, SUBLANES - j, 0), rolled[rows - SUBLANES:, :])
    return jnp.concatenate([rolled[:rows - SUBLANES, :], bottom], axis=0)


def _fold_rows(v):
    return v.reshape(v.shape[0] // SUBLANES, SUBLANES, v.shape[1]).sum(axis=0)


def _conv_causal(pad_ref, w, r0, rows, taps):
    acc = None
    for k in range(taps):
        term = w[k:k + 1, :] * _rows_back(pad_ref, r0, rows, taps - 1 - k)
        acc = term if acc is None else acc + term
    return acc


def _conv_anticausal(pad_ref, w, r0, rows, taps):
    acc = None
    for k in range(taps):
        term = w[k:k + 1, :] * _rows_ahead(pad_ref, r0, rows, taps - 1 - k)
        acc = term if acc is None else acc + term
    return acc


def _conv_wgrad(g, xpad_ref, r0, rows, taps):
    return [jnp.sum(g * _rows_back(xpad_ref, r0, rows, taps - 1 - k), axis=0, keepdims=True) for k in range(taps)]


def _position():
    return lax.axis_index("x"), lax.axis_index("y"), lax.axis_index("c")


def _block_of(x, y, c):
    return 4 * x + 2 * y + c


def _chip(x, y, k):
    return (x + (k & 1)) % 2, (y + (k >> 1)) % 2


def _cols(width):
    def at(ref, d, half=None):
        cols = pl.ds(pl.multiple_of(d * width, LANES), width)
        if half is None:
            return ref.at[:, cols]
        return ref.at[pl.ds(half * (ref.shape[0] // 2), ref.shape[0] // 2), cols]
    return at


def _rows(height):
    def at(ref, d, half=None):
        if half is None:
            return ref.at[pl.ds(pl.multiple_of(d * height, 16), height), :]
        return ref.at[pl.ds(pl.multiple_of(d * height + half * (height // 2), 16), height // 2), :]
    return at


def _lead(ref, d, half=None):
    if half is None:
        return ref.at[d]
    return ref.at[d, pl.ds(half * (ref.shape[1] // 2), ref.shape[1] // 2)]


def _gather_weights(shards, blocks, full_shapes, small, n_now, tokens, gain):
    n = len(shards)
    small_rows = small.shape[0]
    t = tokens.shape[0]
    rc = min(512, t)

    def body(*refs):
        ins, small_in, x_ref, g_ref = refs[:n], refs[n], refs[n + 1], refs[n + 2]
        outs, small_out, proj_ref, h_ref = refs[n + 3:2 * n + 3], refs[2 * n + 3], refs[2 * n + 4], refs[2 * n + 5]
        stage = refs[2 * n + 6:3 * n + 6]
        w_buf, p_buf, send, recv, local, w_sem, p_sem = refs[3 * n + 6:]
        x, y, c = _position()
        me = _block_of(x, y, c)
        sibling = (x, y, 1 - c)

        for a in range(n):
            stage[a][...] = ins[a][...].astype(BF16)
        for r0 in range(0, t, rc):
            normed, _ = _rms_fwd(x_ref[pl.ds(r0, rc), :])
            h_ref[pl.ds(r0, rc), :] = (normed * g_ref[...]).astype(BF16)
        stores = []

        def project(w_ref, block):
            i = len(stores)
            if i >= 2:
                stores[i - 2].wait()
            for r0 in range(0, t, rc):
                p_buf[i % 2, pl.ds(r0, rc), :] = _dot(h_ref[pl.ds(r0, rc), :], w_ref[...]).astype(BF16)
            st = pltpu.make_async_copy(p_buf.at[i % 2], blocks[0](proj_ref, block), p_sem.at[i % 2])
            st.start()
            stores.append(st)

        def project_landed(block):
            ld = pltpu.make_async_copy(blocks[0](outs[0], block), w_buf, w_sem)
            ld.start()
            ld.wait()
            project(w_buf, block)

        def copy(a, k, block, to, src=None, half=None):
            dst = blocks[a](outs[a], block, half)
            return pltpu.make_async_remote_copy(
                src_ref=dst if src is None else src, dst_ref=dst, send_sem=send.at[a, k], recv_sem=recv.at[a, k],
                device_id=to, device_id_type=MESH)

        def small_copy(k):
            px, py, pc = (x + (k & 1)) % 2, (y + ((k >> 1) & 1)) % 2, (c + (k >> 2)) % 2
            return pltpu.make_async_remote_copy(
                src_ref=small_in, dst_ref=small_out.at[me], send_sem=send.at[n_now, k - 1], recv_sem=recv.at[n_now, k - 1],
                device_id=(px, py, pc), device_id_type=MESH)

        def small_arrival(k):
            px, py, pc = (x + (k & 1)) % 2, (y + ((k >> 1) & 1)) % 2, (c + (k >> 2)) % 2
            return pltpu.make_async_remote_copy(
                src_ref=small_in, dst_ref=small_out.at[_block_of(px, py, pc)], send_sem=send.at[n_now, k - 1],
                recv_sem=recv.at[n_now, k - 1], device_id=(px, py, pc), device_id_type=MESH)

        small_out[me] = small_in[...]
        small_sends = [small_copy(k) for k in range(1, N_DEV)]
        for cp in small_sends:
            cp.start()

        mine, first, passed = [], [], []
        for a in range(n):
            own = pltpu.make_async_copy(stage[a], blocks[a](outs[a], me), local.at[a])
            own.start()
            mine.append(own)
            if a >= n_now:
                continue
            sends = [copy(a, 0, me, sibling, src=stage[a])]
            sends += [copy(a, k, me, (*_chip(x, y, k), c), src=stage[a]) for k in (1, 2)]
            for cp in sends:
                cp.start()
            first += sends

        here = (x, y, c)
        across = [(*_chip(x, y, k), c) for k in (1, 2)]
        near = [[_block_of(*_chip(x, y, k), cc) for k in (1, 2)] for cc in (c, 1 - c)]
        far = [_block_of(*_chip(x, y, 3), cc) for cc in (c, 1 - c)]

        def launch(cp):
            cp.start()
            passed.append(cp)

        project(stage[0], me)
        copy(0, 0, _block_of(x, y, 1 - c), here).wait_recv()
        project_landed(_block_of(x, y, 1 - c))
        for a in range(n_now):
            for i in (0, 1):
                copy(a, 1 + i, near[0][i], here).wait_recv()
                launch(copy(a, 3 + i, near[0][i], across[1 - i], half=i))
                launch(copy(a, 5 + i, near[0][i], sibling))
            if a == 0:
                project_landed(near[0][0])
                project_landed(near[0][1])
        for i in (0, 1):
            copy(0, 5 + i, near[1][i], here).wait_recv()
            project_landed(near[1][i])
        for a in range(n_now):
            for i in (0, 1):
                copy(a, 3 + i, far[0], here, half=i).wait_recv()
                launch(copy(a, 7 + i, far[0], sibling, half=i))
            if a == 0:
                project_landed(far[0])
        for a in range(n_now):
            if a > 0:
                copy(a, 0, _block_of(x, y, 1 - c), here).wait_recv()
                for i in (0, 1):
                    copy(a, 5 + i, near[1][i], here).wait_recv()
            for i in (0, 1):
                copy(a, 7 + i, far[1], here, half=i).wait_recv()
            if a == 0:
                project_landed(far[1])
        for k in range(1, N_DEV):
            small_arrival(k).wait_recv()
        for cp in first + passed + small_sends:
            cp.wait_send()
        for done in mine + stores[-2:]:
            done.wait()

    out_shape = [jax.ShapeDtypeStruct(s, BF16) for s in full_shapes]
    out_shape += [jax.ShapeDtypeStruct((N_DEV, small_rows, LANES), F32), jax.ShapeDtypeStruct((t, full_shapes[0][1]), BF16),
                  jax.ShapeDtypeStruct(tokens.shape, BF16)]
    return pl.pallas_call(
        body, name="gather_weights", out_shape=out_shape,
        in_specs=[VMEM_SPEC] * (n + 3), out_specs=[HBM_SPEC] * n + [VMEM_SPEC, HBM_SPEC, VMEM_SPEC],
        scratch_shapes=[pltpu.VMEM(s.shape, BF16) for s in shards]
        + [pltpu.VMEM(shards[0].shape, BF16), pltpu.VMEM((2, t, shards[0].shape[1]), BF16),
           pltpu.SemaphoreType.DMA((n_now + 1, 9)), pltpu.SemaphoreType.DMA((n_now + 1, 9)),
           pltpu.SemaphoreType.DMA((n,)), pltpu.SemaphoreType.DMA(()), pltpu.SemaphoreType.DMA((2,))],
        compiler_params=_params(),
    )(*shards, small, tokens, gain)


def _gather_first(full, blocks, send, recv):
    x, y, c = _position()
    me = _block_of(x, y, c)
    peers = [(x, y, 1 - c)] + [(*_chip(x, y, k), c) for k in (1, 2, 3)]

    def copy(a, k, block):
        at = blocks[a](full[a], block)
        return pltpu.make_async_remote_copy(src_ref=at, dst_ref=at, send_sem=send[4 * a + k], recv_sem=recv[4 * a + k],
                                            device_id=peers[k], device_id_type=MESH)

    sends = [copy(a, k, me) for a in range(len(full)) for k in range(4)]
    arrivals = [copy(a, k, _block_of(*peers[k])) for a in range(len(full)) for k in range(4)]
    return sends, arrivals


def _gather_second(full, blocks, send, recv):
    x, y, c = _position()

    def copy(a, k, cc):
        at = blocks[a](full[a], _block_of(*_chip(x, y, k), cc))
        return pltpu.make_async_remote_copy(src_ref=at, dst_ref=at, send_sem=send[3 * a + k - 1],
                                            recv_sem=recv[3 * a + k - 1], device_id=(x, y, 1 - c), device_id_type=MESH)

    sends = [copy(a, k, c) for a in range(len(full)) for k in (1, 2, 3)]
    arrivals = [copy(a, k, 1 - c) for a in range(len(full)) for k in (1, 2, 3)]
    return sends, arrivals


def _split_call(body, name, arrays, sems_in, n_sems_out, after=None, token=False):
    n, m = len(arrays), len(sems_in)

    def kernel_body(*refs):
        outs = refs[n + m + (after is not None):]
        body(refs[:n], refs[n:n + m], outs[:n_sems_out])
        if token:
            outs[-1][...] = jnp.zeros_like(outs[-1])

    extra_in = [] if after is None else [after]
    outs = pl.pallas_call(
        kernel_body, name=name,
        out_shape=(*[pltpu.SemaphoreType.DMA(())] * n_sems_out, *[pltpu.HBM(a.shape, a.dtype) for a in arrays],
                   *([jax.ShapeDtypeStruct((SUBLANES, LANES), F32)] if token else [])),
        in_specs=[HBM_SPEC] * n + [SEM_SPEC] * m + [pl.BlockSpec(memory_space=pl.ANY)] * len(extra_in),
        out_specs=(*[SEM_SPEC] * n_sems_out, *[HBM_SPEC] * n, *([VMEM_SPEC] if token else [])),
        input_output_aliases={i: n_sems_out + i for i in range(n)},
        compiler_params=pltpu.CompilerParams(has_side_effects=DATAFLOW_EFFECT),
    )(*[pltpu.with_memory_space_constraint(a, pltpu.HBM) for a in arrays], *sems_in, *extra_in)
    sems, rest = list(outs[:n_sems_out]), list(outs[n_sems_out:])
    return (sems, rest[:n], rest[n]) if token else (sems, rest[:n])


def _gather_start(full, blocks, name):
    n = len(full)

    def body(arrays, _, sems):
        for cp in _gather_first(arrays, blocks, sems[:4 * n], sems[4 * n:])[0]:
            cp.start()

    sems, arrays, token = _split_call(body, name, full, [], 8 * n, token=True)
    return sems[:4 * n], sems[4 * n:], arrays, token


def _gather_forward(full, blocks, send_first, recv_first, after, name):
    n = len(full)

    def body(arrays, sems_in, sems):
        sends, arrivals = _gather_first(arrays, blocks, sems_in[:4 * n], sems_in[4 * n:])
        for cp in arrivals:
            cp.wait_recv()
        for cp in _gather_second(arrays, blocks, sems[:3 * n], sems[3 * n:])[0]:
            cp.start()
        for cp in sends:
            cp.wait_send()

    sems, arrays = _split_call(body, name, full, [*send_first, *recv_first], 6 * n, after=after)
    return sems[:3 * n], sems[3 * n:], arrays


def _gather_finish(full, blocks, send_second, recv_second, after, name):
    n = len(full)

    def body(arrays, sems_in, _):
        sends, arrivals = _gather_second(arrays, blocks, sems_in[:3 * n], sems_in[3 * n:])
        for cp in sends:
            cp.wait_send()
        for cp in arrivals:
            cp.wait_recv()

    return _split_call(body, name, full, [*send_second, *recv_second], 0, after=after)[1]


def _reduce_pair(grads, blocks, shard_shapes, name):
    n = len(grads)

    def body(*refs):
        ins, outs = refs[:n], refs[n:2 * n]
        got, own = refs[2 * n:3 * n], refs[3 * n:4 * n]
        send, recv, local = refs[4 * n:]
        x, y, c = _position()
        copies, loads = [], []
        for a in range(n):
            for k in range(4):
                chip = _chip(x, y, k)
                cp = pltpu.make_async_remote_copy(
                    src_ref=blocks[a](ins[a], _block_of(*chip, 1 - c)), dst_ref=got[a].at[k],
                    send_sem=send.at[a, k], recv_sem=recv.at[a, k], device_id=(x, y, 1 - c), device_id_type=MESH)
                cp.start()
                copies.append(cp)
                ld = pltpu.make_async_copy(blocks[a](ins[a], _block_of(*chip, c)), own[a].at[k], local.at[a, k])
                ld.start()
                loads.append(ld)
        for a in range(n):
            for k in range(4):
                loads[4 * a + k].wait()
                copies[4 * a + k].wait_recv()
                outs[a][k] = (own[a][k].astype(F32) + got[a][k].astype(F32)).astype(BF16)
        for cp in copies:
            cp.wait_send()

    slots = [(4,) + tuple(s) for s in shard_shapes]
    return pl.pallas_call(
        body, name=name, out_shape=[jax.ShapeDtypeStruct(s, BF16) for s in slots],
        in_specs=[HBM_SPEC] * n, out_specs=[VMEM_SPEC] * n,
        scratch_shapes=[pltpu.VMEM(s, BF16) for s in slots] * 2
        + [pltpu.SemaphoreType.DMA((n, 4)), pltpu.SemaphoreType.DMA((n, 4)), pltpu.SemaphoreType.DMA((n, 4))],
        compiler_params=_params(),
    )(*grads)


def _chip_copies(sums, lands, send, recv):
    x, y, c = _position()
    return [pltpu.make_async_remote_copy(
        src_ref=sums[a].at[k], dst_ref=lands[a].at[k - 1], send_sem=send[3 * a + k - 1], recv_sem=recv[3 * a + k - 1],
        device_id=(*_chip(x, y, k), c), device_id_type=MESH) for a in range(len(sums)) for k in (1, 2, 3)]


def _exchange_chips_start(pair_sums, name):
    n = len(pair_sums)
    lands = [pltpu.with_memory_space_constraint(lax.empty((3,) + tuple(p.shape[1:]), BF16), pltpu.HBM) for p in pair_sums]

    def body(*refs):
        sums, zones = refs[:n], refs[n:2 * n]
        send, recv = refs[2 * n:5 * n], refs[5 * n:8 * n]
        token = refs[-1]
        for cp in _chip_copies(sums, zones, send, recv):
            cp.start()
        token[...] = jnp.zeros_like(token)

    outs = pl.pallas_call(
        body, name=name,
        out_shape=(*[pltpu.SemaphoreType.DMA(())] * (6 * n),
                   *[pltpu.HBM(p.shape, BF16) for p in pair_sums], *[pltpu.HBM(z.shape, BF16) for z in lands],
                   jax.ShapeDtypeStruct((SUBLANES, LANES), F32)),
        in_specs=[HBM_SPEC] * (2 * n), out_specs=(*[SEM_SPEC] * (6 * n), *[HBM_SPEC] * (2 * n), VMEM_SPEC),
        input_output_aliases={i: 6 * n + i for i in range(2 * n)},
        compiler_params=pltpu.CompilerParams(has_side_effects=DATAFLOW_EFFECT),
    )(*[pltpu.with_memory_space_constraint(p, pltpu.HBM) for p in pair_sums], *lands)
    return outs[:3 * n], outs[3 * n:6 * n], outs[6 * n:7 * n], outs[7 * n:8 * n], outs[-1]


def _exchange_chips_wait(send, recv, sums, lands, after, name):
    n = len(sums)

    def body(*refs):
        sums_in, zones = refs[:n], refs[n:2 * n]
        send_in, recv_in = refs[2 * n:5 * n], refs[5 * n:8 * n]
        for cp in _chip_copies(sums_in, zones, send_in, recv_in):
            cp.wait_send()
            cp.wait_recv()

    outs = pl.pallas_call(
        body, name=name,
        out_shape=(*[pltpu.HBM(p.shape, BF16) for p in sums], *[pltpu.HBM(z.shape, BF16) for z in lands]),
        in_specs=[HBM_SPEC] * (2 * n) + [SEM_SPEC] * (6 * n) + [pl.BlockSpec(memory_space=pl.ANY)],
        out_specs=[HBM_SPEC] * (2 * n), input_output_aliases={i: i for i in range(2 * n)},
        compiler_params=pltpu.CompilerParams(has_side_effects=DATAFLOW_EFFECT),
    )(*sums, *lands, *send, *recv, after)
    return outs[:n], outs[n:]


def _small_copies(mine, land, send, recv):
    x, y, c = _position()
    me = _block_of(x, y, c)

    def peer(k):
        return (x + (k & 1)) % 2, (y + ((k >> 1) & 1)) % 2, (c + (k >> 2)) % 2

    def copy(k, slot):
        return pltpu.make_async_remote_copy(src_ref=mine, dst_ref=land.at[slot], send_sem=send[k - 1], recv_sem=recv[k - 1],
                                            device_id=peer(k), device_id_type=MESH)

    return [copy(k, me) for k in range(1, N_DEV)], [copy(k, _block_of(*peer(k))) for k in range(1, N_DEV)]


def _small_start(part, name):
    land = jnp.zeros((N_DEV,) + part.shape, F32)

    def body(arrays, _, sems):
        for cp in _small_copies(arrays[0], arrays[1], sems[:7], sems[7:])[0]:
            cp.start()

    sems, arrays, token = _split_call(body, name, [part, land], [], 14, token=True)
    return sems[:7], sems[7:], arrays[0], arrays[1], token


def _small_wait(send, recv, part, land, after, name):
    def body(arrays, sems_in, _):
        sends, arrivals = _small_copies(arrays[0], arrays[1], sems_in[:7], sems_in[7:])
        for cp in sends:
            cp.wait_send()
        for cp in arrivals:
            cp.wait_recv()

    return _split_call(body, name, [part, land], [*send, *recv], 0, after=after)[1]


def _small_sum(pairs, me):
    n = len(pairs)

    def body(me_ref, *refs):
        for i in range(n):
            mine, land, out = refs[2 * i], refs[2 * i + 1], refs[2 * n + i]
            total = jnp.zeros(mine.shape, F32)
            for d in range(N_DEV):
                total = total + land[d] + jnp.where(me_ref[0] == d, mine[...], 0.0)
            out[...] = total

    flat = [a for pair in pairs for a in pair]
    return pl.pallas_call(
        body, name="small_sum", out_shape=[jax.ShapeDtypeStruct(mine.shape, F32) for mine, _ in pairs],
        in_specs=[pl.BlockSpec(memory_space=pltpu.SMEM)] + [VMEM_SPEC] * (2 * n), out_specs=[VMEM_SPEC] * n,
        compiler_params=_params(),
    )(me.reshape(1).astype(jnp.int32), *flat)


def _section(s, t):
    return pl.BlockSpec((t, CB), lambda h, s=s: (0, s * (D_MODEL // CB) + h))


def _conv_mixer_fwd(proj, w_short):
    t = proj.shape[0]
    rc = _row_chunk(t)

    def body(b_ref, c_ref, x_ref, w_ref, y_ref, pad):
        pad[pl.ds(0, PAD), :] = jnp.zeros((PAD, CB), F32)
        for r0 in range(0, t, rc):
            rows = pl.ds(r0, rc)
            pad[pl.ds(PAD + r0, rc), :] = c_ref[rows, :].astype(F32) * x_ref[rows, :].astype(F32)
        w = w_ref[...]
        for r0 in range(0, t, rc):
            rows = pl.ds(r0, rc)
            y_ref[rows, :] = (b_ref[rows, :].astype(F32) * _conv_causal(pad, w, r0, rc, 3)).astype(BF16)

    return pl.pallas_call(
        body, name="conv_mixer_fwd", grid=(D_MODEL // CB,),
        out_shape=jax.ShapeDtypeStruct((t, D_MODEL), BF16),
        in_specs=[_section(0, t), _section(1, t), _section(2, t), pl.BlockSpec((3, CB), lambda h: (0, h))],
        out_specs=pl.BlockSpec((t, CB), lambda h: (0, h)),
        scratch_shapes=[pltpu.VMEM((t + PAD, CB), F32)],
        compiler_params=_params("parallel"),
    )(proj, proj, proj, w_short)


def _lru_gates(xl, wa, ba, wx, bx, ls, first_row):
    xb = xl.astype(BF16)
    ra = jax.nn.sigmoid(_dot(xb, wa) + ba)
    ia = jax.nn.sigmoid(_dot(xb, wx) + bx)
    la = LRU_C * ra * ls
    a = jnp.exp(la)
    one_minus = -_expm1_neg(2.0 * la)
    mult = jnp.where(first_row, 1.0, jnp.sqrt(one_minus))
    return xb, ra, ia, a, one_minus, mult


def _head_specs():
    vec = pl.BlockSpec((1, CB), lambda h: (0, h))
    mat = pl.BlockSpec((N_DEV, None, HEAD_DIM // N_DEV, HEAD_DIM), lambda h: (0, h, 0, 0))
    return vec, mat


def _lru_fwd(proj, w_conv, b_conv, wa, ba, wx, bx, lam):
    t = proj.shape[0]
    rc = _row_chunk(t)
    vec, mat = _head_specs()

    def body(lx_ref, ly_ref, wc_ref, bc_ref, wa_ref, ba_ref, wx_ref, bx_ref, lam_ref, yb_ref, hl_ref, a_ref, kept_ref,
             pad, u_s):
        pad[pl.ds(0, PAD), :] = jnp.zeros((PAD, CB), F32)
        for r0 in range(0, t, rc):
            pad[pl.ds(PAD + r0, rc), :] = lx_ref[pl.ds(r0, rc), :].astype(F32)
        wc, bc = wc_ref[...], bc_ref[...]
        wa_m, wx_m = wa_ref[...].reshape(HEAD_DIM, HEAD_DIM), wx_ref[...].reshape(HEAD_DIM, HEAD_DIM)
        ls = _log_sigmoid(lam_ref[...])
        for r0 in range(0, t, rc):
            rows = pl.ds(r0, rc)
            xl = _conv_causal(pad, wc, r0, rc, 4) + bc
            first = (lax.broadcasted_iota(jnp.int32, (rc, CB), 0) + r0) == 0
            xb, ra, ia, a, _, mult = _lru_gates(xl, wa_m, ba_ref[...], wx_m, bx_ref[...], ls, first)
            a_ref[rows, :] = a
            u_s[rows, :] = mult * (ia * xl)
            kept_ref[0, rows, :] = xb
            kept_ref[1, rows, :] = ra.astype(BF16)
            kept_ref[2, rows, :] = ia.astype(BF16)

        row = lax.broadcasted_iota(jnp.int32, (SUBLANES, CB), 0)

        def group(g, carry):
            r = pl.multiple_of(g * SUBLANES, SUBLANES)
            a_g, b_g = a_ref[pl.ds(r, SUBLANES), :], u_s[pl.ds(r, SUBLANES), :]
            for s in (1, 2, 4):
                keep = row >= s
                b_g = jnp.where(keep, a_g * pltpu.roll(b_g, s, 0) + b_g, b_g)
                a_g = jnp.where(keep, a_g * pltpu.roll(a_g, s, 0), a_g)
            h_g = b_g + a_g * carry
            hl_ref[pl.ds(r, SUBLANES), :] = h_g
            return jnp.broadcast_to(h_g[SUBLANES - 1:SUBLANES, :], (SUBLANES, CB))

        def trip(i, carry):
            for j in range(SCAN_UNROLL):
                carry = group(i * SCAN_UNROLL + j, carry)
            return carry

        lax.fori_loop(0, t // SUBLANES // SCAN_UNROLL, trip, jnp.zeros((SUBLANES, CB), F32))
        for r0 in range(0, t, rc):
            rows = pl.ds(r0, rc)
            yb_ref[rows, :] = (hl_ref[rows, :] * _gelu(ly_ref[rows, :].astype(F32))).astype(BF16)

    blk = pl.BlockSpec((t, CB), lambda h: (0, h))
    res = jax.ShapeDtypeStruct((t, D_MODEL), F32)
    return pl.pallas_call(
        body, name="lru_fwd", grid=(N_HEADS,),
        out_shape=[jax.ShapeDtypeStruct((t, D_MODEL), BF16), res, res, jax.ShapeDtypeStruct((3, t, D_MODEL), BF16)],
        in_specs=[_section(3, t), _section(4, t), pl.BlockSpec((4, CB), lambda h: (0, h)), vec, mat, vec, mat, vec, vec],
        out_specs=[blk, blk, blk, pl.BlockSpec((3, t, CB), lambda h: (0, 0, h))],
        scratch_shapes=[pltpu.VMEM((t + PAD, CB), F32), pltpu.VMEM((t, CB), F32)],
        compiler_params=_params("parallel"),
    )(proj, proj, w_conv, b_conv, wa, ba, wx, bx, lam)


def _merge(y_a, y_b, proj, x, w_cb, w_lb, w_out, g2, g3):
    t = x.shape[0]
    tm = min(512, t)

    def body(ya_ref, yb_ref, gc_ref, gl_ref, x_ref, wcb_ref, wlb_ref, wo_ref, g2_ref, g3_ref,
             pa_ref, pb_ref, mg_ref, mix_ref, x1_ref, h2_ref):
        pa = _dot(ya_ref[...], wcb_ref[...]).astype(BF16)
        pb = _dot(yb_ref[...], wlb_ref[...]).astype(BF16)
        pa_ref[...] = pa
        pb_ref[...] = pb
        merged = (jax.nn.sigmoid(gc_ref[...].astype(F32)) * pa.astype(F32)
                  + jax.nn.sigmoid(gl_ref[...].astype(F32)) * pb.astype(F32)).astype(BF16)
        mg_ref[...] = merged
        mix = _dot(merged, wo_ref[...])
        mix_ref[...] = mix
        n2, _ = _rms_fwd(mix)
        x1 = x_ref[...] + n2 * g2_ref[...]
        x1_ref[...] = x1
        n3, _ = _rms_fwd(x1)
        h2_ref[...] = (n3 * g3_ref[...]).astype(BF16)

    row = pl.BlockSpec((tm, D_MODEL), lambda i: (i, 0))
    full = pl.BlockSpec((D_MODEL, D_MODEL), lambda i: (0, 0))
    vec = pl.BlockSpec((1, D_MODEL), lambda i: (0, 0))
    act = jax.ShapeDtypeStruct((t, D_MODEL), BF16)
    res = jax.ShapeDtypeStruct((t, D_MODEL), F32)
    return pl.pallas_call(
        body, name="merge_fwd", grid=(t // tm,), out_shape=[act, act, act, res, res, act],
        in_specs=[row, row, pl.BlockSpec((tm, D_MODEL), lambda i: (i, 5)), pl.BlockSpec((tm, D_MODEL), lambda i: (i, 6)),
                  row, full, full, full, vec, vec],
        out_specs=[row] * 6,
        compiler_params=_params("parallel"),
    )(y_a, y_b, proj, proj, x, w_cb, w_lb, w_out, g2, g3)


N_FF_BLOCKS = D_FF // CB
FFN_BWD_COLS = 512


def _ffn_up(h2, w_up, w_conv, b_conv):
    t = h2.shape[0]
    rc = _row_chunk(t)
    nb = N_FF_BLOCKS

    def body(h_ref, w_ref, c_ref, b_ref, up_ref, act_ref, f_ref, pad, gate):
        k = pl.program_id(1)
        pad[pl.ds(0, PAD), :] = jnp.zeros((PAD, CB), F32)
        for r0 in range(0, t, rc):
            rows = pl.ds(r0, rc)
            up = _dot(h_ref[rows, :], w_ref[...]).astype(BF16)
            up_ref[rows, :] = up
            pad[pl.ds(PAD + r0, rc), :] = up.astype(F32)
        cw = c_ref[...]
        for r0 in range(0, t, rc):
            rows = pl.ds(r0, rc)
            act = _conv_causal(pad, cw, r0, rc, 3) + b_ref[...]
            act_ref[rows, :] = act.astype(BF16)

            @pl.when(k == 0)
            def _():
                gate[rows, :] = act

            @pl.when(k == 1)
            def _():
                f_ref[rows, :] = (_gelu(gate[rows, :]) * act).astype(BF16)

    half = lambda rows: pl.BlockSpec((rows, CB), lambda j, k: (0, nb * k + j))
    wide = jax.ShapeDtypeStruct((t, 2 * D_FF), BF16)
    return pl.pallas_call(
        body, name="ffn_up_fwd", grid=(nb, 2), out_shape=[wide, wide, jax.ShapeDtypeStruct((t, D_FF), BF16)],
        in_specs=[pl.BlockSpec((t, D_MODEL), lambda j, k: (0, 0)), half(D_MODEL), half(3), half(1)],
        out_specs=[half(t), half(t), pl.BlockSpec((t, CB), lambda j, k: (0, j))],
        scratch_shapes=[pltpu.VMEM((t + PAD, CB), F32), pltpu.VMEM((t, CB), F32)],
        compiler_params=_params("parallel", "arbitrary"),
    )(h2, w_up, w_conv, b_conv)


def _ffn_down(f, act, w_down, x1, target, g4):
    t = f.shape[0]
    tm = min(256, t)
    cc = 512

    def body(f_ref, act_ref, w_ref, x1_ref, tg_ref, g_ref, dy_ref, dout_ref, back_ref, dg_ref, loss_ref):
        @pl.when(pl.program_id(0) == 0)
        def _():
            dg_ref[...] = jnp.zeros_like(dg_ref)
            loss_ref[...] = jnp.zeros_like(loss_ref)
        out = _dot(f_ref[...], w_ref[...])
        n4, r4 = _rms_fwd(out)
        err = x1_ref[...] + n4 * g_ref[...] - tg_ref[...]
        loss_ref[...] += jnp.full(loss_ref.shape, 0.5 / D_MODEL, F32) * jnp.sum(err * err)
        dy = err * (1.0 / D_MODEL)
        dy_ref[...] = dy
        dg_ref[...] += jnp.sum(dy * n4, axis=0, keepdims=True)
        d_out = _rms_bwd(n4, r4, dy * g_ref[...]).astype(BF16)
        dout_ref[...] = d_out
        for c0 in range(0, D_FF, cc):
            d_f = _dot_nt(d_out, w_ref[pl.ds(c0, cc), :])
            gelu, d_gelu = _gelu_and_grad(act_ref[:, pl.ds(c0, cc)].astype(F32))
            val = act_ref[:, pl.ds(D_FF + c0, cc)].astype(F32)
            back_ref[:, pl.ds(c0, cc)] = (d_f * val * d_gelu).astype(BF16)
            back_ref[:, pl.ds(D_FF + c0, cc)] = (d_f * gelu).astype(BF16)

    row = pl.BlockSpec((tm, D_MODEL), lambda i: (i, 0))
    wide = pl.BlockSpec((tm, 2 * D_FF), lambda i: (i, 0))
    vec = pl.BlockSpec((1, D_MODEL), lambda i: (0, 0))
    return pl.pallas_call(
        body, name="ffn_down_fwd_bwd", grid=(t // tm,),
        out_shape=[jax.ShapeDtypeStruct((t, D_MODEL), F32), jax.ShapeDtypeStruct((t, D_MODEL), BF16),
                   jax.ShapeDtypeStruct((t, 2 * D_FF), BF16), jax.ShapeDtypeStruct((1, D_MODEL), F32),
                   jax.ShapeDtypeStruct((SUBLANES, LANES), F32)],
        in_specs=[pl.BlockSpec((tm, D_FF), lambda i: (i, 0)), wide, pl.BlockSpec((D_FF, D_MODEL), lambda i: (0, 0)),
                  row, row, vec],
        out_specs=[row, row, wide, vec, pl.BlockSpec((SUBLANES, LANES), lambda i: (0, 0))],
        compiler_params=_params("arbitrary"),
    )(f, act, w_down, x1, target, g4)


def _grad_tn(a, b, bm, name):
    t, m = a.shape
    n = b.shape[1]

    def body(a_ref, b_ref, o_ref):
        o_ref[...] = _dot_tn(a_ref[...], b_ref[...]).astype(BF16)

    return pl.pallas_call(
        body, name=name, grid=(m // bm,), out_shape=jax.ShapeDtypeStruct((m, n), BF16),
        in_specs=[pl.BlockSpec((t, bm), lambda i: (0, i)), pl.BlockSpec((t, n), lambda i: (0, 0))],
        out_specs=pl.BlockSpec((bm, n), lambda i: (i, 0)),
        compiler_params=_params("parallel"),
    )(a, b)


def _ffn_up_bwd(up, back, w_conv, h2, w_up):
    t = h2.shape[0]
    rc = _row_chunk(t)
    cb = FFN_BWD_COLS

    def body(up_ref, back_ref, c_ref, h_ref, w_ref, dw_ref, dcw_ref, dcb_ref, dh_ref, pad, after, d_up):
        @pl.when(pl.program_id(0) == 0)
        def _():
            dh_ref[...] = jnp.zeros_like(dh_ref)
        pad[pl.ds(0, PAD), :] = jnp.zeros((PAD, cb), F32)
        after[pl.ds(t, PAD), :] = jnp.zeros((PAD, cb), F32)
        for r0 in range(0, t, rc):
            pad[pl.ds(PAD + r0, rc), :] = up_ref[pl.ds(r0, rc), :].astype(F32)
            after[pl.ds(r0, rc), :] = back_ref[pl.ds(r0, rc), :].astype(F32)
        cw = c_ref[...]
        taps = [jnp.zeros((SUBLANES, cb), F32)] * 3
        bias = jnp.zeros((SUBLANES, cb), F32)
        for r0 in range(0, t, rc):
            for q0 in range(r0, r0 + rc, ROW_SLICE):
                rows = pl.ds(q0, ROW_SLICE)
                d_up[rows, :] = _conv_anticausal(after, cw, q0, ROW_SLICE, 3).astype(BF16)
                g = after[rows, :]
                taps = [acc + _fold_rows(g * _rows_back(pad, q0, ROW_SLICE, 2 - k)) for k, acc in enumerate(taps)]
                bias = bias + _fold_rows(g)
            rows = pl.ds(r0, rc)
            dh_ref[rows, :] += _dot_nt(d_up[rows, :], w_ref[...])
        dw_ref[...] = _dot_tn(h_ref[...], d_up[...]).astype(BF16)
        dcw_ref[...] = jnp.concatenate([jnp.sum(acc, axis=0, keepdims=True) for acc in taps], axis=0)
        dcb_ref[...] = jnp.sum(bias, axis=0, keepdims=True)

    cols = lambda rows: pl.BlockSpec((rows, cb), lambda j: (0, j))
    whole = pl.BlockSpec((t, D_MODEL), lambda j: (0, 0))
    return pl.pallas_call(
        body, name="ffn_up_bwd", grid=(2 * D_FF // cb,),
        out_shape=[jax.ShapeDtypeStruct((D_MODEL, 2 * D_FF), BF16), jax.ShapeDtypeStruct((3, 2 * D_FF), F32),
                   jax.ShapeDtypeStruct((1, 2 * D_FF), F32), jax.ShapeDtypeStruct((t, D_MODEL), F32)],
        in_specs=[cols(t), cols(t), cols(3), whole, cols(D_MODEL)],
        out_specs=[cols(D_MODEL), cols(3), cols(1), whole],
        scratch_shapes=[pltpu.VMEM((t + PAD, cb), F32), pltpu.VMEM((t + PAD, cb), F32), pltpu.VMEM((t, cb), BF16)],
        compiler_params=_params("arbitrary"),
    )(up, back, w_conv, h2, w_up)


def _merge_bwd(dy, d_h2, x1, mix, g3, g2, w_out, w_cb, w_lb, pa, pb, proj):
    t = dy.shape[0]
    tm = min(256, t)

    def body(dy_ref, dh2_ref, x1_ref, mix_ref, g3_ref, g2_ref, wo_ref, wcb_ref, wlb_ref, pa_ref, pb_ref, gc_ref, gl_ref,
             dx1_ref, dmix_ref, dpa_ref, dpb_ref, dya_ref, dyb_ref, dgate_ref, dg3_ref, dg2_ref):
        @pl.when(pl.program_id(0) == 0)
        def _():
            dg3_ref[...] = jnp.zeros_like(dg3_ref)
            dg2_ref[...] = jnp.zeros_like(dg2_ref)
        n3, r3 = _rms_fwd(x1_ref[...])
        d_h2 = dh2_ref[...]
        dg3_ref[...] += jnp.sum(d_h2 * n3, axis=0, keepdims=True)
        dx1 = dy_ref[...] + _rms_bwd(n3, r3, d_h2 * g3_ref[...])
        dx1_ref[...] = dx1
        n2, r2 = _rms_fwd(mix_ref[...])
        dg2_ref[...] += jnp.sum(dx1 * n2, axis=0, keepdims=True)
        d_mix = _rms_bwd(n2, r2, dx1 * g2_ref[...]).astype(BF16)
        dmix_ref[...] = d_mix
        d_merged = _dot_nt(d_mix, wo_ref[...])
        sc = jax.nn.sigmoid(gc_ref[...].astype(F32))
        sl = jax.nn.sigmoid(gl_ref[...].astype(F32))
        d_pa = (d_merged * sc).astype(BF16)
        d_pb = (d_merged * sl).astype(BF16)
        dpa_ref[...] = d_pa
        dpb_ref[...] = d_pb
        dgate_ref[0] = (d_merged * pa_ref[...].astype(F32) * sc * (1.0 - sc)).astype(BF16)
        dgate_ref[1] = (d_merged * pb_ref[...].astype(F32) * sl * (1.0 - sl)).astype(BF16)
        dya_ref[...] = _dot_nt(d_pa, wcb_ref[...]).astype(BF16)
        dyb_ref[...] = _dot_nt(d_pb, wlb_ref[...]).astype(BF16)

    row = pl.BlockSpec((tm, D_MODEL), lambda i: (i, 0))
    full = pl.BlockSpec((D_MODEL, D_MODEL), lambda i: (0, 0))
    vec = pl.BlockSpec((1, D_MODEL), lambda i: (0, 0))
    act = jax.ShapeDtypeStruct((t, D_MODEL), BF16)
    small = jax.ShapeDtypeStruct((1, D_MODEL), F32)
    return pl.pallas_call(
        body, name="merge_bwd", grid=(t // tm,),
        out_shape=[jax.ShapeDtypeStruct((t, D_MODEL), F32), act, act, act, act, act,
                   jax.ShapeDtypeStruct((2, t, D_MODEL), BF16), small, small],
        in_specs=[row, row, row, row, vec, vec, full, full, full, row, row,
                  pl.BlockSpec((tm, D_MODEL), lambda i: (i, 5)), pl.BlockSpec((tm, D_MODEL), lambda i: (i, 6))],
        out_specs=[row] * 6 + [pl.BlockSpec((2, tm, D_MODEL), lambda i: (0, i, 0)), vec, vec],
        compiler_params=_params("arbitrary"),
    )(dy, d_h2, x1, mix, g3, g2, w_out, w_cb, w_lb, pa, pb, proj, proj)


def _conv_mixer_bwd(proj, d_ya, w_short):
    t = proj.shape[0]
    rc = _row_chunk(t)

    def body(b_ref, c_ref, x_ref, dy_ref, w_ref, d_ref, dw_ref, pad, back):
        pad[pl.ds(0, PAD), :] = jnp.zeros((PAD, CB), F32)
        back[pl.ds(t, PAD), :] = jnp.zeros((PAD, CB), F32)
        for r0 in range(0, t, rc):
            rows = pl.ds(r0, rc)
            pad[pl.ds(PAD + r0, rc), :] = c_ref[rows, :].astype(F32) * x_ref[rows, :].astype(F32)
        w = w_ref[...]
        for r0 in range(0, t, rc):
            rows = pl.ds(r0, rc)
            d_y = dy_ref[rows, :].astype(F32)
            d_ref[0, rows, :] = (d_y * _conv_causal(pad, w, r0, rc, 3)).astype(BF16)
            back[rows, :] = d_y * b_ref[rows, :].astype(F32)
        taps = [jnp.zeros((1, CB), F32)] * 3
        for r0 in range(0, t, rc):
            rows = pl.ds(r0, rc)
            d_u = _conv_anticausal(back, w, r0, rc, 3)
            d_ref[1, rows, :] = (d_u * x_ref[rows, :].astype(F32)).astype(BF16)
            d_ref[2, rows, :] = (d_u * c_ref[rows, :].astype(F32)).astype(BF16)
            taps = [acc + new for acc, new in zip(taps, _conv_wgrad(back[rows, :], pad, r0, rc, 3))]
        dw_ref[...] = jnp.concatenate(taps, axis=0)

    blk = pl.BlockSpec((t, CB), lambda h: (0, h))
    return pl.pallas_call(
        body, name="conv_mixer_bwd", grid=(D_MODEL // CB,),
        out_shape=[jax.ShapeDtypeStruct((3, t, D_MODEL), BF16), jax.ShapeDtypeStruct((3, D_MODEL), F32)],
        in_specs=[_section(0, t), _section(1, t), _section(2, t), blk, pl.BlockSpec((3, CB), lambda h: (0, h))],
        out_specs=[pl.BlockSpec((3, t, CB), lambda h: (0, 0, h)), pl.BlockSpec((3, CB), lambda h: (0, h))],
        scratch_shapes=[pltpu.VMEM((t + PAD, CB), F32), pltpu.VMEM((t + PAD, CB), F32)],
        compiler_params=_params("parallel"),
    )(proj, proj, proj, d_ya, w_short)


LRU_SMALL_ROWS = 8


def _lru_bwd(proj, hl, a_all, kept, d_yb, w_conv, wa, wx, lam):
    t = proj.shape[0]
    rc = _row_chunk(t)
    vec, mat = _head_specs()

    def body(lx_ref, ly_ref, hl_ref, a_ref, kept_ref, dy_ref, wc_ref, wa_ref, wx_ref, lam_ref,
             d_ref, dwa_ref, dwx_ref, small_ref, pad, a_next, dh_s, h_prev, back, acc_a, acc_x, dz_a, dz_x):
        zeros = jnp.zeros((PAD, CB), F32)
        pad[pl.ds(0, PAD), :] = zeros
        h_prev[pl.ds(0, PAD), :] = zeros
        a_next[pl.ds(t, PAD), :] = zeros
        back[pl.ds(t, PAD), :] = zeros
        for r0 in range(0, t, ROW_SLICE):
            rows = pl.ds(r0, ROW_SLICE)
            pad[pl.ds(PAD + r0, ROW_SLICE), :] = lx_ref[rows, :].astype(F32)
            h_prev[pl.ds(PAD + r0, ROW_SLICE), :] = hl_ref[rows, :]
            a_next[pl.ds(PAD - 1 + r0, ROW_SLICE), :] = a_ref[rows, :]
            act, d_act = _gelu_and_grad(ly_ref[rows, :].astype(F32))
            d_y = dy_ref[rows, :].astype(F32)
            dh_s[rows, :] = d_y * act
            d_ref[1, rows, :] = (d_y * hl_ref[rows, :] * d_act).astype(BF16)
        wc = wc_ref[...]
        wa_m, wx_m = wa_ref[...].reshape(HEAD_DIM, HEAD_DIM), wx_ref[...].reshape(HEAD_DIM, HEAD_DIM)
        ls = _log_sigmoid(lam_ref[...])

        row = lax.broadcasted_iota(jnp.int32, (SUBLANES, CB), 0)
        groups = t // SUBLANES

        def group(i, carry):
            r = pl.multiple_of((groups - 1 - i) * SUBLANES, SUBLANES)
            a_g, b_g = a_next[pl.ds(PAD + r, SUBLANES), :], dh_s[pl.ds(r, SUBLANES), :]
            for s in (1, 2, 4):
                keep = row < SUBLANES - s
                b_g = jnp.where(keep, a_g * pltpu.roll(b_g, SUBLANES - s, 0) + b_g, b_g)
                a_g = jnp.where(keep, a_g * pltpu.roll(a_g, SUBLANES - s, 0), a_g)
            d_g = b_g + a_g * carry
            dh_s[pl.ds(r, SUBLANES), :] = d_g
            return jnp.broadcast_to(d_g[0:1, :], (SUBLANES, CB))

        lax.fori_loop(0, groups, group, jnp.zeros((SUBLANES, CB), F32))

        acc_a[...] = jnp.zeros_like(acc_a)
        acc_x[...] = jnp.zeros_like(acc_x)
        d_ba = d_bx = d_ls = jnp.zeros((SUBLANES, CB), F32)
        for r0 in range(0, t, rc):
            for q0 in range(r0, r0 + rc, ROW_SLICE):
                rows, local = pl.ds(q0, ROW_SLICE), pl.ds(q0 - r0, ROW_SLICE)
                a = a_ref[rows, :]
                xl, ra, ia = (kept_ref[i, rows, :].astype(F32) for i in range(3))
                a_sq = a * a
                mult = jnp.sqrt(1.0 - a_sq)
                slope = -a_sq / mult
                if q0 == 0:
                    first = lax.broadcasted_iota(jnp.int32, (ROW_SLICE, CB), 0) == 0
                    mult, slope = jnp.where(first, 1.0, mult), jnp.where(first, 0.0, slope)
                d_h = dh_s[rows, :]
                d_la = d_h * _rows_back(h_prev, q0, ROW_SLICE, 1) * a + d_h * ia * xl * slope
                d_za = d_la * (LRU_C * ls) * ra * (1.0 - ra)
                d_zx = d_h * mult * xl * ia * (1.0 - ia)
                d_ls = d_ls + _fold_rows(d_la * ra)
                d_ba = d_ba + _fold_rows(d_za)
                d_bx = d_bx + _fold_rows(d_zx)
                dz_a[local, :] = d_za.astype(BF16)
                dz_x[local, :] = d_zx.astype(BF16)
                back[rows, :] = d_h * mult * ia
            rows = pl.ds(r0, rc)
            xb = kept_ref[0, rows, :]
            acc_a[...] += _dot_tn(xb, dz_a[...])
            acc_x[...] += _dot_tn(xb, dz_x[...])
            back[rows, :] += _dot_nt(dz_a[...], wa_m) + _dot_nt(dz_x[...], wx_m)
        taps = [jnp.zeros((SUBLANES, CB), F32)] * 4
        d_bc = jnp.zeros((SUBLANES, CB), F32)
        for q0 in range(0, t, ROW_SLICE):
            rows = pl.ds(q0, ROW_SLICE)
            d_ref[0, rows, :] = _conv_anticausal(back, wc, q0, ROW_SLICE, 4).astype(BF16)
            g = back[rows, :]
            taps = [acc + _fold_rows(g * _rows_back(pad, q0, ROW_SLICE, 3 - k)) for k, acc in enumerate(taps)]
            d_bc = d_bc + _fold_rows(g)
        d_lam = d_ls * LRU_C * jax.nn.sigmoid(-lam_ref[...])
        small_ref[...] = jnp.concatenate(
            [jnp.sum(v, axis=0, keepdims=True) for v in taps + [d_bc, d_ba, d_bx, d_lam]], axis=0)
        dwa_ref[...] = acc_a[...].reshape(N_DEV, HEAD_DIM // N_DEV, HEAD_DIM).astype(BF16)
        dwx_ref[...] = acc_x[...].reshape(N_DEV, HEAD_DIM // N_DEV, HEAD_DIM).astype(BF16)

    blk = pl.BlockSpec((t, CB), lambda h: (0, h))
    gate_grad = jax.ShapeDtypeStruct((N_DEV, N_HEADS, HEAD_DIM // N_DEV, HEAD_DIM), BF16)
    return pl.pallas_call(
        body, name="lru_bwd", grid=(N_HEADS,),
        out_shape=[jax.ShapeDtypeStruct((2, t, D_MODEL), BF16), gate_grad, gate_grad,
                   jax.ShapeDtypeStruct((LRU_SMALL_ROWS, D_MODEL), F32)],
        in_specs=[_section(3, t), _section(4, t), blk, blk, pl.BlockSpec((3, t, CB), lambda h: (0, 0, h)), blk,
                  pl.BlockSpec((4, CB), lambda h: (0, h)), mat, mat, vec],
        out_specs=[pl.BlockSpec((2, t, CB), lambda h: (0, 0, h)), mat, mat,
                   pl.BlockSpec((LRU_SMALL_ROWS, CB), lambda h: (0, h))],
        scratch_shapes=[pltpu.VMEM((t + PAD, CB), F32), pltpu.VMEM((t + PAD, CB), F32), pltpu.VMEM((t, CB), F32),
                        pltpu.VMEM((t + PAD, CB), F32), pltpu.VMEM((t + PAD, CB), F32),
                        pltpu.VMEM((HEAD_DIM, HEAD_DIM), F32), pltpu.VMEM((HEAD_DIM, HEAD_DIM), F32),
                        pltpu.VMEM((rc, CB), BF16), pltpu.VMEM((rc, CB), BF16)],
        compiler_params=_params("parallel"),
    )(proj, proj, hl, a_all, kept, d_yb, w_conv, wa, wx, lam)


def _stack_maps(halves):
    def conv(sec, part):
        return jnp.minimum(sec, 2), jnp.where(sec < 3, part, halves - 1)

    def lru(sec, part):
        return jnp.clip(sec - 3, 0, 1), jnp.where(sec < 3, 0, jnp.where(sec < 5, part, halves - 1))

    def gate(sec, part):
        return jnp.clip(sec - 5, 0, 1), jnp.where(sec < 5, 0, part)

    return conv, lru, gate


def _pick_stack(sec, refs, fn):
    @pl.when(sec < 3)
    def _():
        fn(refs[0])

    @pl.when((sec >= 3) & (sec < 5))
    def _():
        fn(refs[1])

    @pl.when(sec >= 5)
    def _():
        fn(refs[2])


def _in_proj_wgrad(h, d_conv, d_lru, d_gate):
    t = h.shape[0]
    halves, bn = 1, D_MODEL
    maps = _stack_maps(halves)

    def body(h_ref, dc_ref, dl_ref, dg_ref, o_ref):
        def emit(ref):
            o_ref[...] = _dot_tn(h_ref[...], ref[...]).astype(BF16)
        _pick_stack(pl.program_id(0) // halves, (dc_ref, dl_ref, dg_ref), emit)

    def spec(m):
        def index(s):
            stack, part = m(s // halves, s % halves)
            return stack, 0, part
        return pl.BlockSpec((None, t, bn), index)

    return pl.pallas_call(
        body, name="in_proj_wgrad", grid=(7 * halves,), out_shape=jax.ShapeDtypeStruct((D_MODEL, IN_COLS), BF16),
        in_specs=[pl.BlockSpec((t, D_MODEL), lambda s: (0, 0))] + [spec(m) for m in maps],
        out_specs=pl.BlockSpec((D_MODEL, bn), lambda s: (0, s)),
        compiler_params=_params("arbitrary"),
    )(h, d_conv, d_lru, d_gate)


def _in_proj_xgrad(d_conv, d_lru, d_gate, w_in, x, dx1, g1):
    t = x.shape[0]
    tm = min(1024, t)
    maps = _stack_maps(1)

    def body(dc_ref, dl_ref, dg_ref, w_ref, x_ref, dx1_ref, g_ref, dx_ref, dgain_ref, acc):
        i, s = pl.program_id(0), pl.program_id(1)

        @pl.when((i == 0) & (s == 0))
        def _():
            dgain_ref[...] = jnp.zeros_like(dgain_ref)

        @pl.when(s == 0)
        def _():
            acc[...] = jnp.zeros_like(acc)

        def add(ref):
            acc[...] += _dot_nt(ref[...], w_ref[...])
        _pick_stack(s, (dc_ref, dl_ref, dg_ref), add)

        @pl.when(s == 6)
        def _():
            n1, r1 = _rms_fwd(x_ref[...])
            d_h = acc[...]
            dgain_ref[...] += jnp.sum(d_h * n1, axis=0, keepdims=True)
            dx_ref[...] = dx1_ref[...] + _rms_bwd(n1, r1, d_h * g_ref[...])

    def spec(m):
        def index(i, s):
            return m(s, 0)[0], i, 0
        return pl.BlockSpec((None, tm, D_MODEL), index)

    row = pl.BlockSpec((tm, D_MODEL), lambda i, s: (i, 0))
    vec = pl.BlockSpec((1, D_MODEL), lambda i, s: (0, 0))
    return pl.pallas_call(
        body, name="in_proj_xgrad", grid=(t // tm, 7),
        out_shape=[jax.ShapeDtypeStruct((t, D_MODEL), F32), jax.ShapeDtypeStruct((1, D_MODEL), F32)],
        in_specs=[spec(m) for m in maps] + [pl.BlockSpec((D_MODEL, D_MODEL), lambda i, s: (0, s)), row, row, vec],
        out_specs=[row, vec],
        scratch_shapes=[pltpu.VMEM((tm, D_MODEL), F32)],
        compiler_params=_params("arbitrary", "arbitrary"),
    )(d_conv, d_lru, d_gate, w_in, x, dx1, g1)


def _adamw(w, g, m, v):
    m = ADAM_B1 * m + (1.0 - ADAM_B1) * g
    v = ADAM_B2 * v + (1.0 - ADAM_B2) * (g * g)
    m_hat = m / (1.0 - ADAM_B1 ** ADAM_STEP)
    v_hat = v / (1.0 - ADAM_B2 ** ADAM_STEP)
    return -ADAM_LR * (m_hat / (jnp.sqrt(v_hat) + ADAM_EPS) + ADAM_WD * w), m, v


def _adam_large(ws, ms, vs, owns, others, name):
    n = len(ws)
    shape = ws[0].shape
    cols = shape[-1]
    flat = [[a.reshape(-1, cols) for a in group] for group in (ws, ms, vs)]
    rows = flat[0][0].shape[0]
    owns, others = [o.reshape(4, rows, cols) for o in owns], [o.reshape(3, rows, cols) for o in others]
    rb = _row_block(rows, 512)

    def body(*refs):
        ins, outs = refs[:5 * n], refs[5 * n:]
        for i in range(n):
            w_ref, m_ref, v_ref, own_ref, oth_ref = ins[i::n]
            g = own_ref[...].astype(F32)
            for k in range(3):
                g = g + oth_ref[k].astype(F32)
            outs[i][...] = g
            outs[n + i][...], outs[2 * n + i][...], outs[3 * n + i][...] = _adamw(w_ref[...], g, m_ref[...], v_ref[...])

    blk = pl.BlockSpec((rb, cols), lambda i: (i, 0))
    res = jax.ShapeDtypeStruct((rows, cols), F32)
    outs = pl.pallas_call(
        body, name=name, grid=(rows // rb,), out_shape=[res] * (4 * n),
        in_specs=[blk] * (3 * n) + [pl.BlockSpec((None, rb, cols), lambda i: (0, i, 0))] * n
        + [pl.BlockSpec((3, rb, cols), lambda i: (0, i, 0))] * n,
        out_specs=[blk] * (4 * n), compiler_params=_params("parallel"),
    )(*flat[0], *flat[1], *flat[2], *owns, *others)
    outs = [o.reshape(shape) for o in outs]
    return outs[:n], outs[n:2 * n], outs[2 * n:3 * n], outs[3 * n:]


def _adam_small(ws, gs, ms, vs):
    n = len(ws)

    def body(*refs):
        w_refs, g_refs, m_refs, v_refs = (refs[i * n:(i + 1) * n] for i in range(4))
        outs = refs[4 * n:]
        for i in range(n):
            d, m, v = _adamw(w_refs[i][...], g_refs[i][...], m_refs[i][...], v_refs[i][...])
            outs[i][...], outs[n + i][...], outs[2 * n + i][...] = d, m, v

    shapes = [jax.ShapeDtypeStruct(w.shape, F32) for w in ws]
    outs = pl.pallas_call(
        body, name="adam_small", out_shape=shapes * 3,
        in_specs=[VMEM_SPEC] * (4 * n), out_specs=[VMEM_SPEC] * (3 * n), compiler_params=_params(),
    )(*ws, *gs, *ms, *vs)
    return outs[:n], outs[n:2 * n], outs[2 * n:]


def _pack_rows(pieces):
    tile = SUBLANES * LANES
    return jnp.concatenate([jnp.pad(p.reshape(-1), (0, (-p.size) % tile)).reshape(-1, LANES) for p in pieces], axis=0)


def _packed_starts(sizes):
    tile = SUBLANES * LANES
    starts = [0]
    for s in sizes:
        starts.append(starts[-1] + (s + tile - 1) // tile * SUBLANES)
    return starts


def kernel(x, norm_mix_pre, norm_mix_post, norm_ffn_pre, norm_ffn_post, w_in, conv_short_w, w_conv_branch, lru_conv_w, lru_conv_b, lru_wa, lru_ba, lru_wx, lru_bx, lru_lambda, w_lru_branch, w_out, ffn_w_up, ffn_conv_w, ffn_conv_b, ffn_w_down, loss_target, m_norm_mix_pre, m_norm_mix_post, m_norm_ffn_pre, m_norm_ffn_post, m_w_in, m_conv_short_w, m_w_conv_branch, m_lru_conv_w, m_lru_conv_b, m_lru_wa, m_lru_ba, m_lru_wx, m_lru_bx, m_lru_lambda, m_w_lru_branch, m_w_out, m_ffn_w_up, m_ffn_conv_w, m_ffn_conv_b, m_ffn_w_down, v_norm_mix_pre, v_norm_mix_post, v_norm_ffn_pre, v_norm_ffn_post, v_w_in, v_conv_short_w, v_w_conv_branch, v_lru_conv_w, v_lru_conv_b, v_lru_wa, v_lru_ba, v_lru_wx, v_lru_bx, v_lru_lambda, v_w_lru_branch, v_w_out, v_ffn_w_up, v_ffn_conv_w, v_ffn_conv_b, v_ffn_w_down):
    t = x.shape[1]
    xi, yi, ci = _position()
    me = _block_of(xi, yi, ci)
    x2, target = x[0], loss_target[0]
    shard_in, shard_up = IN_COLS // N_DEV, 2 * D_FF // N_DEV
    shard_sq, shard_down, shard_head = D_MODEL // N_DEV, D_FF // N_DEV, HEAD_DIM // N_DEV

    names = ["w_in", "lru_wa", "lru_wx", "w_conv_branch", "w_lru_branch", "w_out", "ffn_w_up", "ffn_w_down"]
    large = [w_in[0], lru_wa[0], lru_wx[0], w_conv_branch[0], w_lru_branch[0], w_out[0], ffn_w_up[0], ffn_w_down[0]]
    blocks = [_cols(shard_in), _lead, _lead, _rows(shard_sq), _rows(shard_sq), _rows(shard_sq),
              _cols(shard_up), _rows(shard_down)]
    gate_full = (N_DEV, N_HEADS, shard_head, HEAD_DIM)
    full_shapes = [(D_MODEL, IN_COLS), gate_full, gate_full, (D_MODEL, D_MODEL), (D_MODEL, D_MODEL), (D_MODEL, D_MODEL),
                   (D_MODEL, 2 * D_FF), (D_FF, D_MODEL)]
    n_now = 3
    small_sharded = [conv_short_w, lru_conv_w, lru_ba, lru_bx, ffn_conv_w]
    small_mine = _pack_rows(small_sharded)
    small_at = _packed_starts([p.size for p in small_sharded])
    *gathered, small_all, proj, h = _gather_weights(large, blocks, full_shapes, small_mine, n_now, x2, norm_mix_pre)
    g_in, g_wa, g_wx = gathered[:n_now]
    later_blocks = blocks[n_now:]
    send1, recv1, later, gather_token = _gather_start(gathered[n_now:], later_blocks, "gather_start")

    def behind(token, operand):
        return operand + token[0:1, 0:1]

    def forward(lo, hi, after, tag):
        return _gather_forward(later[lo:hi], later_blocks[lo:hi], send1[4 * lo:4 * hi], recv1[4 * lo:4 * hi], after,
                               "gather_forward_" + tag)

    def finish(lo, hi, flight, after, tag):
        return _gather_finish(flight[2], later_blocks[lo:hi], flight[0], flight[1], after, "gather_finish_" + tag)

    def cols_of(r0, n, width):
        part = small_all[:, r0:r0 + n * width // LANES, :].reshape(N_DEV, n, width)
        return part.transpose(1, 0, 2).reshape(n, N_DEV * width)

    c_short = cols_of(small_at[0], 3, LANES)
    c_lru = cols_of(small_at[1], 4, LANES)
    b_a = cols_of(small_at[2], N_HEADS, shard_head).reshape(1, D_MODEL)
    b_x = cols_of(small_at[3], N_HEADS, shard_head).reshape(1, D_MODEL)
    c_ffn = cols_of(small_at[4], 3, shard_up)

    y_a = _conv_mixer_fwd(proj, behind(gather_token, c_short))
    y_b, hl, decay, lru_kept = _lru_fwd(proj, behind(gather_token, c_lru), lru_conv_b, g_wa, b_a, g_wx, b_x, lru_lambda)
    flight_mix_w = forward(0, 3, y_b, "mix")
    g_cb, g_lb, g_out = finish(0, 3, flight_mix_w, y_b, "mix")
    pa, pb, merged, mix, x1, h2 = _merge(y_a, y_b, proj, x2, g_cb, g_lb, g_out, norm_mix_post, norm_ffn_pre)
    flight_up_w = forward(3, 4, h2, "up")
    (g_up,) = finish(3, 4, flight_up_w, h2, "up")
    up, act, f = _ffn_up(h2, g_up, c_ffn, ffn_conv_b)
    flight_down_w = forward(4, 5, f, "down")
    (g_down,) = finish(4, 5, flight_down_w, f, "down")
    dy, d_out, d_act, dg4, loss_part = _ffn_down(f, act, g_down, x1, target, norm_ffn_post)

    block_of = dict(zip(names, blocks))
    shard_shapes = {"w_in": (D_MODEL, shard_in), "w_conv_branch": (shard_sq, D_MODEL), "w_lru_branch": (shard_sq, D_MODEL),
                    "w_out": (shard_sq, D_MODEL), "lru_wa": (N_HEADS, shard_head, HEAD_DIM),
                    "lru_wx": (N_HEADS, shard_head, HEAD_DIM), "ffn_w_up": (D_MODEL, shard_up),
                    "ffn_w_down": (shard_down, D_MODEL)}

    def reduce_start(tag, grads):
        keys = list(grads)
        sums = _reduce_pair([grads[k] for k in keys], [block_of[k] for k in keys], [shard_shapes[k] for k in keys],
                            "reduce_pair_" + tag)
        return (keys,) + _exchange_chips_start(sums, "reduce_chip_start_" + tag)

    gw_down = _grad_tn(f, d_out, min(512, D_FF), "ffn_down_wgrad")
    flight_down = reduce_start("down", {"ffn_w_down": gw_down})
    gw_up, gc_ffn, gb_ffn, d_h2 = _ffn_up_bwd(up, d_act, behind(flight_down[-1], c_ffn), h2, g_up)
    flight_up = reduce_start("up", {"ffn_w_up": gw_up})
    dx1, d_mix, d_pa, d_pb, d_ya, d_yb, d_gate, dg3, dg2 = _merge_bwd(
        dy, d_h2, x1, mix, behind(flight_up[-1], norm_ffn_pre), norm_mix_post, g_out, g_cb, g_lb, pa, pb, proj)
    gw_out = _grad_tn(merged, d_mix, CB, "w_out_wgrad")
    gw_cb = _grad_tn(y_a, d_pa, CB, "w_conv_branch_wgrad")
    gw_lb = _grad_tn(y_b, d_pb, CB, "w_lru_branch_wgrad")
    flight_mix = reduce_start("mix", {"w_conv_branch": gw_cb, "w_lru_branch": gw_lb, "w_out": gw_out})
    d_conv, gc_short = _conv_mixer_bwd(proj, d_ya, behind(flight_mix[-1], c_short))
    d_lru, gw_a, gw_x, g_lru_small = _lru_bwd(proj, hl, decay, lru_kept, d_yb, c_lru, g_wa, g_wx, lru_lambda)
    early = [dg2, dg3, dg4, g_lru_small[4:5], g_lru_small[7:8], gb_ffn, gc_short, g_lru_small[0:4],
             g_lru_small[5:6], g_lru_small[6:7], gc_ffn, loss_part]
    flight_small = _small_start(_pack_rows(early), "small_start")
    gw_in = _in_proj_wgrad(h, d_conv, d_lru, d_gate)
    flight_in = reduce_start("in", {"lru_wa": gw_a, "lru_wx": gw_x, "w_in": gw_in})
    dx, dg1 = _in_proj_xgrad(d_conv, d_lru, d_gate, g_in, x2, dx1,
                             behind(flight_small[-1], behind(flight_in[-1], norm_mix_pre)))
    flight_late = _small_start(_pack_rows([dg1]), "small_start_late")

    moments ={"w_in": (m_w_in, v_w_in), "w_conv_branch": (m_w_conv_branch, v_w_conv_branch),
               "w_lru_branch": (m_w_lru_branch, v_w_lru_branch), "w_out": (m_w_out, v_w_out),
               "lru_wa": (m_lru_wa, v_lru_wa), "lru_wx": (m_lru_wx, v_lru_wx), "ffn_w_up": (m_ffn_w_up, v_ffn_w_up),
               "ffn_w_down": (m_ffn_w_down, v_ffn_w_down)}
    weights = {"w_in": w_in, "w_conv_branch": w_conv_branch, "w_lru_branch": w_lru_branch, "w_out": w_out,
               "lru_wa": lru_wa, "lru_wx": lru_wx, "ffn_w_up": ffn_w_up, "ffn_w_down": ffn_w_down}
    out_g, out_d, out_m, out_v = {}, {}, {}, {}

    after = flight_late[-1]
    for tag, (keys, send, recv, sums, lands, _) in (("down", flight_down), ("up", flight_up), ("mix", flight_mix),
                                                    ("in", flight_in)):
        sums, others = _exchange_chips_wait(send, recv, sums, lands, after, "reduce_chip_wait_" + tag)
        by_key = dict(zip(keys, zip(sums, others)))
        for shape in dict.fromkeys(shard_shapes[k] for k in keys):
            same = [k for k in keys if shard_shapes[k] == shape]
            results = _adam_large([weights[k] for k in same], [moments[k][0] for k in same], [moments[k][1] for k in same],
                                  [by_key[k][0] for k in same], [by_key[k][1] for k in same], "adam_" + same[0])
            for out, values in zip((out_g, out_d, out_m, out_v), results):
                out.update(zip(same, values))
        after = out_d[keys[-1]]

    total, total_late = _small_sum([_small_wait(*flight_small[:4], after, "small_wait"),
                                    _small_wait(*flight_late[:4], after, "small_wait_late")], me)
    sizes = [p.size for p in early]
    starts = _packed_starts(sizes)

    def piece(i, shape):
        if i == 0:
            return total_late.reshape(-1)[:D_MODEL].reshape(shape)
        return total[starts[i - 1]:starts[i]].reshape(-1)[:sizes[i - 1]].reshape(shape)

    loss = total[starts[11], 0]

    def col_shard(full, width):
        return lax.dynamic_slice_in_dim(full, me * width, width, axis=1)

    def head_shard(full):
        return lax.dynamic_slice_in_dim(full.reshape(N_HEADS, HEAD_DIM), me * shard_head, shard_head, axis=1)

    small_names = ["norm_mix_pre", "norm_mix_post", "norm_ffn_pre", "norm_ffn_post", "lru_conv_b", "lru_lambda",
                   "ffn_conv_b", "conv_short_w", "lru_conv_w", "lru_ba", "lru_bx", "ffn_conv_w"]
    small_g = [piece(0, (1, D_MODEL)), piece(1, (1, D_MODEL)), piece(2, (1, D_MODEL)), piece(3, (1, D_MODEL)),
               piece(4, (1, D_MODEL)), piece(5, (1, D_MODEL)), piece(6, (1, 2 * D_FF)),
               col_shard(piece(7, (3, D_MODEL)), LANES), col_shard(piece(8, (4, D_MODEL)), LANES),
               head_shard(piece(9, (1, D_MODEL))), head_shard(piece(10, (1, D_MODEL))),
               col_shard(piece(11, (3, 2 * D_FF)), shard_up)]
    small_w = [norm_mix_pre, norm_mix_post, norm_ffn_pre, norm_ffn_post, lru_conv_b, lru_lambda, ffn_conv_b,
               conv_short_w[0], lru_conv_w[0], lru_ba[0], lru_bx[0], ffn_conv_w[0]]
    small_m = [m_norm_mix_pre, m_norm_mix_post, m_norm_ffn_pre, m_norm_ffn_post, m_lru_conv_b, m_lru_lambda,
               m_ffn_conv_b, m_conv_short_w[0], m_lru_conv_w[0], m_lru_ba[0], m_lru_bx[0], m_ffn_conv_w[0]]
    small_v = [v_norm_mix_pre, v_norm_mix_post, v_norm_ffn_pre, v_norm_ffn_post, v_lru_conv_b, v_lru_lambda,
               v_ffn_conv_b, v_conv_short_w[0], v_lru_conv_w[0], v_lru_ba[0], v_lru_bx[0], v_ffn_conv_w[0]]
    s_d, s_m, s_v = _adam_small(small_w, small_g, small_m, small_v)
    for i, name in enumerate(small_names):
        shape = small_w[i].shape if i < 7 else (1,) + small_w[i].shape
        out_g[name] = small_g[i].reshape(shape)
        out_d[name], out_m[name], out_v[name] = s_d[i].reshape(shape), s_m[i].reshape(shape), s_v[i].reshape(shape)

    order = ["norm_mix_pre", "norm_mix_post", "norm_ffn_pre", "norm_ffn_post", "w_in", "conv_short_w", "w_conv_branch",
             "lru_conv_w", "lru_conv_b", "lru_wa", "lru_ba", "lru_wx", "lru_bx", "lru_lambda", "w_lru_branch", "w_out",
             "ffn_w_up", "ffn_conv_w", "ffn_conv_b", "ffn_w_down"]
    return (loss, dx.reshape(1, t, D_MODEL), *[out_g[k] for k in order], *[out_d[k] for k in order],
            *[out_m[k] for k in order], *[out_v[k] for k in order])
```

```python
import functools
import math

import jax
import jax.numpy as jnp
from jax import lax
from jax.experimental import pallas as pl
from jax.experimental.pallas import tpu as pltpu

F32 = jnp.float32
BF16 = jnp.bfloat16
MESH = pl.DeviceIdType.MESH

N_DEV = 8
D_MODEL = 1024
N_HEADS = 4
HEAD_DIM = D_MODEL // N_HEADS
D_FF = 3 * D_MODEL
IN_COLS = 7 * D_MODEL
LRU_C = 8.0
RMS_EPS = 1e-6
ADAM_LR = 0.001
ADAM_B1 = 0.9
ADAM_B2 = 0.999
ADAM_EPS = 1e-08
ADAM_WD = 0.01
ADAM_STEP = 10
GELU_K = math.sqrt(2.0 / math.pi)
GELU_C = 0.044715

LANES = 128
SUBLANES = 8
PAD = SUBLANES
VMEM_LIMIT = 56 * 1024 * 1024
CB = 256
ROW_SLICE = 32
SCAN_UNROLL = 4

HBM_SPEC = pl.BlockSpec(memory_space=pltpu.HBM)
SEM_SPEC = pl.BlockSpec(memory_space=pltpu.SEMAPHORE)
DATAFLOW_EFFECT = pltpu.SideEffectType.DATAFLOW_SIDE_EFFECTING
VMEM_SPEC = pl.BlockSpec(memory_space=pltpu.VMEM)


def _params(*sem):
    if sem:
        return pltpu.CompilerParams(dimension_semantics=sem, vmem_limit_bytes=VMEM_LIMIT)
    return pltpu.CompilerParams(vmem_limit_bytes=VMEM_LIMIT)


def _row_chunk(t):
    return min(256, t)


def _row_block(rows, cap):
    return next(rb for rb in range(min(cap, rows), 0, -16) if rows % rb == 0)


def _gelu(x):
    return 0.5 * x * (1.0 + jnp.tanh(GELU_K * (x + GELU_C * x * x * x)))


def _gelu_and_grad(x):
    t = jnp.tanh(GELU_K * (x + GELU_C * x * x * x))
    g = 0.5 * x * (1.0 + t)
    dg = 0.5 * (1.0 + t) + 0.5 * x * (1.0 - t * t) * GELU_K * (1.0 + 3.0 * GELU_C * x * x)
    return g, dg


def _expm1_neg(x):
    series = x * (1.0 + x * (0.5 + x * (1.0 / 6.0 + x * (1.0 / 24.0 + x * (1.0 / 120.0)))))
    return jnp.where(x > -0.05, series, jnp.exp(x) - 1.0)


def _log_sigmoid(x):
    return jnp.minimum(x, 0.0) - jnp.log1p(jnp.exp(-jnp.abs(x)))


def _dot(a, b):
    return jnp.dot(a, b, preferred_element_type=F32)


def _dot_nt(a, b):
    return lax.dot_general(a, b, (((1,), (1,)), ((), ())), preferred_element_type=F32)


def _dot_tn(a, b):
    return lax.dot_general(a, b, (((0,), (0,)), ((), ())), preferred_element_type=F32)


def _rms_fwd(x):
    r = lax.rsqrt(jnp.mean(x * x, axis=-1, keepdims=True) + RMS_EPS)
    return x * r, r


def _rms_bwd(n, r, gdy):
    return r * (gdy - n * jnp.mean(n * gdy, axis=-1, keepdims=True))


def _rows_back(pad_ref, r0, rows, j):
    cur = pad_ref[pl.ds(PAD + r0, rows), :]
    if j == 0:
        return cur
    before = pad_ref[pl.ds(PAD + r0 - SUBLANES, SUBLANES), :]
    row = lax.broadcasted_iota(jnp.int32, before.shape, 0)
    rolled = pltpu.roll(cur, j, 0)
    top = jnp.where(row < j, pltpu.roll(before, j, 0), rolled[0:SUBLANES, :])
    return jnp.concatenate([top, rolled[SUBLANES:, :]], axis=0)


def _rows_ahead(pad_ref, r0, rows, j):
    cur = pad_ref[pl.ds(r0, rows), :]
    if j == 0:
        return cur
    after = pad_ref[pl.ds(r0 + rows, SUBLANES), :]
    row = lax.broadcasted_iota(jnp.int32, after.shape, 0)
    rolled = pltpu.roll(cur, rows - j, 0)
    bottom = jnp.where(row >= SUBLANES - j, pltpu.roll(after, SUBLANES - j, 0), rolled[rows - SUBLANES:, :])
    return jnp.concatenate([rolled[:rows - SUBLANES, :], bottom], axis=0)


def _fold_rows(v):
    return v.reshape(v.shape[0] // SUBLANES, SUBLANES, v.shape[1]).sum(axis=0)


def _conv_causal(pad_ref, w, r0, rows, taps):
    acc = None
    for k in range(taps):
        term = w[k:k + 1, :] * _rows_back(pad_ref, r0, rows, taps - 1 - k)
        acc = term if acc is None else acc + term
    return acc


def _conv_anticausal(pad_ref, w, r0, rows, taps):
    acc = None
    for k in range(taps):
        term = w[k:k + 1, :] * _rows_ahead(pad_ref, r0, rows, taps - 1 - k)
        acc = term if acc is None else acc + term
    return acc


def _conv_wgrad(g, xpad_ref, r0, rows, taps):
    return [jnp.sum(g * _rows_back(xpad_ref, r0, rows, taps - 1 - k), axis=0, keepdims=True) for k in range(taps)]


def _position():
    return lax.axis_index("x"), lax.axis_index("y"), lax.axis_index("c")


def _block_of(x, y, c):
    return 4 * x + 2 * y + c


def _chip(x, y, k):
    return (x + (k & 1)) % 2, (y + (k >> 1)) % 2


def _cols(width):
    def at(ref, d, half=None):
        cols = pl.ds(pl.multiple_of(d * width, LANES), width)
        if half is None:
            return ref.at[:, cols]
        return ref.at[pl.ds(half * (ref.shape[0] // 2), ref.shape[0] // 2), cols]
    return at


def _rows(height):
    def at(ref, d, half=None):
        if half is None:
            return ref.at[pl.ds(pl.multiple_of(d * height, 16), height), :]
        return ref.at[pl.ds(pl.multiple_of(d * height + half * (height // 2), 16), height // 2), :]
    return at


def _lead(ref, d, half=None):
    if half is None:
        return ref.at[d]
    return ref.at[d, pl.ds(half * (ref.shape[1] // 2), ref.shape[1] // 2)]


def _gather_weights(shards, blocks, full_shapes, small, n_now, tokens, gain):
    n = len(shards)
    small_rows = small.shape[0]
    t = tokens.shape[0]
    rc = min(512, t)

    def body(*refs):
        ins, small_in, x_ref, g_ref = refs[:n], refs[n], refs[n + 1], refs[n + 2]
        outs, small_out, proj_ref, h_ref = refs[n + 3:2 * n + 3], refs[2 * n + 3], refs[2 * n + 4], refs[2 * n + 5]
        stage = refs[2 * n + 6:3 * n + 6]
        w_buf, p_buf, send, recv, local, w_sem, p_sem = refs[3 * n + 6:]
        x, y, c = _position()
        me = _block_of(x, y, c)
        sibling = (x, y, 1 - c)

        for a in range(n):
            stage[a][...] = ins[a][...].astype(BF16)
        for r0 in range(0, t, rc):
            normed, _ = _rms_fwd(x_ref[pl.ds(r0, rc), :])
            h_ref[pl.ds(r0, rc), :] = (normed * g_ref[...]).astype(BF16)
        stores = []

        def project(w_ref, block):
            i = len(stores)
            if i >= 2:
                stores[i - 2].wait()
            for r0 in range(0, t, rc):
                p_buf[i % 2, pl.ds(r0, rc), :] = _dot(h_ref[pl.ds(r0, rc), :], w_ref[...]).astype(BF16)
            st = pltpu.make_async_copy(p_buf.at[i % 2], blocks[0](proj_ref, block), p_sem.at[i % 2])
            st.start()
            stores.append(st)

        def project_landed(block):
            ld = pltpu.make_async_copy(blocks[0](outs[0], block), w_buf, w_sem)
            ld.start()
            ld.wait()
            project(w_buf, block)

        def copy(a, k, block, to, src=None, half=None):
            dst = blocks[a](outs[a], block, half)
            return pltpu.make_async_remote_copy(
                src_ref=dst if src is None else src, dst_ref=dst, send_sem=send.at[a, k], recv_sem=recv.at[a, k],
                device_id=to, device_id_type=MESH)

        def small_copy(k):
            px, py, pc = (x + (k & 1)) % 2, (y + ((k >> 1) & 1)) % 2, (c + (k >> 2)) % 2
            return pltpu.make_async_remote_copy(
                src_ref=small_in, dst_ref=small_out.at[me], send_sem=send.at[n_now, k - 1], recv_sem=recv.at[n_now, k - 1],
                device_id=(px, py, pc), device_id_type=MESH)

        def small_arrival(k):
            px, py, pc = (x + (k & 1)) % 2, (y + ((k >> 1) & 1)) % 2, (c + (k >> 2)) % 2
            return pltpu.make_async_remote_copy(
                src_ref=small_in, dst_ref=small_out.at[_block_of(px, py, pc)], send_sem=send.at[n_now, k - 1],
                recv_sem=recv.at[n_now, k - 1], device_id=(px, py, pc), device_id_type=MESH)

        small_out[me] = small_in[...]
        small_sends = [small_copy(k) for k in range(1, N_DEV)]
        for cp in small_sends:
            cp.start()

        mine, first, passed = [], [], []
        for a in range(n):
            own = pltpu.make_async_copy(stage[a], blocks[a](outs[a], me), local.at[a])
            own.start()
            mine.append(own)
            if a >= n_now:
                continue
            sends = [copy(a, 0, me, sibling, src=stage[a])]
            sends += [copy(a, k, me, (*_chip(x, y, k), c), src=stage[a]) for k in (1, 2)]
            for cp in sends:
                cp.start()
            first += sends

        here = (x, y, c)
        across = [(*_chip(x, y, k), c) for k in (1, 2)]
        near = [[_block_of(*_chip(x, y, k), cc) for k in (1, 2)] for cc in (c, 1 - c)]
        far = [_block_of(*_chip(x, y, 3), cc) for cc in (c, 1 - c)]

        def launch(cp):
            cp.start()
            passed.append(cp)

        project(stage[0], me)
        copy(0, 0, _block_of(x, y, 1 - c), here).wait_recv()
        project_landed(_block_of(x, y, 1 - c))
        for a in range(n_now):
            for i in (0, 1):
                copy(a, 1 + i, near[0][i], here).wait_recv()
                launch(copy(a, 3 + i, near[0][i], across[1 - i], half=i))
                launch(copy(a, 5 + i, near[0][i], sibling))
            if a == 0:
                project_landed(near[0][0])
                project_landed(near[0][1])
        for i in (0, 1):
            copy(0, 5 + i, near[1][i], here).wait_recv()
            project_landed(near[1][i])
        for a in range(n_now):
            for i in (0, 1):
                copy(a, 3 + i, far[0], here, half=i).wait_recv()
                launch(copy(a, 7 + i, far[0], sibling, half=i))
            if a == 0:
                project_landed(far[0])
        for a in range(n_now):
            if a > 0:
                copy(a, 0, _block_of(x, y, 1 - c), here).wait_recv()
                for i in (0, 1):
                    copy(a, 5 + i, near[1][i], here).wait_recv()
            for i in (0, 1):
                copy(a, 7 + i, far[1], here, half=i).wait_recv()
            if a == 0:
                project_landed(far[1])
        for k in range(1, N_DEV):
            small_arrival(k).wait_recv()
        for cp in first + passed + small_sends:
            cp.wait_send()
        for done in mine + stores[-2:]:
            done.wait()

    out_shape = [jax.ShapeDtypeStruct(s, BF16) for s in full_shapes]
    out_shape += [jax.ShapeDtypeStruct((N_DEV, small_rows, LANES), F32), jax.ShapeDtypeStruct((t, full_shapes[0][1]), BF16),
                  jax.ShapeDtypeStruct(tokens.shape, BF16)]
    return pl.pallas_call(
        body, name="gather_weights", out_shape=out_shape,
        in_specs=[VMEM_SPEC] * (n + 3), out_specs=[HBM_SPEC] * n + [VMEM_SPEC, HBM_SPEC, VMEM_SPEC],
        scratch_shapes=[pltpu.VMEM(s.shape, BF16) for s in shards]
        + [pltpu.VMEM(shards[0].shape, BF16), pltpu.VMEM((2, t, shards[0].shape[1]), BF16),
           pltpu.SemaphoreType.DMA((n_now + 1, 9)), pltpu.SemaphoreType.DMA((n_now + 1, 9)),
           pltpu.SemaphoreType.DMA((n,)), pltpu.SemaphoreType.DMA(()), pltpu.SemaphoreType.DMA((2,))],
        compiler_params=_params(),
    )(*shards, small, tokens, gain)


def _gather_first(full, blocks, send, recv):
    x, y, c = _position()
    me = _block_of(x, y, c)
    peers = [(x, y, 1 - c)] + [(*_chip(x, y, k), c) for k in (1, 2, 3)]

    def copy(a, k, block):
        at = blocks[a](full[a], block)
        return pltpu.make_async_remote_copy(src_ref=at, dst_ref=at, send_sem=send[4 * a + k], recv_sem=recv[4 * a + k],
                                            device_id=peers[k], device_id_type=MESH)

    sends = [copy(a, k, me) for a in range(len(full)) for k in range(4)]
    arrivals = [copy(a, k, _block_of(*peers[k])) for a in range(len(full)) for k in range(4)]
    return sends, arrivals


def _gather_second(full, blocks, send, recv):
    x, y, c = _position()

    def copy(a, k, cc):
        at = blocks[a](full[a], _block_of(*_chip(x, y, k), cc))
        return pltpu.make_async_remote_copy(src_ref=at, dst_ref=at, send_sem=send[3 * a + k - 1],
                                            recv_sem=recv[3 * a + k - 1], device_id=(x, y, 1 - c), device_id_type=MESH)

    sends = [copy(a, k, c) for a in range(len(full)) for k in (1, 2, 3)]
    arrivals = [copy(a, k, 1 - c) for a in range(len(full)) for k in (1, 2, 3)]
    return sends, arrivals


def _split_call(body, name, arrays, sems_in, n_sems_out, after=None, token=False):
    n, m = len(arrays), len(sems_in)

    def kernel_body(*refs):
        outs = refs[n + m + (after is not None):]
        body(refs[:n], refs[n:n + m], outs[:n_sems_out])
        if token:
            outs[-1][...] = jnp.zeros_like(outs[-1])

    extra_in = [] if after is None else [after]
    outs = pl.pallas_call(
        kernel_body, name=name,
        out_shape=(*[pltpu.SemaphoreType.DMA(())] * n_sems_out, *[pltpu.HBM(a.shape, a.dtype) for a in arrays],
                   *([jax.ShapeDtypeStruct((SUBLANES, LANES), F32)] if token else [])),
        in_specs=[HBM_SPEC] * n + [SEM_SPEC] * m + [pl.BlockSpec(memory_space=pl.ANY)] * len(extra_in),
        out_specs=(*[SEM_SPEC] * n_sems_out, *[HBM_SPEC] * n, *([VMEM_SPEC] if token else [])),
        input_output_aliases={i: n_sems_out + i for i in range(n)},
        compiler_params=pltpu.CompilerParams(has_side_effects=DATAFLOW_EFFECT),
    )(*[pltpu.with_memory_space_constraint(a, pltpu.HBM) for a in arrays], *sems_in, *extra_in)
    sems, rest = list(outs[:n_sems_out]), list(outs[n_sems_out:])
    return (sems, rest[:n], rest[n]) if token else (sems, rest[:n])


def _gather_start(full, blocks, name):
    n = len(full)

    def body(arrays, _, sems):
        for cp in _gather_first(arrays, blocks, sems[:4 * n], sems[4 * n:])[0]:
            cp.start()

    sems, arrays, token = _split_call(body, name, full, [], 8 * n, token=True)
    return sems[:4 * n], sems[4 * n:], arrays, token


def _gather_forward(full, blocks, send_first, recv_first, after, name):
    n = len(full)

    def body(arrays, sems_in, sems):
        sends, arrivals = _gather_first(arrays, blocks, sems_in[:4 * n], sems_in[4 * n:])
        for cp in arrivals:
            cp.wait_recv()
        for cp in _gather_second(arrays, blocks, sems[:3 * n], sems[3 * n:])[0]:
            cp.start()
        for cp in sends:
            cp.wait_send()

    sems, arrays = _split_call(body, name, full, [*send_first, *recv_first], 6 * n, after=after)
    return sems[:3 * n], sems[3 * n:], arrays


def _gather_finish(full, blocks, send_second, recv_second, after, name):
    n = len(full)

    def body(arrays, sems_in, _):
        sends, arrivals = _gather_second(arrays, blocks, sems_in[:3 * n], sems_in[3 * n:])
        for cp in sends:
            cp.wait_send()
        for cp in arrivals:
            cp.wait_recv()

    return _split_call(body, name, full, [*send_second, *recv_second], 0, after=after)[1]


def _reduce_pair(grads, blocks, shard_shapes, name):
    n = len(grads)

    def body(*refs):
        ins, outs = refs[:n], refs[n:2 * n]
        got, own = refs[2 * n:3 * n], refs[3 * n:4 * n]
        send, recv, local = refs[4 * n:]
        x, y, c = _position()
        copies, loads = [], []
        for a in range(n):
            for k in range(4):
                chip = _chip(x, y, k)
                cp = pltpu.make_async_remote_copy(
                    src_ref=blocks[a](ins[a], _block_of(*chip, 1 - c)), dst_ref=got[a].at[k],
                    send_sem=send.at[a, k], recv_sem=recv.at[a, k], device_id=(x, y, 1 - c), device_id_type=MESH)
                cp.start()
                copies.append(cp)
                ld = pltpu.make_async_copy(blocks[a](ins[a], _block_of(*chip, c)), own[a].at[k], local.at[a, k])
                ld.start()
                loads.append(ld)
        for a in range(n):
            for k in range(4):
                loads[4 * a + k].wait()
                copies[4 * a + k].wait_recv()
                outs[a][k] = (own[a][k].astype(F32) + got[a][k].astype(F32)).astype(BF16)
        for cp in copies:
            cp.wait_send()

    slots = [(4,) + tuple(s) for s in shard_shapes]
    return pl.pallas_call(
        body, name=name, out_shape=[jax.ShapeDtypeStruct(s, BF16) for s in slots],
        in_specs=[HBM_SPEC] * n, out_specs=[VMEM_SPEC] * n,
        scratch_shapes=[pltpu.VMEM(s, BF16) for s in slots] * 2
        + [pltpu.SemaphoreType.DMA((n, 4)), pltpu.SemaphoreType.DMA((n, 4)), pltpu.SemaphoreType.DMA((n, 4))],
        compiler_params=_params(),
    )(*grads)


def _chip_copies(sums, lands, send, recv):
    x, y, c = _position()
    return [pltpu.make_async_remote_copy(
        src_ref=sums[a].at[k], dst_ref=lands[a].at[k - 1], send_sem=send[3 * a + k - 1], recv_sem=recv[3 * a + k - 1],
        device_id=(*_chip(x, y, k), c), device_id_type=MESH) for a in range(len(sums)) for k in (1, 2, 3)]


def _exchange_chips_start(pair_sums, name):
    n = len(pair_sums)
    lands = [pltpu.with_memory_space_constraint(lax.empty((3,) + tuple(p.shape[1:]), BF16), pltpu.HBM) for p in pair_sums]

    def body(*refs):
        sums, zones = refs[:n], refs[n:2 * n]
        send, recv = refs[2 * n:5 * n], refs[5 * n:8 * n]
        token = refs[-1]
        for cp in _chip_copies(sums, zones, send, recv):
            cp.start()
        token[...] = jnp.zeros_like(token)

    outs = pl.pallas_call(
        body, name=name,
        out_shape=(*[pltpu.SemaphoreType.DMA(())] * (6 * n),
                   *[pltpu.HBM(p.shape, BF16) for p in pair_sums], *[pltpu.HBM(z.shape, BF16) for z in lands],
                   jax.ShapeDtypeStruct((SUBLANES, LANES), F32)),
        in_specs=[HBM_SPEC] * (2 * n), out_specs=(*[SEM_SPEC] * (6 * n), *[HBM_SPEC] * (2 * n), VMEM_SPEC),
        input_output_aliases={i: 6 * n + i for i in range(2 * n)},
        compiler_params=pltpu.CompilerParams(has_side_effects=DATAFLOW_EFFECT),
    )(*[pltpu.with_memory_space_constraint(p, pltpu.HBM) for p in pair_sums], *lands)
    return outs[:3 * n], outs[3 * n:6 * n], outs[6 * n:7 * n], outs[7 * n:8 * n], outs[-1]


def _exchange_chips_wait(send, recv, sums, lands, after, name):
    n = len(sums)

    def body(*refs):
        sums_in, zones = refs[:n], refs[n:2 * n]
        send_in, recv_in = refs[2 * n:5 * n], refs[5 * n:8 * n]
        for cp in _chip_copies(sums_in, zones, send_in, recv_in):
            cp.wait_send()
            cp.wait_recv()

    outs = pl.pallas_call(
        body, name=name,
        out_shape=(*[pltpu.HBM(p.shape, BF16) for p in sums], *[pltpu.HBM(z.shape, BF16) for z in lands]),
        in_specs=[HBM_SPEC] * (2 * n) + [SEM_SPEC] * (6 * n) + [pl.BlockSpec(memory_space=pl.ANY)],
        out_specs=[HBM_SPEC] * (2 * n), input_output_aliases={i: i for i in range(2 * n)},
        compiler_params=pltpu.CompilerParams(has_side_effects=DATAFLOW_EFFECT),
    )(*sums, *lands, *send, *recv, after)
    return outs[:n], outs[n:]


def _small_copies(mine, land, send, recv):
    x, y, c = _position()
    me = _block_of(x, y, c)

    def peer(k):
        return (x + (k & 1)) % 2, (y + ((k >> 1) & 1)) % 2, (c + (k >> 2)) % 2

    def copy(k, slot):
        return pltpu.make_async_remote_copy(src_ref=mine, dst_ref=land.at[slot], send_sem=send[k - 1], recv_sem=recv[k - 1],
                                            device_id=peer(k), device_id_type=MESH)

    return [copy(k, me) for k in range(1, N_DEV)], [copy(k, _block_of(*peer(k))) for k in range(1, N_DEV)]


def _small_start(part, name):
    land = jnp.zeros((N_DEV,) + part.shape, F32)

    def body(arrays, _, sems):
        for cp in _small_copies(arrays[0], arrays[1], sems[:7], sems[7:])[0]:
            cp.start()

    sems, arrays, token = _split_call(body, name, [part, land], [], 14, token=True)
    return sems[:7], sems[7:], arrays[0], arrays[1], token


def _small_wait(send, recv, part, land, after, name):
    def body(arrays, sems_in, _):
        sends, arrivals = _small_copies(arrays[0], arrays[1], sems_in[:7], sems_in[7:])
        for cp in sends:
            cp.wait_send()
        for cp in arrivals:
            cp.wait_recv()

    return _split_call(body, name, [part, land], [*send, *recv], 0, after=after)[1]


def _small_sum(pairs, me):
    n = len(pairs)

    def body(me_ref, *refs):
        for i in range(n):
            mine, land, out = refs[2 * i], refs[2 * i + 1], refs[2 * n + i]
            total = jnp.zeros(mine.shape, F32)
            for d in range(N_DEV):
                total = total + land[d] + jnp.where(me_ref[0] == d, mine[...], 0.0)
            out[...] = total

    flat = [a for pair in pairs for a in pair]
    return pl.pallas_call(
        body, name="small_sum", out_shape=[jax.ShapeDtypeStruct(mine.shape, F32) for mine, _ in pairs],
        in_specs=[pl.BlockSpec(memory_space=pltpu.SMEM)] + [VMEM_SPEC] * (2 * n), out_specs=[VMEM_SPEC] * n,
        compiler_params=_params(),
    )(me.reshape(1).astype(jnp.int32), *flat)


def _section(s, t):
    return pl.BlockSpec((t, CB), lambda h, s=s: (0, s * (D_MODEL // CB) + h))


def _conv_mixer_fwd(proj, w_short):
    t = proj.shape[0]
    rc = _row_chunk(t)

    def body(b_ref, c_ref, x_ref, w_ref, y_ref, pad):
        pad[pl.ds(0, PAD), :] = jnp.zeros((PAD, CB), F32)
        for r0 in range(0, t, rc):
            rows = pl.ds(r0, rc)
            pad[pl.ds(PAD + r0, rc), :] = c_ref[rows, :].astype(F32) * x_ref[rows, :].astype(F32)
        w = w_ref[...]
        for r0 in range(0, t, rc):
            rows = pl.ds(r0, rc)
            y_ref[rows, :] = (b_ref[rows, :].astype(F32) * _conv_causal(pad, w, r0, rc, 3)).astype(BF16)

    return pl.pallas_call(
        body, name="conv_mixer_fwd", grid=(D_MODEL // CB,),
        out_shape=jax.ShapeDtypeStruct((t, D_MODEL), BF16),
        in_specs=[_section(0, t), _section(1, t), _section(2, t), pl.BlockSpec((3, CB), lambda h: (0, h))],
        out_specs=pl.BlockSpec((t, CB), lambda h: (0, h)),
        scratch_shapes=[pltpu.VMEM((t + PAD, CB), F32)],
        compiler_params=_params("parallel"),
    )(proj, proj, proj, w_short)


def _lru_gates(xl, wa, ba, wx, bx, ls, first_row):
    xb = xl.astype(BF16)
    ra = jax.nn.sigmoid(_dot(xb, wa) + ba)
    ia = jax.nn.sigmoid(_dot(xb, wx) + bx)
    la = LRU_C * ra * ls
    a = jnp.exp(la)
    one_minus = -_expm1_neg(2.0 * la)
    mult = jnp.where(first_row, 1.0, jnp.sqrt(one_minus))
    return xb, ra, ia, a, one_minus, mult


def _head_specs():
    vec = pl.BlockSpec((1, CB), lambda h: (0, h))
    mat = pl.BlockSpec((N_DEV, None, HEAD_DIM // N_DEV, HEAD_DIM), lambda h: (0, h, 0, 0))
    return vec, mat


def _lru_fwd(proj, w_conv, b_conv, wa, ba, wx, bx, lam):
    t = proj.shape[0]
    rc = _row_chunk(t)
    vec, mat = _head_specs()

    def body(lx_ref, ly_ref, wc_ref, bc_ref, wa_ref, ba_ref, wx_ref, bx_ref, lam_ref, yb_ref, hl_ref, a_ref, kept_ref,
             pad, u_s):
        pad[pl.ds(0, PAD), :] = jnp.zeros((PAD, CB), F32)
        for r0 in range(0, t, rc):
            pad[pl.ds(PAD + r0, rc), :] = lx_ref[pl.ds(r0, rc), :].astype(F32)
        wc, bc = wc_ref[...], bc_ref[...]
        wa_m, wx_m = wa_ref[...].reshape(HEAD_DIM, HEAD_DIM), wx_ref[...].reshape(HEAD_DIM, HEAD_DIM)
        ls = _log_sigmoid(lam_ref[...])
        for r0 in range(0, t, rc):
            rows = pl.ds(r0, rc)
            xl = _conv_causal(pad, wc, r0, rc, 4) + bc
            first = (lax.broadcasted_iota(jnp.int32, (rc, CB), 0) + r0) == 0
            xb, ra, ia, a, _, mult = _lru_gates(xl, wa_m, ba_ref[...], wx_m, bx_ref[...], ls, first)
            a_ref[rows, :] = a
            u_s[rows, :] = mult * (ia * xl)
            kept_ref[0, rows, :] = xb
            kept_ref[1, rows, :] = ra.astype(BF16)
            kept_ref[2, rows, :] = ia.astype(BF16)

        row = lax.broadcasted_iota(jnp.int32, (SUBLANES, CB), 0)

        def group(g, carry):
            r = pl.multiple_of(g * SUBLANES, SUBLANES)
            a_g, b_g = a_ref[pl.ds(r, SUBLANES), :], u_s[pl.ds(r, SUBLANES), :]
            for s in (1, 2, 4):
                keep = row >= s
                b_g = jnp.where(keep, a_g * pltpu.roll(b_g, s, 0) + b_g, b_g)
                a_g = jnp.where(keep, a_g * pltpu.roll(a_g, s, 0), a_g)
            h_g = b_g + a_g * carry
            hl_ref[pl.ds(r, SUBLANES), :] = h_g
            return jnp.broadcast_to(h_g[SUBLANES - 1:SUBLANES, :], (SUBLANES, CB))

        def trip(i, carry):
            for j in range(SCAN_UNROLL):
                carry = group(i * SCAN_UNROLL + j, carry)
            return carry

        lax.fori_loop(0, t // SUBLANES // SCAN_UNROLL, trip, jnp.zeros((SUBLANES, CB), F32))
        for r0 in range(0, t, rc):
            rows = pl.ds(r0, rc)
            yb_ref[rows, :] = (hl_ref[rows, :] * _gelu(ly_ref[rows, :].astype(F32))).astype(BF16)

    blk = pl.BlockSpec((t, CB), lambda h: (0, h))
    res = jax.ShapeDtypeStruct((t, D_MODEL), F32)
    return pl.pallas_call(
        body, name="lru_fwd", grid=(N_HEADS,),
        out_shape=[jax.ShapeDtypeStruct((t, D_MODEL), BF16), res, res, jax.ShapeDtypeStruct((3, t, D_MODEL), BF16)],
        in_specs=[_section(3, t), _section(4, t), pl.BlockSpec((4, CB), lambda h: (0, h)), vec, mat, vec, mat, vec, vec],
        out_specs=[blk, blk, blk, pl.BlockSpec((3, t, CB), lambda h: (0, 0, h))],
        scratch_shapes=[pltpu.VMEM((t + PAD, CB), F32), pltpu.VMEM((t, CB), F32)],
        compiler_params=_params("parallel"),
    )(proj, proj, w_conv, b_conv, wa, ba, wx, bx, lam)


def _merge(y_a, y_b, proj, x, w_cb, w_lb, w_out, g2, g3):
    t = x.shape[0]
    tm = min(512, t)

    def body(ya_ref, yb_ref, gc_ref, gl_ref, x_ref, wcb_ref, wlb_ref, wo_ref, g2_ref, g3_ref,
             pa_ref, pb_ref, mg_ref, mix_ref, x1_ref, h2_ref):
        pa = _dot(ya_ref[...], wcb_ref[...]).astype(BF16)
        pb = _dot(yb_ref[...], wlb_ref[...]).astype(BF16)
        pa_ref[...] = pa
        pb_ref[...] = pb
        merged = (jax.nn.sigmoid(gc_ref[...].astype(F32)) * pa.astype(F32)
                  + jax.nn.sigmoid(gl_ref[...].astype(F32)) * pb.astype(F32)).astype(BF16)
        mg_ref[...] = merged
        mix = _dot(merged, wo_ref[...])
        mix_ref[...] = mix
        n2, _ = _rms_fwd(mix)
        x1 = x_ref[...] + n2 * g2_ref[...]
        x1_ref[...] = x1
        n3, _ = _rms_fwd(x1)
        h2_ref[...] = (n3 * g3_ref[...]).astype(BF16)

    row = pl.BlockSpec((tm, D_MODEL), lambda i: (i, 0))
    full = pl.BlockSpec((D_MODEL, D_MODEL), lambda i: (0, 0))
    vec = pl.BlockSpec((1, D_MODEL), lambda i: (0, 0))
    act = jax.ShapeDtypeStruct((t, D_MODEL), BF16)
    res = jax.ShapeDtypeStruct((t, D_MODEL), F32)
    return pl.pallas_call(
        body, name="merge_fwd", grid=(t // tm,), out_shape=[act, act, act, res, res, act],
        in_specs=[row, row, pl.BlockSpec((tm, D_MODEL), lambda i: (i, 5)), pl.BlockSpec((tm, D_MODEL), lambda i: (i, 6)),
                  row, full, full, full, vec, vec],
        out_specs=[row] * 6,
        compiler_params=_params("parallel"),
    )(y_a, y_b, proj, proj, x, w_cb, w_lb, w_out, g2, g3)


N_FF_BLOCKS = D_FF // CB
FFN_BWD_COLS = 512


def _ffn_up(h2, w_up, w_conv, b_conv):
    t = h2.shape[0]
    rc = _row_chunk(t)
    nb = N_FF_BLOCKS

    def body(h_ref, w_ref, c_ref, b_ref, up_ref, act_ref, f_ref, pad, gate):
        k = pl.program_id(1)
        pad[pl.ds(0, PAD), :] = jnp.zeros((PAD, CB), F32)
        for r0 in range(0, t, rc):
            rows = pl.ds(r0, rc)
            up = _dot(h_ref[rows, :], w_ref[...]).astype(BF16)
            up_ref[rows, :] = up
            pad[pl.ds(PAD + r0, rc), :] = up.astype(F32)
        cw = c_ref[...]
        for r0 in range(0, t, rc):
            rows = pl.ds(r0, rc)
            act = _conv_causal(pad, cw, r0, rc, 3) + b_ref[...]
            act_ref[rows, :] = act.astype(BF16)

            @pl.when(k == 0)
            def _():
                gate[rows, :] = act

            @pl.when(k == 1)
            def _():
                f_ref[rows, :] = (_gelu(gate[rows, :]) * act).astype(BF16)

    half = lambda rows: pl.BlockSpec((rows, CB), lambda j, k: (0, nb * k + j))
    wide = jax.ShapeDtypeStruct((t, 2 * D_FF), BF16)
    return pl.pallas_call(
        body, name="ffn_up_fwd", grid=(nb, 2), out_shape=[wide, wide, jax.ShapeDtypeStruct((t, D_FF), BF16)],
        in_specs=[pl.BlockSpec((t, D_MODEL), lambda j, k: (0, 0)), half(D_MODEL), half(3), half(1)],
        out_specs=[half(t), half(t), pl.BlockSpec((t, CB), lambda j, k: (0, j))],
        scratch_shapes=[pltpu.VMEM((t + PAD, CB), F32), pltpu.VMEM((t, CB), F32)],
        compiler_params=_params("parallel", "arbitrary"),
    )(h2, w_up, w_conv, b_conv)


def _ffn_down(f, act, w_down, x1, target, g4):
    t = f.shape[0]
    tm = min(256, t)
    cc = 512

    def body(f_ref, act_ref, w_ref, x1_ref, tg_ref, g_ref, dy_ref, dout_ref, back_ref, dg_ref, loss_ref):
        @pl.when(pl.program_id(0) == 0)
        def _():
            dg_ref[...] = jnp.zeros_like(dg_ref)
            loss_ref[...] = jnp.zeros_like(loss_ref)
        out = _dot(f_ref[...], w_ref[...])
        n4, r4 = _rms_fwd(out)
        err = x1_ref[...] + n4 * g_ref[...] - tg_ref[...]
        loss_ref[...] += jnp.full(loss_ref.shape, 0.5 / D_MODEL, F32) * jnp.sum(err * err)
        dy = err * (1.0 / D_MODEL)
        dy_ref[...] = dy
        dg_ref[...] += jnp.sum(dy * n4, axis=0, keepdims=True)
        d_out = _rms_bwd(n4, r4, dy * g_ref[...]).astype(BF16)
        dout_ref[...] = d_out
        for c0 in range(0, D_FF, cc):
            d_f = _dot_nt(d_out, w_ref[pl.ds(c0, cc), :])
            gelu, d_gelu = _gelu_and_grad(act_ref[:, pl.ds(c0, cc)].astype(F32))
            val = act_ref[:, pl.ds(D_FF + c0, cc)].astype(F32)
            back_ref[:, pl.ds(c0, cc)] = (d_f * val * d_gelu).astype(BF16)
            back_ref[:, pl.ds(D_FF + c0, cc)] = (d_f * gelu).astype(BF16)

    row = pl.BlockSpec((tm, D_MODEL), lambda i: (i, 0))
    wide = pl.BlockSpec((tm, 2 * D_FF), lambda i: (i, 0))
    vec = pl.BlockSpec((1, D_MODEL), lambda i: (0, 0))
    return pl.pallas_call(
        body, name="ffn_down_fwd_bwd", grid=(t // tm,),
        out_shape=[jax.ShapeDtypeStruct((t, D_MODEL), F32), jax.ShapeDtypeStruct((t, D_MODEL), BF16),
                   jax.ShapeDtypeStruct((t, 2 * D_FF), BF16), jax.ShapeDtypeStruct((1, D_MODEL), F32),
                   jax.ShapeDtypeStruct((SUBLANES, LANES), F32)],
        in_specs=[pl.BlockSpec((tm, D_FF), lambda i: (i, 0)), wide, pl.BlockSpec((D_FF, D_MODEL), lambda i: (0, 0)),
                  row, row, vec],
        out_specs=[row, row, wide, vec, pl.BlockSpec((SUBLANES, LANES), lambda i: (0, 0))],
        compiler_params=_params("arbitrary"),
    )(f, act, w_down, x1, target, g4)


def _grad_tn(a, b, bm, name):
    t, m = a.shape
    n = b.shape[1]

    def body(a_ref, b_ref, o_ref):
        o_ref[...] = _dot_tn(a_ref[...], b_ref[...]).astype(BF16)

    return pl.pallas_call(
        body, name=name, grid=(m // bm,), out_shape=jax.ShapeDtypeStruct((m, n), BF16),
        in_specs=[pl.BlockSpec((t, bm), lambda i: (0, i)), pl.BlockSpec((t, n), lambda i: (0, 0))],
        out_specs=pl.BlockSpec((bm, n), lambda i: (i, 0)),
        compiler_params=_params("parallel"),
    )(a, b)


def _ffn_up_bwd(up, back, w_conv, h2, w_up):
    t = h2.shape[0]
    rc = _row_chunk(t)
    cb = FFN_BWD_COLS

    def body(up_ref, back_ref, c_ref, h_ref, w_ref, dw_ref, dcw_ref, dcb_ref, dh_ref, pad, after, d_up):
        @pl.when(pl.program_id(0) == 0)
        def _():
            dh_ref[...] = jnp.zeros_like(dh_ref)
        pad[pl.ds(0, PAD), :] = jnp.zeros((PAD, cb), F32)
        after[pl.ds(t, PAD), :] = jnp.zeros((PAD, cb), F32)
        for r0 in range(0, t, rc):
            pad[pl.ds(PAD + r0, rc), :] = up_ref[pl.ds(r0, rc), :].astype(F32)
            after[pl.ds(r0, rc), :] = back_ref[pl.ds(r0, rc), :].astype(F32)
        cw = c_ref[...]
        taps = [jnp.zeros((SUBLANES, cb), F32)] * 3
        bias = jnp.zeros((SUBLANES, cb), F32)
        for r0 in range(0, t, rc):
            for q0 in range(r0, r0 + rc, ROW_SLICE):
                rows = pl.ds(q0, ROW_SLICE)
                d_up[rows, :] = _conv_anticausal(after, cw, q0, ROW_SLICE, 3).astype(BF16)
                g = after[rows, :]
                taps = [acc + _fold_rows(g * _rows_back(pad, q0, ROW_SLICE, 2 - k)) for k, acc in enumerate(taps)]
                bias = bias + _fold_rows(g)
            rows = pl.ds(r0, rc)
            dh_ref[rows, :] += _dot_nt(d_up[rows, :], w_ref[...])
        dw_ref[...] = _dot_tn(h_ref[...], d_up[...]).astype(BF16)
        dcw_ref[...] = jnp.concatenate([jnp.sum(acc, axis=0, keepdims=True) for acc in taps], axis=0)
        dcb_ref[...] = jnp.sum(bias, axis=0, keepdims=True)

    cols = lambda rows: pl.BlockSpec((rows, cb), lambda j: (0, j))
    whole = pl.BlockSpec((t, D_MODEL), lambda j: (0, 0))
    return pl.pallas_call(
        body, name="ffn_up_bwd", grid=(2 * D_FF // cb,),
        out_shape=[jax.ShapeDtypeStruct((D_MODEL, 2 * D_FF), BF16), jax.ShapeDtypeStruct((3, 2 * D_FF), F32),
                   jax.ShapeDtypeStruct((1, 2 * D_FF), F32), jax.ShapeDtypeStruct((t, D_MODEL), F32)],
        in_specs=[cols(t), cols(t), cols(3), whole, cols(D_MODEL)],
        out_specs=[cols(D_MODEL), cols(3), cols(1), whole],
        scratch_shapes=[pltpu.VMEM((t + PAD, cb), F32), pltpu.VMEM((t + PAD, cb), F32), pltpu.VMEM((t, cb), BF16)],
        compiler_params=_params("arbitrary"),
    )(up, back, w_conv, h2, w_up)


def _merge_bwd(dy, d_h2, x1, mix, g3, g2, w_out, w_cb, w_lb, pa, pb, proj):
    t = dy.shape[0]
    tm = min(256, t)

    def body(dy_ref, dh2_ref, x1_ref, mix_ref, g3_ref, g2_ref, wo_ref, wcb_ref, wlb_ref, pa_ref, pb_ref, gc_ref, gl_ref,
             dx1_ref, dmix_ref, dpa_ref, dpb_ref, dya_ref, dyb_ref, dgate_ref, dg3_ref, dg2_ref):
        @pl.when(pl.program_id(0) == 0)
        def _():
            dg3_ref[...] = jnp.zeros_like(dg3_ref)
            dg2_ref[...] = jnp.zeros_like(dg2_ref)
        n3, r3 = _rms_fwd(x1_ref[...])
        d_h2 = dh2_ref[...]
        dg3_ref[...] += jnp.sum(d_h2 * n3, axis=0, keepdims=True)
        dx1 = dy_ref[...] + _rms_bwd(n3, r3, d_h2 * g3_ref[...])
        dx1_ref[...] = dx1
        n2, r2 = _rms_fwd(mix_ref[...])
        dg2_ref[...] += jnp.sum(dx1 * n2, axis=0, keepdims=True)
        d_mix = _rms_bwd(n2, r2, dx1 * g2_ref[...]).astype(BF16)
        dmix_ref[...] = d_mix
        d_merged = _dot_nt(d_mix, wo_ref[...])
        sc = jax.nn.sigmoid(gc_ref[...].astype(F32))
        sl = jax.nn.sigmoid(gl_ref[...].astype(F32))
        d_pa = (d_merged * sc).astype(BF16)
        d_pb = (d_merged * sl).astype(BF16)
        dpa_ref[...] = d_pa
        dpb_ref[...] = d_pb
        dgate_ref[0] = (d_merged * pa_ref[...].astype(F32) * sc * (1.0 - sc)).astype(BF16)
        dgate_ref[1] = (d_merged * pb_ref[...].astype(F32) * sl * (1.0 - sl)).astype(BF16)
        dya_ref[...] = _dot_nt(d_pa, wcb_ref[...]).astype(BF16)
        dyb_ref[...] = _dot_nt(d_pb, wlb_ref[...]).astype(BF16)

    row = pl.BlockSpec((tm, D_MODEL), lambda i: (i, 0))
    full = pl.BlockSpec((D_MODEL, D_MODEL), lambda i: (0, 0))
    vec = pl.BlockSpec((1, D_MODEL), lambda i: (0, 0))
    act = jax.ShapeDtypeStruct((t, D_MODEL), BF16)
    small = jax.ShapeDtypeStruct((1, D_MODEL), F32)
    return pl.pallas_call(
        body, name="merge_bwd", grid=(t // tm,),
        out_shape=[jax.ShapeDtypeStruct((t, D_MODEL), F32), act, act, act, act, act,
                   jax.ShapeDtypeStruct((2, t, D_MODEL), BF16), small, small],
        in_specs=[row, row, row, row, vec, vec, full, full, full, row, row,
                  pl.BlockSpec((tm, D_MODEL), lambda i: (i, 5)), pl.BlockSpec((tm, D_MODEL), lambda i: (i, 6))],
        out_specs=[row] * 6 + [pl.BlockSpec((2, tm, D_MODEL), lambda i: (0, i, 0)), vec, vec],
        compiler_params=_params("arbitrary"),
    )(dy, d_h2, x1, mix, g3, g2, w_out, w_cb, w_lb, pa, pb, proj, proj)


def _conv_mixer_bwd(proj, d_ya, w_short):
    t = proj.shape[0]
    rc = _row_chunk(t)

    def body(b_ref, c_ref, x_ref, dy_ref, w_ref, d_ref, dw_ref, pad, back):
        pad[pl.ds(0, PAD), :] = jnp.zeros((PAD, CB), F32)
        back[pl.ds(t, PAD), :] = jnp.zeros((PAD, CB), F32)
        for r0 in range(0, t, rc):
            rows = pl.ds(r0, rc)
            pad[pl.ds(PAD + r0, rc), :] = c_ref[rows, :].astype(F32) * x_ref[rows, :].astype(F32)
        w = w_ref[...]
        for r0 in range(0, t, rc):
            rows = pl.ds(r0, rc)
            d_y = dy_ref[rows, :].astype(F32)
            d_ref[0, rows, :] = (d_y * _conv_causal(pad, w, r0, rc, 3)).astype(BF16)
            back[rows, :] = d_y * b_ref[rows, :].astype(F32)
        taps = [jnp.zeros((1, CB), F32)] * 3
        for r0 in range(0, t, rc):
            rows = pl.ds(r0, rc)
            d_u = _conv_anticausal(back, w, r0, rc, 3)
            d_ref[1, rows, :] = (d_u * x_ref[rows, :].astype(F32)).astype(BF16)
            d_ref[2, rows, :] = (d_u * c_ref[rows, :].astype(F32)).astype(BF16)
            taps = [acc + new for acc, new in zip(taps, _conv_wgrad(back[rows, :], pad, r0, rc, 3))]
        dw_ref[...] = jnp.concatenate(taps, axis=0)

    blk = pl.BlockSpec((t, CB), lambda h: (0, h))
    return pl.pallas_call(
        body, name="conv_mixer_bwd", grid=(D_MODEL // CB,),
        out_shape=[jax.ShapeDtypeStruct((3, t, D_MODEL), BF16), jax.ShapeDtypeStruct((3, D_MODEL), F32)],
        in_specs=[_section(0, t), _section(1, t), _section(2, t), blk, pl.BlockSpec((3, CB), lambda h: (0, h))],
        out_specs=[pl.BlockSpec((3, t, CB), lambda h: (0, 0, h)), pl.BlockSpec((3, CB), lambda h: (0, h))],
        scratch_shapes=[pltpu.VMEM((t + PAD, CB), F32), pltpu.VMEM((t + PAD, CB), F32)],
        compiler_params=_params("parallel"),
    )(proj, proj, proj, d_ya, w_short)


LRU_SMALL_ROWS = 8


def _lru_bwd(proj, hl, a_all, kept, d_yb, w_conv, wa, wx, lam):
    t = proj.shape[0]
    rc = _row_chunk(t)
    vec, mat = _head_specs()

    def body(lx_ref, ly_ref, hl_ref, a_ref, kept_ref, dy_ref, wc_ref, wa_ref, wx_ref, lam_ref,
             d_ref, dwa_ref, dwx_ref, small_ref, pad, a_next, dh_s, dh_o, h_prev, back, acc_a, acc_x, dz_a, dz_x):
        zeros = jnp.zeros((PAD, CB), F32)
        pad[pl.ds(0, PAD), :] = zeros
        h_prev[pl.ds(0, PAD), :] = zeros
        a_next[pl.ds(t, PAD), :] = zeros
        back[pl.ds(t, PAD), :] = zeros
        for r0 in range(0, t, ROW_SLICE):
            rows = pl.ds(r0, ROW_SLICE)
            pad[pl.ds(PAD + r0, ROW_SLICE), :] = lx_ref[rows, :].astype(F32)
            h_prev[pl.ds(PAD + r0, ROW_SLICE), :] = hl_ref[rows, :]
            a_next[pl.ds(PAD - 1 + r0, ROW_SLICE), :] = a_ref[rows, :]
            act, d_act = _gelu_and_grad(ly_ref[rows, :].astype(F32))
            d_y = dy_ref[rows, :].astype(F32)
            dh_s[rows, :] = d_y * act
            d_ref[1, rows, :] = (d_y * hl_ref[rows, :] * d_act).astype(BF16)
        wc = wc_ref[...]
        wa_m, wx_m = wa_ref[...].reshape(HEAD_DIM, HEAD_DIM), wx_ref[...].reshape(HEAD_DIM, HEAD_DIM)
        ls = _log_sigmoid(lam_ref[...])

        row = lax.broadcasted_iota(jnp.int32, (SUBLANES, CB), 0)
        groups = t // SUBLANES

        def group(i, carry):
            r = pl.multiple_of((groups - 1 - i) * SUBLANES, SUBLANES)
            a_g, b_g = a_next[pl.ds(PAD + r, SUBLANES), :], dh_s[pl.ds(r, SUBLANES), :]
            for s in (1, 2, 4):
                keep = row < SUBLANES - s
                b_g = jnp.where(keep, a_g * pltpu.roll(b_g, SUBLANES - s, 0) + b_g, b_g)
                a_g = jnp.where(keep, a_g * pltpu.roll(a_g, SUBLANES - s, 0), a_g)
            d_g = b_g + a_g * carry
            dh_o[pl.ds(r, SUBLANES), :] = d_g
            return jnp.broadcast_to(d_g[0:1, :], (SUBLANES, CB))

        def trip(i, carry):
            for j in range(SCAN_UNROLL):
                carry = group(i * SCAN_UNROLL + j, carry)
            return carry

        lax.fori_loop(0, groups // SCAN_UNROLL, trip, jnp.zeros((SUBLANES, CB), F32))

        acc_a[...] = jnp.zeros_like(acc_a)
        acc_x[...] = jnp.zeros_like(acc_x)
        d_ba = d_bx = d_ls = jnp.zeros((SUBLANES, CB), F32)
        for r0 in range(0, t, rc):
            for q0 in range(r0, r0 + rc, ROW_SLICE):
                rows, local = pl.ds(q0, ROW_SLICE), pl.ds(q0 - r0, ROW_SLICE)
                a = a_ref[rows, :]
                xl, ra, ia = (kept_ref[i, rows, :].astype(F32) for i in range(3))
                a_sq = a * a
                mult = jnp.sqrt(1.0 - a_sq)
                slope = -a_sq / mult
                if q0 == 0:
                    first = lax.broadcasted_iota(jnp.int32, (ROW_SLICE, CB), 0) == 0
                    mult, slope = jnp.where(first, 1.0, mult), jnp.where(first, 0.0, slope)
                d_h = dh_o[rows, :]
                d_la = d_h * _rows_back(h_prev, q0, ROW_SLICE, 1) * a + d_h * ia * xl * slope
                d_za = d_la * (LRU_C * ls) * ra * (1.0 - ra)
                d_zx = d_h * mult * xl * ia * (1.0 - ia)
                d_ls = d_ls + _fold_rows(d_la * ra)
                d_ba = d_ba + _fold_rows(d_za)
                d_bx = d_bx + _fold_rows(d_zx)
                dz_a[local, :] = d_za.astype(BF16)
                dz_x[local, :] = d_zx.astype(BF16)
                back[rows, :] = d_h * mult * ia
            rows = pl.ds(r0, rc)
            xb = kept_ref[0, rows, :]
            acc_a[...] += _dot_tn(xb, dz_a[...])
            acc_x[...] += _dot_tn(xb, dz_x[...])
            back[rows, :] += _dot_nt(dz_a[...], wa_m) + _dot_nt(dz_x[...], wx_m)
        taps = [jnp.zeros((SUBLANES, CB), F32)] * 4
        d_bc = jnp.zeros((SUBLANES, CB), F32)
        for q0 in range(0, t, ROW_SLICE):
            rows = pl.ds(q0, ROW_SLICE)
            d_ref[0, rows, :] = _conv_anticausal(back, wc, q0, ROW_SLICE, 4).astype(BF16)
            g = back[rows, :]
            taps = [acc + _fold_rows(g * _rows_back(pad, q0, ROW_SLICE, 3 - k)) for k, acc in enumerate(taps)]
            d_bc = d_bc + _fold_rows(g)
        d_lam = d_ls * LRU_C * jax.nn.sigmoid(-lam_ref[...])
        small_ref[...] = jnp.concatenate(
            [jnp.sum(v, axis=0, keepdims=True) for v in taps + [d_bc, d_ba, d_bx, d_lam]], axis=0)
        dwa_ref[...] = acc_a[...].reshape(N_DEV, HEAD_DIM // N_DEV, HEAD_DIM).astype(BF16)
        dwx_ref[...] = acc_x[...].reshape(N_DEV, HEAD_DIM // N_DEV, HEAD_DIM).astype(BF16)

    blk = pl.BlockSpec((t, CB), lambda h: (0, h))
    gate_grad = jax.ShapeDtypeStruct((N_DEV, N_HEADS, HEAD_DIM // N_DEV, HEAD_DIM), BF16)
    return pl.pallas_call(
        body, name="lru_bwd", grid=(N_HEADS,),
        out_shape=[jax.ShapeDtypeStruct((2, t, D_MODEL), BF16), gate_grad, gate_grad,
                   jax.ShapeDtypeStruct((LRU_SMALL_ROWS, D_MODEL), F32)],
        in_specs=[_section(3, t), _section(4, t), blk, blk, pl.BlockSpec((3, t, CB), lambda h: (0, 0, h)), blk,
                  pl.BlockSpec((4, CB), lambda h: (0, h)), mat, mat, vec],
        out_specs=[pl.BlockSpec((2, t, CB), lambda h: (0, 0, h)), mat, mat,
                   pl.BlockSpec((LRU_SMALL_ROWS, CB), lambda h: (0, h))],
        scratch_shapes=[pltpu.VMEM((t + PAD, CB), F32), pltpu.VMEM((t + PAD, CB), F32), pltpu.VMEM((t, CB), F32),
                        pltpu.VMEM((t, CB), F32), pltpu.VMEM((t + PAD, CB), F32), pltpu.VMEM((t + PAD, CB), F32),
                        pltpu.VMEM((HEAD_DIM, HEAD_DIM), F32), pltpu.VMEM((HEAD_DIM, HEAD_DIM), F32),
                        pltpu.VMEM((rc, CB), BF16), pltpu.VMEM((rc, CB), BF16)],
        compiler_params=_params("parallel"),
    )(proj, proj, hl, a_all, kept, d_yb, w_conv, wa, wx, lam)


def _stack_maps(halves):
    def conv(sec, part):
        return jnp.minimum(sec, 2), jnp.where(sec < 3, part, halves - 1)

    def lru(sec, part):
        return jnp.clip(sec - 3, 0, 1), jnp.where(sec < 3, 0, jnp.where(sec < 5, part, halves - 1))

    def gate(sec, part):
        return jnp.clip(sec - 5, 0, 1), jnp.where(sec < 5, 0, part)

    return conv, lru, gate


def _pick_stack(sec, refs, fn):
    @pl.when(sec < 3)
    def _():
        fn(refs[0])

    @pl.when((sec >= 3) & (sec < 5))
    def _():
        fn(refs[1])

    @pl.when(sec >= 5)
    def _():
        fn(refs[2])


def _in_proj_wgrad(h, d_conv, d_lru, d_gate):
    t = h.shape[0]
    halves, bn = 1, D_MODEL
    maps = _stack_maps(halves)

    def body(h_ref, dc_ref, dl_ref, dg_ref, o_ref):
        def emit(ref):
            o_ref[...] = _dot_tn(h_ref[...], ref[...]).astype(BF16)
        _pick_stack(pl.program_id(0) // halves, (dc_ref, dl_ref, dg_ref), emit)

    def spec(m):
        def index(s):
            stack, part = m(s // halves, s % halves)
            return stack, 0, part
        return pl.BlockSpec((None, t, bn), index)

    return pl.pallas_call(
        body, name="in_proj_wgrad", grid=(7 * halves,), out_shape=jax.ShapeDtypeStruct((D_MODEL, IN_COLS), BF16),
        in_specs=[pl.BlockSpec((t, D_MODEL), lambda s: (0, 0))] + [spec(m) for m in maps],
        out_specs=pl.BlockSpec((D_MODEL, bn), lambda s: (0, s)),
        compiler_params=_params("arbitrary"),
    )(h, d_conv, d_lru, d_gate)


def _in_proj_xgrad(d_conv, d_lru, d_gate, w_in, x, dx1, g1):
    t = x.shape[0]
    tm = min(1024, t)
    maps = _stack_maps(1)

    def body(dc_ref, dl_ref, dg_ref, w_ref, x_ref, dx1_ref, g_ref, dx_ref, dgain_ref, acc):
        i, s = pl.program_id(0), pl.program_id(1)

        @pl.when((i == 0) & (s == 0))
        def _():
            dgain_ref[...] = jnp.zeros_like(dgain_ref)

        @pl.when(s == 0)
        def _():
            acc[...] = jnp.zeros_like(acc)

        def add(ref):
            acc[...] += _dot_nt(ref[...], w_ref[...])
        _pick_stack(s, (dc_ref, dl_ref, dg_ref), add)

        @pl.when(s == 6)
        def _():
            n1, r1 = _rms_fwd(x_ref[...])
            d_h = acc[...]
            dgain_ref[...] += jnp.sum(d_h * n1, axis=0, keepdims=True)
            dx_ref[...] = dx1_ref[...] + _rms_bwd(n1, r1, d_h * g_ref[...])

    def spec(m):
        def index(i, s):
            return m(s, 0)[0], i, 0
        return pl.BlockSpec((None, tm, D_MODEL), index)

    row = pl.BlockSpec((tm, D_MODEL), lambda i, s: (i, 0))
    vec = pl.BlockSpec((1, D_MODEL), lambda i, s: (0, 0))
    return pl.pallas_call(
        body, name="in_proj_xgrad", grid=(t // tm, 7),
        out_shape=[jax.ShapeDtypeStruct((t, D_MODEL), F32), jax.ShapeDtypeStruct((1, D_MODEL), F32)],
        in_specs=[spec(m) for m in maps] + [pl.BlockSpec((D_MODEL, D_MODEL), lambda i, s: (0, s)), row, row, vec],
        out_specs=[row, vec],
        scratch_shapes=[pltpu.VMEM((tm, D_MODEL), F32)],
        compiler_params=_params("arbitrary", "arbitrary"),
    )(d_conv, d_lru, d_gate, w_in, x, dx1, g1)


def _adamw(w, g, m, v):
    m = ADAM_B1 * m + (1.0 - ADAM_B1) * g
    v = ADAM_B2 * v + (1.0 - ADAM_B2) * (g * g)
    m_hat = m / (1.0 - ADAM_B1 ** ADAM_STEP)
    v_hat = v / (1.0 - ADAM_B2 ** ADAM_STEP)
    return -ADAM_LR * (m_hat / (jnp.sqrt(v_hat) + ADAM_EPS) + ADAM_WD * w), m, v


def _adam_large(ws, ms, vs, owns, others, name):
    n = len(ws)
    shape = ws[0].shape
    cols = shape[-1]
    flat = [[a.reshape(-1, cols) for a in group] for group in (ws, ms, vs)]
    rows = flat[0][0].shape[0]
    owns, others = [o.reshape(4, rows, cols) for o in owns], [o.reshape(3, rows, cols) for o in others]
    rb = _row_block(rows, 512)

    def body(*refs):
        ins, outs = refs[:5 * n], refs[5 * n:]
        for i in range(n):
            w_ref, m_ref, v_ref, own_ref, oth_ref = ins[i::n]
            g = own_ref[...].astype(F32)
            for k in range(3):
                g = g + oth_ref[k].astype(F32)
            outs[i][...] = g
            outs[n + i][...], outs[2 * n + i][...], outs[3 * n + i][...] = _adamw(w_ref[...], g, m_ref[...], v_ref[...])

    blk = pl.BlockSpec((rb, cols), lambda i: (i, 0))
    res = jax.ShapeDtypeStruct((rows, cols), F32)
    outs = pl.pallas_call(
        body, name=name, grid=(rows // rb,), out_shape=[res] * (4 * n),
        in_specs=[blk] * (3 * n) + [pl.BlockSpec((None, rb, cols), lambda i: (0, i, 0))] * n
        + [pl.BlockSpec((3, rb, cols), lambda i: (0, i, 0))] * n,
        out_specs=[blk] * (4 * n), compiler_params=_params("parallel"),
    )(*flat[0], *flat[1], *flat[2], *owns, *others)
    outs = [o.reshape(shape) for o in outs]
    return outs[:n], outs[n:2 * n], outs[2 * n:3 * n], outs[3 * n:]


def _adam_small(ws, gs, ms, vs):
    n = len(ws)

    def body(*refs):
        w_refs, g_refs, m_refs, v_refs = (refs[i * n:(i + 1) * n] for i in range(4))
        outs = refs[4 * n:]
        for i in range(n):
            d, m, v = _adamw(w_refs[i][...], g_refs[i][...], m_refs[i][...], v_refs[i][...])
            outs[i][...], outs[n + i][...], outs[2 * n + i][...] = d, m, v

    shapes = [jax.ShapeDtypeStruct(w.shape, F32) for w in ws]
    outs = pl.pallas_call(
        body, name="adam_small", out_shape=shapes * 3,
        in_specs=[VMEM_SPEC] * (4 * n), out_specs=[VMEM_SPEC] * (3 * n), compiler_params=_params(),
    )(*ws, *gs, *ms, *vs)
    return outs[:n], outs[n:2 * n], outs[2 * n:]


def _pack_rows(pieces):
    tile = SUBLANES * LANES
    return jnp.concatenate([jnp.pad(p.reshape(-1), (0, (-p.size) % tile)).reshape(-1, LANES) for p in pieces], axis=0)


def _packed_starts(sizes):
    tile = SUBLANES * LANES
    starts = [0]
    for s in sizes:
        starts.append(starts[-1] + (s + tile - 1) // tile * SUBLANES)
    return starts


def kernel(x, norm_mix_pre, norm_mix_post, norm_ffn_pre, norm_ffn_post, w_in, conv_short_w, w_conv_branch, lru_conv_w, lru_conv_b, lru_wa, lru_ba, lru_wx, lru_bx, lru_lambda, w_lru_branch, w_out, ffn_w_up, ffn_conv_w, ffn_conv_b, ffn_w_down, loss_target, m_norm_mix_pre, m_norm_mix_post, m_norm_ffn_pre, m_norm_ffn_post, m_w_in, m_conv_short_w, m_w_conv_branch, m_lru_conv_w, m_lru_conv_b, m_lru_wa, m_lru_ba, m_lru_wx, m_lru_bx, m_lru_lambda, m_w_lru_branch, m_w_out, m_ffn_w_up, m_ffn_conv_w, m_ffn_conv_b, m_ffn_w_down, v_norm_mix_pre, v_norm_mix_post, v_norm_ffn_pre, v_norm_ffn_post, v_w_in, v_conv_short_w, v_w_conv_branch, v_lru_conv_w, v_lru_conv_b, v_lru_wa, v_lru_ba, v_lru_wx, v_lru_bx, v_lru_lambda, v_w_lru_branch, v_w_out, v_ffn_w_up, v_ffn_conv_w, v_ffn_conv_b, v_ffn_w_down):
    t = x.shape[1]
    xi, yi, ci = _position()
    me = _block_of(xi, yi, ci)
    x2, target = x[0], loss_target[0]
    shard_in, shard_up = IN_COLS // N_DEV, 2 * D_FF // N_DEV
    shard_sq, shard_down, shard_head = D_MODEL // N_DEV, D_FF // N_DEV, HEAD_DIM // N_DEV

    names = ["w_in", "lru_wa", "lru_wx", "w_conv_branch", "w_lru_branch", "w_out", "ffn_w_up", "ffn_w_down"]
    large = [w_in[0], lru_wa[0], lru_wx[0], w_conv_branch[0], w_lru_branch[0], w_out[0], ffn_w_up[0], ffn_w_down[0]]
    blocks = [_cols(shard_in), _lead, _lead, _rows(shard_sq), _rows(shard_sq), _rows(shard_sq),
              _cols(shard_up), _rows(shard_down)]
    gate_full = (N_DEV, N_HEADS, shard_head, HEAD_DIM)
    full_shapes = [(D_MODEL, IN_COLS), gate_full, gate_full, (D_MODEL, D_MODEL), (D_MODEL, D_MODEL), (D_MODEL, D_MODEL),
                   (D_MODEL, 2 * D_FF), (D_FF, D_MODEL)]
    n_now = 3
    small_sharded = [conv_short_w, lru_conv_w, lru_ba, lru_bx, ffn_conv_w]
    small_mine = _pack_rows(small_sharded)
    small_at = _packed_starts([p.size for p in small_sharded])
    *gathered, small_all, proj, h = _gather_weights(large, blocks, full_shapes, small_mine, n_now, x2, norm_mix_pre)
    g_in, g_wa, g_wx = gathered[:n_now]
    later_blocks = blocks[n_now:]
    send1, recv1, later, gather_token = _gather_start(gathered[n_now:], later_blocks, "gather_start")

    def behind(token, operand):
        return operand + token[0:1, 0:1]

    def forward(lo, hi, after, tag):
        return _gather_forward(later[lo:hi], later_blocks[lo:hi], send1[4 * lo:4 * hi], recv1[4 * lo:4 * hi], after,
                               "gather_forward_" + tag)

    def finish(lo, hi, flight, after, tag):
        return _gather_finish(flight[2], later_blocks[lo:hi], flight[0], flight[1], after, "gather_finish_" + tag)

    def cols_of(r0, n, width):
        part = small_all[:, r0:r0 + n * width // LANES, :].reshape(N_DEV, n, width)
        return part.transpose(1, 0, 2).reshape(n, N_DEV * width)

    c_short = cols_of(small_at[0], 3, LANES)
    c_lru = cols_of(small_at[1], 4, LANES)
    b_a = cols_of(small_at[2], N_HEADS, shard_head).reshape(1, D_MODEL)
    b_x = cols_of(small_at[3], N_HEADS, shard_head).reshape(1, D_MODEL)
    c_ffn = cols_of(small_at[4], 3, shard_up)

    y_a = _conv_mixer_fwd(proj, behind(gather_token, c_short))
    y_b, hl, decay, lru_kept = _lru_fwd(proj, behind(gather_token, c_lru), lru_conv_b, g_wa, b_a, g_wx, b_x, lru_lambda)
    flight_mix_w = forward(0, 3, y_b, "mix")
    g_cb, g_lb, g_out = finish(0, 3, flight_mix_w, y_b, "mix")
    pa, pb, merged, mix, x1, h2 = _merge(y_a, y_b, proj, x2, g_cb, g_lb, g_out, norm_mix_post, norm_ffn_pre)
    flight_up_w = forward(3, 4, h2, "up")
    (g_up,) = finish(3, 4, flight_up_w, h2, "up")
    up, act, f = _ffn_up(h2, g_up, c_ffn, ffn_conv_b)
    flight_down_w = forward(4, 5, f, "down")
    (g_down,) = finish(4, 5, flight_down_w, f, "down")
    dy, d_out, d_act, dg4, loss_part = _ffn_down(f, act, g_down, x1, target, norm_ffn_post)

    block_of = dict(zip(names, blocks))
    shard_shapes = {"w_in": (D_MODEL, shard_in), "w_conv_branch": (shard_sq, D_MODEL), "w_lru_branch": (shard_sq, D_MODEL),
                    "w_out": (shard_sq, D_MODEL), "lru_wa": (N_HEADS, shard_head, HEAD_DIM),
                    "lru_wx": (N_HEADS, shard_head, HEAD_DIM), "ffn_w_up": (D_MODEL, shard_up),
                    "ffn_w_down": (shard_down, D_MODEL)}

    def reduce_start(tag, grads):
        keys = list(grads)
        sums = _reduce_pair([grads[k] for k in keys], [block_of[k] for k in keys], [shard_shapes[k] for k in keys],
                            "reduce_pair_" + tag)
        return (keys,) + _exchange_chips_start(sums, "reduce_chip_start_" + tag)

    gw_down = _grad_tn(f, d_out, min(512, D_FF), "ffn_down_wgrad")
    flight_down = reduce_start("down", {"ffn_w_down": gw_down})
    gw_up, gc_ffn, gb_ffn, d_h2 = _ffn_up_bwd(up, d_act, behind(flight_down[-1], c_ffn), h2, g_up)
    flight_up = reduce_start("up", {"ffn_w_up": gw_up})
    dx1, d_mix, d_pa, d_pb, d_ya, d_yb, d_gate, dg3, dg2 = _merge_bwd(
        dy, d_h2, x1, mix, behind(flight_up[-1], norm_ffn_pre), norm_mix_post, g_out, g_cb, g_lb, pa, pb, proj)
    gw_out = _grad_tn(merged, d_mix, CB, "w_out_wgrad")
    gw_cb = _grad_tn(y_a, d_pa, CB, "w_conv_branch_wgrad")
    gw_lb = _grad_tn(y_b, d_pb, CB, "w_lru_branch_wgrad")
    flight_mix = reduce_start("mix", {"w_conv_branch": gw_cb, "w_lru_branch": gw_lb, "w_out": gw_out})
    d_conv, gc_short = _conv_mixer_bwd(proj, d_ya, behind(flight_mix[-1], c_short))
    d_lru, gw_a, gw_x, g_lru_small = _lru_bwd(proj, hl, decay, lru_kept, d_yb, c_lru, g_wa, g_wx, lru_lambda)
    early = [dg2, dg3, dg4, g_lru_small[4:5], g_lru_small[7:8], gb_ffn, gc_short, g_lru_small[0:4],
             g_lru_small[5:6], g_lru_small[6:7], gc_ffn, loss_part]
    flight_small = _small_start(_pack_rows(early), "small_start")
    gw_in = _in_proj_wgrad(h, d_conv, d_lru, d_gate)
    flight_in = reduce_start("in", {"lru_wa": gw_a, "lru_wx": gw_x, "w_in": gw_in})
    dx, dg1 = _in_proj_xgrad(d_conv, d_lru, d_gate, g_in, x2, dx1,
                             behind(flight_small[-1], behind(flight_in[-1], norm_mix_pre)))
    flight_late = _small_start(_pack_rows([dg1]), "small_start_late")

    moments ={"w_in": (m_w_in, v_w_in), "w_conv_branch": (m_w_conv_branch, v_w_conv_branch),
               "w_lru_branch": (m_w_lru_branch, v_w_lru_branch), "w_out": (m_w_out, v_w_out),
               "lru_wa": (m_lru_wa, v_lru_wa), "lru_wx": (m_lru_wx, v_lru_wx), "ffn_w_up": (m_ffn_w_up, v_ffn_w_up),
               "ffn_w_down": (m_ffn_w_down, v_ffn_w_down)}
    weights = {"w_in": w_in, "w_conv_branch": w_conv_branch, "w_lru_branch": w_lru_branch, "w_out": w_out,
               "lru_wa": lru_wa, "lru_wx": lru_wx, "ffn_w_up": ffn_w_up, "ffn_w_down": ffn_w_down}
    out_g, out_d, out_m, out_v = {}, {}, {}, {}

    after = flight_late[-1]
    for tag, (keys, send, recv, sums, lands, _) in (("down", flight_down), ("up", flight_up), ("mix", flight_mix),
                                                    ("in", flight_in)):
        sums, others = _exchange_chips_wait(send, recv, sums, lands, after, "reduce_chip_wait_" + tag)
        by_key = dict(zip(keys, zip(sums, others)))
        for shape in dict.fromkeys(shard_shapes[k] for k in keys):
            same = [k for k in keys if shard_shapes[k] == shape]
            results = _adam_large([weights[k] for k in same], [moments[k][0] for k in same], [moments[k][1] for k in same],
                                  [by_key[k][0] for k in same], [by_key[k][1] for k in same], "adam_" + same[0])
            for out, values in zip((out_g, out_d, out_m, out_v), results):
                out.update(zip(same, values))
        after = out_d[keys[-1]]

    total, total_late = _small_sum([_small_wait(*flight_small[:4], after, "small_wait"),
                                    _small_wait(*flight_late[:4], after, "small_wait_late")], me)
    sizes = [p.size for p in early]
    starts = _packed_starts(sizes)

    def piece(i, shape):
        if i == 0:
            return total_late.reshape(-1)[:D_MODEL].reshape(shape)
        return total[starts[i - 1]:starts[i]].reshape(-1)[:sizes[i - 1]].reshape(shape)

    loss = total[starts[11], 0]

    def col_shard(full, width):
        return lax.dynamic_slice_in_dim(full, me * width, width, axis=1)

    def head_shard(full):
        return lax.dynamic_slice_in_dim(full.reshape(N_HEADS, HEAD_DIM), me * shard_head, shard_head, axis=1)

    small_names = ["norm_mix_pre", "norm_mix_post", "norm_ffn_pre", "norm_ffn_post", "lru_conv_b", "lru_lambda",
                   "ffn_conv_b", "conv_short_w", "lru_conv_w", "lru_ba", "lru_bx", "ffn_conv_w"]
    small_g = [piece(0, (1, D_MODEL)), piece(1, (1, D_MODEL)), piece(2, (1, D_MODEL)), piece(3, (1, D_MODEL)),
               piece(4, (1, D_MODEL)), piece(5, (1, D_MODEL)), piece(6, (1, 2 * D_FF)),
               col_shard(piece(7, (3, D_MODEL)), LANES), col_shard(piece(8, (4, D_MODEL)), LANES),
               head_shard(piece(9, (1, D_MODEL))), head_shard(piece(10, (1, D_MODEL))),
               col_shard(piece(11, (3, 2 * D_FF)), shard_up)]
    small_w = [norm_mix_pre, norm_mix_post, norm_ffn_pre, norm_ffn_post, lru_conv_b, lru_lambda, ffn_conv_b,
               conv_short_w[0], lru_conv_w[0], lru_ba[0], lru_bx[0], ffn_conv_w[0]]
    small_m = [m_norm_mix_pre, m_norm_mix_post, m_norm_ffn_pre, m_norm_ffn_post, m_lru_conv_b, m_lru_lambda,
               m_ffn_conv_b, m_conv_short_w[0], m_lru_conv_w[0], m_lru_ba[0], m_lru_bx[0], m_ffn_conv_w[0]]
    small_v = [v_norm_mix_pre, v_norm_mix_post, v_norm_ffn_pre, v_norm_ffn_post, v_lru_conv_b, v_lru_lambda,
               v_ffn_conv_b, v_conv_short_w[0], v_lru_conv_w[0], v_lru_ba[0], v_lru_bx[0], v_ffn_conv_w[0]]
    s_d, s_m, s_v = _adam_small(small_w, small_g, small_m, small_v)
    for i, name in enumerate(small_names):
        shape = small_w[i].shape if i < 7 else (1,) + small_w[i].shape
        out_g[name] = small_g[i].reshape(shape)
        out_d[name], out_m[name], out_v[name] = s_d[i].reshape(shape), s_m[i].reshape(shape), s_v[i].reshape(shape)

    order = ["norm_mix_pre", "norm_mix_post", "norm_ffn_pre", "norm_ffn_post", "w_in", "conv_short_w", "w_conv_branch",
             "lru_conv_w", "lru_conv_b", "lru_wa", "lru_ba", "lru_wx", "lru_bx", "lru_lambda", "w_lru_branch", "w_out",
             "ffn_w_up", "ffn_conv_w", "ffn_conv_b", "ffn_w_down"]
    return (loss, dx.reshape(1, t, D_MODEL), *[out_g[k] for k in order], *[out_d[k] for k in order],
            *[out_m[k] for k in order], *[out_v[k] for k in order])
```

```python
import functools
import math

import jax
import jax.numpy as jnp
from jax import lax
from jax.experimental import pallas as pl
from jax.experimental.pallas import tpu as pltpu

F32 = jnp.float32
BF16 = jnp.bfloat16
MESH = pl.DeviceIdType.MESH

N_DEV = 8
D_MODEL = 1024
N_HEADS = 4
HEAD_DIM = D_MODEL // N_HEADS
D_FF = 3 * D_MODEL
IN_COLS = 7 * D_MODEL
LRU_C = 8.0
RMS_EPS = 1e-6
ADAM_LR = 0.001
ADAM_B1 = 0.9
ADAM_B2 = 0.999
ADAM_EPS = 1e-08
ADAM_WD = 0.01
ADAM_STEP = 10
GELU_K = math.sqrt(2.0 / math.pi)
GELU_C = 0.044715

LANES = 128
SUBLANES = 8
PAD = SUBLANES
VMEM_LIMIT = 56 * 1024 * 1024
CB = 256
ROW_SLICE = 32
SCAN_UNROLL = 4

HBM_SPEC = pl.BlockSpec(memory_space=pltpu.HBM)
SEM_SPEC = pl.BlockSpec(memory_space=pltpu.SEMAPHORE)
DATAFLOW_EFFECT = pltpu.SideEffectType.DATAFLOW_SIDE_EFFECTING
VMEM_SPEC = pl.BlockSpec(memory_space=pltpu.VMEM)


def _params(*sem):
    if sem:
        return pltpu.CompilerParams(dimension_semantics=sem, vmem_limit_bytes=VMEM_LIMIT)
    return pltpu.CompilerParams(vmem_limit_bytes=VMEM_LIMIT)


def _row_chunk(t):
    return min(256, t)


def _row_block(rows, cap):
    return next(rb for rb in range(min(cap, rows), 0, -16) if rows % rb == 0)


def _gelu(x):
    return 0.5 * x * (1.0 + jnp.tanh(GELU_K * (x + GELU_C * x * x * x)))


def _gelu_and_grad(x):
    t = jnp.tanh(GELU_K * (x + GELU_C * x * x * x))
    g = 0.5 * x * (1.0 + t)
    dg = 0.5 * (1.0 + t) + 0.5 * x * (1.0 - t * t) * GELU_K * (1.0 + 3.0 * GELU_C * x * x)
    return g, dg


def _expm1_neg(x):
    series = x * (1.0 + x * (0.5 + x * (1.0 / 6.0 + x * (1.0 / 24.0 + x * (1.0 / 120.0)))))
    return jnp.where(x > -0.05, series, jnp.exp(x) - 1.0)


def _log_sigmoid(x):
    return jnp.minimum(x, 0.0) - jnp.log1p(jnp.exp(-jnp.abs(x)))


def _dot(a, b):
    return jnp.dot(a, b, preferred_element_type=F32)


def _dot_nt(a, b):
    return lax.dot_general(a, b, (((1,), (1,)), ((), ())), preferred_element_type=F32)


def _dot_tn(a, b):
    return lax.dot_general(a, b, (((0,), (0,)), ((), ())), preferred_element_type=F32)


def _rms_fwd(x):
    r = lax.rsqrt(jnp.mean(x * x, axis=-1, keepdims=True) + RMS_EPS)
    return x * r, r


def _rms_bwd(n, r, gdy):
    return r * (gdy - n * jnp.mean(n * gdy, axis=-1, keepdims=True))


def _rows_back(pad_ref, r0, rows, j):
    cur = pad_ref[pl.ds(PAD + r0, rows), :]
    if j == 0:
        return cur
    before = pad_ref[pl.ds(PAD + r0 - SUBLANES, SUBLANES), :]
    row = lax.broadcasted_iota(jnp.int32, before.shape, 0)
    rolled = pltpu.roll(cur, j, 0)
    top = jnp.where(row < j, pltpu.roll(before, j, 0), rolled[0:SUBLANES, :])
    return jnp.concatenate([top, rolled[SUBLANES:, :]], axis=0)


def _rows_ahead(pad_ref, r0, rows, j):
    cur = pad_ref[pl.ds(r0, rows), :]
    if j == 0:
        return cur
    after = pad_ref[pl.ds(r0 + rows, SUBLANES), :]
    row = lax.broadcasted_iota(jnp.int32, after.shape, 0)
    rolled = pltpu.roll(cur, rows - j, 0)
    bottom = jnp.where(row >= SUBLANES - j, pltpu.roll(after, SUBLANES - j, 0), rolled[rows - SUBLANES:, :])
    return jnp.concatenate([rolled[:rows - SUBLANES, :], bottom], axis=0)


def _fold_rows(v):
    return v.reshape(v.shape[0] // SUBLANES, SUBLANES, v.shape[1]).sum(axis=0)


def _conv_causal(pad_ref, w, r0, rows, taps):
    acc = None
    for k in range(taps):
        term = w[k:k + 1, :] * _rows_back(pad_ref, r0, rows, taps - 1 - k)
        acc = term if acc is None else acc + term
    return acc


def _conv_anticausal(pad_ref, w, r0, rows, taps):
    acc = None
    for k in range(taps):
        term = w[k:k + 1, :] * _rows_ahead(pad_ref, r0, rows, taps - 1 - k)
        acc = term if acc is None else acc + term
    return acc


def _conv_wgrad(g, xpad_ref, r0, rows, taps):
    return [jnp.sum(g * _rows_back(xpad_ref, r0, rows, taps - 1 - k), axis=0, keepdims=True) for k in range(taps)]


def _position():
    return lax.axis_index("x"), lax.axis_index("y"), lax.axis_index("c")


def _block_of(x, y, c):
    return 4 * x + 2 * y + c


def _chip(x, y, k):
    return (x + (k & 1)) % 2, (y + (k >> 1)) % 2


def _cols(width):
    def at(ref, d, half=None):
        cols = pl.ds(pl.multiple_of(d * width, LANES), width)
        if half is None:
            return ref.at[:, cols]
        return ref.at[pl.ds(half * (ref.shape[0] // 2), ref.shape[0] // 2), cols]
    return at


def _rows(height):
    def at(ref, d, half=None):
        if half is None:
            return ref.at[pl.ds(pl.multiple_of(d * height, 16), height), :]
        return ref.at[pl.ds(pl.multiple_of(d * height + half * (height // 2), 16), height // 2), :]
    return at


def _lead(ref, d, half=None):
    if half is None:
        return ref.at[d]
    return ref.at[d, pl.ds(half * (ref.shape[1] // 2), ref.shape[1] // 2)]


def _gather_weights(shards, blocks, full_shapes, small, n_now, tokens, gain):
    n = len(shards)
    small_rows = small.shape[0]
    t = tokens.shape[0]
    rc = min(512, t)

    def body(*refs):
        ins, small_in, x_ref, g_ref = refs[:n], refs[n], refs[n + 1], refs[n + 2]
        outs, small_out, proj_ref, h_ref = refs[n + 3:2 * n + 3], refs[2 * n + 3], refs[2 * n + 4], refs[2 * n + 5]
        stage = refs[2 * n + 6:3 * n + 6]
        w_buf, p_buf, send, recv, local, w_sem, p_sem = refs[3 * n + 6:]
        x, y, c = _position()
        me = _block_of(x, y, c)
        sibling = (x, y, 1 - c)

        for a in range(n):
            stage[a][...] = ins[a][...].astype(BF16)
        for r0 in range(0, t, rc):
            normed, _ = _rms_fwd(x_ref[pl.ds(r0, rc), :])
            h_ref[pl.ds(r0, rc), :] = (normed * g_ref[...]).astype(BF16)
        stores = []

        def project(w_ref, block):
            i = len(stores)
            if i >= 2:
                stores[i - 2].wait()
            for r0 in range(0, t, rc):
                p_buf[i % 2, pl.ds(r0, rc), :] = _dot(h_ref[pl.ds(r0, rc), :], w_ref[...]).astype(BF16)
            st = pltpu.make_async_copy(p_buf.at[i % 2], blocks[0](proj_ref, block), p_sem.at[i % 2])
            st.start()
            stores.append(st)

        def project_landed(block):
            ld = pltpu.make_async_copy(blocks[0](outs[0], block), w_buf, w_sem)
            ld.start()
            ld.wait()
            project(w_buf, block)

        def copy(a, k, block, to, src=None, half=None):
            dst = blocks[a](outs[a], block, half)
            return pltpu.make_async_remote_copy(
                src_ref=dst if src is None else src, dst_ref=dst, send_sem=send.at[a, k], recv_sem=recv.at[a, k],
                device_id=to, device_id_type=MESH)

        def small_copy(k):
            px, py, pc = (x + (k & 1)) % 2, (y + ((k >> 1) & 1)) % 2, (c + (k >> 2)) % 2
            return pltpu.make_async_remote_copy(
                src_ref=small_in, dst_ref=small_out.at[me], send_sem=send.at[n_now, k - 1], recv_sem=recv.at[n_now, k - 1],
                device_id=(px, py, pc), device_id_type=MESH)

        def small_arrival(k):
            px, py, pc = (x + (k & 1)) % 2, (y + ((k >> 1) & 1)) % 2, (c + (k >> 2)) % 2
            return pltpu.make_async_remote_copy(
                src_ref=small_in, dst_ref=small_out.at[_block_of(px, py, pc)], send_sem=send.at[n_now, k - 1],
                recv_sem=recv.at[n_now, k - 1], device_id=(px, py, pc), device_id_type=MESH)

        small_out[me] = small_in[...]
        small_sends = [small_copy(k) for k in range(1, N_DEV)]
        for cp in small_sends:
            cp.start()

        mine, first, passed = [], [], []
        for a in range(n):
            own = pltpu.make_async_copy(stage[a], blocks[a](outs[a], me), local.at[a])
            own.start()
            mine.append(own)
            if a >= n_now:
                continue
            sends = [copy(a, 0, me, sibling, src=stage[a])]
            sends += [copy(a, k, me, (*_chip(x, y, k), c), src=stage[a]) for k in (1, 2)]
            for cp in sends:
                cp.start()
            first += sends

        here = (x, y, c)
        across = [(*_chip(x, y, k), c) for k in (1, 2)]
        near = [[_block_of(*_chip(x, y, k), cc) for k in (1, 2)] for cc in (c, 1 - c)]
        far = [_block_of(*_chip(x, y, 3), cc) for cc in (c, 1 - c)]

        def launch(cp):
            cp.start()
            passed.append(cp)

        project(stage[0], me)
        copy(0, 0, _block_of(x, y, 1 - c), here).wait_recv()
        project_landed(_block_of(x, y, 1 - c))
        for a in range(n_now):
            for i in (0, 1):
                copy(a, 1 + i, near[0][i], here).wait_recv()
                launch(copy(a, 3 + i, near[0][i], across[1 - i], half=i))
                launch(copy(a, 5 + i, near[0][i], sibling))
            if a == 0:
                project_landed(near[0][0])
                project_landed(near[0][1])
        for i in (0, 1):
            copy(0, 5 + i, near[1][i], here).wait_recv()
            project_landed(near[1][i])
        for a in range(n_now):
            for i in (0, 1):
                copy(a, 3 + i, far[0], here, half=i).wait_recv()
                launch(copy(a, 7 + i, far[0], sibling, half=i))
            if a == 0:
                project_landed(far[0])
        for a in range(n_now):
            if a > 0:
                copy(a, 0, _block_of(x, y, 1 - c), here).wait_recv()
                for i in (0, 1):
                    copy(a, 5 + i, near[1][i], here).wait_recv()
            for i in (0, 1):
                copy(a, 7 + i, far[1], here, half=i).wait_recv()
            if a == 0:
                project_landed(far[1])
        for k in range(1, N_DEV):
            small_arrival(k).wait_recv()
        for cp in first + passed + small_sends:
            cp.wait_send()
        for done in mine + stores[-2:]:
            done.wait()

    out_shape = [jax.ShapeDtypeStruct(s, BF16) for s in full_shapes]
    out_shape += [jax.ShapeDtypeStruct((N_DEV, small_rows, LANES), F32), jax.ShapeDtypeStruct((t, full_shapes[0][1]), BF16),
                  jax.ShapeDtypeStruct(tokens.shape, BF16)]
    return pl.pallas_call(
        body, name="gather_weights", out_shape=out_shape,
        in_specs=[VMEM_SPEC] * (n + 3), out_specs=[HBM_SPEC] * n + [VMEM_SPEC, HBM_SPEC, VMEM_SPEC],
        scratch_shapes=[pltpu.VMEM(s.shape, BF16) for s in shards]
        + [pltpu.VMEM(shards[0].shape, BF16), pltpu.VMEM((2, t, shards[0].shape[1]), BF16),
           pltpu.SemaphoreType.DMA((n_now + 1, 9)), pltpu.SemaphoreType.DMA((n_now + 1, 9)),
           pltpu.SemaphoreType.DMA((n,)), pltpu.SemaphoreType.DMA(()), pltpu.SemaphoreType.DMA((2,))],
        compiler_params=_params(),
    )(*shards, small, tokens, gain)


def _gather_first(full, blocks, send, recv):
    x, y, c = _position()
    me = _block_of(x, y, c)
    peers = [(x, y, 1 - c)] + [(*_chip(x, y, k), c) for k in (1, 2, 3)]

    def copy(a, k, block):
        at = blocks[a](full[a], block)
        return pltpu.make_async_remote_copy(src_ref=at, dst_ref=at, send_sem=send[4 * a + k], recv_sem=recv[4 * a + k],
                                            device_id=peers[k], device_id_type=MESH)

    sends = [copy(a, k, me) for a in range(len(full)) for k in range(4)]
    arrivals = [copy(a, k, _block_of(*peers[k])) for a in range(len(full)) for k in range(4)]
    return sends, arrivals


def _gather_second(full, blocks, send, recv):
    x, y, c = _position()

    def copy(a, k, cc):
        at = blocks[a](full[a], _block_of(*_chip(x, y, k), cc))
        return pltpu.make_async_remote_copy(src_ref=at, dst_ref=at, send_sem=send[3 * a + k - 1],
                                            recv_sem=recv[3 * a + k - 1], device_id=(x, y, 1 - c), device_id_type=MESH)

    sends = [copy(a, k, c) for a in range(len(full)) for k in (1, 2, 3)]
    arrivals = [copy(a, k, 1 - c) for a in range(len(full)) for k in (1, 2, 3)]
    return sends, arrivals


def _split_call(body, name, arrays, sems_in, n_sems_out, after=None, token=False):
    n, m = len(arrays), len(sems_in)

    def kernel_body(*refs):
        outs = refs[n + m + (after is not None):]
        body(refs[:n], refs[n:n + m], outs[:n_sems_out])
        if token:
            outs[-1][...] = jnp.zeros_like(outs[-1])

    extra_in = [] if after is None else [after]
    outs = pl.pallas_call(
        kernel_body, name=name,
        out_shape=(*[pltpu.SemaphoreType.DMA(())] * n_sems_out, *[pltpu.HBM(a.shape, a.dtype) for a in arrays],
                   *([jax.ShapeDtypeStruct((SUBLANES, LANES), F32)] if token else [])),
        in_specs=[HBM_SPEC] * n + [SEM_SPEC] * m + [pl.BlockSpec(memory_space=pl.ANY)] * len(extra_in),
        out_specs=(*[SEM_SPEC] * n_sems_out, *[HBM_SPEC] * n, *([VMEM_SPEC] if token else [])),
        input_output_aliases={i: n_sems_out + i for i in range(n)},
        compiler_params=pltpu.CompilerParams(has_side_effects=DATAFLOW_EFFECT),
    )(*[pltpu.with_memory_space_constraint(a, pltpu.HBM) for a in arrays], *sems_in, *extra_in)
    sems, rest = list(outs[:n_sems_out]), list(outs[n_sems_out:])
    return (sems, rest[:n], rest[n]) if token else (sems, rest[:n])


def _gather_start(full, blocks, name):
    n = len(full)

    def body(arrays, _, sems):
        for cp in _gather_first(arrays, blocks, sems[:4 * n], sems[4 * n:])[0]:
            cp.start()

    sems, arrays, token = _split_call(body, name, full, [], 8 * n, token=True)
    return sems[:4 * n], sems[4 * n:], arrays, token


def _gather_forward(full, blocks, send_first, recv_first, after, name):
    n = len(full)

    def body(arrays, sems_in, sems):
        sends, arrivals = _gather_first(arrays, blocks, sems_in[:4 * n], sems_in[4 * n:])
        for cp in arrivals:
            cp.wait_recv()
        for cp in _gather_second(arrays, blocks, sems[:3 * n], sems[3 * n:])[0]:
            cp.start()
        for cp in sends:
            cp.wait_send()

    sems, arrays = _split_call(body, name, full, [*send_first, *recv_first], 6 * n, after=after)
    return sems[:3 * n], sems[3 * n:], arrays


def _gather_finish(full, blocks, send_second, recv_second, after, name):
    n = len(full)

    def body(arrays, sems_in, _):
        sends, arrivals = _gather_second(arrays, blocks, sems_in[:3 * n], sems_in[3 * n:])
        for cp in sends:
            cp.wait_send()
        for cp in arrivals:
            cp.wait_recv()

    return _split_call(body, name, full, [*send_second, *recv_second], 0, after=after)[1]


def _reduce_pair(grads, blocks, shard_shapes, name):
    n = len(grads)

    def body(*refs):
        ins, outs = refs[:n], refs[n:2 * n]
        got, own = refs[2 * n:3 * n], refs[3 * n:4 * n]
        send, recv, local = refs[4 * n:]
        x, y, c = _position()
        copies, loads = [], []
        for a in range(n):
            for k in range(4):
                chip = _chip(x, y, k)
                cp = pltpu.make_async_remote_copy(
                    src_ref=blocks[a](ins[a], _block_of(*chip, 1 - c)), dst_ref=got[a].at[k],
                    send_sem=send.at[a, k], recv_sem=recv.at[a, k], device_id=(x, y, 1 - c), device_id_type=MESH)
                cp.start()
                copies.append(cp)
                ld = pltpu.make_async_copy(blocks[a](ins[a], _block_of(*chip, c)), own[a].at[k], local.at[a, k])
                ld.start()
                loads.append(ld)
        for a in range(n):
            for k in range(4):
                loads[4 * a + k].wait()
                copies[4 * a + k].wait_recv()
                outs[a][k] = (own[a][k].astype(F32) + got[a][k].astype(F32)).astype(BF16)
        for cp in copies:
            cp.wait_send()

    slots = [(4,) + tuple(s) for s in shard_shapes]
    return pl.pallas_call(
        body, name=name, out_shape=[jax.ShapeDtypeStruct(s, BF16) for s in slots],
        in_specs=[HBM_SPEC] * n, out_specs=[VMEM_SPEC] * n,
        scratch_shapes=[pltpu.VMEM(s, BF16) for s in slots] * 2
        + [pltpu.SemaphoreType.DMA((n, 4)), pltpu.SemaphoreType.DMA((n, 4)), pltpu.SemaphoreType.DMA((n, 4))],
        compiler_params=_params(),
    )(*grads)


def _chip_copies(sums, lands, send, recv):
    x, y, c = _position()
    return [pltpu.make_async_remote_copy(
        src_ref=sums[a].at[k], dst_ref=lands[a].at[k - 1], send_sem=send[3 * a + k - 1], recv_sem=recv[3 * a + k - 1],
        device_id=(*_chip(x, y, k), c), device_id_type=MESH) for a in range(len(sums)) for k in (1, 2, 3)]


def _exchange_chips_start(pair_sums, name):
    n = len(pair_sums)
    lands = [pltpu.with_memory_space_constraint(lax.empty((3,) + tuple(p.shape[1:]), BF16), pltpu.HBM) for p in pair_sums]

    def body(*refs):
        sums, zones = refs[:n], refs[n:2 * n]
        send, recv = refs[2 * n:5 * n], refs[5 * n:8 * n]
        token = refs[-1]
        for cp in _chip_copies(sums, zones, send, recv):
            cp.start()
        token[...] = jnp.zeros_like(token)

    outs = pl.pallas_call(
        body, name=name,
        out_shape=(*[pltpu.SemaphoreType.DMA(())] * (6 * n),
                   *[pltpu.HBM(p.shape, BF16) for p in pair_sums], *[pltpu.HBM(z.shape, BF16) for z in lands],
                   jax.ShapeDtypeStruct((SUBLANES, LANES), F32)),
        in_specs=[HBM_SPEC] * (2 * n), out_specs=(*[SEM_SPEC] * (6 * n), *[HBM_SPEC] * (2 * n), VMEM_SPEC),
        input_output_aliases={i: 6 * n + i for i in range(2 * n)},
        compiler_params=pltpu.CompilerParams(has_side_effects=DATAFLOW_EFFECT),
    )(*[pltpu.with_memory_space_constraint(p, pltpu.HBM) for p in pair_sums], *lands)
    return outs[:3 * n], outs[3 * n:6 * n], outs[6 * n:7 * n], outs[7 * n:8 * n], outs[-1]


def _exchange_chips_wait(send, recv, sums, lands, after, name):
    n = len(sums)

    def body(*refs):
        sums_in, zones = refs[:n], refs[n:2 * n]
        send_in, recv_in = refs[2 * n:5 * n], refs[5 * n:8 * n]
        for cp in _chip_copies(sums_in, zones, send_in, recv_in):
            cp.wait_send()
            cp.wait_recv()

    outs = pl.pallas_call(
        body, name=name,
        out_shape=(*[pltpu.HBM(p.shape, BF16) for p in sums], *[pltpu.HBM(z.shape, BF16) for z in lands]),
        in_specs=[HBM_SPEC] * (2 * n) + [SEM_SPEC] * (6 * n) + [pl.BlockSpec(memory_space=pl.ANY)],
        out_specs=[HBM_SPEC] * (2 * n), input_output_aliases={i: i for i in range(2 * n)},
        compiler_params=pltpu.CompilerParams(has_side_effects=DATAFLOW_EFFECT),
    )(*sums, *lands, *send, *recv, after)
    return outs[:n], outs[n:]


def _small_copies(mine, land, send, recv):
    x, y, c = _position()
    me = _block_of(x, y, c)

    def peer(k):
        return (x + (k & 1)) % 2, (y + ((k >> 1) & 1)) % 2, (c + (k >> 2)) % 2

    def copy(k, slot):
        return pltpu.make_async_remote_copy(src_ref=mine, dst_ref=land.at[slot], send_sem=send[k - 1], recv_sem=recv[k - 1],
                                            device_id=peer(k), device_id_type=MESH)

    return [copy(k, me) for k in range(1, N_DEV)], [copy(k, _block_of(*peer(k))) for k in range(1, N_DEV)]


def _small_start(part, name):
    land = jnp.zeros((N_DEV,) + part.shape, F32)

    def body(arrays, _, sems):
        for cp in _small_copies(arrays[0], arrays[1], sems[:7], sems[7:])[0]:
            cp.start()

    sems, arrays, token = _split_call(body, name, [part, land], [], 14, token=True)
    return sems[:7], sems[7:], arrays[0], arrays[1], token


def _small_wait(send, recv, part, land, after, name):
    def body(arrays, sems_in, _):
        sends, arrivals = _small_copies(arrays[0], arrays[1], sems_in[:7], sems_in[7:])
        for cp in sends:
            cp.wait_send()
        for cp in arrivals:
            cp.wait_recv()

    return _split_call(body, name, [part, land], [*send, *recv], 0, after=after)[1]


def _small_sum(pairs, me):
    n = len(pairs)

    def body(me_ref, *refs):
        for i in range(n):
            mine, land, out = refs[2 * i], refs[2 * i + 1], refs[2 * n + i]
            total = jnp.zeros(mine.shape, F32)
            for d in range(N_DEV):
                total = total + land[d] + jnp.where(me_ref[0] == d, mine[...], 0.0)
            out[...] = total

    flat = [a for pair in pairs for a in pair]
    return pl.pallas_call(
        body, name="small_sum", out_shape=[jax.ShapeDtypeStruct(mine.shape, F32) for mine, _ in pairs],
        in_specs=[pl.BlockSpec(memory_space=pltpu.SMEM)] + [VMEM_SPEC] * (2 * n), out_specs=[VMEM_SPEC] * n,
        compiler_params=_params(),
    )(me.reshape(1).astype(jnp.int32), *flat)


def _section(s, t):
    return pl.BlockSpec((t, CB), lambda h, s=s: (0, s * (D_MODEL // CB) + h))


def _conv_mixer_fwd(proj, w_short):
    t = proj.shape[0]
    rc = _row_chunk(t)

    def body(b_ref, c_ref, x_ref, w_ref, y_ref, pad):
        pad[pl.ds(0, PAD), :] = jnp.zeros((PAD, CB), F32)
        for r0 in range(0, t, rc):
            rows = pl.ds(r0, rc)
            pad[pl.ds(PAD + r0, rc), :] = c_ref[rows, :].astype(F32) * x_ref[rows, :].astype(F32)
        w = w_ref[...]
        for r0 in range(0, t, rc):
            rows = pl.ds(r0, rc)
            y_ref[rows, :] = (b_ref[rows, :].astype(F32) * _conv_causal(pad, w, r0, rc, 3)).astype(BF16)

    return pl.pallas_call(
        body, name="conv_mixer_fwd", grid=(D_MODEL // CB,),
        out_shape=jax.ShapeDtypeStruct((t, D_MODEL), BF16),
        in_specs=[_section(0, t), _section(1, t), _section(2, t), pl.BlockSpec((3, CB), lambda h: (0, h))],
        out_specs=pl.BlockSpec((t, CB), lambda h: (0, h)),
        scratch_shapes=[pltpu.VMEM((t + PAD, CB), F32)],
        compiler_params=_params("parallel"),
    )(proj, proj, proj, w_short)


def _lru_gates(xl, wa, ba, wx, bx, ls, first_row):
    xb = xl.astype(BF16)
    ra = jax.nn.sigmoid(_dot(xb, wa) + ba)
    ia = jax.nn.sigmoid(_dot(xb, wx) + bx)
    la = LRU_C * ra * ls
    a = jnp.exp(la)
    one_minus = -_expm1_neg(2.0 * la)
    mult = jnp.where(first_row, 1.0, jnp.sqrt(one_minus))
    return xb, ra, ia, a, one_minus, mult


def _head_specs():
    vec = pl.BlockSpec((1, CB), lambda h: (0, h))
    mat = pl.BlockSpec((N_DEV, None, HEAD_DIM // N_DEV, HEAD_DIM), lambda h: (0, h, 0, 0))
    return vec, mat


def _lru_fwd(proj, w_conv, b_conv, wa, ba, wx, bx, lam):
    t = proj.shape[0]
    rc = _row_chunk(t)
    vec, mat = _head_specs()

    def body(lx_ref, ly_ref, wc_ref, bc_ref, wa_ref, ba_ref, wx_ref, bx_ref, lam_ref, yb_ref, hl_ref, a_ref, kept_ref,
             pad, u_s):
        pad[pl.ds(0, PAD), :] = jnp.zeros((PAD, CB), F32)
        for r0 in range(0, t, rc):
            pad[pl.ds(PAD + r0, rc), :] = lx_ref[pl.ds(r0, rc), :].astype(F32)
        wc, bc = wc_ref[...], bc_ref[...]
        wa_m, wx_m = wa_ref[...].reshape(HEAD_DIM, HEAD_DIM), wx_ref[...].reshape(HEAD_DIM, HEAD_DIM)
        ls = _log_sigmoid(lam_ref[...])
        for r0 in range(0, t, rc):
            rows = pl.ds(r0, rc)
            xl = _conv_causal(pad, wc, r0, rc, 4) + bc
            first = (lax.broadcasted_iota(jnp.int32, (rc, CB), 0) + r0) == 0
            xb, ra, ia, a, _, mult = _lru_gates(xl, wa_m, ba_ref[...], wx_m, bx_ref[...], ls, first)
            a_ref[rows, :] = a
            u_s[rows, :] = mult * (ia * xl)
            kept_ref[0, rows, :] = xb
            kept_ref[1, rows, :] = ra.astype(BF16)
            kept_ref[2, rows, :] = ia.astype(BF16)

        row = lax.broadcasted_iota(jnp.int32, (SUBLANES, CB), 0)

        def group(g, carry):
            r = pl.multiple_of(g * SUBLANES, SUBLANES)
            a_g, b_g = a_ref[pl.ds(r, SUBLANES), :], u_s[pl.ds(r, SUBLANES), :]
            for s in (1, 2, 4):
                keep = row >= s
                b_g = jnp.where(keep, a_g * pltpu.roll(b_g, s, 0) + b_g, b_g)
                a_g = jnp.where(keep, a_g * pltpu.roll(a_g, s, 0), a_g)
            h_g = b_g + a_g * carry
            hl_ref[pl.ds(r, SUBLANES), :] = h_g
            return jnp.broadcast_to(h_g[SUBLANES - 1:SUBLANES, :], (SUBLANES, CB))

        def trip(i, carry):
            for j in range(SCAN_UNROLL):
                carry = group(i * SCAN_UNROLL + j, carry)
            return carry

        lax.fori_loop(0, t // SUBLANES // SCAN_UNROLL, trip, jnp.zeros((SUBLANES, CB), F32))
        for r0 in range(0, t, rc):
            rows = pl.ds(r0, rc)
            yb_ref[rows, :] = (hl_ref[rows, :] * _gelu(ly_ref[rows, :].astype(F32))).astype(BF16)

    blk = pl.BlockSpec((t, CB), lambda h: (0, h))
    res = jax.ShapeDtypeStruct((t, D_MODEL), F32)
    return pl.pallas_call(
        body, name="lru_fwd", grid=(N_HEADS,),
        out_shape=[jax.ShapeDtypeStruct((t, D_MODEL), BF16), res, res, jax.ShapeDtypeStruct((3, t, D_MODEL), BF16)],
        in_specs=[_section(3, t), _section(4, t), pl.BlockSpec((4, CB), lambda h: (0, h)), vec, mat, vec, mat, vec, vec],
        out_specs=[blk, blk, blk, pl.BlockSpec((3, t, CB), lambda h: (0, 0, h))],
        scratch_shapes=[pltpu.VMEM((t + PAD, CB), F32), pltpu.VMEM((t, CB), F32)],
        compiler_params=_params("parallel"),
    )(proj, proj, w_conv, b_conv, wa, ba, wx, bx, lam)


def _merge(y_a, y_b, proj, x, w_cb, w_lb, w_out, g2, g3):
    t = x.shape[0]
    tm = min(512, t)

    def body(ya_ref, yb_ref, gc_ref, gl_ref, x_ref, wcb_ref, wlb_ref, wo_ref, g2_ref, g3_ref,
             pa_ref, pb_ref, mg_ref, mix_ref, x1_ref, h2_ref):
        pa = _dot(ya_ref[...], wcb_ref[...]).astype(BF16)
        pb = _dot(yb_ref[...], wlb_ref[...]).astype(BF16)
        pa_ref[...] = pa
        pb_ref[...] = pb
        merged = (jax.nn.sigmoid(gc_ref[...].astype(F32)) * pa.astype(F32)
                  + jax.nn.sigmoid(gl_ref[...].astype(F32)) * pb.astype(F32)).astype(BF16)
        mg_ref[...] = merged
        mix = _dot(merged, wo_ref[...])
        mix_ref[...] = mix
        n2, _ = _rms_fwd(mix)
        x1 = x_ref[...] + n2 * g2_ref[...]
        x1_ref[...] = x1
        n3, _ = _rms_fwd(x1)
        h2_ref[...] = (n3 * g3_ref[...]).astype(BF16)

    row = pl.BlockSpec((tm, D_MODEL), lambda i: (i, 0))
    full = pl.BlockSpec((D_MODEL, D_MODEL), lambda i: (0, 0))
    vec = pl.BlockSpec((1, D_MODEL), lambda i: (0, 0))
    act = jax.ShapeDtypeStruct((t, D_MODEL), BF16)
    res = jax.ShapeDtypeStruct((t, D_MODEL), F32)
    return pl.pallas_call(
        body, name="merge_fwd", grid=(t // tm,), out_shape=[act, act, act, res, res, act],
        in_specs=[row, row, pl.BlockSpec((tm, D_MODEL), lambda i: (i, 5)), pl.BlockSpec((tm, D_MODEL), lambda i: (i, 6)),
                  row, full, full, full, vec, vec],
        out_specs=[row] * 6,
        compiler_params=_params("parallel"),
    )(y_a, y_b, proj, proj, x, w_cb, w_lb, w_out, g2, g3)


N_FF_BLOCKS = D_FF // CB
FFN_BWD_COLS = 512


def _ffn_up(h2, w_up, w_conv, b_conv):
    t = h2.shape[0]
    rc = _row_chunk(t)
    nb = N_FF_BLOCKS

    def body(h_ref, w_ref, c_ref, b_ref, up_ref, act_ref, f_ref, pad, gate):
        k = pl.program_id(1)
        pad[pl.ds(0, PAD), :] = jnp.zeros((PAD, CB), F32)
        for r0 in range(0, t, rc):
            rows = pl.ds(r0, rc)
            up = _dot(h_ref[rows, :], w_ref[...]).astype(BF16)
            up_ref[rows, :] = up
            pad[pl.ds(PAD + r0, rc), :] = up.astype(F32)
        def conv(keep_gate):
            cw = c_ref[...]
            for r0 in range(0, t, rc):
                rows = pl.ds(r0, rc)
                act = _conv_causal(pad, cw, r0, rc, 3) + b_ref[...]
                act_ref[rows, :] = act.astype(BF16)
                if keep_gate:
                    gate[rows, :] = act
                else:
                    f_ref[rows, :] = (_gelu(gate[rows, :]) * act).astype(BF16)

        @pl.when(k == 0)
        def _():
            conv(True)

        @pl.when(k == 1)
        def _():
            conv(False)

    half = lambda rows: pl.BlockSpec((rows, CB), lambda j, k: (0, nb * k + j))
    wide = jax.ShapeDtypeStruct((t, 2 * D_FF), BF16)
    return pl.pallas_call(
        body, name="ffn_up_fwd", grid=(nb, 2), out_shape=[wide, wide, jax.ShapeDtypeStruct((t, D_FF), BF16)],
        in_specs=[pl.BlockSpec((t, D_MODEL), lambda j, k: (0, 0)), half(D_MODEL), half(3), half(1)],
        out_specs=[half(t), half(t), pl.BlockSpec((t, CB), lambda j, k: (0, j))],
        scratch_shapes=[pltpu.VMEM((t + PAD, CB), F32), pltpu.VMEM((t, CB), F32)],
        compiler_params=_params("parallel", "arbitrary"),
    )(h2, w_up, w_conv, b_conv)


def _ffn_down(f, act, w_down, x1, target, g4):
    t = f.shape[0]
    tm = min(256, t)
    cc = 512

    def body(f_ref, act_ref, w_ref, x1_ref, tg_ref, g_ref, dy_ref, dout_ref, back_ref, dg_ref, loss_ref):
        @pl.when(pl.program_id(0) == 0)
        def _():
            dg_ref[...] = jnp.zeros_like(dg_ref)
            loss_ref[...] = jnp.zeros_like(loss_ref)
        out = _dot(f_ref[...], w_ref[...])
        n4, r4 = _rms_fwd(out)
        err = x1_ref[...] + n4 * g_ref[...] - tg_ref[...]
        loss_ref[...] += jnp.full(loss_ref.shape, 0.5 / D_MODEL, F32) * jnp.sum(err * err)
        dy = err * (1.0 / D_MODEL)
        dy_ref[...] = dy
        dg_ref[...] += jnp.sum(dy * n4, axis=0, keepdims=True)
        d_out = _rms_bwd(n4, r4, dy * g_ref[...]).astype(BF16)
        dout_ref[...] = d_out
        for c0 in range(0, D_FF, cc):
            d_f = _dot_nt(d_out, w_ref[pl.ds(c0, cc), :])
            gelu, d_gelu = _gelu_and_grad(act_ref[:, pl.ds(c0, cc)].astype(F32))
            val = act_ref[:, pl.ds(D_FF + c0, cc)].astype(F32)
            back_ref[:, pl.ds(c0, cc)] = (d_f * val * d_gelu).astype(BF16)
            back_ref[:, pl.ds(D_FF + c0, cc)] = (d_f * gelu).astype(BF16)

    row = pl.BlockSpec((tm, D_MODEL), lambda i: (i, 0))
    wide = pl.BlockSpec((tm, 2 * D_FF), lambda i: (i, 0))
    vec = pl.BlockSpec((1, D_MODEL), lambda i: (0, 0))
    return pl.pallas_call(
        body, name="ffn_down_fwd_bwd", grid=(t // tm,),
        out_shape=[jax.ShapeDtypeStruct((t, D_MODEL), F32), jax.ShapeDtypeStruct((t, D_MODEL), BF16),
                   jax.ShapeDtypeStruct((t, 2 * D_FF), BF16), jax.ShapeDtypeStruct((1, D_MODEL), F32),
                   jax.ShapeDtypeStruct((SUBLANES, LANES), F32)],
        in_specs=[pl.BlockSpec((tm, D_FF), lambda i: (i, 0)), wide, pl.BlockSpec((D_FF, D_MODEL), lambda i: (0, 0)),
                  row, row, vec],
        out_specs=[row, row, wide, vec, pl.BlockSpec((SUBLANES, LANES), lambda i: (0, 0))],
        compiler_params=_params("arbitrary"),
    )(f, act, w_down, x1, target, g4)


def _grad_tn(a, b, bm, name):
    t, m = a.shape
    n = b.shape[1]

    def body(a_ref, b_ref, o_ref):
        o_ref[...] = _dot_tn(a_ref[...], b_ref[...]).astype(BF16)

    return pl.pallas_call(
        body, name=name, grid=(m // bm,), out_shape=jax.ShapeDtypeStruct((m, n), BF16),
        in_specs=[pl.BlockSpec((t, bm), lambda i: (0, i)), pl.BlockSpec((t, n), lambda i: (0, 0))],
        out_specs=pl.BlockSpec((bm, n), lambda i: (i, 0)),
        compiler_params=_params("parallel"),
    )(a, b)


def _ffn_up_bwd(up, back, w_conv, h2, w_up):
    t = h2.shape[0]
    rc = _row_chunk(t)
    cb = FFN_BWD_COLS

    def body(up_ref, back_ref, c_ref, h_ref, w_ref, dw_ref, dcw_ref, dcb_ref, dh_ref, pad, after, d_up):
        @pl.when(pl.program_id(0) == 0)
        def _():
            dh_ref[...] = jnp.zeros_like(dh_ref)
        pad[pl.ds(0, PAD), :] = jnp.zeros((PAD, cb), F32)
        after[pl.ds(t, PAD), :] = jnp.zeros((PAD, cb), F32)
        for r0 in range(0, t, rc):
            pad[pl.ds(PAD + r0, rc), :] = up_ref[pl.ds(r0, rc), :].astype(F32)
            after[pl.ds(r0, rc), :] = back_ref[pl.ds(r0, rc), :].astype(F32)
        cw = c_ref[...]
        taps = [jnp.zeros((SUBLANES, cb), F32)] * 3
        bias = jnp.zeros((SUBLANES, cb), F32)
        for r0 in range(0, t, rc):
            for q0 in range(r0, r0 + rc, ROW_SLICE):
                rows = pl.ds(q0, ROW_SLICE)
                d_up[rows, :] = _conv_anticausal(after, cw, q0, ROW_SLICE, 3).astype(BF16)
                g = after[rows, :]
                taps = [acc + _fold_rows(g * _rows_back(pad, q0, ROW_SLICE, 2 - k)) for k, acc in enumerate(taps)]
                bias = bias + _fold_rows(g)
            rows = pl.ds(r0, rc)
            dh_ref[rows, :] += _dot_nt(d_up[rows, :], w_ref[...])
        dw_ref[...] = _dot_tn(h_ref[...], d_up[...]).astype(BF16)
        dcw_ref[...] = jnp.concatenate([jnp.sum(acc, axis=0, keepdims=True) for acc in taps], axis=0)
        dcb_ref[...] = jnp.sum(bias, axis=0, keepdims=True)

    cols = lambda rows: pl.BlockSpec((rows, cb), lambda j: (0, j))
    whole = pl.BlockSpec((t, D_MODEL), lambda j: (0, 0))
    return pl.pallas_call(
        body, name="ffn_up_bwd", grid=(2 * D_FF // cb,),
        out_shape=[jax.ShapeDtypeStruct((D_MODEL, 2 * D_FF), BF16), jax.ShapeDtypeStruct((3, 2 * D_FF), F32),
                   jax.ShapeDtypeStruct((1, 2 * D_FF), F32), jax.ShapeDtypeStruct((t, D_MODEL), F32)],
        in_specs=[cols(t), cols(t), cols(3), whole, cols(D_MODEL)],
        out_specs=[cols(D_MODEL), cols(3), cols(1), whole],
        scratch_shapes=[pltpu.VMEM((t + PAD, cb), F32), pltpu.VMEM((t + PAD, cb), F32), pltpu.VMEM((t, cb), BF16)],
        compiler_params=_params("arbitrary"),
    )(up, back, w_conv, h2, w_up)


def _merge_bwd(dy, d_h2, x1, mix, g3, g2, w_out, w_cb, w_lb, pa, pb, proj):
    t = dy.shape[0]
    tm = min(256, t)

    def body(dy_ref, dh2_ref, x1_ref, mix_ref, g3_ref, g2_ref, wo_ref, wcb_ref, wlb_ref, pa_ref, pb_ref, gc_ref, gl_ref,
             dx1_ref, dmix_ref, dpa_ref, dpb_ref, dya_ref, dyb_ref, dgate_ref, dg3_ref, dg2_ref):
        @pl.when(pl.program_id(0) == 0)
        def _():
            dg3_ref[...] = jnp.zeros_like(dg3_ref)
            dg2_ref[...] = jnp.zeros_like(dg2_ref)
        n3, r3 = _rms_fwd(x1_ref[...])
        d_h2 = dh2_ref[...]
        dg3_ref[...] += jnp.sum(d_h2 * n3, axis=0, keepdims=True)
        dx1 = dy_ref[...] + _rms_bwd(n3, r3, d_h2 * g3_ref[...])
        dx1_ref[...] = dx1
        n2, r2 = _rms_fwd(mix_ref[...])
        dg2_ref[...] += jnp.sum(dx1 * n2, axis=0, keepdims=True)
        d_mix = _rms_bwd(n2, r2, dx1 * g2_ref[...]).astype(BF16)
        dmix_ref[...] = d_mix
        d_merged = _dot_nt(d_mix, wo_ref[...])
        sc = jax.nn.sigmoid(gc_ref[...].astype(F32))
        sl = jax.nn.sigmoid(gl_ref[...].astype(F32))
        d_pa = (d_merged * sc).astype(BF16)
        d_pb = (d_merged * sl).astype(BF16)
        dpa_ref[...] = d_pa
        dpb_ref[...] = d_pb
        dgate_ref[0] = (d_merged * pa_ref[...].astype(F32) * sc * (1.0 - sc)).astype(BF16)
        dgate_ref[1] = (d_merged * pb_ref[...].astype(F32) * sl * (1.0 - sl)).astype(BF16)
        dya_ref[...] = _dot_nt(d_pa, wcb_ref[...]).astype(BF16)
        dyb_ref[...] = _dot_nt(d_pb, wlb_ref[...]).astype(BF16)

    row = pl.BlockSpec((tm, D_MODEL), lambda i: (i, 0))
    full = pl.BlockSpec((D_MODEL, D_MODEL), lambda i: (0, 0))
    vec = pl.BlockSpec((1, D_MODEL), lambda i: (0, 0))
    act = jax.ShapeDtypeStruct((t, D_MODEL), BF16)
    small = jax.ShapeDtypeStruct((1, D_MODEL), F32)
    return pl.pallas_call(
        body, name="merge_bwd", grid=(t // tm,),
        out_shape=[jax.ShapeDtypeStruct((t, D_MODEL), F32), act, act, act, act, act,
                   jax.ShapeDtypeStruct((2, t, D_MODEL), BF16), small, small],
        in_specs=[row, row, row, row, vec, vec, full, full, full, row, row,
                  pl.BlockSpec((tm, D_MODEL), lambda i: (i, 5)), pl.BlockSpec((tm, D_MODEL), lambda i: (i, 6))],
        out_specs=[row] * 6 + [pl.BlockSpec((2, tm, D_MODEL), lambda i: (0, i, 0)), vec, vec],
        compiler_params=_params("arbitrary"),
    )(dy, d_h2, x1, mix, g3, g2, w_out, w_cb, w_lb, pa, pb, proj, proj)


def _conv_mixer_bwd(proj, d_ya, w_short):
    t = proj.shape[0]
    rc = _row_chunk(t)

    def body(b_ref, c_ref, x_ref, dy_ref, w_ref, d_ref, dw_ref, pad, back):
        pad[pl.ds(0, PAD), :] = jnp.zeros((PAD, CB), F32)
        back[pl.ds(t, PAD), :] = jnp.zeros((PAD, CB), F32)
        for r0 in range(0, t, rc):
            rows = pl.ds(r0, rc)
            pad[pl.ds(PAD + r0, rc), :] = c_ref[rows, :].astype(F32) * x_ref[rows, :].astype(F32)
        w = w_ref[...]
        for r0 in range(0, t, rc):
            rows = pl.ds(r0, rc)
            d_y = dy_ref[rows, :].astype(F32)
            d_ref[0, rows, :] = (d_y * _conv_causal(pad, w, r0, rc, 3)).astype(BF16)
            back[rows, :] = d_y * b_ref[rows, :].astype(F32)
        taps = [jnp.zeros((1, CB), F32)] * 3
        for r0 in range(0, t, rc):
            rows = pl.ds(r0, rc)
            d_u = _conv_anticausal(back, w, r0, rc, 3)
            d_ref[1, rows, :] = (d_u * x_ref[rows, :].astype(F32)).astype(BF16)
            d_ref[2, rows, :] = (d_u * c_ref[rows, :].astype(F32)).astype(BF16)
            taps = [acc + new for acc, new in zip(taps, _conv_wgrad(back[rows, :], pad, r0, rc, 3))]
        dw_ref[...] = jnp.concatenate(taps, axis=0)

    blk = pl.BlockSpec((t, CB), lambda h: (0, h))
    return pl.pallas_call(
        body, name="conv_mixer_bwd", grid=(D_MODEL // CB,),
        out_shape=[jax.ShapeDtypeStruct((3, t, D_MODEL), BF16), jax.ShapeDtypeStruct((3, D_MODEL), F32)],
        in_specs=[_section(0, t), _section(1, t), _section(2, t), blk, pl.BlockSpec((3, CB), lambda h: (0, h))],
        out_specs=[pl.BlockSpec((3, t, CB), lambda h: (0, 0, h)), pl.BlockSpec((3, CB), lambda h: (0, h))],
        scratch_shapes=[pltpu.VMEM((t + PAD, CB), F32), pltpu.VMEM((t + PAD, CB), F32)],
        compiler_params=_params("parallel"),
    )(proj, proj, proj, d_ya, w_short)


LRU_SMALL_ROWS = 8


def _lru_bwd(proj, hl, a_all, kept, d_yb, w_conv, wa, wx, lam):
    t = proj.shape[0]
    rc = _row_chunk(t)
    vec, mat = _head_specs()

    def body(lx_ref, ly_ref, hl_ref, a_ref, kept_ref, dy_ref, wc_ref, wa_ref, wx_ref, lam_ref,
             d_ref, dwa_ref, dwx_ref, small_ref, pad, a_next, dh_s, dh_o, h_prev, back, acc_a, acc_x, dz_a, dz_x):
        zeros = jnp.zeros((PAD, CB), F32)
        pad[pl.ds(0, PAD), :] = zeros
        h_prev[pl.ds(0, PAD), :] = zeros
        a_next[pl.ds(t, PAD), :] = zeros
        back[pl.ds(t, PAD), :] = zeros
        for r0 in range(0, t, ROW_SLICE):
            rows = pl.ds(r0, ROW_SLICE)
            pad[pl.ds(PAD + r0, ROW_SLICE), :] = lx_ref[rows, :].astype(F32)
            h_prev[pl.ds(PAD + r0, ROW_SLICE), :] = hl_ref[rows, :]
            a_next[pl.ds(PAD - 1 + r0, ROW_SLICE), :] = a_ref[rows, :]
            act, d_act = _gelu_and_grad(ly_ref[rows, :].astype(F32))
            d_y = dy_ref[rows, :].astype(F32)
            dh_s[rows, :] = d_y * act
            d_ref[1, rows, :] = (d_y * hl_ref[rows, :] * d_act).astype(BF16)
        wc = wc_ref[...]
        wa_m, wx_m = wa_ref[...].reshape(HEAD_DIM, HEAD_DIM), wx_ref[...].reshape(HEAD_DIM, HEAD_DIM)
        ls = _log_sigmoid(lam_ref[...])

        row = lax.broadcasted_iota(jnp.int32, (SUBLANES, CB), 0)
        groups = t // SUBLANES

        def group(i, carry):
            r = pl.multiple_of((groups - 1 - i) * SUBLANES, SUBLANES)
            a_g, b_g = a_next[pl.ds(PAD + r, SUBLANES), :], dh_s[pl.ds(r, SUBLANES), :]
            for s in (1, 2, 4):
                keep = row < SUBLANES - s
                b_g = jnp.where(keep, a_g * pltpu.roll(b_g, SUBLANES - s, 0) + b_g, b_g)
                a_g = jnp.where(keep, a_g * pltpu.roll(a_g, SUBLANES - s, 0), a_g)
            d_g = b_g + a_g * carry
            dh_o[pl.ds(r, SUBLANES), :] = d_g
            return jnp.broadcast_to(d_g[0:1, :], (SUBLANES, CB))

        def trip(i, carry):
            for j in range(SCAN_UNROLL):
                carry = group(i * SCAN_UNROLL + j, carry)
            return carry

        lax.fori_loop(0, groups // SCAN_UNROLL, trip, jnp.zeros((SUBLANES, CB), F32))

        acc_a[...] = jnp.zeros_like(acc_a)
        acc_x[...] = jnp.zeros_like(acc_x)
        d_ba = d_bx = d_ls = jnp.zeros((SUBLANES, CB), F32)
        for r0 in range(0, t, rc):
            for q0 in range(r0, r0 + rc, ROW_SLICE):
                rows, local = pl.ds(q0, ROW_SLICE), pl.ds(q0 - r0, ROW_SLICE)
                a = a_ref[rows, :]
                xl, ra, ia = (kept_ref[i, rows, :].astype(F32) for i in range(3))
                a_sq = a * a
                mult = jnp.sqrt(1.0 - a_sq)
                slope = -a_sq / mult
                if q0 == 0:
                    first = lax.broadcasted_iota(jnp.int32, (ROW_SLICE, CB), 0) == 0
                    mult, slope = jnp.where(first, 1.0, mult), jnp.where(first, 0.0, slope)
                d_h = dh_o[rows, :]
                d_la = d_h * _rows_back(h_prev, q0, ROW_SLICE, 1) * a + d_h * ia * xl * slope
                d_za = d_la * (LRU_C * ls) * ra * (1.0 - ra)
                d_zx = d_h * mult * xl * ia * (1.0 - ia)
                d_ls = d_ls + _fold_rows(d_la * ra)
                d_ba = d_ba + _fold_rows(d_za)
                d_bx = d_bx + _fold_rows(d_zx)
                dz_a[local, :] = d_za.astype(BF16)
                dz_x[local, :] = d_zx.astype(BF16)
                back[rows, :] = d_h * mult * ia
            rows = pl.ds(r0, rc)
            xb = kept_ref[0, rows, :]
            acc_a[...] += _dot_tn(xb, dz_a[...])
            acc_x[...] += _dot_tn(xb, dz_x[...])
            back[rows, :] += _dot_nt(dz_a[...], wa_m) + _dot_nt(dz_x[...], wx_m)
        taps = [jnp.zeros((SUBLANES, CB), F32)] * 4
        d_bc = jnp.zeros((SUBLANES, CB), F32)
        for q0 in range(0, t, ROW_SLICE):
            rows = pl.ds(q0, ROW_SLICE)
            d_ref[0, rows, :] = _conv_anticausal(back, wc, q0, ROW_SLICE, 4).astype(BF16)
            g = back[rows, :]
            taps = [acc + _fold_rows(g * _rows_back(pad, q0, ROW_SLICE, 3 - k)) for k, acc in enumerate(taps)]
            d_bc = d_bc + _fold_rows(g)
        d_lam = d_ls * LRU_C * jax.nn.sigmoid(-lam_ref[...])
        small_ref[...] = jnp.concatenate(
            [jnp.sum(v, axis=0, keepdims=True) for v in taps + [d_bc, d_ba, d_bx, d_lam]], axis=0)
        dwa_ref[...] = acc_a[...].reshape(N_DEV, HEAD_DIM // N_DEV, HEAD_DIM).astype(BF16)
        dwx_ref[...] = acc_x[...].reshape(N_DEV, HEAD_DIM // N_DEV, HEAD_DIM).astype(BF16)

    blk = pl.BlockSpec((t, CB), lambda h: (0, h))
    gate_grad = jax.ShapeDtypeStruct((N_DEV, N_HEADS, HEAD_DIM // N_DEV, HEAD_DIM), BF16)
    return pl.pallas_call(
        body, name="lru_bwd", grid=(N_HEADS,),
        out_shape=[jax.ShapeDtypeStruct((2, t, D_MODEL), BF16), gate_grad, gate_grad,
                   jax.ShapeDtypeStruct((LRU_SMALL_ROWS, D_MODEL), F32)],
        in_specs=[_section(3, t), _section(4, t), blk, blk, pl.BlockSpec((3, t, CB), lambda h: (0, 0, h)), blk,
                  pl.BlockSpec((4, CB), lambda h: (0, h)), mat, mat, vec],
        out_specs=[pl.BlockSpec((2, t, CB), lambda h: (0, 0, h)), mat, mat,
                   pl.BlockSpec((LRU_SMALL_ROWS, CB), lambda h: (0, h))],
        scratch_shapes=[pltpu.VMEM((t + PAD, CB), F32), pltpu.VMEM((t + PAD, CB), F32), pltpu.VMEM((t, CB), F32),
                        pltpu.VMEM((t, CB), F32), pltpu.VMEM((t + PAD, CB), F32), pltpu.VMEM((t + PAD, CB), F32),
                        pltpu.VMEM((HEAD_DIM, HEAD_DIM), F32), pltpu.VMEM((HEAD_DIM, HEAD_DIM), F32),
                        pltpu.VMEM((rc, CB), BF16), pltpu.VMEM((rc, CB), BF16)],
        compiler_params=_params("parallel"),
    )(proj, proj, hl, a_all, kept, d_yb, w_conv, wa, wx, lam)


def _stack_maps(halves):
    def conv(sec, part):
        return jnp.minimum(sec, 2), jnp.where(sec < 3, part, halves - 1)

    def lru(sec, part):
        return jnp.clip(sec - 3, 0, 1), jnp.where(sec < 3, 0, jnp.where(sec < 5, part, halves - 1))

    def gate(sec, part):
        return jnp.clip(sec - 5, 0, 1), jnp.where(sec < 5, 0, part)

    return conv, lru, gate


def _pick_stack(sec, refs, fn):
    @pl.when(sec < 3)
    def _():
        fn(refs[0])

    @pl.when((sec >= 3) & (sec < 5))
    def _():
        fn(refs[1])

    @pl.when(sec >= 5)
    def _():
        fn(refs[2])


def _in_proj_wgrad(h, d_conv, d_lru, d_gate):
    t = h.shape[0]
    halves, bn = 1, D_MODEL
    maps = _stack_maps(halves)

    def body(h_ref, dc_ref, dl_ref, dg_ref, o_ref):
        def emit(ref):
            o_ref[...] = _dot_tn(h_ref[...], ref[...]).astype(BF16)
        _pick_stack(pl.program_id(0) // halves, (dc_ref, dl_ref, dg_ref), emit)

    def spec(m):
        def index(s):
            stack, part = m(s // halves, s % halves)
            return stack, 0, part
        return pl.BlockSpec((None, t, bn), index)

    return pl.pallas_call(
        body, name="in_proj_wgrad", grid=(7 * halves,), out_shape=jax.ShapeDtypeStruct((D_MODEL, IN_COLS), BF16),
        in_specs=[pl.BlockSpec((t, D_MODEL), lambda s: (0, 0))] + [spec(m) for m in maps],
        out_specs=pl.BlockSpec((D_MODEL, bn), lambda s: (0, s)),
        compiler_params=_params("arbitrary"),
    )(h, d_conv, d_lru, d_gate)


def _in_proj_xgrad(d_conv, d_lru, d_gate, w_in, x, dx1, g1):
    t = x.shape[0]
    tm = min(1024, t)
    maps = _stack_maps(1)

    def body(dc_ref, dl_ref, dg_ref, w_ref, x_ref, dx1_ref, g_ref, dx_ref, dgain_ref, acc):
        i, s = pl.program_id(0), pl.program_id(1)

        @pl.when((i == 0) & (s == 0))
        def _():
            dgain_ref[...] = jnp.zeros_like(dgain_ref)

        @pl.when(s == 0)
        def _():
            acc[...] = jnp.zeros_like(acc)

        def add(ref):
            acc[...] += _dot_nt(ref[...], w_ref[...])
        _pick_stack(s, (dc_ref, dl_ref, dg_ref), add)

        @pl.when(s == 6)
        def _():
            n1, r1 = _rms_fwd(x_ref[...])
            d_h = acc[...]
            dgain_ref[...] += jnp.sum(d_h * n1, axis=0, keepdims=True)
            dx_ref[...] = dx1_ref[...] + _rms_bwd(n1, r1, d_h * g_ref[...])

    def spec(m):
        def index(i, s):
            return m(s, 0)[0], i, 0
        return pl.BlockSpec((None, tm, D_MODEL), index)

    row = pl.BlockSpec((tm, D_MODEL), lambda i, s: (i, 0))
    vec = pl.BlockSpec((1, D_MODEL), lambda i, s: (0, 0))
    return pl.pallas_call(
        body, name="in_proj_xgrad", grid=(t // tm, 7),
        out_shape=[jax.ShapeDtypeStruct((t, D_MODEL), F32), jax.ShapeDtypeStruct((1, D_MODEL), F32)],
        in_specs=[spec(m) for m in maps] + [pl.BlockSpec((D_MODEL, D_MODEL), lambda i, s: (0, s)), row, row, vec],
        out_specs=[row, vec],
        scratch_shapes=[pltpu.VMEM((tm, D_MODEL), F32)],
        compiler_params=_params("arbitrary", "arbitrary"),
    )(d_conv, d_lru, d_gate, w_in, x, dx1, g1)


def _adamw(w, g, m, v):
    m = ADAM_B1 * m + (1.0 - ADAM_B1) * g
    v = ADAM_B2 * v + (1.0 - ADAM_B2) * (g * g)
    m_hat = m / (1.0 - ADAM_B1 ** ADAM_STEP)
    v_hat = v / (1.0 - ADAM_B2 ** ADAM_STEP)
    return -ADAM_LR * (m_hat / (jnp.sqrt(v_hat) + ADAM_EPS) + ADAM_WD * w), m, v


def _adam_large(ws, ms, vs, owns, others, name):
    n = len(ws)
    shape = ws[0].shape
    cols = shape[-1]
    flat = [[a.reshape(-1, cols) for a in group] for group in (ws, ms, vs)]
    rows = flat[0][0].shape[0]
    owns, others = [o.reshape(4, rows, cols) for o in owns], [o.reshape(3, rows, cols) for o in others]
    rb = _row_block(rows, 512)

    def body(*refs):
        ins, outs = refs[:5 * n], refs[5 * n:]
        for i in range(n):
            w_ref, m_ref, v_ref, own_ref, oth_ref = ins[i::n]
            g = own_ref[...].astype(F32)
            for k in range(3):
                g = g + oth_ref[k].astype(F32)
            outs[i][...] = g
            outs[n + i][...], outs[2 * n + i][...], outs[3 * n + i][...] = _adamw(w_ref[...], g, m_ref[...], v_ref[...])

    blk = pl.BlockSpec((rb, cols), lambda i: (i, 0))
    res = jax.ShapeDtypeStruct((rows, cols), F32)
    outs = pl.pallas_call(
        body, name=name, grid=(rows // rb,), out_shape=[res] * (4 * n),
        in_specs=[blk] * (3 * n) + [pl.BlockSpec((None, rb, cols), lambda i: (0, i, 0))] * n
        + [pl.BlockSpec((3, rb, cols), lambda i: (0, i, 0))] * n,
        out_specs=[blk] * (4 * n), compiler_params=_params("parallel"),
    )(*flat[0], *flat[1], *flat[2], *owns, *others)
    outs = [o.reshape(shape) for o in outs]
    return outs[:n], outs[n:2 * n], outs[2 * n:3 * n], outs[3 * n:]


def _adam_small(ws, gs, ms, vs):
    n = len(ws)

    def body(*refs):
        w_refs, g_refs, m_refs, v_refs = (refs[i * n:(i + 1) * n] for i in range(4))
        outs = refs[4 * n:]
        for i in range(n):
            d, m, v = _adamw(w_refs[i][...], g_refs[i][...], m_refs[i][...], v_refs[i][...])
            outs[i][...], outs[n + i][...], outs[2 * n + i][...] = d, m, v

    shapes = [jax.ShapeDtypeStruct(w.shape, F32) for w in ws]
    outs = pl.pallas_call(
        body, name="adam_small", out_shape=shapes * 3,
        in_specs=[VMEM_SPEC] * (4 * n), out_specs=[VMEM_SPEC] * (3 * n), compiler_params=_params(),
    )(*ws, *gs, *ms, *vs)
    return outs[:n], outs[n:2 * n], outs[2 * n:]


def _pack_rows(pieces):
    tile = SUBLANES * LANES
    return jnp.concatenate([jnp.pad(p.reshape(-1), (0, (-p.size) % tile)).reshape(-1, LANES) for p in pieces], axis=0)


def _packed_starts(sizes):
    tile = SUBLANES * LANES
    starts = [0]
    for s in sizes:
        starts.append(starts[-1] + (s + tile - 1) // tile * SUBLANES)
    return starts


def kernel(x, norm_mix_pre, norm_mix_post, norm_ffn_pre, norm_ffn_post, w_in, conv_short_w, w_conv_branch, lru_conv_w, lru_conv_b, lru_wa, lru_ba, lru_wx, lru_bx, lru_lambda, w_lru_branch, w_out, ffn_w_up, ffn_conv_w, ffn_conv_b, ffn_w_down, loss_target, m_norm_mix_pre, m_norm_mix_post, m_norm_ffn_pre, m_norm_ffn_post, m_w_in, m_conv_short_w, m_w_conv_branch, m_lru_conv_w, m_lru_conv_b, m_lru_wa, m_lru_ba, m_lru_wx, m_lru_bx, m_lru_lambda, m_w_lru_branch, m_w_out, m_ffn_w_up, m_ffn_conv_w, m_ffn_conv_b, m_ffn_w_down, v_norm_mix_pre, v_norm_mix_post, v_norm_ffn_pre, v_norm_ffn_post, v_w_in, v_conv_short_w, v_w_conv_branch, v_lru_conv_w, v_lru_conv_b, v_lru_wa, v_lru_ba, v_lru_wx, v_lru_bx, v_lru_lambda, v_w_lru_branch, v_w_out, v_ffn_w_up, v_ffn_conv_w, v_ffn_conv_b, v_ffn_w_down):
    t = x.shape[1]
    xi, yi, ci = _position()
    me = _block_of(xi, yi, ci)
    x2, target = x[0], loss_target[0]
    shard_in, shard_up = IN_COLS // N_DEV, 2 * D_FF // N_DEV
    shard_sq, shard_down, shard_head = D_MODEL // N_DEV, D_FF // N_DEV, HEAD_DIM // N_DEV

    names = ["w_in", "lru_wa", "lru_wx", "w_conv_branch", "w_lru_branch", "w_out", "ffn_w_up", "ffn_w_down"]
    large = [w_in[0], lru_wa[0], lru_wx[0], w_conv_branch[0], w_lru_branch[0], w_out[0], ffn_w_up[0], ffn_w_down[0]]
    blocks = [_cols(shard_in), _lead, _lead, _rows(shard_sq), _rows(shard_sq), _rows(shard_sq),
              _cols(shard_up), _rows(shard_down)]
    gate_full = (N_DEV, N_HEADS, shard_head, HEAD_DIM)
    full_shapes = [(D_MODEL, IN_COLS), gate_full, gate_full, (D_MODEL, D_MODEL), (D_MODEL, D_MODEL), (D_MODEL, D_MODEL),
                   (D_MODEL, 2 * D_FF), (D_FF, D_MODEL)]
    n_now = 3
    small_sharded = [conv_short_w, lru_conv_w, lru_ba, lru_bx, ffn_conv_w]
    small_mine = _pack_rows(small_sharded)
    small_at = _packed_starts([p.size for p in small_sharded])
    *gathered, small_all, proj, h = _gather_weights(large, blocks, full_shapes, small_mine, n_now, x2, norm_mix_pre)
    g_in, g_wa, g_wx = gathered[:n_now]
    later_blocks = blocks[n_now:]
    send1, recv1, later, gather_token = _gather_start(gathered[n_now:], later_blocks, "gather_start")

    def behind(token, operand):
        return operand + token[0:1, 0:1]

    def forward(lo, hi, after, tag):
        return _gather_forward(later[lo:hi], later_blocks[lo:hi], send1[4 * lo:4 * hi], recv1[4 * lo:4 * hi], after,
                               "gather_forward_" + tag)

    def finish(lo, hi, flight, after, tag):
        return _gather_finish(flight[2], later_blocks[lo:hi], flight[0], flight[1], after, "gather_finish_" + tag)

    def cols_of(r0, n, width):
        part = small_all[:, r0:r0 + n * width // LANES, :].reshape(N_DEV, n, width)
        return part.transpose(1, 0, 2).reshape(n, N_DEV * width)

    c_short = cols_of(small_at[0], 3, LANES)
    c_lru = cols_of(small_at[1], 4, LANES)
    b_a = cols_of(small_at[2], N_HEADS, shard_head).reshape(1, D_MODEL)
    b_x = cols_of(small_at[3], N_HEADS, shard_head).reshape(1, D_MODEL)
    c_ffn = cols_of(small_at[4], 3, shard_up)

    y_a = _conv_mixer_fwd(proj, behind(gather_token, c_short))
    y_b, hl, decay, lru_kept = _lru_fwd(proj, behind(gather_token, c_lru), lru_conv_b, g_wa, b_a, g_wx, b_x, lru_lambda)
    flight_mix_w = forward(0, 3, y_b, "mix")
    g_cb, g_lb, g_out = finish(0, 3, flight_mix_w, y_b, "mix")
    pa, pb, merged, mix, x1, h2 = _merge(y_a, y_b, proj, x2, g_cb, g_lb, g_out, norm_mix_post, norm_ffn_pre)
    flight_up_w = forward(3, 4, h2, "up")
    (g_up,) = finish(3, 4, flight_up_w, h2, "up")
    up, act, f = _ffn_up(h2, g_up, c_ffn, ffn_conv_b)
    flight_down_w = forward(4, 5, f, "down")
    (g_down,) = finish(4, 5, flight_down_w, f, "down")
    dy, d_out, d_act, dg4, loss_part = _ffn_down(f, act, g_down, x1, target, norm_ffn_post)

    block_of = dict(zip(names, blocks))
    shard_shapes = {"w_in": (D_MODEL, shard_in), "w_conv_branch": (shard_sq, D_MODEL), "w_lru_branch": (shard_sq, D_MODEL),
                    "w_out": (shard_sq, D_MODEL), "lru_wa": (N_HEADS, shard_head, HEAD_DIM),
                    "lru_wx": (N_HEADS, shard_head, HEAD_DIM), "ffn_w_up": (D_MODEL, shard_up),
                    "ffn_w_down": (shard_down, D_MODEL)}

    def reduce_start(tag, grads):
        keys = list(grads)
        sums = _reduce_pair([grads[k] for k in keys], [block_of[k] for k in keys], [shard_shapes[k] for k in keys],
                            "reduce_pair_" + tag)
        return (keys,) + _exchange_chips_start(sums, "reduce_chip_start_" + tag)

    gw_down = _grad_tn(f, d_out, min(512, D_FF), "ffn_down_wgrad")
    flight_down = reduce_start("down", {"ffn_w_down": gw_down})
    gw_up, gc_ffn, gb_ffn, d_h2 = _ffn_up_bwd(up, d_act, behind(flight_down[-1], c_ffn), h2, g_up)
    flight_up = reduce_start("up", {"ffn_w_up": gw_up})
    dx1, d_mix, d_pa, d_pb, d_ya, d_yb, d_gate, dg3, dg2 = _merge_bwd(
        dy, d_h2, x1, mix, behind(flight_up[-1], norm_ffn_pre), norm_mix_post, g_out, g_cb, g_lb, pa, pb, proj)
    gw_out = _grad_tn(merged, d_mix, CB, "w_out_wgrad")
    gw_cb = _grad_tn(y_a, d_pa, CB, "w_conv_branch_wgrad")
    gw_lb = _grad_tn(y_b, d_pb, CB, "w_lru_branch_wgrad")
    flight_mix = reduce_start("mix", {"w_conv_branch": gw_cb, "w_lru_branch": gw_lb, "w_out": gw_out})
    d_conv, gc_short = _conv_mixer_bwd(proj, d_ya, behind(flight_mix[-1], c_short))
    d_lru, gw_a, gw_x, g_lru_small = _lru_bwd(proj, hl, decay, lru_kept, d_yb, c_lru, g_wa, g_wx, lru_lambda)
    early = [dg2, dg3, dg4, g_lru_small[4:5], g_lru_small[7:8], gb_ffn, gc_short, g_lru_small[0:4],
             g_lru_small[5:6], g_lru_small[6:7], gc_ffn, loss_part]
    flight_small = _small_start(_pack_rows(early), "small_start")
    gw_in = _in_proj_wgrad(h, d_conv, d_lru, d_gate)
    flight_in = reduce_start("in", {"lru_wa": gw_a, "lru_wx": gw_x, "w_in": gw_in})
    dx, dg1 = _in_proj_xgrad(d_conv, d_lru, d_gate, g_in, x2, dx1,
                             behind(flight_small[-1], behind(flight_in[-1], norm_mix_pre)))
    flight_late = _small_start(_pack_rows([dg1]), "small_start_late")

    moments ={"w_in": (m_w_in, v_w_in), "w_conv_branch": (m_w_conv_branch, v_w_conv_branch),
               "w_lru_branch": (m_w_lru_branch, v_w_lru_branch), "w_out": (m_w_out, v_w_out),
               "lru_wa": (m_lru_wa, v_lru_wa), "lru_wx": (m_lru_wx, v_lru_wx), "ffn_w_up": (m_ffn_w_up, v_ffn_w_up),
               "ffn_w_down": (m_ffn_w_down, v_ffn_w_down)}
    weights = {"w_in": w_in, "w_conv_branch": w_conv_branch, "w_lru_branch": w_lru_branch, "w_out": w_out,
               "lru_wa": lru_wa, "lru_wx": lru_wx, "ffn_w_up": ffn_w_up, "ffn_w_down": ffn_w_down}
    out_g, out_d, out_m, out_v = {}, {}, {}, {}

    after = flight_late[-1]
    for tag, (keys, send, recv, sums, lands, _) in (("down", flight_down), ("up", flight_up), ("mix", flight_mix),
                                                    ("in", flight_in)):
        sums, others = _exchange_chips_wait(send, recv, sums, lands, after, "reduce_chip_wait_" + tag)
        by_key = dict(zip(keys, zip(sums, others)))
        for shape in dict.fromkeys(shard_shapes[k] for k in keys):
            same = [k for k in keys if shard_shapes[k] == shape]
            results = _adam_large([weights[k] for k in same], [moments[k][0] for k in same], [moments[k][1] for k in same],
                                  [by_key[k][0] for k in same], [by_key[k][1] for k in same], "adam_" + same[0])
            for out, values in zip((out_g, out_d, out_m, out_v), results):
                out.update(zip(same, values))
        after = out_d[keys[-1]]

    total, total_late = _small_sum([_small_wait(*flight_small[:4], after, "small_wait"),
                                    _small_wait(*flight_late[:4], after, "small_wait_late")], me)
    sizes = [p.size for p in early]
    starts = _packed_starts(sizes)

    def piece(i, shape):
        if i == 0:
            return total_late.reshape(-1)[:D_MODEL].reshape(shape)
        return total[starts[i - 1]:starts[i]].reshape(-1)[:sizes[i - 1]].reshape(shape)

    loss = total[starts[11], 0]

    def col_shard(full, width):
        return lax.dynamic_slice_in_dim(full, me * width, width, axis=1)

    def head_shard(full):
        return lax.dynamic_slice_in_dim(full.reshape(N_HEADS, HEAD_DIM), me * shard_head, shard_head, axis=1)

    small_names = ["norm_mix_pre", "norm_mix_post", "norm_ffn_pre", "norm_ffn_post", "lru_conv_b", "lru_lambda",
                   "ffn_conv_b", "conv_short_w", "lru_conv_w", "lru_ba", "lru_bx", "ffn_conv_w"]
    small_g = [piece(0, (1, D_MODEL)), piece(1, (1, D_MODEL)), piece(2, (1, D_MODEL)), piece(3, (1, D_MODEL)),
               piece(4, (1, D_MODEL)), piece(5, (1, D_MODEL)), piece(6, (1, 2 * D_FF)),
               col_shard(piece(7, (3, D_MODEL)), LANES), col_shard(piece(8, (4, D_MODEL)), LANES),
               head_shard(piece(9, (1, D_MODEL))), head_shard(piece(10, (1, D_MODEL))),
               col_shard(piece(11, (3, 2 * D_FF)), shard_up)]
    small_w = [norm_mix_pre, norm_mix_post, norm_ffn_pre, norm_ffn_post, lru_conv_b, lru_lambda, ffn_conv_b,
               conv_short_w[0], lru_conv_w[0], lru_ba[0], lru_bx[0], ffn_conv_w[0]]
    small_m = [m_norm_mix_pre, m_norm_mix_post, m_norm_ffn_pre, m_norm_ffn_post, m_lru_conv_b, m_lru_lambda,
               m_ffn_conv_b, m_conv_short_w[0], m_lru_conv_w[0], m_lru_ba[0], m_lru_bx[0], m_ffn_conv_w[0]]
    small_v = [v_norm_mix_pre, v_norm_mix_post, v_norm_ffn_pre, v_norm_ffn_post, v_lru_conv_b, v_lru_lambda,
               v_ffn_conv_b, v_conv_short_w[0], v_lru_conv_w[0], v_lru_ba[0], v_lru_bx[0], v_ffn_conv_w[0]]
    s_d, s_m, s_v = _adam_small(small_w, small_g, small_m, small_v)
    for i, name in enumerate(small_names):
        shape = small_w[i].shape if i < 7 else (1,) + small_w[i].shape
        out_g[name] = small_g[i].reshape(shape)
        out_d[name], out_m[name], out_v[name] = s_d[i].reshape(shape), s_m[i].reshape(shape), s_v[i].reshape(shape)

    order = ["norm_mix_pre", "norm_mix_post", "norm_ffn_pre", "norm_ffn_post", "w_in", "conv_short_w", "w_conv_branch",
             "lru_conv_w", "lru_conv_b", "lru_wa", "lru_ba", "lru_wx", "lru_bx", "lru_lambda", "w_lru_branch", "w_out",
             "ffn_w_up", "ffn_conv_w", "ffn_conv_b", "ffn_w_down"]
    return (loss, dx.reshape(1, t, D_MODEL), *[out_g[k] for k in order], *[out_d[k] for k in order],
            *[out_m[k] for k in order], *[out_v[k] for k in order])
```

```python
import functools
import math

import jax
import jax.numpy as jnp
from jax import lax
from jax.experimental import pallas as pl
from jax.experimental.pallas import tpu as pltpu

F32 = jnp.float32
BF16 = jnp.bfloat16
MESH = pl.DeviceIdType.MESH

N_DEV = 8
D_MODEL = 1024
N_HEADS = 4
HEAD_DIM = D_MODEL // N_HEADS
D_FF = 3 * D_MODEL
IN_COLS = 7 * D_MODEL
LRU_C = 8.0
RMS_EPS = 1e-6
ADAM_LR = 0.001
ADAM_B1 = 0.9
ADAM_B2 = 0.999
ADAM_EPS = 1e-08
ADAM_WD = 0.01
ADAM_STEP = 10
GELU_K = math.sqrt(2.0 / math.pi)
GELU_C = 0.044715

LANES = 128
SUBLANES = 8
PAD = SUBLANES
VMEM_LIMIT = 56 * 1024 * 1024
CB = 256
ROW_SLICE = 32
SCAN_UNROLL = 8

HBM_SPEC = pl.BlockSpec(memory_space=pltpu.HBM)
SEM_SPEC = pl.BlockSpec(memory_space=pltpu.SEMAPHORE)
DATAFLOW_EFFECT = pltpu.SideEffectType.DATAFLOW_SIDE_EFFECTING
VMEM_SPEC = pl.BlockSpec(memory_space=pltpu.VMEM)


def _params(*sem):
    if sem:
        return pltpu.CompilerParams(dimension_semantics=sem, vmem_limit_bytes=VMEM_LIMIT)
    return pltpu.CompilerParams(vmem_limit_bytes=VMEM_LIMIT)


def _row_chunk(t):
    return min(256, t)


def _row_block(rows, cap):
    return next(rb for rb in range(min(cap, rows), 0, -16) if rows % rb == 0)


def _gelu(x):
    return 0.5 * x * (1.0 + jnp.tanh(GELU_K * (x + GELU_C * x * x * x)))


def _gelu_and_grad(x):
    t = jnp.tanh(GELU_K * (x + GELU_C * x * x * x))
    g = 0.5 * x * (1.0 + t)
    dg = 0.5 * (1.0 + t) + 0.5 * x * (1.0 - t * t) * GELU_K * (1.0 + 3.0 * GELU_C * x * x)
    return g, dg


def _expm1_neg(x):
    series = x * (1.0 + x * (0.5 + x * (1.0 / 6.0 + x * (1.0 / 24.0 + x * (1.0 / 120.0)))))
    return jnp.where(x > -0.05, series, jnp.exp(x) - 1.0)


def _log_sigmoid(x):
    return jnp.minimum(x, 0.0) - jnp.log1p(jnp.exp(-jnp.abs(x)))


def _dot(a, b):
    return jnp.dot(a, b, preferred_element_type=F32)


def _dot_nt(a, b):
    return lax.dot_general(a, b, (((1,), (1,)), ((), ())), preferred_element_type=F32)


def _dot_tn(a, b):
    return lax.dot_general(a, b, (((0,), (0,)), ((), ())), preferred_element_type=F32)


def _rms_fwd(x):
    r = lax.rsqrt(jnp.mean(x * x, axis=-1, keepdims=True) + RMS_EPS)
    return x * r, r


def _rms_bwd(n, r, gdy):
    return r * (gdy - n * jnp.mean(n * gdy, axis=-1, keepdims=True))


def _rows_back(pad_ref, r0, rows, j):
    cur = pad_ref[pl.ds(PAD + r0, rows), :]
    if j == 0:
        return cur
    before = pad_ref[pl.ds(PAD + r0 - SUBLANES, SUBLANES), :]
    row = lax.broadcasted_iota(jnp.int32, before.shape, 0)
    rolled = pltpu.roll(cur, j, 0)
    top = jnp.where(row < j, pltpu.roll(before, j, 0), rolled[0:SUBLANES, :])
    return jnp.concatenate([top, rolled[SUBLANES:, :]], axis=0)


def _rows_ahead(pad_ref, r0, rows, j):
    cur = pad_ref[pl.ds(r0, rows), :]
    if j == 0:
        return cur
    after = pad_ref[pl.ds(r0 + rows, SUBLANES), :]
    row = lax.broadcasted_iota(jnp.int32, after.shape, 0)
    rolled = pltpu.roll(cur, rows - j, 0)
    bottom = jnp.where(row >= SUBLANES - j, pltpu.roll(after, SUBLANES - j, 0), rolled[rows - SUBLANES:, :])
    return jnp.concatenate([rolled[:rows - SUBLANES, :], bottom], axis=0)


def _fold_rows(v):
    return v.reshape(v.shape[0] // SUBLANES, SUBLANES, v.shape[1]).sum(axis=0)


def _conv_causal(pad_ref, w, r0, rows, taps):
    acc = None
    for k in range(taps):
        term = w[k:k + 1, :] * _rows_back(pad_ref, r0, rows, taps - 1 - k)
        acc = term if acc is None else acc + term
    return acc


def _conv_anticausal(pad_ref, w, r0, rows, taps):
    acc = None
    for k in range(taps):
        term = w[k:k + 1, :] * _rows_ahead(pad_ref, r0, rows, taps - 1 - k)
        acc = term if acc is None else acc + term
    return acc


def _conv_wgrad(g, xpad_ref, r0, rows, taps):
    return [jnp.sum(g * _rows_back(xpad_ref, r0, rows, taps - 1 - k), axis=0, keepdims=True) for k in range(taps)]


def _position():
    return lax.axis_index("x"), lax.axis_index("y"), lax.axis_index("c")


def _block_of(x, y, c):
    return 4 * x + 2 * y + c


def _chip(x, y, k):
    return (x + (k & 1)) % 2, (y + (k >> 1)) % 2


def _cols(width):
    def at(ref, d, half=None):
        cols = pl.ds(pl.multiple_of(d * width, LANES), width)
        if half is None:
            return ref.at[:, cols]
        return ref.at[pl.ds(half * (ref.shape[0] // 2), ref.shape[0] // 2), cols]
    return at


def _rows(height):
    def at(ref, d, half=None):
        if half is None:
            return ref.at[pl.ds(pl.multiple_of(d * height, 16), height), :]
        return ref.at[pl.ds(pl.multiple_of(d * height + half * (height // 2), 16), height // 2), :]
    return at


def _lead(ref, d, half=None):
    if half is None:
        return ref.at[d]
    return ref.at[d, pl.ds(half * (ref.shape[1] // 2), ref.shape[1] // 2)]


def _gather_weights(shards, blocks, full_shapes, small, n_now, tokens, gain):
    n = len(shards)
    small_rows = small.shape[0]
    t = tokens.shape[0]
    rc = min(512, t)

    def body(*refs):
        ins, small_in, x_ref, g_ref = refs[:n], refs[n], refs[n + 1], refs[n + 2]
        outs, small_out, proj_ref, h_ref = refs[n + 3:2 * n + 3], refs[2 * n + 3], refs[2 * n + 4], refs[2 * n + 5]
        stage = refs[2 * n + 6:3 * n + 6]
        w_buf, p_buf, send, recv, local, w_sem, p_sem = refs[3 * n + 6:]
        x, y, c = _position()
        me = _block_of(x, y, c)
        sibling = (x, y, 1 - c)

        for a in range(n):
            stage[a][...] = ins[a][...].astype(BF16)
        for r0 in range(0, t, rc):
            normed, _ = _rms_fwd(x_ref[pl.ds(r0, rc), :])
            h_ref[pl.ds(r0, rc), :] = (normed * g_ref[...]).astype(BF16)
        stores = []

        def project(w_ref, block):
            i = len(stores)
            if i >= 2:
                stores[i - 2].wait()
            for r0 in range(0, t, rc):
                p_buf[i % 2, pl.ds(r0, rc), :] = _dot(h_ref[pl.ds(r0, rc), :], w_ref[...]).astype(BF16)
            st = pltpu.make_async_copy(p_buf.at[i % 2], blocks[0](proj_ref, block), p_sem.at[i % 2])
            st.start()
            stores.append(st)

        def project_landed(block):
            ld = pltpu.make_async_copy(blocks[0](outs[0], block), w_buf, w_sem)
            ld.start()
            ld.wait()
            project(w_buf, block)

        def copy(a, k, block, to, src=None, half=None):
            dst = blocks[a](outs[a], block, half)
            return pltpu.make_async_remote_copy(
                src_ref=dst if src is None else src, dst_ref=dst, send_sem=send.at[a, k], recv_sem=recv.at[a, k],
                device_id=to, device_id_type=MESH)

        def small_copy(k):
            px, py, pc = (x + (k & 1)) % 2, (y + ((k >> 1) & 1)) % 2, (c + (k >> 2)) % 2
            return pltpu.make_async_remote_copy(
                src_ref=small_in, dst_ref=small_out.at[me], send_sem=send.at[n_now, k - 1], recv_sem=recv.at[n_now, k - 1],
                device_id=(px, py, pc), device_id_type=MESH)

        def small_arrival(k):
            px, py, pc = (x + (k & 1)) % 2, (y + ((k >> 1) & 1)) % 2, (c + (k >> 2)) % 2
            return pltpu.make_async_remote_copy(
                src_ref=small_in, dst_ref=small_out.at[_block_of(px, py, pc)], send_sem=send.at[n_now, k - 1],
                recv_sem=recv.at[n_now, k - 1], device_id=(px, py, pc), device_id_type=MESH)

        small_out[me] = small_in[...]
        small_sends = [small_copy(k) for k in range(1, N_DEV)]
        for cp in small_sends:
            cp.start()

        mine, first, passed = [], [], []
        for a in range(n):
            own = pltpu.make_async_copy(stage[a], blocks[a](outs[a], me), local.at[a])
            own.start()
            mine.append(own)
            if a >= n_now:
                continue
            sends = [copy(a, 0, me, sibling, src=stage[a])]
            sends += [copy(a, k, me, (*_chip(x, y, k), c), src=stage[a]) for k in (1, 2)]
            for cp in sends:
                cp.start()
            first += sends

        here = (x, y, c)
        across = [(*_chip(x, y, k), c) for k in (1, 2)]
        near = [[_block_of(*_chip(x, y, k), cc) for k in (1, 2)] for cc in (c, 1 - c)]
        far = [_block_of(*_chip(x, y, 3), cc) for cc in (c, 1 - c)]

        def launch(cp):
            cp.start()
            passed.append(cp)

        project(stage[0], me)
        copy(0, 0, _block_of(x, y, 1 - c), here).wait_recv()
        project_landed(_block_of(x, y, 1 - c))
        for a in range(n_now):
            for i in (0, 1):
                copy(a, 1 + i, near[0][i], here).wait_recv()
                launch(copy(a, 3 + i, near[0][i], across[1 - i], half=i))
                launch(copy(a, 5 + i, near[0][i], sibling))
            if a == 0:
                project_landed(near[0][0])
                project_landed(near[0][1])
        for i in (0, 1):
            copy(0, 5 + i, near[1][i], here).wait_recv()
            project_landed(near[1][i])
        for a in range(n_now):
            for i in (0, 1):
                copy(a, 3 + i, far[0], here, half=i).wait_recv()
                launch(copy(a, 7 + i, far[0], sibling, half=i))
            if a == 0:
                project_landed(far[0])
        for a in range(n_now):
            if a > 0:
                copy(a, 0, _block_of(x, y, 1 - c), here).wait_recv()
                for i in (0, 1):
                    copy(a, 5 + i, near[1][i], here).wait_recv()
            for i in (0, 1):
                copy(a, 7 + i, far[1], here, half=i).wait_recv()
            if a == 0:
                project_landed(far[1])
        for k in range(1, N_DEV):
            small_arrival(k).wait_recv()
        for cp in first + passed + small_sends:
            cp.wait_send()
        for done in mine + stores[-2:]:
            done.wait()

    out_shape = [jax.ShapeDtypeStruct(s, BF16) for s in full_shapes]
    out_shape += [jax.ShapeDtypeStruct((N_DEV, small_rows, LANES), F32), jax.ShapeDtypeStruct((t, full_shapes[0][1]), BF16),
                  jax.ShapeDtypeStruct(tokens.shape, BF16)]
    return pl.pallas_call(
        body, name="gather_weights", out_shape=out_shape,
        in_specs=[VMEM_SPEC] * (n + 3), out_specs=[HBM_SPEC] * n + [VMEM_SPEC, HBM_SPEC, VMEM_SPEC],
        scratch_shapes=[pltpu.VMEM(s.shape, BF16) for s in shards]
        + [pltpu.VMEM(shards[0].shape, BF16), pltpu.VMEM((2, t, shards[0].shape[1]), BF16),
           pltpu.SemaphoreType.DMA((n_now + 1, 9)), pltpu.SemaphoreType.DMA((n_now + 1, 9)),
           pltpu.SemaphoreType.DMA((n,)), pltpu.SemaphoreType.DMA(()), pltpu.SemaphoreType.DMA((2,))],
        compiler_params=_params(),
    )(*shards, small, tokens, gain)


def _gather_first(full, blocks, send, recv):
    x, y, c = _position()
    me = _block_of(x, y, c)
    peers = [(x, y, 1 - c)] + [(*_chip(x, y, k), c) for k in (1, 2, 3)]

    def copy(a, k, block):
        at = blocks[a](full[a], block)
        return pltpu.make_async_remote_copy(src_ref=at, dst_ref=at, send_sem=send[4 * a + k], recv_sem=recv[4 * a + k],
                                            device_id=peers[k], device_id_type=MESH)

    sends = [copy(a, k, me) for a in range(len(full)) for k in range(4)]
    arrivals = [copy(a, k, _block_of(*peers[k])) for a in range(len(full)) for k in range(4)]
    return sends, arrivals


def _gather_second(full, blocks, send, recv):
    x, y, c = _position()

    def copy(a, k, cc):
        at = blocks[a](full[a], _block_of(*_chip(x, y, k), cc))
        return pltpu.make_async_remote_copy(src_ref=at, dst_ref=at, send_sem=send[3 * a + k - 1],
                                            recv_sem=recv[3 * a + k - 1], device_id=(x, y, 1 - c), device_id_type=MESH)

    sends = [copy(a, k, c) for a in range(len(full)) for k in (1, 2, 3)]
    arrivals = [copy(a, k, 1 - c) for a in range(len(full)) for k in (1, 2, 3)]
    return sends, arrivals


def _split_call(body, name, arrays, sems_in, n_sems_out, after=None, token=False):
    n, m = len(arrays), len(sems_in)

    def kernel_body(*refs):
        outs = refs[n + m + (after is not None):]
        body(refs[:n], refs[n:n + m], outs[:n_sems_out])
        if token:
            outs[-1][...] = jnp.zeros_like(outs[-1])

    extra_in = [] if after is None else [after]
    outs = pl.pallas_call(
        kernel_body, name=name,
        out_shape=(*[pltpu.SemaphoreType.DMA(())] * n_sems_out, *[pltpu.HBM(a.shape, a.dtype) for a in arrays],
                   *([jax.ShapeDtypeStruct((SUBLANES, LANES), F32)] if token else [])),
        in_specs=[HBM_SPEC] * n + [SEM_SPEC] * m + [pl.BlockSpec(memory_space=pl.ANY)] * len(extra_in),
        out_specs=(*[SEM_SPEC] * n_sems_out, *[HBM_SPEC] * n, *([VMEM_SPEC] if token else [])),
        input_output_aliases={i: n_sems_out + i for i in range(n)},
        compiler_params=pltpu.CompilerParams(has_side_effects=DATAFLOW_EFFECT),
    )(*[pltpu.with_memory_space_constraint(a, pltpu.HBM) for a in arrays], *sems_in, *extra_in)
    sems, rest = list(outs[:n_sems_out]), list(outs[n_sems_out:])
    return (sems, rest[:n], rest[n]) if token else (sems, rest[:n])


def _gather_start(full, blocks, name):
    n = len(full)

    def body(arrays, _, sems):
        for cp in _gather_first(arrays, blocks, sems[:4 * n], sems[4 * n:])[0]:
            cp.start()

    sems, arrays, token = _split_call(body, name, full, [], 8 * n, token=True)
    return sems[:4 * n], sems[4 * n:], arrays, token


def _gather_forward(full, blocks, send_first, recv_first, after, name):
    n = len(full)

    def body(arrays, sems_in, sems):
        sends, arrivals = _gather_first(arrays, blocks, sems_in[:4 * n], sems_in[4 * n:])
        for cp in arrivals:
            cp.wait_recv()
        for cp in _gather_second(arrays, blocks, sems[:3 * n], sems[3 * n:])[0]:
            cp.start()
        for cp in sends:
            cp.wait_send()

    sems, arrays = _split_call(body, name, full, [*send_first, *recv_first], 6 * n, after=after)
    return sems[:3 * n], sems[3 * n:], arrays


def _gather_finish(full, blocks, send_second, recv_second, after, name):
    n = len(full)

    def body(arrays, sems_in, _):
        sends, arrivals = _gather_second(arrays, blocks, sems_in[:3 * n], sems_in[3 * n:])
        for cp in sends:
            cp.wait_send()
        for cp in arrivals:
            cp.wait_recv()

    return _split_call(body, name, full, [*send_second, *recv_second], 0, after=after)[1]


def _reduce_pair(grads, blocks, shard_shapes, name):
    n = len(grads)

    def body(*refs):
        ins, outs = refs[:n], refs[n:2 * n]
        got, own = refs[2 * n:3 * n], refs[3 * n:4 * n]
        send, recv, local = refs[4 * n:]
        x, y, c = _position()
        copies, loads = [], []
        for a in range(n):
            for k in range(4):
                chip = _chip(x, y, k)
                cp = pltpu.make_async_remote_copy(
                    src_ref=blocks[a](ins[a], _block_of(*chip, 1 - c)), dst_ref=got[a].at[k],
                    send_sem=send.at[a, k], recv_sem=recv.at[a, k], device_id=(x, y, 1 - c), device_id_type=MESH)
                cp.start()
                copies.append(cp)
                ld = pltpu.make_async_copy(blocks[a](ins[a], _block_of(*chip, c)), own[a].at[k], local.at[a, k])
                ld.start()
                loads.append(ld)
        for a in range(n):
            for k in range(4):
                loads[4 * a + k].wait()
                copies[4 * a + k].wait_recv()
                outs[a][k] = (own[a][k].astype(F32) + got[a][k].astype(F32)).astype(BF16)
        for cp in copies:
            cp.wait_send()

    slots = [(4,) + tuple(s) for s in shard_shapes]
    return pl.pallas_call(
        body, name=name, out_shape=[jax.ShapeDtypeStruct(s, BF16) for s in slots],
        in_specs=[HBM_SPEC] * n, out_specs=[VMEM_SPEC] * n,
        scratch_shapes=[pltpu.VMEM(s, BF16) for s in slots] * 2
        + [pltpu.SemaphoreType.DMA((n, 4)), pltpu.SemaphoreType.DMA((n, 4)), pltpu.SemaphoreType.DMA((n, 4))],
        compiler_params=_params(),
    )(*grads)


def _chip_copies(sums, lands, send, recv):
    x, y, c = _position()
    return [pltpu.make_async_remote_copy(
        src_ref=sums[a].at[k], dst_ref=lands[a].at[k - 1], send_sem=send[3 * a + k - 1], recv_sem=recv[3 * a + k - 1],
        device_id=(*_chip(x, y, k), c), device_id_type=MESH) for a in range(len(sums)) for k in (1, 2, 3)]


def _exchange_chips_start(pair_sums, name):
    n = len(pair_sums)
    lands = [pltpu.with_memory_space_constraint(lax.empty((3,) + tuple(p.shape[1:]), BF16), pltpu.HBM) for p in pair_sums]

    def body(*refs):
        sums, zones = refs[:n], refs[n:2 * n]
        send, recv = refs[2 * n:5 * n], refs[5 * n:8 * n]
        token = refs[-1]
        for cp in _chip_copies(sums, zones, send, recv):
            cp.start()
        token[...] = jnp.zeros_like(token)

    outs = pl.pallas_call(
        body, name=name,
        out_shape=(*[pltpu.SemaphoreType.DMA(())] * (6 * n),
                   *[pltpu.HBM(p.shape, BF16) for p in pair_sums], *[pltpu.HBM(z.shape, BF16) for z in lands],
                   jax.ShapeDtypeStruct((SUBLANES, LANES), F32)),
        in_specs=[HBM_SPEC] * (2 * n), out_specs=(*[SEM_SPEC] * (6 * n), *[HBM_SPEC] * (2 * n), VMEM_SPEC),
        input_output_aliases={i: 6 * n + i for i in range(2 * n)},
        compiler_params=pltpu.CompilerParams(has_side_effects=DATAFLOW_EFFECT),
    )(*[pltpu.with_memory_space_constraint(p, pltpu.HBM) for p in pair_sums], *lands)
    return outs[:3 * n], outs[3 * n:6 * n], outs[6 * n:7 * n], outs[7 * n:8 * n], outs[-1]


def _exchange_chips_wait(send, recv, sums, lands, after, name):
    n = len(sums)

    def body(*refs):
        sums_in, zones = refs[:n], refs[n:2 * n]
        send_in, recv_in = refs[2 * n:5 * n], refs[5 * n:8 * n]
        for cp in _chip_copies(sums_in, zones, send_in, recv_in):
            cp.wait_send()
            cp.wait_recv()

    outs = pl.pallas_call(
        body, name=name,
        out_shape=(*[pltpu.HBM(p.shape, BF16) for p in sums], *[pltpu.HBM(z.shape, BF16) for z in lands]),
        in_specs=[HBM_SPEC] * (2 * n) + [SEM_SPEC] * (6 * n) + [pl.BlockSpec(memory_space=pl.ANY)],
        out_specs=[HBM_SPEC] * (2 * n), input_output_aliases={i: i for i in range(2 * n)},
        compiler_params=pltpu.CompilerParams(has_side_effects=DATAFLOW_EFFECT),
    )(*sums, *lands, *send, *recv, after)
    return outs[:n], outs[n:]


def _small_copies(mine, land, send, recv):
    x, y, c = _position()
    me = _block_of(x, y, c)

    def peer(k):
        return (x + (k & 1)) % 2, (y + ((k >> 1) & 1)) % 2, (c + (k >> 2)) % 2

    def copy(k, slot):
        return pltpu.make_async_remote_copy(src_ref=mine, dst_ref=land.at[slot], send_sem=send[k - 1], recv_sem=recv[k - 1],
                                            device_id=peer(k), device_id_type=MESH)

    return [copy(k, me) for k in range(1, N_DEV)], [copy(k, _block_of(*peer(k))) for k in range(1, N_DEV)]


def _small_start(part, name):
    land = jnp.zeros((N_DEV,) + part.shape, F32)

    def body(arrays, _, sems):
        for cp in _small_copies(arrays[0], arrays[1], sems[:7], sems[7:])[0]:
            cp.start()

    sems, arrays, token = _split_call(body, name, [part, land], [], 14, token=True)
    return sems[:7], sems[7:], arrays[0], arrays[1], token


def _small_wait(send, recv, part, land, after, name):
    def body(arrays, sems_in, _):
        sends, arrivals = _small_copies(arrays[0], arrays[1], sems_in[:7], sems_in[7:])
        for cp in sends:
            cp.wait_send()
        for cp in arrivals:
            cp.wait_recv()

    return _split_call(body, name, [part, land], [*send, *recv], 0, after=after)[1]


def _small_sum(pairs, me):
    n = len(pairs)

    def body(me_ref, *refs):
        for i in range(n):
            mine, land, out = refs[2 * i], refs[2 * i + 1], refs[2 * n + i]
            total = jnp.zeros(mine.shape, F32)
            for d in range(N_DEV):
                total = total + land[d] + jnp.where(me_ref[0] == d, mine[...], 0.0)
            out[...] = total

    flat = [a for pair in pairs for a in pair]
    return pl.pallas_call(
        body, name="small_sum", out_shape=[jax.ShapeDtypeStruct(mine.shape, F32) for mine, _ in pairs],
        in_specs=[pl.BlockSpec(memory_space=pltpu.SMEM)] + [VMEM_SPEC] * (2 * n), out_specs=[VMEM_SPEC] * n,
        compiler_params=_params(),
    )(me.reshape(1).astype(jnp.int32), *flat)


def _section(s, t):
    return pl.BlockSpec((t, CB), lambda h, s=s: (0, s * (D_MODEL // CB) + h))


def _conv_mixer_fwd(proj, w_short):
    t = proj.shape[0]
    rc = _row_chunk(t)

    def body(b_ref, c_ref, x_ref, w_ref, y_ref, pad):
        pad[pl.ds(0, PAD), :] = jnp.zeros((PAD, CB), F32)
        for r0 in range(0, t, rc):
            rows = pl.ds(r0, rc)
            pad[pl.ds(PAD + r0, rc), :] = c_ref[rows, :].astype(F32) * x_ref[rows, :].astype(F32)
        w = w_ref[...]
        for r0 in range(0, t, rc):
            rows = pl.ds(r0, rc)
            y_ref[rows, :] = (b_ref[rows, :].astype(F32) * _conv_causal(pad, w, r0, rc, 3)).astype(BF16)

    return pl.pallas_call(
        body, name="conv_mixer_fwd", grid=(D_MODEL // CB,),
        out_shape=jax.ShapeDtypeStruct((t, D_MODEL), BF16),
        in_specs=[_section(0, t), _section(1, t), _section(2, t), pl.BlockSpec((3, CB), lambda h: (0, h))],
        out_specs=pl.BlockSpec((t, CB), lambda h: (0, h)),
        scratch_shapes=[pltpu.VMEM((t + PAD, CB), F32)],
        compiler_params=_params("parallel"),
    )(proj, proj, proj, w_short)


def _lru_gates(xl, wa, ba, wx, bx, ls, first_row):
    xb = xl.astype(BF16)
    ra = jax.nn.sigmoid(_dot(xb, wa) + ba)
    ia = jax.nn.sigmoid(_dot(xb, wx) + bx)
    la = LRU_C * ra * ls
    a = jnp.exp(la)
    one_minus = -_expm1_neg(2.0 * la)
    mult = jnp.where(first_row, 1.0, jnp.sqrt(one_minus))
    return xb, ra, ia, a, one_minus, mult


def _head_specs():
    vec = pl.BlockSpec((1, CB), lambda h: (0, h))
    mat = pl.BlockSpec((N_DEV, None, HEAD_DIM // N_DEV, HEAD_DIM), lambda h: (0, h, 0, 0))
    return vec, mat


def _lru_fwd(proj, w_conv, b_conv, wa, ba, wx, bx, lam):
    t = proj.shape[0]
    rc = _row_chunk(t)
    vec, mat = _head_specs()

    def body(lx_ref, ly_ref, wc_ref, bc_ref, wa_ref, ba_ref, wx_ref, bx_ref, lam_ref, yb_ref, hl_ref, a_ref, kept_ref,
             pad, u_s):
        pad[pl.ds(0, PAD), :] = jnp.zeros((PAD, CB), F32)
        for r0 in range(0, t, rc):
            pad[pl.ds(PAD + r0, rc), :] = lx_ref[pl.ds(r0, rc), :].astype(F32)
        wc, bc = wc_ref[...], bc_ref[...]
        wa_m, wx_m = wa_ref[...].reshape(HEAD_DIM, HEAD_DIM), wx_ref[...].reshape(HEAD_DIM, HEAD_DIM)
        ls = _log_sigmoid(lam_ref[...])
        for r0 in range(0, t, rc):
            rows = pl.ds(r0, rc)
            xl = _conv_causal(pad, wc, r0, rc, 4) + bc
            first = (lax.broadcasted_iota(jnp.int32, (rc, CB), 0) + r0) == 0
            xb, ra, ia, a, _, mult = _lru_gates(xl, wa_m, ba_ref[...], wx_m, bx_ref[...], ls, first)
            a_ref[rows, :] = a
            u_s[rows, :] = mult * (ia * xl)
            kept_ref[0, rows, :] = xb
            kept_ref[1, rows, :] = ra.astype(BF16)
            kept_ref[2, rows, :] = ia.astype(BF16)

        row = lax.broadcasted_iota(jnp.int32, (SUBLANES, CB), 0)

        def group(g, carry):
            r = pl.multiple_of(g * SUBLANES, SUBLANES)
            a_g, b_g = a_ref[pl.ds(r, SUBLANES), :], u_s[pl.ds(r, SUBLANES), :]
            for s in (1, 2, 4):
                keep = row >= s
                b_g = jnp.where(keep, a_g * pltpu.roll(b_g, s, 0) + b_g, b_g)
                a_g = jnp.where(keep, a_g * pltpu.roll(a_g, s, 0), a_g)
            h_g = b_g + a_g * carry
            hl_ref[pl.ds(r, SUBLANES), :] = h_g
            return jnp.broadcast_to(h_g[SUBLANES - 1:SUBLANES, :], (SUBLANES, CB))

        def trip(i, carry):
            for j in range(SCAN_UNROLL):
                carry = group(i * SCAN_UNROLL + j, carry)
            return carry

        lax.fori_loop(0, t // SUBLANES // SCAN_UNROLL, trip, jnp.zeros((SUBLANES, CB), F32))
        for r0 in range(0, t, rc):
            rows = pl.ds(r0, rc)
            yb_ref[rows, :] = (hl_ref[rows, :] * _gelu(ly_ref[rows, :].astype(F32))).astype(BF16)

    blk = pl.BlockSpec((t, CB), lambda h: (0, h))
    res = jax.ShapeDtypeStruct((t, D_MODEL), F32)
    return pl.pallas_call(
        body, name="lru_fwd", grid=(N_HEADS,),
        out_shape=[jax.ShapeDtypeStruct((t, D_MODEL), BF16), res, res, jax.ShapeDtypeStruct((3, t, D_MODEL), BF16)],
        in_specs=[_section(3, t), _section(4, t), pl.BlockSpec((4, CB), lambda h: (0, h)), vec, mat, vec, mat, vec, vec],
        out_specs=[blk, blk, blk, pl.BlockSpec((3, t, CB), lambda h: (0, 0, h))],
        scratch_shapes=[pltpu.VMEM((t + PAD, CB), F32), pltpu.VMEM((t, CB), F32)],
        compiler_params=_params("parallel"),
    )(proj, proj, w_conv, b_conv, wa, ba, wx, bx, lam)


def _merge(y_a, y_b, proj, x, w_cb, w_lb, w_out, g2, g3):
    t = x.shape[0]
    tm = min(512, t)

    def body(ya_ref, yb_ref, gc_ref, gl_ref, x_ref, wcb_ref, wlb_ref, wo_ref, g2_ref, g3_ref,
             pa_ref, pb_ref, mg_ref, mix_ref, x1_ref, h2_ref):
        pa = _dot(ya_ref[...], wcb_ref[...]).astype(BF16)
        pb = _dot(yb_ref[...], wlb_ref[...]).astype(BF16)
        pa_ref[...] = pa
        pb_ref[...] = pb
        merged = (jax.nn.sigmoid(gc_ref[...].astype(F32)) * pa.astype(F32)
                  + jax.nn.sigmoid(gl_ref[...].astype(F32)) * pb.astype(F32)).astype(BF16)
        mg_ref[...] = merged
        mix = _dot(merged, wo_ref[...])
        mix_ref[...] = mix
        n2, _ = _rms_fwd(mix)
        x1 = x_ref[...] + n2 * g2_ref[...]
        x1_ref[...] = x1
        n3, _ = _rms_fwd(x1)
        h2_ref[...] = (n3 * g3_ref[...]).astype(BF16)

    row = pl.BlockSpec((tm, D_MODEL), lambda i: (i, 0))
    full = pl.BlockSpec((D_MODEL, D_MODEL), lambda i: (0, 0))
    vec = pl.BlockSpec((1, D_MODEL), lambda i: (0, 0))
    act = jax.ShapeDtypeStruct((t, D_MODEL), BF16)
    res = jax.ShapeDtypeStruct((t, D_MODEL), F32)
    return pl.pallas_call(
        body, name="merge_fwd", grid=(t // tm,), out_shape=[act, act, act, res, res, act],
        in_specs=[row, row, pl.BlockSpec((tm, D_MODEL), lambda i: (i, 5)), pl.BlockSpec((tm, D_MODEL), lambda i: (i, 6)),
                  row, full, full, full, vec, vec],
        out_specs=[row] * 6,
        compiler_params=_params("parallel"),
    )(y_a, y_b, proj, proj, x, w_cb, w_lb, w_out, g2, g3)


N_FF_BLOCKS = D_FF // CB
FFN_BWD_COLS = 512


def _ffn_up(h2, w_up, w_conv, b_conv):
    t = h2.shape[0]
    rc = _row_chunk(t)
    nb = N_FF_BLOCKS

    def body(h_ref, w_ref, c_ref, b_ref, up_ref, act_ref, f_ref, pad, gate):
        k = pl.program_id(1)
        pad[pl.ds(0, PAD), :] = jnp.zeros((PAD, CB), F32)
        for r0 in range(0, t, rc):
            rows = pl.ds(r0, rc)
            up = _dot(h_ref[rows, :], w_ref[...]).astype(BF16)
            up_ref[rows, :] = up
            pad[pl.ds(PAD + r0, rc), :] = up.astype(F32)
        def conv(keep_gate):
            cw = c_ref[...]
            for r0 in range(0, t, rc):
                rows = pl.ds(r0, rc)
                act = _conv_causal(pad, cw, r0, rc, 3) + b_ref[...]
                act_ref[rows, :] = act.astype(BF16)
                if keep_gate:
                    gate[rows, :] = act
                else:
                    f_ref[rows, :] = (_gelu(gate[rows, :]) * act).astype(BF16)

        @pl.when(k == 0)
        def _():
            conv(True)

        @pl.when(k == 1)
        def _():
            conv(False)

    half = lambda rows: pl.BlockSpec((rows, CB), lambda j, k: (0, nb * k + j))
    wide = jax.ShapeDtypeStruct((t, 2 * D_FF), BF16)
    return pl.pallas_call(
        body, name="ffn_up_fwd", grid=(nb, 2), out_shape=[wide, wide, jax.ShapeDtypeStruct((t, D_FF), BF16)],
        in_specs=[pl.BlockSpec((t, D_MODEL), lambda j, k: (0, 0)), half(D_MODEL), half(3), half(1)],
        out_specs=[half(t), half(t), pl.BlockSpec((t, CB), lambda j, k: (0, j))],
        scratch_shapes=[pltpu.VMEM((t + PAD, CB), F32), pltpu.VMEM((t, CB), F32)],
        compiler_params=_params("parallel", "arbitrary"),
    )(h2, w_up, w_conv, b_conv)


def _ffn_down(f, act, w_down, x1, target, g4):
    t = f.shape[0]
    tm = min(256, t)
    cc = 512

    def body(f_ref, act_ref, w_ref, x1_ref, tg_ref, g_ref, dy_ref, dout_ref, back_ref, dg_ref, loss_ref):
        @pl.when(pl.program_id(0) == 0)
        def _():
            dg_ref[...] = jnp.zeros_like(dg_ref)
            loss_ref[...] = jnp.zeros_like(loss_ref)
        out = _dot(f_ref[...], w_ref[...])
        n4, r4 = _rms_fwd(out)
        err = x1_ref[...] + n4 * g_ref[...] - tg_ref[...]
        loss_ref[...] += jnp.full(loss_ref.shape, 0.5 / D_MODEL, F32) * jnp.sum(err * err)
        dy = err * (1.0 / D_MODEL)
        dy_ref[...] = dy
        dg_ref[...] += jnp.sum(dy * n4, axis=0, keepdims=True)
        d_out = _rms_bwd(n4, r4, dy * g_ref[...]).astype(BF16)
        dout_ref[...] = d_out
        for c0 in range(0, D_FF, cc):
            d_f = _dot_nt(d_out, w_ref[pl.ds(c0, cc), :])
            gelu, d_gelu = _gelu_and_grad(act_ref[:, pl.ds(c0, cc)].astype(F32))
            val = act_ref[:, pl.ds(D_FF + c0, cc)].astype(F32)
            back_ref[:, pl.ds(c0, cc)] = (d_f * val * d_gelu).astype(BF16)
            back_ref[:, pl.ds(D_FF + c0, cc)] = (d_f * gelu).astype(BF16)

    row = pl.BlockSpec((tm, D_MODEL), lambda i: (i, 0))
    wide = pl.BlockSpec((tm, 2 * D_FF), lambda i: (i, 0))
    vec = pl.BlockSpec((1, D_MODEL), lambda i: (0, 0))
    return pl.pallas_call(
        body, name="ffn_down_fwd_bwd", grid=(t // tm,),
        out_shape=[jax.ShapeDtypeStruct((t, D_MODEL), F32), jax.ShapeDtypeStruct((t, D_MODEL), BF16),
                   jax.ShapeDtypeStruct((t, 2 * D_FF), BF16), jax.ShapeDtypeStruct((1, D_MODEL), F32),
                   jax.ShapeDtypeStruct((SUBLANES, LANES), F32)],
        in_specs=[pl.BlockSpec((tm, D_FF), lambda i: (i, 0)), wide, pl.BlockSpec((D_FF, D_MODEL), lambda i: (0, 0)),
                  row, row, vec],
        out_specs=[row, row, wide, vec, pl.BlockSpec((SUBLANES, LANES), lambda i: (0, 0))],
        compiler_params=_params("arbitrary"),
    )(f, act, w_down, x1, target, g4)


def _grad_tn(pairs, bm, name):
    k = len(pairs)
    t, m = pairs[0][0].shape
    n = pairs[0][1].shape[1]

    def body(*refs):
        for i in range(k):
            refs[2 * k + i][...] = _dot_tn(refs[2 * i][...], refs[2 * i + 1][...]).astype(BF16)

    return pl.pallas_call(
        body, name=name, grid=(m // bm,), out_shape=[jax.ShapeDtypeStruct((m, n), BF16)] * k,
        in_specs=[pl.BlockSpec((t, bm), lambda i: (0, i)), pl.BlockSpec((t, n), lambda i: (0, 0))] * k,
        out_specs=[pl.BlockSpec((bm, n), lambda i: (i, 0))] * k,
        compiler_params=_params("parallel"),
    )(*[x for pair in pairs for x in pair])


def _ffn_up_bwd(up, back, w_conv, h2, w_up):
    t = h2.shape[0]
    rc = _row_chunk(t)
    cb = FFN_BWD_COLS

    def body(up_ref, back_ref, c_ref, h_ref, w_ref, dw_ref, dcw_ref, dcb_ref, dh_ref, pad, after, d_up):
        @pl.when(pl.program_id(0) == 0)
        def _():
            dh_ref[...] = jnp.zeros_like(dh_ref)
        pad[pl.ds(0, PAD), :] = jnp.zeros((PAD, cb), F32)
        after[pl.ds(t, PAD), :] = jnp.zeros((PAD, cb), F32)
        for r0 in range(0, t, rc):
            pad[pl.ds(PAD + r0, rc), :] = up_ref[pl.ds(r0, rc), :].astype(F32)
            after[pl.ds(r0, rc), :] = back_ref[pl.ds(r0, rc), :].astype(F32)
        cw = c_ref[...]
        taps = [jnp.zeros((SUBLANES, cb), F32)] * 3
        bias = jnp.zeros((SUBLANES, cb), F32)
        for r0 in range(0, t, rc):
            for q0 in range(r0, r0 + rc, ROW_SLICE):
                rows = pl.ds(q0, ROW_SLICE)
                d_up[rows, :] = _conv_anticausal(after, cw, q0, ROW_SLICE, 3).astype(BF16)
                g = after[rows, :]
                taps = [acc + _fold_rows(g * _rows_back(pad, q0, ROW_SLICE, 2 - k)) for k, acc in enumerate(taps)]
                bias = bias + _fold_rows(g)
            rows = pl.ds(r0, rc)
            dh_ref[rows, :] += _dot_nt(d_up[rows, :], w_ref[...])
        dw_ref[...] = _dot_tn(h_ref[...], d_up[...]).astype(BF16)
        dcw_ref[...] = jnp.concatenate([jnp.sum(acc, axis=0, keepdims=True) for acc in taps], axis=0)
        dcb_ref[...] = jnp.sum(bias, axis=0, keepdims=True)

    cols = lambda rows: pl.BlockSpec((rows, cb), lambda j: (0, j))
    whole = pl.BlockSpec((t, D_MODEL), lambda j: (0, 0))
    return pl.pallas_call(
        body, name="ffn_up_bwd", grid=(2 * D_FF // cb,),
        out_shape=[jax.ShapeDtypeStruct((D_MODEL, 2 * D_FF), BF16), jax.ShapeDtypeStruct((3, 2 * D_FF), F32),
                   jax.ShapeDtypeStruct((1, 2 * D_FF), F32), jax.ShapeDtypeStruct((t, D_MODEL), F32)],
        in_specs=[cols(t), cols(t), cols(3), whole, cols(D_MODEL)],
        out_specs=[cols(D_MODEL), cols(3), cols(1), whole],
        scratch_shapes=[pltpu.VMEM((t + PAD, cb), F32), pltpu.VMEM((t + PAD, cb), F32), pltpu.VMEM((t, cb), BF16)],
        compiler_params=_params("arbitrary"),
    )(up, back, w_conv, h2, w_up)


def _merge_bwd(dy, d_h2, x1, mix, g3, g2, w_out, w_cb, w_lb, pa, pb, proj):
    t = dy.shape[0]
    tm = min(256, t)

    def body(dy_ref, dh2_ref, x1_ref, mix_ref, g3_ref, g2_ref, wo_ref, wcb_ref, wlb_ref, pa_ref, pb_ref, gc_ref, gl_ref,
             dx1_ref, dmix_ref, dpa_ref, dpb_ref, dya_ref, dyb_ref, dgate_ref, dg3_ref, dg2_ref):
        @pl.when(pl.program_id(0) == 0)
        def _():
            dg3_ref[...] = jnp.zeros_like(dg3_ref)
            dg2_ref[...] = jnp.zeros_like(dg2_ref)
        n3, r3 = _rms_fwd(x1_ref[...])
        d_h2 = dh2_ref[...]
        dg3_ref[...] += jnp.sum(d_h2 * n3, axis=0, keepdims=True)
        dx1 = dy_ref[...] + _rms_bwd(n3, r3, d_h2 * g3_ref[...])
        dx1_ref[...] = dx1
        n2, r2 = _rms_fwd(mix_ref[...])
        dg2_ref[...] += jnp.sum(dx1 * n2, axis=0, keepdims=True)
        d_mix = _rms_bwd(n2, r2, dx1 * g2_ref[...]).astype(BF16)
        dmix_ref[...] = d_mix
        d_merged = _dot_nt(d_mix, wo_ref[...])
        sc = jax.nn.sigmoid(gc_ref[...].astype(F32))
        sl = jax.nn.sigmoid(gl_ref[...].astype(F32))
        d_pa = (d_merged * sc).astype(BF16)
        d_pb = (d_merged * sl).astype(BF16)
        dpa_ref[...] = d_pa
        dpb_ref[...] = d_pb
        dgate_ref[0] = (d_merged * pa_ref[...].astype(F32) * sc * (1.0 - sc)).astype(BF16)
        dgate_ref[1] = (d_merged * pb_ref[...].astype(F32) * sl * (1.0 - sl)).astype(BF16)
        dya_ref[...] = _dot_nt(d_pa, wcb_ref[...]).astype(BF16)
        dyb_ref[...] = _dot_nt(d_pb, wlb_ref[...]).astype(BF16)

    row = pl.BlockSpec((tm, D_MODEL), lambda i: (i, 0))
    full = pl.BlockSpec((D_MODEL, D_MODEL), lambda i: (0, 0))
    vec = pl.BlockSpec((1, D_MODEL), lambda i: (0, 0))
    act = jax.ShapeDtypeStruct((t, D_MODEL), BF16)
    small = jax.ShapeDtypeStruct((1, D_MODEL), F32)
    return pl.pallas_call(
        body, name="merge_bwd", grid=(t // tm,),
        out_shape=[jax.ShapeDtypeStruct((t, D_MODEL), F32), act, act, act, act, act,
                   jax.ShapeDtypeStruct((2, t, D_MODEL), BF16), small, small],
        in_specs=[row, row, row, row, vec, vec, full, full, full, row, row,
                  pl.BlockSpec((tm, D_MODEL), lambda i: (i, 5)), pl.BlockSpec((tm, D_MODEL), lambda i: (i, 6))],
        out_specs=[row] * 6 + [pl.BlockSpec((2, tm, D_MODEL), lambda i: (0, i, 0)), vec, vec],
        compiler_params=_params("arbitrary"),
    )(dy, d_h2, x1, mix, g3, g2, w_out, w_cb, w_lb, pa, pb, proj, proj)


def _conv_mixer_bwd(proj, d_ya, w_short):
    t = proj.shape[0]
    rc = _row_chunk(t)

    def body(b_ref, c_ref, x_ref, dy_ref, w_ref, d_ref, dw_ref, pad, back):
        pad[pl.ds(0, PAD), :] = jnp.zeros((PAD, CB), F32)
        back[pl.ds(t, PAD), :] = jnp.zeros((PAD, CB), F32)
        for r0 in range(0, t, rc):
            rows = pl.ds(r0, rc)
            pad[pl.ds(PAD + r0, rc), :] = c_ref[rows, :].astype(F32) * x_ref[rows, :].astype(F32)
        w = w_ref[...]
        for r0 in range(0, t, rc):
            rows = pl.ds(r0, rc)
            d_y = dy_ref[rows, :].astype(F32)
            d_ref[0, rows, :] = (d_y * _conv_causal(pad, w, r0, rc, 3)).astype(BF16)
            back[rows, :] = d_y * b_ref[rows, :].astype(F32)
        taps = [jnp.zeros((1, CB), F32)] * 3
        for r0 in range(0, t, rc):
            rows = pl.ds(r0, rc)
            d_u = _conv_anticausal(back, w, r0, rc, 3)
            d_ref[1, rows, :] = (d_u * x_ref[rows, :].astype(F32)).astype(BF16)
            d_ref[2, rows, :] = (d_u * c_ref[rows, :].astype(F32)).astype(BF16)
            taps = [acc + new for acc, new in zip(taps, _conv_wgrad(back[rows, :], pad, r0, rc, 3))]
        dw_ref[...] = jnp.concatenate(taps, axis=0)

    blk = pl.BlockSpec((t, CB), lambda h: (0, h))
    return pl.pallas_call(
        body, name="conv_mixer_bwd", grid=(D_MODEL // CB,),
        out_shape=[jax.ShapeDtypeStruct((3, t, D_MODEL), BF16), jax.ShapeDtypeStruct((3, D_MODEL), F32)],
        in_specs=[_section(0, t), _section(1, t), _section(2, t), blk, pl.BlockSpec((3, CB), lambda h: (0, h))],
        out_specs=[pl.BlockSpec((3, t, CB), lambda h: (0, 0, h)), pl.BlockSpec((3, CB), lambda h: (0, h))],
        scratch_shapes=[pltpu.VMEM((t + PAD, CB), F32), pltpu.VMEM((t + PAD, CB), F32)],
        compiler_params=_params("parallel"),
    )(proj, proj, proj, d_ya, w_short)


LRU_SMALL_ROWS = 8


def _lru_bwd(proj, hl, a_all, kept, d_yb, w_conv, wa, wx, lam):
    t = proj.shape[0]
    rc = _row_chunk(t)
    vec, mat = _head_specs()

    def body(lx_ref, ly_ref, hl_ref, a_ref, kept_ref, dy_ref, wc_ref, wa_ref, wx_ref, lam_ref,
             d_ref, dwa_ref, dwx_ref, small_ref, pad, a_next, dh_s, dh_o, h_prev, back, acc_a, acc_x, dz_a, dz_x):
        zeros = jnp.zeros((PAD, CB), F32)
        pad[pl.ds(0, PAD), :] = zeros
        h_prev[pl.ds(0, PAD), :] = zeros
        a_next[pl.ds(t, PAD), :] = zeros
        back[pl.ds(t, PAD), :] = zeros
        for r0 in range(0, t, ROW_SLICE):
            rows = pl.ds(r0, ROW_SLICE)
            pad[pl.ds(PAD + r0, ROW_SLICE), :] = lx_ref[rows, :].astype(F32)
            h_prev[pl.ds(PAD + r0, ROW_SLICE), :] = hl_ref[rows, :]
            a_next[pl.ds(PAD - 1 + r0, ROW_SLICE), :] = a_ref[rows, :]
            act, d_act = _gelu_and_grad(ly_ref[rows, :].astype(F32))
            d_y = dy_ref[rows, :].astype(F32)
            dh_s[rows, :] = d_y * act
            d_ref[1, rows, :] = (d_y * hl_ref[rows, :] * d_act).astype(BF16)
        wc = wc_ref[...]
        wa_m, wx_m = wa_ref[...].reshape(HEAD_DIM, HEAD_DIM), wx_ref[...].reshape(HEAD_DIM, HEAD_DIM)
        ls = _log_sigmoid(lam_ref[...])

        row = lax.broadcasted_iota(jnp.int32, (SUBLANES, CB), 0)
        groups = t // SUBLANES

        def group(i, carry):
            r = pl.multiple_of((groups - 1 - i) * SUBLANES, SUBLANES)
            a_g, b_g = a_next[pl.ds(PAD + r, SUBLANES), :], dh_s[pl.ds(r, SUBLANES), :]
            for s in (1, 2, 4):
                keep = row < SUBLANES - s
                b_g = jnp.where(keep, a_g * pltpu.roll(b_g, SUBLANES - s, 0) + b_g, b_g)
                a_g = jnp.where(keep, a_g * pltpu.roll(a_g, SUBLANES - s, 0), a_g)
            d_g = b_g + a_g * carry
            dh_o[pl.ds(r, SUBLANES), :] = d_g
            return jnp.broadcast_to(d_g[0:1, :], (SUBLANES, CB))

        def trip(i, carry):
            for j in range(SCAN_UNROLL):
                carry = group(i * SCAN_UNROLL + j, carry)
            return carry

        lax.fori_loop(0, groups // SCAN_UNROLL, trip, jnp.zeros((SUBLANES, CB), F32))

        acc_a[...] = jnp.zeros_like(acc_a)
        acc_x[...] = jnp.zeros_like(acc_x)
        d_ba = d_bx = d_ls = jnp.zeros((SUBLANES, CB), F32)
        for r0 in range(0, t, rc):
            for q0 in range(r0, r0 + rc, ROW_SLICE):
                rows, local = pl.ds(q0, ROW_SLICE), pl.ds(q0 - r0, ROW_SLICE)
                a = a_ref[rows, :]
                xl, ra, ia = (kept_ref[i, rows, :].astype(F32) for i in range(3))
                a_sq = a * a
                mult = jnp.sqrt(1.0 - a_sq)
                slope = -a_sq / mult
                if q0 == 0:
                    first = lax.broadcasted_iota(jnp.int32, (ROW_SLICE, CB), 0) == 0
                    mult, slope = jnp.where(first, 1.0, mult), jnp.where(first, 0.0, slope)
                d_h = dh_o[rows, :]
                d_la = d_h * _rows_back(h_prev, q0, ROW_SLICE, 1) * a + d_h * ia * xl * slope
                d_za = d_la * (LRU_C * ls) * ra * (1.0 - ra)
                d_zx = d_h * mult * xl * ia * (1.0 - ia)
                d_ls = d_ls + _fold_rows(d_la * ra)
                d_ba = d_ba + _fold_rows(d_za)
                d_bx = d_bx + _fold_rows(d_zx)
                dz_a[local, :] = d_za.astype(BF16)
                dz_x[local, :] = d_zx.astype(BF16)
                back[rows, :] = d_h * mult * ia
            rows = pl.ds(r0, rc)
            xb = kept_ref[0, rows, :]
            acc_a[...] += _dot_tn(xb, dz_a[...])
            acc_x[...] += _dot_tn(xb, dz_x[...])
            back[rows, :] += _dot_nt(dz_a[...], wa_m) + _dot_nt(dz_x[...], wx_m)
        taps = [jnp.zeros((SUBLANES, CB), F32)] * 4
        d_bc = jnp.zeros((SUBLANES, CB), F32)
        for q0 in range(0, t, ROW_SLICE):
            rows = pl.ds(q0, ROW_SLICE)
            d_ref[0, rows, :] = _conv_anticausal(back, wc, q0, ROW_SLICE, 4).astype(BF16)
            g = back[rows, :]
            taps = [acc + _fold_rows(g * _rows_back(pad, q0, ROW_SLICE, 3 - k)) for k, acc in enumerate(taps)]
            d_bc = d_bc + _fold_rows(g)
        d_lam = d_ls * LRU_C * jax.nn.sigmoid(-lam_ref[...])
        small_ref[...] = jnp.concatenate(
            [jnp.sum(v, axis=0, keepdims=True) for v in taps + [d_bc, d_ba, d_bx, d_lam]], axis=0)
        dwa_ref[...] = acc_a[...].reshape(N_DEV, HEAD_DIM // N_DEV, HEAD_DIM).astype(BF16)
        dwx_ref[...] = acc_x[...].reshape(N_DEV, HEAD_DIM // N_DEV, HEAD_DIM).astype(BF16)

    blk = pl.BlockSpec((t, CB), lambda h: (0, h))
    gate_grad = jax.ShapeDtypeStruct((N_DEV, N_HEADS, HEAD_DIM // N_DEV, HEAD_DIM), BF16)
    return pl.pallas_call(
        body, name="lru_bwd", grid=(N_HEADS,),
        out_shape=[jax.ShapeDtypeStruct((2, t, D_MODEL), BF16), gate_grad, gate_grad,
                   jax.ShapeDtypeStruct((LRU_SMALL_ROWS, D_MODEL), F32)],
        in_specs=[_section(3, t), _section(4, t), blk, blk, pl.BlockSpec((3, t, CB), lambda h: (0, 0, h)), blk,
                  pl.BlockSpec((4, CB), lambda h: (0, h)), mat, mat, vec],
        out_specs=[pl.BlockSpec((2, t, CB), lambda h: (0, 0, h)), mat, mat,
                   pl.BlockSpec((LRU_SMALL_ROWS, CB), lambda h: (0, h))],
        scratch_shapes=[pltpu.VMEM((t + PAD, CB), F32), pltpu.VMEM((t + PAD, CB), F32), pltpu.VMEM((t, CB), F32),
                        pltpu.VMEM((t, CB), F32), pltpu.VMEM((t + PAD, CB), F32), pltpu.VMEM((t + PAD, CB), F32),
                        pltpu.VMEM((HEAD_DIM, HEAD_DIM), F32), pltpu.VMEM((HEAD_DIM, HEAD_DIM), F32),
                        pltpu.VMEM((rc, CB), BF16), pltpu.VMEM((rc, CB), BF16)],
        compiler_params=_params("parallel"),
    )(proj, proj, hl, a_all, kept, d_yb, w_conv, wa, wx, lam)


def _stack_maps(halves):
    def conv(sec, part):
        return jnp.minimum(sec, 2), jnp.where(sec < 3, part, halves - 1)

    def lru(sec, part):
        return jnp.clip(sec - 3, 0, 1), jnp.where(sec < 3, 0, jnp.where(sec < 5, part, halves - 1))

    def gate(sec, part):
        return jnp.clip(sec - 5, 0, 1), jnp.where(sec < 5, 0, part)

    return conv, lru, gate


def _pick_stack(sec, refs, fn):
    @pl.when(sec < 3)
    def _():
        fn(refs[0])

    @pl.when((sec >= 3) & (sec < 5))
    def _():
        fn(refs[1])

    @pl.when(sec >= 5)
    def _():
        fn(refs[2])


def _in_proj_wgrad(h, d_conv, d_lru, d_gate):
    t = h.shape[0]
    halves, bn = 1, D_MODEL
    maps = _stack_maps(halves)

    def body(h_ref, dc_ref, dl_ref, dg_ref, o_ref):
        def emit(ref):
            o_ref[...] = _dot_tn(h_ref[...], ref[...]).astype(BF16)
        _pick_stack(pl.program_id(0) // halves, (dc_ref, dl_ref, dg_ref), emit)

    def spec(m):
        def index(s):
            stack, part = m(s // halves, s % halves)
            return stack, 0, part
        return pl.BlockSpec((None, t, bn), index)

    return pl.pallas_call(
        body, name="in_proj_wgrad", grid=(7 * halves,), out_shape=jax.ShapeDtypeStruct((D_MODEL, IN_COLS), BF16),
        in_specs=[pl.BlockSpec((t, D_MODEL), lambda s: (0, 0))] + [spec(m) for m in maps],
        out_specs=pl.BlockSpec((D_MODEL, bn), lambda s: (0, s)),
        compiler_params=_params("arbitrary"),
    )(h, d_conv, d_lru, d_gate)


def _in_proj_xgrad(d_conv, d_lru, d_gate, w_in, x, dx1, g1):
    t = x.shape[0]
    tm = min(1024, t)
    maps = _stack_maps(1)

    def body(dc_ref, dl_ref, dg_ref, w_ref, x_ref, dx1_ref, g_ref, dx_ref, dgain_ref, acc):
        i, s = pl.program_id(0), pl.program_id(1)

        @pl.when((i == 0) & (s == 0))
        def _():
            dgain_ref[...] = jnp.zeros_like(dgain_ref)

        @pl.when(s == 0)
        def _():
            acc[...] = jnp.zeros_like(acc)

        def add(ref):
            acc[...] += _dot_nt(ref[...], w_ref[...])
        _pick_stack(s, (dc_ref, dl_ref, dg_ref), add)

        @pl.when(s == 6)
        def _():
            n1, r1 = _rms_fwd(x_ref[...])
            d_h = acc[...]
            dgain_ref[...] += jnp.sum(d_h * n1, axis=0, keepdims=True)
            dx_ref[...] = dx1_ref[...] + _rms_bwd(n1, r1, d_h * g_ref[...])

    def spec(m):
        def index(i, s):
            return m(s, 0)[0], i, 0
        return pl.BlockSpec((None, tm, D_MODEL), index)

    row = pl.BlockSpec((tm, D_MODEL), lambda i, s: (i, 0))
    vec = pl.BlockSpec((1, D_MODEL), lambda i, s: (0, 0))
    return pl.pallas_call(
        body, name="in_proj_xgrad", grid=(t // tm, 7),
        out_shape=[jax.ShapeDtypeStruct((t, D_MODEL), F32), jax.ShapeDtypeStruct((1, D_MODEL), F32)],
        in_specs=[spec(m) for m in maps] + [pl.BlockSpec((D_MODEL, D_MODEL), lambda i, s: (0, s)), row, row, vec],
        out_specs=[row, vec],
        scratch_shapes=[pltpu.VMEM((tm, D_MODEL), F32)],
        compiler_params=_params("arbitrary", "arbitrary"),
    )(d_conv, d_lru, d_gate, w_in, x, dx1, g1)


def _adamw(w, g, m, v):
    m = ADAM_B1 * m + (1.0 - ADAM_B1) * g
    v = ADAM_B2 * v + (1.0 - ADAM_B2) * (g * g)
    m_hat = m / (1.0 - ADAM_B1 ** ADAM_STEP)
    v_hat = v / (1.0 - ADAM_B2 ** ADAM_STEP)
    return -ADAM_LR * (m_hat / (jnp.sqrt(v_hat) + ADAM_EPS) + ADAM_WD * w), m, v


def _adam_large(ws, ms, vs, owns, others, name):
    n = len(ws)
    shape = ws[0].shape
    cols = shape[-1]
    flat = [[a.reshape(-1, cols) for a in group] for group in (ws, ms, vs)]
    rows = flat[0][0].shape[0]
    owns, others = [o.reshape(4, rows, cols) for o in owns], [o.reshape(3, rows, cols) for o in others]
    rb = _row_block(rows, 512)

    def body(*refs):
        ins, outs = refs[:5 * n], refs[5 * n:]
        for i in range(n):
            w_ref, m_ref, v_ref, own_ref, oth_ref = ins[i::n]
            g = own_ref[...].astype(F32)
            for k in range(3):
                g = g + oth_ref[k].astype(F32)
            outs[i][...] = g
            outs[n + i][...], outs[2 * n + i][...], outs[3 * n + i][...] = _adamw(w_ref[...], g, m_ref[...], v_ref[...])

    blk = pl.BlockSpec((rb, cols), lambda i: (i, 0))
    res = jax.ShapeDtypeStruct((rows, cols), F32)
    outs = pl.pallas_call(
        body, name=name, grid=(rows // rb,), out_shape=[res] * (4 * n),
        in_specs=[blk] * (3 * n) + [pl.BlockSpec((None, rb, cols), lambda i: (0, i, 0))] * n
        + [pl.BlockSpec((3, rb, cols), lambda i: (0, i, 0))] * n,
        out_specs=[blk] * (4 * n), compiler_params=_params("parallel"),
    )(*flat[0], *flat[1], *flat[2], *owns, *others)
    outs = [o.reshape(shape) for o in outs]
    return outs[:n], outs[n:2 * n], outs[2 * n:3 * n], outs[3 * n:]


def _adam_small(ws, gs, ms, vs):
    n = len(ws)

    def body(*refs):
        w_refs, g_refs, m_refs, v_refs = (refs[i * n:(i + 1) * n] for i in range(4))
        outs = refs[4 * n:]
        for i in range(n):
            d, m, v = _adamw(w_refs[i][...], g_refs[i][...], m_refs[i][...], v_refs[i][...])
            outs[i][...], outs[n + i][...], outs[2 * n + i][...] = d, m, v

    shapes = [jax.ShapeDtypeStruct(w.shape, F32) for w in ws]
    outs = pl.pallas_call(
        body, name="adam_small", out_shape=shapes * 3,
        in_specs=[VMEM_SPEC] * (4 * n), out_specs=[VMEM_SPEC] * (3 * n), compiler_params=_params(),
    )(*ws, *gs, *ms, *vs)
    return outs[:n], outs[n:2 * n], outs[2 * n:]


def _pack_rows(pieces):
    tile = SUBLANES * LANES
    return jnp.concatenate([jnp.pad(p.reshape(-1), (0, (-p.size) % tile)).reshape(-1, LANES) for p in pieces], axis=0)


def _packed_starts(sizes):
    tile = SUBLANES * LANES
    starts = [0]
    for s in sizes:
        starts.append(starts[-1] + (s + tile - 1) // tile * SUBLANES)
    return starts


def kernel(x, norm_mix_pre, norm_mix_post, norm_ffn_pre, norm_ffn_post, w_in, conv_short_w, w_conv_branch, lru_conv_w, lru_conv_b, lru_wa, lru_ba, lru_wx, lru_bx, lru_lambda, w_lru_branch, w_out, ffn_w_up, ffn_conv_w, ffn_conv_b, ffn_w_down, loss_target, m_norm_mix_pre, m_norm_mix_post, m_norm_ffn_pre, m_norm_ffn_post, m_w_in, m_conv_short_w, m_w_conv_branch, m_lru_conv_w, m_lru_conv_b, m_lru_wa, m_lru_ba, m_lru_wx, m_lru_bx, m_lru_lambda, m_w_lru_branch, m_w_out, m_ffn_w_up, m_ffn_conv_w, m_ffn_conv_b, m_ffn_w_down, v_norm_mix_pre, v_norm_mix_post, v_norm_ffn_pre, v_norm_ffn_post, v_w_in, v_conv_short_w, v_w_conv_branch, v_lru_conv_w, v_lru_conv_b, v_lru_wa, v_lru_ba, v_lru_wx, v_lru_bx, v_lru_lambda, v_w_lru_branch, v_w_out, v_ffn_w_up, v_ffn_conv_w, v_ffn_conv_b, v_ffn_w_down):
    t = x.shape[1]
    xi, yi, ci = _position()
    me = _block_of(xi, yi, ci)
    x2, target = x[0], loss_target[0]
    shard_in, shard_up = IN_COLS // N_DEV, 2 * D_FF // N_DEV
    shard_sq, shard_down, shard_head = D_MODEL // N_DEV, D_FF // N_DEV, HEAD_DIM // N_DEV

    names = ["w_in", "lru_wa", "lru_wx", "w_conv_branch", "w_lru_branch", "w_out", "ffn_w_up", "ffn_w_down"]
    large = [w_in[0], lru_wa[0], lru_wx[0], w_conv_branch[0], w_lru_branch[0], w_out[0], ffn_w_up[0], ffn_w_down[0]]
    blocks = [_cols(shard_in), _lead, _lead, _rows(shard_sq), _rows(shard_sq), _rows(shard_sq),
              _cols(shard_up), _rows(shard_down)]
    gate_full = (N_DEV, N_HEADS, shard_head, HEAD_DIM)
    full_shapes = [(D_MODEL, IN_COLS), gate_full, gate_full, (D_MODEL, D_MODEL), (D_MODEL, D_MODEL), (D_MODEL, D_MODEL),
                   (D_MODEL, 2 * D_FF), (D_FF, D_MODEL)]
    n_now = 3
    small_sharded = [conv_short_w, lru_conv_w, lru_ba, lru_bx, ffn_conv_w]
    small_mine = _pack_rows(small_sharded)
    small_at = _packed_starts([p.size for p in small_sharded])
    *gathered, small_all, proj, h = _gather_weights(large, blocks, full_shapes, small_mine, n_now, x2, norm_mix_pre)
    g_in, g_wa, g_wx = gathered[:n_now]
    later_blocks = blocks[n_now:]
    send1, recv1, later, gather_token = _gather_start(gathered[n_now:], later_blocks, "gather_start")

    def behind(token, operand):
        return operand + token[0:1, 0:1]

    def forward(lo, hi, after, tag):
        return _gather_forward(later[lo:hi], later_blocks[lo:hi], send1[4 * lo:4 * hi], recv1[4 * lo:4 * hi], after,
                               "gather_forward_" + tag)

    def finish(lo, hi, flight, after, tag):
        return _gather_finish(flight[2], later_blocks[lo:hi], flight[0], flight[1], after, "gather_finish_" + tag)

    def cols_of(r0, n, width):
        part = small_all[:, r0:r0 + n * width // LANES, :].reshape(N_DEV, n, width)
        return part.transpose(1, 0, 2).reshape(n, N_DEV * width)

    c_short = cols_of(small_at[0], 3, LANES)
    c_lru = cols_of(small_at[1], 4, LANES)
    b_a = cols_of(small_at[2], N_HEADS, shard_head).reshape(1, D_MODEL)
    b_x = cols_of(small_at[3], N_HEADS, shard_head).reshape(1, D_MODEL)
    c_ffn = cols_of(small_at[4], 3, shard_up)

    y_a = _conv_mixer_fwd(proj, behind(gather_token, c_short))
    y_b, hl, decay, lru_kept = _lru_fwd(proj, behind(gather_token, c_lru), lru_conv_b, g_wa, b_a, g_wx, b_x, lru_lambda)
    flight_mix_w = forward(0, 3, y_b, "mix")
    g_cb, g_lb, g_out = finish(0, 3, flight_mix_w, y_b, "mix")
    pa, pb, merged, mix, x1, h2 = _merge(y_a, y_b, proj, x2, g_cb, g_lb, g_out, norm_mix_post, norm_ffn_pre)
    flight_up_w = forward(3, 4, h2, "up")
    (g_up,) = finish(3, 4, flight_up_w, h2, "up")
    up, act, f = _ffn_up(h2, g_up, c_ffn, ffn_conv_b)
    flight_down_w = forward(4, 5, f, "down")
    (g_down,) = finish(4, 5, flight_down_w, f, "down")
    dy, d_out, d_act, dg4, loss_part = _ffn_down(f, act, g_down, x1, target, norm_ffn_post)

    block_of = dict(zip(names, blocks))
    shard_shapes = {"w_in": (D_MODEL, shard_in), "w_conv_branch": (shard_sq, D_MODEL), "w_lru_branch": (shard_sq, D_MODEL),
                    "w_out": (shard_sq, D_MODEL), "lru_wa": (N_HEADS, shard_head, HEAD_DIM),
                    "lru_wx": (N_HEADS, shard_head, HEAD_DIM), "ffn_w_up": (D_MODEL, shard_up),
                    "ffn_w_down": (shard_down, D_MODEL)}

    def reduce_start(tag, grads):
        keys = list(grads)
        sums = _reduce_pair([grads[k] for k in keys], [block_of[k] for k in keys], [shard_shapes[k] for k in keys],
                            "reduce_pair_" + tag)
        return (keys,) + _exchange_chips_start(sums, "reduce_chip_start_" + tag)

    (gw_down,) = _grad_tn([(f, d_out)], min(512, D_FF), "ffn_down_wgrad")
    flight_down = reduce_start("down", {"ffn_w_down": gw_down})
    gw_up, gc_ffn, gb_ffn, d_h2 = _ffn_up_bwd(up, d_act, behind(flight_down[-1], c_ffn), h2, g_up)
    flight_up = reduce_start("up", {"ffn_w_up": gw_up})
    dx1, d_mix, d_pa, d_pb, d_ya, d_yb, d_gate, dg3, dg2 = _merge_bwd(
        dy, d_h2, x1, mix, behind(flight_up[-1], norm_ffn_pre), norm_mix_post, g_out, g_cb, g_lb, pa, pb, proj)
    gw_out, gw_cb, gw_lb = _grad_tn([(merged, d_mix), (y_a, d_pa), (y_b, d_pb)], CB, "merge_wgrads")
    flight_mix = reduce_start("mix", {"w_conv_branch": gw_cb, "w_lru_branch": gw_lb, "w_out": gw_out})
    d_conv, gc_short = _conv_mixer_bwd(proj, d_ya, behind(flight_mix[-1], c_short))
    d_lru, gw_a, gw_x, g_lru_small = _lru_bwd(proj, hl, decay, lru_kept, d_yb, c_lru, g_wa, g_wx, lru_lambda)
    early = [dg2, dg3, dg4, g_lru_small[4:5], g_lru_small[7:8], gb_ffn, gc_short, g_lru_small[0:4],
             g_lru_small[5:6], g_lru_small[6:7], gc_ffn, loss_part]
    flight_small = _small_start(_pack_rows(early), "small_start")
    gw_in = _in_proj_wgrad(h, d_conv, d_lru, d_gate)
    flight_in = reduce_start("in", {"lru_wa": gw_a, "lru_wx": gw_x, "w_in": gw_in})
    dx, dg1 = _in_proj_xgrad(d_conv, d_lru, d_gate, g_in, x2, dx1,
                             behind(flight_small[-1], behind(flight_in[-1], norm_mix_pre)))
    flight_late = _small_start(_pack_rows([dg1]), "small_start_late")

    moments ={"w_in": (m_w_in, v_w_in), "w_conv_branch": (m_w_conv_branch, v_w_conv_branch),
               "w_lru_branch": (m_w_lru_branch, v_w_lru_branch), "w_out": (m_w_out, v_w_out),
               "lru_wa": (m_lru_wa, v_lru_wa), "lru_wx": (m_lru_wx, v_lru_wx), "ffn_w_up": (m_ffn_w_up, v_ffn_w_up),
               "ffn_w_down": (m_ffn_w_down, v_ffn_w_down)}
    weights = {"w_in": w_in, "w_conv_branch": w_conv_branch, "w_lru_branch": w_lru_branch, "w_out": w_out,
               "lru_wa": lru_wa, "lru_wx": lru_wx, "ffn_w_up": ffn_w_up, "ffn_w_down": ffn_w_down}
    out_g, out_d, out_m, out_v = {}, {}, {}, {}

    after = flight_late[-1]
    for tag, (keys, send, recv, sums, lands, _) in (("down", flight_down), ("up", flight_up), ("mix", flight_mix),
                                                    ("in", flight_in)):
        sums, others = _exchange_chips_wait(send, recv, sums, lands, after, "reduce_chip_wait_" + tag)
        by_key = dict(zip(keys, zip(sums, others)))
        for shape in dict.fromkeys(shard_shapes[k] for k in keys):
            same = [k for k in keys if shard_shapes[k] == shape]
            results = _adam_large([weights[k] for k in same], [moments[k][0] for k in same], [moments[k][1] for k in same],
                                  [by_key[k][0] for k in same], [by_key[k][1] for k in same], "adam_" + same[0])
            for out, values in zip((out_g, out_d, out_m, out_v), results):
                out.update(zip(same, values))
        after = out_d[keys[-1]]

    total, total_late = _small_sum([_small_wait(*flight_small[:4], after, "small_wait"),
                                    _small_wait(*flight_late[:4], after, "small_wait_late")], me)
    sizes = [p.size for p in early]
    starts = _packed_starts(sizes)

    def piece(i, shape):
        if i == 0:
            return total_late.reshape(-1)[:D_MODEL].reshape(shape)
        return total[starts[i - 1]:starts[i]].reshape(-1)[:sizes[i - 1]].reshape(shape)

    loss = total[starts[11], 0]

    def col_shard(full, width):
        return lax.dynamic_slice_in_dim(full, me * width, width, axis=1)

    def head_shard(full):
        return lax.dynamic_slice_in_dim(full.reshape(N_HEADS, HEAD_DIM), me * shard_head, shard_head, axis=1)

    small_names = ["norm_mix_pre", "norm_mix_post", "norm_ffn_pre", "norm_ffn_post", "lru_conv_b", "lru_lambda",
                   "ffn_conv_b", "conv_short_w", "lru_conv_w", "lru_ba", "lru_bx", "ffn_conv_w"]
    small_g = [piece(0, (1, D_MODEL)), piece(1, (1, D_MODEL)), piece(2, (1, D_MODEL)), piece(3, (1, D_MODEL)),
               piece(4, (1, D_MODEL)), piece(5, (1, D_MODEL)), piece(6, (1, 2 * D_FF)),
               col_shard(piece(7, (3, D_MODEL)), LANES), col_shard(piece(8, (4, D_MODEL)), LANES),
               head_shard(piece(9, (1, D_MODEL))), head_shard(piece(10, (1, D_MODEL))),
               col_shard(piece(11, (3, 2 * D_FF)), shard_up)]
    small_w = [norm_mix_pre, norm_mix_post, norm_ffn_pre, norm_ffn_post, lru_conv_b, lru_lambda, ffn_conv_b,
               conv_short_w[0], lru_conv_w[0], lru_ba[0], lru_bx[0], ffn_conv_w[0]]
    small_m = [m_norm_mix_pre, m_norm_mix_post, m_norm_ffn_pre, m_norm_ffn_post, m_lru_conv_b, m_lru_lambda,
               m_ffn_conv_b, m_conv_short_w[0], m_lru_conv_w[0], m_lru_ba[0], m_lru_bx[0], m_ffn_conv_w[0]]
    small_v = [v_norm_mix_pre, v_norm_mix_post, v_norm_ffn_pre, v_norm_ffn_post, v_lru_conv_b, v_lru_lambda,
               v_ffn_conv_b, v_conv_short_w[0], v_lru_conv_w[0], v_lru_ba[0], v_lru_bx[0], v_ffn_conv_w[0]]
    s_d, s_m, s_v = _adam_small(small_w, small_g, small_m, small_v)
    for i, name in enumerate(small_names):
        shape = small_w[i].shape if i < 7 else (1,) + small_w[i].shape
        out_g[name] = small_g[i].reshape(shape)
        out_d[name], out_m[name], out_v[name] = s_d[i].reshape(shape), s_m[i].reshape(shape), s_v[i].reshape(shape)

    order = ["norm_mix_pre", "norm_mix_post", "norm_ffn_pre", "norm_ffn_post", "w_in", "conv_short_w", "w_conv_branch",
             "lru_conv_w", "lru_conv_b", "lru_wa", "lru_ba", "lru_wx", "lru_bx", "lru_lambda", "w_lru_branch", "w_out",
             "ffn_w_up", "ffn_conv_w", "ffn_conv_b", "ffn_w_down"]
    return (loss, dx.reshape(1, t, D_MODEL), *[out_g[k] for k in order], *[out_d[k] for k in order],
            *[out_m[k] for k in order], *[out_v[k] for k in order])
```

```python
import functools
import math

import jax
import jax.numpy as jnp
from jax import lax
from jax.experimental import pallas as pl
from jax.experimental.pallas import tpu as pltpu

F32 = jnp.float32
BF16 = jnp.bfloat16
MESH = pl.DeviceIdType.MESH

N_DEV = 8
D_MODEL = 1024
N_HEADS = 4
HEAD_DIM = D_MODEL // N_HEADS
D_FF = 3 * D_MODEL
IN_COLS = 7 * D_MODEL
LRU_C = 8.0
RMS_EPS = 1e-6
ADAM_LR = 0.001
ADAM_B1 = 0.9
ADAM_B2 = 0.999
ADAM_EPS = 1e-08
ADAM_WD = 0.01
ADAM_STEP = 10
GELU_K = math.sqrt(2.0 / math.pi)
GELU_C = 0.044715

LANES = 128
SUBLANES = 8
PAD = SUBLANES
VMEM_LIMIT = 56 * 1024 * 1024
CB = 256
ROW_SLICE = 32
SCAN_UNROLL = 8

HBM_SPEC = pl.BlockSpec(memory_space=pltpu.HBM)
SEM_SPEC = pl.BlockSpec(memory_space=pltpu.SEMAPHORE)
DATAFLOW_EFFECT = pltpu.SideEffectType.DATAFLOW_SIDE_EFFECTING
VMEM_SPEC = pl.BlockSpec(memory_space=pltpu.VMEM)


def _params(*sem):
    if sem:
        return pltpu.CompilerParams(dimension_semantics=sem, vmem_limit_bytes=VMEM_LIMIT)
    return pltpu.CompilerParams(vmem_limit_bytes=VMEM_LIMIT)


def _row_chunk(t):
    return min(256, t)


def _row_block(rows, cap):
    return next(rb for rb in range(min(cap, rows), 0, -16) if rows % rb == 0)


def _gelu(x):
    return 0.5 * x * (1.0 + jnp.tanh(GELU_K * (x + GELU_C * x * x * x)))


def _gelu_and_grad(x):
    t = jnp.tanh(GELU_K * (x + GELU_C * x * x * x))
    g = 0.5 * x * (1.0 + t)
    dg = 0.5 * (1.0 + t) + 0.5 * x * (1.0 - t * t) * GELU_K * (1.0 + 3.0 * GELU_C * x * x)
    return g, dg


def _expm1_neg(x):
    series = x * (1.0 + x * (0.5 + x * (1.0 / 6.0 + x * (1.0 / 24.0 + x * (1.0 / 120.0)))))
    return jnp.where(x > -0.05, series, jnp.exp(x) - 1.0)


def _log_sigmoid(x):
    return jnp.minimum(x, 0.0) - jnp.log1p(jnp.exp(-jnp.abs(x)))


def _dot(a, b):
    return jnp.dot(a, b, preferred_element_type=F32)


def _dot_nt(a, b):
    return lax.dot_general(a, b, (((1,), (1,)), ((), ())), preferred_element_type=F32)


def _dot_tn(a, b):
    return lax.dot_general(a, b, (((0,), (0,)), ((), ())), preferred_element_type=F32)


def _rms_fwd(x):
    r = lax.rsqrt(jnp.mean(x * x, axis=-1, keepdims=True) + RMS_EPS)
    return x * r, r


def _rms_bwd(n, r, gdy):
    return r * (gdy - n * jnp.mean(n * gdy, axis=-1, keepdims=True))


def _rows_back(pad_ref, r0, rows, j):
    cur = pad_ref[pl.ds(PAD + r0, rows), :]
    if j == 0:
        return cur
    before = pad_ref[pl.ds(PAD + r0 - SUBLANES, SUBLANES), :]
    row = lax.broadcasted_iota(jnp.int32, before.shape, 0)
    rolled = pltpu.roll(cur, j, 0)
    top = jnp.where(row < j, pltpu.roll(before, j, 0), rolled[0:SUBLANES, :])
    return jnp.concatenate([top, rolled[SUBLANES:, :]], axis=0)


def _rows_ahead(pad_ref, r0, rows, j):
    cur = pad_ref[pl.ds(r0, rows), :]
    if j == 0:
        return cur
    after = pad_ref[pl.ds(r0 + rows, SUBLANES), :]
    row = lax.broadcasted_iota(jnp.int32, after.shape, 0)
    rolled = pltpu.roll(cur, rows - j, 0)
    bottom = jnp.where(row >= SUBLANES - j, pltpu.roll(after, SUBLANES - j, 0), rolled[rows - SUBLANES:, :])
    return jnp.concatenate([rolled[:rows - SUBLANES, :], bottom], axis=0)


def _fold_rows(v):
    return v.reshape(v.shape[0] // SUBLANES, SUBLANES, v.shape[1]).sum(axis=0)


def _conv_causal(pad_ref, w, r0, rows, taps):
    acc = None
    for k in range(taps):
        term = w[k:k + 1, :] * _rows_back(pad_ref, r0, rows, taps - 1 - k)
        acc = term if acc is None else acc + term
    return acc


def _conv_anticausal(pad_ref, w, r0, rows, taps):
    acc = None
    for k in range(taps):
        term = w[k:k + 1, :] * _rows_ahead(pad_ref, r0, rows, taps - 1 - k)
        acc = term if acc is None else acc + term
    return acc


def _conv_wgrad(g, xpad_ref, r0, rows, taps):
    return [jnp.sum(g * _rows_back(xpad_ref, r0, rows, taps - 1 - k), axis=0, keepdims=True) for k in range(taps)]


def _position():
    return lax.axis_index("x"), lax.axis_index("y"), lax.axis_index("c")


def _block_of(x, y, c):
    return 4 * x + 2 * y + c


def _chip(x, y, k):
    return (x + (k & 1)) % 2, (y + (k >> 1)) % 2


def _cols(width):
    def at(ref, d, half=None):
        cols = pl.ds(pl.multiple_of(d * width, LANES), width)
        if half is None:
            return ref.at[:, cols]
        return ref.at[pl.ds(half * (ref.shape[0] // 2), ref.shape[0] // 2), cols]
    return at


def _rows(height):
    def at(ref, d, half=None):
        if half is None:
            return ref.at[pl.ds(pl.multiple_of(d * height, 16), height), :]
        return ref.at[pl.ds(pl.multiple_of(d * height + half * (height // 2), 16), height // 2), :]
    return at


def _lead(ref, d, half=None):
    if half is None:
        return ref.at[d]
    return ref.at[d, pl.ds(half * (ref.shape[1] // 2), ref.shape[1] // 2)]


def _gather_weights(shards, blocks, full_shapes, small, n_now, tokens, gain):
    n = len(shards)
    small_rows = small.shape[0]
    t = tokens.shape[0]
    rc = min(512, t)

    def body(*refs):
        ins, small_in, x_ref, g_ref = refs[:n], refs[n], refs[n + 1], refs[n + 2]
        outs, small_out, proj_ref, h_ref = refs[n + 3:2 * n + 3], refs[2 * n + 3], refs[2 * n + 4], refs[2 * n + 5]
        stage = refs[2 * n + 6:3 * n + 6]
        w_buf, p_buf, send, recv, local, w_sem, p_sem = refs[3 * n + 6:]
        x, y, c = _position()
        me = _block_of(x, y, c)
        sibling = (x, y, 1 - c)

        for a in range(n):
            stage[a][...] = ins[a][...].astype(BF16)
        for r0 in range(0, t, rc):
            normed, _ = _rms_fwd(x_ref[pl.ds(r0, rc), :])
            h_ref[pl.ds(r0, rc), :] = (normed * g_ref[...]).astype(BF16)
        stores = []

        def project(w_ref, block):
            i = len(stores)
            if i >= 2:
                stores[i - 2].wait()
            for r0 in range(0, t, rc):
                p_buf[i % 2, pl.ds(r0, rc), :] = _dot(h_ref[pl.ds(r0, rc), :], w_ref[...]).astype(BF16)
            st = pltpu.make_async_copy(p_buf.at[i % 2], blocks[0](proj_ref, block), p_sem.at[i % 2])
            st.start()
            stores.append(st)

        def project_landed(block):
            ld = pltpu.make_async_copy(blocks[0](outs[0], block), w_buf, w_sem)
            ld.start()
            ld.wait()
            project(w_buf, block)

        def copy(a, k, block, to, src=None, half=None):
            dst = blocks[a](outs[a], block, half)
            return pltpu.make_async_remote_copy(
                src_ref=dst if src is None else src, dst_ref=dst, send_sem=send.at[a, k], recv_sem=recv.at[a, k],
                device_id=to, device_id_type=MESH)

        def small_copy(k):
            px, py, pc = (x + (k & 1)) % 2, (y + ((k >> 1) & 1)) % 2, (c + (k >> 2)) % 2
            return pltpu.make_async_remote_copy(
                src_ref=small_in, dst_ref=small_out.at[me], send_sem=send.at[n_now, k - 1], recv_sem=recv.at[n_now, k - 1],
                device_id=(px, py, pc), device_id_type=MESH)

        def small_arrival(k):
            px, py, pc = (x + (k & 1)) % 2, (y + ((k >> 1) & 1)) % 2, (c + (k >> 2)) % 2
            return pltpu.make_async_remote_copy(
                src_ref=small_in, dst_ref=small_out.at[_block_of(px, py, pc)], send_sem=send.at[n_now, k - 1],
                recv_sem=recv.at[n_now, k - 1], device_id=(px, py, pc), device_id_type=MESH)

        small_out[me] = small_in[...]
        small_sends = [small_copy(k) for k in range(1, N_DEV)]
        for cp in small_sends:
            cp.start()

        mine, first, passed = [], [], []
        for a in range(n):
            own = pltpu.make_async_copy(stage[a], blocks[a](outs[a], me), local.at[a])
            own.start()
            mine.append(own)
            if a >= n_now:
                continue
            sends = [copy(a, 0, me, sibling, src=stage[a])]
            sends += [copy(a, k, me, (*_chip(x, y, k), c), src=stage[a]) for k in (1, 2)]
            for cp in sends:
                cp.start()
            first += sends

        here = (x, y, c)
        across = [(*_chip(x, y, k), c) for k in (1, 2)]
        near = [[_block_of(*_chip(x, y, k), cc) for k in (1, 2)] for cc in (c, 1 - c)]
        far = [_block_of(*_chip(x, y, 3), cc) for cc in (c, 1 - c)]

        def launch(cp):
            cp.start()
            passed.append(cp)

        project(stage[0], me)
        copy(0, 0, _block_of(x, y, 1 - c), here).wait_recv()
        project_landed(_block_of(x, y, 1 - c))
        for a in range(n_now):
            for i in (0, 1):
                copy(a, 1 + i, near[0][i], here).wait_recv()
                launch(copy(a, 3 + i, near[0][i], across[1 - i], half=i))
                launch(copy(a, 5 + i, near[0][i], sibling))
            if a == 0:
                project_landed(near[0][0])
                project_landed(near[0][1])
        for i in (0, 1):
            copy(0, 5 + i, near[1][i], here).wait_recv()
            project_landed(near[1][i])
        for a in range(n_now):
            for i in (0, 1):
                copy(a, 3 + i, far[0], here, half=i).wait_recv()
                launch(copy(a, 7 + i, far[0], sibling, half=i))
            if a == 0:
                project_landed(far[0])
        for a in range(n_now):
            if a > 0:
                copy(a, 0, _block_of(x, y, 1 - c), here).wait_recv()
                for i in (0, 1):
                    copy(a, 5 + i, near[1][i], here).wait_recv()
            for i in (0, 1):
                copy(a, 7 + i, far[1], here, half=i).wait_recv()
            if a == 0:
                project_landed(far[1])
        for k in range(1, N_DEV):
            small_arrival(k).wait_recv()
        for cp in first + passed + small_sends:
            cp.wait_send()
        for done in mine + stores[-2:]:
            done.wait()

    out_shape = [jax.ShapeDtypeStruct(s, BF16) for s in full_shapes]
    out_shape += [jax.ShapeDtypeStruct((N_DEV, small_rows, LANES), F32), jax.ShapeDtypeStruct((t, full_shapes[0][1]), BF16),
                  jax.ShapeDtypeStruct(tokens.shape, BF16)]
    return pl.pallas_call(
        body, name="gather_weights", out_shape=out_shape,
        in_specs=[VMEM_SPEC] * (n + 3), out_specs=[HBM_SPEC] * n + [VMEM_SPEC, HBM_SPEC, VMEM_SPEC],
        scratch_shapes=[pltpu.VMEM(s.shape, BF16) for s in shards]
        + [pltpu.VMEM(shards[0].shape, BF16), pltpu.VMEM((2, t, shards[0].shape[1]), BF16),
           pltpu.SemaphoreType.DMA((n_now + 1, 9)), pltpu.SemaphoreType.DMA((n_now + 1, 9)),
           pltpu.SemaphoreType.DMA((n,)), pltpu.SemaphoreType.DMA(()), pltpu.SemaphoreType.DMA((2,))],
        compiler_params=_params(),
    )(*shards, small, tokens, gain)


def _gather_first(full, blocks, send, recv):
    x, y, c = _position()
    me = _block_of(x, y, c)
    peers = [(x, y, 1 - c)] + [(*_chip(x, y, k), c) for k in (1, 2, 3)]

    def copy(a, k, block):
        at = blocks[a](full[a], block)
        return pltpu.make_async_remote_copy(src_ref=at, dst_ref=at, send_sem=send[4 * a + k], recv_sem=recv[4 * a + k],
                                            device_id=peers[k], device_id_type=MESH)

    sends = [copy(a, k, me) for a in range(len(full)) for k in range(4)]
    arrivals = [copy(a, k, _block_of(*peers[k])) for a in range(len(full)) for k in range(4)]
    return sends, arrivals


def _gather_second(full, blocks, send, recv):
    x, y, c = _position()

    def copy(a, k, cc):
        at = blocks[a](full[a], _block_of(*_chip(x, y, k), cc))
        return pltpu.make_async_remote_copy(src_ref=at, dst_ref=at, send_sem=send[3 * a + k - 1],
                                            recv_sem=recv[3 * a + k - 1], device_id=(x, y, 1 - c), device_id_type=MESH)

    sends = [copy(a, k, c) for a in range(len(full)) for k in (1, 2, 3)]
    arrivals = [copy(a, k, 1 - c) for a in range(len(full)) for k in (1, 2, 3)]
    return sends, arrivals


def _split_call(body, name, arrays, sems_in, n_sems_out, after=None, token=False):
    n, m = len(arrays), len(sems_in)

    def kernel_body(*refs):
        outs = refs[n + m + (after is not None):]
        body(refs[:n], refs[n:n + m], outs[:n_sems_out])
        if token:
            outs[-1][...] = jnp.zeros_like(outs[-1])

    extra_in = [] if after is None else [after]
    outs = pl.pallas_call(
        kernel_body, name=name,
        out_shape=(*[pltpu.SemaphoreType.DMA(())] * n_sems_out, *[pltpu.HBM(a.shape, a.dtype) for a in arrays],
                   *([jax.ShapeDtypeStruct((SUBLANES, LANES), F32)] if token else [])),
        in_specs=[HBM_SPEC] * n + [SEM_SPEC] * m + [pl.BlockSpec(memory_space=pl.ANY)] * len(extra_in),
        out_specs=(*[SEM_SPEC] * n_sems_out, *[HBM_SPEC] * n, *([VMEM_SPEC] if token else [])),
        input_output_aliases={i: n_sems_out + i for i in range(n)},
        compiler_params=pltpu.CompilerParams(has_side_effects=DATAFLOW_EFFECT),
    )(*[pltpu.with_memory_space_constraint(a, pltpu.HBM) for a in arrays], *sems_in, *extra_in)
    sems, rest = list(outs[:n_sems_out]), list(outs[n_sems_out:])
    return (sems, rest[:n], rest[n]) if token else (sems, rest[:n])


def _gather_start(full, blocks, name):
    n = len(full)

    def body(arrays, _, sems):
        for cp in _gather_first(arrays, blocks, sems[:4 * n], sems[4 * n:])[0]:
            cp.start()

    sems, arrays, token = _split_call(body, name, full, [], 8 * n, token=True)
    return sems[:4 * n], sems[4 * n:], arrays, token


def _gather_forward(full, blocks, send_first, recv_first, after, name):
    n = len(full)

    def body(arrays, sems_in, sems):
        sends, arrivals = _gather_first(arrays, blocks, sems_in[:4 * n], sems_in[4 * n:])
        for cp in arrivals:
            cp.wait_recv()
        for cp in _gather_second(arrays, blocks, sems[:3 * n], sems[3 * n:])[0]:
            cp.start()
        for cp in sends:
            cp.wait_send()

    sems, arrays = _split_call(body, name, full, [*send_first, *recv_first], 6 * n, after=after)
    return sems[:3 * n], sems[3 * n:], arrays


def _gather_finish(full, blocks, send_second, recv_second, after, name):
    n = len(full)

    def body(arrays, sems_in, _):
        sends, arrivals = _gather_second(arrays, blocks, sems_in[:3 * n], sems_in[3 * n:])
        for cp in sends:
            cp.wait_send()
        for cp in arrivals:
            cp.wait_recv()

    return _split_call(body, name, full, [*send_second, *recv_second], 0, after=after)[1]


def _reduce_pair(grads, blocks, shard_shapes, name):
    n = len(grads)

    def body(*refs):
        ins, outs = refs[:n], refs[n:2 * n]
        got, own = refs[2 * n:3 * n], refs[3 * n:4 * n]
        send, recv, local = refs[4 * n:]
        x, y, c = _position()
        copies, loads = [], []
        for a in range(n):
            for k in range(4):
                chip = _chip(x, y, k)
                cp = pltpu.make_async_remote_copy(
                    src_ref=blocks[a](ins[a], _block_of(*chip, 1 - c)), dst_ref=got[a].at[k],
                    send_sem=send.at[a, k], recv_sem=recv.at[a, k], device_id=(x, y, 1 - c), device_id_type=MESH)
                cp.start()
                copies.append(cp)
                ld = pltpu.make_async_copy(blocks[a](ins[a], _block_of(*chip, c)), own[a].at[k], local.at[a, k])
                ld.start()
                loads.append(ld)
        for a in range(n):
            for k in range(4):
                loads[4 * a + k].wait()
                copies[4 * a + k].wait_recv()
                outs[a][k] = (own[a][k].astype(F32) + got[a][k].astype(F32)).astype(BF16)
        for cp in copies:
            cp.wait_send()

    slots = [(4,) + tuple(s) for s in shard_shapes]
    return pl.pallas_call(
        body, name=name, out_shape=[jax.ShapeDtypeStruct(s, BF16) for s in slots],
        in_specs=[HBM_SPEC] * n, out_specs=[VMEM_SPEC] * n,
        scratch_shapes=[pltpu.VMEM(s, BF16) for s in slots] * 2
        + [pltpu.SemaphoreType.DMA((n, 4)), pltpu.SemaphoreType.DMA((n, 4)), pltpu.SemaphoreType.DMA((n, 4))],
        compiler_params=_params(),
    )(*grads)


def _chip_copies(sums, lands, send, recv):
    x, y, c = _position()
    return [pltpu.make_async_remote_copy(
        src_ref=sums[a].at[k], dst_ref=lands[a].at[k - 1], send_sem=send[3 * a + k - 1], recv_sem=recv[3 * a + k - 1],
        device_id=(*_chip(x, y, k), c), device_id_type=MESH) for a in range(len(sums)) for k in (1, 2, 3)]


def _exchange_chips_start(pair_sums, name):
    n = len(pair_sums)
    lands = [pltpu.with_memory_space_constraint(lax.empty((3,) + tuple(p.shape[1:]), BF16), pltpu.HBM) for p in pair_sums]

    def body(*refs):
        sums, zones = refs[:n], refs[n:2 * n]
        send, recv = refs[2 * n:5 * n], refs[5 * n:8 * n]
        token = refs[-1]
        for cp in _chip_copies(sums, zones, send, recv):
            cp.start()
        token[...] = jnp.zeros_like(token)

    outs = pl.pallas_call(
        body, name=name,
        out_shape=(*[pltpu.SemaphoreType.DMA(())] * (6 * n),
                   *[pltpu.HBM(p.shape, BF16) for p in pair_sums], *[pltpu.HBM(z.shape, BF16) for z in lands],
                   jax.ShapeDtypeStruct((SUBLANES, LANES), F32)),
        in_specs=[HBM_SPEC] * (2 * n), out_specs=(*[SEM_SPEC] * (6 * n), *[HBM_SPEC] * (2 * n), VMEM_SPEC),
        input_output_aliases={i: 6 * n + i for i in range(2 * n)},
        compiler_params=pltpu.CompilerParams(has_side_effects=DATAFLOW_EFFECT),
    )(*[pltpu.with_memory_space_constraint(p, pltpu.HBM) for p in pair_sums], *lands)
    return outs[:3 * n], outs[3 * n:6 * n], outs[6 * n:7 * n], outs[7 * n:8 * n], outs[-1]


def _exchange_chips_wait(send, recv, sums, lands, after, name):
    n = len(sums)

    def body(*refs):
        sums_in, zones = refs[:n], refs[n:2 * n]
        send_in, recv_in = refs[2 * n:5 * n], refs[5 * n:8 * n]
        for cp in _chip_copies(sums_in, zones, send_in, recv_in):
            cp.wait_send()
            cp.wait_recv()

    outs = pl.pallas_call(
        body, name=name,
        out_shape=(*[pltpu.HBM(p.shape, BF16) for p in sums], *[pltpu.HBM(z.shape, BF16) for z in lands]),
        in_specs=[HBM_SPEC] * (2 * n) + [SEM_SPEC] * (6 * n) + [pl.BlockSpec(memory_space=pl.ANY)],
        out_specs=[HBM_SPEC] * (2 * n), input_output_aliases={i: i for i in range(2 * n)},
        compiler_params=pltpu.CompilerParams(has_side_effects=DATAFLOW_EFFECT),
    )(*sums, *lands, *send, *recv, after)
    return outs[:n], outs[n:]


def _small_copies(mine, land, send, recv):
    x, y, c = _position()
    me = _block_of(x, y, c)

    def peer(k):
        return (x + (k & 1)) % 2, (y + ((k >> 1) & 1)) % 2, (c + (k >> 2)) % 2

    def copy(k, slot):
        return pltpu.make_async_remote_copy(src_ref=mine, dst_ref=land.at[slot], send_sem=send[k - 1], recv_sem=recv[k - 1],
                                            device_id=peer(k), device_id_type=MESH)

    return [copy(k, me) for k in range(1, N_DEV)], [copy(k, _block_of(*peer(k))) for k in range(1, N_DEV)]


def _small_start(part, name):
    land = jnp.zeros((N_DEV,) + part.shape, F32)

    def body(arrays, _, sems):
        for cp in _small_copies(arrays[0], arrays[1], sems[:7], sems[7:])[0]:
            cp.start()

    sems, arrays, token = _split_call(body, name, [part, land], [], 14, token=True)
    return sems[:7], sems[7:], arrays[0], arrays[1], token


def _small_wait(send, recv, part, land, after, name):
    def body(arrays, sems_in, _):
        sends, arrivals = _small_copies(arrays[0], arrays[1], sems_in[:7], sems_in[7:])
        for cp in sends:
            cp.wait_send()
        for cp in arrivals:
            cp.wait_recv()

    return _split_call(body, name, [part, land], [*send, *recv], 0, after=after)[1]


def _small_sum(pairs, me):
    n = len(pairs)

    def body(me_ref, *refs):
        for i in range(n):
            mine, land, out = refs[2 * i], refs[2 * i + 1], refs[2 * n + i]
            total = jnp.zeros(mine.shape, F32)
            for d in range(N_DEV):
                total = total + land[d] + jnp.where(me_ref[0] == d, mine[...], 0.0)
            out[...] = total

    flat = [a for pair in pairs for a in pair]
    return pl.pallas_call(
        body, name="small_sum", out_shape=[jax.ShapeDtypeStruct(mine.shape, F32) for mine, _ in pairs],
        in_specs=[pl.BlockSpec(memory_space=pltpu.SMEM)] + [VMEM_SPEC] * (2 * n), out_specs=[VMEM_SPEC] * n,
        compiler_params=_params(),
    )(me.reshape(1).astype(jnp.int32), *flat)


def _section(s, t):
    return pl.BlockSpec((t, CB), lambda h, s=s: (0, s * (D_MODEL // CB) + h))


def _conv_mixer_fwd(proj, w_short):
    t = proj.shape[0]
    rc = _row_chunk(t)

    def body(b_ref, c_ref, x_ref, w_ref, y_ref, pad):
        pad[pl.ds(0, PAD), :] = jnp.zeros((PAD, CB), F32)
        for r0 in range(0, t, rc):
            rows = pl.ds(r0, rc)
            pad[pl.ds(PAD + r0, rc), :] = c_ref[rows, :].astype(F32) * x_ref[rows, :].astype(F32)
        w = w_ref[...]
        for r0 in range(0, t, rc):
            rows = pl.ds(r0, rc)
            y_ref[rows, :] = (b_ref[rows, :].astype(F32) * _conv_causal(pad, w, r0, rc, 3)).astype(BF16)

    return pl.pallas_call(
        body, name="conv_mixer_fwd", grid=(D_MODEL // CB,),
        out_shape=jax.ShapeDtypeStruct((t, D_MODEL), BF16),
        in_specs=[_section(0, t), _section(1, t), _section(2, t), pl.BlockSpec((3, CB), lambda h: (0, h))],
        out_specs=pl.BlockSpec((t, CB), lambda h: (0, h)),
        scratch_shapes=[pltpu.VMEM((t + PAD, CB), F32)],
        compiler_params=_params("parallel"),
    )(proj, proj, proj, w_short)


def _lru_gates(xl, wa, ba, wx, bx, ls, first_row):
    xb = xl.astype(BF16)
    ra = jax.nn.sigmoid(_dot(xb, wa) + ba)
    ia = jax.nn.sigmoid(_dot(xb, wx) + bx)
    la = LRU_C * ra * ls
    a = jnp.exp(la)
    one_minus = -_expm1_neg(2.0 * la)
    mult = jnp.where(first_row, 1.0, jnp.sqrt(one_minus))
    return xb, ra, ia, a, one_minus, mult


def _head_specs():
    vec = pl.BlockSpec((1, CB), lambda h: (0, h))
    mat = pl.BlockSpec((N_DEV, None, HEAD_DIM // N_DEV, HEAD_DIM), lambda h: (0, h, 0, 0))
    return vec, mat


def _lru_fwd(proj, w_conv, b_conv, wa, ba, wx, bx, lam):
    t = proj.shape[0]
    rc = _row_chunk(t)
    vec, mat = _head_specs()

    def body(lx_ref, ly_ref, wc_ref, bc_ref, wa_ref, ba_ref, wx_ref, bx_ref, lam_ref, yb_ref, hl_ref, a_ref, kept_ref,
             pad, u_s):
        pad[pl.ds(0, PAD), :] = jnp.zeros((PAD, CB), F32)
        for r0 in range(0, t, rc):
            pad[pl.ds(PAD + r0, rc), :] = lx_ref[pl.ds(r0, rc), :].astype(F32)
        wc, bc = wc_ref[...], bc_ref[...]
        wa_m, wx_m = wa_ref[...].reshape(HEAD_DIM, HEAD_DIM), wx_ref[...].reshape(HEAD_DIM, HEAD_DIM)
        ls = _log_sigmoid(lam_ref[...])
        for r0 in range(0, t, rc):
            rows = pl.ds(r0, rc)
            xl = _conv_causal(pad, wc, r0, rc, 4) + bc
            first = (lax.broadcasted_iota(jnp.int32, (rc, CB), 0) + r0) == 0
            xb, ra, ia, a, _, mult = _lru_gates(xl, wa_m, ba_ref[...], wx_m, bx_ref[...], ls, first)
            a_ref[rows, :] = a
            u_s[rows, :] = mult * (ia * xl)
            kept_ref[0, rows, :] = xb
            kept_ref[1, rows, :] = ra.astype(BF16)
            kept_ref[2, rows, :] = ia.astype(BF16)

        row = lax.broadcasted_iota(jnp.int32, (SUBLANES, CB), 0)

        def group(g, carry):
            r = pl.multiple_of(g * SUBLANES, SUBLANES)
            a_g, b_g = a_ref[pl.ds(r, SUBLANES), :], u_s[pl.ds(r, SUBLANES), :]
            for s in (1, 2, 4):
                keep = row >= s
                b_g = jnp.where(keep, a_g * pltpu.roll(b_g, s, 0) + b_g, b_g)
                a_g = jnp.where(keep, a_g * pltpu.roll(a_g, s, 0), a_g)
            h_g = b_g + a_g * carry
            hl_ref[pl.ds(r, SUBLANES), :] = h_g
            return jnp.broadcast_to(h_g[SUBLANES - 1:SUBLANES, :], (SUBLANES, CB))

        def trip(i, carry):
            for j in range(SCAN_UNROLL):
                carry = group(i * SCAN_UNROLL + j, carry)
            return carry

        lax.fori_loop(0, t // SUBLANES // SCAN_UNROLL, trip, jnp.zeros((SUBLANES, CB), F32))
        for r0 in range(0, t, rc):
            rows = pl.ds(r0, rc)
            yb_ref[rows, :] = (hl_ref[rows, :] * _gelu(ly_ref[rows, :].astype(F32))).astype(BF16)

    blk = pl.BlockSpec((t, CB), lambda h: (0, h))
    res = jax.ShapeDtypeStruct((t, D_MODEL), F32)
    return pl.pallas_call(
        body, name="lru_fwd", grid=(N_HEADS,),
        out_shape=[jax.ShapeDtypeStruct((t, D_MODEL), BF16), res, res, jax.ShapeDtypeStruct((3, t, D_MODEL), BF16)],
        in_specs=[_section(3, t), _section(4, t), pl.BlockSpec((4, CB), lambda h: (0, h)), vec, mat, vec, mat, vec, vec],
        out_specs=[blk, blk, blk, pl.BlockSpec((3, t, CB), lambda h: (0, 0, h))],
        scratch_shapes=[pltpu.VMEM((t + PAD, CB), F32), pltpu.VMEM((t, CB), F32)],
        compiler_params=_params("parallel"),
    )(proj, proj, w_conv, b_conv, wa, ba, wx, bx, lam)


def _merge(y_a, y_b, proj, x, w_cb, w_lb, w_out, g2, g3):
    t = x.shape[0]
    tm = min(512, t)

    def body(ya_ref, yb_ref, gc_ref, gl_ref, x_ref, wcb_ref, wlb_ref, wo_ref, g2_ref, g3_ref,
             pa_ref, pb_ref, mg_ref, mix_ref, x1_ref, h2_ref):
        pa = _dot(ya_ref[...], wcb_ref[...]).astype(BF16)
        pb = _dot(yb_ref[...], wlb_ref[...]).astype(BF16)
        pa_ref[...] = pa
        pb_ref[...] = pb
        merged = (jax.nn.sigmoid(gc_ref[...].astype(F32)) * pa.astype(F32)
                  + jax.nn.sigmoid(gl_ref[...].astype(F32)) * pb.astype(F32)).astype(BF16)
        mg_ref[...] = merged
        mix = _dot(merged, wo_ref[...])
        mix_ref[...] = mix
        n2, _ = _rms_fwd(mix)
        x1 = x_ref[...] + n2 * g2_ref[...]
        x1_ref[...] = x1
        n3, _ = _rms_fwd(x1)
        h2_ref[...] = (n3 * g3_ref[...]).astype(BF16)

    row = pl.BlockSpec((tm, D_MODEL), lambda i: (i, 0))
    full = pl.BlockSpec((D_MODEL, D_MODEL), lambda i: (0, 0))
    vec = pl.BlockSpec((1, D_MODEL), lambda i: (0, 0))
    act = jax.ShapeDtypeStruct((t, D_MODEL), BF16)
    res = jax.ShapeDtypeStruct((t, D_MODEL), F32)
    return pl.pallas_call(
        body, name="merge_fwd", grid=(t // tm,), out_shape=[act, act, act, res, res, act],
        in_specs=[row, row, pl.BlockSpec((tm, D_MODEL), lambda i: (i, 5)), pl.BlockSpec((tm, D_MODEL), lambda i: (i, 6)),
                  row, full, full, full, vec, vec],
        out_specs=[row] * 6,
        compiler_params=_params("parallel"),
    )(y_a, y_b, proj, proj, x, w_cb, w_lb, w_out, g2, g3)


N_FF_BLOCKS = D_FF // CB
FFN_BWD_COLS = 512


def _ffn_up(h2, w_up, w_conv, b_conv):
    t = h2.shape[0]
    rc = _row_chunk(t)
    nb = N_FF_BLOCKS

    def body(h_ref, w_ref, c_ref, b_ref, up_ref, act_ref, f_ref, pad, gate):
        k = pl.program_id(1)
        pad[pl.ds(0, PAD), :] = jnp.zeros((PAD, CB), F32)
        for r0 in range(0, t, rc):
            rows = pl.ds(r0, rc)
            up = _dot(h_ref[rows, :], w_ref[...]).astype(BF16)
            up_ref[rows, :] = up
            pad[pl.ds(PAD + r0, rc), :] = up.astype(F32)
        def conv(keep_gate):
            cw = c_ref[...]
            for r0 in range(0, t, rc):
                rows = pl.ds(r0, rc)
                act = _conv_causal(pad, cw, r0, rc, 3) + b_ref[...]
                act_ref[rows, :] = act.astype(BF16)
                if keep_gate:
                    gate[rows, :] = act
                else:
                    f_ref[rows, :] = (_gelu(gate[rows, :]) * act).astype(BF16)

        @pl.when(k == 0)
        def _():
            conv(True)

        @pl.when(k == 1)
        def _():
            conv(False)

    half = lambda rows: pl.BlockSpec((rows, CB), lambda j, k: (0, nb * k + j))
    wide = jax.ShapeDtypeStruct((t, 2 * D_FF), BF16)
    return pl.pallas_call(
        body, name="ffn_up_fwd", grid=(nb, 2), out_shape=[wide, wide, jax.ShapeDtypeStruct((t, D_FF), BF16)],
        in_specs=[pl.BlockSpec((t, D_MODEL), lambda j, k: (0, 0)), half(D_MODEL), half(3), half(1)],
        out_specs=[half(t), half(t), pl.BlockSpec((t, CB), lambda j, k: (0, j))],
        scratch_shapes=[pltpu.VMEM((t + PAD, CB), F32), pltpu.VMEM((t, CB), F32)],
        compiler_params=_params("parallel", "arbitrary"),
    )(h2, w_up, w_conv, b_conv)


def _ffn_down(f, act, w_down, x1, target, g4):
    t = f.shape[0]
    tm = min(256, t)
    cc = 512

    def body(f_ref, act_ref, w_ref, x1_ref, tg_ref, g_ref, dy_ref, dout_ref, back_ref, dg_ref, loss_ref):
        @pl.when(pl.program_id(0) == 0)
        def _():
            dg_ref[...] = jnp.zeros_like(dg_ref)
            loss_ref[...] = jnp.zeros_like(loss_ref)
        out = _dot(f_ref[...], w_ref[...])
        n4, r4 = _rms_fwd(out)
        err = x1_ref[...] + n4 * g_ref[...] - tg_ref[...]
        loss_ref[...] += jnp.full(loss_ref.shape, 0.5 / D_MODEL, F32) * jnp.sum(err * err)
        dy = err * (1.0 / D_MODEL)
        dy_ref[...] = dy
        dg_ref[...] += jnp.sum(dy * n4, axis=0, keepdims=True)
        d_out = _rms_bwd(n4, r4, dy * g_ref[...]).astype(BF16)
        dout_ref[...] = d_out
        for c0 in range(0, D_FF, cc):
            d_f = _dot_nt(d_out, w_ref[pl.ds(c0, cc), :])
            gelu, d_gelu = _gelu_and_grad(act_ref[:, pl.ds(c0, cc)].astype(F32))
            val = act_ref[:, pl.ds(D_FF + c0, cc)].astype(F32)
            back_ref[:, pl.ds(c0, cc)] = (d_f * val * d_gelu).astype(BF16)
            back_ref[:, pl.ds(D_FF + c0, cc)] = (d_f * gelu).astype(BF16)

    row = pl.BlockSpec((tm, D_MODEL), lambda i: (i, 0))
    wide = pl.BlockSpec((tm, 2 * D_FF), lambda i: (i, 0))
    vec = pl.BlockSpec((1, D_MODEL), lambda i: (0, 0))
    return pl.pallas_call(
        body, name="ffn_down_fwd_bwd", grid=(t // tm,),
        out_shape=[jax.ShapeDtypeStruct((t, D_MODEL), F32), jax.ShapeDtypeStruct((t, D_MODEL), BF16),
                   jax.ShapeDtypeStruct((t, 2 * D_FF), BF16), jax.ShapeDtypeStruct((1, D_MODEL), F32),
                   jax.ShapeDtypeStruct((SUBLANES, LANES), F32)],
        in_specs=[pl.BlockSpec((tm, D_FF), lambda i: (i, 0)), wide, pl.BlockSpec((D_FF, D_MODEL), lambda i: (0, 0)),
                  row, row, vec],
        out_specs=[row, row, wide, vec, pl.BlockSpec((SUBLANES, LANES), lambda i: (0, 0))],
        compiler_params=_params("arbitrary"),
    )(f, act, w_down, x1, target, g4)


def _grad_tn(pairs, bm, name):
    k = len(pairs)
    t, m = pairs[0][0].shape
    n = pairs[0][1].shape[1]

    def body(*refs):
        for i in range(k):
            refs[2 * k + i][...] = _dot_tn(refs[2 * i][...], refs[2 * i + 1][...]).astype(BF16)

    return pl.pallas_call(
        body, name=name, grid=(m // bm,), out_shape=[jax.ShapeDtypeStruct((m, n), BF16)] * k,
        in_specs=[pl.BlockSpec((t, bm), lambda i: (0, i)), pl.BlockSpec((t, n), lambda i: (0, 0))] * k,
        out_specs=[pl.BlockSpec((bm, n), lambda i: (i, 0))] * k,
        compiler_params=_params("parallel"),
    )(*[x for pair in pairs for x in pair])


def _ffn_up_bwd(up, back, w_conv, h2, w_up):
    t = h2.shape[0]
    rc = _row_chunk(t)
    cb = FFN_BWD_COLS

    def body(up_ref, back_ref, c_ref, h_ref, w_ref, dw_ref, dcw_ref, dcb_ref, dh_ref, pad, after, d_up):
        @pl.when(pl.program_id(0) == 0)
        def _():
            dh_ref[...] = jnp.zeros_like(dh_ref)
        pad[pl.ds(0, PAD), :] = jnp.zeros((PAD, cb), F32)
        after[pl.ds(t, PAD), :] = jnp.zeros((PAD, cb), F32)
        for r0 in range(0, t, rc):
            pad[pl.ds(PAD + r0, rc), :] = up_ref[pl.ds(r0, rc), :].astype(F32)
            after[pl.ds(r0, rc), :] = back_ref[pl.ds(r0, rc), :].astype(F32)
        cw = c_ref[...]
        taps = [jnp.zeros((SUBLANES, cb), F32)] * 3
        bias = jnp.zeros((SUBLANES, cb), F32)
        for r0 in range(0, t, rc):
            for q0 in range(r0, r0 + rc, ROW_SLICE):
                rows = pl.ds(q0, ROW_SLICE)
                d_up[rows, :] = _conv_anticausal(after, cw, q0, ROW_SLICE, 3).astype(BF16)
                g = after[rows, :]
                taps = [acc + _fold_rows(g * _rows_back(pad, q0, ROW_SLICE, 2 - k)) for k, acc in enumerate(taps)]
                bias = bias + _fold_rows(g)
            rows = pl.ds(r0, rc)
            dh_ref[rows, :] += _dot_nt(d_up[rows, :], w_ref[...])
        dw_ref[...] = _dot_tn(h_ref[...], d_up[...]).astype(BF16)
        dcw_ref[...] = jnp.concatenate([jnp.sum(acc, axis=0, keepdims=True) for acc in taps], axis=0)
        dcb_ref[...] = jnp.sum(bias, axis=0, keepdims=True)

    cols = lambda rows: pl.BlockSpec((rows, cb), lambda j: (0, j))
    whole = pl.BlockSpec((t, D_MODEL), lambda j: (0, 0))
    return pl.pallas_call(
        body, name="ffn_up_bwd", grid=(2 * D_FF // cb,),
        out_shape=[jax.ShapeDtypeStruct((D_MODEL, 2 * D_FF), BF16), jax.ShapeDtypeStruct((3, 2 * D_FF), F32),
                   jax.ShapeDtypeStruct((1, 2 * D_FF), F32), jax.ShapeDtypeStruct((t, D_MODEL), F32)],
        in_specs=[cols(t), cols(t), cols(3), whole, cols(D_MODEL)],
        out_specs=[cols(D_MODEL), cols(3), cols(1), whole],
        scratch_shapes=[pltpu.VMEM((t + PAD, cb), F32), pltpu.VMEM((t + PAD, cb), F32), pltpu.VMEM((t, cb), BF16)],
        compiler_params=_params("arbitrary"),
    )(up, back, w_conv, h2, w_up)


def _merge_bwd(dy, d_h2, x1, mix, g3, g2, w_out, w_cb, w_lb, pa, pb, proj):
    t = dy.shape[0]
    tm = min(256, t)

    def body(dy_ref, dh2_ref, x1_ref, mix_ref, g3_ref, g2_ref, wo_ref, wcb_ref, wlb_ref, pa_ref, pb_ref, gc_ref, gl_ref,
             dx1_ref, dmix_ref, dpa_ref, dpb_ref, dya_ref, dyb_ref, dgate_ref, dg3_ref, dg2_ref):
        @pl.when(pl.program_id(0) == 0)
        def _():
            dg3_ref[...] = jnp.zeros_like(dg3_ref)
            dg2_ref[...] = jnp.zeros_like(dg2_ref)
        n3, r3 = _rms_fwd(x1_ref[...])
        d_h2 = dh2_ref[...]
        dg3_ref[...] += jnp.sum(d_h2 * n3, axis=0, keepdims=True)
        dx1 = dy_ref[...] + _rms_bwd(n3, r3, d_h2 * g3_ref[...])
        dx1_ref[...] = dx1
        n2, r2 = _rms_fwd(mix_ref[...])
        dg2_ref[...] += jnp.sum(dx1 * n2, axis=0, keepdims=True)
        d_mix = _rms_bwd(n2, r2, dx1 * g2_ref[...]).astype(BF16)
        dmix_ref[...] = d_mix
        d_merged = _dot_nt(d_mix, wo_ref[...])
        sc = jax.nn.sigmoid(gc_ref[...].astype(F32))
        sl = jax.nn.sigmoid(gl_ref[...].astype(F32))
        d_pa = (d_merged * sc).astype(BF16)
        d_pb = (d_merged * sl).astype(BF16)
        dpa_ref[...] = d_pa
        dpb_ref[...] = d_pb
        dgate_ref[0] = (d_merged * pa_ref[...].astype(F32) * sc * (1.0 - sc)).astype(BF16)
        dgate_ref[1] = (d_merged * pb_ref[...].astype(F32) * sl * (1.0 - sl)).astype(BF16)
        dya_ref[...] = _dot_nt(d_pa, wcb_ref[...]).astype(BF16)
        dyb_ref[...] = _dot_nt(d_pb, wlb_ref[...]).astype(BF16)

    row = pl.BlockSpec((tm, D_MODEL), lambda i: (i, 0))
    full = pl.BlockSpec((D_MODEL, D_MODEL), lambda i: (0, 0))
    vec = pl.BlockSpec((1, D_MODEL), lambda i: (0, 0))
    act = jax.ShapeDtypeStruct((t, D_MODEL), BF16)
    small = jax.ShapeDtypeStruct((1, D_MODEL), F32)
    return pl.pallas_call(
        body, name="merge_bwd", grid=(t // tm,),
        out_shape=[jax.ShapeDtypeStruct((t, D_MODEL), F32), act, act, act, act, act,
                   jax.ShapeDtypeStruct((2, t, D_MODEL), BF16), small, small],
        in_specs=[row, row, row, row, vec, vec, full, full, full, row, row,
                  pl.BlockSpec((tm, D_MODEL), lambda i: (i, 5)), pl.BlockSpec((tm, D_MODEL), lambda i: (i, 6))],
        out_specs=[row] * 6 + [pl.BlockSpec((2, tm, D_MODEL), lambda i: (0, i, 0)), vec, vec],
        compiler_params=_params("arbitrary"),
    )(dy, d_h2, x1, mix, g3, g2, w_out, w_cb, w_lb, pa, pb, proj, proj)


def _conv_mixer_bwd(proj, d_ya, w_short):
    t = proj.shape[0]
    rc = _row_chunk(t)

    def body(b_ref, c_ref, x_ref, dy_ref, w_ref, d_ref, dw_ref, pad, back):
        pad[pl.ds(0, PAD), :] = jnp.zeros((PAD, CB), F32)
        back[pl.ds(t, PAD), :] = jnp.zeros((PAD, CB), F32)
        for r0 in range(0, t, rc):
            rows = pl.ds(r0, rc)
            pad[pl.ds(PAD + r0, rc), :] = c_ref[rows, :].astype(F32) * x_ref[rows, :].astype(F32)
        w = w_ref[...]
        for r0 in range(0, t, rc):
            rows = pl.ds(r0, rc)
            d_y = dy_ref[rows, :].astype(F32)
            d_ref[0, rows, :] = (d_y * _conv_causal(pad, w, r0, rc, 3)).astype(BF16)
            back[rows, :] = d_y * b_ref[rows, :].astype(F32)
        taps = [jnp.zeros((1, CB), F32)] * 3
        for r0 in range(0, t, rc):
            rows = pl.ds(r0, rc)
            d_u = _conv_anticausal(back, w, r0, rc, 3)
            d_ref[1, rows, :] = (d_u * x_ref[rows, :].astype(F32)).astype(BF16)
            d_ref[2, rows, :] = (d_u * c_ref[rows, :].astype(F32)).astype(BF16)
            taps = [acc + new for acc, new in zip(taps, _conv_wgrad(back[rows, :], pad, r0, rc, 3))]
        dw_ref[...] = jnp.concatenate(taps, axis=0)

    blk = pl.BlockSpec((t, CB), lambda h: (0, h))
    return pl.pallas_call(
        body, name="conv_mixer_bwd", grid=(D_MODEL // CB,),
        out_shape=[jax.ShapeDtypeStruct((3, t, D_MODEL), BF16), jax.ShapeDtypeStruct((3, D_MODEL), F32)],
        in_specs=[_section(0, t), _section(1, t), _section(2, t), blk, pl.BlockSpec((3, CB), lambda h: (0, h))],
        out_specs=[pl.BlockSpec((3, t, CB), lambda h: (0, 0, h)), pl.BlockSpec((3, CB), lambda h: (0, h))],
        scratch_shapes=[pltpu.VMEM((t + PAD, CB), F32), pltpu.VMEM((t + PAD, CB), F32)],
        compiler_params=_params("parallel"),
    )(proj, proj, proj, d_ya, w_short)


LRU_SMALL_ROWS = 8


def _lru_bwd(proj, hl, a_all, kept, d_yb, w_conv, wa, wx, lam):
    t = proj.shape[0]
    rc = _row_chunk(t)
    vec, mat = _head_specs()

    def body(lx_ref, ly_ref, hl_ref, a_ref, kept_ref, dy_ref, wc_ref, wa_ref, wx_ref, lam_ref,
             d_ref, dwa_ref, dwx_ref, small_ref, pad, a_next, dh_s, dh_o, h_prev, back, acc_a, acc_x, dz_a, dz_x):
        zeros = jnp.zeros((PAD, CB), F32)
        pad[pl.ds(0, PAD), :] = zeros
        h_prev[pl.ds(0, PAD), :] = zeros
        a_next[pl.ds(t, PAD), :] = zeros
        back[pl.ds(t, PAD), :] = zeros
        for r0 in range(0, t, ROW_SLICE):
            rows = pl.ds(r0, ROW_SLICE)
            pad[pl.ds(PAD + r0, ROW_SLICE), :] = lx_ref[rows, :].astype(F32)
            h_prev[pl.ds(PAD + r0, ROW_SLICE), :] = hl_ref[rows, :]
            a_next[pl.ds(PAD - 1 + r0, ROW_SLICE), :] = a_ref[rows, :]
            act, d_act = _gelu_and_grad(ly_ref[rows, :].astype(F32))
            d_y = dy_ref[rows, :].astype(F32)
            dh_s[rows, :] = d_y * act
            d_ref[1, rows, :] = (d_y * hl_ref[rows, :] * d_act).astype(BF16)
        wc = wc_ref[...]
        wa_m, wx_m = wa_ref[...].reshape(HEAD_DIM, HEAD_DIM), wx_ref[...].reshape(HEAD_DIM, HEAD_DIM)
        ls = _log_sigmoid(lam_ref[...])

        row = lax.broadcasted_iota(jnp.int32, (SUBLANES, CB), 0)
        groups = t // SUBLANES

        def group(i, carry):
            r = pl.multiple_of((groups - 1 - i) * SUBLANES, SUBLANES)
            a_g, b_g = a_next[pl.ds(PAD + r, SUBLANES), :], dh_s[pl.ds(r, SUBLANES), :]
            for s in (1, 2, 4):
                keep = row < SUBLANES - s
                b_g = jnp.where(keep, a_g * pltpu.roll(b_g, SUBLANES - s, 0) + b_g, b_g)
                a_g = jnp.where(keep, a_g * pltpu.roll(a_g, SUBLANES - s, 0), a_g)
            d_g = b_g + a_g * carry
            dh_o[pl.ds(r, SUBLANES), :] = d_g
            return jnp.broadcast_to(d_g[0:1, :], (SUBLANES, CB))

        def trip(i, carry):
            for j in range(SCAN_UNROLL):
                carry = group(i * SCAN_UNROLL + j, carry)
            return carry

        lax.fori_loop(0, groups // SCAN_UNROLL, trip, jnp.zeros((SUBLANES, CB), F32))

        acc_a[...] = jnp.zeros_like(acc_a)
        acc_x[...] = jnp.zeros_like(acc_x)
        d_ba = d_bx = d_ls = jnp.zeros((SUBLANES, CB), F32)
        for r0 in range(0, t, rc):
            for q0 in range(r0, r0 + rc, ROW_SLICE):
                rows, local = pl.ds(q0, ROW_SLICE), pl.ds(q0 - r0, ROW_SLICE)
                a = a_ref[rows, :]
                xl, ra, ia = (kept_ref[i, rows, :].astype(F32) for i in range(3))
                a_sq = a * a
                mult = jnp.sqrt(1.0 - a_sq)
                slope = -a_sq / mult
                if q0 == 0:
                    first = lax.broadcasted_iota(jnp.int32, (ROW_SLICE, CB), 0) == 0
                    mult, slope = jnp.where(first, 1.0, mult), jnp.where(first, 0.0, slope)
                d_h = dh_o[rows, :]
                d_la = d_h * _rows_back(h_prev, q0, ROW_SLICE, 1) * a + d_h * ia * xl * slope
                d_za = d_la * (LRU_C * ls) * ra * (1.0 - ra)
                d_zx = d_h * mult * xl * ia * (1.0 - ia)
                d_ls = d_ls + _fold_rows(d_la * ra)
                d_ba = d_ba + _fold_rows(d_za)
                d_bx = d_bx + _fold_rows(d_zx)
                dz_a[local, :] = d_za.astype(BF16)
                dz_x[local, :] = d_zx.astype(BF16)
                back[rows, :] = d_h * mult * ia
            rows = pl.ds(r0, rc)
            xb = kept_ref[0, rows, :]
            acc_a[...] += _dot_tn(xb, dz_a[...])
            acc_x[...] += _dot_tn(xb, dz_x[...])
            back[rows, :] += _dot_nt(dz_a[...], wa_m) + _dot_nt(dz_x[...], wx_m)
        taps = [jnp.zeros((SUBLANES, CB), F32)] * 4
        d_bc = jnp.zeros((SUBLANES, CB), F32)
        for q0 in range(0, t, ROW_SLICE):
            rows = pl.ds(q0, ROW_SLICE)
            d_ref[0, rows, :] = _conv_anticausal(back, wc, q0, ROW_SLICE, 4).astype(BF16)
            g = back[rows, :]
            taps = [acc + _fold_rows(g * _rows_back(pad, q0, ROW_SLICE, 3 - k)) for k, acc in enumerate(taps)]
            d_bc = d_bc + _fold_rows(g)
        d_lam = d_ls * LRU_C * jax.nn.sigmoid(-lam_ref[...])
        small_ref[...] = jnp.concatenate(
            [jnp.sum(v, axis=0, keepdims=True) for v in taps + [d_bc, d_ba, d_bx, d_lam]], axis=0)
        dwa_ref[...] = acc_a[...].reshape(N_DEV, HEAD_DIM // N_DEV, HEAD_DIM).astype(BF16)
        dwx_ref[...] = acc_x[...].reshape(N_DEV, HEAD_DIM // N_DEV, HEAD_DIM).astype(BF16)

    blk = pl.BlockSpec((t, CB), lambda h: (0, h))
    gate_grad = jax.ShapeDtypeStruct((N_DEV, N_HEADS, HEAD_DIM // N_DEV, HEAD_DIM), BF16)
    return pl.pallas_call(
        body, name="lru_bwd", grid=(N_HEADS,),
        out_shape=[jax.ShapeDtypeStruct((2, t, D_MODEL), BF16), gate_grad, gate_grad,
                   jax.ShapeDtypeStruct((LRU_SMALL_ROWS, D_MODEL), F32)],
        in_specs=[_section(3, t), _section(4, t), blk, blk, pl.BlockSpec((3, t, CB), lambda h: (0, 0, h)), blk,
                  pl.BlockSpec((4, CB), lambda h: (0, h)), mat, mat, vec],
        out_specs=[pl.BlockSpec((2, t, CB), lambda h: (0, 0, h)), mat, mat,
                   pl.BlockSpec((LRU_SMALL_ROWS, CB), lambda h: (0, h))],
        scratch_shapes=[pltpu.VMEM((t + PAD, CB), F32), pltpu.VMEM((t + PAD, CB), F32), pltpu.VMEM((t, CB), F32),
                        pltpu.VMEM((t, CB), F32), pltpu.VMEM((t + PAD, CB), F32), pltpu.VMEM((t + PAD, CB), F32),
                        pltpu.VMEM((HEAD_DIM, HEAD_DIM), F32), pltpu.VMEM((HEAD_DIM, HEAD_DIM), F32),
                        pltpu.VMEM((rc, CB), BF16), pltpu.VMEM((rc, CB), BF16)],
        compiler_params=_params("parallel"),
    )(proj, proj, hl, a_all, kept, d_yb, w_conv, wa, wx, lam)


def _stack_maps(halves):
    def conv(sec, part):
        return jnp.minimum(sec, 2), jnp.where(sec < 3, part, halves - 1)

    def lru(sec, part):
        return jnp.clip(sec - 3, 0, 1), jnp.where(sec < 3, 0, jnp.where(sec < 5, part, halves - 1))

    def gate(sec, part):
        return jnp.clip(sec - 5, 0, 1), jnp.where(sec < 5, 0, part)

    return conv, lru, gate


def _pick_stack(sec, refs, fn):
    @pl.when(sec < 3)
    def _():
        fn(refs[0])

    @pl.when((sec >= 3) & (sec < 5))
    def _():
        fn(refs[1])

    @pl.when(sec >= 5)
    def _():
        fn(refs[2])


def _in_proj_wgrad(h, d_conv, d_lru, d_gate):
    t = h.shape[0]
    halves, bn = 1, D_MODEL
    maps = _stack_maps(halves)

    def body(h_ref, dc_ref, dl_ref, dg_ref, o_ref):
        def emit(ref):
            o_ref[...] = _dot_tn(h_ref[...], ref[...]).astype(BF16)
        _pick_stack(pl.program_id(0) // halves, (dc_ref, dl_ref, dg_ref), emit)

    def spec(m):
        def index(s):
            stack, part = m(s // halves, s % halves)
            return stack, 0, part
        return pl.BlockSpec((None, t, bn), index)

    return pl.pallas_call(
        body, name="in_proj_wgrad", grid=(7 * halves,), out_shape=jax.ShapeDtypeStruct((D_MODEL, IN_COLS), BF16),
        in_specs=[pl.BlockSpec((t, D_MODEL), lambda s: (0, 0))] + [spec(m) for m in maps],
        out_specs=pl.BlockSpec((D_MODEL, bn), lambda s: (0, s)),
        compiler_params=_params("arbitrary"),
    )(h, d_conv, d_lru, d_gate)


def _in_proj_xgrad(d_conv, d_lru, d_gate, w_in, x, dx1, g1):
    t = x.shape[0]
    tm = min(1024, t)
    maps = _stack_maps(1)

    def body(dc_ref, dl_ref, dg_ref, w_ref, x_ref, dx1_ref, g_ref, dx_ref, dgain_ref, acc):
        i, s = pl.program_id(0), pl.program_id(1)

        @pl.when((i == 0) & (s == 0))
        def _():
            dgain_ref[...] = jnp.zeros_like(dgain_ref)

        @pl.when(s == 0)
        def _():
            acc[...] = jnp.zeros_like(acc)

        def add(ref):
            acc[...] += _dot_nt(ref[...], w_ref[...])
        _pick_stack(s, (dc_ref, dl_ref, dg_ref), add)

        @pl.when(s == 6)
        def _():
            n1, r1 = _rms_fwd(x_ref[...])
            d_h = acc[...]
            dgain_ref[...] += jnp.sum(d_h * n1, axis=0, keepdims=True)
            dx_ref[...] = dx1_ref[...] + _rms_bwd(n1, r1, d_h * g_ref[...])

    def spec(m):
        def index(i, s):
            return m(s, 0)[0], i, 0
        return pl.BlockSpec((None, tm, D_MODEL), index)

    row = pl.BlockSpec((tm, D_MODEL), lambda i, s: (i, 0))
    vec = pl.BlockSpec((1, D_MODEL), lambda i, s: (0, 0))
    return pl.pallas_call(
        body, name="in_proj_xgrad", grid=(t // tm, 7),
        out_shape=[jax.ShapeDtypeStruct((t, D_MODEL), F32), jax.ShapeDtypeStruct((1, D_MODEL), F32)],
        in_specs=[spec(m) for m in maps] + [pl.BlockSpec((D_MODEL, D_MODEL), lambda i, s: (0, s)), row, row, vec],
        out_specs=[row, vec],
        scratch_shapes=[pltpu.VMEM((tm, D_MODEL), F32)],
        compiler_params=_params("arbitrary", "arbitrary"),
    )(d_conv, d_lru, d_gate, w_in, x, dx1, g1)


def _adamw(w, g, m, v):
    m = ADAM_B1 * m + (1.0 - ADAM_B1) * g
    v = ADAM_B2 * v + (1.0 - ADAM_B2) * (g * g)
    m_hat = m / (1.0 - ADAM_B1 ** ADAM_STEP)
    v_hat = v / (1.0 - ADAM_B2 ** ADAM_STEP)
    return -ADAM_LR * (m_hat / (jnp.sqrt(v_hat) + ADAM_EPS) + ADAM_WD * w), m, v


def _adam_large(ws, ms, vs, owns, others, name):
    n = len(ws)
    shape = ws[0].shape
    cols = shape[-1]
    flat = [[a.reshape(-1, cols) for a in group] for group in (ws, ms, vs)]
    rows = flat[0][0].shape[0]
    owns, others = [o.reshape(4, rows, cols) for o in owns], [o.reshape(3, rows, cols) for o in others]
    rb = _row_block(rows, 512)

    def body(*refs):
        ins, outs = refs[:5 * n], refs[5 * n:]
        for i in range(n):
            w_ref, m_ref, v_ref, own_ref, oth_ref = ins[i::n]
            g = own_ref[...].astype(F32)
            for k in range(3):
                g = g + oth_ref[k].astype(F32)
            outs[i][...] = g
            outs[n + i][...], outs[2 * n + i][...], outs[3 * n + i][...] = _adamw(w_ref[...], g, m_ref[...], v_ref[...])

    blk = pl.BlockSpec((rb, cols), lambda i: (i, 0))
    res = jax.ShapeDtypeStruct((rows, cols), F32)
    outs = pl.pallas_call(
        body, name=name, grid=(rows // rb,), out_shape=[res] * (4 * n),
        in_specs=[blk] * (3 * n) + [pl.BlockSpec((None, rb, cols), lambda i: (0, i, 0))] * n
        + [pl.BlockSpec((3, rb, cols), lambda i: (0, i, 0))] * n,
        out_specs=[blk] * (4 * n), compiler_params=_params("parallel"),
    )(*flat[0], *flat[1], *flat[2], *owns, *others)
    outs = [o.reshape(shape) for o in outs]
    return outs[:n], outs[n:2 * n], outs[2 * n:3 * n], outs[3 * n:]


def _adam_small(ws, gs, ms, vs):
    n = len(ws)

    def body(*refs):
        w_refs, g_refs, m_refs, v_refs = (refs[i * n:(i + 1) * n] for i in range(4))
        outs = refs[4 * n:]
        for i in range(n):
            d, m, v = _adamw(w_refs[i][...], g_refs[i][...], m_refs[i][...], v_refs[i][...])
            outs[i][...], outs[n + i][...], outs[2 * n + i][...] = d, m, v

    shapes = [jax.ShapeDtypeStruct(w.shape, F32) for w in ws]
    outs = pl.pallas_call(
        body, name="adam_small", out_shape=shapes * 3,
        in_specs=[VMEM_SPEC] * (4 * n), out_specs=[VMEM_SPEC] * (3 * n), compiler_params=_params(),
    )(*ws, *gs, *ms, *vs)
    return outs[:n], outs[n:2 * n], outs[2 * n:]


def _pack_rows(pieces):
    tile = SUBLANES * LANES
    return jnp.concatenate([jnp.pad(p.reshape(-1), (0, (-p.size) % tile)).reshape(-1, LANES) for p in pieces], axis=0)


def _packed_starts(sizes):
    tile = SUBLANES * LANES
    starts = [0]
    for s in sizes:
        starts.append(starts[-1] + (s + tile - 1) // tile * SUBLANES)
    return starts


def kernel(x, norm_mix_pre, norm_mix_post, norm_ffn_pre, norm_ffn_post, w_in, conv_short_w, w_conv_branch, lru_conv_w, lru_conv_b, lru_wa, lru_ba, lru_wx, lru_bx, lru_lambda, w_lru_branch, w_out, ffn_w_up, ffn_conv_w, ffn_conv_b, ffn_w_down, loss_target, m_norm_mix_pre, m_norm_mix_post, m_norm_ffn_pre, m_norm_ffn_post, m_w_in, m_conv_short_w, m_w_conv_branch, m_lru_conv_w, m_lru_conv_b, m_lru_wa, m_lru_ba, m_lru_wx, m_lru_bx, m_lru_lambda, m_w_lru_branch, m_w_out, m_ffn_w_up, m_ffn_conv_w, m_ffn_conv_b, m_ffn_w_down, v_norm_mix_pre, v_norm_mix_post, v_norm_ffn_pre, v_norm_ffn_post, v_w_in, v_conv_short_w, v_w_conv_branch, v_lru_conv_w, v_lru_conv_b, v_lru_wa, v_lru_ba, v_lru_wx, v_lru_bx, v_lru_lambda, v_w_lru_branch, v_w_out, v_ffn_w_up, v_ffn_conv_w, v_ffn_conv_b, v_ffn_w_down):
    t = x.shape[1]
    xi, yi, ci = _position()
    me = _block_of(xi, yi, ci)
    x2, target = x[0], loss_target[0]
    shard_in, shard_up = IN_COLS // N_DEV, 2 * D_FF // N_DEV
    shard_sq, shard_down, shard_head = D_MODEL // N_DEV, D_FF // N_DEV, HEAD_DIM // N_DEV

    names = ["w_in", "lru_wa", "lru_wx", "w_conv_branch", "w_lru_branch", "w_out", "ffn_w_up", "ffn_w_down"]
    large = [w_in[0], lru_wa[0], lru_wx[0], w_conv_branch[0], w_lru_branch[0], w_out[0], ffn_w_up[0], ffn_w_down[0]]
    blocks = [_cols(shard_in), _lead, _lead, _rows(shard_sq), _rows(shard_sq), _rows(shard_sq),
              _cols(shard_up), _rows(shard_down)]
    gate_full = (N_DEV, N_HEADS, shard_head, HEAD_DIM)
    full_shapes = [(D_MODEL, IN_COLS), gate_full, gate_full, (D_MODEL, D_MODEL), (D_MODEL, D_MODEL), (D_MODEL, D_MODEL),
                   (D_MODEL, 2 * D_FF), (D_FF, D_MODEL)]
    n_now = 3
    small_sharded = [conv_short_w, lru_conv_w, lru_ba, lru_bx, ffn_conv_w]
    small_mine = _pack_rows(small_sharded)
    small_at = _packed_starts([p.size for p in small_sharded])
    *gathered, small_all, proj, h = _gather_weights(large, blocks, full_shapes, small_mine, n_now, x2, norm_mix_pre)
    g_in, g_wa, g_wx = gathered[:n_now]
    later_blocks = blocks[n_now:]
    send1, recv1, later, gather_token = _gather_start(gathered[n_now:], later_blocks, "gather_start")

    def behind(token, operand):
        return operand + token[0:1, 0:1]

    def forward(lo, hi, after, tag):
        return _gather_forward(later[lo:hi], later_blocks[lo:hi], send1[4 * lo:4 * hi], recv1[4 * lo:4 * hi], after,
                               "gather_forward_" + tag)

    def finish(lo, hi, flight, after, tag):
        return _gather_finish(flight[2], later_blocks[lo:hi], flight[0], flight[1], after, "gather_finish_" + tag)

    def cols_of(r0, n, width):
        part = small_all[:, r0:r0 + n * width // LANES, :].reshape(N_DEV, n, width)
        return part.transpose(1, 0, 2).reshape(n, N_DEV * width)

    c_short = cols_of(small_at[0], 3, LANES)
    c_lru = cols_of(small_at[1], 4, LANES)
    b_a = cols_of(small_at[2], N_HEADS, shard_head).reshape(1, D_MODEL)
    b_x = cols_of(small_at[3], N_HEADS, shard_head).reshape(1, D_MODEL)
    c_ffn = cols_of(small_at[4], 3, shard_up)

    y_a = _conv_mixer_fwd(proj, behind(gather_token, c_short))
    y_b, hl, decay, lru_kept = _lru_fwd(proj, behind(gather_token, c_lru), lru_conv_b, g_wa, b_a, g_wx, b_x, lru_lambda)
    flight_mix_w = forward(0, 3, y_b, "mix")
    g_cb, g_lb, g_out = finish(0, 3, flight_mix_w, y_b, "mix")
    pa, pb, merged, mix, x1, h2 = _merge(y_a, y_b, proj, x2, g_cb, g_lb, g_out, norm_mix_post, norm_ffn_pre)
    flight_up_w = forward(3, 4, h2, "up")
    (g_up,) = finish(3, 4, flight_up_w, h2, "up")
    up, act, f = _ffn_up(h2, g_up, c_ffn, ffn_conv_b)
    flight_down_w = forward(4, 5, f, "down")
    (g_down,) = finish(4, 5, flight_down_w, f, "down")
    dy, d_out, d_act, dg4, loss_part = _ffn_down(f, act, g_down, x1, target, norm_ffn_post)

    block_of = dict(zip(names, blocks))
    shard_shapes = {"w_in": (D_MODEL, shard_in), "w_conv_branch": (shard_sq, D_MODEL), "w_lru_branch": (shard_sq, D_MODEL),
                    "w_out": (shard_sq, D_MODEL), "lru_wa": (N_HEADS, shard_head, HEAD_DIM),
                    "lru_wx": (N_HEADS, shard_head, HEAD_DIM), "ffn_w_up": (D_MODEL, shard_up),
                    "ffn_w_down": (shard_down, D_MODEL)}

    def reduce_start(tag, grads):
        keys = list(grads)
        sums = _reduce_pair([grads[k] for k in keys], [block_of[k] for k in keys], [shard_shapes[k] for k in keys],
                            "reduce_pair_" + tag)
        return (keys,) + _exchange_chips_start(sums, "reduce_chip_start_" + tag)

    (gw_down,) = _grad_tn([(f, d_out)], min(1024, D_FF), "ffn_down_wgrad")
    flight_down = reduce_start("down", {"ffn_w_down": gw_down})
    gw_up, gc_ffn, gb_ffn, d_h2 = _ffn_up_bwd(up, d_act, behind(flight_down[-1], c_ffn), h2, g_up)
    flight_up = reduce_start("up", {"ffn_w_up": gw_up})
    dx1, d_mix, d_pa, d_pb, d_ya, d_yb, d_gate, dg3, dg2 = _merge_bwd(
        dy, d_h2, x1, mix, behind(flight_up[-1], norm_ffn_pre), norm_mix_post, g_out, g_cb, g_lb, pa, pb, proj)
    gw_out, gw_cb, gw_lb = _grad_tn([(merged, d_mix), (y_a, d_pa), (y_b, d_pb)], 2 * CB, "merge_wgrads")
    flight_mix = reduce_start("mix", {"w_conv_branch": gw_cb, "w_lru_branch": gw_lb, "w_out": gw_out})
    d_conv, gc_short = _conv_mixer_bwd(proj, d_ya, behind(flight_mix[-1], c_short))
    d_lru, gw_a, gw_x, g_lru_small = _lru_bwd(proj, hl, decay, lru_kept, d_yb, c_lru, g_wa, g_wx, lru_lambda)
    early = [dg2, dg3, dg4, g_lru_small[4:5], g_lru_small[7:8], gb_ffn, gc_short, g_lru_small[0:4],
             g_lru_small[5:6], g_lru_small[6:7], gc_ffn, loss_part]
    flight_small = _small_start(_pack_rows(early), "small_start")
    gw_in = _in_proj_wgrad(h, d_conv, d_lru, d_gate)
    flight_in = reduce_start("in", {"lru_wa": gw_a, "lru_wx": gw_x, "w_in": gw_in})
    dx, dg1 = _in_proj_xgrad(d_conv, d_lru, d_gate, g_in, x2, dx1,
                             behind(flight_small[-1], behind(flight_in[-1], norm_mix_pre)))
    flight_late = _small_start(_pack_rows([dg1]), "small_start_late")

    moments ={"w_in": (m_w_in, v_w_in), "w_conv_branch": (m_w_conv_branch, v_w_conv_branch),
               "w_lru_branch": (m_w_lru_branch, v_w_lru_branch), "w_out": (m_w_out, v_w_out),
               "lru_wa": (m_lru_wa, v_lru_wa), "lru_wx": (m_lru_wx, v_lru_wx), "ffn_w_up": (m_ffn_w_up, v_ffn_w_up),
               "ffn_w_down": (m_ffn_w_down, v_ffn_w_down)}
    weights = {"w_in": w_in, "w_conv_branch": w_conv_branch, "w_lru_branch": w_lru_branch, "w_out": w_out,
               "lru_wa": lru_wa, "lru_wx": lru_wx, "ffn_w_up": ffn_w_up, "ffn_w_down": ffn_w_down}
    out_g, out_d, out_m, out_v = {}, {}, {}, {}

    after = flight_late[-1]
    for tag, (keys, send, recv, sums, lands, _) in (("down", flight_down), ("up", flight_up), ("mix", flight_mix),
                                                    ("in", flight_in)):
        sums, others = _exchange_chips_wait(send, recv, sums, lands, after, "reduce_chip_wait_" + tag)
        by_key = dict(zip(keys, zip(sums, others)))
        for shape in dict.fromkeys(shard_shapes[k] for k in keys):
            same = [k for k in keys if shard_shapes[k] == shape]
            results = _adam_large([weights[k] for k in same], [moments[k][0] for k in same], [moments[k][1] for k in same],
                                  [by_key[k][0] for k in same], [by_key[k][1] for k in same], "adam_" + same[0])
            for out, values in zip((out_g, out_d, out_m, out_v), results):
                out.update(zip(same, values))
        after = out_d[keys[-1]]

    total, total_late = _small_sum([_small_wait(*flight_small[:4], after, "small_wait"),
                                    _small_wait(*flight_late[:4], after, "small_wait_late")], me)
    sizes = [p.size for p in early]
    starts = _packed_starts(sizes)

    def piece(i, shape):
        if i == 0:
            return total_late.reshape(-1)[:D_MODEL].reshape(shape)
        return total[starts[i - 1]:starts[i]].reshape(-1)[:sizes[i - 1]].reshape(shape)

    loss = total[starts[11], 0]

    def col_shard(full, width):
        return lax.dynamic_slice_in_dim(full, me * width, width, axis=1)

    def head_shard(full):
        return lax.dynamic_slice_in_dim(full.reshape(N_HEADS, HEAD_DIM), me * shard_head, shard_head, axis=1)

    small_names = ["norm_mix_pre", "norm_mix_post", "norm_ffn_pre", "norm_ffn_post", "lru_conv_b", "lru_lambda",
                   "ffn_conv_b", "conv_short_w", "lru_conv_w", "lru_ba", "lru_bx", "ffn_conv_w"]
    small_g = [piece(0, (1, D_MODEL)), piece(1, (1, D_MODEL)), piece(2, (1, D_MODEL)), piece(3, (1, D_MODEL)),
               piece(4, (1, D_MODEL)), piece(5, (1, D_MODEL)), piece(6, (1, 2 * D_FF)),
               col_shard(piece(7, (3, D_MODEL)), LANES), col_shard(piece(8, (4, D_MODEL)), LANES),
               head_shard(piece(9, (1, D_MODEL))), head_shard(piece(10, (1, D_MODEL))),
               col_shard(piece(11, (3, 2 * D_FF)), shard_up)]
    small_w = [norm_mix_pre, norm_mix_post, norm_ffn_pre, norm_ffn_post, lru_conv_b, lru_lambda, ffn_conv_b,
               conv_short_w[0], lru_conv_w[0], lru_ba[0], lru_bx[0], ffn_conv_w[0]]
    small_m = [m_norm_mix_pre, m_norm_mix_post, m_norm_ffn_pre, m_norm_ffn_post, m_lru_conv_b, m_lru_lambda,
               m_ffn_conv_b, m_conv_short_w[0], m_lru_conv_w[0], m_lru_ba[0], m_lru_bx[0], m_ffn_conv_w[0]]
    small_v = [v_norm_mix_pre, v_norm_mix_post, v_norm_ffn_pre, v_norm_ffn_post, v_lru_conv_b, v_lru_lambda,
               v_ffn_conv_b, v_conv_short_w[0], v_lru_conv_w[0], v_lru_ba[0], v_lru_bx[0], v_ffn_conv_w[0]]
    s_d, s_m, s_v = _adam_small(small_w, small_g, small_m, small_v)
    for i, name in enumerate(small_names):
        shape = small_w[i].shape if i < 7 else (1,) + small_w[i].shape
        out_g[name] = small_g[i].reshape(shape)
        out_d[name], out_m[name], out_v[name] = s_d[i].reshape(shape), s_m[i].reshape(shape), s_v[i].reshape(shape)

    order = ["norm_mix_pre", "norm_mix_post", "norm_ffn_pre", "norm_ffn_post", "w_in", "conv_short_w", "w_conv_branch",
             "lru_conv_w", "lru_conv_b", "lru_wa", "lru_ba", "lru_wx", "lru_bx", "lru_lambda", "w_lru_branch", "w_out",
             "ffn_w_up", "ffn_conv_w", "ffn_conv_b", "ffn_w_down"]
    return (loss, dx.reshape(1, t, D_MODEL), *[out_g[k] for k in order], *[out_d[k] for k in order],
            *[out_m[k] for k in order], *[out_v[k] for k in order])
```

```python
import functools
import math

import jax
import jax.numpy as jnp
from jax import lax
from jax.experimental import pallas as pl
from jax.experimental.pallas import tpu as pltpu

F32 = jnp.float32
BF16 = jnp.bfloat16
MESH = pl.DeviceIdType.MESH

N_DEV = 8
D_MODEL = 1024
N_HEADS = 4
HEAD_DIM = D_MODEL // N_HEADS
D_FF = 3 * D_MODEL
IN_COLS = 7 * D_MODEL
LRU_C = 8.0
RMS_EPS = 1e-6
ADAM_LR = 0.001
ADAM_B1 = 0.9
ADAM_B2 = 0.999
ADAM_EPS = 1e-08
ADAM_WD = 0.01
ADAM_STEP = 10
GELU_K = math.sqrt(2.0 / math.pi)
GELU_C = 0.044715

LANES = 128
SUBLANES = 8
PAD = SUBLANES
VMEM_LIMIT = 56 * 1024 * 1024
CB = 256
ROW_SLICE = 32
SCAN_UNROLL = 8

HBM_SPEC = pl.BlockSpec(memory_space=pltpu.HBM)
SEM_SPEC = pl.BlockSpec(memory_space=pltpu.SEMAPHORE)
DATAFLOW_EFFECT = pltpu.SideEffectType.DATAFLOW_SIDE_EFFECTING
VMEM_SPEC = pl.BlockSpec(memory_space=pltpu.VMEM)


def _params(*sem):
    if sem:
        return pltpu.CompilerParams(dimension_semantics=sem, vmem_limit_bytes=VMEM_LIMIT)
    return pltpu.CompilerParams(vmem_limit_bytes=VMEM_LIMIT)


def _row_chunk(t):
    return min(256, t)


def _row_block(rows, cap):
    return next(rb for rb in range(min(cap, rows), 0, -16) if rows % rb == 0)


def _gelu(x):
    return 0.5 * x * (1.0 + jnp.tanh(GELU_K * (x + GELU_C * x * x * x)))


def _gelu_and_grad(x):
    t = jnp.tanh(GELU_K * (x + GELU_C * x * x * x))
    g = 0.5 * x * (1.0 + t)
    dg = 0.5 * (1.0 + t) + 0.5 * x * (1.0 - t * t) * GELU_K * (1.0 + 3.0 * GELU_C * x * x)
    return g, dg


def _expm1_neg(x):
    series = x * (1.0 + x * (0.5 + x * (1.0 / 6.0 + x * (1.0 / 24.0 + x * (1.0 / 120.0)))))
    return jnp.where(x > -0.05, series, jnp.exp(x) - 1.0)


def _log_sigmoid(x):
    return jnp.minimum(x, 0.0) - jnp.log1p(jnp.exp(-jnp.abs(x)))


def _dot(a, b):
    return jnp.dot(a, b, preferred_element_type=F32)


def _dot_nt(a, b):
    return lax.dot_general(a, b, (((1,), (1,)), ((), ())), preferred_element_type=F32)


def _dot_tn(a, b):
    return lax.dot_general(a, b, (((0,), (0,)), ((), ())), preferred_element_type=F32)


def _rms_fwd(x):
    r = lax.rsqrt(jnp.mean(x * x, axis=-1, keepdims=True) + RMS_EPS)
    return x * r, r


def _rms_bwd(n, r, gdy):
    return r * (gdy - n * jnp.mean(n * gdy, axis=-1, keepdims=True))


def _rows_back(pad_ref, r0, rows, j):
    cur = pad_ref[pl.ds(PAD + r0, rows), :]
    if j == 0:
        return cur
    before = pad_ref[pl.ds(PAD + r0 - SUBLANES, SUBLANES), :]
    row = lax.broadcasted_iota(jnp.int32, before.shape, 0)
    rolled = pltpu.roll(cur, j, 0)
    top = jnp.where(row < j, pltpu.roll(before, j, 0), rolled[0:SUBLANES, :])
    return jnp.concatenate([top, rolled[SUBLANES:, :]], axis=0)


def _rows_ahead(pad_ref, r0, rows, j):
    cur = pad_ref[pl.ds(r0, rows), :]
    if j == 0:
        return cur
    after = pad_ref[pl.ds(r0 + rows, SUBLANES), :]
    row = lax.broadcasted_iota(jnp.int32, after.shape, 0)
    rolled = pltpu.roll(cur, rows - j, 0)
    bottom = jnp.where(row >= SUBLANES - j, pltpu.roll(after, SUBLANES - j, 0), rolled[rows - SUBLANES:, :])
    return jnp.concatenate([rolled[:rows - SUBLANES, :], bottom], axis=0)


def _fold_rows(v):
    return v.reshape(v.shape[0] // SUBLANES, SUBLANES, v.shape[1]).sum(axis=0)


def _conv_causal(pad_ref, w, r0, rows, taps):
    acc = None
    for k in range(taps):
        term = w[k:k + 1, :] * _rows_back(pad_ref, r0, rows, taps - 1 - k)
        acc = term if acc is None else acc + term
    return acc


def _conv_anticausal(pad_ref, w, r0, rows, taps):
    acc = None
    for k in range(taps):
        term = w[k:k + 1, :] * _rows_ahead(pad_ref, r0, rows, taps - 1 - k)
        acc = term if acc is None else acc + term
    return acc


def _position():
    return lax.axis_index("x"), lax.axis_index("y"), lax.axis_index("c")


def _block_of(x, y, c):
    return 4 * x + 2 * y + c


def _chip(x, y, k):
    return (x + (k & 1)) % 2, (y + (k >> 1)) % 2


def _cols(width):
    def at(ref, d, half=None):
        cols = pl.ds(pl.multiple_of(d * width, LANES), width)
        if half is None:
            return ref.at[:, cols]
        return ref.at[pl.ds(half * (ref.shape[0] // 2), ref.shape[0] // 2), cols]
    return at


def _rows(height):
    def at(ref, d, half=None):
        if half is None:
            return ref.at[pl.ds(pl.multiple_of(d * height, 16), height), :]
        return ref.at[pl.ds(pl.multiple_of(d * height + half * (height // 2), 16), height // 2), :]
    return at


def _lead(ref, d, half=None):
    if half is None:
        return ref.at[d]
    return ref.at[d, pl.ds(half * (ref.shape[1] // 2), ref.shape[1] // 2)]


def _gather_weights(shards, blocks, full_shapes, small, n_now, tokens, gain):
    n = len(shards)
    small_rows = small.shape[0]
    t = tokens.shape[0]
    rc = min(512, t)

    def body(*refs):
        ins, small_in, x_ref, g_ref = refs[:n], refs[n], refs[n + 1], refs[n + 2]
        outs, small_out, proj_ref, h_ref = refs[n + 3:2 * n + 3], refs[2 * n + 3], refs[2 * n + 4], refs[2 * n + 5]
        stage = refs[2 * n + 6:3 * n + 6]
        w_buf, p_buf, send, recv, local, w_sem, p_sem = refs[3 * n + 6:]
        x, y, c = _position()
        me = _block_of(x, y, c)
        sibling = (x, y, 1 - c)

        for a in range(n):
            stage[a][...] = ins[a][...].astype(BF16)
        for r0 in range(0, t, rc):
            normed, _ = _rms_fwd(x_ref[pl.ds(r0, rc), :])
            h_ref[pl.ds(r0, rc), :] = (normed * g_ref[...]).astype(BF16)
        stores = []

        def project(w_ref, block):
            i = len(stores)
            if i >= 2:
                stores[i - 2].wait()
            for r0 in range(0, t, rc):
                p_buf[i % 2, pl.ds(r0, rc), :] = _dot(h_ref[pl.ds(r0, rc), :], w_ref[...]).astype(BF16)
            st = pltpu.make_async_copy(p_buf.at[i % 2], blocks[0](proj_ref, block), p_sem.at[i % 2])
            st.start()
            stores.append(st)

        def project_landed(block):
            ld = pltpu.make_async_copy(blocks[0](outs[0], block), w_buf, w_sem)
            ld.start()
            ld.wait()
            project(w_buf, block)

        def copy(a, k, block, to, src=None, half=None):
            dst = blocks[a](outs[a], block, half)
            return pltpu.make_async_remote_copy(
                src_ref=dst if src is None else src, dst_ref=dst, send_sem=send.at[a, k], recv_sem=recv.at[a, k],
                device_id=to, device_id_type=MESH)

        def small_copy(k):
            px, py, pc = (x + (k & 1)) % 2, (y + ((k >> 1) & 1)) % 2, (c + (k >> 2)) % 2
            return pltpu.make_async_remote_copy(
                src_ref=small_in, dst_ref=small_out.at[me], send_sem=send.at[n_now, k - 1], recv_sem=recv.at[n_now, k - 1],
                device_id=(px, py, pc), device_id_type=MESH)

        def small_arrival(k):
            px, py, pc = (x + (k & 1)) % 2, (y + ((k >> 1) & 1)) % 2, (c + (k >> 2)) % 2
            return pltpu.make_async_remote_copy(
                src_ref=small_in, dst_ref=small_out.at[_block_of(px, py, pc)], send_sem=send.at[n_now, k - 1],
                recv_sem=recv.at[n_now, k - 1], device_id=(px, py, pc), device_id_type=MESH)

        small_out[me] = small_in[...]
        small_sends = [small_copy(k) for k in range(1, N_DEV)]
        for cp in small_sends:
            cp.start()

        mine, first, passed = [], [], []
        for a in range(n):
            own = pltpu.make_async_copy(stage[a], blocks[a](outs[a], me), local.at[a])
            own.start()
            mine.append(own)
            if a >= n_now:
                continue
            sends = [copy(a, 0, me, sibling, src=stage[a])]
            sends += [copy(a, k, me, (*_chip(x, y, k), c), src=stage[a]) for k in (1, 2)]
            for cp in sends:
                cp.start()
            first += sends

        here = (x, y, c)
        across = [(*_chip(x, y, k), c) for k in (1, 2)]
        near = [[_block_of(*_chip(x, y, k), cc) for k in (1, 2)] for cc in (c, 1 - c)]
        far = [_block_of(*_chip(x, y, 3), cc) for cc in (c, 1 - c)]

        def launch(cp):
            cp.start()
            passed.append(cp)

        project(stage[0], me)
        copy(0, 0, _block_of(x, y, 1 - c), here).wait_recv()
        project_landed(_block_of(x, y, 1 - c))
        for a in range(n_now):
            for i in (0, 1):
                copy(a, 1 + i, near[0][i], here).wait_recv()
                launch(copy(a, 3 + i, near[0][i], across[1 - i], half=i))
                launch(copy(a, 5 + i, near[0][i], sibling))
            if a == 0:
                project_landed(near[0][0])
                project_landed(near[0][1])
        for i in (0, 1):
            copy(0, 5 + i, near[1][i], here).wait_recv()
            project_landed(near[1][i])
        for a in range(n_now):
            for i in (0, 1):
                copy(a, 3 + i, far[0], here, half=i).wait_recv()
                launch(copy(a, 7 + i, far[0], sibling, half=i))
            if a == 0:
                project_landed(far[0])
        for a in range(n_now):
            if a > 0:
                copy(a, 0, _block_of(x, y, 1 - c), here).wait_recv()
                for i in (0, 1):
                    copy(a, 5 + i, near[1][i], here).wait_recv()
            for i in (0, 1):
                copy(a, 7 + i, far[1], here, half=i).wait_recv()
            if a == 0:
                project_landed(far[1])
        for k in range(1, N_DEV):
            small_arrival(k).wait_recv()
        for cp in first + passed + small_sends:
            cp.wait_send()
        for done in mine + stores[-2:]:
            done.wait()

    out_shape = [jax.ShapeDtypeStruct(s, BF16) for s in full_shapes]
    out_shape += [jax.ShapeDtypeStruct((N_DEV, small_rows, LANES), F32), jax.ShapeDtypeStruct((t, full_shapes[0][1]), BF16),
                  jax.ShapeDtypeStruct(tokens.shape, BF16)]
    return pl.pallas_call(
        body, name="gather_weights", out_shape=out_shape,
        in_specs=[VMEM_SPEC] * (n + 3), out_specs=[HBM_SPEC] * n + [VMEM_SPEC, HBM_SPEC, VMEM_SPEC],
        scratch_shapes=[pltpu.VMEM(s.shape, BF16) for s in shards]
        + [pltpu.VMEM(shards[0].shape, BF16), pltpu.VMEM((2, t, shards[0].shape[1]), BF16),
           pltpu.SemaphoreType.DMA((n_now + 1, 9)), pltpu.SemaphoreType.DMA((n_now + 1, 9)),
           pltpu.SemaphoreType.DMA((n,)), pltpu.SemaphoreType.DMA(()), pltpu.SemaphoreType.DMA((2,))],
        compiler_params=_params(),
    )(*shards, small, tokens, gain)


def _gather_first(full, blocks, send, recv):
    x, y, c = _position()
    me = _block_of(x, y, c)
    peers = [(x, y, 1 - c)] + [(*_chip(x, y, k), c) for k in (1, 2, 3)]

    def copy(a, k, block):
        at = blocks[a](full[a], block)
        return pltpu.make_async_remote_copy(src_ref=at, dst_ref=at, send_sem=send[4 * a + k], recv_sem=recv[4 * a + k],
                                            device_id=peers[k], device_id_type=MESH)

    sends = [copy(a, k, me) for a in range(len(full)) for k in range(4)]
    arrivals = [copy(a, k, _block_of(*peers[k])) for a in range(len(full)) for k in range(4)]
    return sends, arrivals


def _gather_second(full, blocks, send, recv):
    x, y, c = _position()

    def copy(a, k, cc):
        at = blocks[a](full[a], _block_of(*_chip(x, y, k), cc))
        return pltpu.make_async_remote_copy(src_ref=at, dst_ref=at, send_sem=send[3 * a + k - 1],
                                            recv_sem=recv[3 * a + k - 1], device_id=(x, y, 1 - c), device_id_type=MESH)

    sends = [copy(a, k, c) for a in range(len(full)) for k in (1, 2, 3)]
    arrivals = [copy(a, k, 1 - c) for a in range(len(full)) for k in (1, 2, 3)]
    return sends, arrivals


def _split_call(body, name, arrays, sems_in, n_sems_out, after=None, token=False):
    n, m = len(arrays), len(sems_in)

    def kernel_body(*refs):
        outs = refs[n + m + (after is not None):]
        body(refs[:n], refs[n:n + m], outs[:n_sems_out])
        if token:
            outs[-1][...] = jnp.zeros_like(outs[-1])

    extra_in = [] if after is None else [after]
    outs = pl.pallas_call(
        kernel_body, name=name,
        out_shape=(*[pltpu.SemaphoreType.DMA(())] * n_sems_out, *[pltpu.HBM(a.shape, a.dtype) for a in arrays],
                   *([jax.ShapeDtypeStruct((SUBLANES, LANES), F32)] if token else [])),
        in_specs=[HBM_SPEC] * n + [SEM_SPEC] * m + [pl.BlockSpec(memory_space=pl.ANY)] * len(extra_in),
        out_specs=(*[SEM_SPEC] * n_sems_out, *[HBM_SPEC] * n, *([VMEM_SPEC] if token else [])),
        input_output_aliases={i: n_sems_out + i for i in range(n)},
        compiler_params=pltpu.CompilerParams(has_side_effects=DATAFLOW_EFFECT),
    )(*[pltpu.with_memory_space_constraint(a, pltpu.HBM) for a in arrays], *sems_in, *extra_in)
    sems, rest = list(outs[:n_sems_out]), list(outs[n_sems_out:])
    return (sems, rest[:n], rest[n]) if token else (sems, rest[:n])


def _gather_start(full, blocks, name):
    n = len(full)

    def body(arrays, _, sems):
        for cp in _gather_first(arrays, blocks, sems[:4 * n], sems[4 * n:])[0]:
            cp.start()

    sems, arrays, token = _split_call(body, name, full, [], 8 * n, token=True)
    return sems[:4 * n], sems[4 * n:], arrays, token


def _gather_forward(full, blocks, send_first, recv_first, after, name):
    n = len(full)

    def body(arrays, sems_in, sems):
        sends, arrivals = _gather_first(arrays, blocks, sems_in[:4 * n], sems_in[4 * n:])
        for cp in arrivals:
            cp.wait_recv()
        for cp in _gather_second(arrays, blocks, sems[:3 * n], sems[3 * n:])[0]:
            cp.start()
        for cp in sends:
            cp.wait_send()

    sems, arrays = _split_call(body, name, full, [*send_first, *recv_first], 6 * n, after=after)
    return sems[:3 * n], sems[3 * n:], arrays


def _gather_finish(full, blocks, send_second, recv_second, after, name):
    n = len(full)

    def body(arrays, sems_in, _):
        sends, arrivals = _gather_second(arrays, blocks, sems_in[:3 * n], sems_in[3 * n:])
        for cp in sends:
            cp.wait_send()
        for cp in arrivals:
            cp.wait_recv()

    return _split_call(body, name, full, [*send_second, *recv_second], 0, after=after)[1]


def _reduce_pair(grads, blocks, shard_shapes, name):
    n = len(grads)

    def body(*refs):
        ins, outs = refs[:n], refs[n:2 * n]
        got, own = refs[2 * n:3 * n], refs[3 * n:4 * n]
        send, recv, local = refs[4 * n:]
        x, y, c = _position()
        copies, loads = [], []
        for a in range(n):
            for k in range(4):
                chip = _chip(x, y, k)
                cp = pltpu.make_async_remote_copy(
                    src_ref=blocks[a](ins[a], _block_of(*chip, 1 - c)), dst_ref=got[a].at[k],
                    send_sem=send.at[a, k], recv_sem=recv.at[a, k], device_id=(x, y, 1 - c), device_id_type=MESH)
                cp.start()
                copies.append(cp)
                ld = pltpu.make_async_copy(blocks[a](ins[a], _block_of(*chip, c)), own[a].at[k], local.at[a, k])
                ld.start()
                loads.append(ld)
        for a in range(n):
            for k in range(4):
                loads[4 * a + k].wait()
                copies[4 * a + k].wait_recv()
                outs[a][k] = (own[a][k].astype(F32) + got[a][k].astype(F32)).astype(BF16)
        for cp in copies:
            cp.wait_send()

    slots = [(4,) + tuple(s) for s in shard_shapes]
    return pl.pallas_call(
        body, name=name, out_shape=[jax.ShapeDtypeStruct(s, BF16) for s in slots],
        in_specs=[HBM_SPEC] * n, out_specs=[VMEM_SPEC] * n,
        scratch_shapes=[pltpu.VMEM(s, BF16) for s in slots] * 2
        + [pltpu.SemaphoreType.DMA((n, 4)), pltpu.SemaphoreType.DMA((n, 4)), pltpu.SemaphoreType.DMA((n, 4))],
        compiler_params=_params(),
    )(*grads)


def _chip_copies(sums, lands, send, recv):
    x, y, c = _position()
    return [pltpu.make_async_remote_copy(
        src_ref=sums[a].at[k], dst_ref=lands[a].at[k - 1], send_sem=send[3 * a + k - 1], recv_sem=recv[3 * a + k - 1],
        device_id=(*_chip(x, y, k), c), device_id_type=MESH) for a in range(len(sums)) for k in (1, 2, 3)]


def _exchange_chips_start(pair_sums, name):
    n = len(pair_sums)
    lands = [pltpu.with_memory_space_constraint(lax.empty((3,) + tuple(p.shape[1:]), BF16), pltpu.HBM) for p in pair_sums]

    def body(*refs):
        sums, zones = refs[:n], refs[n:2 * n]
        send, recv = refs[2 * n:5 * n], refs[5 * n:8 * n]
        token = refs[-1]
        for cp in _chip_copies(sums, zones, send, recv):
            cp.start()
        token[...] = jnp.zeros_like(token)

    outs = pl.pallas_call(
        body, name=name,
        out_shape=(*[pltpu.SemaphoreType.DMA(())] * (6 * n),
                   *[pltpu.HBM(p.shape, BF16) for p in pair_sums], *[pltpu.HBM(z.shape, BF16) for z in lands],
                   jax.ShapeDtypeStruct((SUBLANES, LANES), F32)),
        in_specs=[HBM_SPEC] * (2 * n), out_specs=(*[SEM_SPEC] * (6 * n), *[HBM_SPEC] * (2 * n), VMEM_SPEC),
        input_output_aliases={i: 6 * n + i for i in range(2 * n)},
        compiler_params=pltpu.CompilerParams(has_side_effects=DATAFLOW_EFFECT),
    )(*[pltpu.with_memory_space_constraint(p, pltpu.HBM) for p in pair_sums], *lands)
    return outs[:3 * n], outs[3 * n:6 * n], outs[6 * n:7 * n], outs[7 * n:8 * n], outs[-1]


def _exchange_chips_wait(send, recv, sums, lands, after, name):
    n = len(sums)

    def body(*refs):
        sums_in, zones = refs[:n], refs[n:2 * n]
        send_in, recv_in = refs[2 * n:5 * n], refs[5 * n:8 * n]
        for cp in _chip_copies(sums_in, zones, send_in, recv_in):
            cp.wait_send()
            cp.wait_recv()

    outs = pl.pallas_call(
        body, name=name,
        out_shape=(*[pltpu.HBM(p.shape, BF16) for p in sums], *[pltpu.HBM(z.shape, BF16) for z in lands]),
        in_specs=[HBM_SPEC] * (2 * n) + [SEM_SPEC] * (6 * n) + [pl.BlockSpec(memory_space=pl.ANY)],
        out_specs=[HBM_SPEC] * (2 * n), input_output_aliases={i: i for i in range(2 * n)},
        compiler_params=pltpu.CompilerParams(has_side_effects=DATAFLOW_EFFECT),
    )(*sums, *lands, *send, *recv, after)
    return outs[:n], outs[n:]


def _small_copies(mine, land, send, recv):
    x, y, c = _position()
    me = _block_of(x, y, c)

    def peer(k):
        return (x + (k & 1)) % 2, (y + ((k >> 1) & 1)) % 2, (c + (k >> 2)) % 2

    def copy(k, slot):
        return pltpu.make_async_remote_copy(src_ref=mine, dst_ref=land.at[slot], send_sem=send[k - 1], recv_sem=recv[k - 1],
                                            device_id=peer(k), device_id_type=MESH)

    return [copy(k, me) for k in range(1, N_DEV)], [copy(k, _block_of(*peer(k))) for k in range(1, N_DEV)]


def _small_start(part, name):
    land = jnp.zeros((N_DEV,) + part.shape, F32)

    def body(arrays, _, sems):
        for cp in _small_copies(arrays[0], arrays[1], sems[:7], sems[7:])[0]:
            cp.start()

    sems, arrays, token = _split_call(body, name, [part, land], [], 14, token=True)
    return sems[:7], sems[7:], arrays[0], arrays[1], token


def _small_wait(send, recv, part, land, after, name):
    def body(arrays, sems_in, _):
        sends, arrivals = _small_copies(arrays[0], arrays[1], sems_in[:7], sems_in[7:])
        for cp in sends:
            cp.wait_send()
        for cp in arrivals:
            cp.wait_recv()

    return _split_call(body, name, [part, land], [*send, *recv], 0, after=after)[1]


def _small_sum(pairs, me):
    n = len(pairs)

    def body(me_ref, *refs):
        for i in range(n):
            mine, land, out = refs[2 * i], refs[2 * i + 1], refs[2 * n + i]
            total = jnp.zeros(mine.shape, F32)
            for d in range(N_DEV):
                total = total + land[d] + jnp.where(me_ref[0] == d, mine[...], 0.0)
            out[...] = total

    flat = [a for pair in pairs for a in pair]
    return pl.pallas_call(
        body, name="small_sum", out_shape=[jax.ShapeDtypeStruct(mine.shape, F32) for mine, _ in pairs],
        in_specs=[pl.BlockSpec(memory_space=pltpu.SMEM)] + [VMEM_SPEC] * (2 * n), out_specs=[VMEM_SPEC] * n,
        compiler_params=_params(),
    )(me.reshape(1).astype(jnp.int32), *flat)


def _section(s, t):
    return pl.BlockSpec((t, CB), lambda h, s=s: (0, s * (D_MODEL // CB) + h))


def _conv_mixer_fwd(proj, w_short):
    t = proj.shape[0]
    rc = _row_chunk(t)

    def body(b_ref, c_ref, x_ref, w_ref, y_ref, pad):
        pad[pl.ds(0, PAD), :] = jnp.zeros((PAD, CB), F32)
        for r0 in range(0, t, rc):
            rows = pl.ds(r0, rc)
            pad[pl.ds(PAD + r0, rc), :] = c_ref[rows, :].astype(F32) * x_ref[rows, :].astype(F32)
        w = w_ref[...]
        for r0 in range(0, t, rc):
            rows = pl.ds(r0, rc)
            y_ref[rows, :] = (b_ref[rows, :].astype(F32) * _conv_causal(pad, w, r0, rc, 3)).astype(BF16)

    return pl.pallas_call(
        body, name="conv_mixer_fwd", grid=(D_MODEL // CB,),
        out_shape=jax.ShapeDtypeStruct((t, D_MODEL), BF16),
        in_specs=[_section(0, t), _section(1, t), _section(2, t), pl.BlockSpec((3, CB), lambda h: (0, h))],
        out_specs=pl.BlockSpec((t, CB), lambda h: (0, h)),
        scratch_shapes=[pltpu.VMEM((t + PAD, CB), F32)],
        compiler_params=_params("parallel"),
    )(proj, proj, proj, w_short)


def _lru_gates(xl, wa, ba, wx, bx, ls, first_row):
    xb = xl.astype(BF16)
    ra = jax.nn.sigmoid(_dot(xb, wa) + ba)
    ia = jax.nn.sigmoid(_dot(xb, wx) + bx)
    la = LRU_C * ra * ls
    a = jnp.exp(la)
    one_minus = -_expm1_neg(2.0 * la)
    mult = jnp.where(first_row, 1.0, jnp.sqrt(one_minus))
    return xb, ra, ia, a, one_minus, mult


def _head_specs():
    vec = pl.BlockSpec((1, CB), lambda h: (0, h))
    mat = pl.BlockSpec((N_DEV, None, HEAD_DIM // N_DEV, HEAD_DIM), lambda h: (0, h, 0, 0))
    return vec, mat


def _lru_fwd(proj, w_conv, b_conv, wa, ba, wx, bx, lam):
    t = proj.shape[0]
    rc = _row_chunk(t)
    vec, mat = _head_specs()

    def body(lx_ref, ly_ref, wc_ref, bc_ref, wa_ref, ba_ref, wx_ref, bx_ref, lam_ref, yb_ref, hl_ref, a_ref, kept_ref,
             pad, u_s):
        pad[pl.ds(0, PAD), :] = jnp.zeros((PAD, CB), F32)
        for r0 in range(0, t, rc):
            pad[pl.ds(PAD + r0, rc), :] = lx_ref[pl.ds(r0, rc), :].astype(F32)
        wc, bc = wc_ref[...], bc_ref[...]
        wa_m, wx_m = wa_ref[...].reshape(HEAD_DIM, HEAD_DIM), wx_ref[...].reshape(HEAD_DIM, HEAD_DIM)
        ls = _log_sigmoid(lam_ref[...])
        for r0 in range(0, t, rc):
            rows = pl.ds(r0, rc)
            xl = _conv_causal(pad, wc, r0, rc, 4) + bc
            first = (lax.broadcasted_iota(jnp.int32, (rc, CB), 0) + r0) == 0
            xb, ra, ia, a, _, mult = _lru_gates(xl, wa_m, ba_ref[...], wx_m, bx_ref[...], ls, first)
            a_ref[rows, :] = a
            u_s[rows, :] = mult * (ia * xl)
            kept_ref[0, rows, :] = xb
            kept_ref[1, rows, :] = ra.astype(BF16)
            kept_ref[2, rows, :] = ia.astype(BF16)

        row = lax.broadcasted_iota(jnp.int32, (SUBLANES, CB), 0)

        def group(g, carry):
            r = pl.multiple_of(g * SUBLANES, SUBLANES)
            a_g, b_g = a_ref[pl.ds(r, SUBLANES), :], u_s[pl.ds(r, SUBLANES), :]
            for s in (1, 2, 4):
                keep = row >= s
                b_g = jnp.where(keep, a_g * pltpu.roll(b_g, s, 0) + b_g, b_g)
                a_g = jnp.where(keep, a_g * pltpu.roll(a_g, s, 0), a_g)
            h_g = b_g + a_g * carry
            hl_ref[pl.ds(r, SUBLANES), :] = h_g
            return jnp.broadcast_to(h_g[SUBLANES - 1:SUBLANES, :], (SUBLANES, CB))

        def trip(i, carry):
            for j in range(SCAN_UNROLL):
                carry = group(i * SCAN_UNROLL + j, carry)
            return carry

        lax.fori_loop(0, t // SUBLANES // SCAN_UNROLL, trip, jnp.zeros((SUBLANES, CB), F32))
        for r0 in range(0, t, rc):
            rows = pl.ds(r0, rc)
            yb_ref[rows, :] = (hl_ref[rows, :] * _gelu(ly_ref[rows, :].astype(F32))).astype(BF16)

    blk = pl.BlockSpec((t, CB), lambda h: (0, h))
    res = jax.ShapeDtypeStruct((t, D_MODEL), F32)
    return pl.pallas_call(
        body, name="lru_fwd", grid=(N_HEADS,),
        out_shape=[jax.ShapeDtypeStruct((t, D_MODEL), BF16), res, res, jax.ShapeDtypeStruct((3, t, D_MODEL), BF16)],
        in_specs=[_section(3, t), _section(4, t), pl.BlockSpec((4, CB), lambda h: (0, h)), vec, mat, vec, mat, vec, vec],
        out_specs=[blk, blk, blk, pl.BlockSpec((3, t, CB), lambda h: (0, 0, h))],
        scratch_shapes=[pltpu.VMEM((t + PAD, CB), F32), pltpu.VMEM((t, CB), F32)],
        compiler_params=_params("parallel"),
    )(proj, proj, w_conv, b_conv, wa, ba, wx, bx, lam)


def _merge(y_a, y_b, proj, x, w_cb, w_lb, w_out, g2, g3):
    t = x.shape[0]
    tm = min(512, t)

    def body(ya_ref, yb_ref, gc_ref, gl_ref, x_ref, wcb_ref, wlb_ref, wo_ref, g2_ref, g3_ref,
             pa_ref, pb_ref, mg_ref, mix_ref, x1_ref, h2_ref):
        pa = _dot(ya_ref[...], wcb_ref[...]).astype(BF16)
        pb = _dot(yb_ref[...], wlb_ref[...]).astype(BF16)
        pa_ref[...] = pa
        pb_ref[...] = pb
        merged = (jax.nn.sigmoid(gc_ref[...].astype(F32)) * pa.astype(F32)
                  + jax.nn.sigmoid(gl_ref[...].astype(F32)) * pb.astype(F32)).astype(BF16)
        mg_ref[...] = merged
        mix = _dot(merged, wo_ref[...])
        mix_ref[...] = mix
        n2, _ = _rms_fwd(mix)
        x1 = x_ref[...] + n2 * g2_ref[...]
        x1_ref[...] = x1
        n3, _ = _rms_fwd(x1)
        h2_ref[...] = (n3 * g3_ref[...]).astype(BF16)

    row = pl.BlockSpec((tm, D_MODEL), lambda i: (i, 0))
    full = pl.BlockSpec((D_MODEL, D_MODEL), lambda i: (0, 0))
    vec = pl.BlockSpec((1, D_MODEL), lambda i: (0, 0))
    act = jax.ShapeDtypeStruct((t, D_MODEL), BF16)
    res = jax.ShapeDtypeStruct((t, D_MODEL), F32)
    return pl.pallas_call(
        body, name="merge_fwd", grid=(t // tm,), out_shape=[act, act, act, res, res, act],
        in_specs=[row, row, pl.BlockSpec((tm, D_MODEL), lambda i: (i, 5)), pl.BlockSpec((tm, D_MODEL), lambda i: (i, 6)),
                  row, full, full, full, vec, vec],
        out_specs=[row] * 6,
        compiler_params=_params("parallel"),
    )(y_a, y_b, proj, proj, x, w_cb, w_lb, w_out, g2, g3)


N_FF_BLOCKS = D_FF // CB
FFN_BWD_COLS = 512


def _ffn_up(h2, w_up, w_conv, b_conv):
    t = h2.shape[0]
    rc = _row_chunk(t)
    nb = N_FF_BLOCKS

    def body(h_ref, w_ref, c_ref, b_ref, up_ref, act_ref, f_ref, pad, gate):
        k = pl.program_id(1)
        pad[pl.ds(0, PAD), :] = jnp.zeros((PAD, CB), F32)
        for r0 in range(0, t, rc):
            rows = pl.ds(r0, rc)
            up = _dot(h_ref[rows, :], w_ref[...]).astype(BF16)
            up_ref[rows, :] = up
            pad[pl.ds(PAD + r0, rc), :] = up.astype(F32)
        def conv(keep_gate):
            cw = c_ref[...]
            for r0 in range(0, t, rc):
                rows = pl.ds(r0, rc)
                act = _conv_causal(pad, cw, r0, rc, 3) + b_ref[...]
                act_ref[rows, :] = act.astype(BF16)
                if keep_gate:
                    gate[rows, :] = act
                else:
                    f_ref[rows, :] = (_gelu(gate[rows, :]) * act).astype(BF16)

        @pl.when(k == 0)
        def _():
            conv(True)

        @pl.when(k == 1)
        def _():
            conv(False)

    half = lambda rows: pl.BlockSpec((rows, CB), lambda j, k: (0, nb * k + j))
    wide = jax.ShapeDtypeStruct((t, 2 * D_FF), BF16)
    return pl.pallas_call(
        body, name="ffn_up_fwd", grid=(nb, 2), out_shape=[wide, wide, jax.ShapeDtypeStruct((t, D_FF), BF16)],
        in_specs=[pl.BlockSpec((t, D_MODEL), lambda j, k: (0, 0)), half(D_MODEL), half(3), half(1)],
        out_specs=[half(t), half(t), pl.BlockSpec((t, CB), lambda j, k: (0, j))],
        scratch_shapes=[pltpu.VMEM((t + PAD, CB), F32), pltpu.VMEM((t, CB), F32)],
        compiler_params=_params("parallel", "arbitrary"),
    )(h2, w_up, w_conv, b_conv)


def _ffn_down(f, act, w_down, x1, target, g4):
    t = f.shape[0]
    tm = min(256, t)
    cc = 512

    def body(f_ref, act_ref, w_ref, x1_ref, tg_ref, g_ref, dy_ref, dout_ref, back_ref, dg_ref, loss_ref):
        @pl.when(pl.program_id(0) == 0)
        def _():
            dg_ref[...] = jnp.zeros_like(dg_ref)
            loss_ref[...] = jnp.zeros_like(loss_ref)
        out = _dot(f_ref[...], w_ref[...])
        n4, r4 = _rms_fwd(out)
        err = x1_ref[...] + n4 * g_ref[...] - tg_ref[...]
        loss_ref[...] += jnp.full(loss_ref.shape, 0.5 / D_MODEL, F32) * jnp.sum(err * err)
        dy = err * (1.0 / D_MODEL)
        dy_ref[...] = dy
        dg_ref[...] += jnp.sum(dy * n4, axis=0, keepdims=True)
        d_out = _rms_bwd(n4, r4, dy * g_ref[...]).astype(BF16)
        dout_ref[...] = d_out
        for c0 in range(0, D_FF, cc):
            d_f = _dot_nt(d_out, w_ref[pl.ds(c0, cc), :])
            gelu, d_gelu = _gelu_and_grad(act_ref[:, pl.ds(c0, cc)].astype(F32))
            val = act_ref[:, pl.ds(D_FF + c0, cc)].astype(F32)
            back_ref[:, pl.ds(c0, cc)] = (d_f * val * d_gelu).astype(BF16)
            back_ref[:, pl.ds(D_FF + c0, cc)] = (d_f * gelu).astype(BF16)

    row = pl.BlockSpec((tm, D_MODEL), lambda i: (i, 0))
    wide = pl.BlockSpec((tm, 2 * D_FF), lambda i: (i, 0))
    vec = pl.BlockSpec((1, D_MODEL), lambda i: (0, 0))
    return pl.pallas_call(
        body, name="ffn_down_fwd_bwd", grid=(t // tm,),
        out_shape=[jax.ShapeDtypeStruct((t, D_MODEL), F32), jax.ShapeDtypeStruct((t, D_MODEL), BF16),
                   jax.ShapeDtypeStruct((t, 2 * D_FF), BF16), jax.ShapeDtypeStruct((1, D_MODEL), F32),
                   jax.ShapeDtypeStruct((SUBLANES, LANES), F32)],
        in_specs=[pl.BlockSpec((tm, D_FF), lambda i: (i, 0)), wide, pl.BlockSpec((D_FF, D_MODEL), lambda i: (0, 0)),
                  row, row, vec],
        out_specs=[row, row, wide, vec, pl.BlockSpec((SUBLANES, LANES), lambda i: (0, 0))],
        compiler_params=_params("arbitrary"),
    )(f, act, w_down, x1, target, g4)


def _grad_tn(pairs, bm, name):
    k = len(pairs)
    t, m = pairs[0][0].shape
    n = pairs[0][1].shape[1]

    def body(*refs):
        for i in range(k):
            refs[2 * k + i][...] = _dot_tn(refs[2 * i][...], refs[2 * i + 1][...]).astype(BF16)

    return pl.pallas_call(
        body, name=name, grid=(m // bm,), out_shape=[jax.ShapeDtypeStruct((m, n), BF16)] * k,
        in_specs=[pl.BlockSpec((t, bm), lambda i: (0, i)), pl.BlockSpec((t, n), lambda i: (0, 0))] * k,
        out_specs=[pl.BlockSpec((bm, n), lambda i: (i, 0))] * k,
        compiler_params=_params("parallel"),
    )(*[x for pair in pairs for x in pair])


def _ffn_up_bwd(up, back, w_conv, h2, w_up):
    t = h2.shape[0]
    rc = _row_chunk(t)
    cb = FFN_BWD_COLS

    def body(up_ref, back_ref, c_ref, h_ref, w_ref, dw_ref, dcw_ref, dcb_ref, dh_ref, pad, after, d_up):
        @pl.when(pl.program_id(0) == 0)
        def _():
            dh_ref[...] = jnp.zeros_like(dh_ref)
        pad[pl.ds(0, PAD), :] = jnp.zeros((PAD, cb), F32)
        after[pl.ds(t, PAD), :] = jnp.zeros((PAD, cb), F32)
        for r0 in range(0, t, rc):
            pad[pl.ds(PAD + r0, rc), :] = up_ref[pl.ds(r0, rc), :].astype(F32)
            after[pl.ds(r0, rc), :] = back_ref[pl.ds(r0, rc), :].astype(F32)
        cw = c_ref[...]
        taps = [jnp.zeros((SUBLANES, cb), F32)] * 3
        bias = jnp.zeros((SUBLANES, cb), F32)
        for r0 in range(0, t, rc):
            for q0 in range(r0, r0 + rc, ROW_SLICE):
                rows = pl.ds(q0, ROW_SLICE)
                d_up[rows, :] = _conv_anticausal(after, cw, q0, ROW_SLICE, 3).astype(BF16)
                g = after[rows, :]
                taps = [acc + _fold_rows(g * _rows_back(pad, q0, ROW_SLICE, 2 - k)) for k, acc in enumerate(taps)]
                bias = bias + _fold_rows(g)
            rows = pl.ds(r0, rc)
            dh_ref[rows, :] += _dot_nt(d_up[rows, :], w_ref[...])
        dw_ref[...] = _dot_tn(h_ref[...], d_up[...]).astype(BF16)
        dcw_ref[...] = jnp.concatenate([jnp.sum(acc, axis=0, keepdims=True) for acc in taps], axis=0)
        dcb_ref[...] = jnp.sum(bias, axis=0, keepdims=True)

    cols = lambda rows: pl.BlockSpec((rows, cb), lambda j: (0, j))
    whole = pl.BlockSpec((t, D_MODEL), lambda j: (0, 0))
    return pl.pallas_call(
        body, name="ffn_up_bwd", grid=(2 * D_FF // cb,),
        out_shape=[jax.ShapeDtypeStruct((D_MODEL, 2 * D_FF), BF16), jax.ShapeDtypeStruct((3, 2 * D_FF), F32),
                   jax.ShapeDtypeStruct((1, 2 * D_FF), F32), jax.ShapeDtypeStruct((t, D_MODEL), F32)],
        in_specs=[cols(t), cols(t), cols(3), whole, cols(D_MODEL)],
        out_specs=[cols(D_MODEL), cols(3), cols(1), whole],
        scratch_shapes=[pltpu.VMEM((t + PAD, cb), F32), pltpu.VMEM((t + PAD, cb), F32), pltpu.VMEM((t, cb), BF16)],
        compiler_params=_params("arbitrary"),
    )(up, back, w_conv, h2, w_up)


def _merge_bwd(dy, d_h2, x1, mix, g3, g2, w_out, w_cb, w_lb, pa, pb, proj):
    t = dy.shape[0]
    tm = min(256, t)

    def body(dy_ref, dh2_ref, x1_ref, mix_ref, g3_ref, g2_ref, wo_ref, wcb_ref, wlb_ref, pa_ref, pb_ref, gc_ref, gl_ref,
             dx1_ref, dmix_ref, dpa_ref, dpb_ref, dya_ref, dyb_ref, dgate_ref, dg3_ref, dg2_ref):
        @pl.when(pl.program_id(0) == 0)
        def _():
            dg3_ref[...] = jnp.zeros_like(dg3_ref)
            dg2_ref[...] = jnp.zeros_like(dg2_ref)
        n3, r3 = _rms_fwd(x1_ref[...])
        d_h2 = dh2_ref[...]
        dg3_ref[...] += jnp.sum(d_h2 * n3, axis=0, keepdims=True)
        dx1 = dy_ref[...] + _rms_bwd(n3, r3, d_h2 * g3_ref[...])
        dx1_ref[...] = dx1
        n2, r2 = _rms_fwd(mix_ref[...])
        dg2_ref[...] += jnp.sum(dx1 * n2, axis=0, keepdims=True)
        d_mix = _rms_bwd(n2, r2, dx1 * g2_ref[...]).astype(BF16)
        dmix_ref[...] = d_mix
        d_merged = _dot_nt(d_mix, wo_ref[...])
        sc = jax.nn.sigmoid(gc_ref[...].astype(F32))
        sl = jax.nn.sigmoid(gl_ref[...].astype(F32))
        d_pa = (d_merged * sc).astype(BF16)
        d_pb = (d_merged * sl).astype(BF16)
        dpa_ref[...] = d_pa
        dpb_ref[...] = d_pb
        dgate_ref[0] = (d_merged * pa_ref[...].astype(F32) * sc * (1.0 - sc)).astype(BF16)
        dgate_ref[1] = (d_merged * pb_ref[...].astype(F32) * sl * (1.0 - sl)).astype(BF16)
        dya_ref[...] = _dot_nt(d_pa, wcb_ref[...]).astype(BF16)
        dyb_ref[...] = _dot_nt(d_pb, wlb_ref[...]).astype(BF16)

    row = pl.BlockSpec((tm, D_MODEL), lambda i: (i, 0))
    full = pl.BlockSpec((D_MODEL, D_MODEL), lambda i: (0, 0))
    vec = pl.BlockSpec((1, D_MODEL), lambda i: (0, 0))
    act = jax.ShapeDtypeStruct((t, D_MODEL), BF16)
    small = jax.ShapeDtypeStruct((1, D_MODEL), F32)
    return pl.pallas_call(
        body, name="merge_bwd", grid=(t // tm,),
        out_shape=[jax.ShapeDtypeStruct((t, D_MODEL), F32), act, act, act, act, act,
                   jax.ShapeDtypeStruct((2, t, D_MODEL), BF16), small, small],
        in_specs=[row, row, row, row, vec, vec, full, full, full, row, row,
                  pl.BlockSpec((tm, D_MODEL), lambda i: (i, 5)), pl.BlockSpec((tm, D_MODEL), lambda i: (i, 6))],
        out_specs=[row] * 6 + [pl.BlockSpec((2, tm, D_MODEL), lambda i: (0, i, 0)), vec, vec],
        compiler_params=_params("arbitrary"),
    )(dy, d_h2, x1, mix, g3, g2, w_out, w_cb, w_lb, pa, pb, proj, proj)


def _conv_mixer_bwd(proj, d_ya, w_short):
    t = proj.shape[0]
    rc = _row_chunk(t)

    def body(b_ref, c_ref, x_ref, dy_ref, w_ref, d_ref, dw_ref, pad, back):
        pad[pl.ds(0, PAD), :] = jnp.zeros((PAD, CB), F32)
        back[pl.ds(t, PAD), :] = jnp.zeros((PAD, CB), F32)
        for r0 in range(0, t, rc):
            rows = pl.ds(r0, rc)
            pad[pl.ds(PAD + r0, rc), :] = c_ref[rows, :].astype(F32) * x_ref[rows, :].astype(F32)
        w = w_ref[...]
        for r0 in range(0, t, ROW_SLICE):
            rows = pl.ds(r0, ROW_SLICE)
            d_y = dy_ref[rows, :].astype(F32)
            d_ref[0, rows, :] = (d_y * _conv_causal(pad, w, r0, ROW_SLICE, 3)).astype(BF16)
            back[rows, :] = d_y * b_ref[rows, :].astype(F32)
        taps = [jnp.zeros((SUBLANES, CB), F32)] * 3
        for r0 in range(0, t, ROW_SLICE):
            rows = pl.ds(r0, ROW_SLICE)
            d_u = _conv_anticausal(back, w, r0, ROW_SLICE, 3)
            d_ref[1, rows, :] = (d_u * x_ref[rows, :].astype(F32)).astype(BF16)
            d_ref[2, rows, :] = (d_u * c_ref[rows, :].astype(F32)).astype(BF16)
            g = back[rows, :]
            taps = [acc + _fold_rows(g * _rows_back(pad, r0, ROW_SLICE, 2 - k)) for k, acc in enumerate(taps)]
        dw_ref[...] = jnp.concatenate([jnp.sum(acc, axis=0, keepdims=True) for acc in taps], axis=0)

    blk = pl.BlockSpec((t, CB), lambda h: (0, h))
    return pl.pallas_call(
        body, name="conv_mixer_bwd", grid=(D_MODEL // CB,),
        out_shape=[jax.ShapeDtypeStruct((3, t, D_MODEL), BF16), jax.ShapeDtypeStruct((3, D_MODEL), F32)],
        in_specs=[_section(0, t), _section(1, t), _section(2, t), blk, pl.BlockSpec((3, CB), lambda h: (0, h))],
        out_specs=[pl.BlockSpec((3, t, CB), lambda h: (0, 0, h)), pl.BlockSpec((3, CB), lambda h: (0, h))],
        scratch_shapes=[pltpu.VMEM((t + PAD, CB), F32), pltpu.VMEM((t + PAD, CB), F32)],
        compiler_params=_params("parallel"),
    )(proj, proj, proj, d_ya, w_short)


LRU_SMALL_ROWS = 8


def _lru_bwd(proj, hl, a_all, kept, d_yb, w_conv, wa, wx, lam):
    t = proj.shape[0]
    rc = _row_chunk(t)
    vec, mat = _head_specs()

    def body(lx_ref, ly_ref, hl_ref, a_ref, kept_ref, dy_ref, wc_ref, wa_ref, wx_ref, lam_ref,
             d_ref, dwa_ref, dwx_ref, small_ref, pad, a_next, dh_s, dh_o, h_prev, back, acc_a, acc_x, dz_a, dz_x):
        zeros = jnp.zeros((PAD, CB), F32)
        pad[pl.ds(0, PAD), :] = zeros
        h_prev[pl.ds(0, PAD), :] = zeros
        a_next[pl.ds(t, PAD), :] = zeros
        back[pl.ds(t, PAD), :] = zeros
        for r0 in range(0, t, ROW_SLICE):
            rows = pl.ds(r0, ROW_SLICE)
            pad[pl.ds(PAD + r0, ROW_SLICE), :] = lx_ref[rows, :].astype(F32)
            h_prev[pl.ds(PAD + r0, ROW_SLICE), :] = hl_ref[rows, :]
            a_next[pl.ds(PAD - 1 + r0, ROW_SLICE), :] = a_ref[rows, :]
            act, d_act = _gelu_and_grad(ly_ref[rows, :].astype(F32))
            d_y = dy_ref[rows, :].astype(F32)
            dh_s[rows, :] = d_y * act
            d_ref[1, rows, :] = (d_y * hl_ref[rows, :] * d_act).astype(BF16)
        wc = wc_ref[...]
        wa_m, wx_m = wa_ref[...].reshape(HEAD_DIM, HEAD_DIM), wx_ref[...].reshape(HEAD_DIM, HEAD_DIM)
        ls = _log_sigmoid(lam_ref[...])

        row = lax.broadcasted_iota(jnp.int32, (SUBLANES, CB), 0)
        groups = t // SUBLANES

        def group(i, carry):
            r = pl.multiple_of((groups - 1 - i) * SUBLANES, SUBLANES)
            a_g, b_g = a_next[pl.ds(PAD + r, SUBLANES), :], dh_s[pl.ds(r, SUBLANES), :]
            for s in (1, 2, 4):
                keep = row < SUBLANES - s
                b_g = jnp.where(keep, a_g * pltpu.roll(b_g, SUBLANES - s, 0) + b_g, b_g)
                a_g = jnp.where(keep, a_g * pltpu.roll(a_g, SUBLANES - s, 0), a_g)
            d_g = b_g + a_g * carry
            dh_o[pl.ds(r, SUBLANES), :] = d_g
            return jnp.broadcast_to(d_g[0:1, :], (SUBLANES, CB))

        def trip(i, carry):
            for j in range(SCAN_UNROLL):
                carry = group(i * SCAN_UNROLL + j, carry)
            return carry

        lax.fori_loop(0, groups // SCAN_UNROLL, trip, jnp.zeros((SUBLANES, CB), F32))

        acc_a[...] = jnp.zeros_like(acc_a)
        acc_x[...] = jnp.zeros_like(acc_x)
        d_ba = d_bx = d_ls = jnp.zeros((SUBLANES, CB), F32)
        for r0 in range(0, t, rc):
            for q0 in range(r0, r0 + rc, ROW_SLICE):
                rows, local = pl.ds(q0, ROW_SLICE), pl.ds(q0 - r0, ROW_SLICE)
                a = a_ref[rows, :]
                xl, ra, ia = (kept_ref[i, rows, :].astype(F32) for i in range(3))
                a_sq = a * a
                mult = jnp.sqrt(1.0 - a_sq)
                slope = -a_sq / mult
                if q0 == 0:
                    first = lax.broadcasted_iota(jnp.int32, (ROW_SLICE, CB), 0) == 0
                    mult, slope = jnp.where(first, 1.0, mult), jnp.where(first, 0.0, slope)
                d_h = dh_o[rows, :]
                d_la = d_h * _rows_back(h_prev, q0, ROW_SLICE, 1) * a + d_h * ia * xl * slope
                d_za = d_la * (LRU_C * ls) * ra * (1.0 - ra)
                d_zx = d_h * mult * xl * ia * (1.0 - ia)
                d_ls = d_ls + _fold_rows(d_la * ra)
                d_ba = d_ba + _fold_rows(d_za)
                d_bx = d_bx + _fold_rows(d_zx)
                dz_a[local, :] = d_za.astype(BF16)
                dz_x[local, :] = d_zx.astype(BF16)
                back[rows, :] = d_h * mult * ia
            rows = pl.ds(r0, rc)
            xb = kept_ref[0, rows, :]
            acc_a[...] += _dot_tn(xb, dz_a[...])
            acc_x[...] += _dot_tn(xb, dz_x[...])
            back[rows, :] += _dot_nt(dz_a[...], wa_m) + _dot_nt(dz_x[...], wx_m)
        taps = [jnp.zeros((SUBLANES, CB), F32)] * 4
        d_bc = jnp.zeros((SUBLANES, CB), F32)
        for q0 in range(0, t, ROW_SLICE):
            rows = pl.ds(q0, ROW_SLICE)
            d_ref[0, rows, :] = _conv_anticausal(back, wc, q0, ROW_SLICE, 4).astype(BF16)
            g = back[rows, :]
            taps = [acc + _fold_rows(g * _rows_back(pad, q0, ROW_SLICE, 3 - k)) for k, acc in enumerate(taps)]
            d_bc = d_bc + _fold_rows(g)
        d_lam = d_ls * LRU_C * jax.nn.sigmoid(-lam_ref[...])
        small_ref[...] = jnp.concatenate(
            [jnp.sum(v, axis=0, keepdims=True) for v in taps + [d_bc, d_ba, d_bx, d_lam]], axis=0)
        dwa_ref[...] = acc_a[...].reshape(N_DEV, HEAD_DIM // N_DEV, HEAD_DIM).astype(BF16)
        dwx_ref[...] = acc_x[...].reshape(N_DEV, HEAD_DIM // N_DEV, HEAD_DIM).astype(BF16)

    blk = pl.BlockSpec((t, CB), lambda h: (0, h))
    gate_grad = jax.ShapeDtypeStruct((N_DEV, N_HEADS, HEAD_DIM // N_DEV, HEAD_DIM), BF16)
    return pl.pallas_call(
        body, name="lru_bwd", grid=(N_HEADS,),
        out_shape=[jax.ShapeDtypeStruct((2, t, D_MODEL), BF16), gate_grad, gate_grad,
                   jax.ShapeDtypeStruct((LRU_SMALL_ROWS, D_MODEL), F32)],
        in_specs=[_section(3, t), _section(4, t), blk, blk, pl.BlockSpec((3, t, CB), lambda h: (0, 0, h)), blk,
                  pl.BlockSpec((4, CB), lambda h: (0, h)), mat, mat, vec],
        out_specs=[pl.BlockSpec((2, t, CB), lambda h: (0, 0, h)), mat, mat,
                   pl.BlockSpec((LRU_SMALL_ROWS, CB), lambda h: (0, h))],
        scratch_shapes=[pltpu.VMEM((t + PAD, CB), F32), pltpu.VMEM((t + PAD, CB), F32), pltpu.VMEM((t, CB), F32),
                        pltpu.VMEM((t, CB), F32), pltpu.VMEM((t + PAD, CB), F32), pltpu.VMEM((t + PAD, CB), F32),
                        pltpu.VMEM((HEAD_DIM, HEAD_DIM), F32), pltpu.VMEM((HEAD_DIM, HEAD_DIM), F32),
                        pltpu.VMEM((rc, CB), BF16), pltpu.VMEM((rc, CB), BF16)],
        compiler_params=_params("parallel"),
    )(proj, proj, hl, a_all, kept, d_yb, w_conv, wa, wx, lam)


def _stack_maps(halves):
    def conv(sec, part):
        return jnp.minimum(sec, 2), jnp.where(sec < 3, part, halves - 1)

    def lru(sec, part):
        return jnp.clip(sec - 3, 0, 1), jnp.where(sec < 3, 0, jnp.where(sec < 5, part, halves - 1))

    def gate(sec, part):
        return jnp.clip(sec - 5, 0, 1), jnp.where(sec < 5, 0, part)

    return conv, lru, gate


def _pick_stack(sec, refs, fn):
    @pl.when(sec < 3)
    def _():
        fn(refs[0])

    @pl.when((sec >= 3) & (sec < 5))
    def _():
        fn(refs[1])

    @pl.when(sec >= 5)
    def _():
        fn(refs[2])


def _in_proj_wgrad(h, d_conv, d_lru, d_gate):
    t = h.shape[0]
    halves, bn = 1, D_MODEL
    maps = _stack_maps(halves)

    def body(h_ref, dc_ref, dl_ref, dg_ref, o_ref):
        def emit(ref):
            o_ref[...] = _dot_tn(h_ref[...], ref[...]).astype(BF16)
        _pick_stack(pl.program_id(0) // halves, (dc_ref, dl_ref, dg_ref), emit)

    def spec(m):
        def index(s):
            stack, part = m(s // halves, s % halves)
            return stack, 0, part
        return pl.BlockSpec((None, t, bn), index)

    return pl.pallas_call(
        body, name="in_proj_wgrad", grid=(7 * halves,), out_shape=jax.ShapeDtypeStruct((D_MODEL, IN_COLS), BF16),
        in_specs=[pl.BlockSpec((t, D_MODEL), lambda s: (0, 0))] + [spec(m) for m in maps],
        out_specs=pl.BlockSpec((D_MODEL, bn), lambda s: (0, s)),
        compiler_params=_params("arbitrary"),
    )(h, d_conv, d_lru, d_gate)


def _in_proj_xgrad(d_conv, d_lru, d_gate, w_in, x, dx1, g1):
    t = x.shape[0]
    tm = min(1024, t)
    maps = _stack_maps(1)

    def body(dc_ref, dl_ref, dg_ref, w_ref, x_ref, dx1_ref, g_ref, dx_ref, dgain_ref, acc):
        i, s = pl.program_id(0), pl.program_id(1)

        @pl.when((i == 0) & (s == 0))
        def _():
            dgain_ref[...] = jnp.zeros_like(dgain_ref)

        @pl.when(s == 0)
        def _():
            acc[...] = jnp.zeros_like(acc)

        def add(ref):
            acc[...] += _dot_nt(ref[...], w_ref[...])
        _pick_stack(s, (dc_ref, dl_ref, dg_ref), add)

        @pl.when(s == 6)
        def _():
            n1, r1 = _rms_fwd(x_ref[...])
            d_h = acc[...]
            dgain_ref[...] += jnp.sum(d_h * n1, axis=0, keepdims=True)
            dx_ref[...] = dx1_ref[...] + _rms_bwd(n1, r1, d_h * g_ref[...])

    def spec(m):
        def index(i, s):
            return m(s, 0)[0], i, 0
        return pl.BlockSpec((None, tm, D_MODEL), index)

    row = pl.BlockSpec((tm, D_MODEL), lambda i, s: (i, 0))
    vec = pl.BlockSpec((1, D_MODEL), lambda i, s: (0, 0))
    return pl.pallas_call(
        body, name="in_proj_xgrad", grid=(t // tm, 7),
        out_shape=[jax.ShapeDtypeStruct((t, D_MODEL), F32), jax.ShapeDtypeStruct((1, D_MODEL), F32)],
        in_specs=[spec(m) for m in maps] + [pl.BlockSpec((D_MODEL, D_MODEL), lambda i, s: (0, s)), row, row, vec],
        out_specs=[row, vec],
        scratch_shapes=[pltpu.VMEM((tm, D_MODEL), F32)],
        compiler_params=_params("arbitrary", "arbitrary"),
    )(d_conv, d_lru, d_gate, w_in, x, dx1, g1)


def _adamw(w, g, m, v):
    m = ADAM_B1 * m + (1.0 - ADAM_B1) * g
    v = ADAM_B2 * v + (1.0 - ADAM_B2) * (g * g)
    m_hat = m / (1.0 - ADAM_B1 ** ADAM_STEP)
    v_hat = v / (1.0 - ADAM_B2 ** ADAM_STEP)
    return -ADAM_LR * (m_hat / (jnp.sqrt(v_hat) + ADAM_EPS) + ADAM_WD * w), m, v


def _adam_large(ws, ms, vs, owns, others, name):
    n = len(ws)
    shape = ws[0].shape
    cols = shape[-1]
    flat = [[a.reshape(-1, cols) for a in group] for group in (ws, ms, vs)]
    rows = flat[0][0].shape[0]
    owns, others = [o.reshape(4, rows, cols) for o in owns], [o.reshape(3, rows, cols) for o in others]
    rb = _row_block(rows, 512)

    def body(*refs):
        ins, outs = refs[:5 * n], refs[5 * n:]
        for i in range(n):
            w_ref, m_ref, v_ref, own_ref, oth_ref = ins[i::n]
            g = own_ref[...].astype(F32)
            for k in range(3):
                g = g + oth_ref[k].astype(F32)
            outs[i][...] = g
            outs[n + i][...], outs[2 * n + i][...], outs[3 * n + i][...] = _adamw(w_ref[...], g, m_ref[...], v_ref[...])

    blk = pl.BlockSpec((rb, cols), lambda i: (i, 0))
    res = jax.ShapeDtypeStruct((rows, cols), F32)
    outs = pl.pallas_call(
        body, name=name, grid=(rows // rb,), out_shape=[res] * (4 * n),
        in_specs=[blk] * (3 * n) + [pl.BlockSpec((None, rb, cols), lambda i: (0, i, 0))] * n
        + [pl.BlockSpec((3, rb, cols), lambda i: (0, i, 0))] * n,
        out_specs=[blk] * (4 * n), compiler_params=_params("parallel"),
    )(*flat[0], *flat[1], *flat[2], *owns, *others)
    outs = [o.reshape(shape) for o in outs]
    return outs[:n], outs[n:2 * n], outs[2 * n:3 * n], outs[3 * n:]


def _adam_small(ws, gs, ms, vs):
    n = len(ws)

    def body(*refs):
        w_refs, g_refs, m_refs, v_refs = (refs[i * n:(i + 1) * n] for i in range(4))
        outs = refs[4 * n:]
        for i in range(n):
            d, m, v = _adamw(w_refs[i][...], g_refs[i][...], m_refs[i][...], v_refs[i][...])
            outs[i][...], outs[n + i][...], outs[2 * n + i][...] = d, m, v

    shapes = [jax.ShapeDtypeStruct(w.shape, F32) for w in ws]
    outs = pl.pallas_call(
        body, name="adam_small", out_shape=shapes * 3,
        in_specs=[VMEM_SPEC] * (4 * n), out_specs=[VMEM_SPEC] * (3 * n), compiler_params=_params(),
    )(*ws, *gs, *ms, *vs)
    return outs[:n], outs[n:2 * n], outs[2 * n:]


def _pack_rows(pieces):
    tile = SUBLANES * LANES
    return jnp.concatenate([jnp.pad(p.reshape(-1), (0, (-p.size) % tile)).reshape(-1, LANES) for p in pieces], axis=0)


def _packed_starts(sizes):
    tile = SUBLANES * LANES
    starts = [0]
    for s in sizes:
        starts.append(starts[-1] + (s + tile - 1) // tile * SUBLANES)
    return starts


def kernel(x, norm_mix_pre, norm_mix_post, norm_ffn_pre, norm_ffn_post, w_in, conv_short_w, w_conv_branch, lru_conv_w, lru_conv_b, lru_wa, lru_ba, lru_wx, lru_bx, lru_lambda, w_lru_branch, w_out, ffn_w_up, ffn_conv_w, ffn_conv_b, ffn_w_down, loss_target, m_norm_mix_pre, m_norm_mix_post, m_norm_ffn_pre, m_norm_ffn_post, m_w_in, m_conv_short_w, m_w_conv_branch, m_lru_conv_w, m_lru_conv_b, m_lru_wa, m_lru_ba, m_lru_wx, m_lru_bx, m_lru_lambda, m_w_lru_branch, m_w_out, m_ffn_w_up, m_ffn_conv_w, m_ffn_conv_b, m_ffn_w_down, v_norm_mix_pre, v_norm_mix_post, v_norm_ffn_pre, v_norm_ffn_post, v_w_in, v_conv_short_w, v_w_conv_branch, v_lru_conv_w, v_lru_conv_b, v_lru_wa, v_lru_ba, v_lru_wx, v_lru_bx, v_lru_lambda, v_w_lru_branch, v_w_out, v_ffn_w_up, v_ffn_conv_w, v_ffn_conv_b, v_ffn_w_down):
    t = x.shape[1]
    xi, yi, ci = _position()
    me = _block_of(xi, yi, ci)
    x2, target = x[0], loss_target[0]
    shard_in, shard_up = IN_COLS // N_DEV, 2 * D_FF // N_DEV
    shard_sq, shard_down, shard_head = D_MODEL // N_DEV, D_FF // N_DEV, HEAD_DIM // N_DEV

    names = ["w_in", "lru_wa", "lru_wx", "w_conv_branch", "w_lru_branch", "w_out", "ffn_w_up", "ffn_w_down"]
    large = [w_in[0], lru_wa[0], lru_wx[0], w_conv_branch[0], w_lru_branch[0], w_out[0], ffn_w_up[0], ffn_w_down[0]]
    blocks = [_cols(shard_in), _lead, _lead, _rows(shard_sq), _rows(shard_sq), _rows(shard_sq),
              _cols(shard_up), _rows(shard_down)]
    gate_full = (N_DEV, N_HEADS, shard_head, HEAD_DIM)
    full_shapes = [(D_MODEL, IN_COLS), gate_full, gate_full, (D_MODEL, D_MODEL), (D_MODEL, D_MODEL), (D_MODEL, D_MODEL),
                   (D_MODEL, 2 * D_FF), (D_FF, D_MODEL)]
    n_now = 3
    small_sharded = [conv_short_w, lru_conv_w, lru_ba, lru_bx, ffn_conv_w]
    small_mine = _pack_rows(small_sharded)
    small_at = _packed_starts([p.size for p in small_sharded])
    *gathered, small_all, proj, h = _gather_weights(large, blocks, full_shapes, small_mine, n_now, x2, norm_mix_pre)
    g_in, g_wa, g_wx = gathered[:n_now]
    later_blocks = blocks[n_now:]
    send1, recv1, later, gather_token = _gather_start(gathered[n_now:], later_blocks, "gather_start")

    def behind(token, operand):
        return operand + token[0:1, 0:1]

    def forward(lo, hi, after, tag):
        return _gather_forward(later[lo:hi], later_blocks[lo:hi], send1[4 * lo:4 * hi], recv1[4 * lo:4 * hi], after,
                               "gather_forward_" + tag)

    def finish(lo, hi, flight, after, tag):
        return _gather_finish(flight[2], later_blocks[lo:hi], flight[0], flight[1], after, "gather_finish_" + tag)

    def cols_of(r0, n, width):
        part = small_all[:, r0:r0 + n * width // LANES, :].reshape(N_DEV, n, width)
        return part.transpose(1, 0, 2).reshape(n, N_DEV * width)

    c_short = cols_of(small_at[0], 3, LANES)
    c_lru = cols_of(small_at[1], 4, LANES)
    b_a = cols_of(small_at[2], N_HEADS, shard_head).reshape(1, D_MODEL)
    b_x = cols_of(small_at[3], N_HEADS, shard_head).reshape(1, D_MODEL)
    c_ffn = cols_of(small_at[4], 3, shard_up)

    y_a = _conv_mixer_fwd(proj, behind(gather_token, c_short))
    y_b, hl, decay, lru_kept = _lru_fwd(proj, behind(gather_token, c_lru), lru_conv_b, g_wa, b_a, g_wx, b_x, lru_lambda)
    flight_mix_w = forward(0, 3, y_b, "mix")
    g_cb, g_lb, g_out = finish(0, 3, flight_mix_w, y_b, "mix")
    pa, pb, merged, mix, x1, h2 = _merge(y_a, y_b, proj, x2, g_cb, g_lb, g_out, norm_mix_post, norm_ffn_pre)
    flight_up_w = forward(3, 4, h2, "up")
    (g_up,) = finish(3, 4, flight_up_w, h2, "up")
    up, act, f = _ffn_up(h2, g_up, c_ffn, ffn_conv_b)
    flight_down_w = forward(4, 5, f, "down")
    (g_down,) = finish(4, 5, flight_down_w, f, "down")
    dy, d_out, d_act, dg4, loss_part = _ffn_down(f, act, g_down, x1, target, norm_ffn_post)

    block_of = dict(zip(names, blocks))
    shard_shapes = {"w_in": (D_MODEL, shard_in), "w_conv_branch": (shard_sq, D_MODEL), "w_lru_branch": (shard_sq, D_MODEL),
                    "w_out": (shard_sq, D_MODEL), "lru_wa": (N_HEADS, shard_head, HEAD_DIM),
                    "lru_wx": (N_HEADS, shard_head, HEAD_DIM), "ffn_w_up": (D_MODEL, shard_up),
                    "ffn_w_down": (shard_down, D_MODEL)}

    def reduce_start(tag, grads):
        keys = list(grads)
        sums = _reduce_pair([grads[k] for k in keys], [block_of[k] for k in keys], [shard_shapes[k] for k in keys],
                            "reduce_pair_" + tag)
        return (keys,) + _exchange_chips_start(sums, "reduce_chip_start_" + tag)

    (gw_down,) = _grad_tn([(f, d_out)], min(1024, D_FF), "ffn_down_wgrad")
    flight_down = reduce_start("down", {"ffn_w_down": gw_down})
    gw_up, gc_ffn, gb_ffn, d_h2 = _ffn_up_bwd(up, d_act, behind(flight_down[-1], c_ffn), h2, g_up)
    flight_up = reduce_start("up", {"ffn_w_up": gw_up})
    dx1, d_mix, d_pa, d_pb, d_ya, d_yb, d_gate, dg3, dg2 = _merge_bwd(
        dy, d_h2, x1, mix, behind(flight_up[-1], norm_ffn_pre), norm_mix_post, g_out, g_cb, g_lb, pa, pb, proj)
    gw_out, gw_cb, gw_lb = _grad_tn([(merged, d_mix), (y_a, d_pa), (y_b, d_pb)], 2 * CB, "merge_wgrads")
    flight_mix = reduce_start("mix", {"w_conv_branch": gw_cb, "w_lru_branch": gw_lb, "w_out": gw_out})
    d_conv, gc_short = _conv_mixer_bwd(proj, d_ya, behind(flight_mix[-1], c_short))
    d_lru, gw_a, gw_x, g_lru_small = _lru_bwd(proj, hl, decay, lru_kept, d_yb, c_lru, g_wa, g_wx, lru_lambda)
    early = [dg2, dg3, dg4, g_lru_small[4:5], g_lru_small[7:8], gb_ffn, gc_short, g_lru_small[0:4],
             g_lru_small[5:6], g_lru_small[6:7], gc_ffn, loss_part]
    flight_small = _small_start(_pack_rows(early), "small_start")
    gw_in = _in_proj_wgrad(h, d_conv, d_lru, d_gate)
    flight_in = reduce_start("in", {"lru_wa": gw_a, "lru_wx": gw_x, "w_in": gw_in})
    dx, dg1 = _in_proj_xgrad(d_conv, d_lru, d_gate, g_in, x2, dx1,
                             behind(flight_small[-1], behind(flight_in[-1], norm_mix_pre)))
    flight_late = _small_start(_pack_rows([dg1]), "small_start_late")

    moments ={"w_in": (m_w_in, v_w_in), "w_conv_branch": (m_w_conv_branch, v_w_conv_branch),
               "w_lru_branch": (m_w_lru_branch, v_w_lru_branch), "w_out": (m_w_out, v_w_out),
               "lru_wa": (m_lru_wa, v_lru_wa), "lru_wx": (m_lru_wx, v_lru_wx), "ffn_w_up": (m_ffn_w_up, v_ffn_w_up),
               "ffn_w_down": (m_ffn_w_down, v_ffn_w_down)}
    weights = {"w_in": w_in, "w_conv_branch": w_conv_branch, "w_lru_branch": w_lru_branch, "w_out": w_out,
               "lru_wa": lru_wa, "lru_wx": lru_wx, "ffn_w_up": ffn_w_up, "ffn_w_down": ffn_w_down}
    out_g, out_d, out_m, out_v = {}, {}, {}, {}

    after = flight_late[-1]
    for tag, (keys, send, recv, sums, lands, _) in (("down", flight_down), ("up", flight_up), ("mix", flight_mix),
                                                    ("in", flight_in)):
        sums, others = _exchange_chips_wait(send, recv, sums, lands, after, "reduce_chip_wait_" + tag)
        by_key = dict(zip(keys, zip(sums, others)))
        for shape in dict.fromkeys(shard_shapes[k] for k in keys):
            same = [k for k in keys if shard_shapes[k] == shape]
            results = _adam_large([weights[k] for k in same], [moments[k][0] for k in same], [moments[k][1] for k in same],
                                  [by_key[k][0] for k in same], [by_key[k][1] for k in same], "adam_" + same[0])
            for out, values in zip((out_g, out_d, out_m, out_v), results):
                out.update(zip(same, values))
        after = out_d[keys[-1]]

    total, total_late = _small_sum([_small_wait(*flight_small[:4], after, "small_wait"),
                                    _small_wait(*flight_late[:4], after, "small_wait_late")], me)
    sizes = [p.size for p in early]
    starts = _packed_starts(sizes)

    def piece(i, shape):
        if i == 0:
            return total_late.reshape(-1)[:D_MODEL].reshape(shape)
        return total[starts[i - 1]:starts[i]].reshape(-1)[:sizes[i - 1]].reshape(shape)

    loss = total[starts[11], 0]

    def col_shard(full, width):
        return lax.dynamic_slice_in_dim(full, me * width, width, axis=1)

    def head_shard(full):
        return lax.dynamic_slice_in_dim(full.reshape(N_HEADS, HEAD_DIM), me * shard_head, shard_head, axis=1)

    small_names = ["norm_mix_pre", "norm_mix_post", "norm_ffn_pre", "norm_ffn_post", "lru_conv_b", "lru_lambda",
                   "ffn_conv_b", "conv_short_w", "lru_conv_w", "lru_ba", "lru_bx", "ffn_conv_w"]
    small_g = [piece(0, (1, D_MODEL)), piece(1, (1, D_MODEL)), piece(2, (1, D_MODEL)), piece(3, (1, D_MODEL)),
               piece(4, (1, D_MODEL)), piece(5, (1, D_MODEL)), piece(6, (1, 2 * D_FF)),
               col_shard(piece(7, (3, D_MODEL)), LANES), col_shard(piece(8, (4, D_MODEL)), LANES),
               head_shard(piece(9, (1, D_MODEL))), head_shard(piece(10, (1, D_MODEL))),
               col_shard(piece(11, (3, 2 * D_FF)), shard_up)]
    small_w = [norm_mix_pre, norm_mix_post, norm_ffn_pre, norm_ffn_post, lru_conv_b, lru_lambda, ffn_conv_b,
               conv_short_w[0], lru_conv_w[0], lru_ba[0], lru_bx[0], ffn_conv_w[0]]
    small_m = [m_norm_mix_pre, m_norm_mix_post, m_norm_ffn_pre, m_norm_ffn_post, m_lru_conv_b, m_lru_lambda,
               m_ffn_conv_b, m_conv_short_w[0], m_lru_conv_w[0], m_lru_ba[0], m_lru_bx[0], m_ffn_conv_w[0]]
    small_v = [v_norm_mix_pre, v_norm_mix_post, v_norm_ffn_pre, v_norm_ffn_post, v_lru_conv_b, v_lru_lambda,
               v_ffn_conv_b, v_conv_short_w[0], v_lru_conv_w[0], v_lru_ba[0], v_lru_bx[0], v_ffn_conv_w[0]]
    s_d, s_m, s_v = _adam_small(small_w, small_g, small_m, small_v)
    for i, name in enumerate(small_names):
        shape = small_w[i].shape if i < 7 else (1,) + small_w[i].shape
        out_g[name] = small_g[i].reshape(shape)
        out_d[name], out_m[name], out_v[name] = s_d[i].reshape(shape), s_m[i].reshape(shape), s_v[i].reshape(shape)

    order = ["norm_mix_pre", "norm_mix_post", "norm_ffn_pre", "norm_ffn_post", "w_in", "conv_short_w", "w_conv_branch",
             "lru_conv_w", "lru_conv_b", "lru_wa", "lru_ba", "lru_wx", "lru_bx", "lru_lambda", "w_lru_branch", "w_out",
             "ffn_w_up", "ffn_conv_w", "ffn_conv_b", "ffn_w_down"]
    return (loss, dx.reshape(1, t, D_MODEL), *[out_g[k] for k in order], *[out_d[k] for k in order],
            *[out_m[k] for k in order], *[out_v[k] for k in order])
```

```python
import functools
import math

import jax
import jax.numpy as jnp
from jax import lax
from jax.experimental import pallas as pl
from jax.experimental.pallas import tpu as pltpu

F32 = jnp.float32
BF16 = jnp.bfloat16
MESH = pl.DeviceIdType.MESH

N_DEV = 8
D_MODEL = 1024
N_HEADS = 4
HEAD_DIM = D_MODEL // N_HEADS
D_FF = 3 * D_MODEL
IN_COLS = 7 * D_MODEL
LRU_C = 8.0
RMS_EPS = 1e-6
ADAM_LR = 0.001
ADAM_B1 = 0.9
ADAM_B2 = 0.999
ADAM_EPS = 1e-08
ADAM_WD = 0.01
ADAM_STEP = 10
GELU_K = math.sqrt(2.0 / math.pi)
GELU_C = 0.044715

LANES = 128
SUBLANES = 8
PAD = SUBLANES
VMEM_LIMIT = 56 * 1024 * 1024
CB = 256
ROW_SLICE = 32
SCAN_UNROLL = 16

HBM_SPEC = pl.BlockSpec(memory_space=pltpu.HBM)
SEM_SPEC = pl.BlockSpec(memory_space=pltpu.SEMAPHORE)
DATAFLOW_EFFECT = pltpu.SideEffectType.DATAFLOW_SIDE_EFFECTING
VMEM_SPEC = pl.BlockSpec(memory_space=pltpu.VMEM)


def _params(*sem):
    if sem:
        return pltpu.CompilerParams(dimension_semantics=sem, vmem_limit_bytes=VMEM_LIMIT)
    return pltpu.CompilerParams(vmem_limit_bytes=VMEM_LIMIT)


def _row_chunk(t):
    return min(256, t)


def _row_block(rows, cap):
    return next(rb for rb in range(min(cap, rows), 0, -16) if rows % rb == 0)


def _gelu(x):
    return 0.5 * x * (1.0 + jnp.tanh(GELU_K * (x + GELU_C * x * x * x)))


def _gelu_and_grad(x):
    t = jnp.tanh(GELU_K * (x + GELU_C * x * x * x))
    g = 0.5 * x * (1.0 + t)
    dg = 0.5 * (1.0 + t) + 0.5 * x * (1.0 - t * t) * GELU_K * (1.0 + 3.0 * GELU_C * x * x)
    return g, dg


def _expm1_neg(x):
    series = x * (1.0 + x * (0.5 + x * (1.0 / 6.0 + x * (1.0 / 24.0 + x * (1.0 / 120.0)))))
    return jnp.where(x > -0.05, series, jnp.exp(x) - 1.0)


def _log_sigmoid(x):
    return jnp.minimum(x, 0.0) - jnp.log1p(jnp.exp(-jnp.abs(x)))


def _dot(a, b):
    return jnp.dot(a, b, preferred_element_type=F32)


def _dot_nt(a, b):
    return lax.dot_general(a, b, (((1,), (1,)), ((), ())), preferred_element_type=F32)


def _dot_tn(a, b):
    return lax.dot_general(a, b, (((0,), (0,)), ((), ())), preferred_element_type=F32)


def _rms_fwd(x):
    r = lax.rsqrt(jnp.mean(x * x, axis=-1, keepdims=True) + RMS_EPS)
    return x * r, r


def _rms_bwd(n, r, gdy):
    return r * (gdy - n * jnp.mean(n * gdy, axis=-1, keepdims=True))


def _rows_back(pad_ref, r0, rows, j):
    cur = pad_ref[pl.ds(PAD + r0, rows), :]
    if j == 0:
        return cur
    before = pad_ref[pl.ds(PAD + r0 - SUBLANES, SUBLANES), :]
    row = lax.broadcasted_iota(jnp.int32, before.shape, 0)
    rolled = pltpu.roll(cur, j, 0)
    top = jnp.where(row < j, pltpu.roll(before, j, 0), rolled[0:SUBLANES, :])
    return jnp.concatenate([top, rolled[SUBLANES:, :]], axis=0)


def _rows_ahead(pad_ref, r0, rows, j):
    cur = pad_ref[pl.ds(r0, rows), :]
    if j == 0:
        return cur
    after = pad_ref[pl.ds(r0 + rows, SUBLANES), :]
    row = lax.broadcasted_iota(jnp.int32, after.shape, 0)
    rolled = pltpu.roll(cur, rows - j, 0)
    bottom = jnp.where(row >= SUBLANES - j, pltpu.roll(after, SUBLANES - j, 0), rolled[rows - SUBLANES:, :])
    return jnp.concatenate([rolled[:rows - SUBLANES, :], bottom], axis=0)


def _fold_rows(v):
    return v.reshape(v.shape[0] // SUBLANES, SUBLANES, v.shape[1]).sum(axis=0)


def _conv_causal(pad_ref, w, r0, rows, taps):
    acc = None
    for k in range(taps):
        term = w[k:k + 1, :] * _rows_back(pad_ref, r0, rows, taps - 1 - k)
        acc = term if acc is None else acc + term
    return acc


def _conv_anticausal(pad_ref, w, r0, rows, taps):
    acc = None
    for k in range(taps):
        term = w[k:k + 1, :] * _rows_ahead(pad_ref, r0, rows, taps - 1 - k)
        acc = term if acc is None else acc + term
    return acc


def _conv_wgrad(g, xpad_ref, r0, rows, taps):
    return [jnp.sum(g * _rows_back(xpad_ref, r0, rows, taps - 1 - k), axis=0, keepdims=True) for k in range(taps)]


def _position():
    return lax.axis_index("x"), lax.axis_index("y"), lax.axis_index("c")


def _block_of(x, y, c):
    return 4 * x + 2 * y + c


def _chip(x, y, k):
    return (x + (k & 1)) % 2, (y + (k >> 1)) % 2


def _cols(width):
    def at(ref, d, half=None):
        cols = pl.ds(pl.multiple_of(d * width, LANES), width)
        if half is None:
            return ref.at[:, cols]
        return ref.at[pl.ds(half * (ref.shape[0] // 2), ref.shape[0] // 2), cols]
    return at


def _rows(height):
    def at(ref, d, half=None):
        if half is None:
            return ref.at[pl.ds(pl.multiple_of(d * height, 16), height), :]
        return ref.at[pl.ds(pl.multiple_of(d * height + half * (height // 2), 16), height // 2), :]
    return at


def _lead(ref, d, half=None):
    if half is None:
        return ref.at[d]
    return ref.at[d, pl.ds(half * (ref.shape[1] // 2), ref.shape[1] // 2)]


def _gather_weights(shards, blocks, full_shapes, small, n_now, tokens, gain):
    n = len(shards)
    small_rows = small.shape[0]
    t = tokens.shape[0]
    rc = min(512, t)

    def body(*refs):
        ins, small_in, x_ref, g_ref = refs[:n], refs[n], refs[n + 1], refs[n + 2]
        outs, small_out, proj_ref, h_ref = refs[n + 3:2 * n + 3], refs[2 * n + 3], refs[2 * n + 4], refs[2 * n + 5]
        stage = refs[2 * n + 6:3 * n + 6]
        w_buf, p_buf, send, recv, local, w_sem, p_sem = refs[3 * n + 6:]
        x, y, c = _position()
        me = _block_of(x, y, c)
        sibling = (x, y, 1 - c)

        for a in range(n):
            stage[a][...] = ins[a][...].astype(BF16)
        for r0 in range(0, t, rc):
            normed, _ = _rms_fwd(x_ref[pl.ds(r0, rc), :])
            h_ref[pl.ds(r0, rc), :] = (normed * g_ref[...]).astype(BF16)
        stores = []

        def project(w_ref, block):
            i = len(stores)
            if i >= 2:
                stores[i - 2].wait()
            for r0 in range(0, t, rc):
                p_buf[i % 2, pl.ds(r0, rc), :] = _dot(h_ref[pl.ds(r0, rc), :], w_ref[...]).astype(BF16)
            st = pltpu.make_async_copy(p_buf.at[i % 2], blocks[0](proj_ref, block), p_sem.at[i % 2])
            st.start()
            stores.append(st)

        def project_landed(block):
            ld = pltpu.make_async_copy(blocks[0](outs[0], block), w_buf, w_sem)
            ld.start()
            ld.wait()
            project(w_buf, block)

        def copy(a, k, block, to, src=None, half=None):
            dst = blocks[a](outs[a], block, half)
            return pltpu.make_async_remote_copy(
                src_ref=dst if src is None else src, dst_ref=dst, send_sem=send.at[a, k], recv_sem=recv.at[a, k],
                device_id=to, device_id_type=MESH)

        def small_copy(k):
            px, py, pc = (x + (k & 1)) % 2, (y + ((k >> 1) & 1)) % 2, (c + (k >> 2)) % 2
            return pltpu.make_async_remote_copy(
                src_ref=small_in, dst_ref=small_out.at[me], send_sem=send.at[n_now, k - 1], recv_sem=recv.at[n_now, k - 1],
                device_id=(px, py, pc), device_id_type=MESH)

        def small_arrival(k):
            px, py, pc = (x + (k & 1)) % 2, (y + ((k >> 1) & 1)) % 2, (c + (k >> 2)) % 2
            return pltpu.make_async_remote_copy(
                src_ref=small_in, dst_ref=small_out.at[_block_of(px, py, pc)], send_sem=send.at[n_now, k - 1],
                recv_sem=recv.at[n_now, k - 1], device_id=(px, py, pc), device_id_type=MESH)

        small_out[me] = small_in[...]
        small_sends = [small_copy(k) for k in range(1, N_DEV)]
        for cp in small_sends:
            cp.start()

        mine, first, passed = [], [], []
        for a in range(n):
            own = pltpu.make_async_copy(stage[a], blocks[a](outs[a], me), local.at[a])
            own.start()
            mine.append(own)
            if a >= n_now:
                continue
            sends = [copy(a, 0, me, sibling, src=stage[a])]
            sends += [copy(a, k, me, (*_chip(x, y, k), c), src=stage[a]) for k in (1, 2)]
            for cp in sends:
                cp.start()
            first += sends

        here = (x, y, c)
        across = [(*_chip(x, y, k), c) for k in (1, 2)]
        near = [[_block_of(*_chip(x, y, k), cc) for k in (1, 2)] for cc in (c, 1 - c)]
        far = [_block_of(*_chip(x, y, 3), cc) for cc in (c, 1 - c)]

        def launch(cp):
            cp.start()
            passed.append(cp)

        project(stage[0], me)
        copy(0, 0, _block_of(x, y, 1 - c), here).wait_recv()
        project_landed(_block_of(x, y, 1 - c))
        for a in range(n_now):
            for i in (0, 1):
                copy(a, 1 + i, near[0][i], here).wait_recv()
                launch(copy(a, 3 + i, near[0][i], across[1 - i], half=i))
                launch(copy(a, 5 + i, near[0][i], sibling))
            if a == 0:
                project_landed(near[0][0])
                project_landed(near[0][1])
        for i in (0, 1):
            copy(0, 5 + i, near[1][i], here).wait_recv()
            project_landed(near[1][i])
        for a in range(n_now):
            for i in (0, 1):
                copy(a, 3 + i, far[0], here, half=i).wait_recv()
                launch(copy(a, 7 + i, far[0], sibling, half=i))
            if a == 0:
                project_landed(far[0])
        for a in range(n_now):
            if a > 0:
                copy(a, 0, _block_of(x, y, 1 - c), here).wait_recv()
                for i in (0, 1):
                    copy(a, 5 + i, near[1][i], here).wait_recv()
            for i in (0, 1):
                copy(a, 7 + i, far[1], here, half=i).wait_recv()
            if a == 0:
                project_landed(far[1])
        for k in range(1, N_DEV):
            small_arrival(k).wait_recv()
        for cp in first + passed + small_sends:
            cp.wait_send()
        for done in mine + stores[-2:]:
            done.wait()

    out_shape = [jax.ShapeDtypeStruct(s, BF16) for s in full_shapes]
    out_shape += [jax.ShapeDtypeStruct((N_DEV, small_rows, LANES), F32), jax.ShapeDtypeStruct((t, full_shapes[0][1]), BF16),
                  jax.ShapeDtypeStruct(tokens.shape, BF16)]
    return pl.pallas_call(
        body, name="gather_weights", out_shape=out_shape,
        in_specs=[VMEM_SPEC] * (n + 3), out_specs=[HBM_SPEC] * n + [VMEM_SPEC, HBM_SPEC, VMEM_SPEC],
        scratch_shapes=[pltpu.VMEM(s.shape, BF16) for s in shards]
        + [pltpu.VMEM(shards[0].shape, BF16), pltpu.VMEM((2, t, shards[0].shape[1]), BF16),
           pltpu.SemaphoreType.DMA((n_now + 1, 9)), pltpu.SemaphoreType.DMA((n_now + 1, 9)),
           pltpu.SemaphoreType.DMA((n,)), pltpu.SemaphoreType.DMA(()), pltpu.SemaphoreType.DMA((2,))],
        compiler_params=_params(),
    )(*shards, small, tokens, gain)


def _gather_first(full, blocks, send, recv):
    x, y, c = _position()
    me = _block_of(x, y, c)
    peers = [(x, y, 1 - c)] + [(*_chip(x, y, k), c) for k in (1, 2, 3)]

    def copy(a, k, block):
        at = blocks[a](full[a], block)
        return pltpu.make_async_remote_copy(src_ref=at, dst_ref=at, send_sem=send[4 * a + k], recv_sem=recv[4 * a + k],
                                            device_id=peers[k], device_id_type=MESH)

    sends = [copy(a, k, me) for a in range(len(full)) for k in range(4)]
    arrivals = [copy(a, k, _block_of(*peers[k])) for a in range(len(full)) for k in range(4)]
    return sends, arrivals


def _gather_second(full, blocks, send, recv):
    x, y, c = _position()

    def copy(a, k, cc):
        at = blocks[a](full[a], _block_of(*_chip(x, y, k), cc))
        return pltpu.make_async_remote_copy(src_ref=at, dst_ref=at, send_sem=send[3 * a + k - 1],
                                            recv_sem=recv[3 * a + k - 1], device_id=(x, y, 1 - c), device_id_type=MESH)

    sends = [copy(a, k, c) for a in range(len(full)) for k in (1, 2, 3)]
    arrivals = [copy(a, k, 1 - c) for a in range(len(full)) for k in (1, 2, 3)]
    return sends, arrivals


def _split_call(body, name, arrays, sems_in, n_sems_out, after=None, token=False):
    n, m = len(arrays), len(sems_in)

    def kernel_body(*refs):
        outs = refs[n + m + (after is not None):]
        body(refs[:n], refs[n:n + m], outs[:n_sems_out])
        if token:
            outs[-1][...] = jnp.zeros_like(outs[-1])

    extra_in = [] if after is None else [after]
    outs = pl.pallas_call(
        kernel_body, name=name,
        out_shape=(*[pltpu.SemaphoreType.DMA(())] * n_sems_out, *[pltpu.HBM(a.shape, a.dtype) for a in arrays],
                   *([jax.ShapeDtypeStruct((SUBLANES, LANES), F32)] if token else [])),
        in_specs=[HBM_SPEC] * n + [SEM_SPEC] * m + [pl.BlockSpec(memory_space=pl.ANY)] * len(extra_in),
        out_specs=(*[SEM_SPEC] * n_sems_out, *[HBM_SPEC] * n, *([VMEM_SPEC] if token else [])),
        input_output_aliases={i: n_sems_out + i for i in range(n)},
        compiler_params=pltpu.CompilerParams(has_side_effects=DATAFLOW_EFFECT),
    )(*[pltpu.with_memory_space_constraint(a, pltpu.HBM) for a in arrays], *sems_in, *extra_in)
    sems, rest = list(outs[:n_sems_out]), list(outs[n_sems_out:])
    return (sems, rest[:n], rest[n]) if token else (sems, rest[:n])


def _gather_start(full, blocks, name):
    n = len(full)

    def body(arrays, _, sems):
        for cp in _gather_first(arrays, blocks, sems[:4 * n], sems[4 * n:])[0]:
            cp.start()

    sems, arrays, token = _split_call(body, name, full, [], 8 * n, token=True)
    return sems[:4 * n], sems[4 * n:], arrays, token


def _gather_forward(full, blocks, send_first, recv_first, after, name):
    n = len(full)

    def body(arrays, sems_in, sems):
        sends, arrivals = _gather_first(arrays, blocks, sems_in[:4 * n], sems_in[4 * n:])
        for cp in arrivals:
            cp.wait_recv()
        for cp in _gather_second(arrays, blocks, sems[:3 * n], sems[3 * n:])[0]:
            cp.start()
        for cp in sends:
            cp.wait_send()

    sems, arrays = _split_call(body, name, full, [*send_first, *recv_first], 6 * n, after=after)
    return sems[:3 * n], sems[3 * n:], arrays


def _gather_finish(full, blocks, send_second, recv_second, after, name):
    n = len(full)

    def body(arrays, sems_in, _):
        sends, arrivals = _gather_second(arrays, blocks, sems_in[:3 * n], sems_in[3 * n:])
        for cp in sends:
            cp.wait_send()
        for cp in arrivals:
            cp.wait_recv()

    return _split_call(body, name, full, [*send_second, *recv_second], 0, after=after)[1]


def _reduce_pair(grads, blocks, shard_shapes, name):
    n = len(grads)

    def body(*refs):
        ins, outs = refs[:n], refs[n:2 * n]
        got, own = refs[2 * n:3 * n], refs[3 * n:4 * n]
        send, recv, local = refs[4 * n:]
        x, y, c = _position()
        copies, loads = [], []
        for a in range(n):
            for k in range(4):
                chip = _chip(x, y, k)
                cp = pltpu.make_async_remote_copy(
                    src_ref=blocks[a](ins[a], _block_of(*chip, 1 - c)), dst_ref=got[a].at[k],
                    send_sem=send.at[a, k], recv_sem=recv.at[a, k], device_id=(x, y, 1 - c), device_id_type=MESH)
                cp.start()
                copies.append(cp)
                ld = pltpu.make_async_copy(blocks[a](ins[a], _block_of(*chip, c)), own[a].at[k], local.at[a, k])
                ld.start()
                loads.append(ld)
        for a in range(n):
            for k in range(4):
                loads[4 * a + k].wait()
                copies[4 * a + k].wait_recv()
                outs[a][k] = (own[a][k].astype(F32) + got[a][k].astype(F32)).astype(BF16)
        for cp in copies:
            cp.wait_send()

    slots = [(4,) + tuple(s) for s in shard_shapes]
    return pl.pallas_call(
        body, name=name, out_shape=[jax.ShapeDtypeStruct(s, BF16) for s in slots],
        in_specs=[HBM_SPEC] * n, out_specs=[VMEM_SPEC] * n,
        scratch_shapes=[pltpu.VMEM(s, BF16) for s in slots] * 2
        + [pltpu.SemaphoreType.DMA((n, 4)), pltpu.SemaphoreType.DMA((n, 4)), pltpu.SemaphoreType.DMA((n, 4))],
        compiler_params=_params(),
    )(*grads)


def _chip_copies(sums, lands, send, recv):
    x, y, c = _position()
    return [pltpu.make_async_remote_copy(
        src_ref=sums[a].at[k], dst_ref=lands[a].at[k - 1], send_sem=send[3 * a + k - 1], recv_sem=recv[3 * a + k - 1],
        device_id=(*_chip(x, y, k), c), device_id_type=MESH) for a in range(len(sums)) for k in (1, 2, 3)]


def _exchange_chips_start(pair_sums, name):
    n = len(pair_sums)
    lands = [pltpu.with_memory_space_constraint(lax.empty((3,) + tuple(p.shape[1:]), BF16), pltpu.HBM) for p in pair_sums]

    def body(*refs):
        sums, zones = refs[:n], refs[n:2 * n]
        send, recv = refs[2 * n:5 * n], refs[5 * n:8 * n]
        token = refs[-1]
        for cp in _chip_copies(sums, zones, send, recv):
            cp.start()
        token[...] = jnp.zeros_like(token)

    outs = pl.pallas_call(
        body, name=name,
        out_shape=(*[pltpu.SemaphoreType.DMA(())] * (6 * n),
                   *[pltpu.HBM(p.shape, BF16) for p in pair_sums], *[pltpu.HBM(z.shape, BF16) for z in lands],
                   jax.ShapeDtypeStruct((SUBLANES, LANES), F32)),
        in_specs=[HBM_SPEC] * (2 * n), out_specs=(*[SEM_SPEC] * (6 * n), *[HBM_SPEC] * (2 * n), VMEM_SPEC),
        input_output_aliases={i: 6 * n + i for i in range(2 * n)},
        compiler_params=pltpu.CompilerParams(has_side_effects=DATAFLOW_EFFECT),
    )(*[pltpu.with_memory_space_constraint(p, pltpu.HBM) for p in pair_sums], *lands)
    return outs[:3 * n], outs[3 * n:6 * n], outs[6 * n:7 * n], outs[7 * n:8 * n], outs[-1]


def _exchange_chips_wait(send, recv, sums, lands, after, name):
    n = len(sums)

    def body(*refs):
        sums_in, zones = refs[:n], refs[n:2 * n]
        send_in, recv_in = refs[2 * n:5 * n], refs[5 * n:8 * n]
        for cp in _chip_copies(sums_in, zones, send_in, recv_in):
            cp.wait_send()
            cp.wait_recv()

    outs = pl.pallas_call(
        body, name=name,
        out_shape=(*[pltpu.HBM(p.shape, BF16) for p in sums], *[pltpu.HBM(z.shape, BF16) for z in lands]),
        in_specs=[HBM_SPEC] * (2 * n) + [SEM_SPEC] * (6 * n) + [pl.BlockSpec(memory_space=pl.ANY)],
        out_specs=[HBM_SPEC] * (2 * n), input_output_aliases={i: i for i in range(2 * n)},
        compiler_params=pltpu.CompilerParams(has_side_effects=DATAFLOW_EFFECT),
    )(*sums, *lands, *send, *recv, after)
    return outs[:n], outs[n:]


def _small_copies(mine, land, send, recv):
    x, y, c = _position()
    me = _block_of(x, y, c)

    def peer(k):
        return (x + (k & 1)) % 2, (y + ((k >> 1) & 1)) % 2, (c + (k >> 2)) % 2

    def copy(k, slot):
        return pltpu.make_async_remote_copy(src_ref=mine, dst_ref=land.at[slot], send_sem=send[k - 1], recv_sem=recv[k - 1],
                                            device_id=peer(k), device_id_type=MESH)

    return [copy(k, me) for k in range(1, N_DEV)], [copy(k, _block_of(*peer(k))) for k in range(1, N_DEV)]


def _small_start(part, name):
    land = jnp.zeros((N_DEV,) + part.shape, F32)

    def body(arrays, _, sems):
        for cp in _small_copies(arrays[0], arrays[1], sems[:7], sems[7:])[0]:
            cp.start()

    sems, arrays, token = _split_call(body, name, [part, land], [], 14, token=True)
    return sems[:7], sems[7:], arrays[0], arrays[1], token


def _small_wait(send, recv, part, land, after, name):
    def body(arrays, sems_in, _):
        sends, arrivals = _small_copies(arrays[0], arrays[1], sems_in[:7], sems_in[7:])
        for cp in sends:
            cp.wait_send()
        for cp in arrivals:
            cp.wait_recv()

    return _split_call(body, name, [part, land], [*send, *recv], 0, after=after)[1]


def _small_sum(pairs, me):
    n = len(pairs)

    def body(me_ref, *refs):
        for i in range(n):
            mine, land, out = refs[2 * i], refs[2 * i + 1], refs[2 * n + i]
            total = jnp.zeros(mine.shape, F32)
            for d in range(N_DEV):
                total = total + land[d] + jnp.where(me_ref[0] == d, mine[...], 0.0)
            out[...] = total

    flat = [a for pair in pairs for a in pair]
    return pl.pallas_call(
        body, name="small_sum", out_shape=[jax.ShapeDtypeStruct(mine.shape, F32) for mine, _ in pairs],
        in_specs=[pl.BlockSpec(memory_space=pltpu.SMEM)] + [VMEM_SPEC] * (2 * n), out_specs=[VMEM_SPEC] * n,
        compiler_params=_params(),
    )(me.reshape(1).astype(jnp.int32), *flat)


def _section(s, t):
    return pl.BlockSpec((t, CB), lambda h, s=s: (0, s * (D_MODEL // CB) + h))


def _conv_mixer_fwd(proj, w_short):
    t = proj.shape[0]
    rc = _row_chunk(t)

    def body(b_ref, c_ref, x_ref, w_ref, y_ref, pad):
        pad[pl.ds(0, PAD), :] = jnp.zeros((PAD, CB), F32)
        for r0 in range(0, t, rc):
            rows = pl.ds(r0, rc)
            pad[pl.ds(PAD + r0, rc), :] = c_ref[rows, :].astype(F32) * x_ref[rows, :].astype(F32)
        w = w_ref[...]
        for r0 in range(0, t, rc):
            rows = pl.ds(r0, rc)
            y_ref[rows, :] = (b_ref[rows, :].astype(F32) * _conv_causal(pad, w, r0, rc, 3)).astype(BF16)

    return pl.pallas_call(
        body, name="conv_mixer_fwd", grid=(D_MODEL // CB,),
        out_shape=jax.ShapeDtypeStruct((t, D_MODEL), BF16),
        in_specs=[_section(0, t), _section(1, t), _section(2, t), pl.BlockSpec((3, CB), lambda h: (0, h))],
        out_specs=pl.BlockSpec((t, CB), lambda h: (0, h)),
        scratch_shapes=[pltpu.VMEM((t + PAD, CB), F32)],
        compiler_params=_params("parallel"),
    )(proj, proj, proj, w_short)


def _lru_gates(xl, wa, ba, wx, bx, ls, first_row):
    xb = xl.astype(BF16)
    ra = jax.nn.sigmoid(_dot(xb, wa) + ba)
    ia = jax.nn.sigmoid(_dot(xb, wx) + bx)
    la = LRU_C * ra * ls
    a = jnp.exp(la)
    one_minus = -_expm1_neg(2.0 * la)
    mult = jnp.where(first_row, 1.0, jnp.sqrt(one_minus))
    return xb, ra, ia, a, one_minus, mult


def _head_specs():
    vec = pl.BlockSpec((1, CB), lambda h: (0, h))
    mat = pl.BlockSpec((N_DEV, None, HEAD_DIM // N_DEV, HEAD_DIM), lambda h: (0, h, 0, 0))
    return vec, mat


def _lru_fwd(proj, w_conv, b_conv, wa, ba, wx, bx, lam):
    t = proj.shape[0]
    rc = _row_chunk(t)
    vec, mat = _head_specs()

    def body(lx_ref, ly_ref, wc_ref, bc_ref, wa_ref, ba_ref, wx_ref, bx_ref, lam_ref, yb_ref, hl_ref, a_ref, kept_ref,
             pad, u_s):
        pad[pl.ds(0, PAD), :] = jnp.zeros((PAD, CB), F32)
        for r0 in range(0, t, rc):
            pad[pl.ds(PAD + r0, rc), :] = lx_ref[pl.ds(r0, rc), :].astype(F32)
        wc, bc = wc_ref[...], bc_ref[...]
        wa_m, wx_m = wa_ref[...].reshape(HEAD_DIM, HEAD_DIM), wx_ref[...].reshape(HEAD_DIM, HEAD_DIM)
        ls = _log_sigmoid(lam_ref[...])
        for r0 in range(0, t, rc):
            rows = pl.ds(r0, rc)
            xl = _conv_causal(pad, wc, r0, rc, 4) + bc
            first = (lax.broadcasted_iota(jnp.int32, (rc, CB), 0) + r0) == 0
            xb, ra, ia, a, _, mult = _lru_gates(xl, wa_m, ba_ref[...], wx_m, bx_ref[...], ls, first)
            a_ref[rows, :] = a
            u_s[rows, :] = mult * (ia * xl)
            kept_ref[0, rows, :] = xb
            kept_ref[1, rows, :] = ra.astype(BF16)
            kept_ref[2, rows, :] = ia.astype(BF16)

        row = lax.broadcasted_iota(jnp.int32, (SUBLANES, CB), 0)

        def group(g, carry):
            r = pl.multiple_of(g * SUBLANES, SUBLANES)
            a_g, b_g = a_ref[pl.ds(r, SUBLANES), :], u_s[pl.ds(r, SUBLANES), :]
            for s in (1, 2, 4):
                keep = row >= s
                b_g = jnp.where(keep, a_g * pltpu.roll(b_g, s, 0) + b_g, b_g)
                a_g = jnp.where(keep, a_g * pltpu.roll(a_g, s, 0), a_g)
            h_g = b_g + a_g * carry
            hl_ref[pl.ds(r, SUBLANES), :] = h_g
            return jnp.broadcast_to(h_g[SUBLANES - 1:SUBLANES, :], (SUBLANES, CB))

        def trip(i, carry):
            for j in range(SCAN_UNROLL):
                carry = group(i * SCAN_UNROLL + j, carry)
            return carry

        lax.fori_loop(0, t // SUBLANES // SCAN_UNROLL, trip, jnp.zeros((SUBLANES, CB), F32))
        for r0 in range(0, t, rc):
            rows = pl.ds(r0, rc)
            yb_ref[rows, :] = (hl_ref[rows, :] * _gelu(ly_ref[rows, :].astype(F32))).astype(BF16)

    blk = pl.BlockSpec((t, CB), lambda h: (0, h))
    res = jax.ShapeDtypeStruct((t, D_MODEL), F32)
    return pl.pallas_call(
        body, name="lru_fwd", grid=(N_HEADS,),
        out_shape=[jax.ShapeDtypeStruct((t, D_MODEL), BF16), res, res, jax.ShapeDtypeStruct((3, t, D_MODEL), BF16)],
        in_specs=[_section(3, t), _section(4, t), pl.BlockSpec((4, CB), lambda h: (0, h)), vec, mat, vec, mat, vec, vec],
        out_specs=[blk, blk, blk, pl.BlockSpec((3, t, CB), lambda h: (0, 0, h))],
        scratch_shapes=[pltpu.VMEM((t + PAD, CB), F32), pltpu.VMEM((t, CB), F32)],
        compiler_params=_params("parallel"),
    )(proj, proj, w_conv, b_conv, wa, ba, wx, bx, lam)


def _merge(y_a, y_b, proj, x, w_cb, w_lb, w_out, g2, g3):
    t = x.shape[0]
    tm = min(512, t)

    def body(ya_ref, yb_ref, gc_ref, gl_ref, x_ref, wcb_ref, wlb_ref, wo_ref, g2_ref, g3_ref,
             pa_ref, pb_ref, mg_ref, mix_ref, x1_ref, h2_ref):
        pa = _dot(ya_ref[...], wcb_ref[...]).astype(BF16)
        pb = _dot(yb_ref[...], wlb_ref[...]).astype(BF16)
        pa_ref[...] = pa
        pb_ref[...] = pb
        merged = (jax.nn.sigmoid(gc_ref[...].astype(F32)) * pa.astype(F32)
                  + jax.nn.sigmoid(gl_ref[...].astype(F32)) * pb.astype(F32)).astype(BF16)
        mg_ref[...] = merged
        mix = _dot(merged, wo_ref[...])
        mix_ref[...] = mix
        n2, _ = _rms_fwd(mix)
        x1 = x_ref[...] + n2 * g2_ref[...]
        x1_ref[...] = x1
        n3, _ = _rms_fwd(x1)
        h2_ref[...] = (n3 * g3_ref[...]).astype(BF16)

    row = pl.BlockSpec((tm, D_MODEL), lambda i: (i, 0))
    full = pl.BlockSpec((D_MODEL, D_MODEL), lambda i: (0, 0))
    vec = pl.BlockSpec((1, D_MODEL), lambda i: (0, 0))
    act = jax.ShapeDtypeStruct((t, D_MODEL), BF16)
    res = jax.ShapeDtypeStruct((t, D_MODEL), F32)
    return pl.pallas_call(
        body, name="merge_fwd", grid=(t // tm,), out_shape=[act, act, act, res, res, act],
        in_specs=[row, row, pl.BlockSpec((tm, D_MODEL), lambda i: (i, 5)), pl.BlockSpec((tm, D_MODEL), lambda i: (i, 6)),
                  row, full, full, full, vec, vec],
        out_specs=[row] * 6,
        compiler_params=_params("parallel"),
    )(y_a, y_b, proj, proj, x, w_cb, w_lb, w_out, g2, g3)


N_FF_BLOCKS = D_FF // CB
FFN_BWD_COLS = 512


def _ffn_up(h2, w_up, w_conv, b_conv):
    t = h2.shape[0]
    rc = _row_chunk(t)
    nb = N_FF_BLOCKS

    def body(h_ref, w_ref, c_ref, b_ref, up_ref, act_ref, f_ref, pad, gate):
        k = pl.program_id(1)
        pad[pl.ds(0, PAD), :] = jnp.zeros((PAD, CB), F32)
        for r0 in range(0, t, rc):
            rows = pl.ds(r0, rc)
            up = _dot(h_ref[rows, :], w_ref[...]).astype(BF16)
            up_ref[rows, :] = up
            pad[pl.ds(PAD + r0, rc), :] = up.astype(F32)
        def conv(keep_gate):
            cw = c_ref[...]
            for r0 in range(0, t, rc):
                rows = pl.ds(r0, rc)
                act = _conv_causal(pad, cw, r0, rc, 3) + b_ref[...]
                act_ref[rows, :] = act.astype(BF16)
                if keep_gate:
                    gate[rows, :] = act
                else:
                    f_ref[rows, :] = (_gelu(gate[rows, :]) * act).astype(BF16)

        @pl.when(k == 0)
        def _():
            conv(True)

        @pl.when(k == 1)
        def _():
            conv(False)

    half = lambda rows: pl.BlockSpec((rows, CB), lambda j, k: (0, nb * k + j))
    wide = jax.ShapeDtypeStruct((t, 2 * D_FF), BF16)
    return pl.pallas_call(
        body, name="ffn_up_fwd", grid=(nb, 2), out_shape=[wide, wide, jax.ShapeDtypeStruct((t, D_FF), BF16)],
        in_specs=[pl.BlockSpec((t, D_MODEL), lambda j, k: (0, 0)), half(D_MODEL), half(3), half(1)],
        out_specs=[half(t), half(t), pl.BlockSpec((t, CB), lambda j, k: (0, j))],
        scratch_shapes=[pltpu.VMEM((t + PAD, CB), F32), pltpu.VMEM((t, CB), F32)],
        compiler_params=_params("parallel", "arbitrary"),
    )(h2, w_up, w_conv, b_conv)


def _ffn_down(f, act, w_down, x1, target, g4):
    t = f.shape[0]
    tm = min(256, t)
    cc = 512

    def body(f_ref, act_ref, w_ref, x1_ref, tg_ref, g_ref, dy_ref, dout_ref, back_ref, dg_ref, loss_ref):
        @pl.when(pl.program_id(0) == 0)
        def _():
            dg_ref[...] = jnp.zeros_like(dg_ref)
            loss_ref[...] = jnp.zeros_like(loss_ref)
        out = _dot(f_ref[...], w_ref[...])
        n4, r4 = _rms_fwd(out)
        err = x1_ref[...] + n4 * g_ref[...] - tg_ref[...]
        loss_ref[...] += jnp.full(loss_ref.shape, 0.5 / D_MODEL, F32) * jnp.sum(err * err)
        dy = err * (1.0 / D_MODEL)
        dy_ref[...] = dy
        dg_ref[...] += jnp.sum(dy * n4, axis=0, keepdims=True)
        d_out = _rms_bwd(n4, r4, dy * g_ref[...]).astype(BF16)
        dout_ref[...] = d_out
        for c0 in range(0, D_FF, cc):
            d_f = _dot_nt(d_out, w_ref[pl.ds(c0, cc), :])
            gelu, d_gelu = _gelu_and_grad(act_ref[:, pl.ds(c0, cc)].astype(F32))
            val = act_ref[:, pl.ds(D_FF + c0, cc)].astype(F32)
            back_ref[:, pl.ds(c0, cc)] = (d_f * val * d_gelu).astype(BF16)
            back_ref[:, pl.ds(D_FF + c0, cc)] = (d_f * gelu).astype(BF16)

    row = pl.BlockSpec((tm, D_MODEL), lambda i: (i, 0))
    wide = pl.BlockSpec((tm, 2 * D_FF), lambda i: (i, 0))
    vec = pl.BlockSpec((1, D_MODEL), lambda i: (0, 0))
    return pl.pallas_call(
        body, name="ffn_down_fwd_bwd", grid=(t // tm,),
        out_shape=[jax.ShapeDtypeStruct((t, D_MODEL), F32), jax.ShapeDtypeStruct((t, D_MODEL), BF16),
                   jax.ShapeDtypeStruct((t, 2 * D_FF), BF16), jax.ShapeDtypeStruct((1, D_MODEL), F32),
                   jax.ShapeDtypeStruct((SUBLANES, LANES), F32)],
        in_specs=[pl.BlockSpec((tm, D_FF), lambda i: (i, 0)), wide, pl.BlockSpec((D_FF, D_MODEL), lambda i: (0, 0)),
                  row, row, vec],
        out_specs=[row, row, wide, vec, pl.BlockSpec((SUBLANES, LANES), lambda i: (0, 0))],
        compiler_params=_params("arbitrary"),
    )(f, act, w_down, x1, target, g4)


def _grad_tn(pairs, bm, name):
    k = len(pairs)
    t, m = pairs[0][0].shape
    n = pairs[0][1].shape[1]

    def body(*refs):
        for i in range(k):
            refs[2 * k + i][...] = _dot_tn(refs[2 * i][...], refs[2 * i + 1][...]).astype(BF16)

    return pl.pallas_call(
        body, name=name, grid=(m // bm,), out_shape=[jax.ShapeDtypeStruct((m, n), BF16)] * k,
        in_specs=[pl.BlockSpec((t, bm), lambda i: (0, i)), pl.BlockSpec((t, n), lambda i: (0, 0))] * k,
        out_specs=[pl.BlockSpec((bm, n), lambda i: (i, 0))] * k,
        compiler_params=_params("parallel"),
    )(*[x for pair in pairs for x in pair])


def _ffn_up_bwd(up, back, w_conv, h2, w_up):
    t = h2.shape[0]
    rc = _row_chunk(t)
    cb = FFN_BWD_COLS

    def body(up_ref, back_ref, c_ref, h_ref, w_ref, dw_ref, dcw_ref, dcb_ref, dh_ref, pad, after, d_up):
        @pl.when(pl.program_id(0) == 0)
        def _():
            dh_ref[...] = jnp.zeros_like(dh_ref)
        pad[pl.ds(0, PAD), :] = jnp.zeros((PAD, cb), F32)
        after[pl.ds(t, PAD), :] = jnp.zeros((PAD, cb), F32)
        for r0 in range(0, t, rc):
            pad[pl.ds(PAD + r0, rc), :] = up_ref[pl.ds(r0, rc), :].astype(F32)
            after[pl.ds(r0, rc), :] = back_ref[pl.ds(r0, rc), :].astype(F32)
        cw = c_ref[...]
        taps = [jnp.zeros((SUBLANES, cb), F32)] * 3
        bias = jnp.zeros((SUBLANES, cb), F32)
        for r0 in range(0, t, rc):
            for q0 in range(r0, r0 + rc, ROW_SLICE):
                rows = pl.ds(q0, ROW_SLICE)
                d_up[rows, :] = _conv_anticausal(after, cw, q0, ROW_SLICE, 3).astype(BF16)
                g = after[rows, :]
                taps = [acc + _fold_rows(g * _rows_back(pad, q0, ROW_SLICE, 2 - k)) for k, acc in enumerate(taps)]
                bias = bias + _fold_rows(g)
            rows = pl.ds(r0, rc)
            dh_ref[rows, :] += _dot_nt(d_up[rows, :], w_ref[...])
        dw_ref[...] = _dot_tn(h_ref[...], d_up[...]).astype(BF16)
        dcw_ref[...] = jnp.concatenate([jnp.sum(acc, axis=0, keepdims=True) for acc in taps], axis=0)
        dcb_ref[...] = jnp.sum(bias, axis=0, keepdims=True)

    cols = lambda rows: pl.BlockSpec((rows, cb), lambda j: (0, j))
    whole = pl.BlockSpec((t, D_MODEL), lambda j: (0, 0))
    return pl.pallas_call(
        body, name="ffn_up_bwd", grid=(2 * D_FF // cb,),
        out_shape=[jax.ShapeDtypeStruct((D_MODEL, 2 * D_FF), BF16), jax.ShapeDtypeStruct((3, 2 * D_FF), F32),
                   jax.ShapeDtypeStruct((1, 2 * D_FF), F32), jax.ShapeDtypeStruct((t, D_MODEL), F32)],
        in_specs=[cols(t), cols(t), cols(3), whole, cols(D_MODEL)],
        out_specs=[cols(D_MODEL), cols(3), cols(1), whole],
        scratch_shapes=[pltpu.VMEM((t + PAD, cb), F32), pltpu.VMEM((t + PAD, cb), F32), pltpu.VMEM((t, cb), BF16)],
        compiler_params=_params("arbitrary"),
    )(up, back, w_conv, h2, w_up)


def _merge_bwd(dy, d_h2, x1, mix, g3, g2, w_out, w_cb, w_lb, pa, pb, proj):
    t = dy.shape[0]
    tm = min(256, t)

    def body(dy_ref, dh2_ref, x1_ref, mix_ref, g3_ref, g2_ref, wo_ref, wcb_ref, wlb_ref, pa_ref, pb_ref, gc_ref, gl_ref,
             dx1_ref, dmix_ref, dpa_ref, dpb_ref, dya_ref, dyb_ref, dgate_ref, dg3_ref, dg2_ref):
        @pl.when(pl.program_id(0) == 0)
        def _():
            dg3_ref[...] = jnp.zeros_like(dg3_ref)
            dg2_ref[...] = jnp.zeros_like(dg2_ref)
        n3, r3 = _rms_fwd(x1_ref[...])
        d_h2 = dh2_ref[...]
        dg3_ref[...] += jnp.sum(d_h2 * n3, axis=0, keepdims=True)
        dx1 = dy_ref[...] + _rms_bwd(n3, r3, d_h2 * g3_ref[...])
        dx1_ref[...] = dx1
        n2, r2 = _rms_fwd(mix_ref[...])
        dg2_ref[...] += jnp.sum(dx1 * n2, axis=0, keepdims=True)
        d_mix = _rms_bwd(n2, r2, dx1 * g2_ref[...]).astype(BF16)
        dmix_ref[...] = d_mix
        d_merged = _dot_nt(d_mix, wo_ref[...])
        sc = jax.nn.sigmoid(gc_ref[...].astype(F32))
        sl = jax.nn.sigmoid(gl_ref[...].astype(F32))
        d_pa = (d_merged * sc).astype(BF16)
        d_pb = (d_merged * sl).astype(BF16)
        dpa_ref[...] = d_pa
        dpb_ref[...] = d_pb
        dgate_ref[0] = (d_merged * pa_ref[...].astype(F32) * sc * (1.0 - sc)).astype(BF16)
        dgate_ref[1] = (d_merged * pb_ref[...].astype(F32) * sl * (1.0 - sl)).astype(BF16)
        dya_ref[...] = _dot_nt(d_pa, wcb_ref[...]).astype(BF16)
        dyb_ref[...] = _dot_nt(d_pb, wlb_ref[...]).astype(BF16)

    row = pl.BlockSpec((tm, D_MODEL), lambda i: (i, 0))
    full = pl.BlockSpec((D_MODEL, D_MODEL), lambda i: (0, 0))
    vec = pl.BlockSpec((1, D_MODEL), lambda i: (0, 0))
    act = jax.ShapeDtypeStruct((t, D_MODEL), BF16)
    small = jax.ShapeDtypeStruct((1, D_MODEL), F32)
    return pl.pallas_call(
        body, name="merge_bwd", grid=(t // tm,),
        out_shape=[jax.ShapeDtypeStruct((t, D_MODEL), F32), act, act, act, act, act,
                   jax.ShapeDtypeStruct((2, t, D_MODEL), BF16), small, small],
        in_specs=[row, row, row, row, vec, vec, full, full, full, row, row,
                  pl.BlockSpec((tm, D_MODEL), lambda i: (i, 5)), pl.BlockSpec((tm, D_MODEL), lambda i: (i, 6))],
        out_specs=[row] * 6 + [pl.BlockSpec((2, tm, D_MODEL), lambda i: (0, i, 0)), vec, vec],
        compiler_params=_params("arbitrary"),
    )(dy, d_h2, x1, mix, g3, g2, w_out, w_cb, w_lb, pa, pb, proj, proj)


def _conv_mixer_bwd(proj, d_ya, w_short):
    t = proj.shape[0]
    rc = _row_chunk(t)

    def body(b_ref, c_ref, x_ref, dy_ref, w_ref, d_ref, dw_ref, pad, back):
        pad[pl.ds(0, PAD), :] = jnp.zeros((PAD, CB), F32)
        back[pl.ds(t, PAD), :] = jnp.zeros((PAD, CB), F32)
        for r0 in range(0, t, rc):
            rows = pl.ds(r0, rc)
            pad[pl.ds(PAD + r0, rc), :] = c_ref[rows, :].astype(F32) * x_ref[rows, :].astype(F32)
        w = w_ref[...]
        for r0 in range(0, t, rc):
            rows = pl.ds(r0, rc)
            d_y = dy_ref[rows, :].astype(F32)
            d_ref[0, rows, :] = (d_y * _conv_causal(pad, w, r0, rc, 3)).astype(BF16)
            back[rows, :] = d_y * b_ref[rows, :].astype(F32)
        taps = [jnp.zeros((1, CB), F32)] * 3
        for r0 in range(0, t, rc):
            rows = pl.ds(r0, rc)
            d_u = _conv_anticausal(back, w, r0, rc, 3)
            d_ref[1, rows, :] = (d_u * x_ref[rows, :].astype(F32)).astype(BF16)
            d_ref[2, rows, :] = (d_u * c_ref[rows, :].astype(F32)).astype(BF16)
            taps = [acc + new for acc, new in zip(taps, _conv_wgrad(back[rows, :], pad, r0, rc, 3))]
        dw_ref[...] = jnp.concatenate(taps, axis=0)

    blk = pl.BlockSpec((t, CB), lambda h: (0, h))
    return pl.pallas_call(
        body, name="conv_mixer_bwd", grid=(D_MODEL // CB,),
        out_shape=[jax.ShapeDtypeStruct((3, t, D_MODEL), BF16), jax.ShapeDtypeStruct((3, D_MODEL), F32)],
        in_specs=[_section(0, t), _section(1, t), _section(2, t), blk, pl.BlockSpec((3, CB), lambda h: (0, h))],
        out_specs=[pl.BlockSpec((3, t, CB), lambda h: (0, 0, h)), pl.BlockSpec((3, CB), lambda h: (0, h))],
        scratch_shapes=[pltpu.VMEM((t + PAD, CB), F32), pltpu.VMEM((t + PAD, CB), F32)],
        compiler_params=_params("parallel"),
    )(proj, proj, proj, d_ya, w_short)


LRU_SMALL_ROWS = 8


def _lru_bwd(proj, hl, a_all, kept, d_yb, w_conv, wa, wx, lam):
    t = proj.shape[0]
    rc = _row_chunk(t)
    vec, mat = _head_specs()

    def body(lx_ref, ly_ref, hl_ref, a_ref, kept_ref, dy_ref, wc_ref, wa_ref, wx_ref, lam_ref,
             d_ref, dwa_ref, dwx_ref, small_ref, pad, a_next, dh_s, dh_o, h_prev, back, acc_a, acc_x, dz_a, dz_x):
        zeros = jnp.zeros((PAD, CB), F32)
        pad[pl.ds(0, PAD), :] = zeros
        h_prev[pl.ds(0, PAD), :] = zeros
        a_next[pl.ds(t, PAD), :] = zeros
        back[pl.ds(t, PAD), :] = zeros
        for r0 in range(0, t, ROW_SLICE):
            rows = pl.ds(r0, ROW_SLICE)
            pad[pl.ds(PAD + r0, ROW_SLICE), :] = lx_ref[rows, :].astype(F32)
            h_prev[pl.ds(PAD + r0, ROW_SLICE), :] = hl_ref[rows, :]
            a_next[pl.ds(PAD - 1 + r0, ROW_SLICE), :] = a_ref[rows, :]
            act, d_act = _gelu_and_grad(ly_ref[rows, :].astype(F32))
            d_y = dy_ref[rows, :].astype(F32)
            dh_s[rows, :] = d_y * act
            d_ref[1, rows, :] = (d_y * hl_ref[rows, :] * d_act).astype(BF16)
        wc = wc_ref[...]
        wa_m, wx_m = wa_ref[...].reshape(HEAD_DIM, HEAD_DIM), wx_ref[...].reshape(HEAD_DIM, HEAD_DIM)
        ls = _log_sigmoid(lam_ref[...])

        row = lax.broadcasted_iota(jnp.int32, (SUBLANES, CB), 0)
        groups = t // SUBLANES

        def group(i, carry):
            r = pl.multiple_of((groups - 1 - i) * SUBLANES, SUBLANES)
            a_g, b_g = a_next[pl.ds(PAD + r, SUBLANES), :], dh_s[pl.ds(r, SUBLANES), :]
            for s in (1, 2, 4):
                keep = row < SUBLANES - s
                b_g = jnp.where(keep, a_g * pltpu.roll(b_g, SUBLANES - s, 0) + b_g, b_g)
                a_g = jnp.where(keep, a_g * pltpu.roll(a_g, SUBLANES - s, 0), a_g)
            d_g = b_g + a_g * carry
            dh_o[pl.ds(r, SUBLANES), :] = d_g
            return jnp.broadcast_to(d_g[0:1, :], (SUBLANES, CB))

        def trip(i, carry):
            for j in range(SCAN_UNROLL):
                carry = group(i * SCAN_UNROLL + j, carry)
            return carry

        lax.fori_loop(0, groups // SCAN_UNROLL, trip, jnp.zeros((SUBLANES, CB), F32))

        acc_a[...] = jnp.zeros_like(acc_a)
        acc_x[...] = jnp.zeros_like(acc_x)
        d_ba = d_bx = d_ls = jnp.zeros((SUBLANES, CB), F32)
        for r0 in range(0, t, rc):
            for q0 in range(r0, r0 + rc, ROW_SLICE):
                rows, local = pl.ds(q0, ROW_SLICE), pl.ds(q0 - r0, ROW_SLICE)
                a = a_ref[rows, :]
                xl, ra, ia = (kept_ref[i, rows, :].astype(F32) for i in range(3))
                a_sq = a * a
                mult = jnp.sqrt(1.0 - a_sq)
                slope = -a_sq / mult
                if q0 == 0:
                    first = lax.broadcasted_iota(jnp.int32, (ROW_SLICE, CB), 0) == 0
                    mult, slope = jnp.where(first, 1.0, mult), jnp.where(first, 0.0, slope)
                d_h = dh_o[rows, :]
                d_la = d_h * _rows_back(h_prev, q0, ROW_SLICE, 1) * a + d_h * ia * xl * slope
                d_za = d_la * (LRU_C * ls) * ra * (1.0 - ra)
                d_zx = d_h * mult * xl * ia * (1.0 - ia)
                d_ls = d_ls + _fold_rows(d_la * ra)
                d_ba = d_ba + _fold_rows(d_za)
                d_bx = d_bx + _fold_rows(d_zx)
                dz_a[local, :] = d_za.astype(BF16)
                dz_x[local, :] = d_zx.astype(BF16)
                back[rows, :] = d_h * mult * ia
            rows = pl.ds(r0, rc)
            xb = kept_ref[0, rows, :]
            acc_a[...] += _dot_tn(xb, dz_a[...])
            acc_x[...] += _dot_tn(xb, dz_x[...])
            back[rows, :] += _dot_nt(dz_a[...], wa_m) + _dot_nt(dz_x[...], wx_m)
        taps = [jnp.zeros((SUBLANES, CB), F32)] * 4
        d_bc = jnp.zeros((SUBLANES, CB), F32)
        for q0 in range(0, t, ROW_SLICE):
            rows = pl.ds(q0, ROW_SLICE)
            d_ref[0, rows, :] = _conv_anticausal(back, wc, q0, ROW_SLICE, 4).astype(BF16)
            g = back[rows, :]
            taps = [acc + _fold_rows(g * _rows_back(pad, q0, ROW_SLICE, 3 - k)) for k, acc in enumerate(taps)]
            d_bc = d_bc + _fold_rows(g)
        d_lam = d_ls * LRU_C * jax.nn.sigmoid(-lam_ref[...])
        small_ref[...] = jnp.concatenate(
            [jnp.sum(v, axis=0, keepdims=True) for v in taps + [d_bc, d_ba, d_bx, d_lam]], axis=0)
        dwa_ref[...] = acc_a[...].reshape(N_DEV, HEAD_DIM // N_DEV, HEAD_DIM).astype(BF16)
        dwx_ref[...] = acc_x[...].reshape(N_DEV, HEAD_DIM // N_DEV, HEAD_DIM).astype(BF16)

    blk = pl.BlockSpec((t, CB), lambda h: (0, h))
    gate_grad = jax.ShapeDtypeStruct((N_DEV, N_HEADS, HEAD_DIM // N_DEV, HEAD_DIM), BF16)
    return pl.pallas_call(
        body, name="lru_bwd", grid=(N_HEADS,),
        out_shape=[jax.ShapeDtypeStruct((2, t, D_MODEL), BF16), gate_grad, gate_grad,
                   jax.ShapeDtypeStruct((LRU_SMALL_ROWS, D_MODEL), F32)],
        in_specs=[_section(3, t), _section(4, t), blk, blk, pl.BlockSpec((3, t, CB), lambda h: (0, 0, h)), blk,
                  pl.BlockSpec((4, CB), lambda h: (0, h)), mat, mat, vec],
        out_specs=[pl.BlockSpec((2, t, CB), lambda h: (0, 0, h)), mat, mat,
                   pl.BlockSpec((LRU_SMALL_ROWS, CB), lambda h: (0, h))],
        scratch_shapes=[pltpu.VMEM((t + PAD, CB), F32), pltpu.VMEM((t + PAD, CB), F32), pltpu.VMEM((t, CB), F32),
                        pltpu.VMEM((t, CB), F32), pltpu.VMEM((t + PAD, CB), F32), pltpu.VMEM((t + PAD, CB), F32),
                        pltpu.VMEM((HEAD_DIM, HEAD_DIM), F32), pltpu.VMEM((HEAD_DIM, HEAD_DIM), F32),
                        pltpu.VMEM((rc, CB), BF16), pltpu.VMEM((rc, CB), BF16)],
        compiler_params=_params("parallel"),
    )(proj, proj, hl, a_all, kept, d_yb, w_conv, wa, wx, lam)


def _stack_maps(halves):
    def conv(sec, part):
        return jnp.minimum(sec, 2), jnp.where(sec < 3, part, halves - 1)

    def lru(sec, part):
        return jnp.clip(sec - 3, 0, 1), jnp.where(sec < 3, 0, jnp.where(sec < 5, part, halves - 1))

    def gate(sec, part):
        return jnp.clip(sec - 5, 0, 1), jnp.where(sec < 5, 0, part)

    return conv, lru, gate


def _pick_stack(sec, refs, fn):
    @pl.when(sec < 3)
    def _():
        fn(refs[0])

    @pl.when((sec >= 3) & (sec < 5))
    def _():
        fn(refs[1])

    @pl.when(sec >= 5)
    def _():
        fn(refs[2])


def _in_proj_wgrad(h, d_conv, d_lru, d_gate):
    t = h.shape[0]
    halves, bn = 1, D_MODEL
    maps = _stack_maps(halves)

    def body(h_ref, dc_ref, dl_ref, dg_ref, o_ref):
        def emit(ref):
            o_ref[...] = _dot_tn(h_ref[...], ref[...]).astype(BF16)
        _pick_stack(pl.program_id(0) // halves, (dc_ref, dl_ref, dg_ref), emit)

    def spec(m):
        def index(s):
            stack, part = m(s // halves, s % halves)
            return stack, 0, part
        return pl.BlockSpec((None, t, bn), index)

    return pl.pallas_call(
        body, name="in_proj_wgrad", grid=(7 * halves,), out_shape=jax.ShapeDtypeStruct((D_MODEL, IN_COLS), BF16),
        in_specs=[pl.BlockSpec((t, D_MODEL), lambda s: (0, 0))] + [spec(m) for m in maps],
        out_specs=pl.BlockSpec((D_MODEL, bn), lambda s: (0, s)),
        compiler_params=_params("arbitrary"),
    )(h, d_conv, d_lru, d_gate)


def _in_proj_xgrad(d_conv, d_lru, d_gate, w_in, x, dx1, g1):
    t = x.shape[0]
    tm = min(1024, t)
    maps = _stack_maps(1)

    def body(dc_ref, dl_ref, dg_ref, w_ref, x_ref, dx1_ref, g_ref, dx_ref, dgain_ref, acc):
        i, s = pl.program_id(0), pl.program_id(1)

        @pl.when((i == 0) & (s == 0))
        def _():
            dgain_ref[...] = jnp.zeros_like(dgain_ref)

        @pl.when(s == 0)
        def _():
            acc[...] = jnp.zeros_like(acc)

        def add(ref):
            acc[...] += _dot_nt(ref[...], w_ref[...])
        _pick_stack(s, (dc_ref, dl_ref, dg_ref), add)

        @pl.when(s == 6)
        def _():
            n1, r1 = _rms_fwd(x_ref[...])
            d_h = acc[...]
            dgain_ref[...] += jnp.sum(d_h * n1, axis=0, keepdims=True)
            dx_ref[...] = dx1_ref[...] + _rms_bwd(n1, r1, d_h * g_ref[...])

    def spec(m):
        def index(i, s):
            return m(s, 0)[0], i, 0
        return pl.BlockSpec((None, tm, D_MODEL), index)

    row = pl.BlockSpec((tm, D_MODEL), lambda i, s: (i, 0))
    vec = pl.BlockSpec((1, D_MODEL), lambda i, s: (0, 0))
    return pl.pallas_call(
        body, name="in_proj_xgrad", grid=(t // tm, 7),
        out_shape=[jax.ShapeDtypeStruct((t, D_MODEL), F32), jax.ShapeDtypeStruct((1, D_MODEL), F32)],
        in_specs=[spec(m) for m in maps] + [pl.BlockSpec((D_MODEL, D_MODEL), lambda i, s: (0, s)), row, row, vec],
        out_specs=[row, vec],
        scratch_shapes=[pltpu.VMEM((tm, D_MODEL), F32)],
        compiler_params=_params("arbitrary", "arbitrary"),
    )(d_conv, d_lru, d_gate, w_in, x, dx1, g1)


def _adamw(w, g, m, v):
    m = ADAM_B1 * m + (1.0 - ADAM_B1) * g
    v = ADAM_B2 * v + (1.0 - ADAM_B2) * (g * g)
    m_hat = m / (1.0 - ADAM_B1 ** ADAM_STEP)
    v_hat = v / (1.0 - ADAM_B2 ** ADAM_STEP)
    return -ADAM_LR * (m_hat / (jnp.sqrt(v_hat) + ADAM_EPS) + ADAM_WD * w), m, v


def _adam_large(ws, ms, vs, owns, others, name):
    n = len(ws)
    shape = ws[0].shape
    cols = shape[-1]
    flat = [[a.reshape(-1, cols) for a in group] for group in (ws, ms, vs)]
    rows = flat[0][0].shape[0]
    owns, others = [o.reshape(4, rows, cols) for o in owns], [o.reshape(3, rows, cols) for o in others]
    rb = _row_block(rows, 512)

    def body(*refs):
        ins, outs = refs[:5 * n], refs[5 * n:]
        for i in range(n):
            w_ref, m_ref, v_ref, own_ref, oth_ref = ins[i::n]
            g = own_ref[...].astype(F32)
            for k in range(3):
                g = g + oth_ref[k].astype(F32)
            outs[i][...] = g
            outs[n + i][...], outs[2 * n + i][...], outs[3 * n + i][...] = _adamw(w_ref[...], g, m_ref[...], v_ref[...])

    blk = pl.BlockSpec((rb, cols), lambda i: (i, 0))
    res = jax.ShapeDtypeStruct((rows, cols), F32)
    outs = pl.pallas_call(
        body, name=name, grid=(rows // rb,), out_shape=[res] * (4 * n),
        in_specs=[blk] * (3 * n) + [pl.BlockSpec((None, rb, cols), lambda i: (0, i, 0))] * n
        + [pl.BlockSpec((3, rb, cols), lambda i: (0, i, 0))] * n,
        out_specs=[blk] * (4 * n), compiler_params=_params("parallel"),
    )(*flat[0], *flat[1], *flat[2], *owns, *others)
    outs = [o.reshape(shape) for o in outs]
    return outs[:n], outs[n:2 * n], outs[2 * n:3 * n], outs[3 * n:]


def _adam_small(ws, gs, ms, vs):
    n = len(ws)

    def body(*refs):
        w_refs, g_refs, m_refs, v_refs = (refs[i * n:(i + 1) * n] for i in range(4))
        outs = refs[4 * n:]
        for i in range(n):
            d, m, v = _adamw(w_refs[i][...], g_refs[i][...], m_refs[i][...], v_refs[i][...])
            outs[i][...], outs[n + i][...], outs[2 * n + i][...] = d, m, v

    shapes = [jax.ShapeDtypeStruct(w.shape, F32) for w in ws]
    outs = pl.pallas_call(
        body, name="adam_small", out_shape=shapes * 3,
        in_specs=[VMEM_SPEC] * (4 * n), out_specs=[VMEM_SPEC] * (3 * n), compiler_params=_params(),
    )(*ws, *gs, *ms, *vs)
    return outs[:n], outs[n:2 * n], outs[2 * n:]


def _pack_rows(pieces):
    tile = SUBLANES * LANES
    return jnp.concatenate([jnp.pad(p.reshape(-1), (0, (-p.size) % tile)).reshape(-1, LANES) for p in pieces], axis=0)


def _packed_starts(sizes):
    tile = SUBLANES * LANES
    starts = [0]
    for s in sizes:
        starts.append(starts[-1] + (s + tile - 1) // tile * SUBLANES)
    return starts


def kernel(x, norm_mix_pre, norm_mix_post, norm_ffn_pre, norm_ffn_post, w_in, conv_short_w, w_conv_branch, lru_conv_w, lru_conv_b, lru_wa, lru_ba, lru_wx, lru_bx, lru_lambda, w_lru_branch, w_out, ffn_w_up, ffn_conv_w, ffn_conv_b, ffn_w_down, loss_target, m_norm_mix_pre, m_norm_mix_post, m_norm_ffn_pre, m_norm_ffn_post, m_w_in, m_conv_short_w, m_w_conv_branch, m_lru_conv_w, m_lru_conv_b, m_lru_wa, m_lru_ba, m_lru_wx, m_lru_bx, m_lru_lambda, m_w_lru_branch, m_w_out, m_ffn_w_up, m_ffn_conv_w, m_ffn_conv_b, m_ffn_w_down, v_norm_mix_pre, v_norm_mix_post, v_norm_ffn_pre, v_norm_ffn_post, v_w_in, v_conv_short_w, v_w_conv_branch, v_lru_conv_w, v_lru_conv_b, v_lru_wa, v_lru_ba, v_lru_wx, v_lru_bx, v_lru_lambda, v_w_lru_branch, v_w_out, v_ffn_w_up, v_ffn_conv_w, v_ffn_conv_b, v_ffn_w_down):
    t = x.shape[1]
    xi, yi, ci = _position()
    me = _block_of(xi, yi, ci)
    x2, target = x[0], loss_target[0]
    shard_in, shard_up = IN_COLS // N_DEV, 2 * D_FF // N_DEV
    shard_sq, shard_down, shard_head = D_MODEL // N_DEV, D_FF // N_DEV, HEAD_DIM // N_DEV

    names = ["w_in", "lru_wa", "lru_wx", "w_conv_branch", "w_lru_branch", "w_out", "ffn_w_up", "ffn_w_down"]
    large = [w_in[0], lru_wa[0], lru_wx[0], w_conv_branch[0], w_lru_branch[0], w_out[0], ffn_w_up[0], ffn_w_down[0]]
    blocks = [_cols(shard_in), _lead, _lead, _rows(shard_sq), _rows(shard_sq), _rows(shard_sq),
              _cols(shard_up), _rows(shard_down)]
    gate_full = (N_DEV, N_HEADS, shard_head, HEAD_DIM)
    full_shapes = [(D_MODEL, IN_COLS), gate_full, gate_full, (D_MODEL, D_MODEL), (D_MODEL, D_MODEL), (D_MODEL, D_MODEL),
                   (D_MODEL, 2 * D_FF), (D_FF, D_MODEL)]
    n_now = 3
    small_sharded = [conv_short_w, lru_conv_w, lru_ba, lru_bx, ffn_conv_w]
    small_mine = _pack_rows(small_sharded)
    small_at = _packed_starts([p.size for p in small_sharded])
    *gathered, small_all, proj, h = _gather_weights(large, blocks, full_shapes, small_mine, n_now, x2, norm_mix_pre)
    g_in, g_wa, g_wx = gathered[:n_now]
    later_blocks = blocks[n_now:]
    send1, recv1, later, gather_token = _gather_start(gathered[n_now:], later_blocks, "gather_start")

    def behind(token, operand):
        return operand + token[0:1, 0:1]

    def forward(lo, hi, after, tag):
        return _gather_forward(later[lo:hi], later_blocks[lo:hi], send1[4 * lo:4 * hi], recv1[4 * lo:4 * hi], after,
                               "gather_forward_" + tag)

    def finish(lo, hi, flight, after, tag):
        return _gather_finish(flight[2], later_blocks[lo:hi], flight[0], flight[1], after, "gather_finish_" + tag)

    def cols_of(r0, n, width):
        part = small_all[:, r0:r0 + n * width // LANES, :].reshape(N_DEV, n, width)
        return part.transpose(1, 0, 2).reshape(n, N_DEV * width)

    c_short = cols_of(small_at[0], 3, LANES)
    c_lru = cols_of(small_at[1], 4, LANES)
    b_a = cols_of(small_at[2], N_HEADS, shard_head).reshape(1, D_MODEL)
    b_x = cols_of(small_at[3], N_HEADS, shard_head).reshape(1, D_MODEL)
    c_ffn = cols_of(small_at[4], 3, shard_up)

    y_a = _conv_mixer_fwd(proj, behind(gather_token, c_short))
    y_b, hl, decay, lru_kept = _lru_fwd(proj, behind(gather_token, c_lru), lru_conv_b, g_wa, b_a, g_wx, b_x, lru_lambda)
    flight_mix_w = forward(0, 3, y_b, "mix")
    g_cb, g_lb, g_out = finish(0, 3, flight_mix_w, y_b, "mix")
    pa, pb, merged, mix, x1, h2 = _merge(y_a, y_b, proj, x2, g_cb, g_lb, g_out, norm_mix_post, norm_ffn_pre)
    flight_up_w = forward(3, 4, h2, "up")
    (g_up,) = finish(3, 4, flight_up_w, h2, "up")
    up, act, f = _ffn_up(h2, g_up, c_ffn, ffn_conv_b)
    flight_down_w = forward(4, 5, f, "down")
    (g_down,) = finish(4, 5, flight_down_w, f, "down")
    dy, d_out, d_act, dg4, loss_part = _ffn_down(f, act, g_down, x1, target, norm_ffn_post)

    block_of = dict(zip(names, blocks))
    shard_shapes = {"w_in": (D_MODEL, shard_in), "w_conv_branch": (shard_sq, D_MODEL), "w_lru_branch": (shard_sq, D_MODEL),
                    "w_out": (shard_sq, D_MODEL), "lru_wa": (N_HEADS, shard_head, HEAD_DIM),
                    "lru_wx": (N_HEADS, shard_head, HEAD_DIM), "ffn_w_up": (D_MODEL, shard_up),
                    "ffn_w_down": (shard_down, D_MODEL)}

    def reduce_start(tag, grads):
        keys = list(grads)
        sums = _reduce_pair([grads[k] for k in keys], [block_of[k] for k in keys], [shard_shapes[k] for k in keys],
                            "reduce_pair_" + tag)
        return (keys,) + _exchange_chips_start(sums, "reduce_chip_start_" + tag)

    (gw_down,) = _grad_tn([(f, d_out)], min(1024, D_FF), "ffn_down_wgrad")
    flight_down = reduce_start("down", {"ffn_w_down": gw_down})
    gw_up, gc_ffn, gb_ffn, d_h2 = _ffn_up_bwd(up, d_act, behind(flight_down[-1], c_ffn), h2, g_up)
    flight_up = reduce_start("up", {"ffn_w_up": gw_up})
    dx1, d_mix, d_pa, d_pb, d_ya, d_yb, d_gate, dg3, dg2 = _merge_bwd(
        dy, d_h2, x1, mix, behind(flight_up[-1], norm_ffn_pre), norm_mix_post, g_out, g_cb, g_lb, pa, pb, proj)
    gw_out, gw_cb, gw_lb = _grad_tn([(merged, d_mix), (y_a, d_pa), (y_b, d_pb)], 2 * CB, "merge_wgrads")
    flight_mix = reduce_start("mix", {"w_conv_branch": gw_cb, "w_lru_branch": gw_lb, "w_out": gw_out})
    d_conv, gc_short = _conv_mixer_bwd(proj, d_ya, behind(flight_mix[-1], c_short))
    d_lru, gw_a, gw_x, g_lru_small = _lru_bwd(proj, hl, decay, lru_kept, d_yb, c_lru, g_wa, g_wx, lru_lambda)
    early = [dg2, dg3, dg4, g_lru_small[4:5], g_lru_small[7:8], gb_ffn, gc_short, g_lru_small[0:4],
             g_lru_small[5:6], g_lru_small[6:7], gc_ffn, loss_part]
    flight_small = _small_start(_pack_rows(early), "small_start")
    gw_in = _in_proj_wgrad(h, d_conv, d_lru, d_gate)
    flight_in = reduce_start("in", {"lru_wa": gw_a, "lru_wx": gw_x, "w_in": gw_in})
    dx, dg1 = _in_proj_xgrad(d_conv, d_lru, d_gate, g_in, x2, dx1,
                             behind(flight_small[-1], behind(flight_in[-1], norm_mix_pre)))
    flight_late = _small_start(_pack_rows([dg1]), "small_start_late")

    moments ={"w_in": (m_w_in, v_w_in), "w_conv_branch": (m_w_conv_branch, v_w_conv_branch),
               "w_lru_branch": (m_w_lru_branch, v_w_lru_branch), "w_out": (m_w_out, v_w_out),
               "lru_wa": (m_lru_wa, v_lru_wa), "lru_wx": (m_lru_wx, v_lru_wx), "ffn_w_up": (m_ffn_w_up, v_ffn_w_up),
               "ffn_w_down": (m_ffn_w_down, v_ffn_w_down)}
    weights = {"w_in": w_in, "w_conv_branch": w_conv_branch, "w_lru_branch": w_lru_branch, "w_out": w_out,
               "lru_wa": lru_wa, "lru_wx": lru_wx, "ffn_w_up": ffn_w_up, "ffn_w_down": ffn_w_down}
    out_g, out_d, out_m, out_v = {}, {}, {}, {}

    after = flight_late[-1]
    for tag, (keys, send, recv, sums, lands, _) in (("down", flight_down), ("up", flight_up), ("mix", flight_mix),
                                                    ("in", flight_in)):
        sums, others = _exchange_chips_wait(send, recv, sums, lands, after, "reduce_chip_wait_" + tag)
        by_key = dict(zip(keys, zip(sums, others)))
        for shape in dict.fromkeys(shard_shapes[k] for k in keys):
            same = [k for k in keys if shard_shapes[k] == shape]
            results = _adam_large([weights[k] for k in same], [moments[k][0] for k in same], [moments[k][1] for k in same],
                                  [by_key[k][0] for k in same], [by_key[k][1] for k in same], "adam_" + same[0])
            for out, values in zip((out_g, out_d, out_m, out_v), results):
                out.update(zip(same, values))
        after = out_d[keys[-1]]

    total, total_late = _small_sum([_small_wait(*flight_small[:4], after, "small_wait"),
                                    _small_wait(*flight_late[:4], after, "small_wait_late")], me)
    sizes = [p.size for p in early]
    starts = _packed_starts(sizes)

    def piece(i, shape):
        if i == 0:
            return total_late.reshape(-1)[:D_MODEL].reshape(shape)
        return total[starts[i - 1]:starts[i]].reshape(-1)[:sizes[i - 1]].reshape(shape)

    loss = total[starts[11], 0]

    def col_shard(full, width):
        return lax.dynamic_slice_in_dim(full, me * width, width, axis=1)

    def head_shard(full):
        return lax.dynamic_slice_in_dim(full.reshape(N_HEADS, HEAD_DIM), me * shard_head, shard_head, axis=1)

    small_names = ["norm_mix_pre", "norm_mix_post", "norm_ffn_pre", "norm_ffn_post", "lru_conv_b", "lru_lambda",
                   "ffn_conv_b", "conv_short_w", "lru_conv_w", "lru_ba", "lru_bx", "ffn_conv_w"]
    small_g = [piece(0, (1, D_MODEL)), piece(1, (1, D_MODEL)), piece(2, (1, D_MODEL)), piece(3, (1, D_MODEL)),
               piece(4, (1, D_MODEL)), piece(5, (1, D_MODEL)), piece(6, (1, 2 * D_FF)),
               col_shard(piece(7, (3, D_MODEL)), LANES), col_shard(piece(8, (4, D_MODEL)), LANES),
               head_shard(piece(9, (1, D_MODEL))), head_shard(piece(10, (1, D_MODEL))),
               col_shard(piece(11, (3, 2 * D_FF)), shard_up)]
    small_w = [norm_mix_pre, norm_mix_post, norm_ffn_pre, norm_ffn_post, lru_conv_b, lru_lambda, ffn_conv_b,
               conv_short_w[0], lru_conv_w[0], lru_ba[0], lru_bx[0], ffn_conv_w[0]]
    small_m = [m_norm_mix_pre, m_norm_mix_post, m_norm_ffn_pre, m_norm_ffn_post, m_lru_conv_b, m_lru_lambda,
               m_ffn_conv_b, m_conv_short_w[0], m_lru_conv_w[0], m_lru_ba[0], m_lru_bx[0], m_ffn_conv_w[0]]
    small_v = [v_norm_mix_pre, v_norm_mix_post, v_norm_ffn_pre, v_norm_ffn_post, v_lru_conv_b, v_lru_lambda,
               v_ffn_conv_b, v_conv_short_w[0], v_lru_conv_w[0], v_lru_ba[0], v_lru_bx[0], v_ffn_conv_w[0]]
    s_d, s_m, s_v = _adam_small(small_w, small_g, small_m, small_v)
    for i, name in enumerate(small_names):
        shape = small_w[i].shape if i < 7 else (1,) + small_w[i].shape
        out_g[name] = small_g[i].reshape(shape)
        out_d[name], out_m[name], out_v[name] = s_d[i].reshape(shape), s_m[i].reshape(shape), s_v[i].reshape(shape)

    order = ["norm_mix_pre", "norm_mix_post", "norm_ffn_pre", "norm_ffn_post", "w_in", "conv_short_w", "w_conv_branch",
             "lru_conv_w", "lru_conv_b", "lru_wa", "lru_ba", "lru_wx", "lru_bx", "lru_lambda", "w_lru_branch", "w_out",
             "ffn_w_up", "ffn_conv_w", "ffn_conv_b", "ffn_w_down"]
    return (loss, dx.reshape(1, t, D_MODEL), *[out_g[k] for k in order], *[out_d[k] for k in order],
            *[out_m[k] for k in order], *[out_v[k] for k in order])
```

```python
import functools
import math

import jax
import jax.numpy as jnp
from jax import lax
from jax.experimental import pallas as pl
from jax.experimental.pallas import tpu as pltpu

F32 = jnp.float32
BF16 = jnp.bfloat16
MESH = pl.DeviceIdType.MESH

N_DEV = 8
D_MODEL = 1024
N_HEADS = 4
HEAD_DIM = D_MODEL // N_HEADS
D_FF = 3 * D_MODEL
IN_COLS = 7 * D_MODEL
LRU_C = 8.0
RMS_EPS = 1e-6
ADAM_LR = 0.001
ADAM_B1 = 0.9
ADAM_B2 = 0.999
ADAM_EPS = 1e-08
ADAM_WD = 0.01
ADAM_STEP = 10
GELU_K = math.sqrt(2.0 / math.pi)
GELU_C = 0.044715

LANES = 128
SUBLANES = 8
PAD = SUBLANES
VMEM_LIMIT = 56 * 1024 * 1024
CB = 256
ROW_SLICE = 32
SCAN_UNROLL = 8

HBM_SPEC = pl.BlockSpec(memory_space=pltpu.HBM)
SEM_SPEC = pl.BlockSpec(memory_space=pltpu.SEMAPHORE)
DATAFLOW_EFFECT = pltpu.SideEffectType.DATAFLOW_SIDE_EFFECTING
VMEM_SPEC = pl.BlockSpec(memory_space=pltpu.VMEM)


def _params(*sem):
    if sem:
        return pltpu.CompilerParams(dimension_semantics=sem, vmem_limit_bytes=VMEM_LIMIT)
    return pltpu.CompilerParams(vmem_limit_bytes=VMEM_LIMIT)


def _row_chunk(t):
    return min(256, t)


def _row_block(rows, cap):
    return next(rb for rb in range(min(cap, rows), 0, -16) if rows % rb == 0)


def _gelu(x):
    return 0.5 * x * (1.0 + jnp.tanh(GELU_K * (x + GELU_C * x * x * x)))


def _gelu_and_grad(x):
    t = jnp.tanh(GELU_K * (x + GELU_C * x * x * x))
    g = 0.5 * x * (1.0 + t)
    dg = 0.5 * (1.0 + t) + 0.5 * x * (1.0 - t * t) * GELU_K * (1.0 + 3.0 * GELU_C * x * x)
    return g, dg


def _expm1_neg(x):
    series = x * (1.0 + x * (0.5 + x * (1.0 / 6.0 + x * (1.0 / 24.0 + x * (1.0 / 120.0)))))
    return jnp.where(x > -0.05, series, jnp.exp(x) - 1.0)


def _log_sigmoid(x):
    return jnp.minimum(x, 0.0) - jnp.log1p(jnp.exp(-jnp.abs(x)))


def _dot(a, b):
    return jnp.dot(a, b, preferred_element_type=F32)


def _dot_nt(a, b):
    return lax.dot_general(a, b, (((1,), (1,)), ((), ())), preferred_element_type=F32)


def _dot_tn(a, b):
    return lax.dot_general(a, b, (((0,), (0,)), ((), ())), preferred_element_type=F32)


def _rms_fwd(x):
    r = lax.rsqrt(jnp.mean(x * x, axis=-1, keepdims=True) + RMS_EPS)
    return x * r, r


def _rms_bwd(n, r, gdy):
    return r * (gdy - n * jnp.mean(n * gdy, axis=-1, keepdims=True))


def _rows_back(pad_ref, r0, rows, j):
    cur = pad_ref[pl.ds(PAD + r0, rows), :]
    if j == 0:
        return cur
    before = pad_ref[pl.ds(PAD + r0 - SUBLANES, SUBLANES), :]
    row = lax.broadcasted_iota(jnp.int32, before.shape, 0)
    rolled = pltpu.roll(cur, j, 0)
    top = jnp.where(row < j, pltpu.roll(before, j, 0), rolled[0:SUBLANES, :])
    return jnp.concatenate([top, rolled[SUBLANES:, :]], axis=0)


def _rows_ahead(pad_ref, r0, rows, j):
    cur = pad_ref[pl.ds(r0, rows), :]
    if j == 0:
        return cur
    after = pad_ref[pl.ds(r0 + rows, SUBLANES), :]
    row = lax.broadcasted_iota(jnp.int32, after.shape, 0)
    rolled = pltpu.roll(cur, rows - j, 0)
    bottom = jnp.where(row >= SUBLANES - j, pltpu.roll(after, SUBLANES - j, 0), rolled[rows - SUBLANES:, :])
    return jnp.concatenate([rolled[:rows - SUBLANES, :], bottom], axis=0)


def _fold_rows(v):
    return v.reshape(v.shape[0] // SUBLANES, SUBLANES, v.shape[1]).sum(axis=0)


def _conv_causal(pad_ref, w, r0, rows, taps):
    acc = None
    for k in range(taps):
        term = w[k:k + 1, :] * _rows_back(pad_ref, r0, rows, taps - 1 - k)
        acc = term if acc is None else acc + term
    return acc


def _conv_anticausal(pad_ref, w, r0, rows, taps):
    acc = None
    for k in range(taps):
        term = w[k:k + 1, :] * _rows_ahead(pad_ref, r0, rows, taps - 1 - k)
        acc = term if acc is None else acc + term
    return acc


def _conv_wgrad(g, xpad_ref, r0, rows, taps):
    return [jnp.sum(g * _rows_back(xpad_ref, r0, rows, taps - 1 - k), axis=0, keepdims=True) for k in range(taps)]


def _position():
    return lax.axis_index("x"), lax.axis_index("y"), lax.axis_index("c")


def _block_of(x, y, c):
    return 4 * x + 2 * y + c


def _chip(x, y, k):
    return (x + (k & 1)) % 2, (y + (k >> 1)) % 2


def _cols(width):
    def at(ref, d, half=None):
        cols = pl.ds(pl.multiple_of(d * width, LANES), width)
        if half is None:
            return ref.at[:, cols]
        return ref.at[pl.ds(half * (ref.shape[0] // 2), ref.shape[0] // 2), cols]
    return at


def _rows(height):
    def at(ref, d, half=None):
        if half is None:
            return ref.at[pl.ds(pl.multiple_of(d * height, 16), height), :]
        return ref.at[pl.ds(pl.multiple_of(d * height + half * (height // 2), 16), height // 2), :]
    return at


def _lead(ref, d, half=None):
    if half is None:
        return ref.at[d]
    return ref.at[d, pl.ds(half * (ref.shape[1] // 2), ref.shape[1] // 2)]


def _gather_weights(shards, blocks, full_shapes, small, n_now, tokens, gain):
    n = len(shards)
    small_rows = small.shape[0]
    t = tokens.shape[0]
    rc = min(512, t)

    def body(*refs):
        ins, small_in, x_ref, g_ref = refs[:n], refs[n], refs[n + 1], refs[n + 2]
        outs, small_out, proj_ref, h_ref = refs[n + 3:2 * n + 3], refs[2 * n + 3], refs[2 * n + 4], refs[2 * n + 5]
        stage = refs[2 * n + 6:3 * n + 6]
        w_buf, p_buf, send, recv, local, w_sem, p_sem = refs[3 * n + 6:]
        x, y, c = _position()
        me = _block_of(x, y, c)
        sibling = (x, y, 1 - c)

        for a in range(n):
            stage[a][...] = ins[a][...].astype(BF16)
        for r0 in range(0, t, rc):
            normed, _ = _rms_fwd(x_ref[pl.ds(r0, rc), :])
            h_ref[pl.ds(r0, rc), :] = (normed * g_ref[...]).astype(BF16)
        stores = []

        def project(w_ref, block):
            i = len(stores)
            if i >= 2:
                stores[i - 2].wait()
            for r0 in range(0, t, rc):
                p_buf[i % 2, pl.ds(r0, rc), :] = _dot(h_ref[pl.ds(r0, rc), :], w_ref[...]).astype(BF16)
            st = pltpu.make_async_copy(p_buf.at[i % 2], blocks[0](proj_ref, block), p_sem.at[i % 2])
            st.start()
            stores.append(st)

        def project_landed(block):
            ld = pltpu.make_async_copy(blocks[0](outs[0], block), w_buf, w_sem)
            ld.start()
            ld.wait()
            project(w_buf, block)

        def copy(a, k, block, to, src=None, half=None):
            dst = blocks[a](outs[a], block, half)
            return pltpu.make_async_remote_copy(
                src_ref=dst if src is None else src, dst_ref=dst, send_sem=send.at[a, k], recv_sem=recv.at[a, k],
                device_id=to, device_id_type=MESH)

        def small_copy(k):
            px, py, pc = (x + (k & 1)) % 2, (y + ((k >> 1) & 1)) % 2, (c + (k >> 2)) % 2
            return pltpu.make_async_remote_copy(
                src_ref=small_in, dst_ref=small_out.at[me], send_sem=send.at[n_now, k - 1], recv_sem=recv.at[n_now, k - 1],
                device_id=(px, py, pc), device_id_type=MESH)

        def small_arrival(k):
            px, py, pc = (x + (k & 1)) % 2, (y + ((k >> 1) & 1)) % 2, (c + (k >> 2)) % 2
            return pltpu.make_async_remote_copy(
                src_ref=small_in, dst_ref=small_out.at[_block_of(px, py, pc)], send_sem=send.at[n_now, k - 1],
                recv_sem=recv.at[n_now, k - 1], device_id=(px, py, pc), device_id_type=MESH)

        small_out[me] = small_in[...]
        small_sends = [small_copy(k) for k in range(1, N_DEV)]
        for cp in small_sends:
            cp.start()

        mine, first, passed = [], [], []
        for a in range(n):
            own = pltpu.make_async_copy(stage[a], blocks[a](outs[a], me), local.at[a])
            own.start()
            mine.append(own)
            if a >= n_now:
                continue
            sends = [copy(a, 0, me, sibling, src=stage[a])]
            sends += [copy(a, k, me, (*_chip(x, y, k), c), src=stage[a]) for k in (1, 2)]
            for cp in sends:
                cp.start()
            first += sends

        here = (x, y, c)
        across = [(*_chip(x, y, k), c) for k in (1, 2)]
        near = [[_block_of(*_chip(x, y, k), cc) for k in (1, 2)] for cc in (c, 1 - c)]
        far = [_block_of(*_chip(x, y, 3), cc) for cc in (c, 1 - c)]

        def launch(cp):
            cp.start()
            passed.append(cp)

        project(stage[0], me)
        copy(0, 0, _block_of(x, y, 1 - c), here).wait_recv()
        project_landed(_block_of(x, y, 1 - c))
        for a in range(n_now):
            for i in (0, 1):
                copy(a, 1 + i, near[0][i], here).wait_recv()
                launch(copy(a, 3 + i, near[0][i], across[1 - i], half=i))
                launch(copy(a, 5 + i, near[0][i], sibling))
            if a == 0:
                project_landed(near[0][0])
                project_landed(near[0][1])
        for i in (0, 1):
            copy(0, 5 + i, near[1][i], here).wait_recv()
            project_landed(near[1][i])
        for a in range(n_now):
            for i in (0, 1):
                copy(a, 3 + i, far[0], here, half=i).wait_recv()
                launch(copy(a, 7 + i, far[0], sibling, half=i))
            if a == 0:
                project_landed(far[0])
        for a in range(n_now):
            if a > 0:
                copy(a, 0, _block_of(x, y, 1 - c), here).wait_recv()
                for i in (0, 1):
                    copy(a, 5 + i, near[1][i], here).wait_recv()
            for i in (0, 1):
                copy(a, 7 + i, far[1], here, half=i).wait_recv()
            if a == 0:
                project_landed(far[1])
        for k in range(1, N_DEV):
            small_arrival(k).wait_recv()
        for cp in first + passed + small_sends:
            cp.wait_send()
        for done in mine + stores[-2:]:
            done.wait()

    out_shape = [jax.ShapeDtypeStruct(s, BF16) for s in full_shapes]
    out_shape += [jax.ShapeDtypeStruct((N_DEV, small_rows, LANES), F32), jax.ShapeDtypeStruct((t, full_shapes[0][1]), BF16),
                  jax.ShapeDtypeStruct(tokens.shape, BF16)]
    return pl.pallas_call(
        body, name="gather_weights", out_shape=out_shape,
        in_specs=[VMEM_SPEC] * (n + 3), out_specs=[HBM_SPEC] * n + [VMEM_SPEC, HBM_SPEC, VMEM_SPEC],
        scratch_shapes=[pltpu.VMEM(s.shape, BF16) for s in shards]
        + [pltpu.VMEM(shards[0].shape, BF16), pltpu.VMEM((2, t, shards[0].shape[1]), BF16),
           pltpu.SemaphoreType.DMA((n_now + 1, 9)), pltpu.SemaphoreType.DMA((n_now + 1, 9)),
           pltpu.SemaphoreType.DMA((n,)), pltpu.SemaphoreType.DMA(()), pltpu.SemaphoreType.DMA((2,))],
        compiler_params=_params(),
    )(*shards, small, tokens, gain)


def _gather_first(full, blocks, send, recv):
    x, y, c = _position()
    me = _block_of(x, y, c)
    peers = [(x, y, 1 - c)] + [(*_chip(x, y, k), c) for k in (1, 2, 3)]

    def copy(a, k, block):
        at = blocks[a](full[a], block)
        return pltpu.make_async_remote_copy(src_ref=at, dst_ref=at, send_sem=send[4 * a + k], recv_sem=recv[4 * a + k],
                                            device_id=peers[k], device_id_type=MESH)

    sends = [copy(a, k, me) for a in range(len(full)) for k in range(4)]
    arrivals = [copy(a, k, _block_of(*peers[k])) for a in range(len(full)) for k in range(4)]
    return sends, arrivals


def _gather_second(full, blocks, send, recv):
    x, y, c = _position()

    def copy(a, k, cc):
        at = blocks[a](full[a], _block_of(*_chip(x, y, k), cc))
        return pltpu.make_async_remote_copy(src_ref=at, dst_ref=at, send_sem=send[3 * a + k - 1],
                                            recv_sem=recv[3 * a + k - 1], device_id=(x, y, 1 - c), device_id_type=MESH)

    sends = [copy(a, k, c) for a in range(len(full)) for k in (1, 2, 3)]
    arrivals = [copy(a, k, 1 - c) for a in range(len(full)) for k in (1, 2, 3)]
    return sends, arrivals


def _split_call(body, name, arrays, sems_in, n_sems_out, after=None, token=False):
    n, m = len(arrays), len(sems_in)

    def kernel_body(*refs):
        outs = refs[n + m + (after is not None):]
        body(refs[:n], refs[n:n + m], outs[:n_sems_out])
        if token:
            outs[-1][...] = jnp.zeros_like(outs[-1])

    extra_in = [] if after is None else [after]
    outs = pl.pallas_call(
        kernel_body, name=name,
        out_shape=(*[pltpu.SemaphoreType.DMA(())] * n_sems_out, *[pltpu.HBM(a.shape, a.dtype) for a in arrays],
                   *([jax.ShapeDtypeStruct((SUBLANES, LANES), F32)] if token else [])),
        in_specs=[HBM_SPEC] * n + [SEM_SPEC] * m + [pl.BlockSpec(memory_space=pl.ANY)] * len(extra_in),
        out_specs=(*[SEM_SPEC] * n_sems_out, *[HBM_SPEC] * n, *([VMEM_SPEC] if token else [])),
        input_output_aliases={i: n_sems_out + i for i in range(n)},
        compiler_params=pltpu.CompilerParams(has_side_effects=DATAFLOW_EFFECT),
    )(*[pltpu.with_memory_space_constraint(a, pltpu.HBM) for a in arrays], *sems_in, *extra_in)
    sems, rest = list(outs[:n_sems_out]), list(outs[n_sems_out:])
    return (sems, rest[:n], rest[n]) if token else (sems, rest[:n])


def _gather_start(full, blocks, name):
    n = len(full)

    def body(arrays, _, sems):
        for cp in _gather_first(arrays, blocks, sems[:4 * n], sems[4 * n:])[0]:
            cp.start()

    sems, arrays, token = _split_call(body, name, full, [], 8 * n, token=True)
    return sems[:4 * n], sems[4 * n:], arrays, token


def _gather_forward(full, blocks, send_first, recv_first, after, name):
    n = len(full)

    def body(arrays, sems_in, sems):
        sends, arrivals = _gather_first(arrays, blocks, sems_in[:4 * n], sems_in[4 * n:])
        for cp in arrivals:
            cp.wait_recv()
        for cp in _gather_second(arrays, blocks, sems[:3 * n], sems[3 * n:])[0]:
            cp.start()
        for cp in sends:
            cp.wait_send()

    sems, arrays = _split_call(body, name, full, [*send_first, *recv_first], 6 * n, after=after)
    return sems[:3 * n], sems[3 * n:], arrays


def _gather_finish(full, blocks, send_second, recv_second, after, name):
    n = len(full)

    def body(arrays, sems_in, _):
        sends, arrivals = _gather_second(arrays, blocks, sems_in[:3 * n], sems_in[3 * n:])
        for cp in sends:
            cp.wait_send()
        for cp in arrivals:
            cp.wait_recv()

    return _split_call(body, name, full, [*send_second, *recv_second], 0, after=after)[1]


def _reduce_pair(grads, blocks, shard_shapes, name):
    n = len(grads)

    def body(*refs):
        ins, outs = refs[:n], refs[n:2 * n]
        got, own = refs[2 * n:3 * n], refs[3 * n:4 * n]
        send, recv, local = refs[4 * n:]
        x, y, c = _position()
        copies, loads = [], []
        for a in range(n):
            for k in range(4):
                chip = _chip(x, y, k)
                cp = pltpu.make_async_remote_copy(
                    src_ref=blocks[a](ins[a], _block_of(*chip, 1 - c)), dst_ref=got[a].at[k],
                    send_sem=send.at[a, k], recv_sem=recv.at[a, k], device_id=(x, y, 1 - c), device_id_type=MESH)
                cp.start()
                copies.append(cp)
                ld = pltpu.make_async_copy(blocks[a](ins[a], _block_of(*chip, c)), own[a].at[k], local.at[a, k])
                ld.start()
                loads.append(ld)
        for a in range(n):
            for k in range(4):
                loads[4 * a + k].wait()
                copies[4 * a + k].wait_recv()
                outs[a][k] = (own[a][k].astype(F32) + got[a][k].astype(F32)).astype(BF16)
        for cp in copies:
            cp.wait_send()

    slots = [(4,) + tuple(s) for s in shard_shapes]
    return pl.pallas_call(
        body, name=name, out_shape=[jax.ShapeDtypeStruct(s, BF16) for s in slots],
        in_specs=[HBM_SPEC] * n, out_specs=[VMEM_SPEC] * n,
        scratch_shapes=[pltpu.VMEM(s, BF16) for s in slots] * 2
        + [pltpu.SemaphoreType.DMA((n, 4)), pltpu.SemaphoreType.DMA((n, 4)), pltpu.SemaphoreType.DMA((n, 4))],
        compiler_params=_params(),
    )(*grads)


def _chip_copies(sums, lands, send, recv):
    x, y, c = _position()
    return [pltpu.make_async_remote_copy(
        src_ref=sums[a].at[k], dst_ref=lands[a].at[k - 1], send_sem=send[3 * a + k - 1], recv_sem=recv[3 * a + k - 1],
        device_id=(*_chip(x, y, k), c), device_id_type=MESH) for a in range(len(sums)) for k in (1, 2, 3)]


def _exchange_chips_start(pair_sums, name):
    n = len(pair_sums)
    lands = [pltpu.with_memory_space_constraint(lax.empty((3,) + tuple(p.shape[1:]), BF16), pltpu.HBM) for p in pair_sums]

    def body(*refs):
        sums, zones = refs[:n], refs[n:2 * n]
        send, recv = refs[2 * n:5 * n], refs[5 * n:8 * n]
        token = refs[-1]
        for cp in _chip_copies(sums, zones, send, recv):
            cp.start()
        token[...] = jnp.zeros_like(token)

    outs = pl.pallas_call(
        body, name=name,
        out_shape=(*[pltpu.SemaphoreType.DMA(())] * (6 * n),
                   *[pltpu.HBM(p.shape, BF16) for p in pair_sums], *[pltpu.HBM(z.shape, BF16) for z in lands],
                   jax.ShapeDtypeStruct((SUBLANES, LANES), F32)),
        in_specs=[HBM_SPEC] * (2 * n), out_specs=(*[SEM_SPEC] * (6 * n), *[HBM_SPEC] * (2 * n), VMEM_SPEC),
        input_output_aliases={i: 6 * n + i for i in range(2 * n)},
        compiler_params=pltpu.CompilerParams(has_side_effects=DATAFLOW_EFFECT),
    )(*[pltpu.with_memory_space_constraint(p, pltpu.HBM) for p in pair_sums], *lands)
    return outs[:3 * n], outs[3 * n:6 * n], outs[6 * n:7 * n], outs[7 * n:8 * n], outs[-1]


def _exchange_chips_wait(send, recv, sums, lands, after, name):
    n = len(sums)

    def body(*refs):
        sums_in, zones = refs[:n], refs[n:2 * n]
        send_in, recv_in = refs[2 * n:5 * n], refs[5 * n:8 * n]
        for cp in _chip_copies(sums_in, zones, send_in, recv_in):
            cp.wait_send()
            cp.wait_recv()

    outs = pl.pallas_call(
        body, name=name,
        out_shape=(*[pltpu.HBM(p.shape, BF16) for p in sums], *[pltpu.HBM(z.shape, BF16) for z in lands]),
        in_specs=[HBM_SPEC] * (2 * n) + [SEM_SPEC] * (6 * n) + [pl.BlockSpec(memory_space=pl.ANY)],
        out_specs=[HBM_SPEC] * (2 * n), input_output_aliases={i: i for i in range(2 * n)},
        compiler_params=pltpu.CompilerParams(has_side_effects=DATAFLOW_EFFECT),
    )(*sums, *lands, *send, *recv, after)
    return outs[:n], outs[n:]


def _small_copies(mine, land, send, recv):
    x, y, c = _position()
    me = _block_of(x, y, c)

    def peer(k):
        return (x + (k & 1)) % 2, (y + ((k >> 1) & 1)) % 2, (c + (k >> 2)) % 2

    def copy(k, slot):
        return pltpu.make_async_remote_copy(src_ref=mine, dst_ref=land.at[slot], send_sem=send[k - 1], recv_sem=recv[k - 1],
                                            device_id=peer(k), device_id_type=MESH)

    return [copy(k, me) for k in range(1, N_DEV)], [copy(k, _block_of(*peer(k))) for k in range(1, N_DEV)]


def _small_start(part, name):
    land = jnp.zeros((N_DEV,) + part.shape, F32)

    def body(arrays, _, sems):
        for cp in _small_copies(arrays[0], arrays[1], sems[:7], sems[7:])[0]:
            cp.start()

    sems, arrays, token = _split_call(body, name, [part, land], [], 14, token=True)
    return sems[:7], sems[7:], arrays[0], arrays[1], token


def _small_wait(send, recv, part, land, after, name):
    def body(arrays, sems_in, _):
        sends, arrivals = _small_copies(arrays[0], arrays[1], sems_in[:7], sems_in[7:])
        for cp in sends:
            cp.wait_send()
        for cp in arrivals:
            cp.wait_recv()

    return _split_call(body, name, [part, land], [*send, *recv], 0, after=after)[1]


def _small_sum(pairs, me):
    n = len(pairs)

    def body(me_ref, *refs):
        for i in range(n):
            mine, land, out = refs[2 * i], refs[2 * i + 1], refs[2 * n + i]
            total = jnp.zeros(mine.shape, F32)
            for d in range(N_DEV):
                total = total + land[d] + jnp.where(me_ref[0] == d, mine[...], 0.0)
            out[...] = total

    flat = [a for pair in pairs for a in pair]
    return pl.pallas_call(
        body, name="small_sum", out_shape=[jax.ShapeDtypeStruct(mine.shape, F32) for mine, _ in pairs],
        in_specs=[pl.BlockSpec(memory_space=pltpu.SMEM)] + [VMEM_SPEC] * (2 * n), out_specs=[VMEM_SPEC] * n,
        compiler_params=_params(),
    )(me.reshape(1).astype(jnp.int32), *flat)


def _section(s, t):
    return pl.BlockSpec((t, CB), lambda h, s=s: (0, s * (D_MODEL // CB) + h))


def _conv_mixer_fwd(proj, w_short):
    t = proj.shape[0]
    rc = _row_chunk(t)

    def body(b_ref, c_ref, x_ref, w_ref, y_ref, pad):
        pad[pl.ds(0, PAD), :] = jnp.zeros((PAD, CB), F32)
        for r0 in range(0, t, rc):
            rows = pl.ds(r0, rc)
            pad[pl.ds(PAD + r0, rc), :] = c_ref[rows, :].astype(F32) * x_ref[rows, :].astype(F32)
        w = w_ref[...]
        for r0 in range(0, t, rc):
            rows = pl.ds(r0, rc)
            y_ref[rows, :] = (b_ref[rows, :].astype(F32) * _conv_causal(pad, w, r0, rc, 3)).astype(BF16)

    return pl.pallas_call(
        body, name="conv_mixer_fwd", grid=(D_MODEL // CB,),
        out_shape=jax.ShapeDtypeStruct((t, D_MODEL), BF16),
        in_specs=[_section(0, t), _section(1, t), _section(2, t), pl.BlockSpec((3, CB), lambda h: (0, h))],
        out_specs=pl.BlockSpec((t, CB), lambda h: (0, h)),
        scratch_shapes=[pltpu.VMEM((t + PAD, CB), F32)],
        compiler_params=_params("parallel"),
    )(proj, proj, proj, w_short)


def _lru_gates(xl, wa, ba, wx, bx, ls, first_row):
    xb = xl.astype(BF16)
    ra = jax.nn.sigmoid(_dot(xb, wa) + ba)
    ia = jax.nn.sigmoid(_dot(xb, wx) + bx)
    la = LRU_C * ra * ls
    a = jnp.exp(la)
    one_minus = -_expm1_neg(2.0 * la)
    mult = jnp.where(first_row, 1.0, jnp.sqrt(one_minus))
    return xb, ra, ia, a, one_minus, mult


def _head_specs():
    vec = pl.BlockSpec((1, CB), lambda h: (0, h))
    mat = pl.BlockSpec((N_DEV, None, HEAD_DIM // N_DEV, HEAD_DIM), lambda h: (0, h, 0, 0))
    return vec, mat


def _lru_fwd(proj, w_conv, b_conv, wa, ba, wx, bx, lam):
    t = proj.shape[0]
    rc = _row_chunk(t)
    vec, mat = _head_specs()

    def body(lx_ref, ly_ref, wc_ref, bc_ref, wa_ref, ba_ref, wx_ref, bx_ref, lam_ref, yb_ref, hl_ref, a_ref, kept_ref,
             pad, u_s):
        pad[pl.ds(0, PAD), :] = jnp.zeros((PAD, CB), F32)
        for r0 in range(0, t, rc):
            pad[pl.ds(PAD + r0, rc), :] = lx_ref[pl.ds(r0, rc), :].astype(F32)
        wc, bc = wc_ref[...], bc_ref[...]
        wa_m, wx_m = wa_ref[...].reshape(HEAD_DIM, HEAD_DIM), wx_ref[...].reshape(HEAD_DIM, HEAD_DIM)
        ls = _log_sigmoid(lam_ref[...])
        for r0 in range(0, t, rc):
            rows = pl.ds(r0, rc)
            xl = _conv_causal(pad, wc, r0, rc, 4) + bc
            first = (lax.broadcasted_iota(jnp.int32, (rc, CB), 0) + r0) == 0
            xb, ra, ia, a, _, mult = _lru_gates(xl, wa_m, ba_ref[...], wx_m, bx_ref[...], ls, first)
            a_ref[rows, :] = a
            u_s[rows, :] = mult * (ia * xl)
            kept_ref[0, rows, :] = xb
            kept_ref[1, rows, :] = ra.astype(BF16)
            kept_ref[2, rows, :] = ia.astype(BF16)

        row = lax.broadcasted_iota(jnp.int32, (SUBLANES, CB), 0)

        def group(g, carry):
            r = pl.multiple_of(g * SUBLANES, SUBLANES)
            a_g, b_g = a_ref[pl.ds(r, SUBLANES), :], u_s[pl.ds(r, SUBLANES), :]
            for s in (1, 2, 4):
                keep = row >= s
                b_g = jnp.where(keep, a_g * pltpu.roll(b_g, s, 0) + b_g, b_g)
                a_g = jnp.where(keep, a_g * pltpu.roll(a_g, s, 0), a_g)
            h_g = b_g + a_g * carry
            hl_ref[pl.ds(r, SUBLANES), :] = h_g
            return jnp.broadcast_to(h_g[SUBLANES - 1:SUBLANES, :], (SUBLANES, CB))

        def trip(i, carry):
            for j in range(SCAN_UNROLL):
                carry = group(i * SCAN_UNROLL + j, carry)
            return carry

        lax.fori_loop(0, t // SUBLANES // SCAN_UNROLL, trip, jnp.zeros((SUBLANES, CB), F32))
        for r0 in range(0, t, rc):
            rows = pl.ds(r0, rc)
            yb_ref[rows, :] = (hl_ref[rows, :] * _gelu(ly_ref[rows, :].astype(F32))).astype(BF16)

    blk = pl.BlockSpec((t, CB), lambda h: (0, h))
    res = jax.ShapeDtypeStruct((t, D_MODEL), F32)
    return pl.pallas_call(
        body, name="lru_fwd", grid=(N_HEADS,),
        out_shape=[jax.ShapeDtypeStruct((t, D_MODEL), BF16), res, res, jax.ShapeDtypeStruct((3, t, D_MODEL), BF16)],
        in_specs=[_section(3, t), _section(4, t), pl.BlockSpec((4, CB), lambda h: (0, h)), vec, mat, vec, mat, vec, vec],
        out_specs=[blk, blk, blk, pl.BlockSpec((3, t, CB), lambda h: (0, 0, h))],
        scratch_shapes=[pltpu.VMEM((t + PAD, CB), F32), pltpu.VMEM((t, CB), F32)],
        compiler_params=_params("parallel"),
    )(proj, proj, w_conv, b_conv, wa, ba, wx, bx, lam)


def _merge(y_a, y_b, proj, x, w_cb, w_lb, w_out, g2, g3):
    t = x.shape[0]
    tm = min(512, t)

    def body(ya_ref, yb_ref, gc_ref, gl_ref, x_ref, wcb_ref, wlb_ref, wo_ref, g2_ref, g3_ref,
             pa_ref, pb_ref, mg_ref, mix_ref, x1_ref, h2_ref):
        pa = _dot(ya_ref[...], wcb_ref[...]).astype(BF16)
        pb = _dot(yb_ref[...], wlb_ref[...]).astype(BF16)
        pa_ref[...] = pa
        pb_ref[...] = pb
        merged = (jax.nn.sigmoid(gc_ref[...].astype(F32)) * pa.astype(F32)
                  + jax.nn.sigmoid(gl_ref[...].astype(F32)) * pb.astype(F32)).astype(BF16)
        mg_ref[...] = merged
        mix = _dot(merged, wo_ref[...])
        mix_ref[...] = mix.astype(BF16)
        n2, _ = _rms_fwd(mix)
        x1 = x_ref[...] + n2 * g2_ref[...]
        x1_ref[...] = x1
        n3, _ = _rms_fwd(x1)
        h2_ref[...] = (n3 * g3_ref[...]).astype(BF16)

    row = pl.BlockSpec((tm, D_MODEL), lambda i: (i, 0))
    full = pl.BlockSpec((D_MODEL, D_MODEL), lambda i: (0, 0))
    vec = pl.BlockSpec((1, D_MODEL), lambda i: (0, 0))
    act = jax.ShapeDtypeStruct((t, D_MODEL), BF16)
    res = jax.ShapeDtypeStruct((t, D_MODEL), F32)
    return pl.pallas_call(
        body, name="merge_fwd", grid=(t // tm,), out_shape=[act, act, act, act, res, act],
        in_specs=[row, row, pl.BlockSpec((tm, D_MODEL), lambda i: (i, 5)), pl.BlockSpec((tm, D_MODEL), lambda i: (i, 6)),
                  row, full, full, full, vec, vec],
        out_specs=[row] * 6,
        compiler_params=_params("parallel"),
    )(y_a, y_b, proj, proj, x, w_cb, w_lb, w_out, g2, g3)


N_FF_BLOCKS = D_FF // CB
FFN_BWD_COLS = 512


def _ffn_up(h2, w_up, w_conv, b_conv):
    t = h2.shape[0]
    rc = _row_chunk(t)
    nb = N_FF_BLOCKS

    def body(h_ref, w_ref, c_ref, b_ref, up_ref, act_ref, f_ref, pad, gate):
        k = pl.program_id(1)
        pad[pl.ds(0, PAD), :] = jnp.zeros((PAD, CB), F32)
        for r0 in range(0, t, rc):
            rows = pl.ds(r0, rc)
            up = _dot(h_ref[rows, :], w_ref[...]).astype(BF16)
            up_ref[rows, :] = up
            pad[pl.ds(PAD + r0, rc), :] = up.astype(F32)
        def conv(keep_gate):
            cw = c_ref[...]
            for r0 in range(0, t, rc):
                rows = pl.ds(r0, rc)
                act = _conv_causal(pad, cw, r0, rc, 3) + b_ref[...]
                act_ref[rows, :] = act.astype(BF16)
                if keep_gate:
                    gate[rows, :] = act
                else:
                    f_ref[rows, :] = (_gelu(gate[rows, :]) * act).astype(BF16)

        @pl.when(k == 0)
        def _():
            conv(True)

        @pl.when(k == 1)
        def _():
            conv(False)

    half = lambda rows: pl.BlockSpec((rows, CB), lambda j, k: (0, nb * k + j))
    wide = jax.ShapeDtypeStruct((t, 2 * D_FF), BF16)
    return pl.pallas_call(
        body, name="ffn_up_fwd", grid=(nb, 2), out_shape=[wide, wide, jax.ShapeDtypeStruct((t, D_FF), BF16)],
        in_specs=[pl.BlockSpec((t, D_MODEL), lambda j, k: (0, 0)), half(D_MODEL), half(3), half(1)],
        out_specs=[half(t), half(t), pl.BlockSpec((t, CB), lambda j, k: (0, j))],
        scratch_shapes=[pltpu.VMEM((t + PAD, CB), F32), pltpu.VMEM((t, CB), F32)],
        compiler_params=_params("parallel", "arbitrary"),
    )(h2, w_up, w_conv, b_conv)


def _ffn_down(f, act, w_down, x1, target, g4):
    t = f.shape[0]
    tm = min(256, t)
    cc = 512

    def body(f_ref, act_ref, w_ref, x1_ref, tg_ref, g_ref, dy_ref, dout_ref, back_ref, dg_ref, loss_ref):
        @pl.when(pl.program_id(0) == 0)
        def _():
            dg_ref[...] = jnp.zeros_like(dg_ref)
            loss_ref[...] = jnp.zeros_like(loss_ref)
        out = _dot(f_ref[...], w_ref[...])
        n4, r4 = _rms_fwd(out)
        err = x1_ref[...] + n4 * g_ref[...] - tg_ref[...]
        loss_ref[...] += jnp.full(loss_ref.shape, 0.5 / D_MODEL, F32) * jnp.sum(err * err)
        dy = err * (1.0 / D_MODEL)
        dy_ref[...] = dy
        dg_ref[...] += jnp.sum(dy * n4, axis=0, keepdims=True)
        d_out = _rms_bwd(n4, r4, dy * g_ref[...]).astype(BF16)
        dout_ref[...] = d_out
        for c0 in range(0, D_FF, cc):
            d_f = _dot_nt(d_out, w_ref[pl.ds(c0, cc), :])
            gelu, d_gelu = _gelu_and_grad(act_ref[:, pl.ds(c0, cc)].astype(F32))
            val = act_ref[:, pl.ds(D_FF + c0, cc)].astype(F32)
            back_ref[:, pl.ds(c0, cc)] = (d_f * val * d_gelu).astype(BF16)
            back_ref[:, pl.ds(D_FF + c0, cc)] = (d_f * gelu).astype(BF16)

    row = pl.BlockSpec((tm, D_MODEL), lambda i: (i, 0))
    wide = pl.BlockSpec((tm, 2 * D_FF), lambda i: (i, 0))
    vec = pl.BlockSpec((1, D_MODEL), lambda i: (0, 0))
    return pl.pallas_call(
        body, name="ffn_down_fwd_bwd", grid=(t // tm,),
        out_shape=[jax.ShapeDtypeStruct((t, D_MODEL), F32), jax.ShapeDtypeStruct((t, D_MODEL), BF16),
                   jax.ShapeDtypeStruct((t, 2 * D_FF), BF16), jax.ShapeDtypeStruct((1, D_MODEL), F32),
                   jax.ShapeDtypeStruct((SUBLANES, LANES), F32)],
        in_specs=[pl.BlockSpec((tm, D_FF), lambda i: (i, 0)), wide, pl.BlockSpec((D_FF, D_MODEL), lambda i: (0, 0)),
                  row, row, vec],
        out_specs=[row, row, wide, vec, pl.BlockSpec((SUBLANES, LANES), lambda i: (0, 0))],
        compiler_params=_params("arbitrary"),
    )(f, act, w_down, x1, target, g4)


def _grad_tn(pairs, bm, name):
    k = len(pairs)
    t, m = pairs[0][0].shape
    n = pairs[0][1].shape[1]

    def body(*refs):
        for i in range(k):
            refs[2 * k + i][...] = _dot_tn(refs[2 * i][...], refs[2 * i + 1][...]).astype(BF16)

    return pl.pallas_call(
        body, name=name, grid=(m // bm,), out_shape=[jax.ShapeDtypeStruct((m, n), BF16)] * k,
        in_specs=[pl.BlockSpec((t, bm), lambda i: (0, i)), pl.BlockSpec((t, n), lambda i: (0, 0))] * k,
        out_specs=[pl.BlockSpec((bm, n), lambda i: (i, 0))] * k,
        compiler_params=_params("parallel"),
    )(*[x for pair in pairs for x in pair])


def _ffn_up_bwd(up, back, w_conv, h2, w_up):
    t = h2.shape[0]
    rc = _row_chunk(t)
    cb = FFN_BWD_COLS

    def body(up_ref, back_ref, c_ref, h_ref, w_ref, dw_ref, dcw_ref, dcb_ref, dh_ref, pad, after, d_up):
        @pl.when(pl.program_id(0) == 0)
        def _():
            dh_ref[...] = jnp.zeros_like(dh_ref)
        pad[pl.ds(0, PAD), :] = jnp.zeros((PAD, cb), F32)
        after[pl.ds(t, PAD), :] = jnp.zeros((PAD, cb), F32)
        for r0 in range(0, t, rc):
            pad[pl.ds(PAD + r0, rc), :] = up_ref[pl.ds(r0, rc), :].astype(F32)
            after[pl.ds(r0, rc), :] = back_ref[pl.ds(r0, rc), :].astype(F32)
        cw = c_ref[...]
        taps = [jnp.zeros((SUBLANES, cb), F32)] * 3
        bias = jnp.zeros((SUBLANES, cb), F32)
        for r0 in range(0, t, rc):
            for q0 in range(r0, r0 + rc, ROW_SLICE):
                rows = pl.ds(q0, ROW_SLICE)
                d_up[rows, :] = _conv_anticausal(after, cw, q0, ROW_SLICE, 3).astype(BF16)
                g = after[rows, :]
                taps = [acc + _fold_rows(g * _rows_back(pad, q0, ROW_SLICE, 2 - k)) for k, acc in enumerate(taps)]
                bias = bias + _fold_rows(g)
            rows = pl.ds(r0, rc)
            dh_ref[rows, :] += _dot_nt(d_up[rows, :], w_ref[...])
        dw_ref[...] = _dot_tn(h_ref[...], d_up[...]).astype(BF16)
        dcw_ref[...] = jnp.concatenate([jnp.sum(acc, axis=0, keepdims=True) for acc in taps], axis=0)
        dcb_ref[...] = jnp.sum(bias, axis=0, keepdims=True)

    cols = lambda rows: pl.BlockSpec((rows, cb), lambda j: (0, j))
    whole = pl.BlockSpec((t, D_MODEL), lambda j: (0, 0))
    return pl.pallas_call(
        body, name="ffn_up_bwd", grid=(2 * D_FF // cb,),
        out_shape=[jax.ShapeDtypeStruct((D_MODEL, 2 * D_FF), BF16), jax.ShapeDtypeStruct((3, 2 * D_FF), F32),
                   jax.ShapeDtypeStruct((1, 2 * D_FF), F32), jax.ShapeDtypeStruct((t, D_MODEL), F32)],
        in_specs=[cols(t), cols(t), cols(3), whole, cols(D_MODEL)],
        out_specs=[cols(D_MODEL), cols(3), cols(1), whole],
        scratch_shapes=[pltpu.VMEM((t + PAD, cb), F32), pltpu.VMEM((t + PAD, cb), F32), pltpu.VMEM((t, cb), BF16)],
        compiler_params=_params("arbitrary"),
    )(up, back, w_conv, h2, w_up)


def _merge_bwd(dy, d_h2, x1, mix, g3, g2, w_out, w_cb, w_lb, pa, pb, proj):
    t = dy.shape[0]
    tm = min(256, t)

    def body(dy_ref, dh2_ref, x1_ref, mix_ref, g3_ref, g2_ref, wo_ref, wcb_ref, wlb_ref, pa_ref, pb_ref, gc_ref, gl_ref,
             dx1_ref, dmix_ref, dpa_ref, dpb_ref, dya_ref, dyb_ref, dgate_ref, dg3_ref, dg2_ref):
        @pl.when(pl.program_id(0) == 0)
        def _():
            dg3_ref[...] = jnp.zeros_like(dg3_ref)
            dg2_ref[...] = jnp.zeros_like(dg2_ref)
        n3, r3 = _rms_fwd(x1_ref[...])
        d_h2 = dh2_ref[...]
        dg3_ref[...] += jnp.sum(d_h2 * n3, axis=0, keepdims=True)
        dx1 = dy_ref[...] + _rms_bwd(n3, r3, d_h2 * g3_ref[...])
        dx1_ref[...] = dx1
        n2, r2 = _rms_fwd(mix_ref[...].astype(F32))
        dg2_ref[...] += jnp.sum(dx1 * n2, axis=0, keepdims=True)
        d_mix = _rms_bwd(n2, r2, dx1 * g2_ref[...]).astype(BF16)
        dmix_ref[...] = d_mix
        d_merged = _dot_nt(d_mix, wo_ref[...])
        sc = jax.nn.sigmoid(gc_ref[...].astype(F32))
        sl = jax.nn.sigmoid(gl_ref[...].astype(F32))
        d_pa = (d_merged * sc).astype(BF16)
        d_pb = (d_merged * sl).astype(BF16)
        dpa_ref[...] = d_pa
        dpb_ref[...] = d_pb
        dgate_ref[0] = (d_merged * pa_ref[...].astype(F32) * sc * (1.0 - sc)).astype(BF16)
        dgate_ref[1] = (d_merged * pb_ref[...].astype(F32) * sl * (1.0 - sl)).astype(BF16)
        dya_ref[...] = _dot_nt(d_pa, wcb_ref[...]).astype(BF16)
        dyb_ref[...] = _dot_nt(d_pb, wlb_ref[...]).astype(BF16)

    row = pl.BlockSpec((tm, D_MODEL), lambda i: (i, 0))
    full = pl.BlockSpec((D_MODEL, D_MODEL), lambda i: (0, 0))
    vec = pl.BlockSpec((1, D_MODEL), lambda i: (0, 0))
    act = jax.ShapeDtypeStruct((t, D_MODEL), BF16)
    small = jax.ShapeDtypeStruct((1, D_MODEL), F32)
    return pl.pallas_call(
        body, name="merge_bwd", grid=(t // tm,),
        out_shape=[jax.ShapeDtypeStruct((t, D_MODEL), F32), act, act, act, act, act,
                   jax.ShapeDtypeStruct((2, t, D_MODEL), BF16), small, small],
        in_specs=[row, row, row, row, vec, vec, full, full, full, row, row,
                  pl.BlockSpec((tm, D_MODEL), lambda i: (i, 5)), pl.BlockSpec((tm, D_MODEL), lambda i: (i, 6))],
        out_specs=[row] * 6 + [pl.BlockSpec((2, tm, D_MODEL), lambda i: (0, i, 0)), vec, vec],
        compiler_params=_params("arbitrary"),
    )(dy, d_h2, x1, mix, g3, g2, w_out, w_cb, w_lb, pa, pb, proj, proj)


def _conv_mixer_bwd(proj, d_ya, w_short):
    t = proj.shape[0]
    rc = _row_chunk(t)

    def body(b_ref, c_ref, x_ref, dy_ref, w_ref, d_ref, dw_ref, pad, back):
        pad[pl.ds(0, PAD), :] = jnp.zeros((PAD, CB), F32)
        back[pl.ds(t, PAD), :] = jnp.zeros((PAD, CB), F32)
        for r0 in range(0, t, rc):
            rows = pl.ds(r0, rc)
            pad[pl.ds(PAD + r0, rc), :] = c_ref[rows, :].astype(F32) * x_ref[rows, :].astype(F32)
        w = w_ref[...]
        for r0 in range(0, t, rc):
            rows = pl.ds(r0, rc)
            d_y = dy_ref[rows, :].astype(F32)
            d_ref[0, rows, :] = (d_y * _conv_causal(pad, w, r0, rc, 3)).astype(BF16)
            back[rows, :] = d_y * b_ref[rows, :].astype(F32)
        taps = [jnp.zeros((1, CB), F32)] * 3
        for r0 in range(0, t, rc):
            rows = pl.ds(r0, rc)
            d_u = _conv_anticausal(back, w, r0, rc, 3)
            d_ref[1, rows, :] = (d_u * x_ref[rows, :].astype(F32)).astype(BF16)
            d_ref[2, rows, :] = (d_u * c_ref[rows, :].astype(F32)).astype(BF16)
            taps = [acc + new for acc, new in zip(taps, _conv_wgrad(back[rows, :], pad, r0, rc, 3))]
        dw_ref[...] = jnp.concatenate(taps, axis=0)

    blk = pl.BlockSpec((t, CB), lambda h: (0, h))
    return pl.pallas_call(
        body, name="conv_mixer_bwd", grid=(D_MODEL // CB,),
        out_shape=[jax.ShapeDtypeStruct((3, t, D_MODEL), BF16), jax.ShapeDtypeStruct((3, D_MODEL), F32)],
        in_specs=[_section(0, t), _section(1, t), _section(2, t), blk, pl.BlockSpec((3, CB), lambda h: (0, h))],
        out_specs=[pl.BlockSpec((3, t, CB), lambda h: (0, 0, h)), pl.BlockSpec((3, CB), lambda h: (0, h))],
        scratch_shapes=[pltpu.VMEM((t + PAD, CB), F32), pltpu.VMEM((t + PAD, CB), F32)],
        compiler_params=_params("parallel"),
    )(proj, proj, proj, d_ya, w_short)


LRU_SMALL_ROWS = 8


def _lru_bwd(proj, hl, a_all, kept, d_yb, w_conv, wa, wx, lam):
    t = proj.shape[0]
    rc = _row_chunk(t)
    vec, mat = _head_specs()

    def body(lx_ref, ly_ref, hl_ref, a_ref, kept_ref, dy_ref, wc_ref, wa_ref, wx_ref, lam_ref,
             d_ref, dwa_ref, dwx_ref, small_ref, pad, a_next, dh_s, dh_o, h_prev, back, acc_a, acc_x, dz_a, dz_x):
        zeros = jnp.zeros((PAD, CB), F32)
        pad[pl.ds(0, PAD), :] = zeros
        h_prev[pl.ds(0, PAD), :] = zeros
        a_next[pl.ds(t, PAD), :] = zeros
        back[pl.ds(t, PAD), :] = zeros
        for r0 in range(0, t, ROW_SLICE):
            rows = pl.ds(r0, ROW_SLICE)
            pad[pl.ds(PAD + r0, ROW_SLICE), :] = lx_ref[rows, :].astype(F32)
            h_prev[pl.ds(PAD + r0, ROW_SLICE), :] = hl_ref[rows, :]
            a_next[pl.ds(PAD - 1 + r0, ROW_SLICE), :] = a_ref[rows, :]
            act, d_act = _gelu_and_grad(ly_ref[rows, :].astype(F32))
            d_y = dy_ref[rows, :].astype(F32)
            dh_s[rows, :] = d_y * act
            d_ref[1, rows, :] = (d_y * hl_ref[rows, :] * d_act).astype(BF16)
        wc = wc_ref[...]
        wa_m, wx_m = wa_ref[...].reshape(HEAD_DIM, HEAD_DIM), wx_ref[...].reshape(HEAD_DIM, HEAD_DIM)
        ls = _log_sigmoid(lam_ref[...])

        row = lax.broadcasted_iota(jnp.int32, (SUBLANES, CB), 0)
        groups = t // SUBLANES

        def group(i, carry):
            r = pl.multiple_of((groups - 1 - i) * SUBLANES, SUBLANES)
            a_g, b_g = a_next[pl.ds(PAD + r, SUBLANES), :], dh_s[pl.ds(r, SUBLANES), :]
            for s in (1, 2, 4):
                keep = row < SUBLANES - s
                b_g = jnp.where(keep, a_g * pltpu.roll(b_g, SUBLANES - s, 0) + b_g, b_g)
                a_g = jnp.where(keep, a_g * pltpu.roll(a_g, SUBLANES - s, 0), a_g)
            d_g = b_g + a_g * carry
            dh_o[pl.ds(r, SUBLANES), :] = d_g
            return jnp.broadcast_to(d_g[0:1, :], (SUBLANES, CB))

        def trip(i, carry):
            for j in range(SCAN_UNROLL):
                carry = group(i * SCAN_UNROLL + j, carry)
            return carry

        lax.fori_loop(0, groups // SCAN_UNROLL, trip, jnp.zeros((SUBLANES, CB), F32))

        acc_a[...] = jnp.zeros_like(acc_a)
        acc_x[...] = jnp.zeros_like(acc_x)
        d_ba = d_bx = d_ls = jnp.zeros((SUBLANES, CB), F32)
        for r0 in range(0, t, rc):
            for q0 in range(r0, r0 + rc, ROW_SLICE):
                rows, local = pl.ds(q0, ROW_SLICE), pl.ds(q0 - r0, ROW_SLICE)
                a = a_ref[rows, :]
                xl, ra, ia = (kept_ref[i, rows, :].astype(F32) for i in range(3))
                a_sq = a * a
                mult = jnp.sqrt(1.0 - a_sq)
                slope = -a_sq / mult
                if q0 == 0:
                    first = lax.broadcasted_iota(jnp.int32, (ROW_SLICE, CB), 0) == 0
                    mult, slope = jnp.where(first, 1.0, mult), jnp.where(first, 0.0, slope)
                d_h = dh_o[rows, :]
                d_la = d_h * _rows_back(h_prev, q0, ROW_SLICE, 1) * a + d_h * ia * xl * slope
                d_za = d_la * (LRU_C * ls) * ra * (1.0 - ra)
                d_zx = d_h * mult * xl * ia * (1.0 - ia)
                d_ls = d_ls + _fold_rows(d_la * ra)
                d_ba = d_ba + _fold_rows(d_za)
                d_bx = d_bx + _fold_rows(d_zx)
                dz_a[local, :] = d_za.astype(BF16)
                dz_x[local, :] = d_zx.astype(BF16)
                back[rows, :] = d_h * mult * ia
            rows = pl.ds(r0, rc)
            xb = kept_ref[0, rows, :]
            acc_a[...] += _dot_tn(xb, dz_a[...])
            acc_x[...] += _dot_tn(xb, dz_x[...])
            back[rows, :] += _dot_nt(dz_a[...], wa_m) + _dot_nt(dz_x[...], wx_m)
        taps = [jnp.zeros((SUBLANES, CB), F32)] * 4
        d_bc = jnp.zeros((SUBLANES, CB), F32)
        for q0 in range(0, t, ROW_SLICE):
            rows = pl.ds(q0, ROW_SLICE)
            d_ref[0, rows, :] = _conv_anticausal(back, wc, q0, ROW_SLICE, 4).astype(BF16)
            g = back[rows, :]
            taps = [acc + _fold_rows(g * _rows_back(pad, q0, ROW_SLICE, 3 - k)) for k, acc in enumerate(taps)]
            d_bc = d_bc + _fold_rows(g)
        d_lam = d_ls * LRU_C * jax.nn.sigmoid(-lam_ref[...])
        small_ref[...] = jnp.concatenate(
            [jnp.sum(v, axis=0, keepdims=True) for v in taps + [d_bc, d_ba, d_bx, d_lam]], axis=0)
        dwa_ref[...] = acc_a[...].reshape(N_DEV, HEAD_DIM // N_DEV, HEAD_DIM).astype(BF16)
        dwx_ref[...] = acc_x[...].reshape(N_DEV, HEAD_DIM // N_DEV, HEAD_DIM).astype(BF16)

    blk = pl.BlockSpec((t, CB), lambda h: (0, h))
    gate_grad = jax.ShapeDtypeStruct((N_DEV, N_HEADS, HEAD_DIM // N_DEV, HEAD_DIM), BF16)
    return pl.pallas_call(
        body, name="lru_bwd", grid=(N_HEADS,),
        out_shape=[jax.ShapeDtypeStruct((2, t, D_MODEL), BF16), gate_grad, gate_grad,
                   jax.ShapeDtypeStruct((LRU_SMALL_ROWS, D_MODEL), F32)],
        in_specs=[_section(3, t), _section(4, t), blk, blk, pl.BlockSpec((3, t, CB), lambda h: (0, 0, h)), blk,
                  pl.BlockSpec((4, CB), lambda h: (0, h)), mat, mat, vec],
        out_specs=[pl.BlockSpec((2, t, CB), lambda h: (0, 0, h)), mat, mat,
                   pl.BlockSpec((LRU_SMALL_ROWS, CB), lambda h: (0, h))],
        scratch_shapes=[pltpu.VMEM((t + PAD, CB), F32), pltpu.VMEM((t + PAD, CB), F32), pltpu.VMEM((t, CB), F32),
                        pltpu.VMEM((t, CB), F32), pltpu.VMEM((t + PAD, CB), F32), pltpu.VMEM((t + PAD, CB), F32),
                        pltpu.VMEM((HEAD_DIM, HEAD_DIM), F32), pltpu.VMEM((HEAD_DIM, HEAD_DIM), F32),
                        pltpu.VMEM((rc, CB), BF16), pltpu.VMEM((rc, CB), BF16)],
        compiler_params=_params("parallel"),
    )(proj, proj, hl, a_all, kept, d_yb, w_conv, wa, wx, lam)


def _stack_maps(halves):
    def conv(sec, part):
        return jnp.minimum(sec, 2), jnp.where(sec < 3, part, halves - 1)

    def lru(sec, part):
        return jnp.clip(sec - 3, 0, 1), jnp.where(sec < 3, 0, jnp.where(sec < 5, part, halves - 1))

    def gate(sec, part):
        return jnp.clip(sec - 5, 0, 1), jnp.where(sec < 5, 0, part)

    return conv, lru, gate


def _pick_stack(sec, refs, fn):
    @pl.when(sec < 3)
    def _():
        fn(refs[0])

    @pl.when((sec >= 3) & (sec < 5))
    def _():
        fn(refs[1])

    @pl.when(sec >= 5)
    def _():
        fn(refs[2])


def _in_proj_wgrad(h, d_conv, d_lru, d_gate):
    t = h.shape[0]
    halves, bn = 1, D_MODEL
    maps = _stack_maps(halves)

    def body(h_ref, dc_ref, dl_ref, dg_ref, o_ref):
        def emit(ref):
            o_ref[...] = _dot_tn(h_ref[...], ref[...]).astype(BF16)
        _pick_stack(pl.program_id(0) // halves, (dc_ref, dl_ref, dg_ref), emit)

    def spec(m):
        def index(s):
            stack, part = m(s // halves, s % halves)
            return stack, 0, part
        return pl.BlockSpec((None, t, bn), index)

    return pl.pallas_call(
        body, name="in_proj_wgrad", grid=(7 * halves,), out_shape=jax.ShapeDtypeStruct((D_MODEL, IN_COLS), BF16),
        in_specs=[pl.BlockSpec((t, D_MODEL), lambda s: (0, 0))] + [spec(m) for m in maps],
        out_specs=pl.BlockSpec((D_MODEL, bn), lambda s: (0, s)),
        compiler_params=_params("arbitrary"),
    )(h, d_conv, d_lru, d_gate)


def _in_proj_xgrad(d_conv, d_lru, d_gate, w_in, x, dx1, g1):
    t = x.shape[0]
    tm = min(1024, t)
    maps = _stack_maps(1)

    def body(dc_ref, dl_ref, dg_ref, w_ref, x_ref, dx1_ref, g_ref, dx_ref, dgain_ref, acc):
        i, s = pl.program_id(0), pl.program_id(1)

        @pl.when((i == 0) & (s == 0))
        def _():
            dgain_ref[...] = jnp.zeros_like(dgain_ref)

        @pl.when(s == 0)
        def _():
            acc[...] = jnp.zeros_like(acc)

        def add(ref):
            acc[...] += _dot_nt(ref[...], w_ref[...])
        _pick_stack(s, (dc_ref, dl_ref, dg_ref), add)

        @pl.when(s == 6)
        def _():
            n1, r1 = _rms_fwd(x_ref[...])
            d_h = acc[...]
            dgain_ref[...] += jnp.sum(d_h * n1, axis=0, keepdims=True)
            dx_ref[...] = dx1_ref[...] + _rms_bwd(n1, r1, d_h * g_ref[...])

    def spec(m):
        def index(i, s):
            return m(s, 0)[0], i, 0
        return pl.BlockSpec((None, tm, D_MODEL), index)

    row = pl.BlockSpec((tm, D_MODEL), lambda i, s: (i, 0))
    vec = pl.BlockSpec((1, D_MODEL), lambda i, s: (0, 0))
    return pl.pallas_call(
        body, name="in_proj_xgrad", grid=(t // tm, 7),
        out_shape=[jax.ShapeDtypeStruct((t, D_MODEL), F32), jax.ShapeDtypeStruct((1, D_MODEL), F32)],
        in_specs=[spec(m) for m in maps] + [pl.BlockSpec((D_MODEL, D_MODEL), lambda i, s: (0, s)), row, row, vec],
        out_specs=[row, vec],
        scratch_shapes=[pltpu.VMEM((tm, D_MODEL), F32)],
        compiler_params=_params("arbitrary", "arbitrary"),
    )(d_conv, d_lru, d_gate, w_in, x, dx1, g1)


def _adamw(w, g, m, v):
    m = ADAM_B1 * m + (1.0 - ADAM_B1) * g
    v = ADAM_B2 * v + (1.0 - ADAM_B2) * (g * g)
    m_hat = m / (1.0 - ADAM_B1 ** ADAM_STEP)
    v_hat = v / (1.0 - ADAM_B2 ** ADAM_STEP)
    return -ADAM_LR * (m_hat / (jnp.sqrt(v_hat) + ADAM_EPS) + ADAM_WD * w), m, v


def _adam_large(ws, ms, vs, owns, others, name):
    n = len(ws)
    shape = ws[0].shape
    cols = shape[-1]
    flat = [[a.reshape(-1, cols) for a in group] for group in (ws, ms, vs)]
    rows = flat[0][0].shape[0]
    owns, others = [o.reshape(4, rows, cols) for o in owns], [o.reshape(3, rows, cols) for o in others]
    rb = _row_block(rows, 512)

    def body(*refs):
        ins, outs = refs[:5 * n], refs[5 * n:]
        for i in range(n):
            w_ref, m_ref, v_ref, own_ref, oth_ref = ins[i::n]
            g = own_ref[...].astype(F32)
            for k in range(3):
                g = g + oth_ref[k].astype(F32)
            outs[i][...] = g
            outs[n + i][...], outs[2 * n + i][...], outs[3 * n + i][...] = _adamw(w_ref[...], g, m_ref[...], v_ref[...])

    blk = pl.BlockSpec((rb, cols), lambda i: (i, 0))
    res = jax.ShapeDtypeStruct((rows, cols), F32)
    outs = pl.pallas_call(
        body, name=name, grid=(rows // rb,), out_shape=[res] * (4 * n),
        in_specs=[blk] * (3 * n) + [pl.BlockSpec((None, rb, cols), lambda i: (0, i, 0))] * n
        + [pl.BlockSpec((3, rb, cols), lambda i: (0, i, 0))] * n,
        out_specs=[blk] * (4 * n), compiler_params=_params("parallel"),
    )(*flat[0], *flat[1], *flat[2], *owns, *others)
    outs = [o.reshape(shape) for o in outs]
    return outs[:n], outs[n:2 * n], outs[2 * n:3 * n], outs[3 * n:]


def _adam_small(ws, gs, ms, vs):
    n = len(ws)

    def body(*refs):
        w_refs, g_refs, m_refs, v_refs = (refs[i * n:(i + 1) * n] for i in range(4))
        outs = refs[4 * n:]
        for i in range(n):
            d, m, v = _adamw(w_refs[i][...], g_refs[i][...], m_refs[i][...], v_refs[i][...])
            outs[i][...], outs[n + i][...], outs[2 * n + i][...] = d, m, v

    shapes = [jax.ShapeDtypeStruct(w.shape, F32) for w in ws]
    outs = pl.pallas_call(
        body, name="adam_small", out_shape=shapes * 3,
        in_specs=[VMEM_SPEC] * (4 * n), out_specs=[VMEM_SPEC] * (3 * n), compiler_params=_params(),
    )(*ws, *gs, *ms, *vs)
    return outs[:n], outs[n:2 * n], outs[2 * n:]


def _pack_rows(pieces):
    tile = SUBLANES * LANES
    return jnp.concatenate([jnp.pad(p.reshape(-1), (0, (-p.size) % tile)).reshape(-1, LANES) for p in pieces], axis=0)


def _packed_starts(sizes):
    tile = SUBLANES * LANES
    starts = [0]
    for s in sizes:
        starts.append(starts[-1] + (s + tile - 1) // tile * SUBLANES)
    return starts


def kernel(x, norm_mix_pre, norm_mix_post, norm_ffn_pre, norm_ffn_post, w_in, conv_short_w, w_conv_branch, lru_conv_w, lru_conv_b, lru_wa, lru_ba, lru_wx, lru_bx, lru_lambda, w_lru_branch, w_out, ffn_w_up, ffn_conv_w, ffn_conv_b, ffn_w_down, loss_target, m_norm_mix_pre, m_norm_mix_post, m_norm_ffn_pre, m_norm_ffn_post, m_w_in, m_conv_short_w, m_w_conv_branch, m_lru_conv_w, m_lru_conv_b, m_lru_wa, m_lru_ba, m_lru_wx, m_lru_bx, m_lru_lambda, m_w_lru_branch, m_w_out, m_ffn_w_up, m_ffn_conv_w, m_ffn_conv_b, m_ffn_w_down, v_norm_mix_pre, v_norm_mix_post, v_norm_ffn_pre, v_norm_ffn_post, v_w_in, v_conv_short_w, v_w_conv_branch, v_lru_conv_w, v_lru_conv_b, v_lru_wa, v_lru_ba, v_lru_wx, v_lru_bx, v_lru_lambda, v_w_lru_branch, v_w_out, v_ffn_w_up, v_ffn_conv_w, v_ffn_conv_b, v_ffn_w_down):
    t = x.shape[1]
    xi, yi, ci = _position()
    me = _block_of(xi, yi, ci)
    x2, target = x[0], loss_target[0]
    shard_in, shard_up = IN_COLS // N_DEV, 2 * D_FF // N_DEV
    shard_sq, shard_down, shard_head = D_MODEL // N_DEV, D_FF // N_DEV, HEAD_DIM // N_DEV

    names = ["w_in", "lru_wa", "lru_wx", "w_conv_branch", "w_lru_branch", "w_out", "ffn_w_up", "ffn_w_down"]
    large = [w_in[0], lru_wa[0], lru_wx[0], w_conv_branch[0], w_lru_branch[0], w_out[0], ffn_w_up[0], ffn_w_down[0]]
    blocks = [_cols(shard_in), _lead, _lead, _rows(shard_sq), _rows(shard_sq), _rows(shard_sq),
              _cols(shard_up), _rows(shard_down)]
    gate_full = (N_DEV, N_HEADS, shard_head, HEAD_DIM)
    full_shapes = [(D_MODEL, IN_COLS), gate_full, gate_full, (D_MODEL, D_MODEL), (D_MODEL, D_MODEL), (D_MODEL, D_MODEL),
                   (D_MODEL, 2 * D_FF), (D_FF, D_MODEL)]
    n_now = 3
    small_sharded = [conv_short_w, lru_conv_w, lru_ba, lru_bx, ffn_conv_w]
    small_mine = _pack_rows(small_sharded)
    small_at = _packed_starts([p.size for p in small_sharded])
    *gathered, small_all, proj, h = _gather_weights(large, blocks, full_shapes, small_mine, n_now, x2, norm_mix_pre)
    g_in, g_wa, g_wx = gathered[:n_now]
    later_blocks = blocks[n_now:]
    send1, recv1, later, gather_token = _gather_start(gathered[n_now:], later_blocks, "gather_start")

    def behind(token, operand):
        return operand + token[0:1, 0:1]

    def forward(lo, hi, after, tag):
        return _gather_forward(later[lo:hi], later_blocks[lo:hi], send1[4 * lo:4 * hi], recv1[4 * lo:4 * hi], after,
                               "gather_forward_" + tag)

    def finish(lo, hi, flight, after, tag):
        return _gather_finish(flight[2], later_blocks[lo:hi], flight[0], flight[1], after, "gather_finish_" + tag)

    def cols_of(r0, n, width):
        part = small_all[:, r0:r0 + n * width // LANES, :].reshape(N_DEV, n, width)
        return part.transpose(1, 0, 2).reshape(n, N_DEV * width)

    c_short = cols_of(small_at[0], 3, LANES)
    c_lru = cols_of(small_at[1], 4, LANES)
    b_a = cols_of(small_at[2], N_HEADS, shard_head).reshape(1, D_MODEL)
    b_x = cols_of(small_at[3], N_HEADS, shard_head).reshape(1, D_MODEL)
    c_ffn = cols_of(small_at[4], 3, shard_up)

    y_a = _conv_mixer_fwd(proj, behind(gather_token, c_short))
    y_b, hl, decay, lru_kept = _lru_fwd(proj, behind(gather_token, c_lru), lru_conv_b, g_wa, b_a, g_wx, b_x, lru_lambda)
    flight_mix_w = forward(0, 3, y_b, "mix")
    g_cb, g_lb, g_out = finish(0, 3, flight_mix_w, y_b, "mix")
    pa, pb, merged, mix, x1, h2 = _merge(y_a, y_b, proj, x2, g_cb, g_lb, g_out, norm_mix_post, norm_ffn_pre)
    flight_up_w = forward(3, 4, h2, "up")
    (g_up,) = finish(3, 4, flight_up_w, h2, "up")
    up, act, f = _ffn_up(h2, g_up, c_ffn, ffn_conv_b)
    flight_down_w = forward(4, 5, f, "down")
    (g_down,) = finish(4, 5, flight_down_w, f, "down")
    dy, d_out, d_act, dg4, loss_part = _ffn_down(f, act, g_down, x1, target, norm_ffn_post)

    block_of = dict(zip(names, blocks))
    shard_shapes = {"w_in": (D_MODEL, shard_in), "w_conv_branch": (shard_sq, D_MODEL), "w_lru_branch": (shard_sq, D_MODEL),
                    "w_out": (shard_sq, D_MODEL), "lru_wa": (N_HEADS, shard_head, HEAD_DIM),
                    "lru_wx": (N_HEADS, shard_head, HEAD_DIM), "ffn_w_up": (D_MODEL, shard_up),
                    "ffn_w_down": (shard_down, D_MODEL)}

    def reduce_start(tag, grads):
        keys = list(grads)
        sums = _reduce_pair([grads[k] for k in keys], [block_of[k] for k in keys], [shard_shapes[k] for k in keys],
                            "reduce_pair_" + tag)
        return (keys,) + _exchange_chips_start(sums, "reduce_chip_start_" + tag)

    (gw_down,) = _grad_tn([(f, d_out)], min(1024, D_FF), "ffn_down_wgrad")
    flight_down = reduce_start("down", {"ffn_w_down": gw_down})
    gw_up, gc_ffn, gb_ffn, d_h2 = _ffn_up_bwd(up, d_act, behind(flight_down[-1], c_ffn), h2, g_up)
    flight_up = reduce_start("up", {"ffn_w_up": gw_up})
    dx1, d_mix, d_pa, d_pb, d_ya, d_yb, d_gate, dg3, dg2 = _merge_bwd(
        dy, d_h2, x1, mix, behind(flight_up[-1], norm_ffn_pre), norm_mix_post, g_out, g_cb, g_lb, pa, pb, proj)
    gw_out, gw_cb, gw_lb = _grad_tn([(merged, d_mix), (y_a, d_pa), (y_b, d_pb)], 2 * CB, "merge_wgrads")
    flight_mix = reduce_start("mix", {"w_conv_branch": gw_cb, "w_lru_branch": gw_lb, "w_out": gw_out})
    d_conv, gc_short = _conv_mixer_bwd(proj, d_ya, behind(flight_mix[-1], c_short))
    d_lru, gw_a, gw_x, g_lru_small = _lru_bwd(proj, hl, decay, lru_kept, d_yb, c_lru, g_wa, g_wx, lru_lambda)
    early = [dg2, dg3, dg4, g_lru_small[4:5], g_lru_small[7:8], gb_ffn, gc_short, g_lru_small[0:4],
             g_lru_small[5:6], g_lru_small[6:7], gc_ffn, loss_part]
    flight_small = _small_start(_pack_rows(early), "small_start")
    gw_in = _in_proj_wgrad(h, d_conv, d_lru, d_gate)
    flight_in = reduce_start("in", {"lru_wa": gw_a, "lru_wx": gw_x, "w_in": gw_in})
    dx, dg1 = _in_proj_xgrad(d_conv, d_lru, d_gate, g_in, x2, dx1,
                             behind(flight_small[-1], behind(flight_in[-1], norm_mix_pre)))
    flight_late = _small_start(_pack_rows([dg1]), "small_start_late")

    moments ={"w_in": (m_w_in, v_w_in), "w_conv_branch": (m_w_conv_branch, v_w_conv_branch),
               "w_lru_branch": (m_w_lru_branch, v_w_lru_branch), "w_out": (m_w_out, v_w_out),
               "lru_wa": (m_lru_wa, v_lru_wa), "lru_wx": (m_lru_wx, v_lru_wx), "ffn_w_up": (m_ffn_w_up, v_ffn_w_up),
               "ffn_w_down": (m_ffn_w_down, v_ffn_w_down)}
    weights = {"w_in": w_in, "w_conv_branch": w_conv_branch, "w_lru_branch": w_lru_branch, "w_out": w_out,
               "lru_wa": lru_wa, "lru_wx": lru_wx, "ffn_w_up": ffn_w_up, "ffn_w_down": ffn_w_down}
    out_g, out_d, out_m, out_v = {}, {}, {}, {}

    after = flight_late[-1]
    for tag, (keys, send, recv, sums, lands, _) in (("down", flight_down), ("up", flight_up), ("mix", flight_mix),
                                                    ("in", flight_in)):
        sums, others = _exchange_chips_wait(send, recv, sums, lands, after, "reduce_chip_wait_" + tag)
        by_key = dict(zip(keys, zip(sums, others)))
        for shape in dict.fromkeys(shard_shapes[k] for k in keys):
            same = [k for k in keys if shard_shapes[k] == shape]
            results = _adam_large([weights[k] for k in same], [moments[k][0] for k in same], [moments[k][1] for k in same],
                                  [by_key[k][0] for k in same], [by_key[k][1] for k in same], "adam_" + same[0])
            for out, values in zip((out_g, out_d, out_m, out_v), results):
                out.update(zip(same, values))
        after = out_d[keys[-1]]

    total, total_late = _small_sum([_small_wait(*flight_small[:4], after, "small_wait"),
                                    _small_wait(*flight_late[:4], after, "small_wait_late")], me)
    sizes = [p.size for p in early]
    starts = _packed_starts(sizes)

    def piece(i, shape):
        if i == 0:
            return total_late.reshape(-1)[:D_MODEL].reshape(shape)
        return total[starts[i - 1]:starts[i]].reshape(-1)[:sizes[i - 1]].reshape(shape)

    loss = total[starts[11], 0]

    def col_shard(full, width):
        return lax.dynamic_slice_in_dim(full, me * width, width, axis=1)

    def head_shard(full):
        return lax.dynamic_slice_in_dim(full.reshape(N_HEADS, HEAD_DIM), me * shard_head, shard_head, axis=1)

    small_names = ["norm_mix_pre", "norm_mix_post", "norm_ffn_pre", "norm_ffn_post", "lru_conv_b", "lru_lambda",
                   "ffn_conv_b", "conv_short_w", "lru_conv_w", "lru_ba", "lru_bx", "ffn_conv_w"]
    small_g = [piece(0, (1, D_MODEL)), piece(1, (1, D_MODEL)), piece(2, (1, D_MODEL)), piece(3, (1, D_MODEL)),
               piece(4, (1, D_MODEL)), piece(5, (1, D_MODEL)), piece(6, (1, 2 * D_FF)),
               col_shard(piece(7, (3, D_MODEL)), LANES), col_shard(piece(8, (4, D_MODEL)), LANES),
               head_shard(piece(9, (1, D_MODEL))), head_shard(piece(10, (1, D_MODEL))),
               col_shard(piece(11, (3, 2 * D_FF)), shard_up)]
    small_w = [norm_mix_pre, norm_mix_post, norm_ffn_pre, norm_ffn_post, lru_conv_b, lru_lambda, ffn_conv_b,
               conv_short_w[0], lru_conv_w[0], lru_ba[0], lru_bx[0], ffn_conv_w[0]]
    small_m = [m_norm_mix_pre, m_norm_mix_post, m_norm_ffn_pre, m_norm_ffn_post, m_lru_conv_b, m_lru_lambda,
               m_ffn_conv_b, m_conv_short_w[0], m_lru_conv_w[0], m_lru_ba[0], m_lru_bx[0], m_ffn_conv_w[0]]
    small_v = [v_norm_mix_pre, v_norm_mix_post, v_norm_ffn_pre, v_norm_ffn_post, v_lru_conv_b, v_lru_lambda,
               v_ffn_conv_b, v_conv_short_w[0], v_lru_conv_w[0], v_lru_ba[0], v_lru_bx[0], v_ffn_conv_w[0]]
    s_d, s_m, s_v = _adam_small(small_w, small_g, small_m, small_v)
    for i, name in enumerate(small_names):
        shape = small_w[i].shape if i < 7 else (1,) + small_w[i].shape
        out_g[name] = small_g[i].reshape(shape)
        out_d[name], out_m[name], out_v[name] = s_d[i].reshape(shape), s_m[i].reshape(shape), s_v[i].reshape(shape)

    order = ["norm_mix_pre", "norm_mix_post", "norm_ffn_pre", "norm_ffn_post", "w_in", "conv_short_w", "w_conv_branch",
             "lru_conv_w", "lru_conv_b", "lru_wa", "lru_ba", "lru_wx", "lru_bx", "lru_lambda", "w_lru_branch", "w_out",
             "ffn_w_up", "ffn_conv_w", "ffn_conv_b", "ffn_w_down"]
    return (loss, dx.reshape(1, t, D_MODEL), *[out_g[k] for k in order], *[out_d[k] for k in order],
            *[out_m[k] for k in order], *[out_v[k] for k in order])
```
